```python
import jax, jax.numpy as jnp
from jax import lax
import numpy as np

D_MODEL = 1024
BATCH = 16
SEQ = 4096
DEPTH = 1

MLA_HEADS = 8
QK_NOPE_DIM = 128
QK_ROPE_DIM = 64
V_HEAD_DIM = 128
Q_LORA_RANK = 256
KV_LORA_RANK = 256
ROPE_THETA = 10000.0
Q_BLOCK = 128
SSM_D_INNER = 2 * D_MODEL
SSM_HEAD_DIM = 64
SSM_HEADS = SSM_D_INNER // SSM_HEAD_DIM
SSM_GROUPS = 8
SSM_HEADS_PER_GROUP = SSM_HEADS // SSM_GROUPS
SSM_STATE = 128
CONV_WIDTH = 4
SSD_CHUNK = 128
CONV_CH = SSM_D_INNER + 2 * SSM_GROUPS * SSM_STATE
D_FF = 4 * D_MODEL
EPS = 1e-6
IN_SIZES = (Q_LORA_RANK, KV_LORA_RANK, QK_ROPE_DIM, SSM_D_INNER, CONV_CH, SSM_HEADS, D_MODEL, D_MODEL)
IN_COLS = Q_LORA_RANK + KV_LORA_RANK + QK_ROPE_DIM + SSM_D_INNER + CONV_CH + SSM_HEADS + 2 * D_MODEL

kernel_name = "hybrid_mla_ssd_gated_block"


def rms_norm(x, g):
    xf = x.astype(jnp.float32)
    y = xf * lax.rsqrt(jnp.mean(xf * xf, axis=-1, keepdims=True) + EPS)
    return (y * g.astype(jnp.float32)).astype(x.dtype)


def split_cols(t, sizes):
    idx = [int(v) for v in np.cumsum(sizes)[:-1]]
    return jnp.split(t, idx, axis=-1)


def rope_tables(positions):
    half = QK_ROPE_DIM // 2
    inv = ROPE_THETA ** (-jnp.arange(half, dtype=jnp.float32) / half)
    ang = positions.astype(jnp.float32)[..., None] * inv
    return jnp.cos(ang), jnp.sin(ang)


def apply_rope(t, cos, sin):
    half = QK_ROPE_DIM // 2
    t1, t2 = t[..., :half], t[..., half:]
    return jnp.concatenate([t1 * cos - t2 * sin, t1 * sin + t2 * cos], axis=-1).astype(t.dtype)


def mla_attention(q_lat, kv_lat, k_rope, positions, g_q, g_kv, w_uq, w_ukv):
    B, S, _ = q_lat.shape
    q = (rms_norm(q_lat, g_q) @ w_uq).reshape(B, S, MLA_HEADS, QK_NOPE_DIM + QK_ROPE_DIM)
    q_nope, q_rope = q[..., :QK_NOPE_DIM], q[..., QK_NOPE_DIM:]
    kv = (rms_norm(kv_lat, g_kv) @ w_ukv).reshape(B, S, MLA_HEADS, QK_NOPE_DIM + V_HEAD_DIM)
    k_nope, v = kv[..., :QK_NOPE_DIM], kv[..., QK_NOPE_DIM:]
    cos, sin = rope_tables(positions)
    q_rope = apply_rope(q_rope, cos[:, :, None, :], sin[:, :, None, :])
    k_rope = apply_rope(k_rope, cos, sin)
    scale = (QK_NOPE_DIM + QK_ROPE_DIM) ** -0.5
    outs = []
    for i in range(S // Q_BLOCK):
        q0, q1 = i * Q_BLOCK, (i + 1) * Q_BLOCK
        s = (jnp.einsum('bqhd,bkhd->bhqk', q_nope[:, q0:q1], k_nope[:, :q1]).astype(jnp.float32)
             + jnp.einsum('bqhr,bkr->bhqk', q_rope[:, q0:q1], k_rope[:, :q1]).astype(jnp.float32)) * scale
        causal = jnp.arange(q1)[None, :] <= jnp.arange(q0, q1)[:, None]
        p = jax.nn.softmax(jnp.where(causal, s, -jnp.inf), axis=-1).astype(v.dtype)
        outs.append(jnp.einsum('bhqk,bkhd->bqhd', p, v[:, :q1]))
    return jnp.concatenate(outs, axis=1).reshape(B, S, MLA_HEADS * V_HEAD_DIM)


def causal_depthwise_conv(u, w, b):
    S = u.shape[1]
    up = jnp.pad(u, ((0, 0), (CONV_WIDTH - 1, 0), (0, 0)))
    out = b
    for k in range(CONV_WIDTH):
        out = out + up[:, k:k + S] * w[k]
    return out


def ssd_chunked(xh, dt, A, Bm, Cm):
    Bsz, S, G, Hg, P = xh.shape
    N = Bm.shape[-1]
    L = SSD_CHUNK
    nc = S // L
    xdt = (xh * dt[..., None]).reshape(Bsz, nc, L, G, Hg, P)
    a_cum = jnp.cumsum((dt * A).reshape(Bsz, nc, L, G, Hg), axis=2)
    Bc = Bm.reshape(Bsz, nc, L, G, N).astype(jnp.float32)
    Cc = Cm.reshape(Bsz, nc, L, G, N).astype(jnp.float32)
    seg = a_cum[:, :, :, None] - a_cum[:, :, None, :]
    tri = jnp.tril(jnp.ones((L, L), dtype=bool))[:, :, None, None]
    decay = jnp.exp(jnp.where(tri, seg, -jnp.inf))
    cb = jnp.einsum('bclgn,bcsgn->bclsg', Cc, Bc)
    y_diag = jnp.einsum('bclsg,bclsgh,bcsghp->bclghp', cb, decay, xdt)
    decay_to_end = jnp.exp(a_cum[:, :, -1:] - a_cum)
    states = jnp.einsum('bclgn,bclgh,bclghp->bcghpn', Bc, decay_to_end, xdt)
    chunk_decay = jnp.exp(a_cum[:, :, -1])

    def step(h, inp):
        s_c, d_c = inp
        return h * d_c[..., None, None] + s_c, h

    h0 = jnp.zeros((Bsz, G, Hg, P, N), dtype=states.dtype)
    _, h_prev = lax.scan(step, h0, (jnp.moveaxis(states, 1, 0), jnp.moveaxis(chunk_decay, 1, 0)))
    h_prev = jnp.moveaxis(h_prev, 0, 1)
    y_off = jnp.einsum('bclgn,bcghpn,bclgh->bclghp', Cc, h_prev, jnp.exp(a_cum))
    return (y_diag + y_off).reshape(Bsz, S, G, Hg, P)


def mamba2_mixer(z, xbc, dt_raw, conv_w, conv_b, dt_bias, a_log, d_skip, g_ssm_out):
    B, S, _ = z.shape
    xbc = jax.nn.silu(causal_depthwise_conv(xbc, conv_w, conv_b))
    xs, Bm, Cm = split_cols(xbc, (SSM_D_INNER, SSM_GROUPS * SSM_STATE, SSM_GROUPS * SSM_STATE))
    xh = xs.reshape(B, S, SSM_GROUPS, SSM_HEADS_PER_GROUP, SSM_HEAD_DIM)
    Bm = Bm.reshape(B, S, SSM_GROUPS, SSM_STATE)
    Cm = Cm.reshape(B, S, SSM_GROUPS, SSM_STATE)
    dt = jax.nn.softplus(dt_raw.astype(jnp.float32).reshape(B, S, SSM_GROUPS, SSM_HEADS_PER_GROUP)
                         + dt_bias.astype(jnp.float32).reshape(SSM_GROUPS, SSM_HEADS_PER_GROUP))
    A = -jnp.exp(a_log.astype(jnp.float32)).reshape(SSM_GROUPS, SSM_HEADS_PER_GROUP)
    y = ssd_chunked(xh, dt, A, Bm, Cm)
    y = y + d_skip.reshape(SSM_GROUPS, SSM_HEADS_PER_GROUP)[:, :, None] * xh
    y = y.reshape(B, S, SSM_D_INNER).astype(z.dtype)
    yg = (y * jax.nn.silu(z)).reshape(B, S, SSM_GROUPS, SSM_D_INNER // SSM_GROUPS)
    yg = rms_norm(yg, g_ssm_out.reshape(SSM_GROUPS, SSM_D_INNER // SSM_GROUPS))
    return yg.reshape(B, S, SSM_D_INNER)


def _fwd_setup_inputs(seed: int = 0) -> dict:
    key = jax.random.key(seed)
    ks = jax.random.split(key, 32)
    f32 = jnp.float32

    def nrm(k, shape, scale):
        return jax.random.normal(k, shape, f32) * scale

    def gain(k, shape):
        return 1.0 + 0.05 * jax.random.normal(k, shape, f32)

    Ld = DEPTH
    x = jax.random.normal(ks[0], (BATCH, SEQ, D_MODEL), f32)
    c = jax.random.normal(ks[1], (BATCH, D_MODEL), f32)
    offs = jax.random.randint(ks[2], (BATCH, 1), 0, 2048, dtype=jnp.int32)
    positions = (offs + jnp.arange(SEQ, dtype=jnp.int32)[None, :]).astype(jnp.int32)
    dt0 = jnp.exp(jax.random.uniform(ks[16], (Ld, SSM_HEADS), f32, np.log(1e-3), np.log(1e-1)))
    return {
        "x": x,
        "c": c,
        "positions": positions,
        "w_ada": nrm(ks[3], (Ld, D_MODEL, 6 * D_MODEL), 0.5 * D_MODEL ** -0.5),
        "b_ada": nrm(ks[4], (Ld, 6 * D_MODEL), 0.01),
        "g_pre_mix": gain(ks[5], (Ld, D_MODEL)),
        "g_post_mix": gain(ks[6], (Ld, D_MODEL)),
        "w_in": nrm(ks[7], (Ld, D_MODEL, IN_COLS), D_MODEL ** -0.5),
        "g_q_lat": gain(ks[8], (Ld, Q_LORA_RANK)),
        "g_kv_lat": gain(ks[9], (Ld, KV_LORA_RANK)),
        "w_uq": nrm(ks[10], (Ld, Q_LORA_RANK, MLA_HEADS * (QK_NOPE_DIM + QK_ROPE_DIM)), Q_LORA_RANK ** -0.5),
        "w_ukv": nrm(ks[11], (Ld, KV_LORA_RANK, MLA_HEADS * (QK_NOPE_DIM + V_HEAD_DIM)), KV_LORA_RANK ** -0.5),
        "w_o_attn": nrm(ks[12], (Ld, MLA_HEADS * V_HEAD_DIM, D_MODEL), (MLA_HEADS * V_HEAD_DIM) ** -0.5),
        "conv_w": nrm(ks[13], (Ld, CONV_WIDTH, CONV_CH), CONV_WIDTH ** -0.5),
        "conv_b": nrm(ks[14], (Ld, CONV_CH), 0.01),
        "dt_bias": dt0 + jnp.log(-jnp.expm1(-dt0)),
        "a_log": jnp.log(jax.random.uniform(ks[17], (Ld, SSM_HEADS), f32, 1.0, 16.0)),
        "d_skip": gain(ks[18], (Ld, SSM_HEADS)),
        "g_ssm_out": gain(ks[19], (Ld, SSM_D_INNER)),
        "w_o_ssm": nrm(ks[20], (Ld, SSM_D_INNER, D_MODEL), SSM_D_INNER ** -0.5),
        "w_out": nrm(ks[21], (Ld, D_MODEL, D_MODEL), D_MODEL ** -0.5),
        "g_pre_mlp": gain(ks[22], (Ld, D_MODEL)),
        "g_post_mlp": gain(ks[23], (Ld, D_MODEL)),
        "w_ff1": nrm(ks[24], (Ld, D_MODEL, D_FF), D_MODEL ** -0.5),
        "w_ff2": nrm(ks[25], (Ld, D_FF, D_MODEL), D_FF ** -0.5),
    }


def _fwd_reference(x, c, positions, w_ada, b_ada, g_pre_mix, g_post_mix, w_in, g_q_lat, g_kv_lat,
              w_uq, w_ukv, w_o_attn, conv_w, conv_b, dt_bias, a_log, d_skip, g_ssm_out,
              w_o_ssm, w_out, g_pre_mlp, g_post_mlp, w_ff1, w_ff2):
    sc = jax.nn.silu(c)
    for l in range(DEPTH):
        mod = sc @ w_ada[l] + b_ada[l]
        shift1, scale1, gate1, shift2, scale2, gate2 = [m[:, None, :] for m in jnp.split(mod, 6, axis=-1)]
        h = rms_norm(x, g_pre_mix[l]) * (1.0 + scale1) + shift1
        q_lat, kv_lat, k_rope, z, xbc, dt_raw, gate_a, gate_b = split_cols(h @ w_in[l], IN_SIZES)
        attn = mla_attention(q_lat, kv_lat, k_rope, positions, g_q_lat[l], g_kv_lat[l], w_uq[l], w_ukv[l]) @ w_o_attn[l]
        ssm = mamba2_mixer(z, xbc, dt_raw, conv_w[l], conv_b[l], dt_bias[l], a_log[l], d_skip[l], g_ssm_out[l]) @ w_o_ssm[l]
        merged = jax.nn.sigmoid(gate_a) * attn + jax.nn.sigmoid(gate_b) * ssm
        x = x + gate1 * rms_norm(merged @ w_out[l], g_post_mix[l])
        h2 = rms_norm(x, g_pre_mlp[l]) * (1.0 + scale2) + shift2
        ff = jnp.square(jax.nn.relu(h2 @ w_ff1[l])) @ w_ff2[l]
        x = x + gate2 * rms_norm(ff, g_post_mlp[l])
    return x


import jax as _jax
import jax.numpy as _jnp

TWIN_FORMAT = 'train_step'
FWD_PARAMS = ['x', 'c', 'positions', 'w_ada', 'b_ada', 'g_pre_mix', 'g_post_mix', 'w_in', 'g_q_lat', 'g_kv_lat', 'w_uq', 'w_ukv', 'w_o_attn', 'conv_w', 'conv_b', 'dt_bias', 'a_log', 'd_skip', 'g_ssm_out', 'w_o_ssm', 'w_out', 'g_pre_mlp', 'g_post_mlp', 'w_ff1', 'w_ff2']
TWIN_WEIGHTS = ['w_ada', 'b_ada', 'g_pre_mix', 'g_post_mix', 'w_in', 'g_q_lat', 'g_kv_lat', 'w_uq', 'w_ukv', 'w_o_attn', 'conv_w', 'conv_b', 'dt_bias', 'a_log', 'd_skip', 'g_ssm_out', 'w_o_ssm', 'w_out', 'g_pre_mlp', 'g_post_mlp', 'w_ff1', 'w_ff2']
TWIN_DIFF_INPUT = 'x'
TWIN_INPUTS = ['x', 'c', 'positions', 'w_ada', 'b_ada', 'g_pre_mix', 'g_post_mix', 'w_in', 'g_q_lat', 'g_kv_lat', 'w_uq', 'w_ukv', 'w_o_attn', 'conv_w', 'conv_b', 'dt_bias', 'a_log', 'd_skip', 'g_ssm_out', 'w_o_ssm', 'w_out', 'g_pre_mlp', 'g_post_mlp', 'w_ff1', 'w_ff2', 'loss_target', 'm_w_ada', 'm_b_ada', 'm_g_pre_mix', 'm_g_post_mix', 'm_w_in', 'm_g_q_lat', 'm_g_kv_lat', 'm_w_uq', 'm_w_ukv', 'm_w_o_attn', 'm_conv_w', 'm_conv_b', 'm_dt_bias', 'm_a_log', 'm_d_skip', 'm_g_ssm_out', 'm_w_o_ssm', 'm_w_out', 'm_g_pre_mlp', 'm_g_post_mlp', 'm_w_ff1', 'm_w_ff2', 'v_w_ada', 'v_b_ada', 'v_g_pre_mix', 'v_g_post_mix', 'v_w_in', 'v_g_q_lat', 'v_g_kv_lat', 'v_w_uq', 'v_w_ukv', 'v_w_o_attn', 'v_conv_w', 'v_conv_b', 'v_dt_bias', 'v_a_log', 'v_d_skip', 'v_g_ssm_out', 'v_w_o_ssm', 'v_w_out', 'v_g_pre_mlp', 'v_g_post_mlp', 'v_w_ff1', 'v_w_ff2']
TWIN_OUTPUTS = ['loss', 'grad_x', 'grad_w_ada', 'grad_b_ada', 'grad_g_pre_mix', 'grad_g_post_mix', 'grad_w_in', 'grad_g_q_lat', 'grad_g_kv_lat', 'grad_w_uq', 'grad_w_ukv', 'grad_w_o_attn', 'grad_conv_w', 'grad_conv_b', 'grad_dt_bias', 'grad_a_log', 'grad_d_skip', 'grad_g_ssm_out', 'grad_w_o_ssm', 'grad_w_out', 'grad_g_pre_mlp', 'grad_g_post_mlp', 'grad_w_ff1', 'grad_w_ff2', 'delta_w_ada', 'delta_b_ada', 'delta_g_pre_mix', 'delta_g_post_mix', 'delta_w_in', 'delta_g_q_lat', 'delta_g_kv_lat', 'delta_w_uq', 'delta_w_ukv', 'delta_w_o_attn', 'delta_conv_w', 'delta_conv_b', 'delta_dt_bias', 'delta_a_log', 'delta_d_skip', 'delta_g_ssm_out', 'delta_w_o_ssm', 'delta_w_out', 'delta_g_pre_mlp', 'delta_g_post_mlp', 'delta_w_ff1', 'delta_w_ff2', 'new_m_w_ada', 'new_m_b_ada', 'new_m_g_pre_mix', 'new_m_g_post_mix', 'new_m_w_in', 'new_m_g_q_lat', 'new_m_g_kv_lat', 'new_m_w_uq', 'new_m_w_ukv', 'new_m_w_o_attn', 'new_m_conv_w', 'new_m_conv_b', 'new_m_dt_bias', 'new_m_a_log', 'new_m_d_skip', 'new_m_g_ssm_out', 'new_m_w_o_ssm', 'new_m_w_out', 'new_m_g_pre_mlp', 'new_m_g_post_mlp', 'new_m_w_ff1', 'new_m_w_ff2', 'new_v_w_ada', 'new_v_b_ada', 'new_v_g_pre_mix', 'new_v_g_post_mix', 'new_v_w_in', 'new_v_g_q_lat', 'new_v_g_kv_lat', 'new_v_w_uq', 'new_v_w_ukv', 'new_v_w_o_attn', 'new_v_conv_w', 'new_v_conv_b', 'new_v_dt_bias', 'new_v_a_log', 'new_v_d_skip', 'new_v_g_ssm_out', 'new_v_w_o_ssm', 'new_v_w_out', 'new_v_g_pre_mlp', 'new_v_g_post_mlp', 'new_v_w_ff1', 'new_v_w_ff2']
TWIN_LEAF_KINDS = {'loss': 'loss', 'grad_x': 'grad_x', 'grad_w_ada': 'grad_w', 'grad_b_ada': 'grad_w', 'grad_g_pre_mix': 'grad_w', 'grad_g_post_mix': 'grad_w', 'grad_w_in': 'grad_w', 'grad_g_q_lat': 'grad_w', 'grad_g_kv_lat': 'grad_w', 'grad_w_uq': 'grad_w', 'grad_w_ukv': 'grad_w', 'grad_w_o_attn': 'grad_w', 'grad_conv_w': 'grad_w', 'grad_conv_b': 'grad_w', 'grad_dt_bias': 'grad_w', 'grad_a_log': 'grad_w', 'grad_d_skip': 'grad_w', 'grad_g_ssm_out': 'grad_w', 'grad_w_o_ssm': 'grad_w', 'grad_w_out': 'grad_w', 'grad_g_pre_mlp': 'grad_w', 'grad_g_post_mlp': 'grad_w', 'grad_w_ff1': 'grad_w', 'grad_w_ff2': 'grad_w', 'delta_w_ada': 'delta_w', 'delta_b_ada': 'delta_w', 'delta_g_pre_mix': 'delta_w', 'delta_g_post_mix': 'delta_w', 'delta_w_in': 'delta_w', 'delta_g_q_lat': 'delta_w', 'delta_g_kv_lat': 'delta_w', 'delta_w_uq': 'delta_w', 'delta_w_ukv': 'delta_w', 'delta_w_o_attn': 'delta_w', 'delta_conv_w': 'delta_w', 'delta_conv_b': 'delta_w', 'delta_dt_bias': 'delta_w', 'delta_a_log': 'delta_w', 'delta_d_skip': 'delta_w', 'delta_g_ssm_out': 'delta_w', 'delta_w_o_ssm': 'delta_w', 'delta_w_out': 'delta_w', 'delta_g_pre_mlp': 'delta_w', 'delta_g_post_mlp': 'delta_w', 'delta_w_ff1': 'delta_w', 'delta_w_ff2': 'delta_w', 'new_m_w_ada': 'new_m', 'new_m_b_ada': 'new_m', 'new_m_g_pre_mix': 'new_m', 'new_m_g_post_mix': 'new_m', 'new_m_w_in': 'new_m', 'new_m_g_q_lat': 'new_m', 'new_m_g_kv_lat': 'new_m', 'new_m_w_uq': 'new_m', 'new_m_w_ukv': 'new_m', 'new_m_w_o_attn': 'new_m', 'new_m_conv_w': 'new_m', 'new_m_conv_b': 'new_m', 'new_m_dt_bias': 'new_m', 'new_m_a_log': 'new_m', 'new_m_d_skip': 'new_m', 'new_m_g_ssm_out': 'new_m', 'new_m_w_o_ssm': 'new_m', 'new_m_w_out': 'new_m', 'new_m_g_pre_mlp': 'new_m', 'new_m_g_post_mlp': 'new_m', 'new_m_w_ff1': 'new_m', 'new_m_w_ff2': 'new_m', 'new_v_w_ada': 'new_v', 'new_v_b_ada': 'new_v', 'new_v_g_pre_mix': 'new_v', 'new_v_g_post_mix': 'new_v', 'new_v_w_in': 'new_v', 'new_v_g_q_lat': 'new_v', 'new_v_g_kv_lat': 'new_v', 'new_v_w_uq': 'new_v', 'new_v_w_ukv': 'new_v', 'new_v_w_o_attn': 'new_v', 'new_v_conv_w': 'new_v', 'new_v_conv_b': 'new_v', 'new_v_dt_bias': 'new_v', 'new_v_a_log': 'new_v', 'new_v_d_skip': 'new_v', 'new_v_g_ssm_out': 'new_v', 'new_v_w_o_ssm': 'new_v', 'new_v_w_out': 'new_v', 'new_v_g_pre_mlp': 'new_v', 'new_v_g_post_mlp': 'new_v', 'new_v_w_ff1': 'new_v', 'new_v_w_ff2': 'new_v'}


def _forward(args):
    return _fwd_reference(*[args[k] for k in FWD_PARAMS])


def _output_shape():
    out = _jax.eval_shape(lambda: _forward(_fwd_setup_inputs(0)))
    return out.shape, out.dtype

N_MICROBATCH = 1
ADAM_LR = 0.001
ADAM_B1 = 0.9
ADAM_B2 = 0.999
ADAM_EPS = 1e-08
ADAM_WD = 0.01
ADAM_STEP = 10
PER_EXAMPLE_BATCH_AXIS = {'x': 0, 'c': 0, 'positions': 0, 'loss_target': 0}
SHARED_INPUTS = []
_WEIGHT_DTYPES = {'w_ada': _jnp.float32, 'b_ada': _jnp.float32, 'g_pre_mix': _jnp.float32, 'g_post_mix': _jnp.float32, 'w_in': _jnp.float32, 'g_q_lat': _jnp.float32, 'g_kv_lat': _jnp.float32, 'w_uq': _jnp.float32, 'w_ukv': _jnp.float32, 'w_o_attn': _jnp.float32, 'conv_w': _jnp.float32, 'conv_b': _jnp.float32, 'dt_bias': _jnp.float32, 'a_log': _jnp.float32, 'd_skip': _jnp.float32, 'g_ssm_out': _jnp.float32, 'w_o_ssm': _jnp.float32, 'w_out': _jnp.float32, 'g_pre_mlp': _jnp.float32, 'g_post_mlp': _jnp.float32, 'w_ff1': _jnp.float32, 'w_ff2': _jnp.float32}
MOMENT_SCALE = {'w_ada': 3.590376e+00, 'b_ada': 6.705718e+00, 'g_pre_mix': 2.047193e-01, 'g_post_mix': 7.333138e+00, 'w_in': 1.275438e-01, 'g_q_lat': 4.782323e-02, 'g_kv_lat': 7.898561e-01, 'w_uq': 1.755244e-02, 'w_ukv': 2.735367e-01, 'w_o_attn': 3.882109e-01, 'conv_w': 1.205539e-01, 'conv_b': 3.084365e-01, 'dt_bias': 2.839268e-01, 'a_log': 3.494520e-01, 'd_skip': 5.828430e-01, 'g_ssm_out': 2.181258e-01, 'w_o_ssm': 3.242920e-01, 'w_out': 4.882737e-01, 'g_pre_mlp': 2.083057e-01, 'g_post_mlp': 8.079184e+00, 'w_ff1': 1.575939e-01, 'w_ff2': 7.947589e-01}


def _to_microbatches(a, axis):
    t = _jnp.moveaxis(a, axis, 0)
    t = t.reshape((N_MICROBATCH, t.shape[0] // N_MICROBATCH) + t.shape[1:])
    return _jnp.moveaxis(t, 1, axis + 1)


def setup_inputs(seed: int = 0) -> dict:
    inp = _fwd_setup_inputs(seed)
    key = _jax.random.fold_in(_jax.random.key(seed), 7919)
    shape, _ = _output_shape()
    out = dict(inp)
    out["loss_target"] = _jax.random.normal(_jax.random.fold_in(key, 0), shape, _jnp.float32)
    for i, name in enumerate(TWIN_WEIGHTS):
        w = inp[name].astype(_jnp.float32)
        if MOMENT_SCALE is None:
            s = _jnp.sqrt(_jnp.mean(_jnp.square(w)) + 1e-30)
        else:
            s = MOMENT_SCALE[name]
        km, kv = _jax.random.split(_jax.random.fold_in(key, i + 1))
        out[name] = w
        out["m_" + name] = s * _jax.random.normal(km, w.shape, _jnp.float32)
        out["v_" + name] = (s * s) * _jax.random.uniform(kv, w.shape, _jnp.float32, 0.5, 1.5)
    if N_MICROBATCH > 1:
        for name, axis in PER_EXAMPLE_BATCH_AXIS.items():
            out[name] = _to_microbatches(out[name], axis)
    return {'x': out['x'], 'c': out['c'], 'positions': out['positions'], 'w_ada': out['w_ada'], 'b_ada': out['b_ada'], 'g_pre_mix': out['g_pre_mix'], 'g_post_mix': out['g_post_mix'], 'w_in': out['w_in'], 'g_q_lat': out['g_q_lat'], 'g_kv_lat': out['g_kv_lat'], 'w_uq': out['w_uq'], 'w_ukv': out['w_ukv'], 'w_o_attn': out['w_o_attn'], 'conv_w': out['conv_w'], 'conv_b': out['conv_b'], 'dt_bias': out['dt_bias'], 'a_log': out['a_log'], 'd_skip': out['d_skip'], 'g_ssm_out': out['g_ssm_out'], 'w_o_ssm': out['w_o_ssm'], 'w_out': out['w_out'], 'g_pre_mlp': out['g_pre_mlp'], 'g_post_mlp': out['g_post_mlp'], 'w_ff1': out['w_ff1'], 'w_ff2': out['w_ff2'], 'loss_target': out['loss_target'], 'm_w_ada': out['m_w_ada'], 'm_b_ada': out['m_b_ada'], 'm_g_pre_mix': out['m_g_pre_mix'], 'm_g_post_mix': out['m_g_post_mix'], 'm_w_in': out['m_w_in'], 'm_g_q_lat': out['m_g_q_lat'], 'm_g_kv_lat': out['m_g_kv_lat'], 'm_w_uq': out['m_w_uq'], 'm_w_ukv': out['m_w_ukv'], 'm_w_o_attn': out['m_w_o_attn'], 'm_conv_w': out['m_conv_w'], 'm_conv_b': out['m_conv_b'], 'm_dt_bias': out['m_dt_bias'], 'm_a_log': out['m_a_log'], 'm_d_skip': out['m_d_skip'], 'm_g_ssm_out': out['m_g_ssm_out'], 'm_w_o_ssm': out['m_w_o_ssm'], 'm_w_out': out['m_w_out'], 'm_g_pre_mlp': out['m_g_pre_mlp'], 'm_g_post_mlp': out['m_g_post_mlp'], 'm_w_ff1': out['m_w_ff1'], 'm_w_ff2': out['m_w_ff2'], 'v_w_ada': out['v_w_ada'], 'v_b_ada': out['v_b_ada'], 'v_g_pre_mix': out['v_g_pre_mix'], 'v_g_post_mix': out['v_g_post_mix'], 'v_w_in': out['v_w_in'], 'v_g_q_lat': out['v_g_q_lat'], 'v_g_kv_lat': out['v_g_kv_lat'], 'v_w_uq': out['v_w_uq'], 'v_w_ukv': out['v_w_ukv'], 'v_w_o_attn': out['v_w_o_attn'], 'v_conv_w': out['v_conv_w'], 'v_conv_b': out['v_conv_b'], 'v_dt_bias': out['v_dt_bias'], 'v_a_log': out['v_a_log'], 'v_d_skip': out['v_d_skip'], 'v_g_ssm_out': out['v_g_ssm_out'], 'v_w_o_ssm': out['v_w_o_ssm'], 'v_w_out': out['v_w_out'], 'v_g_pre_mlp': out['v_g_pre_mlp'], 'v_g_post_mlp': out['v_g_post_mlp'], 'v_w_ff1': out['v_w_ff1'], 'v_w_ff2': out['v_w_ff2']}


def _loss(weights, diff, rest, loss_target):
    with _jax.named_scope("forward"):
        args = {**rest, TWIN_DIFF_INPUT: diff, **{k: w.astype(_WEIGHT_DTYPES[k]) for k, w in weights.items()}}
        y = _forward(args)
    with _jax.named_scope("loss_head"):
        err = _jnp.square(y.astype(_jnp.float32) - loss_target)
        return 0.5 * _jnp.sum(_jnp.mean(err, axis=-1)) if err.ndim else 0.5 * err


def _adamw(w, g, m, v):
    m = ADAM_B1 * m + (1.0 - ADAM_B1) * g
    v = ADAM_B2 * v + (1.0 - ADAM_B2) * _jnp.square(g)
    m_hat = m / (1.0 - ADAM_B1 ** ADAM_STEP)
    v_hat = v / (1.0 - ADAM_B2 ** ADAM_STEP)
    delta = -ADAM_LR * (m_hat / (_jnp.sqrt(v_hat) + ADAM_EPS) + ADAM_WD * w)
    return delta, m, v


def reference(x, c, positions, w_ada, b_ada, g_pre_mix, g_post_mix, w_in, g_q_lat, g_kv_lat, w_uq, w_ukv, w_o_attn, conv_w, conv_b, dt_bias, a_log, d_skip, g_ssm_out, w_o_ssm, w_out, g_pre_mlp, g_post_mlp, w_ff1, w_ff2, loss_target, m_w_ada, m_b_ada, m_g_pre_mix, m_g_post_mix, m_w_in, m_g_q_lat, m_g_kv_lat, m_w_uq, m_w_ukv, m_w_o_attn, m_conv_w, m_conv_b, m_dt_bias, m_a_log, m_d_skip, m_g_ssm_out, m_w_o_ssm, m_w_out, m_g_pre_mlp, m_g_post_mlp, m_w_ff1, m_w_ff2, v_w_ada, v_b_ada, v_g_pre_mix, v_g_post_mix, v_w_in, v_g_q_lat, v_g_kv_lat, v_w_uq, v_w_ukv, v_w_o_attn, v_conv_w, v_conv_b, v_dt_bias, v_a_log, v_d_skip, v_g_ssm_out, v_w_o_ssm, v_w_out, v_g_pre_mlp, v_g_post_mlp, v_w_ff1, v_w_ff2):
    given = dict(x=x, c=c, positions=positions, w_ada=w_ada, b_ada=b_ada, g_pre_mix=g_pre_mix, g_post_mix=g_post_mix, w_in=w_in, g_q_lat=g_q_lat, g_kv_lat=g_kv_lat, w_uq=w_uq, w_ukv=w_ukv, w_o_attn=w_o_attn, conv_w=conv_w, conv_b=conv_b, dt_bias=dt_bias, a_log=a_log, d_skip=d_skip, g_ssm_out=g_ssm_out, w_o_ssm=w_o_ssm, w_out=w_out, g_pre_mlp=g_pre_mlp, g_post_mlp=g_post_mlp, w_ff1=w_ff1, w_ff2=w_ff2, loss_target=loss_target, m_w_ada=m_w_ada, m_b_ada=m_b_ada, m_g_pre_mix=m_g_pre_mix, m_g_post_mix=m_g_post_mix, m_w_in=m_w_in, m_g_q_lat=m_g_q_lat, m_g_kv_lat=m_g_kv_lat, m_w_uq=m_w_uq, m_w_ukv=m_w_ukv, m_w_o_attn=m_w_o_attn, m_conv_w=m_conv_w, m_conv_b=m_conv_b, m_dt_bias=m_dt_bias, m_a_log=m_a_log, m_d_skip=m_d_skip, m_g_ssm_out=m_g_ssm_out, m_w_o_ssm=m_w_o_ssm, m_w_out=m_w_out, m_g_pre_mlp=m_g_pre_mlp, m_g_post_mlp=m_g_post_mlp, m_w_ff1=m_w_ff1, m_w_ff2=m_w_ff2, v_w_ada=v_w_ada, v_b_ada=v_b_ada, v_g_pre_mix=v_g_pre_mix, v_g_post_mix=v_g_post_mix, v_w_in=v_w_in, v_g_q_lat=v_g_q_lat, v_g_kv_lat=v_g_kv_lat, v_w_uq=v_w_uq, v_w_ukv=v_w_ukv, v_w_o_attn=v_w_o_attn, v_conv_w=v_conv_w, v_conv_b=v_conv_b, v_dt_bias=v_dt_bias, v_a_log=v_a_log, v_d_skip=v_d_skip, v_g_ssm_out=v_g_ssm_out, v_w_o_ssm=v_w_o_ssm, v_w_out=v_w_out, v_g_pre_mlp=v_g_pre_mlp, v_g_post_mlp=v_g_post_mlp, v_w_ff1=v_w_ff1, v_w_ff2=v_w_ff2)
    weights = {n: given[n] for n in TWIN_WEIGHTS}
    shared = {n: given[n] for n in SHARED_INPUTS}
    per_example = {n: given[n] for n in ['x', 'c', 'positions']}
    grad_fn = _jax.value_and_grad(_loss, argnums=(0, 1))

    def one_microbatch(ex, loss_target):
        ex = dict(ex)
        diff = ex.pop(TWIN_DIFF_INPUT)
        return grad_fn(weights, diff, {**shared, **ex}, loss_target)

    if N_MICROBATCH == 1:
        loss, (grad_w, grad_x) = one_microbatch(per_example, given["loss_target"])
    else:
        def body(carry, xs):
            loss_sum, grad_sum = carry
            l_k, (gw_k, gx_k) = one_microbatch(xs[0], xs[1])
            with _jax.named_scope("update"):
                return (loss_sum + l_k, _jax.tree.map(_jnp.add, grad_sum, gw_k)), gx_k

        init = (_jnp.zeros((), _jnp.float32), _jax.tree.map(_jnp.zeros_like, weights))
        (loss, grad_w), grad_x = _jax.lax.scan(body, init, (per_example, given["loss_target"]))
    with _jax.named_scope("update"):
        delta_w, new_m, new_v = {}, {}, {}
        for n in TWIN_WEIGHTS:
            delta_w[n], new_m[n], new_v[n] = _adamw(weights[n], grad_w[n], given["m_" + n], given["v_" + n])
    return (loss, grad_x, *[grad_w[n] for n in TWIN_WEIGHTS], *[delta_w[n] for n in TWIN_WEIGHTS],
            *[new_m[n] for n in TWIN_WEIGHTS], *[new_v[n] for n in TWIN_WEIGHTS])
```

```python
import functools
import math

import numpy as np
import jax
import jax.numpy as jnp
from jax import lax
from jax.experimental import pallas as pl
from jax.experimental.pallas import tpu as pltpu

F32 = jnp.float32
BF16 = jnp.bfloat16
MESH = pl.DeviceIdType.MESH

D_MODEL = 1024
N_HEADS = 8
NOPE = 128
ROPE = 64
V_DIM = 128
Q_RANK = 256
KV_RANK = 256
ROPE_THETA = 10000.0
D_INNER = 2048
SSM_HEADS = 32
SSM_GROUPS = 8
HEAD_P = 64
STATE_N = 128
CONV_K = 4
CHUNK = 128
CONV_CH = D_INNER + 2 * SSM_GROUPS * STATE_N
D_FF = 4096
EPS = 1e-6
IN_SIZES = (Q_RANK, KV_RANK, ROPE, D_INNER, CONV_CH, SSM_HEADS, D_MODEL, D_MODEL)
ADAM_LR, ADAM_B1, ADAM_B2, ADAM_EPS, ADAM_WD, ADAM_STEP = 0.001, 0.9, 0.999, 1e-08, 0.01, 10

VMEM_LIMIT_BYTES = 52 * 1024 * 1024
LANE = 128
QK_PAD = 256
PACK_W = 512
PACK_ROW_ALIGN = 1024

WEIGHTS = ['w_ada', 'b_ada', 'g_pre_mix', 'g_post_mix', 'w_in', 'g_q_lat', 'g_kv_lat', 'w_uq', 'w_ukv',
           'w_o_attn', 'conv_w', 'conv_b', 'dt_bias', 'a_log', 'd_skip', 'g_ssm_out', 'w_o_ssm', 'w_out',
           'g_pre_mlp', 'g_post_mlp', 'w_ff1', 'w_ff2']
COL_SHARDED = ('w_ada', 'w_in', 'w_uq', 'w_ukv', 'conv_w', 'w_ff1')
ROW_SHARDED = ('w_o_attn', 'w_o_ssm', 'w_out', 'w_ff2')


def _cparams(sem):
    return pltpu.CompilerParams(dimension_semantics=sem, vmem_limit_bytes=VMEM_LIMIT_BYTES)


def _tile(n, cap):
    if n <= cap:
        return n
    k = n // LANE
    best = LANE
    for d in range(1, k + 1):
        if k % d == 0 and d * LANE <= cap:
            best = d * LANE
    return best


def _mm(a, w, name):
    M, K = a.shape
    N = w.shape[1]
    tm = min(M, 512)
    tn = _tile(N, 1024)
    tk = _tile(K, 2048)
    nk = K // tk

    def body(a_ref, w_ref, o_ref):
        part = jnp.dot(a_ref[...].astype(BF16), w_ref[...], preferred_element_type=F32)
        if nk == 1:
            o_ref[...] = part
        else:
            k = pl.program_id(2)

            @pl.when(k == 0)
            def _():
                o_ref[...] = part

            @pl.when(k > 0)
            def _():
                o_ref[...] += part

    return pl.pallas_call(
        body, grid=(M // tm, N // tn, nk),
        in_specs=[pl.BlockSpec((tm, tk), lambda i, j, k: (i, k)), pl.BlockSpec((tk, tn), lambda i, j, k: (k, j))],
        out_specs=pl.BlockSpec((tm, tn), lambda i, j, k: (i, j)),
        out_shape=jax.ShapeDtypeStruct((M, N), F32), name=name,
        compiler_params=_cparams(("parallel", "parallel", "arbitrary")))(a, w)


def _mm_tn(a, g, name):
    M, K = a.shape
    N = g.shape[1]
    tm = min(M, 512)
    tk = _tile(K, 1024)
    tn = _tile(N, 1024)
    nm = M // tm

    def body(a_ref, g_ref, o_ref):
        part = lax.dot_general(a_ref[...].astype(BF16), g_ref[...].astype(BF16), (((0,), (0,)), ((), ())),
                               preferred_element_type=F32)
        m = pl.program_id(2)

        @pl.when(m == 0)
        def _():
            o_ref[...] = part

        @pl.when(m > 0)
        def _():
            o_ref[...] += part

    return pl.pallas_call(
        body, grid=(K // tk, N // tn, nm),
        in_specs=[pl.BlockSpec((tm, tk), lambda i, j, m: (m, i)), pl.BlockSpec((tm, tn), lambda i, j, m: (m, j))],
        out_specs=pl.BlockSpec((tk, tn), lambda i, j, m: (i, j)),
        out_shape=jax.ShapeDtypeStruct((K, N), F32), name=name,
        compiler_params=_cparams(("parallel", "parallel", "arbitrary")))(a, g)


def make_linear(name):
    @jax.custom_vjp
    def linear(a, w, tok):
        return _mm(a, w, name + "_fwd")

    def fwd(a, w, tok):
        return _mm(a, w, name + "_fwd"), (a, w)

    def bwd(res, g):
        a, w = res
        da = _mm(g, w.T, name + "_dx")
        dw = _mm_tn(a, g, name + "_dw")
        return da, jnp.zeros_like(w), dw

    linear.defvjp(fwd, bwd)
    return linear


def make_rowwise(name, f, n_rows, n_seqs, n_pars, out_kinds, ncol=1, nodiff=(), ts_cap=512):
    n_in = n_rows + n_seqs + n_pars
    diff_idx = [i for i in range(n_in) if i not in nodiff]

    def _dims(rows):
        B, S = rows[0].shape[0], rows[0].shape[1]
        ts = min(S, ts_cap)
        return B, S, ts

    def _in_specs(rows, seqs, pars, ts):
        specs = []
        for r in rows:
            specs.append(pl.BlockSpec((1, ts, r.shape[2] // ncol), lambda k, b, s: (b, s, k)))
        for q in seqs:
            specs.append(pl.BlockSpec((1, 1, q.shape[2] // ncol), lambda k, b, s: (b, 0, k)))
        for p in pars:
            specs.append(pl.BlockSpec((1, p.shape[1] // ncol), lambda k, b, s: (0, k)))
        return specs

    def _load(refs):
        vals = [r[0] for r in refs[:n_rows + n_seqs]]
        vals += [r[...] for r in refs[n_rows + n_seqs:n_in]]
        return vals

    def _out_struct(rows, seqs, pars, ts):
        blocks = [jax.ShapeDtypeStruct((ts, r.shape[2] // ncol), r.dtype) for r in rows]
        blocks += [jax.ShapeDtypeStruct((1, q.shape[2] // ncol), q.dtype) for q in seqs]
        blocks += [jax.ShapeDtypeStruct((1, p.shape[1] // ncol), p.dtype) for p in pars]
        return jax.eval_shape(f, *blocks)

    def _fwd_call(rows, seqs, pars):
        B, S, ts = _dims(rows)
        outs = _out_struct(rows, seqs, pars, ts)
        n_out = len(outs)

        def body(*refs):
            res = f(*_load(refs))
            first = (pl.program_id(1) == 0) & (pl.program_id(2) == 0)
            for o_ref, val, kind in zip(refs[n_in:], res, out_kinds):
                if kind == 'row':
                    o_ref[0] = val
                else:
                    tot = jnp.sum(val, axis=0, keepdims=True)

                    @pl.when(first)
                    def _(o_ref=o_ref, tot=tot):
                        o_ref[...] = tot

                    @pl.when(jnp.logical_not(first))
                    def _(o_ref=o_ref, tot=tot):
                        o_ref[...] += tot

        out_shape, out_specs = [], []
        for o, kind in zip(outs, out_kinds):
            d = o.shape[1]
            if kind == 'row':
                out_shape.append(jax.ShapeDtypeStruct((B, S, ncol * d), o.dtype))
                out_specs.append(pl.BlockSpec((1, ts, d), lambda k, b, s: (b, s, k)))
            else:
                out_shape.append(jax.ShapeDtypeStruct((1, ncol * d), o.dtype))
                out_specs.append(pl.BlockSpec((1, d), lambda k, b, s: (0, k)))
        res = pl.pallas_call(
            body, grid=(ncol, B, S // ts), in_specs=_in_specs(rows, seqs, pars, ts), out_specs=out_specs,
            out_shape=out_shape, name=name + "_fwd",
            compiler_params=_cparams(("arbitrary", "arbitrary", "arbitrary")))(*rows, *seqs, *pars)
        return tuple(res)

    def _bwd_call(rows, seqs, pars, cots):
        B, S, ts = _dims(rows)
        outs = _out_struct(rows, seqs, pars, ts)
        n_out = len(outs)
        all_in = list(rows) + list(seqs) + list(pars)

        def body(*refs):
            vals = _load(refs)
            cts = []
            for c_ref, o, kind in zip(refs[n_in:n_in + n_out], outs, out_kinds):
                if kind == 'row':
                    cts.append(c_ref[0])
                else:
                    cts.append(jnp.broadcast_to(c_ref[...], o.shape))

            def g(*dv):
                full = list(vals)
                for i, v in zip(diff_idx, dv):
                    full[i] = v
                return tuple(f(*full))

            _, vjp = jax.vjp(g, *[vals[i] for i in diff_idx])
            grads = vjp(tuple(cts))
            b, s = pl.program_id(1), pl.program_id(2)
            for o_ref, i, gr in zip(refs[n_in + n_out:], diff_idx, grads):
                if i < n_rows:
                    o_ref[0] = gr
                else:
                    first = (s == 0) if i < n_rows + n_seqs else ((b == 0) & (s == 0))
                    target = (lambda r: r.at[0]) if i < n_rows + n_seqs else (lambda r: r)

                    @pl.when(first)
                    def _(o_ref=o_ref, gr=gr, target=target):
                        target(o_ref)[...] = gr

                    @pl.when(jnp.logical_not(first))
                    def _(o_ref=o_ref, gr=gr, target=target):
                        target(o_ref)[...] += gr

        cot_specs = []
        for o, kind in zip(outs, out_kinds):
            d = o.shape[1]
            if kind == 'row':
                cot_specs.append(pl.BlockSpec((1, ts, d), lambda k, b, s: (b, s, k)))
            else:
                cot_specs.append(pl.BlockSpec((1, d), lambda k, b, s: (0, k)))
        out_shape, out_specs = [], []
        for i in diff_idx:
            a = all_in[i]
            out_shape.append(jax.ShapeDtypeStruct(a.shape, a.dtype))
            if i < n_rows:
                out_specs.append(pl.BlockSpec((1, ts, a.shape[2] // ncol), lambda k, b, s: (b, s, k)))
            elif i < n_rows + n_seqs:
                out_specs.append(pl.BlockSpec((1, 1, a.shape[2] // ncol), lambda k, b, s: (b, 0, k)))
            else:
                out_specs.append(pl.BlockSpec((1, a.shape[1] // ncol), lambda k, b, s: (0, k)))
        res = pl.pallas_call(
            body, grid=(ncol, B, S // ts), in_specs=_in_specs(rows, seqs, pars, ts) + cot_specs,
            out_specs=out_specs, out_shape=out_shape, name=name + "_bwd",
            compiler_params=_cparams(("arbitrary", "arbitrary", "arbitrary")))(*all_in, *cots)
        grads = [None] * n_in
        for i, r in zip(diff_idx, res):
            grads[i] = r
        for i in nodiff:
            grads[i] = jnp.zeros_like(all_in[i])
        return tuple(grads[:n_rows]), tuple(grads[n_rows:n_rows + n_seqs]), tuple(grads[n_rows + n_seqs:])

    @jax.custom_vjp
    def op(rows, seqs, pars):
        return _fwd_call(rows, seqs, pars)

    def fwd(rows, seqs, pars):
        return _fwd_call(rows, seqs, pars), (rows, seqs, pars)

    def bwd(res, cots):
        rows, seqs, pars = res
        return _bwd_call(rows, seqs, pars, cots)

    op.defvjp(fwd, bwd)
    return op


def _rms(x, g):
    return x * lax.rsqrt(jnp.mean(x * x, axis=-1, keepdims=True) + EPS) * g


def _silu(x):
    return x * lax.logistic(x)


def _f_silu(c):
    return (_silu(c),)


def _f_modulate(x, scale, shift, g):
    return (_rms(x, g) * (1.0 + scale) + shift,)


def _f_rms(x, g):
    return (_rms(x, g),)


def _f_dt(dt_raw, dt_bias, a_log):
    z = dt_raw + dt_bias
    dt = jnp.maximum(z, 0.0) + jnp.log1p(jnp.exp(-jnp.abs(z)))
    return dt, dt * (-jnp.exp(a_log))


def _f_gated_norm(y, z, g):
    return (_rms(y * _silu(z), g),)


def _f_merge(attn, ssm, ga, gb):
    return (lax.logistic(ga) * attn + lax.logistic(gb) * ssm,)


def _f_post(x, m, gate, g):
    return (x + gate * _rms(m, g),)


def _f_relu2(u):
    r = jnp.maximum(u, 0.0)
    return (r * r,)


def _f_final_loss(x, ff, target, gate, g):
    e = x + gate * _rms(ff, g) - target
    return (e * e * (0.5 / D_MODEL),)


def _rope_tables(posf, inv_lane):
    B, S, _ = posf.shape
    ts = min(S, 512)

    def body(p_ref, inv_ref, c_ref, a_ref, b_ref):
        ang = p_ref[0] * inv_ref[...]
        cs, sn = jnp.cos(ang), jnp.sin(ang)
        lane = lax.broadcasted_iota(jnp.int32, ang.shape, 1)
        c_ref[0] = jnp.where(lane < ROPE, cs, 0.0)
        a_ref[0] = jnp.where(lane < ROPE // 2, -sn, 0.0)
        b_ref[0] = jnp.where((lane >= ROPE // 2) & (lane < ROPE), sn, 0.0)

    spec = pl.BlockSpec((1, ts, LANE), lambda b, s: (b, s, 0))
    sds = jax.ShapeDtypeStruct((B, S, LANE), F32)
    return pl.pallas_call(
        body, grid=(B, S // ts),
        in_specs=[pl.BlockSpec((1, ts, 1), lambda b, s: (b, s, 0)), pl.BlockSpec((1, LANE), lambda b, s: (0, 0))],
        out_specs=[spec, spec, spec], out_shape=[sds, sds, sds], name="rope_tables",
        compiler_params=_cparams(("parallel", "parallel")))(posf, inv_lane)


def _rot(u, c, a, bm):
    return u * c + pltpu.roll(u, 96, 1) * a + pltpu.roll(u, 32, 1) * bm


def _rot_t(g, c, a, bm):
    return g * c + pltpu.roll(g * a, 32, 1) + pltpu.roll(g * bm, 96, 1)


def _rope_q_call(q, tabs, transpose, name):
    B, S, W = q.shape
    ts = min(S, 512)
    fn = _rot_t if transpose else _rot

    def body(q_ref, c_ref, a_ref, b_ref, o_ref):
        u = q_ref[0]
        r = fn(u[:, NOPE:], c_ref[0], a_ref[0], b_ref[0])
        o_ref[0] = jnp.concatenate([u[:, :NOPE], r], axis=1)

    tspec = pl.BlockSpec((1, ts, LANE), lambda b, s, h: (b, s, 0))
    qspec = pl.BlockSpec((1, ts, QK_PAD), lambda b, s, h: (b, s, h))
    return pl.pallas_call(
        body, grid=(B, S // ts, W // QK_PAD), in_specs=[qspec, tspec, tspec, tspec], out_specs=qspec,
        out_shape=jax.ShapeDtypeStruct(q.shape, q.dtype), name=name,
        compiler_params=_cparams(("parallel", "parallel", "parallel")))(q, *tabs)


@jax.custom_vjp
def rope_q(q, tabs):
    return _rope_q_call(q, tabs, False, "rope_q_fwd")


def _rope_q_fwd(q, tabs):
    return _rope_q_call(q, tabs, False, "rope_q_fwd"), tabs


def _rope_q_bwd(tabs, g):
    return _rope_q_call(g, tabs, True, "rope_q_bwd"), tuple(jnp.zeros_like(t) for t in tabs)


rope_q.defvjp(_rope_q_fwd, _rope_q_bwd)


def _build_k_fwd_call(kv, kr, tabs):
    B, S, _ = kv.shape
    ts = min(S, 512)

    def body(kv_ref, kr_ref, c_ref, a_ref, b_ref, o_ref):
        r = _rot(kr_ref[0], c_ref[0], a_ref[0], b_ref[0])
        o_ref[0] = jnp.concatenate([kv_ref[0], r], axis=1)

    tspec = pl.BlockSpec((1, ts, LANE), lambda b, s, h: (b, s, 0))
    return pl.pallas_call(
        body, grid=(B, S // ts, N_HEADS),
        in_specs=[pl.BlockSpec((1, ts, LANE), lambda b, s, h: (b, s, h)), tspec, tspec, tspec, tspec],
        out_specs=pl.BlockSpec((1, ts, QK_PAD), lambda b, s, h: (b, s, h)),
        out_shape=jax.ShapeDtypeStruct((B, S, N_HEADS * QK_PAD), F32), name="build_k_fwd",
        compiler_params=_cparams(("parallel", "parallel", "arbitrary")))(kv, kr, *tabs)


def _build_k_bwd_call(g, tabs):
    B, S, _ = g.shape
    ts = min(S, 512)

    def body(g_ref, c_ref, a_ref, b_ref, dk_ref, dr_ref):
        gg = g_ref[0]
        dk_ref[0] = gg[:, :NOPE]
        r = _rot_t(gg[:, NOPE:], c_ref[0], a_ref[0], b_ref[0])
        h = pl.program_id(2)

        @pl.when(h == 0)
        def _():
            dr_ref[0] = r

        @pl.when(h > 0)
        def _():
            dr_ref[0] += r

    tspec = pl.BlockSpec((1, ts, LANE), lambda b, s, h: (b, s, 0))
    return pl.pallas_call(
        body, grid=(B, S // ts, N_HEADS),
        in_specs=[pl.BlockSpec((1, ts, QK_PAD), lambda b, s, h: (b, s, h)), tspec, tspec, tspec],
        out_specs=[pl.BlockSpec((1, ts, LANE), lambda b, s, h: (b, s, h)), tspec],
        out_shape=[jax.ShapeDtypeStruct((B, S, N_HEADS * NOPE), F32), jax.ShapeDtypeStruct((B, S, LANE), F32)],
        name="build_k_bwd", compiler_params=_cparams(("parallel", "parallel", "arbitrary")))(g, *tabs)


@jax.custom_vjp
def build_k(kv, kr, tabs):
    return _build_k_fwd_call(kv, kr, tabs)


def _build_k_fwd(kv, kr, tabs):
    return _build_k_fwd_call(kv, kr, tabs), (tabs, kv.shape)


def _build_k_bwd(res, g):
    tabs, kv_shape = res
    dk, dr = _build_k_bwd_call(g, tabs)
    dkv = jnp.concatenate([dk, jnp.zeros((kv_shape[0], kv_shape[1], kv_shape[2] - dk.shape[2]), F32)], axis=-1)
    return dkv, dr, tuple(jnp.zeros_like(t) for t in tabs)


build_k.defvjp(_build_k_fwd, _build_k_bwd)


ATT_SCALE = (NOPE + ROPE) ** -0.5
NEG = -1e30


def _att_tiles(S):
    t = min(S, 512)
    return t, S // t


def _scores(q, k, i, j, t):
    s = lax.dot_general(q, k, (((1,), (1,)), ((), ())), preferred_element_type=F32) * ATT_SCALE
    row = lax.broadcasted_iota(jnp.int32, s.shape, 0) + i * t
    col = lax.broadcasted_iota(jnp.int32, s.shape, 1) + j * t
    return jnp.where(col <= row, s, NEG)


def _attn_fwd_call(q, k, vsrc, v_blk0):
    B, S, _ = q.shape
    t, n = _att_tiles(S)

    def body(q_ref, k_ref, v_ref, o_ref, lse_ref, m_sc, l_sc, acc_sc):
        i, j = pl.program_id(2), pl.program_id(3)

        @pl.when(j == 0)
        def _():
            m_sc[...] = jnp.full(m_sc.shape, NEG, F32)
            l_sc[...] = jnp.zeros(l_sc.shape, F32)
            acc_sc[...] = jnp.zeros(acc_sc.shape, F32)

        @pl.when(j <= i)
        def _():
            s = _scores(q_ref[0].astype(BF16), k_ref[0].astype(BF16), i, j, t)
            m_prev = m_sc[...]
            m_new = jnp.maximum(m_prev, jnp.max(s, axis=1, keepdims=True))
            alpha = jnp.exp(m_prev - m_new)
            p = jnp.exp(s - m_new)
            l_sc[...] = alpha * l_sc[...] + jnp.sum(p, axis=1, keepdims=True)
            acc_sc[...] = alpha * acc_sc[...] + jnp.dot(p.astype(BF16), v_ref[0].astype(BF16),
                                                        preferred_element_type=F32)
            m_sc[...] = m_new

        @pl.when(j == i)
        def _():
            o_ref[0] = acc_sc[...] / l_sc[...]
            lse_ref[0] = jnp.broadcast_to(m_sc[...] + jnp.log(l_sc[...]), (t, LANE))

    return pl.pallas_call(
        body, grid=(B, N_HEADS, n, n),
        in_specs=[pl.BlockSpec((1, t, QK_PAD), lambda b, h, i, j: (b, i, h)),
                  pl.BlockSpec((1, t, QK_PAD), lambda b, h, i, j: (b, jnp.minimum(j, i), h)),
                  pl.BlockSpec((1, t, V_DIM), lambda b, h, i, j: (b, jnp.minimum(j, i), v_blk0 + h))],
        out_specs=[pl.BlockSpec((1, t, V_DIM), lambda b, h, i, j: (b, i, h)),
                   pl.BlockSpec((1, t, LANE), lambda b, h, i, j: (b, i, h))],
        out_shape=[jax.ShapeDtypeStruct((B, S, N_HEADS * V_DIM), F32),
                   jax.ShapeDtypeStruct((B, S, N_HEADS * LANE), F32)],
        scratch_shapes=[pltpu.VMEM((t, 1), F32), pltpu.VMEM((t, 1), F32), pltpu.VMEM((t, V_DIM), F32)],
        name="attn_fwd", compiler_params=_cparams(("parallel", "parallel", "parallel", "arbitrary")))(q, k, vsrc)


def _attn_p_ds(q, k, v, o, do, lse, i, j, t):
    s = _scores(q, k, i, j, t)
    p = jnp.exp(s - jnp.tile(lse, (1, t // LANE)))
    dp = lax.dot_general(do.astype(BF16), v, (((1,), (1,)), ((), ())), preferred_element_type=F32)
    delta = jnp.sum(do * o, axis=1, keepdims=True)
    ds = p * (dp - delta) * ATT_SCALE
    return p, ds


def _attn_dkv_call(q, k, vsrc, v_blk0, o, do, lse):
    B, S, _ = q.shape
    t, n = _att_tiles(S)

    def body(q_ref, k_ref, v_ref, o_ref, do_ref, lse_ref, dk_ref, dv_ref, dk_sc, dv_sc):
        j, i = pl.program_id(2), pl.program_id(3)

        @pl.when(i == 0)
        def _():
            dk_sc[...] = jnp.zeros(dk_sc.shape, F32)
            dv_sc[...] = jnp.zeros(dv_sc.shape, F32)

        @pl.when(i >= j)
        def _():
            qb = q_ref[0].astype(BF16)
            dob = do_ref[0]
            p, ds = _attn_p_ds(qb, k_ref[0].astype(BF16), v_ref[0].astype(BF16), o_ref[0], dob, lse_ref[0], i, j, t)
            dv_sc[...] += lax.dot_general(p.astype(BF16), dob.astype(BF16), (((0,), (0,)), ((), ())),
                                          preferred_element_type=F32)
            dk_sc[...] += lax.dot_general(ds.astype(BF16), qb, (((0,), (0,)), ((), ())),
                                          preferred_element_type=F32)

        @pl.when(i == n - 1)
        def _():
            dk_ref[0] = dk_sc[...]
            dv_ref[0] = dv_sc[...]

    qi = lambda b, h, j, i: (b, jnp.maximum(i, j), h)
    return pl.pallas_call(
        body, grid=(B, N_HEADS, n, n),
        in_specs=[pl.BlockSpec((1, t, QK_PAD), qi),
                  pl.BlockSpec((1, t, QK_PAD), lambda b, h, j, i: (b, j, h)),
                  pl.BlockSpec((1, t, V_DIM), lambda b, h, j, i: (b, j, v_blk0 + h)),
                  pl.BlockSpec((1, t, V_DIM), qi), pl.BlockSpec((1, t, V_DIM), qi), pl.BlockSpec((1, t, LANE), qi)],
        out_specs=[pl.BlockSpec((1, t, QK_PAD), lambda b, h, j, i: (b, j, h)),
                   pl.BlockSpec((1, t, V_DIM), lambda b, h, j, i: (b, j, h))],
        out_shape=[jax.ShapeDtypeStruct((B, S, N_HEADS * QK_PAD), F32),
                   jax.ShapeDtypeStruct((B, S, N_HEADS * V_DIM), F32)],
        scratch_shapes=[pltpu.VMEM((t, QK_PAD), F32), pltpu.VMEM((t, V_DIM), F32)],
        name="attn_dkv", compiler_params=_cparams(("parallel", "parallel", "parallel", "arbitrary")))(
            q, k, vsrc, o, do, lse)


def _attn_dq_call(q, k, vsrc, v_blk0, o, do, lse):
    B, S, _ = q.shape
    t, n = _att_tiles(S)

    def body(q_ref, k_ref, v_ref, o_ref, do_ref, lse_ref, dq_ref, dq_sc):
        i, j = pl.program_id(2), pl.program_id(3)

        @pl.when(j == 0)
        def _():
            dq_sc[...] = jnp.zeros(dq_sc.shape, F32)

        @pl.when(j <= i)
        def _():
            kb = k_ref[0].astype(BF16)
            _, ds = _attn_p_ds(q_ref[0].astype(BF16), kb, v_ref[0].astype(BF16), o_ref[0], do_ref[0], lse_ref[0],
                               i, j, t)
            dq_sc[...] += jnp.dot(ds.astype(BF16), kb, preferred_element_type=F32)

        @pl.when(j == n - 1)
        def _():
            dq_ref[0] = dq_sc[...]

    qi = lambda b, h, i, j: (b, i, h)
    kj = lambda b, h, i, j: (b, jnp.minimum(j, i), h)
    return pl.pallas_call(
        body, grid=(B, N_HEADS, n, n),
        in_specs=[pl.BlockSpec((1, t, QK_PAD), qi), pl.BlockSpec((1, t, QK_PAD), kj),
                  pl.BlockSpec((1, t, V_DIM), lambda b, h, i, j: (b, jnp.minimum(j, i), v_blk0 + h)),
                  pl.BlockSpec((1, t, V_DIM), qi), pl.BlockSpec((1, t, V_DIM), qi), pl.BlockSpec((1, t, LANE), qi)],
        out_specs=pl.BlockSpec((1, t, QK_PAD), qi),
        out_shape=jax.ShapeDtypeStruct((B, S, N_HEADS * QK_PAD), F32),
        scratch_shapes=[pltpu.VMEM((t, QK_PAD), F32)],
        name="attn_dq", compiler_params=_cparams(("parallel", "parallel", "parallel", "arbitrary")))(
            q, k, vsrc, o, do, lse)


@jax.custom_vjp
def attention(q, k, kv):
    return _attn_fwd_call(q, k, kv, N_HEADS)[0]


def _attention_fwd(q, k, kv):
    o, lse = _attn_fwd_call(q, k, kv, N_HEADS)
    return o, (q, k, kv, o, lse)


def _attention_bwd(res, do):
    q, k, kv, o, lse = res
    dk, dv = _attn_dkv_call(q, k, kv, N_HEADS, o, do, lse)
    dq = _attn_dq_call(q, k, kv, N_HEADS, o, do, lse)
    dkv = jnp.concatenate([jnp.zeros_like(dv), dv], axis=-1)
    return dq, dk, dkv


attention.defvjp(_attention_fwd, _attention_bwd)


def _shift_down(v, sh, rows):
    return jnp.where(rows >= sh, pltpu.roll(v, sh, 0), 0.0)


def _shift_up(v, sh, rows, S):
    return jnp.where(rows < S - sh, pltpu.roll(v, S - sh, 0), 0.0)


def _conv_pre(u, w_ref, b_ref, rows):
    acc = b_ref[...] + w_ref[pl.ds(CONV_K - 1, 1), :] * u
    for k in range(CONV_K - 1):
        acc = acc + w_ref[pl.ds(k, 1), :] * _shift_down(u, CONV_K - 1 - k, rows)
    return acc


def _conv_fwd_call(u, w, b):
    B, S, C = u.shape

    def body(u_ref, w_ref, b_ref, o_ref):
        uu = u_ref[0]
        rows = lax.broadcasted_iota(jnp.int32, uu.shape, 0)
        o_ref[0] = _silu(_conv_pre(uu, w_ref, b_ref, rows))

    spec = pl.BlockSpec((1, S, LANE), lambda c, bb: (bb, 0, c))
    return pl.pallas_call(
        body, grid=(C // LANE, B),
        in_specs=[spec, pl.BlockSpec((CONV_K, LANE), lambda c, bb: (0, c)), pl.BlockSpec((1, LANE), lambda c, bb: (0, c))],
        out_specs=spec, out_shape=jax.ShapeDtypeStruct(u.shape, F32), name="conv_fwd",
        compiler_params=_cparams(("parallel", "arbitrary")))(u, w, b)


def _conv_bwd_call(u, w, b, g):
    B, S, C = u.shape

    def body(u_ref, w_ref, b_ref, g_ref, du_ref, dw_ref, db_ref):
        uu = u_ref[0]
        rows = lax.broadcasted_iota(jnp.int32, uu.shape, 0)
        pre = _conv_pre(uu, w_ref, b_ref, rows)
        sg = lax.logistic(pre)
        dpre = g_ref[0] * sg * (1.0 + pre * (1.0 - sg))
        du = w_ref[pl.ds(CONV_K - 1, 1), :] * dpre
        dws = [None] * CONV_K
        dws[CONV_K - 1] = jnp.sum(dpre * uu, axis=0, keepdims=True)
        for k in range(CONV_K - 1):
            sh = CONV_K - 1 - k
            du = du + w_ref[pl.ds(k, 1), :] * _shift_up(dpre, sh, rows, S)
            dws[k] = jnp.sum(dpre * _shift_down(uu, sh, rows), axis=0, keepdims=True)
        du_ref[0] = du
        dbv = jnp.sum(dpre, axis=0, keepdims=True)
        first = pl.program_id(1) == 0

        @pl.when(first)
        def _():
            for k in range(CONV_K):
                dw_ref[pl.ds(k, 1), :] = dws[k]
            db_ref[...] = dbv

        @pl.when(jnp.logical_not(first))
        def _():
            for k in range(CONV_K):
                dw_ref[pl.ds(k, 1), :] += dws[k]
            db_ref[...] += dbv

    spec = pl.BlockSpec((1, S, LANE), lambda c, bb: (bb, 0, c))
    wspec = pl.BlockSpec((CONV_K, LANE), lambda c, bb: (0, c))
    bspec = pl.BlockSpec((1, LANE), lambda c, bb: (0, c))
    return pl.pallas_call(
        body, grid=(C // LANE, B), in_specs=[spec, wspec, bspec, spec], out_specs=[spec, wspec, bspec],
        out_shape=[jax.ShapeDtypeStruct(u.shape, F32), jax.ShapeDtypeStruct(w.shape, F32),
                   jax.ShapeDtypeStruct(b.shape, F32)],
        name="conv_bwd", compiler_params=_cparams(("parallel", "arbitrary")))(u, w, b, g)


@jax.custom_vjp
def conv_silu(u, w, b):
    return _conv_fwd_call(u, w, b)


def _conv_silu_fwd(u, w, b):
    return _conv_fwd_call(u, w, b), (u, w, b)


def _conv_silu_bwd(res, g):
    return tuple(_conv_bwd_call(*res, g))


conv_silu.defvjp(_conv_silu_fwd, _conv_silu_bwd)


def _chunk_cumsum_call(a, reverse, name):
    B, S, W = a.shape

    def body(a_ref, o_ref):
        r = lax.broadcasted_iota(jnp.int32, (CHUNK, CHUNK), 0)
        c = lax.broadcasted_iota(jnp.int32, (CHUNK, CHUNK), 1)
        tri = jnp.where((c >= r) if reverse else (c <= r), 1.0, 0.0).astype(F32)
        o_ref[0] = jnp.dot(tri, a_ref[0], preferred_element_type=F32, precision=lax.Precision.HIGHEST)

    spec = pl.BlockSpec((1, CHUNK, W), lambda b, c: (b, c, 0))
    return pl.pallas_call(body, grid=(B, S // CHUNK), in_specs=[spec], out_specs=spec,
                          out_shape=jax.ShapeDtypeStruct(a.shape, F32), name=name,
                          compiler_params=_cparams(("parallel", "parallel")))(a)


@jax.custom_vjp
def chunk_cumsum(a):
    return _chunk_cumsum_call(a, False, "chunk_cumsum_fwd")


chunk_cumsum.defvjp(lambda a: (_chunk_cumsum_call(a, False, "chunk_cumsum_fwd"), None),
                    lambda _, g: (_chunk_cumsum_call(g, True, "chunk_cumsum_bwd"),))


GROUP_W = 4 * HEAD_P
HPG = SSM_HEADS // SSM_GROUPS


def _ssd_masks():
    lane = lax.broadcasted_iota(jnp.int32, (1, GROUP_W), 1)
    return [((lane >= HEAD_P * j) & (lane < HEAD_P * (j + 1))).astype(F32) for j in range(HPG)]


def _ssd_decays(acx, acr_ref, masks):
    r = lax.broadcasted_iota(jnp.int32, (CHUNK, CHUNK), 0)
    c = lax.broadcasted_iota(jnp.int32, (CHUNK, CHUNK), 1)
    out = []
    for j in range(HPG):
        col = jnp.min(acx * masks[j], axis=1, keepdims=True)
        seg = col - acr_ref[0, j]
        out.append(jnp.exp(jnp.where(c <= r, seg, NEG)))
    return out


def _dot(a, b, dims):
    return lax.dot_general(a.astype(BF16), b.astype(BF16), (dims, ((), ())), preferred_element_type=F32)


NN = ((1,), (0,))
NT = ((1,), (1,))
TN = ((0,), (0,))


def _ssd_in_specs(nc, rev):
    cc = (lambda c: nc - 1 - c) if rev else (lambda c: c)
    specs = [pl.BlockSpec((1, CHUNK, GROUP_W), lambda b, g, c: (b, cc(c), g)),
             pl.BlockSpec((1, CHUNK, STATE_N), lambda b, g, c: (b, cc(c), D_INNER // STATE_N + g)),
             pl.BlockSpec((1, CHUNK, STATE_N), lambda b, g, c: (b, cc(c), D_INNER // STATE_N + SSM_GROUPS + g)),
             pl.BlockSpec((1, CHUNK, GROUP_W), lambda b, g, c: (b, cc(c), g)),
             pl.BlockSpec((1, CHUNK, GROUP_W), lambda b, g, c: (b, cc(c), g))]
    specs.append(pl.BlockSpec((1, HPG, 1, CHUNK), lambda b, g, c: (b, g, 0, cc(c))))
    specs.append(pl.BlockSpec((1, 1, 1, GROUP_W), lambda b, g, c: (b, cc(c), 0, g)))
    specs.append(pl.BlockSpec((1, GROUP_W), lambda b, g, c: (0, g)))
    return specs


def _ssd_fwd_call(xbc, dtx, acx, acr, acl, dsk):
    B, S, _ = xbc.shape
    nc = S // CHUNK

    def body(x_ref, b_ref, c_ref, dt_ref, ac_ref, ar_ref, al_ref, ds_ref, y_ref, hp_ref, h_sc):
        @pl.when(pl.program_id(2) == 0)
        def _():
            h_sc[...] = jnp.zeros(h_sc.shape, F32)

        x, bm, cm, dt, ac = x_ref[0], b_ref[0], c_ref[0], dt_ref[0], ac_ref[0]
        last = al_ref[0, 0]
        masks = _ssd_masks()
        decays = _ssd_decays(ac, ar_ref, masks)
        xd = x * dt
        cb = _dot(cm, bm, NT)
        hprev = h_sc[...]
        hp_ref[0, 0, 0] = hprev
        y = _dot(cm, hprev, NN) * jnp.exp(ac) + ds_ref[...] * x
        for j in range(HPG):
            y = y + _dot(cb * decays[j], xd * masks[j], NN)
        y_ref[0] = y
        h_sc[...] = hprev * jnp.exp(last) + _dot(bm, xd * jnp.exp(last - ac), TN)

    return pl.pallas_call(
        body, grid=(B, SSM_GROUPS, nc), in_specs=_ssd_in_specs(nc, False),
        out_specs=[pl.BlockSpec((1, CHUNK, GROUP_W), lambda b, g, c: (b, c, g)),
                   pl.BlockSpec((1, 1, 1, STATE_N, GROUP_W), lambda b, g, c: (b, g, c, 0, 0))],
        out_shape=[jax.ShapeDtypeStruct((B, S, D_INNER), F32),
                   jax.ShapeDtypeStruct((B, SSM_GROUPS, nc, STATE_N, GROUP_W), F32)],
        scratch_shapes=[pltpu.VMEM((STATE_N, GROUP_W), F32)], name="ssd_fwd",
        compiler_params=_cparams(("parallel", "parallel", "arbitrary")))(
            xbc, xbc, xbc, dtx, acx, acr, acl, dsk)


def _ssd_bwd_call(xbc, dtx, acx, acr, acl, dsk, hps, dy):
    B, S, _ = xbc.shape
    nc = S // CHUNK

    def body(x_ref, b_ref, c_ref, dt_ref, ac_ref, ar_ref, al_ref, ds_ref, hp_ref, dy_ref,
             dx_ref, db_ref, dc_ref, ddt_ref, dac_ref, dar_ref, dal_ref, dds_ref, dh_sc):
        first = pl.program_id(2) == 0

        @pl.when(first)
        def _():
            dh_sc[...] = jnp.zeros(dh_sc.shape, F32)

        x, bm, cm, dt, ac = x_ref[0], b_ref[0], c_ref[0], dt_ref[0], ac_ref[0]
        last = al_ref[0, 0]
        g = dy_ref[0]
        hprev = hp_ref[0, 0, 0]
        dh = dh_sc[...]
        masks = _ssd_masks()
        decays = _ssd_decays(ac, ar_ref, masks)
        xd = x * dt
        cb = _dot(cm, bm, NT)
        e_c = jnp.exp(ac)
        e_end = jnp.exp(last - ac)
        e_last = jnp.exp(last)
        z = _dot(cm, hprev, NN)
        dz = g * e_c
        dac = g * z * e_c
        dc = _dot(dz, hprev, NT)
        dhprev = _dot(cm, dz, TN) + dh * e_last
        dcb = jnp.zeros((CHUNK, CHUNK), F32)
        dxd = jnp.zeros(xd.shape, F32)
        for j in range(HPG):
            gj = cb * decays[j]
            dgj = _dot(g * masks[j], xd, NT)
            dxd = dxd + _dot(gj, g, TN) * masks[j]
            dcb = dcb + dgj * decays[j]
            dseg = dgj * gj
            dac = dac + jnp.sum(dseg, axis=1, keepdims=True) * masks[j] * (1.0 / HEAD_P)
            dar_ref[0, j] = -jnp.sum(dseg, axis=0, keepdims=True)
        dc = dc + _dot(dcb, bm, NN)
        db = _dot(dcb, cm, TN)
        sx = xd * e_end
        db = db + _dot(sx, dh, NT)
        dsx = _dot(bm, dh, NN)
        dxd = dxd + dsx * e_end
        de = dsx * sx
        dac = dac - de
        dlast = jnp.sum(de, axis=0, keepdims=True) + jnp.sum(dh * hprev, axis=0, keepdims=True) * e_last
        dsk = ds_ref[...]
        dx_ref[0] = dxd * dt + dsk * g
        ddt_ref[0] = dxd * x
        dac_ref[0] = dac
        db_ref[0] = db
        dc_ref[0] = dc
        dal_ref[0, 0] = dlast
        dds = jnp.sum(g * x, axis=0, keepdims=True)
        first_all = first & (pl.program_id(1) == 0)

        @pl.when(first_all)
        def _():
            dds_ref[...] = dds

        @pl.when(jnp.logical_not(first_all))
        def _():
            dds_ref[...] += dds

        dh_sc[...] = dhprev

    rc = lambda c: nc - 1 - c
    gw = pl.BlockSpec((1, CHUNK, GROUP_W), lambda g, b, c: (b, rc(c), g))
    sn = pl.BlockSpec((1, CHUNK, STATE_N), lambda g, b, c: (b, rc(c), g))
    in_specs = [pl.BlockSpec(s.block_shape, (lambda g, b, c, f=s.index_map: f(b, g, c))) for s in _ssd_in_specs(nc, True)]
    in_specs.append(pl.BlockSpec((1, 1, 1, STATE_N, GROUP_W), lambda g, b, c: (b, g, rc(c), 0, 0)))
    in_specs.append(gw)
    out_specs = [gw, sn, sn, gw, gw]
    out_shape = [jax.ShapeDtypeStruct((B, S, D_INNER), F32),
                 jax.ShapeDtypeStruct((B, S, SSM_GROUPS * STATE_N), F32),
                 jax.ShapeDtypeStruct((B, S, SSM_GROUPS * STATE_N), F32),
                 jax.ShapeDtypeStruct((B, S, D_INNER), F32), jax.ShapeDtypeStruct((B, S, D_INNER), F32)]
    out_specs.append(pl.BlockSpec((1, HPG, 1, CHUNK), lambda g, b, c: (b, g, 0, rc(c))))
    out_shape.append(jax.ShapeDtypeStruct(acr.shape, F32))
    out_specs.append(pl.BlockSpec((1, 1, 1, GROUP_W), lambda g, b, c: (b, rc(c), 0, g)))
    out_shape.append(jax.ShapeDtypeStruct(acl.shape, F32))
    out_specs.append(pl.BlockSpec((1, GROUP_W), lambda g, b, c: (0, g)))
    out_shape.append(jax.ShapeDtypeStruct(dsk.shape, F32))
    return pl.pallas_call(
        body, grid=(SSM_GROUPS, B, nc), in_specs=in_specs, out_specs=out_specs, out_shape=out_shape,
        scratch_shapes=[pltpu.VMEM((STATE_N, GROUP_W), F32)], name="ssd_bwd",
        compiler_params=_cparams(("arbitrary", "arbitrary", "arbitrary")))(
            xbc, xbc, xbc, dtx, acx, acr, acl, dsk, hps, dy)


@jax.custom_vjp
def ssd(xbc, dtx, acx, acr, acl, dsk):
    return _ssd_fwd_call(xbc, dtx, acx, acr, acl, dsk)[0]


def _ssd_fwd(xbc, dtx, acx, acr, acl, dsk):
    y, hps = _ssd_fwd_call(xbc, dtx, acx, acr, acl, dsk)
    return y, (xbc, dtx, acx, acr, acl, dsk, hps)


def _ssd_bwd(res, dy):
    dx, db, dc, ddt, dac, dacr, dal, dds = _ssd_bwd_call(*res, dy)
    return jnp.concatenate([dx, db, dc], axis=-1), ddt, dac, dacr, dal, dds


ssd.defvjp(_ssd_fwd, _ssd_bwd)


def _pack_rows(n_elems):
    rows = -(-n_elems // PACK_W)
    return -(-rows // PACK_ROW_ALIGN) * PACK_ROW_ALIGN


def _pack(arrs, dtype):
    flat = jnp.concatenate([a.reshape(-1).astype(dtype) for a in arrs])
    rows = _pack_rows(flat.shape[0])
    flat = jnp.pad(flat, (0, rows * PACK_W - flat.shape[0]))
    return flat.reshape(rows, PACK_W)


def _unpack(buf, shapes):
    flat = buf.reshape(-1)
    out, off = [], 0
    for shp in shapes:
        n = int(np.prod(shp))
        out.append(flat[off:off + n].reshape(shp))
        off += n
    return out


def _elementwise_call(fn, ins, n_out, name, tr_cap=512):
    R = ins[-1].shape[-2]
    tr = R
    for cand in range(min(R, tr_cap), 7, -8):
        if R % cand == 0:
            tr = cand
            break

    def body(*refs):
        res = fn(*[r[...] for r in refs[:len(ins)]])
        for o_ref, v in zip(refs[len(ins):], res):
            o_ref[...] = v

    in_specs = []
    for a in ins:
        if a.ndim == 3:
            in_specs.append(pl.BlockSpec((a.shape[0], tr, PACK_W), lambda i: (0, i, 0)))
        else:
            in_specs.append(pl.BlockSpec((tr, PACK_W), lambda i: (i, 0)))
    spec = pl.BlockSpec((tr, PACK_W), lambda i: (i, 0))
    return pl.pallas_call(
        body, grid=(R // tr,), in_specs=in_specs, out_specs=[spec] * n_out,
        out_shape=[jax.ShapeDtypeStruct((R, PACK_W), F32)] * n_out, name=name,
        compiler_params=_cparams(("parallel",)))(*ins)


def _adam_fn(w, g, m, v):
    m = ADAM_B1 * m + (1.0 - ADAM_B1) * g
    v = ADAM_B2 * v + (1.0 - ADAM_B2) * (g * g)
    m_hat = m / (1.0 - ADAM_B1 ** ADAM_STEP)
    v_hat = v / (1.0 - ADAM_B2 ** ADAM_STEP)
    delta = -ADAM_LR * (m_hat / (jnp.sqrt(v_hat) + ADAM_EPS) + ADAM_WD * w)
    return delta, m, v


def _mesh_pos():
    return lax.axis_index("x"), lax.axis_index("y"), lax.axis_index("c")


def _other_chips(x, y):
    return [(1 - x, y), (x, 1 - y), (1 - x, 1 - y)]


HBM_SPEC = pl.BlockSpec(memory_space=pl.ANY)


def _all_gather_shards(shard):
    R, W = shard.shape
    half = R // 2

    def body(in_ref, out_ref, send_sems, recv_sems, local_sem):
        x, y, c = _mesh_pos()
        me, sibling = (x, y, c), (x, y, 1 - c)
        chips = _other_chips(x, y)
        mine_rows = pl.ds(pl.multiple_of(c * half, 8), half)
        sib_rows = pl.ds(pl.multiple_of((1 - c) * half, 8), half)

        def copy(k, src, dst, to):
            return pltpu.make_async_remote_copy(src_ref=src, dst_ref=dst, send_sem=send_sems.at[k],
                                                recv_sem=recv_sems.at[k], device_id=to, device_id_type=MESH)

        own = pltpu.make_async_copy(in_ref, out_ref.at[2 * x + y], local_sem)
        own.start()
        first = [copy(j, in_ref.at[mine_rows], out_ref.at[2 * x + y, mine_rows], (cx, cy, c))
                 for j, (cx, cy) in enumerate(chips)]
        for cp in first:
            cp.start()
        passed = []
        for j, (cx, cy) in enumerate(chips):
            landed = out_ref.at[2 * cx + cy, mine_rows]
            copy(j, landed, landed, me).wait_recv()
            fwd = copy(3 + j, landed, landed, sibling)
            fwd.start()
            passed.append(fwd)
        for j, (cx, cy) in enumerate(chips):
            theirs = out_ref.at[2 * cx + cy, sib_rows]
            copy(3 + j, theirs, theirs, me).wait_recv()
        for cp in first + passed:
            cp.wait_send()
        own.wait()

    return pl.pallas_call(
        body, in_specs=[HBM_SPEC], out_specs=HBM_SPEC, out_shape=jax.ShapeDtypeStruct((4, R, W), shard.dtype),
        scratch_shapes=[pltpu.SemaphoreType.DMA((6,)), pltpu.SemaphoreType.DMA((6,)), pltpu.SemaphoreType.DMA],
        name="all_gather_weights")(shard)


def _sibling_exchange(send):
    def body(in_ref, out_ref, send_sem, recv_sem):
        x, y, c = _mesh_pos()
        cp = pltpu.make_async_remote_copy(src_ref=in_ref, dst_ref=out_ref, send_sem=send_sem, recv_sem=recv_sem,
                                          device_id=(x, y, 1 - c), device_id_type=MESH)
        cp.start()
        cp.wait()

    return pl.pallas_call(
        body, in_specs=[HBM_SPEC], out_specs=HBM_SPEC, out_shape=jax.ShapeDtypeStruct(send.shape, send.dtype),
        scratch_shapes=[pltpu.SemaphoreType.DMA, pltpu.SemaphoreType.DMA], name="grad_sibling_exchange")(send)


def _chip_exchange(part):
    def body(in_ref, out_ref, send_sems, recv_sems, local_sem):
        x, y, c = _mesh_pos()
        s = 2 * x + y
        chips = _other_chips(x, y)
        own = pltpu.make_async_copy(in_ref.at[s], out_ref.at[s], local_sem)
        own.start()
        sends = [pltpu.make_async_remote_copy(src_ref=in_ref.at[2 * cx + cy], dst_ref=out_ref.at[s],
                                              send_sem=send_sems.at[j], recv_sem=recv_sems.at[j],
                                              device_id=(cx, cy, c), device_id_type=MESH)
                 for j, (cx, cy) in enumerate(chips)]
        for cp in sends:
            cp.start()
        for j, (cx, cy) in enumerate(chips):
            slot = out_ref.at[2 * cx + cy]
            pltpu.make_async_remote_copy(src_ref=slot, dst_ref=slot, send_sem=send_sems.at[j],
                                         recv_sem=recv_sems.at[j], device_id=(x, y, c),
                                         device_id_type=MESH).wait_recv()
        for cp in sends:
            cp.wait_send()
        own.wait()

    return pl.pallas_call(
        body, in_specs=[HBM_SPEC], out_specs=HBM_SPEC, out_shape=jax.ShapeDtypeStruct(part.shape, part.dtype),
        scratch_shapes=[pltpu.SemaphoreType.DMA((3,)), pltpu.SemaphoreType.DMA((3,)), pltpu.SemaphoreType.DMA],
        name="grad_chip_exchange")(part)


def _sibling_concat(mine):
    half, W = mine.shape

    def body(in_ref, out_ref, send_sem, recv_sem, local_sem):
        x, y, c = _mesh_pos()
        mine_rows = pl.ds(pl.multiple_of(c * half, 8), half)
        sib_rows = pl.ds(pl.multiple_of((1 - c) * half, 8), half)
        own = pltpu.make_async_copy(in_ref, out_ref.at[mine_rows], local_sem)
        own.start()
        cp = pltpu.make_async_remote_copy(src_ref=in_ref, dst_ref=out_ref.at[mine_rows], send_sem=send_sem,
                                          recv_sem=recv_sem, device_id=(x, y, 1 - c), device_id_type=MESH)
        cp.start()
        theirs = out_ref.at[sib_rows]
        pltpu.make_async_remote_copy(src_ref=theirs, dst_ref=theirs, send_sem=send_sem, recv_sem=recv_sem,
                                     device_id=(x, y, c), device_id_type=MESH).wait_recv()
        cp.wait_send()
        own.wait()

    return pl.pallas_call(
        body, in_specs=[HBM_SPEC], out_specs=HBM_SPEC, out_shape=jax.ShapeDtypeStruct((2 * half, W), mine.dtype),
        scratch_shapes=[pltpu.SemaphoreType.DMA, pltpu.SemaphoreType.DMA, pltpu.SemaphoreType.DMA],
        name="grad_sibling_concat")(mine)


def _reduce_scatter(sections):
    _, R, W = sections.shape
    half = R // 2
    c = lax.axis_index("c")
    keep = lax.dynamic_slice_in_dim(sections, c * half, half, axis=1)
    give = lax.dynamic_slice_in_dim(sections, (1 - c) * half, half, axis=1)
    got = _sibling_exchange(give)
    pair = _elementwise_call(lambda a, b: (a + b,), [keep.reshape(4 * half, W), got.reshape(4 * half, W)], 1,
                             "grad_pair_sum")[0]
    quad = _chip_exchange(pair.reshape(4, half, W))
    mine = _elementwise_call(lambda q: (((q[0] + q[1]) + q[2]) + q[3],), [quad], 1, "grad_chip_sum")[0]
    return _sibling_concat(mine)


def _pad_cols(a, n):
    return jnp.concatenate([a, jnp.zeros((a.shape[0], n - a.shape[1]), a.dtype)], axis=1)


def _lay_w_in(w):
    idx = np.cumsum(IN_SIZES)[:-1]
    segs = jnp.split(w, [int(v) for v in idx], axis=1)
    segs[2] = _pad_cols(segs[2], LANE)
    segs[5] = _pad_cols(segs[5], LANE)
    return jnp.concatenate(segs, axis=1)


IN_PAD_SIZES = (Q_RANK, KV_RANK, LANE, D_INNER, CONV_CH, LANE, D_MODEL, D_MODEL)


def _lay_w_uq(w):
    w3 = w.reshape(Q_RANK, N_HEADS, NOPE + ROPE)
    w3 = jnp.concatenate([w3, jnp.zeros((Q_RANK, N_HEADS, QK_PAD - NOPE - ROPE), w.dtype)], axis=2)
    return w3.reshape(Q_RANK, N_HEADS * QK_PAD)


def _lay_w_ukv(w):
    w3 = w.reshape(KV_RANK, N_HEADS, NOPE + V_DIM)
    return jnp.concatenate([w3[:, :, :NOPE].reshape(KV_RANK, -1), w3[:, :, NOPE:].reshape(KV_RANK, -1)], axis=1)


def _pad_lanes(v, n=LANE):
    return jnp.concatenate([v, jnp.zeros((v.shape[0], n - v.shape[1]), v.dtype)], axis=1)


def _local_loss(toks, small, x, wb, c8, posf, target):
    B, S, D = x.shape
    T = B * S

    def lin(name, a, key, lay=lambda w: w):
        return make_linear(name)(a, lay(wb[key]), lay(toks[key]))

    rows2 = lambda a: a.reshape(T, a.shape[-1])
    rows3 = lambda a: a.reshape(B, S, a.shape[-1])

    sc = make_rowwise("silu_c", _f_silu, 1, 0, 0, ('row',))((c8[None],), (), ())[0][0]
    mod = lin("ada", sc, 'w_ada')[:B] + small['b_ada']
    shift1, scale1, gate1, shift2, scale2, gate2 = [m[:, None, :] for m in jnp.split(mod, 6, axis=-1)]

    modulate = make_rowwise("modulate1", _f_modulate, 1, 2, 1, ('row',))
    h = modulate((x,), (scale1, shift1), (small['g_pre_mix'],))[0]
    proj = rows3(lin("w_in", rows2(h), 'w_in', _lay_w_in))
    offs = [int(v) for v in np.cumsum(IN_PAD_SIZES)[:-1]]
    q_lat, kv_lat, k_rope, z, xbc, dt_raw, gate_a, gate_b = jnp.split(proj, offs, axis=-1)

    inv = ROPE_THETA ** (-jnp.arange(ROPE // 2, dtype=F32) / (ROPE // 2))
    inv_lane = jnp.concatenate([inv, inv, jnp.zeros((LANE - ROPE,), F32)])[None]
    tabs = tuple(_rope_tables(posf, inv_lane))
    qn = make_rowwise("rms_q", _f_rms, 1, 0, 1, ('row',))((q_lat,), (), (small['g_q_lat'],))[0]
    kvn = make_rowwise("rms_kv", _f_rms, 1, 0, 1, ('row',))((kv_lat,), (), (small['g_kv_lat'],))[0]
    qp = rows3(lin("w_uq", rows2(qn), 'w_uq', _lay_w_uq))
    kvp = rows3(lin("w_ukv", rows2(kvn), 'w_ukv', _lay_w_ukv))
    qr = rope_q(qp, tabs)
    kr = build_k(kvp, k_rope, tabs)
    att = attention(qr, kr, kvp)
    attn = rows3(lin("w_o_attn", rows2(att), 'w_o_attn'))

    xa = conv_silu(xbc, wb['conv_w_f32'], small['conv_b'])
    dt_pad, a_pad = make_rowwise("dt_softplus", _f_dt, 1, 0, 2, ('row', 'row'))(
        (dt_raw,), (), (_pad_lanes(small['dt_bias']), _pad_lanes(small['a_log'])))
    ac_pad = chunk_cumsum(a_pad)
    dt32, ac32 = dt_pad[..., :SSM_HEADS], ac_pad[..., :SSM_HEADS]
    dtx = jnp.repeat(dt32, HEAD_P, axis=-1)
    acx = jnp.repeat(ac32, HEAD_P, axis=-1)
    acr = jnp.transpose(ac32, (0, 2, 1))[:, :, None, :]
    acl = acx.reshape(B, S // CHUNK, CHUNK, D_INNER)[:, :, CHUNK - 1:CHUNK, :]
    dsk = jnp.repeat(small['d_skip'], HEAD_P, axis=-1)
    y = ssd(xa, dtx, acx, acr, acl, dsk)
    yg = make_rowwise("gated_norm", _f_gated_norm, 2, 0, 1, ('row',), ncol=SSM_GROUPS)(
        (y, z), (), (small['g_ssm_out'],))[0]
    ssm = rows3(lin("w_o_ssm", rows2(yg), 'w_o_ssm'))

    merged = make_rowwise("merge", _f_merge, 4, 0, 0, ('row',))((attn, ssm, gate_a, gate_b), (), ())[0]
    mix = rows3(lin("w_out", rows2(merged), 'w_out'))
    x1 = make_rowwise("post_mix", _f_post, 2, 1, 1, ('row',))((x, mix), (gate1,), (small['g_post_mix'],))[0]

    h2 = make_rowwise("modulate2", _f_modulate, 1, 2, 1, ('row',))((x1,), (scale2, shift2), (small['g_pre_mlp'],))[0]
    u = rows3(lin("w_ff1", rows2(h2), 'w_ff1'))
    act = make_rowwise("relu2", _f_relu2, 1, 0, 0, ('row',))((u,), (), ())[0]
    ff = rows3(lin("w_ff2", rows2(act), 'w_ff2'))
    lvec = make_rowwise("final_loss", _f_final_loss, 3, 1, 1, ('sum',), nodiff=(2,))(
        (x1, ff, target), (gate2,), (small['g_post_mlp'],))[0]
    return jnp.sum(lvec)


MATRICES = COL_SHARDED + ROW_SHARDED


def _local_step(x, c, positions, target, wb, small):
    B = x.shape[0]
    c8 = jnp.concatenate([c, jnp.zeros((16 - B, c.shape[1]), F32)], axis=0)
    posf = positions.astype(F32)[..., None]
    toks = {k: jnp.zeros(wb[k].shape, F32) for k in MATRICES if k != 'conv_w'}
    conv_w = wb['conv_w_f32']

    def loss_fn(toks, small, conv_w, x):
        wbl = dict(wb)
        wbl['conv_w_f32'] = conv_w
        return _local_loss(toks, small, x, wbl, c8, posf, target)

    loss, (g_tok, g_small, g_conv, g_x) = jax.value_and_grad(loss_fn, argnums=(0, 1, 2, 3))(toks, small, conv_w, x)
    grads = dict(g_tok)
    grads.update(g_small)
    grads['conv_w'] = g_conv
    return loss, g_x, grads


def _shard_shapes(args):
    return [tuple(args[n].shape) for n in WEIGHTS]


def kernel(x, c, positions, w_ada, b_ada, g_pre_mix, g_post_mix, w_in, g_q_lat, g_kv_lat, w_uq, w_ukv, w_o_attn, conv_w, conv_b, dt_bias, a_log, d_skip, g_ssm_out, w_o_ssm, w_out, g_pre_mlp, g_post_mlp, w_ff1, w_ff2, loss_target, m_w_ada, m_b_ada, m_g_pre_mix, m_g_post_mix, m_w_in, m_g_q_lat, m_g_kv_lat, m_w_uq, m_w_ukv, m_w_o_attn, m_conv_w, m_conv_b, m_dt_bias, m_a_log, m_d_skip, m_g_ssm_out, m_w_o_ssm, m_w_out, m_g_pre_mlp, m_g_post_mlp, m_w_ff1, m_w_ff2, v_w_ada, v_b_ada, v_g_pre_mix, v_g_post_mix, v_w_in, v_g_q_lat, v_g_kv_lat, v_w_uq, v_w_ukv, v_w_o_attn, v_conv_w, v_conv_b, v_dt_bias, v_a_log, v_d_skip, v_g_ssm_out, v_w_o_ssm, v_w_out, v_g_pre_mlp, v_g_post_mlp, v_w_ff1, v_w_ff2):
    given = dict(locals())
    w_loc = {n: given[n] for n in WEIGHTS}
    m_loc = [given["m_" + n] for n in WEIGHTS]
    v_loc = [given["v_" + n] for n in WEIGHTS]
    shapes = _shard_shapes(w_loc)

    conv_lo = conv_w - conv_w.astype(BF16).astype(F32)
    names = WEIGHTS + ['conv_w_lo']
    gathered = _all_gather_shards(_pack([w_loc[n] for n in WEIGHTS] + [conv_lo], BF16))
    per_chip = [dict(zip(names, _unpack(gathered[j], shapes + [tuple(conv_w.shape)]))) for j in range(4)]
    wb = {}
    for n in COL_SHARDED + ('conv_w_lo',):
        wb[n] = jnp.concatenate([per_chip[j][n][0] for j in range(4)], axis=1)
    for n in ROW_SHARDED:
        wb[n] = jnp.concatenate([per_chip[j][n][0] for j in range(4)], axis=0)
    wb['conv_w_f32'] = wb.pop('conv_w').astype(F32) + wb.pop('conv_w_lo').astype(F32)
    small = {n: w_loc[n] for n in WEIGHTS if n not in MATRICES}

    loss_part, grad_x, grads = _local_step(x, c, positions, loss_target, wb, small)
    loss = lax.psum(loss_part, ("x", "y", "c"))

    sections = []
    for j in range(4):
        parts = []
        for n, shp in zip(WEIGHTS, shapes):
            g = grads[n]
            if n in COL_SHARDED:
                w = shp[2]
                g = g[:, j * w:(j + 1) * w]
            elif n in ROW_SHARDED:
                r = shp[1]
                g = g[j * r:(j + 1) * r, :]
            parts.append(g)
        sections.append(_pack(parts, F32))
    g_pack = _reduce_scatter(jnp.stack(sections))

    w_pack = _pack([w_loc[n] for n in WEIGHTS], F32)
    m_pack = _pack(m_loc, F32)
    v_pack = _pack(v_loc, F32)
    d_pack, nm_pack, nv_pack = _elementwise_call(_adam_fn, [w_pack, g_pack, m_pack, v_pack], 3, "adamw")
    return (loss, grad_x, *_unpack(g_pack, shapes), *_unpack(d_pack, shapes), *_unpack(nm_pack, shapes),
            *_unpack(nv_pack, shapes))
```

```python
import functools
import math

import numpy as np
import jax
import jax.numpy as jnp
from jax import lax
from jax.experimental import pallas as pl
from jax.experimental.pallas import tpu as pltpu

F32 = jnp.float32
BF16 = jnp.bfloat16
MESH = pl.DeviceIdType.MESH

D_MODEL = 1024
N_HEADS = 8
NOPE = 128
ROPE = 64
V_DIM = 128
Q_RANK = 256
KV_RANK = 256
ROPE_THETA = 10000.0
D_INNER = 2048
SSM_HEADS = 32
SSM_GROUPS = 8
HEAD_P = 64
STATE_N = 128
CONV_K = 4
CHUNK = 128
CONV_CH = D_INNER + 2 * SSM_GROUPS * STATE_N
D_FF = 4096
EPS = 1e-6
IN_SIZES = (Q_RANK, KV_RANK, ROPE, D_INNER, CONV_CH, SSM_HEADS, D_MODEL, D_MODEL)
ADAM_LR, ADAM_B1, ADAM_B2, ADAM_EPS, ADAM_WD, ADAM_STEP = 0.001, 0.9, 0.999, 1e-08, 0.01, 10

VMEM_LIMIT_BYTES = 52 * 1024 * 1024
LANE = 128
QK_PAD = 256

WEIGHTS = ['w_ada', 'b_ada', 'g_pre_mix', 'g_post_mix', 'w_in', 'g_q_lat', 'g_kv_lat', 'w_uq', 'w_ukv',
           'w_o_attn', 'conv_w', 'conv_b', 'dt_bias', 'a_log', 'd_skip', 'g_ssm_out', 'w_o_ssm', 'w_out',
           'g_pre_mlp', 'g_post_mlp', 'w_ff1', 'w_ff2']
COL_SHARDED = ('w_ada', 'w_in', 'w_uq', 'w_ukv', 'conv_w', 'w_ff1')
ROW_SHARDED = ('w_o_attn', 'w_o_ssm', 'w_out', 'w_ff2')


def _cparams(sem):
    return pltpu.CompilerParams(dimension_semantics=sem, vmem_limit_bytes=VMEM_LIMIT_BYTES)


def _tile(n, cap):
    if n <= cap:
        return n
    k = n // LANE
    best = LANE
    for d in range(1, k + 1):
        if k % d == 0 and d * LANE <= cap:
            best = d * LANE
    return best


def _mm(a, w, name, out_dtype=F32):
    M, K = a.shape
    N = w.shape[1]
    tm = min(M, 512)
    tn = _tile(N, 1024)
    tk = _tile(K, 2048)
    nk = K // tk
    assert nk == 1 or out_dtype == F32

    def body(a_ref, w_ref, o_ref):
        part = jnp.dot(a_ref[...].astype(BF16), w_ref[...], preferred_element_type=F32)
        if nk == 1:
            o_ref[...] = part.astype(out_dtype)
        else:
            k = pl.program_id(2)

            @pl.when(k == 0)
            def _():
                o_ref[...] = part

            @pl.when(k > 0)
            def _():
                o_ref[...] += part

    return pl.pallas_call(
        body, grid=(M // tm, N // tn, nk),
        in_specs=[pl.BlockSpec((tm, tk), lambda i, j, k: (i, k)), pl.BlockSpec((tk, tn), lambda i, j, k: (k, j))],
        out_specs=pl.BlockSpec((tm, tn), lambda i, j, k: (i, j)),
        out_shape=jax.ShapeDtypeStruct((M, N), out_dtype), name=name,
        compiler_params=_cparams(("parallel", "parallel", "arbitrary")))(a, w)


def _mm_tn(a, g, name):
    M, K = a.shape
    N = g.shape[1]
    tm = min(M, 512)
    tk = _tile(K, 1024)
    tn = _tile(N, 1024)
    nm = M // tm

    def body(a_ref, g_ref, o_ref):
        part = lax.dot_general(a_ref[...].astype(BF16), g_ref[...].astype(BF16), (((0,), (0,)), ((), ())),
                               preferred_element_type=F32)
        m = pl.program_id(2)

        @pl.when(m == 0)
        def _():
            o_ref[...] = part

        @pl.when(m > 0)
        def _():
            o_ref[...] += part

    return pl.pallas_call(
        body, grid=(K // tk, N // tn, nm),
        in_specs=[pl.BlockSpec((tm, tk), lambda i, j, m: (m, i)), pl.BlockSpec((tm, tn), lambda i, j, m: (m, j))],
        out_specs=pl.BlockSpec((tk, tn), lambda i, j, m: (i, j)),
        out_shape=jax.ShapeDtypeStruct((K, N), F32), name=name,
        compiler_params=_cparams(("parallel", "parallel", "arbitrary")))(a, g)


def make_linear(name, out_dtype=F32):
    @jax.custom_vjp
    def linear(a, w, tok):
        return _mm(a, w, name + "_fwd", out_dtype)

    def fwd(a, w, tok):
        return _mm(a, w, name + "_fwd", out_dtype), (a, w)

    def bwd(res, g):
        a, w = res
        da = _mm(g, w.T, name + "_dx", a.dtype)
        dw = _mm_tn(a, g, name + "_dw")
        return da, jnp.zeros_like(w), dw

    linear.defvjp(fwd, bwd)
    return linear


def make_rowwise(name, f, n_rows, n_seqs, n_pars, out_kinds, ncol=1, nodiff=(), ts_cap=512):
    n_in = n_rows + n_seqs + n_pars
    diff_idx = [i for i in range(n_in) if i not in nodiff]

    def _dims(rows):
        B, S = rows[0].shape[0], rows[0].shape[1]
        ts = min(S, ts_cap)
        return B, S, ts

    def _in_specs(rows, seqs, pars, ts):
        specs = []
        for r in rows:
            specs.append(pl.BlockSpec((1, ts, r.shape[2] // ncol), lambda k, b, s: (b, s, k)))
        for q in seqs:
            specs.append(pl.BlockSpec((1, 1, q.shape[2] // ncol), lambda k, b, s: (b, 0, k)))
        for p in pars:
            specs.append(pl.BlockSpec((1, p.shape[1] // ncol), lambda k, b, s: (0, k)))
        return specs

    def _load(refs):
        vals = [r[0] for r in refs[:n_rows + n_seqs]]
        vals += [r[...] for r in refs[n_rows + n_seqs:n_in]]
        return vals

    def _out_struct(rows, seqs, pars, ts):
        blocks = [jax.ShapeDtypeStruct((ts, r.shape[2] // ncol), r.dtype) for r in rows]
        blocks += [jax.ShapeDtypeStruct((1, q.shape[2] // ncol), q.dtype) for q in seqs]
        blocks += [jax.ShapeDtypeStruct((1, p.shape[1] // ncol), p.dtype) for p in pars]
        return jax.eval_shape(f, *blocks)

    def _fwd_call(rows, seqs, pars):
        B, S, ts = _dims(rows)
        outs = _out_struct(rows, seqs, pars, ts)
        n_out = len(outs)

        def body(*refs):
            res = f(*_load(refs))
            first = (pl.program_id(1) == 0) & (pl.program_id(2) == 0)
            for o_ref, val, kind in zip(refs[n_in:], res, out_kinds):
                if kind == 'row':
                    o_ref[0] = val
                else:
                    tot = jnp.sum(val, axis=0, keepdims=True)

                    @pl.when(first)
                    def _(o_ref=o_ref, tot=tot):
                        o_ref[...] = tot

                    @pl.when(jnp.logical_not(first))
                    def _(o_ref=o_ref, tot=tot):
                        o_ref[...] += tot

        out_shape, out_specs = [], []
        for o, kind in zip(outs, out_kinds):
            d = o.shape[1]
            if kind == 'row':
                out_shape.append(jax.ShapeDtypeStruct((B, S, ncol * d), o.dtype))
                out_specs.append(pl.BlockSpec((1, ts, d), lambda k, b, s: (b, s, k)))
            else:
                out_shape.append(jax.ShapeDtypeStruct((1, ncol * d), o.dtype))
                out_specs.append(pl.BlockSpec((1, d), lambda k, b, s: (0, k)))
        res = pl.pallas_call(
            body, grid=(ncol, B, S // ts), in_specs=_in_specs(rows, seqs, pars, ts), out_specs=out_specs,
            out_shape=out_shape, name=name + "_fwd",
            compiler_params=_cparams(("arbitrary", "arbitrary", "arbitrary")))(*rows, *seqs, *pars)
        return tuple(res)

    def _bwd_call(rows, seqs, pars, cots):
        B, S, ts = _dims(rows)
        outs = _out_struct(rows, seqs, pars, ts)
        n_out = len(outs)
        all_in = list(rows) + list(seqs) + list(pars)

        def body(*refs):
            vals = _load(refs)
            cts = []
            for c_ref, o, kind in zip(refs[n_in:n_in + n_out], outs, out_kinds):
                if kind == 'row':
                    cts.append(c_ref[0])
                else:
                    cts.append(jnp.broadcast_to(c_ref[...], o.shape))

            def g(*dv):
                full = list(vals)
                for i, v in zip(diff_idx, dv):
                    full[i] = v
                return tuple(f(*full))

            _, vjp = jax.vjp(g, *[vals[i] for i in diff_idx])
            grads = vjp(tuple(cts))
            b, s = pl.program_id(1), pl.program_id(2)
            for o_ref, i, gr in zip(refs[n_in + n_out:], diff_idx, grads):
                if i < n_rows:
                    o_ref[0] = gr
                else:
                    first = (s == 0) if i < n_rows + n_seqs else ((b == 0) & (s == 0))
                    target = (lambda r: r.at[0]) if i < n_rows + n_seqs else (lambda r: r)

                    @pl.when(first)
                    def _(o_ref=o_ref, gr=gr, target=target):
                        target(o_ref)[...] = gr

                    @pl.when(jnp.logical_not(first))
                    def _(o_ref=o_ref, gr=gr, target=target):
                        target(o_ref)[...] += gr

        cot_specs = []
        for o, kind in zip(outs, out_kinds):
            d = o.shape[1]
            if kind == 'row':
                cot_specs.append(pl.BlockSpec((1, ts, d), lambda k, b, s: (b, s, k)))
            else:
                cot_specs.append(pl.BlockSpec((1, d), lambda k, b, s: (0, k)))
        out_shape, out_specs = [], []
        for i in diff_idx:
            a = all_in[i]
            out_shape.append(jax.ShapeDtypeStruct(a.shape, a.dtype))
            if i < n_rows:
                out_specs.append(pl.BlockSpec((1, ts, a.shape[2] // ncol), lambda k, b, s: (b, s, k)))
            elif i < n_rows + n_seqs:
                out_specs.append(pl.BlockSpec((1, 1, a.shape[2] // ncol), lambda k, b, s: (b, 0, k)))
            else:
                out_specs.append(pl.BlockSpec((1, a.shape[1] // ncol), lambda k, b, s: (0, k)))
        res = pl.pallas_call(
            body, grid=(ncol, B, S // ts), in_specs=_in_specs(rows, seqs, pars, ts) + cot_specs,
            out_specs=out_specs, out_shape=out_shape, name=name + "_bwd",
            compiler_params=_cparams(("arbitrary", "arbitrary", "arbitrary")))(*all_in, *cots)
        grads = [None] * n_in
        for i, r in zip(diff_idx, res):
            grads[i] = r
        for i in nodiff:
            grads[i] = jnp.zeros_like(all_in[i])
        return tuple(grads[:n_rows]), tuple(grads[n_rows:n_rows + n_seqs]), tuple(grads[n_rows + n_seqs:])

    @jax.custom_vjp
    def op(rows, seqs, pars):
        return _fwd_call(rows, seqs, pars)

    def fwd(rows, seqs, pars):
        return _fwd_call(rows, seqs, pars), (rows, seqs, pars)

    def bwd(res, cots):
        rows, seqs, pars = res
        return _bwd_call(rows, seqs, pars, cots)

    op.defvjp(fwd, bwd)
    return op


def _rms(x, g):
    return x * lax.rsqrt(jnp.mean(x * x, axis=-1, keepdims=True) + EPS) * g


def _silu(x):
    return x * lax.logistic(x)


def _f_silu(c):
    return (_silu(c),)


def _f_modulate(x, scale, shift, g):
    return (_rms(x, g) * (1.0 + scale) + shift,)


def _f_rms(x, g):
    return (_rms(x, g),)


def _f_dt(dt_raw, dt_bias, a_log):
    z = dt_raw + dt_bias
    dt = jnp.maximum(z, 0.0) + jnp.log1p(jnp.exp(-jnp.abs(z)))
    return dt, dt * (-jnp.exp(a_log))


def _f_gated_norm(y, z, g):
    return (_rms(y * _silu(z), g),)


def _f_merge(attn, ssm, ga, gb):
    return (lax.logistic(ga) * attn + lax.logistic(gb) * ssm,)


def _f_post(x, m, gate, g):
    return (x + gate * _rms(m, g),)


def _f_relu2(u):
    r = jnp.maximum(u, 0.0)
    return (r * r,)


def _f_final_loss(x, ff, target, gate, g):
    e = x + gate * _rms(ff, g) - target
    return (e * e * (0.5 / D_MODEL),)


def _rope_tables(posf, inv_lane):
    B, S, _ = posf.shape
    ts = min(S, 512)

    def body(p_ref, inv_ref, c_ref, a_ref, b_ref):
        ang = p_ref[0] * inv_ref[...]
        cs, sn = jnp.cos(ang), jnp.sin(ang)
        lane = lax.broadcasted_iota(jnp.int32, ang.shape, 1)
        c_ref[0] = jnp.where(lane < ROPE, cs, 0.0)
        a_ref[0] = jnp.where(lane < ROPE // 2, -sn, 0.0)
        b_ref[0] = jnp.where((lane >= ROPE // 2) & (lane < ROPE), sn, 0.0)

    spec = pl.BlockSpec((1, ts, LANE), lambda b, s: (b, s, 0))
    sds = jax.ShapeDtypeStruct((B, S, LANE), F32)
    return pl.pallas_call(
        body, grid=(B, S // ts),
        in_specs=[pl.BlockSpec((1, ts, 1), lambda b, s: (b, s, 0)), pl.BlockSpec((1, LANE), lambda b, s: (0, 0))],
        out_specs=[spec, spec, spec], out_shape=[sds, sds, sds], name="rope_tables",
        compiler_params=_cparams(("parallel", "parallel")))(posf, inv_lane)


def _rot(u, c, a, bm):
    return u * c + pltpu.roll(u, 96, 1) * a + pltpu.roll(u, 32, 1) * bm


def _rot_t(g, c, a, bm):
    return g * c + pltpu.roll(g * a, 32, 1) + pltpu.roll(g * bm, 96, 1)


def _rope_q_call(q, tabs, transpose, name):
    B, S, W = q.shape
    ts = min(S, 512)
    fn = _rot_t if transpose else _rot
    out_dtype = F32 if transpose else BF16

    def body(q_ref, c_ref, a_ref, b_ref, o_ref):
        u = q_ref[0].astype(F32) * ATT_SCALE
        r = fn(u[:, NOPE:], c_ref[0], a_ref[0], b_ref[0])
        o_ref[0] = jnp.concatenate([u[:, :NOPE], r], axis=1).astype(out_dtype)

    tspec = pl.BlockSpec((1, ts, LANE), lambda b, s, h: (b, s, 0))
    qspec = pl.BlockSpec((1, ts, QK_PAD), lambda b, s, h: (b, s, h))
    return pl.pallas_call(
        body, grid=(B, S // ts, W // QK_PAD), in_specs=[qspec, tspec, tspec, tspec], out_specs=qspec,
        out_shape=jax.ShapeDtypeStruct(q.shape, out_dtype), name=name,
        compiler_params=_cparams(("parallel", "parallel", "parallel")))(q, *tabs)


@jax.custom_vjp
def rope_q(q, tabs):
    return _rope_q_call(q, tabs, False, "rope_q_fwd")


def _rope_q_fwd(q, tabs):
    return _rope_q_call(q, tabs, False, "rope_q_fwd"), tabs


def _rope_q_bwd(tabs, g):
    return _rope_q_call(g, tabs, True, "rope_q_bwd"), tuple(jnp.zeros_like(t) for t in tabs)


rope_q.defvjp(_rope_q_fwd, _rope_q_bwd)


def _build_k_fwd_call(kv, kr, tabs):
    B, S, _ = kv.shape
    ts = min(S, 512)

    def body(kv_ref, kr_ref, c_ref, a_ref, b_ref, o_ref):
        r = _rot(kr_ref[0], c_ref[0], a_ref[0], b_ref[0])
        o_ref[0] = jnp.concatenate([kv_ref[0], r.astype(BF16)], axis=1)

    tspec = pl.BlockSpec((1, ts, LANE), lambda b, s, h: (b, s, 0))
    return pl.pallas_call(
        body, grid=(B, S // ts, N_HEADS),
        in_specs=[pl.BlockSpec((1, ts, LANE), lambda b, s, h: (b, s, h)), tspec, tspec, tspec, tspec],
        out_specs=pl.BlockSpec((1, ts, QK_PAD), lambda b, s, h: (b, s, h)),
        out_shape=jax.ShapeDtypeStruct((B, S, N_HEADS * QK_PAD), BF16), name="build_k_fwd",
        compiler_params=_cparams(("parallel", "parallel", "arbitrary")))(kv, kr, *tabs)


def _build_k_bwd_call(g, tabs):
    B, S, _ = g.shape
    ts = min(S, 512)

    def body(g_ref, c_ref, a_ref, b_ref, dk_ref, dr_ref):
        gg = g_ref[0]
        dk_ref[0] = gg[:, :NOPE]
        r = _rot_t(gg[:, NOPE:].astype(F32), c_ref[0], a_ref[0], b_ref[0])
        h = pl.program_id(2)

        @pl.when(h == 0)
        def _():
            dr_ref[0] = r

        @pl.when(h > 0)
        def _():
            dr_ref[0] += r

    tspec = pl.BlockSpec((1, ts, LANE), lambda b, s, h: (b, s, 0))
    return pl.pallas_call(
        body, grid=(B, S // ts, N_HEADS),
        in_specs=[pl.BlockSpec((1, ts, QK_PAD), lambda b, s, h: (b, s, h)), tspec, tspec, tspec],
        out_specs=[pl.BlockSpec((1, ts, LANE), lambda b, s, h: (b, s, h)), tspec],
        out_shape=[jax.ShapeDtypeStruct((B, S, N_HEADS * NOPE), BF16), jax.ShapeDtypeStruct((B, S, LANE), F32)],
        name="build_k_bwd", compiler_params=_cparams(("parallel", "parallel", "arbitrary")))(g, *tabs)


@jax.custom_vjp
def build_k(kv, kr, tabs):
    return _build_k_fwd_call(kv, kr, tabs)


def _build_k_fwd(kv, kr, tabs):
    return _build_k_fwd_call(kv, kr, tabs), (tabs, kv.shape)


def _build_k_bwd(res, g):
    tabs, kv_shape = res
    dk, dr = _build_k_bwd_call(g, tabs)
    dkv = jnp.concatenate([dk, jnp.zeros((kv_shape[0], kv_shape[1], kv_shape[2] - dk.shape[2]), BF16)], axis=-1)
    return dkv, dr, tuple(jnp.zeros_like(t) for t in tabs)


build_k.defvjp(_build_k_fwd, _build_k_bwd)


ATT_SCALE = (NOPE + ROPE) ** -0.5
NEG = -1e30


def _att_tiles(S):
    t = min(S, 512)
    return t, S // t


def _scores(q, k, diagonal):
    s = lax.dot_general(q, k, (((1,), (1,)), ((), ())), preferred_element_type=F32)
    if diagonal:
        row = lax.broadcasted_iota(jnp.int32, s.shape, 0)
        col = lax.broadcasted_iota(jnp.int32, s.shape, 1)
        s = jnp.where(col <= row, s, NEG)
    return s


def _attn_fwd_call(q, k, vsrc, v_blk0):
    B, S, _ = q.shape
    t, n = _att_tiles(S)

    def body(q_ref, k_ref, v_ref, o_ref, lse_ref, m_sc, l_sc, acc_sc):
        i, j = pl.program_id(2), pl.program_id(3)

        @pl.when(j == 0)
        def _():
            m_sc[...] = jnp.full(m_sc.shape, NEG, F32)
            l_sc[...] = jnp.zeros(l_sc.shape, F32)
            acc_sc[...] = jnp.zeros(acc_sc.shape, F32)

        def step(diagonal):
            s = _scores(q_ref[0], k_ref[0], diagonal)
            m_prev = m_sc[...]
            m_new = jnp.maximum(m_prev, jnp.max(s, axis=1, keepdims=True))
            alpha = jnp.exp(m_prev - m_new)
            p = jnp.exp(s - jnp.tile(m_new, (1, t // LANE)))
            l_sc[...] = alpha * l_sc[...] + jnp.sum(p, axis=1, keepdims=True)
            acc_sc[...] = alpha * acc_sc[...] + jnp.dot(p.astype(BF16), v_ref[0], preferred_element_type=F32)
            m_sc[...] = m_new

        @pl.when(j < i)
        def _():
            step(False)

        @pl.when(j == i)
        def _():
            step(True)
            o_ref[0] = acc_sc[...] / l_sc[...]
            lse_ref[0] = m_sc[...] + jnp.log(l_sc[...])

    return pl.pallas_call(
        body, grid=(B, N_HEADS, n, n),
        in_specs=[pl.BlockSpec((1, t, QK_PAD), lambda b, h, i, j: (b, i, h)),
                  pl.BlockSpec((1, t, QK_PAD), lambda b, h, i, j: (b, jnp.minimum(j, i), h)),
                  pl.BlockSpec((1, t, V_DIM), lambda b, h, i, j: (b, jnp.minimum(j, i), v_blk0 + h))],
        out_specs=[pl.BlockSpec((1, t, V_DIM), lambda b, h, i, j: (b, i, h)),
                   pl.BlockSpec((1, t, LANE), lambda b, h, i, j: (b, i, h))],
        out_shape=[jax.ShapeDtypeStruct((B, S, N_HEADS * V_DIM), F32),
                   jax.ShapeDtypeStruct((B, S, N_HEADS * LANE), F32)],
        scratch_shapes=[pltpu.VMEM((t, LANE), F32), pltpu.VMEM((t, LANE), F32), pltpu.VMEM((t, V_DIM), F32)],
        name="attn_fwd", compiler_params=_cparams(("parallel", "parallel", "parallel", "arbitrary")))(q, k, vsrc)


def _attn_p_ds(q, k, v, o, do, lse, diagonal, t):
    s = _scores(q, k, diagonal)
    p = jnp.exp(s - jnp.tile(lse, (1, t // LANE)))
    dp = lax.dot_general(do.astype(BF16), v, (((1,), (1,)), ((), ())), preferred_element_type=F32)
    delta = jnp.sum(do * o, axis=1, keepdims=True)
    ds = p * (dp - delta)
    return p, ds


def _attn_dkv_call(q, k, vsrc, v_blk0, o, do, lse):
    B, S, _ = q.shape
    t, n = _att_tiles(S)

    def body(q_ref, k_ref, v_ref, o_ref, do_ref, lse_ref, dk_ref, dv_ref, dk_sc, dv_sc):
        j, i = pl.program_id(2), pl.program_id(3)

        @pl.when(i == 0)
        def _():
            dk_sc[...] = jnp.zeros(dk_sc.shape, F32)
            dv_sc[...] = jnp.zeros(dv_sc.shape, F32)

        def step(diagonal):
            qb = q_ref[0]
            dob = do_ref[0]
            p, ds = _attn_p_ds(qb, k_ref[0], v_ref[0], o_ref[0], dob, lse_ref[0], diagonal, t)
            dv_sc[...] += lax.dot_general(p.astype(BF16), dob.astype(BF16), (((0,), (0,)), ((), ())),
                                          preferred_element_type=F32)
            dk_sc[...] += lax.dot_general(ds.astype(BF16), qb, (((0,), (0,)), ((), ())),
                                          preferred_element_type=F32)

        @pl.when(i > j)
        def _():
            step(False)

        @pl.when(i == j)
        def _():
            step(True)

        @pl.when(i == n - 1)
        def _():
            dk_ref[0] = dk_sc[...].astype(BF16)
            dv_ref[0] = dv_sc[...].astype(BF16)

    qi = lambda b, h, j, i: (b, jnp.maximum(i, j), h)
    return pl.pallas_call(
        body, grid=(B, N_HEADS, n, n),
        in_specs=[pl.BlockSpec((1, t, QK_PAD), qi),
                  pl.BlockSpec((1, t, QK_PAD), lambda b, h, j, i: (b, j, h)),
                  pl.BlockSpec((1, t, V_DIM), lambda b, h, j, i: (b, j, v_blk0 + h)),
                  pl.BlockSpec((1, t, V_DIM), qi), pl.BlockSpec((1, t, V_DIM), qi), pl.BlockSpec((1, t, LANE), qi)],
        out_specs=[pl.BlockSpec((1, t, QK_PAD), lambda b, h, j, i: (b, j, h)),
                   pl.BlockSpec((1, t, V_DIM), lambda b, h, j, i: (b, j, h))],
        out_shape=[jax.ShapeDtypeStruct((B, S, N_HEADS * QK_PAD), BF16),
                   jax.ShapeDtypeStruct((B, S, N_HEADS * V_DIM), BF16)],
        scratch_shapes=[pltpu.VMEM((t, QK_PAD), F32), pltpu.VMEM((t, V_DIM), F32)],
        name="attn_dkv", compiler_params=_cparams(("parallel", "parallel", "parallel", "arbitrary")))(
            q, k, vsrc, o, do, lse)


def _attn_dq_call(q, k, vsrc, v_blk0, o, do, lse):
    B, S, _ = q.shape
    t, n = _att_tiles(S)

    def body(q_ref, k_ref, v_ref, o_ref, do_ref, lse_ref, dq_ref, dq_sc):
        i, j = pl.program_id(2), pl.program_id(3)

        @pl.when(j == 0)
        def _():
            dq_sc[...] = jnp.zeros(dq_sc.shape, F32)

        def step(diagonal):
            kb = k_ref[0]
            _, ds = _attn_p_ds(q_ref[0], kb, v_ref[0], o_ref[0], do_ref[0], lse_ref[0], diagonal, t)
            dq_sc[...] += jnp.dot(ds.astype(BF16), kb, preferred_element_type=F32)

        @pl.when(j < i)
        def _():
            step(False)

        @pl.when(j == i)
        def _():
            step(True)

        @pl.when(j == n - 1)
        def _():
            dq_ref[0] = dq_sc[...].astype(BF16)

    qi = lambda b, h, i, j: (b, i, h)
    kj = lambda b, h, i, j: (b, jnp.minimum(j, i), h)
    return pl.pallas_call(
        body, grid=(B, N_HEADS, n, n),
        in_specs=[pl.BlockSpec((1, t, QK_PAD), qi), pl.BlockSpec((1, t, QK_PAD), kj),
                  pl.BlockSpec((1, t, V_DIM), lambda b, h, i, j: (b, jnp.minimum(j, i), v_blk0 + h)),
                  pl.BlockSpec((1, t, V_DIM), qi), pl.BlockSpec((1, t, V_DIM), qi), pl.BlockSpec((1, t, LANE), qi)],
        out_specs=pl.BlockSpec((1, t, QK_PAD), qi),
        out_shape=jax.ShapeDtypeStruct((B, S, N_HEADS * QK_PAD), BF16),
        scratch_shapes=[pltpu.VMEM((t, QK_PAD), F32)],
        name="attn_dq", compiler_params=_cparams(("parallel", "parallel", "parallel", "arbitrary")))(
            q, k, vsrc, o, do, lse)


@jax.custom_vjp
def attention(q, k, kv):
    return _attn_fwd_call(q, k, kv, N_HEADS)[0]


def _attention_fwd(q, k, kv):
    o, lse = _attn_fwd_call(q, k, kv, N_HEADS)
    return o, (q, k, kv, o, lse)


def _attention_bwd(res, do):
    q, k, kv, o, lse = res
    dk, dv = _attn_dkv_call(q, k, kv, N_HEADS, o, do, lse)
    dq = _attn_dq_call(q, k, kv, N_HEADS, o, do, lse)
    dkv = jnp.concatenate([jnp.zeros_like(dv), dv], axis=-1)
    return dq, dk, dkv


attention.defvjp(_attention_fwd, _attention_bwd)


def _shift_down(v, sh, rows):
    return jnp.where(rows >= sh, pltpu.roll(v, sh, 0), 0.0)


def _shift_up(v, sh, rows, S):
    return jnp.where(rows < S - sh, pltpu.roll(v, S - sh, 0), 0.0)


def _conv_pre(u, w_ref, b_ref, rows):
    acc = b_ref[...] + w_ref[pl.ds(CONV_K - 1, 1), :] * u
    for k in range(CONV_K - 1):
        acc = acc + w_ref[pl.ds(k, 1), :] * _shift_down(u, CONV_K - 1 - k, rows)
    return acc


def _conv_fwd_call(u, w, b):
    B, S, C = u.shape

    def body(u_ref, w_ref, b_ref, o_ref):
        uu = u_ref[0]
        rows = lax.broadcasted_iota(jnp.int32, uu.shape, 0)
        o_ref[0] = _silu(_conv_pre(uu, w_ref, b_ref, rows))

    spec = pl.BlockSpec((1, S, LANE), lambda c, bb: (bb, 0, c))
    return pl.pallas_call(
        body, grid=(C // LANE, B),
        in_specs=[spec, pl.BlockSpec((CONV_K, LANE), lambda c, bb: (0, c)), pl.BlockSpec((1, LANE), lambda c, bb: (0, c))],
        out_specs=spec, out_shape=jax.ShapeDtypeStruct(u.shape, F32), name="conv_fwd",
        compiler_params=_cparams(("parallel", "arbitrary")))(u, w, b)


def _conv_bwd_call(u, w, b, g):
    B, S, C = u.shape

    def body(u_ref, w_ref, b_ref, g_ref, du_ref, dw_ref, db_ref):
        uu = u_ref[0]
        rows = lax.broadcasted_iota(jnp.int32, uu.shape, 0)
        pre = _conv_pre(uu, w_ref, b_ref, rows)
        sg = lax.logistic(pre)
        dpre = g_ref[0] * sg * (1.0 + pre * (1.0 - sg))
        du = w_ref[pl.ds(CONV_K - 1, 1), :] * dpre
        dws = [None] * CONV_K
        dws[CONV_K - 1] = jnp.sum(dpre * uu, axis=0, keepdims=True)
        for k in range(CONV_K - 1):
            sh = CONV_K - 1 - k
            du = du + w_ref[pl.ds(k, 1), :] * _shift_up(dpre, sh, rows, S)
            dws[k] = jnp.sum(dpre * _shift_down(uu, sh, rows), axis=0, keepdims=True)
        du_ref[0] = du
        dbv = jnp.sum(dpre, axis=0, keepdims=True)
        first = pl.program_id(1) == 0

        @pl.when(first)
        def _():
            for k in range(CONV_K):
                dw_ref[pl.ds(k, 1), :] = dws[k]
            db_ref[...] = dbv

        @pl.when(jnp.logical_not(first))
        def _():
            for k in range(CONV_K):
                dw_ref[pl.ds(k, 1), :] += dws[k]
            db_ref[...] += dbv

    spec = pl.BlockSpec((1, S, LANE), lambda c, bb: (bb, 0, c))
    wspec = pl.BlockSpec((CONV_K, LANE), lambda c, bb: (0, c))
    bspec = pl.BlockSpec((1, LANE), lambda c, bb: (0, c))
    return pl.pallas_call(
        body, grid=(C // LANE, B), in_specs=[spec, wspec, bspec, spec], out_specs=[spec, wspec, bspec],
        out_shape=[jax.ShapeDtypeStruct(u.shape, F32), jax.ShapeDtypeStruct(w.shape, F32),
                   jax.ShapeDtypeStruct(b.shape, F32)],
        name="conv_bwd", compiler_params=_cparams(("parallel", "arbitrary")))(u, w, b, g)


@jax.custom_vjp
def conv_silu(u, w, b):
    return _conv_fwd_call(u, w, b)


def _conv_silu_fwd(u, w, b):
    return _conv_fwd_call(u, w, b), (u, w, b)


def _conv_silu_bwd(res, g):
    return tuple(_conv_bwd_call(*res, g))


conv_silu.defvjp(_conv_silu_fwd, _conv_silu_bwd)


def _chunk_cumsum_call(a, reverse, name):
    B, S, W = a.shape

    def body(a_ref, o_ref):
        r = lax.broadcasted_iota(jnp.int32, (CHUNK, CHUNK), 0)
        c = lax.broadcasted_iota(jnp.int32, (CHUNK, CHUNK), 1)
        tri = jnp.where((c >= r) if reverse else (c <= r), 1.0, 0.0).astype(F32)
        o_ref[0] = jnp.dot(tri, a_ref[0], preferred_element_type=F32, precision=lax.Precision.HIGHEST)

    spec = pl.BlockSpec((1, CHUNK, W), lambda b, c: (b, c, 0))
    return pl.pallas_call(body, grid=(B, S // CHUNK), in_specs=[spec], out_specs=spec,
                          out_shape=jax.ShapeDtypeStruct(a.shape, F32), name=name,
                          compiler_params=_cparams(("parallel", "parallel")))(a)


@jax.custom_vjp
def chunk_cumsum(a):
    return _chunk_cumsum_call(a, False, "chunk_cumsum_fwd")


chunk_cumsum.defvjp(lambda a: (_chunk_cumsum_call(a, False, "chunk_cumsum_fwd"), None),
                    lambda _, g: (_chunk_cumsum_call(g, True, "chunk_cumsum_bwd"),))


GROUP_W = 4 * HEAD_P
HPG = SSM_HEADS // SSM_GROUPS


def _ssd_masks():
    lane = lax.broadcasted_iota(jnp.int32, (1, GROUP_W), 1)
    return [((lane >= HEAD_P * j) & (lane < HEAD_P * (j + 1))).astype(F32) for j in range(HPG)]


def _ssd_decays(acx, acr_ref, masks):
    r = lax.broadcasted_iota(jnp.int32, (CHUNK, CHUNK), 0)
    c = lax.broadcasted_iota(jnp.int32, (CHUNK, CHUNK), 1)
    out = []
    for j in range(HPG):
        col = jnp.min(acx * masks[j], axis=1, keepdims=True)
        seg = col - acr_ref[0, j]
        out.append(jnp.exp(jnp.where(c <= r, seg, NEG)))
    return out


def _dot(a, b, dims):
    return lax.dot_general(a.astype(BF16), b.astype(BF16), (dims, ((), ())), preferred_element_type=F32)


NN = ((1,), (0,))
NT = ((1,), (1,))
TN = ((0,), (0,))


def _ssd_in_specs(nc, rev):
    cc = (lambda c: nc - 1 - c) if rev else (lambda c: c)
    specs = [pl.BlockSpec((1, CHUNK, GROUP_W), lambda b, g, c: (b, cc(c), g)),
             pl.BlockSpec((1, CHUNK, STATE_N), lambda b, g, c: (b, cc(c), D_INNER // STATE_N + g)),
             pl.BlockSpec((1, CHUNK, STATE_N), lambda b, g, c: (b, cc(c), D_INNER // STATE_N + SSM_GROUPS + g)),
             pl.BlockSpec((1, CHUNK, GROUP_W), lambda b, g, c: (b, cc(c), g)),
             pl.BlockSpec((1, CHUNK, GROUP_W), lambda b, g, c: (b, cc(c), g))]
    specs.append(pl.BlockSpec((1, HPG, 1, CHUNK), lambda b, g, c: (b, g, 0, cc(c))))
    specs.append(pl.BlockSpec((1, 1, 1, GROUP_W), lambda b, g, c: (b, cc(c), 0, g)))
    specs.append(pl.BlockSpec((1, GROUP_W), lambda b, g, c: (0, g)))
    return specs


def _ssd_fwd_call(xbc, dtx, acx, acr, acl, dsk):
    B, S, _ = xbc.shape
    nc = S // CHUNK

    def body(x_ref, b_ref, c_ref, dt_ref, ac_ref, ar_ref, al_ref, ds_ref, y_ref, hp_ref, h_sc):
        @pl.when(pl.program_id(2) == 0)
        def _():
            h_sc[...] = jnp.zeros(h_sc.shape, F32)

        x, bm, cm, dt, ac = x_ref[0], b_ref[0], c_ref[0], dt_ref[0], ac_ref[0]
        last = al_ref[0, 0]
        masks = _ssd_masks()
        decays = _ssd_decays(ac, ar_ref, masks)
        xd = x * dt
        cb = _dot(cm, bm, NT)
        hprev = h_sc[...]
        hp_ref[0, 0, 0] = hprev
        y = _dot(cm, hprev, NN) * jnp.exp(ac) + ds_ref[...] * x
        for j in range(HPG):
            y = y + _dot(cb * decays[j], xd * masks[j], NN)
        y_ref[0] = y
        h_sc[...] = hprev * jnp.exp(last) + _dot(bm, xd * jnp.exp(last - ac), TN)

    return pl.pallas_call(
        body, grid=(B, SSM_GROUPS, nc), in_specs=_ssd_in_specs(nc, False),
        out_specs=[pl.BlockSpec((1, CHUNK, GROUP_W), lambda b, g, c: (b, c, g)),
                   pl.BlockSpec((1, 1, 1, STATE_N, GROUP_W), lambda b, g, c: (b, g, c, 0, 0))],
        out_shape=[jax.ShapeDtypeStruct((B, S, D_INNER), F32),
                   jax.ShapeDtypeStruct((B, SSM_GROUPS, nc, STATE_N, GROUP_W), F32)],
        scratch_shapes=[pltpu.VMEM((STATE_N, GROUP_W), F32)], name="ssd_fwd",
        compiler_params=_cparams(("parallel", "parallel", "arbitrary")))(
            xbc, xbc, xbc, dtx, acx, acr, acl, dsk)


def _ssd_bwd_call(xbc, dtx, acx, acr, acl, dsk, hps, dy):
    B, S, _ = xbc.shape
    nc = S // CHUNK

    def body(x_ref, b_ref, c_ref, dt_ref, ac_ref, ar_ref, al_ref, ds_ref, hp_ref, dy_ref,
             dx_ref, db_ref, dc_ref, ddt_ref, dac_ref, dar_ref, dal_ref, dds_ref, dh_sc):
        first = pl.program_id(2) == 0

        @pl.when(first)
        def _():
            dh_sc[...] = jnp.zeros(dh_sc.shape, F32)

        x, bm, cm, dt, ac = x_ref[0], b_ref[0], c_ref[0], dt_ref[0], ac_ref[0]
        last = al_ref[0, 0]
        g = dy_ref[0]
        hprev = hp_ref[0, 0, 0]
        dh = dh_sc[...]
        masks = _ssd_masks()
        decays = _ssd_decays(ac, ar_ref, masks)
        xd = x * dt
        cb = _dot(cm, bm, NT)
        e_c = jnp.exp(ac)
        e_end = jnp.exp(last - ac)
        e_last = jnp.exp(last)
        z = _dot(cm, hprev, NN)
        dz = g * e_c
        dac = g * z * e_c
        dc = _dot(dz, hprev, NT)
        dhprev = _dot(cm, dz, TN) + dh * e_last
        dcb = jnp.zeros((CHUNK, CHUNK), F32)
        dxd = jnp.zeros(xd.shape, F32)
        for j in range(HPG):
            gj = cb * decays[j]
            dgj = _dot(g * masks[j], xd, NT)
            dxd = dxd + _dot(gj, g, TN) * masks[j]
            dcb = dcb + dgj * decays[j]
            dseg = dgj * gj
            dac = dac + jnp.sum(dseg, axis=1, keepdims=True) * masks[j] * (1.0 / HEAD_P)
            dar_ref[0, j] = -jnp.sum(dseg, axis=0, keepdims=True)
        dc = dc + _dot(dcb, bm, NN)
        db = _dot(dcb, cm, TN)
        sx = xd * e_end
        db = db + _dot(sx, dh, NT)
        dsx = _dot(bm, dh, NN)
        dxd = dxd + dsx * e_end
        de = dsx * sx
        dac = dac - de
        dlast = jnp.sum(de, axis=0, keepdims=True) + jnp.sum(dh * hprev, axis=0, keepdims=True) * e_last
        dsk = ds_ref[...]
        dx_ref[0] = dxd * dt + dsk * g
        ddt_ref[0] = dxd * x
        dac_ref[0] = dac
        db_ref[0] = db
        dc_ref[0] = dc
        dal_ref[0, 0] = dlast
        dds = jnp.sum(g * x, axis=0, keepdims=True)
        first_all = first & (pl.program_id(1) == 0)

        @pl.when(first_all)
        def _():
            dds_ref[...] = dds

        @pl.when(jnp.logical_not(first_all))
        def _():
            dds_ref[...] += dds

        dh_sc[...] = dhprev

    rc = lambda c: nc - 1 - c
    gw = pl.BlockSpec((1, CHUNK, GROUP_W), lambda g, b, c: (b, rc(c), g))
    sn = pl.BlockSpec((1, CHUNK, STATE_N), lambda g, b, c: (b, rc(c), g))
    in_specs = [pl.BlockSpec(s.block_shape, (lambda g, b, c, f=s.index_map: f(b, g, c))) for s in _ssd_in_specs(nc, True)]
    in_specs.append(pl.BlockSpec((1, 1, 1, STATE_N, GROUP_W), lambda g, b, c: (b, g, rc(c), 0, 0)))
    in_specs.append(gw)
    out_specs = [gw, sn, sn, gw, gw]
    out_shape = [jax.ShapeDtypeStruct((B, S, D_INNER), F32),
                 jax.ShapeDtypeStruct((B, S, SSM_GROUPS * STATE_N), F32),
                 jax.ShapeDtypeStruct((B, S, SSM_GROUPS * STATE_N), F32),
                 jax.ShapeDtypeStruct((B, S, D_INNER), F32), jax.ShapeDtypeStruct((B, S, D_INNER), F32)]
    out_specs.append(pl.BlockSpec((1, HPG, 1, CHUNK), lambda g, b, c: (b, g, 0, rc(c))))
    out_shape.append(jax.ShapeDtypeStruct(acr.shape, F32))
    out_specs.append(pl.BlockSpec((1, 1, 1, GROUP_W), lambda g, b, c: (b, rc(c), 0, g)))
    out_shape.append(jax.ShapeDtypeStruct(acl.shape, F32))
    out_specs.append(pl.BlockSpec((1, GROUP_W), lambda g, b, c: (0, g)))
    out_shape.append(jax.ShapeDtypeStruct(dsk.shape, F32))
    return pl.pallas_call(
        body, grid=(SSM_GROUPS, B, nc), in_specs=in_specs, out_specs=out_specs, out_shape=out_shape,
        scratch_shapes=[pltpu.VMEM((STATE_N, GROUP_W), F32)], name="ssd_bwd",
        compiler_params=_cparams(("arbitrary", "arbitrary", "arbitrary")))(
            xbc, xbc, xbc, dtx, acx, acr, acl, dsk, hps, dy)


@jax.custom_vjp
def ssd(xbc, dtx, acx, acr, acl, dsk):
    return _ssd_fwd_call(xbc, dtx, acx, acr, acl, dsk)[0]


def _ssd_fwd(xbc, dtx, acx, acr, acl, dsk):
    y, hps = _ssd_fwd_call(xbc, dtx, acx, acr, acl, dsk)
    return y, (xbc, dtx, acx, acr, acl, dsk, hps)


def _ssd_bwd(res, dy):
    dx, db, dc, ddt, dac, dacr, dal, dds = _ssd_bwd_call(*res, dy)
    return jnp.concatenate([dx, db, dc], axis=-1), ddt, dac, dacr, dal, dds


ssd.defvjp(_ssd_fwd, _ssd_bwd)


def _pack_small(arrs):
    flat = jnp.concatenate([a.reshape(-1) for a in arrs])
    rows = -(-flat.shape[0] // (8 * LANE)) * 8
    return jnp.pad(flat, (0, rows * LANE - flat.shape[0])).reshape(rows, LANE)


def _unpack_small(buf, shapes):
    flat = buf.reshape(-1)
    out, off = [], 0
    for shp in shapes:
        n = int(np.prod(shp))
        out.append(flat[off:off + n].reshape(shp))
        off += n
    return out


def _rows_tile(rows, cap):
    for cand in range(min(rows, cap), 7, -8):
        if rows % cand == 0:
            return cand
    return rows


def _pair_sum(mine, theirs, cidx, name):
    n4, kk, nn = mine.shape
    half = kk // 2
    tr = _rows_tile(half, 256)
    nb = half // tr

    def body(c_ref, a_ref, b_ref, o_ref):
        o_ref[...] = a_ref[...] + b_ref[...]

    spec = pl.BlockSpec((1, tr, nn), lambda j, i, c: (j, i, 0))
    grid_spec = pltpu.PrefetchScalarGridSpec(
        num_scalar_prefetch=1, grid=(n4, nb),
        in_specs=[pl.BlockSpec((1, tr, nn), lambda j, i, c: (j, c[0] * nb + i, 0)), spec], out_specs=spec)
    return pl.pallas_call(body, grid_spec=grid_spec, out_shape=jax.ShapeDtypeStruct((n4, half, nn), F32), name=name,
                          compiler_params=_cparams(("parallel", "parallel")))(cidx, mine, theirs)


def _stack_sum(stack, name):
    n, rows, nn = stack.shape
    tr = _rows_tile(rows, 256)

    def body(s_ref, o_ref):
        acc = s_ref[0]
        for d in range(1, n):
            acc = acc + s_ref[d]
        o_ref[...] = acc

    return pl.pallas_call(
        body, grid=(rows // tr,), in_specs=[pl.BlockSpec((n, tr, nn), lambda i: (0, i, 0))],
        out_specs=pl.BlockSpec((tr, nn), lambda i: (i, 0)), out_shape=jax.ShapeDtypeStruct((rows, nn), F32),
        name=name, compiler_params=_cparams(("parallel",)))(stack)


def _adam_call(w, g, m, v, name):
    rows, nn = w.shape
    tr = _rows_tile(rows, 128)

    def body(w_ref, g_ref, m_ref, v_ref, d_ref, nm_ref, nv_ref):
        d_ref[...], nm_ref[...], nv_ref[...] = _adam_fn(w_ref[...], g_ref[...], m_ref[...], v_ref[...])

    spec = pl.BlockSpec((tr, nn), lambda i: (i, 0))
    sds = jax.ShapeDtypeStruct((rows, nn), F32)
    return pl.pallas_call(body, grid=(rows // tr,), in_specs=[spec] * 4, out_specs=[spec] * 3,
                          out_shape=[sds] * 3, name=name, compiler_params=_cparams(("parallel",)))(w, g, m, v)


def _adam_fn(w, g, m, v):
    m = ADAM_B1 * m + (1.0 - ADAM_B1) * g
    v = ADAM_B2 * v + (1.0 - ADAM_B2) * (g * g)
    m_hat = m / (1.0 - ADAM_B1 ** ADAM_STEP)
    v_hat = v / (1.0 - ADAM_B2 ** ADAM_STEP)
    delta = -ADAM_LR * (m_hat / (jnp.sqrt(v_hat) + ADAM_EPS) + ADAM_WD * w)
    return delta, m, v


def _mesh_pos():
    return lax.axis_index("x"), lax.axis_index("y"), lax.axis_index("c")


def _other_chips(x, y):
    return [(1 - x, y), (x, 1 - y), (1 - x, 1 - y)]


HBM_SPEC = pl.BlockSpec(memory_space=pl.ANY)


def _remote(src, dst, send_sems, recv_sems, k, to):
    return pltpu.make_async_remote_copy(src_ref=src, dst_ref=dst, send_sem=send_sems.at[k], recv_sem=recv_sems.at[k],
                                        device_id=to, device_id_type=MESH)


def _half_rows(c, rows, align):
    half = rows // 2
    return (pl.ds(pl.multiple_of(c * half, align), half), pl.ds(pl.multiple_of((1 - c) * half, align), half))


def _gather_weights(mats, conv):
    n = len(mats)

    def body(*refs):
        ins, conv_in = refs[:n], refs[n]
        outs, conv_out = refs[n + 1:2 * n + 1], refs[2 * n + 1]
        send_sems, recv_sems, local_sems = refs[2 * n + 2:]
        x, y, c = _mesh_pos()
        me, sibling, s = (x, y, c), (x, y, 1 - c), 2 * x + y
        chips = _other_chips(x, y)
        rows = [_half_rows(c, m.shape[0], 16) for m in mats]
        own = [pltpu.make_async_copy(ins[i], outs[i].at[s], local_sems.at[i]) for i in range(n)]
        own.append(pltpu.make_async_copy(conv_in, conv_out.at[s], local_sems.at[n]))
        for cp in own:
            cp.start()
        sent = []
        for i in range(n):
            mine = rows[i][0]
            for j, (cx, cy) in enumerate(chips):
                sent.append(_remote(ins[i].at[mine], outs[i].at[s, mine], send_sems, recv_sems, 6 * i + j, (cx, cy, c)))
        for j, (cx, cy) in enumerate(chips):
            sent.append(_remote(conv_in, conv_out.at[s], send_sems, recv_sems, 6 * n + j, (cx, cy, c)))
        for cp in sent:
            cp.start()
        for i in range(n):
            mine = rows[i][0]
            for j, (cx, cy) in enumerate(chips):
                landed = outs[i].at[2 * cx + cy, mine]
                _remote(landed, landed, send_sems, recv_sems, 6 * i + j, me).wait_recv()
                fwd = _remote(landed, landed, send_sems, recv_sems, 6 * i + 3 + j, sibling)
                fwd.start()
                sent.append(fwd)
        for j, (cx, cy) in enumerate(chips):
            slot = conv_out.at[2 * cx + cy]
            _remote(slot, slot, send_sems, recv_sems, 6 * n + j, me).wait_recv()
        for i in range(n):
            theirs_rows = rows[i][1]
            for j, (cx, cy) in enumerate(chips):
                theirs = outs[i].at[2 * cx + cy, theirs_rows]
                _remote(theirs, theirs, send_sems, recv_sems, 6 * i + 3 + j, me).wait_recv()
        for cp in sent:
            cp.wait_send()
        for cp in own:
            cp.wait()

    out_shape = [jax.ShapeDtypeStruct((4,) + m.shape, m.dtype) for m in mats]
    out_shape.append(jax.ShapeDtypeStruct((4,) + conv.shape, conv.dtype))
    res = pl.pallas_call(
        body, in_specs=[HBM_SPEC] * (n + 1), out_specs=[HBM_SPEC] * (n + 1), out_shape=out_shape,
        scratch_shapes=[pltpu.SemaphoreType.DMA((6 * n + 3,)), pltpu.SemaphoreType.DMA((6 * n + 3,)),
                        pltpu.SemaphoreType.DMA((n + 1,))],
        name="all_gather_weights")(*mats, conv)
    return res[:n], res[n]


def _sibling_exchange(stacks):
    n = len(stacks)

    def body(*refs):
        ins, outs = refs[:n], refs[n:2 * n]
        send_sems, recv_sems = refs[2 * n:]
        x, y, c = _mesh_pos()
        cps = []
        for i in range(n):
            theirs = _half_rows(c, stacks[i].shape[1], 8)[1]
            cps.append(_remote(ins[i].at[:, theirs, :], outs[i], send_sems, recv_sems, i, (x, y, 1 - c)))
        for cp in cps:
            cp.start()
        for cp in cps:
            cp.wait()

    out_shape = [jax.ShapeDtypeStruct((4, s.shape[1] // 2, s.shape[2]), s.dtype) for s in stacks]
    return pl.pallas_call(
        body, in_specs=[HBM_SPEC] * n, out_specs=[HBM_SPEC] * n, out_shape=out_shape,
        scratch_shapes=[pltpu.SemaphoreType.DMA((n,)), pltpu.SemaphoreType.DMA((n,))],
        name="grad_sibling_exchange")(*stacks)


def _chip_exchange(parts):
    n = len(parts)

    def body(*refs):
        ins, outs = refs[:n], refs[n:2 * n]
        send_sems, recv_sems, local_sems = refs[2 * n:]
        x, y, c = _mesh_pos()
        me, s = (x, y, c), 2 * x + y
        chips = _other_chips(x, y)
        own = [pltpu.make_async_copy(ins[i].at[s], outs[i].at[s], local_sems.at[i]) for i in range(n)]
        for cp in own:
            cp.start()
        sent = [_remote(ins[i].at[2 * cx + cy], outs[i].at[s], send_sems, recv_sems, 3 * i + j, (cx, cy, c))
                for i in range(n) for j, (cx, cy) in enumerate(chips)]
        for cp in sent:
            cp.start()
        for i in range(n):
            for j, (cx, cy) in enumerate(chips):
                slot = outs[i].at[2 * cx + cy]
                _remote(slot, slot, send_sems, recv_sems, 3 * i + j, me).wait_recv()
        for cp in sent:
            cp.wait_send()
        for cp in own:
            cp.wait()

    return pl.pallas_call(
        body, in_specs=[HBM_SPEC] * n, out_specs=[HBM_SPEC] * n,
        out_shape=[jax.ShapeDtypeStruct(p.shape, p.dtype) for p in parts],
        scratch_shapes=[pltpu.SemaphoreType.DMA((3 * n,)), pltpu.SemaphoreType.DMA((3 * n,)),
                        pltpu.SemaphoreType.DMA((n,))],
        name="grad_chip_exchange")(*parts)


def _sibling_concat(halves):
    n = len(halves)

    def body(*refs):
        ins, outs = refs[:n], refs[n:2 * n]
        send_sems, recv_sems, local_sems = refs[2 * n:]
        x, y, c = _mesh_pos()
        own, sent = [], []
        for i in range(n):
            mine, theirs = _half_rows(c, 2 * halves[i].shape[0], 8)
            own.append(pltpu.make_async_copy(ins[i], outs[i].at[mine], local_sems.at[i]))
            sent.append(_remote(ins[i], outs[i].at[mine], send_sems, recv_sems, i, (x, y, 1 - c)))
        for cp in own + sent:
            cp.start()
        for i in range(n):
            theirs = outs[i].at[_half_rows(c, 2 * halves[i].shape[0], 8)[1]]
            _remote(theirs, theirs, send_sems, recv_sems, i, (x, y, c)).wait_recv()
        for cp in sent:
            cp.wait_send()
        for cp in own:
            cp.wait()

    return pl.pallas_call(
        body, in_specs=[HBM_SPEC] * n, out_specs=[HBM_SPEC] * n,
        out_shape=[jax.ShapeDtypeStruct((2 * h.shape[0], h.shape[1]), h.dtype) for h in halves],
        scratch_shapes=[pltpu.SemaphoreType.DMA((n,)), pltpu.SemaphoreType.DMA((n,)), pltpu.SemaphoreType.DMA((n,))],
        name="grad_sibling_concat")(*halves)


def _gather_small(vec):
    def body(in_ref, out_ref, send_sems, recv_sems, local_sem):
        x, y, c = _mesh_pos()
        me = (x, y, c)
        own = pltpu.make_async_copy(in_ref, out_ref.at[4 * x + 2 * y + c], local_sem)
        own.start()
        peers = [(1 - x if k & 4 else x, 1 - y if k & 2 else y, 1 - c if k & 1 else c) for k in range(1, 8)]
        sent = [_remote(in_ref, out_ref.at[4 * x + 2 * y + c], send_sems, recv_sems, k, p) for k, p in enumerate(peers)]
        for cp in sent:
            cp.start()
        for k, (px, py, pc) in enumerate(peers):
            slot = out_ref.at[4 * px + 2 * py + pc]
            _remote(slot, slot, send_sems, recv_sems, k, me).wait_recv()
        for cp in sent:
            cp.wait_send()
        own.wait()

    return pl.pallas_call(
        body, in_specs=[HBM_SPEC], out_specs=HBM_SPEC, out_shape=jax.ShapeDtypeStruct((8,) + vec.shape, vec.dtype),
        scratch_shapes=[pltpu.SemaphoreType.DMA((7,)), pltpu.SemaphoreType.DMA((7,)), pltpu.SemaphoreType.DMA],
        name="grad_gather_small")(vec)


def _reduce_matrices(stacks, names):
    cidx = lax.axis_index("c").astype(jnp.int32).reshape(1)
    got = _sibling_exchange(stacks)
    pairs = [_pair_sum(a, b, cidx, "grad_pair_sum_" + nm) for a, b, nm in zip(stacks, got, names)]
    quads = _chip_exchange(pairs)
    halves = [_stack_sum(q, "grad_chip_sum_" + nm) for q, nm in zip(quads, names)]
    return _sibling_concat(halves)


def _pad_cols(a, n):
    return jnp.concatenate([a, jnp.zeros((a.shape[0], n - a.shape[1]), a.dtype)], axis=1)


def _lay_w_in(w):
    idx = np.cumsum(IN_SIZES)[:-1]
    segs = jnp.split(w, [int(v) for v in idx], axis=1)
    segs[2] = _pad_cols(segs[2], LANE)
    segs[5] = _pad_cols(segs[5], LANE)
    return jnp.concatenate(segs, axis=1)


IN_PAD_SIZES = (Q_RANK, KV_RANK, LANE, D_INNER, CONV_CH, LANE, D_MODEL, D_MODEL)
IN_PAD_OFFS = [int(v) for v in np.cumsum(IN_PAD_SIZES)[:-1]]


@jax.custom_vjp
def split_proj(proj):
    return tuple(jnp.split(proj, IN_PAD_OFFS, axis=-1))


split_proj.defvjp(lambda proj: (tuple(jnp.split(proj, IN_PAD_OFFS, axis=-1)), None),
                  lambda _, cots: (jnp.concatenate(cots, axis=-1),))


def _lay_w_uq(w):
    w3 = w.reshape(Q_RANK, N_HEADS, NOPE + ROPE)
    w3 = jnp.concatenate([w3, jnp.zeros((Q_RANK, N_HEADS, QK_PAD - NOPE - ROPE), w.dtype)], axis=2)
    return w3.reshape(Q_RANK, N_HEADS * QK_PAD)


def _lay_w_ukv(w):
    w3 = w.reshape(KV_RANK, N_HEADS, NOPE + V_DIM)
    return jnp.concatenate([w3[:, :, :NOPE].reshape(KV_RANK, -1), w3[:, :, NOPE:].reshape(KV_RANK, -1)], axis=1)


def _pad_lanes(v, n=LANE):
    return jnp.concatenate([v, jnp.zeros((v.shape[0], n - v.shape[1]), v.dtype)], axis=1)


def _local_loss(toks, small, x, wb, c8, posf, target):
    B, S, D = x.shape
    T = B * S

    def lin(name, a, key, lay=lambda w: w, out_dtype=F32):
        return make_linear(name, out_dtype)(a, lay(wb[key]), lay(toks[key]))

    rows2 = lambda a: a.reshape(T, a.shape[-1])
    rows3 = lambda a: a.reshape(B, S, a.shape[-1])

    sc = make_rowwise("silu_c", _f_silu, 1, 0, 0, ('row',))((c8[None],), (), ())[0][0]
    mod = lin("ada", sc, 'w_ada')[:B] + small['b_ada']
    shift1, scale1, gate1, shift2, scale2, gate2 = [m[:, None, :] for m in jnp.split(mod, 6, axis=-1)]

    modulate = make_rowwise("modulate1", _f_modulate, 1, 2, 1, ('row',))
    h = modulate((x,), (scale1, shift1), (small['g_pre_mix'],))[0]
    proj = rows3(lin("w_in", rows2(h), 'w_in', _lay_w_in))
    q_lat, kv_lat, k_rope, z, xbc, dt_raw, gate_a, gate_b = split_proj(proj)

    inv = ROPE_THETA ** (-jnp.arange(ROPE // 2, dtype=F32) / (ROPE // 2))
    inv_lane = jnp.concatenate([inv, inv, jnp.zeros((LANE - ROPE,), F32)])[None]
    tabs = tuple(_rope_tables(posf, inv_lane))
    qn = make_rowwise("rms_q", _f_rms, 1, 0, 1, ('row',))((q_lat,), (), (small['g_q_lat'],))[0]
    kvn = make_rowwise("rms_kv", _f_rms, 1, 0, 1, ('row',))((kv_lat,), (), (small['g_kv_lat'],))[0]
    qp = rows3(lin("w_uq", rows2(qn), 'w_uq', _lay_w_uq))
    kvp = rows3(lin("w_ukv", rows2(kvn), 'w_ukv', _lay_w_ukv, BF16))
    qr = rope_q(qp, tabs)
    kr = build_k(kvp, k_rope, tabs)
    att = attention(qr, kr, kvp)
    attn = rows3(lin("w_o_attn", rows2(att), 'w_o_attn'))

    xa = conv_silu(xbc, wb['conv_w_f32'], small['conv_b'])
    dt_pad, a_pad = make_rowwise("dt_softplus", _f_dt, 1, 0, 2, ('row', 'row'))(
        (dt_raw,), (), (_pad_lanes(small['dt_bias']), _pad_lanes(small['a_log'])))
    ac_pad = chunk_cumsum(a_pad)
    dt32, ac32 = dt_pad[..., :SSM_HEADS], ac_pad[..., :SSM_HEADS]
    dtx = jnp.repeat(dt32, HEAD_P, axis=-1)
    acx = jnp.repeat(ac32, HEAD_P, axis=-1)
    acr = jnp.transpose(ac32, (0, 2, 1))[:, :, None, :]
    acl = acx.reshape(B, S // CHUNK, CHUNK, D_INNER)[:, :, CHUNK - 1:CHUNK, :]
    dsk = jnp.repeat(small['d_skip'], HEAD_P, axis=-1)
    y = ssd(xa, dtx, acx, acr, acl, dsk)
    yg = make_rowwise("gated_norm", _f_gated_norm, 2, 0, 1, ('row',), ncol=SSM_GROUPS)(
        (y, z), (), (small['g_ssm_out'],))[0]
    ssm = rows3(lin("w_o_ssm", rows2(yg), 'w_o_ssm'))

    merged = make_rowwise("merge", _f_merge, 4, 0, 0, ('row',))((attn, ssm, gate_a, gate_b), (), ())[0]
    mix = rows3(lin("w_out", rows2(merged), 'w_out'))
    x1 = make_rowwise("post_mix", _f_post, 2, 1, 1, ('row',))((x, mix), (gate1,), (small['g_post_mix'],))[0]

    h2 = make_rowwise("modulate2", _f_modulate, 1, 2, 1, ('row',))((x1,), (scale2, shift2), (small['g_pre_mlp'],))[0]
    u = rows3(lin("w_ff1", rows2(h2), 'w_ff1'))
    act = make_rowwise("relu2", _f_relu2, 1, 0, 0, ('row',))((u,), (), ())[0]
    ff = rows3(lin("w_ff2", rows2(act), 'w_ff2'))
    lvec = make_rowwise("final_loss", _f_final_loss, 3, 1, 1, ('sum',), nodiff=(2,))(
        (x1, ff, target), (gate2,), (small['g_post_mlp'],))[0]
    return jnp.sum(lvec)


MATRICES = COL_SHARDED + ROW_SHARDED


def _local_step(x, c, positions, target, wb, small):
    B = x.shape[0]
    c8 = jnp.concatenate([c, jnp.zeros((16 - B, c.shape[1]), F32)], axis=0)
    posf = positions.astype(F32)[..., None]
    toks = {k: jnp.zeros(wb[k].shape, F32) for k in MATRICES if k != 'conv_w'}
    conv_w = wb['conv_w_f32']

    def loss_fn(toks, small, conv_w, x):
        wbl = dict(wb)
        wbl['conv_w_f32'] = conv_w
        return _local_loss(toks, small, x, wbl, c8, posf, target)

    loss, (g_tok, g_small, g_conv, g_x) = jax.value_and_grad(loss_fn, argnums=(0, 1, 2, 3))(toks, small, conv_w, x)
    grads = dict(g_tok)
    grads.update(g_small)
    grads['conv_w'] = g_conv
    return loss, g_x, grads


def kernel(x, c, positions, w_ada, b_ada, g_pre_mix, g_post_mix, w_in, g_q_lat, g_kv_lat, w_uq, w_ukv, w_o_attn, conv_w, conv_b, dt_bias, a_log, d_skip, g_ssm_out, w_o_ssm, w_out, g_pre_mlp, g_post_mlp, w_ff1, w_ff2, loss_target, m_w_ada, m_b_ada, m_g_pre_mix, m_g_post_mix, m_w_in, m_g_q_lat, m_g_kv_lat, m_w_uq, m_w_ukv, m_w_o_attn, m_conv_w, m_conv_b, m_dt_bias, m_a_log, m_d_skip, m_g_ssm_out, m_w_o_ssm, m_w_out, m_g_pre_mlp, m_g_post_mlp, m_w_ff1, m_w_ff2, v_w_ada, v_b_ada, v_g_pre_mix, v_g_post_mix, v_w_in, v_g_q_lat, v_g_kv_lat, v_w_uq, v_w_ukv, v_w_o_attn, v_conv_w, v_conv_b, v_dt_bias, v_a_log, v_d_skip, v_g_ssm_out, v_w_o_ssm, v_w_out, v_g_pre_mlp, v_g_post_mlp, v_w_ff1, v_w_ff2):
    given = dict(locals())
    w_loc = {n: given[n] for n in WEIGHTS}
    m_loc = {n: given["m_" + n] for n in WEIGHTS}
    v_loc = {n: given["v_" + n] for n in WEIGHTS}
    mats = [n for n in WEIGHTS if n in MATRICES and n != 'conv_w']
    vecs = [n for n in WEIGHTS if n not in MATRICES]

    g_mats, g_conv = _gather_weights([w_loc[n][0].astype(BF16) for n in mats], conv_w[0])
    wb = {}
    for n, g in zip(mats, g_mats):
        if n in COL_SHARDED:
            wb[n] = jnp.transpose(g, (1, 0, 2)).reshape(g.shape[1], -1)
        else:
            wb[n] = g.reshape(-1, g.shape[2])
    wb['conv_w_f32'] = jnp.transpose(g_conv, (1, 0, 2)).reshape(CONV_K, -1)
    small = {n: w_loc[n] for n in vecs}

    loss_part, grad_x, grads = _local_step(x, c, positions, loss_target, wb, small)
    loss = lax.psum(loss_part, ("x", "y", "c"))

    stacks = []
    for n in mats:
        kk, nn = w_loc[n].shape[1:]
        if n in COL_SHARDED:
            stacks.append(jnp.transpose(grads[n].reshape(kk, 4, nn), (1, 0, 2)))
        else:
            stacks.append(grads[n].reshape(4, kk, nn))
    g_shard = dict(zip(mats, _reduce_matrices(stacks, mats)))

    vec_shapes = [tuple(grads[n].shape) for n in vecs] + [tuple(grads['conv_w'].shape)]
    total = _stack_sum(_gather_small(_pack_small([grads[n] for n in vecs] + [grads['conv_w']])), "grad_sum_small")
    g_vec = _unpack_small(total, vec_shapes)
    n_conv = conv_w.shape[2]
    chip = 2 * lax.axis_index("x") + lax.axis_index("y")
    g_shard['conv_w'] = lax.dynamic_slice_in_dim(g_vec[-1], chip * n_conv, n_conv, axis=1)
    for n, g in zip(vecs, g_vec):
        g_shard[n] = g

    delta, new_m, new_v = {}, {}, {}
    for n in mats:
        delta[n], new_m[n], new_v[n] = _adam_call(w_loc[n][0], g_shard[n], m_loc[n][0], v_loc[n][0], "adamw_" + n)
    rest = vecs + ['conv_w']
    rest_shapes = [tuple(w_loc[n].shape) for n in rest]
    packed = [_pack_small([src[n] for n in rest]) for src in (w_loc, g_shard, m_loc, v_loc)]
    for dst, buf in zip((delta, new_m, new_v), _adam_call(*packed, "adamw_small")):
        dst.update(zip(rest, _unpack_small(buf, rest_shapes)))

    def out(d):
        return [d[n].reshape(w_loc[n].shape) for n in WEIGHTS]

    return (loss, grad_x, *out(g_shard), *out(delta), *out(new_m), *out(new_v))
```

```python
import functools
import math

import numpy as np
import jax
import jax.numpy as jnp
from jax import lax
from jax.experimental import pallas as pl
from jax.experimental.pallas import tpu as pltpu

F32 = jnp.float32
BF16 = jnp.bfloat16
MESH = pl.DeviceIdType.MESH

D_MODEL = 1024
N_HEADS = 8
NOPE = 128
ROPE = 64
V_DIM = 128
Q_RANK = 256
KV_RANK = 256
ROPE_THETA = 10000.0
D_INNER = 2048
SSM_HEADS = 32
SSM_GROUPS = 8
HEAD_P = 64
STATE_N = 128
CONV_K = 4
CHUNK = 128
CONV_CH = D_INNER + 2 * SSM_GROUPS * STATE_N
D_FF = 4096
EPS = 1e-6
IN_SIZES = (Q_RANK, KV_RANK, ROPE, D_INNER, CONV_CH, SSM_HEADS, D_MODEL, D_MODEL)
ADAM_LR, ADAM_B1, ADAM_B2, ADAM_EPS, ADAM_WD, ADAM_STEP = 0.001, 0.9, 0.999, 1e-08, 0.01, 10

VMEM_LIMIT_BYTES = 52 * 1024 * 1024
LANE = 128
QK_PAD = 256

WEIGHTS = ['w_ada', 'b_ada', 'g_pre_mix', 'g_post_mix', 'w_in', 'g_q_lat', 'g_kv_lat', 'w_uq', 'w_ukv',
           'w_o_attn', 'conv_w', 'conv_b', 'dt_bias', 'a_log', 'd_skip', 'g_ssm_out', 'w_o_ssm', 'w_out',
           'g_pre_mlp', 'g_post_mlp', 'w_ff1', 'w_ff2']
COL_SHARDED = ('w_ada', 'w_in', 'w_uq', 'w_ukv', 'conv_w', 'w_ff1')
ROW_SHARDED = ('w_o_attn', 'w_o_ssm', 'w_out', 'w_ff2')


def _cparams(sem):
    return pltpu.CompilerParams(dimension_semantics=sem, vmem_limit_bytes=VMEM_LIMIT_BYTES)


def _tile(n, cap):
    if n <= cap:
        return n
    k = n // LANE
    best = LANE
    for d in range(1, k + 1):
        if k % d == 0 and d * LANE <= cap:
            best = d * LANE
    return best


def _mm(a, w, name, out_dtype=F32):
    M, K = a.shape
    N = w.shape[1]
    tm = min(M, 512)
    tn = _tile(N, 1024)
    tk = _tile(K, 2048)
    nk = K // tk
    assert nk == 1 or out_dtype == F32

    def body(a_ref, w_ref, o_ref):
        part = jnp.dot(a_ref[...].astype(BF16), w_ref[...], preferred_element_type=F32)
        if nk == 1:
            o_ref[...] = part.astype(out_dtype)
        else:
            k = pl.program_id(2)

            @pl.when(k == 0)
            def _():
                o_ref[...] = part

            @pl.when(k > 0)
            def _():
                o_ref[...] += part

    return pl.pallas_call(
        body, grid=(M // tm, N // tn, nk),
        in_specs=[pl.BlockSpec((tm, tk), lambda i, j, k: (i, k)), pl.BlockSpec((tk, tn), lambda i, j, k: (k, j))],
        out_specs=pl.BlockSpec((tm, tn), lambda i, j, k: (i, j)),
        out_shape=jax.ShapeDtypeStruct((M, N), out_dtype), name=name,
        compiler_params=_cparams(("parallel", "parallel", "arbitrary")))(a, w)


def _mm_tn(a, g, name):
    M, K = a.shape
    N = g.shape[1]
    tm = min(M, 512)
    tk = _tile(K, 1024)
    tn = _tile(N, 1024)
    nm = M // tm

    def body(a_ref, g_ref, o_ref):
        part = lax.dot_general(a_ref[...].astype(BF16), g_ref[...].astype(BF16), (((0,), (0,)), ((), ())),
                               preferred_element_type=F32)
        m = pl.program_id(2)

        @pl.when(m == 0)
        def _():
            o_ref[...] = part

        @pl.when(m > 0)
        def _():
            o_ref[...] += part

    return pl.pallas_call(
        body, grid=(K // tk, N // tn, nm),
        in_specs=[pl.BlockSpec((tm, tk), lambda i, j, m: (m, i)), pl.BlockSpec((tm, tn), lambda i, j, m: (m, j))],
        out_specs=pl.BlockSpec((tk, tn), lambda i, j, m: (i, j)),
        out_shape=jax.ShapeDtypeStruct((K, N), F32), name=name,
        compiler_params=_cparams(("parallel", "parallel", "arbitrary")))(a, g)


def make_linear(name, out_dtype=F32):
    @jax.custom_vjp
    def linear(a, w, tok):
        return _mm(a, w, name + "_fwd", out_dtype)

    def fwd(a, w, tok):
        return _mm(a, w, name + "_fwd", out_dtype), (a, w)

    def bwd(res, g):
        a, w = res
        da = _mm(g, w.T, name + "_dx", a.dtype)
        dw = _mm_tn(a, g, name + "_dw")
        return da, jnp.zeros_like(w), dw

    linear.defvjp(fwd, bwd)
    return linear


def make_rowwise(name, f, n_rows, n_seqs, n_pars, out_kinds, ncol=1, nodiff=(), ts_cap=512):
    n_in = n_rows + n_seqs + n_pars
    diff_idx = [i for i in range(n_in) if i not in nodiff]

    def _dims(rows):
        B, S = rows[0].shape[0], rows[0].shape[1]
        ts = min(S, ts_cap)
        return B, S, ts

    def _in_specs(rows, seqs, pars, ts):
        specs = []
        for r in rows:
            specs.append(pl.BlockSpec((1, ts, r.shape[2] // ncol), lambda k, b, s: (b, s, k)))
        for q in seqs:
            specs.append(pl.BlockSpec((1, 1, q.shape[2] // ncol), lambda k, b, s: (b, 0, k)))
        for p in pars:
            specs.append(pl.BlockSpec((1, p.shape[1] // ncol), lambda k, b, s: (0, k)))
        return specs

    def _load(refs):
        vals = [r[0] for r in refs[:n_rows + n_seqs]]
        vals += [r[...] for r in refs[n_rows + n_seqs:n_in]]
        return vals

    def _out_struct(rows, seqs, pars, ts):
        blocks = [jax.ShapeDtypeStruct((ts, r.shape[2] // ncol), r.dtype) for r in rows]
        blocks += [jax.ShapeDtypeStruct((1, q.shape[2] // ncol), q.dtype) for q in seqs]
        blocks += [jax.ShapeDtypeStruct((1, p.shape[1] // ncol), p.dtype) for p in pars]
        return jax.eval_shape(f, *blocks)

    def _fwd_call(rows, seqs, pars):
        B, S, ts = _dims(rows)
        outs = _out_struct(rows, seqs, pars, ts)
        n_out = len(outs)

        def body(*refs):
            res = f(*_load(refs))
            first = (pl.program_id(1) == 0) & (pl.program_id(2) == 0)
            for o_ref, val, kind in zip(refs[n_in:], res, out_kinds):
                if kind == 'row':
                    o_ref[0] = val
                else:
                    tot = jnp.sum(val, axis=0, keepdims=True)

                    @pl.when(first)
                    def _(o_ref=o_ref, tot=tot):
                        o_ref[...] = tot

                    @pl.when(jnp.logical_not(first))
                    def _(o_ref=o_ref, tot=tot):
                        o_ref[...] += tot

        out_shape, out_specs = [], []
        for o, kind in zip(outs, out_kinds):
            d = o.shape[1]
            if kind == 'row':
                out_shape.append(jax.ShapeDtypeStruct((B, S, ncol * d), o.dtype))
                out_specs.append(pl.BlockSpec((1, ts, d), lambda k, b, s: (b, s, k)))
            else:
                out_shape.append(jax.ShapeDtypeStruct((1, ncol * d), o.dtype))
                out_specs.append(pl.BlockSpec((1, d), lambda k, b, s: (0, k)))
        res = pl.pallas_call(
            body, grid=(ncol, B, S // ts), in_specs=_in_specs(rows, seqs, pars, ts), out_specs=out_specs,
            out_shape=out_shape, name=name + "_fwd",
            compiler_params=_cparams(("arbitrary", "arbitrary", "arbitrary")))(*rows, *seqs, *pars)
        return tuple(res)

    def _bwd_call(rows, seqs, pars, cots):
        B, S, ts = _dims(rows)
        outs = _out_struct(rows, seqs, pars, ts)
        n_out = len(outs)
        all_in = list(rows) + list(seqs) + list(pars)

        def body(*refs):
            vals = _load(refs)
            cts = []
            for c_ref, o, kind in zip(refs[n_in:n_in + n_out], outs, out_kinds):
                if kind == 'row':
                    cts.append(c_ref[0])
                else:
                    cts.append(jnp.broadcast_to(c_ref[...], o.shape))

            def g(*dv):
                full = list(vals)
                for i, v in zip(diff_idx, dv):
                    full[i] = v
                return tuple(f(*full))

            _, vjp = jax.vjp(g, *[vals[i] for i in diff_idx])
            grads = vjp(tuple(cts))
            b, s = pl.program_id(1), pl.program_id(2)
            for o_ref, i, gr in zip(refs[n_in + n_out:], diff_idx, grads):
                if i < n_rows:
                    o_ref[0] = gr
                else:
                    first = (s == 0) if i < n_rows + n_seqs else ((b == 0) & (s == 0))
                    target = (lambda r: r.at[0]) if i < n_rows + n_seqs else (lambda r: r)

                    @pl.when(first)
                    def _(o_ref=o_ref, gr=gr, target=target):
                        target(o_ref)[...] = gr

                    @pl.when(jnp.logical_not(first))
                    def _(o_ref=o_ref, gr=gr, target=target):
                        target(o_ref)[...] += gr

        cot_specs = []
        for o, kind in zip(outs, out_kinds):
            d = o.shape[1]
            if kind == 'row':
                cot_specs.append(pl.BlockSpec((1, ts, d), lambda k, b, s: (b, s, k)))
            else:
                cot_specs.append(pl.BlockSpec((1, d), lambda k, b, s: (0, k)))
        out_shape, out_specs = [], []
        for i in diff_idx:
            a = all_in[i]
            out_shape.append(jax.ShapeDtypeStruct(a.shape, a.dtype))
            if i < n_rows:
                out_specs.append(pl.BlockSpec((1, ts, a.shape[2] // ncol), lambda k, b, s: (b, s, k)))
            elif i < n_rows + n_seqs:
                out_specs.append(pl.BlockSpec((1, 1, a.shape[2] // ncol), lambda k, b, s: (b, 0, k)))
            else:
                out_specs.append(pl.BlockSpec((1, a.shape[1] // ncol), lambda k, b, s: (0, k)))
        res = pl.pallas_call(
            body, grid=(ncol, B, S // ts), in_specs=_in_specs(rows, seqs, pars, ts) + cot_specs,
            out_specs=out_specs, out_shape=out_shape, name=name + "_bwd",
            compiler_params=_cparams(("arbitrary", "arbitrary", "arbitrary")))(*all_in, *cots)
        grads = [None] * n_in
        for i, r in zip(diff_idx, res):
            grads[i] = r
        for i in nodiff:
            grads[i] = jnp.zeros_like(all_in[i])
        return tuple(grads[:n_rows]), tuple(grads[n_rows:n_rows + n_seqs]), tuple(grads[n_rows + n_seqs:])

    @jax.custom_vjp
    def op(rows, seqs, pars):
        return _fwd_call(rows, seqs, pars)

    def fwd(rows, seqs, pars):
        return _fwd_call(rows, seqs, pars), (rows, seqs, pars)

    def bwd(res, cots):
        rows, seqs, pars = res
        return _bwd_call(rows, seqs, pars, cots)

    op.defvjp(fwd, bwd)
    return op


def _rms(x, g):
    return x * lax.rsqrt(jnp.mean(x * x, axis=-1, keepdims=True) + EPS) * g


def _silu(x):
    return x * lax.logistic(x)


def _f_silu(c):
    return (_silu(c),)


def _f_modulate(x, scale, shift, g):
    return (_rms(x, g) * (1.0 + scale) + shift,)


def _f_rms(x, g):
    return (_rms(x, g),)


def _f_dt(dt_raw, dt_bias, a_log):
    z = dt_raw + dt_bias
    dt = jnp.maximum(z, 0.0) + jnp.log1p(jnp.exp(-jnp.abs(z)))
    return dt, dt * (-jnp.exp(a_log))


def _f_gated_norm(y, z, g):
    return (_rms(y * _silu(z), g),)


def _f_merge(attn, ssm, ga, gb):
    return (lax.logistic(ga) * attn + lax.logistic(gb) * ssm,)


def _f_post(x, m, gate, g):
    return (x + gate * _rms(m, g),)


def _f_relu2(u):
    r = jnp.maximum(u, 0.0)
    return (r * r,)


def _f_final_loss(x, ff, target, gate, g):
    e = x + gate * _rms(ff, g) - target
    return (e * e * (0.5 / D_MODEL),)


def _rope_tables(posf, inv_lane):
    B, S, _ = posf.shape
    ts = min(S, 512)

    def body(p_ref, inv_ref, c_ref, a_ref, b_ref):
        ang = p_ref[0] * inv_ref[...]
        cs, sn = jnp.cos(ang), jnp.sin(ang)
        lane = lax.broadcasted_iota(jnp.int32, ang.shape, 1)
        c_ref[0] = jnp.where(lane < ROPE, cs, 0.0)
        a_ref[0] = jnp.where(lane < ROPE // 2, -sn, 0.0)
        b_ref[0] = jnp.where((lane >= ROPE // 2) & (lane < ROPE), sn, 0.0)

    spec = pl.BlockSpec((1, ts, LANE), lambda b, s: (b, s, 0))
    sds = jax.ShapeDtypeStruct((B, S, LANE), F32)
    return pl.pallas_call(
        body, grid=(B, S // ts),
        in_specs=[pl.BlockSpec((1, ts, 1), lambda b, s: (b, s, 0)), pl.BlockSpec((1, LANE), lambda b, s: (0, 0))],
        out_specs=[spec, spec, spec], out_shape=[sds, sds, sds], name="rope_tables",
        compiler_params=_cparams(("parallel", "parallel")))(posf, inv_lane)


def _rot(u, c, a, bm):
    return u * c + pltpu.roll(u, 96, 1) * a + pltpu.roll(u, 32, 1) * bm


def _rot_t(g, c, a, bm):
    return g * c + pltpu.roll(g * a, 32, 1) + pltpu.roll(g * bm, 96, 1)


def _rope_q_call(q, tabs, transpose, name):
    B, S, W = q.shape
    ts = min(S, 512)
    fn = _rot_t if transpose else _rot
    out_dtype = F32 if transpose else BF16

    def body(q_ref, c_ref, a_ref, b_ref, o_ref):
        u = q_ref[0].astype(F32) * ATT_SCALE
        r = fn(u[:, NOPE:], c_ref[0], a_ref[0], b_ref[0])
        o_ref[0] = jnp.concatenate([u[:, :NOPE], r], axis=1).astype(out_dtype)

    tspec = pl.BlockSpec((1, ts, LANE), lambda b, s, h: (b, s, 0))
    qspec = pl.BlockSpec((1, ts, QK_PAD), lambda b, s, h: (b, s, h))
    return pl.pallas_call(
        body, grid=(B, S // ts, W // QK_PAD), in_specs=[qspec, tspec, tspec, tspec], out_specs=qspec,
        out_shape=jax.ShapeDtypeStruct(q.shape, out_dtype), name=name,
        compiler_params=_cparams(("parallel", "parallel", "parallel")))(q, *tabs)


@jax.custom_vjp
def rope_q(q, tabs):
    return _rope_q_call(q, tabs, False, "rope_q_fwd")


def _rope_q_fwd(q, tabs):
    return _rope_q_call(q, tabs, False, "rope_q_fwd"), tabs


def _rope_q_bwd(tabs, g):
    return _rope_q_call(g, tabs, True, "rope_q_bwd"), tuple(jnp.zeros_like(t) for t in tabs)


rope_q.defvjp(_rope_q_fwd, _rope_q_bwd)


def _build_k_fwd_call(kv, kr, tabs):
    B, S, _ = kv.shape
    ts = min(S, 512)

    def body(kv_ref, kr_ref, c_ref, a_ref, b_ref, o_ref):
        r = _rot(kr_ref[0], c_ref[0], a_ref[0], b_ref[0])
        o_ref[0] = jnp.concatenate([kv_ref[0], r.astype(BF16)], axis=1)

    tspec = pl.BlockSpec((1, ts, LANE), lambda b, s, h: (b, s, 0))
    return pl.pallas_call(
        body, grid=(B, S // ts, N_HEADS),
        in_specs=[pl.BlockSpec((1, ts, LANE), lambda b, s, h: (b, s, h)), tspec, tspec, tspec, tspec],
        out_specs=pl.BlockSpec((1, ts, QK_PAD), lambda b, s, h: (b, s, h)),
        out_shape=jax.ShapeDtypeStruct((B, S, N_HEADS * QK_PAD), BF16), name="build_k_fwd",
        compiler_params=_cparams(("parallel", "parallel", "arbitrary")))(kv, kr, *tabs)


def _build_k_bwd_call(g, tabs):
    B, S, _ = g.shape
    ts = min(S, 512)

    def body(g_ref, c_ref, a_ref, b_ref, dk_ref, dr_ref):
        gg = g_ref[0]
        dk_ref[0] = gg[:, :NOPE]
        r = _rot_t(gg[:, NOPE:].astype(F32), c_ref[0], a_ref[0], b_ref[0])
        h = pl.program_id(2)

        @pl.when(h == 0)
        def _():
            dr_ref[0] = r

        @pl.when(h > 0)
        def _():
            dr_ref[0] += r

    tspec = pl.BlockSpec((1, ts, LANE), lambda b, s, h: (b, s, 0))
    return pl.pallas_call(
        body, grid=(B, S // ts, N_HEADS),
        in_specs=[pl.BlockSpec((1, ts, QK_PAD), lambda b, s, h: (b, s, h)), tspec, tspec, tspec],
        out_specs=[pl.BlockSpec((1, ts, LANE), lambda b, s, h: (b, s, h)), tspec],
        out_shape=[jax.ShapeDtypeStruct((B, S, N_HEADS * NOPE), BF16), jax.ShapeDtypeStruct((B, S, LANE), F32)],
        name="build_k_bwd", compiler_params=_cparams(("parallel", "parallel", "arbitrary")))(g, *tabs)


@jax.custom_vjp
def build_k(kv, kr, tabs):
    return _build_k_fwd_call(kv, kr, tabs)


def _build_k_fwd(kv, kr, tabs):
    return _build_k_fwd_call(kv, kr, tabs), (tabs, kv.shape)


def _build_k_bwd(res, g):
    tabs, kv_shape = res
    dk, dr = _build_k_bwd_call(g, tabs)
    dkv = jnp.concatenate([dk, jnp.zeros((kv_shape[0], kv_shape[1], kv_shape[2] - dk.shape[2]), BF16)], axis=-1)
    return dkv, dr, tuple(jnp.zeros_like(t) for t in tabs)


build_k.defvjp(_build_k_fwd, _build_k_bwd)


ATT_SCALE = (NOPE + ROPE) ** -0.5
NEG = -1e30


def _att_tiles(S):
    t = min(S, 512)
    return t, S // t


def _scores(q, k, diagonal):
    s = lax.dot_general(q, k, (((1,), (1,)), ((), ())), preferred_element_type=F32)
    if diagonal:
        row = lax.broadcasted_iota(jnp.int32, s.shape, 0)
        col = lax.broadcasted_iota(jnp.int32, s.shape, 1)
        s = jnp.where(col <= row, s, NEG)
    return s


ATT_HB = 4


def _causal_pairs(n, by_key):
    if by_key:
        pairs = [(i, j) for j in range(n) for i in range(j, n)]
    else:
        pairs = [(i, j) for i in range(n) for j in range(i + 1)]
    return (jnp.asarray([p[0] for p in pairs], jnp.int32), jnp.asarray([p[1] for p in pairs], jnp.int32))


def _head(ref_or_val, h, w):
    return ref_or_val[:, h * w:(h + 1) * w]


def _attn_fwd_call(q, k, vsrc, v_blk0):
    B, S, _ = q.shape
    t, n = _att_tiles(S)
    qi, kj = _causal_pairs(n, False)

    def body(qi_ref, kj_ref, q_ref, k_ref, v_ref, o_ref, lse_ref, m_sc, l_sc, acc_sc):
        p_id = pl.program_id(2)
        i, j = qi_ref[p_id], kj_ref[p_id]

        @pl.when(j == 0)
        def _():
            m_sc[...] = jnp.full(m_sc.shape, NEG, F32)
            l_sc[...] = jnp.zeros(l_sc.shape, F32)
            acc_sc[...] = jnp.zeros(acc_sc.shape, F32)

        def step(diagonal):
            qa, ka, va = q_ref[0], k_ref[0], v_ref[0]
            for h in range(ATT_HB):
                lanes = slice(h * LANE, (h + 1) * LANE)
                s = _scores(_head(qa, h, QK_PAD), _head(ka, h, QK_PAD), diagonal)
                m_prev = m_sc[:, lanes]
                m_new = jnp.maximum(m_prev, jnp.max(s, axis=1, keepdims=True))
                alpha = jnp.exp(m_prev - m_new)
                p = jnp.exp(s - jnp.tile(m_new, (1, t // LANE)))
                l_sc[:, lanes] = alpha * l_sc[:, lanes] + jnp.sum(p, axis=1, keepdims=True)
                acc_sc[:, lanes] = alpha * acc_sc[:, lanes] + jnp.dot(p.astype(BF16), _head(va, h, V_DIM),
                                                                      preferred_element_type=F32)
                m_sc[:, lanes] = m_new

        @pl.when(j < i)
        def _():
            step(False)

        @pl.when(j == i)
        def _():
            step(True)
            o_ref[0] = acc_sc[...] / l_sc[...]
            lse_ref[0] = m_sc[...] + jnp.log(l_sc[...])

    wq, wv = ATT_HB * QK_PAD, ATT_HB * V_DIM
    grid_spec = pltpu.PrefetchScalarGridSpec(
        num_scalar_prefetch=2, grid=(B, N_HEADS // ATT_HB, qi.shape[0]),
        in_specs=[pl.BlockSpec((1, t, wq), lambda b, h, p, qi, kj: (b, qi[p], h)),
                  pl.BlockSpec((1, t, wq), lambda b, h, p, qi, kj: (b, kj[p], h)),
                  pl.BlockSpec((1, t, wv), lambda b, h, p, qi, kj: (b, kj[p], v_blk0 + h))],
        out_specs=[pl.BlockSpec((1, t, wv), lambda b, h, p, qi, kj: (b, qi[p], h)),
                   pl.BlockSpec((1, t, wv), lambda b, h, p, qi, kj: (b, qi[p], h))],
        scratch_shapes=[pltpu.VMEM((t, wv), F32), pltpu.VMEM((t, wv), F32), pltpu.VMEM((t, wv), F32)])
    return pl.pallas_call(
        body, grid_spec=grid_spec,
        out_shape=[jax.ShapeDtypeStruct((B, S, N_HEADS * V_DIM), F32),
                   jax.ShapeDtypeStruct((B, S, N_HEADS * LANE), F32)],
        name="attn_fwd", compiler_params=_cparams(("parallel", "parallel", "arbitrary")))(qi, kj, q, k, vsrc)


def _attn_p_ds(q, k, v, o, do, lse, diagonal, t):
    s = _scores(q, k, diagonal)
    p = jnp.exp(s - jnp.tile(lse, (1, t // LANE)))
    dp = lax.dot_general(do.astype(BF16), v, (((1,), (1,)), ((), ())), preferred_element_type=F32)
    delta = jnp.sum(do * o, axis=1, keepdims=True)
    ds = p * (dp - delta)
    return p, ds


def _attn_dkv_call(q, k, vsrc, v_blk0, o, do, lse):
    B, S, _ = q.shape
    t, n = _att_tiles(S)

    qi, kj = _causal_pairs(n, True)

    def body(qi_ref, kj_ref, q_ref, k_ref, v_ref, o_ref, do_ref, lse_ref, dk_ref, dv_ref, dk_sc, dv_sc):
        p_id = pl.program_id(2)
        i, j = qi_ref[p_id], kj_ref[p_id]

        @pl.when(i == j)
        def _():
            dk_sc[...] = jnp.zeros(dk_sc.shape, F32)
            dv_sc[...] = jnp.zeros(dv_sc.shape, F32)

        def step(diagonal):
            qa, ka, va, oa, doa, la = q_ref[0], k_ref[0], v_ref[0], o_ref[0], do_ref[0], lse_ref[0]
            for h in range(ATT_HB):
                qb, dob = _head(qa, h, QK_PAD), _head(doa, h, V_DIM)
                p, ds = _attn_p_ds(qb, _head(ka, h, QK_PAD), _head(va, h, V_DIM), _head(oa, h, V_DIM), dob,
                                   _head(la, h, LANE), diagonal, t)
                dv_sc[:, h * V_DIM:(h + 1) * V_DIM] += lax.dot_general(
                    p.astype(BF16), dob.astype(BF16), (((0,), (0,)), ((), ())), preferred_element_type=F32)
                dk_sc[:, h * QK_PAD:(h + 1) * QK_PAD] += lax.dot_general(
                    ds.astype(BF16), qb, (((0,), (0,)), ((), ())), preferred_element_type=F32)

        @pl.when(i > j)
        def _():
            step(False)

        @pl.when(i == j)
        def _():
            step(True)

        @pl.when(i == n - 1)
        def _():
            dk_ref[0] = dk_sc[...].astype(BF16)
            dv_ref[0] = dv_sc[...].astype(BF16)

    wq, wv = ATT_HB * QK_PAD, ATT_HB * V_DIM
    at_q = lambda b, h, p, qi, kj: (b, qi[p], h)
    at_k = lambda b, h, p, qi, kj: (b, kj[p], h)
    grid_spec = pltpu.PrefetchScalarGridSpec(
        num_scalar_prefetch=2, grid=(B, N_HEADS // ATT_HB, qi.shape[0]),
        in_specs=[pl.BlockSpec((1, t, wq), at_q), pl.BlockSpec((1, t, wq), at_k),
                  pl.BlockSpec((1, t, wv), lambda b, h, p, qi, kj: (b, kj[p], v_blk0 + h)),
                  pl.BlockSpec((1, t, wv), at_q), pl.BlockSpec((1, t, wv), at_q), pl.BlockSpec((1, t, wv), at_q)],
        out_specs=[pl.BlockSpec((1, t, wq), at_k), pl.BlockSpec((1, t, wv), at_k)],
        scratch_shapes=[pltpu.VMEM((t, wq), F32), pltpu.VMEM((t, wv), F32)])
    return pl.pallas_call(
        body, grid_spec=grid_spec,
        out_shape=[jax.ShapeDtypeStruct((B, S, N_HEADS * QK_PAD), BF16),
                   jax.ShapeDtypeStruct((B, S, N_HEADS * V_DIM), BF16)],
        name="attn_dkv", compiler_params=_cparams(("parallel", "parallel", "arbitrary")))(
            qi, kj, q, k, vsrc, o, do, lse)


def _attn_dq_call(q, k, vsrc, v_blk0, o, do, lse):
    B, S, _ = q.shape
    t, n = _att_tiles(S)

    qi, kj = _causal_pairs(n, False)

    def body(qi_ref, kj_ref, q_ref, k_ref, v_ref, o_ref, do_ref, lse_ref, dq_ref, dq_sc):
        p_id = pl.program_id(2)
        i, j = qi_ref[p_id], kj_ref[p_id]

        @pl.when(j == 0)
        def _():
            dq_sc[...] = jnp.zeros(dq_sc.shape, F32)

        def step(diagonal):
            qa, ka, va, oa, doa, la = q_ref[0], k_ref[0], v_ref[0], o_ref[0], do_ref[0], lse_ref[0]
            for h in range(ATT_HB):
                kb = _head(ka, h, QK_PAD)
                _, ds = _attn_p_ds(_head(qa, h, QK_PAD), kb, _head(va, h, V_DIM), _head(oa, h, V_DIM),
                                   _head(doa, h, V_DIM), _head(la, h, LANE), diagonal, t)
                dq_sc[:, h * QK_PAD:(h + 1) * QK_PAD] += jnp.dot(ds.astype(BF16), kb, preferred_element_type=F32)

        @pl.when(j < i)
        def _():
            step(False)

        @pl.when(j == i)
        def _():
            step(True)
            dq_ref[0] = dq_sc[...].astype(BF16)

    wq, wv = ATT_HB * QK_PAD, ATT_HB * V_DIM
    at_q = lambda b, h, p, qi, kj: (b, qi[p], h)
    at_k = lambda b, h, p, qi, kj: (b, kj[p], h)
    grid_spec = pltpu.PrefetchScalarGridSpec(
        num_scalar_prefetch=2, grid=(B, N_HEADS // ATT_HB, qi.shape[0]),
        in_specs=[pl.BlockSpec((1, t, wq), at_q), pl.BlockSpec((1, t, wq), at_k),
                  pl.BlockSpec((1, t, wv), lambda b, h, p, qi, kj: (b, kj[p], v_blk0 + h)),
                  pl.BlockSpec((1, t, wv), at_q), pl.BlockSpec((1, t, wv), at_q), pl.BlockSpec((1, t, wv), at_q)],
        out_specs=pl.BlockSpec((1, t, wq), at_q),
        scratch_shapes=[pltpu.VMEM((t, wq), F32)])
    return pl.pallas_call(
        body, grid_spec=grid_spec, out_shape=jax.ShapeDtypeStruct((B, S, N_HEADS * QK_PAD), BF16),
        name="attn_dq", compiler_params=_cparams(("parallel", "parallel", "arbitrary")))(
            qi, kj, q, k, vsrc, o, do, lse)


@jax.custom_vjp
def attention(q, k, kv):
    return _attn_fwd_call(q, k, kv, N_HEADS // ATT_HB)[0]


def _attention_fwd(q, k, kv):
    o, lse = _attn_fwd_call(q, k, kv, N_HEADS // ATT_HB)
    return o, (q, k, kv, o, lse)


def _attention_bwd(res, do):
    q, k, kv, o, lse = res
    dk, dv = _attn_dkv_call(q, k, kv, N_HEADS // ATT_HB, o, do, lse)
    dq = _attn_dq_call(q, k, kv, N_HEADS // ATT_HB, o, do, lse)
    dkv = jnp.concatenate([jnp.zeros_like(dv), dv], axis=-1)
    return dq, dk, dkv


attention.defvjp(_attention_fwd, _attention_bwd)


def _shift_down(v, sh, rows):
    return jnp.where(rows >= sh, pltpu.roll(v, sh, 0), 0.0)


def _shift_up(v, sh, rows, S):
    return jnp.where(rows < S - sh, pltpu.roll(v, S - sh, 0), 0.0)


def _conv_pre(u, w_ref, b_ref, rows):
    acc = b_ref[...] + w_ref[pl.ds(CONV_K - 1, 1), :] * u
    for k in range(CONV_K - 1):
        acc = acc + w_ref[pl.ds(k, 1), :] * _shift_down(u, CONV_K - 1 - k, rows)
    return acc


def _conv_fwd_call(u, w, b):
    B, S, C = u.shape

    def body(u_ref, w_ref, b_ref, o_ref):
        uu = u_ref[0]
        rows = lax.broadcasted_iota(jnp.int32, uu.shape, 0)
        o_ref[0] = _silu(_conv_pre(uu, w_ref, b_ref, rows))

    spec = pl.BlockSpec((1, S, LANE), lambda c, bb: (bb, 0, c))
    return pl.pallas_call(
        body, grid=(C // LANE, B),
        in_specs=[spec, pl.BlockSpec((CONV_K, LANE), lambda c, bb: (0, c)), pl.BlockSpec((1, LANE), lambda c, bb: (0, c))],
        out_specs=spec, out_shape=jax.ShapeDtypeStruct(u.shape, F32), name="conv_fwd",
        compiler_params=_cparams(("parallel", "arbitrary")))(u, w, b)


def _conv_bwd_call(u, w, b, g):
    B, S, C = u.shape

    def body(u_ref, w_ref, b_ref, g_ref, du_ref, dw_ref, db_ref):
        uu = u_ref[0]
        rows = lax.broadcasted_iota(jnp.int32, uu.shape, 0)
        pre = _conv_pre(uu, w_ref, b_ref, rows)
        sg = lax.logistic(pre)
        dpre = g_ref[0] * sg * (1.0 + pre * (1.0 - sg))
        du = w_ref[pl.ds(CONV_K - 1, 1), :] * dpre
        dws = [None] * CONV_K
        dws[CONV_K - 1] = jnp.sum(dpre * uu, axis=0, keepdims=True)
        for k in range(CONV_K - 1):
            sh = CONV_K - 1 - k
            du = du + w_ref[pl.ds(k, 1), :] * _shift_up(dpre, sh, rows, S)
            dws[k] = jnp.sum(dpre * _shift_down(uu, sh, rows), axis=0, keepdims=True)
        du_ref[0] = du
        dbv = jnp.sum(dpre, axis=0, keepdims=True)
        first = pl.program_id(1) == 0

        @pl.when(first)
        def _():
            for k in range(CONV_K):
                dw_ref[pl.ds(k, 1), :] = dws[k]
            db_ref[...] = dbv

        @pl.when(jnp.logical_not(first))
        def _():
            for k in range(CONV_K):
                dw_ref[pl.ds(k, 1), :] += dws[k]
            db_ref[...] += dbv

    spec = pl.BlockSpec((1, S, LANE), lambda c, bb: (bb, 0, c))
    wspec = pl.BlockSpec((CONV_K, LANE), lambda c, bb: (0, c))
    bspec = pl.BlockSpec((1, LANE), lambda c, bb: (0, c))
    return pl.pallas_call(
        body, grid=(C // LANE, B), in_specs=[spec, wspec, bspec, spec], out_specs=[spec, wspec, bspec],
        out_shape=[jax.ShapeDtypeStruct(u.shape, F32), jax.ShapeDtypeStruct(w.shape, F32),
                   jax.ShapeDtypeStruct(b.shape, F32)],
        name="conv_bwd", compiler_params=_cparams(("parallel", "arbitrary")))(u, w, b, g)


@jax.custom_vjp
def conv_silu(u, w, b):
    return _conv_fwd_call(u, w, b)


def _conv_silu_fwd(u, w, b):
    return _conv_fwd_call(u, w, b), (u, w, b)


def _conv_silu_bwd(res, g):
    return tuple(_conv_bwd_call(*res, g))


conv_silu.defvjp(_conv_silu_fwd, _conv_silu_bwd)


def _chunk_cumsum_call(a, reverse, name):
    B, S, W = a.shape

    def body(a_ref, o_ref):
        r = lax.broadcasted_iota(jnp.int32, (CHUNK, CHUNK), 0)
        c = lax.broadcasted_iota(jnp.int32, (CHUNK, CHUNK), 1)
        tri = jnp.where((c >= r) if reverse else (c <= r), 1.0, 0.0).astype(F32)
        o_ref[0] = jnp.dot(tri, a_ref[0], preferred_element_type=F32, precision=lax.Precision.HIGHEST)

    spec = pl.BlockSpec((1, CHUNK, W), lambda b, c: (b, c, 0))
    return pl.pallas_call(body, grid=(B, S // CHUNK), in_specs=[spec], out_specs=spec,
                          out_shape=jax.ShapeDtypeStruct(a.shape, F32), name=name,
                          compiler_params=_cparams(("parallel", "parallel")))(a)


@jax.custom_vjp
def chunk_cumsum(a):
    return _chunk_cumsum_call(a, False, "chunk_cumsum_fwd")


chunk_cumsum.defvjp(lambda a: (_chunk_cumsum_call(a, False, "chunk_cumsum_fwd"), None),
                    lambda _, g: (_chunk_cumsum_call(g, True, "chunk_cumsum_bwd"),))


GROUP_W = 4 * HEAD_P
HPG = SSM_HEADS // SSM_GROUPS


def _ssd_masks():
    lane = lax.broadcasted_iota(jnp.int32, (1, GROUP_W), 1)
    return [((lane >= HEAD_P * j) & (lane < HEAD_P * (j + 1))).astype(F32) for j in range(HPG)]


def _ssd_decays(acx, acr_ref, masks):
    r = lax.broadcasted_iota(jnp.int32, (CHUNK, CHUNK), 0)
    c = lax.broadcasted_iota(jnp.int32, (CHUNK, CHUNK), 1)
    out = []
    for j in range(HPG):
        col = jnp.min(acx * masks[j], axis=1, keepdims=True)
        seg = col - acr_ref[0, j]
        out.append(jnp.exp(jnp.where(c <= r, seg, NEG)))
    return out


def _dot(a, b, dims):
    return lax.dot_general(a.astype(BF16), b.astype(BF16), (dims, ((), ())), preferred_element_type=F32)


NN = ((1,), (0,))
NT = ((1,), (1,))
TN = ((0,), (0,))


XBC_W = GROUP_W + 2 * STATE_N


def _ssd_expand(g):
    r = lax.broadcasted_iota(jnp.int32, (LANE, GROUP_W), 0)
    l = lax.broadcasted_iota(jnp.int32, (LANE, GROUP_W), 1)
    return jnp.where(r == HPG * g + l // HEAD_P, 1.0, 0.0).astype(F32)


def _ssd_load(xbc_ref, dt_ref, ac_ref, expand):
    blk = xbc_ref[0]
    x, bm, cm = blk[:, :GROUP_W], blk[:, GROUP_W:GROUP_W + STATE_N], blk[:, GROUP_W + STATE_N:]
    dt = jnp.dot(dt_ref[0], expand, preferred_element_type=F32, precision=lax.Precision.HIGHEST)
    ac = jnp.dot(ac_ref[0], expand, preferred_element_type=F32, precision=lax.Precision.HIGHEST)
    is_last = (lax.broadcasted_iota(jnp.int32, (CHUNK, GROUP_W), 0) == CHUNK - 1).astype(F32)
    return x, bm, cm, dt, ac, is_last


def _ssd_in_specs(nc, rev):
    cc = (lambda c: nc - 1 - c) if rev else (lambda c: c)
    return [pl.BlockSpec((1, CHUNK, XBC_W), lambda b, g, c: (b, cc(c), g)),
            pl.BlockSpec((1, CHUNK, LANE), lambda b, g, c: (b, cc(c), 0)),
            pl.BlockSpec((1, CHUNK, LANE), lambda b, g, c: (b, cc(c), 0)),
            pl.BlockSpec((1, HPG, 1, CHUNK), lambda b, g, c: (b, g, 0, cc(c))),
            pl.BlockSpec((1, GROUP_W), lambda b, g, c: (0, g))]


def _ssd_fwd_call(xbc, dtp, acp, acr, dsk):
    B, S, _ = xbc.shape
    nc = S // CHUNK

    def body(xbc_ref, dt_ref, ac_ref, ar_ref, ds_ref, y_ref, hp_ref, h_sc):
        @pl.when(pl.program_id(2) == 0)
        def _():
            h_sc[...] = jnp.zeros(h_sc.shape, F32)

        x, bm, cm, dt, ac, is_last = _ssd_load(xbc_ref, dt_ref, ac_ref, _ssd_expand(pl.program_id(1)))
        last = jnp.sum(ac * is_last, axis=0, keepdims=True)
        masks = _ssd_masks()
        decays = _ssd_decays(ac, ar_ref, masks)
        xd = x * dt
        cb = _dot(cm, bm, NT)
        hprev = h_sc[...]
        hp_ref[0, 0, 0] = hprev
        y = _dot(cm, hprev, NN) * jnp.exp(ac) + ds_ref[...] * x
        for j in range(HPG):
            y = y + _dot(cb * decays[j], xd * masks[j], NN)
        y_ref[0] = y
        h_sc[...] = hprev * jnp.exp(last) + _dot(bm, xd * jnp.exp(last - ac), TN)

    return pl.pallas_call(
        body, grid=(B, SSM_GROUPS, nc), in_specs=_ssd_in_specs(nc, False),
        out_specs=[pl.BlockSpec((1, CHUNK, GROUP_W), lambda b, g, c: (b, c, g)),
                   pl.BlockSpec((1, 1, 1, STATE_N, GROUP_W), lambda b, g, c: (b, g, c, 0, 0))],
        out_shape=[jax.ShapeDtypeStruct((B, S, D_INNER), F32),
                   jax.ShapeDtypeStruct((B, SSM_GROUPS, nc, STATE_N, GROUP_W), F32)],
        scratch_shapes=[pltpu.VMEM((STATE_N, GROUP_W), F32)], name="ssd_fwd",
        compiler_params=_cparams(("parallel", "parallel", "arbitrary")))(xbc, dtp, acp, acr, dsk)


def _ssd_bwd_call(xbc, dtp, acp, acr, dsk, hps, dy):
    B, S, _ = xbc.shape
    nc = S // CHUNK

    def body(xbc_ref, dt_ref, ac_ref, ar_ref, ds_ref, hp_ref, dy_ref,
             dxbc_ref, ddt_ref, dac_ref, dar_ref, dds_ref, dh_sc):
        first = pl.program_id(2) == 0

        @pl.when(first)
        def _():
            dh_sc[...] = jnp.zeros(dh_sc.shape, F32)

        expand = _ssd_expand(pl.program_id(0))
        x, bm, cm, dt, ac, is_last = _ssd_load(xbc_ref, dt_ref, ac_ref, expand)
        last = jnp.sum(ac * is_last, axis=0, keepdims=True)
        g = dy_ref[0]
        hprev = hp_ref[0, 0, 0]
        dh = dh_sc[...]
        masks = _ssd_masks()
        decays = _ssd_decays(ac, ar_ref, masks)
        xd = x * dt
        cb = _dot(cm, bm, NT)
        e_c = jnp.exp(ac)
        e_end = jnp.exp(last - ac)
        e_last = jnp.exp(last)
        z = _dot(cm, hprev, NN)
        dz = g * e_c
        dac = g * z * e_c
        dc = _dot(dz, hprev, NT)
        dhprev = _dot(cm, dz, TN) + dh * e_last
        dcb = jnp.zeros((CHUNK, CHUNK), F32)
        dxd = jnp.zeros(xd.shape, F32)
        for j in range(HPG):
            gj = cb * decays[j]
            dgj = _dot(g * masks[j], xd, NT)
            dxd = dxd + _dot(gj, g, TN) * masks[j]
            dcb = dcb + dgj * decays[j]
            dseg = dgj * gj
            dac = dac + jnp.sum(dseg, axis=1, keepdims=True) * masks[j] * (1.0 / HEAD_P)
            dar_ref[0, j] = -jnp.sum(dseg, axis=0, keepdims=True)
        dc = dc + _dot(dcb, bm, NN)
        db = _dot(dcb, cm, TN)
        sx = xd * e_end
        db = db + _dot(sx, dh, NT)
        dsx = _dot(bm, dh, NN)
        dxd = dxd + dsx * e_end
        de = dsx * sx
        dac = dac - de
        dlast = jnp.sum(de, axis=0, keepdims=True) + jnp.sum(dh * hprev, axis=0, keepdims=True) * e_last
        dsk = ds_ref[...]
        dxbc_ref[0] = jnp.concatenate([dxd * dt + dsk * g, db, dc], axis=1)
        dac = dac + is_last * dlast
        ddt_ref[0, 0] = lax.dot_general(dxd * x, expand, ((NT), ((), ())), preferred_element_type=F32,
                                        precision=lax.Precision.HIGHEST)
        dac_ref[0, 0] = lax.dot_general(dac, expand, ((NT), ((), ())), preferred_element_type=F32,
                                        precision=lax.Precision.HIGHEST)
        dds = jnp.sum(g * x, axis=0, keepdims=True)
        first_all = first & (pl.program_id(1) == 0)

        @pl.when(first_all)
        def _():
            dds_ref[...] = dds

        @pl.when(jnp.logical_not(first_all))
        def _():
            dds_ref[...] += dds

        dh_sc[...] = dhprev

    rc = lambda c: nc - 1 - c
    in_specs = [pl.BlockSpec(s.block_shape, (lambda g, b, c, f=s.index_map: f(b, g, c))) for s in _ssd_in_specs(nc, True)]
    in_specs.append(pl.BlockSpec((1, 1, 1, STATE_N, GROUP_W), lambda g, b, c: (b, g, rc(c), 0, 0)))
    in_specs.append(pl.BlockSpec((1, CHUNK, GROUP_W), lambda g, b, c: (b, rc(c), g)))
    per_group = pl.BlockSpec((1, 1, CHUNK, LANE), lambda g, b, c: (b, g, rc(c), 0))
    out_specs = [pl.BlockSpec((1, CHUNK, XBC_W), lambda g, b, c: (b, rc(c), g)), per_group, per_group,
                 pl.BlockSpec((1, HPG, 1, CHUNK), lambda g, b, c: (b, g, 0, rc(c))),
                 pl.BlockSpec((1, GROUP_W), lambda g, b, c: (0, g))]
    out_shape = [jax.ShapeDtypeStruct(xbc.shape, F32),
                 jax.ShapeDtypeStruct((B, SSM_GROUPS, S, LANE), F32), jax.ShapeDtypeStruct((B, SSM_GROUPS, S, LANE), F32),
                 jax.ShapeDtypeStruct(acr.shape, F32), jax.ShapeDtypeStruct(dsk.shape, F32)]
    return pl.pallas_call(
        body, grid=(SSM_GROUPS, B, nc), in_specs=in_specs, out_specs=out_specs, out_shape=out_shape,
        scratch_shapes=[pltpu.VMEM((STATE_N, GROUP_W), F32)], name="ssd_bwd",
        compiler_params=_cparams(("arbitrary", "arbitrary", "arbitrary")))(xbc, dtp, acp, acr, dsk, hps, dy)


@jax.custom_vjp
def ssd(xbc, dtp, acp, acr, dsk):
    return _ssd_fwd_call(xbc, dtp, acp, acr, dsk)[0]


def _ssd_fwd(xbc, dtp, acp, acr, dsk):
    y, hps = _ssd_fwd_call(xbc, dtp, acp, acr, dsk)
    return y, (xbc, dtp, acp, acr, dsk, hps)


def _ssd_bwd(res, dy):
    dxbc, ddt, dac, dacr, dds = _ssd_bwd_call(*res, dy)
    return dxbc, jnp.sum(ddt, axis=1), jnp.sum(dac, axis=1), dacr, dds


ssd.defvjp(_ssd_fwd, _ssd_bwd)


def _pack_small(arrs):
    flat = jnp.concatenate([a.reshape(-1) for a in arrs])
    rows = -(-flat.shape[0] // (8 * LANE)) * 8
    return jnp.pad(flat, (0, rows * LANE - flat.shape[0])).reshape(rows, LANE)


def _unpack_small(buf, shapes):
    flat = buf.reshape(-1)
    out, off = [], 0
    for shp in shapes:
        n = int(np.prod(shp))
        out.append(flat[off:off + n].reshape(shp))
        off += n
    return out


def _rows_tile(rows, cap):
    for cand in range(min(rows, cap), 7, -8):
        if rows % cand == 0:
            return cand
    return rows


def _pair_sum(mine, theirs, cidx, name):
    n4, kk, nn = mine.shape
    half = kk // 2
    tr = _rows_tile(half, 256)
    nb = half // tr

    def body(c_ref, a_ref, b_ref, o_ref):
        o_ref[...] = a_ref[...] + b_ref[...]

    spec = pl.BlockSpec((1, tr, nn), lambda j, i, c: (j, i, 0))
    grid_spec = pltpu.PrefetchScalarGridSpec(
        num_scalar_prefetch=1, grid=(n4, nb),
        in_specs=[pl.BlockSpec((1, tr, nn), lambda j, i, c: (j, c[0] * nb + i, 0)), spec], out_specs=spec)
    return pl.pallas_call(body, grid_spec=grid_spec, out_shape=jax.ShapeDtypeStruct((n4, half, nn), F32), name=name,
                          compiler_params=_cparams(("parallel", "parallel")))(cidx, mine, theirs)


def _stack_sum(stack, name):
    n, rows, nn = stack.shape
    tr = _rows_tile(rows, 256)

    def body(s_ref, o_ref):
        acc = s_ref[0]
        for d in range(1, n):
            acc = acc + s_ref[d]
        o_ref[...] = acc

    return pl.pallas_call(
        body, grid=(rows // tr,), in_specs=[pl.BlockSpec((n, tr, nn), lambda i: (0, i, 0))],
        out_specs=pl.BlockSpec((tr, nn), lambda i: (i, 0)), out_shape=jax.ShapeDtypeStruct((rows, nn), F32),
        name=name, compiler_params=_cparams(("parallel",)))(stack)


def _adam_call(w, g, m, v, name):
    rows, nn = w.shape
    tr = _rows_tile(rows, 128)

    def body(w_ref, g_ref, m_ref, v_ref, d_ref, nm_ref, nv_ref):
        d_ref[...], nm_ref[...], nv_ref[...] = _adam_fn(w_ref[...], g_ref[...], m_ref[...], v_ref[...])

    spec = pl.BlockSpec((tr, nn), lambda i: (i, 0))
    sds = jax.ShapeDtypeStruct((rows, nn), F32)
    return pl.pallas_call(body, grid=(rows // tr,), in_specs=[spec] * 4, out_specs=[spec] * 3,
                          out_shape=[sds] * 3, name=name, compiler_params=_cparams(("parallel",)))(w, g, m, v)


def _adam_fn(w, g, m, v):
    m = ADAM_B1 * m + (1.0 - ADAM_B1) * g
    v = ADAM_B2 * v + (1.0 - ADAM_B2) * (g * g)
    m_hat = m / (1.0 - ADAM_B1 ** ADAM_STEP)
    v_hat = v / (1.0 - ADAM_B2 ** ADAM_STEP)
    delta = -ADAM_LR * (m_hat / (jnp.sqrt(v_hat) + ADAM_EPS) + ADAM_WD * w)
    return delta, m, v


def _mesh_pos():
    return lax.axis_index("x"), lax.axis_index("y"), lax.axis_index("c")


def _other_chips(x, y):
    return [(1 - x, y), (x, 1 - y), (1 - x, 1 - y)]


HBM_SPEC = pl.BlockSpec(memory_space=pl.ANY)


def _remote(src, dst, send_sems, recv_sems, k, to):
    return pltpu.make_async_remote_copy(src_ref=src, dst_ref=dst, send_sem=send_sems.at[k], recv_sem=recv_sems.at[k],
                                        device_id=to, device_id_type=MESH)


def _half_rows(c, rows, align):
    half = rows // 2
    return (pl.ds(pl.multiple_of(c * half, align), half), pl.ds(pl.multiple_of((1 - c) * half, align), half))


def _gather_weights(mats, conv):
    n = len(mats)

    def body(*refs):
        ins, conv_in = refs[:n], refs[n]
        outs, conv_out = refs[n + 1:2 * n + 1], refs[2 * n + 1]
        send_sems, recv_sems, local_sems = refs[2 * n + 2:]
        x, y, c = _mesh_pos()
        me, sibling, s = (x, y, c), (x, y, 1 - c), 2 * x + y
        chips = _other_chips(x, y)
        rows = [_half_rows(c, m.shape[0], 16) for m in mats]
        own = [pltpu.make_async_copy(ins[i], outs[i].at[s], local_sems.at[i]) for i in range(n)]
        own.append(pltpu.make_async_copy(conv_in, conv_out.at[s], local_sems.at[n]))
        for cp in own:
            cp.start()
        sent = []
        for i in range(n):
            mine = rows[i][0]
            for j, (cx, cy) in enumerate(chips):
                sent.append(_remote(ins[i].at[mine], outs[i].at[s, mine], send_sems, recv_sems, 6 * i + j, (cx, cy, c)))
        for j, (cx, cy) in enumerate(chips):
            sent.append(_remote(conv_in, conv_out.at[s], send_sems, recv_sems, 6 * n + j, (cx, cy, c)))
        for cp in sent:
            cp.start()
        for i in range(n):
            mine = rows[i][0]
            for j, (cx, cy) in enumerate(chips):
                landed = outs[i].at[2 * cx + cy, mine]
                _remote(landed, landed, send_sems, recv_sems, 6 * i + j, me).wait_recv()
                fwd = _remote(landed, landed, send_sems, recv_sems, 6 * i + 3 + j, sibling)
                fwd.start()
                sent.append(fwd)
        for j, (cx, cy) in enumerate(chips):
            slot = conv_out.at[2 * cx + cy]
            _remote(slot, slot, send_sems, recv_sems, 6 * n + j, me).wait_recv()
        for i in range(n):
            theirs_rows = rows[i][1]
            for j, (cx, cy) in enumerate(chips):
                theirs = outs[i].at[2 * cx + cy, theirs_rows]
                _remote(theirs, theirs, send_sems, recv_sems, 6 * i + 3 + j, me).wait_recv()
        for cp in sent:
            cp.wait_send()
        for cp in own:
            cp.wait()

    out_shape = [jax.ShapeDtypeStruct((4,) + m.shape, m.dtype) for m in mats]
    out_shape.append(jax.ShapeDtypeStruct((4,) + conv.shape, conv.dtype))
    res = pl.pallas_call(
        body, in_specs=[HBM_SPEC] * (n + 1), out_specs=[HBM_SPEC] * (n + 1), out_shape=out_shape,
        scratch_shapes=[pltpu.SemaphoreType.DMA((6 * n + 3,)), pltpu.SemaphoreType.DMA((6 * n + 3,)),
                        pltpu.SemaphoreType.DMA((n + 1,))],
        name="all_gather_weights")(*mats, conv)
    return res[:n], res[n]


def _sibling_exchange(stacks):
    n = len(stacks)

    def body(*refs):
        ins, outs = refs[:n], refs[n:2 * n]
        send_sems, recv_sems = refs[2 * n:]
        x, y, c = _mesh_pos()
        cps = []
        for i in range(n):
            theirs = _half_rows(c, stacks[i].shape[1], 8)[1]
            cps.append(_remote(ins[i].at[:, theirs, :], outs[i], send_sems, recv_sems, i, (x, y, 1 - c)))
        for cp in cps:
            cp.start()
        for cp in cps:
            cp.wait()

    out_shape = [jax.ShapeDtypeStruct((4, s.shape[1] // 2, s.shape[2]), s.dtype) for s in stacks]
    return pl.pallas_call(
        body, in_specs=[HBM_SPEC] * n, out_specs=[HBM_SPEC] * n, out_shape=out_shape,
        scratch_shapes=[pltpu.SemaphoreType.DMA((n,)), pltpu.SemaphoreType.DMA((n,))],
        name="grad_sibling_exchange")(*stacks)


def _chip_exchange(parts):
    n = len(parts)

    def body(*refs):
        ins, outs = refs[:n], refs[n:2 * n]
        send_sems, recv_sems, local_sems = refs[2 * n:]
        x, y, c = _mesh_pos()
        me, s = (x, y, c), 2 * x + y
        chips = _other_chips(x, y)
        own = [pltpu.make_async_copy(ins[i].at[s], outs[i].at[s], local_sems.at[i]) for i in range(n)]
        for cp in own:
            cp.start()
        sent = [_remote(ins[i].at[2 * cx + cy], outs[i].at[s], send_sems, recv_sems, 3 * i + j, (cx, cy, c))
                for i in range(n) for j, (cx, cy) in enumerate(chips)]
        for cp in sent:
            cp.start()
        for i in range(n):
            for j, (cx, cy) in enumerate(chips):
                slot = outs[i].at[2 * cx + cy]
                _remote(slot, slot, send_sems, recv_sems, 3 * i + j, me).wait_recv()
        for cp in sent:
            cp.wait_send()
        for cp in own:
            cp.wait()

    return pl.pallas_call(
        body, in_specs=[HBM_SPEC] * n, out_specs=[HBM_SPEC] * n,
        out_shape=[jax.ShapeDtypeStruct(p.shape, p.dtype) for p in parts],
        scratch_shapes=[pltpu.SemaphoreType.DMA((3 * n,)), pltpu.SemaphoreType.DMA((3 * n,)),
                        pltpu.SemaphoreType.DMA((n,))],
        name="grad_chip_exchange")(*parts)


def _sibling_concat(halves):
    n = len(halves)

    def body(*refs):
        ins, outs = refs[:n], refs[n:2 * n]
        send_sems, recv_sems, local_sems = refs[2 * n:]
        x, y, c = _mesh_pos()
        own, sent = [], []
        for i in range(n):
            mine, theirs = _half_rows(c, 2 * halves[i].shape[0], 8)
            own.append(pltpu.make_async_copy(ins[i], outs[i].at[mine], local_sems.at[i]))
            sent.append(_remote(ins[i], outs[i].at[mine], send_sems, recv_sems, i, (x, y, 1 - c)))
        for cp in own + sent:
            cp.start()
        for i in range(n):
            theirs = outs[i].at[_half_rows(c, 2 * halves[i].shape[0], 8)[1]]
            _remote(theirs, theirs, send_sems, recv_sems, i, (x, y, c)).wait_recv()
        for cp in sent:
            cp.wait_send()
        for cp in own:
            cp.wait()

    return pl.pallas_call(
        body, in_specs=[HBM_SPEC] * n, out_specs=[HBM_SPEC] * n,
        out_shape=[jax.ShapeDtypeStruct((2 * h.shape[0], h.shape[1]), h.dtype) for h in halves],
        scratch_shapes=[pltpu.SemaphoreType.DMA((n,)), pltpu.SemaphoreType.DMA((n,)), pltpu.SemaphoreType.DMA((n,))],
        name="grad_sibling_concat")(*halves)


def _gather_small(vec):
    def body(in_ref, out_ref, send_sems, recv_sems, local_sem):
        x, y, c = _mesh_pos()
        me = (x, y, c)
        own = pltpu.make_async_copy(in_ref, out_ref.at[4 * x + 2 * y + c], local_sem)
        own.start()
        peers = [(1 - x if k & 4 else x, 1 - y if k & 2 else y, 1 - c if k & 1 else c) for k in range(1, 8)]
        sent = [_remote(in_ref, out_ref.at[4 * x + 2 * y + c], send_sems, recv_sems, k, p) for k, p in enumerate(peers)]
        for cp in sent:
            cp.start()
        for k, (px, py, pc) in enumerate(peers):
            slot = out_ref.at[4 * px + 2 * py + pc]
            _remote(slot, slot, send_sems, recv_sems, k, me).wait_recv()
        for cp in sent:
            cp.wait_send()
        own.wait()

    return pl.pallas_call(
        body, in_specs=[HBM_SPEC], out_specs=HBM_SPEC, out_shape=jax.ShapeDtypeStruct((8,) + vec.shape, vec.dtype),
        scratch_shapes=[pltpu.SemaphoreType.DMA((7,)), pltpu.SemaphoreType.DMA((7,)), pltpu.SemaphoreType.DMA],
        name="grad_gather_small")(vec)


def _reduce_matrices(stacks, names):
    cidx = lax.axis_index("c").astype(jnp.int32).reshape(1)
    got = _sibling_exchange(stacks)
    pairs = [_pair_sum(a, b, cidx, "grad_pair_sum_" + nm) for a, b, nm in zip(stacks, got, names)]
    quads = _chip_exchange(pairs)
    halves = [_stack_sum(q, "grad_chip_sum_" + nm) for q, nm in zip(quads, names)]
    return _sibling_concat(halves)


def _pad_cols(a, n):
    return jnp.concatenate([a, jnp.zeros((a.shape[0], n - a.shape[1]), a.dtype)], axis=1)


def _group_channels(a):
    lead = a.shape[:-1]
    xs = a[..., :D_INNER].reshape(lead + (SSM_GROUPS, GROUP_W))
    bs = a[..., D_INNER:D_INNER + SSM_GROUPS * STATE_N].reshape(lead + (SSM_GROUPS, STATE_N))
    cs = a[..., D_INNER + SSM_GROUPS * STATE_N:].reshape(lead + (SSM_GROUPS, STATE_N))
    return jnp.concatenate([xs, bs, cs], axis=-1).reshape(lead + (CONV_CH,))


def _lay_w_in(w):
    idx = np.cumsum(IN_SIZES)[:-1]
    segs = jnp.split(w, [int(v) for v in idx], axis=1)
    segs[2] = _pad_cols(segs[2], LANE)
    segs[4] = _group_channels(segs[4])
    segs[5] = _pad_cols(segs[5], LANE)
    return jnp.concatenate(segs, axis=1)


IN_PAD_SIZES = (Q_RANK, KV_RANK, LANE, D_INNER, CONV_CH, LANE, D_MODEL, D_MODEL)
IN_PAD_OFFS = [int(v) for v in np.cumsum(IN_PAD_SIZES)[:-1]]


@jax.custom_vjp
def split_proj(proj):
    return tuple(jnp.split(proj, IN_PAD_OFFS, axis=-1))


split_proj.defvjp(lambda proj: (tuple(jnp.split(proj, IN_PAD_OFFS, axis=-1)), None),
                  lambda _, cots: (jnp.concatenate(cots, axis=-1),))


def _lay_w_uq(w):
    w3 = w.reshape(Q_RANK, N_HEADS, NOPE + ROPE)
    w3 = jnp.concatenate([w3, jnp.zeros((Q_RANK, N_HEADS, QK_PAD - NOPE - ROPE), w.dtype)], axis=2)
    return w3.reshape(Q_RANK, N_HEADS * QK_PAD)


def _lay_w_ukv(w):
    w3 = w.reshape(KV_RANK, N_HEADS, NOPE + V_DIM)
    return jnp.concatenate([w3[:, :, :NOPE].reshape(KV_RANK, -1), w3[:, :, NOPE:].reshape(KV_RANK, -1)], axis=1)


def _pad_lanes(v, n=LANE):
    return jnp.concatenate([v, jnp.zeros((v.shape[0], n - v.shape[1]), v.dtype)], axis=1)


def _local_loss(toks, small, x, wb, c8, posf, target):
    B, S, D = x.shape
    T = B * S

    def lin(name, a, key, lay=lambda w: w, out_dtype=F32):
        return make_linear(name, out_dtype)(a, lay(wb[key]), lay(toks[key]))

    rows2 = lambda a: a.reshape(T, a.shape[-1])
    rows3 = lambda a: a.reshape(B, S, a.shape[-1])

    sc = make_rowwise("silu_c", _f_silu, 1, 0, 0, ('row',))((c8[None],), (), ())[0][0]
    mod = lin("ada", sc, 'w_ada')[:B] + small['b_ada']
    shift1, scale1, gate1, shift2, scale2, gate2 = [m[:, None, :] for m in jnp.split(mod, 6, axis=-1)]

    modulate = make_rowwise("modulate1", _f_modulate, 1, 2, 1, ('row',))
    h = modulate((x,), (scale1, shift1), (small['g_pre_mix'],))[0]
    proj = rows3(lin("w_in", rows2(h), 'w_in', _lay_w_in))
    q_lat, kv_lat, k_rope, z, xbc, dt_raw, gate_a, gate_b = split_proj(proj)

    inv = ROPE_THETA ** (-jnp.arange(ROPE // 2, dtype=F32) / (ROPE // 2))
    inv_lane = jnp.concatenate([inv, inv, jnp.zeros((LANE - ROPE,), F32)])[None]
    tabs = tuple(_rope_tables(posf, inv_lane))
    qn = make_rowwise("rms_q", _f_rms, 1, 0, 1, ('row',))((q_lat,), (), (small['g_q_lat'],))[0]
    kvn = make_rowwise("rms_kv", _f_rms, 1, 0, 1, ('row',))((kv_lat,), (), (small['g_kv_lat'],))[0]
    qp = rows3(lin("w_uq", rows2(qn), 'w_uq', _lay_w_uq))
    kvp = rows3(lin("w_ukv", rows2(kvn), 'w_ukv', _lay_w_ukv, BF16))
    qr = rope_q(qp, tabs)
    kr = build_k(kvp, k_rope, tabs)
    att = attention(qr, kr, kvp)
    attn = rows3(lin("w_o_attn", rows2(att), 'w_o_attn'))

    xa = conv_silu(xbc, _group_channels(wb['conv_w_f32']), _group_channels(small['conv_b']))
    dt_pad, a_pad = make_rowwise("dt_softplus", _f_dt, 1, 0, 2, ('row', 'row'))(
        (dt_raw,), (), (_pad_lanes(small['dt_bias']), _pad_lanes(small['a_log'])))
    ac_pad = chunk_cumsum(a_pad)
    acr = jnp.transpose(ac_pad[..., :SSM_HEADS], (0, 2, 1))[:, :, None, :]
    dsk = jnp.repeat(small['d_skip'], HEAD_P, axis=-1)
    y = ssd(xa, dt_pad, ac_pad, acr, dsk)
    yg = make_rowwise("gated_norm", _f_gated_norm, 2, 0, 1, ('row',), ncol=SSM_GROUPS)(
        (y, z), (), (small['g_ssm_out'],))[0]
    ssm = rows3(lin("w_o_ssm", rows2(yg), 'w_o_ssm'))

    merged = make_rowwise("merge", _f_merge, 4, 0, 0, ('row',))((attn, ssm, gate_a, gate_b), (), ())[0]
    mix = rows3(lin("w_out", rows2(merged), 'w_out'))
    x1 = make_rowwise("post_mix", _f_post, 2, 1, 1, ('row',))((x, mix), (gate1,), (small['g_post_mix'],))[0]

    h2 = make_rowwise("modulate2", _f_modulate, 1, 2, 1, ('row',))((x1,), (scale2, shift2), (small['g_pre_mlp'],))[0]
    u = rows3(lin("w_ff1", rows2(h2), 'w_ff1'))
    act = make_rowwise("relu2", _f_relu2, 1, 0, 0, ('row',))((u,), (), ())[0]
    ff = rows3(lin("w_ff2", rows2(act), 'w_ff2'))
    lvec = make_rowwise("final_loss", _f_final_loss, 3, 1, 1, ('sum',), nodiff=(2,))(
        (x1, ff, target), (gate2,), (small['g_post_mlp'],))[0]
    return jnp.sum(lvec)


MATRICES = COL_SHARDED + ROW_SHARDED


def _local_step(x, c, positions, target, wb, small):
    B = x.shape[0]
    c8 = jnp.concatenate([c, jnp.zeros((16 - B, c.shape[1]), F32)], axis=0)
    posf = positions.astype(F32)[..., None]
    toks = {k: jnp.zeros(wb[k].shape, F32) for k in MATRICES if k != 'conv_w'}
    conv_w = wb['conv_w_f32']

    def loss_fn(toks, small, conv_w, x):
        wbl = dict(wb)
        wbl['conv_w_f32'] = conv_w
        return _local_loss(toks, small, x, wbl, c8, posf, target)

    loss, (g_tok, g_small, g_conv, g_x) = jax.value_and_grad(loss_fn, argnums=(0, 1, 2, 3))(toks, small, conv_w, x)
    grads = dict(g_tok)
    grads.update(g_small)
    grads['conv_w'] = g_conv
    return loss, g_x, grads


def kernel(x, c, positions, w_ada, b_ada, g_pre_mix, g_post_mix, w_in, g_q_lat, g_kv_lat, w_uq, w_ukv, w_o_attn, conv_w, conv_b, dt_bias, a_log, d_skip, g_ssm_out, w_o_ssm, w_out, g_pre_mlp, g_post_mlp, w_ff1, w_ff2, loss_target, m_w_ada, m_b_ada, m_g_pre_mix, m_g_post_mix, m_w_in, m_g_q_lat, m_g_kv_lat, m_w_uq, m_w_ukv, m_w_o_attn, m_conv_w, m_conv_b, m_dt_bias, m_a_log, m_d_skip, m_g_ssm_out, m_w_o_ssm, m_w_out, m_g_pre_mlp, m_g_post_mlp, m_w_ff1, m_w_ff2, v_w_ada, v_b_ada, v_g_pre_mix, v_g_post_mix, v_w_in, v_g_q_lat, v_g_kv_lat, v_w_uq, v_w_ukv, v_w_o_attn, v_conv_w, v_conv_b, v_dt_bias, v_a_log, v_d_skip, v_g_ssm_out, v_w_o_ssm, v_w_out, v_g_pre_mlp, v_g_post_mlp, v_w_ff1, v_w_ff2):
    given = dict(locals())
    w_loc = {n: given[n] for n in WEIGHTS}
    m_loc = {n: given["m_" + n] for n in WEIGHTS}
    v_loc = {n: given["v_" + n] for n in WEIGHTS}
    mats = [n for n in WEIGHTS if n in MATRICES and n != 'conv_w']
    vecs = [n for n in WEIGHTS if n not in MATRICES]

    g_mats, g_conv = _gather_weights([w_loc[n][0].astype(BF16) for n in mats], conv_w[0])
    wb = {}
    for n, g in zip(mats, g_mats):
        if n in COL_SHARDED:
            wb[n] = jnp.transpose(g, (1, 0, 2)).reshape(g.shape[1], -1)
        else:
            wb[n] = g.reshape(-1, g.shape[2])
    wb['conv_w_f32'] = jnp.transpose(g_conv, (1, 0, 2)).reshape(CONV_K, -1)
    small = {n: w_loc[n] for n in vecs}

    loss_part, grad_x, grads = _local_step(x, c, positions, loss_target, wb, small)
    loss = lax.psum(loss_part, ("x", "y", "c"))

    stacks = []
    for n in mats:
        kk, nn = w_loc[n].shape[1:]
        if n in COL_SHARDED:
            stacks.append(jnp.transpose(grads[n].reshape(kk, 4, nn), (1, 0, 2)))
        else:
            stacks.append(grads[n].reshape(4, kk, nn))
    g_shard = dict(zip(mats, _reduce_matrices(stacks, mats)))

    vec_shapes = [tuple(grads[n].shape) for n in vecs] + [tuple(grads['conv_w'].shape)]
    total = _stack_sum(_gather_small(_pack_small([grads[n] for n in vecs] + [grads['conv_w']])), "grad_sum_small")
    g_vec = _unpack_small(total, vec_shapes)
    n_conv = conv_w.shape[2]
    chip = 2 * lax.axis_index("x") + lax.axis_index("y")
    g_shard['conv_w'] = lax.dynamic_slice_in_dim(g_vec[-1], chip * n_conv, n_conv, axis=1)
    for n, g in zip(vecs, g_vec):
        g_shard[n] = g

    delta, new_m, new_v = {}, {}, {}
    for n in mats:
        delta[n], new_m[n], new_v[n] = _adam_call(w_loc[n][0], g_shard[n], m_loc[n][0], v_loc[n][0], "adamw_" + n)
    rest = vecs + ['conv_w']
    rest_shapes = [tuple(w_loc[n].shape) for n in rest]
    packed = [_pack_small([src[n] for n in rest]) for src in (w_loc, g_shard, m_loc, v_loc)]
    for dst, buf in zip((delta, new_m, new_v), _adam_call(*packed, "adamw_small")):
        dst.update(zip(rest, _unpack_small(buf, rest_shapes)))

    def out(d):
        return [d[n].reshape(w_loc[n].shape) for n in WEIGHTS]

    return (loss, grad_x, *out(g_shard), *out(delta), *out(new_m), *out(new_v))
```

```python
import functools
import math

import numpy as np
import jax
import jax.numpy as jnp
from jax import lax
from jax.experimental import pallas as pl
from jax.experimental.pallas import tpu as pltpu

F32 = jnp.float32
BF16 = jnp.bfloat16
MESH = pl.DeviceIdType.MESH

D_MODEL = 1024
N_HEADS = 8
NOPE = 128
ROPE = 64
V_DIM = 128
Q_RANK = 256
KV_RANK = 256
ROPE_THETA = 10000.0
D_INNER = 2048
SSM_HEADS = 32
SSM_GROUPS = 8
HEAD_P = 64
STATE_N = 128
CONV_K = 4
CHUNK = 128
CONV_CH = D_INNER + 2 * SSM_GROUPS * STATE_N
D_FF = 4096
EPS = 1e-6
IN_SIZES = (Q_RANK, KV_RANK, ROPE, D_INNER, CONV_CH, SSM_HEADS, D_MODEL, D_MODEL)
ADAM_LR, ADAM_B1, ADAM_B2, ADAM_EPS, ADAM_WD, ADAM_STEP = 0.001, 0.9, 0.999, 1e-08, 0.01, 10

VMEM_LIMIT_BYTES = 52 * 1024 * 1024
LANE = 128
QK_PAD = 256

WEIGHTS = ['w_ada', 'b_ada', 'g_pre_mix', 'g_post_mix', 'w_in', 'g_q_lat', 'g_kv_lat', 'w_uq', 'w_ukv',
           'w_o_attn', 'conv_w', 'conv_b', 'dt_bias', 'a_log', 'd_skip', 'g_ssm_out', 'w_o_ssm', 'w_out',
           'g_pre_mlp', 'g_post_mlp', 'w_ff1', 'w_ff2']
COL_SHARDED = ('w_ada', 'w_in', 'w_uq', 'w_ukv', 'conv_w', 'w_ff1')
ROW_SHARDED = ('w_o_attn', 'w_o_ssm', 'w_out', 'w_ff2')


def _cparams(sem):
    return pltpu.CompilerParams(dimension_semantics=sem, vmem_limit_bytes=VMEM_LIMIT_BYTES)


def _tile(n, cap):
    if n <= cap:
        return n
    k = n // LANE
    best = LANE
    for d in range(1, k + 1):
        if k % d == 0 and d * LANE <= cap:
            best = d * LANE
    return best


def _mm(a, w, name, out_dtype=F32):
    M, K = a.shape
    N = w.shape[1]
    tm = min(M, 512)
    tn = _tile(N, 1024)
    tk = _tile(K, 2048)
    nk = K // tk
    assert nk == 1 or out_dtype == F32

    def body(a_ref, w_ref, o_ref):
        part = jnp.dot(a_ref[...].astype(BF16), w_ref[...], preferred_element_type=F32)
        if nk == 1:
            o_ref[...] = part.astype(out_dtype)
        else:
            k = pl.program_id(2)

            @pl.when(k == 0)
            def _():
                o_ref[...] = part

            @pl.when(k > 0)
            def _():
                o_ref[...] += part

    return pl.pallas_call(
        body, grid=(M // tm, N // tn, nk),
        in_specs=[pl.BlockSpec((tm, tk), lambda i, j, k: (i, k)), pl.BlockSpec((tk, tn), lambda i, j, k: (k, j))],
        out_specs=pl.BlockSpec((tm, tn), lambda i, j, k: (i, j)),
        out_shape=jax.ShapeDtypeStruct((M, N), out_dtype), name=name,
        compiler_params=_cparams(("parallel", "parallel", "arbitrary")))(a, w)


def _mm_tn(a, g, name):
    M, K = a.shape
    N = g.shape[1]
    tm = min(M, 512)
    tk = _tile(K, 1024)
    tn = _tile(N, 1024)
    nm = M // tm

    def body(a_ref, g_ref, o_ref):
        part = lax.dot_general(a_ref[...].astype(BF16), g_ref[...].astype(BF16), (((0,), (0,)), ((), ())),
                               preferred_element_type=F32)
        m = pl.program_id(2)

        @pl.when(m == 0)
        def _():
            o_ref[...] = part

        @pl.when(m > 0)
        def _():
            o_ref[...] += part

    return pl.pallas_call(
        body, grid=(K // tk, N // tn, nm),
        in_specs=[pl.BlockSpec((tm, tk), lambda i, j, m: (m, i)), pl.BlockSpec((tm, tn), lambda i, j, m: (m, j))],
        out_specs=pl.BlockSpec((tk, tn), lambda i, j, m: (i, j)),
        out_shape=jax.ShapeDtypeStruct((K, N), F32), name=name,
        compiler_params=_cparams(("parallel", "parallel", "arbitrary")))(a, g)


def make_linear(name, out_dtype=F32):
    @jax.custom_vjp
    def linear(a, w, tok):
        return _mm(a, w, name + "_fwd", out_dtype)

    def fwd(a, w, tok):
        return _mm(a, w, name + "_fwd", out_dtype), (a, w)

    def bwd(res, g):
        a, w = res
        da = _mm(g, w.T, name + "_dx", a.dtype)
        dw = _mm_tn(a, g, name + "_dw")
        return da, jnp.zeros_like(w), dw

    linear.defvjp(fwd, bwd)
    return linear


def make_rowwise(name, f, n_rows, n_seqs, n_pars, out_kinds, ncol=1, nodiff=(), ts_cap=512):
    n_in = n_rows + n_seqs + n_pars
    diff_idx = [i for i in range(n_in) if i not in nodiff]

    def _dims(rows):
        B, S = rows[0].shape[0], rows[0].shape[1]
        ts = min(S, ts_cap)
        return B, S, ts

    def _in_specs(rows, seqs, pars, ts):
        specs = []
        for r in rows:
            specs.append(pl.BlockSpec((1, ts, r.shape[2] // ncol), lambda k, b, s: (b, s, k)))
        for q in seqs:
            specs.append(pl.BlockSpec((1, 1, q.shape[2] // ncol), lambda k, b, s: (b, 0, k)))
        for p in pars:
            specs.append(pl.BlockSpec((1, p.shape[1] // ncol), lambda k, b, s: (0, k)))
        return specs

    def _load(refs):
        vals = [r[0] for r in refs[:n_rows + n_seqs]]
        vals += [r[...] for r in refs[n_rows + n_seqs:n_in]]
        return vals

    def _out_struct(rows, seqs, pars, ts):
        blocks = [jax.ShapeDtypeStruct((ts, r.shape[2] // ncol), r.dtype) for r in rows]
        blocks += [jax.ShapeDtypeStruct((1, q.shape[2] // ncol), q.dtype) for q in seqs]
        blocks += [jax.ShapeDtypeStruct((1, p.shape[1] // ncol), p.dtype) for p in pars]
        return jax.eval_shape(f, *blocks)

    def _fwd_call(rows, seqs, pars):
        B, S, ts = _dims(rows)
        outs = _out_struct(rows, seqs, pars, ts)
        n_out = len(outs)

        def body(*refs):
            res = f(*_load(refs))
            first = (pl.program_id(1) == 0) & (pl.program_id(2) == 0)
            for o_ref, val, kind in zip(refs[n_in:], res, out_kinds):
                if kind == 'row':
                    o_ref[0] = val
                else:
                    tot = jnp.sum(val, axis=0, keepdims=True)

                    @pl.when(first)
                    def _(o_ref=o_ref, tot=tot):
                        o_ref[...] = tot

                    @pl.when(jnp.logical_not(first))
                    def _(o_ref=o_ref, tot=tot):
                        o_ref[...] += tot

        out_shape, out_specs = [], []
        for o, kind in zip(outs, out_kinds):
            d = o.shape[1]
            if kind == 'row':
                out_shape.append(jax.ShapeDtypeStruct((B, S, ncol * d), o.dtype))
                out_specs.append(pl.BlockSpec((1, ts, d), lambda k, b, s: (b, s, k)))
            else:
                out_shape.append(jax.ShapeDtypeStruct((1, ncol * d), o.dtype))
                out_specs.append(pl.BlockSpec((1, d), lambda k, b, s: (0, k)))
        res = pl.pallas_call(
            body, grid=(ncol, B, S // ts), in_specs=_in_specs(rows, seqs, pars, ts), out_specs=out_specs,
            out_shape=out_shape, name=name + "_fwd",
            compiler_params=_cparams(("arbitrary", "arbitrary", "arbitrary")))(*rows, *seqs, *pars)
        return tuple(res)

    def _bwd_call(rows, seqs, pars, cots):
        B, S, ts = _dims(rows)
        outs = _out_struct(rows, seqs, pars, ts)
        n_out = len(outs)
        all_in = list(rows) + list(seqs) + list(pars)

        def body(*refs):
            vals = _load(refs)
            cts = []
            for c_ref, o, kind in zip(refs[n_in:n_in + n_out], outs, out_kinds):
                if kind == 'row':
                    cts.append(c_ref[0])
                else:
                    cts.append(jnp.broadcast_to(c_ref[...], o.shape))

            def g(*dv):
                full = list(vals)
                for i, v in zip(diff_idx, dv):
                    full[i] = v
                return tuple(f(*full))

            _, vjp = jax.vjp(g, *[vals[i] for i in diff_idx])
            grads = vjp(tuple(cts))
            b, s = pl.program_id(1), pl.program_id(2)
            for o_ref, i, gr in zip(refs[n_in + n_out:], diff_idx, grads):
                if i < n_rows:
                    o_ref[0] = gr
                else:
                    first = (s == 0) if i < n_rows + n_seqs else ((b == 0) & (s == 0))
                    target = (lambda r: r.at[0]) if i < n_rows + n_seqs else (lambda r: r)

                    @pl.when(first)
                    def _(o_ref=o_ref, gr=gr, target=target):
                        target(o_ref)[...] = gr

                    @pl.when(jnp.logical_not(first))
                    def _(o_ref=o_ref, gr=gr, target=target):
                        target(o_ref)[...] += gr

        cot_specs = []
        for o, kind in zip(outs, out_kinds):
            d = o.shape[1]
            if kind == 'row':
                cot_specs.append(pl.BlockSpec((1, ts, d), lambda k, b, s: (b, s, k)))
            else:
                cot_specs.append(pl.BlockSpec((1, d), lambda k, b, s: (0, k)))
        out_shape, out_specs = [], []
        for i in diff_idx:
            a = all_in[i]
            out_shape.append(jax.ShapeDtypeStruct(a.shape, a.dtype))
            if i < n_rows:
                out_specs.append(pl.BlockSpec((1, ts, a.shape[2] // ncol), lambda k, b, s: (b, s, k)))
            elif i < n_rows + n_seqs:
                out_specs.append(pl.BlockSpec((1, 1, a.shape[2] // ncol), lambda k, b, s: (b, 0, k)))
            else:
                out_specs.append(pl.BlockSpec((1, a.shape[1] // ncol), lambda k, b, s: (0, k)))
        res = pl.pallas_call(
            body, grid=(ncol, B, S // ts), in_specs=_in_specs(rows, seqs, pars, ts) + cot_specs,
            out_specs=out_specs, out_shape=out_shape, name=name + "_bwd",
            compiler_params=_cparams(("arbitrary", "arbitrary", "arbitrary")))(*all_in, *cots)
        grads = [None] * n_in
        for i, r in zip(diff_idx, res):
            grads[i] = r
        for i in nodiff:
            grads[i] = jnp.zeros_like(all_in[i])
        return tuple(grads[:n_rows]), tuple(grads[n_rows:n_rows + n_seqs]), tuple(grads[n_rows + n_seqs:])

    @jax.custom_vjp
    def op(rows, seqs, pars):
        return _fwd_call(rows, seqs, pars)

    def fwd(rows, seqs, pars):
        return _fwd_call(rows, seqs, pars), (rows, seqs, pars)

    def bwd(res, cots):
        rows, seqs, pars = res
        return _bwd_call(rows, seqs, pars, cots)

    op.defvjp(fwd, bwd)
    return op


def _rms(x, g):
    return x * lax.rsqrt(jnp.mean(x * x, axis=-1, keepdims=True) + EPS) * g


def _silu(x):
    return x * lax.logistic(x)


def _f_silu(c):
    return (_silu(c),)


def _f_modulate(x, scale, shift, g):
    return (_rms(x, g) * (1.0 + scale) + shift,)


def _f_rms(x, g):
    return (_rms(x, g),)


def _f_dt(dt_raw, dt_bias, a_log):
    z = dt_raw + dt_bias
    dt = jnp.maximum(z, 0.0) + jnp.log1p(jnp.exp(-jnp.abs(z)))
    return dt, dt * (-jnp.exp(a_log))


def _f_gated_norm(y, z, g):
    return (_rms(y * _silu(z), g),)


def _f_merge(attn, ssm, ga, gb):
    return (lax.logistic(ga) * attn + lax.logistic(gb) * ssm,)


def _f_post(x, m, gate, g):
    return (x + gate * _rms(m, g),)


def _f_relu2(u):
    r = jnp.maximum(u, 0.0)
    return (r * r,)


def _f_final_loss(x, ff, target, gate, g):
    e = x + gate * _rms(ff, g) - target
    return (e * e * (0.5 / D_MODEL),)


def _rope_tables(posf, inv_lane):
    B, S, _ = posf.shape
    ts = min(S, 512)

    def body(p_ref, inv_ref, c_ref, a_ref, b_ref):
        ang = p_ref[0] * inv_ref[...]
        cs, sn = jnp.cos(ang), jnp.sin(ang)
        lane = lax.broadcasted_iota(jnp.int32, ang.shape, 1)
        c_ref[0] = jnp.where(lane < ROPE, cs, 0.0)
        a_ref[0] = jnp.where(lane < ROPE // 2, -sn, 0.0)
        b_ref[0] = jnp.where((lane >= ROPE // 2) & (lane < ROPE), sn, 0.0)

    spec = pl.BlockSpec((1, ts, LANE), lambda b, s: (b, s, 0))
    sds = jax.ShapeDtypeStruct((B, S, LANE), F32)
    return pl.pallas_call(
        body, grid=(B, S // ts),
        in_specs=[pl.BlockSpec((1, ts, 1), lambda b, s: (b, s, 0)), pl.BlockSpec((1, LANE), lambda b, s: (0, 0))],
        out_specs=[spec, spec, spec], out_shape=[sds, sds, sds], name="rope_tables",
        compiler_params=_cparams(("parallel", "parallel")))(posf, inv_lane)


def _rot(u, c, a, bm):
    return u * c + pltpu.roll(u, 96, 1) * a + pltpu.roll(u, 32, 1) * bm


def _rot_t(g, c, a, bm):
    return g * c + pltpu.roll(g * a, 32, 1) + pltpu.roll(g * bm, 96, 1)


def _rope_q_call(q, tabs, transpose, name):
    B, S, W = q.shape
    ts = min(S, 512)
    fn = _rot_t if transpose else _rot
    out_dtype = F32 if transpose else BF16

    def body(q_ref, c_ref, a_ref, b_ref, o_ref):
        u = q_ref[0].astype(F32) * ATT_SCALE
        r = fn(u[:, NOPE:], c_ref[0], a_ref[0], b_ref[0])
        o_ref[0] = jnp.concatenate([u[:, :NOPE], r], axis=1).astype(out_dtype)

    tspec = pl.BlockSpec((1, ts, LANE), lambda b, s, h: (b, s, 0))
    qspec = pl.BlockSpec((1, ts, QK_PAD), lambda b, s, h: (b, s, h))
    return pl.pallas_call(
        body, grid=(B, S // ts, W // QK_PAD), in_specs=[qspec, tspec, tspec, tspec], out_specs=qspec,
        out_shape=jax.ShapeDtypeStruct(q.shape, out_dtype), name=name,
        compiler_params=_cparams(("parallel", "parallel", "parallel")))(q, *tabs)


@jax.custom_vjp
def rope_q(q, tabs):
    return _rope_q_call(q, tabs, False, "rope_q_fwd")


def _rope_q_fwd(q, tabs):
    return _rope_q_call(q, tabs, False, "rope_q_fwd"), tabs


def _rope_q_bwd(tabs, g):
    return _rope_q_call(g, tabs, True, "rope_q_bwd"), tuple(jnp.zeros_like(t) for t in tabs)


rope_q.defvjp(_rope_q_fwd, _rope_q_bwd)


def _build_k_fwd_call(kv, kr, tabs):
    B, S, _ = kv.shape
    ts = min(S, 512)

    def body(kv_ref, kr_ref, c_ref, a_ref, b_ref, o_ref):
        r = _rot(kr_ref[0], c_ref[0], a_ref[0], b_ref[0])
        o_ref[0] = jnp.concatenate([kv_ref[0], r.astype(BF16)], axis=1)

    tspec = pl.BlockSpec((1, ts, LANE), lambda b, s, h: (b, s, 0))
    return pl.pallas_call(
        body, grid=(B, S // ts, N_HEADS),
        in_specs=[pl.BlockSpec((1, ts, LANE), lambda b, s, h: (b, s, h)), tspec, tspec, tspec, tspec],
        out_specs=pl.BlockSpec((1, ts, QK_PAD), lambda b, s, h: (b, s, h)),
        out_shape=jax.ShapeDtypeStruct((B, S, N_HEADS * QK_PAD), BF16), name="build_k_fwd",
        compiler_params=_cparams(("parallel", "parallel", "arbitrary")))(kv, kr, *tabs)


def _build_k_bwd_call(g, tabs):
    B, S, _ = g.shape
    ts = min(S, 512)

    def body(g_ref, c_ref, a_ref, b_ref, dk_ref, dr_ref):
        gg = g_ref[0]
        dk_ref[0] = gg[:, :NOPE]
        r = _rot_t(gg[:, NOPE:].astype(F32), c_ref[0], a_ref[0], b_ref[0])
        h = pl.program_id(2)

        @pl.when(h == 0)
        def _():
            dr_ref[0] = r

        @pl.when(h > 0)
        def _():
            dr_ref[0] += r

    tspec = pl.BlockSpec((1, ts, LANE), lambda b, s, h: (b, s, 0))
    return pl.pallas_call(
        body, grid=(B, S // ts, N_HEADS),
        in_specs=[pl.BlockSpec((1, ts, QK_PAD), lambda b, s, h: (b, s, h)), tspec, tspec, tspec],
        out_specs=[pl.BlockSpec((1, ts, LANE), lambda b, s, h: (b, s, h)), tspec],
        out_shape=[jax.ShapeDtypeStruct((B, S, N_HEADS * NOPE), BF16), jax.ShapeDtypeStruct((B, S, LANE), F32)],
        name="build_k_bwd", compiler_params=_cparams(("parallel", "parallel", "arbitrary")))(g, *tabs)


@jax.custom_vjp
def build_k(kv, kr, tabs):
    return _build_k_fwd_call(kv, kr, tabs)


def _build_k_fwd(kv, kr, tabs):
    return _build_k_fwd_call(kv, kr, tabs), (tabs, kv.shape)


def _build_k_bwd(res, g):
    tabs, kv_shape = res
    dk, dr = _build_k_bwd_call(g, tabs)
    dkv = jnp.concatenate([dk, jnp.zeros((kv_shape[0], kv_shape[1], kv_shape[2] - dk.shape[2]), BF16)], axis=-1)
    return dkv, dr, tuple(jnp.zeros_like(t) for t in tabs)


build_k.defvjp(_build_k_fwd, _build_k_bwd)


ATT_SCALE = (NOPE + ROPE) ** -0.5
NEG = -1e30


def _att_tiles(S):
    t = min(S, 512)
    return t, S // t


def _scores(q, k, diagonal):
    s = lax.dot_general(q, k, (((1,), (1,)), ((), ())), preferred_element_type=F32)
    if diagonal:
        row = lax.broadcasted_iota(jnp.int32, s.shape, 0)
        col = lax.broadcasted_iota(jnp.int32, s.shape, 1)
        s = jnp.where(col <= row, s, NEG)
    return s


ATT_HB = 4


def _causal_pairs(n, by_key):
    if by_key:
        pairs = [(i, j) for j in range(n) for i in range(j, n)]
    else:
        pairs = [(i, j) for i in range(n) for j in range(i + 1)]
    return (jnp.asarray([p[0] for p in pairs], jnp.int32), jnp.asarray([p[1] for p in pairs], jnp.int32))


def _head(ref_or_val, h, w):
    return ref_or_val[:, h * w:(h + 1) * w]


def _attn_fwd_call(q, k, vsrc, v_blk0):
    B, S, _ = q.shape
    t, n = _att_tiles(S)
    qi, kj = _causal_pairs(n, False)

    def body(qi_ref, kj_ref, q_ref, k_ref, v_ref, o_ref, lse_ref, m_sc, l_sc, acc_sc):
        p_id = pl.program_id(2)
        i, j = qi_ref[p_id], kj_ref[p_id]

        @pl.when(j == 0)
        def _():
            m_sc[...] = jnp.full(m_sc.shape, NEG, F32)
            l_sc[...] = jnp.zeros(l_sc.shape, F32)
            acc_sc[...] = jnp.zeros(acc_sc.shape, F32)

        def step(diagonal):
            qa, ka, va = q_ref[0], k_ref[0], v_ref[0]
            for h in range(ATT_HB):
                lanes = slice(h * LANE, (h + 1) * LANE)
                s = _scores(_head(qa, h, QK_PAD), _head(ka, h, QK_PAD), diagonal)
                m_prev = m_sc[:, lanes]
                m_new = jnp.maximum(m_prev, jnp.max(s, axis=1, keepdims=True))
                alpha = jnp.exp(m_prev - m_new)
                p = jnp.exp(s - jnp.tile(m_new, (1, t // LANE)))
                l_sc[:, lanes] = alpha * l_sc[:, lanes] + jnp.sum(p, axis=1, keepdims=True)
                acc_sc[:, lanes] = alpha * acc_sc[:, lanes] + jnp.dot(p.astype(BF16), _head(va, h, V_DIM),
                                                                      preferred_element_type=F32)
                m_sc[:, lanes] = m_new

        @pl.when(j < i)
        def _():
            step(False)

        @pl.when(j == i)
        def _():
            step(True)
            o_ref[0] = acc_sc[...] / l_sc[...]
            lse_ref[0] = m_sc[...] + jnp.log(l_sc[...])

    wq, wv = ATT_HB * QK_PAD, ATT_HB * V_DIM
    grid_spec = pltpu.PrefetchScalarGridSpec(
        num_scalar_prefetch=2, grid=(B, N_HEADS // ATT_HB, qi.shape[0]),
        in_specs=[pl.BlockSpec((1, t, wq), lambda b, h, p, qi, kj: (b, qi[p], h)),
                  pl.BlockSpec((1, t, wq), lambda b, h, p, qi, kj: (b, kj[p], h)),
                  pl.BlockSpec((1, t, wv), lambda b, h, p, qi, kj: (b, kj[p], v_blk0 + h))],
        out_specs=[pl.BlockSpec((1, t, wv), lambda b, h, p, qi, kj: (b, qi[p], h)),
                   pl.BlockSpec((1, t, wv), lambda b, h, p, qi, kj: (b, qi[p], h))],
        scratch_shapes=[pltpu.VMEM((t, wv), F32), pltpu.VMEM((t, wv), F32), pltpu.VMEM((t, wv), F32)])
    return pl.pallas_call(
        body, grid_spec=grid_spec,
        out_shape=[jax.ShapeDtypeStruct((B, S, N_HEADS * V_DIM), F32),
                   jax.ShapeDtypeStruct((B, S, N_HEADS * LANE), F32)],
        name="attn_fwd", compiler_params=_cparams(("parallel", "parallel", "arbitrary")))(qi, kj, q, k, vsrc)


def _attn_p_ds(q, k, v, o, do, lse, diagonal, t):
    s = _scores(q, k, diagonal)
    p = jnp.exp(s - jnp.tile(lse, (1, t // LANE)))
    dp = lax.dot_general(do.astype(BF16), v, (((1,), (1,)), ((), ())), preferred_element_type=F32)
    delta = jnp.sum(do * o, axis=1, keepdims=True)
    ds = p * (dp - delta)
    return p, ds


def _attn_dkv_call(q, k, vsrc, v_blk0, o, do, lse):
    B, S, _ = q.shape
    t, n = _att_tiles(S)

    qi, kj = _causal_pairs(n, True)

    def body(qi_ref, kj_ref, q_ref, k_ref, v_ref, o_ref, do_ref, lse_ref, dk_ref, dv_ref, dk_sc, dv_sc):
        p_id = pl.program_id(2)
        i, j = qi_ref[p_id], kj_ref[p_id]

        @pl.when(i == j)
        def _():
            dk_sc[...] = jnp.zeros(dk_sc.shape, F32)
            dv_sc[...] = jnp.zeros(dv_sc.shape, F32)

        def step(diagonal):
            qa, ka, va, oa, doa, la = q_ref[0], k_ref[0], v_ref[0], o_ref[0], do_ref[0], lse_ref[0]
            for h in range(ATT_HB):
                qb, dob = _head(qa, h, QK_PAD), _head(doa, h, V_DIM)
                p, ds = _attn_p_ds(qb, _head(ka, h, QK_PAD), _head(va, h, V_DIM), _head(oa, h, V_DIM), dob,
                                   _head(la, h, LANE), diagonal, t)
                dv_sc[:, h * V_DIM:(h + 1) * V_DIM] += lax.dot_general(
                    p.astype(BF16), dob.astype(BF16), (((0,), (0,)), ((), ())), preferred_element_type=F32)
                dk_sc[:, h * QK_PAD:(h + 1) * QK_PAD] += lax.dot_general(
                    ds.astype(BF16), qb, (((0,), (0,)), ((), ())), preferred_element_type=F32)

        @pl.when(i > j)
        def _():
            step(False)

        @pl.when(i == j)
        def _():
            step(True)

        @pl.when(i == n - 1)
        def _():
            dk_ref[0] = dk_sc[...].astype(BF16)
            dv_ref[0] = dv_sc[...].astype(BF16)

    wq, wv = ATT_HB * QK_PAD, ATT_HB * V_DIM
    at_q = lambda b, h, p, qi, kj: (b, qi[p], h)
    at_k = lambda b, h, p, qi, kj: (b, kj[p], h)
    grid_spec = pltpu.PrefetchScalarGridSpec(
        num_scalar_prefetch=2, grid=(B, N_HEADS // ATT_HB, qi.shape[0]),
        in_specs=[pl.BlockSpec((1, t, wq), at_q), pl.BlockSpec((1, t, wq), at_k),
                  pl.BlockSpec((1, t, wv), lambda b, h, p, qi, kj: (b, kj[p], v_blk0 + h)),
                  pl.BlockSpec((1, t, wv), at_q), pl.BlockSpec((1, t, wv), at_q), pl.BlockSpec((1, t, wv), at_q)],
        out_specs=[pl.BlockSpec((1, t, wq), at_k), pl.BlockSpec((1, t, wv), at_k)],
        scratch_shapes=[pltpu.VMEM((t, wq), F32), pltpu.VMEM((t, wv), F32)])
    return pl.pallas_call(
        body, grid_spec=grid_spec,
        out_shape=[jax.ShapeDtypeStruct((B, S, N_HEADS * QK_PAD), BF16),
                   jax.ShapeDtypeStruct((B, S, N_HEADS * V_DIM), BF16)],
        name="attn_dkv", compiler_params=_cparams(("parallel", "parallel", "arbitrary")))(
            qi, kj, q, k, vsrc, o, do, lse)


def _attn_dq_call(q, k, vsrc, v_blk0, o, do, lse):
    B, S, _ = q.shape
    t, n = _att_tiles(S)

    qi, kj = _causal_pairs(n, False)

    def body(qi_ref, kj_ref, q_ref, k_ref, v_ref, o_ref, do_ref, lse_ref, dq_ref, dq_sc):
        p_id = pl.program_id(2)
        i, j = qi_ref[p_id], kj_ref[p_id]

        @pl.when(j == 0)
        def _():
            dq_sc[...] = jnp.zeros(dq_sc.shape, F32)

        def step(diagonal):
            qa, ka, va, oa, doa, la = q_ref[0], k_ref[0], v_ref[0], o_ref[0], do_ref[0], lse_ref[0]
            for h in range(ATT_HB):
                kb = _head(ka, h, QK_PAD)
                _, ds = _attn_p_ds(_head(qa, h, QK_PAD), kb, _head(va, h, V_DIM), _head(oa, h, V_DIM),
                                   _head(doa, h, V_DIM), _head(la, h, LANE), diagonal, t)
                dq_sc[:, h * QK_PAD:(h + 1) * QK_PAD] += jnp.dot(ds.astype(BF16), kb, preferred_element_type=F32)

        @pl.when(j < i)
        def _():
            step(False)

        @pl.when(j == i)
        def _():
            step(True)
            dq_ref[0] = dq_sc[...].astype(BF16)

    wq, wv = ATT_HB * QK_PAD, ATT_HB * V_DIM
    at_q = lambda b, h, p, qi, kj: (b, qi[p], h)
    at_k = lambda b, h, p, qi, kj: (b, kj[p], h)
    grid_spec = pltpu.PrefetchScalarGridSpec(
        num_scalar_prefetch=2, grid=(B, N_HEADS // ATT_HB, qi.shape[0]),
        in_specs=[pl.BlockSpec((1, t, wq), at_q), pl.BlockSpec((1, t, wq), at_k),
                  pl.BlockSpec((1, t, wv), lambda b, h, p, qi, kj: (b, kj[p], v_blk0 + h)),
                  pl.BlockSpec((1, t, wv), at_q), pl.BlockSpec((1, t, wv), at_q), pl.BlockSpec((1, t, wv), at_q)],
        out_specs=pl.BlockSpec((1, t, wq), at_q),
        scratch_shapes=[pltpu.VMEM((t, wq), F32)])
    return pl.pallas_call(
        body, grid_spec=grid_spec, out_shape=jax.ShapeDtypeStruct((B, S, N_HEADS * QK_PAD), BF16),
        name="attn_dq", compiler_params=_cparams(("parallel", "parallel", "arbitrary")))(
            qi, kj, q, k, vsrc, o, do, lse)


@jax.custom_vjp
def attention(q, k, kv):
    return _attn_fwd_call(q, k, kv, N_HEADS // ATT_HB)[0]


def _attention_fwd(q, k, kv):
    o, lse = _attn_fwd_call(q, k, kv, N_HEADS // ATT_HB)
    return o, (q, k, kv, o, lse)


def _attention_bwd(res, do):
    q, k, kv, o, lse = res
    dk, dv = _attn_dkv_call(q, k, kv, N_HEADS // ATT_HB, o, do, lse)
    dq = _attn_dq_call(q, k, kv, N_HEADS // ATT_HB, o, do, lse)
    dkv = jnp.concatenate([jnp.zeros_like(dv), dv], axis=-1)
    return dq, dk, dkv


attention.defvjp(_attention_fwd, _attention_bwd)


def _shift_down(v, sh, rows):
    return jnp.where(rows >= sh, pltpu.roll(v, sh, 0), 0.0)


def _shift_up(v, sh, rows, S):
    return jnp.where(rows < S - sh, pltpu.roll(v, S - sh, 0), 0.0)


def _conv_pre(u, w_ref, b_ref, rows):
    acc = b_ref[...] + w_ref[pl.ds(CONV_K - 1, 1), :] * u
    for k in range(CONV_K - 1):
        acc = acc + w_ref[pl.ds(k, 1), :] * _shift_down(u, CONV_K - 1 - k, rows)
    return acc


def _conv_fwd_call(u, w, b):
    B, S, C = u.shape

    def body(u_ref, w_ref, b_ref, o_ref):
        uu = u_ref[0]
        rows = lax.broadcasted_iota(jnp.int32, uu.shape, 0)
        o_ref[0] = _silu(_conv_pre(uu, w_ref, b_ref, rows))

    spec = pl.BlockSpec((1, S, LANE), lambda c, bb: (bb, 0, c))
    return pl.pallas_call(
        body, grid=(C // LANE, B),
        in_specs=[spec, pl.BlockSpec((CONV_K, LANE), lambda c, bb: (0, c)), pl.BlockSpec((1, LANE), lambda c, bb: (0, c))],
        out_specs=spec, out_shape=jax.ShapeDtypeStruct(u.shape, F32), name="conv_fwd",
        compiler_params=_cparams(("parallel", "arbitrary")))(u, w, b)


def _conv_bwd_call(u, w, b, g):
    B, S, C = u.shape

    def body(u_ref, w_ref, b_ref, g_ref, du_ref, dw_ref, db_ref):
        uu = u_ref[0]
        rows = lax.broadcasted_iota(jnp.int32, uu.shape, 0)
        pre = _conv_pre(uu, w_ref, b_ref, rows)
        sg = lax.logistic(pre)
        dpre = g_ref[0] * sg * (1.0 + pre * (1.0 - sg))
        du = w_ref[pl.ds(CONV_K - 1, 1), :] * dpre
        dws = [None] * CONV_K
        dws[CONV_K - 1] = jnp.sum(dpre * uu, axis=0, keepdims=True)
        for k in range(CONV_K - 1):
            sh = CONV_K - 1 - k
            du = du + w_ref[pl.ds(k, 1), :] * _shift_up(dpre, sh, rows, S)
            dws[k] = jnp.sum(dpre * _shift_down(uu, sh, rows), axis=0, keepdims=True)
        du_ref[0] = du
        dbv = jnp.sum(dpre, axis=0, keepdims=True)
        first = pl.program_id(1) == 0

        @pl.when(first)
        def _():
            for k in range(CONV_K):
                dw_ref[pl.ds(k, 1), :] = dws[k]
            db_ref[...] = dbv

        @pl.when(jnp.logical_not(first))
        def _():
            for k in range(CONV_K):
                dw_ref[pl.ds(k, 1), :] += dws[k]
            db_ref[...] += dbv

    spec = pl.BlockSpec((1, S, LANE), lambda c, bb: (bb, 0, c))
    wspec = pl.BlockSpec((CONV_K, LANE), lambda c, bb: (0, c))
    bspec = pl.BlockSpec((1, LANE), lambda c, bb: (0, c))
    return pl.pallas_call(
        body, grid=(C // LANE, B), in_specs=[spec, wspec, bspec, spec], out_specs=[spec, wspec, bspec],
        out_shape=[jax.ShapeDtypeStruct(u.shape, F32), jax.ShapeDtypeStruct(w.shape, F32),
                   jax.ShapeDtypeStruct(b.shape, F32)],
        name="conv_bwd", compiler_params=_cparams(("parallel", "arbitrary")))(u, w, b, g)


@jax.custom_vjp
def conv_silu(u, w, b):
    return _conv_fwd_call(u, w, b)


def _conv_silu_fwd(u, w, b):
    return _conv_fwd_call(u, w, b), (u, w, b)


def _conv_silu_bwd(res, g):
    return tuple(_conv_bwd_call(*res, g))


conv_silu.defvjp(_conv_silu_fwd, _conv_silu_bwd)


def _chunk_cumsum_call(a, reverse, name):
    B, S, W = a.shape

    def body(a_ref, o_ref):
        r = lax.broadcasted_iota(jnp.int32, (CHUNK, CHUNK), 0)
        c = lax.broadcasted_iota(jnp.int32, (CHUNK, CHUNK), 1)
        tri = jnp.where((c >= r) if reverse else (c <= r), 1.0, 0.0).astype(F32)
        o_ref[0] = jnp.dot(tri, a_ref[0], preferred_element_type=F32, precision=lax.Precision.HIGHEST)

    spec = pl.BlockSpec((1, CHUNK, W), lambda b, c: (b, c, 0))
    return pl.pallas_call(body, grid=(B, S // CHUNK), in_specs=[spec], out_specs=spec,
                          out_shape=jax.ShapeDtypeStruct(a.shape, F32), name=name,
                          compiler_params=_cparams(("parallel", "parallel")))(a)


@jax.custom_vjp
def chunk_cumsum(a):
    return _chunk_cumsum_call(a, False, "chunk_cumsum_fwd")


chunk_cumsum.defvjp(lambda a: (_chunk_cumsum_call(a, False, "chunk_cumsum_fwd"), None),
                    lambda _, g: (_chunk_cumsum_call(g, True, "chunk_cumsum_bwd"),))


GROUP_W = 4 * HEAD_P
HPG = SSM_HEADS // SSM_GROUPS


def _ssd_masks():
    lane = lax.broadcasted_iota(jnp.int32, (1, GROUP_W), 1)
    return [((lane >= HEAD_P * j) & (lane < HEAD_P * (j + 1))).astype(F32) for j in range(HPG)]


def _ssd_decays(ac_cols, acr_ref):
    r = lax.broadcasted_iota(jnp.int32, (CHUNK, CHUNK), 0)
    c = lax.broadcasted_iota(jnp.int32, (CHUNK, CHUNK), 1)
    return [jnp.exp(jnp.where(c <= r, ac_cols[j] - acr_ref[0, j], NEG)) for j in range(HPG)]


def _ssd_cols(blk, g):
    lane = lax.broadcasted_iota(jnp.int32, blk.shape, 1)
    return [jnp.sum(jnp.where(lane == HPG * g + j, blk, 0.0), axis=1, keepdims=True) for j in range(HPG)]


def _ssd_spread(cols, masks):
    out = cols[0] * masks[0]
    for j in range(1, HPG):
        out = out + cols[j] * masks[j]
    return out


def _ssd_gather(val, cols, masks, g):
    lane = lax.broadcasted_iota(jnp.int32, (1, LANE), 1)
    out = jnp.zeros((CHUNK, LANE), F32)
    for j in range(HPG):
        tot = jnp.sum(val * masks[j], axis=1, keepdims=True)
        if cols is not None:
            tot = tot + cols[j]
        out = out + tot * (lane == HPG * g + j).astype(F32)
    return out


def _dot(a, b, dims):
    return lax.dot_general(a.astype(BF16), b.astype(BF16), (dims, ((), ())), preferred_element_type=F32)


NN = ((1,), (0,))
NT = ((1,), (1,))
TN = ((0,), (0,))


XBC_W = GROUP_W + 2 * STATE_N


def _ssd_load(xbc_ref, dt_ref, ac_ref, masks, g):
    blk = xbc_ref[0]
    x, bm, cm = blk[:, :GROUP_W], blk[:, GROUP_W:GROUP_W + STATE_N], blk[:, GROUP_W + STATE_N:]
    ac_cols = _ssd_cols(ac_ref[0], g)
    dt = _ssd_spread(_ssd_cols(dt_ref[0], g), masks)
    ac = _ssd_spread(ac_cols, masks)
    is_last = (lax.broadcasted_iota(jnp.int32, (CHUNK, GROUP_W), 0) == CHUNK - 1).astype(F32)
    return x, bm, cm, dt, ac, ac_cols, is_last


def _ssd_in_specs(nc, rev):
    cc = (lambda c: nc - 1 - c) if rev else (lambda c: c)
    return [pl.BlockSpec((1, CHUNK, XBC_W), lambda b, g, c: (b, cc(c), g)),
            pl.BlockSpec((1, CHUNK, LANE), lambda b, g, c: (b, cc(c), 0)),
            pl.BlockSpec((1, CHUNK, LANE), lambda b, g, c: (b, cc(c), 0)),
            pl.BlockSpec((1, HPG, 1, CHUNK), lambda b, g, c: (b, g, 0, cc(c))),
            pl.BlockSpec((1, GROUP_W), lambda b, g, c: (0, g))]


def _ssd_fwd_call(xbc, dtp, acp, acr, dsk):
    B, S, _ = xbc.shape
    nc = S // CHUNK

    def body(xbc_ref, dt_ref, ac_ref, ar_ref, ds_ref, y_ref, hp_ref, h_sc):
        @pl.when(pl.program_id(2) == 0)
        def _():
            h_sc[...] = jnp.zeros(h_sc.shape, F32)

        masks = _ssd_masks()
        x, bm, cm, dt, ac, ac_cols, is_last = _ssd_load(xbc_ref, dt_ref, ac_ref, masks, pl.program_id(1))
        last = jnp.sum(ac * is_last, axis=0, keepdims=True)
        decays = _ssd_decays(ac_cols, ar_ref)
        xd = x * dt
        cb = _dot(cm, bm, NT)
        hprev = h_sc[...]
        hp_ref[0, 0, 0] = hprev
        y = _dot(cm, hprev, NN) * jnp.exp(ac) + ds_ref[...] * x
        for j in range(HPG):
            y = y + _dot(cb * decays[j], xd * masks[j], NN)
        y_ref[0] = y
        h_sc[...] = hprev * jnp.exp(last) + _dot(bm, xd * jnp.exp(last - ac), TN)

    return pl.pallas_call(
        body, grid=(B, SSM_GROUPS, nc), in_specs=_ssd_in_specs(nc, False),
        out_specs=[pl.BlockSpec((1, CHUNK, GROUP_W), lambda b, g, c: (b, c, g)),
                   pl.BlockSpec((1, 1, 1, STATE_N, GROUP_W), lambda b, g, c: (b, g, c, 0, 0))],
        out_shape=[jax.ShapeDtypeStruct((B, S, D_INNER), F32),
                   jax.ShapeDtypeStruct((B, SSM_GROUPS, nc, STATE_N, GROUP_W), F32)],
        scratch_shapes=[pltpu.VMEM((STATE_N, GROUP_W), F32)], name="ssd_fwd",
        compiler_params=_cparams(("parallel", "parallel", "arbitrary")))(xbc, dtp, acp, acr, dsk)


def _ssd_bwd_call(xbc, dtp, acp, acr, dsk, hps, dy):
    B, S, _ = xbc.shape
    nc = S // CHUNK

    def body(xbc_ref, dt_ref, ac_ref, ar_ref, ds_ref, hp_ref, dy_ref,
             dxbc_ref, ddt_ref, dac_ref, dar_ref, dds_ref, dh_sc):
        first = pl.program_id(2) == 0

        @pl.when(first)
        def _():
            dh_sc[...] = jnp.zeros(dh_sc.shape, F32)

        masks = _ssd_masks()
        x, bm, cm, dt, ac, ac_cols, is_last = _ssd_load(xbc_ref, dt_ref, ac_ref, masks, pl.program_id(0))
        last = jnp.sum(ac * is_last, axis=0, keepdims=True)
        g = dy_ref[0]
        hprev = hp_ref[0, 0, 0]
        dh = dh_sc[...]
        decays = _ssd_decays(ac_cols, ar_ref)
        dcols = []
        xd = x * dt
        cb = _dot(cm, bm, NT)
        e_c = jnp.exp(ac)
        e_end = jnp.exp(last - ac)
        e_last = jnp.exp(last)
        z = _dot(cm, hprev, NN)
        dz = g * e_c
        dac = g * z * e_c
        dc = _dot(dz, hprev, NT)
        dhprev = _dot(cm, dz, TN) + dh * e_last
        dcb = jnp.zeros((CHUNK, CHUNK), F32)
        dxd = jnp.zeros(xd.shape, F32)
        for j in range(HPG):
            gj = cb * decays[j]
            dgj = _dot(g * masks[j], xd, NT)
            dxd = dxd + _dot(gj, g, TN) * masks[j]
            dcb = dcb + dgj * decays[j]
            dseg = dgj * gj
            dcols.append(jnp.sum(dseg, axis=1, keepdims=True))
            dar_ref[0, j] = -jnp.sum(dseg, axis=0, keepdims=True)
        dc = dc + _dot(dcb, bm, NN)
        db = _dot(dcb, cm, TN)
        sx = xd * e_end
        db = db + _dot(sx, dh, NT)
        dsx = _dot(bm, dh, NN)
        dxd = dxd + dsx * e_end
        de = dsx * sx
        dac = dac - de
        dlast = jnp.sum(de, axis=0, keepdims=True) + jnp.sum(dh * hprev, axis=0, keepdims=True) * e_last
        dsk = ds_ref[...]
        dxbc_ref[0] = jnp.concatenate([dxd * dt + dsk * g, db, dc], axis=1)
        grp = pl.program_id(0)
        ddt_ref[0, 0] = _ssd_gather(dxd * x, None, masks, grp)
        dac_ref[0, 0] = _ssd_gather(dac + is_last * dlast, dcols, masks, grp)
        dds = jnp.sum(g * x, axis=0, keepdims=True)
        first_all = first & (pl.program_id(1) == 0)

        @pl.when(first_all)
        def _():
            dds_ref[...] = dds

        @pl.when(jnp.logical_not(first_all))
        def _():
            dds_ref[...] += dds

        dh_sc[...] = dhprev

    rc = lambda c: nc - 1 - c
    in_specs = [pl.BlockSpec(s.block_shape, (lambda g, b, c, f=s.index_map: f(b, g, c))) for s in _ssd_in_specs(nc, True)]
    in_specs.append(pl.BlockSpec((1, 1, 1, STATE_N, GROUP_W), lambda g, b, c: (b, g, rc(c), 0, 0)))
    in_specs.append(pl.BlockSpec((1, CHUNK, GROUP_W), lambda g, b, c: (b, rc(c), g)))
    per_group = pl.BlockSpec((1, 1, CHUNK, LANE), lambda g, b, c: (b, g, rc(c), 0))
    out_specs = [pl.BlockSpec((1, CHUNK, XBC_W), lambda g, b, c: (b, rc(c), g)), per_group, per_group,
                 pl.BlockSpec((1, HPG, 1, CHUNK), lambda g, b, c: (b, g, 0, rc(c))),
                 pl.BlockSpec((1, GROUP_W), lambda g, b, c: (0, g))]
    out_shape = [jax.ShapeDtypeStruct(xbc.shape, F32),
                 jax.ShapeDtypeStruct((B, SSM_GROUPS, S, LANE), F32), jax.ShapeDtypeStruct((B, SSM_GROUPS, S, LANE), F32),
                 jax.ShapeDtypeStruct(acr.shape, F32), jax.ShapeDtypeStruct(dsk.shape, F32)]
    return pl.pallas_call(
        body, grid=(SSM_GROUPS, B, nc), in_specs=in_specs, out_specs=out_specs, out_shape=out_shape,
        scratch_shapes=[pltpu.VMEM((STATE_N, GROUP_W), F32)], name="ssd_bwd",
        compiler_params=_cparams(("arbitrary", "arbitrary", "arbitrary")))(xbc, dtp, acp, acr, dsk, hps, dy)


@jax.custom_vjp
def ssd(xbc, dtp, acp, acr, dsk):
    return _ssd_fwd_call(xbc, dtp, acp, acr, dsk)[0]


def _ssd_fwd(xbc, dtp, acp, acr, dsk):
    y, hps = _ssd_fwd_call(xbc, dtp, acp, acr, dsk)
    return y, (xbc, dtp, acp, acr, dsk, hps)


def _ssd_bwd(res, dy):
    dxbc, ddt, dac, dacr, dds = _ssd_bwd_call(*res, dy)
    return dxbc, jnp.sum(ddt, axis=1), jnp.sum(dac, axis=1), dacr, dds


ssd.defvjp(_ssd_fwd, _ssd_bwd)


def _pack_small(arrs):
    flat = jnp.concatenate([a.reshape(-1) for a in arrs])
    rows = -(-flat.shape[0] // (8 * LANE)) * 8
    return jnp.pad(flat, (0, rows * LANE - flat.shape[0])).reshape(rows, LANE)


def _unpack_small(buf, shapes):
    flat = buf.reshape(-1)
    out, off = [], 0
    for shp in shapes:
        n = int(np.prod(shp))
        out.append(flat[off:off + n].reshape(shp))
        off += n
    return out


def _rows_tile(rows, cap):
    for cand in range(min(rows, cap), 7, -8):
        if rows % cand == 0:
            return cand
    return rows


def _pair_sum(mine, theirs, cidx, name):
    n4, kk, nn = mine.shape
    half = kk // 2
    tr = _rows_tile(half, 256)
    nb = half // tr

    def body(c_ref, a_ref, b_ref, o_ref, ob_ref):
        tot = a_ref[...] + b_ref[...]
        o_ref[...] = tot
        ob_ref[...] = tot.astype(BF16)

    spec = pl.BlockSpec((1, tr, nn), lambda j, i, c: (j, i, 0))
    grid_spec = pltpu.PrefetchScalarGridSpec(
        num_scalar_prefetch=1, grid=(n4, nb),
        in_specs=[pl.BlockSpec((1, tr, nn), lambda j, i, c: (j, c[0] * nb + i, 0)), spec], out_specs=[spec, spec])
    return pl.pallas_call(
        body, grid_spec=grid_spec,
        out_shape=[jax.ShapeDtypeStruct((n4, half, nn), F32), jax.ShapeDtypeStruct((n4, half, nn), BF16)],
        name=name, compiler_params=_cparams(("parallel", "parallel")))(cidx, mine, theirs)


def _chip_sum(quad, pair, chip_idx, name):
    _, rows, nn = quad.shape
    tr = _rows_tile(rows, 256)

    def body(s_ref, q_ref, p_ref, o_ref):
        for mine in range(4):
            @pl.when(s_ref[0] == mine)
            def _(mine=mine):
                acc = None
                for d in range(4):
                    term = p_ref[0] if d == mine else q_ref[d].astype(F32)
                    acc = term if acc is None else acc + term
                o_ref[...] = acc

    grid_spec = pltpu.PrefetchScalarGridSpec(
        num_scalar_prefetch=1, grid=(rows // tr,),
        in_specs=[pl.BlockSpec((4, tr, nn), lambda i, s: (0, i, 0)), pl.BlockSpec((1, tr, nn), lambda i, s: (s[0], i, 0))],
        out_specs=pl.BlockSpec((tr, nn), lambda i, s: (i, 0)))
    return pl.pallas_call(body, grid_spec=grid_spec, out_shape=jax.ShapeDtypeStruct((rows, nn), F32), name=name,
                          compiler_params=_cparams(("parallel",)))(chip_idx, quad, pair)


def _adam_halves_call(w, mine, other, cidx, m, v, name):
    rows, nn = w.shape
    half = rows // 2
    tr = _rows_tile(half, 128)
    nb = half // tr

    def body(c_ref, w_ref, a_ref, b_ref, m_ref, v_ref, g_ref, d_ref, nm_ref, nv_ref):
        upper = (pl.program_id(0) >= nb).astype(jnp.int32)
        g = jnp.where(upper == c_ref[0], a_ref[...], b_ref[...])
        g_ref[...] = g
        d_ref[...], nm_ref[...], nv_ref[...] = _adam_fn(w_ref[...], g, m_ref[...], v_ref[...])

    spec = pl.BlockSpec((tr, nn), lambda i, c: (i, 0))
    hspec = pl.BlockSpec((tr, nn), lambda i, c: (i % nb, 0))
    grid_spec = pltpu.PrefetchScalarGridSpec(num_scalar_prefetch=1, grid=(2 * nb,),
                                             in_specs=[spec, hspec, hspec, spec, spec], out_specs=[spec] * 4)
    return pl.pallas_call(body, grid_spec=grid_spec, out_shape=[jax.ShapeDtypeStruct((rows, nn), F32)] * 4, name=name,
                          compiler_params=_cparams(("parallel",)))(cidx, w, mine, other, m, v)


def _stack_sum(stack, name):
    n, rows, nn = stack.shape
    tr = _rows_tile(rows, 256)

    def body(s_ref, o_ref):
        acc = s_ref[0]
        for d in range(1, n):
            acc = acc + s_ref[d]
        o_ref[...] = acc

    return pl.pallas_call(
        body, grid=(rows // tr,), in_specs=[pl.BlockSpec((n, tr, nn), lambda i: (0, i, 0))],
        out_specs=pl.BlockSpec((tr, nn), lambda i: (i, 0)), out_shape=jax.ShapeDtypeStruct((rows, nn), F32),
        name=name, compiler_params=_cparams(("parallel",)))(stack)


def _adam_call(w, g, m, v, name):
    rows, nn = w.shape
    tr = _rows_tile(rows, 128)

    def body(w_ref, g_ref, m_ref, v_ref, d_ref, nm_ref, nv_ref):
        d_ref[...], nm_ref[...], nv_ref[...] = _adam_fn(w_ref[...], g_ref[...], m_ref[...], v_ref[...])

    spec = pl.BlockSpec((tr, nn), lambda i: (i, 0))
    sds = jax.ShapeDtypeStruct((rows, nn), F32)
    return pl.pallas_call(body, grid=(rows // tr,), in_specs=[spec] * 4, out_specs=[spec] * 3,
                          out_shape=[sds] * 3, name=name, compiler_params=_cparams(("parallel",)))(w, g, m, v)


def _adam_fn(w, g, m, v):
    m = ADAM_B1 * m + (1.0 - ADAM_B1) * g
    v = ADAM_B2 * v + (1.0 - ADAM_B2) * (g * g)
    m_hat = m / (1.0 - ADAM_B1 ** ADAM_STEP)
    v_hat = v / (1.0 - ADAM_B2 ** ADAM_STEP)
    delta = -ADAM_LR * (m_hat / (jnp.sqrt(v_hat) + ADAM_EPS) + ADAM_WD * w)
    return delta, m, v


def _mesh_pos():
    return lax.axis_index("x"), lax.axis_index("y"), lax.axis_index("c")


def _other_chips(x, y):
    return [(1 - x, y), (x, 1 - y), (1 - x, 1 - y)]


HBM_SPEC = pl.BlockSpec(memory_space=pl.ANY)


def _remote(src, dst, send_sems, recv_sems, k, to):
    return pltpu.make_async_remote_copy(src_ref=src, dst_ref=dst, send_sem=send_sems.at[k], recv_sem=recv_sems.at[k],
                                        device_id=to, device_id_type=MESH)


def _half_rows(c, rows, align):
    half = rows // 2
    return (pl.ds(pl.multiple_of(c * half, align), half), pl.ds(pl.multiple_of((1 - c) * half, align), half))


def _gather_weights(mats, conv):
    n = len(mats)

    def body(*refs):
        ins, conv_in = refs[:n], refs[n]
        outs, conv_out = refs[n + 1:2 * n + 1], refs[2 * n + 1]
        send_sems, recv_sems, local_sem = refs[2 * n + 2:]
        x, y, c = _mesh_pos()
        me, sibling, s = (x, y, c), (x, y, 1 - c), 2 * x + y
        chips = _other_chips(x, y)
        rows = [_half_rows(c, m.shape[0], 16) for m in mats]
        own = pltpu.make_async_copy(conv_in, conv_out.at[s], local_sem)
        own.start()
        sent = []
        for i in range(n):
            mine = rows[i][0]
            for j, (cx, cy) in enumerate(chips):
                sent.append(_remote(ins[i].at[mine], outs[i].at[s, mine], send_sems, recv_sems, 6 * i + j, (cx, cy, c)))
        for j, (cx, cy) in enumerate(chips):
            sent.append(_remote(conv_in, conv_out.at[s], send_sems, recv_sems, 6 * n + j, (cx, cy, c)))
        for cp in sent:
            cp.start()
        for i in range(n):
            mine = rows[i][0]
            for j, (cx, cy) in enumerate(chips):
                landed = outs[i].at[2 * cx + cy, mine]
                _remote(landed, landed, send_sems, recv_sems, 6 * i + j, me).wait_recv()
                fwd = _remote(landed, landed, send_sems, recv_sems, 6 * i + 3 + j, sibling)
                fwd.start()
                sent.append(fwd)
        for j, (cx, cy) in enumerate(chips):
            slot = conv_out.at[2 * cx + cy]
            _remote(slot, slot, send_sems, recv_sems, 6 * n + j, me).wait_recv()
        for i in range(n):
            theirs_rows = rows[i][1]
            for j, (cx, cy) in enumerate(chips):
                theirs = outs[i].at[2 * cx + cy, theirs_rows]
                _remote(theirs, theirs, send_sems, recv_sems, 6 * i + 3 + j, me).wait_recv()
        for cp in sent:
            cp.wait_send()
        own.wait()

    out_shape = [jax.ShapeDtypeStruct((4,) + m.shape, m.dtype) for m in mats]
    out_shape.append(jax.ShapeDtypeStruct((4,) + conv.shape, conv.dtype))
    res = pl.pallas_call(
        body, in_specs=[HBM_SPEC] * (n + 1), out_specs=[HBM_SPEC] * (n + 1), out_shape=out_shape,
        scratch_shapes=[pltpu.SemaphoreType.DMA((6 * n + 3,)), pltpu.SemaphoreType.DMA((6 * n + 3,)),
                        pltpu.SemaphoreType.DMA],
        name="all_gather_weights")(*mats, conv)
    chip = 2 * lax.axis_index("x") + lax.axis_index("y")
    full = [lax.dynamic_update_slice_in_dim(r, m[None], chip, axis=0) for r, m in zip(res[:n], mats)]
    return full, res[n]


def _sibling_exchange(stacks):
    n = len(stacks)

    def body(*refs):
        ins, outs = refs[:n], refs[n:2 * n]
        send_sems, recv_sems = refs[2 * n:]
        x, y, c = _mesh_pos()
        cps = []
        for i in range(n):
            theirs = _half_rows(c, stacks[i].shape[1], 8)[1]
            cps.append(_remote(ins[i].at[:, theirs, :], outs[i], send_sems, recv_sems, i, (x, y, 1 - c)))
        for cp in cps:
            cp.start()
        for cp in cps:
            cp.wait()

    out_shape = [jax.ShapeDtypeStruct((4, s.shape[1] // 2, s.shape[2]), s.dtype) for s in stacks]
    return pl.pallas_call(
        body, in_specs=[HBM_SPEC] * n, out_specs=[HBM_SPEC] * n, out_shape=out_shape,
        scratch_shapes=[pltpu.SemaphoreType.DMA((n,)), pltpu.SemaphoreType.DMA((n,))],
        name="grad_sibling_exchange")(*stacks)


def _chip_exchange(parts):
    n = len(parts)

    def body(*refs):
        ins, outs = refs[:n], refs[n:2 * n]
        send_sems, recv_sems = refs[2 * n:]
        x, y, c = _mesh_pos()
        me, s = (x, y, c), 2 * x + y
        chips = _other_chips(x, y)
        sent = [_remote(ins[i].at[2 * cx + cy], outs[i].at[s], send_sems, recv_sems, 3 * i + j, (cx, cy, c))
                for i in range(n) for j, (cx, cy) in enumerate(chips)]
        for cp in sent:
            cp.start()
        for i in range(n):
            for j, (cx, cy) in enumerate(chips):
                slot = outs[i].at[2 * cx + cy]
                _remote(slot, slot, send_sems, recv_sems, 3 * i + j, me).wait_recv()
        for cp in sent:
            cp.wait_send()

    return pl.pallas_call(
        body, in_specs=[HBM_SPEC] * n, out_specs=[HBM_SPEC] * n,
        out_shape=[jax.ShapeDtypeStruct(p.shape, p.dtype) for p in parts],
        scratch_shapes=[pltpu.SemaphoreType.DMA((3 * n,)), pltpu.SemaphoreType.DMA((3 * n,))],
        name="grad_chip_exchange")(*parts)


def _sibling_swap(halves):
    n = len(halves)

    def body(*refs):
        ins, outs = refs[:n], refs[n:2 * n]
        send_sems, recv_sems = refs[2 * n:]
        x, y, c = _mesh_pos()
        cps = [_remote(ins[i], outs[i], send_sems, recv_sems, i, (x, y, 1 - c)) for i in range(n)]
        for cp in cps:
            cp.start()
        for cp in cps:
            cp.wait()

    return pl.pallas_call(
        body, in_specs=[HBM_SPEC] * n, out_specs=[HBM_SPEC] * n,
        out_shape=[jax.ShapeDtypeStruct(h.shape, h.dtype) for h in halves],
        scratch_shapes=[pltpu.SemaphoreType.DMA((n,)), pltpu.SemaphoreType.DMA((n,))],
        name="grad_sibling_swap")(*halves)


def _gather_small(vec):
    def body(in_ref, out_ref, send_sems, recv_sems, local_sem):
        x, y, c = _mesh_pos()
        me = (x, y, c)
        own = pltpu.make_async_copy(in_ref, out_ref.at[4 * x + 2 * y + c], local_sem)
        own.start()
        peers = [(1 - x if k & 4 else x, 1 - y if k & 2 else y, 1 - c if k & 1 else c) for k in range(1, 8)]
        sent = [_remote(in_ref, out_ref.at[4 * x + 2 * y + c], send_sems, recv_sems, k, p) for k, p in enumerate(peers)]
        for cp in sent:
            cp.start()
        for k, (px, py, pc) in enumerate(peers):
            slot = out_ref.at[4 * px + 2 * py + pc]
            _remote(slot, slot, send_sems, recv_sems, k, me).wait_recv()
        for cp in sent:
            cp.wait_send()
        own.wait()

    return pl.pallas_call(
        body, in_specs=[HBM_SPEC], out_specs=HBM_SPEC, out_shape=jax.ShapeDtypeStruct((8,) + vec.shape, vec.dtype),
        scratch_shapes=[pltpu.SemaphoreType.DMA((7,)), pltpu.SemaphoreType.DMA((7,)), pltpu.SemaphoreType.DMA],
        name="grad_gather_small")(vec)


def _reduce_matrices(stacks, names):
    cidx = lax.axis_index("c").astype(jnp.int32).reshape(1)
    chip = (2 * lax.axis_index("x") + lax.axis_index("y")).astype(jnp.int32).reshape(1)
    got = _sibling_exchange(stacks)
    pairs = [_pair_sum(a, b, cidx, "grad_pair_sum_" + nm) for a, b, nm in zip(stacks, got, names)]
    quads = _chip_exchange([p[1] for p in pairs])
    mine = [_chip_sum(q, p[0], chip, "grad_chip_sum_" + nm) for q, p, nm in zip(quads, pairs, names)]
    return mine, _sibling_swap(mine)


def _pad_cols(a, n):
    return jnp.concatenate([a, jnp.zeros((a.shape[0], n - a.shape[1]), a.dtype)], axis=1)


def _group_channels(a):
    lead = a.shape[:-1]
    xs = a[..., :D_INNER].reshape(lead + (SSM_GROUPS, GROUP_W))
    bs = a[..., D_INNER:D_INNER + SSM_GROUPS * STATE_N].reshape(lead + (SSM_GROUPS, STATE_N))
    cs = a[..., D_INNER + SSM_GROUPS * STATE_N:].reshape(lead + (SSM_GROUPS, STATE_N))
    return jnp.concatenate([xs, bs, cs], axis=-1).reshape(lead + (CONV_CH,))


def _lay_w_in(w):
    idx = np.cumsum(IN_SIZES)[:-1]
    segs = jnp.split(w, [int(v) for v in idx], axis=1)
    segs[2] = _pad_cols(segs[2], LANE)
    segs[4] = _group_channels(segs[4])
    segs[5] = _pad_cols(segs[5], LANE)
    return jnp.concatenate(segs, axis=1)


IN_PAD_SIZES = (Q_RANK, KV_RANK, LANE, D_INNER, CONV_CH, LANE, D_MODEL, D_MODEL)
IN_PAD_OFFS = [int(v) for v in np.cumsum(IN_PAD_SIZES)[:-1]]


@jax.custom_vjp
def split_proj(proj):
    return tuple(jnp.split(proj, IN_PAD_OFFS, axis=-1))


split_proj.defvjp(lambda proj: (tuple(jnp.split(proj, IN_PAD_OFFS, axis=-1)), None),
                  lambda _, cots: (jnp.concatenate(cots, axis=-1),))


def _lay_w_uq(w):
    w3 = w.reshape(Q_RANK, N_HEADS, NOPE + ROPE)
    w3 = jnp.concatenate([w3, jnp.zeros((Q_RANK, N_HEADS, QK_PAD - NOPE - ROPE), w.dtype)], axis=2)
    return w3.reshape(Q_RANK, N_HEADS * QK_PAD)


def _lay_w_ukv(w):
    w3 = w.reshape(KV_RANK, N_HEADS, NOPE + V_DIM)
    return jnp.concatenate([w3[:, :, :NOPE].reshape(KV_RANK, -1), w3[:, :, NOPE:].reshape(KV_RANK, -1)], axis=1)


def _pad_lanes(v, n=LANE):
    return jnp.concatenate([v, jnp.zeros((v.shape[0], n - v.shape[1]), v.dtype)], axis=1)


def _local_loss(toks, small, x, wb, c8, posf, target):
    B, S, D = x.shape
    T = B * S

    def lin(name, a, key, lay=lambda w: w, out_dtype=F32):
        return make_linear(name, out_dtype)(a, lay(wb[key]), lay(toks[key]))

    rows2 = lambda a: a.reshape(T, a.shape[-1])
    rows3 = lambda a: a.reshape(B, S, a.shape[-1])

    sc = make_rowwise("silu_c", _f_silu, 1, 0, 0, ('row',))((c8[None],), (), ())[0][0]
    mod = lin("ada", sc, 'w_ada')[:B] + small['b_ada']
    shift1, scale1, gate1, shift2, scale2, gate2 = [m[:, None, :] for m in jnp.split(mod, 6, axis=-1)]

    modulate = make_rowwise("modulate1", _f_modulate, 1, 2, 1, ('row',))
    h = modulate((x,), (scale1, shift1), (small['g_pre_mix'],))[0]
    proj = rows3(lin("w_in", rows2(h), 'w_in', _lay_w_in))
    q_lat, kv_lat, k_rope, z, xbc, dt_raw, gate_a, gate_b = split_proj(proj)

    inv = ROPE_THETA ** (-jnp.arange(ROPE // 2, dtype=F32) / (ROPE // 2))
    inv_lane = jnp.concatenate([inv, inv, jnp.zeros((LANE - ROPE,), F32)])[None]
    tabs = tuple(_rope_tables(posf, inv_lane))
    qn = make_rowwise("rms_q", _f_rms, 1, 0, 1, ('row',))((q_lat,), (), (small['g_q_lat'],))[0]
    kvn = make_rowwise("rms_kv", _f_rms, 1, 0, 1, ('row',))((kv_lat,), (), (small['g_kv_lat'],))[0]
    qp = rows3(lin("w_uq", rows2(qn), 'w_uq', _lay_w_uq))
    kvp = rows3(lin("w_ukv", rows2(kvn), 'w_ukv', _lay_w_ukv, BF16))
    qr = rope_q(qp, tabs)
    kr = build_k(kvp, k_rope, tabs)
    att = attention(qr, kr, kvp)
    attn = rows3(lin("w_o_attn", rows2(att), 'w_o_attn'))

    xa = conv_silu(xbc, _group_channels(wb['conv_w_f32']), _group_channels(small['conv_b']))
    dt_pad, a_pad = make_rowwise("dt_softplus", _f_dt, 1, 0, 2, ('row', 'row'))(
        (dt_raw,), (), (_pad_lanes(small['dt_bias']), _pad_lanes(small['a_log'])))
    ac_pad = chunk_cumsum(a_pad)
    acr = jnp.transpose(ac_pad[..., :SSM_HEADS], (0, 2, 1))[:, :, None, :]
    dsk = jnp.repeat(small['d_skip'], HEAD_P, axis=-1)
    y = ssd(xa, dt_pad, ac_pad, acr, dsk)
    yg = make_rowwise("gated_norm", _f_gated_norm, 2, 0, 1, ('row',), ncol=SSM_GROUPS)(
        (y, z), (), (small['g_ssm_out'],))[0]
    ssm = rows3(lin("w_o_ssm", rows2(yg), 'w_o_ssm'))

    merged = make_rowwise("merge", _f_merge, 4, 0, 0, ('row',))((attn, ssm, gate_a, gate_b), (), ())[0]
    mix = rows3(lin("w_out", rows2(merged), 'w_out'))
    x1 = make_rowwise("post_mix", _f_post, 2, 1, 1, ('row',))((x, mix), (gate1,), (small['g_post_mix'],))[0]

    h2 = make_rowwise("modulate2", _f_modulate, 1, 2, 1, ('row',))((x1,), (scale2, shift2), (small['g_pre_mlp'],))[0]
    u = rows3(lin("w_ff1", rows2(h2), 'w_ff1'))
    act = make_rowwise("relu2", _f_relu2, 1, 0, 0, ('row',))((u,), (), ())[0]
    ff = rows3(lin("w_ff2", rows2(act), 'w_ff2'))
    lvec = make_rowwise("final_loss", _f_final_loss, 3, 1, 1, ('sum',), nodiff=(2,))(
        (x1, ff, target), (gate2,), (small['g_post_mlp'],))[0]
    return jnp.sum(lvec)


MATRICES = COL_SHARDED + ROW_SHARDED


def _local_step(x, c, positions, target, wb, small):
    B = x.shape[0]
    c8 = jnp.concatenate([c, jnp.zeros((16 - B, c.shape[1]), F32)], axis=0)
    posf = positions.astype(F32)[..., None]
    toks = {k: jnp.zeros(wb[k].shape, F32) for k in MATRICES if k != 'conv_w'}
    conv_w = wb['conv_w_f32']

    def loss_fn(toks, small, conv_w, x):
        wbl = dict(wb)
        wbl['conv_w_f32'] = conv_w
        return _local_loss(toks, small, x, wbl, c8, posf, target)

    loss, (g_tok, g_small, g_conv, g_x) = jax.value_and_grad(loss_fn, argnums=(0, 1, 2, 3))(toks, small, conv_w, x)
    grads = dict(g_tok)
    grads.update(g_small)
    grads['conv_w'] = g_conv
    return loss, g_x, grads


def kernel(x, c, positions, w_ada, b_ada, g_pre_mix, g_post_mix, w_in, g_q_lat, g_kv_lat, w_uq, w_ukv, w_o_attn, conv_w, conv_b, dt_bias, a_log, d_skip, g_ssm_out, w_o_ssm, w_out, g_pre_mlp, g_post_mlp, w_ff1, w_ff2, loss_target, m_w_ada, m_b_ada, m_g_pre_mix, m_g_post_mix, m_w_in, m_g_q_lat, m_g_kv_lat, m_w_uq, m_w_ukv, m_w_o_attn, m_conv_w, m_conv_b, m_dt_bias, m_a_log, m_d_skip, m_g_ssm_out, m_w_o_ssm, m_w_out, m_g_pre_mlp, m_g_post_mlp, m_w_ff1, m_w_ff2, v_w_ada, v_b_ada, v_g_pre_mix, v_g_post_mix, v_w_in, v_g_q_lat, v_g_kv_lat, v_w_uq, v_w_ukv, v_w_o_attn, v_conv_w, v_conv_b, v_dt_bias, v_a_log, v_d_skip, v_g_ssm_out, v_w_o_ssm, v_w_out, v_g_pre_mlp, v_g_post_mlp, v_w_ff1, v_w_ff2):
    given = dict(locals())
    w_loc = {n: given[n] for n in WEIGHTS}
    m_loc = {n: given["m_" + n] for n in WEIGHTS}
    v_loc = {n: given["v_" + n] for n in WEIGHTS}
    mats = [n for n in WEIGHTS if n in MATRICES and n != 'conv_w']
    vecs = [n for n in WEIGHTS if n not in MATRICES]

    g_mats, g_conv = _gather_weights([w_loc[n][0].astype(BF16) for n in mats], conv_w[0])
    wb = {}
    for n, g in zip(mats, g_mats):
        if n in COL_SHARDED:
            wb[n] = jnp.transpose(g, (1, 0, 2)).reshape(g.shape[1], -1)
        else:
            wb[n] = g.reshape(-1, g.shape[2])
    wb['conv_w_f32'] = jnp.transpose(g_conv, (1, 0, 2)).reshape(CONV_K, -1)
    small = {n: w_loc[n] for n in vecs}

    loss_part, grad_x, grads = _local_step(x, c, positions, loss_target, wb, small)
    loss = lax.psum(loss_part, ("x", "y", "c"))

    stacks = []
    for n in mats:
        kk, nn = w_loc[n].shape[1:]
        if n in COL_SHARDED:
            stacks.append(jnp.transpose(grads[n].reshape(kk, 4, nn), (1, 0, 2)))
        else:
            stacks.append(grads[n].reshape(4, kk, nn))
    g_mine, g_other = _reduce_matrices(stacks, mats)
    g_shard = {}

    vec_shapes = [tuple(grads[n].shape) for n in vecs] + [tuple(grads['conv_w'].shape)]
    total = _stack_sum(_gather_small(_pack_small([grads[n] for n in vecs] + [grads['conv_w']])), "grad_sum_small")
    g_vec = _unpack_small(total, vec_shapes)
    n_conv = conv_w.shape[2]
    chip = 2 * lax.axis_index("x") + lax.axis_index("y")
    g_shard['conv_w'] = lax.dynamic_slice_in_dim(g_vec[-1], chip * n_conv, n_conv, axis=1)
    for n, g in zip(vecs, g_vec):
        g_shard[n] = g

    delta, new_m, new_v = {}, {}, {}
    cidx = lax.axis_index("c").astype(jnp.int32).reshape(1)
    for n, mine, other in zip(mats, g_mine, g_other):
        g_shard[n], delta[n], new_m[n], new_v[n] = _adam_halves_call(
            w_loc[n][0], mine, other, cidx, m_loc[n][0], v_loc[n][0], "adamw_" + n)
    rest = vecs + ['conv_w']
    rest_shapes = [tuple(w_loc[n].shape) for n in rest]
    packed = [_pack_small([src[n] for n in rest]) for src in (w_loc, g_shard, m_loc, v_loc)]
    for dst, buf in zip((delta, new_m, new_v), _adam_call(*packed, "adamw_small")):
        dst.update(zip(rest, _unpack_small(buf, rest_shapes)))

    def out(d):
        return [d[n].reshape(w_loc[n].shape) for n in WEIGHTS]

    return (loss, grad_x, *out(g_shard), *out(delta), *out(new_m), *out(new_v))
```

```python
import functools
import math

import numpy as np
import jax
import jax.numpy as jnp
from jax import lax
from jax.experimental import pallas as pl
from jax.experimental.pallas import tpu as pltpu

F32 = jnp.float32
BF16 = jnp.bfloat16
MESH = pl.DeviceIdType.MESH

D_MODEL = 1024
N_HEADS = 8
NOPE = 128
ROPE = 64
V_DIM = 128
Q_RANK = 256
KV_RANK = 256
ROPE_THETA = 10000.0
D_INNER = 2048
SSM_HEADS = 32
SSM_GROUPS = 8
HEAD_P = 64
STATE_N = 128
CONV_K = 4
CHUNK = 128
CONV_CH = D_INNER + 2 * SSM_GROUPS * STATE_N
D_FF = 4096
EPS = 1e-6
IN_SIZES = (Q_RANK, KV_RANK, ROPE, D_INNER, CONV_CH, SSM_HEADS, D_MODEL, D_MODEL)
ADAM_LR, ADAM_B1, ADAM_B2, ADAM_EPS, ADAM_WD, ADAM_STEP = 0.001, 0.9, 0.999, 1e-08, 0.01, 10

VMEM_LIMIT_BYTES = 52 * 1024 * 1024
LANE = 128
QK_PAD = 256

WEIGHTS = ['w_ada', 'b_ada', 'g_pre_mix', 'g_post_mix', 'w_in', 'g_q_lat', 'g_kv_lat', 'w_uq', 'w_ukv',
           'w_o_attn', 'conv_w', 'conv_b', 'dt_bias', 'a_log', 'd_skip', 'g_ssm_out', 'w_o_ssm', 'w_out',
           'g_pre_mlp', 'g_post_mlp', 'w_ff1', 'w_ff2']
COL_SHARDED = ('w_ada', 'w_in', 'w_uq', 'w_ukv', 'conv_w', 'w_ff1')
ROW_SHARDED = ('w_o_attn', 'w_o_ssm', 'w_out', 'w_ff2')


def _cparams(sem):
    return pltpu.CompilerParams(dimension_semantics=sem, vmem_limit_bytes=VMEM_LIMIT_BYTES)


def _tile(n, cap):
    if n <= cap:
        return n
    k = n // LANE
    best = LANE
    for d in range(1, k + 1):
        if k % d == 0 and d * LANE <= cap:
            best = d * LANE
    return best


def _mm(a, w, name, out_dtype=F32, epilogue=None, extras=(), out_dtypes=None):
    M, K = a.shape
    N = w.shape[1]
    tm = min(M, 1024)
    tn = _tile(N, 1024)
    tk = _tile(K, 2048)
    nk = K // tk
    dts = tuple(out_dtypes) if epilogue is not None else (out_dtype,)
    n_x, n_o = len(extras), len(dts)

    def finish(acc, refs):
        res = epilogue(acc, *[r[...] for r in refs[:n_x]]) if epilogue is not None else (acc,)
        for o_ref, val, dt in zip(refs[n_x:n_x + n_o], res, dts):
            o_ref[...] = val.astype(dt)

    def body(a_ref, w_ref, *refs):
        part = jnp.dot(a_ref[...].astype(BF16), w_ref[...], preferred_element_type=F32)
        if nk == 1:
            finish(part, refs)
        else:
            acc_ref = refs[-1]
            k = pl.program_id(2)

            @pl.when(k == 0)
            def _():
                acc_ref[...] = part

            @pl.when(k > 0)
            def _():
                acc_ref[...] += part

            @pl.when(k == nk - 1)
            def _():
                finish(acc_ref[...], refs)

    ospec = pl.BlockSpec((tm, tn), lambda i, j, k: (i, j))
    res = pl.pallas_call(
        body, grid=(M // tm, N // tn, nk),
        in_specs=[pl.BlockSpec((tm, tk), lambda i, j, k: (i, k)), pl.BlockSpec((tk, tn), lambda i, j, k: (k, j))]
        + [ospec] * n_x,
        out_specs=[ospec] * n_o, out_shape=[jax.ShapeDtypeStruct((M, N), dt) for dt in dts],
        scratch_shapes=[pltpu.VMEM((tm, tn), F32)] if nk > 1 else [], name=name,
        compiler_params=_cparams(("parallel", "parallel", "arbitrary")))(a, w, *extras)
    return res if epilogue is not None else res[0]


def _mm_tn(a, g, name):
    M, K = a.shape
    N = g.shape[1]
    tm = min(M, 1024)
    tk = _tile(K, 1024)
    tn = _tile(N, 1024)
    nm = M // tm

    def body(a_ref, g_ref, o_ref):
        part = lax.dot_general(a_ref[...].astype(BF16), g_ref[...].astype(BF16), (((0,), (0,)), ((), ())),
                               preferred_element_type=F32)
        m = pl.program_id(2)

        @pl.when(m == 0)
        def _():
            o_ref[...] = part

        @pl.when(m > 0)
        def _():
            o_ref[...] += part

    return pl.pallas_call(
        body, grid=(K // tk, N // tn, nm),
        in_specs=[pl.BlockSpec((tm, tk), lambda i, j, m: (m, i)), pl.BlockSpec((tm, tn), lambda i, j, m: (m, j))],
        out_specs=pl.BlockSpec((tk, tn), lambda i, j, m: (i, j)),
        out_shape=jax.ShapeDtypeStruct((K, N), F32), name=name,
        compiler_params=_cparams(("parallel", "parallel", "arbitrary")))(a, g)


def make_linear(name, out_dtype=F32):
    @jax.custom_vjp
    def linear(a, w, tok):
        return _mm(a, w, name + "_fwd", out_dtype)

    def fwd(a, w, tok):
        return _mm(a, w, name + "_fwd", out_dtype), (a, w)

    def bwd(res, g):
        a, w = res
        da = _mm(g, w.T, name + "_dx", a.dtype)
        dw = _mm_tn(a, g, name + "_dw")
        return da, jnp.zeros_like(w), dw

    linear.defvjp(fwd, bwd)
    return linear


def _relu2_epilogue(acc):
    r = jnp.maximum(acc, 0.0)
    return r * r, r


def _relu2_bwd_epilogue(acc, r):
    return (acc * (2.0 * r.astype(F32)),)


@jax.custom_vjp
def ffn(h, w1, tok1, w2, tok2):
    act, _ = _mm(h, w1, "w_ff1_fwd", epilogue=_relu2_epilogue, out_dtypes=(BF16, BF16))
    return _mm(act, w2, "w_ff2_fwd")


def _ffn_fwd(h, w1, tok1, w2, tok2):
    act, r = _mm(h, w1, "w_ff1_fwd", epilogue=_relu2_epilogue, out_dtypes=(BF16, BF16))
    return _mm(act, w2, "w_ff2_fwd"), (h, w1, w2, act, r)


def _ffn_bwd(res, g):
    h, w1, w2, act, r = res
    du = _mm(g, w2.T, "w_ff2_dx", epilogue=_relu2_bwd_epilogue, extras=(r,), out_dtypes=(BF16,))[0]
    dw2 = _mm_tn(act, g, "w_ff2_dw")
    dw1 = _mm_tn(h, du, "w_ff1_dw")
    dh = _mm(du, w1.T, "w_ff1_dx", h.dtype)
    return dh, jnp.zeros_like(w1), dw1, jnp.zeros_like(w2), dw2


ffn.defvjp(_ffn_fwd, _ffn_bwd)


def make_rowwise(name, f, n_rows, n_seqs, n_pars, out_kinds, ncol=1, nodiff=(), ts_cap=512):
    n_in = n_rows + n_seqs + n_pars
    diff_idx = [i for i in range(n_in) if i not in nodiff]

    def _dims(rows):
        B, S = rows[0].shape[0], rows[0].shape[1]
        ts = min(S, ts_cap)
        return B, S, ts

    def _in_specs(rows, seqs, pars, ts):
        specs = []
        for r in rows:
            specs.append(pl.BlockSpec((1, ts, r.shape[2] // ncol), lambda k, b, s: (b, s, k)))
        for q in seqs:
            specs.append(pl.BlockSpec((1, 1, q.shape[2] // ncol), lambda k, b, s: (b, 0, k)))
        for p in pars:
            specs.append(pl.BlockSpec((1, p.shape[1] // ncol), lambda k, b, s: (0, k)))
        return specs

    def _load(refs):
        vals = [r[0] for r in refs[:n_rows + n_seqs]]
        vals += [r[...] for r in refs[n_rows + n_seqs:n_in]]
        return vals

    def _out_struct(rows, seqs, pars, ts):
        blocks = [jax.ShapeDtypeStruct((ts, r.shape[2] // ncol), r.dtype) for r in rows]
        blocks += [jax.ShapeDtypeStruct((1, q.shape[2] // ncol), q.dtype) for q in seqs]
        blocks += [jax.ShapeDtypeStruct((1, p.shape[1] // ncol), p.dtype) for p in pars]
        return jax.eval_shape(f, *blocks)

    def _fwd_call(rows, seqs, pars):
        B, S, ts = _dims(rows)
        outs = _out_struct(rows, seqs, pars, ts)
        n_out = len(outs)

        def body(*refs):
            res = f(*_load(refs))
            first = (pl.program_id(1) == 0) & (pl.program_id(2) == 0)
            for o_ref, val, kind in zip(refs[n_in:], res, out_kinds):
                if kind == 'row':
                    o_ref[0] = val
                else:
                    tot = jnp.sum(val, axis=0, keepdims=True)

                    @pl.when(first)
                    def _(o_ref=o_ref, tot=tot):
                        o_ref[...] = tot

                    @pl.when(jnp.logical_not(first))
                    def _(o_ref=o_ref, tot=tot):
                        o_ref[...] += tot

        out_shape, out_specs = [], []
        for o, kind in zip(outs, out_kinds):
            d = o.shape[1]
            if kind == 'row':
                out_shape.append(jax.ShapeDtypeStruct((B, S, ncol * d), o.dtype))
                out_specs.append(pl.BlockSpec((1, ts, d), lambda k, b, s: (b, s, k)))
            else:
                out_shape.append(jax.ShapeDtypeStruct((1, ncol * d), o.dtype))
                out_specs.append(pl.BlockSpec((1, d), lambda k, b, s: (0, k)))
        res = pl.pallas_call(
            body, grid=(ncol, B, S // ts), in_specs=_in_specs(rows, seqs, pars, ts), out_specs=out_specs,
            out_shape=out_shape, name=name + "_fwd",
            compiler_params=_cparams(("arbitrary", "arbitrary", "arbitrary")))(*rows, *seqs, *pars)
        return tuple(res)

    def _bwd_call(rows, seqs, pars, cots):
        B, S, ts = _dims(rows)
        outs = _out_struct(rows, seqs, pars, ts)
        n_out = len(outs)
        all_in = list(rows) + list(seqs) + list(pars)

        def body(*refs):
            vals = _load(refs)
            cts = []
            for c_ref, o, kind in zip(refs[n_in:n_in + n_out], outs, out_kinds):
                if kind == 'row':
                    cts.append(c_ref[0])
                else:
                    cts.append(jnp.broadcast_to(c_ref[...], o.shape))

            def g(*dv):
                full = list(vals)
                for i, v in zip(diff_idx, dv):
                    full[i] = v
                return tuple(f(*full))

            _, vjp = jax.vjp(g, *[vals[i] for i in diff_idx])
            grads = vjp(tuple(cts))
            b, s = pl.program_id(1), pl.program_id(2)
            for o_ref, i, gr in zip(refs[n_in + n_out:], diff_idx, grads):
                if i < n_rows:
                    o_ref[0] = gr
                else:
                    first = (s == 0) if i < n_rows + n_seqs else ((b == 0) & (s == 0))
                    target = (lambda r: r.at[0]) if i < n_rows + n_seqs else (lambda r: r)

                    @pl.when(first)
                    def _(o_ref=o_ref, gr=gr, target=target):
                        target(o_ref)[...] = gr

                    @pl.when(jnp.logical_not(first))
                    def _(o_ref=o_ref, gr=gr, target=target):
                        target(o_ref)[...] += gr

        cot_specs = []
        for o, kind in zip(outs, out_kinds):
            d = o.shape[1]
            if kind == 'row':
                cot_specs.append(pl.BlockSpec((1, ts, d), lambda k, b, s: (b, s, k)))
            else:
                cot_specs.append(pl.BlockSpec((1, d), lambda k, b, s: (0, k)))
        out_shape, out_specs = [], []
        for i in diff_idx:
            a = all_in[i]
            out_shape.append(jax.ShapeDtypeStruct(a.shape, a.dtype))
            if i < n_rows:
                out_specs.append(pl.BlockSpec((1, ts, a.shape[2] // ncol), lambda k, b, s: (b, s, k)))
            elif i < n_rows + n_seqs:
                out_specs.append(pl.BlockSpec((1, 1, a.shape[2] // ncol), lambda k, b, s: (b, 0, k)))
            else:
                out_specs.append(pl.BlockSpec((1, a.shape[1] // ncol), lambda k, b, s: (0, k)))
        res = pl.pallas_call(
            body, grid=(ncol, B, S // ts), in_specs=_in_specs(rows, seqs, pars, ts) + cot_specs,
            out_specs=out_specs, out_shape=out_shape, name=name + "_bwd",
            compiler_params=_cparams(("arbitrary", "arbitrary", "arbitrary")))(*all_in, *cots)
        grads = [None] * n_in
        for i, r in zip(diff_idx, res):
            grads[i] = r
        for i in nodiff:
            grads[i] = jnp.zeros_like(all_in[i])
        return tuple(grads[:n_rows]), tuple(grads[n_rows:n_rows + n_seqs]), tuple(grads[n_rows + n_seqs:])

    @jax.custom_vjp
    def op(rows, seqs, pars):
        return _fwd_call(rows, seqs, pars)

    def fwd(rows, seqs, pars):
        return _fwd_call(rows, seqs, pars), (rows, seqs, pars)

    def bwd(res, cots):
        rows, seqs, pars = res
        return _bwd_call(rows, seqs, pars, cots)

    op.defvjp(fwd, bwd)
    return op


def _rms(x, g):
    return x * lax.rsqrt(jnp.mean(x * x, axis=-1, keepdims=True) + EPS) * g


def _silu(x):
    return x * lax.logistic(x)


def _f_silu(c):
    return (_silu(c),)


def _f_modulate(x, scale, shift, g):
    return ((_rms(x, g) * (1.0 + scale) + shift).astype(BF16),)


def _f_rms(x, g):
    return (_rms(x, g).astype(BF16),)


def _f_dt(dt_raw, dt_bias, a_log):
    z = dt_raw + dt_bias
    dt = jnp.maximum(z, 0.0) + jnp.log1p(jnp.exp(-jnp.abs(z)))
    return dt, dt * (-jnp.exp(a_log))


def _f_gated_norm(y, z, g):
    return (_rms(y * _silu(z), g).astype(BF16),)


def _f_merge(attn, ssm, ga, gb):
    return ((lax.logistic(ga) * attn + lax.logistic(gb) * ssm).astype(BF16),)


def _f_post(x, m, gate, g):
    return (x + gate * _rms(m, g),)


def _f_final_loss(x, ff, target, gate, g):
    e = x + gate * _rms(ff, g) - target
    return (e * e * (0.5 / D_MODEL),)


def _rope_tables(posf, inv_lane):
    B, S, _ = posf.shape
    ts = min(S, 512)

    def body(p_ref, inv_ref, c_ref, a_ref, b_ref):
        ang = p_ref[0] * inv_ref[...]
        cs, sn = jnp.cos(ang), jnp.sin(ang)
        lane = lax.broadcasted_iota(jnp.int32, ang.shape, 1)
        c_ref[0] = jnp.where(lane < ROPE, cs, 0.0)
        a_ref[0] = jnp.where(lane < ROPE // 2, -sn, 0.0)
        b_ref[0] = jnp.where((lane >= ROPE // 2) & (lane < ROPE), sn, 0.0)

    spec = pl.BlockSpec((1, ts, LANE), lambda b, s: (b, s, 0))
    sds = jax.ShapeDtypeStruct((B, S, LANE), F32)
    return pl.pallas_call(
        body, grid=(B, S // ts),
        in_specs=[pl.BlockSpec((1, ts, 1), lambda b, s: (b, s, 0)), pl.BlockSpec((1, LANE), lambda b, s: (0, 0))],
        out_specs=[spec, spec, spec], out_shape=[sds, sds, sds], name="rope_tables",
        compiler_params=_cparams(("parallel", "parallel")))(posf, inv_lane)


def _rot(u, c, a, bm):
    return u * c + pltpu.roll(u, 96, 1) * a + pltpu.roll(u, 32, 1) * bm


def _rot_t(g, c, a, bm):
    return g * c + pltpu.roll(g * a, 32, 1) + pltpu.roll(g * bm, 96, 1)


def _rope_q_call(q, tabs, transpose, name):
    B, S, W = q.shape
    ts = min(S, 512)
    fn = _rot_t if transpose else _rot
    out_dtype = F32 if transpose else BF16

    def body(q_ref, c_ref, a_ref, b_ref, o_ref):
        u = q_ref[0].astype(F32) * ATT_SCALE
        r = fn(u[:, NOPE:], c_ref[0], a_ref[0], b_ref[0])
        o_ref[0] = jnp.concatenate([u[:, :NOPE], r], axis=1).astype(out_dtype)

    tspec = pl.BlockSpec((1, ts, LANE), lambda b, s, h: (b, s, 0))
    qspec = pl.BlockSpec((1, ts, QK_PAD), lambda b, s, h: (b, s, h))
    return pl.pallas_call(
        body, grid=(B, S // ts, W // QK_PAD), in_specs=[qspec, tspec, tspec, tspec], out_specs=qspec,
        out_shape=jax.ShapeDtypeStruct(q.shape, out_dtype), name=name,
        compiler_params=_cparams(("parallel", "parallel", "parallel")))(q, *tabs)


@jax.custom_vjp
def rope_q(q, tabs):
    return _rope_q_call(q, tabs, False, "rope_q_fwd")


def _rope_q_fwd(q, tabs):
    return _rope_q_call(q, tabs, False, "rope_q_fwd"), tabs


def _rope_q_bwd(tabs, g):
    return _rope_q_call(g, tabs, True, "rope_q_bwd"), tuple(jnp.zeros_like(t) for t in tabs)


rope_q.defvjp(_rope_q_fwd, _rope_q_bwd)


def _build_k_fwd_call(kv, kr, tabs):
    B, S, _ = kv.shape
    ts = min(S, 512)

    def body(kv_ref, kr_ref, c_ref, a_ref, b_ref, o_ref):
        r = _rot(kr_ref[0], c_ref[0], a_ref[0], b_ref[0])
        o_ref[0] = jnp.concatenate([kv_ref[0], r.astype(BF16)], axis=1)

    tspec = pl.BlockSpec((1, ts, LANE), lambda b, s, h: (b, s, 0))
    return pl.pallas_call(
        body, grid=(B, S // ts, N_HEADS),
        in_specs=[pl.BlockSpec((1, ts, LANE), lambda b, s, h: (b, s, h)), tspec, tspec, tspec, tspec],
        out_specs=pl.BlockSpec((1, ts, QK_PAD), lambda b, s, h: (b, s, h)),
        out_shape=jax.ShapeDtypeStruct((B, S, N_HEADS * QK_PAD), BF16), name="build_k_fwd",
        compiler_params=_cparams(("parallel", "parallel", "arbitrary")))(kv, kr, *tabs)


def _build_k_bwd_call(g, tabs):
    B, S, _ = g.shape
    ts = min(S, 512)

    def body(g_ref, c_ref, a_ref, b_ref, dk_ref, dr_ref):
        gg = g_ref[0]
        dk_ref[0] = gg[:, :NOPE]
        r = _rot_t(gg[:, NOPE:].astype(F32), c_ref[0], a_ref[0], b_ref[0])
        h = pl.program_id(2)

        @pl.when(h == 0)
        def _():
            dr_ref[0] = r

        @pl.when(h > 0)
        def _():
            dr_ref[0] += r

    tspec = pl.BlockSpec((1, ts, LANE), lambda b, s, h: (b, s, 0))
    return pl.pallas_call(
        body, grid=(B, S // ts, N_HEADS),
        in_specs=[pl.BlockSpec((1, ts, QK_PAD), lambda b, s, h: (b, s, h)), tspec, tspec, tspec],
        out_specs=[pl.BlockSpec((1, ts, LANE), lambda b, s, h: (b, s, h)), tspec],
        out_shape=[jax.ShapeDtypeStruct((B, S, N_HEADS * NOPE), BF16), jax.ShapeDtypeStruct((B, S, LANE), F32)],
        name="build_k_bwd", compiler_params=_cparams(("parallel", "parallel", "arbitrary")))(g, *tabs)


@jax.custom_vjp
def build_k(kv, kr, tabs):
    return _build_k_fwd_call(kv, kr, tabs)


def _build_k_fwd(kv, kr, tabs):
    return _build_k_fwd_call(kv, kr, tabs), (tabs, kv.shape)


def _build_k_bwd(res, g):
    tabs, kv_shape = res
    dk, dr = _build_k_bwd_call(g, tabs)
    dkv = jnp.concatenate([dk, jnp.zeros((kv_shape[0], kv_shape[1], kv_shape[2] - dk.shape[2]), BF16)], axis=-1)
    return dkv, dr, tuple(jnp.zeros_like(t) for t in tabs)


build_k.defvjp(_build_k_fwd, _build_k_bwd)


ATT_SCALE = (NOPE + ROPE) ** -0.5
NEG = -1e30


def _att_tiles(S):
    t = min(S, 512)
    return t, S // t


def _scores(q, k, diagonal):
    s = lax.dot_general(q, k, (((1,), (1,)), ((), ())), preferred_element_type=F32)
    if diagonal:
        row = lax.broadcasted_iota(jnp.int32, s.shape, 0)
        col = lax.broadcasted_iota(jnp.int32, s.shape, 1)
        s = jnp.where(col <= row, s, NEG)
    return s


ATT_HB = 4


def _causal_pairs(n, by_key):
    if by_key:
        pairs = [(i, j) for j in range(n) for i in range(j, n)]
    else:
        pairs = [(i, j) for i in range(n) for j in range(i + 1)]
    return (jnp.asarray([p[0] for p in pairs], jnp.int32), jnp.asarray([p[1] for p in pairs], jnp.int32))


def _head(ref_or_val, h, w):
    return ref_or_val[:, h * w:(h + 1) * w]


def _attn_fwd_call(q, k, vsrc, v_blk0):
    B, S, _ = q.shape
    t, n = _att_tiles(S)
    qi, kj = _causal_pairs(n, False)

    def body(qi_ref, kj_ref, q_ref, k_ref, v_ref, o_ref, lse_ref, m_sc, l_sc, acc_sc):
        p_id = pl.program_id(2)
        i, j = qi_ref[p_id], kj_ref[p_id]

        @pl.when(j == 0)
        def _():
            m_sc[...] = jnp.full(m_sc.shape, NEG, F32)
            l_sc[...] = jnp.zeros(l_sc.shape, F32)
            acc_sc[...] = jnp.zeros(acc_sc.shape, F32)

        def step(diagonal):
            qa, ka, va = q_ref[0], k_ref[0], v_ref[0]
            for h in range(ATT_HB):
                lanes = slice(h * LANE, (h + 1) * LANE)
                s = _scores(_head(qa, h, QK_PAD), _head(ka, h, QK_PAD), diagonal)
                m_prev = m_sc[:, lanes]
                m_new = jnp.maximum(m_prev, jnp.max(s, axis=1, keepdims=True))
                alpha = jnp.exp(m_prev - m_new)
                p = jnp.exp(s - jnp.tile(m_new, (1, t // LANE)))
                l_sc[:, lanes] = alpha * l_sc[:, lanes] + jnp.sum(p, axis=1, keepdims=True)
                acc_sc[:, lanes] = alpha * acc_sc[:, lanes] + jnp.dot(p.astype(BF16), _head(va, h, V_DIM),
                                                                      preferred_element_type=F32)
                m_sc[:, lanes] = m_new

        @pl.when(j < i)
        def _():
            step(False)

        @pl.when(j == i)
        def _():
            step(True)
            o_ref[0] = acc_sc[...] / l_sc[...]
            lse_ref[0] = m_sc[...] + jnp.log(l_sc[...])

    wq, wv = ATT_HB * QK_PAD, ATT_HB * V_DIM
    grid_spec = pltpu.PrefetchScalarGridSpec(
        num_scalar_prefetch=2, grid=(B, N_HEADS // ATT_HB, qi.shape[0]),
        in_specs=[pl.BlockSpec((1, t, wq), lambda b, h, p, qi, kj: (b, qi[p], h)),
                  pl.BlockSpec((1, t, wq), lambda b, h, p, qi, kj: (b, kj[p], h)),
                  pl.BlockSpec((1, t, wv), lambda b, h, p, qi, kj: (b, kj[p], v_blk0 + h))],
        out_specs=[pl.BlockSpec((1, t, wv), lambda b, h, p, qi, kj: (b, qi[p], h)),
                   pl.BlockSpec((1, t, wv), lambda b, h, p, qi, kj: (b, qi[p], h))],
        scratch_shapes=[pltpu.VMEM((t, wv), F32), pltpu.VMEM((t, wv), F32), pltpu.VMEM((t, wv), F32)])
    return pl.pallas_call(
        body, grid_spec=grid_spec,
        out_shape=[jax.ShapeDtypeStruct((B, S, N_HEADS * V_DIM), F32),
                   jax.ShapeDtypeStruct((B, S, N_HEADS * LANE), F32)],
        name="attn_fwd", compiler_params=_cparams(("parallel", "parallel", "arbitrary")))(qi, kj, q, k, vsrc)


def _attn_p_ds(q, k, v, o, do, lse, diagonal, t):
    s = _scores(q, k, diagonal)
    p = jnp.exp(s - jnp.tile(lse, (1, t // LANE)))
    dp = lax.dot_general(do.astype(BF16), v, (((1,), (1,)), ((), ())), preferred_element_type=F32)
    delta = jnp.sum(do * o, axis=1, keepdims=True)
    ds = p * (dp - delta)
    return p, ds


def _attn_dkv_call(q, k, vsrc, v_blk0, o, do, lse):
    B, S, _ = q.shape
    t, n = _att_tiles(S)

    qi, kj = _causal_pairs(n, True)

    def body(qi_ref, kj_ref, q_ref, k_ref, v_ref, o_ref, do_ref, lse_ref, dk_ref, dv_ref, dk_sc, dv_sc):
        p_id = pl.program_id(2)
        i, j = qi_ref[p_id], kj_ref[p_id]

        @pl.when(i == j)
        def _():
            dk_sc[...] = jnp.zeros(dk_sc.shape, F32)
            dv_sc[...] = jnp.zeros(dv_sc.shape, F32)

        def step(diagonal):
            qa, ka, va, oa, doa, la = q_ref[0], k_ref[0], v_ref[0], o_ref[0], do_ref[0], lse_ref[0]
            for h in range(ATT_HB):
                qb, dob = _head(qa, h, QK_PAD), _head(doa, h, V_DIM)
                p, ds = _attn_p_ds(qb, _head(ka, h, QK_PAD), _head(va, h, V_DIM), _head(oa, h, V_DIM), dob,
                                   _head(la, h, LANE), diagonal, t)
                dv_sc[:, h * V_DIM:(h + 1) * V_DIM] += lax.dot_general(
                    p.astype(BF16), dob.astype(BF16), (((0,), (0,)), ((), ())), preferred_element_type=F32)
                dk_sc[:, h * QK_PAD:(h + 1) * QK_PAD] += lax.dot_general(
                    ds.astype(BF16), qb, (((0,), (0,)), ((), ())), preferred_element_type=F32)

        @pl.when(i > j)
        def _():
            step(False)

        @pl.when(i == j)
        def _():
            step(True)

        @pl.when(i == n - 1)
        def _():
            dk_ref[0] = dk_sc[...].astype(BF16)
            dv_ref[0] = dv_sc[...].astype(BF16)

    wq, wv = ATT_HB * QK_PAD, ATT_HB * V_DIM
    at_q = lambda b, h, p, qi, kj: (b, qi[p], h)
    at_k = lambda b, h, p, qi, kj: (b, kj[p], h)
    grid_spec = pltpu.PrefetchScalarGridSpec(
        num_scalar_prefetch=2, grid=(B, N_HEADS // ATT_HB, qi.shape[0]),
        in_specs=[pl.BlockSpec((1, t, wq), at_q), pl.BlockSpec((1, t, wq), at_k),
                  pl.BlockSpec((1, t, wv), lambda b, h, p, qi, kj: (b, kj[p], v_blk0 + h)),
                  pl.BlockSpec((1, t, wv), at_q), pl.BlockSpec((1, t, wv), at_q), pl.BlockSpec((1, t, wv), at_q)],
        out_specs=[pl.BlockSpec((1, t, wq), at_k), pl.BlockSpec((1, t, wv), at_k)],
        scratch_shapes=[pltpu.VMEM((t, wq), F32), pltpu.VMEM((t, wv), F32)])
    return pl.pallas_call(
        body, grid_spec=grid_spec,
        out_shape=[jax.ShapeDtypeStruct((B, S, N_HEADS * QK_PAD), BF16),
                   jax.ShapeDtypeStruct((B, S, N_HEADS * V_DIM), BF16)],
        name="attn_dkv", compiler_params=_cparams(("parallel", "parallel", "arbitrary")))(
            qi, kj, q, k, vsrc, o, do, lse)


def _attn_dq_call(q, k, vsrc, v_blk0, o, do, lse):
    B, S, _ = q.shape
    t, n = _att_tiles(S)

    qi, kj = _causal_pairs(n, False)

    def body(qi_ref, kj_ref, q_ref, k_ref, v_ref, o_ref, do_ref, lse_ref, dq_ref, dq_sc):
        p_id = pl.program_id(2)
        i, j = qi_ref[p_id], kj_ref[p_id]

        @pl.when(j == 0)
        def _():
            dq_sc[...] = jnp.zeros(dq_sc.shape, F32)

        def step(diagonal):
            qa, ka, va, oa, doa, la = q_ref[0], k_ref[0], v_ref[0], o_ref[0], do_ref[0], lse_ref[0]
            for h in range(ATT_HB):
                kb = _head(ka, h, QK_PAD)
                _, ds = _attn_p_ds(_head(qa, h, QK_PAD), kb, _head(va, h, V_DIM), _head(oa, h, V_DIM),
                                   _head(doa, h, V_DIM), _head(la, h, LANE), diagonal, t)
                dq_sc[:, h * QK_PAD:(h + 1) * QK_PAD] += jnp.dot(ds.astype(BF16), kb, preferred_element_type=F32)

        @pl.when(j < i)
        def _():
            step(False)

        @pl.when(j == i)
        def _():
            step(True)
            dq_ref[0] = dq_sc[...].astype(BF16)

    wq, wv = ATT_HB * QK_PAD, ATT_HB * V_DIM
    at_q = lambda b, h, p, qi, kj: (b, qi[p], h)
    at_k = lambda b, h, p, qi, kj: (b, kj[p], h)
    grid_spec = pltpu.PrefetchScalarGridSpec(
        num_scalar_prefetch=2, grid=(B, N_HEADS // ATT_HB, qi.shape[0]),
        in_specs=[pl.BlockSpec((1, t, wq), at_q), pl.BlockSpec((1, t, wq), at_k),
                  pl.BlockSpec((1, t, wv), lambda b, h, p, qi, kj: (b, kj[p], v_blk0 + h)),
                  pl.BlockSpec((1, t, wv), at_q), pl.BlockSpec((1, t, wv), at_q), pl.BlockSpec((1, t, wv), at_q)],
        out_specs=pl.BlockSpec((1, t, wq), at_q),
        scratch_shapes=[pltpu.VMEM((t, wq), F32)])
    return pl.pallas_call(
        body, grid_spec=grid_spec, out_shape=jax.ShapeDtypeStruct((B, S, N_HEADS * QK_PAD), BF16),
        name="attn_dq", compiler_params=_cparams(("parallel", "parallel", "arbitrary")))(
            qi, kj, q, k, vsrc, o, do, lse)


@jax.custom_vjp
def attention(q, k, kv):
    return _attn_fwd_call(q, k, kv, N_HEADS // ATT_HB)[0]


def _attention_fwd(q, k, kv):
    o, lse = _attn_fwd_call(q, k, kv, N_HEADS // ATT_HB)
    return o, (q, k, kv, o, lse)


def _attention_bwd(res, do):
    q, k, kv, o, lse = res
    dk, dv = _attn_dkv_call(q, k, kv, N_HEADS // ATT_HB, o, do, lse)
    dq = _attn_dq_call(q, k, kv, N_HEADS // ATT_HB, o, do, lse)
    dkv = jnp.concatenate([jnp.zeros_like(dv), dv], axis=-1)
    return dq, dk, dkv


attention.defvjp(_attention_fwd, _attention_bwd)


def _shift_down(v, sh, rows):
    return jnp.where(rows >= sh, pltpu.roll(v, sh, 0), 0.0)


def _shift_up(v, sh, rows, S):
    return jnp.where(rows < S - sh, pltpu.roll(v, S - sh, 0), 0.0)


def _conv_pre(u, w_ref, b_ref, rows):
    acc = b_ref[...] + w_ref[pl.ds(CONV_K - 1, 1), :] * u
    for k in range(CONV_K - 1):
        acc = acc + w_ref[pl.ds(k, 1), :] * _shift_down(u, CONV_K - 1 - k, rows)
    return acc


def _conv_fwd_call(u, w, b):
    B, S, C = u.shape

    def body(u_ref, w_ref, b_ref, o_ref):
        uu = u_ref[0]
        rows = lax.broadcasted_iota(jnp.int32, uu.shape, 0)
        o_ref[0] = _silu(_conv_pre(uu, w_ref, b_ref, rows))

    spec = pl.BlockSpec((1, S, LANE), lambda c, bb: (bb, 0, c))
    return pl.pallas_call(
        body, grid=(C // LANE, B),
        in_specs=[spec, pl.BlockSpec((CONV_K, LANE), lambda c, bb: (0, c)), pl.BlockSpec((1, LANE), lambda c, bb: (0, c))],
        out_specs=spec, out_shape=jax.ShapeDtypeStruct(u.shape, F32), name="conv_fwd",
        compiler_params=_cparams(("parallel", "arbitrary")))(u, w, b)


def _conv_bwd_call(u, w, b, g):
    B, S, C = u.shape

    def body(u_ref, w_ref, b_ref, g_ref, du_ref, dw_ref, db_ref):
        uu = u_ref[0]
        rows = lax.broadcasted_iota(jnp.int32, uu.shape, 0)
        pre = _conv_pre(uu, w_ref, b_ref, rows)
        sg = lax.logistic(pre)
        dpre = g_ref[0] * sg * (1.0 + pre * (1.0 - sg))
        du = w_ref[pl.ds(CONV_K - 1, 1), :] * dpre
        dws = [None] * CONV_K
        dws[CONV_K - 1] = jnp.sum(dpre * uu, axis=0, keepdims=True)
        for k in range(CONV_K - 1):
            sh = CONV_K - 1 - k
            du = du + w_ref[pl.ds(k, 1), :] * _shift_up(dpre, sh, rows, S)
            dws[k] = jnp.sum(dpre * _shift_down(uu, sh, rows), axis=0, keepdims=True)
        du_ref[0] = du
        dbv = jnp.sum(dpre, axis=0, keepdims=True)
        first = pl.program_id(1) == 0

        @pl.when(first)
        def _():
            for k in range(CONV_K):
                dw_ref[pl.ds(k, 1), :] = dws[k]
            db_ref[...] = dbv

        @pl.when(jnp.logical_not(first))
        def _():
            for k in range(CONV_K):
                dw_ref[pl.ds(k, 1), :] += dws[k]
            db_ref[...] += dbv

    spec = pl.BlockSpec((1, S, LANE), lambda c, bb: (bb, 0, c))
    wspec = pl.BlockSpec((CONV_K, LANE), lambda c, bb: (0, c))
    bspec = pl.BlockSpec((1, LANE), lambda c, bb: (0, c))
    return pl.pallas_call(
        body, grid=(C // LANE, B), in_specs=[spec, wspec, bspec, spec], out_specs=[spec, wspec, bspec],
        out_shape=[jax.ShapeDtypeStruct(u.shape, F32), jax.ShapeDtypeStruct(w.shape, F32),
                   jax.ShapeDtypeStruct(b.shape, F32)],
        name="conv_bwd", compiler_params=_cparams(("parallel", "arbitrary")))(u, w, b, g)


@jax.custom_vjp
def conv_silu(u, w, b):
    return _conv_fwd_call(u, w, b)


def _conv_silu_fwd(u, w, b):
    return _conv_fwd_call(u, w, b), (u, w, b)


def _conv_silu_bwd(res, g):
    return tuple(_conv_bwd_call(*res, g))


conv_silu.defvjp(_conv_silu_fwd, _conv_silu_bwd)


def _chunk_cumsum_call(a, reverse, name):
    B, S, W = a.shape

    def body(a_ref, o_ref):
        r = lax.broadcasted_iota(jnp.int32, (CHUNK, CHUNK), 0)
        c = lax.broadcasted_iota(jnp.int32, (CHUNK, CHUNK), 1)
        tri = jnp.where((c >= r) if reverse else (c <= r), 1.0, 0.0).astype(F32)
        o_ref[0] = jnp.dot(tri, a_ref[0], preferred_element_type=F32, precision=lax.Precision.HIGHEST)

    spec = pl.BlockSpec((1, CHUNK, W), lambda b, c: (b, c, 0))
    return pl.pallas_call(body, grid=(B, S // CHUNK), in_specs=[spec], out_specs=spec,
                          out_shape=jax.ShapeDtypeStruct(a.shape, F32), name=name,
                          compiler_params=_cparams(("parallel", "parallel")))(a)


@jax.custom_vjp
def chunk_cumsum(a):
    return _chunk_cumsum_call(a, False, "chunk_cumsum_fwd")


chunk_cumsum.defvjp(lambda a: (_chunk_cumsum_call(a, False, "chunk_cumsum_fwd"), None),
                    lambda _, g: (_chunk_cumsum_call(g, True, "chunk_cumsum_bwd"),))


GROUP_W = 4 * HEAD_P
HPG = SSM_HEADS // SSM_GROUPS


def _ssd_masks():
    lane = lax.broadcasted_iota(jnp.int32, (1, GROUP_W), 1)
    return [((lane >= HEAD_P * j) & (lane < HEAD_P * (j + 1))).astype(F32) for j in range(HPG)]


def _ssd_decays(ac_cols, acr_ref, gi):
    r = lax.broadcasted_iota(jnp.int32, (CHUNK, CHUNK), 0)
    c = lax.broadcasted_iota(jnp.int32, (CHUNK, CHUNK), 1)
    return [jnp.exp(jnp.where(c <= r, ac_cols[j] - acr_ref[0, gi * HPG + j], NEG)) for j in range(HPG)]


def _ssd_cols(blk, g):
    lane = lax.broadcasted_iota(jnp.int32, blk.shape, 1)
    return [jnp.sum(jnp.where(lane == HPG * g + j, blk, 0.0), axis=1, keepdims=True) for j in range(HPG)]


def _ssd_spread(cols, masks):
    out = cols[0] * masks[0]
    for j in range(1, HPG):
        out = out + cols[j] * masks[j]
    return out


def _ssd_gather(val, cols, masks, g):
    lane = lax.broadcasted_iota(jnp.int32, (1, LANE), 1)
    out = jnp.zeros((CHUNK, LANE), F32)
    for j in range(HPG):
        tot = jnp.sum(val * masks[j], axis=1, keepdims=True)
        if cols is not None:
            tot = tot + cols[j]
        out = out + tot * (lane == HPG * g + j).astype(F32)
    return out


def _dot(a, b, dims):
    return lax.dot_general(a.astype(BF16), b.astype(BF16), (dims, ((), ())), preferred_element_type=F32)


NN = ((1,), (0,))
NT = ((1,), (1,))
TN = ((0,), (0,))


XBC_W = GROUP_W + 2 * STATE_N


SSD_GB = 2


def _ssd_load(xbc_ref, dt_ref, ac_ref, masks, g, gi):
    x = xbc_ref[0, :, gi * XBC_W:gi * XBC_W + GROUP_W]
    bm = xbc_ref[0, :, gi * XBC_W + GROUP_W:gi * XBC_W + GROUP_W + STATE_N]
    cm = xbc_ref[0, :, gi * XBC_W + GROUP_W + STATE_N:(gi + 1) * XBC_W]
    ac_cols = _ssd_cols(ac_ref[0], g)
    dt = _ssd_spread(_ssd_cols(dt_ref[0], g), masks)
    ac = _ssd_spread(ac_cols, masks)
    is_last = (lax.broadcasted_iota(jnp.int32, (CHUNK, GROUP_W), 0) == CHUNK - 1).astype(F32)
    return x, bm, cm, dt, ac, ac_cols, is_last


def _ssd_in_specs(nc, rev):
    cc = (lambda c: nc - 1 - c) if rev else (lambda c: c)
    return [pl.BlockSpec((1, CHUNK, SSD_GB * XBC_W), lambda b, g, c: (b, cc(c), g)),
            pl.BlockSpec((1, CHUNK, LANE), lambda b, g, c: (b, cc(c), 0)),
            pl.BlockSpec((1, CHUNK, LANE), lambda b, g, c: (b, cc(c), 0)),
            pl.BlockSpec((1, SSD_GB * HPG, 1, CHUNK), lambda b, g, c: (b, g, 0, cc(c))),
            pl.BlockSpec((1, SSD_GB * GROUP_W), lambda b, g, c: (0, g))]


def _ssd_fwd_call(xbc, dtp, acp, acr, dsk):
    B, S, _ = xbc.shape
    nc = S // CHUNK

    def body(xbc_ref, dt_ref, ac_ref, ar_ref, ds_ref, y_ref, hp_ref, h_sc):
        @pl.when(pl.program_id(2) == 0)
        def _():
            h_sc[...] = jnp.zeros(h_sc.shape, F32)

        masks = _ssd_masks()
        ys = []
        for gi in range(SSD_GB):
            grp = SSD_GB * pl.program_id(1) + gi
            x, bm, cm, dt, ac, ac_cols, is_last = _ssd_load(xbc_ref, dt_ref, ac_ref, masks, grp, gi)
            last = jnp.sum(ac * is_last, axis=0, keepdims=True)
            decays = _ssd_decays(ac_cols, ar_ref, gi)
            xd = x * dt
            cb = _dot(cm, bm, NT)
            hprev = h_sc[gi]
            hp_ref[0, gi, 0] = hprev
            y = _dot(cm, hprev, NN) * jnp.exp(ac) + ds_ref[:, gi * GROUP_W:(gi + 1) * GROUP_W] * x
            for j in range(HPG):
                y = y + _dot(cb * decays[j], xd * masks[j], NN)
            ys.append(y)
            h_sc[gi] = hprev * jnp.exp(last) + _dot(bm, xd * jnp.exp(last - ac), TN)
        y_ref[0] = jnp.concatenate(ys, axis=1)

    ng = SSM_GROUPS // SSD_GB
    return pl.pallas_call(
        body, grid=(B, ng, nc), in_specs=_ssd_in_specs(nc, False),
        out_specs=[pl.BlockSpec((1, CHUNK, SSD_GB * GROUP_W), lambda b, g, c: (b, c, g)),
                   pl.BlockSpec((1, SSD_GB, 1, STATE_N, GROUP_W), lambda b, g, c: (b, g, c, 0, 0))],
        out_shape=[jax.ShapeDtypeStruct((B, S, D_INNER), F32),
                   jax.ShapeDtypeStruct((B, SSM_GROUPS, nc, STATE_N, GROUP_W), F32)],
        scratch_shapes=[pltpu.VMEM((SSD_GB, STATE_N, GROUP_W), F32)], name="ssd_fwd",
        compiler_params=_cparams(("parallel", "parallel", "arbitrary")))(xbc, dtp, acp, acr, dsk)


def _ssd_bwd_call(xbc, dtp, acp, acr, dsk, hps, dy):
    B, S, _ = xbc.shape
    nc = S // CHUNK

    def body(xbc_ref, dt_ref, ac_ref, ar_ref, ds_ref, hp_ref, dy_ref,
             dxbc_ref, ddt_ref, dac_ref, dar_ref, dds_ref, dh_sc):
        first = pl.program_id(2) == 0

        @pl.when(first)
        def _():
            dh_sc[...] = jnp.zeros(dh_sc.shape, F32)

        masks = _ssd_masks()
        dxbc_parts, dds_parts = [], []
        for gi in range(SSD_GB):
            grp = SSD_GB * pl.program_id(0) + gi
            x, bm, cm, dt, ac, ac_cols, is_last = _ssd_load(xbc_ref, dt_ref, ac_ref, masks, grp, gi)
            last = jnp.sum(ac * is_last, axis=0, keepdims=True)
            g = dy_ref[0, :, gi * GROUP_W:(gi + 1) * GROUP_W]
            hprev = hp_ref[0, gi, 0]
            dh = dh_sc[gi]
            decays = _ssd_decays(ac_cols, ar_ref, gi)
            dcols = []
            xd = x * dt
            cb = _dot(cm, bm, NT)
            e_c = jnp.exp(ac)
            e_end = jnp.exp(last - ac)
            e_last = jnp.exp(last)
            z = _dot(cm, hprev, NN)
            dz = g * e_c
            dac = g * z * e_c
            dc = _dot(dz, hprev, NT)
            dhprev = _dot(cm, dz, TN) + dh * e_last
            dcb = jnp.zeros((CHUNK, CHUNK), F32)
            dxd = jnp.zeros(xd.shape, F32)
            for j in range(HPG):
                gj = cb * decays[j]
                dgj = _dot(g * masks[j], xd, NT)
                dxd = dxd + _dot(gj, g, TN) * masks[j]
                dcb = dcb + dgj * decays[j]
                dseg = dgj * gj
                dcols.append(jnp.sum(dseg, axis=1, keepdims=True))
                dar_ref[0, gi * HPG + j] = -jnp.sum(dseg, axis=0, keepdims=True)
            dc = dc + _dot(dcb, bm, NN)
            db = _dot(dcb, cm, TN)
            sx = xd * e_end
            db = db + _dot(sx, dh, NT)
            dsx = _dot(bm, dh, NN)
            dxd = dxd + dsx * e_end
            de = dsx * sx
            dac = dac - de
            dlast = jnp.sum(de, axis=0, keepdims=True) + jnp.sum(dh * hprev, axis=0, keepdims=True) * e_last
            dsk = ds_ref[:, gi * GROUP_W:(gi + 1) * GROUP_W]
            dxbc_parts += [dxd * dt + dsk * g, db, dc]
            ddt_ref[0, gi] = _ssd_gather(dxd * x, None, masks, grp)
            dac_ref[0, gi] = _ssd_gather(dac + is_last * dlast, dcols, masks, grp)
            dds_parts.append(jnp.sum(g * x, axis=0, keepdims=True))
            dh_sc[gi] = dhprev
        dxbc_ref[0] = jnp.concatenate(dxbc_parts, axis=1)
        dds = jnp.concatenate(dds_parts, axis=1)
        first_all = first & (pl.program_id(1) == 0)

        @pl.when(first_all)
        def _():
            dds_ref[...] = dds

        @pl.when(jnp.logical_not(first_all))
        def _():
            dds_ref[...] += dds

    rc = lambda c: nc - 1 - c
    ng = SSM_GROUPS // SSD_GB
    in_specs = [pl.BlockSpec(s.block_shape, (lambda g, b, c, f=s.index_map: f(b, g, c))) for s in _ssd_in_specs(nc, True)]
    in_specs.append(pl.BlockSpec((1, SSD_GB, 1, STATE_N, GROUP_W), lambda g, b, c: (b, g, rc(c), 0, 0)))
    in_specs.append(pl.BlockSpec((1, CHUNK, SSD_GB * GROUP_W), lambda g, b, c: (b, rc(c), g)))
    per_group = pl.BlockSpec((1, SSD_GB, CHUNK, LANE), lambda g, b, c: (b, g, rc(c), 0))
    out_specs = [pl.BlockSpec((1, CHUNK, SSD_GB * XBC_W), lambda g, b, c: (b, rc(c), g)), per_group, per_group,
                 pl.BlockSpec((1, SSD_GB * HPG, 1, CHUNK), lambda g, b, c: (b, g, 0, rc(c))),
                 pl.BlockSpec((1, SSD_GB * GROUP_W), lambda g, b, c: (0, g))]
    out_shape = [jax.ShapeDtypeStruct(xbc.shape, F32),
                 jax.ShapeDtypeStruct((B, SSM_GROUPS, S, LANE), F32), jax.ShapeDtypeStruct((B, SSM_GROUPS, S, LANE), F32),
                 jax.ShapeDtypeStruct(acr.shape, F32), jax.ShapeDtypeStruct(dsk.shape, F32)]
    return pl.pallas_call(
        body, grid=(ng, B, nc), in_specs=in_specs, out_specs=out_specs, out_shape=out_shape,
        scratch_shapes=[pltpu.VMEM((SSD_GB, STATE_N, GROUP_W), F32)], name="ssd_bwd",
        compiler_params=_cparams(("arbitrary", "arbitrary", "arbitrary")))(xbc, dtp, acp, acr, dsk, hps, dy)


@jax.custom_vjp
def ssd(xbc, dtp, acp, acr, dsk):
    return _ssd_fwd_call(xbc, dtp, acp, acr, dsk)[0]


def _ssd_fwd(xbc, dtp, acp, acr, dsk):
    y, hps = _ssd_fwd_call(xbc, dtp, acp, acr, dsk)
    return y, (xbc, dtp, acp, acr, dsk, hps)


def _ssd_bwd(res, dy):
    dxbc, ddt, dac, dacr, dds = _ssd_bwd_call(*res, dy)
    return dxbc, jnp.sum(ddt, axis=1), jnp.sum(dac, axis=1), dacr, dds


ssd.defvjp(_ssd_fwd, _ssd_bwd)


def _pack_small(arrs):
    flat = jnp.concatenate([a.reshape(-1) for a in arrs])
    rows = -(-flat.shape[0] // (8 * LANE)) * 8
    return jnp.pad(flat, (0, rows * LANE - flat.shape[0])).reshape(rows, LANE)


def _unpack_small(buf, shapes):
    flat = buf.reshape(-1)
    out, off = [], 0
    for shp in shapes:
        n = int(np.prod(shp))
        out.append(flat[off:off + n].reshape(shp))
        off += n
    return out


def _rows_tile(rows, cap):
    for cand in range(min(rows, cap), 7, -8):
        if rows % cand == 0:
            return cand
    return rows


def _pair_sum(mine, theirs, cidx, name):
    n4, kk, nn = mine.shape
    half = kk // 2
    tr = _rows_tile(half, 256)
    nb = half // tr

    def body(c_ref, a_ref, b_ref, o_ref, ob_ref):
        tot = a_ref[...] + b_ref[...]
        o_ref[...] = tot
        ob_ref[...] = tot.astype(BF16)

    spec = pl.BlockSpec((1, tr, nn), lambda j, i, c: (j, i, 0))
    grid_spec = pltpu.PrefetchScalarGridSpec(
        num_scalar_prefetch=1, grid=(n4, nb),
        in_specs=[pl.BlockSpec((1, tr, nn), lambda j, i, c: (j, c[0] * nb + i, 0)), spec], out_specs=[spec, spec])
    return pl.pallas_call(
        body, grid_spec=grid_spec,
        out_shape=[jax.ShapeDtypeStruct((n4, half, nn), F32), jax.ShapeDtypeStruct((n4, half, nn), BF16)],
        name=name, compiler_params=_cparams(("parallel", "parallel")))(cidx, mine, theirs)


def _chip_sum(quad, pair, chip_idx, name):
    _, rows, nn = quad.shape
    tr = _rows_tile(rows, 256)

    def body(s_ref, q_ref, p_ref, o_ref):
        for mine in range(4):
            @pl.when(s_ref[0] == mine)
            def _(mine=mine):
                acc = None
                for d in range(4):
                    term = p_ref[0] if d == mine else q_ref[d].astype(F32)
                    acc = term if acc is None else acc + term
                o_ref[...] = acc

    grid_spec = pltpu.PrefetchScalarGridSpec(
        num_scalar_prefetch=1, grid=(rows // tr,),
        in_specs=[pl.BlockSpec((4, tr, nn), lambda i, s: (0, i, 0)), pl.BlockSpec((1, tr, nn), lambda i, s: (s[0], i, 0))],
        out_specs=pl.BlockSpec((tr, nn), lambda i, s: (i, 0)))
    return pl.pallas_call(body, grid_spec=grid_spec, out_shape=jax.ShapeDtypeStruct((rows, nn), F32), name=name,
                          compiler_params=_cparams(("parallel",)))(chip_idx, quad, pair)


def _adam_halves_call(w, mine, other, cidx, m, v, name):
    rows, nn = w.shape
    half = rows // 2
    tr = _rows_tile(half, 128)
    nb = half // tr

    def body(c_ref, w_ref, a_ref, b_ref, m_ref, v_ref, g_ref, d_ref, nm_ref, nv_ref):
        upper = (pl.program_id(0) >= nb).astype(jnp.int32)
        g = jnp.where(upper == c_ref[0], a_ref[...], b_ref[...])
        g_ref[...] = g
        d_ref[...], nm_ref[...], nv_ref[...] = _adam_fn(w_ref[...], g, m_ref[...], v_ref[...])

    spec = pl.BlockSpec((tr, nn), lambda i, c: (i, 0))
    hspec = pl.BlockSpec((tr, nn), lambda i, c: (i % nb, 0))
    grid_spec = pltpu.PrefetchScalarGridSpec(num_scalar_prefetch=1, grid=(2 * nb,),
                                             in_specs=[spec, hspec, hspec, spec, spec], out_specs=[spec] * 4)
    return pl.pallas_call(body, grid_spec=grid_spec, out_shape=[jax.ShapeDtypeStruct((rows, nn), F32)] * 4, name=name,
                          compiler_params=_cparams(("parallel",)))(cidx, w, mine, other, m, v)


def _stack_sum(stack, name):
    n, rows, nn = stack.shape
    tr = _rows_tile(rows, 256)

    def body(s_ref, o_ref):
        acc = s_ref[0]
        for d in range(1, n):
            acc = acc + s_ref[d]
        o_ref[...] = acc

    return pl.pallas_call(
        body, grid=(rows // tr,), in_specs=[pl.BlockSpec((n, tr, nn), lambda i: (0, i, 0))],
        out_specs=pl.BlockSpec((tr, nn), lambda i: (i, 0)), out_shape=jax.ShapeDtypeStruct((rows, nn), F32),
        name=name, compiler_params=_cparams(("parallel",)))(stack)


def _adam_call(w, g, m, v, name):
    rows, nn = w.shape
    tr = _rows_tile(rows, 128)

    def body(w_ref, g_ref, m_ref, v_ref, d_ref, nm_ref, nv_ref):
        d_ref[...], nm_ref[...], nv_ref[...] = _adam_fn(w_ref[...], g_ref[...], m_ref[...], v_ref[...])

    spec = pl.BlockSpec((tr, nn), lambda i: (i, 0))
    sds = jax.ShapeDtypeStruct((rows, nn), F32)
    return pl.pallas_call(body, grid=(rows // tr,), in_specs=[spec] * 4, out_specs=[spec] * 3,
                          out_shape=[sds] * 3, name=name, compiler_params=_cparams(("parallel",)))(w, g, m, v)


def _adam_fn(w, g, m, v):
    m = ADAM_B1 * m + (1.0 - ADAM_B1) * g
    v = ADAM_B2 * v + (1.0 - ADAM_B2) * (g * g)
    m_hat = m / (1.0 - ADAM_B1 ** ADAM_STEP)
    v_hat = v / (1.0 - ADAM_B2 ** ADAM_STEP)
    delta = -ADAM_LR * (m_hat / (jnp.sqrt(v_hat) + ADAM_EPS) + ADAM_WD * w)
    return delta, m, v


def _mesh_pos():
    return lax.axis_index("x"), lax.axis_index("y"), lax.axis_index("c")


def _other_chips(x, y):
    return [(1 - x, y), (x, 1 - y), (1 - x, 1 - y)]


HBM_SPEC = pl.BlockSpec(memory_space=pl.ANY)


def _remote(src, dst, send_sems, recv_sems, k, to):
    return pltpu.make_async_remote_copy(src_ref=src, dst_ref=dst, send_sem=send_sems.at[k], recv_sem=recv_sems.at[k],
                                        device_id=to, device_id_type=MESH)


def _half_rows(c, rows, align):
    half = rows // 2
    return (pl.ds(pl.multiple_of(c * half, align), half), pl.ds(pl.multiple_of((1 - c) * half, align), half))


def _gather_weights(mats, conv):
    n = len(mats)

    def body(*refs):
        ins, conv_in = refs[:n], refs[n]
        outs, conv_out = refs[n + 1:2 * n + 1], refs[2 * n + 1]
        send_sems, recv_sems, local_sem = refs[2 * n + 2:]
        x, y, c = _mesh_pos()
        me, sibling, s = (x, y, c), (x, y, 1 - c), 2 * x + y
        chips = _other_chips(x, y)
        rows = [_half_rows(c, m.shape[0], 16) for m in mats]
        own = pltpu.make_async_copy(conv_in, conv_out.at[s], local_sem)
        own.start()
        sent = []
        for i in range(n):
            mine = rows[i][0]
            for j, (cx, cy) in enumerate(chips):
                sent.append(_remote(ins[i].at[mine], outs[i].at[s, mine], send_sems, recv_sems, 6 * i + j, (cx, cy, c)))
        for j, (cx, cy) in enumerate(chips):
            sent.append(_remote(conv_in, conv_out.at[s], send_sems, recv_sems, 6 * n + j, (cx, cy, c)))
        for cp in sent:
            cp.start()
        for i in range(n):
            mine = rows[i][0]
            for j, (cx, cy) in enumerate(chips):
                landed = outs[i].at[2 * cx + cy, mine]
                _remote(landed, landed, send_sems, recv_sems, 6 * i + j, me).wait_recv()
                fwd = _remote(landed, landed, send_sems, recv_sems, 6 * i + 3 + j, sibling)
                fwd.start()
                sent.append(fwd)
        for j, (cx, cy) in enumerate(chips):
            slot = conv_out.at[2 * cx + cy]
            _remote(slot, slot, send_sems, recv_sems, 6 * n + j, me).wait_recv()
        for i in range(n):
            theirs_rows = rows[i][1]
            for j, (cx, cy) in enumerate(chips):
                theirs = outs[i].at[2 * cx + cy, theirs_rows]
                _remote(theirs, theirs, send_sems, recv_sems, 6 * i + 3 + j, me).wait_recv()
        for cp in sent:
            cp.wait_send()
        own.wait()

    out_shape = [jax.ShapeDtypeStruct((4,) + m.shape, m.dtype) for m in mats]
    out_shape.append(jax.ShapeDtypeStruct((4,) + conv.shape, conv.dtype))
    res = pl.pallas_call(
        body, in_specs=[HBM_SPEC] * (n + 1), out_specs=[HBM_SPEC] * (n + 1), out_shape=out_shape,
        scratch_shapes=[pltpu.SemaphoreType.DMA((6 * n + 3,)), pltpu.SemaphoreType.DMA((6 * n + 3,)),
                        pltpu.SemaphoreType.DMA],
        name="all_gather_weights")(*mats, conv)
    chip = 2 * lax.axis_index("x") + lax.axis_index("y")
    full = [lax.dynamic_update_slice_in_dim(r, m[None], chip, axis=0) for r, m in zip(res[:n], mats)]
    return full, res[n]


def _sibling_exchange(stacks):
    n = len(stacks)

    def body(*refs):
        ins, outs = refs[:n], refs[n:2 * n]
        send_sems, recv_sems = refs[2 * n:]
        x, y, c = _mesh_pos()
        cps = []
        for i in range(n):
            theirs = _half_rows(c, stacks[i].shape[1], 8)[1]
            cps.append(_remote(ins[i].at[:, theirs, :], outs[i], send_sems, recv_sems, i, (x, y, 1 - c)))
        for cp in cps:
            cp.start()
        for cp in cps:
            cp.wait()

    out_shape = [jax.ShapeDtypeStruct((4, s.shape[1] // 2, s.shape[2]), s.dtype) for s in stacks]
    return pl.pallas_call(
        body, in_specs=[HBM_SPEC] * n, out_specs=[HBM_SPEC] * n, out_shape=out_shape,
        scratch_shapes=[pltpu.SemaphoreType.DMA((n,)), pltpu.SemaphoreType.DMA((n,))],
        name="grad_sibling_exchange")(*stacks)


def _chip_exchange(parts):
    n = len(parts)

    def body(*refs):
        ins, outs = refs[:n], refs[n:2 * n]
        send_sems, recv_sems = refs[2 * n:]
        x, y, c = _mesh_pos()
        me, s = (x, y, c), 2 * x + y
        chips = _other_chips(x, y)
        sent = [_remote(ins[i].at[2 * cx + cy], outs[i].at[s], send_sems, recv_sems, 3 * i + j, (cx, cy, c))
                for i in range(n) for j, (cx, cy) in enumerate(chips)]
        for cp in sent:
            cp.start()
        for i in range(n):
            for j, (cx, cy) in enumerate(chips):
                slot = outs[i].at[2 * cx + cy]
                _remote(slot, slot, send_sems, recv_sems, 3 * i + j, me).wait_recv()
        for cp in sent:
            cp.wait_send()

    return pl.pallas_call(
        body, in_specs=[HBM_SPEC] * n, out_specs=[HBM_SPEC] * n,
        out_shape=[jax.ShapeDtypeStruct(p.shape, p.dtype) for p in parts],
        scratch_shapes=[pltpu.SemaphoreType.DMA((3 * n,)), pltpu.SemaphoreType.DMA((3 * n,))],
        name="grad_chip_exchange")(*parts)


def _sibling_swap(halves):
    n = len(halves)

    def body(*refs):
        ins, outs = refs[:n], refs[n:2 * n]
        send_sems, recv_sems = refs[2 * n:]
        x, y, c = _mesh_pos()
        cps = [_remote(ins[i], outs[i], send_sems, recv_sems, i, (x, y, 1 - c)) for i in range(n)]
        for cp in cps:
            cp.start()
        for cp in cps:
            cp.wait()

    return pl.pallas_call(
        body, in_specs=[HBM_SPEC] * n, out_specs=[HBM_SPEC] * n,
        out_shape=[jax.ShapeDtypeStruct(h.shape, h.dtype) for h in halves],
        scratch_shapes=[pltpu.SemaphoreType.DMA((n,)), pltpu.SemaphoreType.DMA((n,))],
        name="grad_sibling_swap")(*halves)


def _gather_small(vec):
    def body(in_ref, out_ref, send_sems, recv_sems, local_sem):
        x, y, c = _mesh_pos()
        me = (x, y, c)
        own = pltpu.make_async_copy(in_ref, out_ref.at[4 * x + 2 * y + c], local_sem)
        own.start()
        peers = [(1 - x if k & 4 else x, 1 - y if k & 2 else y, 1 - c if k & 1 else c) for k in range(1, 8)]
        sent = [_remote(in_ref, out_ref.at[4 * x + 2 * y + c], send_sems, recv_sems, k, p) for k, p in enumerate(peers)]
        for cp in sent:
            cp.start()
        for k, (px, py, pc) in enumerate(peers):
            slot = out_ref.at[4 * px + 2 * py + pc]
            _remote(slot, slot, send_sems, recv_sems, k, me).wait_recv()
        for cp in sent:
            cp.wait_send()
        own.wait()

    return pl.pallas_call(
        body, in_specs=[HBM_SPEC], out_specs=HBM_SPEC, out_shape=jax.ShapeDtypeStruct((8,) + vec.shape, vec.dtype),
        scratch_shapes=[pltpu.SemaphoreType.DMA((7,)), pltpu.SemaphoreType.DMA((7,)), pltpu.SemaphoreType.DMA],
        name="grad_gather_small")(vec)


def _reduce_matrices(stacks, names):
    cidx = lax.axis_index("c").astype(jnp.int32).reshape(1)
    chip = (2 * lax.axis_index("x") + lax.axis_index("y")).astype(jnp.int32).reshape(1)
    got = _sibling_exchange(stacks)
    pairs = [_pair_sum(a, b, cidx, "grad_pair_sum_" + nm) for a, b, nm in zip(stacks, got, names)]
    quads = _chip_exchange([p[1] for p in pairs])
    mine = [_chip_sum(q, p[0], chip, "grad_chip_sum_" + nm) for q, p, nm in zip(quads, pairs, names)]
    return mine, _sibling_swap(mine)


def _pad_cols(a, n):
    return jnp.concatenate([a, jnp.zeros((a.shape[0], n - a.shape[1]), a.dtype)], axis=1)


def _group_channels(a):
    lead = a.shape[:-1]
    xs = a[..., :D_INNER].reshape(lead + (SSM_GROUPS, GROUP_W))
    bs = a[..., D_INNER:D_INNER + SSM_GROUPS * STATE_N].reshape(lead + (SSM_GROUPS, STATE_N))
    cs = a[..., D_INNER + SSM_GROUPS * STATE_N:].reshape(lead + (SSM_GROUPS, STATE_N))
    return jnp.concatenate([xs, bs, cs], axis=-1).reshape(lead + (CONV_CH,))


def _lay_w_in(w):
    idx = np.cumsum(IN_SIZES)[:-1]
    segs = jnp.split(w, [int(v) for v in idx], axis=1)
    segs[2] = _pad_cols(segs[2], LANE)
    segs[4] = _group_channels(segs[4])
    segs[5] = _pad_cols(segs[5], LANE)
    return jnp.concatenate(segs, axis=1)


IN_PAD_SIZES = (Q_RANK, KV_RANK, LANE, D_INNER, CONV_CH, LANE, D_MODEL, D_MODEL)
IN_PAD_OFFS = [int(v) for v in np.cumsum(IN_PAD_SIZES)[:-1]]


@jax.custom_vjp
def split_proj(proj):
    return tuple(jnp.split(proj, IN_PAD_OFFS, axis=-1))


split_proj.defvjp(lambda proj: (tuple(jnp.split(proj, IN_PAD_OFFS, axis=-1)), None),
                  lambda _, cots: (jnp.concatenate(cots, axis=-1),))


def _lay_w_uq(w):
    w3 = w.reshape(Q_RANK, N_HEADS, NOPE + ROPE)
    w3 = jnp.concatenate([w3, jnp.zeros((Q_RANK, N_HEADS, QK_PAD - NOPE - ROPE), w.dtype)], axis=2)
    return w3.reshape(Q_RANK, N_HEADS * QK_PAD)


def _lay_w_ukv(w):
    w3 = w.reshape(KV_RANK, N_HEADS, NOPE + V_DIM)
    return jnp.concatenate([w3[:, :, :NOPE].reshape(KV_RANK, -1), w3[:, :, NOPE:].reshape(KV_RANK, -1)], axis=1)


def _pad_lanes(v, n=LANE):
    return jnp.concatenate([v, jnp.zeros((v.shape[0], n - v.shape[1]), v.dtype)], axis=1)


def _local_loss(toks, small, x, wb, c8, posf, target):
    B, S, D = x.shape
    T = B * S

    def lin(name, a, key, lay=lambda w: w, out_dtype=F32):
        return make_linear(name, out_dtype)(a, lay(wb[key]), lay(toks[key]))

    rows2 = lambda a: a.reshape(T, a.shape[-1])
    rows3 = lambda a: a.reshape(B, S, a.shape[-1])

    sc = make_rowwise("silu_c", _f_silu, 1, 0, 0, ('row',))((c8[None],), (), ())[0][0]
    mod = lin("ada", sc, 'w_ada')[:B] + small['b_ada']
    shift1, scale1, gate1, shift2, scale2, gate2 = [m[:, None, :] for m in jnp.split(mod, 6, axis=-1)]

    modulate = make_rowwise("modulate1", _f_modulate, 1, 2, 1, ('row',))
    h = modulate((x,), (scale1, shift1), (small['g_pre_mix'],))[0]
    proj = rows3(lin("w_in", rows2(h), 'w_in', _lay_w_in))
    q_lat, kv_lat, k_rope, z, xbc, dt_raw, gate_a, gate_b = split_proj(proj)

    inv = ROPE_THETA ** (-jnp.arange(ROPE // 2, dtype=F32) / (ROPE // 2))
    inv_lane = jnp.concatenate([inv, inv, jnp.zeros((LANE - ROPE,), F32)])[None]
    tabs = tuple(_rope_tables(posf, inv_lane))
    qn = make_rowwise("rms_q", _f_rms, 1, 0, 1, ('row',))((q_lat,), (), (small['g_q_lat'],))[0]
    kvn = make_rowwise("rms_kv", _f_rms, 1, 0, 1, ('row',))((kv_lat,), (), (small['g_kv_lat'],))[0]
    qp = rows3(lin("w_uq", rows2(qn), 'w_uq', _lay_w_uq))
    kvp = rows3(lin("w_ukv", rows2(kvn), 'w_ukv', _lay_w_ukv, BF16))
    qr = rope_q(qp, tabs)
    kr = build_k(kvp, k_rope, tabs)
    att = attention(qr, kr, kvp)
    attn = rows3(lin("w_o_attn", rows2(att), 'w_o_attn'))

    xa = conv_silu(xbc, _group_channels(wb['conv_w_f32']), _group_channels(small['conv_b']))
    dt_pad, a_pad = make_rowwise("dt_softplus", _f_dt, 1, 0, 2, ('row', 'row'))(
        (dt_raw,), (), (_pad_lanes(small['dt_bias']), _pad_lanes(small['a_log'])))
    ac_pad = chunk_cumsum(a_pad)
    acr = jnp.transpose(ac_pad[..., :SSM_HEADS], (0, 2, 1))[:, :, None, :]
    dsk = jnp.repeat(small['d_skip'], HEAD_P, axis=-1)
    y = ssd(xa, dt_pad, ac_pad, acr, dsk)
    yg = make_rowwise("gated_norm", _f_gated_norm, 2, 0, 1, ('row',), ncol=SSM_GROUPS)(
        (y, z), (), (small['g_ssm_out'],))[0]
    ssm = rows3(lin("w_o_ssm", rows2(yg), 'w_o_ssm'))

    merged = make_rowwise("merge", _f_merge, 4, 0, 0, ('row',))((attn, ssm, gate_a, gate_b), (), ())[0]
    mix = rows3(lin("w_out", rows2(merged), 'w_out'))
    x1 = make_rowwise("post_mix", _f_post, 2, 1, 1, ('row',))((x, mix), (gate1,), (small['g_post_mix'],))[0]

    h2 = make_rowwise("modulate2", _f_modulate, 1, 2, 1, ('row',))((x1,), (scale2, shift2), (small['g_pre_mlp'],))[0]
    ff = rows3(ffn(rows2(h2), wb['w_ff1'], toks['w_ff1'], wb['w_ff2'], toks['w_ff2']))
    lvec = make_rowwise("final_loss", _f_final_loss, 3, 1, 1, ('sum',), nodiff=(2,))(
        (x1, ff, target), (gate2,), (small['g_post_mlp'],))[0]
    return jnp.sum(lvec)


MATRICES = COL_SHARDED + ROW_SHARDED


def _local_step(x, c, positions, target, wb, small):
    B = x.shape[0]
    c8 = jnp.concatenate([c, jnp.zeros((16 - B, c.shape[1]), F32)], axis=0)
    posf = positions.astype(F32)[..., None]
    toks = {k: jnp.zeros(wb[k].shape, F32) for k in MATRICES if k != 'conv_w'}
    conv_w = wb['conv_w_f32']

    def loss_fn(toks, small, conv_w, x):
        wbl = dict(wb)
        wbl['conv_w_f32'] = conv_w
        return _local_loss(toks, small, x, wbl, c8, posf, target)

    loss, (g_tok, g_small, g_conv, g_x) = jax.value_and_grad(loss_fn, argnums=(0, 1, 2, 3))(toks, small, conv_w, x)
    grads = dict(g_tok)
    grads.update(g_small)
    grads['conv_w'] = g_conv
    return loss, g_x, grads


def kernel(x, c, positions, w_ada, b_ada, g_pre_mix, g_post_mix, w_in, g_q_lat, g_kv_lat, w_uq, w_ukv, w_o_attn, conv_w, conv_b, dt_bias, a_log, d_skip, g_ssm_out, w_o_ssm, w_out, g_pre_mlp, g_post_mlp, w_ff1, w_ff2, loss_target, m_w_ada, m_b_ada, m_g_pre_mix, m_g_post_mix, m_w_in, m_g_q_lat, m_g_kv_lat, m_w_uq, m_w_ukv, m_w_o_attn, m_conv_w, m_conv_b, m_dt_bias, m_a_log, m_d_skip, m_g_ssm_out, m_w_o_ssm, m_w_out, m_g_pre_mlp, m_g_post_mlp, m_w_ff1, m_w_ff2, v_w_ada, v_b_ada, v_g_pre_mix, v_g_post_mix, v_w_in, v_g_q_lat, v_g_kv_lat, v_w_uq, v_w_ukv, v_w_o_attn, v_conv_w, v_conv_b, v_dt_bias, v_a_log, v_d_skip, v_g_ssm_out, v_w_o_ssm, v_w_out, v_g_pre_mlp, v_g_post_mlp, v_w_ff1, v_w_ff2):
    given = dict(locals())
    w_loc = {n: given[n] for n in WEIGHTS}
    m_loc = {n: given["m_" + n] for n in WEIGHTS}
    v_loc = {n: given["v_" + n] for n in WEIGHTS}
    mats = [n for n in WEIGHTS if n in MATRICES and n != 'conv_w']
    vecs = [n for n in WEIGHTS if n not in MATRICES]

    g_mats, g_conv = _gather_weights([w_loc[n][0].astype(BF16) for n in mats], conv_w[0])
    wb = {}
    for n, g in zip(mats, g_mats):
        if n in COL_SHARDED:
            wb[n] = jnp.transpose(g, (1, 0, 2)).reshape(g.shape[1], -1)
        else:
            wb[n] = g.reshape(-1, g.shape[2])
    wb['conv_w_f32'] = jnp.transpose(g_conv, (1, 0, 2)).reshape(CONV_K, -1)
    small = {n: w_loc[n] for n in vecs}

    loss_part, grad_x, grads = _local_step(x, c, positions, loss_target, wb, small)
    loss = lax.psum(loss_part, ("x", "y", "c"))

    stacks = []
    for n in mats:
        kk, nn = w_loc[n].shape[1:]
        if n in COL_SHARDED:
            stacks.append(jnp.transpose(grads[n].reshape(kk, 4, nn), (1, 0, 2)))
        else:
            stacks.append(grads[n].reshape(4, kk, nn))
    g_mine, g_other = _reduce_matrices(stacks, mats)
    g_shard = {}

    vec_shapes = [tuple(grads[n].shape) for n in vecs] + [tuple(grads['conv_w'].shape)]
    total = _stack_sum(_gather_small(_pack_small([grads[n] for n in vecs] + [grads['conv_w']])), "grad_sum_small")
    g_vec = _unpack_small(total, vec_shapes)
    n_conv = conv_w.shape[2]
    chip = 2 * lax.axis_index("x") + lax.axis_index("y")
    g_shard['conv_w'] = lax.dynamic_slice_in_dim(g_vec[-1], chip * n_conv, n_conv, axis=1)
    for n, g in zip(vecs, g_vec):
        g_shard[n] = g

    delta, new_m, new_v = {}, {}, {}
    cidx = lax.axis_index("c").astype(jnp.int32).reshape(1)
    for n, mine, other in zip(mats, g_mine, g_other):
        g_shard[n], delta[n], new_m[n], new_v[n] = _adam_halves_call(
            w_loc[n][0], mine, other, cidx, m_loc[n][0], v_loc[n][0], "adamw_" + n)
    rest = vecs + ['conv_w']
    rest_shapes = [tuple(w_loc[n].shape) for n in rest]
    packed = [_pack_small([src[n] for n in rest]) for src in (w_loc, g_shard, m_loc, v_loc)]
    for dst, buf in zip((delta, new_m, new_v), _adam_call(*packed, "adamw_small")):
        dst.update(zip(rest, _unpack_small(buf, rest_shapes)))

    def out(d):
        return [d[n].reshape(w_loc[n].shape) for n in WEIGHTS]

    return (loss, grad_x, *out(g_shard), *out(delta), *out(new_m), *out(new_v))
```

```python
import functools
import math

import numpy as np
import jax
import jax.numpy as jnp
from jax import lax
from jax.experimental import pallas as pl
from jax.experimental.pallas import tpu as pltpu

F32 = jnp.float32
BF16 = jnp.bfloat16
MESH = pl.DeviceIdType.MESH

D_MODEL = 1024
N_HEADS = 8
NOPE = 128
ROPE = 64
V_DIM = 128
Q_RANK = 256
KV_RANK = 256
ROPE_THETA = 10000.0
D_INNER = 2048
SSM_HEADS = 32
SSM_GROUPS = 8
HEAD_P = 64
STATE_N = 128
CONV_K = 4
CHUNK = 128
CONV_CH = D_INNER + 2 * SSM_GROUPS * STATE_N
D_FF = 4096
EPS = 1e-6
IN_SIZES = (Q_RANK, KV_RANK, ROPE, D_INNER, CONV_CH, SSM_HEADS, D_MODEL, D_MODEL)
ADAM_LR, ADAM_B1, ADAM_B2, ADAM_EPS, ADAM_WD, ADAM_STEP = 0.001, 0.9, 0.999, 1e-08, 0.01, 10

VMEM_LIMIT_BYTES = 52 * 1024 * 1024
LANE = 128
QK_PAD = 256

WEIGHTS = ['w_ada', 'b_ada', 'g_pre_mix', 'g_post_mix', 'w_in', 'g_q_lat', 'g_kv_lat', 'w_uq', 'w_ukv',
           'w_o_attn', 'conv_w', 'conv_b', 'dt_bias', 'a_log', 'd_skip', 'g_ssm_out', 'w_o_ssm', 'w_out',
           'g_pre_mlp', 'g_post_mlp', 'w_ff1', 'w_ff2']
COL_SHARDED = ('w_ada', 'w_in', 'w_uq', 'w_ukv', 'conv_w', 'w_ff1')
ROW_SHARDED = ('w_o_attn', 'w_o_ssm', 'w_out', 'w_ff2')


def _cparams(sem):
    return pltpu.CompilerParams(dimension_semantics=sem, vmem_limit_bytes=VMEM_LIMIT_BYTES)


def _tile(n, cap):
    if n <= cap:
        return n
    k = n // LANE
    best = LANE
    for d in range(1, k + 1):
        if k % d == 0 and d * LANE <= cap:
            best = d * LANE
    return best


def _mm(a, w, name, out_dtype=F32, epilogue=None, extras=(), out_dtypes=None):
    M, K = a.shape
    N = w.shape[1]
    tm = min(M, 1024)
    tn = _tile(N, 1024)
    tk = _tile(K, 2048)
    nk = K // tk
    dts = tuple(out_dtypes) if epilogue is not None else (out_dtype,)
    n_x, n_o = len(extras), len(dts)

    def finish(acc, refs):
        res = epilogue(acc, *[r[...] for r in refs[:n_x]]) if epilogue is not None else (acc,)
        for o_ref, val, dt in zip(refs[n_x:n_x + n_o], res, dts):
            o_ref[...] = val.astype(dt)

    def body(a_ref, w_ref, *refs):
        part = jnp.dot(a_ref[...].astype(BF16), w_ref[...], preferred_element_type=F32)
        if nk == 1:
            finish(part, refs)
        else:
            acc_ref = refs[-1]
            k = pl.program_id(2)

            @pl.when(k == 0)
            def _():
                acc_ref[...] = part

            @pl.when(k > 0)
            def _():
                acc_ref[...] += part

            @pl.when(k == nk - 1)
            def _():
                finish(acc_ref[...], refs)

    ospec = pl.BlockSpec((tm, tn), lambda i, j, k: (i, j))
    res = pl.pallas_call(
        body, grid=(M // tm, N // tn, nk),
        in_specs=[pl.BlockSpec((tm, tk), lambda i, j, k: (i, k)), pl.BlockSpec((tk, tn), lambda i, j, k: (k, j))]
        + [ospec] * n_x,
        out_specs=[ospec] * n_o, out_shape=[jax.ShapeDtypeStruct((M, N), dt) for dt in dts],
        scratch_shapes=[pltpu.VMEM((tm, tn), F32)] if nk > 1 else [], name=name,
        compiler_params=_cparams(("parallel", "parallel", "arbitrary")))(a, w, *extras)
    return res if epilogue is not None else res[0]


def _mm_tn(a, g, name):
    M, K = a.shape
    N = g.shape[1]
    tm = min(M, 1024)
    tk = _tile(K, 1024)
    tn = _tile(N, 1024)
    nm = M // tm

    def body(a_ref, g_ref, o_ref):
        part = lax.dot_general(a_ref[...].astype(BF16), g_ref[...].astype(BF16), (((0,), (0,)), ((), ())),
                               preferred_element_type=F32)
        m = pl.program_id(2)

        @pl.when(m == 0)
        def _():
            o_ref[...] = part

        @pl.when(m > 0)
        def _():
            o_ref[...] += part

    return pl.pallas_call(
        body, grid=(K // tk, N // tn, nm),
        in_specs=[pl.BlockSpec((tm, tk), lambda i, j, m: (m, i)), pl.BlockSpec((tm, tn), lambda i, j, m: (m, j))],
        out_specs=pl.BlockSpec((tk, tn), lambda i, j, m: (i, j)),
        out_shape=jax.ShapeDtypeStruct((K, N), F32), name=name,
        compiler_params=_cparams(("parallel", "parallel", "arbitrary")))(a, g)


def make_linear(name, out_dtype=F32):
    @jax.custom_vjp
    def linear(a, w, tok):
        return _mm(a, w, name + "_fwd", out_dtype)

    def fwd(a, w, tok):
        return _mm(a, w, name + "_fwd", out_dtype), (a, w)

    def bwd(res, g):
        a, w = res
        da = _mm(g, w.T, name + "_dx", a.dtype)
        dw = _mm_tn(a, g, name + "_dw")
        return da, jnp.zeros_like(w), dw

    linear.defvjp(fwd, bwd)
    return linear


def _relu2_epilogue(acc):
    r = jnp.maximum(acc, 0.0)
    return r * r, r


def _relu2_bwd_epilogue(acc, r):
    return (acc * (2.0 * r.astype(F32)),)


@jax.custom_vjp
def ffn(h, w1, tok1, w2, tok2):
    act, _ = _mm(h, w1, "w_ff1_fwd", epilogue=_relu2_epilogue, out_dtypes=(BF16, BF16))
    return _mm(act, w2, "w_ff2_fwd")


def _ffn_fwd(h, w1, tok1, w2, tok2):
    act, r = _mm(h, w1, "w_ff1_fwd", epilogue=_relu2_epilogue, out_dtypes=(BF16, BF16))
    return _mm(act, w2, "w_ff2_fwd"), (h, w1, w2, act, r)


def _ffn_bwd(res, g):
    h, w1, w2, act, r = res
    du = _mm(g, w2.T, "w_ff2_dx", epilogue=_relu2_bwd_epilogue, extras=(r,), out_dtypes=(BF16,))[0]
    dw2 = _mm_tn(act, g, "w_ff2_dw")
    dw1 = _mm_tn(h, du, "w_ff1_dw")
    dh = _mm(du, w1.T, "w_ff1_dx", h.dtype)
    return dh, jnp.zeros_like(w1), dw1, jnp.zeros_like(w2), dw2


ffn.defvjp(_ffn_fwd, _ffn_bwd)


def make_rowwise(name, f, n_rows, n_seqs, n_pars, out_kinds, ncol=1, nodiff=(), ts_cap=512):
    n_in = n_rows + n_seqs + n_pars
    diff_idx = [i for i in range(n_in) if i not in nodiff]

    def _dims(rows):
        B, S = rows[0].shape[0], rows[0].shape[1]
        ts = min(S, ts_cap)
        return B, S, ts

    def _in_specs(rows, seqs, pars, ts):
        specs = []
        for r in rows:
            specs.append(pl.BlockSpec((1, ts, r.shape[2] // ncol), lambda k, b, s: (b, s, k)))
        for q in seqs:
            specs.append(pl.BlockSpec((1, 1, q.shape[2] // ncol), lambda k, b, s: (b, 0, k)))
        for p in pars:
            specs.append(pl.BlockSpec((1, p.shape[1] // ncol), lambda k, b, s: (0, k)))
        return specs

    def _load(refs):
        vals = [r[0] for r in refs[:n_rows + n_seqs]]
        vals += [r[...] for r in refs[n_rows + n_seqs:n_in]]
        return vals

    def _out_struct(rows, seqs, pars, ts):
        blocks = [jax.ShapeDtypeStruct((ts, r.shape[2] // ncol), r.dtype) for r in rows]
        blocks += [jax.ShapeDtypeStruct((1, q.shape[2] // ncol), q.dtype) for q in seqs]
        blocks += [jax.ShapeDtypeStruct((1, p.shape[1] // ncol), p.dtype) for p in pars]
        return jax.eval_shape(f, *blocks)

    def _fwd_call(rows, seqs, pars):
        B, S, ts = _dims(rows)
        outs = _out_struct(rows, seqs, pars, ts)
        n_out = len(outs)

        def body(*refs):
            res = f(*_load(refs))
            first = (pl.program_id(1) == 0) & (pl.program_id(2) == 0)
            for o_ref, val, kind in zip(refs[n_in:], res, out_kinds):
                if kind == 'row':
                    o_ref[0] = val
                else:
                    tot = jnp.sum(val, axis=0, keepdims=True)

                    @pl.when(first)
                    def _(o_ref=o_ref, tot=tot):
                        o_ref[...] = tot

                    @pl.when(jnp.logical_not(first))
                    def _(o_ref=o_ref, tot=tot):
                        o_ref[...] += tot

        out_shape, out_specs = [], []
        for o, kind in zip(outs, out_kinds):
            d = o.shape[1]
            if kind == 'row':
                out_shape.append(jax.ShapeDtypeStruct((B, S, ncol * d), o.dtype))
                out_specs.append(pl.BlockSpec((1, ts, d), lambda k, b, s: (b, s, k)))
            else:
                out_shape.append(jax.ShapeDtypeStruct((1, ncol * d), o.dtype))
                out_specs.append(pl.BlockSpec((1, d), lambda k, b, s: (0, k)))
        res = pl.pallas_call(
            body, grid=(ncol, B, S // ts), in_specs=_in_specs(rows, seqs, pars, ts), out_specs=out_specs,
            out_shape=out_shape, name=name + "_fwd",
            compiler_params=_cparams(("arbitrary", "arbitrary", "arbitrary")))(*rows, *seqs, *pars)
        return tuple(res)

    def _bwd_call(rows, seqs, pars, cots):
        B, S, ts = _dims(rows)
        outs = _out_struct(rows, seqs, pars, ts)
        n_out = len(outs)
        all_in = list(rows) + list(seqs) + list(pars)

        def body(*refs):
            vals = _load(refs)
            cts = []
            for c_ref, o, kind in zip(refs[n_in:n_in + n_out], outs, out_kinds):
                if kind == 'row':
                    cts.append(c_ref[0])
                else:
                    cts.append(jnp.broadcast_to(c_ref[...], o.shape))

            def g(*dv):
                full = list(vals)
                for i, v in zip(diff_idx, dv):
                    full[i] = v
                return tuple(f(*full))

            _, vjp = jax.vjp(g, *[vals[i] for i in diff_idx])
            grads = vjp(tuple(cts))
            b, s = pl.program_id(1), pl.program_id(2)
            for o_ref, i, gr in zip(refs[n_in + n_out:], diff_idx, grads):
                if i < n_rows:
                    o_ref[0] = gr
                else:
                    first = (s == 0) if i < n_rows + n_seqs else ((b == 0) & (s == 0))
                    target = (lambda r: r.at[0]) if i < n_rows + n_seqs else (lambda r: r)

                    @pl.when(first)
                    def _(o_ref=o_ref, gr=gr, target=target):
                        target(o_ref)[...] = gr

                    @pl.when(jnp.logical_not(first))
                    def _(o_ref=o_ref, gr=gr, target=target):
                        target(o_ref)[...] += gr

        cot_specs = []
        for o, kind in zip(outs, out_kinds):
            d = o.shape[1]
            if kind == 'row':
                cot_specs.append(pl.BlockSpec((1, ts, d), lambda k, b, s: (b, s, k)))
            else:
                cot_specs.append(pl.BlockSpec((1, d), lambda k, b, s: (0, k)))
        out_shape, out_specs = [], []
        for i in diff_idx:
            a = all_in[i]
            out_shape.append(jax.ShapeDtypeStruct(a.shape, a.dtype))
            if i < n_rows:
                out_specs.append(pl.BlockSpec((1, ts, a.shape[2] // ncol), lambda k, b, s: (b, s, k)))
            elif i < n_rows + n_seqs:
                out_specs.append(pl.BlockSpec((1, 1, a.shape[2] // ncol), lambda k, b, s: (b, 0, k)))
            else:
                out_specs.append(pl.BlockSpec((1, a.shape[1] // ncol), lambda k, b, s: (0, k)))
        res = pl.pallas_call(
            body, grid=(ncol, B, S // ts), in_specs=_in_specs(rows, seqs, pars, ts) + cot_specs,
            out_specs=out_specs, out_shape=out_shape, name=name + "_bwd",
            compiler_params=_cparams(("arbitrary", "arbitrary", "arbitrary")))(*all_in, *cots)
        grads = [None] * n_in
        for i, r in zip(diff_idx, res):
            grads[i] = r
        for i in nodiff:
            grads[i] = jnp.zeros_like(all_in[i])
        return tuple(grads[:n_rows]), tuple(grads[n_rows:n_rows + n_seqs]), tuple(grads[n_rows + n_seqs:])

    @jax.custom_vjp
    def op(rows, seqs, pars):
        return _fwd_call(rows, seqs, pars)

    def fwd(rows, seqs, pars):
        return _fwd_call(rows, seqs, pars), (rows, seqs, pars)

    def bwd(res, cots):
        rows, seqs, pars = res
        return _bwd_call(rows, seqs, pars, cots)

    op.defvjp(fwd, bwd)
    return op


def _rms(x, g):
    return x * lax.rsqrt(jnp.mean(x * x, axis=-1, keepdims=True) + EPS) * g


def _silu(x):
    return x * lax.logistic(x)


def _f_silu(c):
    return (_silu(c),)


def _f_modulate(x, scale, shift, g):
    return ((_rms(x, g) * (1.0 + scale) + shift).astype(BF16),)


def _f_rms(x, g):
    return (_rms(x, g).astype(BF16),)


def _f_dt(dt_raw, dt_bias, a_log):
    z = dt_raw + dt_bias
    dt = jnp.maximum(z, 0.0) + jnp.log1p(jnp.exp(-jnp.abs(z)))
    return dt, dt * (-jnp.exp(a_log))


def _f_gated_norm(y, z, g):
    return (_rms(y * _silu(z), g).astype(BF16),)


def _f_merge(attn, ssm, ga, gb):
    return ((lax.logistic(ga) * attn + lax.logistic(gb) * ssm).astype(BF16),)


def _f_post(x, m, gate, g):
    return (x + gate * _rms(m, g),)


def _f_final_loss(x, ff, target, gate, g):
    e = x + gate * _rms(ff, g) - target
    return (e * e * (0.5 / D_MODEL),)


def _rope_tables(posf, inv_lane):
    B, S, _ = posf.shape
    ts = min(S, 512)

    def body(p_ref, inv_ref, c_ref, a_ref, b_ref):
        ang = p_ref[0] * inv_ref[...]
        cs, sn = jnp.cos(ang), jnp.sin(ang)
        lane = lax.broadcasted_iota(jnp.int32, ang.shape, 1)
        c_ref[0] = jnp.where(lane < ROPE, cs, 0.0)
        a_ref[0] = jnp.where(lane < ROPE // 2, -sn, 0.0)
        b_ref[0] = jnp.where((lane >= ROPE // 2) & (lane < ROPE), sn, 0.0)

    spec = pl.BlockSpec((1, ts, LANE), lambda b, s: (b, s, 0))
    sds = jax.ShapeDtypeStruct((B, S, LANE), F32)
    return pl.pallas_call(
        body, grid=(B, S // ts),
        in_specs=[pl.BlockSpec((1, ts, 1), lambda b, s: (b, s, 0)), pl.BlockSpec((1, LANE), lambda b, s: (0, 0))],
        out_specs=[spec, spec, spec], out_shape=[sds, sds, sds], name="rope_tables",
        compiler_params=_cparams(("parallel", "parallel")))(posf, inv_lane)


def _rot(u, c, a, bm):
    return u * c + pltpu.roll(u, 96, 1) * a + pltpu.roll(u, 32, 1) * bm


def _rot_t(g, c, a, bm):
    return g * c + pltpu.roll(g * a, 32, 1) + pltpu.roll(g * bm, 96, 1)


def _rope_q_call(q, tabs, transpose, name):
    B, S, W = q.shape
    ts = min(S, 512)
    fn = _rot_t if transpose else _rot
    out_dtype = F32 if transpose else BF16

    def body(q_ref, c_ref, a_ref, b_ref, o_ref):
        tc, ta, tb = c_ref[0], a_ref[0], b_ref[0]
        for h in range(W // QK_PAD):
            u = q_ref[0, :, h * QK_PAD:(h + 1) * QK_PAD].astype(F32) * ATT_SCALE
            r = fn(u[:, NOPE:], tc, ta, tb)
            o_ref[0, :, h * QK_PAD:(h + 1) * QK_PAD] = jnp.concatenate([u[:, :NOPE], r], axis=1).astype(out_dtype)

    tspec = pl.BlockSpec((1, ts, LANE), lambda b, s: (b, s, 0))
    qspec = pl.BlockSpec((1, ts, W), lambda b, s: (b, s, 0))
    return pl.pallas_call(
        body, grid=(B, S // ts), in_specs=[qspec, tspec, tspec, tspec], out_specs=qspec,
        out_shape=jax.ShapeDtypeStruct(q.shape, out_dtype), name=name,
        compiler_params=_cparams(("parallel", "parallel")))(q, *tabs)


@jax.custom_vjp
def rope_q(q, tabs):
    return _rope_q_call(q, tabs, False, "rope_q_fwd")


def _rope_q_fwd(q, tabs):
    return _rope_q_call(q, tabs, False, "rope_q_fwd"), tabs


def _rope_q_bwd(tabs, g):
    return _rope_q_call(g, tabs, True, "rope_q_bwd"), tuple(jnp.zeros_like(t) for t in tabs)


rope_q.defvjp(_rope_q_fwd, _rope_q_bwd)


def _build_k_fwd_call(kv, kr, tabs):
    B, S, _ = kv.shape
    ts = min(S, 512)

    def body(kv_ref, kr_ref, c_ref, a_ref, b_ref, o_ref):
        r = _rot(kr_ref[0], c_ref[0], a_ref[0], b_ref[0]).astype(BF16)
        for h in range(N_HEADS):
            o_ref[0, :, h * QK_PAD:(h + 1) * QK_PAD] = jnp.concatenate(
                [kv_ref[0, :, h * NOPE:(h + 1) * NOPE], r], axis=1)

    tspec = pl.BlockSpec((1, ts, LANE), lambda b, s: (b, s, 0))
    return pl.pallas_call(
        body, grid=(B, S // ts),
        in_specs=[pl.BlockSpec((1, ts, N_HEADS * NOPE), lambda b, s: (b, s, 0)), tspec, tspec, tspec, tspec],
        out_specs=pl.BlockSpec((1, ts, N_HEADS * QK_PAD), lambda b, s: (b, s, 0)),
        out_shape=jax.ShapeDtypeStruct((B, S, N_HEADS * QK_PAD), BF16), name="build_k_fwd",
        compiler_params=_cparams(("parallel", "parallel")))(kv, kr, *tabs)


def _build_k_bwd_call(g, tabs):
    B, S, _ = g.shape
    ts = min(S, 512)

    def body(g_ref, c_ref, a_ref, b_ref, dk_ref, dr_ref):
        tot = None
        for h in range(N_HEADS):
            dk_ref[0, :, h * NOPE:(h + 1) * NOPE] = g_ref[0, :, h * QK_PAD:h * QK_PAD + NOPE]
            part = g_ref[0, :, h * QK_PAD + NOPE:(h + 1) * QK_PAD].astype(F32)
            tot = part if tot is None else tot + part
        dr_ref[0] = _rot_t(tot, c_ref[0], a_ref[0], b_ref[0])

    tspec = pl.BlockSpec((1, ts, LANE), lambda b, s: (b, s, 0))
    return pl.pallas_call(
        body, grid=(B, S // ts),
        in_specs=[pl.BlockSpec((1, ts, N_HEADS * QK_PAD), lambda b, s: (b, s, 0)), tspec, tspec, tspec],
        out_specs=[pl.BlockSpec((1, ts, N_HEADS * NOPE), lambda b, s: (b, s, 0)), tspec],
        out_shape=[jax.ShapeDtypeStruct((B, S, N_HEADS * NOPE), BF16), jax.ShapeDtypeStruct((B, S, LANE), F32)],
        name="build_k_bwd", compiler_params=_cparams(("parallel", "parallel")))(g, *tabs)


@jax.custom_vjp
def build_k(kv, kr, tabs):
    return _build_k_fwd_call(kv, kr, tabs)


def _build_k_fwd(kv, kr, tabs):
    return _build_k_fwd_call(kv, kr, tabs), (tabs, kv.shape)


def _build_k_bwd(res, g):
    tabs, kv_shape = res
    dk, dr = _build_k_bwd_call(g, tabs)
    dkv = jnp.concatenate([dk, jnp.zeros((kv_shape[0], kv_shape[1], kv_shape[2] - dk.shape[2]), BF16)], axis=-1)
    return dkv, dr, tuple(jnp.zeros_like(t) for t in tabs)


build_k.defvjp(_build_k_fwd, _build_k_bwd)


ATT_SCALE = (NOPE + ROPE) ** -0.5
NEG = -1e30


def _att_tiles(S):
    t = min(S, 512)
    return t, S // t


def _scores(q, k, diagonal):
    s = lax.dot_general(q, k, (((1,), (1,)), ((), ())), preferred_element_type=F32)
    if diagonal:
        row = lax.broadcasted_iota(jnp.int32, s.shape, 0)
        col = lax.broadcasted_iota(jnp.int32, s.shape, 1)
        s = jnp.where(col <= row, s, NEG)
    return s


ATT_HB = 4


def _causal_pairs(n):
    pairs = [(i, j) for i in range(n) for j in range(i + 1)]
    return (jnp.asarray([p[0] for p in pairs], jnp.int32), jnp.asarray([p[1] for p in pairs], jnp.int32))


def _head(ref_or_val, h, w):
    return ref_or_val[:, h * w:(h + 1) * w]


def _attn_fwd_call(q, k, vsrc, v_blk0):
    B, S, _ = q.shape
    t, n = _att_tiles(S)
    qi, kj = _causal_pairs(n)

    def body(qi_ref, kj_ref, q_ref, k_ref, v_ref, o_ref, lse_ref, m_sc, l_sc, acc_sc):
        p_id = pl.program_id(2)
        i, j = qi_ref[p_id], kj_ref[p_id]

        @pl.when(j == 0)
        def _():
            m_sc[...] = jnp.full(m_sc.shape, NEG, F32)
            l_sc[...] = jnp.zeros(l_sc.shape, F32)
            acc_sc[...] = jnp.zeros(acc_sc.shape, F32)

        def step(diagonal):
            qa, ka, va = q_ref[0], k_ref[0], v_ref[0]
            for h in range(ATT_HB):
                lanes = slice(h * LANE, (h + 1) * LANE)
                s = _scores(_head(qa, h, QK_PAD), _head(ka, h, QK_PAD), diagonal)
                m_prev = m_sc[:, lanes]
                m_new = jnp.maximum(m_prev, jnp.max(s, axis=1, keepdims=True))
                alpha = jnp.exp(m_prev - m_new)
                p = jnp.exp(s - jnp.tile(m_new, (1, t // LANE)))
                l_sc[:, lanes] = alpha * l_sc[:, lanes] + jnp.sum(p, axis=1, keepdims=True)
                acc_sc[:, lanes] = alpha * acc_sc[:, lanes] + jnp.dot(p.astype(BF16), _head(va, h, V_DIM),
                                                                      preferred_element_type=F32)
                m_sc[:, lanes] = m_new

        @pl.when(j < i)
        def _():
            step(False)

        @pl.when(j == i)
        def _():
            step(True)
            o_ref[0] = acc_sc[...] / l_sc[...]
            lse_ref[0] = m_sc[...] + jnp.log(l_sc[...])

    wq, wv = ATT_HB * QK_PAD, ATT_HB * V_DIM
    grid_spec = pltpu.PrefetchScalarGridSpec(
        num_scalar_prefetch=2, grid=(B, N_HEADS // ATT_HB, qi.shape[0]),
        in_specs=[pl.BlockSpec((1, t, wq), lambda b, h, p, qi, kj: (b, qi[p], h)),
                  pl.BlockSpec((1, t, wq), lambda b, h, p, qi, kj: (b, kj[p], h)),
                  pl.BlockSpec((1, t, wv), lambda b, h, p, qi, kj: (b, kj[p], v_blk0 + h))],
        out_specs=[pl.BlockSpec((1, t, wv), lambda b, h, p, qi, kj: (b, qi[p], h)),
                   pl.BlockSpec((1, t, wv), lambda b, h, p, qi, kj: (b, qi[p], h))],
        scratch_shapes=[pltpu.VMEM((t, wv), F32), pltpu.VMEM((t, wv), F32), pltpu.VMEM((t, wv), F32)])
    return pl.pallas_call(
        body, grid_spec=grid_spec,
        out_shape=[jax.ShapeDtypeStruct((B, S, N_HEADS * V_DIM), F32),
                   jax.ShapeDtypeStruct((B, S, N_HEADS * LANE), F32)],
        name="attn_fwd", compiler_params=_cparams(("parallel", "parallel", "arbitrary")))(qi, kj, q, k, vsrc)


def _attn_p_ds(q, k, v, o, do, lse, diagonal, t):
    s = _scores(q, k, diagonal)
    p = jnp.exp(s - jnp.tile(lse, (1, t // LANE)))
    dp = lax.dot_general(do.astype(BF16), v, (((1,), (1,)), ((), ())), preferred_element_type=F32)
    delta = jnp.sum(do * o, axis=1, keepdims=True)
    ds = p * (dp - delta)
    return p, ds


ATT_HB_BWD = 2


def _attn_bwd_call(q, k, vsrc, o, do, lse):
    B, S, _ = q.shape
    t, n = _att_tiles(S)
    qi, kj = _causal_pairs(n)
    n_pairs = qi.shape[0]
    hb = ATT_HB_BWD
    v_blk0 = N_HEADS // hb

    def body(qi_ref, kj_ref, q_ref, k_ref, v_ref, o_ref, do_ref, lse_ref, dq_ref, dk_ref, dv_ref, dq_sc, dk_sc, dv_sc):
        p_id = pl.program_id(2)
        i, j = qi_ref[p_id], kj_ref[p_id]

        @pl.when(p_id == 0)
        def _():
            dk_sc[...] = jnp.zeros(dk_sc.shape, F32)
            dv_sc[...] = jnp.zeros(dv_sc.shape, F32)

        @pl.when(j == 0)
        def _():
            dq_sc[...] = jnp.zeros(dq_sc.shape, F32)

        rows = pl.ds(pl.multiple_of(j * t, t), t)

        def step(diagonal):
            qa, ka, va, oa, doa, la = q_ref[0], k_ref[0], v_ref[0], o_ref[0], do_ref[0], lse_ref[0]
            for h in range(hb):
                qb, kb, dob = _head(qa, h, QK_PAD), _head(ka, h, QK_PAD), _head(doa, h, V_DIM)
                p, ds = _attn_p_ds(qb, kb, _head(va, h, V_DIM), _head(oa, h, V_DIM), dob, _head(la, h, LANE),
                                   diagonal, t)
                dsb = ds.astype(BF16)
                dq_sc[:, h * QK_PAD:(h + 1) * QK_PAD] += jnp.dot(dsb, kb, preferred_element_type=F32)
                dv_sc[rows, h * V_DIM:(h + 1) * V_DIM] += lax.dot_general(
                    p.astype(BF16), dob.astype(BF16), (((0,), (0,)), ((), ())), preferred_element_type=F32)
                dk_sc[rows, h * QK_PAD:(h + 1) * QK_PAD] += lax.dot_general(
                    dsb, qb, (((0,), (0,)), ((), ())), preferred_element_type=F32)

        @pl.when(j < i)
        def _():
            step(False)

        @pl.when(j == i)
        def _():
            step(True)
            dq_ref[0] = dq_sc[...].astype(BF16)

        @pl.when(p_id == n_pairs - 1)
        def _():
            dk_ref[0] = dk_sc[...].astype(BF16)
            dv_ref[0] = dv_sc[...].astype(BF16)

    wq, wv = hb * QK_PAD, hb * V_DIM
    at_q = lambda b, h, p, qi, kj: (b, qi[p], h)
    at_k = lambda b, h, p, qi, kj: (b, kj[p], h)
    whole = lambda b, h, p, qi, kj: (b, 0, h)
    grid_spec = pltpu.PrefetchScalarGridSpec(
        num_scalar_prefetch=2, grid=(B, N_HEADS // hb, n_pairs),
        in_specs=[pl.BlockSpec((1, t, wq), at_q), pl.BlockSpec((1, t, wq), at_k),
                  pl.BlockSpec((1, t, wv), lambda b, h, p, qi, kj: (b, kj[p], v_blk0 + h)),
                  pl.BlockSpec((1, t, wv), at_q), pl.BlockSpec((1, t, wv), at_q), pl.BlockSpec((1, t, wv), at_q)],
        out_specs=[pl.BlockSpec((1, t, wq), at_q), pl.BlockSpec((1, S, wq), whole), pl.BlockSpec((1, S, wv), whole)],
        scratch_shapes=[pltpu.VMEM((t, wq), F32), pltpu.VMEM((S, wq), F32), pltpu.VMEM((S, wv), F32)])
    return pl.pallas_call(
        body, grid_spec=grid_spec,
        out_shape=[jax.ShapeDtypeStruct((B, S, N_HEADS * QK_PAD), BF16),
                   jax.ShapeDtypeStruct((B, S, N_HEADS * QK_PAD), BF16),
                   jax.ShapeDtypeStruct((B, S, N_HEADS * V_DIM), BF16)],
        name="attn_bwd", compiler_params=_cparams(("parallel", "parallel", "arbitrary")))(
            qi, kj, q, k, vsrc, o, do, lse)


@jax.custom_vjp
def attention(q, k, kv):
    return _attn_fwd_call(q, k, kv, N_HEADS // ATT_HB)[0]


def _attention_fwd(q, k, kv):
    o, lse = _attn_fwd_call(q, k, kv, N_HEADS // ATT_HB)
    return o, (q, k, kv, o, lse)


def _attention_bwd(res, do):
    q, k, kv, o, lse = res
    dq, dk, dv = _attn_bwd_call(q, k, kv, o, do, lse)
    dkv = jnp.concatenate([jnp.zeros_like(dv), dv], axis=-1)
    return dq, dk, dkv


attention.defvjp(_attention_fwd, _attention_bwd)


def _shift_down(v, sh, rows):
    return jnp.where(rows >= sh, pltpu.roll(v, sh, 0), 0.0)


def _shift_up(v, sh, rows, S):
    return jnp.where(rows < S - sh, pltpu.roll(v, S - sh, 0), 0.0)


def _conv_pre(u, w_ref, b_ref, rows):
    acc = b_ref[...] + w_ref[pl.ds(CONV_K - 1, 1), :] * u
    for k in range(CONV_K - 1):
        acc = acc + w_ref[pl.ds(k, 1), :] * _shift_down(u, CONV_K - 1 - k, rows)
    return acc


def _conv_fwd_call(u, w, b):
    B, S, C = u.shape

    def body(u_ref, w_ref, b_ref, o_ref):
        uu = u_ref[0]
        rows = lax.broadcasted_iota(jnp.int32, uu.shape, 0)
        o_ref[0] = _silu(_conv_pre(uu, w_ref, b_ref, rows))

    spec = pl.BlockSpec((1, S, LANE), lambda c, bb: (bb, 0, c))
    return pl.pallas_call(
        body, grid=(C // LANE, B),
        in_specs=[spec, pl.BlockSpec((CONV_K, LANE), lambda c, bb: (0, c)), pl.BlockSpec((1, LANE), lambda c, bb: (0, c))],
        out_specs=spec, out_shape=jax.ShapeDtypeStruct(u.shape, F32), name="conv_fwd",
        compiler_params=_cparams(("parallel", "arbitrary")))(u, w, b)


def _conv_bwd_call(u, w, b, g):
    B, S, C = u.shape

    def body(u_ref, w_ref, b_ref, g_ref, du_ref, dw_ref, db_ref):
        uu = u_ref[0]
        rows = lax.broadcasted_iota(jnp.int32, uu.shape, 0)
        pre = _conv_pre(uu, w_ref, b_ref, rows)
        sg = lax.logistic(pre)
        dpre = g_ref[0] * sg * (1.0 + pre * (1.0 - sg))
        du = w_ref[pl.ds(CONV_K - 1, 1), :] * dpre
        dws = [None] * CONV_K
        dws[CONV_K - 1] = jnp.sum(dpre * uu, axis=0, keepdims=True)
        for k in range(CONV_K - 1):
            sh = CONV_K - 1 - k
            du = du + w_ref[pl.ds(k, 1), :] * _shift_up(dpre, sh, rows, S)
            dws[k] = jnp.sum(dpre * _shift_down(uu, sh, rows), axis=0, keepdims=True)
        du_ref[0] = du
        dbv = jnp.sum(dpre, axis=0, keepdims=True)
        first = pl.program_id(1) == 0

        @pl.when(first)
        def _():
            for k in range(CONV_K):
                dw_ref[pl.ds(k, 1), :] = dws[k]
            db_ref[...] = dbv

        @pl.when(jnp.logical_not(first))
        def _():
            for k in range(CONV_K):
                dw_ref[pl.ds(k, 1), :] += dws[k]
            db_ref[...] += dbv

    spec = pl.BlockSpec((1, S, LANE), lambda c, bb: (bb, 0, c))
    wspec = pl.BlockSpec((CONV_K, LANE), lambda c, bb: (0, c))
    bspec = pl.BlockSpec((1, LANE), lambda c, bb: (0, c))
    return pl.pallas_call(
        body, grid=(C // LANE, B), in_specs=[spec, wspec, bspec, spec], out_specs=[spec, wspec, bspec],
        out_shape=[jax.ShapeDtypeStruct(u.shape, F32), jax.ShapeDtypeStruct(w.shape, F32),
                   jax.ShapeDtypeStruct(b.shape, F32)],
        name="conv_bwd", compiler_params=_cparams(("parallel", "arbitrary")))(u, w, b, g)


@jax.custom_vjp
def conv_silu(u, w, b):
    return _conv_fwd_call(u, w, b)


def _conv_silu_fwd(u, w, b):
    return _conv_fwd_call(u, w, b), (u, w, b)


def _conv_silu_bwd(res, g):
    return tuple(_conv_bwd_call(*res, g))


conv_silu.defvjp(_conv_silu_fwd, _conv_silu_bwd)


def _chunk_cumsum_call(a, reverse, name):
    B, S, W = a.shape

    def body(a_ref, o_ref):
        r = lax.broadcasted_iota(jnp.int32, (CHUNK, CHUNK), 0)
        c = lax.broadcasted_iota(jnp.int32, (CHUNK, CHUNK), 1)
        tri = jnp.where((c >= r) if reverse else (c <= r), 1.0, 0.0).astype(F32)
        o_ref[0] = jnp.dot(tri, a_ref[0], preferred_element_type=F32, precision=lax.Precision.HIGHEST)

    spec = pl.BlockSpec((1, CHUNK, W), lambda b, c: (b, c, 0))
    return pl.pallas_call(body, grid=(B, S // CHUNK), in_specs=[spec], out_specs=spec,
                          out_shape=jax.ShapeDtypeStruct(a.shape, F32), name=name,
                          compiler_params=_cparams(("parallel", "parallel")))(a)


@jax.custom_vjp
def chunk_cumsum(a):
    return _chunk_cumsum_call(a, False, "chunk_cumsum_fwd")


chunk_cumsum.defvjp(lambda a: (_chunk_cumsum_call(a, False, "chunk_cumsum_fwd"), None),
                    lambda _, g: (_chunk_cumsum_call(g, True, "chunk_cumsum_bwd"),))


GROUP_W = 4 * HEAD_P
HPG = SSM_HEADS // SSM_GROUPS


def _ssd_masks():
    lane = lax.broadcasted_iota(jnp.int32, (1, GROUP_W), 1)
    return [((lane >= HEAD_P * j) & (lane < HEAD_P * (j + 1))).astype(F32) for j in range(HPG)]


def _ssd_decays(ac_cols, acr_ref, gi):
    r = lax.broadcasted_iota(jnp.int32, (CHUNK, CHUNK), 0)
    c = lax.broadcasted_iota(jnp.int32, (CHUNK, CHUNK), 1)
    return [jnp.exp(jnp.where(c <= r, ac_cols[j] - acr_ref[0, gi * HPG + j], NEG)) for j in range(HPG)]


def _ssd_cols(blk, g):
    lane = lax.broadcasted_iota(jnp.int32, blk.shape, 1)
    return [jnp.sum(jnp.where(lane == HPG * g + j, blk, 0.0), axis=1, keepdims=True) for j in range(HPG)]


def _ssd_spread(cols, masks):
    out = cols[0] * masks[0]
    for j in range(1, HPG):
        out = out + cols[j] * masks[j]
    return out


def _ssd_gather(val, cols, masks, g):
    lane = lax.broadcasted_iota(jnp.int32, (1, LANE), 1)
    out = jnp.zeros((CHUNK, LANE), F32)
    for j in range(HPG):
        tot = jnp.sum(val * masks[j], axis=1, keepdims=True)
        if cols is not None:
            tot = tot + cols[j]
        out = out + tot * (lane == HPG * g + j).astype(F32)
    return out


def _dot(a, b, dims):
    return lax.dot_general(a.astype(BF16), b.astype(BF16), (dims, ((), ())), preferred_element_type=F32)


NN = ((1,), (0,))
NT = ((1,), (1,))
TN = ((0,), (0,))


XBC_W = GROUP_W + 2 * STATE_N


SSD_GB = 2


def _ssd_load(xbc_ref, dt_ref, ac_ref, masks, g, gi):
    x = xbc_ref[0, :, gi * XBC_W:gi * XBC_W + GROUP_W]
    bm = xbc_ref[0, :, gi * XBC_W + GROUP_W:gi * XBC_W + GROUP_W + STATE_N]
    cm = xbc_ref[0, :, gi * XBC_W + GROUP_W + STATE_N:(gi + 1) * XBC_W]
    ac_cols = _ssd_cols(ac_ref[0], g)
    dt = _ssd_spread(_ssd_cols(dt_ref[0], g), masks)
    ac = _ssd_spread(ac_cols, masks)
    is_last = (lax.broadcasted_iota(jnp.int32, (CHUNK, GROUP_W), 0) == CHUNK - 1).astype(F32)
    return x, bm, cm, dt, ac, ac_cols, is_last


def _ssd_in_specs(nc, rev):
    cc = (lambda c: nc - 1 - c) if rev else (lambda c: c)
    return [pl.BlockSpec((1, CHUNK, SSD_GB * XBC_W), lambda b, g, c: (b, cc(c), g)),
            pl.BlockSpec((1, CHUNK, LANE), lambda b, g, c: (b, cc(c), 0)),
            pl.BlockSpec((1, CHUNK, LANE), lambda b, g, c: (b, cc(c), 0)),
            pl.BlockSpec((1, SSD_GB * HPG, 1, CHUNK), lambda b, g, c: (b, g, 0, cc(c))),
            pl.BlockSpec((1, SSD_GB * GROUP_W), lambda b, g, c: (0, g))]


def _ssd_fwd_call(xbc, dtp, acp, acr, dsk):
    B, S, _ = xbc.shape
    nc = S // CHUNK

    def body(xbc_ref, dt_ref, ac_ref, ar_ref, ds_ref, y_ref, hp_ref, h_sc):
        @pl.when(pl.program_id(2) == 0)
        def _():
            h_sc[...] = jnp.zeros(h_sc.shape, F32)

        masks = _ssd_masks()
        ys = []
        for gi in range(SSD_GB):
            grp = SSD_GB * pl.program_id(1) + gi
            x, bm, cm, dt, ac, ac_cols, is_last = _ssd_load(xbc_ref, dt_ref, ac_ref, masks, grp, gi)
            last = jnp.sum(ac * is_last, axis=0, keepdims=True)
            decays = _ssd_decays(ac_cols, ar_ref, gi)
            xd = x * dt
            cb = _dot(cm, bm, NT)
            hprev = h_sc[gi]
            hp_ref[0, gi, 0] = hprev
            y = _dot(cm, hprev, NN) * jnp.exp(ac) + ds_ref[:, gi * GROUP_W:(gi + 1) * GROUP_W] * x
            for j in range(HPG):
                y = y + _dot(cb * decays[j], xd * masks[j], NN)
            ys.append(y)
            h_sc[gi] = hprev * jnp.exp(last) + _dot(bm, xd * jnp.exp(last - ac), TN)
        y_ref[0] = jnp.concatenate(ys, axis=1)

    ng = SSM_GROUPS // SSD_GB
    return pl.pallas_call(
        body, grid=(B, ng, nc), in_specs=_ssd_in_specs(nc, False),
        out_specs=[pl.BlockSpec((1, CHUNK, SSD_GB * GROUP_W), lambda b, g, c: (b, c, g)),
                   pl.BlockSpec((1, SSD_GB, 1, STATE_N, GROUP_W), lambda b, g, c: (b, g, c, 0, 0))],
        out_shape=[jax.ShapeDtypeStruct((B, S, D_INNER), F32),
                   jax.ShapeDtypeStruct((B, SSM_GROUPS, nc, STATE_N, GROUP_W), F32)],
        scratch_shapes=[pltpu.VMEM((SSD_GB, STATE_N, GROUP_W), F32)], name="ssd_fwd",
        compiler_params=_cparams(("parallel", "parallel", "arbitrary")))(xbc, dtp, acp, acr, dsk)


def _ssd_bwd_call(xbc, dtp, acp, acr, dsk, hps, dy):
    B, S, _ = xbc.shape
    nc = S // CHUNK

    def body(xbc_ref, dt_ref, ac_ref, ar_ref, ds_ref, hp_ref, dy_ref,
             dxbc_ref, ddt_ref, dac_ref, dar_ref, dds_ref, dh_sc):
        first = pl.program_id(2) == 0

        @pl.when(first)
        def _():
            dh_sc[...] = jnp.zeros(dh_sc.shape, F32)

        masks = _ssd_masks()
        dxbc_parts, dds_parts = [], []
        for gi in range(SSD_GB):
            grp = SSD_GB * pl.program_id(0) + gi
            x, bm, cm, dt, ac, ac_cols, is_last = _ssd_load(xbc_ref, dt_ref, ac_ref, masks, grp, gi)
            last = jnp.sum(ac * is_last, axis=0, keepdims=True)
            g = dy_ref[0, :, gi * GROUP_W:(gi + 1) * GROUP_W]
            hprev = hp_ref[0, gi, 0]
            dh = dh_sc[gi]
            decays = _ssd_decays(ac_cols, ar_ref, gi)
            dcols = []
            xd = x * dt
            cb = _dot(cm, bm, NT)
            e_c = jnp.exp(ac)
            e_end = jnp.exp(last - ac)
            e_last = jnp.exp(last)
            z = _dot(cm, hprev, NN)
            dz = g * e_c
            dac = g * z * e_c
            dc = _dot(dz, hprev, NT)
            dhprev = _dot(cm, dz, TN) + dh * e_last
            dcb = jnp.zeros((CHUNK, CHUNK), F32)
            dxd = jnp.zeros(xd.shape, F32)
            for j in range(HPG):
                gj = cb * decays[j]
                dgj = _dot(g * masks[j], xd, NT)
                dxd = dxd + _dot(gj, g, TN) * masks[j]
                dcb = dcb + dgj * decays[j]
                dseg = dgj * gj
                dcols.append(jnp.sum(dseg, axis=1, keepdims=True))
                dar_ref[0, gi * HPG + j] = -jnp.sum(dseg, axis=0, keepdims=True)
            dc = dc + _dot(dcb, bm, NN)
            db = _dot(dcb, cm, TN)
            sx = xd * e_end
            db = db + _dot(sx, dh, NT)
            dsx = _dot(bm, dh, NN)
            dxd = dxd + dsx * e_end
            de = dsx * sx
            dac = dac - de
            dlast = jnp.sum(de, axis=0, keepdims=True) + jnp.sum(dh * hprev, axis=0, keepdims=True) * e_last
            dsk = ds_ref[:, gi * GROUP_W:(gi + 1) * GROUP_W]
            dxbc_parts += [dxd * dt + dsk * g, db, dc]
            ddt_ref[0, gi] = _ssd_gather(dxd * x, None, masks, grp)
            dac_ref[0, gi] = _ssd_gather(dac + is_last * dlast, dcols, masks, grp)
            dds_parts.append(jnp.sum(g * x, axis=0, keepdims=True))
            dh_sc[gi] = dhprev
        dxbc_ref[0] = jnp.concatenate(dxbc_parts, axis=1)
        dds = jnp.concatenate(dds_parts, axis=1)
        first_all = first & (pl.program_id(1) == 0)

        @pl.when(first_all)
        def _():
            dds_ref[...] = dds

        @pl.when(jnp.logical_not(first_all))
        def _():
            dds_ref[...] += dds

    rc = lambda c: nc - 1 - c
    ng = SSM_GROUPS // SSD_GB
    in_specs = [pl.BlockSpec(s.block_shape, (lambda g, b, c, f=s.index_map: f(b, g, c))) for s in _ssd_in_specs(nc, True)]
    in_specs.append(pl.BlockSpec((1, SSD_GB, 1, STATE_N, GROUP_W), lambda g, b, c: (b, g, rc(c), 0, 0)))
    in_specs.append(pl.BlockSpec((1, CHUNK, SSD_GB * GROUP_W), lambda g, b, c: (b, rc(c), g)))
    per_group = pl.BlockSpec((1, SSD_GB, CHUNK, LANE), lambda g, b, c: (b, g, rc(c), 0))
    out_specs = [pl.BlockSpec((1, CHUNK, SSD_GB * XBC_W), lambda g, b, c: (b, rc(c), g)), per_group, per_group,
                 pl.BlockSpec((1, SSD_GB * HPG, 1, CHUNK), lambda g, b, c: (b, g, 0, rc(c))),
                 pl.BlockSpec((1, SSD_GB * GROUP_W), lambda g, b, c: (0, g))]
    out_shape = [jax.ShapeDtypeStruct(xbc.shape, F32),
                 jax.ShapeDtypeStruct((B, SSM_GROUPS, S, LANE), F32), jax.ShapeDtypeStruct((B, SSM_GROUPS, S, LANE), F32),
                 jax.ShapeDtypeStruct(acr.shape, F32), jax.ShapeDtypeStruct(dsk.shape, F32)]
    return pl.pallas_call(
        body, grid=(ng, B, nc), in_specs=in_specs, out_specs=out_specs, out_shape=out_shape,
        scratch_shapes=[pltpu.VMEM((SSD_GB, STATE_N, GROUP_W), F32)], name="ssd_bwd",
        compiler_params=_cparams(("arbitrary", "arbitrary", "arbitrary")))(xbc, dtp, acp, acr, dsk, hps, dy)


@jax.custom_vjp
def ssd(xbc, dtp, acp, acr, dsk):
    return _ssd_fwd_call(xbc, dtp, acp, acr, dsk)[0]


def _ssd_fwd(xbc, dtp, acp, acr, dsk):
    y, hps = _ssd_fwd_call(xbc, dtp, acp, acr, dsk)
    return y, (xbc, dtp, acp, acr, dsk, hps)


def _ssd_bwd(res, dy):
    dxbc, ddt, dac, dacr, dds = _ssd_bwd_call(*res, dy)
    return dxbc, jnp.sum(ddt, axis=1), jnp.sum(dac, axis=1), dacr, dds


ssd.defvjp(_ssd_fwd, _ssd_bwd)


def _pack_small(arrs):
    flat = jnp.concatenate([a.reshape(-1) for a in arrs])
    rows = -(-flat.shape[0] // (8 * LANE)) * 8
    return jnp.pad(flat, (0, rows * LANE - flat.shape[0])).reshape(rows, LANE)


def _unpack_small(buf, shapes):
    flat = buf.reshape(-1)
    out, off = [], 0
    for shp in shapes:
        n = int(np.prod(shp))
        out.append(flat[off:off + n].reshape(shp))
        off += n
    return out


def _rows_tile(rows, cap):
    for cand in range(min(rows, cap), 7, -8):
        if rows % cand == 0:
            return cand
    return rows


def _pair_sum(mine, theirs, cidx, name):
    n4, kk, nn = mine.shape
    half = kk // 2
    tr = _rows_tile(half, 256)
    nb = half // tr

    def body(c_ref, a_ref, b_ref, o_ref, ob_ref):
        tot = a_ref[...] + b_ref[...]
        o_ref[...] = tot
        ob_ref[...] = tot.astype(BF16)

    spec = pl.BlockSpec((1, tr, nn), lambda j, i, c: (j, i, 0))
    grid_spec = pltpu.PrefetchScalarGridSpec(
        num_scalar_prefetch=1, grid=(n4, nb),
        in_specs=[pl.BlockSpec((1, tr, nn), lambda j, i, c: (j, c[0] * nb + i, 0)), spec], out_specs=[spec, spec])
    return pl.pallas_call(
        body, grid_spec=grid_spec,
        out_shape=[jax.ShapeDtypeStruct((n4, half, nn), F32), jax.ShapeDtypeStruct((n4, half, nn), BF16)],
        name=name, compiler_params=_cparams(("parallel", "parallel")))(cidx, mine, theirs)


def _chip_sum(quad, pair, chip_idx, name):
    _, rows, nn = quad.shape
    tr = _rows_tile(rows, 256)

    def body(s_ref, q_ref, p_ref, o_ref):
        for mine in range(4):
            @pl.when(s_ref[0] == mine)
            def _(mine=mine):
                acc = None
                for d in range(4):
                    term = p_ref[0] if d == mine else q_ref[d].astype(F32)
                    acc = term if acc is None else acc + term
                o_ref[...] = acc

    grid_spec = pltpu.PrefetchScalarGridSpec(
        num_scalar_prefetch=1, grid=(rows // tr,),
        in_specs=[pl.BlockSpec((4, tr, nn), lambda i, s: (0, i, 0)), pl.BlockSpec((1, tr, nn), lambda i, s: (s[0], i, 0))],
        out_specs=pl.BlockSpec((tr, nn), lambda i, s: (i, 0)))
    return pl.pallas_call(body, grid_spec=grid_spec, out_shape=jax.ShapeDtypeStruct((rows, nn), F32), name=name,
                          compiler_params=_cparams(("parallel",)))(chip_idx, quad, pair)


def _adam_halves_call(w, mine, other, cidx, m, v, name):
    rows, nn = w.shape
    half = rows // 2
    tr = _rows_tile(half, 128)
    nb = half // tr

    def body(c_ref, w_ref, a_ref, b_ref, m_ref, v_ref, g_ref, d_ref, nm_ref, nv_ref):
        upper = (pl.program_id(0) >= nb).astype(jnp.int32)
        g = jnp.where(upper == c_ref[0], a_ref[...], b_ref[...])
        g_ref[...] = g
        d_ref[...], nm_ref[...], nv_ref[...] = _adam_fn(w_ref[...], g, m_ref[...], v_ref[...])

    spec = pl.BlockSpec((tr, nn), lambda i, c: (i, 0))
    hspec = pl.BlockSpec((tr, nn), lambda i, c: (i % nb, 0))
    grid_spec = pltpu.PrefetchScalarGridSpec(num_scalar_prefetch=1, grid=(2 * nb,),
                                             in_specs=[spec, hspec, hspec, spec, spec], out_specs=[spec] * 4)
    return pl.pallas_call(body, grid_spec=grid_spec, out_shape=[jax.ShapeDtypeStruct((rows, nn), F32)] * 4, name=name,
                          compiler_params=_cparams(("parallel",)))(cidx, w, mine, other, m, v)


def _stack_sum(stack, name):
    n, rows, nn = stack.shape
    tr = _rows_tile(rows, 256)

    def body(s_ref, o_ref):
        acc = s_ref[0]
        for d in range(1, n):
            acc = acc + s_ref[d]
        o_ref[...] = acc

    return pl.pallas_call(
        body, grid=(rows // tr,), in_specs=[pl.BlockSpec((n, tr, nn), lambda i: (0, i, 0))],
        out_specs=pl.BlockSpec((tr, nn), lambda i: (i, 0)), out_shape=jax.ShapeDtypeStruct((rows, nn), F32),
        name=name, compiler_params=_cparams(("parallel",)))(stack)


def _adam_call(w, g, m, v, name):
    rows, nn = w.shape
    tr = _rows_tile(rows, 128)

    def body(w_ref, g_ref, m_ref, v_ref, d_ref, nm_ref, nv_ref):
        d_ref[...], nm_ref[...], nv_ref[...] = _adam_fn(w_ref[...], g_ref[...], m_ref[...], v_ref[...])

    spec = pl.BlockSpec((tr, nn), lambda i: (i, 0))
    sds = jax.ShapeDtypeStruct((rows, nn), F32)
    return pl.pallas_call(body, grid=(rows // tr,), in_specs=[spec] * 4, out_specs=[spec] * 3,
                          out_shape=[sds] * 3, name=name, compiler_params=_cparams(("parallel",)))(w, g, m, v)


def _adam_fn(w, g, m, v):
    m = ADAM_B1 * m + (1.0 - ADAM_B1) * g
    v = ADAM_B2 * v + (1.0 - ADAM_B2) * (g * g)
    m_hat = m / (1.0 - ADAM_B1 ** ADAM_STEP)
    v_hat = v / (1.0 - ADAM_B2 ** ADAM_STEP)
    delta = -ADAM_LR * (m_hat / (jnp.sqrt(v_hat) + ADAM_EPS) + ADAM_WD * w)
    return delta, m, v


def _mesh_pos():
    return lax.axis_index("x"), lax.axis_index("y"), lax.axis_index("c")


def _other_chips(x, y):
    return [(1 - x, y), (x, 1 - y), (1 - x, 1 - y)]


HBM_SPEC = pl.BlockSpec(memory_space=pl.ANY)


def _remote(src, dst, send_sems, recv_sems, k, to):
    return pltpu.make_async_remote_copy(src_ref=src, dst_ref=dst, send_sem=send_sems.at[k], recv_sem=recv_sems.at[k],
                                        device_id=to, device_id_type=MESH)


def _half_rows(c, rows, align):
    half = rows // 2
    return (pl.ds(pl.multiple_of(c * half, align), half), pl.ds(pl.multiple_of((1 - c) * half, align), half))


def _gather_weights(mats, conv):
    n = len(mats)

    def body(*refs):
        ins, conv_in = refs[:n], refs[n]
        outs, conv_out = refs[n + 1:2 * n + 1], refs[2 * n + 1]
        send_sems, recv_sems, local_sem = refs[2 * n + 2:]
        x, y, c = _mesh_pos()
        me, sibling, s = (x, y, c), (x, y, 1 - c), 2 * x + y
        chips = _other_chips(x, y)
        rows = [_half_rows(c, m.shape[0], 16) for m in mats]
        own = pltpu.make_async_copy(conv_in, conv_out.at[s], local_sem)
        own.start()
        sent = []
        for i in range(n):
            mine = rows[i][0]
            for j, (cx, cy) in enumerate(chips):
                sent.append(_remote(ins[i].at[mine], outs[i].at[s, mine], send_sems, recv_sems, 6 * i + j, (cx, cy, c)))
        for j, (cx, cy) in enumerate(chips):
            sent.append(_remote(conv_in, conv_out.at[s], send_sems, recv_sems, 6 * n + j, (cx, cy, c)))
        for cp in sent:
            cp.start()
        for i in range(n):
            mine = rows[i][0]
            for j, (cx, cy) in enumerate(chips):
                landed = outs[i].at[2 * cx + cy, mine]
                _remote(landed, landed, send_sems, recv_sems, 6 * i + j, me).wait_recv()
                fwd = _remote(landed, landed, send_sems, recv_sems, 6 * i + 3 + j, sibling)
                fwd.start()
                sent.append(fwd)
        for j, (cx, cy) in enumerate(chips):
            slot = conv_out.at[2 * cx + cy]
            _remote(slot, slot, send_sems, recv_sems, 6 * n + j, me).wait_recv()
        for i in range(n):
            theirs_rows = rows[i][1]
            for j, (cx, cy) in enumerate(chips):
                theirs = outs[i].at[2 * cx + cy, theirs_rows]
                _remote(theirs, theirs, send_sems, recv_sems, 6 * i + 3 + j, me).wait_recv()
        for cp in sent:
            cp.wait_send()
        own.wait()

    out_shape = [jax.ShapeDtypeStruct((4,) + m.shape, m.dtype) for m in mats]
    out_shape.append(jax.ShapeDtypeStruct((4,) + conv.shape, conv.dtype))
    res = pl.pallas_call(
        body, in_specs=[HBM_SPEC] * (n + 1), out_specs=[HBM_SPEC] * (n + 1), out_shape=out_shape,
        scratch_shapes=[pltpu.SemaphoreType.DMA((6 * n + 3,)), pltpu.SemaphoreType.DMA((6 * n + 3,)),
                        pltpu.SemaphoreType.DMA],
        name="all_gather_weights")(*mats, conv)
    chip = 2 * lax.axis_index("x") + lax.axis_index("y")
    full = [lax.dynamic_update_slice_in_dim(r, m[None], chip, axis=0) for r, m in zip(res[:n], mats)]
    return full, res[n]


def _sibling_exchange(stacks):
    n = len(stacks)

    def body(*refs):
        ins, outs = refs[:n], refs[n:2 * n]
        send_sems, recv_sems = refs[2 * n:]
        x, y, c = _mesh_pos()
        cps = []
        for i in range(n):
            theirs = _half_rows(c, stacks[i].shape[1], 8)[1]
            cps.append(_remote(ins[i].at[:, theirs, :], outs[i], send_sems, recv_sems, i, (x, y, 1 - c)))
        for cp in cps:
            cp.start()
        for cp in cps:
            cp.wait()

    out_shape = [jax.ShapeDtypeStruct((4, s.shape[1] // 2, s.shape[2]), s.dtype) for s in stacks]
    return pl.pallas_call(
        body, in_specs=[HBM_SPEC] * n, out_specs=[HBM_SPEC] * n, out_shape=out_shape,
        scratch_shapes=[pltpu.SemaphoreType.DMA((n,)), pltpu.SemaphoreType.DMA((n,))],
        name="grad_sibling_exchange")(*stacks)


def _chip_exchange(parts):
    n = len(parts)

    def body(*refs):
        ins, outs = refs[:n], refs[n:2 * n]
        send_sems, recv_sems = refs[2 * n:]
        x, y, c = _mesh_pos()
        me, s = (x, y, c), 2 * x + y
        chips = _other_chips(x, y)
        sent = [_remote(ins[i].at[2 * cx + cy], outs[i].at[s], send_sems, recv_sems, 3 * i + j, (cx, cy, c))
                for i in range(n) for j, (cx, cy) in enumerate(chips)]
        for cp in sent:
            cp.start()
        for i in range(n):
            for j, (cx, cy) in enumerate(chips):
                slot = outs[i].at[2 * cx + cy]
                _remote(slot, slot, send_sems, recv_sems, 3 * i + j, me).wait_recv()
        for cp in sent:
            cp.wait_send()

    return pl.pallas_call(
        body, in_specs=[HBM_SPEC] * n, out_specs=[HBM_SPEC] * n,
        out_shape=[jax.ShapeDtypeStruct(p.shape, p.dtype) for p in parts],
        scratch_shapes=[pltpu.SemaphoreType.DMA((3 * n,)), pltpu.SemaphoreType.DMA((3 * n,))],
        name="grad_chip_exchange")(*parts)


def _sibling_swap(halves):
    n = len(halves)

    def body(*refs):
        ins, outs = refs[:n], refs[n:2 * n]
        send_sems, recv_sems = refs[2 * n:]
        x, y, c = _mesh_pos()
        cps = [_remote(ins[i], outs[i], send_sems, recv_sems, i, (x, y, 1 - c)) for i in range(n)]
        for cp in cps:
            cp.start()
        for cp in cps:
            cp.wait()

    return pl.pallas_call(
        body, in_specs=[HBM_SPEC] * n, out_specs=[HBM_SPEC] * n,
        out_shape=[jax.ShapeDtypeStruct(h.shape, h.dtype) for h in halves],
        scratch_shapes=[pltpu.SemaphoreType.DMA((n,)), pltpu.SemaphoreType.DMA((n,))],
        name="grad_sibling_swap")(*halves)


def _gather_small(vec):
    def body(in_ref, out_ref, send_sems, recv_sems, local_sem):
        x, y, c = _mesh_pos()
        me = (x, y, c)
        own = pltpu.make_async_copy(in_ref, out_ref.at[4 * x + 2 * y + c], local_sem)
        own.start()
        peers = [(1 - x if k & 4 else x, 1 - y if k & 2 else y, 1 - c if k & 1 else c) for k in range(1, 8)]
        sent = [_remote(in_ref, out_ref.at[4 * x + 2 * y + c], send_sems, recv_sems, k, p) for k, p in enumerate(peers)]
        for cp in sent:
            cp.start()
        for k, (px, py, pc) in enumerate(peers):
            slot = out_ref.at[4 * px + 2 * py + pc]
            _remote(slot, slot, send_sems, recv_sems, k, me).wait_recv()
        for cp in sent:
            cp.wait_send()
        own.wait()

    return pl.pallas_call(
        body, in_specs=[HBM_SPEC], out_specs=HBM_SPEC, out_shape=jax.ShapeDtypeStruct((8,) + vec.shape, vec.dtype),
        scratch_shapes=[pltpu.SemaphoreType.DMA((7,)), pltpu.SemaphoreType.DMA((7,)), pltpu.SemaphoreType.DMA],
        name="grad_gather_small")(vec)


def _reduce_matrices(stacks, names):
    cidx = lax.axis_index("c").astype(jnp.int32).reshape(1)
    chip = (2 * lax.axis_index("x") + lax.axis_index("y")).astype(jnp.int32).reshape(1)
    got = _sibling_exchange(stacks)
    pairs = [_pair_sum(a, b, cidx, "grad_pair_sum_" + nm) for a, b, nm in zip(stacks, got, names)]
    quads = _chip_exchange([p[1] for p in pairs])
    mine = [_chip_sum(q, p[0], chip, "grad_chip_sum_" + nm) for q, p, nm in zip(quads, pairs, names)]
    return mine, _sibling_swap(mine)


def _pad_cols(a, n):
    return jnp.concatenate([a, jnp.zeros((a.shape[0], n - a.shape[1]), a.dtype)], axis=1)


def _group_channels(a):
    lead = a.shape[:-1]
    xs = a[..., :D_INNER].reshape(lead + (SSM_GROUPS, GROUP_W))
    bs = a[..., D_INNER:D_INNER + SSM_GROUPS * STATE_N].reshape(lead + (SSM_GROUPS, STATE_N))
    cs = a[..., D_INNER + SSM_GROUPS * STATE_N:].reshape(lead + (SSM_GROUPS, STATE_N))
    return jnp.concatenate([xs, bs, cs], axis=-1).reshape(lead + (CONV_CH,))


def _lay_w_in(w):
    idx = np.cumsum(IN_SIZES)[:-1]
    segs = jnp.split(w, [int(v) for v in idx], axis=1)
    segs[2] = _pad_cols(segs[2], LANE)
    segs[4] = _group_channels(segs[4])
    segs[5] = _pad_cols(segs[5], LANE)
    return jnp.concatenate(segs, axis=1)


IN_PAD_SIZES = (Q_RANK, KV_RANK, LANE, D_INNER, CONV_CH, LANE, D_MODEL, D_MODEL)
IN_PAD_OFFS = [int(v) for v in np.cumsum(IN_PAD_SIZES)[:-1]]


@jax.custom_vjp
def split_proj(proj):
    return tuple(jnp.split(proj, IN_PAD_OFFS, axis=-1))


split_proj.defvjp(lambda proj: (tuple(jnp.split(proj, IN_PAD_OFFS, axis=-1)), None),
                  lambda _, cots: (jnp.concatenate(cots, axis=-1),))


def _lay_w_uq(w):
    w3 = w.reshape(Q_RANK, N_HEADS, NOPE + ROPE)
    w3 = jnp.concatenate([w3, jnp.zeros((Q_RANK, N_HEADS, QK_PAD - NOPE - ROPE), w.dtype)], axis=2)
    return w3.reshape(Q_RANK, N_HEADS * QK_PAD)


def _lay_w_ukv(w):
    w3 = w.reshape(KV_RANK, N_HEADS, NOPE + V_DIM)
    return jnp.concatenate([w3[:, :, :NOPE].reshape(KV_RANK, -1), w3[:, :, NOPE:].reshape(KV_RANK, -1)], axis=1)


def _pad_lanes(v, n=LANE):
    return jnp.concatenate([v, jnp.zeros((v.shape[0], n - v.shape[1]), v.dtype)], axis=1)


def _local_loss(toks, small, x, wb, c8, posf, target):
    B, S, D = x.shape
    T = B * S

    def lin(name, a, key, lay=lambda w: w, out_dtype=F32):
        return make_linear(name, out_dtype)(a, lay(wb[key]), lay(toks[key]))

    rows2 = lambda a: a.reshape(T, a.shape[-1])
    rows3 = lambda a: a.reshape(B, S, a.shape[-1])

    sc = make_rowwise("silu_c", _f_silu, 1, 0, 0, ('row',))((c8[None],), (), ())[0][0]
    mod = lin("ada", sc, 'w_ada')[:B] + small['b_ada']
    shift1, scale1, gate1, shift2, scale2, gate2 = [m[:, None, :] for m in jnp.split(mod, 6, axis=-1)]

    modulate = make_rowwise("modulate1", _f_modulate, 1, 2, 1, ('row',))
    h = modulate((x,), (scale1, shift1), (small['g_pre_mix'],))[0]
    proj = rows3(lin("w_in", rows2(h), 'w_in', _lay_w_in))
    q_lat, kv_lat, k_rope, z, xbc, dt_raw, gate_a, gate_b = split_proj(proj)

    inv = ROPE_THETA ** (-jnp.arange(ROPE // 2, dtype=F32) / (ROPE // 2))
    inv_lane = jnp.concatenate([inv, inv, jnp.zeros((LANE - ROPE,), F32)])[None]
    tabs = tuple(_rope_tables(posf, inv_lane))
    qn = make_rowwise("rms_q", _f_rms, 1, 0, 1, ('row',))((q_lat,), (), (small['g_q_lat'],))[0]
    kvn = make_rowwise("rms_kv", _f_rms, 1, 0, 1, ('row',))((kv_lat,), (), (small['g_kv_lat'],))[0]
    qp = rows3(lin("w_uq", rows2(qn), 'w_uq', _lay_w_uq))
    kvp = rows3(lin("w_ukv", rows2(kvn), 'w_ukv', _lay_w_ukv, BF16))
    qr = rope_q(qp, tabs)
    kr = build_k(kvp, k_rope, tabs)
    att = attention(qr, kr, kvp)
    attn = rows3(lin("w_o_attn", rows2(att), 'w_o_attn'))

    xa = conv_silu(xbc, _group_channels(wb['conv_w_f32']), _group_channels(small['conv_b']))
    dt_pad, a_pad = make_rowwise("dt_softplus", _f_dt, 1, 0, 2, ('row', 'row'))(
        (dt_raw,), (), (_pad_lanes(small['dt_bias']), _pad_lanes(small['a_log'])))
    ac_pad = chunk_cumsum(a_pad)
    acr = jnp.transpose(ac_pad[..., :SSM_HEADS], (0, 2, 1))[:, :, None, :]
    dsk = jnp.repeat(small['d_skip'], HEAD_P, axis=-1)
    y = ssd(xa, dt_pad, ac_pad, acr, dsk)
    yg = make_rowwise("gated_norm", _f_gated_norm, 2, 0, 1, ('row',), ncol=SSM_GROUPS, ts_cap=2048)(
        (y, z), (), (small['g_ssm_out'],))[0]
    ssm = rows3(lin("w_o_ssm", rows2(yg), 'w_o_ssm'))

    merged = make_rowwise("merge", _f_merge, 4, 0, 0, ('row',))((attn, ssm, gate_a, gate_b), (), ())[0]
    mix = rows3(lin("w_out", rows2(merged), 'w_out'))
    x1 = make_rowwise("post_mix", _f_post, 2, 1, 1, ('row',))((x, mix), (gate1,), (small['g_post_mix'],))[0]

    h2 = make_rowwise("modulate2", _f_modulate, 1, 2, 1, ('row',))((x1,), (scale2, shift2), (small['g_pre_mlp'],))[0]
    ff = rows3(ffn(rows2(h2), wb['w_ff1'], toks['w_ff1'], wb['w_ff2'], toks['w_ff2']))
    lvec = make_rowwise("final_loss", _f_final_loss, 3, 1, 1, ('sum',), nodiff=(2,))(
        (x1, ff, target), (gate2,), (small['g_post_mlp'],))[0]
    return jnp.sum(lvec)


MATRICES = COL_SHARDED + ROW_SHARDED


def _local_step(x, c, positions, target, wb, small):
    B = x.shape[0]
    c8 = jnp.concatenate([c, jnp.zeros((16 - B, c.shape[1]), F32)], axis=0)
    posf = positions.astype(F32)[..., None]
    toks = {k: jnp.zeros(wb[k].shape, F32) for k in MATRICES if k != 'conv_w'}
    conv_w = wb['conv_w_f32']

    def loss_fn(toks, small, conv_w, x):
        wbl = dict(wb)
        wbl['conv_w_f32'] = conv_w
        return _local_loss(toks, small, x, wbl, c8, posf, target)

    loss, (g_tok, g_small, g_conv, g_x) = jax.value_and_grad(loss_fn, argnums=(0, 1, 2, 3))(toks, small, conv_w, x)
    grads = dict(g_tok)
    grads.update(g_small)
    grads['conv_w'] = g_conv
    return loss, g_x, grads


def kernel(x, c, positions, w_ada, b_ada, g_pre_mix, g_post_mix, w_in, g_q_lat, g_kv_lat, w_uq, w_ukv, w_o_attn, conv_w, conv_b, dt_bias, a_log, d_skip, g_ssm_out, w_o_ssm, w_out, g_pre_mlp, g_post_mlp, w_ff1, w_ff2, loss_target, m_w_ada, m_b_ada, m_g_pre_mix, m_g_post_mix, m_w_in, m_g_q_lat, m_g_kv_lat, m_w_uq, m_w_ukv, m_w_o_attn, m_conv_w, m_conv_b, m_dt_bias, m_a_log, m_d_skip, m_g_ssm_out, m_w_o_ssm, m_w_out, m_g_pre_mlp, m_g_post_mlp, m_w_ff1, m_w_ff2, v_w_ada, v_b_ada, v_g_pre_mix, v_g_post_mix, v_w_in, v_g_q_lat, v_g_kv_lat, v_w_uq, v_w_ukv, v_w_o_attn, v_conv_w, v_conv_b, v_dt_bias, v_a_log, v_d_skip, v_g_ssm_out, v_w_o_ssm, v_w_out, v_g_pre_mlp, v_g_post_mlp, v_w_ff1, v_w_ff2):
    given = dict(locals())
    w_loc = {n: given[n] for n in WEIGHTS}
    m_loc = {n: given["m_" + n] for n in WEIGHTS}
    v_loc = {n: given["v_" + n] for n in WEIGHTS}
    mats = [n for n in WEIGHTS if n in MATRICES and n != 'conv_w']
    vecs = [n for n in WEIGHTS if n not in MATRICES]

    g_mats, g_conv = _gather_weights([w_loc[n][0].astype(BF16) for n in mats], conv_w[0])
    wb = {}
    for n, g in zip(mats, g_mats):
        if n in COL_SHARDED:
            wb[n] = jnp.transpose(g, (1, 0, 2)).reshape(g.shape[1], -1)
        else:
            wb[n] = g.reshape(-1, g.shape[2])
    wb['conv_w_f32'] = jnp.transpose(g_conv, (1, 0, 2)).reshape(CONV_K, -1)
    small = {n: w_loc[n] for n in vecs}

    loss_part, grad_x, grads = _local_step(x, c, positions, loss_target, wb, small)
    loss = lax.psum(loss_part, ("x", "y", "c"))

    stacks = []
    for n in mats:
        kk, nn = w_loc[n].shape[1:]
        if n in COL_SHARDED:
            stacks.append(jnp.transpose(grads[n].reshape(kk, 4, nn), (1, 0, 2)))
        else:
            stacks.append(grads[n].reshape(4, kk, nn))
    g_mine, g_other = _reduce_matrices(stacks, mats)
    g_shard = {}

    vec_shapes = [tuple(grads[n].shape) for n in vecs] + [tuple(grads['conv_w'].shape)]
    total = _stack_sum(_gather_small(_pack_small([grads[n] for n in vecs] + [grads['conv_w']])), "grad_sum_small")
    g_vec = _unpack_small(total, vec_shapes)
    n_conv = conv_w.shape[2]
    chip = 2 * lax.axis_index("x") + lax.axis_index("y")
    g_shard['conv_w'] = lax.dynamic_slice_in_dim(g_vec[-1], chip * n_conv, n_conv, axis=1)
    for n, g in zip(vecs, g_vec):
        g_shard[n] = g

    delta, new_m, new_v = {}, {}, {}
    cidx = lax.axis_index("c").astype(jnp.int32).reshape(1)
    for n, mine, other in zip(mats, g_mine, g_other):
        g_shard[n], delta[n], new_m[n], new_v[n] = _adam_halves_call(
            w_loc[n][0], mine, other, cidx, m_loc[n][0], v_loc[n][0], "adamw_" + n)
    rest = vecs + ['conv_w']
    rest_shapes = [tuple(w_loc[n].shape) for n in rest]
    packed = [_pack_small([src[n] for n in rest]) for src in (w_loc, g_shard, m_loc, v_loc)]
    for dst, buf in zip((delta, new_m, new_v), _adam_call(*packed, "adamw_small")):
        dst.update(zip(rest, _unpack_small(buf, rest_shapes)))

    def out(d):
        return [d[n].reshape(w_loc[n].shape) for n in WEIGHTS]

    return (loss, grad_x, *out(g_shard), *out(delta), *out(new_m), *out(new_v))
```

```python
import functools
import math

import numpy as np
import jax
import jax.numpy as jnp
from jax import lax
from jax.experimental import pallas as pl
from jax.experimental.pallas import tpu as pltpu

F32 = jnp.float32
BF16 = jnp.bfloat16
MESH = pl.DeviceIdType.MESH

D_MODEL = 1024
N_HEADS = 8
NOPE = 128
ROPE = 64
V_DIM = 128
Q_RANK = 256
KV_RANK = 256
ROPE_THETA = 10000.0
D_INNER = 2048
SSM_HEADS = 32
SSM_GROUPS = 8
HEAD_P = 64
STATE_N = 128
CONV_K = 4
CHUNK = 128
CONV_CH = D_INNER + 2 * SSM_GROUPS * STATE_N
D_FF = 4096
EPS = 1e-6
IN_SIZES = (Q_RANK, KV_RANK, ROPE, D_INNER, CONV_CH, SSM_HEADS, D_MODEL, D_MODEL)
ADAM_LR, ADAM_B1, ADAM_B2, ADAM_EPS, ADAM_WD, ADAM_STEP = 0.001, 0.9, 0.999, 1e-08, 0.01, 10

VMEM_LIMIT_BYTES = 52 * 1024 * 1024
LANE = 128
QK_PAD = 256

WEIGHTS = ['w_ada', 'b_ada', 'g_pre_mix', 'g_post_mix', 'w_in', 'g_q_lat', 'g_kv_lat', 'w_uq', 'w_ukv',
           'w_o_attn', 'conv_w', 'conv_b', 'dt_bias', 'a_log', 'd_skip', 'g_ssm_out', 'w_o_ssm', 'w_out',
           'g_pre_mlp', 'g_post_mlp', 'w_ff1', 'w_ff2']
COL_SHARDED = ('w_ada', 'w_in', 'w_uq', 'w_ukv', 'conv_w', 'w_ff1')
ROW_SHARDED = ('w_o_attn', 'w_o_ssm', 'w_out', 'w_ff2')


def _cparams(sem):
    return pltpu.CompilerParams(dimension_semantics=sem, vmem_limit_bytes=VMEM_LIMIT_BYTES)


def _tile(n, cap):
    if n <= cap:
        return n
    k = n // LANE
    best = LANE
    for d in range(1, k + 1):
        if k % d == 0 and d * LANE <= cap:
            best = d * LANE
    return best


def _mm(a, w, name, out_dtype=F32, epilogue=None, extras=(), out_dtypes=None):
    M, K = a.shape
    N = w.shape[1]
    tm = min(M, 1024)
    tn = _tile(N, 1024)
    tk = _tile(K, 2048)
    nk = K // tk
    dts = tuple(out_dtypes) if epilogue is not None else (out_dtype,)
    n_x, n_o = len(extras), len(dts)

    def finish(acc, refs):
        res = epilogue(acc, *[r[...] for r in refs[:n_x]]) if epilogue is not None else (acc,)
        for o_ref, val, dt in zip(refs[n_x:n_x + n_o], res, dts):
            o_ref[...] = val.astype(dt)

    def body(a_ref, w_ref, *refs):
        part = jnp.dot(a_ref[...].astype(BF16), w_ref[...], preferred_element_type=F32)
        if nk == 1:
            finish(part, refs)
        else:
            acc_ref = refs[-1]
            k = pl.program_id(2)

            @pl.when(k == 0)
            def _():
                acc_ref[...] = part

            @pl.when(k > 0)
            def _():
                acc_ref[...] += part

            @pl.when(k == nk - 1)
            def _():
                finish(acc_ref[...], refs)

    ospec = pl.BlockSpec((tm, tn), lambda i, j, k: (i, j))
    res = pl.pallas_call(
        body, grid=(M // tm, N // tn, nk),
        in_specs=[pl.BlockSpec((tm, tk), lambda i, j, k: (i, k)), pl.BlockSpec((tk, tn), lambda i, j, k: (k, j))]
        + [ospec] * n_x,
        out_specs=[ospec] * n_o, out_shape=[jax.ShapeDtypeStruct((M, N), dt) for dt in dts],
        scratch_shapes=[pltpu.VMEM((tm, tn), F32)] if nk > 1 else [], name=name,
        compiler_params=_cparams(("parallel", "parallel", "arbitrary")))(a, w, *extras)
    return res if epilogue is not None else res[0]


def _mm_tn(a, g, name):
    M, K = a.shape
    N = g.shape[1]
    tm = min(M, 1024)
    tk = _tile(K, 1024)
    tn = _tile(N, 1024)
    nm = M // tm

    def body(a_ref, g_ref, o_ref):
        part = lax.dot_general(a_ref[...].astype(BF16), g_ref[...].astype(BF16), (((0,), (0,)), ((), ())),
                               preferred_element_type=F32)
        m = pl.program_id(2)

        @pl.when(m == 0)
        def _():
            o_ref[...] = part

        @pl.when(m > 0)
        def _():
            o_ref[...] += part

    return pl.pallas_call(
        body, grid=(K // tk, N // tn, nm),
        in_specs=[pl.BlockSpec((tm, tk), lambda i, j, m: (m, i)), pl.BlockSpec((tm, tn), lambda i, j, m: (m, j))],
        out_specs=pl.BlockSpec((tk, tn), lambda i, j, m: (i, j)),
        out_shape=jax.ShapeDtypeStruct((K, N), F32), name=name,
        compiler_params=_cparams(("parallel", "parallel", "arbitrary")))(a, g)


def make_linear(name, out_dtype=F32):
    @jax.custom_vjp
    def linear(a, w, tok):
        return _mm(a, w, name + "_fwd", out_dtype)

    def fwd(a, w, tok):
        return _mm(a, w, name + "_fwd", out_dtype), (a, w)

    def bwd(res, g):
        a, w = res
        da = _mm(g, w.T, name + "_dx", a.dtype)
        dw = _mm_tn(a, g, name + "_dw")
        return da, jnp.zeros_like(w), dw

    linear.defvjp(fwd, bwd)
    return linear


def _relu2_epilogue(acc):
    r = jnp.maximum(acc, 0.0)
    return r * r, r


def _relu2_bwd_epilogue(acc, r):
    return (acc * (2.0 * r.astype(F32)),)


@jax.custom_vjp
def ffn(h, w1, tok1, w2, tok2):
    act, _ = _mm(h, w1, "w_ff1_fwd", epilogue=_relu2_epilogue, out_dtypes=(BF16, BF16))
    return _mm(act, w2, "w_ff2_fwd")


def _ffn_fwd(h, w1, tok1, w2, tok2):
    act, r = _mm(h, w1, "w_ff1_fwd", epilogue=_relu2_epilogue, out_dtypes=(BF16, BF16))
    return _mm(act, w2, "w_ff2_fwd"), (h, w1, w2, act, r)


def _ffn_bwd(res, g):
    h, w1, w2, act, r = res
    du = _mm(g, w2.T, "w_ff2_dx", epilogue=_relu2_bwd_epilogue, extras=(r,), out_dtypes=(BF16,))[0]
    dw2 = _mm_tn(act, g, "w_ff2_dw")
    dw1 = _mm_tn(h, du, "w_ff1_dw")
    dh = _mm(du, w1.T, "w_ff1_dx", h.dtype)
    return dh, jnp.zeros_like(w1), dw1, jnp.zeros_like(w2), dw2


ffn.defvjp(_ffn_fwd, _ffn_bwd)


def make_rowwise(name, f, n_rows, n_seqs, n_pars, out_kinds, ncol=1, nodiff=(), ts_cap=512, windows=None):
    windows = dict(windows or {})
    n_in = n_rows + n_seqs + n_pars
    diff_idx = [i for i in range(n_in) if i not in nodiff]

    def _dims(rows):
        B, S = rows[0].shape[0], rows[0].shape[1]
        ts = min(S, ts_cap)
        return B, S, ts

    def _width(i, r):
        return windows[i][1] if i in windows else r.shape[2]

    def _in_specs(rows, seqs, pars, ts):
        specs = []
        for i, r in enumerate(rows):
            col0 = windows[i][0] if i in windows else 0
            specs.append(pl.BlockSpec((1, ts, _width(i, r) // ncol), lambda k, b, s, col0=col0: (b, s, k + col0)))
        for q in seqs:
            specs.append(pl.BlockSpec((1, 1, q.shape[2] // ncol), lambda k, b, s: (b, 0, k)))
        for p in pars:
            specs.append(pl.BlockSpec((1, p.shape[1] // ncol), lambda k, b, s: (0, k)))
        return specs

    def _load(refs):
        vals = [r[0] for r in refs[:n_rows + n_seqs]]
        vals += [r[...] for r in refs[n_rows + n_seqs:n_in]]
        return vals

    def _out_struct(rows, seqs, pars, ts):
        blocks = [jax.ShapeDtypeStruct((ts, _width(i, r) // ncol), r.dtype) for i, r in enumerate(rows)]
        blocks += [jax.ShapeDtypeStruct((1, q.shape[2] // ncol), q.dtype) for q in seqs]
        blocks += [jax.ShapeDtypeStruct((1, p.shape[1] // ncol), p.dtype) for p in pars]
        return jax.eval_shape(f, *blocks)

    def _fwd_call(rows, seqs, pars):
        B, S, ts = _dims(rows)
        outs = _out_struct(rows, seqs, pars, ts)
        n_out = len(outs)

        def body(*refs):
            res = f(*_load(refs))
            first = (pl.program_id(1) == 0) & (pl.program_id(2) == 0)
            for o_ref, val, kind in zip(refs[n_in:], res, out_kinds):
                if kind == 'row':
                    o_ref[0] = val
                else:
                    tot = jnp.sum(val, axis=0, keepdims=True)

                    @pl.when(first)
                    def _(o_ref=o_ref, tot=tot):
                        o_ref[...] = tot

                    @pl.when(jnp.logical_not(first))
                    def _(o_ref=o_ref, tot=tot):
                        o_ref[...] += tot

        out_shape, out_specs = [], []
        for o, kind in zip(outs, out_kinds):
            d = o.shape[1]
            if kind == 'row':
                out_shape.append(jax.ShapeDtypeStruct((B, S, ncol * d), o.dtype))
                out_specs.append(pl.BlockSpec((1, ts, d), lambda k, b, s: (b, s, k)))
            else:
                out_shape.append(jax.ShapeDtypeStruct((1, ncol * d), o.dtype))
                out_specs.append(pl.BlockSpec((1, d), lambda k, b, s: (0, k)))
        res = pl.pallas_call(
            body, grid=(ncol, B, S // ts), in_specs=_in_specs(rows, seqs, pars, ts), out_specs=out_specs,
            out_shape=out_shape, name=name + "_fwd",
            compiler_params=_cparams(("arbitrary", "arbitrary", "arbitrary")))(*rows, *seqs, *pars)
        return tuple(res)

    def _bwd_call(rows, seqs, pars, cots):
        B, S, ts = _dims(rows)
        outs = _out_struct(rows, seqs, pars, ts)
        n_out = len(outs)
        all_in = list(rows) + list(seqs) + list(pars)

        def body(*refs):
            vals = _load(refs)
            cts = []
            for c_ref, o, kind in zip(refs[n_in:n_in + n_out], outs, out_kinds):
                if kind == 'row':
                    cts.append(c_ref[0])
                else:
                    cts.append(jnp.broadcast_to(c_ref[...], o.shape))

            def g(*dv):
                full = list(vals)
                for i, v in zip(diff_idx, dv):
                    full[i] = v
                return tuple(f(*full))

            _, vjp = jax.vjp(g, *[vals[i] for i in diff_idx])
            grads = vjp(tuple(cts))
            b, s = pl.program_id(1), pl.program_id(2)
            for o_ref, i, gr in zip(refs[n_in + n_out:], diff_idx, grads):
                if i < n_rows:
                    o_ref[0] = gr
                else:
                    first = (s == 0) if i < n_rows + n_seqs else ((b == 0) & (s == 0))
                    target = (lambda r: r.at[0]) if i < n_rows + n_seqs else (lambda r: r)

                    @pl.when(first)
                    def _(o_ref=o_ref, gr=gr, target=target):
                        target(o_ref)[...] = gr

                    @pl.when(jnp.logical_not(first))
                    def _(o_ref=o_ref, gr=gr, target=target):
                        target(o_ref)[...] += gr

        cot_specs = []
        for o, kind in zip(outs, out_kinds):
            d = o.shape[1]
            if kind == 'row':
                cot_specs.append(pl.BlockSpec((1, ts, d), lambda k, b, s: (b, s, k)))
            else:
                cot_specs.append(pl.BlockSpec((1, d), lambda k, b, s: (0, k)))
        out_shape, out_specs = [], []
        for i in diff_idx:
            a = all_in[i]
            if i < n_rows:
                out_shape.append(jax.ShapeDtypeStruct((B, S, _width(i, a)), a.dtype))
                out_specs.append(pl.BlockSpec((1, ts, _width(i, a) // ncol), lambda k, b, s: (b, s, k)))
                continue
            out_shape.append(jax.ShapeDtypeStruct(a.shape, a.dtype))
            if i < n_rows + n_seqs:
                out_specs.append(pl.BlockSpec((1, 1, a.shape[2] // ncol), lambda k, b, s: (b, 0, k)))
            else:
                out_specs.append(pl.BlockSpec((1, a.shape[1] // ncol), lambda k, b, s: (0, k)))
        res = pl.pallas_call(
            body, grid=(ncol, B, S // ts), in_specs=_in_specs(rows, seqs, pars, ts) + cot_specs,
            out_specs=out_specs, out_shape=out_shape, name=name + "_bwd",
            compiler_params=_cparams(("arbitrary", "arbitrary", "arbitrary")))(*all_in, *cots)
        grads = [None] * n_in
        for i, r in zip(diff_idx, res):
            grads[i] = r
        for i in nodiff:
            grads[i] = jnp.zeros_like(all_in[i])
        stand_in_grads = tuple(grads[i] for i in sorted(windows))
        for i in windows:
            grads[i] = jnp.zeros_like(all_in[i])
        return (tuple(grads[:n_rows]), tuple(grads[n_rows:n_rows + n_seqs]), tuple(grads[n_rows + n_seqs:]),
                stand_in_grads)

    @jax.custom_vjp
    def op(rows, seqs, pars, stand_ins):
        return _fwd_call(rows, seqs, pars)

    def fwd(rows, seqs, pars, stand_ins):
        return _fwd_call(rows, seqs, pars), (rows, seqs, pars)

    def bwd(res, cots):
        rows, seqs, pars = res
        return _bwd_call(rows, seqs, pars, cots)

    op.defvjp(fwd, bwd)
    return lambda rows, seqs, pars, stand_ins=(): op(tuple(rows), tuple(seqs), tuple(pars), tuple(stand_ins))


def _rms(x, g):
    return x * lax.rsqrt(jnp.mean(x * x, axis=-1, keepdims=True) + EPS) * g


def _silu(x):
    return x * lax.logistic(x)


def _f_silu(c):
    return (_silu(c),)


def _f_modulate(x, scale, shift, g):
    return ((_rms(x, g) * (1.0 + scale) + shift).astype(BF16),)


def _f_rms(x, g):
    return (_rms(x, g).astype(BF16),)


def _f_dt(dt_raw, dt_bias, a_log):
    z = dt_raw + dt_bias
    dt = jnp.maximum(z, 0.0) + jnp.log1p(jnp.exp(-jnp.abs(z)))
    return dt, dt * (-jnp.exp(a_log))


def _f_gated_norm(y, z, g):
    return (_rms(y * _silu(z), g).astype(BF16),)


def _f_merge(attn, ssm, ga, gb):
    return ((lax.logistic(ga) * attn + lax.logistic(gb) * ssm).astype(BF16),)


def _f_post(x, m, gate, g):
    return (x + gate * _rms(m, g),)


def _f_final_loss(x, ff, target, gate, g):
    e = x + gate * _rms(ff, g) - target
    return (e * e * (0.5 / D_MODEL),)


def _rope_tables(posf, inv_lane):
    B, S, _ = posf.shape
    ts = min(S, 512)

    def body(p_ref, inv_ref, c_ref, a_ref, b_ref):
        ang = p_ref[0] * inv_ref[...]
        cs, sn = jnp.cos(ang), jnp.sin(ang)
        lane = lax.broadcasted_iota(jnp.int32, ang.shape, 1)
        c_ref[0] = jnp.where(lane < ROPE, cs, 0.0)
        a_ref[0] = jnp.where(lane < ROPE // 2, -sn, 0.0)
        b_ref[0] = jnp.where((lane >= ROPE // 2) & (lane < ROPE), sn, 0.0)

    spec = pl.BlockSpec((1, ts, LANE), lambda b, s: (b, s, 0))
    sds = jax.ShapeDtypeStruct((B, S, LANE), F32)
    return pl.pallas_call(
        body, grid=(B, S // ts),
        in_specs=[pl.BlockSpec((1, ts, 1), lambda b, s: (b, s, 0)), pl.BlockSpec((1, LANE), lambda b, s: (0, 0))],
        out_specs=[spec, spec, spec], out_shape=[sds, sds, sds], name="rope_tables",
        compiler_params=_cparams(("parallel", "parallel")))(posf, inv_lane)


def _rot(u, c, a, bm):
    return u * c + pltpu.roll(u, 96, 1) * a + pltpu.roll(u, 32, 1) * bm


def _rot_t(g, c, a, bm):
    return g * c + pltpu.roll(g * a, 32, 1) + pltpu.roll(g * bm, 96, 1)


def _rope_q_call(q, tabs, transpose, name):
    B, S, W = q.shape
    ts = min(S, 512)
    fn = _rot_t if transpose else _rot
    out_dtype = F32 if transpose else BF16

    def body(q_ref, c_ref, a_ref, b_ref, o_ref):
        tc, ta, tb = c_ref[0], a_ref[0], b_ref[0]
        for h in range(W // QK_PAD):
            u = q_ref[0, :, h * QK_PAD:(h + 1) * QK_PAD].astype(F32) * ATT_SCALE
            r = fn(u[:, NOPE:], tc, ta, tb)
            o_ref[0, :, h * QK_PAD:(h + 1) * QK_PAD] = jnp.concatenate([u[:, :NOPE], r], axis=1).astype(out_dtype)

    tspec = pl.BlockSpec((1, ts, LANE), lambda b, s: (b, s, 0))
    qspec = pl.BlockSpec((1, ts, W), lambda b, s: (b, s, 0))
    return pl.pallas_call(
        body, grid=(B, S // ts), in_specs=[qspec, tspec, tspec, tspec], out_specs=qspec,
        out_shape=jax.ShapeDtypeStruct(q.shape, out_dtype), name=name,
        compiler_params=_cparams(("parallel", "parallel")))(q, *tabs)


@jax.custom_vjp
def rope_q(q, tabs):
    return _rope_q_call(q, tabs, False, "rope_q_fwd")


def _rope_q_fwd(q, tabs):
    return _rope_q_call(q, tabs, False, "rope_q_fwd"), tabs


def _rope_q_bwd(tabs, g):
    return _rope_q_call(g, tabs, True, "rope_q_bwd"), tuple(jnp.zeros_like(t) for t in tabs)


rope_q.defvjp(_rope_q_fwd, _rope_q_bwd)


def _build_k_fwd_call(kv, kr, tabs):
    B, S, _ = kv.shape
    ts = min(S, 512)

    def body(kv_ref, kr_ref, c_ref, a_ref, b_ref, o_ref):
        r = _rot(kr_ref[0], c_ref[0], a_ref[0], b_ref[0]).astype(BF16)
        for h in range(N_HEADS):
            o_ref[0, :, h * QK_PAD:(h + 1) * QK_PAD] = jnp.concatenate(
                [kv_ref[0, :, h * NOPE:(h + 1) * NOPE], r], axis=1)

    tspec = pl.BlockSpec((1, ts, LANE), lambda b, s: (b, s, 0))
    kr_spec = pl.BlockSpec((1, ts, LANE), lambda b, s: (b, s, KR_LANE0 // LANE))
    return pl.pallas_call(
        body, grid=(B, S // ts),
        in_specs=[pl.BlockSpec((1, ts, N_HEADS * NOPE), lambda b, s: (b, s, 0)), kr_spec, tspec, tspec, tspec],
        out_specs=pl.BlockSpec((1, ts, N_HEADS * QK_PAD), lambda b, s: (b, s, 0)),
        out_shape=jax.ShapeDtypeStruct((B, S, N_HEADS * QK_PAD), BF16), name="build_k_fwd",
        compiler_params=_cparams(("parallel", "parallel")))(kv, kr, *tabs)


def _build_k_bwd_call(g, tabs):
    B, S, _ = g.shape
    ts = min(S, 512)

    def body(g_ref, c_ref, a_ref, b_ref, dk_ref, dr_ref):
        tot = None
        for h in range(N_HEADS):
            dk_ref[0, :, h * NOPE:(h + 1) * NOPE] = g_ref[0, :, h * QK_PAD:h * QK_PAD + NOPE]
            part = g_ref[0, :, h * QK_PAD + NOPE:(h + 1) * QK_PAD].astype(F32)
            tot = part if tot is None else tot + part
        dr_ref[0] = _rot_t(tot, c_ref[0], a_ref[0], b_ref[0])

    tspec = pl.BlockSpec((1, ts, LANE), lambda b, s: (b, s, 0))
    return pl.pallas_call(
        body, grid=(B, S // ts),
        in_specs=[pl.BlockSpec((1, ts, N_HEADS * QK_PAD), lambda b, s: (b, s, 0)), tspec, tspec, tspec],
        out_specs=[pl.BlockSpec((1, ts, N_HEADS * NOPE), lambda b, s: (b, s, 0)), tspec],
        out_shape=[jax.ShapeDtypeStruct((B, S, N_HEADS * NOPE), BF16), jax.ShapeDtypeStruct((B, S, LANE), F32)],
        name="build_k_bwd", compiler_params=_cparams(("parallel", "parallel")))(g, *tabs)


@jax.custom_vjp
def build_k(kv, src, stand_in, tabs):
    return _build_k_fwd_call(kv, src, tabs)


def _build_k_fwd(kv, src, stand_in, tabs):
    return _build_k_fwd_call(kv, src, tabs), (tabs, kv.shape, src)


def _build_k_bwd(res, g):
    tabs, kv_shape, src = res
    dk, dr = _build_k_bwd_call(g, tabs)
    dkv = jnp.concatenate([dk, jnp.zeros((kv_shape[0], kv_shape[1], kv_shape[2] - dk.shape[2]), BF16)], axis=-1)
    return dkv, jnp.zeros_like(src), dr, tuple(jnp.zeros_like(t) for t in tabs)


build_k.defvjp(_build_k_fwd, _build_k_bwd)


ATT_SCALE = (NOPE + ROPE) ** -0.5
NEG = -1e30


def _att_tiles(S):
    t = min(S, 512)
    return t, S // t


def _scores(q, k, diagonal):
    s = lax.dot_general(q, k, (((1,), (1,)), ((), ())), preferred_element_type=F32)
    if diagonal:
        row = lax.broadcasted_iota(jnp.int32, s.shape, 0)
        col = lax.broadcasted_iota(jnp.int32, s.shape, 1)
        s = jnp.where(col <= row, s, NEG)
    return s


ATT_HB = 4


def _causal_pairs(n):
    pairs = [(i, j) for i in range(n) for j in range(i + 1)]
    return (jnp.asarray([p[0] for p in pairs], jnp.int32), jnp.asarray([p[1] for p in pairs], jnp.int32))


def _head(ref_or_val, h, w):
    return ref_or_val[:, h * w:(h + 1) * w]


def _attn_fwd_call(q, k, vsrc, v_blk0):
    B, S, _ = q.shape
    t, n = _att_tiles(S)
    qi, kj = _causal_pairs(n)

    def body(qi_ref, kj_ref, q_ref, k_ref, v_ref, o_ref, lse_ref, m_sc, l_sc, acc_sc):
        p_id = pl.program_id(2)
        i, j = qi_ref[p_id], kj_ref[p_id]

        @pl.when(j == 0)
        def _():
            m_sc[...] = jnp.full(m_sc.shape, NEG, F32)
            l_sc[...] = jnp.zeros(l_sc.shape, F32)
            acc_sc[...] = jnp.zeros(acc_sc.shape, F32)

        def step(diagonal):
            qa, ka, va = q_ref[0], k_ref[0], v_ref[0]
            for h in range(ATT_HB):
                lanes = slice(h * LANE, (h + 1) * LANE)
                s = _scores(_head(qa, h, QK_PAD), _head(ka, h, QK_PAD), diagonal)
                m_prev = m_sc[:, lanes]
                m_new = jnp.maximum(m_prev, jnp.max(s, axis=1, keepdims=True))
                alpha = jnp.exp(m_prev - m_new)
                p = jnp.exp(s - jnp.tile(m_new, (1, t // LANE)))
                l_sc[:, lanes] = alpha * l_sc[:, lanes] + jnp.sum(p, axis=1, keepdims=True)
                acc_sc[:, lanes] = alpha * acc_sc[:, lanes] + jnp.dot(p.astype(BF16), _head(va, h, V_DIM),
                                                                      preferred_element_type=F32)
                m_sc[:, lanes] = m_new

        @pl.when(j < i)
        def _():
            step(False)

        @pl.when(j == i)
        def _():
            step(True)
            o_ref[0] = acc_sc[...] / l_sc[...]
            lse_ref[0] = m_sc[...] + jnp.log(l_sc[...])

    wq, wv = ATT_HB * QK_PAD, ATT_HB * V_DIM
    grid_spec = pltpu.PrefetchScalarGridSpec(
        num_scalar_prefetch=2, grid=(B, N_HEADS // ATT_HB, qi.shape[0]),
        in_specs=[pl.BlockSpec((1, t, wq), lambda b, h, p, qi, kj: (b, qi[p], h)),
                  pl.BlockSpec((1, t, wq), lambda b, h, p, qi, kj: (b, kj[p], h)),
                  pl.BlockSpec((1, t, wv), lambda b, h, p, qi, kj: (b, kj[p], v_blk0 + h))],
        out_specs=[pl.BlockSpec((1, t, wv), lambda b, h, p, qi, kj: (b, qi[p], h)),
                   pl.BlockSpec((1, t, wv), lambda b, h, p, qi, kj: (b, qi[p], h))],
        scratch_shapes=[pltpu.VMEM((t, wv), F32), pltpu.VMEM((t, wv), F32), pltpu.VMEM((t, wv), F32)])
    return pl.pallas_call(
        body, grid_spec=grid_spec,
        out_shape=[jax.ShapeDtypeStruct((B, S, N_HEADS * V_DIM), F32),
                   jax.ShapeDtypeStruct((B, S, N_HEADS * LANE), F32)],
        name="attn_fwd", compiler_params=_cparams(("parallel", "parallel", "arbitrary")))(qi, kj, q, k, vsrc)


def _attn_p_ds(q, k, v, o, do, lse, diagonal, t):
    s = _scores(q, k, diagonal)
    p = jnp.exp(s - jnp.tile(lse, (1, t // LANE)))
    dp = lax.dot_general(do.astype(BF16), v, (((1,), (1,)), ((), ())), preferred_element_type=F32)
    delta = jnp.sum(do * o, axis=1, keepdims=True)
    ds = p * (dp - delta)
    return p, ds


ATT_HB_BWD = 2


def _attn_bwd_call(q, k, vsrc, o, do, lse):
    B, S, _ = q.shape
    t, n = _att_tiles(S)
    qi, kj = _causal_pairs(n)
    n_pairs = qi.shape[0]
    hb = ATT_HB_BWD
    v_blk0 = N_HEADS // hb

    def body(qi_ref, kj_ref, q_ref, k_ref, v_ref, o_ref, do_ref, lse_ref, dq_ref, dk_ref, dv_ref, dq_sc, dk_sc, dv_sc):
        p_id = pl.program_id(2)
        i, j = qi_ref[p_id], kj_ref[p_id]

        @pl.when(p_id == 0)
        def _():
            dk_sc[...] = jnp.zeros(dk_sc.shape, F32)
            dv_sc[...] = jnp.zeros(dv_sc.shape, F32)

        @pl.when(j == 0)
        def _():
            dq_sc[...] = jnp.zeros(dq_sc.shape, F32)

        rows = pl.ds(pl.multiple_of(j * t, t), t)

        def step(diagonal):
            qa, ka, va, oa, doa, la = q_ref[0], k_ref[0], v_ref[0], o_ref[0], do_ref[0], lse_ref[0]
            for h in range(hb):
                qb, kb, dob = _head(qa, h, QK_PAD), _head(ka, h, QK_PAD), _head(doa, h, V_DIM)
                p, ds = _attn_p_ds(qb, kb, _head(va, h, V_DIM), _head(oa, h, V_DIM), dob, _head(la, h, LANE),
                                   diagonal, t)
                dsb = ds.astype(BF16)
                dq_sc[:, h * QK_PAD:(h + 1) * QK_PAD] += jnp.dot(dsb, kb, preferred_element_type=F32)
                dv_sc[rows, h * V_DIM:(h + 1) * V_DIM] += lax.dot_general(
                    p.astype(BF16), dob.astype(BF16), (((0,), (0,)), ((), ())), preferred_element_type=F32)
                dk_sc[rows, h * QK_PAD:(h + 1) * QK_PAD] += lax.dot_general(
                    dsb, qb, (((0,), (0,)), ((), ())), preferred_element_type=F32)

        @pl.when(j < i)
        def _():
            step(False)

        @pl.when(j == i)
        def _():
            step(True)
            dq_ref[0] = dq_sc[...].astype(BF16)

        @pl.when(p_id == n_pairs - 1)
        def _():
            dk_ref[0] = dk_sc[...].astype(BF16)
            dv_ref[0] = dv_sc[...].astype(BF16)

    wq, wv = hb * QK_PAD, hb * V_DIM
    at_q = lambda b, h, p, qi, kj: (b, qi[p], h)
    at_k = lambda b, h, p, qi, kj: (b, kj[p], h)
    whole = lambda b, h, p, qi, kj: (b, 0, h)
    grid_spec = pltpu.PrefetchScalarGridSpec(
        num_scalar_prefetch=2, grid=(B, N_HEADS // hb, n_pairs),
        in_specs=[pl.BlockSpec((1, t, wq), at_q), pl.BlockSpec((1, t, wq), at_k),
                  pl.BlockSpec((1, t, wv), lambda b, h, p, qi, kj: (b, kj[p], v_blk0 + h)),
                  pl.BlockSpec((1, t, wv), at_q), pl.BlockSpec((1, t, wv), at_q), pl.BlockSpec((1, t, wv), at_q)],
        out_specs=[pl.BlockSpec((1, t, wq), at_q), pl.BlockSpec((1, S, wq), whole), pl.BlockSpec((1, S, wv), whole)],
        scratch_shapes=[pltpu.VMEM((t, wq), F32), pltpu.VMEM((S, wq), F32), pltpu.VMEM((S, wv), F32)])
    return pl.pallas_call(
        body, grid_spec=grid_spec,
        out_shape=[jax.ShapeDtypeStruct((B, S, N_HEADS * QK_PAD), BF16),
                   jax.ShapeDtypeStruct((B, S, N_HEADS * QK_PAD), BF16),
                   jax.ShapeDtypeStruct((B, S, N_HEADS * V_DIM), BF16)],
        name="attn_bwd", compiler_params=_cparams(("parallel", "parallel", "arbitrary")))(
            qi, kj, q, k, vsrc, o, do, lse)


@jax.custom_vjp
def attention(q, k, kv):
    return _attn_fwd_call(q, k, kv, N_HEADS // ATT_HB)[0]


def _attention_fwd(q, k, kv):
    o, lse = _attn_fwd_call(q, k, kv, N_HEADS // ATT_HB)
    return o, (q, k, kv, o, lse)


def _attention_bwd(res, do):
    q, k, kv, o, lse = res
    dq, dk, dv = _attn_bwd_call(q, k, kv, o, do, lse)
    dkv = jnp.concatenate([jnp.zeros_like(dv), dv], axis=-1)
    return dq, dk, dkv


attention.defvjp(_attention_fwd, _attention_bwd)


def _shift_down(v, sh, rows):
    return jnp.where(rows >= sh, pltpu.roll(v, sh, 0), 0.0)


def _shift_up(v, sh, rows, S):
    return jnp.where(rows < S - sh, pltpu.roll(v, S - sh, 0), 0.0)


def _conv_pre(u, w_ref, b_ref, rows):
    acc = b_ref[...] + w_ref[pl.ds(CONV_K - 1, 1), :] * u
    for k in range(CONV_K - 1):
        acc = acc + w_ref[pl.ds(k, 1), :] * _shift_down(u, CONV_K - 1 - k, rows)
    return acc


def _conv_fwd_call(src, w, b):
    B, S, _ = src.shape
    C = w.shape[1]

    def body(u_ref, w_ref, b_ref, o_ref):
        uu = u_ref[0]
        rows = lax.broadcasted_iota(jnp.int32, uu.shape, 0)
        o_ref[0] = _silu(_conv_pre(uu, w_ref, b_ref, rows))

    spec = pl.BlockSpec((1, S, LANE), lambda c, bb: (bb, 0, c))
    return pl.pallas_call(
        body, grid=(C // LANE, B),
        in_specs=[pl.BlockSpec((1, S, LANE), lambda c, bb: (bb, 0, c + CONV_LANE0 // LANE)),
                  pl.BlockSpec((CONV_K, LANE), lambda c, bb: (0, c)), pl.BlockSpec((1, LANE), lambda c, bb: (0, c))],
        out_specs=spec, out_shape=jax.ShapeDtypeStruct((B, S, C), F32), name="conv_fwd",
        compiler_params=_cparams(("parallel", "arbitrary")))(src, w, b)


def _conv_bwd_call(src, w, b, g):
    B, S, _ = src.shape
    C = w.shape[1]

    def body(u_ref, w_ref, b_ref, g_ref, du_ref, dw_ref, db_ref):
        uu = u_ref[0]
        rows = lax.broadcasted_iota(jnp.int32, uu.shape, 0)
        pre = _conv_pre(uu, w_ref, b_ref, rows)
        sg = lax.logistic(pre)
        dpre = g_ref[0] * sg * (1.0 + pre * (1.0 - sg))
        du = w_ref[pl.ds(CONV_K - 1, 1), :] * dpre
        dws = [None] * CONV_K
        dws[CONV_K - 1] = jnp.sum(dpre * uu, axis=0, keepdims=True)
        for k in range(CONV_K - 1):
            sh = CONV_K - 1 - k
            du = du + w_ref[pl.ds(k, 1), :] * _shift_up(dpre, sh, rows, S)
            dws[k] = jnp.sum(dpre * _shift_down(uu, sh, rows), axis=0, keepdims=True)
        du_ref[0] = du
        dbv = jnp.sum(dpre, axis=0, keepdims=True)
        first = pl.program_id(1) == 0

        @pl.when(first)
        def _():
            for k in range(CONV_K):
                dw_ref[pl.ds(k, 1), :] = dws[k]
            db_ref[...] = dbv

        @pl.when(jnp.logical_not(first))
        def _():
            for k in range(CONV_K):
                dw_ref[pl.ds(k, 1), :] += dws[k]
            db_ref[...] += dbv

    spec = pl.BlockSpec((1, S, LANE), lambda c, bb: (bb, 0, c))
    wspec = pl.BlockSpec((CONV_K, LANE), lambda c, bb: (0, c))
    bspec = pl.BlockSpec((1, LANE), lambda c, bb: (0, c))
    uspec = pl.BlockSpec((1, S, LANE), lambda c, bb: (bb, 0, c + CONV_LANE0 // LANE))
    return pl.pallas_call(
        body, grid=(C // LANE, B), in_specs=[uspec, wspec, bspec, spec], out_specs=[spec, wspec, bspec],
        out_shape=[jax.ShapeDtypeStruct((B, S, C), F32), jax.ShapeDtypeStruct(w.shape, F32),
                   jax.ShapeDtypeStruct(b.shape, F32)],
        name="conv_bwd", compiler_params=_cparams(("parallel", "arbitrary")))(src, w, b, g)


@jax.custom_vjp
def conv_silu(src, stand_in, w, b):
    return _conv_fwd_call(src, w, b)


def _conv_silu_fwd(src, stand_in, w, b):
    return _conv_fwd_call(src, w, b), (src, w, b)


def _conv_silu_bwd(res, g):
    du, dw, db = _conv_bwd_call(*res, g)
    return jnp.zeros_like(res[0]), du, dw, db


conv_silu.defvjp(_conv_silu_fwd, _conv_silu_bwd)


def _chunk_cumsum_call(a, reverse, name):
    B, S, W = a.shape

    def body(a_ref, o_ref):
        r = lax.broadcasted_iota(jnp.int32, (CHUNK, CHUNK), 0)
        c = lax.broadcasted_iota(jnp.int32, (CHUNK, CHUNK), 1)
        tri = jnp.where((c >= r) if reverse else (c <= r), 1.0, 0.0).astype(F32)
        o_ref[0] = jnp.dot(tri, a_ref[0], preferred_element_type=F32, precision=lax.Precision.HIGHEST)

    spec = pl.BlockSpec((1, CHUNK, W), lambda b, c: (b, c, 0))
    return pl.pallas_call(body, grid=(B, S // CHUNK), in_specs=[spec], out_specs=spec,
                          out_shape=jax.ShapeDtypeStruct(a.shape, F32), name=name,
                          compiler_params=_cparams(("parallel", "parallel")))(a)


@jax.custom_vjp
def chunk_cumsum(a):
    return _chunk_cumsum_call(a, False, "chunk_cumsum_fwd")


chunk_cumsum.defvjp(lambda a: (_chunk_cumsum_call(a, False, "chunk_cumsum_fwd"), None),
                    lambda _, g: (_chunk_cumsum_call(g, True, "chunk_cumsum_bwd"),))


GROUP_W = 4 * HEAD_P
HPG = SSM_HEADS // SSM_GROUPS


def _ssd_masks():
    lane = lax.broadcasted_iota(jnp.int32, (1, GROUP_W), 1)
    return [((lane >= HEAD_P * j) & (lane < HEAD_P * (j + 1))).astype(F32) for j in range(HPG)]


def _ssd_decays(ac_cols, acr_ref, gi):
    r = lax.broadcasted_iota(jnp.int32, (CHUNK, CHUNK), 0)
    c = lax.broadcasted_iota(jnp.int32, (CHUNK, CHUNK), 1)
    return [jnp.exp(jnp.where(c <= r, ac_cols[j] - acr_ref[0, gi * HPG + j], NEG)) for j in range(HPG)]


def _ssd_cols(blk, g):
    lane = lax.broadcasted_iota(jnp.int32, blk.shape, 1)
    return [jnp.sum(jnp.where(lane == HPG * g + j, blk, 0.0), axis=1, keepdims=True) for j in range(HPG)]


def _ssd_spread(cols):
    lane = lax.broadcasted_iota(jnp.int32, (1, GROUP_W), 1)
    out = jnp.broadcast_to(cols[HPG - 1], (CHUNK, GROUP_W))
    for j in range(HPG - 2, -1, -1):
        out = jnp.where(lane < HEAD_P * (j + 1), cols[j], out)
    return out


def _ssd_gather(val, cols, masks, g):
    lane = lax.broadcasted_iota(jnp.int32, (1, LANE), 1)
    out = jnp.zeros((CHUNK, LANE), F32)
    for j in range(HPG):
        tot = jnp.sum(val * masks[j], axis=1, keepdims=True)
        if cols is not None:
            tot = tot + cols[j]
        out = out + tot * (lane == HPG * g + j).astype(F32)
    return out


def _dot(a, b, dims):
    return lax.dot_general(a.astype(BF16), b.astype(BF16), (dims, ((), ())), preferred_element_type=F32)


NN = ((1,), (0,))
NT = ((1,), (1,))
TN = ((0,), (0,))


XBC_W = GROUP_W + 2 * STATE_N


SSD_GB = 4


def _ssd_load(xbc_ref, dt_ref, ac_ref, masks, g, gi):
    x = xbc_ref[0, :, gi * XBC_W:gi * XBC_W + GROUP_W]
    bm = xbc_ref[0, :, gi * XBC_W + GROUP_W:gi * XBC_W + GROUP_W + STATE_N]
    cm = xbc_ref[0, :, gi * XBC_W + GROUP_W + STATE_N:(gi + 1) * XBC_W]
    ac_cols = _ssd_cols(ac_ref[0], g)
    dt = _ssd_spread(_ssd_cols(dt_ref[0], g))
    ac = _ssd_spread(ac_cols)
    is_last = (lax.broadcasted_iota(jnp.int32, (CHUNK, GROUP_W), 0) == CHUNK - 1).astype(F32)
    return x, bm, cm, dt, ac, ac_cols, is_last


def _ssd_in_specs(nc, rev):
    cc = (lambda c: nc - 1 - c) if rev else (lambda c: c)
    return [pl.BlockSpec((1, CHUNK, SSD_GB * XBC_W), lambda b, g, c: (b, cc(c), g)),
            pl.BlockSpec((1, CHUNK, LANE), lambda b, g, c: (b, cc(c), 0)),
            pl.BlockSpec((1, CHUNK, LANE), lambda b, g, c: (b, cc(c), 0)),
            pl.BlockSpec((1, SSD_GB * HPG, 1, CHUNK), lambda b, g, c: (b, g, 0, cc(c))),
            pl.BlockSpec((1, SSD_GB * GROUP_W), lambda b, g, c: (0, g))]


def _ssd_fwd_call(xbc, dtp, acp, acr, dsk):
    B, S, _ = xbc.shape
    nc = S // CHUNK

    def body(xbc_ref, dt_ref, ac_ref, ar_ref, ds_ref, y_ref, hp_ref, h_sc):
        @pl.when(pl.program_id(2) == 0)
        def _():
            h_sc[...] = jnp.zeros(h_sc.shape, F32)

        masks = _ssd_masks()
        ys = []
        for gi in range(SSD_GB):
            grp = SSD_GB * pl.program_id(1) + gi
            x, bm, cm, dt, ac, ac_cols, is_last = _ssd_load(xbc_ref, dt_ref, ac_ref, masks, grp, gi)
            last = jnp.sum(ac * is_last, axis=0, keepdims=True)
            decays = _ssd_decays(ac_cols, ar_ref, gi)
            xd = x * dt
            cb = _dot(cm, bm, NT)
            hprev = h_sc[gi]
            hp_ref[0, gi, 0] = hprev
            y = _dot(cm, hprev, NN) * jnp.exp(ac) + ds_ref[:, gi * GROUP_W:(gi + 1) * GROUP_W] * x
            for j in range(HPG):
                y = y + _dot(cb * decays[j], xd * masks[j], NN)
            ys.append(y)
            h_sc[gi] = hprev * jnp.exp(last) + _dot(bm, xd * jnp.exp(last - ac), TN)
        y_ref[0] = jnp.concatenate(ys, axis=1)

    ng = SSM_GROUPS // SSD_GB
    return pl.pallas_call(
        body, grid=(B, ng, nc), in_specs=_ssd_in_specs(nc, False),
        out_specs=[pl.BlockSpec((1, CHUNK, SSD_GB * GROUP_W), lambda b, g, c: (b, c, g)),
                   pl.BlockSpec((1, SSD_GB, 1, STATE_N, GROUP_W), lambda b, g, c: (b, g, c, 0, 0))],
        out_shape=[jax.ShapeDtypeStruct((B, S, D_INNER), F32),
                   jax.ShapeDtypeStruct((B, SSM_GROUPS, nc, STATE_N, GROUP_W), F32)],
        scratch_shapes=[pltpu.VMEM((SSD_GB, STATE_N, GROUP_W), F32)], name="ssd_fwd",
        compiler_params=_cparams(("parallel", "parallel", "arbitrary")))(xbc, dtp, acp, acr, dsk)


def _ssd_bwd_call(xbc, dtp, acp, acr, dsk, hps, dy):
    B, S, _ = xbc.shape
    nc = S // CHUNK

    def body(xbc_ref, dt_ref, ac_ref, ar_ref, ds_ref, hp_ref, dy_ref,
             dxbc_ref, ddt_ref, dac_ref, dar_ref, dds_ref, dh_sc):
        first = pl.program_id(2) == 0

        @pl.when(first)
        def _():
            dh_sc[...] = jnp.zeros(dh_sc.shape, F32)

        masks = _ssd_masks()
        dxbc_parts, dds_parts = [], []
        for gi in range(SSD_GB):
            grp = SSD_GB * pl.program_id(0) + gi
            x, bm, cm, dt, ac, ac_cols, is_last = _ssd_load(xbc_ref, dt_ref, ac_ref, masks, grp, gi)
            last = jnp.sum(ac * is_last, axis=0, keepdims=True)
            g = dy_ref[0, :, gi * GROUP_W:(gi + 1) * GROUP_W]
            hprev = hp_ref[0, gi, 0]
            dh = dh_sc[gi]
            decays = _ssd_decays(ac_cols, ar_ref, gi)
            dcols = []
            xd = x * dt
            cb = _dot(cm, bm, NT)
            e_c = jnp.exp(ac)
            e_end = jnp.exp(last - ac)
            e_last = jnp.exp(last)
            z = _dot(cm, hprev, NN)
            dz = g * e_c
            dac = g * z * e_c
            dc = _dot(dz, hprev, NT)
            dhprev = _dot(cm, dz, TN) + dh * e_last
            dcb = jnp.zeros((CHUNK, CHUNK), F32)
            dxd = jnp.zeros(xd.shape, F32)
            for j in range(HPG):
                gj = cb * decays[j]
                dgj = _dot(g * masks[j], xd, NT)
                dxd = dxd + _dot(gj, g, TN) * masks[j]
                dcb = dcb + dgj * decays[j]
                dseg = dgj * gj
                dcols.append(jnp.sum(dseg, axis=1, keepdims=True))
                dar_ref[0, gi * HPG + j] = -jnp.sum(dseg, axis=0, keepdims=True)
            dc = dc + _dot(dcb, bm, NN)
            db = _dot(dcb, cm, TN)
            sx = xd * e_end
            db = db + _dot(sx, dh, NT)
            dsx = _dot(bm, dh, NN)
            dxd = dxd + dsx * e_end
            de = dsx * sx
            dac = dac - de
            dlast = jnp.sum(de, axis=0, keepdims=True) + jnp.sum(dh * hprev, axis=0, keepdims=True) * e_last
            dsk = ds_ref[:, gi * GROUP_W:(gi + 1) * GROUP_W]
            dxbc_parts += [dxd * dt + dsk * g, db, dc]
            ddt_ref[0, gi] = _ssd_gather(dxd * x, None, masks, grp)
            dac_ref[0, gi] = _ssd_gather(dac + is_last * dlast, dcols, masks, grp)
            dds_parts.append(jnp.sum(g * x, axis=0, keepdims=True))
            dh_sc[gi] = dhprev
        dxbc_ref[0] = jnp.concatenate(dxbc_parts, axis=1)
        dds = jnp.concatenate(dds_parts, axis=1)
        first_all = first & (pl.program_id(1) == 0)

        @pl.when(first_all)
        def _():
            dds_ref[...] = dds

        @pl.when(jnp.logical_not(first_all))
        def _():
            dds_ref[...] += dds

    rc = lambda c: nc - 1 - c
    ng = SSM_GROUPS // SSD_GB
    in_specs = [pl.BlockSpec(s.block_shape, (lambda g, b, c, f=s.index_map: f(b, g, c))) for s in _ssd_in_specs(nc, True)]
    in_specs.append(pl.BlockSpec((1, SSD_GB, 1, STATE_N, GROUP_W), lambda g, b, c: (b, g, rc(c), 0, 0)))
    in_specs.append(pl.BlockSpec((1, CHUNK, SSD_GB * GROUP_W), lambda g, b, c: (b, rc(c), g)))
    per_group = pl.BlockSpec((1, SSD_GB, CHUNK, LANE), lambda g, b, c: (b, g, rc(c), 0))
    out_specs = [pl.BlockSpec((1, CHUNK, SSD_GB * XBC_W), lambda g, b, c: (b, rc(c), g)), per_group, per_group,
                 pl.BlockSpec((1, SSD_GB * HPG, 1, CHUNK), lambda g, b, c: (b, g, 0, rc(c))),
                 pl.BlockSpec((1, SSD_GB * GROUP_W), lambda g, b, c: (0, g))]
    out_shape = [jax.ShapeDtypeStruct(xbc.shape, F32),
                 jax.ShapeDtypeStruct((B, SSM_GROUPS, S, LANE), F32), jax.ShapeDtypeStruct((B, SSM_GROUPS, S, LANE), F32),
                 jax.ShapeDtypeStruct(acr.shape, F32), jax.ShapeDtypeStruct(dsk.shape, F32)]
    return pl.pallas_call(
        body, grid=(ng, B, nc), in_specs=in_specs, out_specs=out_specs, out_shape=out_shape,
        scratch_shapes=[pltpu.VMEM((SSD_GB, STATE_N, GROUP_W), F32)], name="ssd_bwd",
        compiler_params=_cparams(("arbitrary", "arbitrary", "arbitrary")))(xbc, dtp, acp, acr, dsk, hps, dy)


@jax.custom_vjp
def ssd(xbc, dtp, acp, acr, dsk):
    return _ssd_fwd_call(xbc, dtp, acp, acr, dsk)[0]


def _ssd_fwd(xbc, dtp, acp, acr, dsk):
    y, hps = _ssd_fwd_call(xbc, dtp, acp, acr, dsk)
    return y, (xbc, dtp, acp, acr, dsk, hps)


def _ssd_bwd(res, dy):
    dxbc, ddt, dac, dacr, dds = _ssd_bwd_call(*res, dy)
    return dxbc, jnp.sum(ddt, axis=1), jnp.sum(dac, axis=1), dacr, dds


ssd.defvjp(_ssd_fwd, _ssd_bwd)


def _pack_small(arrs):
    flat = jnp.concatenate([a.reshape(-1) for a in arrs])
    rows = -(-flat.shape[0] // (8 * LANE)) * 8
    return jnp.pad(flat, (0, rows * LANE - flat.shape[0])).reshape(rows, LANE)


def _unpack_small(buf, shapes):
    flat = buf.reshape(-1)
    out, off = [], 0
    for shp in shapes:
        n = int(np.prod(shp))
        out.append(flat[off:off + n].reshape(shp))
        off += n
    return out


def _rows_tile(rows, cap):
    for cand in range(min(rows, cap), 7, -8):
        if rows % cand == 0:
            return cand
    return rows


def _pair_sum(mine, theirs, cidx, name):
    n4, kk, nn = mine.shape
    half = kk // 2
    tr = _rows_tile(half, 256)
    nb = half // tr

    def body(c_ref, a_ref, b_ref, o_ref, ob_ref):
        tot = a_ref[...] + b_ref[...]
        o_ref[...] = tot
        ob_ref[...] = tot.astype(BF16)

    spec = pl.BlockSpec((1, tr, nn), lambda j, i, c: (j, i, 0))
    grid_spec = pltpu.PrefetchScalarGridSpec(
        num_scalar_prefetch=1, grid=(n4, nb),
        in_specs=[pl.BlockSpec((1, tr, nn), lambda j, i, c: (j, c[0] * nb + i, 0)), spec], out_specs=[spec, spec])
    return pl.pallas_call(
        body, grid_spec=grid_spec,
        out_shape=[jax.ShapeDtypeStruct((n4, half, nn), F32), jax.ShapeDtypeStruct((n4, half, nn), BF16)],
        name=name, compiler_params=_cparams(("parallel", "parallel")))(cidx, mine, theirs)


def _chip_sum(quad, pair, chip_idx, name):
    _, rows, nn = quad.shape
    tr = _rows_tile(rows, 256)

    def body(s_ref, q_ref, p_ref, o_ref):
        for mine in range(4):
            @pl.when(s_ref[0] == mine)
            def _(mine=mine):
                acc = None
                for d in range(4):
                    term = p_ref[0] if d == mine else q_ref[d].astype(F32)
                    acc = term if acc is None else acc + term
                o_ref[...] = acc

    grid_spec = pltpu.PrefetchScalarGridSpec(
        num_scalar_prefetch=1, grid=(rows // tr,),
        in_specs=[pl.BlockSpec((4, tr, nn), lambda i, s: (0, i, 0)), pl.BlockSpec((1, tr, nn), lambda i, s: (s[0], i, 0))],
        out_specs=pl.BlockSpec((tr, nn), lambda i, s: (i, 0)))
    return pl.pallas_call(body, grid_spec=grid_spec, out_shape=jax.ShapeDtypeStruct((rows, nn), F32), name=name,
                          compiler_params=_cparams(("parallel",)))(chip_idx, quad, pair)


def _adam_halves_call(w, mine, other, cidx, m, v, name):
    rows, nn = w.shape
    half = rows // 2
    tr = _rows_tile(half, 128)
    nb = half // tr

    def body(c_ref, w_ref, a_ref, b_ref, m_ref, v_ref, g_ref, d_ref, nm_ref, nv_ref):
        upper = (pl.program_id(0) >= nb).astype(jnp.int32)
        g = jnp.where(upper == c_ref[0], a_ref[...], b_ref[...])
        g_ref[...] = g
        d_ref[...], nm_ref[...], nv_ref[...] = _adam_fn(w_ref[...], g, m_ref[...], v_ref[...])

    spec = pl.BlockSpec((tr, nn), lambda i, c: (i, 0))
    hspec = pl.BlockSpec((tr, nn), lambda i, c: (i % nb, 0))
    grid_spec = pltpu.PrefetchScalarGridSpec(num_scalar_prefetch=1, grid=(2 * nb,),
                                             in_specs=[spec, hspec, hspec, spec, spec], out_specs=[spec] * 4)
    return pl.pallas_call(body, grid_spec=grid_spec, out_shape=[jax.ShapeDtypeStruct((rows, nn), F32)] * 4, name=name,
                          compiler_params=_cparams(("parallel",)))(cidx, w, mine, other, m, v)


def _stack_sum(stack, name):
    n, rows, nn = stack.shape
    tr = _rows_tile(rows, 256)

    def body(s_ref, o_ref):
        acc = s_ref[0]
        for d in range(1, n):
            acc = acc + s_ref[d]
        o_ref[...] = acc

    return pl.pallas_call(
        body, grid=(rows // tr,), in_specs=[pl.BlockSpec((n, tr, nn), lambda i: (0, i, 0))],
        out_specs=pl.BlockSpec((tr, nn), lambda i: (i, 0)), out_shape=jax.ShapeDtypeStruct((rows, nn), F32),
        name=name, compiler_params=_cparams(("parallel",)))(stack)


def _adam_call(w, g, m, v, name):
    rows, nn = w.shape
    tr = _rows_tile(rows, 128)

    def body(w_ref, g_ref, m_ref, v_ref, d_ref, nm_ref, nv_ref):
        d_ref[...], nm_ref[...], nv_ref[...] = _adam_fn(w_ref[...], g_ref[...], m_ref[...], v_ref[...])

    spec = pl.BlockSpec((tr, nn), lambda i: (i, 0))
    sds = jax.ShapeDtypeStruct((rows, nn), F32)
    return pl.pallas_call(body, grid=(rows // tr,), in_specs=[spec] * 4, out_specs=[spec] * 3,
                          out_shape=[sds] * 3, name=name, compiler_params=_cparams(("parallel",)))(w, g, m, v)


def _adam_fn(w, g, m, v):
    m = ADAM_B1 * m + (1.0 - ADAM_B1) * g
    v = ADAM_B2 * v + (1.0 - ADAM_B2) * (g * g)
    m_hat = m / (1.0 - ADAM_B1 ** ADAM_STEP)
    v_hat = v / (1.0 - ADAM_B2 ** ADAM_STEP)
    delta = -ADAM_LR * (m_hat / (jnp.sqrt(v_hat) + ADAM_EPS) + ADAM_WD * w)
    return delta, m, v


def _mesh_pos():
    return lax.axis_index("x"), lax.axis_index("y"), lax.axis_index("c")


def _other_chips(x, y):
    return [(1 - x, y), (x, 1 - y), (1 - x, 1 - y)]


HBM_SPEC = pl.BlockSpec(memory_space=pl.ANY)


def _remote(src, dst, send_sems, recv_sems, k, to):
    return pltpu.make_async_remote_copy(src_ref=src, dst_ref=dst, send_sem=send_sems.at[k], recv_sem=recv_sems.at[k],
                                        device_id=to, device_id_type=MESH)


def _half_rows(c, rows, align):
    half = rows // 2
    return (pl.ds(pl.multiple_of(c * half, align), half), pl.ds(pl.multiple_of((1 - c) * half, align), half))


def _gather_weights(mats, conv):
    n = len(mats)

    def body(*refs):
        ins, conv_in = refs[:n], refs[n]
        outs, conv_out = refs[n + 1:2 * n + 1], refs[2 * n + 1]
        send_sems, recv_sems, local_sem = refs[2 * n + 2:]
        x, y, c = _mesh_pos()
        me, sibling, s = (x, y, c), (x, y, 1 - c), 2 * x + y
        chips = _other_chips(x, y)
        rows = [_half_rows(c, m.shape[0], 16) for m in mats]
        own = pltpu.make_async_copy(conv_in, conv_out.at[s], local_sem)
        own.start()
        sent = []
        for i in range(n):
            mine = rows[i][0]
            for j, (cx, cy) in enumerate(chips):
                sent.append(_remote(ins[i].at[mine], outs[i].at[s, mine], send_sems, recv_sems, 6 * i + j, (cx, cy, c)))
        for j, (cx, cy) in enumerate(chips):
            sent.append(_remote(conv_in, conv_out.at[s], send_sems, recv_sems, 6 * n + j, (cx, cy, c)))
        for cp in sent:
            cp.start()
        for i in range(n):
            mine = rows[i][0]
            for j, (cx, cy) in enumerate(chips):
                landed = outs[i].at[2 * cx + cy, mine]
                _remote(landed, landed, send_sems, recv_sems, 6 * i + j, me).wait_recv()
                fwd = _remote(landed, landed, send_sems, recv_sems, 6 * i + 3 + j, sibling)
                fwd.start()
                sent.append(fwd)
        for j, (cx, cy) in enumerate(chips):
            slot = conv_out.at[2 * cx + cy]
            _remote(slot, slot, send_sems, recv_sems, 6 * n + j, me).wait_recv()
        for i in range(n):
            theirs_rows = rows[i][1]
            for j, (cx, cy) in enumerate(chips):
                theirs = outs[i].at[2 * cx + cy, theirs_rows]
                _remote(theirs, theirs, send_sems, recv_sems, 6 * i + 3 + j, me).wait_recv()
        for cp in sent:
            cp.wait_send()
        own.wait()

    out_shape = [jax.ShapeDtypeStruct((4,) + m.shape, m.dtype) for m in mats]
    out_shape.append(jax.ShapeDtypeStruct((4,) + conv.shape, conv.dtype))
    res = pl.pallas_call(
        body, in_specs=[HBM_SPEC] * (n + 1), out_specs=[HBM_SPEC] * (n + 1), out_shape=out_shape,
        scratch_shapes=[pltpu.SemaphoreType.DMA((6 * n + 3,)), pltpu.SemaphoreType.DMA((6 * n + 3,)),
                        pltpu.SemaphoreType.DMA],
        name="all_gather_weights")(*mats, conv)
    chip = 2 * lax.axis_index("x") + lax.axis_index("y")
    full = [lax.dynamic_update_slice_in_dim(r, m[None], chip, axis=0) for r, m in zip(res[:n], mats)]
    return full, res[n]


def _sibling_exchange(stacks):
    n = len(stacks)

    def body(*refs):
        ins, outs = refs[:n], refs[n:2 * n]
        send_sems, recv_sems = refs[2 * n:]
        x, y, c = _mesh_pos()
        cps = []
        for i in range(n):
            theirs = _half_rows(c, stacks[i].shape[1], 8)[1]
            cps.append(_remote(ins[i].at[:, theirs, :], outs[i], send_sems, recv_sems, i, (x, y, 1 - c)))
        for cp in cps:
            cp.start()
        for cp in cps:
            cp.wait()

    out_shape = [jax.ShapeDtypeStruct((4, s.shape[1] // 2, s.shape[2]), s.dtype) for s in stacks]
    return pl.pallas_call(
        body, in_specs=[HBM_SPEC] * n, out_specs=[HBM_SPEC] * n, out_shape=out_shape,
        scratch_shapes=[pltpu.SemaphoreType.DMA((n,)), pltpu.SemaphoreType.DMA((n,))],
        name="grad_sibling_exchange")(*stacks)


def _chip_exchange(parts):
    n = len(parts)

    def body(*refs):
        ins, outs = refs[:n], refs[n:2 * n]
        send_sems, recv_sems = refs[2 * n:]
        x, y, c = _mesh_pos()
        me, s = (x, y, c), 2 * x + y
        chips = _other_chips(x, y)
        sent = [_remote(ins[i].at[2 * cx + cy], outs[i].at[s], send_sems, recv_sems, 3 * i + j, (cx, cy, c))
                for i in range(n) for j, (cx, cy) in enumerate(chips)]
        for cp in sent:
            cp.start()
        for i in range(n):
            for j, (cx, cy) in enumerate(chips):
                slot = outs[i].at[2 * cx + cy]
                _remote(slot, slot, send_sems, recv_sems, 3 * i + j, me).wait_recv()
        for cp in sent:
            cp.wait_send()

    return pl.pallas_call(
        body, in_specs=[HBM_SPEC] * n, out_specs=[HBM_SPEC] * n,
        out_shape=[jax.ShapeDtypeStruct(p.shape, p.dtype) for p in parts],
        scratch_shapes=[pltpu.SemaphoreType.DMA((3 * n,)), pltpu.SemaphoreType.DMA((3 * n,))],
        name="grad_chip_exchange")(*parts)


def _sibling_swap(halves):
    n = len(halves)

    def body(*refs):
        ins, outs = refs[:n], refs[n:2 * n]
        send_sems, recv_sems = refs[2 * n:]
        x, y, c = _mesh_pos()
        cps = [_remote(ins[i], outs[i], send_sems, recv_sems, i, (x, y, 1 - c)) for i in range(n)]
        for cp in cps:
            cp.start()
        for cp in cps:
            cp.wait()

    return pl.pallas_call(
        body, in_specs=[HBM_SPEC] * n, out_specs=[HBM_SPEC] * n,
        out_shape=[jax.ShapeDtypeStruct(h.shape, h.dtype) for h in halves],
        scratch_shapes=[pltpu.SemaphoreType.DMA((n,)), pltpu.SemaphoreType.DMA((n,))],
        name="grad_sibling_swap")(*halves)


def _gather_small(vec):
    def body(in_ref, out_ref, send_sems, recv_sems, local_sem):
        x, y, c = _mesh_pos()
        me = (x, y, c)
        own = pltpu.make_async_copy(in_ref, out_ref.at[4 * x + 2 * y + c], local_sem)
        own.start()
        peers = [(1 - x if k & 4 else x, 1 - y if k & 2 else y, 1 - c if k & 1 else c) for k in range(1, 8)]
        sent = [_remote(in_ref, out_ref.at[4 * x + 2 * y + c], send_sems, recv_sems, k, p) for k, p in enumerate(peers)]
        for cp in sent:
            cp.start()
        for k, (px, py, pc) in enumerate(peers):
            slot = out_ref.at[4 * px + 2 * py + pc]
            _remote(slot, slot, send_sems, recv_sems, k, me).wait_recv()
        for cp in sent:
            cp.wait_send()
        own.wait()

    return pl.pallas_call(
        body, in_specs=[HBM_SPEC], out_specs=HBM_SPEC, out_shape=jax.ShapeDtypeStruct((8,) + vec.shape, vec.dtype),
        scratch_shapes=[pltpu.SemaphoreType.DMA((7,)), pltpu.SemaphoreType.DMA((7,)), pltpu.SemaphoreType.DMA],
        name="grad_gather_small")(vec)


def _reduce_matrices(stacks, names):
    cidx = lax.axis_index("c").astype(jnp.int32).reshape(1)
    chip = (2 * lax.axis_index("x") + lax.axis_index("y")).astype(jnp.int32).reshape(1)
    got = _sibling_exchange(stacks)
    pairs = [_pair_sum(a, b, cidx, "grad_pair_sum_" + nm) for a, b, nm in zip(stacks, got, names)]
    quads = _chip_exchange([p[1] for p in pairs])
    mine = [_chip_sum(q, p[0], chip, "grad_chip_sum_" + nm) for q, p, nm in zip(quads, pairs, names)]
    return mine, _sibling_swap(mine)


def _pad_cols(a, n):
    return jnp.concatenate([a, jnp.zeros((a.shape[0], n - a.shape[1]), a.dtype)], axis=1)


def _group_channels(a):
    lead = a.shape[:-1]
    xs = a[..., :D_INNER].reshape(lead + (SSM_GROUPS, GROUP_W))
    bs = a[..., D_INNER:D_INNER + SSM_GROUPS * STATE_N].reshape(lead + (SSM_GROUPS, STATE_N))
    cs = a[..., D_INNER + SSM_GROUPS * STATE_N:].reshape(lead + (SSM_GROUPS, STATE_N))
    return jnp.concatenate([xs, bs, cs], axis=-1).reshape(lead + (CONV_CH,))


PROJ_SEGS = (('gate_a', D_MODEL), ('gate_b', D_MODEL), ('z', D_INNER), ('xbc', CONV_CH), ('q_lat', Q_RANK),
             ('kv_lat', KV_RANK), ('k_rope', LANE), ('dt', LANE))
PROJ_LANE0 = dict(zip([n for n, _ in PROJ_SEGS], [int(v) for v in np.cumsum([0] + [w for _, w in PROJ_SEGS])[:-1]]))
CONV_LANE0 = PROJ_LANE0['xbc']
KR_LANE0 = PROJ_LANE0['k_rope']


def _lay_w_in(w):
    idx = np.cumsum(IN_SIZES)[:-1]
    q_lat, kv_lat, k_rope, z, xbc, dt, gate_a, gate_b = jnp.split(w, [int(v) for v in idx], axis=1)
    return jnp.concatenate([gate_a, gate_b, z, _group_channels(xbc), q_lat, kv_lat, _pad_cols(k_rope, LANE),
                            _pad_cols(dt, LANE)], axis=1)


@jax.custom_vjp
def project(h, w, tok):
    return _project_impl(h, w)


def _project_impl(h, w):
    return (_mm(h, w, "w_in_fwd"),) + tuple(jnp.zeros((h.shape[0], wd), F32) for _, wd in PROJ_SEGS)


def _project_fwd(h, w, tok):
    return _project_impl(h, w), (h, w)


def _project_bwd(res, cots):
    h, w = res
    g = jnp.concatenate([c.astype(BF16) for c in cots[1:]], axis=1)
    return _mm(g, w.T, "w_in_dx", h.dtype), jnp.zeros_like(w), _mm_tn(h, g, "w_in_dw")


project.defvjp(_project_fwd, _project_bwd)


def _lay_w_uq(w):
    w3 = w.reshape(Q_RANK, N_HEADS, NOPE + ROPE)
    w3 = jnp.concatenate([w3, jnp.zeros((Q_RANK, N_HEADS, QK_PAD - NOPE - ROPE), w.dtype)], axis=2)
    return w3.reshape(Q_RANK, N_HEADS * QK_PAD)


def _lay_w_ukv(w):
    w3 = w.reshape(KV_RANK, N_HEADS, NOPE + V_DIM)
    return jnp.concatenate([w3[:, :, :NOPE].reshape(KV_RANK, -1), w3[:, :, NOPE:].reshape(KV_RANK, -1)], axis=1)


def _pad_lanes(v, n=LANE):
    return jnp.concatenate([v, jnp.zeros((v.shape[0], n - v.shape[1]), v.dtype)], axis=1)


def _local_loss(toks, small, x, wb, c8, posf, target):
    B, S, D = x.shape
    T = B * S

    def lin(name, a, key, lay=lambda w: w, out_dtype=F32):
        return make_linear(name, out_dtype)(a, lay(wb[key]), lay(toks[key]))

    rows2 = lambda a: a.reshape(T, a.shape[-1])
    rows3 = lambda a: a.reshape(B, S, a.shape[-1])

    sc = make_rowwise("silu_c", _f_silu, 1, 0, 0, ('row',))((c8[None],), (), ())[0][0]
    mod = lin("ada", sc, 'w_ada')[:B] + small['b_ada']
    shift1, scale1, gate1, shift2, scale2, gate2 = [m[:, None, :] for m in jnp.split(mod, 6, axis=-1)]

    modulate = make_rowwise("modulate1", _f_modulate, 1, 2, 1, ('row',))
    h = modulate((x,), (scale1, shift1), (small['g_pre_mix'],))[0]
    outs = project(rows2(h), _lay_w_in(wb['w_in']), _lay_w_in(toks['w_in']))
    proj = lax.stop_gradient(rows3(outs[0]))
    stand = {n: rows3(o) for (n, _), o in zip(PROJ_SEGS, outs[1:])}

    def win(seg, block):
        return (PROJ_LANE0[seg] // block, dict(PROJ_SEGS)[seg])

    inv = ROPE_THETA ** (-jnp.arange(ROPE // 2, dtype=F32) / (ROPE // 2))
    inv_lane = jnp.concatenate([inv, inv, jnp.zeros((LANE - ROPE,), F32)])[None]
    tabs = tuple(_rope_tables(posf, inv_lane))
    qn = make_rowwise("rms_q", _f_rms, 1, 0, 1, ('row',), windows={0: win('q_lat', Q_RANK)})(
        (proj,), (), (small['g_q_lat'],), (stand['q_lat'],))[0]
    kvn = make_rowwise("rms_kv", _f_rms, 1, 0, 1, ('row',), windows={0: win('kv_lat', KV_RANK)})(
        (proj,), (), (small['g_kv_lat'],), (stand['kv_lat'],))[0]
    qp = rows3(lin("w_uq", rows2(qn), 'w_uq', _lay_w_uq))
    kvp = rows3(lin("w_ukv", rows2(kvn), 'w_ukv', _lay_w_ukv, BF16))
    qr = rope_q(qp, tabs)
    kr = build_k(kvp, proj, stand['k_rope'], tabs)
    att = attention(qr, kr, kvp)
    attn = rows3(lin("w_o_attn", rows2(att), 'w_o_attn'))

    xa = conv_silu(proj, stand['xbc'], _group_channels(wb['conv_w_f32']), _group_channels(small['conv_b']))
    dt_pad, a_pad = make_rowwise("dt_softplus", _f_dt, 1, 0, 2, ('row', 'row'), windows={0: win('dt', LANE)})(
        (proj,), (), (_pad_lanes(small['dt_bias']), _pad_lanes(small['a_log'])), (stand['dt'],))
    ac_pad = chunk_cumsum(a_pad)
    acr = jnp.transpose(ac_pad[..., :SSM_HEADS], (0, 2, 1))[:, :, None, :]
    dsk = jnp.repeat(small['d_skip'], HEAD_P, axis=-1)
    y = ssd(xa, dt_pad, ac_pad, acr, dsk)
    yg = make_rowwise("gated_norm", _f_gated_norm, 2, 0, 1, ('row',), ncol=SSM_GROUPS, ts_cap=2048,
                      windows={1: win('z', GROUP_W)})((y, proj), (), (small['g_ssm_out'],), (stand['z'],))[0]
    ssm = rows3(lin("w_o_ssm", rows2(yg), 'w_o_ssm'))

    merged = make_rowwise("merge", _f_merge, 4, 0, 0, ('row',),
                          windows={2: win('gate_a', D_MODEL), 3: win('gate_b', D_MODEL)})(
        (attn, ssm, proj, proj), (), (), (stand['gate_a'], stand['gate_b']))[0]
    mix = rows3(lin("w_out", rows2(merged), 'w_out'))
    x1 = make_rowwise("post_mix", _f_post, 2, 1, 1, ('row',))((x, mix), (gate1,), (small['g_post_mix'],))[0]

    h2 = make_rowwise("modulate2", _f_modulate, 1, 2, 1, ('row',))((x1,), (scale2, shift2), (small['g_pre_mlp'],))[0]
    ff = rows3(ffn(rows2(h2), wb['w_ff1'], toks['w_ff1'], wb['w_ff2'], toks['w_ff2']))
    lvec = make_rowwise("final_loss", _f_final_loss, 3, 1, 1, ('sum',), nodiff=(2,))(
        (x1, ff, target), (gate2,), (small['g_post_mlp'],))[0]
    return jnp.sum(lvec)


MATRICES = COL_SHARDED + ROW_SHARDED


def _local_step(x, c, positions, target, wb, small):
    B = x.shape[0]
    c8 = jnp.concatenate([c, jnp.zeros((16 - B, c.shape[1]), F32)], axis=0)
    posf = positions.astype(F32)[..., None]
    toks = {k: jnp.zeros(wb[k].shape, F32) for k in MATRICES if k != 'conv_w'}
    conv_w = wb['conv_w_f32']

    def loss_fn(toks, small, conv_w, x):
        wbl = dict(wb)
        wbl['conv_w_f32'] = conv_w
        return _local_loss(toks, small, x, wbl, c8, posf, target)

    loss, (g_tok, g_small, g_conv, g_x) = jax.value_and_grad(loss_fn, argnums=(0, 1, 2, 3))(toks, small, conv_w, x)
    grads = dict(g_tok)
    grads.update(g_small)
    grads['conv_w'] = g_conv
    return loss, g_x, grads


def kernel(x, c, positions, w_ada, b_ada, g_pre_mix, g_post_mix, w_in, g_q_lat, g_kv_lat, w_uq, w_ukv, w_o_attn, conv_w, conv_b, dt_bias, a_log, d_skip, g_ssm_out, w_o_ssm, w_out, g_pre_mlp, g_post_mlp, w_ff1, w_ff2, loss_target, m_w_ada, m_b_ada, m_g_pre_mix, m_g_post_mix, m_w_in, m_g_q_lat, m_g_kv_lat, m_w_uq, m_w_ukv, m_w_o_attn, m_conv_w, m_conv_b, m_dt_bias, m_a_log, m_d_skip, m_g_ssm_out, m_w_o_ssm, m_w_out, m_g_pre_mlp, m_g_post_mlp, m_w_ff1, m_w_ff2, v_w_ada, v_b_ada, v_g_pre_mix, v_g_post_mix, v_w_in, v_g_q_lat, v_g_kv_lat, v_w_uq, v_w_ukv, v_w_o_attn, v_conv_w, v_conv_b, v_dt_bias, v_a_log, v_d_skip, v_g_ssm_out, v_w_o_ssm, v_w_out, v_g_pre_mlp, v_g_post_mlp, v_w_ff1, v_w_ff2):
    given = dict(locals())
    w_loc = {n: given[n] for n in WEIGHTS}
    m_loc = {n: given["m_" + n] for n in WEIGHTS}
    v_loc = {n: given["v_" + n] for n in WEIGHTS}
    mats = [n for n in WEIGHTS if n in MATRICES and n != 'conv_w']
    vecs = [n for n in WEIGHTS if n not in MATRICES]

    g_mats, g_conv = _gather_weights([w_loc[n][0].astype(BF16) for n in mats], conv_w[0])
    wb = {}
    for n, g in zip(mats, g_mats):
        if n in COL_SHARDED:
            wb[n] = jnp.transpose(g, (1, 0, 2)).reshape(g.shape[1], -1)
        else:
            wb[n] = g.reshape(-1, g.shape[2])
    wb['conv_w_f32'] = jnp.transpose(g_conv, (1, 0, 2)).reshape(CONV_K, -1)
    small = {n: w_loc[n] for n in vecs}

    loss_part, grad_x, grads = _local_step(x, c, positions, loss_target, wb, small)
    loss = lax.psum(loss_part, ("x", "y", "c"))

    stacks = []
    for n in mats:
        kk, nn = w_loc[n].shape[1:]
        if n in COL_SHARDED:
            stacks.append(jnp.transpose(grads[n].reshape(kk, 4, nn), (1, 0, 2)))
        else:
            stacks.append(grads[n].reshape(4, kk, nn))
    g_mine, g_other = _reduce_matrices(stacks, mats)
    g_shard = {}

    vec_shapes = [tuple(grads[n].shape) for n in vecs] + [tuple(grads['conv_w'].shape)]
    total = _stack_sum(_gather_small(_pack_small([grads[n] for n in vecs] + [grads['conv_w']])), "grad_sum_small")
    g_vec = _unpack_small(total, vec_shapes)
    n_conv = conv_w.shape[2]
    chip = 2 * lax.axis_index("x") + lax.axis_index("y")
    g_shard['conv_w'] = lax.dynamic_slice_in_dim(g_vec[-1], chip * n_conv, n_conv, axis=1)
    for n, g in zip(vecs, g_vec):
        g_shard[n] = g

    delta, new_m, new_v = {}, {}, {}
    cidx = lax.axis_index("c").astype(jnp.int32).reshape(1)
    for n, mine, other in zip(mats, g_mine, g_other):
        g_shard[n], delta[n], new_m[n], new_v[n] = _adam_halves_call(
            w_loc[n][0], mine, other, cidx, m_loc[n][0], v_loc[n][0], "adamw_" + n)
    rest = vecs + ['conv_w']
    rest_shapes = [tuple(w_loc[n].shape) for n in rest]
    packed = [_pack_small([src[n] for n in rest]) for src in (w_loc, g_shard, m_loc, v_loc)]
    for dst, buf in zip((delta, new_m, new_v), _adam_call(*packed, "adamw_small")):
        dst.update(zip(rest, _unpack_small(buf, rest_shapes)))

    def out(d):
        return [d[n].reshape(w_loc[n].shape) for n in WEIGHTS]

    return (loss, grad_x, *out(g_shard), *out(delta), *out(new_m), *out(new_v))
```

```python
import functools
import math

import numpy as np
import jax
import jax.numpy as jnp
from jax import lax
from jax.experimental import pallas as pl
from jax.experimental.pallas import tpu as pltpu

F32 = jnp.float32
BF16 = jnp.bfloat16
MESH = pl.DeviceIdType.MESH

D_MODEL = 1024
N_HEADS = 8
NOPE = 128
ROPE = 64
V_DIM = 128
Q_RANK = 256
KV_RANK = 256
ROPE_THETA = 10000.0
D_INNER = 2048
SSM_HEADS = 32
SSM_GROUPS = 8
HEAD_P = 64
STATE_N = 128
CONV_K = 4
CHUNK = 128
CONV_CH = D_INNER + 2 * SSM_GROUPS * STATE_N
D_FF = 4096
EPS = 1e-6
IN_SIZES = (Q_RANK, KV_RANK, ROPE, D_INNER, CONV_CH, SSM_HEADS, D_MODEL, D_MODEL)
ADAM_LR, ADAM_B1, ADAM_B2, ADAM_EPS, ADAM_WD, ADAM_STEP = 0.001, 0.9, 0.999, 1e-08, 0.01, 10

VMEM_LIMIT_BYTES = 52 * 1024 * 1024
LANE = 128
QK_PAD = 256

WEIGHTS = ['w_ada', 'b_ada', 'g_pre_mix', 'g_post_mix', 'w_in', 'g_q_lat', 'g_kv_lat', 'w_uq', 'w_ukv',
           'w_o_attn', 'conv_w', 'conv_b', 'dt_bias', 'a_log', 'd_skip', 'g_ssm_out', 'w_o_ssm', 'w_out',
           'g_pre_mlp', 'g_post_mlp', 'w_ff1', 'w_ff2']
COL_SHARDED = ('w_ada', 'w_in', 'w_uq', 'w_ukv', 'conv_w', 'w_ff1')
ROW_SHARDED = ('w_o_attn', 'w_o_ssm', 'w_out', 'w_ff2')


def _cparams(sem):
    return pltpu.CompilerParams(dimension_semantics=sem, vmem_limit_bytes=VMEM_LIMIT_BYTES)


def _tile(n, cap):
    if n <= cap:
        return n
    k = n // LANE
    best = LANE
    for d in range(1, k + 1):
        if k % d == 0 and d * LANE <= cap:
            best = d * LANE
    return best


def _mm(a, w, name, out_dtype=F32, epilogue=None, extras=(), out_dtypes=None):
    M, K = a.shape
    N = w.shape[1]
    tm = min(M, 1024)
    tn = _tile(N, 1024)
    tk = _tile(K, 2048)
    nk = K // tk
    dts = tuple(out_dtypes) if epilogue is not None else (out_dtype,)
    n_x, n_o = len(extras), len(dts)

    def finish(acc, refs):
        res = epilogue(acc, *[r[...] for r in refs[:n_x]]) if epilogue is not None else (acc,)
        for o_ref, val, dt in zip(refs[n_x:n_x + n_o], res, dts):
            o_ref[...] = val.astype(dt)

    def body(a_ref, w_ref, *refs):
        part = jnp.dot(a_ref[...].astype(BF16), w_ref[...], preferred_element_type=F32)
        if nk == 1:
            finish(part, refs)
        else:
            acc_ref = refs[-1]
            k = pl.program_id(2)

            @pl.when(k == 0)
            def _():
                acc_ref[...] = part

            @pl.when(k > 0)
            def _():
                acc_ref[...] += part

            @pl.when(k == nk - 1)
            def _():
                finish(acc_ref[...], refs)

    ospec = pl.BlockSpec((tm, tn), lambda i, j, k: (i, j))
    res = pl.pallas_call(
        body, grid=(M // tm, N // tn, nk),
        in_specs=[pl.BlockSpec((tm, tk), lambda i, j, k: (i, k)), pl.BlockSpec((tk, tn), lambda i, j, k: (k, j))]
        + [ospec] * n_x,
        out_specs=[ospec] * n_o, out_shape=[jax.ShapeDtypeStruct((M, N), dt) for dt in dts],
        scratch_shapes=[pltpu.VMEM((tm, tn), F32)] if nk > 1 else [], name=name,
        compiler_params=_cparams(("parallel", "parallel", "arbitrary")))(a, w, *extras)
    return res if epilogue is not None else res[0]


def _mm_tn(a, g, name):
    M, K = a.shape
    N = g.shape[1]
    tm = min(M, 1024)
    tk = _tile(K, 1024)
    tn = _tile(N, 1024)
    nm = M // tm

    def body(a_ref, g_ref, o_ref):
        part = lax.dot_general(a_ref[...].astype(BF16), g_ref[...].astype(BF16), (((0,), (0,)), ((), ())),
                               preferred_element_type=F32)
        m = pl.program_id(2)

        @pl.when(m == 0)
        def _():
            o_ref[...] = part

        @pl.when(m > 0)
        def _():
            o_ref[...] += part

    return pl.pallas_call(
        body, grid=(K // tk, N // tn, nm),
        in_specs=[pl.BlockSpec((tm, tk), lambda i, j, m: (m, i)), pl.BlockSpec((tm, tn), lambda i, j, m: (m, j))],
        out_specs=pl.BlockSpec((tk, tn), lambda i, j, m: (i, j)),
        out_shape=jax.ShapeDtypeStruct((K, N), F32), name=name,
        compiler_params=_cparams(("parallel", "parallel", "arbitrary")))(a, g)


def make_linear(name, out_dtype=F32):
    @jax.custom_vjp
    def linear(a, w, tok):
        return _mm(a, w, name + "_fwd", out_dtype)

    def fwd(a, w, tok):
        return _mm(a, w, name + "_fwd", out_dtype), (a, w)

    def bwd(res, g):
        a, w = res
        da = _mm(g, w.T, name + "_dx", a.dtype)
        dw = _mm_tn(a, g, name + "_dw")
        return da, jnp.zeros_like(w), dw

    linear.defvjp(fwd, bwd)
    return linear


def _relu2_epilogue(acc):
    r = jnp.maximum(acc, 0.0)
    return r * r, r


def _relu2_bwd_epilogue(acc, r):
    return (acc * (2.0 * r.astype(F32)),)


@jax.custom_vjp
def ffn(h, w1, tok1, w2, tok2):
    act, _ = _mm(h, w1, "w_ff1_fwd", epilogue=_relu2_epilogue, out_dtypes=(BF16, BF16))
    return _mm(act, w2, "w_ff2_fwd")


def _ffn_fwd(h, w1, tok1, w2, tok2):
    act, r = _mm(h, w1, "w_ff1_fwd", epilogue=_relu2_epilogue, out_dtypes=(BF16, BF16))
    return _mm(act, w2, "w_ff2_fwd"), (h, w1, w2, act, r)


def _ffn_bwd(res, g):
    h, w1, w2, act, r = res
    du = _mm(g, w2.T, "w_ff2_dx", epilogue=_relu2_bwd_epilogue, extras=(r,), out_dtypes=(BF16,))[0]
    dw2 = _mm_tn(act, g, "w_ff2_dw")
    dw1 = _mm_tn(h, du, "w_ff1_dw")
    dh = _mm(du, w1.T, "w_ff1_dx", h.dtype)
    return dh, jnp.zeros_like(w1), dw1, jnp.zeros_like(w2), dw2


ffn.defvjp(_ffn_fwd, _ffn_bwd)


def make_rowwise(name, f, n_rows, n_seqs, n_pars, out_kinds, ncol=1, nodiff=(), ts_cap=512, windows=None):
    windows = dict(windows or {})
    n_in = n_rows + n_seqs + n_pars
    diff_idx = [i for i in range(n_in) if i not in nodiff]

    def _dims(rows):
        B, S = rows[0].shape[0], rows[0].shape[1]
        ts = min(S, ts_cap)
        return B, S, ts

    def _width(i, r):
        return windows[i][1] if i in windows else r.shape[2]

    def _in_specs(rows, seqs, pars, ts):
        specs = []
        for i, r in enumerate(rows):
            col0 = windows[i][0] if i in windows else 0
            specs.append(pl.BlockSpec((1, ts, _width(i, r) // ncol), lambda k, b, s, col0=col0: (b, s, k + col0)))
        for q in seqs:
            specs.append(pl.BlockSpec((1, 1, q.shape[2] // ncol), lambda k, b, s: (b, 0, k)))
        for p in pars:
            specs.append(pl.BlockSpec((1, p.shape[1] // ncol), lambda k, b, s: (0, k)))
        return specs

    def _load(refs):
        vals = [r[0] for r in refs[:n_rows + n_seqs]]
        vals += [r[...] for r in refs[n_rows + n_seqs:n_in]]
        return vals

    def _out_struct(rows, seqs, pars, ts):
        blocks = [jax.ShapeDtypeStruct((ts, _width(i, r) // ncol), r.dtype) for i, r in enumerate(rows)]
        blocks += [jax.ShapeDtypeStruct((1, q.shape[2] // ncol), q.dtype) for q in seqs]
        blocks += [jax.ShapeDtypeStruct((1, p.shape[1] // ncol), p.dtype) for p in pars]
        return jax.eval_shape(f, *blocks)

    def _fwd_call(rows, seqs, pars):
        B, S, ts = _dims(rows)
        outs = _out_struct(rows, seqs, pars, ts)
        n_out = len(outs)

        def body(*refs):
            res = f(*_load(refs))
            first = (pl.program_id(1) == 0) & (pl.program_id(2) == 0)
            for o_ref, val, kind in zip(refs[n_in:], res, out_kinds):
                if kind == 'row':
                    o_ref[0] = val
                else:
                    tot = jnp.sum(val, axis=0, keepdims=True)

                    @pl.when(first)
                    def _(o_ref=o_ref, tot=tot):
                        o_ref[...] = tot

                    @pl.when(jnp.logical_not(first))
                    def _(o_ref=o_ref, tot=tot):
                        o_ref[...] += tot

        out_shape, out_specs = [], []
        for o, kind in zip(outs, out_kinds):
            d = o.shape[1]
            if kind == 'row':
                out_shape.append(jax.ShapeDtypeStruct((B, S, ncol * d), o.dtype))
                out_specs.append(pl.BlockSpec((1, ts, d), lambda k, b, s: (b, s, k)))
            else:
                out_shape.append(jax.ShapeDtypeStruct((1, ncol * d), o.dtype))
                out_specs.append(pl.BlockSpec((1, d), lambda k, b, s: (0, k)))
        res = pl.pallas_call(
            body, grid=(ncol, B, S // ts), in_specs=_in_specs(rows, seqs, pars, ts), out_specs=out_specs,
            out_shape=out_shape, name=name + "_fwd",
            compiler_params=_cparams(("arbitrary", "arbitrary", "arbitrary")))(*rows, *seqs, *pars)
        return tuple(res)

    def _bwd_call(rows, seqs, pars, cots):
        B, S, ts = _dims(rows)
        outs = _out_struct(rows, seqs, pars, ts)
        n_out = len(outs)
        all_in = list(rows) + list(seqs) + list(pars)

        def body(*refs):
            vals = _load(refs)
            cts = []
            for c_ref, o, kind in zip(refs[n_in:n_in + n_out], outs, out_kinds):
                if kind == 'row':
                    cts.append(c_ref[0])
                else:
                    cts.append(jnp.broadcast_to(c_ref[...], o.shape))

            def g(*dv):
                full = list(vals)
                for i, v in zip(diff_idx, dv):
                    full[i] = v
                return tuple(f(*full))

            _, vjp = jax.vjp(g, *[vals[i] for i in diff_idx])
            grads = vjp(tuple(cts))
            b, s = pl.program_id(1), pl.program_id(2)
            for o_ref, i, gr in zip(refs[n_in + n_out:], diff_idx, grads):
                if i < n_rows:
                    o_ref[0] = gr.astype(o_ref.dtype)
                else:
                    first = (s == 0) if i < n_rows + n_seqs else ((b == 0) & (s == 0))
                    target = (lambda r: r.at[0]) if i < n_rows + n_seqs else (lambda r: r)

                    @pl.when(first)
                    def _(o_ref=o_ref, gr=gr, target=target):
                        target(o_ref)[...] = gr

                    @pl.when(jnp.logical_not(first))
                    def _(o_ref=o_ref, gr=gr, target=target):
                        target(o_ref)[...] += gr

        cot_specs = []
        for o, kind in zip(outs, out_kinds):
            d = o.shape[1]
            if kind == 'row':
                cot_specs.append(pl.BlockSpec((1, ts, d), lambda k, b, s: (b, s, k)))
            else:
                cot_specs.append(pl.BlockSpec((1, d), lambda k, b, s: (0, k)))
        out_shape, out_specs = [], []
        for i in diff_idx:
            a = all_in[i]
            if i < n_rows:
                out_shape.append(jax.ShapeDtypeStruct((B, S, _width(i, a)), BF16 if i in windows else a.dtype))
                out_specs.append(pl.BlockSpec((1, ts, _width(i, a) // ncol), lambda k, b, s: (b, s, k)))
                continue
            out_shape.append(jax.ShapeDtypeStruct(a.shape, a.dtype))
            if i < n_rows + n_seqs:
                out_specs.append(pl.BlockSpec((1, 1, a.shape[2] // ncol), lambda k, b, s: (b, 0, k)))
            else:
                out_specs.append(pl.BlockSpec((1, a.shape[1] // ncol), lambda k, b, s: (0, k)))
        res = pl.pallas_call(
            body, grid=(ncol, B, S // ts), in_specs=_in_specs(rows, seqs, pars, ts) + cot_specs,
            out_specs=out_specs, out_shape=out_shape, name=name + "_bwd",
            compiler_params=_cparams(("arbitrary", "arbitrary", "arbitrary")))(*all_in, *cots)
        grads = [None] * n_in
        for i, r in zip(diff_idx, res):
            grads[i] = r
        for i in nodiff:
            grads[i] = jnp.zeros_like(all_in[i])
        stand_in_grads = tuple(grads[i] for i in sorted(windows))
        for i in windows:
            grads[i] = jnp.zeros_like(all_in[i])
        return (tuple(grads[:n_rows]), tuple(grads[n_rows:n_rows + n_seqs]), tuple(grads[n_rows + n_seqs:]),
                stand_in_grads)

    @jax.custom_vjp
    def op(rows, seqs, pars, stand_ins):
        return _fwd_call(rows, seqs, pars)

    def fwd(rows, seqs, pars, stand_ins):
        return _fwd_call(rows, seqs, pars), (rows, seqs, pars)

    def bwd(res, cots):
        rows, seqs, pars = res
        return _bwd_call(rows, seqs, pars, cots)

    op.defvjp(fwd, bwd)
    return lambda rows, seqs, pars, stand_ins=(): op(tuple(rows), tuple(seqs), tuple(pars), tuple(stand_ins))


def _rms(x, g):
    return x * lax.rsqrt(jnp.mean(x * x, axis=-1, keepdims=True) + EPS) * g


def _silu(x):
    return x * lax.logistic(x)


def _f_silu(c):
    return (_silu(c),)


def _f_modulate(x, scale, shift, g):
    return ((_rms(x, g) * (1.0 + scale) + shift).astype(BF16),)


def _f_rms(x, g):
    return (_rms(x, g).astype(BF16),)


def _f_dt(dt_raw, dt_bias, a_log):
    z = dt_raw + dt_bias
    dt = jnp.maximum(z, 0.0) + jnp.log1p(jnp.exp(-jnp.abs(z)))
    return dt, dt * (-jnp.exp(a_log))


def _f_gated_norm(y, z, g):
    return (_rms(y * _silu(z), g).astype(BF16),)


def _f_merge(attn, ssm, ga, gb):
    return ((lax.logistic(ga) * attn + lax.logistic(gb) * ssm).astype(BF16),)


def _f_post(x, m, gate, g):
    return (x + gate * _rms(m, g),)


def _f_final_loss(x, ff, target, gate, g):
    e = x + gate * _rms(ff, g) - target
    return (e * e * (0.5 / D_MODEL),)


def _rope_tables(posf, inv_lane):
    B, S, _ = posf.shape
    ts = min(S, 512)

    def body(p_ref, inv_ref, c_ref, a_ref, b_ref):
        ang = p_ref[0] * inv_ref[...]
        cs, sn = jnp.cos(ang), jnp.sin(ang)
        lane = lax.broadcasted_iota(jnp.int32, ang.shape, 1)
        c_ref[0] = jnp.where(lane < ROPE, cs, 0.0)
        a_ref[0] = jnp.where(lane < ROPE // 2, -sn, 0.0)
        b_ref[0] = jnp.where((lane >= ROPE // 2) & (lane < ROPE), sn, 0.0)

    spec = pl.BlockSpec((1, ts, LANE), lambda b, s: (b, s, 0))
    sds = jax.ShapeDtypeStruct((B, S, LANE), F32)
    return pl.pallas_call(
        body, grid=(B, S // ts),
        in_specs=[pl.BlockSpec((1, ts, 1), lambda b, s: (b, s, 0)), pl.BlockSpec((1, LANE), lambda b, s: (0, 0))],
        out_specs=[spec, spec, spec], out_shape=[sds, sds, sds], name="rope_tables",
        compiler_params=_cparams(("parallel", "parallel")))(posf, inv_lane)


def _rot(u, c, a, bm):
    return u * c + pltpu.roll(u, 96, 1) * a + pltpu.roll(u, 32, 1) * bm


def _rot_t(g, c, a, bm):
    return g * c + pltpu.roll(g * a, 32, 1) + pltpu.roll(g * bm, 96, 1)


def _rope_q_call(q, tabs, transpose, name):
    B, S, W = q.shape
    ts = min(S, 512)
    fn = _rot_t if transpose else _rot
    out_dtype = F32 if transpose else BF16

    def body(q_ref, c_ref, a_ref, b_ref, o_ref):
        tc, ta, tb = c_ref[0], a_ref[0], b_ref[0]
        for h in range(W // QK_PAD):
            u = q_ref[0, :, h * QK_PAD:(h + 1) * QK_PAD].astype(F32) * ATT_SCALE
            r = fn(u[:, NOPE:], tc, ta, tb)
            o_ref[0, :, h * QK_PAD:(h + 1) * QK_PAD] = jnp.concatenate([u[:, :NOPE], r], axis=1).astype(out_dtype)

    tspec = pl.BlockSpec((1, ts, LANE), lambda b, s: (b, s, 0))
    qspec = pl.BlockSpec((1, ts, W), lambda b, s: (b, s, 0))
    return pl.pallas_call(
        body, grid=(B, S // ts), in_specs=[qspec, tspec, tspec, tspec], out_specs=qspec,
        out_shape=jax.ShapeDtypeStruct(q.shape, out_dtype), name=name,
        compiler_params=_cparams(("parallel", "parallel")))(q, *tabs)


@jax.custom_vjp
def rope_q(q, tabs):
    return _rope_q_call(q, tabs, False, "rope_q_fwd")


def _rope_q_fwd(q, tabs):
    return _rope_q_call(q, tabs, False, "rope_q_fwd"), tabs


def _rope_q_bwd(tabs, g):
    return _rope_q_call(g, tabs, True, "rope_q_bwd"), tuple(jnp.zeros_like(t) for t in tabs)


rope_q.defvjp(_rope_q_fwd, _rope_q_bwd)


def _build_k_fwd_call(kv, kr, tabs):
    B, S, _ = kv.shape
    ts = min(S, 512)

    def body(kv_ref, kr_ref, c_ref, a_ref, b_ref, o_ref):
        r = _rot(kr_ref[0], c_ref[0], a_ref[0], b_ref[0]).astype(BF16)
        for h in range(N_HEADS):
            o_ref[0, :, h * QK_PAD:(h + 1) * QK_PAD] = jnp.concatenate(
                [kv_ref[0, :, h * NOPE:(h + 1) * NOPE], r], axis=1)

    tspec = pl.BlockSpec((1, ts, LANE), lambda b, s: (b, s, 0))
    kr_spec = pl.BlockSpec((1, ts, LANE), lambda b, s: (b, s, KR_LANE0 // LANE))
    return pl.pallas_call(
        body, grid=(B, S // ts),
        in_specs=[pl.BlockSpec((1, ts, N_HEADS * NOPE), lambda b, s: (b, s, 0)), kr_spec, tspec, tspec, tspec],
        out_specs=pl.BlockSpec((1, ts, N_HEADS * QK_PAD), lambda b, s: (b, s, 0)),
        out_shape=jax.ShapeDtypeStruct((B, S, N_HEADS * QK_PAD), BF16), name="build_k_fwd",
        compiler_params=_cparams(("parallel", "parallel")))(kv, kr, *tabs)


def _build_k_bwd_call(g, tabs):
    B, S, _ = g.shape
    ts = min(S, 512)

    def body(g_ref, c_ref, a_ref, b_ref, dk_ref, dr_ref):
        tot = None
        for h in range(N_HEADS):
            dk_ref[0, :, h * NOPE:(h + 1) * NOPE] = g_ref[0, :, h * QK_PAD:h * QK_PAD + NOPE]
            part = g_ref[0, :, h * QK_PAD + NOPE:(h + 1) * QK_PAD].astype(F32)
            tot = part if tot is None else tot + part
        dr_ref[0] = _rot_t(tot, c_ref[0], a_ref[0], b_ref[0]).astype(BF16)

    tspec = pl.BlockSpec((1, ts, LANE), lambda b, s: (b, s, 0))
    return pl.pallas_call(
        body, grid=(B, S // ts),
        in_specs=[pl.BlockSpec((1, ts, N_HEADS * QK_PAD), lambda b, s: (b, s, 0)), tspec, tspec, tspec],
        out_specs=[pl.BlockSpec((1, ts, N_HEADS * NOPE), lambda b, s: (b, s, 0)), tspec],
        out_shape=[jax.ShapeDtypeStruct((B, S, N_HEADS * NOPE), BF16), jax.ShapeDtypeStruct((B, S, LANE), BF16)],
        name="build_k_bwd", compiler_params=_cparams(("parallel", "parallel")))(g, *tabs)


@jax.custom_vjp
def build_k(kv, src, stand_in, tabs):
    return _build_k_fwd_call(kv, src, tabs)


def _build_k_fwd(kv, src, stand_in, tabs):
    return _build_k_fwd_call(kv, src, tabs), (tabs, kv.shape, src)


def _build_k_bwd(res, g):
    tabs, kv_shape, src = res
    dk, dr = _build_k_bwd_call(g, tabs)
    dkv = jnp.concatenate([dk, jnp.zeros((kv_shape[0], kv_shape[1], kv_shape[2] - dk.shape[2]), BF16)], axis=-1)
    return dkv, jnp.zeros_like(src), dr, tuple(jnp.zeros_like(t) for t in tabs)


build_k.defvjp(_build_k_fwd, _build_k_bwd)


ATT_SCALE = (NOPE + ROPE) ** -0.5
NEG = -1e30


def _att_tiles(S):
    t = min(S, 512)
    return t, S // t


def _scores(q, k, diagonal):
    s = lax.dot_general(q, k, (((1,), (1,)), ((), ())), preferred_element_type=F32)
    if diagonal:
        row = lax.broadcasted_iota(jnp.int32, s.shape, 0)
        col = lax.broadcasted_iota(jnp.int32, s.shape, 1)
        s = jnp.where(col <= row, s, NEG)
    return s


ATT_HB = 4


def _causal_pairs(n):
    pairs = [(i, j) for i in range(n) for j in range(i + 1)]
    return (jnp.asarray([p[0] for p in pairs], jnp.int32), jnp.asarray([p[1] for p in pairs], jnp.int32))


def _head(ref_or_val, h, w):
    return ref_or_val[:, h * w:(h + 1) * w]


def _attn_fwd_call(q, k, vsrc, v_blk0):
    B, S, _ = q.shape
    t, n = _att_tiles(S)
    qi, kj = _causal_pairs(n)

    def body(qi_ref, kj_ref, q_ref, k_ref, v_ref, o_ref, lse_ref, m_sc, l_sc, acc_sc):
        p_id = pl.program_id(2)
        i, j = qi_ref[p_id], kj_ref[p_id]

        @pl.when(j == 0)
        def _():
            m_sc[...] = jnp.full(m_sc.shape, NEG, F32)
            l_sc[...] = jnp.zeros(l_sc.shape, F32)
            acc_sc[...] = jnp.zeros(acc_sc.shape, F32)

        def step(diagonal):
            qa, ka, va = q_ref[0], k_ref[0], v_ref[0]
            for h in range(ATT_HB):
                lanes = slice(h * LANE, (h + 1) * LANE)
                s = _scores(_head(qa, h, QK_PAD), _head(ka, h, QK_PAD), diagonal)
                m_prev = m_sc[:, lanes]
                m_new = jnp.maximum(m_prev, jnp.max(s, axis=1, keepdims=True))
                alpha = jnp.exp(m_prev - m_new)
                p = jnp.exp(s - jnp.tile(m_new, (1, t // LANE)))
                l_sc[:, lanes] = alpha * l_sc[:, lanes] + jnp.sum(p, axis=1, keepdims=True)
                acc_sc[:, lanes] = alpha * acc_sc[:, lanes] + jnp.dot(p.astype(BF16), _head(va, h, V_DIM),
                                                                      preferred_element_type=F32)
                m_sc[:, lanes] = m_new

        @pl.when(j < i)
        def _():
            step(False)

        @pl.when(j == i)
        def _():
            step(True)
            o_ref[0] = acc_sc[...] / l_sc[...]
            lse_ref[0] = m_sc[...] + jnp.log(l_sc[...])

    wq, wv = ATT_HB * QK_PAD, ATT_HB * V_DIM
    grid_spec = pltpu.PrefetchScalarGridSpec(
        num_scalar_prefetch=2, grid=(B, N_HEADS // ATT_HB, qi.shape[0]),
        in_specs=[pl.BlockSpec((1, t, wq), lambda b, h, p, qi, kj: (b, qi[p], h)),
                  pl.BlockSpec((1, t, wq), lambda b, h, p, qi, kj: (b, kj[p], h)),
                  pl.BlockSpec((1, t, wv), lambda b, h, p, qi, kj: (b, kj[p], v_blk0 + h))],
        out_specs=[pl.BlockSpec((1, t, wv), lambda b, h, p, qi, kj: (b, qi[p], h)),
                   pl.BlockSpec((1, t, wv), lambda b, h, p, qi, kj: (b, qi[p], h))],
        scratch_shapes=[pltpu.VMEM((t, wv), F32), pltpu.VMEM((t, wv), F32), pltpu.VMEM((t, wv), F32)])
    return pl.pallas_call(
        body, grid_spec=grid_spec,
        out_shape=[jax.ShapeDtypeStruct((B, S, N_HEADS * V_DIM), F32),
                   jax.ShapeDtypeStruct((B, S, N_HEADS * LANE), F32)],
        name="attn_fwd", compiler_params=_cparams(("parallel", "parallel", "arbitrary")))(qi, kj, q, k, vsrc)


def _attn_p_ds(q, k, v, o, do, lse, diagonal, t):
    s = _scores(q, k, diagonal)
    p = jnp.exp(s - jnp.tile(lse, (1, t // LANE)))
    dp = lax.dot_general(do.astype(BF16), v, (((1,), (1,)), ((), ())), preferred_element_type=F32)
    delta = jnp.sum(do * o, axis=1, keepdims=True)
    ds = p * (dp - delta)
    return p, ds


ATT_HB_BWD = 2


def _attn_bwd_call(q, k, vsrc, o, do, lse):
    B, S, _ = q.shape
    t, n = _att_tiles(S)
    qi, kj = _causal_pairs(n)
    n_pairs = qi.shape[0]
    hb = ATT_HB_BWD
    v_blk0 = N_HEADS // hb

    def body(qi_ref, kj_ref, q_ref, k_ref, v_ref, o_ref, do_ref, lse_ref, dq_ref, dk_ref, dv_ref, dq_sc, dk_sc, dv_sc):
        p_id = pl.program_id(2)
        i, j = qi_ref[p_id], kj_ref[p_id]

        @pl.when(p_id == 0)
        def _():
            dk_sc[...] = jnp.zeros(dk_sc.shape, F32)
            dv_sc[...] = jnp.zeros(dv_sc.shape, F32)

        @pl.when(j == 0)
        def _():
            dq_sc[...] = jnp.zeros(dq_sc.shape, F32)

        rows = pl.ds(pl.multiple_of(j * t, t), t)

        def step(diagonal):
            qa, ka, va, oa, doa, la = q_ref[0], k_ref[0], v_ref[0], o_ref[0], do_ref[0], lse_ref[0]
            for h in range(hb):
                qb, kb, dob = _head(qa, h, QK_PAD), _head(ka, h, QK_PAD), _head(doa, h, V_DIM)
                p, ds = _attn_p_ds(qb, kb, _head(va, h, V_DIM), _head(oa, h, V_DIM), dob, _head(la, h, LANE),
                                   diagonal, t)
                dsb = ds.astype(BF16)
                dq_sc[:, h * QK_PAD:(h + 1) * QK_PAD] += jnp.dot(dsb, kb, preferred_element_type=F32)
                dv_sc[rows, h * V_DIM:(h + 1) * V_DIM] += lax.dot_general(
                    p.astype(BF16), dob.astype(BF16), (((0,), (0,)), ((), ())), preferred_element_type=F32)
                dk_sc[rows, h * QK_PAD:(h + 1) * QK_PAD] += lax.dot_general(
                    dsb, qb, (((0,), (0,)), ((), ())), preferred_element_type=F32)

        @pl.when(j < i)
        def _():
            step(False)

        @pl.when(j == i)
        def _():
            step(True)
            dq_ref[0] = dq_sc[...].astype(BF16)

        @pl.when(p_id == n_pairs - 1)
        def _():
            dk_ref[0] = dk_sc[...].astype(BF16)
            dv_ref[0] = dv_sc[...].astype(BF16)

    wq, wv = hb * QK_PAD, hb * V_DIM
    at_q = lambda b, h, p, qi, kj: (b, qi[p], h)
    at_k = lambda b, h, p, qi, kj: (b, kj[p], h)
    whole = lambda b, h, p, qi, kj: (b, 0, h)
    grid_spec = pltpu.PrefetchScalarGridSpec(
        num_scalar_prefetch=2, grid=(B, N_HEADS // hb, n_pairs),
        in_specs=[pl.BlockSpec((1, t, wq), at_q), pl.BlockSpec((1, t, wq), at_k),
                  pl.BlockSpec((1, t, wv), lambda b, h, p, qi, kj: (b, kj[p], v_blk0 + h)),
                  pl.BlockSpec((1, t, wv), at_q), pl.BlockSpec((1, t, wv), at_q), pl.BlockSpec((1, t, wv), at_q)],
        out_specs=[pl.BlockSpec((1, t, wq), at_q), pl.BlockSpec((1, S, wq), whole), pl.BlockSpec((1, S, wv), whole)],
        scratch_shapes=[pltpu.VMEM((t, wq), F32), pltpu.VMEM((S, wq), F32), pltpu.VMEM((S, wv), F32)])
    return pl.pallas_call(
        body, grid_spec=grid_spec,
        out_shape=[jax.ShapeDtypeStruct((B, S, N_HEADS * QK_PAD), BF16),
                   jax.ShapeDtypeStruct((B, S, N_HEADS * QK_PAD), BF16),
                   jax.ShapeDtypeStruct((B, S, N_HEADS * V_DIM), BF16)],
        name="attn_bwd", compiler_params=_cparams(("parallel", "parallel", "arbitrary")))(
            qi, kj, q, k, vsrc, o, do, lse)


@jax.custom_vjp
def attention(q, k, kv):
    return _attn_fwd_call(q, k, kv, N_HEADS // ATT_HB)[0]


def _attention_fwd(q, k, kv):
    o, lse = _attn_fwd_call(q, k, kv, N_HEADS // ATT_HB)
    return o, (q, k, kv, o, lse)


def _attention_bwd(res, do):
    q, k, kv, o, lse = res
    dq, dk, dv = _attn_bwd_call(q, k, kv, o, do, lse)
    dkv = jnp.concatenate([jnp.zeros_like(dv), dv], axis=-1)
    return dq, dk, dkv


attention.defvjp(_attention_fwd, _attention_bwd)


SUBLANES = 8


def _zero_tail(v):
    return jnp.concatenate([v, jnp.zeros((SUBLANES, v.shape[1]), v.dtype)], axis=0)


def _shift_down(vz, sh):
    return pltpu.roll(vz, sh, 0)[:vz.shape[0] - SUBLANES]


def _shift_up(vz, sh):
    return pltpu.roll(vz, vz.shape[0] - sh, 0)[:vz.shape[0] - SUBLANES]


def _conv_pre(u, uz, w_ref, b_ref):
    acc = b_ref[...] + w_ref[pl.ds(CONV_K - 1, 1), :] * u
    for k in range(CONV_K - 1):
        acc = acc + w_ref[pl.ds(k, 1), :] * _shift_down(uz, CONV_K - 1 - k)
    return acc


def _conv_fwd_call(src, w, b):
    B, S, _ = src.shape
    C = w.shape[1]

    def body(u_ref, w_ref, b_ref, o_ref):
        uu = u_ref[0]
        o_ref[0] = _silu(_conv_pre(uu, _zero_tail(uu), w_ref, b_ref))

    spec = pl.BlockSpec((1, S, LANE), lambda c, bb: (bb, 0, c))
    return pl.pallas_call(
        body, grid=(C // LANE, B),
        in_specs=[pl.BlockSpec((1, S, LANE), lambda c, bb: (bb, 0, c + CONV_LANE0 // LANE)),
                  pl.BlockSpec((CONV_K, LANE), lambda c, bb: (0, c)), pl.BlockSpec((1, LANE), lambda c, bb: (0, c))],
        out_specs=spec, out_shape=jax.ShapeDtypeStruct((B, S, C), F32), name="conv_fwd",
        compiler_params=_cparams(("parallel", "arbitrary")))(src, w, b)


def _conv_bwd_call(src, w, b, g):
    B, S, _ = src.shape
    C = w.shape[1]

    def body(u_ref, w_ref, b_ref, g_ref, du_ref, dw_ref, db_ref):
        uu = u_ref[0]
        uz = _zero_tail(uu)
        pre = _conv_pre(uu, uz, w_ref, b_ref)
        sg = lax.logistic(pre)
        dpre = g_ref[0] * sg * (1.0 + pre * (1.0 - sg))
        dz = _zero_tail(dpre)
        du = w_ref[pl.ds(CONV_K - 1, 1), :] * dpre
        dws = [None] * CONV_K
        dws[CONV_K - 1] = jnp.sum(dpre * uu, axis=0, keepdims=True)
        for k in range(CONV_K - 1):
            sh = CONV_K - 1 - k
            du = du + w_ref[pl.ds(k, 1), :] * _shift_up(dz, sh)
            dws[k] = jnp.sum(dpre * _shift_down(uz, sh), axis=0, keepdims=True)
        du_ref[0] = du.astype(du_ref.dtype)
        dbv = jnp.sum(dpre, axis=0, keepdims=True)
        first = pl.program_id(1) == 0

        @pl.when(first)
        def _():
            for k in range(CONV_K):
                dw_ref[pl.ds(k, 1), :] = dws[k]
            db_ref[...] = dbv

        @pl.when(jnp.logical_not(first))
        def _():
            for k in range(CONV_K):
                dw_ref[pl.ds(k, 1), :] += dws[k]
            db_ref[...] += dbv

    spec = pl.BlockSpec((1, S, LANE), lambda c, bb: (bb, 0, c))
    wspec = pl.BlockSpec((CONV_K, LANE), lambda c, bb: (0, c))
    bspec = pl.BlockSpec((1, LANE), lambda c, bb: (0, c))
    uspec = pl.BlockSpec((1, S, LANE), lambda c, bb: (bb, 0, c + CONV_LANE0 // LANE))
    return pl.pallas_call(
        body, grid=(C // LANE, B), in_specs=[uspec, wspec, bspec, spec], out_specs=[spec, wspec, bspec],
        out_shape=[jax.ShapeDtypeStruct((B, S, C), BF16), jax.ShapeDtypeStruct(w.shape, F32),
                   jax.ShapeDtypeStruct(b.shape, F32)],
        name="conv_bwd", compiler_params=_cparams(("parallel", "arbitrary")))(src, w, b, g)


@jax.custom_vjp
def conv_silu(src, stand_in, w, b):
    return _conv_fwd_call(src, w, b)


def _conv_silu_fwd(src, stand_in, w, b):
    return _conv_fwd_call(src, w, b), (src, w, b)


def _conv_silu_bwd(res, g):
    du, dw, db = _conv_bwd_call(*res, g)
    return jnp.zeros_like(res[0]), du, dw, db


conv_silu.defvjp(_conv_silu_fwd, _conv_silu_bwd)


def _chunk_cumsum_call(a, reverse, name):
    B, S, W = a.shape

    def body(a_ref, o_ref):
        r = lax.broadcasted_iota(jnp.int32, (CHUNK, CHUNK), 0)
        c = lax.broadcasted_iota(jnp.int32, (CHUNK, CHUNK), 1)
        tri = jnp.where((c >= r) if reverse else (c <= r), 1.0, 0.0).astype(F32)
        o_ref[0] = jnp.dot(tri, a_ref[0], preferred_element_type=F32, precision=lax.Precision.HIGHEST)

    spec = pl.BlockSpec((1, CHUNK, W), lambda b, c: (b, c, 0))
    return pl.pallas_call(body, grid=(B, S // CHUNK), in_specs=[spec], out_specs=spec,
                          out_shape=jax.ShapeDtypeStruct(a.shape, F32), name=name,
                          compiler_params=_cparams(("parallel", "parallel")))(a)


@jax.custom_vjp
def chunk_cumsum(a):
    return _chunk_cumsum_call(a, False, "chunk_cumsum_fwd")


chunk_cumsum.defvjp(lambda a: (_chunk_cumsum_call(a, False, "chunk_cumsum_fwd"), None),
                    lambda _, g: (_chunk_cumsum_call(g, True, "chunk_cumsum_bwd"),))


GROUP_W = 4 * HEAD_P
HPG = SSM_HEADS // SSM_GROUPS


def _ssd_masks():
    lane = lax.broadcasted_iota(jnp.int32, (1, GROUP_W), 1)
    return [((lane >= HEAD_P * j) & (lane < HEAD_P * (j + 1))).astype(F32) for j in range(HPG)]


def _ssd_decays(ac_cols, acr_ref, gi):
    r = lax.broadcasted_iota(jnp.int32, (CHUNK, CHUNK), 0)
    c = lax.broadcasted_iota(jnp.int32, (CHUNK, CHUNK), 1)
    return [jnp.exp(jnp.where(c <= r, ac_cols[j] - acr_ref[0, gi * HPG + j], NEG)) for j in range(HPG)]


def _ssd_cols(blk, g):
    lane = lax.broadcasted_iota(jnp.int32, blk.shape, 1)
    return [jnp.sum(jnp.where(lane == HPG * g + j, blk, 0.0), axis=1, keepdims=True) for j in range(HPG)]


def _ssd_spread(cols):
    lane = lax.broadcasted_iota(jnp.int32, (1, GROUP_W), 1)
    out = jnp.broadcast_to(cols[HPG - 1], (CHUNK, GROUP_W))
    for j in range(HPG - 2, -1, -1):
        out = jnp.where(lane < HEAD_P * (j + 1), cols[j], out)
    return out


def _ssd_gather(val, cols, masks, g):
    lane = lax.broadcasted_iota(jnp.int32, (1, LANE), 1)
    out = jnp.zeros((CHUNK, LANE), F32)
    for j in range(HPG):
        tot = jnp.sum(val * masks[j], axis=1, keepdims=True)
        if cols is not None:
            tot = tot + cols[j]
        out = out + tot * (lane == HPG * g + j).astype(F32)
    return out


def _dot(a, b, dims):
    return lax.dot_general(a.astype(BF16), b.astype(BF16), (dims, ((), ())), preferred_element_type=F32)


NN = ((1,), (0,))
NT = ((1,), (1,))
TN = ((0,), (0,))


XBC_W = GROUP_W + 2 * STATE_N


SSD_STEP_GROUPS_FWD = 4
SSD_STEP_GROUPS_BWD = 2


def _ssd_load(xbc_ref, dt_ref, ac_ref, masks, g, gi):
    x = xbc_ref[0, :, gi * XBC_W:gi * XBC_W + GROUP_W]
    bm = xbc_ref[0, :, gi * XBC_W + GROUP_W:gi * XBC_W + GROUP_W + STATE_N]
    cm = xbc_ref[0, :, gi * XBC_W + GROUP_W + STATE_N:(gi + 1) * XBC_W]
    ac_cols = _ssd_cols(ac_ref[0], g)
    dt = _ssd_spread(_ssd_cols(dt_ref[0], g))
    ac = _ssd_spread(ac_cols)
    is_last = (lax.broadcasted_iota(jnp.int32, (CHUNK, GROUP_W), 0) == CHUNK - 1).astype(F32)
    return x, bm, cm, dt, ac, ac_cols, is_last


def _ssd_in_specs(nc, rev, gb):
    cc = (lambda c: nc - 1 - c) if rev else (lambda c: c)
    return [pl.BlockSpec((1, CHUNK, gb * XBC_W), lambda b, g, c: (b, cc(c), g)),
            pl.BlockSpec((1, CHUNK, LANE), lambda b, g, c: (b, cc(c), 0)),
            pl.BlockSpec((1, CHUNK, LANE), lambda b, g, c: (b, cc(c), 0)),
            pl.BlockSpec((1, gb * HPG, 1, CHUNK), lambda b, g, c: (b, g, 0, cc(c))),
            pl.BlockSpec((1, gb * GROUP_W), lambda b, g, c: (0, g))]


def _ssd_fwd_call(xbc, dtp, acp, acr, dsk):
    B, S, _ = xbc.shape
    nc = S // CHUNK
    gb = SSD_STEP_GROUPS_FWD

    def body(xbc_ref, dt_ref, ac_ref, ar_ref, ds_ref, y_ref, hp_ref, h_sc):
        @pl.when(pl.program_id(2) == 0)
        def _():
            h_sc[...] = jnp.zeros(h_sc.shape, F32)

        masks = _ssd_masks()
        ys = []
        for gi in range(gb):
            grp = gb * pl.program_id(1) + gi
            x, bm, cm, dt, ac, ac_cols, is_last = _ssd_load(xbc_ref, dt_ref, ac_ref, masks, grp, gi)
            last = jnp.sum(ac * is_last, axis=0, keepdims=True)
            decays = _ssd_decays(ac_cols, ar_ref, gi)
            xd = x * dt
            cb = _dot(cm, bm, NT)
            hprev = h_sc[gi]
            hp_ref[0, gi, 0] = hprev
            y = _dot(cm, hprev, NN) * jnp.exp(ac) + ds_ref[:, gi * GROUP_W:(gi + 1) * GROUP_W] * x
            for j in range(HPG):
                y = y + _dot(cb * decays[j], xd * masks[j], NN)
            ys.append(y)
            h_sc[gi] = hprev * jnp.exp(last) + _dot(bm, xd * jnp.exp(last - ac), TN)
        y_ref[0] = jnp.concatenate(ys, axis=1)

    ng = SSM_GROUPS // gb
    return pl.pallas_call(
        body, grid=(B, ng, nc), in_specs=_ssd_in_specs(nc, False, gb),
        out_specs=[pl.BlockSpec((1, CHUNK, gb * GROUP_W), lambda b, g, c: (b, c, g)),
                   pl.BlockSpec((1, gb, 1, STATE_N, GROUP_W), lambda b, g, c: (b, g, c, 0, 0))],
        out_shape=[jax.ShapeDtypeStruct((B, S, D_INNER), F32),
                   jax.ShapeDtypeStruct((B, SSM_GROUPS, nc, STATE_N, GROUP_W), F32)],
        scratch_shapes=[pltpu.VMEM((gb, STATE_N, GROUP_W), F32)], name="ssd_fwd",
        compiler_params=_cparams(("parallel", "parallel", "arbitrary")))(xbc, dtp, acp, acr, dsk)


def _ssd_bwd_call(xbc, dtp, acp, acr, dsk, hps, dy):
    B, S, _ = xbc.shape
    nc = S // CHUNK
    gb = SSD_STEP_GROUPS_BWD

    def body(xbc_ref, dt_ref, ac_ref, ar_ref, ds_ref, hp_ref, dy_ref,
             dxbc_ref, ddt_ref, dac_ref, dar_ref, dds_ref, dh_sc):
        first = pl.program_id(2) == 0

        @pl.when(first)
        def _():
            dh_sc[...] = jnp.zeros(dh_sc.shape, F32)

        masks = _ssd_masks()
        dxbc_parts, dds_parts = [], []
        for gi in range(gb):
            grp = gb * pl.program_id(0) + gi
            x, bm, cm, dt, ac, ac_cols, is_last = _ssd_load(xbc_ref, dt_ref, ac_ref, masks, grp, gi)
            last = jnp.sum(ac * is_last, axis=0, keepdims=True)
            g = dy_ref[0, :, gi * GROUP_W:(gi + 1) * GROUP_W]
            hprev = hp_ref[0, gi, 0]
            dh = dh_sc[gi]
            decays = _ssd_decays(ac_cols, ar_ref, gi)
            dcols = []
            xd = x * dt
            cb = _dot(cm, bm, NT)
            e_c = jnp.exp(ac)
            e_end = jnp.exp(last - ac)
            e_last = jnp.exp(last)
            z = _dot(cm, hprev, NN)
            dz = g * e_c
            dac = g * z * e_c
            dc = _dot(dz, hprev, NT)
            dhprev = _dot(cm, dz, TN) + dh * e_last
            dcb = jnp.zeros((CHUNK, CHUNK), F32)
            dxd = jnp.zeros(xd.shape, F32)
            for j in range(HPG):
                gj = cb * decays[j]
                dgj = _dot(g * masks[j], xd, NT)
                dxd = dxd + _dot(gj, g, TN) * masks[j]
                dcb = dcb + dgj * decays[j]
                dseg = dgj * gj
                dcols.append(jnp.sum(dseg, axis=1, keepdims=True))
                dar_ref[0, gi * HPG + j] = -jnp.sum(dseg, axis=0, keepdims=True)
            dc = dc + _dot(dcb, bm, NN)
            db = _dot(dcb, cm, TN)
            sx = xd * e_end
            db = db + _dot(sx, dh, NT)
            dsx = _dot(bm, dh, NN)
            dxd = dxd + dsx * e_end
            de = dsx * sx
            dac = dac - de
            dlast = jnp.sum(de, axis=0, keepdims=True) + jnp.sum(dh * hprev, axis=0, keepdims=True) * e_last
            dsk = ds_ref[:, gi * GROUP_W:(gi + 1) * GROUP_W]
            dxbc_parts += [dxd * dt + dsk * g, db, dc]
            ddt_ref[0, gi] = _ssd_gather(dxd * x, None, masks, grp)
            dac_ref[0, gi] = _ssd_gather(dac + is_last * dlast, dcols, masks, grp)
            dds_parts.append(jnp.sum(g * x, axis=0, keepdims=True))
            dh_sc[gi] = dhprev
        dxbc_ref[0] = jnp.concatenate(dxbc_parts, axis=1)
        dds = jnp.concatenate(dds_parts, axis=1)
        first_all = first & (pl.program_id(1) == 0)

        @pl.when(first_all)
        def _():
            dds_ref[...] = dds

        @pl.when(jnp.logical_not(first_all))
        def _():
            dds_ref[...] += dds

    rc = lambda c: nc - 1 - c
    ng = SSM_GROUPS // gb
    in_specs = [pl.BlockSpec(s.block_shape, (lambda g, b, c, f=s.index_map: f(b, g, c))) for s in _ssd_in_specs(nc, True, gb)]
    in_specs.append(pl.BlockSpec((1, gb, 1, STATE_N, GROUP_W), lambda g, b, c: (b, g, rc(c), 0, 0)))
    in_specs.append(pl.BlockSpec((1, CHUNK, gb * GROUP_W), lambda g, b, c: (b, rc(c), g)))
    per_group = pl.BlockSpec((1, gb, CHUNK, LANE), lambda g, b, c: (b, g, rc(c), 0))
    out_specs = [pl.BlockSpec((1, CHUNK, gb * XBC_W), lambda g, b, c: (b, rc(c), g)), per_group, per_group,
                 pl.BlockSpec((1, gb * HPG, 1, CHUNK), lambda g, b, c: (b, g, 0, rc(c))),
                 pl.BlockSpec((1, gb * GROUP_W), lambda g, b, c: (0, g))]
    out_shape = [jax.ShapeDtypeStruct(xbc.shape, F32),
                 jax.ShapeDtypeStruct((B, SSM_GROUPS, S, LANE), F32), jax.ShapeDtypeStruct((B, SSM_GROUPS, S, LANE), F32),
                 jax.ShapeDtypeStruct(acr.shape, F32), jax.ShapeDtypeStruct(dsk.shape, F32)]
    return pl.pallas_call(
        body, grid=(ng, B, nc), in_specs=in_specs, out_specs=out_specs, out_shape=out_shape,
        scratch_shapes=[pltpu.VMEM((gb, STATE_N, GROUP_W), F32)], name="ssd_bwd",
        compiler_params=_cparams(("arbitrary", "arbitrary", "arbitrary")))(xbc, dtp, acp, acr, dsk, hps, dy)


@jax.custom_vjp
def ssd(xbc, dtp, acp, acr, dsk):
    return _ssd_fwd_call(xbc, dtp, acp, acr, dsk)[0]


def _ssd_fwd(xbc, dtp, acp, acr, dsk):
    y, hps = _ssd_fwd_call(xbc, dtp, acp, acr, dsk)
    return y, (xbc, dtp, acp, acr, dsk, hps)


def _ssd_bwd(res, dy):
    dxbc, ddt, dac, dacr, dds = _ssd_bwd_call(*res, dy)
    return dxbc, jnp.sum(ddt, axis=1), jnp.sum(dac, axis=1), dacr, dds


ssd.defvjp(_ssd_fwd, _ssd_bwd)


def _pack_small(arrs):
    flat = jnp.concatenate([a.reshape(-1) for a in arrs])
    rows = -(-flat.shape[0] // (8 * LANE)) * 8
    return jnp.pad(flat, (0, rows * LANE - flat.shape[0])).reshape(rows, LANE)


def _unpack_small(buf, shapes):
    flat = buf.reshape(-1)
    out, off = [], 0
    for shp in shapes:
        n = int(np.prod(shp))
        out.append(flat[off:off + n].reshape(shp))
        off += n
    return out


def _rows_tile(rows, cap):
    for cand in range(min(rows, cap), 7, -8):
        if rows % cand == 0:
            return cand
    return rows


def _pair_sum(mine, theirs, cidx, name):
    n4, kk, nn = mine.shape
    half = kk // 2
    tr = _rows_tile(half, 256)
    nb = half // tr

    def body(c_ref, a_ref, b_ref, o_ref, ob_ref):
        tot = a_ref[...] + b_ref[...]
        o_ref[...] = tot
        ob_ref[...] = tot.astype(BF16)

    spec = pl.BlockSpec((1, tr, nn), lambda j, i, c: (j, i, 0))
    grid_spec = pltpu.PrefetchScalarGridSpec(
        num_scalar_prefetch=1, grid=(n4, nb),
        in_specs=[pl.BlockSpec((1, tr, nn), lambda j, i, c: (j, c[0] * nb + i, 0)), spec], out_specs=[spec, spec])
    return pl.pallas_call(
        body, grid_spec=grid_spec,
        out_shape=[jax.ShapeDtypeStruct((n4, half, nn), F32), jax.ShapeDtypeStruct((n4, half, nn), BF16)],
        name=name, compiler_params=_cparams(("parallel", "parallel")))(cidx, mine, theirs)


def _chip_sum(quad, pair, chip_idx, name):
    _, rows, nn = quad.shape
    tr = _rows_tile(rows, 256)

    def body(s_ref, q_ref, p_ref, o_ref):
        for mine in range(4):
            @pl.when(s_ref[0] == mine)
            def _(mine=mine):
                acc = None
                for d in range(4):
                    term = p_ref[0] if d == mine else q_ref[d].astype(F32)
                    acc = term if acc is None else acc + term
                o_ref[...] = acc

    grid_spec = pltpu.PrefetchScalarGridSpec(
        num_scalar_prefetch=1, grid=(rows // tr,),
        in_specs=[pl.BlockSpec((4, tr, nn), lambda i, s: (0, i, 0)), pl.BlockSpec((1, tr, nn), lambda i, s: (s[0], i, 0))],
        out_specs=pl.BlockSpec((tr, nn), lambda i, s: (i, 0)))
    return pl.pallas_call(body, grid_spec=grid_spec, out_shape=jax.ShapeDtypeStruct((rows, nn), F32), name=name,
                          compiler_params=_cparams(("parallel",)))(chip_idx, quad, pair)


def _adam_halves_call(w, mine, other, cidx, m, v, name):
    rows, nn = w.shape
    half = rows // 2
    tr = _rows_tile(half, 128)
    nb = half // tr

    def body(c_ref, w_ref, a_ref, b_ref, m_ref, v_ref, g_ref, d_ref, nm_ref, nv_ref):
        upper = (pl.program_id(0) >= nb).astype(jnp.int32)
        g = jnp.where(upper == c_ref[0], a_ref[...], b_ref[...])
        g_ref[...] = g
        d_ref[...], nm_ref[...], nv_ref[...] = _adam_fn(w_ref[...], g, m_ref[...], v_ref[...])

    spec = pl.BlockSpec((tr, nn), lambda i, c: (i, 0))
    hspec = pl.BlockSpec((tr, nn), lambda i, c: (i % nb, 0))
    grid_spec = pltpu.PrefetchScalarGridSpec(num_scalar_prefetch=1, grid=(2 * nb,),
                                             in_specs=[spec, hspec, hspec, spec, spec], out_specs=[spec] * 4)
    return pl.pallas_call(body, grid_spec=grid_spec, out_shape=[jax.ShapeDtypeStruct((rows, nn), F32)] * 4, name=name,
                          compiler_params=_cparams(("parallel",)))(cidx, w, mine, other, m, v)


def _stack_sum(stack, name):
    n, rows, nn = stack.shape
    tr = _rows_tile(rows, 256)

    def body(s_ref, o_ref):
        acc = s_ref[0]
        for d in range(1, n):
            acc = acc + s_ref[d]
        o_ref[...] = acc

    return pl.pallas_call(
        body, grid=(rows // tr,), in_specs=[pl.BlockSpec((n, tr, nn), lambda i: (0, i, 0))],
        out_specs=pl.BlockSpec((tr, nn), lambda i: (i, 0)), out_shape=jax.ShapeDtypeStruct((rows, nn), F32),
        name=name, compiler_params=_cparams(("parallel",)))(stack)


def _adam_call(w, g, m, v, name):
    rows, nn = w.shape
    tr = _rows_tile(rows, 128)

    def body(w_ref, g_ref, m_ref, v_ref, d_ref, nm_ref, nv_ref):
        d_ref[...], nm_ref[...], nv_ref[...] = _adam_fn(w_ref[...], g_ref[...], m_ref[...], v_ref[...])

    spec = pl.BlockSpec((tr, nn), lambda i: (i, 0))
    sds = jax.ShapeDtypeStruct((rows, nn), F32)
    return pl.pallas_call(body, grid=(rows // tr,), in_specs=[spec] * 4, out_specs=[spec] * 3,
                          out_shape=[sds] * 3, name=name, compiler_params=_cparams(("parallel",)))(w, g, m, v)


def _adam_fn(w, g, m, v):
    m = ADAM_B1 * m + (1.0 - ADAM_B1) * g
    v = ADAM_B2 * v + (1.0 - ADAM_B2) * (g * g)
    m_hat = m / (1.0 - ADAM_B1 ** ADAM_STEP)
    v_hat = v / (1.0 - ADAM_B2 ** ADAM_STEP)
    delta = -ADAM_LR * (m_hat / (jnp.sqrt(v_hat) + ADAM_EPS) + ADAM_WD * w)
    return delta, m, v


def _mesh_pos():
    return lax.axis_index("x"), lax.axis_index("y"), lax.axis_index("c")


def _other_chips(x, y):
    return [(1 - x, y), (x, 1 - y), (1 - x, 1 - y)]


HBM_SPEC = pl.BlockSpec(memory_space=pl.ANY)


def _remote(src, dst, send_sems, recv_sems, k, to):
    return pltpu.make_async_remote_copy(src_ref=src, dst_ref=dst, send_sem=send_sems.at[k], recv_sem=recv_sems.at[k],
                                        device_id=to, device_id_type=MESH)


def _half_rows(c, rows, align):
    half = rows // 2
    return (pl.ds(pl.multiple_of(c * half, align), half), pl.ds(pl.multiple_of((1 - c) * half, align), half))


def _gather_weights(mats, conv):
    n = len(mats)

    def body(*refs):
        ins, conv_in = refs[:n], refs[n]
        outs, conv_out = refs[n + 1:2 * n + 1], refs[2 * n + 1]
        send_sems, recv_sems, local_sem = refs[2 * n + 2:]
        x, y, c = _mesh_pos()
        me, sibling, s = (x, y, c), (x, y, 1 - c), 2 * x + y
        chips = _other_chips(x, y)
        rows = [_half_rows(c, m.shape[0], 16) for m in mats]
        own = pltpu.make_async_copy(conv_in, conv_out.at[s], local_sem)
        own.start()
        sent = []
        for i in range(n):
            mine = rows[i][0]
            for j, (cx, cy) in enumerate(chips):
                sent.append(_remote(ins[i].at[mine], outs[i].at[s, mine], send_sems, recv_sems, 6 * i + j, (cx, cy, c)))
        for j, (cx, cy) in enumerate(chips):
            sent.append(_remote(conv_in, conv_out.at[s], send_sems, recv_sems, 6 * n + j, (cx, cy, c)))
        for cp in sent:
            cp.start()
        for i in range(n):
            mine = rows[i][0]
            for j, (cx, cy) in enumerate(chips):
                landed = outs[i].at[2 * cx + cy, mine]
                _remote(landed, landed, send_sems, recv_sems, 6 * i + j, me).wait_recv()
                fwd = _remote(landed, landed, send_sems, recv_sems, 6 * i + 3 + j, sibling)
                fwd.start()
                sent.append(fwd)
        for j, (cx, cy) in enumerate(chips):
            slot = conv_out.at[2 * cx + cy]
            _remote(slot, slot, send_sems, recv_sems, 6 * n + j, me).wait_recv()
        for i in range(n):
            theirs_rows = rows[i][1]
            for j, (cx, cy) in enumerate(chips):
                theirs = outs[i].at[2 * cx + cy, theirs_rows]
                _remote(theirs, theirs, send_sems, recv_sems, 6 * i + 3 + j, me).wait_recv()
        for cp in sent:
            cp.wait_send()
        own.wait()

    out_shape = [jax.ShapeDtypeStruct((4,) + m.shape, m.dtype) for m in mats]
    out_shape.append(jax.ShapeDtypeStruct((4,) + conv.shape, conv.dtype))
    res = pl.pallas_call(
        body, in_specs=[HBM_SPEC] * (n + 1), out_specs=[HBM_SPEC] * (n + 1), out_shape=out_shape,
        scratch_shapes=[pltpu.SemaphoreType.DMA((6 * n + 3,)), pltpu.SemaphoreType.DMA((6 * n + 3,)),
                        pltpu.SemaphoreType.DMA],
        name="all_gather_weights")(*mats, conv)
    chip = 2 * lax.axis_index("x") + lax.axis_index("y")
    full = [lax.dynamic_update_slice_in_dim(r, m[None], chip, axis=0) for r, m in zip(res[:n], mats)]
    return full, res[n]


def _sibling_exchange(stacks):
    n = len(stacks)

    def body(*refs):
        ins, outs = refs[:n], refs[n:2 * n]
        send_sems, recv_sems = refs[2 * n:]
        x, y, c = _mesh_pos()
        cps = []
        for i in range(n):
            theirs = _half_rows(c, stacks[i].shape[1], 8)[1]
            cps.append(_remote(ins[i].at[:, theirs, :], outs[i], send_sems, recv_sems, i, (x, y, 1 - c)))
        for cp in cps:
            cp.start()
        for cp in cps:
            cp.wait()

    out_shape = [jax.ShapeDtypeStruct((4, s.shape[1] // 2, s.shape[2]), s.dtype) for s in stacks]
    return pl.pallas_call(
        body, in_specs=[HBM_SPEC] * n, out_specs=[HBM_SPEC] * n, out_shape=out_shape,
        scratch_shapes=[pltpu.SemaphoreType.DMA((n,)), pltpu.SemaphoreType.DMA((n,))],
        name="grad_sibling_exchange")(*stacks)


def _chip_exchange(parts):
    n = len(parts)

    def body(*refs):
        ins, outs = refs[:n], refs[n:2 * n]
        send_sems, recv_sems = refs[2 * n:]
        x, y, c = _mesh_pos()
        me, s = (x, y, c), 2 * x + y
        chips = _other_chips(x, y)
        sent = [_remote(ins[i].at[2 * cx + cy], outs[i].at[s], send_sems, recv_sems, 3 * i + j, (cx, cy, c))
                for i in range(n) for j, (cx, cy) in enumerate(chips)]
        for cp in sent:
            cp.start()
        for i in range(n):
            for j, (cx, cy) in enumerate(chips):
                slot = outs[i].at[2 * cx + cy]
                _remote(slot, slot, send_sems, recv_sems, 3 * i + j, me).wait_recv()
        for cp in sent:
            cp.wait_send()

    return pl.pallas_call(
        body, in_specs=[HBM_SPEC] * n, out_specs=[HBM_SPEC] * n,
        out_shape=[jax.ShapeDtypeStruct(p.shape, p.dtype) for p in parts],
        scratch_shapes=[pltpu.SemaphoreType.DMA((3 * n,)), pltpu.SemaphoreType.DMA((3 * n,))],
        name="grad_chip_exchange")(*parts)


def _sibling_swap(halves):
    n = len(halves)

    def body(*refs):
        ins, outs = refs[:n], refs[n:2 * n]
        send_sems, recv_sems = refs[2 * n:]
        x, y, c = _mesh_pos()
        cps = [_remote(ins[i], outs[i], send_sems, recv_sems, i, (x, y, 1 - c)) for i in range(n)]
        for cp in cps:
            cp.start()
        for cp in cps:
            cp.wait()

    return pl.pallas_call(
        body, in_specs=[HBM_SPEC] * n, out_specs=[HBM_SPEC] * n,
        out_shape=[jax.ShapeDtypeStruct(h.shape, h.dtype) for h in halves],
        scratch_shapes=[pltpu.SemaphoreType.DMA((n,)), pltpu.SemaphoreType.DMA((n,))],
        name="grad_sibling_swap")(*halves)


def _gather_small(vec):
    def body(in_ref, out_ref, send_sems, recv_sems, local_sem):
        x, y, c = _mesh_pos()
        me = (x, y, c)
        own = pltpu.make_async_copy(in_ref, out_ref.at[4 * x + 2 * y + c], local_sem)
        own.start()
        peers = [(1 - x if k & 4 else x, 1 - y if k & 2 else y, 1 - c if k & 1 else c) for k in range(1, 8)]
        sent = [_remote(in_ref, out_ref.at[4 * x + 2 * y + c], send_sems, recv_sems, k, p) for k, p in enumerate(peers)]
        for cp in sent:
            cp.start()
        for k, (px, py, pc) in enumerate(peers):
            slot = out_ref.at[4 * px + 2 * py + pc]
            _remote(slot, slot, send_sems, recv_sems, k, me).wait_recv()
        for cp in sent:
            cp.wait_send()
        own.wait()

    return pl.pallas_call(
        body, in_specs=[HBM_SPEC], out_specs=HBM_SPEC, out_shape=jax.ShapeDtypeStruct((8,) + vec.shape, vec.dtype),
        scratch_shapes=[pltpu.SemaphoreType.DMA((7,)), pltpu.SemaphoreType.DMA((7,)), pltpu.SemaphoreType.DMA],
        name="grad_gather_small")(vec)


def _reduce_matrices(stacks, names):
    cidx = lax.axis_index("c").astype(jnp.int32).reshape(1)
    chip = (2 * lax.axis_index("x") + lax.axis_index("y")).astype(jnp.int32).reshape(1)
    got = _sibling_exchange(stacks)
    pairs = [_pair_sum(a, b, cidx, "grad_pair_sum_" + nm) for a, b, nm in zip(stacks, got, names)]
    quads = _chip_exchange([p[1] for p in pairs])
    mine = [_chip_sum(q, p[0], chip, "grad_chip_sum_" + nm) for q, p, nm in zip(quads, pairs, names)]
    return mine, _sibling_swap(mine)


def _pad_cols(a, n):
    return jnp.concatenate([a, jnp.zeros((a.shape[0], n - a.shape[1]), a.dtype)], axis=1)


def _group_channels(a):
    lead = a.shape[:-1]
    xs = a[..., :D_INNER].reshape(lead + (SSM_GROUPS, GROUP_W))
    bs = a[..., D_INNER:D_INNER + SSM_GROUPS * STATE_N].reshape(lead + (SSM_GROUPS, STATE_N))
    cs = a[..., D_INNER + SSM_GROUPS * STATE_N:].reshape(lead + (SSM_GROUPS, STATE_N))
    return jnp.concatenate([xs, bs, cs], axis=-1).reshape(lead + (CONV_CH,))


PROJ_SEGS = (('gate_a', D_MODEL), ('gate_b', D_MODEL), ('z', D_INNER), ('xbc', CONV_CH), ('q_lat', Q_RANK),
             ('kv_lat', KV_RANK), ('k_rope', LANE), ('dt', LANE))
PROJ_LANE0 = dict(zip([n for n, _ in PROJ_SEGS], [int(v) for v in np.cumsum([0] + [w for _, w in PROJ_SEGS])[:-1]]))
CONV_LANE0 = PROJ_LANE0['xbc']
KR_LANE0 = PROJ_LANE0['k_rope']


def _lay_w_in(w):
    idx = np.cumsum(IN_SIZES)[:-1]
    q_lat, kv_lat, k_rope, z, xbc, dt, gate_a, gate_b = jnp.split(w, [int(v) for v in idx], axis=1)
    return jnp.concatenate([gate_a, gate_b, z, _group_channels(xbc), q_lat, kv_lat, _pad_cols(k_rope, LANE),
                            _pad_cols(dt, LANE)], axis=1)


@jax.custom_vjp
def project(h, w, tok):
    return _project_impl(h, w)


def _project_impl(h, w):
    return (_mm(h, w, "w_in_fwd"),) + tuple(jnp.zeros((h.shape[0], wd), BF16) for _, wd in PROJ_SEGS)


def _project_fwd(h, w, tok):
    return _project_impl(h, w), (h, w)


def _project_bwd(res, cots):
    h, w = res
    g = jnp.concatenate(cots[1:], axis=1)
    return _mm(g, w.T, "w_in_dx", h.dtype), jnp.zeros_like(w), _mm_tn(h, g, "w_in_dw")


project.defvjp(_project_fwd, _project_bwd)


def _lay_w_uq(w):
    w3 = w.reshape(Q_RANK, N_HEADS, NOPE + ROPE)
    w3 = jnp.concatenate([w3, jnp.zeros((Q_RANK, N_HEADS, QK_PAD - NOPE - ROPE), w.dtype)], axis=2)
    return w3.reshape(Q_RANK, N_HEADS * QK_PAD)


def _lay_w_ukv(w):
    w3 = w.reshape(KV_RANK, N_HEADS, NOPE + V_DIM)
    return jnp.concatenate([w3[:, :, :NOPE].reshape(KV_RANK, -1), w3[:, :, NOPE:].reshape(KV_RANK, -1)], axis=1)


def _pad_lanes(v, n=LANE):
    return jnp.concatenate([v, jnp.zeros((v.shape[0], n - v.shape[1]), v.dtype)], axis=1)


def _local_loss(toks, small, x, wb, c8, posf, target):
    B, S, D = x.shape
    T = B * S

    def lin(name, a, key, lay=lambda w: w, out_dtype=F32):
        return make_linear(name, out_dtype)(a, lay(wb[key]), lay(toks[key]))

    rows2 = lambda a: a.reshape(T, a.shape[-1])
    rows3 = lambda a: a.reshape(B, S, a.shape[-1])

    sc = make_rowwise("silu_c", _f_silu, 1, 0, 0, ('row',))((c8[None],), (), ())[0][0]
    mod = lin("ada", sc, 'w_ada')[:B] + small['b_ada']
    shift1, scale1, gate1, shift2, scale2, gate2 = [m[:, None, :] for m in jnp.split(mod, 6, axis=-1)]

    modulate = make_rowwise("modulate1", _f_modulate, 1, 2, 1, ('row',))
    h = modulate((x,), (scale1, shift1), (small['g_pre_mix'],))[0]
    outs = project(rows2(h), _lay_w_in(wb['w_in']), _lay_w_in(toks['w_in']))
    proj = lax.stop_gradient(rows3(outs[0]))
    stand = {n: rows3(o) for (n, _), o in zip(PROJ_SEGS, outs[1:])}

    def win(seg, block):
        return (PROJ_LANE0[seg] // block, dict(PROJ_SEGS)[seg])

    inv = ROPE_THETA ** (-jnp.arange(ROPE // 2, dtype=F32) / (ROPE // 2))
    inv_lane = jnp.concatenate([inv, inv, jnp.zeros((LANE - ROPE,), F32)])[None]
    tabs = tuple(_rope_tables(posf, inv_lane))
    qn = make_rowwise("rms_q", _f_rms, 1, 0, 1, ('row',), windows={0: win('q_lat', Q_RANK)})(
        (proj,), (), (small['g_q_lat'],), (stand['q_lat'],))[0]
    kvn = make_rowwise("rms_kv", _f_rms, 1, 0, 1, ('row',), windows={0: win('kv_lat', KV_RANK)})(
        (proj,), (), (small['g_kv_lat'],), (stand['kv_lat'],))[0]
    qp = rows3(lin("w_uq", rows2(qn), 'w_uq', _lay_w_uq))
    kvp = rows3(lin("w_ukv", rows2(kvn), 'w_ukv', _lay_w_ukv, BF16))
    qr = rope_q(qp, tabs)
    kr = build_k(kvp, proj, stand['k_rope'], tabs)
    att = attention(qr, kr, kvp)
    attn = rows3(lin("w_o_attn", rows2(att), 'w_o_attn'))

    xa = conv_silu(proj, stand['xbc'], _group_channels(wb['conv_w_f32']), _group_channels(small['conv_b']))
    dt_pad, a_pad = make_rowwise("dt_softplus", _f_dt, 1, 0, 2, ('row', 'row'), windows={0: win('dt', LANE)})(
        (proj,), (), (_pad_lanes(small['dt_bias']), _pad_lanes(small['a_log'])), (stand['dt'],))
    ac_pad = chunk_cumsum(a_pad)
    acr = jnp.transpose(ac_pad[..., :SSM_HEADS], (0, 2, 1))[:, :, None, :]
    dsk = jnp.repeat(small['d_skip'], HEAD_P, axis=-1)
    y = ssd(xa, dt_pad, ac_pad, acr, dsk)
    yg = make_rowwise("gated_norm", _f_gated_norm, 2, 0, 1, ('row',), ncol=SSM_GROUPS, ts_cap=2048,
                      windows={1: win('z', GROUP_W)})((y, proj), (), (small['g_ssm_out'],), (stand['z'],))[0]
    ssm = rows3(lin("w_o_ssm", rows2(yg), 'w_o_ssm'))

    merged = make_rowwise("merge", _f_merge, 4, 0, 0, ('row',),
                          windows={2: win('gate_a', D_MODEL), 3: win('gate_b', D_MODEL)})(
        (attn, ssm, proj, proj), (), (), (stand['gate_a'], stand['gate_b']))[0]
    mix = rows3(lin("w_out", rows2(merged), 'w_out'))
    x1 = make_rowwise("post_mix", _f_post, 2, 1, 1, ('row',))((x, mix), (gate1,), (small['g_post_mix'],))[0]

    h2 = make_rowwise("modulate2", _f_modulate, 1, 2, 1, ('row',))((x1,), (scale2, shift2), (small['g_pre_mlp'],))[0]
    ff = rows3(ffn(rows2(h2), wb['w_ff1'], toks['w_ff1'], wb['w_ff2'], toks['w_ff2']))
    lvec = make_rowwise("final_loss", _f_final_loss, 3, 1, 1, ('sum',), nodiff=(2,))(
        (x1, ff, target), (gate2,), (small['g_post_mlp'],))[0]
    return jnp.sum(lvec)


MATRICES = COL_SHARDED + ROW_SHARDED


def _local_step(x, c, positions, target, wb, small):
    B = x.shape[0]
    c8 = jnp.concatenate([c, jnp.zeros((16 - B, c.shape[1]), F32)], axis=0)
    posf = positions.astype(F32)[..., None]
    toks = {k: jnp.zeros(wb[k].shape, F32) for k in MATRICES if k != 'conv_w'}
    conv_w = wb['conv_w_f32']

    def loss_fn(toks, small, conv_w, x):
        wbl = dict(wb)
        wbl['conv_w_f32'] = conv_w
        return _local_loss(toks, small, x, wbl, c8, posf, target)

    loss, (g_tok, g_small, g_conv, g_x) = jax.value_and_grad(loss_fn, argnums=(0, 1, 2, 3))(toks, small, conv_w, x)
    grads = dict(g_tok)
    grads.update(g_small)
    grads['conv_w'] = g_conv
    return loss, g_x, grads


def kernel(x, c, positions, w_ada, b_ada, g_pre_mix, g_post_mix, w_in, g_q_lat, g_kv_lat, w_uq, w_ukv, w_o_attn, conv_w, conv_b, dt_bias, a_log, d_skip, g_ssm_out, w_o_ssm, w_out, g_pre_mlp, g_post_mlp, w_ff1, w_ff2, loss_target, m_w_ada, m_b_ada, m_g_pre_mix, m_g_post_mix, m_w_in, m_g_q_lat, m_g_kv_lat, m_w_uq, m_w_ukv, m_w_o_attn, m_conv_w, m_conv_b, m_dt_bias, m_a_log, m_d_skip, m_g_ssm_out, m_w_o_ssm, m_w_out, m_g_pre_mlp, m_g_post_mlp, m_w_ff1, m_w_ff2, v_w_ada, v_b_ada, v_g_pre_mix, v_g_post_mix, v_w_in, v_g_q_lat, v_g_kv_lat, v_w_uq, v_w_ukv, v_w_o_attn, v_conv_w, v_conv_b, v_dt_bias, v_a_log, v_d_skip, v_g_ssm_out, v_w_o_ssm, v_w_out, v_g_pre_mlp, v_g_post_mlp, v_w_ff1, v_w_ff2):
    given = dict(locals())
    w_loc = {n: given[n] for n in WEIGHTS}
    m_loc = {n: given["m_" + n] for n in WEIGHTS}
    v_loc = {n: given["v_" + n] for n in WEIGHTS}
    mats = [n for n in WEIGHTS if n in MATRICES and n != 'conv_w']
    vecs = [n for n in WEIGHTS if n not in MATRICES]

    g_mats, g_conv = _gather_weights([w_loc[n][0].astype(BF16) for n in mats], conv_w[0])
    wb = {}
    for n, g in zip(mats, g_mats):
        if n in COL_SHARDED:
            wb[n] = jnp.transpose(g, (1, 0, 2)).reshape(g.shape[1], -1)
        else:
            wb[n] = g.reshape(-1, g.shape[2])
    wb['conv_w_f32'] = jnp.transpose(g_conv, (1, 0, 2)).reshape(CONV_K, -1)
    small = {n: w_loc[n] for n in vecs}

    loss_part, grad_x, grads = _local_step(x, c, positions, loss_target, wb, small)
    loss = lax.psum(loss_part, ("x", "y", "c"))

    stacks = []
    for n in mats:
        kk, nn = w_loc[n].shape[1:]
        if n in COL_SHARDED:
            stacks.append(jnp.transpose(grads[n].reshape(kk, 4, nn), (1, 0, 2)))
        else:
            stacks.append(grads[n].reshape(4, kk, nn))
    g_mine, g_other = _reduce_matrices(stacks, mats)
    g_shard = {}

    vec_shapes = [tuple(grads[n].shape) for n in vecs] + [tuple(grads['conv_w'].shape)]
    total = _stack_sum(_gather_small(_pack_small([grads[n] for n in vecs] + [grads['conv_w']])), "grad_sum_small")
    g_vec = _unpack_small(total, vec_shapes)
    n_conv = conv_w.shape[2]
    chip = 2 * lax.axis_index("x") + lax.axis_index("y")
    g_shard['conv_w'] = lax.dynamic_slice_in_dim(g_vec[-1], chip * n_conv, n_conv, axis=1)
    for n, g in zip(vecs, g_vec):
        g_shard[n] = g

    delta, new_m, new_v = {}, {}, {}
    cidx = lax.axis_index("c").astype(jnp.int32).reshape(1)
    for n, mine, other in zip(mats, g_mine, g_other):
        g_shard[n], delta[n], new_m[n], new_v[n] = _adam_halves_call(
            w_loc[n][0], mine, other, cidx, m_loc[n][0], v_loc[n][0], "adamw_" + n)
    rest = vecs + ['conv_w']
    rest_shapes = [tuple(w_loc[n].shape) for n in rest]
    packed = [_pack_small([src[n] for n in rest]) for src in (w_loc, g_shard, m_loc, v_loc)]
    for dst, buf in zip((delta, new_m, new_v), _adam_call(*packed, "adamw_small")):
        dst.update(zip(rest, _unpack_small(buf, rest_shapes)))

    def out(d):
        return [d[n].reshape(w_loc[n].shape) for n in WEIGHTS]

    return (loss, grad_x, *out(g_shard), *out(delta), *out(new_m), *out(new_v))
```

```python
import functools
import math

import numpy as np
import jax
import jax.numpy as jnp
from jax import lax
from jax.experimental import pallas as pl
from jax.experimental.pallas import tpu as pltpu

F32 = jnp.float32
BF16 = jnp.bfloat16
MESH = pl.DeviceIdType.MESH

D_MODEL = 1024
N_HEADS = 8
NOPE = 128
ROPE = 64
V_DIM = 128
Q_RANK = 256
KV_RANK = 256
ROPE_THETA = 10000.0
D_INNER = 2048
SSM_HEADS = 32
SSM_GROUPS = 8
HEAD_P = 64
STATE_N = 128
CONV_K = 4
CHUNK = 128
CONV_CH = D_INNER + 2 * SSM_GROUPS * STATE_N
D_FF = 4096
EPS = 1e-6
IN_SIZES = (Q_RANK, KV_RANK, ROPE, D_INNER, CONV_CH, SSM_HEADS, D_MODEL, D_MODEL)
ADAM_LR, ADAM_B1, ADAM_B2, ADAM_EPS, ADAM_WD, ADAM_STEP = 0.001, 0.9, 0.999, 1e-08, 0.01, 10

VMEM_LIMIT_BYTES = 52 * 1024 * 1024
LANE = 128
QK_PAD = 256

WEIGHTS = ['w_ada', 'b_ada', 'g_pre_mix', 'g_post_mix', 'w_in', 'g_q_lat', 'g_kv_lat', 'w_uq', 'w_ukv',
           'w_o_attn', 'conv_w', 'conv_b', 'dt_bias', 'a_log', 'd_skip', 'g_ssm_out', 'w_o_ssm', 'w_out',
           'g_pre_mlp', 'g_post_mlp', 'w_ff1', 'w_ff2']
COL_SHARDED = ('w_ada', 'w_in', 'w_uq', 'w_ukv', 'conv_w', 'w_ff1')
ROW_SHARDED = ('w_o_attn', 'w_o_ssm', 'w_out', 'w_ff2')


def _cparams(sem):
    return pltpu.CompilerParams(dimension_semantics=sem, vmem_limit_bytes=VMEM_LIMIT_BYTES)


def _tile(n, cap):
    if n <= cap:
        return n
    k = n // LANE
    best = LANE
    for d in range(1, k + 1):
        if k % d == 0 and d * LANE <= cap:
            best = d * LANE
    return best


def _mm(a, w, name, out_dtype=F32, epilogue=None, extras=(), out_dtypes=None):
    M, K = a.shape
    N = w.shape[1]
    tm = min(M, 1024)
    tn = _tile(N, 1024)
    tk = _tile(K, 2048)
    nk = K // tk
    dts = tuple(out_dtypes) if epilogue is not None else (out_dtype,)
    n_x, n_o = len(extras), len(dts)

    def finish(acc, refs):
        res = epilogue(acc, *[r[...] for r in refs[:n_x]]) if epilogue is not None else (acc,)
        for o_ref, val, dt in zip(refs[n_x:n_x + n_o], res, dts):
            o_ref[...] = val.astype(dt)

    def body(a_ref, w_ref, *refs):
        part = jnp.dot(a_ref[...].astype(BF16), w_ref[...], preferred_element_type=F32)
        if nk == 1:
            finish(part, refs)
        else:
            acc_ref = refs[-1]
            k = pl.program_id(2)

            @pl.when(k == 0)
            def _():
                acc_ref[...] = part

            @pl.when(k > 0)
            def _():
                acc_ref[...] += part

            @pl.when(k == nk - 1)
            def _():
                finish(acc_ref[...], refs)

    ospec = pl.BlockSpec((tm, tn), lambda i, j, k: (i, j))
    res = pl.pallas_call(
        body, grid=(M // tm, N // tn, nk),
        in_specs=[pl.BlockSpec((tm, tk), lambda i, j, k: (i, k)), pl.BlockSpec((tk, tn), lambda i, j, k: (k, j))]
        + [ospec] * n_x,
        out_specs=[ospec] * n_o, out_shape=[jax.ShapeDtypeStruct((M, N), dt) for dt in dts],
        scratch_shapes=[pltpu.VMEM((tm, tn), F32)] if nk > 1 else [], name=name,
        compiler_params=_cparams(("parallel", "parallel", "arbitrary")))(a, w, *extras)
    return res if epilogue is not None else res[0]


def _mm_tn(a, g, name):
    M, K = a.shape
    N = g.shape[1]
    tm = min(M, 1024)
    tk = _tile(K, 1024)
    tn = _tile(N, 1024)
    nm = M // tm

    def body(a_ref, g_ref, o_ref):
        part = lax.dot_general(a_ref[...].astype(BF16), g_ref[...].astype(BF16), (((0,), (0,)), ((), ())),
                               preferred_element_type=F32)
        m = pl.program_id(2)

        @pl.when(m == 0)
        def _():
            o_ref[...] = part

        @pl.when(m > 0)
        def _():
            o_ref[...] += part

    return pl.pallas_call(
        body, grid=(K // tk, N // tn, nm),
        in_specs=[pl.BlockSpec((tm, tk), lambda i, j, m: (m, i)), pl.BlockSpec((tm, tn), lambda i, j, m: (m, j))],
        out_specs=pl.BlockSpec((tk, tn), lambda i, j, m: (i, j)),
        out_shape=jax.ShapeDtypeStruct((K, N), F32), name=name,
        compiler_params=_cparams(("parallel", "parallel", "arbitrary")))(a, g)


def make_linear(name, out_dtype=F32):
    @jax.custom_vjp
    def linear(a, w, tok):
        return _mm(a, w, name + "_fwd", out_dtype)

    def fwd(a, w, tok):
        return _mm(a, w, name + "_fwd", out_dtype), (a, w)

    def bwd(res, g):
        a, w = res
        da = _mm(g, w.T, name + "_dx", a.dtype)
        dw = _mm_tn(a, g, name + "_dw")
        return da, jnp.zeros_like(w), dw

    linear.defvjp(fwd, bwd)
    return linear


def _relu2_epilogue(acc):
    r = jnp.maximum(acc, 0.0)
    return r * r, r


def _relu2_bwd_epilogue(acc, r):
    return (acc * (2.0 * r.astype(F32)),)


@jax.custom_vjp
def ffn(h, w1, tok1, w2, tok2):
    act, _ = _mm(h, w1, "w_ff1_fwd", epilogue=_relu2_epilogue, out_dtypes=(BF16, BF16))
    return _mm(act, w2, "w_ff2_fwd")


def _ffn_fwd(h, w1, tok1, w2, tok2):
    act, r = _mm(h, w1, "w_ff1_fwd", epilogue=_relu2_epilogue, out_dtypes=(BF16, BF16))
    return _mm(act, w2, "w_ff2_fwd"), (h, w1, w2, act, r)


def _ffn_bwd(res, g):
    h, w1, w2, act, r = res
    du = _mm(g, w2.T, "w_ff2_dx", epilogue=_relu2_bwd_epilogue, extras=(r,), out_dtypes=(BF16,))[0]
    dw2 = _mm_tn(act, g, "w_ff2_dw")
    dw1 = _mm_tn(h, du, "w_ff1_dw")
    dh = _mm(du, w1.T, "w_ff1_dx", h.dtype)
    return dh, jnp.zeros_like(w1), dw1, jnp.zeros_like(w2), dw2


ffn.defvjp(_ffn_fwd, _ffn_bwd)


def make_rowwise(name, f, n_rows, n_seqs, n_pars, out_kinds, ncol=1, nodiff=(), ts_cap=512, windows=None):
    windows = dict(windows or {})
    n_in = n_rows + n_seqs + n_pars
    diff_idx = [i for i in range(n_in) if i not in nodiff]

    def _dims(rows):
        B, S = rows[0].shape[0], rows[0].shape[1]
        ts = min(S, ts_cap)
        return B, S, ts

    def _width(i, r):
        return windows[i][1] if i in windows else r.shape[2]

    def _in_specs(rows, seqs, pars, ts):
        specs = []
        for i, r in enumerate(rows):
            col0 = windows[i][0] if i in windows else 0
            specs.append(pl.BlockSpec((1, ts, _width(i, r) // ncol), lambda k, b, s, col0=col0: (b, s, k + col0)))
        for q in seqs:
            specs.append(pl.BlockSpec((1, 1, q.shape[2] // ncol), lambda k, b, s: (b, 0, k)))
        for p in pars:
            specs.append(pl.BlockSpec((1, p.shape[1] // ncol), lambda k, b, s: (0, k)))
        return specs

    def _load(refs):
        vals = [r[0] for r in refs[:n_rows + n_seqs]]
        vals += [r[...] for r in refs[n_rows + n_seqs:n_in]]
        return vals

    def _out_struct(rows, seqs, pars, ts):
        blocks = [jax.ShapeDtypeStruct((ts, _width(i, r) // ncol), r.dtype) for i, r in enumerate(rows)]
        blocks += [jax.ShapeDtypeStruct((1, q.shape[2] // ncol), q.dtype) for q in seqs]
        blocks += [jax.ShapeDtypeStruct((1, p.shape[1] // ncol), p.dtype) for p in pars]
        return jax.eval_shape(f, *blocks)

    def _fwd_call(rows, seqs, pars):
        B, S, ts = _dims(rows)
        outs = _out_struct(rows, seqs, pars, ts)
        n_out = len(outs)

        def body(*refs):
            res = f(*_load(refs))
            first = (pl.program_id(1) == 0) & (pl.program_id(2) == 0)
            for o_ref, val, kind in zip(refs[n_in:], res, out_kinds):
                if kind == 'row':
                    o_ref[0] = val
                else:
                    tot = jnp.sum(val, axis=0, keepdims=True)

                    @pl.when(first)
                    def _(o_ref=o_ref, tot=tot):
                        o_ref[...] = tot

                    @pl.when(jnp.logical_not(first))
                    def _(o_ref=o_ref, tot=tot):
                        o_ref[...] += tot

        out_shape, out_specs = [], []
        for o, kind in zip(outs, out_kinds):
            d = o.shape[1]
            if kind == 'row':
                out_shape.append(jax.ShapeDtypeStruct((B, S, ncol * d), o.dtype))
                out_specs.append(pl.BlockSpec((1, ts, d), lambda k, b, s: (b, s, k)))
            else:
                out_shape.append(jax.ShapeDtypeStruct((1, ncol * d), o.dtype))
                out_specs.append(pl.BlockSpec((1, d), lambda k, b, s: (0, k)))
        res = pl.pallas_call(
            body, grid=(ncol, B, S // ts), in_specs=_in_specs(rows, seqs, pars, ts), out_specs=out_specs,
            out_shape=out_shape, name=name + "_fwd",
            compiler_params=_cparams(("arbitrary", "arbitrary", "arbitrary")))(*rows, *seqs, *pars)
        return tuple(res)

    def _bwd_call(rows, seqs, pars, cots):
        B, S, ts = _dims(rows)
        outs = _out_struct(rows, seqs, pars, ts)
        n_out = len(outs)
        all_in = list(rows) + list(seqs) + list(pars)

        def body(*refs):
            vals = _load(refs)
            cts = []
            for c_ref, o, kind in zip(refs[n_in:n_in + n_out], outs, out_kinds):
                if kind == 'row':
                    cts.append(c_ref[0])
                else:
                    cts.append(jnp.broadcast_to(c_ref[...], o.shape))

            def g(*dv):
                full = list(vals)
                for i, v in zip(diff_idx, dv):
                    full[i] = v
                return tuple(f(*full))

            _, vjp = jax.vjp(g, *[vals[i] for i in diff_idx])
            grads = vjp(tuple(cts))
            b, s = pl.program_id(1), pl.program_id(2)
            for o_ref, i, gr in zip(refs[n_in + n_out:], diff_idx, grads):
                if i < n_rows:
                    o_ref[0] = gr.astype(o_ref.dtype)
                else:
                    first = (s == 0) if i < n_rows + n_seqs else ((b == 0) & (s == 0))
                    target = (lambda r: r.at[0]) if i < n_rows + n_seqs else (lambda r: r)

                    @pl.when(first)
                    def _(o_ref=o_ref, gr=gr, target=target):
                        target(o_ref)[...] = gr

                    @pl.when(jnp.logical_not(first))
                    def _(o_ref=o_ref, gr=gr, target=target):
                        target(o_ref)[...] += gr

        cot_specs = []
        for o, kind in zip(outs, out_kinds):
            d = o.shape[1]
            if kind == 'row':
                cot_specs.append(pl.BlockSpec((1, ts, d), lambda k, b, s: (b, s, k)))
            else:
                cot_specs.append(pl.BlockSpec((1, d), lambda k, b, s: (0, k)))
        out_shape, out_specs = [], []
        for i in diff_idx:
            a = all_in[i]
            if i < n_rows:
                out_shape.append(jax.ShapeDtypeStruct((B, S, _width(i, a)), BF16 if i in windows else a.dtype))
                out_specs.append(pl.BlockSpec((1, ts, _width(i, a) // ncol), lambda k, b, s: (b, s, k)))
                continue
            out_shape.append(jax.ShapeDtypeStruct(a.shape, a.dtype))
            if i < n_rows + n_seqs:
                out_specs.append(pl.BlockSpec((1, 1, a.shape[2] // ncol), lambda k, b, s: (b, 0, k)))
            else:
                out_specs.append(pl.BlockSpec((1, a.shape[1] // ncol), lambda k, b, s: (0, k)))
        res = pl.pallas_call(
            body, grid=(ncol, B, S // ts), in_specs=_in_specs(rows, seqs, pars, ts) + cot_specs,
            out_specs=out_specs, out_shape=out_shape, name=name + "_bwd",
            compiler_params=_cparams(("arbitrary", "arbitrary", "arbitrary")))(*all_in, *cots)
        grads = [None] * n_in
        for i, r in zip(diff_idx, res):
            grads[i] = r
        for i in nodiff:
            grads[i] = jnp.zeros_like(all_in[i])
        stand_in_grads = tuple(grads[i] for i in sorted(windows))
        for i in windows:
            grads[i] = jnp.zeros_like(all_in[i])
        return (tuple(grads[:n_rows]), tuple(grads[n_rows:n_rows + n_seqs]), tuple(grads[n_rows + n_seqs:]),
                stand_in_grads)

    @jax.custom_vjp
    def op(rows, seqs, pars, stand_ins):
        return _fwd_call(rows, seqs, pars)

    def fwd(rows, seqs, pars, stand_ins):
        return _fwd_call(rows, seqs, pars), (rows, seqs, pars)

    def bwd(res, cots):
        rows, seqs, pars = res
        return _bwd_call(rows, seqs, pars, cots)

    op.defvjp(fwd, bwd)
    return lambda rows, seqs, pars, stand_ins=(): op(tuple(rows), tuple(seqs), tuple(pars), tuple(stand_ins))


def _rms(x, g):
    return x * lax.rsqrt(jnp.mean(x * x, axis=-1, keepdims=True) + EPS) * g


def _silu(x):
    return x * lax.logistic(x)


def _f_silu(c):
    return (_silu(c),)


def _f_modulate(x, scale, shift, g):
    return ((_rms(x, g) * (1.0 + scale) + shift).astype(BF16),)


def _f_rms(x, g):
    return (_rms(x, g).astype(BF16),)


def _f_dt(dt_raw, dt_bias, a_log):
    z = dt_raw + dt_bias
    dt = jnp.maximum(z, 0.0) + jnp.log1p(jnp.exp(-jnp.abs(z)))
    return dt, dt * (-jnp.exp(a_log))


def _f_gated_norm(y, z, g):
    return (_rms(y * _silu(z.astype(F32)), g).astype(BF16),)


def _f_merge(attn, ssm, ga, gb):
    return ((lax.logistic(ga.astype(F32)) * attn + lax.logistic(gb.astype(F32)) * ssm).astype(BF16),)


def _f_post(x, m, gate, g):
    return (x + gate * _rms(m, g),)


def _f_final_loss(x, ff, target, gate, g):
    e = x + gate * _rms(ff, g) - target
    return (e * e * (0.5 / D_MODEL),)


def _rope_tables(posf, inv_lane):
    B, S, _ = posf.shape
    ts = min(S, 512)

    def body(p_ref, inv_ref, c_ref, a_ref, b_ref):
        ang = p_ref[0] * inv_ref[...]
        cs, sn = jnp.cos(ang), jnp.sin(ang)
        lane = lax.broadcasted_iota(jnp.int32, ang.shape, 1)
        c_ref[0] = jnp.where(lane < ROPE, cs, 0.0)
        a_ref[0] = jnp.where(lane < ROPE // 2, -sn, 0.0)
        b_ref[0] = jnp.where((lane >= ROPE // 2) & (lane < ROPE), sn, 0.0)

    spec = pl.BlockSpec((1, ts, LANE), lambda b, s: (b, s, 0))
    sds = jax.ShapeDtypeStruct((B, S, LANE), F32)
    return pl.pallas_call(
        body, grid=(B, S // ts),
        in_specs=[pl.BlockSpec((1, ts, 1), lambda b, s: (b, s, 0)), pl.BlockSpec((1, LANE), lambda b, s: (0, 0))],
        out_specs=[spec, spec, spec], out_shape=[sds, sds, sds], name="rope_tables",
        compiler_params=_cparams(("parallel", "parallel")))(posf, inv_lane)


def _rot(u, c, a, bm):
    return u * c + pltpu.roll(u, 96, 1) * a + pltpu.roll(u, 32, 1) * bm


def _rot_t(g, c, a, bm):
    return g * c + pltpu.roll(g * a, 32, 1) + pltpu.roll(g * bm, 96, 1)


def _rope_q_call(q, tabs, transpose, name):
    B, S, W = q.shape
    ts = min(S, 512)
    fn = _rot_t if transpose else _rot
    out_dtype = F32 if transpose else BF16

    def body(q_ref, c_ref, a_ref, b_ref, o_ref):
        tc, ta, tb = c_ref[0], a_ref[0], b_ref[0]
        for h in range(W // QK_PAD):
            u = q_ref[0, :, h * QK_PAD:(h + 1) * QK_PAD].astype(F32) * ATT_SCALE
            r = fn(u[:, NOPE:], tc, ta, tb)
            o_ref[0, :, h * QK_PAD:(h + 1) * QK_PAD] = jnp.concatenate([u[:, :NOPE], r], axis=1).astype(out_dtype)

    tspec = pl.BlockSpec((1, ts, LANE), lambda b, s: (b, s, 0))
    qspec = pl.BlockSpec((1, ts, W), lambda b, s: (b, s, 0))
    return pl.pallas_call(
        body, grid=(B, S // ts), in_specs=[qspec, tspec, tspec, tspec], out_specs=qspec,
        out_shape=jax.ShapeDtypeStruct(q.shape, out_dtype), name=name,
        compiler_params=_cparams(("parallel", "parallel")))(q, *tabs)


@jax.custom_vjp
def rope_q(q, tabs):
    return _rope_q_call(q, tabs, False, "rope_q_fwd")


def _rope_q_fwd(q, tabs):
    return _rope_q_call(q, tabs, False, "rope_q_fwd"), tabs


def _rope_q_bwd(tabs, g):
    return _rope_q_call(g, tabs, True, "rope_q_bwd"), tuple(jnp.zeros_like(t) for t in tabs)


rope_q.defvjp(_rope_q_fwd, _rope_q_bwd)


def _build_k_fwd_call(kv, kr, tabs):
    B, S, _ = kv.shape
    ts = min(S, 512)

    def body(kv_ref, kr_ref, c_ref, a_ref, b_ref, o_ref):
        r = _rot(kr_ref[0], c_ref[0], a_ref[0], b_ref[0]).astype(BF16)
        for h in range(N_HEADS):
            o_ref[0, :, h * QK_PAD:(h + 1) * QK_PAD] = jnp.concatenate(
                [kv_ref[0, :, h * NOPE:(h + 1) * NOPE], r], axis=1)

    tspec = pl.BlockSpec((1, ts, LANE), lambda b, s: (b, s, 0))
    kr_spec = pl.BlockSpec((1, ts, LANE), lambda b, s: (b, s, KR_LANE0 // LANE))
    return pl.pallas_call(
        body, grid=(B, S // ts),
        in_specs=[pl.BlockSpec((1, ts, N_HEADS * NOPE), lambda b, s: (b, s, 0)), kr_spec, tspec, tspec, tspec],
        out_specs=pl.BlockSpec((1, ts, N_HEADS * QK_PAD), lambda b, s: (b, s, 0)),
        out_shape=jax.ShapeDtypeStruct((B, S, N_HEADS * QK_PAD), BF16), name="build_k_fwd",
        compiler_params=_cparams(("parallel", "parallel")))(kv, kr, *tabs)


def _build_k_bwd_call(g, tabs):
    B, S, _ = g.shape
    ts = min(S, 512)

    def body(g_ref, c_ref, a_ref, b_ref, dk_ref, dr_ref):
        tot = None
        for h in range(N_HEADS):
            dk_ref[0, :, h * NOPE:(h + 1) * NOPE] = g_ref[0, :, h * QK_PAD:h * QK_PAD + NOPE]
            part = g_ref[0, :, h * QK_PAD + NOPE:(h + 1) * QK_PAD].astype(F32)
            tot = part if tot is None else tot + part
        dr_ref[0] = _rot_t(tot, c_ref[0], a_ref[0], b_ref[0]).astype(BF16)

    tspec = pl.BlockSpec((1, ts, LANE), lambda b, s: (b, s, 0))
    return pl.pallas_call(
        body, grid=(B, S // ts),
        in_specs=[pl.BlockSpec((1, ts, N_HEADS * QK_PAD), lambda b, s: (b, s, 0)), tspec, tspec, tspec],
        out_specs=[pl.BlockSpec((1, ts, N_HEADS * NOPE), lambda b, s: (b, s, 0)), tspec],
        out_shape=[jax.ShapeDtypeStruct((B, S, N_HEADS * NOPE), BF16), jax.ShapeDtypeStruct((B, S, LANE), BF16)],
        name="build_k_bwd", compiler_params=_cparams(("parallel", "parallel")))(g, *tabs)


@jax.custom_vjp
def build_k(kv, src, stand_in, tabs):
    return _build_k_fwd_call(kv, src, tabs)


def _build_k_fwd(kv, src, stand_in, tabs):
    return _build_k_fwd_call(kv, src, tabs), (tabs, kv.shape, src)


def _build_k_bwd(res, g):
    tabs, kv_shape, src = res
    dk, dr = _build_k_bwd_call(g, tabs)
    dkv = jnp.concatenate([dk, jnp.zeros((kv_shape[0], kv_shape[1], kv_shape[2] - dk.shape[2]), BF16)], axis=-1)
    return dkv, jnp.zeros_like(src), dr, tuple(jnp.zeros_like(t) for t in tabs)


build_k.defvjp(_build_k_fwd, _build_k_bwd)


ATT_SCALE = (NOPE + ROPE) ** -0.5
NEG = -1e30


def _att_tiles(S):
    t = min(S, 512)
    return t, S // t


def _scores(q, k, diagonal):
    s = lax.dot_general(q, k, (((1,), (1,)), ((), ())), preferred_element_type=F32)
    if diagonal:
        row = lax.broadcasted_iota(jnp.int32, s.shape, 0)
        col = lax.broadcasted_iota(jnp.int32, s.shape, 1)
        s = jnp.where(col <= row, s, NEG)
    return s


ATT_HB = 4


def _causal_pairs(n):
    pairs = [(i, j) for i in range(n) for j in range(i + 1)]
    return (jnp.asarray([p[0] for p in pairs], jnp.int32), jnp.asarray([p[1] for p in pairs], jnp.int32))


def _head(ref_or_val, h, w):
    return ref_or_val[:, h * w:(h + 1) * w]


def _attn_fwd_call(q, k, vsrc, v_blk0):
    B, S, _ = q.shape
    t, n = _att_tiles(S)
    qi, kj = _causal_pairs(n)

    def body(qi_ref, kj_ref, q_ref, k_ref, v_ref, o_ref, lse_ref, m_sc, l_sc, acc_sc):
        p_id = pl.program_id(2)
        i, j = qi_ref[p_id], kj_ref[p_id]

        @pl.when(j == 0)
        def _():
            m_sc[...] = jnp.full(m_sc.shape, NEG, F32)
            l_sc[...] = jnp.zeros(l_sc.shape, F32)
            acc_sc[...] = jnp.zeros(acc_sc.shape, F32)

        def step(diagonal):
            qa, ka, va = q_ref[0], k_ref[0], v_ref[0]
            for h in range(ATT_HB):
                lanes = slice(h * LANE, (h + 1) * LANE)
                s = _scores(_head(qa, h, QK_PAD), _head(ka, h, QK_PAD), diagonal)
                m_prev = m_sc[:, lanes]
                m_new = jnp.maximum(m_prev, jnp.max(s, axis=1, keepdims=True))
                alpha = jnp.exp(m_prev - m_new)
                p = jnp.exp(s - jnp.tile(m_new, (1, t // LANE))).astype(BF16)
                l_sc[:, lanes] = alpha * l_sc[:, lanes] + jnp.dot(p, jnp.ones((t, LANE), BF16),
                                                                  preferred_element_type=F32)
                acc_sc[:, lanes] = alpha * acc_sc[:, lanes] + jnp.dot(p, _head(va, h, V_DIM),
                                                                      preferred_element_type=F32)
                m_sc[:, lanes] = m_new

        @pl.when(j < i)
        def _():
            step(False)

        @pl.when(j == i)
        def _():
            step(True)
            o_ref[0] = acc_sc[...] / l_sc[...]
            lse_ref[0] = m_sc[...] + jnp.log(l_sc[...])

    wq, wv = ATT_HB * QK_PAD, ATT_HB * V_DIM
    grid_spec = pltpu.PrefetchScalarGridSpec(
        num_scalar_prefetch=2, grid=(B, N_HEADS // ATT_HB, qi.shape[0]),
        in_specs=[pl.BlockSpec((1, t, wq), lambda b, h, p, qi, kj: (b, qi[p], h)),
                  pl.BlockSpec((1, t, wq), lambda b, h, p, qi, kj: (b, kj[p], h)),
                  pl.BlockSpec((1, t, wv), lambda b, h, p, qi, kj: (b, kj[p], v_blk0 + h))],
        out_specs=[pl.BlockSpec((1, t, wv), lambda b, h, p, qi, kj: (b, qi[p], h)),
                   pl.BlockSpec((1, t, wv), lambda b, h, p, qi, kj: (b, qi[p], h))],
        scratch_shapes=[pltpu.VMEM((t, wv), F32), pltpu.VMEM((t, wv), F32), pltpu.VMEM((t, wv), F32)])
    return pl.pallas_call(
        body, grid_spec=grid_spec,
        out_shape=[jax.ShapeDtypeStruct((B, S, N_HEADS * V_DIM), F32),
                   jax.ShapeDtypeStruct((B, S, N_HEADS * LANE), F32)],
        name="attn_fwd", compiler_params=_cparams(("parallel", "parallel", "arbitrary")))(qi, kj, q, k, vsrc)


def _attn_p_ds(q, k, v, o, do, lse, diagonal, t):
    s = _scores(q, k, diagonal)
    p = jnp.exp(s - jnp.tile(lse, (1, t // LANE)))
    dp = lax.dot_general(do.astype(BF16), v, (((1,), (1,)), ((), ())), preferred_element_type=F32)
    delta = jnp.sum(do * o, axis=1, keepdims=True)
    ds = p * (dp - delta)
    return p, ds


ATT_HB_BWD = 2


def _attn_bwd_call(q, k, vsrc, o, do, lse):
    B, S, _ = q.shape
    t, n = _att_tiles(S)
    qi, kj = _causal_pairs(n)
    n_pairs = qi.shape[0]
    hb = ATT_HB_BWD
    v_blk0 = N_HEADS // hb

    def body(qi_ref, kj_ref, q_ref, k_ref, v_ref, o_ref, do_ref, lse_ref, dq_ref, dk_ref, dv_ref, dq_sc, dk_sc, dv_sc):
        p_id = pl.program_id(2)
        i, j = qi_ref[p_id], kj_ref[p_id]

        @pl.when(p_id == 0)
        def _():
            dk_sc[...] = jnp.zeros(dk_sc.shape, F32)
            dv_sc[...] = jnp.zeros(dv_sc.shape, F32)

        @pl.when(j == 0)
        def _():
            dq_sc[...] = jnp.zeros(dq_sc.shape, F32)

        rows = pl.ds(pl.multiple_of(j * t, t), t)

        def step(diagonal):
            qa, ka, va, oa, doa, la = q_ref[0], k_ref[0], v_ref[0], o_ref[0], do_ref[0], lse_ref[0]
            for h in range(hb):
                qb, kb, dob = _head(qa, h, QK_PAD), _head(ka, h, QK_PAD), _head(doa, h, V_DIM)
                p, ds = _attn_p_ds(qb, kb, _head(va, h, V_DIM), _head(oa, h, V_DIM), dob, _head(la, h, LANE),
                                   diagonal, t)
                dsb = ds.astype(BF16)
                dq_sc[:, h * QK_PAD:(h + 1) * QK_PAD] += jnp.dot(dsb, kb, preferred_element_type=F32)
                dv_sc[rows, h * V_DIM:(h + 1) * V_DIM] += lax.dot_general(
                    p.astype(BF16), dob.astype(BF16), (((0,), (0,)), ((), ())), preferred_element_type=F32)
                dk_sc[rows, h * QK_PAD:(h + 1) * QK_PAD] += lax.dot_general(
                    dsb, qb, (((0,), (0,)), ((), ())), preferred_element_type=F32)

        @pl.when(j < i)
        def _():
            step(False)

        @pl.when(j == i)
        def _():
            step(True)
            dq_ref[0] = dq_sc[...].astype(BF16)

        @pl.when(p_id == n_pairs - 1)
        def _():
            dk_ref[0] = dk_sc[...].astype(BF16)
            dv_ref[0] = dv_sc[...].astype(BF16)

    wq, wv = hb * QK_PAD, hb * V_DIM
    at_q = lambda b, h, p, qi, kj: (b, qi[p], h)
    at_k = lambda b, h, p, qi, kj: (b, kj[p], h)
    whole = lambda b, h, p, qi, kj: (b, 0, h)
    grid_spec = pltpu.PrefetchScalarGridSpec(
        num_scalar_prefetch=2, grid=(B, N_HEADS // hb, n_pairs),
        in_specs=[pl.BlockSpec((1, t, wq), at_q), pl.BlockSpec((1, t, wq), at_k),
                  pl.BlockSpec((1, t, wv), lambda b, h, p, qi, kj: (b, kj[p], v_blk0 + h)),
                  pl.BlockSpec((1, t, wv), at_q), pl.BlockSpec((1, t, wv), at_q), pl.BlockSpec((1, t, wv), at_q)],
        out_specs=[pl.BlockSpec((1, t, wq), at_q), pl.BlockSpec((1, S, wq), whole), pl.BlockSpec((1, S, wv), whole)],
        scratch_shapes=[pltpu.VMEM((t, wq), F32), pltpu.VMEM((S, wq), F32), pltpu.VMEM((S, wv), F32)])
    return pl.pallas_call(
        body, grid_spec=grid_spec,
        out_shape=[jax.ShapeDtypeStruct((B, S, N_HEADS * QK_PAD), BF16),
                   jax.ShapeDtypeStruct((B, S, N_HEADS * QK_PAD), BF16),
                   jax.ShapeDtypeStruct((B, S, N_HEADS * V_DIM), BF16)],
        name="attn_bwd", compiler_params=_cparams(("parallel", "parallel", "arbitrary")))(
            qi, kj, q, k, vsrc, o, do, lse)


@jax.custom_vjp
def attention(q, k, kv):
    return _attn_fwd_call(q, k, kv, N_HEADS // ATT_HB)[0]


def _attention_fwd(q, k, kv):
    o, lse = _attn_fwd_call(q, k, kv, N_HEADS // ATT_HB)
    return o, (q, k, kv, o, lse)


def _attention_bwd(res, do):
    q, k, kv, o, lse = res
    dq, dk, dv = _attn_bwd_call(q, k, kv, o, do, lse)
    dkv = jnp.concatenate([jnp.zeros_like(dv), dv], axis=-1)
    return dq, dk, dkv


attention.defvjp(_attention_fwd, _attention_bwd)


SUBLANES = 8


def _zero_tail(v):
    return jnp.concatenate([v, jnp.zeros((SUBLANES, v.shape[1]), v.dtype)], axis=0)


def _shift_down(vz, sh):
    return pltpu.roll(vz, sh, 0)[:vz.shape[0] - SUBLANES]


def _shift_up(vz, sh):
    return pltpu.roll(vz, vz.shape[0] - sh, 0)[:vz.shape[0] - SUBLANES]


def _conv_pre(u, uz, w_ref, b_ref):
    acc = b_ref[...] + w_ref[pl.ds(CONV_K - 1, 1), :] * u
    for k in range(CONV_K - 1):
        acc = acc + w_ref[pl.ds(k, 1), :] * _shift_down(uz, CONV_K - 1 - k)
    return acc


def _conv_fwd_call(src, w, b):
    B, S, _ = src.shape
    C = w.shape[1]

    def body(u_ref, w_ref, b_ref, o_ref):
        uu = u_ref[0].astype(F32)
        o_ref[0] = _silu(_conv_pre(uu, _zero_tail(uu), w_ref, b_ref))

    spec = pl.BlockSpec((1, S, LANE), lambda c, bb: (bb, 0, c))
    return pl.pallas_call(
        body, grid=(C // LANE, B),
        in_specs=[pl.BlockSpec((1, S, LANE), lambda c, bb: (bb, 0, c + CONV_LANE0 // LANE)),
                  pl.BlockSpec((CONV_K, LANE), lambda c, bb: (0, c)), pl.BlockSpec((1, LANE), lambda c, bb: (0, c))],
        out_specs=spec, out_shape=jax.ShapeDtypeStruct((B, S, C), F32), name="conv_fwd",
        compiler_params=_cparams(("parallel", "arbitrary")))(src, w, b)


def _conv_bwd_call(src, w, b, g):
    B, S, _ = src.shape
    C = w.shape[1]

    def body(u_ref, w_ref, b_ref, g_ref, du_ref, dw_ref, db_ref):
        uu = u_ref[0].astype(F32)
        uz = _zero_tail(uu)
        pre = _conv_pre(uu, uz, w_ref, b_ref)
        sg = lax.logistic(pre)
        dpre = g_ref[0] * sg * (1.0 + pre * (1.0 - sg))
        dz = _zero_tail(dpre)
        du = w_ref[pl.ds(CONV_K - 1, 1), :] * dpre
        dws = [None] * CONV_K
        dws[CONV_K - 1] = jnp.sum(dpre * uu, axis=0, keepdims=True)
        for k in range(CONV_K - 1):
            sh = CONV_K - 1 - k
            du = du + w_ref[pl.ds(k, 1), :] * _shift_up(dz, sh)
            dws[k] = jnp.sum(dpre * _shift_down(uz, sh), axis=0, keepdims=True)
        du_ref[0] = du.astype(du_ref.dtype)
        dbv = jnp.sum(dpre, axis=0, keepdims=True)
        first = pl.program_id(1) == 0

        @pl.when(first)
        def _():
            for k in range(CONV_K):
                dw_ref[pl.ds(k, 1), :] = dws[k]
            db_ref[...] = dbv

        @pl.when(jnp.logical_not(first))
        def _():
            for k in range(CONV_K):
                dw_ref[pl.ds(k, 1), :] += dws[k]
            db_ref[...] += dbv

    spec = pl.BlockSpec((1, S, LANE), lambda c, bb: (bb, 0, c))
    wspec = pl.BlockSpec((CONV_K, LANE), lambda c, bb: (0, c))
    bspec = pl.BlockSpec((1, LANE), lambda c, bb: (0, c))
    uspec = pl.BlockSpec((1, S, LANE), lambda c, bb: (bb, 0, c + CONV_LANE0 // LANE))
    return pl.pallas_call(
        body, grid=(C // LANE, B), in_specs=[uspec, wspec, bspec, spec], out_specs=[spec, wspec, bspec],
        out_shape=[jax.ShapeDtypeStruct((B, S, C), BF16), jax.ShapeDtypeStruct(w.shape, F32),
                   jax.ShapeDtypeStruct(b.shape, F32)],
        name="conv_bwd", compiler_params=_cparams(("parallel", "arbitrary")))(src, w, b, g)


@jax.custom_vjp
def conv_silu(src, stand_in, w, b):
    return _conv_fwd_call(src, w, b)


def _conv_silu_fwd(src, stand_in, w, b):
    return _conv_fwd_call(src, w, b), (src, w, b)


def _conv_silu_bwd(res, g):
    du, dw, db = _conv_bwd_call(*res, g)
    return jnp.zeros_like(res[0]), du, dw, db


conv_silu.defvjp(_conv_silu_fwd, _conv_silu_bwd)


def _chunk_cumsum_call(a, reverse, name):
    B, S, W = a.shape

    def body(a_ref, o_ref):
        r = lax.broadcasted_iota(jnp.int32, (CHUNK, CHUNK), 0)
        c = lax.broadcasted_iota(jnp.int32, (CHUNK, CHUNK), 1)
        tri = jnp.where((c >= r) if reverse else (c <= r), 1.0, 0.0).astype(F32)
        o_ref[0] = jnp.dot(tri, a_ref[0], preferred_element_type=F32, precision=lax.Precision.HIGHEST)

    spec = pl.BlockSpec((1, CHUNK, W), lambda b, c: (b, c, 0))
    return pl.pallas_call(body, grid=(B, S // CHUNK), in_specs=[spec], out_specs=spec,
                          out_shape=jax.ShapeDtypeStruct(a.shape, F32), name=name,
                          compiler_params=_cparams(("parallel", "parallel")))(a)


@jax.custom_vjp
def chunk_cumsum(a):
    return _chunk_cumsum_call(a, False, "chunk_cumsum_fwd")


chunk_cumsum.defvjp(lambda a: (_chunk_cumsum_call(a, False, "chunk_cumsum_fwd"), None),
                    lambda _, g: (_chunk_cumsum_call(g, True, "chunk_cumsum_bwd"),))


GROUP_W = 4 * HEAD_P
HPG = SSM_HEADS // SSM_GROUPS


def _ssd_masks():
    lane = lax.broadcasted_iota(jnp.int32, (1, GROUP_W), 1)
    return [((lane >= HEAD_P * j) & (lane < HEAD_P * (j + 1))).astype(F32) for j in range(HPG)]


def _ssd_decays(ac_cols, acr_ref, gi):
    r = lax.broadcasted_iota(jnp.int32, (CHUNK, CHUNK), 0)
    c = lax.broadcasted_iota(jnp.int32, (CHUNK, CHUNK), 1)
    return [jnp.exp(jnp.where(c <= r, ac_cols[j] - acr_ref[0, gi * HPG + j], NEG)) for j in range(HPG)]


def _ssd_cols(blk, g):
    lane = lax.broadcasted_iota(jnp.int32, blk.shape, 1)
    return [jnp.sum(jnp.where(lane == HPG * g + j, blk, 0.0), axis=1, keepdims=True) for j in range(HPG)]


def _ssd_spread(cols):
    lane = lax.broadcasted_iota(jnp.int32, (1, GROUP_W), 1)
    out = jnp.broadcast_to(cols[HPG - 1], (CHUNK, GROUP_W))
    for j in range(HPG - 2, -1, -1):
        out = jnp.where(lane < HEAD_P * (j + 1), cols[j], out)
    return out


def _ssd_gather(val, cols, masks, g):
    lane = lax.broadcasted_iota(jnp.int32, (1, LANE), 1)
    out = jnp.zeros((CHUNK, LANE), F32)
    for j in range(HPG):
        tot = jnp.sum(val * masks[j], axis=1, keepdims=True)
        if cols is not None:
            tot = tot + cols[j]
        out = out + tot * (lane == HPG * g + j).astype(F32)
    return out


def _dot(a, b, dims):
    return lax.dot_general(a.astype(BF16), b.astype(BF16), (dims, ((), ())), preferred_element_type=F32)


NN = ((1,), (0,))
NT = ((1,), (1,))
TN = ((0,), (0,))


XBC_W = GROUP_W + 2 * STATE_N


SSD_STEP_GROUPS_FWD = 4
SSD_STEP_GROUPS_BWD = 2


def _ssd_load(xbc_ref, dt_ref, ac_ref, masks, g, gi):
    x = xbc_ref[0, :, gi * XBC_W:gi * XBC_W + GROUP_W]
    bm = xbc_ref[0, :, gi * XBC_W + GROUP_W:gi * XBC_W + GROUP_W + STATE_N]
    cm = xbc_ref[0, :, gi * XBC_W + GROUP_W + STATE_N:(gi + 1) * XBC_W]
    ac_cols = _ssd_cols(ac_ref[0], g)
    dt = _ssd_spread(_ssd_cols(dt_ref[0], g))
    ac = _ssd_spread(ac_cols)
    is_last = (lax.broadcasted_iota(jnp.int32, (CHUNK, GROUP_W), 0) == CHUNK - 1).astype(F32)
    return x, bm, cm, dt, ac, ac_cols, is_last


def _ssd_in_specs(nc, rev, gb):
    cc = (lambda c: nc - 1 - c) if rev else (lambda c: c)
    return [pl.BlockSpec((1, CHUNK, gb * XBC_W), lambda b, g, c: (b, cc(c), g)),
            pl.BlockSpec((1, CHUNK, LANE), lambda b, g, c: (b, cc(c), 0)),
            pl.BlockSpec((1, CHUNK, LANE), lambda b, g, c: (b, cc(c), 0)),
            pl.BlockSpec((1, gb * HPG, 1, CHUNK), lambda b, g, c: (b, g, 0, cc(c))),
            pl.BlockSpec((1, gb * GROUP_W), lambda b, g, c: (0, g))]


def _ssd_fwd_call(xbc, dtp, acp, acr, dsk):
    B, S, _ = xbc.shape
    nc = S // CHUNK
    gb = SSD_STEP_GROUPS_FWD

    def body(xbc_ref, dt_ref, ac_ref, ar_ref, ds_ref, y_ref, hp_ref, h_sc):
        @pl.when(pl.program_id(2) == 0)
        def _():
            h_sc[...] = jnp.zeros(h_sc.shape, F32)

        masks = _ssd_masks()
        ys = []
        for gi in range(gb):
            grp = gb * pl.program_id(1) + gi
            x, bm, cm, dt, ac, ac_cols, is_last = _ssd_load(xbc_ref, dt_ref, ac_ref, masks, grp, gi)
            last = jnp.sum(ac * is_last, axis=0, keepdims=True)
            decays = _ssd_decays(ac_cols, ar_ref, gi)
            xd = x * dt
            cb = _dot(cm, bm, NT)
            hprev = h_sc[gi]
            hp_ref[0, gi, 0] = hprev
            y = _dot(cm, hprev, NN) * jnp.exp(ac) + ds_ref[:, gi * GROUP_W:(gi + 1) * GROUP_W] * x
            for j in range(HPG):
                y = y + _dot(cb * decays[j], xd * masks[j], NN)
            ys.append(y)
            h_sc[gi] = hprev * jnp.exp(last) + _dot(bm, xd * jnp.exp(last - ac), TN)
        y_ref[0] = jnp.concatenate(ys, axis=1)

    ng = SSM_GROUPS // gb
    return pl.pallas_call(
        body, grid=(B, ng, nc), in_specs=_ssd_in_specs(nc, False, gb),
        out_specs=[pl.BlockSpec((1, CHUNK, gb * GROUP_W), lambda b, g, c: (b, c, g)),
                   pl.BlockSpec((1, gb, 1, STATE_N, GROUP_W), lambda b, g, c: (b, g, c, 0, 0))],
        out_shape=[jax.ShapeDtypeStruct((B, S, D_INNER), F32),
                   jax.ShapeDtypeStruct((B, SSM_GROUPS, nc, STATE_N, GROUP_W), F32)],
        scratch_shapes=[pltpu.VMEM((gb, STATE_N, GROUP_W), F32)], name="ssd_fwd",
        compiler_params=_cparams(("parallel", "parallel", "arbitrary")))(xbc, dtp, acp, acr, dsk)


def _ssd_bwd_call(xbc, dtp, acp, acr, dsk, hps, dy):
    B, S, _ = xbc.shape
    nc = S // CHUNK
    gb = SSD_STEP_GROUPS_BWD

    def body(xbc_ref, dt_ref, ac_ref, ar_ref, ds_ref, hp_ref, dy_ref,
             dxbc_ref, ddt_ref, dac_ref, dar_ref, dds_ref, dh_sc):
        first = pl.program_id(2) == 0

        @pl.when(first)
        def _():
            dh_sc[...] = jnp.zeros(dh_sc.shape, F32)

        masks = _ssd_masks()
        dxbc_parts, dds_parts = [], []
        for gi in range(gb):
            grp = gb * pl.program_id(0) + gi
            x, bm, cm, dt, ac, ac_cols, is_last = _ssd_load(xbc_ref, dt_ref, ac_ref, masks, grp, gi)
            last = jnp.sum(ac * is_last, axis=0, keepdims=True)
            g = dy_ref[0, :, gi * GROUP_W:(gi + 1) * GROUP_W]
            hprev = hp_ref[0, gi, 0]
            dh = dh_sc[gi]
            decays = _ssd_decays(ac_cols, ar_ref, gi)
            dcols = []
            xd = x * dt
            cb = _dot(cm, bm, NT)
            e_c = jnp.exp(ac)
            e_end = jnp.exp(last - ac)
            e_last = jnp.exp(last)
            z = _dot(cm, hprev, NN)
            dz = g * e_c
            dac = g * z * e_c
            dc = _dot(dz, hprev, NT)
            dhprev = _dot(cm, dz, TN) + dh * e_last
            dcb = jnp.zeros((CHUNK, CHUNK), F32)
            dxd = jnp.zeros(xd.shape, F32)
            for j in range(HPG):
                gj = cb * decays[j]
                dgj = _dot(g * masks[j], xd, NT)
                dxd = dxd + _dot(gj, g, TN) * masks[j]
                dcb = dcb + dgj * decays[j]
                dseg = dgj * gj
                dcols.append(jnp.sum(dseg, axis=1, keepdims=True))
                dar_ref[0, gi * HPG + j] = -jnp.sum(dseg, axis=0, keepdims=True)
            dc = dc + _dot(dcb, bm, NN)
            db = _dot(dcb, cm, TN)
            sx = xd * e_end
            db = db + _dot(sx, dh, NT)
            dsx = _dot(bm, dh, NN)
            dxd = dxd + dsx * e_end
            de = dsx * sx
            dac = dac - de
            dlast = jnp.sum(de, axis=0, keepdims=True) + jnp.sum(dh * hprev, axis=0, keepdims=True) * e_last
            dsk = ds_ref[:, gi * GROUP_W:(gi + 1) * GROUP_W]
            dxbc_parts += [dxd * dt + dsk * g, db, dc]
            ddt_ref[0, gi] = _ssd_gather(dxd * x, None, masks, grp)
            dac_ref[0, gi] = _ssd_gather(dac + is_last * dlast, dcols, masks, grp)
            dds_parts.append(jnp.sum(g * x, axis=0, keepdims=True))
            dh_sc[gi] = dhprev
        dxbc_ref[0] = jnp.concatenate(dxbc_parts, axis=1)
        dds = jnp.concatenate(dds_parts, axis=1)
        first_all = first & (pl.program_id(1) == 0)

        @pl.when(first_all)
        def _():
            dds_ref[...] = dds

        @pl.when(jnp.logical_not(first_all))
        def _():
            dds_ref[...] += dds

    rc = lambda c: nc - 1 - c
    ng = SSM_GROUPS // gb
    in_specs = [pl.BlockSpec(s.block_shape, (lambda g, b, c, f=s.index_map: f(b, g, c))) for s in _ssd_in_specs(nc, True, gb)]
    in_specs.append(pl.BlockSpec((1, gb, 1, STATE_N, GROUP_W), lambda g, b, c: (b, g, rc(c), 0, 0)))
    in_specs.append(pl.BlockSpec((1, CHUNK, gb * GROUP_W), lambda g, b, c: (b, rc(c), g)))
    per_group = pl.BlockSpec((1, gb, CHUNK, LANE), lambda g, b, c: (b, g, rc(c), 0))
    out_specs = [pl.BlockSpec((1, CHUNK, gb * XBC_W), lambda g, b, c: (b, rc(c), g)), per_group, per_group,
                 pl.BlockSpec((1, gb * HPG, 1, CHUNK), lambda g, b, c: (b, g, 0, rc(c))),
                 pl.BlockSpec((1, gb * GROUP_W), lambda g, b, c: (0, g))]
    out_shape = [jax.ShapeDtypeStruct(xbc.shape, F32),
                 jax.ShapeDtypeStruct((B, SSM_GROUPS, S, LANE), F32), jax.ShapeDtypeStruct((B, SSM_GROUPS, S, LANE), F32),
                 jax.ShapeDtypeStruct(acr.shape, F32), jax.ShapeDtypeStruct(dsk.shape, F32)]
    return pl.pallas_call(
        body, grid=(ng, B, nc), in_specs=in_specs, out_specs=out_specs, out_shape=out_shape,
        scratch_shapes=[pltpu.VMEM((gb, STATE_N, GROUP_W), F32)], name="ssd_bwd",
        compiler_params=_cparams(("arbitrary", "arbitrary", "arbitrary")))(xbc, dtp, acp, acr, dsk, hps, dy)


@jax.custom_vjp
def ssd(xbc, dtp, acp, acr, dsk):
    return _ssd_fwd_call(xbc, dtp, acp, acr, dsk)[0]


def _ssd_fwd(xbc, dtp, acp, acr, dsk):
    y, hps = _ssd_fwd_call(xbc, dtp, acp, acr, dsk)
    return y, (xbc, dtp, acp, acr, dsk, hps)


def _ssd_bwd(res, dy):
    dxbc, ddt, dac, dacr, dds = _ssd_bwd_call(*res, dy)
    return dxbc, jnp.sum(ddt, axis=1), jnp.sum(dac, axis=1), dacr, dds


ssd.defvjp(_ssd_fwd, _ssd_bwd)


def _pack_small(arrs):
    flat = jnp.concatenate([a.reshape(-1) for a in arrs])
    rows = -(-flat.shape[0] // (8 * LANE)) * 8
    return jnp.pad(flat, (0, rows * LANE - flat.shape[0])).reshape(rows, LANE)


def _unpack_small(buf, shapes):
    flat = buf.reshape(-1)
    out, off = [], 0
    for shp in shapes:
        n = int(np.prod(shp))
        out.append(flat[off:off + n].reshape(shp))
        off += n
    return out


def _rows_tile(rows, cap):
    for cand in range(min(rows, cap), 7, -8):
        if rows % cand == 0:
            return cand
    return rows


def _pair_sum(mine, theirs, cidx, name):
    n4, kk, nn = mine.shape
    half = kk // 2
    tr = _rows_tile(half, 256)
    nb = half // tr

    def body(c_ref, a_ref, b_ref, o_ref, ob_ref):
        tot = a_ref[...] + b_ref[...]
        o_ref[...] = tot
        ob_ref[...] = tot.astype(BF16)

    spec = pl.BlockSpec((1, tr, nn), lambda j, i, c: (j, i, 0))
    grid_spec = pltpu.PrefetchScalarGridSpec(
        num_scalar_prefetch=1, grid=(n4, nb),
        in_specs=[pl.BlockSpec((1, tr, nn), lambda j, i, c: (j, c[0] * nb + i, 0)), spec], out_specs=[spec, spec])
    return pl.pallas_call(
        body, grid_spec=grid_spec,
        out_shape=[jax.ShapeDtypeStruct((n4, half, nn), F32), jax.ShapeDtypeStruct((n4, half, nn), BF16)],
        name=name, compiler_params=_cparams(("parallel", "parallel")))(cidx, mine, theirs)


def _chip_sum(quad, pair, chip_idx, name):
    _, rows, nn = quad.shape
    tr = _rows_tile(rows, 256)

    def body(s_ref, q_ref, p_ref, o_ref):
        for mine in range(4):
            @pl.when(s_ref[0] == mine)
            def _(mine=mine):
                acc = None
                for d in range(4):
                    term = p_ref[0] if d == mine else q_ref[d].astype(F32)
                    acc = term if acc is None else acc + term
                o_ref[...] = acc

    grid_spec = pltpu.PrefetchScalarGridSpec(
        num_scalar_prefetch=1, grid=(rows // tr,),
        in_specs=[pl.BlockSpec((4, tr, nn), lambda i, s: (0, i, 0)), pl.BlockSpec((1, tr, nn), lambda i, s: (s[0], i, 0))],
        out_specs=pl.BlockSpec((tr, nn), lambda i, s: (i, 0)))
    return pl.pallas_call(body, grid_spec=grid_spec, out_shape=jax.ShapeDtypeStruct((rows, nn), F32), name=name,
                          compiler_params=_cparams(("parallel",)))(chip_idx, quad, pair)


def _adam_halves_call(w, mine, other, cidx, m, v, name):
    rows, nn = w.shape
    half = rows // 2
    tr = _rows_tile(half, 128)
    nb = half // tr

    def body(c_ref, w_ref, a_ref, b_ref, m_ref, v_ref, g_ref, d_ref, nm_ref, nv_ref):
        upper = (pl.program_id(0) >= nb).astype(jnp.int32)
        g = jnp.where(upper == c_ref[0], a_ref[...], b_ref[...])
        g_ref[...] = g
        d_ref[...], nm_ref[...], nv_ref[...] = _adam_fn(w_ref[...], g, m_ref[...], v_ref[...])

    spec = pl.BlockSpec((tr, nn), lambda i, c: (i, 0))
    hspec = pl.BlockSpec((tr, nn), lambda i, c: (i % nb, 0))
    grid_spec = pltpu.PrefetchScalarGridSpec(num_scalar_prefetch=1, grid=(2 * nb,),
                                             in_specs=[spec, hspec, hspec, spec, spec], out_specs=[spec] * 4)
    return pl.pallas_call(body, grid_spec=grid_spec, out_shape=[jax.ShapeDtypeStruct((rows, nn), F32)] * 4, name=name,
                          compiler_params=_cparams(("parallel",)))(cidx, w, mine, other, m, v)


def _stack_sum(stack, name):
    n, rows, nn = stack.shape
    tr = _rows_tile(rows, 256)

    def body(s_ref, o_ref):
        acc = s_ref[0]
        for d in range(1, n):
            acc = acc + s_ref[d]
        o_ref[...] = acc

    return pl.pallas_call(
        body, grid=(rows // tr,), in_specs=[pl.BlockSpec((n, tr, nn), lambda i: (0, i, 0))],
        out_specs=pl.BlockSpec((tr, nn), lambda i: (i, 0)), out_shape=jax.ShapeDtypeStruct((rows, nn), F32),
        name=name, compiler_params=_cparams(("parallel",)))(stack)


def _adam_call(w, g, m, v, name):
    rows, nn = w.shape
    tr = _rows_tile(rows, 128)

    def body(w_ref, g_ref, m_ref, v_ref, d_ref, nm_ref, nv_ref):
        d_ref[...], nm_ref[...], nv_ref[...] = _adam_fn(w_ref[...], g_ref[...], m_ref[...], v_ref[...])

    spec = pl.BlockSpec((tr, nn), lambda i: (i, 0))
    sds = jax.ShapeDtypeStruct((rows, nn), F32)
    return pl.pallas_call(body, grid=(rows // tr,), in_specs=[spec] * 4, out_specs=[spec] * 3,
                          out_shape=[sds] * 3, name=name, compiler_params=_cparams(("parallel",)))(w, g, m, v)


def _adam_fn(w, g, m, v):
    m = ADAM_B1 * m + (1.0 - ADAM_B1) * g
    v = ADAM_B2 * v + (1.0 - ADAM_B2) * (g * g)
    m_hat = m / (1.0 - ADAM_B1 ** ADAM_STEP)
    v_hat = v / (1.0 - ADAM_B2 ** ADAM_STEP)
    delta = -ADAM_LR * (m_hat / (jnp.sqrt(v_hat) + ADAM_EPS) + ADAM_WD * w)
    return delta, m, v


def _mesh_pos():
    return lax.axis_index("x"), lax.axis_index("y"), lax.axis_index("c")


def _other_chips(x, y):
    return [(1 - x, y), (x, 1 - y), (1 - x, 1 - y)]


HBM_SPEC = pl.BlockSpec(memory_space=pl.ANY)


def _remote(src, dst, send_sems, recv_sems, k, to):
    return pltpu.make_async_remote_copy(src_ref=src, dst_ref=dst, send_sem=send_sems.at[k], recv_sem=recv_sems.at[k],
                                        device_id=to, device_id_type=MESH)


def _half_rows(c, rows, align):
    half = rows // 2
    return (pl.ds(pl.multiple_of(c * half, align), half), pl.ds(pl.multiple_of((1 - c) * half, align), half))


def _gather_weights(mats, conv):
    n = len(mats)

    def body(*refs):
        ins, conv_in = refs[:n], refs[n]
        outs, conv_out = refs[n + 1:2 * n + 1], refs[2 * n + 1]
        send_sems, recv_sems, local_sem = refs[2 * n + 2:]
        x, y, c = _mesh_pos()
        me, sibling, s = (x, y, c), (x, y, 1 - c), 2 * x + y
        chips = _other_chips(x, y)
        rows = [_half_rows(c, m.shape[0], 16) for m in mats]
        own = pltpu.make_async_copy(conv_in, conv_out.at[s], local_sem)
        own.start()
        sent = []
        for i in range(n):
            mine = rows[i][0]
            for j, (cx, cy) in enumerate(chips):
                sent.append(_remote(ins[i].at[mine], outs[i].at[s, mine], send_sems, recv_sems, 6 * i + j, (cx, cy, c)))
        for j, (cx, cy) in enumerate(chips):
            sent.append(_remote(conv_in, conv_out.at[s], send_sems, recv_sems, 6 * n + j, (cx, cy, c)))
        for cp in sent:
            cp.start()
        for i in range(n):
            mine = rows[i][0]
            for j, (cx, cy) in enumerate(chips):
                landed = outs[i].at[2 * cx + cy, mine]
                _remote(landed, landed, send_sems, recv_sems, 6 * i + j, me).wait_recv()
                fwd = _remote(landed, landed, send_sems, recv_sems, 6 * i + 3 + j, sibling)
                fwd.start()
                sent.append(fwd)
        for j, (cx, cy) in enumerate(chips):
            slot = conv_out.at[2 * cx + cy]
            _remote(slot, slot, send_sems, recv_sems, 6 * n + j, me).wait_recv()
        for i in range(n):
            theirs_rows = rows[i][1]
            for j, (cx, cy) in enumerate(chips):
                theirs = outs[i].at[2 * cx + cy, theirs_rows]
                _remote(theirs, theirs, send_sems, recv_sems, 6 * i + 3 + j, me).wait_recv()
        for cp in sent:
            cp.wait_send()
        own.wait()

    out_shape = [jax.ShapeDtypeStruct((4,) + m.shape, m.dtype) for m in mats]
    out_shape.append(jax.ShapeDtypeStruct((4,) + conv.shape, conv.dtype))
    res = pl.pallas_call(
        body, in_specs=[HBM_SPEC] * (n + 1), out_specs=[HBM_SPEC] * (n + 1), out_shape=out_shape,
        scratch_shapes=[pltpu.SemaphoreType.DMA((6 * n + 3,)), pltpu.SemaphoreType.DMA((6 * n + 3,)),
                        pltpu.SemaphoreType.DMA],
        name="all_gather_weights")(*mats, conv)
    chip = 2 * lax.axis_index("x") + lax.axis_index("y")
    full = [lax.dynamic_update_slice_in_dim(r, m[None], chip, axis=0) for r, m in zip(res[:n], mats)]
    return full, res[n]


def _sibling_exchange(stacks):
    n = len(stacks)

    def body(*refs):
        ins, outs = refs[:n], refs[n:2 * n]
        send_sems, recv_sems = refs[2 * n:]
        x, y, c = _mesh_pos()
        cps = []
        for i in range(n):
            theirs = _half_rows(c, stacks[i].shape[1], 8)[1]
            cps.append(_remote(ins[i].at[:, theirs, :], outs[i], send_sems, recv_sems, i, (x, y, 1 - c)))
        for cp in cps:
            cp.start()
        for cp in cps:
            cp.wait()

    out_shape = [jax.ShapeDtypeStruct((4, s.shape[1] // 2, s.shape[2]), s.dtype) for s in stacks]
    return pl.pallas_call(
        body, in_specs=[HBM_SPEC] * n, out_specs=[HBM_SPEC] * n, out_shape=out_shape,
        scratch_shapes=[pltpu.SemaphoreType.DMA((n,)), pltpu.SemaphoreType.DMA((n,))],
        name="grad_sibling_exchange")(*stacks)


def _chip_exchange(parts):
    n = len(parts)

    def body(*refs):
        ins, outs = refs[:n], refs[n:2 * n]
        send_sems, recv_sems = refs[2 * n:]
        x, y, c = _mesh_pos()
        me, s = (x, y, c), 2 * x + y
        chips = _other_chips(x, y)
        sent = [_remote(ins[i].at[2 * cx + cy], outs[i].at[s], send_sems, recv_sems, 3 * i + j, (cx, cy, c))
                for i in range(n) for j, (cx, cy) in enumerate(chips)]
        for cp in sent:
            cp.start()
        for i in range(n):
            for j, (cx, cy) in enumerate(chips):
                slot = outs[i].at[2 * cx + cy]
                _remote(slot, slot, send_sems, recv_sems, 3 * i + j, me).wait_recv()
        for cp in sent:
            cp.wait_send()

    return pl.pallas_call(
        body, in_specs=[HBM_SPEC] * n, out_specs=[HBM_SPEC] * n,
        out_shape=[jax.ShapeDtypeStruct(p.shape, p.dtype) for p in parts],
        scratch_shapes=[pltpu.SemaphoreType.DMA((3 * n,)), pltpu.SemaphoreType.DMA((3 * n,))],
        name="grad_chip_exchange")(*parts)


def _sibling_swap(halves):
    n = len(halves)

    def body(*refs):
        ins, outs = refs[:n], refs[n:2 * n]
        send_sems, recv_sems = refs[2 * n:]
        x, y, c = _mesh_pos()
        cps = [_remote(ins[i], outs[i], send_sems, recv_sems, i, (x, y, 1 - c)) for i in range(n)]
        for cp in cps:
            cp.start()
        for cp in cps:
            cp.wait()

    return pl.pallas_call(
        body, in_specs=[HBM_SPEC] * n, out_specs=[HBM_SPEC] * n,
        out_shape=[jax.ShapeDtypeStruct(h.shape, h.dtype) for h in halves],
        scratch_shapes=[pltpu.SemaphoreType.DMA((n,)), pltpu.SemaphoreType.DMA((n,))],
        name="grad_sibling_swap")(*halves)


def _gather_small(vec):
    def body(in_ref, out_ref, send_sems, recv_sems, local_sem):
        x, y, c = _mesh_pos()
        me = (x, y, c)
        own = pltpu.make_async_copy(in_ref, out_ref.at[4 * x + 2 * y + c], local_sem)
        own.start()
        peers = [(1 - x if k & 4 else x, 1 - y if k & 2 else y, 1 - c if k & 1 else c) for k in range(1, 8)]
        sent = [_remote(in_ref, out_ref.at[4 * x + 2 * y + c], send_sems, recv_sems, k, p) for k, p in enumerate(peers)]
        for cp in sent:
            cp.start()
        for k, (px, py, pc) in enumerate(peers):
            slot = out_ref.at[4 * px + 2 * py + pc]
            _remote(slot, slot, send_sems, recv_sems, k, me).wait_recv()
        for cp in sent:
            cp.wait_send()
        own.wait()

    return pl.pallas_call(
        body, in_specs=[HBM_SPEC], out_specs=HBM_SPEC, out_shape=jax.ShapeDtypeStruct((8,) + vec.shape, vec.dtype),
        scratch_shapes=[pltpu.SemaphoreType.DMA((7,)), pltpu.SemaphoreType.DMA((7,)), pltpu.SemaphoreType.DMA],
        name="grad_gather_small")(vec)


def _reduce_matrices(stacks, names):
    cidx = lax.axis_index("c").astype(jnp.int32).reshape(1)
    chip = (2 * lax.axis_index("x") + lax.axis_index("y")).astype(jnp.int32).reshape(1)
    got = _sibling_exchange(stacks)
    pairs = [_pair_sum(a, b, cidx, "grad_pair_sum_" + nm) for a, b, nm in zip(stacks, got, names)]
    quads = _chip_exchange([p[1] for p in pairs])
    mine = [_chip_sum(q, p[0], chip, "grad_chip_sum_" + nm) for q, p, nm in zip(quads, pairs, names)]
    return mine, _sibling_swap(mine)


def _pad_cols(a, n):
    return jnp.concatenate([a, jnp.zeros((a.shape[0], n - a.shape[1]), a.dtype)], axis=1)


def _group_channels(a):
    lead = a.shape[:-1]
    xs = a[..., :D_INNER].reshape(lead + (SSM_GROUPS, GROUP_W))
    bs = a[..., D_INNER:D_INNER + SSM_GROUPS * STATE_N].reshape(lead + (SSM_GROUPS, STATE_N))
    cs = a[..., D_INNER + SSM_GROUPS * STATE_N:].reshape(lead + (SSM_GROUPS, STATE_N))
    return jnp.concatenate([xs, bs, cs], axis=-1).reshape(lead + (CONV_CH,))


PROJ_SEGS = (('gate_a', D_MODEL), ('gate_b', D_MODEL), ('z', D_INNER), ('xbc', CONV_CH), ('q_lat', Q_RANK),
             ('kv_lat', KV_RANK), ('k_rope', LANE), ('dt', LANE))
PROJ_WIDE = sum(w for _, w in PROJ_SEGS[:4])
PROJ_LANE0 = {n: (v if v < PROJ_WIDE else v - PROJ_WIDE) for n, v in
              zip([n for n, _ in PROJ_SEGS], [int(v) for v in np.cumsum([0] + [w for _, w in PROJ_SEGS])[:-1]])}
CONV_LANE0 = PROJ_LANE0['xbc']
KR_LANE0 = PROJ_LANE0['k_rope']


def _lay_w_in(w):
    idx = np.cumsum(IN_SIZES)[:-1]
    q_lat, kv_lat, k_rope, z, xbc, dt, gate_a, gate_b = jnp.split(w, [int(v) for v in idx], axis=1)
    return jnp.concatenate([gate_a, gate_b, z, _group_channels(xbc), q_lat, kv_lat, _pad_cols(k_rope, LANE),
                            _pad_cols(dt, LANE)], axis=1)


@jax.custom_vjp
def project(h, w, tok):
    return _project_impl(h, w)


def _project_impl(h, w):
    return (_mm(h, w[:, :PROJ_WIDE], "w_in_fwd", BF16), _mm(h, w[:, PROJ_WIDE:], "w_in_narrow_fwd")) + tuple(
        jnp.zeros((h.shape[0], wd), BF16) for _, wd in PROJ_SEGS)


def _project_fwd(h, w, tok):
    return _project_impl(h, w), (h, w)


def _project_bwd(res, cots):
    h, w = res
    g = jnp.concatenate(cots[2:], axis=1)
    return _mm(g, w.T, "w_in_dx", h.dtype), jnp.zeros_like(w), _mm_tn(h, g, "w_in_dw")


project.defvjp(_project_fwd, _project_bwd)


def _lay_w_uq(w):
    w3 = w.reshape(Q_RANK, N_HEADS, NOPE + ROPE)
    w3 = jnp.concatenate([w3, jnp.zeros((Q_RANK, N_HEADS, QK_PAD - NOPE - ROPE), w.dtype)], axis=2)
    return w3.reshape(Q_RANK, N_HEADS * QK_PAD)


def _lay_w_ukv(w):
    w3 = w.reshape(KV_RANK, N_HEADS, NOPE + V_DIM)
    return jnp.concatenate([w3[:, :, :NOPE].reshape(KV_RANK, -1), w3[:, :, NOPE:].reshape(KV_RANK, -1)], axis=1)


def _pad_lanes(v, n=LANE):
    return jnp.concatenate([v, jnp.zeros((v.shape[0], n - v.shape[1]), v.dtype)], axis=1)


def _local_loss(toks, small, x, wb, c8, posf, target):
    B, S, D = x.shape
    T = B * S

    def lin(name, a, key, lay=lambda w: w, out_dtype=F32):
        return make_linear(name, out_dtype)(a, lay(wb[key]), lay(toks[key]))

    rows2 = lambda a: a.reshape(T, a.shape[-1])
    rows3 = lambda a: a.reshape(B, S, a.shape[-1])

    sc = make_rowwise("silu_c", _f_silu, 1, 0, 0, ('row',))((c8[None],), (), ())[0][0]
    mod = lin("ada", sc, 'w_ada')[:B] + small['b_ada']
    shift1, scale1, gate1, shift2, scale2, gate2 = [m[:, None, :] for m in jnp.split(mod, 6, axis=-1)]

    modulate = make_rowwise("modulate1", _f_modulate, 1, 2, 1, ('row',))
    h = modulate((x,), (scale1, shift1), (small['g_pre_mix'],))[0]
    outs = project(rows2(h), _lay_w_in(wb['w_in']), _lay_w_in(toks['w_in']))
    wide = lax.stop_gradient(rows3(outs[0]))
    proj = lax.stop_gradient(rows3(outs[1]))
    stand = {n: rows3(o) for (n, _), o in zip(PROJ_SEGS, outs[2:])}

    def win(seg, block):
        return (PROJ_LANE0[seg] // block, dict(PROJ_SEGS)[seg])

    inv = ROPE_THETA ** (-jnp.arange(ROPE // 2, dtype=F32) / (ROPE // 2))
    inv_lane = jnp.concatenate([inv, inv, jnp.zeros((LANE - ROPE,), F32)])[None]
    tabs = tuple(_rope_tables(posf, inv_lane))
    qn = make_rowwise("rms_q", _f_rms, 1, 0, 1, ('row',), windows={0: win('q_lat', Q_RANK)})(
        (proj,), (), (small['g_q_lat'],), (stand['q_lat'],))[0]
    kvn = make_rowwise("rms_kv", _f_rms, 1, 0, 1, ('row',), windows={0: win('kv_lat', KV_RANK)})(
        (proj,), (), (small['g_kv_lat'],), (stand['kv_lat'],))[0]
    qp = rows3(lin("w_uq", rows2(qn), 'w_uq', _lay_w_uq))
    kvp = rows3(lin("w_ukv", rows2(kvn), 'w_ukv', _lay_w_ukv, BF16))
    qr = rope_q(qp, tabs)
    kr = build_k(kvp, proj, stand['k_rope'], tabs)
    att = attention(qr, kr, kvp)
    attn = rows3(lin("w_o_attn", rows2(att), 'w_o_attn'))

    xa = conv_silu(wide, stand['xbc'], _group_channels(wb['conv_w_f32']), _group_channels(small['conv_b']))
    dt_pad, a_pad = make_rowwise("dt_softplus", _f_dt, 1, 0, 2, ('row', 'row'), windows={0: win('dt', LANE)})(
        (proj,), (), (_pad_lanes(small['dt_bias']), _pad_lanes(small['a_log'])), (stand['dt'],))
    ac_pad = chunk_cumsum(a_pad)
    acr = jnp.transpose(ac_pad[..., :SSM_HEADS], (0, 2, 1))[:, :, None, :]
    dsk = jnp.repeat(small['d_skip'], HEAD_P, axis=-1)
    y = ssd(xa, dt_pad, ac_pad, acr, dsk)
    yg = make_rowwise("gated_norm", _f_gated_norm, 2, 0, 1, ('row',), ncol=SSM_GROUPS, ts_cap=2048,
                      windows={1: win('z', GROUP_W)})((y, wide), (), (small['g_ssm_out'],), (stand['z'],))[0]
    ssm = rows3(lin("w_o_ssm", rows2(yg), 'w_o_ssm'))

    merged = make_rowwise("merge", _f_merge, 4, 0, 0, ('row',),
                          windows={2: win('gate_a', D_MODEL), 3: win('gate_b', D_MODEL)})(
        (attn, ssm, wide, wide), (), (), (stand['gate_a'], stand['gate_b']))[0]
    mix = rows3(lin("w_out", rows2(merged), 'w_out'))
    x1 = make_rowwise("post_mix", _f_post, 2, 1, 1, ('row',))((x, mix), (gate1,), (small['g_post_mix'],))[0]

    h2 = make_rowwise("modulate2", _f_modulate, 1, 2, 1, ('row',))((x1,), (scale2, shift2), (small['g_pre_mlp'],))[0]
    ff = rows3(ffn(rows2(h2), wb['w_ff1'], toks['w_ff1'], wb['w_ff2'], toks['w_ff2']))
    lvec = make_rowwise("final_loss", _f_final_loss, 3, 1, 1, ('sum',), nodiff=(2,))(
        (x1, ff, target), (gate2,), (small['g_post_mlp'],))[0]
    return jnp.sum(lvec)


MATRICES = COL_SHARDED + ROW_SHARDED


def _local_step(x, c, positions, target, wb, small):
    B = x.shape[0]
    c8 = jnp.concatenate([c, jnp.zeros((16 - B, c.shape[1]), F32)], axis=0)
    posf = positions.astype(F32)[..., None]
    toks = {k: jnp.zeros(wb[k].shape, F32) for k in MATRICES if k != 'conv_w'}
    conv_w = wb['conv_w_f32']

    def loss_fn(toks, small, conv_w, x):
        wbl = dict(wb)
        wbl['conv_w_f32'] = conv_w
        return _local_loss(toks, small, x, wbl, c8, posf, target)

    loss, (g_tok, g_small, g_conv, g_x) = jax.value_and_grad(loss_fn, argnums=(0, 1, 2, 3))(toks, small, conv_w, x)
    grads = dict(g_tok)
    grads.update(g_small)
    grads['conv_w'] = g_conv
    return loss, g_x, grads


def kernel(x, c, positions, w_ada, b_ada, g_pre_mix, g_post_mix, w_in, g_q_lat, g_kv_lat, w_uq, w_ukv, w_o_attn, conv_w, conv_b, dt_bias, a_log, d_skip, g_ssm_out, w_o_ssm, w_out, g_pre_mlp, g_post_mlp, w_ff1, w_ff2, loss_target, m_w_ada, m_b_ada, m_g_pre_mix, m_g_post_mix, m_w_in, m_g_q_lat, m_g_kv_lat, m_w_uq, m_w_ukv, m_w_o_attn, m_conv_w, m_conv_b, m_dt_bias, m_a_log, m_d_skip, m_g_ssm_out, m_w_o_ssm, m_w_out, m_g_pre_mlp, m_g_post_mlp, m_w_ff1, m_w_ff2, v_w_ada, v_b_ada, v_g_pre_mix, v_g_post_mix, v_w_in, v_g_q_lat, v_g_kv_lat, v_w_uq, v_w_ukv, v_w_o_attn, v_conv_w, v_conv_b, v_dt_bias, v_a_log, v_d_skip, v_g_ssm_out, v_w_o_ssm, v_w_out, v_g_pre_mlp, v_g_post_mlp, v_w_ff1, v_w_ff2):
    given = dict(locals())
    w_loc = {n: given[n] for n in WEIGHTS}
    m_loc = {n: given["m_" + n] for n in WEIGHTS}
    v_loc = {n: given["v_" + n] for n in WEIGHTS}
    mats = [n for n in WEIGHTS if n in MATRICES and n != 'conv_w']
    vecs = [n for n in WEIGHTS if n not in MATRICES]

    g_mats, g_conv = _gather_weights([w_loc[n][0].astype(BF16) for n in mats], conv_w[0])
    wb = {}
    for n, g in zip(mats, g_mats):
        if n in COL_SHARDED:
            wb[n] = jnp.transpose(g, (1, 0, 2)).reshape(g.shape[1], -1)
        else:
            wb[n] = g.reshape(-1, g.shape[2])
    wb['conv_w_f32'] = jnp.transpose(g_conv, (1, 0, 2)).reshape(CONV_K, -1)
    small = {n: w_loc[n] for n in vecs}

    loss_part, grad_x, grads = _local_step(x, c, positions, loss_target, wb, small)
    loss = lax.psum(loss_part, ("x", "y", "c"))

    stacks = []
    for n in mats:
        kk, nn = w_loc[n].shape[1:]
        if n in COL_SHARDED:
            stacks.append(jnp.transpose(grads[n].reshape(kk, 4, nn), (1, 0, 2)))
        else:
            stacks.append(grads[n].reshape(4, kk, nn))
    g_mine, g_other = _reduce_matrices(stacks, mats)
    g_shard = {}

    vec_shapes = [tuple(grads[n].shape) for n in vecs] + [tuple(grads['conv_w'].shape)]
    total = _stack_sum(_gather_small(_pack_small([grads[n] for n in vecs] + [grads['conv_w']])), "grad_sum_small")
    g_vec = _unpack_small(total, vec_shapes)
    n_conv = conv_w.shape[2]
    chip = 2 * lax.axis_index("x") + lax.axis_index("y")
    g_shard['conv_w'] = lax.dynamic_slice_in_dim(g_vec[-1], chip * n_conv, n_conv, axis=1)
    for n, g in zip(vecs, g_vec):
        g_shard[n] = g

    delta, new_m, new_v = {}, {}, {}
    cidx = lax.axis_index("c").astype(jnp.int32).reshape(1)
    for n, mine, other in zip(mats, g_mine, g_other):
        g_shard[n], delta[n], new_m[n], new_v[n] = _adam_halves_call(
            w_loc[n][0], mine, other, cidx, m_loc[n][0], v_loc[n][0], "adamw_" + n)
    rest = vecs + ['conv_w']
    rest_shapes = [tuple(w_loc[n].shape) for n in rest]
    packed = [_pack_small([src[n] for n in rest]) for src in (w_loc, g_shard, m_loc, v_loc)]
    for dst, buf in zip((delta, new_m, new_v), _adam_call(*packed, "adamw_small")):
        dst.update(zip(rest, _unpack_small(buf, rest_shapes)))

    def out(d):
        return [d[n].reshape(w_loc[n].shape) for n in WEIGHTS]

    return (loss, grad_x, *out(g_shard), *out(delta), *out(new_m), *out(new_v))
```

```python
import functools
import math

import numpy as np
import jax
import jax.numpy as jnp
from jax import lax
from jax.experimental import pallas as pl
from jax.experimental.pallas import tpu as pltpu

F32 = jnp.float32
BF16 = jnp.bfloat16
MESH = pl.DeviceIdType.MESH

D_MODEL = 1024
N_HEADS = 8
NOPE = 128
ROPE = 64
V_DIM = 128
Q_RANK = 256
KV_RANK = 256
ROPE_THETA = 10000.0
D_INNER = 2048
SSM_HEADS = 32
SSM_GROUPS = 8
HEAD_P = 64
STATE_N = 128
CONV_K = 4
CHUNK = 128
CONV_CH = D_INNER + 2 * SSM_GROUPS * STATE_N
D_FF = 4096
EPS = 1e-6
IN_SIZES = (Q_RANK, KV_RANK, ROPE, D_INNER, CONV_CH, SSM_HEADS, D_MODEL, D_MODEL)
ADAM_LR, ADAM_B1, ADAM_B2, ADAM_EPS, ADAM_WD, ADAM_STEP = 0.001, 0.9, 0.999, 1e-08, 0.01, 10

VMEM_LIMIT_BYTES = 52 * 1024 * 1024
LANE = 128
QK_PAD = 256

WEIGHTS = ['w_ada', 'b_ada', 'g_pre_mix', 'g_post_mix', 'w_in', 'g_q_lat', 'g_kv_lat', 'w_uq', 'w_ukv',
           'w_o_attn', 'conv_w', 'conv_b', 'dt_bias', 'a_log', 'd_skip', 'g_ssm_out', 'w_o_ssm', 'w_out',
           'g_pre_mlp', 'g_post_mlp', 'w_ff1', 'w_ff2']
COL_SHARDED = ('w_ada', 'w_in', 'w_uq', 'w_ukv', 'conv_w', 'w_ff1')
ROW_SHARDED = ('w_o_attn', 'w_o_ssm', 'w_out', 'w_ff2')


def _cparams(sem):
    return pltpu.CompilerParams(dimension_semantics=sem, vmem_limit_bytes=VMEM_LIMIT_BYTES)


def _tile(n, cap):
    if n <= cap:
        return n
    k = n // LANE
    best = LANE
    for d in range(1, k + 1):
        if k % d == 0 and d * LANE <= cap:
            best = d * LANE
    return best


def _mm(a, w, name, out_dtype=F32, epilogue=None, extras=(), out_dtypes=None):
    M, K = a.shape
    N = w.shape[1]
    tm = min(M, 1024)
    tn = _tile(N, 1024)
    tk = _tile(K, 2048)
    nk = K // tk
    dts = tuple(out_dtypes) if epilogue is not None else (out_dtype,)
    n_x, n_o = len(extras), len(dts)

    def finish(acc, refs):
        res = epilogue(acc, *[r[...] for r in refs[:n_x]]) if epilogue is not None else (acc,)
        for o_ref, val, dt in zip(refs[n_x:n_x + n_o], res, dts):
            o_ref[...] = val.astype(dt)

    def body(a_ref, w_ref, *refs):
        part = jnp.dot(a_ref[...].astype(BF16), w_ref[...], preferred_element_type=F32)
        if nk == 1:
            finish(part, refs)
        else:
            acc_ref = refs[-1]
            k = pl.program_id(2)

            @pl.when(k == 0)
            def _():
                acc_ref[...] = part

            @pl.when(k > 0)
            def _():
                acc_ref[...] += part

            @pl.when(k == nk - 1)
            def _():
                finish(acc_ref[...], refs)

    ospec = pl.BlockSpec((tm, tn), lambda i, j, k: (i, j))
    res = pl.pallas_call(
        body, grid=(M // tm, N // tn, nk),
        in_specs=[pl.BlockSpec((tm, tk), lambda i, j, k: (i, k)), pl.BlockSpec((tk, tn), lambda i, j, k: (k, j))]
        + [ospec] * n_x,
        out_specs=[ospec] * n_o, out_shape=[jax.ShapeDtypeStruct((M, N), dt) for dt in dts],
        scratch_shapes=[pltpu.VMEM((tm, tn), F32)] if nk > 1 else [], name=name,
        compiler_params=_cparams(("parallel", "parallel", "arbitrary")))(a, w, *extras)
    return res if epilogue is not None else res[0]


def _mm_tn(a, g, name):
    M, K = a.shape
    N = g.shape[1]
    tm = min(M, 1024)
    tk = _tile(K, 1024)
    tn = _tile(N, 1024)
    nm = M // tm

    def body(a_ref, g_ref, o_ref):
        part = lax.dot_general(a_ref[...].astype(BF16), g_ref[...].astype(BF16), (((0,), (0,)), ((), ())),
                               preferred_element_type=F32)
        m = pl.program_id(2)

        @pl.when(m == 0)
        def _():
            o_ref[...] = part

        @pl.when(m > 0)
        def _():
            o_ref[...] += part

    return pl.pallas_call(
        body, grid=(K // tk, N // tn, nm),
        in_specs=[pl.BlockSpec((tm, tk), lambda i, j, m: (m, i)), pl.BlockSpec((tm, tn), lambda i, j, m: (m, j))],
        out_specs=pl.BlockSpec((tk, tn), lambda i, j, m: (i, j)),
        out_shape=jax.ShapeDtypeStruct((K, N), F32), name=name,
        compiler_params=_cparams(("parallel", "parallel", "arbitrary")))(a, g)


def make_linear(name, out_dtype=F32):
    @jax.custom_vjp
    def linear(a, w, tok):
        return _mm(a, w, name + "_fwd", out_dtype)

    def fwd(a, w, tok):
        return _mm(a, w, name + "_fwd", out_dtype), (a, w)

    def bwd(res, g):
        a, w = res
        da = _mm(g, w.T, name + "_dx", a.dtype)
        dw = _mm_tn(a, g, name + "_dw")
        return da, jnp.zeros_like(w), dw

    linear.defvjp(fwd, bwd)
    return linear


def _relu2_epilogue(acc):
    r = jnp.maximum(acc, 0.0)
    return r * r, r


def _relu2_bwd_epilogue(acc, r):
    return (acc * (2.0 * r.astype(F32)),)


@jax.custom_vjp
def ffn(h, w1, tok1, w2, tok2):
    act, _ = _mm(h, w1, "w_ff1_fwd", epilogue=_relu2_epilogue, out_dtypes=(BF16, BF16))
    return _mm(act, w2, "w_ff2_fwd")


def _ffn_fwd(h, w1, tok1, w2, tok2):
    act, r = _mm(h, w1, "w_ff1_fwd", epilogue=_relu2_epilogue, out_dtypes=(BF16, BF16))
    return _mm(act, w2, "w_ff2_fwd"), (h, w1, w2, act, r)


def _ffn_bwd(res, g):
    h, w1, w2, act, r = res
    du = _mm(g, w2.T, "w_ff2_dx", epilogue=_relu2_bwd_epilogue, extras=(r,), out_dtypes=(BF16,))[0]
    dw2 = _mm_tn(act, g, "w_ff2_dw")
    dw1 = _mm_tn(h, du, "w_ff1_dw")
    dh = _mm(du, w1.T, "w_ff1_dx", h.dtype)
    return dh, jnp.zeros_like(w1), dw1, jnp.zeros_like(w2), dw2


ffn.defvjp(_ffn_fwd, _ffn_bwd)


def make_rowwise(name, f, n_rows, n_seqs, n_pars, out_kinds, ncol=1, nodiff=(), ts_cap=512, windows=None):
    windows = dict(windows or {})
    n_in = n_rows + n_seqs + n_pars
    diff_idx = [i for i in range(n_in) if i not in nodiff]

    def _dims(rows):
        B, S = rows[0].shape[0], rows[0].shape[1]
        ts = min(S, ts_cap)
        return B, S, ts

    def _width(i, r):
        return windows[i][1] if i in windows else r.shape[2]

    def _in_specs(rows, seqs, pars, ts):
        specs = []
        for i, r in enumerate(rows):
            col0 = windows[i][0] if i in windows else 0
            specs.append(pl.BlockSpec((1, ts, _width(i, r) // ncol), lambda k, b, s, col0=col0: (b, s, k + col0)))
        for q in seqs:
            specs.append(pl.BlockSpec((1, 1, q.shape[2] // ncol), lambda k, b, s: (b, 0, k)))
        for p in pars:
            specs.append(pl.BlockSpec((1, p.shape[1] // ncol), lambda k, b, s: (0, k)))
        return specs

    def _load(refs):
        vals = [r[0] for r in refs[:n_rows + n_seqs]]
        vals += [r[...] for r in refs[n_rows + n_seqs:n_in]]
        return vals

    def _out_struct(rows, seqs, pars, ts):
        blocks = [jax.ShapeDtypeStruct((ts, _width(i, r) // ncol), r.dtype) for i, r in enumerate(rows)]
        blocks += [jax.ShapeDtypeStruct((1, q.shape[2] // ncol), q.dtype) for q in seqs]
        blocks += [jax.ShapeDtypeStruct((1, p.shape[1] // ncol), p.dtype) for p in pars]
        return jax.eval_shape(f, *blocks)

    def _fwd_call(rows, seqs, pars):
        B, S, ts = _dims(rows)
        outs = _out_struct(rows, seqs, pars, ts)
        n_out = len(outs)

        def body(*refs):
            res = f(*_load(refs))
            first = (pl.program_id(1) == 0) & (pl.program_id(2) == 0)
            for o_ref, val, kind in zip(refs[n_in:], res, out_kinds):
                if kind == 'row':
                    o_ref[0] = val
                else:
                    tot = jnp.sum(val, axis=0, keepdims=True)

                    @pl.when(first)
                    def _(o_ref=o_ref, tot=tot):
                        o_ref[...] = tot

                    @pl.when(jnp.logical_not(first))
                    def _(o_ref=o_ref, tot=tot):
                        o_ref[...] += tot

        out_shape, out_specs = [], []
        for o, kind in zip(outs, out_kinds):
            d = o.shape[1]
            if kind == 'row':
                out_shape.append(jax.ShapeDtypeStruct((B, S, ncol * d), o.dtype))
                out_specs.append(pl.BlockSpec((1, ts, d), lambda k, b, s: (b, s, k)))
            else:
                out_shape.append(jax.ShapeDtypeStruct((1, ncol * d), o.dtype))
                out_specs.append(pl.BlockSpec((1, d), lambda k, b, s: (0, k)))
        res = pl.pallas_call(
            body, grid=(ncol, B, S // ts), in_specs=_in_specs(rows, seqs, pars, ts), out_specs=out_specs,
            out_shape=out_shape, name=name + "_fwd",
            compiler_params=_cparams(("arbitrary", "arbitrary", "arbitrary")))(*rows, *seqs, *pars)
        return tuple(res)

    def _bwd_call(rows, seqs, pars, cots):
        B, S, ts = _dims(rows)
        outs = _out_struct(rows, seqs, pars, ts)
        n_out = len(outs)
        all_in = list(rows) + list(seqs) + list(pars)

        def body(*refs):
            vals = _load(refs)
            cts = []
            for c_ref, o, kind in zip(refs[n_in:n_in + n_out], outs, out_kinds):
                if kind == 'row':
                    cts.append(c_ref[0])
                else:
                    cts.append(jnp.broadcast_to(c_ref[...], o.shape))

            def g(*dv):
                full = list(vals)
                for i, v in zip(diff_idx, dv):
                    full[i] = v
                return tuple(f(*full))

            _, vjp = jax.vjp(g, *[vals[i] for i in diff_idx])
            grads = vjp(tuple(cts))
            b, s = pl.program_id(1), pl.program_id(2)
            for o_ref, i, gr in zip(refs[n_in + n_out:], diff_idx, grads):
                if i < n_rows:
                    o_ref[0] = gr.astype(o_ref.dtype)
                else:
                    first = (s == 0) if i < n_rows + n_seqs else ((b == 0) & (s == 0))
                    target = (lambda r: r.at[0]) if i < n_rows + n_seqs else (lambda r: r)

                    @pl.when(first)
                    def _(o_ref=o_ref, gr=gr, target=target):
                        target(o_ref)[...] = gr

                    @pl.when(jnp.logical_not(first))
                    def _(o_ref=o_ref, gr=gr, target=target):
                        target(o_ref)[...] += gr

        cot_specs = []
        for o, kind in zip(outs, out_kinds):
            d = o.shape[1]
            if kind == 'row':
                cot_specs.append(pl.BlockSpec((1, ts, d), lambda k, b, s: (b, s, k)))
            else:
                cot_specs.append(pl.BlockSpec((1, d), lambda k, b, s: (0, k)))
        out_shape, out_specs = [], []
        for i in diff_idx:
            a = all_in[i]
            if i < n_rows:
                out_shape.append(jax.ShapeDtypeStruct((B, S, _width(i, a)), BF16 if i in windows else a.dtype))
                out_specs.append(pl.BlockSpec((1, ts, _width(i, a) // ncol), lambda k, b, s: (b, s, k)))
                continue
            out_shape.append(jax.ShapeDtypeStruct(a.shape, a.dtype))
            if i < n_rows + n_seqs:
                out_specs.append(pl.BlockSpec((1, 1, a.shape[2] // ncol), lambda k, b, s: (b, 0, k)))
            else:
                out_specs.append(pl.BlockSpec((1, a.shape[1] // ncol), lambda k, b, s: (0, k)))
        res = pl.pallas_call(
            body, grid=(ncol, B, S // ts), in_specs=_in_specs(rows, seqs, pars, ts) + cot_specs,
            out_specs=out_specs, out_shape=out_shape, name=name + "_bwd",
            compiler_params=_cparams(("arbitrary", "arbitrary", "arbitrary")))(*all_in, *cots)
        grads = [None] * n_in
        for i, r in zip(diff_idx, res):
            grads[i] = r
        for i in nodiff:
            grads[i] = jnp.zeros_like(all_in[i])
        stand_in_grads = tuple(grads[i] for i in sorted(windows))
        for i in windows:
            grads[i] = jnp.zeros_like(all_in[i])
        return (tuple(grads[:n_rows]), tuple(grads[n_rows:n_rows + n_seqs]), tuple(grads[n_rows + n_seqs:]),
                stand_in_grads)

    @jax.custom_vjp
    def op(rows, seqs, pars, stand_ins):
        return _fwd_call(rows, seqs, pars)

    def fwd(rows, seqs, pars, stand_ins):
        return _fwd_call(rows, seqs, pars), (rows, seqs, pars)

    def bwd(res, cots):
        rows, seqs, pars = res
        return _bwd_call(rows, seqs, pars, cots)

    op.defvjp(fwd, bwd)
    return lambda rows, seqs, pars, stand_ins=(): op(tuple(rows), tuple(seqs), tuple(pars), tuple(stand_ins))


def _rms(x, g):
    return x * lax.rsqrt(jnp.mean(x * x, axis=-1, keepdims=True) + EPS) * g


def _silu(x):
    return x * lax.logistic(x)


def _f_silu(c):
    return (_silu(c),)


def _f_modulate(x, scale, shift, g):
    return ((_rms(x, g) * (1.0 + scale) + shift).astype(BF16),)


def _f_rms(x, g):
    return (_rms(x, g).astype(BF16),)


def _f_dt(dt_raw, dt_bias, a_log):
    z = dt_raw + dt_bias
    dt = jnp.maximum(z, 0.0) + jnp.log1p(jnp.exp(-jnp.abs(z)))
    return dt, dt * (-jnp.exp(a_log))


def _f_gated_norm(y, z, g):
    return (_rms(y * _silu(z.astype(F32)), g).astype(BF16),)


def _f_merge(attn, ssm, ga, gb):
    return ((lax.logistic(ga.astype(F32)) * attn + lax.logistic(gb.astype(F32)) * ssm).astype(BF16),)


def _f_post(x, m, gate, g):
    return (x + gate * _rms(m, g),)


def _f_final_loss(x, ff, target, gate, g):
    e = x + gate * _rms(ff, g) - target
    return (e * e * (0.5 / D_MODEL),)


def _rope_tables(posf, inv_lane):
    B, S, _ = posf.shape
    ts = min(S, 512)

    def body(p_ref, inv_ref, c_ref, a_ref, b_ref):
        ang = p_ref[0] * inv_ref[...]
        cs, sn = jnp.cos(ang), jnp.sin(ang)
        lane = lax.broadcasted_iota(jnp.int32, ang.shape, 1)
        c_ref[0] = jnp.where(lane < ROPE, cs, 0.0)
        a_ref[0] = jnp.where(lane < ROPE // 2, -sn, 0.0)
        b_ref[0] = jnp.where((lane >= ROPE // 2) & (lane < ROPE), sn, 0.0)

    spec = pl.BlockSpec((1, ts, LANE), lambda b, s: (b, s, 0))
    sds = jax.ShapeDtypeStruct((B, S, LANE), F32)
    return pl.pallas_call(
        body, grid=(B, S // ts),
        in_specs=[pl.BlockSpec((1, ts, 1), lambda b, s: (b, s, 0)), pl.BlockSpec((1, LANE), lambda b, s: (0, 0))],
        out_specs=[spec, spec, spec], out_shape=[sds, sds, sds], name="rope_tables",
        compiler_params=_cparams(("parallel", "parallel")))(posf, inv_lane)


def _rot(u, c, a, bm):
    return u * c + pltpu.roll(u, 96, 1) * a + pltpu.roll(u, 32, 1) * bm


def _rot_t(g, c, a, bm):
    return g * c + pltpu.roll(g * a, 32, 1) + pltpu.roll(g * bm, 96, 1)


def _rope_q_call(q, tabs, transpose, name):
    B, S, W = q.shape
    ts = min(S, 512)
    fn = _rot_t if transpose else _rot
    out_dtype = F32 if transpose else BF16

    def body(q_ref, c_ref, a_ref, b_ref, o_ref):
        tc, ta, tb = c_ref[0], a_ref[0], b_ref[0]
        for h in range(W // QK_PAD):
            u = q_ref[0, :, h * QK_PAD:(h + 1) * QK_PAD].astype(F32) * ATT_SCALE
            r = fn(u[:, NOPE:], tc, ta, tb)
            o_ref[0, :, h * QK_PAD:(h + 1) * QK_PAD] = jnp.concatenate([u[:, :NOPE], r], axis=1).astype(out_dtype)

    tspec = pl.BlockSpec((1, ts, LANE), lambda b, s: (b, s, 0))
    qspec = pl.BlockSpec((1, ts, W), lambda b, s: (b, s, 0))
    return pl.pallas_call(
        body, grid=(B, S // ts), in_specs=[qspec, tspec, tspec, tspec], out_specs=qspec,
        out_shape=jax.ShapeDtypeStruct(q.shape, out_dtype), name=name,
        compiler_params=_cparams(("parallel", "parallel")))(q, *tabs)


@jax.custom_vjp
def rope_q(q, tabs):
    return _rope_q_call(q, tabs, False, "rope_q_fwd")


def _rope_q_fwd(q, tabs):
    return _rope_q_call(q, tabs, False, "rope_q_fwd"), tabs


def _rope_q_bwd(tabs, g):
    return _rope_q_call(g, tabs, True, "rope_q_bwd"), tuple(jnp.zeros_like(t) for t in tabs)


rope_q.defvjp(_rope_q_fwd, _rope_q_bwd)


def _build_k_fwd_call(kv, kr, tabs):
    B, S, _ = kv.shape
    ts = min(S, 512)

    def body(kv_ref, kr_ref, c_ref, a_ref, b_ref, o_ref):
        r = _rot(kr_ref[0], c_ref[0], a_ref[0], b_ref[0]).astype(BF16)
        for h in range(N_HEADS):
            o_ref[0, :, h * QK_PAD:(h + 1) * QK_PAD] = jnp.concatenate(
                [kv_ref[0, :, h * NOPE:(h + 1) * NOPE], r], axis=1)

    tspec = pl.BlockSpec((1, ts, LANE), lambda b, s: (b, s, 0))
    kr_spec = pl.BlockSpec((1, ts, LANE), lambda b, s: (b, s, KR_LANE0 // LANE))
    return pl.pallas_call(
        body, grid=(B, S // ts),
        in_specs=[pl.BlockSpec((1, ts, N_HEADS * NOPE), lambda b, s: (b, s, 0)), kr_spec, tspec, tspec, tspec],
        out_specs=pl.BlockSpec((1, ts, N_HEADS * QK_PAD), lambda b, s: (b, s, 0)),
        out_shape=jax.ShapeDtypeStruct((B, S, N_HEADS * QK_PAD), BF16), name="build_k_fwd",
        compiler_params=_cparams(("parallel", "parallel")))(kv, kr, *tabs)


def _build_k_bwd_call(g, tabs):
    B, S, _ = g.shape
    ts = min(S, 512)

    def body(g_ref, c_ref, a_ref, b_ref, dk_ref, dr_ref):
        tot = None
        for h in range(N_HEADS):
            dk_ref[0, :, h * NOPE:(h + 1) * NOPE] = g_ref[0, :, h * QK_PAD:h * QK_PAD + NOPE]
            part = g_ref[0, :, h * QK_PAD + NOPE:(h + 1) * QK_PAD].astype(F32)
            tot = part if tot is None else tot + part
        dr_ref[0] = _rot_t(tot, c_ref[0], a_ref[0], b_ref[0]).astype(BF16)

    tspec = pl.BlockSpec((1, ts, LANE), lambda b, s: (b, s, 0))
    return pl.pallas_call(
        body, grid=(B, S // ts),
        in_specs=[pl.BlockSpec((1, ts, N_HEADS * QK_PAD), lambda b, s: (b, s, 0)), tspec, tspec, tspec],
        out_specs=[pl.BlockSpec((1, ts, N_HEADS * NOPE), lambda b, s: (b, s, 0)), tspec],
        out_shape=[jax.ShapeDtypeStruct((B, S, N_HEADS * NOPE), BF16), jax.ShapeDtypeStruct((B, S, LANE), BF16)],
        name="build_k_bwd", compiler_params=_cparams(("parallel", "parallel")))(g, *tabs)


@jax.custom_vjp
def build_k(kv, src, stand_in, tabs):
    return _build_k_fwd_call(kv, src, tabs)


def _build_k_fwd(kv, src, stand_in, tabs):
    return _build_k_fwd_call(kv, src, tabs), (tabs, kv.shape, src)


def _build_k_bwd(res, g):
    tabs, kv_shape, src = res
    dk, dr = _build_k_bwd_call(g, tabs)
    dkv = jnp.concatenate([dk, jnp.zeros((kv_shape[0], kv_shape[1], kv_shape[2] - dk.shape[2]), BF16)], axis=-1)
    return dkv, jnp.zeros_like(src), dr, tuple(jnp.zeros_like(t) for t in tabs)


build_k.defvjp(_build_k_fwd, _build_k_bwd)


ATT_SCALE = (NOPE + ROPE) ** -0.5
NEG = -1e30


def _att_tiles(S):
    t = min(S, 512)
    return t, S // t


def _scores(q, k, diagonal):
    s = lax.dot_general(q, k, (((1,), (1,)), ((), ())), preferred_element_type=F32)
    if diagonal:
        row = lax.broadcasted_iota(jnp.int32, s.shape, 0)
        col = lax.broadcasted_iota(jnp.int32, s.shape, 1)
        s = jnp.where(col <= row, s, NEG)
    return s


ATT_HB = 4


def _causal_pairs(n):
    pairs = [(i, j) for i in range(n) for j in range(i + 1)]
    return (jnp.asarray([p[0] for p in pairs], jnp.int32), jnp.asarray([p[1] for p in pairs], jnp.int32))


def _head(ref_or_val, h, w):
    return ref_or_val[:, h * w:(h + 1) * w]


def _attn_fwd_call(q, k, vsrc, v_blk0):
    B, S, _ = q.shape
    t, n = _att_tiles(S)
    qi, kj = _causal_pairs(n)

    def body(qi_ref, kj_ref, q_ref, k_ref, v_ref, o_ref, lse_ref, m_sc, l_sc, acc_sc):
        p_id = pl.program_id(2)
        i, j = qi_ref[p_id], kj_ref[p_id]

        @pl.when(j == 0)
        def _():
            m_sc[...] = jnp.full(m_sc.shape, NEG, F32)
            l_sc[...] = jnp.zeros(l_sc.shape, F32)
            acc_sc[...] = jnp.zeros(acc_sc.shape, F32)

        def step(diagonal):
            qa, ka, va = q_ref[0], k_ref[0], v_ref[0]
            for h in range(ATT_HB):
                lanes = slice(h * LANE, (h + 1) * LANE)
                s = _scores(_head(qa, h, QK_PAD), _head(ka, h, QK_PAD), diagonal)
                m_prev = m_sc[:, lanes]
                m_new = jnp.maximum(m_prev, jnp.max(s, axis=1, keepdims=True))
                alpha = jnp.exp(m_prev - m_new)
                p = jnp.exp(s - jnp.tile(m_new, (1, t // LANE)))
                l_sc[:, lanes] = alpha * l_sc[:, lanes] + jnp.sum(p, axis=1, keepdims=True)
                acc_sc[:, lanes] = alpha * acc_sc[:, lanes] + jnp.dot(p.astype(BF16), _head(va, h, V_DIM),
                                                                      preferred_element_type=F32)
                m_sc[:, lanes] = m_new

        @pl.when(j < i)
        def _():
            step(False)

        @pl.when(j == i)
        def _():
            step(True)
            o_ref[0] = acc_sc[...] / l_sc[...]
            lse_ref[0] = m_sc[...] + jnp.log(l_sc[...])

    wq, wv = ATT_HB * QK_PAD, ATT_HB * V_DIM
    grid_spec = pltpu.PrefetchScalarGridSpec(
        num_scalar_prefetch=2, grid=(B, N_HEADS // ATT_HB, qi.shape[0]),
        in_specs=[pl.BlockSpec((1, t, wq), lambda b, h, p, qi, kj: (b, qi[p], h)),
                  pl.BlockSpec((1, t, wq), lambda b, h, p, qi, kj: (b, kj[p], h)),
                  pl.BlockSpec((1, t, wv), lambda b, h, p, qi, kj: (b, kj[p], v_blk0 + h))],
        out_specs=[pl.BlockSpec((1, t, wv), lambda b, h, p, qi, kj: (b, qi[p], h)),
                   pl.BlockSpec((1, t, wv), lambda b, h, p, qi, kj: (b, qi[p], h))],
        scratch_shapes=[pltpu.VMEM((t, wv), F32), pltpu.VMEM((t, wv), F32), pltpu.VMEM((t, wv), F32)])
    return pl.pallas_call(
        body, grid_spec=grid_spec,
        out_shape=[jax.ShapeDtypeStruct((B, S, N_HEADS * V_DIM), F32),
                   jax.ShapeDtypeStruct((B, S, N_HEADS * LANE), F32)],
        name="attn_fwd", compiler_params=_cparams(("parallel", "parallel", "arbitrary")))(qi, kj, q, k, vsrc)


def _attn_p_ds(q, k, v, o, do, lse, diagonal, t):
    s = _scores(q, k, diagonal)
    p = jnp.exp(s - jnp.tile(lse, (1, t // LANE)))
    dp = lax.dot_general(do.astype(BF16), v, (((1,), (1,)), ((), ())), preferred_element_type=F32)
    delta = jnp.sum(do * o, axis=1, keepdims=True)
    ds = p * (dp - delta)
    return p, ds


ATT_HB_BWD = 2


def _attn_bwd_call(q, k, vsrc, o, do, lse):
    B, S, _ = q.shape
    t, n = _att_tiles(S)
    qi, kj = _causal_pairs(n)
    n_pairs = qi.shape[0]
    hb = ATT_HB_BWD
    v_blk0 = N_HEADS // hb

    def body(qi_ref, kj_ref, q_ref, k_ref, v_ref, o_ref, do_ref, lse_ref, dq_ref, dk_ref, dv_ref, dq_sc, dk_sc, dv_sc):
        p_id = pl.program_id(2)
        i, j = qi_ref[p_id], kj_ref[p_id]

        @pl.when(p_id == 0)
        def _():
            dk_sc[...] = jnp.zeros(dk_sc.shape, F32)
            dv_sc[...] = jnp.zeros(dv_sc.shape, F32)

        @pl.when(j == 0)
        def _():
            dq_sc[...] = jnp.zeros(dq_sc.shape, F32)

        rows = pl.ds(pl.multiple_of(j * t, t), t)

        def step(diagonal):
            qa, ka, va, oa, doa, la = q_ref[0], k_ref[0], v_ref[0], o_ref[0], do_ref[0], lse_ref[0]
            for h in range(hb):
                qb, kb, dob = _head(qa, h, QK_PAD), _head(ka, h, QK_PAD), _head(doa, h, V_DIM)
                p, ds = _attn_p_ds(qb, kb, _head(va, h, V_DIM), _head(oa, h, V_DIM), dob, _head(la, h, LANE),
                                   diagonal, t)
                dsb = ds.astype(BF16)
                dq_sc[:, h * QK_PAD:(h + 1) * QK_PAD] += jnp.dot(dsb, kb, preferred_element_type=F32)
                dv_sc[rows, h * V_DIM:(h + 1) * V_DIM] += lax.dot_general(
                    p.astype(BF16), dob.astype(BF16), (((0,), (0,)), ((), ())), preferred_element_type=F32)
                dk_sc[rows, h * QK_PAD:(h + 1) * QK_PAD] += lax.dot_general(
                    dsb, qb, (((0,), (0,)), ((), ())), preferred_element_type=F32)

        @pl.when(j < i)
        def _():
            step(False)

        @pl.when(j == i)
        def _():
            step(True)
            dq_ref[0] = dq_sc[...].astype(BF16)

        @pl.when(p_id == n_pairs - 1)
        def _():
            dk_ref[0] = dk_sc[...].astype(BF16)
            dv_ref[0] = dv_sc[...].astype(BF16)

    wq, wv = hb * QK_PAD, hb * V_DIM
    at_q = lambda b, h, p, qi, kj: (b, qi[p], h)
    at_k = lambda b, h, p, qi, kj: (b, kj[p], h)
    whole = lambda b, h, p, qi, kj: (b, 0, h)
    grid_spec = pltpu.PrefetchScalarGridSpec(
        num_scalar_prefetch=2, grid=(B, N_HEADS // hb, n_pairs),
        in_specs=[pl.BlockSpec((1, t, wq), at_q), pl.BlockSpec((1, t, wq), at_k),
                  pl.BlockSpec((1, t, wv), lambda b, h, p, qi, kj: (b, kj[p], v_blk0 + h)),
                  pl.BlockSpec((1, t, wv), at_q), pl.BlockSpec((1, t, wv), at_q), pl.BlockSpec((1, t, wv), at_q)],
        out_specs=[pl.BlockSpec((1, t, wq), at_q), pl.BlockSpec((1, S, wq), whole), pl.BlockSpec((1, S, wv), whole)],
        scratch_shapes=[pltpu.VMEM((t, wq), F32), pltpu.VMEM((S, wq), F32), pltpu.VMEM((S, wv), F32)])
    return pl.pallas_call(
        body, grid_spec=grid_spec,
        out_shape=[jax.ShapeDtypeStruct((B, S, N_HEADS * QK_PAD), BF16),
                   jax.ShapeDtypeStruct((B, S, N_HEADS * QK_PAD), BF16),
                   jax.ShapeDtypeStruct((B, S, N_HEADS * V_DIM), BF16)],
        name="attn_bwd", compiler_params=_cparams(("parallel", "parallel", "arbitrary")))(
            qi, kj, q, k, vsrc, o, do, lse)


@jax.custom_vjp
def attention(q, k, kv):
    return _attn_fwd_call(q, k, kv, N_HEADS // ATT_HB)[0]


def _attention_fwd(q, k, kv):
    o, lse = _attn_fwd_call(q, k, kv, N_HEADS // ATT_HB)
    return o, (q, k, kv, o, lse)


def _attention_bwd(res, do):
    q, k, kv, o, lse = res
    dq, dk, dv = _attn_bwd_call(q, k, kv, o, do, lse)
    dkv = jnp.concatenate([jnp.zeros_like(dv), dv], axis=-1)
    return dq, dk, dkv


attention.defvjp(_attention_fwd, _attention_bwd)


SUBLANES = 8


def _zero_tail(v):
    return jnp.concatenate([v, jnp.zeros((SUBLANES, v.shape[1]), v.dtype)], axis=0)


def _shift_down(vz, sh):
    return pltpu.roll(vz, sh, 0)[:vz.shape[0] - SUBLANES]


def _shift_up(vz, sh):
    return pltpu.roll(vz, vz.shape[0] - sh, 0)[:vz.shape[0] - SUBLANES]


def _conv_pre(u, uz, w_ref, b_ref):
    acc = b_ref[...] + w_ref[pl.ds(CONV_K - 1, 1), :] * u
    for k in range(CONV_K - 1):
        acc = acc + w_ref[pl.ds(k, 1), :] * _shift_down(uz, CONV_K - 1 - k)
    return acc


def _conv_fwd_call(src, w, b):
    B, S, _ = src.shape
    C = w.shape[1]

    def body(u_ref, w_ref, b_ref, o_ref):
        uu = u_ref[0].astype(F32)
        o_ref[0] = _silu(_conv_pre(uu, _zero_tail(uu), w_ref, b_ref))

    spec = pl.BlockSpec((1, S, LANE), lambda c, bb: (bb, 0, c))
    return pl.pallas_call(
        body, grid=(C // LANE, B),
        in_specs=[pl.BlockSpec((1, S, LANE), lambda c, bb: (bb, 0, c + CONV_LANE0 // LANE)),
                  pl.BlockSpec((CONV_K, LANE), lambda c, bb: (0, c)), pl.BlockSpec((1, LANE), lambda c, bb: (0, c))],
        out_specs=spec, out_shape=jax.ShapeDtypeStruct((B, S, C), F32), name="conv_fwd",
        compiler_params=_cparams(("parallel", "arbitrary")))(src, w, b)


def _conv_bwd_call(src, w, b, g):
    B, S, _ = src.shape
    C = w.shape[1]

    def body(u_ref, w_ref, b_ref, g_ref, du_ref, dw_ref, db_ref):
        uu = u_ref[0].astype(F32)
        uz = _zero_tail(uu)
        pre = _conv_pre(uu, uz, w_ref, b_ref)
        sg = lax.logistic(pre)
        dpre = g_ref[0] * sg * (1.0 + pre * (1.0 - sg))
        dz = _zero_tail(dpre)
        du = w_ref[pl.ds(CONV_K - 1, 1), :] * dpre
        dws = [None] * CONV_K
        dws[CONV_K - 1] = jnp.sum(dpre * uu, axis=0, keepdims=True)
        for k in range(CONV_K - 1):
            sh = CONV_K - 1 - k
            du = du + w_ref[pl.ds(k, 1), :] * _shift_up(dz, sh)
            dws[k] = jnp.sum(dpre * _shift_down(uz, sh), axis=0, keepdims=True)
        du_ref[0] = du.astype(du_ref.dtype)
        dbv = jnp.sum(dpre, axis=0, keepdims=True)
        first = pl.program_id(1) == 0

        @pl.when(first)
        def _():
            for k in range(CONV_K):
                dw_ref[pl.ds(k, 1), :] = dws[k]
            db_ref[...] = dbv

        @pl.when(jnp.logical_not(first))
        def _():
            for k in range(CONV_K):
                dw_ref[pl.ds(k, 1), :] += dws[k]
            db_ref[...] += dbv

    spec = pl.BlockSpec((1, S, LANE), lambda c, bb: (bb, 0, c))
    wspec = pl.BlockSpec((CONV_K, LANE), lambda c, bb: (0, c))
    bspec = pl.BlockSpec((1, LANE), lambda c, bb: (0, c))
    uspec = pl.BlockSpec((1, S, LANE), lambda c, bb: (bb, 0, c + CONV_LANE0 // LANE))
    return pl.pallas_call(
        body, grid=(C // LANE, B), in_specs=[uspec, wspec, bspec, spec], out_specs=[spec, wspec, bspec],
        out_shape=[jax.ShapeDtypeStruct((B, S, C), BF16), jax.ShapeDtypeStruct(w.shape, F32),
                   jax.ShapeDtypeStruct(b.shape, F32)],
        name="conv_bwd", compiler_params=_cparams(("parallel", "arbitrary")))(src, w, b, g)


@jax.custom_vjp
def conv_silu(src, stand_in, w, b):
    return _conv_fwd_call(src, w, b)


def _conv_silu_fwd(src, stand_in, w, b):
    return _conv_fwd_call(src, w, b), (src, w, b)


def _conv_silu_bwd(res, g):
    du, dw, db = _conv_bwd_call(*res, g)
    return jnp.zeros_like(res[0]), du, dw, db


conv_silu.defvjp(_conv_silu_fwd, _conv_silu_bwd)


def _chunk_cumsum_call(a, reverse, name):
    B, S, W = a.shape
    per_step = min(S // CHUNK, 8)

    def body(a_ref, o_ref):
        r = lax.broadcasted_iota(jnp.int32, (CHUNK, CHUNK), 0)
        c = lax.broadcasted_iota(jnp.int32, (CHUNK, CHUNK), 1)
        tri = jnp.where((c >= r) if reverse else (c <= r), 1.0, 0.0).astype(F32)
        for i in range(per_step):
            rows = pl.ds(i * CHUNK, CHUNK)
            o_ref[0, rows, :] = jnp.dot(tri, a_ref[0, rows, :], preferred_element_type=F32,
                                        precision=lax.Precision.HIGHEST)

    spec = pl.BlockSpec((1, per_step * CHUNK, W), lambda b, c: (b, c, 0))
    return pl.pallas_call(body, grid=(B, S // (per_step * CHUNK)), in_specs=[spec], out_specs=spec,
                          out_shape=jax.ShapeDtypeStruct(a.shape, F32), name=name,
                          compiler_params=_cparams(("parallel", "parallel")))(a)


@jax.custom_vjp
def chunk_cumsum(a):
    return _chunk_cumsum_call(a, False, "chunk_cumsum_fwd")


chunk_cumsum.defvjp(lambda a: (_chunk_cumsum_call(a, False, "chunk_cumsum_fwd"), None),
                    lambda _, g: (_chunk_cumsum_call(g, True, "chunk_cumsum_bwd"),))


GROUP_W = 4 * HEAD_P
HPG = SSM_HEADS // SSM_GROUPS


def _ssd_masks():
    lane = lax.broadcasted_iota(jnp.int32, (1, GROUP_W), 1)
    return [((lane >= HEAD_P * j) & (lane < HEAD_P * (j + 1))).astype(F32) for j in range(HPG)]


def _ssd_decays(ac_cols, acr_ref, gi):
    r = lax.broadcasted_iota(jnp.int32, (CHUNK, CHUNK), 0)
    c = lax.broadcasted_iota(jnp.int32, (CHUNK, CHUNK), 1)
    return [jnp.exp(jnp.where(c <= r, ac_cols[j] - acr_ref[0, gi * HPG + j], NEG)) for j in range(HPG)]


def _ssd_cols(blk, g):
    lane = lax.broadcasted_iota(jnp.int32, blk.shape, 1)
    return [jnp.sum(jnp.where(lane == HPG * g + j, blk, 0.0), axis=1, keepdims=True) for j in range(HPG)]


def _ssd_spread(cols):
    lane = lax.broadcasted_iota(jnp.int32, (1, GROUP_W), 1)
    out = jnp.broadcast_to(cols[HPG - 1], (CHUNK, GROUP_W))
    for j in range(HPG - 2, -1, -1):
        out = jnp.where(lane < HEAD_P * (j + 1), cols[j], out)
    return out


def _ssd_gather(val, cols, g):
    src = lax.broadcasted_iota(jnp.int32, (GROUP_W, LANE), 0)
    dst = lax.broadcasted_iota(jnp.int32, (GROUP_W, LANE), 1)
    pick = jnp.where(dst == HPG * g + src // HEAD_P, 1.0, 0.0).astype(F32)
    out = jnp.dot(val, pick, preferred_element_type=F32, precision=lax.Precision.HIGH)
    if cols is not None:
        lane = lax.broadcasted_iota(jnp.int32, (1, LANE), 1)
        for j in range(HPG):
            out = out + cols[j] * (lane == HPG * g + j).astype(F32)
    return out


def _dot(a, b, dims):
    return lax.dot_general(a.astype(BF16), b.astype(BF16), (dims, ((), ())), preferred_element_type=F32)


NN = ((1,), (0,))
NT = ((1,), (1,))
TN = ((0,), (0,))


XBC_W = GROUP_W + 2 * STATE_N


SSD_STEP_GROUPS_FWD = 4
SSD_STEP_GROUPS_BWD = 2


def _ssd_load(xbc_ref, dt_ref, ac_ref, masks, g, gi):
    x = xbc_ref[0, :, gi * XBC_W:gi * XBC_W + GROUP_W]
    bm = xbc_ref[0, :, gi * XBC_W + GROUP_W:gi * XBC_W + GROUP_W + STATE_N]
    cm = xbc_ref[0, :, gi * XBC_W + GROUP_W + STATE_N:(gi + 1) * XBC_W]
    ac_cols = _ssd_cols(ac_ref[0], g)
    dt = _ssd_spread(_ssd_cols(dt_ref[0], g))
    ac = _ssd_spread(ac_cols)
    is_last = (lax.broadcasted_iota(jnp.int32, (CHUNK, GROUP_W), 0) == CHUNK - 1).astype(F32)
    return x, bm, cm, dt, ac, ac_cols, is_last


def _ssd_in_specs(nc, rev, gb):
    cc = (lambda c: nc - 1 - c) if rev else (lambda c: c)
    return [pl.BlockSpec((1, CHUNK, gb * XBC_W), lambda b, g, c: (b, cc(c), g)),
            pl.BlockSpec((1, CHUNK, LANE), lambda b, g, c: (b, cc(c), 0)),
            pl.BlockSpec((1, CHUNK, LANE), lambda b, g, c: (b, cc(c), 0)),
            pl.BlockSpec((1, gb * HPG, 1, CHUNK), lambda b, g, c: (b, g, 0, cc(c))),
            pl.BlockSpec((1, gb * GROUP_W), lambda b, g, c: (0, g))]


def _ssd_fwd_call(xbc, dtp, acp, acr, dsk):
    B, S, _ = xbc.shape
    nc = S // CHUNK
    gb = SSD_STEP_GROUPS_FWD

    def body(xbc_ref, dt_ref, ac_ref, ar_ref, ds_ref, y_ref, hp_ref, h_sc):
        @pl.when(pl.program_id(2) == 0)
        def _():
            h_sc[...] = jnp.zeros(h_sc.shape, F32)

        masks = _ssd_masks()
        ys = []
        for gi in range(gb):
            grp = gb * pl.program_id(1) + gi
            x, bm, cm, dt, ac, ac_cols, is_last = _ssd_load(xbc_ref, dt_ref, ac_ref, masks, grp, gi)
            last = jnp.sum(ac * is_last, axis=0, keepdims=True)
            decays = _ssd_decays(ac_cols, ar_ref, gi)
            xd = x * dt
            cb = _dot(cm, bm, NT)
            hprev = h_sc[gi]
            hp_ref[0, gi, 0] = hprev
            y = _dot(cm, hprev, NN) * jnp.exp(ac) + ds_ref[:, gi * GROUP_W:(gi + 1) * GROUP_W] * x
            for j in range(HPG):
                y = y + _dot(cb * decays[j], xd * masks[j], NN)
            ys.append(y)
            h_sc[gi] = hprev * jnp.exp(last) + _dot(bm, xd * jnp.exp(last - ac), TN)
        y_ref[0] = jnp.concatenate(ys, axis=1)

    ng = SSM_GROUPS // gb
    return pl.pallas_call(
        body, grid=(B, ng, nc), in_specs=_ssd_in_specs(nc, False, gb),
        out_specs=[pl.BlockSpec((1, CHUNK, gb * GROUP_W), lambda b, g, c: (b, c, g)),
                   pl.BlockSpec((1, gb, 1, STATE_N, GROUP_W), lambda b, g, c: (b, g, c, 0, 0))],
        out_shape=[jax.ShapeDtypeStruct((B, S, D_INNER), F32),
                   jax.ShapeDtypeStruct((B, SSM_GROUPS, nc, STATE_N, GROUP_W), F32)],
        scratch_shapes=[pltpu.VMEM((gb, STATE_N, GROUP_W), F32)], name="ssd_fwd",
        compiler_params=_cparams(("parallel", "parallel", "arbitrary")))(xbc, dtp, acp, acr, dsk)


def _ssd_bwd_call(xbc, dtp, acp, acr, dsk, hps, dy):
    B, S, _ = xbc.shape
    nc = S // CHUNK
    gb = SSD_STEP_GROUPS_BWD

    def body(xbc_ref, dt_ref, ac_ref, ar_ref, ds_ref, hp_ref, dy_ref,
             dxbc_ref, ddt_ref, dac_ref, dar_ref, dds_ref, dh_sc):
        first = pl.program_id(2) == 0

        @pl.when(first)
        def _():
            dh_sc[...] = jnp.zeros(dh_sc.shape, F32)

        masks = _ssd_masks()
        dxbc_parts, dds_parts = [], []
        for gi in range(gb):
            grp = gb * pl.program_id(0) + gi
            x, bm, cm, dt, ac, ac_cols, is_last = _ssd_load(xbc_ref, dt_ref, ac_ref, masks, grp, gi)
            last = jnp.sum(ac * is_last, axis=0, keepdims=True)
            g = dy_ref[0, :, gi * GROUP_W:(gi + 1) * GROUP_W]
            hprev = hp_ref[0, gi, 0]
            dh = dh_sc[gi]
            decays = _ssd_decays(ac_cols, ar_ref, gi)
            dcols = []
            xd = x * dt
            cb = _dot(cm, bm, NT)
            e_c = jnp.exp(ac)
            e_end = jnp.exp(last - ac)
            e_last = jnp.exp(last)
            z = _dot(cm, hprev, NN)
            dz = g * e_c
            dac = g * z * e_c
            dc = _dot(dz, hprev, NT)
            dhprev = _dot(cm, dz, TN) + dh * e_last
            dcb = jnp.zeros((CHUNK, CHUNK), F32)
            dxd = jnp.zeros(xd.shape, F32)
            for j in range(HPG):
                gj = cb * decays[j]
                dgj = _dot(g * masks[j], xd, NT)
                dxd = dxd + _dot(gj, g, TN) * masks[j]
                dcb = dcb + dgj * decays[j]
                dseg = dgj * gj
                dcols.append(jnp.sum(dseg, axis=1, keepdims=True))
                dar_ref[0, gi * HPG + j] = -jnp.sum(dseg, axis=0, keepdims=True)
            dc = dc + _dot(dcb, bm, NN)
            db = _dot(dcb, cm, TN)
            sx = xd * e_end
            db = db + _dot(sx, dh, NT)
            dsx = _dot(bm, dh, NN)
            dxd = dxd + dsx * e_end
            de = dsx * sx
            dac = dac - de
            dlast = jnp.sum(de, axis=0, keepdims=True) + jnp.sum(dh * hprev, axis=0, keepdims=True) * e_last
            dsk = ds_ref[:, gi * GROUP_W:(gi + 1) * GROUP_W]
            dxbc_parts += [dxd * dt + dsk * g, db, dc]
            ddt_ref[0, gi] = _ssd_gather(dxd * x, None, grp)
            dac_ref[0, gi] = _ssd_gather(dac + is_last * dlast, dcols, grp)
            dds_parts.append(jnp.sum(g * x, axis=0, keepdims=True))
            dh_sc[gi] = dhprev
        dxbc_ref[0] = jnp.concatenate(dxbc_parts, axis=1)
        dds = jnp.concatenate(dds_parts, axis=1)
        first_all = first & (pl.program_id(1) == 0)

        @pl.when(first_all)
        def _():
            dds_ref[...] = dds

        @pl.when(jnp.logical_not(first_all))
        def _():
            dds_ref[...] += dds

    rc = lambda c: nc - 1 - c
    ng = SSM_GROUPS // gb
    in_specs = [pl.BlockSpec(s.block_shape, (lambda g, b, c, f=s.index_map: f(b, g, c))) for s in _ssd_in_specs(nc, True, gb)]
    in_specs.append(pl.BlockSpec((1, gb, 1, STATE_N, GROUP_W), lambda g, b, c: (b, g, rc(c), 0, 0)))
    in_specs.append(pl.BlockSpec((1, CHUNK, gb * GROUP_W), lambda g, b, c: (b, rc(c), g)))
    per_group = pl.BlockSpec((1, gb, CHUNK, LANE), lambda g, b, c: (b, g, rc(c), 0))
    out_specs = [pl.BlockSpec((1, CHUNK, gb * XBC_W), lambda g, b, c: (b, rc(c), g)), per_group, per_group,
                 pl.BlockSpec((1, gb * HPG, 1, CHUNK), lambda g, b, c: (b, g, 0, rc(c))),
                 pl.BlockSpec((1, gb * GROUP_W), lambda g, b, c: (0, g))]
    out_shape = [jax.ShapeDtypeStruct(xbc.shape, F32),
                 jax.ShapeDtypeStruct((B, SSM_GROUPS, S, LANE), F32), jax.ShapeDtypeStruct((B, SSM_GROUPS, S, LANE), F32),
                 jax.ShapeDtypeStruct(acr.shape, F32), jax.ShapeDtypeStruct(dsk.shape, F32)]
    return pl.pallas_call(
        body, grid=(ng, B, nc), in_specs=in_specs, out_specs=out_specs, out_shape=out_shape,
        scratch_shapes=[pltpu.VMEM((gb, STATE_N, GROUP_W), F32)], name="ssd_bwd",
        compiler_params=_cparams(("arbitrary", "arbitrary", "arbitrary")))(xbc, dtp, acp, acr, dsk, hps, dy)


@jax.custom_vjp
def ssd(xbc, dtp, acp, acr, dsk):
    return _ssd_fwd_call(xbc, dtp, acp, acr, dsk)[0]


def _ssd_fwd(xbc, dtp, acp, acr, dsk):
    y, hps = _ssd_fwd_call(xbc, dtp, acp, acr, dsk)
    return y, (xbc, dtp, acp, acr, dsk, hps)


def _ssd_bwd(res, dy):
    dxbc, ddt, dac, dacr, dds = _ssd_bwd_call(*res, dy)
    return dxbc, jnp.sum(ddt, axis=1), jnp.sum(dac, axis=1), dacr, dds


ssd.defvjp(_ssd_fwd, _ssd_bwd)


def _pack_small(arrs):
    flat = jnp.concatenate([a.reshape(-1) for a in arrs])
    rows = -(-flat.shape[0] // (8 * LANE)) * 8
    return jnp.pad(flat, (0, rows * LANE - flat.shape[0])).reshape(rows, LANE)


def _unpack_small(buf, shapes):
    flat = buf.reshape(-1)
    out, off = [], 0
    for shp in shapes:
        n = int(np.prod(shp))
        out.append(flat[off:off + n].reshape(shp))
        off += n
    return out


def _rows_tile(rows, cap):
    for cand in range(min(rows, cap), 7, -8):
        if rows % cand == 0:
            return cand
    return rows


def _pair_sum(mine, theirs, cidx, name):
    n4, kk, nn = mine.shape
    half = kk // 2
    tr = _rows_tile(half, 256)
    nb = half // tr

    def body(c_ref, a_ref, b_ref, o_ref, ob_ref):
        tot = a_ref[...] + b_ref[...]
        o_ref[...] = tot
        ob_ref[...] = tot.astype(BF16)

    spec = pl.BlockSpec((1, tr, nn), lambda j, i, c: (j, i, 0))
    grid_spec = pltpu.PrefetchScalarGridSpec(
        num_scalar_prefetch=1, grid=(n4, nb),
        in_specs=[pl.BlockSpec((1, tr, nn), lambda j, i, c: (j, c[0] * nb + i, 0)), spec], out_specs=[spec, spec])
    return pl.pallas_call(
        body, grid_spec=grid_spec,
        out_shape=[jax.ShapeDtypeStruct((n4, half, nn), F32), jax.ShapeDtypeStruct((n4, half, nn), BF16)],
        name=name, compiler_params=_cparams(("parallel", "parallel")))(cidx, mine, theirs)


def _chip_sum(quad, pair, chip_idx, name):
    _, rows, nn = quad.shape
    tr = _rows_tile(rows, 256)

    def body(s_ref, q_ref, p_ref, o_ref):
        for mine in range(4):
            @pl.when(s_ref[0] == mine)
            def _(mine=mine):
                acc = None
                for d in range(4):
                    term = p_ref[0] if d == mine else q_ref[d].astype(F32)
                    acc = term if acc is None else acc + term
                o_ref[...] = acc

    grid_spec = pltpu.PrefetchScalarGridSpec(
        num_scalar_prefetch=1, grid=(rows // tr,),
        in_specs=[pl.BlockSpec((4, tr, nn), lambda i, s: (0, i, 0)), pl.BlockSpec((1, tr, nn), lambda i, s: (s[0], i, 0))],
        out_specs=pl.BlockSpec((tr, nn), lambda i, s: (i, 0)))
    return pl.pallas_call(body, grid_spec=grid_spec, out_shape=jax.ShapeDtypeStruct((rows, nn), F32), name=name,
                          compiler_params=_cparams(("parallel",)))(chip_idx, quad, pair)


def _adam_halves_call(w, mine, other, cidx, m, v, name):
    rows, nn = w.shape
    half = rows // 2
    tr = _rows_tile(half, 128)
    nb = half // tr

    def body(c_ref, w_ref, a_ref, b_ref, m_ref, v_ref, g_ref, d_ref, nm_ref, nv_ref):
        upper = (pl.program_id(0) >= nb).astype(jnp.int32)
        g = jnp.where(upper == c_ref[0], a_ref[...], b_ref[...])
        g_ref[...] = g
        d_ref[...], nm_ref[...], nv_ref[...] = _adam_fn(w_ref[...], g, m_ref[...], v_ref[...])

    spec = pl.BlockSpec((tr, nn), lambda i, c: (i, 0))
    hspec = pl.BlockSpec((tr, nn), lambda i, c: (i % nb, 0))
    grid_spec = pltpu.PrefetchScalarGridSpec(num_scalar_prefetch=1, grid=(2 * nb,),
                                             in_specs=[spec, hspec, hspec, spec, spec], out_specs=[spec] * 4)
    return pl.pallas_call(body, grid_spec=grid_spec, out_shape=[jax.ShapeDtypeStruct((rows, nn), F32)] * 4, name=name,
                          compiler_params=_cparams(("parallel",)))(cidx, w, mine, other, m, v)


def _stack_sum(stack, name):
    n, rows, nn = stack.shape
    tr = _rows_tile(rows, 256)

    def body(s_ref, o_ref):
        acc = s_ref[0]
        for d in range(1, n):
            acc = acc + s_ref[d]
        o_ref[...] = acc

    return pl.pallas_call(
        body, grid=(rows // tr,), in_specs=[pl.BlockSpec((n, tr, nn), lambda i: (0, i, 0))],
        out_specs=pl.BlockSpec((tr, nn), lambda i: (i, 0)), out_shape=jax.ShapeDtypeStruct((rows, nn), F32),
        name=name, compiler_params=_cparams(("parallel",)))(stack)


def _adam_call(w, g, m, v, name):
    rows, nn = w.shape
    tr = _rows_tile(rows, 128)

    def body(w_ref, g_ref, m_ref, v_ref, d_ref, nm_ref, nv_ref):
        d_ref[...], nm_ref[...], nv_ref[...] = _adam_fn(w_ref[...], g_ref[...], m_ref[...], v_ref[...])

    spec = pl.BlockSpec((tr, nn), lambda i: (i, 0))
    sds = jax.ShapeDtypeStruct((rows, nn), F32)
    return pl.pallas_call(body, grid=(rows // tr,), in_specs=[spec] * 4, out_specs=[spec] * 3,
                          out_shape=[sds] * 3, name=name, compiler_params=_cparams(("parallel",)))(w, g, m, v)


def _adam_fn(w, g, m, v):
    m = ADAM_B1 * m + (1.0 - ADAM_B1) * g
    v = ADAM_B2 * v + (1.0 - ADAM_B2) * (g * g)
    m_hat = m / (1.0 - ADAM_B1 ** ADAM_STEP)
    v_hat = v / (1.0 - ADAM_B2 ** ADAM_STEP)
    delta = -ADAM_LR * (m_hat / (jnp.sqrt(v_hat) + ADAM_EPS) + ADAM_WD * w)
    return delta, m, v


def _mesh_pos():
    return lax.axis_index("x"), lax.axis_index("y"), lax.axis_index("c")


def _other_chips(x, y):
    return [(1 - x, y), (x, 1 - y), (1 - x, 1 - y)]


HBM_SPEC = pl.BlockSpec(memory_space=pl.ANY)


def _remote(src, dst, send_sems, recv_sems, k, to):
    return pltpu.make_async_remote_copy(src_ref=src, dst_ref=dst, send_sem=send_sems.at[k], recv_sem=recv_sems.at[k],
                                        device_id=to, device_id_type=MESH)


def _half_rows(c, rows, align):
    half = rows // 2
    return (pl.ds(pl.multiple_of(c * half, align), half), pl.ds(pl.multiple_of((1 - c) * half, align), half))


def _gather_weights(mats, conv):
    n = len(mats)

    def body(*refs):
        ins, conv_in = refs[:n], refs[n]
        outs, conv_out = refs[n + 1:2 * n + 1], refs[2 * n + 1]
        send_sems, recv_sems, local_sem = refs[2 * n + 2:]
        x, y, c = _mesh_pos()
        me, sibling, s = (x, y, c), (x, y, 1 - c), 2 * x + y
        chips = _other_chips(x, y)
        rows = [_half_rows(c, m.shape[0], 16) for m in mats]
        own = pltpu.make_async_copy(conv_in, conv_out.at[s], local_sem)
        own.start()
        sent = []
        for i in range(n):
            mine = rows[i][0]
            for j, (cx, cy) in enumerate(chips):
                sent.append(_remote(ins[i].at[mine], outs[i].at[s, mine], send_sems, recv_sems, 6 * i + j, (cx, cy, c)))
        for j, (cx, cy) in enumerate(chips):
            sent.append(_remote(conv_in, conv_out.at[s], send_sems, recv_sems, 6 * n + j, (cx, cy, c)))
        for cp in sent:
            cp.start()
        for i in range(n):
            mine = rows[i][0]
            for j, (cx, cy) in enumerate(chips):
                landed = outs[i].at[2 * cx + cy, mine]
                _remote(landed, landed, send_sems, recv_sems, 6 * i + j, me).wait_recv()
                fwd = _remote(landed, landed, send_sems, recv_sems, 6 * i + 3 + j, sibling)
                fwd.start()
                sent.append(fwd)
        for j, (cx, cy) in enumerate(chips):
            slot = conv_out.at[2 * cx + cy]
            _remote(slot, slot, send_sems, recv_sems, 6 * n + j, me).wait_recv()
        for i in range(n):
            theirs_rows = rows[i][1]
            for j, (cx, cy) in enumerate(chips):
                theirs = outs[i].at[2 * cx + cy, theirs_rows]
                _remote(theirs, theirs, send_sems, recv_sems, 6 * i + 3 + j, me).wait_recv()
        for cp in sent:
            cp.wait_send()
        own.wait()

    out_shape = [jax.ShapeDtypeStruct((4,) + m.shape, m.dtype) for m in mats]
    out_shape.append(jax.ShapeDtypeStruct((4,) + conv.shape, conv.dtype))
    res = pl.pallas_call(
        body, in_specs=[HBM_SPEC] * (n + 1), out_specs=[HBM_SPEC] * (n + 1), out_shape=out_shape,
        scratch_shapes=[pltpu.SemaphoreType.DMA((6 * n + 3,)), pltpu.SemaphoreType.DMA((6 * n + 3,)),
                        pltpu.SemaphoreType.DMA],
        name="all_gather_weights")(*mats, conv)
    chip = 2 * lax.axis_index("x") + lax.axis_index("y")
    full = [lax.dynamic_update_slice_in_dim(r, m[None], chip, axis=0) for r, m in zip(res[:n], mats)]
    return full, res[n]


def _sibling_exchange(stacks):
    n = len(stacks)

    def body(*refs):
        ins, outs = refs[:n], refs[n:2 * n]
        send_sems, recv_sems = refs[2 * n:]
        x, y, c = _mesh_pos()
        cps = []
        for i in range(n):
            theirs = _half_rows(c, stacks[i].shape[1], 8)[1]
            cps.append(_remote(ins[i].at[:, theirs, :], outs[i], send_sems, recv_sems, i, (x, y, 1 - c)))
        for cp in cps:
            cp.start()
        for cp in cps:
            cp.wait()

    out_shape = [jax.ShapeDtypeStruct((4, s.shape[1] // 2, s.shape[2]), s.dtype) for s in stacks]
    return pl.pallas_call(
        body, in_specs=[HBM_SPEC] * n, out_specs=[HBM_SPEC] * n, out_shape=out_shape,
        scratch_shapes=[pltpu.SemaphoreType.DMA((n,)), pltpu.SemaphoreType.DMA((n,))],
        name="grad_sibling_exchange")(*stacks)


def _chip_exchange(parts):
    n = len(parts)

    def body(*refs):
        ins, outs = refs[:n], refs[n:2 * n]
        send_sems, recv_sems = refs[2 * n:]
        x, y, c = _mesh_pos()
        me, s = (x, y, c), 2 * x + y
        chips = _other_chips(x, y)
        sent = [_remote(ins[i].at[2 * cx + cy], outs[i].at[s], send_sems, recv_sems, 3 * i + j, (cx, cy, c))
                for i in range(n) for j, (cx, cy) in enumerate(chips)]
        for cp in sent:
            cp.start()
        for i in range(n):
            for j, (cx, cy) in enumerate(chips):
                slot = outs[i].at[2 * cx + cy]
                _remote(slot, slot, send_sems, recv_sems, 3 * i + j, me).wait_recv()
        for cp in sent:
            cp.wait_send()

    return pl.pallas_call(
        body, in_specs=[HBM_SPEC] * n, out_specs=[HBM_SPEC] * n,
        out_shape=[jax.ShapeDtypeStruct(p.shape, p.dtype) for p in parts],
        scratch_shapes=[pltpu.SemaphoreType.DMA((3 * n,)), pltpu.SemaphoreType.DMA((3 * n,))],
        name="grad_chip_exchange")(*parts)


def _sibling_swap(halves):
    n = len(halves)

    def body(*refs):
        ins, outs = refs[:n], refs[n:2 * n]
        send_sems, recv_sems = refs[2 * n:]
        x, y, c = _mesh_pos()
        cps = [_remote(ins[i], outs[i], send_sems, recv_sems, i, (x, y, 1 - c)) for i in range(n)]
        for cp in cps:
            cp.start()
        for cp in cps:
            cp.wait()

    return pl.pallas_call(
        body, in_specs=[HBM_SPEC] * n, out_specs=[HBM_SPEC] * n,
        out_shape=[jax.ShapeDtypeStruct(h.shape, h.dtype) for h in halves],
        scratch_shapes=[pltpu.SemaphoreType.DMA((n,)), pltpu.SemaphoreType.DMA((n,))],
        name="grad_sibling_swap")(*halves)


def _gather_small(vec):
    def body(in_ref, out_ref, send_sems, recv_sems, local_sem):
        x, y, c = _mesh_pos()
        me = (x, y, c)
        own = pltpu.make_async_copy(in_ref, out_ref.at[4 * x + 2 * y + c], local_sem)
        own.start()
        peers = [(1 - x if k & 4 else x, 1 - y if k & 2 else y, 1 - c if k & 1 else c) for k in range(1, 8)]
        sent = [_remote(in_ref, out_ref.at[4 * x + 2 * y + c], send_sems, recv_sems, k, p) for k, p in enumerate(peers)]
        for cp in sent:
            cp.start()
        for k, (px, py, pc) in enumerate(peers):
            slot = out_ref.at[4 * px + 2 * py + pc]
            _remote(slot, slot, send_sems, recv_sems, k, me).wait_recv()
        for cp in sent:
            cp.wait_send()
        own.wait()

    return pl.pallas_call(
        body, in_specs=[HBM_SPEC], out_specs=HBM_SPEC, out_shape=jax.ShapeDtypeStruct((8,) + vec.shape, vec.dtype),
        scratch_shapes=[pltpu.SemaphoreType.DMA((7,)), pltpu.SemaphoreType.DMA((7,)), pltpu.SemaphoreType.DMA],
        name="grad_gather_small")(vec)


def _reduce_matrices(stacks, names):
    cidx = lax.axis_index("c").astype(jnp.int32).reshape(1)
    chip = (2 * lax.axis_index("x") + lax.axis_index("y")).astype(jnp.int32).reshape(1)
    got = _sibling_exchange(stacks)
    pairs = [_pair_sum(a, b, cidx, "grad_pair_sum_" + nm) for a, b, nm in zip(stacks, got, names)]
    quads = _chip_exchange([p[1] for p in pairs])
    mine = [_chip_sum(q, p[0], chip, "grad_chip_sum_" + nm) for q, p, nm in zip(quads, pairs, names)]
    return mine, _sibling_swap(mine)


def _pad_cols(a, n):
    return jnp.concatenate([a, jnp.zeros((a.shape[0], n - a.shape[1]), a.dtype)], axis=1)


def _group_channels(a):
    lead = a.shape[:-1]
    xs = a[..., :D_INNER].reshape(lead + (SSM_GROUPS, GROUP_W))
    bs = a[..., D_INNER:D_INNER + SSM_GROUPS * STATE_N].reshape(lead + (SSM_GROUPS, STATE_N))
    cs = a[..., D_INNER + SSM_GROUPS * STATE_N:].reshape(lead + (SSM_GROUPS, STATE_N))
    return jnp.concatenate([xs, bs, cs], axis=-1).reshape(lead + (CONV_CH,))


PROJ_SEGS = (('gate_a', D_MODEL), ('gate_b', D_MODEL), ('z', D_INNER), ('xbc', CONV_CH), ('q_lat', Q_RANK),
             ('kv_lat', KV_RANK), ('k_rope', LANE), ('dt', LANE))
PROJ_WIDE = sum(w for _, w in PROJ_SEGS[:4])
PROJ_LANE0 = {n: (v if v < PROJ_WIDE else v - PROJ_WIDE) for n, v in
              zip([n for n, _ in PROJ_SEGS], [int(v) for v in np.cumsum([0] + [w for _, w in PROJ_SEGS])[:-1]])}
CONV_LANE0 = PROJ_LANE0['xbc']
KR_LANE0 = PROJ_LANE0['k_rope']


def _lay_w_in(w):
    idx = np.cumsum(IN_SIZES)[:-1]
    q_lat, kv_lat, k_rope, z, xbc, dt, gate_a, gate_b = jnp.split(w, [int(v) for v in idx], axis=1)
    return jnp.concatenate([gate_a, gate_b, z, _group_channels(xbc), q_lat, kv_lat, _pad_cols(k_rope, LANE),
                            _pad_cols(dt, LANE)], axis=1)


@jax.custom_vjp
def project(h, w, tok):
    return _project_impl(h, w)


def _project_impl(h, w):
    return (_mm(h, w[:, :PROJ_WIDE], "w_in_fwd", BF16), _mm(h, w[:, PROJ_WIDE:], "w_in_narrow_fwd")) + tuple(
        jnp.zeros((h.shape[0], wd), BF16) for _, wd in PROJ_SEGS)


def _project_fwd(h, w, tok):
    return _project_impl(h, w), (h, w)


def _project_bwd(res, cots):
    h, w = res
    g = jnp.concatenate(cots[2:], axis=1)
    return _mm(g, w.T, "w_in_dx", h.dtype), jnp.zeros_like(w), _mm_tn(h, g, "w_in_dw")


project.defvjp(_project_fwd, _project_bwd)


def _lay_w_uq(w):
    w3 = w.reshape(Q_RANK, N_HEADS, NOPE + ROPE)
    w3 = jnp.concatenate([w3, jnp.zeros((Q_RANK, N_HEADS, QK_PAD - NOPE - ROPE), w.dtype)], axis=2)
    return w3.reshape(Q_RANK, N_HEADS * QK_PAD)


def _lay_w_ukv(w):
    w3 = w.reshape(KV_RANK, N_HEADS, NOPE + V_DIM)
    return jnp.concatenate([w3[:, :, :NOPE].reshape(KV_RANK, -1), w3[:, :, NOPE:].reshape(KV_RANK, -1)], axis=1)


def _pad_lanes(v, n=LANE):
    return jnp.concatenate([v, jnp.zeros((v.shape[0], n - v.shape[1]), v.dtype)], axis=1)


def _local_loss(toks, small, x, wb, c8, posf, target):
    B, S, D = x.shape
    T = B * S

    def lin(name, a, key, lay=lambda w: w, out_dtype=F32):
        return make_linear(name, out_dtype)(a, lay(wb[key]), lay(toks[key]))

    rows2 = lambda a: a.reshape(T, a.shape[-1])
    rows3 = lambda a: a.reshape(B, S, a.shape[-1])

    sc = make_rowwise("silu_c", _f_silu, 1, 0, 0, ('row',))((c8[None],), (), ())[0][0]
    mod = lin("ada", sc, 'w_ada')[:B] + small['b_ada']
    shift1, scale1, gate1, shift2, scale2, gate2 = [m[:, None, :] for m in jnp.split(mod, 6, axis=-1)]

    modulate = make_rowwise("modulate1", _f_modulate, 1, 2, 1, ('row',))
    h = modulate((x,), (scale1, shift1), (small['g_pre_mix'],))[0]
    outs = project(rows2(h), _lay_w_in(wb['w_in']), _lay_w_in(toks['w_in']))
    wide = lax.stop_gradient(rows3(outs[0]))
    proj = lax.stop_gradient(rows3(outs[1]))
    stand = {n: rows3(o) for (n, _), o in zip(PROJ_SEGS, outs[2:])}

    def win(seg, block):
        return (PROJ_LANE0[seg] // block, dict(PROJ_SEGS)[seg])

    inv = ROPE_THETA ** (-jnp.arange(ROPE // 2, dtype=F32) / (ROPE // 2))
    inv_lane = jnp.concatenate([inv, inv, jnp.zeros((LANE - ROPE,), F32)])[None]
    tabs = tuple(_rope_tables(posf, inv_lane))
    qn = make_rowwise("rms_q", _f_rms, 1, 0, 1, ('row',), windows={0: win('q_lat', Q_RANK)})(
        (proj,), (), (small['g_q_lat'],), (stand['q_lat'],))[0]
    kvn = make_rowwise("rms_kv", _f_rms, 1, 0, 1, ('row',), windows={0: win('kv_lat', KV_RANK)})(
        (proj,), (), (small['g_kv_lat'],), (stand['kv_lat'],))[0]
    qp = rows3(lin("w_uq", rows2(qn), 'w_uq', _lay_w_uq))
    kvp = rows3(lin("w_ukv", rows2(kvn), 'w_ukv', _lay_w_ukv, BF16))
    qr = rope_q(qp, tabs)
    kr = build_k(kvp, proj, stand['k_rope'], tabs)
    att = attention(qr, kr, kvp)
    attn = rows3(lin("w_o_attn", rows2(att), 'w_o_attn'))

    xa = conv_silu(wide, stand['xbc'], _group_channels(wb['conv_w_f32']), _group_channels(small['conv_b']))
    dt_pad, a_pad = make_rowwise("dt_softplus", _f_dt, 1, 0, 2, ('row', 'row'), windows={0: win('dt', LANE)})(
        (proj,), (), (_pad_lanes(small['dt_bias']), _pad_lanes(small['a_log'])), (stand['dt'],))
    ac_pad = chunk_cumsum(a_pad)
    acr = jnp.transpose(ac_pad[..., :SSM_HEADS], (0, 2, 1))[:, :, None, :]
    dsk = jnp.repeat(small['d_skip'], HEAD_P, axis=-1)
    y = ssd(xa, dt_pad, ac_pad, acr, dsk)
    yg = make_rowwise("gated_norm", _f_gated_norm, 2, 0, 1, ('row',), ncol=SSM_GROUPS, ts_cap=2048,
                      windows={1: win('z', GROUP_W)})((y, wide), (), (small['g_ssm_out'],), (stand['z'],))[0]
    ssm = rows3(lin("w_o_ssm", rows2(yg), 'w_o_ssm'))

    merged = make_rowwise("merge", _f_merge, 4, 0, 0, ('row',),
                          windows={2: win('gate_a', D_MODEL), 3: win('gate_b', D_MODEL)})(
        (attn, ssm, wide, wide), (), (), (stand['gate_a'], stand['gate_b']))[0]
    mix = rows3(lin("w_out", rows2(merged), 'w_out'))
    x1 = make_rowwise("post_mix", _f_post, 2, 1, 1, ('row',))((x, mix), (gate1,), (small['g_post_mix'],))[0]

    h2 = make_rowwise("modulate2", _f_modulate, 1, 2, 1, ('row',))((x1,), (scale2, shift2), (small['g_pre_mlp'],))[0]
    ff = rows3(ffn(rows2(h2), wb['w_ff1'], toks['w_ff1'], wb['w_ff2'], toks['w_ff2']))
    lvec = make_rowwise("final_loss", _f_final_loss, 3, 1, 1, ('sum',), nodiff=(2,))(
        (x1, ff, target), (gate2,), (small['g_post_mlp'],))[0]
    return jnp.sum(lvec)


MATRICES = COL_SHARDED + ROW_SHARDED


def _local_step(x, c, positions, target, wb, small):
    B = x.shape[0]
    c8 = jnp.concatenate([c, jnp.zeros((16 - B, c.shape[1]), F32)], axis=0)
    posf = positions.astype(F32)[..., None]
    toks = {k: jnp.zeros(wb[k].shape, F32) for k in MATRICES if k != 'conv_w'}
    conv_w = wb['conv_w_f32']

    def loss_fn(toks, small, conv_w, x):
        wbl = dict(wb)
        wbl['conv_w_f32'] = conv_w
        return _local_loss(toks, small, x, wbl, c8, posf, target)

    loss, (g_tok, g_small, g_conv, g_x) = jax.value_and_grad(loss_fn, argnums=(0, 1, 2, 3))(toks, small, conv_w, x)
    grads = dict(g_tok)
    grads.update(g_small)
    grads['conv_w'] = g_conv
    return loss, g_x, grads


def kernel(x, c, positions, w_ada, b_ada, g_pre_mix, g_post_mix, w_in, g_q_lat, g_kv_lat, w_uq, w_ukv, w_o_attn, conv_w, conv_b, dt_bias, a_log, d_skip, g_ssm_out, w_o_ssm, w_out, g_pre_mlp, g_post_mlp, w_ff1, w_ff2, loss_target, m_w_ada, m_b_ada, m_g_pre_mix, m_g_post_mix, m_w_in, m_g_q_lat, m_g_kv_lat, m_w_uq, m_w_ukv, m_w_o_attn, m_conv_w, m_conv_b, m_dt_bias, m_a_log, m_d_skip, m_g_ssm_out, m_w_o_ssm, m_w_out, m_g_pre_mlp, m_g_post_mlp, m_w_ff1, m_w_ff2, v_w_ada, v_b_ada, v_g_pre_mix, v_g_post_mix, v_w_in, v_g_q_lat, v_g_kv_lat, v_w_uq, v_w_ukv, v_w_o_attn, v_conv_w, v_conv_b, v_dt_bias, v_a_log, v_d_skip, v_g_ssm_out, v_w_o_ssm, v_w_out, v_g_pre_mlp, v_g_post_mlp, v_w_ff1, v_w_ff2):
    given = dict(locals())
    w_loc = {n: given[n] for n in WEIGHTS}
    m_loc = {n: given["m_" + n] for n in WEIGHTS}
    v_loc = {n: given["v_" + n] for n in WEIGHTS}
    mats = [n for n in WEIGHTS if n in MATRICES and n != 'conv_w']
    vecs = [n for n in WEIGHTS if n not in MATRICES]

    g_mats, g_conv = _gather_weights([w_loc[n][0].astype(BF16) for n in mats], conv_w[0])
    wb = {}
    for n, g in zip(mats, g_mats):
        if n in COL_SHARDED:
            wb[n] = jnp.transpose(g, (1, 0, 2)).reshape(g.shape[1], -1)
        else:
            wb[n] = g.reshape(-1, g.shape[2])
    wb['conv_w_f32'] = jnp.transpose(g_conv, (1, 0, 2)).reshape(CONV_K, -1)
    small = {n: w_loc[n] for n in vecs}

    loss_part, grad_x, grads = _local_step(x, c, positions, loss_target, wb, small)
    loss = lax.psum(loss_part, ("x", "y", "c"))

    stacks = []
    for n in mats:
        kk, nn = w_loc[n].shape[1:]
        if n in COL_SHARDED:
            stacks.append(jnp.transpose(grads[n].reshape(kk, 4, nn), (1, 0, 2)))
        else:
            stacks.append(grads[n].reshape(4, kk, nn))
    g_mine, g_other = _reduce_matrices(stacks, mats)
    g_shard = {}

    vec_shapes = [tuple(grads[n].shape) for n in vecs] + [tuple(grads['conv_w'].shape)]
    total = _stack_sum(_gather_small(_pack_small([grads[n] for n in vecs] + [grads['conv_w']])), "grad_sum_small")
    g_vec = _unpack_small(total, vec_shapes)
    n_conv = conv_w.shape[2]
    chip = 2 * lax.axis_index("x") + lax.axis_index("y")
    g_shard['conv_w'] = lax.dynamic_slice_in_dim(g_vec[-1], chip * n_conv, n_conv, axis=1)
    for n, g in zip(vecs, g_vec):
        g_shard[n] = g

    delta, new_m, new_v = {}, {}, {}
    cidx = lax.axis_index("c").astype(jnp.int32).reshape(1)
    for n, mine, other in zip(mats, g_mine, g_other):
        g_shard[n], delta[n], new_m[n], new_v[n] = _adam_halves_call(
            w_loc[n][0], mine, other, cidx, m_loc[n][0], v_loc[n][0], "adamw_" + n)
    rest = vecs + ['conv_w']
    rest_shapes = [tuple(w_loc[n].shape) for n in rest]
    packed = [_pack_small([src[n] for n in rest]) for src in (w_loc, g_shard, m_loc, v_loc)]
    for dst, buf in zip((delta, new_m, new_v), _adam_call(*packed, "adamw_small")):
        dst.update(zip(rest, _unpack_small(buf, rest_shapes)))

    def out(d):
        return [d[n].reshape(w_loc[n].shape) for n in WEIGHTS]

    return (loss, grad_x, *out(g_shard), *out(delta), *out(new_m), *out(new_v))
```

```python
import functools
import math

import numpy as np
import jax
import jax.numpy as jnp
from jax import lax
from jax.experimental import pallas as pl
from jax.experimental.pallas import tpu as pltpu

F32 = jnp.float32
BF16 = jnp.bfloat16
MESH = pl.DeviceIdType.MESH

D_MODEL = 1024
N_HEADS = 8
NOPE = 128
ROPE = 64
V_DIM = 128
Q_RANK = 256
KV_RANK = 256
ROPE_THETA = 10000.0
D_INNER = 2048
SSM_HEADS = 32
SSM_GROUPS = 8
HEAD_P = 64
STATE_N = 128
CONV_K = 4
CHUNK = 128
CONV_CH = D_INNER + 2 * SSM_GROUPS * STATE_N
D_FF = 4096
EPS = 1e-6
IN_SIZES = (Q_RANK, KV_RANK, ROPE, D_INNER, CONV_CH, SSM_HEADS, D_MODEL, D_MODEL)
ADAM_LR, ADAM_B1, ADAM_B2, ADAM_EPS, ADAM_WD, ADAM_STEP = 0.001, 0.9, 0.999, 1e-08, 0.01, 10

VMEM_LIMIT_BYTES = 52 * 1024 * 1024
LANE = 128
QK_PAD = 256

WEIGHTS = ['w_ada', 'b_ada', 'g_pre_mix', 'g_post_mix', 'w_in', 'g_q_lat', 'g_kv_lat', 'w_uq', 'w_ukv',
           'w_o_attn', 'conv_w', 'conv_b', 'dt_bias', 'a_log', 'd_skip', 'g_ssm_out', 'w_o_ssm', 'w_out',
           'g_pre_mlp', 'g_post_mlp', 'w_ff1', 'w_ff2']
COL_SHARDED = ('w_ada', 'w_in', 'w_uq', 'w_ukv', 'conv_w', 'w_ff1')
ROW_SHARDED = ('w_o_attn', 'w_o_ssm', 'w_out', 'w_ff2')


def _cparams(sem):
    return pltpu.CompilerParams(dimension_semantics=sem, vmem_limit_bytes=VMEM_LIMIT_BYTES)


def _tile(n, cap):
    if n <= cap:
        return n
    k = n // LANE
    best = LANE
    for d in range(1, k + 1):
        if k % d == 0 and d * LANE <= cap:
            best = d * LANE
    return best


def _mm(a, w, name, out_dtype=F32, epilogue=None, extras=(), out_dtypes=None):
    M, K = a.shape
    N = w.shape[1]
    tm = min(M, 1024)
    tn = _tile(N, 1024)
    tk = _tile(K, 2048)
    nk = K // tk
    dts = tuple(out_dtypes) if epilogue is not None else (out_dtype,)
    n_x, n_o = len(extras), len(dts)

    def finish(acc, refs):
        res = epilogue(acc, *[r[...] for r in refs[:n_x]]) if epilogue is not None else (acc,)
        for o_ref, val, dt in zip(refs[n_x:n_x + n_o], res, dts):
            o_ref[...] = val.astype(dt)

    def body(a_ref, w_ref, *refs):
        part = jnp.dot(a_ref[...].astype(BF16), w_ref[...], preferred_element_type=F32)
        if nk == 1:
            finish(part, refs)
        else:
            acc_ref = refs[-1]
            k = pl.program_id(2)

            @pl.when(k == 0)
            def _():
                acc_ref[...] = part

            @pl.when(k > 0)
            def _():
                acc_ref[...] += part

            @pl.when(k == nk - 1)
            def _():
                finish(acc_ref[...], refs)

    ospec = pl.BlockSpec((tm, tn), lambda i, j, k: (i, j))
    res = pl.pallas_call(
        body, grid=(M // tm, N // tn, nk),
        in_specs=[pl.BlockSpec((tm, tk), lambda i, j, k: (i, k)), pl.BlockSpec((tk, tn), lambda i, j, k: (k, j))]
        + [ospec] * n_x,
        out_specs=[ospec] * n_o, out_shape=[jax.ShapeDtypeStruct((M, N), dt) for dt in dts],
        scratch_shapes=[pltpu.VMEM((tm, tn), F32)] if nk > 1 else [], name=name,
        compiler_params=_cparams(("parallel", "parallel", "arbitrary")))(a, w, *extras)
    return res if epilogue is not None else res[0]


def _mm_tn(a, g, name):
    M, K = a.shape
    N = g.shape[1]
    tm = min(M, 1024)
    tk = _tile(K, 1024)
    tn = _tile(N, 1024)
    nm = M // tm

    def body(a_ref, g_ref, o_ref):
        part = lax.dot_general(a_ref[...].astype(BF16), g_ref[...].astype(BF16), (((0,), (0,)), ((), ())),
                               preferred_element_type=F32)
        m = pl.program_id(2)

        @pl.when(m == 0)
        def _():
            o_ref[...] = part

        @pl.when(m > 0)
        def _():
            o_ref[...] += part

    return pl.pallas_call(
        body, grid=(K // tk, N // tn, nm),
        in_specs=[pl.BlockSpec((tm, tk), lambda i, j, m: (m, i)), pl.BlockSpec((tm, tn), lambda i, j, m: (m, j))],
        out_specs=pl.BlockSpec((tk, tn), lambda i, j, m: (i, j)),
        out_shape=jax.ShapeDtypeStruct((K, N), F32), name=name,
        compiler_params=_cparams(("parallel", "parallel", "arbitrary")))(a, g)


def make_linear(name, out_dtype=F32):
    @jax.custom_vjp
    def linear(a, w, tok):
        return _mm(a, w, name + "_fwd", out_dtype)

    def fwd(a, w, tok):
        return _mm(a, w, name + "_fwd", out_dtype), (a, w)

    def bwd(res, g):
        a, w = res
        da = _mm(g, w.T, name + "_dx", a.dtype)
        dw = _mm_tn(a, g, name + "_dw")
        return da, jnp.zeros_like(w), dw

    linear.defvjp(fwd, bwd)
    return linear


def _relu2_epilogue(acc):
    r = jnp.maximum(acc, 0.0)
    return r * r, r


def _relu2_bwd_epilogue(acc, r):
    return (acc * (2.0 * r.astype(F32)),)


@jax.custom_vjp
def ffn(h, w1, tok1, w2, tok2):
    act, _ = _mm(h, w1, "w_ff1_fwd", epilogue=_relu2_epilogue, out_dtypes=(BF16, BF16))
    return _mm(act, w2, "w_ff2_fwd")


def _ffn_fwd(h, w1, tok1, w2, tok2):
    act, r = _mm(h, w1, "w_ff1_fwd", epilogue=_relu2_epilogue, out_dtypes=(BF16, BF16))
    return _mm(act, w2, "w_ff2_fwd"), (h, w1, w2, act, r)


def _ffn_bwd(res, g):
    h, w1, w2, act, r = res
    du = _mm(g, w2.T, "w_ff2_dx", epilogue=_relu2_bwd_epilogue, extras=(r,), out_dtypes=(BF16,))[0]
    dw2 = _mm_tn(act, g, "w_ff2_dw")
    dw1 = _mm_tn(h, du, "w_ff1_dw")
    dh = _mm(du, w1.T, "w_ff1_dx", h.dtype)
    return dh, jnp.zeros_like(w1), dw1, jnp.zeros_like(w2), dw2


ffn.defvjp(_ffn_fwd, _ffn_bwd)


def make_rowwise(name, f, n_rows, n_seqs, n_pars, out_kinds, ncol=1, nodiff=(), ts_cap=512, windows=None):
    windows = dict(windows or {})
    n_in = n_rows + n_seqs + n_pars
    diff_idx = [i for i in range(n_in) if i not in nodiff]

    def _dims(rows):
        B, S = rows[0].shape[0], rows[0].shape[1]
        ts = min(S, ts_cap)
        return B, S, ts

    def _width(i, r):
        return windows[i][1] if i in windows else r.shape[2]

    def _in_specs(rows, seqs, pars, ts):
        specs = []
        for i, r in enumerate(rows):
            col0 = windows[i][0] if i in windows else 0
            specs.append(pl.BlockSpec((1, ts, _width(i, r) // ncol), lambda k, b, s, col0=col0: (b, s, k + col0)))
        for q in seqs:
            specs.append(pl.BlockSpec((1, 1, q.shape[2] // ncol), lambda k, b, s: (b, 0, k)))
        for p in pars:
            specs.append(pl.BlockSpec((1, p.shape[1] // ncol), lambda k, b, s: (0, k)))
        return specs

    def _load(refs):
        vals = [r[0] for r in refs[:n_rows + n_seqs]]
        vals += [r[...] for r in refs[n_rows + n_seqs:n_in]]
        return vals

    def _out_struct(rows, seqs, pars, ts):
        blocks = [jax.ShapeDtypeStruct((ts, _width(i, r) // ncol), r.dtype) for i, r in enumerate(rows)]
        blocks += [jax.ShapeDtypeStruct((1, q.shape[2] // ncol), q.dtype) for q in seqs]
        blocks += [jax.ShapeDtypeStruct((1, p.shape[1] // ncol), p.dtype) for p in pars]
        return jax.eval_shape(f, *blocks)

    def _fwd_call(rows, seqs, pars):
        B, S, ts = _dims(rows)
        outs = _out_struct(rows, seqs, pars, ts)
        n_out = len(outs)

        def body(*refs):
            res = f(*_load(refs))
            first = (pl.program_id(1) == 0) & (pl.program_id(2) == 0)
            for o_ref, val, kind in zip(refs[n_in:], res, out_kinds):
                if kind == 'row':
                    o_ref[0] = val
                else:
                    tot = jnp.sum(val, axis=0, keepdims=True)

                    @pl.when(first)
                    def _(o_ref=o_ref, tot=tot):
                        o_ref[...] = tot

                    @pl.when(jnp.logical_not(first))
                    def _(o_ref=o_ref, tot=tot):
                        o_ref[...] += tot

        out_shape, out_specs = [], []
        for o, kind in zip(outs, out_kinds):
            d = o.shape[1]
            if kind == 'row':
                out_shape.append(jax.ShapeDtypeStruct((B, S, ncol * d), o.dtype))
                out_specs.append(pl.BlockSpec((1, ts, d), lambda k, b, s: (b, s, k)))
            else:
                out_shape.append(jax.ShapeDtypeStruct((1, ncol * d), o.dtype))
                out_specs.append(pl.BlockSpec((1, d), lambda k, b, s: (0, k)))
        res = pl.pallas_call(
            body, grid=(ncol, B, S // ts), in_specs=_in_specs(rows, seqs, pars, ts), out_specs=out_specs,
            out_shape=out_shape, name=name + "_fwd",
            compiler_params=_cparams(("arbitrary", "arbitrary", "arbitrary")))(*rows, *seqs, *pars)
        return tuple(res)

    def _bwd_call(rows, seqs, pars, cots):
        B, S, ts = _dims(rows)
        outs = _out_struct(rows, seqs, pars, ts)
        n_out = len(outs)
        all_in = list(rows) + list(seqs) + list(pars)

        def body(*refs):
            vals = _load(refs)
            cts = []
            for c_ref, o, kind in zip(refs[n_in:n_in + n_out], outs, out_kinds):
                if kind == 'row':
                    cts.append(c_ref[0])
                else:
                    cts.append(jnp.broadcast_to(c_ref[...], o.shape))

            def g(*dv):
                full = list(vals)
                for i, v in zip(diff_idx, dv):
                    full[i] = v
                return tuple(f(*full))

            _, vjp = jax.vjp(g, *[vals[i] for i in diff_idx])
            grads = vjp(tuple(cts))
            b, s = pl.program_id(1), pl.program_id(2)
            for o_ref, i, gr in zip(refs[n_in + n_out:], diff_idx, grads):
                if i < n_rows:
                    o_ref[0] = gr.astype(o_ref.dtype)
                else:
                    first = (s == 0) if i < n_rows + n_seqs else ((b == 0) & (s == 0))
                    target = (lambda r: r.at[0]) if i < n_rows + n_seqs else (lambda r: r)

                    @pl.when(first)
                    def _(o_ref=o_ref, gr=gr, target=target):
                        target(o_ref)[...] = gr

                    @pl.when(jnp.logical_not(first))
                    def _(o_ref=o_ref, gr=gr, target=target):
                        target(o_ref)[...] += gr

        cot_specs = []
        for o, kind in zip(outs, out_kinds):
            d = o.shape[1]
            if kind == 'row':
                cot_specs.append(pl.BlockSpec((1, ts, d), lambda k, b, s: (b, s, k)))
            else:
                cot_specs.append(pl.BlockSpec((1, d), lambda k, b, s: (0, k)))
        out_shape, out_specs = [], []
        for i in diff_idx:
            a = all_in[i]
            if i < n_rows:
                out_shape.append(jax.ShapeDtypeStruct((B, S, _width(i, a)), BF16 if i in windows else a.dtype))
                out_specs.append(pl.BlockSpec((1, ts, _width(i, a) // ncol), lambda k, b, s: (b, s, k)))
                continue
            out_shape.append(jax.ShapeDtypeStruct(a.shape, a.dtype))
            if i < n_rows + n_seqs:
                out_specs.append(pl.BlockSpec((1, 1, a.shape[2] // ncol), lambda k, b, s: (b, 0, k)))
            else:
                out_specs.append(pl.BlockSpec((1, a.shape[1] // ncol), lambda k, b, s: (0, k)))
        res = pl.pallas_call(
            body, grid=(ncol, B, S // ts), in_specs=_in_specs(rows, seqs, pars, ts) + cot_specs,
            out_specs=out_specs, out_shape=out_shape, name=name + "_bwd",
            compiler_params=_cparams(("arbitrary", "arbitrary", "arbitrary")))(*all_in, *cots)
        grads = [None] * n_in
        for i, r in zip(diff_idx, res):
            grads[i] = r
        for i in nodiff:
            grads[i] = jnp.zeros_like(all_in[i])
        stand_in_grads = tuple(grads[i] for i in sorted(windows))
        for i in windows:
            grads[i] = jnp.zeros_like(all_in[i])
        return (tuple(grads[:n_rows]), tuple(grads[n_rows:n_rows + n_seqs]), tuple(grads[n_rows + n_seqs:]),
                stand_in_grads)

    @jax.custom_vjp
    def op(rows, seqs, pars, stand_ins):
        return _fwd_call(rows, seqs, pars)

    def fwd(rows, seqs, pars, stand_ins):
        return _fwd_call(rows, seqs, pars), (rows, seqs, pars)

    def bwd(res, cots):
        rows, seqs, pars = res
        return _bwd_call(rows, seqs, pars, cots)

    op.defvjp(fwd, bwd)
    return lambda rows, seqs, pars, stand_ins=(): op(tuple(rows), tuple(seqs), tuple(pars), tuple(stand_ins))


def _rms(x, g):
    return x * lax.rsqrt(jnp.mean(x * x, axis=-1, keepdims=True) + EPS) * g


def _silu(x):
    return x * lax.logistic(x)


def _f_silu(c):
    return (_silu(c),)


def _f_modulate(x, scale, shift, g):
    return ((_rms(x, g) * (1.0 + scale) + shift).astype(BF16),)


def _f_rms(x, g):
    return (_rms(x, g).astype(BF16),)


def _f_dt(dt_raw, dt_bias, a_log):
    z = dt_raw + dt_bias
    dt = jnp.maximum(z, 0.0) + jnp.log1p(jnp.exp(-jnp.abs(z)))
    return dt, dt * (-jnp.exp(a_log))


def _f_gated_norm(y, z, g):
    return (_rms(y * _silu(z.astype(F32)), g).astype(BF16),)


def _f_merge(attn, ssm, ga, gb):
    return ((lax.logistic(ga.astype(F32)) * attn + lax.logistic(gb.astype(F32)) * ssm).astype(BF16),)


def _f_post(x, m, gate, g):
    return (x + gate * _rms(m, g),)


def _f_final_loss(x, ff, target, gate, g):
    e = x + gate * _rms(ff, g) - target
    return (e * e * (0.5 / D_MODEL),)


def _rope_tables(posf, inv_lane):
    B, S, _ = posf.shape
    ts = min(S, 512)

    def body(p_ref, inv_ref, c_ref, a_ref, b_ref):
        ang = p_ref[0] * inv_ref[...]
        cs, sn = jnp.cos(ang), jnp.sin(ang)
        lane = lax.broadcasted_iota(jnp.int32, ang.shape, 1)
        c_ref[0] = jnp.where(lane < ROPE, cs, 0.0)
        a_ref[0] = jnp.where(lane < ROPE // 2, -sn, 0.0)
        b_ref[0] = jnp.where((lane >= ROPE // 2) & (lane < ROPE), sn, 0.0)

    spec = pl.BlockSpec((1, ts, LANE), lambda b, s: (b, s, 0))
    sds = jax.ShapeDtypeStruct((B, S, LANE), F32)
    return pl.pallas_call(
        body, grid=(B, S // ts),
        in_specs=[pl.BlockSpec((1, ts, 1), lambda b, s: (b, s, 0)), pl.BlockSpec((1, LANE), lambda b, s: (0, 0))],
        out_specs=[spec, spec, spec], out_shape=[sds, sds, sds], name="rope_tables",
        compiler_params=_cparams(("parallel", "parallel")))(posf, inv_lane)


def _rot(u, c, a, bm):
    return u * c + pltpu.roll(u, 96, 1) * a + pltpu.roll(u, 32, 1) * bm


def _rot_t(g, c, a, bm):
    return g * c + pltpu.roll(g * a, 32, 1) + pltpu.roll(g * bm, 96, 1)


def _rope_q_call(q, tabs, transpose, name):
    B, S, W = q.shape
    ts = min(S, 512)
    fn = _rot_t if transpose else _rot
    out_dtype = F32 if transpose else BF16

    def body(q_ref, c_ref, a_ref, b_ref, o_ref):
        tc, ta, tb = c_ref[0], a_ref[0], b_ref[0]
        for h in range(W // QK_PAD):
            u = q_ref[0, :, h * QK_PAD:(h + 1) * QK_PAD].astype(F32) * ATT_SCALE
            r = fn(u[:, NOPE:], tc, ta, tb)
            o_ref[0, :, h * QK_PAD:(h + 1) * QK_PAD] = jnp.concatenate([u[:, :NOPE], r], axis=1).astype(out_dtype)

    tspec = pl.BlockSpec((1, ts, LANE), lambda b, s: (b, s, 0))
    qspec = pl.BlockSpec((1, ts, W), lambda b, s: (b, s, 0))
    return pl.pallas_call(
        body, grid=(B, S // ts), in_specs=[qspec, tspec, tspec, tspec], out_specs=qspec,
        out_shape=jax.ShapeDtypeStruct(q.shape, out_dtype), name=name,
        compiler_params=_cparams(("parallel", "parallel")))(q, *tabs)


@jax.custom_vjp
def rope_q(q, tabs):
    return _rope_q_call(q, tabs, False, "rope_q_fwd")


def _rope_q_fwd(q, tabs):
    return _rope_q_call(q, tabs, False, "rope_q_fwd"), tabs


def _rope_q_bwd(tabs, g):
    return _rope_q_call(g, tabs, True, "rope_q_bwd"), tuple(jnp.zeros_like(t) for t in tabs)


rope_q.defvjp(_rope_q_fwd, _rope_q_bwd)


def _build_k_fwd_call(kv, kr, tabs):
    B, S, _ = kv.shape
    ts = min(S, 512)

    def body(kv_ref, kr_ref, c_ref, a_ref, b_ref, o_ref):
        r = _rot(kr_ref[0], c_ref[0], a_ref[0], b_ref[0]).astype(BF16)
        for h in range(N_HEADS):
            o_ref[0, :, h * QK_PAD:(h + 1) * QK_PAD] = jnp.concatenate(
                [kv_ref[0, :, h * NOPE:(h + 1) * NOPE], r], axis=1)

    tspec = pl.BlockSpec((1, ts, LANE), lambda b, s: (b, s, 0))
    kr_spec = pl.BlockSpec((1, ts, LANE), lambda b, s: (b, s, KR_LANE0 // LANE))
    return pl.pallas_call(
        body, grid=(B, S // ts),
        in_specs=[pl.BlockSpec((1, ts, N_HEADS * NOPE), lambda b, s: (b, s, 0)), kr_spec, tspec, tspec, tspec],
        out_specs=pl.BlockSpec((1, ts, N_HEADS * QK_PAD), lambda b, s: (b, s, 0)),
        out_shape=jax.ShapeDtypeStruct((B, S, N_HEADS * QK_PAD), BF16), name="build_k_fwd",
        compiler_params=_cparams(("parallel", "parallel")))(kv, kr, *tabs)


def _build_k_bwd_call(g, tabs):
    B, S, _ = g.shape
    ts = min(S, 512)

    def body(g_ref, c_ref, a_ref, b_ref, dk_ref, dr_ref):
        tot = None
        for h in range(N_HEADS):
            dk_ref[0, :, h * NOPE:(h + 1) * NOPE] = g_ref[0, :, h * QK_PAD:h * QK_PAD + NOPE]
            part = g_ref[0, :, h * QK_PAD + NOPE:(h + 1) * QK_PAD].astype(F32)
            tot = part if tot is None else tot + part
        dr_ref[0] = _rot_t(tot, c_ref[0], a_ref[0], b_ref[0]).astype(BF16)

    tspec = pl.BlockSpec((1, ts, LANE), lambda b, s: (b, s, 0))
    return pl.pallas_call(
        body, grid=(B, S // ts),
        in_specs=[pl.BlockSpec((1, ts, N_HEADS * QK_PAD), lambda b, s: (b, s, 0)), tspec, tspec, tspec],
        out_specs=[pl.BlockSpec((1, ts, N_HEADS * NOPE), lambda b, s: (b, s, 0)), tspec],
        out_shape=[jax.ShapeDtypeStruct((B, S, N_HEADS * NOPE), BF16), jax.ShapeDtypeStruct((B, S, LANE), BF16)],
        name="build_k_bwd", compiler_params=_cparams(("parallel", "parallel")))(g, *tabs)


@jax.custom_vjp
def build_k(kv, src, stand_in, tabs):
    return _build_k_fwd_call(kv, src, tabs)


def _build_k_fwd(kv, src, stand_in, tabs):
    return _build_k_fwd_call(kv, src, tabs), (tabs, kv.shape, src)


def _build_k_bwd(res, g):
    tabs, kv_shape, src = res
    dk, dr = _build_k_bwd_call(g, tabs)
    dkv = jnp.concatenate([dk, jnp.zeros((kv_shape[0], kv_shape[1], kv_shape[2] - dk.shape[2]), BF16)], axis=-1)
    return dkv, jnp.zeros_like(src), dr, tuple(jnp.zeros_like(t) for t in tabs)


build_k.defvjp(_build_k_fwd, _build_k_bwd)


ATT_SCALE = (NOPE + ROPE) ** -0.5
NEG = -1e30


def _att_tiles(S):
    t = min(S, 512)
    return t, S // t


def _scores(q, k, diagonal):
    s = lax.dot_general(q, k, (((1,), (1,)), ((), ())), preferred_element_type=F32)
    if diagonal:
        row = lax.broadcasted_iota(jnp.int32, s.shape, 0)
        col = lax.broadcasted_iota(jnp.int32, s.shape, 1)
        s = jnp.where(col <= row, s, NEG)
    return s


ATT_HB = 4


def _causal_pairs(n):
    pairs = [(i, j) for i in range(n) for j in range(i + 1)]
    return (jnp.asarray([p[0] for p in pairs], jnp.int32), jnp.asarray([p[1] for p in pairs], jnp.int32))


def _head(ref_or_val, h, w):
    return ref_or_val[:, h * w:(h + 1) * w]


def _attn_fwd_call(q, k, vsrc, v_blk0):
    B, S, _ = q.shape
    t, n = _att_tiles(S)
    qi, kj = _causal_pairs(n)

    def body(qi_ref, kj_ref, q_ref, k_ref, v_ref, o_ref, lse_ref, m_sc, l_sc, acc_sc):
        p_id = pl.program_id(2)
        i, j = qi_ref[p_id], kj_ref[p_id]

        @pl.when(j == 0)
        def _():
            m_sc[...] = jnp.full(m_sc.shape, NEG, F32)
            l_sc[...] = jnp.zeros(l_sc.shape, F32)
            acc_sc[...] = jnp.zeros(acc_sc.shape, F32)

        def step(diagonal):
            qa, ka, va = q_ref[0], k_ref[0], v_ref[0]
            for h in range(ATT_HB):
                lanes = slice(h * LANE, (h + 1) * LANE)
                s = _scores(_head(qa, h, QK_PAD), _head(ka, h, QK_PAD), diagonal)
                m_prev = m_sc[:, lanes]
                m_new = jnp.maximum(m_prev, jnp.max(s, axis=1, keepdims=True))
                alpha = jnp.exp(m_prev - m_new)
                p = jnp.exp(s - jnp.tile(m_new, (1, t // LANE)))
                l_sc[:, lanes] = alpha * l_sc[:, lanes] + jnp.sum(p, axis=1, keepdims=True)
                acc_sc[:, lanes] = alpha * acc_sc[:, lanes] + jnp.dot(p.astype(BF16), _head(va, h, V_DIM),
                                                                      preferred_element_type=F32)
                m_sc[:, lanes] = m_new

        @pl.when(j < i)
        def _():
            step(False)

        @pl.when(j == i)
        def _():
            step(True)
            o_ref[0] = acc_sc[...] / l_sc[...]
            lse_ref[0] = m_sc[...] + jnp.log(l_sc[...])

    wq, wv = ATT_HB * QK_PAD, ATT_HB * V_DIM
    grid_spec = pltpu.PrefetchScalarGridSpec(
        num_scalar_prefetch=2, grid=(B, N_HEADS // ATT_HB, qi.shape[0]),
        in_specs=[pl.BlockSpec((1, t, wq), lambda b, h, p, qi, kj: (b, qi[p], h)),
                  pl.BlockSpec((1, t, wq), lambda b, h, p, qi, kj: (b, kj[p], h)),
                  pl.BlockSpec((1, t, wv), lambda b, h, p, qi, kj: (b, kj[p], v_blk0 + h))],
        out_specs=[pl.BlockSpec((1, t, wv), lambda b, h, p, qi, kj: (b, qi[p], h)),
                   pl.BlockSpec((1, t, wv), lambda b, h, p, qi, kj: (b, qi[p], h))],
        scratch_shapes=[pltpu.VMEM((t, wv), F32), pltpu.VMEM((t, wv), F32), pltpu.VMEM((t, wv), F32)])
    return pl.pallas_call(
        body, grid_spec=grid_spec,
        out_shape=[jax.ShapeDtypeStruct((B, S, N_HEADS * V_DIM), F32),
                   jax.ShapeDtypeStruct((B, S, N_HEADS * LANE), F32)],
        name="attn_fwd", compiler_params=_cparams(("parallel", "parallel", "arbitrary")))(qi, kj, q, k, vsrc)


def _attn_p_ds(q, k, v, o, do, lse, diagonal, t):
    s = _scores(q, k, diagonal)
    p = jnp.exp(s - jnp.tile(lse, (1, t // LANE)))
    dp = lax.dot_general(do.astype(BF16), v, (((1,), (1,)), ((), ())), preferred_element_type=F32)
    delta = jnp.sum(do * o, axis=1, keepdims=True)
    ds = p * (dp - delta)
    return p, ds


ATT_HB_BWD = 2


def _attn_bwd_call(q, k, vsrc, o, do, lse):
    B, S, _ = q.shape
    t, n = _att_tiles(S)
    qi, kj = _causal_pairs(n)
    n_pairs = qi.shape[0]
    hb = ATT_HB_BWD
    v_blk0 = N_HEADS // hb

    def body(qi_ref, kj_ref, q_ref, k_ref, v_ref, o_ref, do_ref, lse_ref, dq_ref, dk_ref, dv_ref, dq_sc, dk_sc, dv_sc):
        p_id = pl.program_id(2)
        i, j = qi_ref[p_id], kj_ref[p_id]

        @pl.when(p_id == 0)
        def _():
            dk_sc[...] = jnp.zeros(dk_sc.shape, F32)
            dv_sc[...] = jnp.zeros(dv_sc.shape, F32)

        @pl.when(j == 0)
        def _():
            dq_sc[...] = jnp.zeros(dq_sc.shape, F32)

        rows = pl.ds(pl.multiple_of(j * t, t), t)

        def step(diagonal):
            qa, ka, va, oa, doa, la = q_ref[0], k_ref[0], v_ref[0], o_ref[0], do_ref[0], lse_ref[0]
            for h in range(hb):
                qb, kb, dob = _head(qa, h, QK_PAD), _head(ka, h, QK_PAD), _head(doa, h, V_DIM)
                p, ds = _attn_p_ds(qb, kb, _head(va, h, V_DIM), _head(oa, h, V_DIM), dob, _head(la, h, LANE),
                                   diagonal, t)
                dsb = ds.astype(BF16)
                dq_sc[:, h * QK_PAD:(h + 1) * QK_PAD] += jnp.dot(dsb, kb, preferred_element_type=F32)
                dv_sc[rows, h * V_DIM:(h + 1) * V_DIM] += lax.dot_general(
                    p.astype(BF16), dob.astype(BF16), (((0,), (0,)), ((), ())), preferred_element_type=F32)
                dk_sc[rows, h * QK_PAD:(h + 1) * QK_PAD] += lax.dot_general(
                    dsb, qb, (((0,), (0,)), ((), ())), preferred_element_type=F32)

        @pl.when(j < i)
        def _():
            step(False)

        @pl.when(j == i)
        def _():
            step(True)
            dq_ref[0] = dq_sc[...].astype(BF16)

        @pl.when(p_id == n_pairs - 1)
        def _():
            dk_ref[0] = dk_sc[...].astype(BF16)
            dv_ref[0] = dv_sc[...].astype(BF16)

    wq, wv = hb * QK_PAD, hb * V_DIM
    at_q = lambda b, h, p, qi, kj: (b, qi[p], h)
    at_k = lambda b, h, p, qi, kj: (b, kj[p], h)
    whole = lambda b, h, p, qi, kj: (b, 0, h)
    grid_spec = pltpu.PrefetchScalarGridSpec(
        num_scalar_prefetch=2, grid=(B, N_HEADS // hb, n_pairs),
        in_specs=[pl.BlockSpec((1, t, wq), at_q), pl.BlockSpec((1, t, wq), at_k),
                  pl.BlockSpec((1, t, wv), lambda b, h, p, qi, kj: (b, kj[p], v_blk0 + h)),
                  pl.BlockSpec((1, t, wv), at_q), pl.BlockSpec((1, t, wv), at_q), pl.BlockSpec((1, t, wv), at_q)],
        out_specs=[pl.BlockSpec((1, t, wq), at_q), pl.BlockSpec((1, S, wq), whole), pl.BlockSpec((1, S, wv), whole)],
        scratch_shapes=[pltpu.VMEM((t, wq), F32), pltpu.VMEM((S, wq), F32), pltpu.VMEM((S, wv), F32)])
    return pl.pallas_call(
        body, grid_spec=grid_spec,
        out_shape=[jax.ShapeDtypeStruct((B, S, N_HEADS * QK_PAD), BF16),
                   jax.ShapeDtypeStruct((B, S, N_HEADS * QK_PAD), BF16),
                   jax.ShapeDtypeStruct((B, S, N_HEADS * V_DIM), BF16)],
        name="attn_bwd", compiler_params=_cparams(("parallel", "parallel", "arbitrary")))(
            qi, kj, q, k, vsrc, o, do, lse)


@jax.custom_vjp
def attention(q, k, kv):
    return _attn_fwd_call(q, k, kv, N_HEADS // ATT_HB)[0]


def _attention_fwd(q, k, kv):
    o, lse = _attn_fwd_call(q, k, kv, N_HEADS // ATT_HB)
    return o, (q, k, kv, o, lse)


def _attention_bwd(res, do):
    q, k, kv, o, lse = res
    dq, dk, dv = _attn_bwd_call(q, k, kv, o, do, lse)
    dkv = jnp.concatenate([jnp.zeros_like(dv), dv], axis=-1)
    return dq, dk, dkv


attention.defvjp(_attention_fwd, _attention_bwd)


SUBLANES = 8


def _zero_tail(v):
    return jnp.concatenate([v, jnp.zeros((SUBLANES, v.shape[1]), v.dtype)], axis=0)


def _shift_down(vz, sh):
    return pltpu.roll(vz, sh, 0)[:vz.shape[0] - SUBLANES]


def _shift_up(vz, sh):
    return pltpu.roll(vz, vz.shape[0] - sh, 0)[:vz.shape[0] - SUBLANES]


def _conv_pre(u, uz, w_ref, b_ref):
    acc = b_ref[...] + w_ref[pl.ds(CONV_K - 1, 1), :] * u
    for k in range(CONV_K - 1):
        acc = acc + w_ref[pl.ds(k, 1), :] * _shift_down(uz, CONV_K - 1 - k)
    return acc


def _conv_fwd_call(src, w, b):
    B, S, _ = src.shape
    C = w.shape[1]

    def body(u_ref, w_ref, b_ref, o_ref):
        uu = u_ref[0].astype(F32)
        o_ref[0] = _silu(_conv_pre(uu, _zero_tail(uu), w_ref, b_ref))

    spec = pl.BlockSpec((1, S, LANE), lambda c, bb: (bb, 0, c))
    return pl.pallas_call(
        body, grid=(C // LANE, B),
        in_specs=[pl.BlockSpec((1, S, LANE), lambda c, bb: (bb, 0, c + CONV_LANE0 // LANE)),
                  pl.BlockSpec((CONV_K, LANE), lambda c, bb: (0, c)), pl.BlockSpec((1, LANE), lambda c, bb: (0, c))],
        out_specs=spec, out_shape=jax.ShapeDtypeStruct((B, S, C), F32), name="conv_fwd",
        compiler_params=_cparams(("parallel", "arbitrary")))(src, w, b)


def _conv_bwd_call(src, w, b, g):
    B, S, _ = src.shape
    C = w.shape[1]

    def body(u_ref, w_ref, b_ref, g_ref, du_ref, dw_ref, db_ref):
        uu = u_ref[0].astype(F32)
        uz = _zero_tail(uu)
        pre = _conv_pre(uu, uz, w_ref, b_ref)
        sg = lax.logistic(pre)
        dpre = g_ref[0] * sg * (1.0 + pre * (1.0 - sg))
        dz = _zero_tail(dpre)
        du = w_ref[pl.ds(CONV_K - 1, 1), :] * dpre
        dws = [None] * CONV_K
        dws[CONV_K - 1] = jnp.sum(dpre * uu, axis=0, keepdims=True)
        for k in range(CONV_K - 1):
            sh = CONV_K - 1 - k
            du = du + w_ref[pl.ds(k, 1), :] * _shift_up(dz, sh)
            dws[k] = jnp.sum(dpre * _shift_down(uz, sh), axis=0, keepdims=True)
        du_ref[0] = du.astype(du_ref.dtype)
        dbv = jnp.sum(dpre, axis=0, keepdims=True)
        first = pl.program_id(1) == 0

        @pl.when(first)
        def _():
            for k in range(CONV_K):
                dw_ref[pl.ds(k, 1), :] = dws[k]
            db_ref[...] = dbv

        @pl.when(jnp.logical_not(first))
        def _():
            for k in range(CONV_K):
                dw_ref[pl.ds(k, 1), :] += dws[k]
            db_ref[...] += dbv

    spec = pl.BlockSpec((1, S, LANE), lambda c, bb: (bb, 0, c))
    wspec = pl.BlockSpec((CONV_K, LANE), lambda c, bb: (0, c))
    bspec = pl.BlockSpec((1, LANE), lambda c, bb: (0, c))
    uspec = pl.BlockSpec((1, S, LANE), lambda c, bb: (bb, 0, c + CONV_LANE0 // LANE))
    return pl.pallas_call(
        body, grid=(C // LANE, B), in_specs=[uspec, wspec, bspec, spec], out_specs=[spec, wspec, bspec],
        out_shape=[jax.ShapeDtypeStruct((B, S, C), BF16), jax.ShapeDtypeStruct(w.shape, F32),
                   jax.ShapeDtypeStruct(b.shape, F32)],
        name="conv_bwd", compiler_params=_cparams(("parallel", "arbitrary")))(src, w, b, g)


@jax.custom_vjp
def conv_silu(src, stand_in, w, b):
    return _conv_fwd_call(src, w, b)


def _conv_silu_fwd(src, stand_in, w, b):
    return _conv_fwd_call(src, w, b), (src, w, b)


def _conv_silu_bwd(res, g):
    du, dw, db = _conv_bwd_call(*res, g)
    return jnp.zeros_like(res[0]), du, dw, db


conv_silu.defvjp(_conv_silu_fwd, _conv_silu_bwd)


def _chunk_cumsum_call(a, reverse, name):
    B, S, W = a.shape
    per_step = min(S // CHUNK, 8)

    def body(a_ref, o_ref):
        r = lax.broadcasted_iota(jnp.int32, (CHUNK, CHUNK), 0)
        c = lax.broadcasted_iota(jnp.int32, (CHUNK, CHUNK), 1)
        tri = jnp.where((c >= r) if reverse else (c <= r), 1.0, 0.0).astype(F32)
        for i in range(per_step):
            rows = pl.ds(i * CHUNK, CHUNK)
            o_ref[0, rows, :] = jnp.dot(tri, a_ref[0, rows, :], preferred_element_type=F32,
                                        precision=lax.Precision.HIGHEST)

    spec = pl.BlockSpec((1, per_step * CHUNK, W), lambda b, c: (b, c, 0))
    return pl.pallas_call(body, grid=(B, S // (per_step * CHUNK)), in_specs=[spec], out_specs=spec,
                          out_shape=jax.ShapeDtypeStruct(a.shape, F32), name=name,
                          compiler_params=_cparams(("parallel", "parallel")))(a)


@jax.custom_vjp
def chunk_cumsum(a):
    return _chunk_cumsum_call(a, False, "chunk_cumsum_fwd")


chunk_cumsum.defvjp(lambda a: (_chunk_cumsum_call(a, False, "chunk_cumsum_fwd"), None),
                    lambda _, g: (_chunk_cumsum_call(g, True, "chunk_cumsum_bwd"),))


GROUP_W = 4 * HEAD_P
HPG = SSM_HEADS // SSM_GROUPS


def _ssd_masks():
    lane = lax.broadcasted_iota(jnp.int32, (1, GROUP_W), 1)
    return [((lane >= HEAD_P * j) & (lane < HEAD_P * (j + 1))).astype(F32) for j in range(HPG)]


def _ssd_decays(ac_cols, acr_ref, gi):
    r = lax.broadcasted_iota(jnp.int32, (CHUNK, CHUNK), 0)
    c = lax.broadcasted_iota(jnp.int32, (CHUNK, CHUNK), 1)
    return [jnp.exp(jnp.where(c <= r, ac_cols[j] - acr_ref[0, gi * HPG + j], NEG)) for j in range(HPG)]


def _ssd_cols(blk, g):
    lane = lax.broadcasted_iota(jnp.int32, blk.shape, 1)
    return [jnp.sum(jnp.where(lane == HPG * g + j, blk, 0.0), axis=1, keepdims=True) for j in range(HPG)]


def _ssd_spread(cols):
    lane = lax.broadcasted_iota(jnp.int32, (1, GROUP_W), 1)
    out = jnp.broadcast_to(cols[HPG - 1], (CHUNK, GROUP_W))
    for j in range(HPG - 2, -1, -1):
        out = jnp.where(lane < HEAD_P * (j + 1), cols[j], out)
    return out


def _ssd_gather(val, cols, masks, g):
    lane = lax.broadcasted_iota(jnp.int32, (1, LANE), 1)
    out = jnp.zeros((CHUNK, LANE), F32)
    for j in range(HPG):
        tot = jnp.sum(val * masks[j], axis=1, keepdims=True)
        if cols is not None:
            tot = tot + cols[j]
        out = out + tot * (lane == HPG * g + j).astype(F32)
    return out


def _dot(a, b, dims):
    return lax.dot_general(a.astype(BF16), b.astype(BF16), (dims, ((), ())), preferred_element_type=F32)


NN = ((1,), (0,))
NT = ((1,), (1,))
TN = ((0,), (0,))


XBC_W = GROUP_W + 2 * STATE_N


SSD_STEP_GROUPS_FWD = 4
SSD_STEP_GROUPS_BWD = 2


def _ssd_load(xbc_ref, dt_ref, ac_ref, masks, g, gi):
    x = xbc_ref[0, :, gi * XBC_W:gi * XBC_W + GROUP_W]
    bm = xbc_ref[0, :, gi * XBC_W + GROUP_W:gi * XBC_W + GROUP_W + STATE_N]
    cm = xbc_ref[0, :, gi * XBC_W + GROUP_W + STATE_N:(gi + 1) * XBC_W]
    ac_cols = _ssd_cols(ac_ref[0], g)
    dt = _ssd_spread(_ssd_cols(dt_ref[0], g))
    ac = _ssd_spread(ac_cols)
    is_last = (lax.broadcasted_iota(jnp.int32, (CHUNK, GROUP_W), 0) == CHUNK - 1).astype(F32)
    return x, bm, cm, dt, ac, ac_cols, is_last


def _ssd_in_specs(nc, rev, gb):
    cc = (lambda c: nc - 1 - c) if rev else (lambda c: c)
    return [pl.BlockSpec((1, CHUNK, gb * XBC_W), lambda b, g, c: (b, cc(c), g)),
            pl.BlockSpec((1, CHUNK, LANE), lambda b, g, c: (b, cc(c), 0)),
            pl.BlockSpec((1, CHUNK, LANE), lambda b, g, c: (b, cc(c), 0)),
            pl.BlockSpec((1, gb * HPG, 1, CHUNK), lambda b, g, c: (b, g, 0, cc(c))),
            pl.BlockSpec((1, gb * GROUP_W), lambda b, g, c: (0, g))]


def _ssd_fwd_call(xbc, dtp, acp, acr, dsk):
    B, S, _ = xbc.shape
    nc = S // CHUNK
    gb = SSD_STEP_GROUPS_FWD

    def body(xbc_ref, dt_ref, ac_ref, ar_ref, ds_ref, y_ref, hp_ref, h_sc):
        @pl.when(pl.program_id(2) == 0)
        def _():
            h_sc[...] = jnp.zeros(h_sc.shape, F32)

        masks = _ssd_masks()
        ys = []
        for gi in range(gb):
            grp = gb * pl.program_id(1) + gi
            x, bm, cm, dt, ac, ac_cols, is_last = _ssd_load(xbc_ref, dt_ref, ac_ref, masks, grp, gi)
            last = jnp.sum(ac * is_last, axis=0, keepdims=True)
            decays = _ssd_decays(ac_cols, ar_ref, gi)
            xd = x * dt
            cb = _dot(cm, bm, NT)
            hprev = h_sc[gi]
            hp_ref[0, gi, 0] = hprev
            y = _dot(cm, hprev, NN) * jnp.exp(ac) + ds_ref[:, gi * GROUP_W:(gi + 1) * GROUP_W] * x
            for j in range(HPG):
                y = y + _dot(cb * decays[j], xd * masks[j], NN)
            ys.append(y)
            h_sc[gi] = hprev * jnp.exp(last) + _dot(bm, xd * jnp.exp(last - ac), TN)
        y_ref[0] = jnp.concatenate(ys, axis=1)

    ng = SSM_GROUPS // gb
    return pl.pallas_call(
        body, grid=(B, ng, nc), in_specs=_ssd_in_specs(nc, False, gb),
        out_specs=[pl.BlockSpec((1, CHUNK, gb * GROUP_W), lambda b, g, c: (b, c, g)),
                   pl.BlockSpec((1, gb, 1, STATE_N, GROUP_W), lambda b, g, c: (b, g, c, 0, 0))],
        out_shape=[jax.ShapeDtypeStruct((B, S, D_INNER), F32),
                   jax.ShapeDtypeStruct((B, SSM_GROUPS, nc, STATE_N, GROUP_W), F32)],
        scratch_shapes=[pltpu.VMEM((gb, STATE_N, GROUP_W), F32)], name="ssd_fwd",
        compiler_params=_cparams(("parallel", "parallel", "arbitrary")))(xbc, dtp, acp, acr, dsk)


def _ssd_bwd_call(xbc, dtp, acp, acr, dsk, hps, dy):
    B, S, _ = xbc.shape
    nc = S // CHUNK
    gb = SSD_STEP_GROUPS_BWD

    def body(xbc_ref, dt_ref, ac_ref, ar_ref, ds_ref, hp_ref, dy_ref,
             dxbc_ref, ddt_ref, dac_ref, dar_ref, dds_ref, dh_sc):
        first = pl.program_id(2) == 0

        @pl.when(first)
        def _():
            dh_sc[...] = jnp.zeros(dh_sc.shape, F32)

        masks = _ssd_masks()
        dxbc_parts, dds_parts = [], []
        for gi in range(gb):
            grp = gb * pl.program_id(0) + gi
            x, bm, cm, dt, ac, ac_cols, is_last = _ssd_load(xbc_ref, dt_ref, ac_ref, masks, grp, gi)
            last = jnp.sum(ac * is_last, axis=0, keepdims=True)
            g = dy_ref[0, :, gi * GROUP_W:(gi + 1) * GROUP_W]
            hprev = hp_ref[0, gi, 0]
            dh = dh_sc[gi]
            decays = _ssd_decays(ac_cols, ar_ref, gi)
            dcols = []
            xd = x * dt
            cb = _dot(cm, bm, NT)
            e_c = jnp.exp(ac)
            e_end = jnp.exp(last - ac)
            e_last = jnp.exp(last)
            z = _dot(cm, hprev, NN)
            dz = g * e_c
            dac = g * z * e_c
            dc = _dot(dz, hprev, NT)
            dhprev = _dot(cm, dz, TN) + dh * e_last
            dcb = jnp.zeros((CHUNK, CHUNK), F32)
            dxd = jnp.zeros(xd.shape, F32)
            for j in range(HPG):
                gj = cb * decays[j]
                dgj = _dot(g * masks[j], xd, NT)
                dxd = dxd + _dot(gj, g, TN) * masks[j]
                dcb = dcb + dgj * decays[j]
                dseg = dgj * gj
                dcols.append(jnp.sum(dseg, axis=1, keepdims=True))
                dar_ref[0, gi * HPG + j] = -jnp.sum(dseg, axis=0, keepdims=True)
            dc = dc + _dot(dcb, bm, NN)
            db = _dot(dcb, cm, TN)
            sx = xd * e_end
            db = db + _dot(sx, dh, NT)
            dsx = _dot(bm, dh, NN)
            dxd = dxd + dsx * e_end
            de = dsx * sx
            dac = dac - de
            dlast = jnp.sum(de, axis=0, keepdims=True) + jnp.sum(dh * hprev, axis=0, keepdims=True) * e_last
            dsk = ds_ref[:, gi * GROUP_W:(gi + 1) * GROUP_W]
            dxbc_parts += [dxd * dt + dsk * g, db, dc]
            ddt_ref[0, gi] = _ssd_gather(dxd * x, None, masks, grp)
            dac_ref[0, gi] = _ssd_gather(dac + is_last * dlast, dcols, masks, grp)
            dds_parts.append(jnp.sum(g * x, axis=0, keepdims=True))
            dh_sc[gi] = dhprev
        dxbc_ref[0] = jnp.concatenate(dxbc_parts, axis=1)
        dds = jnp.concatenate(dds_parts, axis=1)
        first_all = first & (pl.program_id(1) == 0)

        @pl.when(first_all)
        def _():
            dds_ref[...] = dds

        @pl.when(jnp.logical_not(first_all))
        def _():
            dds_ref[...] += dds

    rc = lambda c: nc - 1 - c
    ng = SSM_GROUPS // gb
    in_specs = [pl.BlockSpec(s.block_shape, (lambda g, b, c, f=s.index_map: f(b, g, c))) for s in _ssd_in_specs(nc, True, gb)]
    in_specs.append(pl.BlockSpec((1, gb, 1, STATE_N, GROUP_W), lambda g, b, c: (b, g, rc(c), 0, 0)))
    in_specs.append(pl.BlockSpec((1, CHUNK, gb * GROUP_W), lambda g, b, c: (b, rc(c), g)))
    per_group = pl.BlockSpec((1, gb, CHUNK, LANE), lambda g, b, c: (b, g, rc(c), 0))
    out_specs = [pl.BlockSpec((1, CHUNK, gb * XBC_W), lambda g, b, c: (b, rc(c), g)), per_group, per_group,
                 pl.BlockSpec((1, gb * HPG, 1, CHUNK), lambda g, b, c: (b, g, 0, rc(c))),
                 pl.BlockSpec((1, gb * GROUP_W), lambda g, b, c: (0, g))]
    out_shape = [jax.ShapeDtypeStruct(xbc.shape, F32),
                 jax.ShapeDtypeStruct((B, SSM_GROUPS, S, LANE), F32), jax.ShapeDtypeStruct((B, SSM_GROUPS, S, LANE), F32),
                 jax.ShapeDtypeStruct(acr.shape, F32), jax.ShapeDtypeStruct(dsk.shape, F32)]
    return pl.pallas_call(
        body, grid=(ng, B, nc), in_specs=in_specs, out_specs=out_specs, out_shape=out_shape,
        scratch_shapes=[pltpu.VMEM((gb, STATE_N, GROUP_W), F32)], name="ssd_bwd",
        compiler_params=_cparams(("arbitrary", "arbitrary", "arbitrary")))(xbc, dtp, acp, acr, dsk, hps, dy)


@jax.custom_vjp
def ssd(xbc, dtp, acp, acr, dsk):
    return _ssd_fwd_call(xbc, dtp, acp, acr, dsk)[0]


def _ssd_fwd(xbc, dtp, acp, acr, dsk):
    y, hps = _ssd_fwd_call(xbc, dtp, acp, acr, dsk)
    return y, (xbc, dtp, acp, acr, dsk, hps)


def _ssd_bwd(res, dy):
    dxbc, ddt, dac, dacr, dds = _ssd_bwd_call(*res, dy)
    return dxbc, jnp.sum(ddt, axis=1), jnp.sum(dac, axis=1), dacr, dds


ssd.defvjp(_ssd_fwd, _ssd_bwd)


def _pack_small(arrs):
    flat = jnp.concatenate([a.reshape(-1) for a in arrs])
    rows = -(-flat.shape[0] // (8 * LANE)) * 8
    return jnp.pad(flat, (0, rows * LANE - flat.shape[0])).reshape(rows, LANE)


def _unpack_small(buf, shapes):
    flat = buf.reshape(-1)
    out, off = [], 0
    for shp in shapes:
        n = int(np.prod(shp))
        out.append(flat[off:off + n].reshape(shp))
        off += n
    return out


def _rows_tile(rows, cap):
    for cand in range(min(rows, cap), 7, -8):
        if rows % cand == 0:
            return cand
    return rows


def _pair_sum(mine, theirs, cidx, name):
    n4, kk, nn = mine.shape
    half = kk // 2
    tr = _rows_tile(half, 256)
    nb = half // tr

    def body(c_ref, a_ref, b_ref, o_ref, ob_ref):
        tot = a_ref[...] + b_ref[...]
        o_ref[...] = tot
        ob_ref[...] = tot.astype(BF16)

    spec = pl.BlockSpec((1, tr, nn), lambda j, i, c: (j, i, 0))
    grid_spec = pltpu.PrefetchScalarGridSpec(
        num_scalar_prefetch=1, grid=(n4, nb),
        in_specs=[pl.BlockSpec((1, tr, nn), lambda j, i, c: (j, c[0] * nb + i, 0)), spec], out_specs=[spec, spec])
    return pl.pallas_call(
        body, grid_spec=grid_spec,
        out_shape=[jax.ShapeDtypeStruct((n4, half, nn), F32), jax.ShapeDtypeStruct((n4, half, nn), BF16)],
        name=name, compiler_params=_cparams(("parallel", "parallel")))(cidx, mine, theirs)


def _chip_sum(quad, pair, chip_idx, name):
    _, rows, nn = quad.shape
    tr = _rows_tile(rows, 256)

    def body(s_ref, q_ref, p_ref, o_ref):
        for mine in range(4):
            @pl.when(s_ref[0] == mine)
            def _(mine=mine):
                acc = None
                for d in range(4):
                    term = p_ref[0] if d == mine else q_ref[d].astype(F32)
                    acc = term if acc is None else acc + term
                o_ref[...] = acc

    grid_spec = pltpu.PrefetchScalarGridSpec(
        num_scalar_prefetch=1, grid=(rows // tr,),
        in_specs=[pl.BlockSpec((4, tr, nn), lambda i, s: (0, i, 0)), pl.BlockSpec((1, tr, nn), lambda i, s: (s[0], i, 0))],
        out_specs=pl.BlockSpec((tr, nn), lambda i, s: (i, 0)))
    return pl.pallas_call(body, grid_spec=grid_spec, out_shape=jax.ShapeDtypeStruct((rows, nn), F32), name=name,
                          compiler_params=_cparams(("parallel",)))(chip_idx, quad, pair)


def _adam_halves_call(w, mine, other, cidx, m, v, name):
    rows, nn = w.shape
    half = rows // 2
    tr = _rows_tile(half, 128)
    nb = half // tr

    def body(c_ref, w_ref, a_ref, b_ref, m_ref, v_ref, g_ref, d_ref, nm_ref, nv_ref):
        upper = (pl.program_id(0) >= nb).astype(jnp.int32)
        g = jnp.where(upper == c_ref[0], a_ref[...], b_ref[...])
        g_ref[...] = g
        d_ref[...], nm_ref[...], nv_ref[...] = _adam_fn(w_ref[...], g, m_ref[...], v_ref[...])

    spec = pl.BlockSpec((tr, nn), lambda i, c: (i, 0))
    hspec = pl.BlockSpec((tr, nn), lambda i, c: (i % nb, 0))
    grid_spec = pltpu.PrefetchScalarGridSpec(num_scalar_prefetch=1, grid=(2 * nb,),
                                             in_specs=[spec, hspec, hspec, spec, spec], out_specs=[spec] * 4)
    return pl.pallas_call(body, grid_spec=grid_spec, out_shape=[jax.ShapeDtypeStruct((rows, nn), F32)] * 4, name=name,
                          compiler_params=_cparams(("parallel",)))(cidx, w, mine, other, m, v)


def _stack_sum(stack, name):
    n, rows, nn = stack.shape
    tr = _rows_tile(rows, 256)

    def body(s_ref, o_ref):
        acc = s_ref[0]
        for d in range(1, n):
            acc = acc + s_ref[d]
        o_ref[...] = acc

    return pl.pallas_call(
        body, grid=(rows // tr,), in_specs=[pl.BlockSpec((n, tr, nn), lambda i: (0, i, 0))],
        out_specs=pl.BlockSpec((tr, nn), lambda i: (i, 0)), out_shape=jax.ShapeDtypeStruct((rows, nn), F32),
        name=name, compiler_params=_cparams(("parallel",)))(stack)


def _adam_call(w, g, m, v, name):
    rows, nn = w.shape
    tr = _rows_tile(rows, 128)

    def body(w_ref, g_ref, m_ref, v_ref, d_ref, nm_ref, nv_ref):
        d_ref[...], nm_ref[...], nv_ref[...] = _adam_fn(w_ref[...], g_ref[...], m_ref[...], v_ref[...])

    spec = pl.BlockSpec((tr, nn), lambda i: (i, 0))
    sds = jax.ShapeDtypeStruct((rows, nn), F32)
    return pl.pallas_call(body, grid=(rows // tr,), in_specs=[spec] * 4, out_specs=[spec] * 3,
                          out_shape=[sds] * 3, name=name, compiler_params=_cparams(("parallel",)))(w, g, m, v)


def _adam_fn(w, g, m, v):
    m = ADAM_B1 * m + (1.0 - ADAM_B1) * g
    v = ADAM_B2 * v + (1.0 - ADAM_B2) * (g * g)
    m_hat = m / (1.0 - ADAM_B1 ** ADAM_STEP)
    v_hat = v / (1.0 - ADAM_B2 ** ADAM_STEP)
    delta = -ADAM_LR * (m_hat / (jnp.sqrt(v_hat) + ADAM_EPS) + ADAM_WD * w)
    return delta, m, v


def _mesh_pos():
    return lax.axis_index("x"), lax.axis_index("y"), lax.axis_index("c")


def _other_chips(x, y):
    return [(1 - x, y), (x, 1 - y), (1 - x, 1 - y)]


HBM_SPEC = pl.BlockSpec(memory_space=pl.ANY)


def _remote(src, dst, send_sems, recv_sems, k, to):
    return pltpu.make_async_remote_copy(src_ref=src, dst_ref=dst, send_sem=send_sems.at[k], recv_sem=recv_sems.at[k],
                                        device_id=to, device_id_type=MESH)


def _half_rows(c, rows, align):
    half = rows // 2
    return (pl.ds(pl.multiple_of(c * half, align), half), pl.ds(pl.multiple_of((1 - c) * half, align), half))


def _gather_weights(mats, conv):
    n = len(mats)

    def body(*refs):
        ins, conv_in = refs[:n], refs[n]
        outs, conv_out = refs[n + 1:2 * n + 1], refs[2 * n + 1]
        send_sems, recv_sems, local_sem = refs[2 * n + 2:]
        x, y, c = _mesh_pos()
        me, sibling, s = (x, y, c), (x, y, 1 - c), 2 * x + y
        chips = _other_chips(x, y)
        rows = [_half_rows(c, m.shape[0], 16) for m in mats]
        own = pltpu.make_async_copy(conv_in, conv_out.at[s], local_sem)
        own.start()
        sent = []
        for i in range(n):
            mine = rows[i][0]
            for j, (cx, cy) in enumerate(chips):
                sent.append(_remote(ins[i].at[mine], outs[i].at[s, mine], send_sems, recv_sems, 6 * i + j, (cx, cy, c)))
        for j, (cx, cy) in enumerate(chips):
            sent.append(_remote(conv_in, conv_out.at[s], send_sems, recv_sems, 6 * n + j, (cx, cy, c)))
        for cp in sent:
            cp.start()
        for i in range(n):
            mine = rows[i][0]
            for j, (cx, cy) in enumerate(chips):
                landed = outs[i].at[2 * cx + cy, mine]
                _remote(landed, landed, send_sems, recv_sems, 6 * i + j, me).wait_recv()
                fwd = _remote(landed, landed, send_sems, recv_sems, 6 * i + 3 + j, sibling)
                fwd.start()
                sent.append(fwd)
        for j, (cx, cy) in enumerate(chips):
            slot = conv_out.at[2 * cx + cy]
            _remote(slot, slot, send_sems, recv_sems, 6 * n + j, me).wait_recv()
        for i in range(n):
            theirs_rows = rows[i][1]
            for j, (cx, cy) in enumerate(chips):
                theirs = outs[i].at[2 * cx + cy, theirs_rows]
                _remote(theirs, theirs, send_sems, recv_sems, 6 * i + 3 + j, me).wait_recv()
        for cp in sent:
            cp.wait_send()
        own.wait()

    out_shape = [jax.ShapeDtypeStruct((4,) + m.shape, m.dtype) for m in mats]
    out_shape.append(jax.ShapeDtypeStruct((4,) + conv.shape, conv.dtype))
    res = pl.pallas_call(
        body, in_specs=[HBM_SPEC] * (n + 1), out_specs=[HBM_SPEC] * (n + 1), out_shape=out_shape,
        scratch_shapes=[pltpu.SemaphoreType.DMA((6 * n + 3,)), pltpu.SemaphoreType.DMA((6 * n + 3,)),
                        pltpu.SemaphoreType.DMA],
        name="all_gather_weights")(*mats, conv)
    return res[:n], res[n]


def _sibling_exchange(stacks):
    n = len(stacks)

    def body(*refs):
        ins, outs = refs[:n], refs[n:2 * n]
        send_sems, recv_sems = refs[2 * n:]
        x, y, c = _mesh_pos()
        cps = []
        for i in range(n):
            theirs = _half_rows(c, stacks[i].shape[1], 8)[1]
            cps.append(_remote(ins[i].at[:, theirs, :], outs[i], send_sems, recv_sems, i, (x, y, 1 - c)))
        for cp in cps:
            cp.start()
        for cp in cps:
            cp.wait()

    out_shape = [jax.ShapeDtypeStruct((4, s.shape[1] // 2, s.shape[2]), s.dtype) for s in stacks]
    return pl.pallas_call(
        body, in_specs=[HBM_SPEC] * n, out_specs=[HBM_SPEC] * n, out_shape=out_shape,
        scratch_shapes=[pltpu.SemaphoreType.DMA((n,)), pltpu.SemaphoreType.DMA((n,))],
        name="grad_sibling_exchange")(*stacks)


def _chip_exchange(parts):
    n = len(parts)

    def body(*refs):
        ins, outs = refs[:n], refs[n:2 * n]
        send_sems, recv_sems = refs[2 * n:]
        x, y, c = _mesh_pos()
        me, s = (x, y, c), 2 * x + y
        chips = _other_chips(x, y)
        sent = [_remote(ins[i].at[2 * cx + cy], outs[i].at[s], send_sems, recv_sems, 3 * i + j, (cx, cy, c))
                for i in range(n) for j, (cx, cy) in enumerate(chips)]
        for cp in sent:
            cp.start()
        for i in range(n):
            for j, (cx, cy) in enumerate(chips):
                slot = outs[i].at[2 * cx + cy]
                _remote(slot, slot, send_sems, recv_sems, 3 * i + j, me).wait_recv()
        for cp in sent:
            cp.wait_send()

    return pl.pallas_call(
        body, in_specs=[HBM_SPEC] * n, out_specs=[HBM_SPEC] * n,
        out_shape=[jax.ShapeDtypeStruct(p.shape, p.dtype) for p in parts],
        scratch_shapes=[pltpu.SemaphoreType.DMA((3 * n,)), pltpu.SemaphoreType.DMA((3 * n,))],
        name="grad_chip_exchange")(*parts)


def _sibling_swap(halves):
    n = len(halves)

    def body(*refs):
        ins, outs = refs[:n], refs[n:2 * n]
        send_sems, recv_sems = refs[2 * n:]
        x, y, c = _mesh_pos()
        cps = [_remote(ins[i], outs[i], send_sems, recv_sems, i, (x, y, 1 - c)) for i in range(n)]
        for cp in cps:
            cp.start()
        for cp in cps:
            cp.wait()

    return pl.pallas_call(
        body, in_specs=[HBM_SPEC] * n, out_specs=[HBM_SPEC] * n,
        out_shape=[jax.ShapeDtypeStruct(h.shape, h.dtype) for h in halves],
        scratch_shapes=[pltpu.SemaphoreType.DMA((n,)), pltpu.SemaphoreType.DMA((n,))],
        name="grad_sibling_swap")(*halves)


def _gather_small(vec):
    def body(in_ref, out_ref, send_sems, recv_sems, local_sem):
        x, y, c = _mesh_pos()
        me = (x, y, c)
        own = pltpu.make_async_copy(in_ref, out_ref.at[4 * x + 2 * y + c], local_sem)
        own.start()
        peers = [(1 - x if k & 4 else x, 1 - y if k & 2 else y, 1 - c if k & 1 else c) for k in range(1, 8)]
        sent = [_remote(in_ref, out_ref.at[4 * x + 2 * y + c], send_sems, recv_sems, k, p) for k, p in enumerate(peers)]
        for cp in sent:
            cp.start()
        for k, (px, py, pc) in enumerate(peers):
            slot = out_ref.at[4 * px + 2 * py + pc]
            _remote(slot, slot, send_sems, recv_sems, k, me).wait_recv()
        for cp in sent:
            cp.wait_send()
        own.wait()

    return pl.pallas_call(
        body, in_specs=[HBM_SPEC], out_specs=HBM_SPEC, out_shape=jax.ShapeDtypeStruct((8,) + vec.shape, vec.dtype),
        scratch_shapes=[pltpu.SemaphoreType.DMA((7,)), pltpu.SemaphoreType.DMA((7,)), pltpu.SemaphoreType.DMA],
        name="grad_gather_small")(vec)


def _reduce_matrices(stacks, names):
    cidx = lax.axis_index("c").astype(jnp.int32).reshape(1)
    chip = (2 * lax.axis_index("x") + lax.axis_index("y")).astype(jnp.int32).reshape(1)
    got = _sibling_exchange(stacks)
    pairs = [_pair_sum(a, b, cidx, "grad_pair_sum_" + nm) for a, b, nm in zip(stacks, got, names)]
    quads = _chip_exchange([p[1] for p in pairs])
    mine = [_chip_sum(q, p[0], chip, "grad_chip_sum_" + nm) for q, p, nm in zip(quads, pairs, names)]
    return mine, _sibling_swap(mine)


def _pad_cols(a, n):
    return jnp.concatenate([a, jnp.zeros((a.shape[0], n - a.shape[1]), a.dtype)], axis=1)


def _group_channels(a):
    lead = a.shape[:-1]
    xs = a[..., :D_INNER].reshape(lead + (SSM_GROUPS, GROUP_W))
    bs = a[..., D_INNER:D_INNER + SSM_GROUPS * STATE_N].reshape(lead + (SSM_GROUPS, STATE_N))
    cs = a[..., D_INNER + SSM_GROUPS * STATE_N:].reshape(lead + (SSM_GROUPS, STATE_N))
    return jnp.concatenate([xs, bs, cs], axis=-1).reshape(lead + (CONV_CH,))


PROJ_SEGS = (('gate_a', D_MODEL), ('gate_b', D_MODEL), ('z', D_INNER), ('xbc', CONV_CH), ('q_lat', Q_RANK),
             ('kv_lat', KV_RANK), ('k_rope', LANE), ('dt', LANE))
PROJ_WIDE = sum(w for _, w in PROJ_SEGS[:4])
PROJ_LANE0 = {n: (v if v < PROJ_WIDE else v - PROJ_WIDE) for n, v in
              zip([n for n, _ in PROJ_SEGS], [int(v) for v in np.cumsum([0] + [w for _, w in PROJ_SEGS])[:-1]])}
CONV_LANE0 = PROJ_LANE0['xbc']
KR_LANE0 = PROJ_LANE0['k_rope']


def _lay_w_in(w):
    idx = np.cumsum(IN_SIZES)[:-1]
    q_lat, kv_lat, k_rope, z, xbc, dt, gate_a, gate_b = jnp.split(w, [int(v) for v in idx], axis=1)
    return jnp.concatenate([gate_a, gate_b, z, _group_channels(xbc), q_lat, kv_lat, _pad_cols(k_rope, LANE),
                            _pad_cols(dt, LANE)], axis=1)


@jax.custom_vjp
def project(h, w, tok):
    return _project_impl(h, w)


def _project_impl(h, w):
    return (_mm(h, w[:, :PROJ_WIDE], "w_in_fwd", BF16), _mm(h, w[:, PROJ_WIDE:], "w_in_narrow_fwd")) + tuple(
        jnp.zeros((h.shape[0], wd), BF16) for _, wd in PROJ_SEGS)


def _project_fwd(h, w, tok):
    return _project_impl(h, w), (h, w)


def _project_bwd(res, cots):
    h, w = res
    g = jnp.concatenate(cots[2:], axis=1)
    return _mm(g, w.T, "w_in_dx", h.dtype), jnp.zeros_like(w), _mm_tn(h, g, "w_in_dw")


project.defvjp(_project_fwd, _project_bwd)


def _lay_w_uq(w):
    w3 = w.reshape(Q_RANK, N_HEADS, NOPE + ROPE)
    w3 = jnp.concatenate([w3, jnp.zeros((Q_RANK, N_HEADS, QK_PAD - NOPE - ROPE), w.dtype)], axis=2)
    return w3.reshape(Q_RANK, N_HEADS * QK_PAD)


def _lay_w_ukv(w):
    w3 = w.reshape(KV_RANK, N_HEADS, NOPE + V_DIM)
    return jnp.concatenate([w3[:, :, :NOPE].reshape(KV_RANK, -1), w3[:, :, NOPE:].reshape(KV_RANK, -1)], axis=1)


def _pad_lanes(v, n=LANE):
    return jnp.concatenate([v, jnp.zeros((v.shape[0], n - v.shape[1]), v.dtype)], axis=1)


def _local_loss(toks, small, x, wb, c8, posf, target):
    B, S, D = x.shape
    T = B * S

    def lin(name, a, key, lay=lambda w: w, out_dtype=F32):
        return make_linear(name, out_dtype)(a, lay(wb[key]), lay(toks[key]))

    rows2 = lambda a: a.reshape(T, a.shape[-1])
    rows3 = lambda a: a.reshape(B, S, a.shape[-1])

    sc = make_rowwise("silu_c", _f_silu, 1, 0, 0, ('row',))((c8[None],), (), ())[0][0]
    mod = lin("ada", sc, 'w_ada')[:B] + small['b_ada']
    shift1, scale1, gate1, shift2, scale2, gate2 = [m[:, None, :] for m in jnp.split(mod, 6, axis=-1)]

    modulate = make_rowwise("modulate1", _f_modulate, 1, 2, 1, ('row',))
    h = modulate((x,), (scale1, shift1), (small['g_pre_mix'],))[0]
    outs = project(rows2(h), _lay_w_in(wb['w_in']), _lay_w_in(toks['w_in']))
    wide = lax.stop_gradient(rows3(outs[0]))
    proj = lax.stop_gradient(rows3(outs[1]))
    stand = {n: rows3(o) for (n, _), o in zip(PROJ_SEGS, outs[2:])}

    def win(seg, block):
        return (PROJ_LANE0[seg] // block, dict(PROJ_SEGS)[seg])

    inv = ROPE_THETA ** (-jnp.arange(ROPE // 2, dtype=F32) / (ROPE // 2))
    inv_lane = jnp.concatenate([inv, inv, jnp.zeros((LANE - ROPE,), F32)])[None]
    tabs = tuple(_rope_tables(posf, inv_lane))
    qn = make_rowwise("rms_q", _f_rms, 1, 0, 1, ('row',), windows={0: win('q_lat', Q_RANK)})(
        (proj,), (), (small['g_q_lat'],), (stand['q_lat'],))[0]
    kvn = make_rowwise("rms_kv", _f_rms, 1, 0, 1, ('row',), windows={0: win('kv_lat', KV_RANK)})(
        (proj,), (), (small['g_kv_lat'],), (stand['kv_lat'],))[0]
    qp = rows3(lin("w_uq", rows2(qn), 'w_uq', _lay_w_uq))
    kvp = rows3(lin("w_ukv", rows2(kvn), 'w_ukv', _lay_w_ukv, BF16))
    qr = rope_q(qp, tabs)
    kr = build_k(kvp, proj, stand['k_rope'], tabs)
    att = attention(qr, kr, kvp)
    attn = rows3(lin("w_o_attn", rows2(att), 'w_o_attn'))

    xa = conv_silu(wide, stand['xbc'], _group_channels(wb['conv_w_f32']), _group_channels(small['conv_b']))
    dt_pad, a_pad = make_rowwise("dt_softplus", _f_dt, 1, 0, 2, ('row', 'row'), windows={0: win('dt', LANE)})(
        (proj,), (), (_pad_lanes(small['dt_bias']), _pad_lanes(small['a_log'])), (stand['dt'],))
    ac_pad = chunk_cumsum(a_pad)
    acr = jnp.transpose(ac_pad[..., :SSM_HEADS], (0, 2, 1))[:, :, None, :]
    dsk = jnp.repeat(small['d_skip'], HEAD_P, axis=-1)
    y = ssd(xa, dt_pad, ac_pad, acr, dsk)
    yg = make_rowwise("gated_norm", _f_gated_norm, 2, 0, 1, ('row',), ncol=SSM_GROUPS, ts_cap=2048,
                      windows={1: win('z', GROUP_W)})((y, wide), (), (small['g_ssm_out'],), (stand['z'],))[0]
    ssm = rows3(lin("w_o_ssm", rows2(yg), 'w_o_ssm'))

    merged = make_rowwise("merge", _f_merge, 4, 0, 0, ('row',),
                          windows={2: win('gate_a', D_MODEL), 3: win('gate_b', D_MODEL)})(
        (attn, ssm, wide, wide), (), (), (stand['gate_a'], stand['gate_b']))[0]
    mix = rows3(lin("w_out", rows2(merged), 'w_out'))
    x1 = make_rowwise("post_mix", _f_post, 2, 1, 1, ('row',))((x, mix), (gate1,), (small['g_post_mix'],))[0]

    h2 = make_rowwise("modulate2", _f_modulate, 1, 2, 1, ('row',))((x1,), (scale2, shift2), (small['g_pre_mlp'],))[0]
    ff = rows3(ffn(rows2(h2), wb['w_ff1'], toks['w_ff1'], wb['w_ff2'], toks['w_ff2']))
    lvec = make_rowwise("final_loss", _f_final_loss, 3, 1, 1, ('sum',), nodiff=(2,))(
        (x1, ff, target), (gate2,), (small['g_post_mlp'],))[0]
    return jnp.sum(lvec)


MATRICES = COL_SHARDED + ROW_SHARDED


def _local_step(x, c, positions, target, wb, small):
    B = x.shape[0]
    c8 = jnp.concatenate([c, jnp.zeros((16 - B, c.shape[1]), F32)], axis=0)
    posf = positions.astype(F32)[..., None]
    toks = {k: jnp.zeros(wb[k].shape, F32) for k in MATRICES if k != 'conv_w'}
    conv_w = wb['conv_w_f32']

    def loss_fn(toks, small, conv_w, x):
        wbl = dict(wb)
        wbl['conv_w_f32'] = conv_w
        return _local_loss(toks, small, x, wbl, c8, posf, target)

    loss, (g_tok, g_small, g_conv, g_x) = jax.value_and_grad(loss_fn, argnums=(0, 1, 2, 3))(toks, small, conv_w, x)
    grads = dict(g_tok)
    grads.update(g_small)
    grads['conv_w'] = g_conv
    return loss, g_x, grads


def kernel(x, c, positions, w_ada, b_ada, g_pre_mix, g_post_mix, w_in, g_q_lat, g_kv_lat, w_uq, w_ukv, w_o_attn, conv_w, conv_b, dt_bias, a_log, d_skip, g_ssm_out, w_o_ssm, w_out, g_pre_mlp, g_post_mlp, w_ff1, w_ff2, loss_target, m_w_ada, m_b_ada, m_g_pre_mix, m_g_post_mix, m_w_in, m_g_q_lat, m_g_kv_lat, m_w_uq, m_w_ukv, m_w_o_attn, m_conv_w, m_conv_b, m_dt_bias, m_a_log, m_d_skip, m_g_ssm_out, m_w_o_ssm, m_w_out, m_g_pre_mlp, m_g_post_mlp, m_w_ff1, m_w_ff2, v_w_ada, v_b_ada, v_g_pre_mix, v_g_post_mix, v_w_in, v_g_q_lat, v_g_kv_lat, v_w_uq, v_w_ukv, v_w_o_attn, v_conv_w, v_conv_b, v_dt_bias, v_a_log, v_d_skip, v_g_ssm_out, v_w_o_ssm, v_w_out, v_g_pre_mlp, v_g_post_mlp, v_w_ff1, v_w_ff2):
    given = dict(locals())
    w_loc = {n: given[n] for n in WEIGHTS}
    m_loc = {n: given["m_" + n] for n in WEIGHTS}
    v_loc = {n: given["v_" + n] for n in WEIGHTS}
    mats = [n for n in WEIGHTS if n in MATRICES and n != 'conv_w']
    vecs = [n for n in WEIGHTS if n not in MATRICES]

    own = [w_loc[n][0].astype(BF16) for n in mats]
    g_mats, g_conv = _gather_weights(own, conv_w[0])
    chip = 2 * lax.axis_index("x") + lax.axis_index("y")
    wb = {}
    for n, g, mine in zip(mats, g_mats, own):
        shards = [jnp.where(chip == j, mine, g[j]) for j in range(4)]
        wb[n] = jnp.concatenate(shards, axis=1 if n in COL_SHARDED else 0)
    wb['conv_w_f32'] = jnp.transpose(g_conv, (1, 0, 2)).reshape(CONV_K, -1)
    small = {n: w_loc[n] for n in vecs}

    loss_part, grad_x, grads = _local_step(x, c, positions, loss_target, wb, small)
    loss = lax.psum(loss_part, ("x", "y", "c"))

    stacks = []
    for n in mats:
        kk, nn = w_loc[n].shape[1:]
        if n in COL_SHARDED:
            stacks.append(jnp.transpose(grads[n].reshape(kk, 4, nn), (1, 0, 2)))
        else:
            stacks.append(grads[n].reshape(4, kk, nn))
    g_mine, g_other = _reduce_matrices(stacks, mats)
    g_shard = {}

    vec_shapes = [tuple(grads[n].shape) for n in vecs] + [tuple(grads['conv_w'].shape)]
    total = _stack_sum(_gather_small(_pack_small([grads[n] for n in vecs] + [grads['conv_w']])), "grad_sum_small")
    g_vec = _unpack_small(total, vec_shapes)
    n_conv = conv_w.shape[2]
    chip = 2 * lax.axis_index("x") + lax.axis_index("y")
    g_shard['conv_w'] = lax.dynamic_slice_in_dim(g_vec[-1], chip * n_conv, n_conv, axis=1)
    for n, g in zip(vecs, g_vec):
        g_shard[n] = g

    delta, new_m, new_v = {}, {}, {}
    cidx = lax.axis_index("c").astype(jnp.int32).reshape(1)
    for n, mine, other in zip(mats, g_mine, g_other):
        g_shard[n], delta[n], new_m[n], new_v[n] = _adam_halves_call(
            w_loc[n][0], mine, other, cidx, m_loc[n][0], v_loc[n][0], "adamw_" + n)
    rest = vecs + ['conv_w']
    rest_shapes = [tuple(w_loc[n].shape) for n in rest]
    packed = [_pack_small([src[n] for n in rest]) for src in (w_loc, g_shard, m_loc, v_loc)]
    for dst, buf in zip((delta, new_m, new_v), _adam_call(*packed, "adamw_small")):
        dst.update(zip(rest, _unpack_small(buf, rest_shapes)))

    def out(d):
        return [d[n].reshape(w_loc[n].shape) for n in WEIGHTS]

    return (loss, grad_x, *out(g_shard), *out(delta), *out(new_m), *out(new_v))
```

```python
import functools
import math

import numpy as np
import jax
import jax.numpy as jnp
from jax import lax
from jax.experimental import pallas as pl
from jax.experimental.pallas import tpu as pltpu

F32 = jnp.float32
BF16 = jnp.bfloat16
MESH = pl.DeviceIdType.MESH

D_MODEL = 1024
N_HEADS = 8
NOPE = 128
ROPE = 64
V_DIM = 128
Q_RANK = 256
KV_RANK = 256
ROPE_THETA = 10000.0
D_INNER = 2048
SSM_HEADS = 32
SSM_GROUPS = 8
HEAD_P = 64
STATE_N = 128
CONV_K = 4
CHUNK = 128
CONV_CH = D_INNER + 2 * SSM_GROUPS * STATE_N
D_FF = 4096
EPS = 1e-6
IN_SIZES = (Q_RANK, KV_RANK, ROPE, D_INNER, CONV_CH, SSM_HEADS, D_MODEL, D_MODEL)
ADAM_LR, ADAM_B1, ADAM_B2, ADAM_EPS, ADAM_WD, ADAM_STEP = 0.001, 0.9, 0.999, 1e-08, 0.01, 10

VMEM_LIMIT_BYTES = 52 * 1024 * 1024
LANE = 128
QK_PAD = 256

WEIGHTS = ['w_ada', 'b_ada', 'g_pre_mix', 'g_post_mix', 'w_in', 'g_q_lat', 'g_kv_lat', 'w_uq', 'w_ukv',
           'w_o_attn', 'conv_w', 'conv_b', 'dt_bias', 'a_log', 'd_skip', 'g_ssm_out', 'w_o_ssm', 'w_out',
           'g_pre_mlp', 'g_post_mlp', 'w_ff1', 'w_ff2']
COL_SHARDED = ('w_ada', 'w_in', 'w_uq', 'w_ukv', 'conv_w', 'w_ff1')
ROW_SHARDED = ('w_o_attn', 'w_o_ssm', 'w_out', 'w_ff2')


def _cparams(sem):
    return pltpu.CompilerParams(dimension_semantics=sem, vmem_limit_bytes=VMEM_LIMIT_BYTES)


def _tile(n, cap):
    if n <= cap:
        return n
    k = n // LANE
    best = LANE
    for d in range(1, k + 1):
        if k % d == 0 and d * LANE <= cap:
            best = d * LANE
    return best


def _mm(a, w, name, out_dtype=F32, epilogue=None, extras=(), out_dtypes=None):
    M, K = a.shape
    N = w.shape[1]
    tm = min(M, 1024)
    tn = _tile(N, 1024)
    tk = _tile(K, 2048)
    nk = K // tk
    dts = tuple(out_dtypes) if epilogue is not None else (out_dtype,)
    n_x, n_o = len(extras), len(dts)

    def finish(acc, refs):
        res = epilogue(acc, *[r[...] for r in refs[:n_x]]) if epilogue is not None else (acc,)
        for o_ref, val, dt in zip(refs[n_x:n_x + n_o], res, dts):
            o_ref[...] = val.astype(dt)

    def body(a_ref, w_ref, *refs):
        part = jnp.dot(a_ref[...].astype(BF16), w_ref[...], preferred_element_type=F32)
        if nk == 1:
            finish(part, refs)
        else:
            acc_ref = refs[-1]
            k = pl.program_id(2)

            @pl.when(k == 0)
            def _():
                acc_ref[...] = part

            @pl.when(k > 0)
            def _():
                acc_ref[...] += part

            @pl.when(k == nk - 1)
            def _():
                finish(acc_ref[...], refs)

    ospec = pl.BlockSpec((tm, tn), lambda i, j, k: (i, j))
    res = pl.pallas_call(
        body, grid=(M // tm, N // tn, nk),
        in_specs=[pl.BlockSpec((tm, tk), lambda i, j, k: (i, k)), pl.BlockSpec((tk, tn), lambda i, j, k: (k, j))]
        + [ospec] * n_x,
        out_specs=[ospec] * n_o, out_shape=[jax.ShapeDtypeStruct((M, N), dt) for dt in dts],
        scratch_shapes=[pltpu.VMEM((tm, tn), F32)] if nk > 1 else [], name=name,
        compiler_params=_cparams(("parallel", "parallel", "arbitrary")))(a, w, *extras)
    return res if epilogue is not None else res[0]


def _mm_tn(a, g, name, col_shards=1):
    M, K = a.shape
    N = g.shape[1]
    tm = min(M, 1024)
    tk = _tile(K, 1024)
    tn = _tile(N // col_shards, 1024)
    nm = M // tm
    per = N // col_shards // tn

    def body(a_ref, g_ref, o_ref):
        part = lax.dot_general(a_ref[...].astype(BF16), g_ref[...].astype(BF16), (((0,), (0,)), ((), ())),
                               preferred_element_type=F32)
        m = pl.program_id(2)

        @pl.when(m == 0)
        def _():
            o_ref[...] = part.reshape(o_ref.shape)

        @pl.when(m > 0)
        def _():
            o_ref[...] += part.reshape(o_ref.shape)

    if col_shards == 1:
        out_spec = pl.BlockSpec((tk, tn), lambda i, j, m: (i, j))
        out_shape = jax.ShapeDtypeStruct((K, N), F32)
    else:
        out_spec = pl.BlockSpec((1, tk, tn), lambda i, j, m: (j // per, i, j % per))
        out_shape = jax.ShapeDtypeStruct((col_shards, K, N // col_shards), F32)
    return pl.pallas_call(
        body, grid=(K // tk, N // tn, nm),
        in_specs=[pl.BlockSpec((tm, tk), lambda i, j, m: (m, i)), pl.BlockSpec((tm, tn), lambda i, j, m: (m, j))],
        out_specs=out_spec, out_shape=out_shape, name=name,
        compiler_params=_cparams(("parallel", "parallel", "arbitrary")))(a, g)


def make_linear(name, out_dtype=F32, dw_col_shards=1):
    @jax.custom_vjp
    def linear(a, w, tok):
        return _mm(a, w, name + "_fwd", out_dtype)

    def fwd(a, w, tok):
        return _mm(a, w, name + "_fwd", out_dtype), (a, w)

    def bwd(res, g):
        a, w = res
        da = _mm(g, w.T, name + "_dx", a.dtype)
        dw = _mm_tn(a, g, name + "_dw", dw_col_shards)
        return da, jnp.zeros_like(w), dw

    linear.defvjp(fwd, bwd)
    return linear


def _relu2_epilogue(acc):
    r = jnp.maximum(acc, 0.0)
    return r * r, r


def _relu2_bwd_epilogue(acc, r):
    return (acc * (2.0 * r.astype(F32)),)


@jax.custom_vjp
def ffn(h, w1, tok1, w2, tok2):
    act, _ = _mm(h, w1, "w_ff1_fwd", epilogue=_relu2_epilogue, out_dtypes=(BF16, BF16))
    return _mm(act, w2, "w_ff2_fwd")


def _ffn_fwd(h, w1, tok1, w2, tok2):
    act, r = _mm(h, w1, "w_ff1_fwd", epilogue=_relu2_epilogue, out_dtypes=(BF16, BF16))
    return _mm(act, w2, "w_ff2_fwd"), (h, w1, w2, act, r)


def _ffn_bwd(res, g):
    h, w1, w2, act, r = res
    du = _mm(g, w2.T, "w_ff2_dx", epilogue=_relu2_bwd_epilogue, extras=(r,), out_dtypes=(BF16,))[0]
    dw2 = _mm_tn(act, g, "w_ff2_dw")
    dw1 = _mm_tn(h, du, "w_ff1_dw", 4)
    dh = _mm(du, w1.T, "w_ff1_dx", h.dtype)
    return dh, jnp.zeros_like(w1), dw1, jnp.zeros_like(w2), dw2


ffn.defvjp(_ffn_fwd, _ffn_bwd)


def make_rowwise(name, f, n_rows, n_seqs, n_pars, out_kinds, ncol=1, nodiff=(), ts_cap=512, windows=None):
    windows = dict(windows or {})
    n_in = n_rows + n_seqs + n_pars
    diff_idx = [i for i in range(n_in) if i not in nodiff]

    def _dims(rows):
        B, S = rows[0].shape[0], rows[0].shape[1]
        ts = min(S, ts_cap)
        return B, S, ts

    def _width(i, r):
        return windows[i][1] if i in windows else r.shape[2]

    def _in_specs(rows, seqs, pars, ts):
        specs = []
        for i, r in enumerate(rows):
            col0 = windows[i][0] if i in windows else 0
            specs.append(pl.BlockSpec((1, ts, _width(i, r) // ncol), lambda k, b, s, col0=col0: (b, s, k + col0)))
        for q in seqs:
            specs.append(pl.BlockSpec((1, 1, q.shape[2] // ncol), lambda k, b, s: (b, 0, k)))
        for p in pars:
            specs.append(pl.BlockSpec((1, p.shape[1] // ncol), lambda k, b, s: (0, k)))
        return specs

    def _load(refs):
        vals = [r[0] for r in refs[:n_rows + n_seqs]]
        vals += [r[...] for r in refs[n_rows + n_seqs:n_in]]
        return vals

    def _out_struct(rows, seqs, pars, ts):
        blocks = [jax.ShapeDtypeStruct((ts, _width(i, r) // ncol), r.dtype) for i, r in enumerate(rows)]
        blocks += [jax.ShapeDtypeStruct((1, q.shape[2] // ncol), q.dtype) for q in seqs]
        blocks += [jax.ShapeDtypeStruct((1, p.shape[1] // ncol), p.dtype) for p in pars]
        return jax.eval_shape(f, *blocks)

    def _fwd_call(rows, seqs, pars):
        B, S, ts = _dims(rows)
        outs = _out_struct(rows, seqs, pars, ts)
        n_out = len(outs)

        def body(*refs):
            res = f(*_load(refs))
            first = (pl.program_id(1) == 0) & (pl.program_id(2) == 0)
            for o_ref, val, kind in zip(refs[n_in:], res, out_kinds):
                if kind == 'row':
                    o_ref[0] = val
                else:
                    tot = jnp.sum(val, axis=0, keepdims=True)

                    @pl.when(first)
                    def _(o_ref=o_ref, tot=tot):
                        o_ref[...] = tot

                    @pl.when(jnp.logical_not(first))
                    def _(o_ref=o_ref, tot=tot):
                        o_ref[...] += tot

        out_shape, out_specs = [], []
        for o, kind in zip(outs, out_kinds):
            d = o.shape[1]
            if kind == 'row':
                out_shape.append(jax.ShapeDtypeStruct((B, S, ncol * d), o.dtype))
                out_specs.append(pl.BlockSpec((1, ts, d), lambda k, b, s: (b, s, k)))
            else:
                out_shape.append(jax.ShapeDtypeStruct((1, ncol * d), o.dtype))
                out_specs.append(pl.BlockSpec((1, d), lambda k, b, s: (0, k)))
        res = pl.pallas_call(
            body, grid=(ncol, B, S // ts), in_specs=_in_specs(rows, seqs, pars, ts), out_specs=out_specs,
            out_shape=out_shape, name=name + "_fwd",
            compiler_params=_cparams(("arbitrary", "arbitrary", "arbitrary")))(*rows, *seqs, *pars)
        return tuple(res)

    def _bwd_call(rows, seqs, pars, cots):
        B, S, ts = _dims(rows)
        outs = _out_struct(rows, seqs, pars, ts)
        n_out = len(outs)
        all_in = list(rows) + list(seqs) + list(pars)

        def body(*refs):
            vals = _load(refs)
            cts = []
            for c_ref, o, kind in zip(refs[n_in:n_in + n_out], outs, out_kinds):
                if kind == 'row':
                    cts.append(c_ref[0])
                else:
                    cts.append(jnp.broadcast_to(c_ref[...], o.shape))

            def g(*dv):
                full = list(vals)
                for i, v in zip(diff_idx, dv):
                    full[i] = v
                return tuple(f(*full))

            _, vjp = jax.vjp(g, *[vals[i] for i in diff_idx])
            grads = vjp(tuple(cts))
            b, s = pl.program_id(1), pl.program_id(2)
            for o_ref, i, gr in zip(refs[n_in + n_out:], diff_idx, grads):
                if i < n_rows:
                    o_ref[0] = gr.astype(o_ref.dtype)
                else:
                    first = (s == 0) if i < n_rows + n_seqs else ((b == 0) & (s == 0))
                    target = (lambda r: r.at[0]) if i < n_rows + n_seqs else (lambda r: r)

                    @pl.when(first)
                    def _(o_ref=o_ref, gr=gr, target=target):
                        target(o_ref)[...] = gr

                    @pl.when(jnp.logical_not(first))
                    def _(o_ref=o_ref, gr=gr, target=target):
                        target(o_ref)[...] += gr

        cot_specs = []
        for o, kind in zip(outs, out_kinds):
            d = o.shape[1]
            if kind == 'row':
                cot_specs.append(pl.BlockSpec((1, ts, d), lambda k, b, s: (b, s, k)))
            else:
                cot_specs.append(pl.BlockSpec((1, d), lambda k, b, s: (0, k)))
        out_shape, out_specs = [], []
        for i in diff_idx:
            a = all_in[i]
            if i < n_rows:
                out_shape.append(jax.ShapeDtypeStruct((B, S, _width(i, a)), BF16 if i in windows else a.dtype))
                out_specs.append(pl.BlockSpec((1, ts, _width(i, a) // ncol), lambda k, b, s: (b, s, k)))
                continue
            out_shape.append(jax.ShapeDtypeStruct(a.shape, a.dtype))
            if i < n_rows + n_seqs:
                out_specs.append(pl.BlockSpec((1, 1, a.shape[2] // ncol), lambda k, b, s: (b, 0, k)))
            else:
                out_specs.append(pl.BlockSpec((1, a.shape[1] // ncol), lambda k, b, s: (0, k)))
        res = pl.pallas_call(
            body, grid=(ncol, B, S // ts), in_specs=_in_specs(rows, seqs, pars, ts) + cot_specs,
            out_specs=out_specs, out_shape=out_shape, name=name + "_bwd",
            compiler_params=_cparams(("arbitrary", "arbitrary", "arbitrary")))(*all_in, *cots)
        grads = [None] * n_in
        for i, r in zip(diff_idx, res):
            grads[i] = r
        for i in nodiff:
            grads[i] = jnp.zeros_like(all_in[i])
        stand_in_grads = tuple(grads[i] for i in sorted(windows))
        for i in windows:
            grads[i] = jnp.zeros_like(all_in[i])
        return (tuple(grads[:n_rows]), tuple(grads[n_rows:n_rows + n_seqs]), tuple(grads[n_rows + n_seqs:]),
                stand_in_grads)

    @jax.custom_vjp
    def op(rows, seqs, pars, stand_ins):
        return _fwd_call(rows, seqs, pars)

    def fwd(rows, seqs, pars, stand_ins):
        return _fwd_call(rows, seqs, pars), (rows, seqs, pars)

    def bwd(res, cots):
        rows, seqs, pars = res
        return _bwd_call(rows, seqs, pars, cots)

    op.defvjp(fwd, bwd)
    return lambda rows, seqs, pars, stand_ins=(): op(tuple(rows), tuple(seqs), tuple(pars), tuple(stand_ins))


def _rms(x, g):
    return x * lax.rsqrt(jnp.mean(x * x, axis=-1, keepdims=True) + EPS) * g


def _silu(x):
    return x * lax.logistic(x)


def _f_silu(c):
    return (_silu(c),)


def _f_modulate(x, scale, shift, g):
    return ((_rms(x, g) * (1.0 + scale) + shift).astype(BF16),)


def _f_rms(x, g):
    return (_rms(x, g).astype(BF16),)


def _f_dt(dt_raw, dt_bias, a_log):
    z = dt_raw + dt_bias
    dt = jnp.maximum(z, 0.0) + jnp.log1p(jnp.exp(-jnp.abs(z)))
    return dt, dt * (-jnp.exp(a_log))


def _f_gated_norm(y, z, g):
    return (_rms(y * _silu(z.astype(F32)), g).astype(BF16),)


def _f_merge(attn, ssm, ga, gb):
    return ((lax.logistic(ga.astype(F32)) * attn + lax.logistic(gb.astype(F32)) * ssm).astype(BF16),)


def _f_post(x, m, gate, g):
    return (x + gate * _rms(m, g),)


def _f_final_loss(x, ff, target, gate, g):
    e = x + gate * _rms(ff, g) - target
    return (e * e * (0.5 / D_MODEL),)


def _rope_tables(posf, inv_lane):
    B, S, _ = posf.shape
    ts = min(S, 512)

    def body(p_ref, inv_ref, c_ref, a_ref, b_ref):
        ang = p_ref[0] * inv_ref[...]
        cs, sn = jnp.cos(ang), jnp.sin(ang)
        lane = lax.broadcasted_iota(jnp.int32, ang.shape, 1)
        c_ref[0] = jnp.where(lane < ROPE, cs, 0.0)
        a_ref[0] = jnp.where(lane < ROPE // 2, -sn, 0.0)
        b_ref[0] = jnp.where((lane >= ROPE // 2) & (lane < ROPE), sn, 0.0)

    spec = pl.BlockSpec((1, ts, LANE), lambda b, s: (b, s, 0))
    sds = jax.ShapeDtypeStruct((B, S, LANE), F32)
    return pl.pallas_call(
        body, grid=(B, S // ts),
        in_specs=[pl.BlockSpec((1, ts, 1), lambda b, s: (b, s, 0)), pl.BlockSpec((1, LANE), lambda b, s: (0, 0))],
        out_specs=[spec, spec, spec], out_shape=[sds, sds, sds], name="rope_tables",
        compiler_params=_cparams(("parallel", "parallel")))(posf, inv_lane)


def _rot(u, c, a, bm):
    return u * c + pltpu.roll(u, 96, 1) * a + pltpu.roll(u, 32, 1) * bm


def _rot_t(g, c, a, bm):
    return g * c + pltpu.roll(g * a, 32, 1) + pltpu.roll(g * bm, 96, 1)


def _rope_q_call(q, tabs, transpose, name):
    B, S, W = q.shape
    ts = min(S, 512)
    fn = _rot_t if transpose else _rot
    out_dtype = F32 if transpose else BF16

    def body(q_ref, c_ref, a_ref, b_ref, o_ref):
        tc, ta, tb = c_ref[0], a_ref[0], b_ref[0]
        for h in range(W // QK_PAD):
            u = q_ref[0, :, h * QK_PAD:(h + 1) * QK_PAD].astype(F32) * ATT_SCALE
            r = fn(u[:, NOPE:], tc, ta, tb)
            o_ref[0, :, h * QK_PAD:(h + 1) * QK_PAD] = jnp.concatenate([u[:, :NOPE], r], axis=1).astype(out_dtype)

    tspec = pl.BlockSpec((1, ts, LANE), lambda b, s: (b, s, 0))
    qspec = pl.BlockSpec((1, ts, W), lambda b, s: (b, s, 0))
    return pl.pallas_call(
        body, grid=(B, S // ts), in_specs=[qspec, tspec, tspec, tspec], out_specs=qspec,
        out_shape=jax.ShapeDtypeStruct(q.shape, out_dtype), name=name,
        compiler_params=_cparams(("parallel", "parallel")))(q, *tabs)


@jax.custom_vjp
def rope_q(q, tabs):
    return _rope_q_call(q, tabs, False, "rope_q_fwd")


def _rope_q_fwd(q, tabs):
    return _rope_q_call(q, tabs, False, "rope_q_fwd"), tabs


def _rope_q_bwd(tabs, g):
    return _rope_q_call(g, tabs, True, "rope_q_bwd"), tuple(jnp.zeros_like(t) for t in tabs)


rope_q.defvjp(_rope_q_fwd, _rope_q_bwd)


def _build_k_fwd_call(kv, kr, tabs):
    B, S, _ = kv.shape
    ts = min(S, 512)

    def body(kv_ref, kr_ref, c_ref, a_ref, b_ref, o_ref):
        r = _rot(kr_ref[0], c_ref[0], a_ref[0], b_ref[0]).astype(BF16)
        for h in range(N_HEADS):
            o_ref[0, :, h * QK_PAD:(h + 1) * QK_PAD] = jnp.concatenate(
                [kv_ref[0, :, h * NOPE:(h + 1) * NOPE], r], axis=1)

    tspec = pl.BlockSpec((1, ts, LANE), lambda b, s: (b, s, 0))
    kr_spec = pl.BlockSpec((1, ts, LANE), lambda b, s: (b, s, KR_LANE0 // LANE))
    return pl.pallas_call(
        body, grid=(B, S // ts),
        in_specs=[pl.BlockSpec((1, ts, N_HEADS * NOPE), lambda b, s: (b, s, 0)), kr_spec, tspec, tspec, tspec],
        out_specs=pl.BlockSpec((1, ts, N_HEADS * QK_PAD), lambda b, s: (b, s, 0)),
        out_shape=jax.ShapeDtypeStruct((B, S, N_HEADS * QK_PAD), BF16), name="build_k_fwd",
        compiler_params=_cparams(("parallel", "parallel")))(kv, kr, *tabs)


def _build_k_bwd_call(g, tabs):
    B, S, _ = g.shape
    ts = min(S, 512)

    def body(g_ref, c_ref, a_ref, b_ref, dk_ref, dr_ref):
        tot = None
        for h in range(N_HEADS):
            dk_ref[0, :, h * NOPE:(h + 1) * NOPE] = g_ref[0, :, h * QK_PAD:h * QK_PAD + NOPE]
            part = g_ref[0, :, h * QK_PAD + NOPE:(h + 1) * QK_PAD].astype(F32)
            tot = part if tot is None else tot + part
        dr_ref[0] = _rot_t(tot, c_ref[0], a_ref[0], b_ref[0]).astype(BF16)

    tspec = pl.BlockSpec((1, ts, LANE), lambda b, s: (b, s, 0))
    return pl.pallas_call(
        body, grid=(B, S // ts),
        in_specs=[pl.BlockSpec((1, ts, N_HEADS * QK_PAD), lambda b, s: (b, s, 0)), tspec, tspec, tspec],
        out_specs=[pl.BlockSpec((1, ts, N_HEADS * NOPE), lambda b, s: (b, s, 0)), tspec],
        out_shape=[jax.ShapeDtypeStruct((B, S, N_HEADS * NOPE), BF16), jax.ShapeDtypeStruct((B, S, LANE), BF16)],
        name="build_k_bwd", compiler_params=_cparams(("parallel", "parallel")))(g, *tabs)


@jax.custom_vjp
def build_k(kv, src, stand_in, tabs):
    return _build_k_fwd_call(kv, src, tabs)


def _build_k_fwd(kv, src, stand_in, tabs):
    return _build_k_fwd_call(kv, src, tabs), (tabs, kv.shape, src)


def _build_k_bwd(res, g):
    tabs, kv_shape, src = res
    dk, dr = _build_k_bwd_call(g, tabs)
    dkv = jnp.concatenate([dk, jnp.zeros((kv_shape[0], kv_shape[1], kv_shape[2] - dk.shape[2]), BF16)], axis=-1)
    return dkv, jnp.zeros_like(src), dr, tuple(jnp.zeros_like(t) for t in tabs)


build_k.defvjp(_build_k_fwd, _build_k_bwd)


ATT_SCALE = (NOPE + ROPE) ** -0.5
NEG = -1e30


def _att_tiles(S):
    t = min(S, 512)
    return t, S // t


def _scores(q, k, diagonal):
    s = lax.dot_general(q, k, (((1,), (1,)), ((), ())), preferred_element_type=F32)
    if diagonal:
        row = lax.broadcasted_iota(jnp.int32, s.shape, 0)
        col = lax.broadcasted_iota(jnp.int32, s.shape, 1)
        s = jnp.where(col <= row, s, NEG)
    return s


ATT_HB = 4


def _causal_pairs(n):
    pairs = [(i, j) for i in range(n) for j in range(i + 1)]
    return (jnp.asarray([p[0] for p in pairs], jnp.int32), jnp.asarray([p[1] for p in pairs], jnp.int32))


def _head(ref_or_val, h, w):
    return ref_or_val[:, h * w:(h + 1) * w]


def _attn_fwd_call(q, k, vsrc, v_blk0):
    B, S, _ = q.shape
    t, n = _att_tiles(S)
    qi, kj = _causal_pairs(n)

    def body(qi_ref, kj_ref, q_ref, k_ref, v_ref, o_ref, lse_ref, m_sc, l_sc, acc_sc):
        p_id = pl.program_id(2)
        i, j = qi_ref[p_id], kj_ref[p_id]

        @pl.when(j == 0)
        def _():
            m_sc[...] = jnp.full(m_sc.shape, NEG, F32)
            l_sc[...] = jnp.zeros(l_sc.shape, F32)
            acc_sc[...] = jnp.zeros(acc_sc.shape, F32)

        def step(diagonal):
            qa, ka, va = q_ref[0], k_ref[0], v_ref[0]
            for h in range(ATT_HB):
                lanes = slice(h * LANE, (h + 1) * LANE)
                s = _scores(_head(qa, h, QK_PAD), _head(ka, h, QK_PAD), diagonal)
                m_prev = m_sc[:, lanes]
                m_new = jnp.maximum(m_prev, jnp.max(s, axis=1, keepdims=True))
                alpha = jnp.exp(m_prev - m_new)
                p = jnp.exp(s - jnp.tile(m_new, (1, t // LANE)))
                l_sc[:, lanes] = alpha * l_sc[:, lanes] + jnp.sum(p, axis=1, keepdims=True)
                acc_sc[:, lanes] = alpha * acc_sc[:, lanes] + jnp.dot(p.astype(BF16), _head(va, h, V_DIM),
                                                                      preferred_element_type=F32)
                m_sc[:, lanes] = m_new

        @pl.when(j < i)
        def _():
            step(False)

        @pl.when(j == i)
        def _():
            step(True)
            o_ref[0] = acc_sc[...] / l_sc[...]
            lse_ref[0] = m_sc[...] + jnp.log(l_sc[...])

    wq, wv = ATT_HB * QK_PAD, ATT_HB * V_DIM
    grid_spec = pltpu.PrefetchScalarGridSpec(
        num_scalar_prefetch=2, grid=(B, N_HEADS // ATT_HB, qi.shape[0]),
        in_specs=[pl.BlockSpec((1, t, wq), lambda b, h, p, qi, kj: (b, qi[p], h)),
                  pl.BlockSpec((1, t, wq), lambda b, h, p, qi, kj: (b, kj[p], h)),
                  pl.BlockSpec((1, t, wv), lambda b, h, p, qi, kj: (b, kj[p], v_blk0 + h))],
        out_specs=[pl.BlockSpec((1, t, wv), lambda b, h, p, qi, kj: (b, qi[p], h)),
                   pl.BlockSpec((1, t, wv), lambda b, h, p, qi, kj: (b, qi[p], h))],
        scratch_shapes=[pltpu.VMEM((t, wv), F32), pltpu.VMEM((t, wv), F32), pltpu.VMEM((t, wv), F32)])
    return pl.pallas_call(
        body, grid_spec=grid_spec,
        out_shape=[jax.ShapeDtypeStruct((B, S, N_HEADS * V_DIM), F32),
                   jax.ShapeDtypeStruct((B, S, N_HEADS * LANE), F32)],
        name="attn_fwd", compiler_params=_cparams(("parallel", "parallel", "arbitrary")))(qi, kj, q, k, vsrc)


def _attn_p_ds(q, k, v, o, do, lse, diagonal, t):
    s = _scores(q, k, diagonal)
    p = jnp.exp(s - jnp.tile(lse, (1, t // LANE)))
    dp = lax.dot_general(do.astype(BF16), v, (((1,), (1,)), ((), ())), preferred_element_type=F32)
    delta = jnp.sum(do * o, axis=1, keepdims=True)
    ds = p * (dp - delta)
    return p, ds


ATT_HB_BWD = 2


def _attn_bwd_call(q, k, vsrc, o, do, lse):
    B, S, _ = q.shape
    t, n = _att_tiles(S)
    qi, kj = _causal_pairs(n)
    n_pairs = qi.shape[0]
    hb = ATT_HB_BWD
    v_blk0 = N_HEADS // hb

    def body(qi_ref, kj_ref, q_ref, k_ref, v_ref, o_ref, do_ref, lse_ref, dq_ref, dk_ref, dv_ref, dq_sc, dk_sc, dv_sc):
        p_id = pl.program_id(2)
        i, j = qi_ref[p_id], kj_ref[p_id]

        @pl.when(p_id == 0)
        def _():
            dk_sc[...] = jnp.zeros(dk_sc.shape, F32)
            dv_sc[...] = jnp.zeros(dv_sc.shape, F32)

        @pl.when(j == 0)
        def _():
            dq_sc[...] = jnp.zeros(dq_sc.shape, F32)

        rows = pl.ds(pl.multiple_of(j * t, t), t)

        def step(diagonal):
            qa, ka, va, oa, doa, la = q_ref[0], k_ref[0], v_ref[0], o_ref[0], do_ref[0], lse_ref[0]
            for h in range(hb):
                qb, kb, dob = _head(qa, h, QK_PAD), _head(ka, h, QK_PAD), _head(doa, h, V_DIM)
                p, ds = _attn_p_ds(qb, kb, _head(va, h, V_DIM), _head(oa, h, V_DIM), dob, _head(la, h, LANE),
                                   diagonal, t)
                dsb = ds.astype(BF16)
                dq_sc[:, h * QK_PAD:(h + 1) * QK_PAD] += jnp.dot(dsb, kb, preferred_element_type=F32)
                dv_sc[rows, h * V_DIM:(h + 1) * V_DIM] += lax.dot_general(
                    p.astype(BF16), dob.astype(BF16), (((0,), (0,)), ((), ())), preferred_element_type=F32)
                dk_sc[rows, h * QK_PAD:(h + 1) * QK_PAD] += lax.dot_general(
                    dsb, qb, (((0,), (0,)), ((), ())), preferred_element_type=F32)

        @pl.when(j < i)
        def _():
            step(False)

        @pl.when(j == i)
        def _():
            step(True)
            dq_ref[0] = dq_sc[...].astype(BF16)

        @pl.when(p_id == n_pairs - 1)
        def _():
            dk_ref[0] = dk_sc[...].astype(BF16)
            dv_ref[0] = dv_sc[...].astype(BF16)

    wq, wv = hb * QK_PAD, hb * V_DIM
    at_q = lambda b, h, p, qi, kj: (b, qi[p], h)
    at_k = lambda b, h, p, qi, kj: (b, kj[p], h)
    whole = lambda b, h, p, qi, kj: (b, 0, h)
    grid_spec = pltpu.PrefetchScalarGridSpec(
        num_scalar_prefetch=2, grid=(B, N_HEADS // hb, n_pairs),
        in_specs=[pl.BlockSpec((1, t, wq), at_q), pl.BlockSpec((1, t, wq), at_k),
                  pl.BlockSpec((1, t, wv), lambda b, h, p, qi, kj: (b, kj[p], v_blk0 + h)),
                  pl.BlockSpec((1, t, wv), at_q), pl.BlockSpec((1, t, wv), at_q), pl.BlockSpec((1, t, wv), at_q)],
        out_specs=[pl.BlockSpec((1, t, wq), at_q), pl.BlockSpec((1, S, wq), whole), pl.BlockSpec((1, S, wv), whole)],
        scratch_shapes=[pltpu.VMEM((t, wq), F32), pltpu.VMEM((S, wq), F32), pltpu.VMEM((S, wv), F32)])
    return pl.pallas_call(
        body, grid_spec=grid_spec,
        out_shape=[jax.ShapeDtypeStruct((B, S, N_HEADS * QK_PAD), BF16),
                   jax.ShapeDtypeStruct((B, S, N_HEADS * QK_PAD), BF16),
                   jax.ShapeDtypeStruct((B, S, N_HEADS * V_DIM), BF16)],
        name="attn_bwd", compiler_params=_cparams(("parallel", "parallel", "arbitrary")))(
            qi, kj, q, k, vsrc, o, do, lse)


@jax.custom_vjp
def attention(q, k, kv):
    return _attn_fwd_call(q, k, kv, N_HEADS // ATT_HB)[0]


def _attention_fwd(q, k, kv):
    o, lse = _attn_fwd_call(q, k, kv, N_HEADS // ATT_HB)
    return o, (q, k, kv, o, lse)


def _attention_bwd(res, do):
    q, k, kv, o, lse = res
    dq, dk, dv = _attn_bwd_call(q, k, kv, o, do, lse)
    dkv = jnp.concatenate([jnp.zeros_like(dv), dv], axis=-1)
    return dq, dk, dkv


attention.defvjp(_attention_fwd, _attention_bwd)


SUBLANES = 8


def _zero_tail(v):
    return jnp.concatenate([v, jnp.zeros((SUBLANES, v.shape[1]), v.dtype)], axis=0)


def _shift_down(vz, sh):
    return pltpu.roll(vz, sh, 0)[:vz.shape[0] - SUBLANES]


def _shift_up(vz, sh):
    return pltpu.roll(vz, vz.shape[0] - sh, 0)[:vz.shape[0] - SUBLANES]


def _conv_pre(u, uz, w_ref, b_ref):
    acc = b_ref[...] + w_ref[pl.ds(CONV_K - 1, 1), :] * u
    for k in range(CONV_K - 1):
        acc = acc + w_ref[pl.ds(k, 1), :] * _shift_down(uz, CONV_K - 1 - k)
    return acc


def _conv_fwd_call(src, w, b):
    B, S, _ = src.shape
    C = w.shape[1]

    def body(u_ref, w_ref, b_ref, o_ref):
        uu = u_ref[0].astype(F32)
        o_ref[0] = _silu(_conv_pre(uu, _zero_tail(uu), w_ref, b_ref))

    spec = pl.BlockSpec((1, S, LANE), lambda c, bb: (bb, 0, c))
    return pl.pallas_call(
        body, grid=(C // LANE, B),
        in_specs=[pl.BlockSpec((1, S, LANE), lambda c, bb: (bb, 0, c + CONV_LANE0 // LANE)),
                  pl.BlockSpec((CONV_K, LANE), lambda c, bb: (0, c)), pl.BlockSpec((1, LANE), lambda c, bb: (0, c))],
        out_specs=spec, out_shape=jax.ShapeDtypeStruct((B, S, C), F32), name="conv_fwd",
        compiler_params=_cparams(("parallel", "arbitrary")))(src, w, b)


def _conv_bwd_call(src, w, b, g):
    B, S, _ = src.shape
    C = w.shape[1]

    def body(u_ref, w_ref, b_ref, g_ref, du_ref, dw_ref, db_ref):
        uu = u_ref[0].astype(F32)
        uz = _zero_tail(uu)
        pre = _conv_pre(uu, uz, w_ref, b_ref)
        sg = lax.logistic(pre)
        dpre = g_ref[0] * sg * (1.0 + pre * (1.0 - sg))
        dz = _zero_tail(dpre)
        du = w_ref[pl.ds(CONV_K - 1, 1), :] * dpre
        dws = [None] * CONV_K
        dws[CONV_K - 1] = jnp.sum(dpre * uu, axis=0, keepdims=True)
        for k in range(CONV_K - 1):
            sh = CONV_K - 1 - k
            du = du + w_ref[pl.ds(k, 1), :] * _shift_up(dz, sh)
            dws[k] = jnp.sum(dpre * _shift_down(uz, sh), axis=0, keepdims=True)
        du_ref[0] = du.astype(du_ref.dtype)
        dbv = jnp.sum(dpre, axis=0, keepdims=True)
        first = pl.program_id(1) == 0

        @pl.when(first)
        def _():
            for k in range(CONV_K):
                dw_ref[pl.ds(k, 1), :] = dws[k]
            db_ref[...] = dbv

        @pl.when(jnp.logical_not(first))
        def _():
            for k in range(CONV_K):
                dw_ref[pl.ds(k, 1), :] += dws[k]
            db_ref[...] += dbv

    spec = pl.BlockSpec((1, S, LANE), lambda c, bb: (bb, 0, c))
    wspec = pl.BlockSpec((CONV_K, LANE), lambda c, bb: (0, c))
    bspec = pl.BlockSpec((1, LANE), lambda c, bb: (0, c))
    uspec = pl.BlockSpec((1, S, LANE), lambda c, bb: (bb, 0, c + CONV_LANE0 // LANE))
    return pl.pallas_call(
        body, grid=(C // LANE, B), in_specs=[uspec, wspec, bspec, spec], out_specs=[spec, wspec, bspec],
        out_shape=[jax.ShapeDtypeStruct((B, S, C), BF16), jax.ShapeDtypeStruct(w.shape, F32),
                   jax.ShapeDtypeStruct(b.shape, F32)],
        name="conv_bwd", compiler_params=_cparams(("parallel", "arbitrary")))(src, w, b, g)


@jax.custom_vjp
def conv_silu(src, stand_in, w, b):
    return _conv_fwd_call(src, w, b)


def _conv_silu_fwd(src, stand_in, w, b):
    return _conv_fwd_call(src, w, b), (src, w, b)


def _conv_silu_bwd(res, g):
    du, dw, db = _conv_bwd_call(*res, g)
    return jnp.zeros_like(res[0]), du, dw, db


conv_silu.defvjp(_conv_silu_fwd, _conv_silu_bwd)


def _chunk_cumsum_call(a, reverse, name):
    B, S, W = a.shape
    per_step = min(S // CHUNK, 8)

    def body(a_ref, o_ref):
        r = lax.broadcasted_iota(jnp.int32, (CHUNK, CHUNK), 0)
        c = lax.broadcasted_iota(jnp.int32, (CHUNK, CHUNK), 1)
        tri = jnp.where((c >= r) if reverse else (c <= r), 1.0, 0.0).astype(F32)
        for i in range(per_step):
            rows = pl.ds(i * CHUNK, CHUNK)
            o_ref[0, rows, :] = jnp.dot(tri, a_ref[0, rows, :], preferred_element_type=F32,
                                        precision=lax.Precision.HIGHEST)

    spec = pl.BlockSpec((1, per_step * CHUNK, W), lambda b, c: (b, c, 0))
    return pl.pallas_call(body, grid=(B, S // (per_step * CHUNK)), in_specs=[spec], out_specs=spec,
                          out_shape=jax.ShapeDtypeStruct(a.shape, F32), name=name,
                          compiler_params=_cparams(("parallel", "parallel")))(a)


@jax.custom_vjp
def chunk_cumsum(a):
    return _chunk_cumsum_call(a, False, "chunk_cumsum_fwd")


chunk_cumsum.defvjp(lambda a: (_chunk_cumsum_call(a, False, "chunk_cumsum_fwd"), None),
                    lambda _, g: (_chunk_cumsum_call(g, True, "chunk_cumsum_bwd"),))


GROUP_W = 4 * HEAD_P
HPG = SSM_HEADS // SSM_GROUPS


def _ssd_masks():
    lane = lax.broadcasted_iota(jnp.int32, (1, GROUP_W), 1)
    return [((lane >= HEAD_P * j) & (lane < HEAD_P * (j + 1))).astype(F32) for j in range(HPG)]


def _ssd_decays(ac_cols, acr_ref, gi):
    r = lax.broadcasted_iota(jnp.int32, (CHUNK, CHUNK), 0)
    c = lax.broadcasted_iota(jnp.int32, (CHUNK, CHUNK), 1)
    return [jnp.exp(jnp.where(c <= r, ac_cols[j] - acr_ref[0, gi * HPG + j], NEG)) for j in range(HPG)]


def _ssd_cols(blk, g):
    lane = lax.broadcasted_iota(jnp.int32, blk.shape, 1)
    return [jnp.sum(jnp.where(lane == HPG * g + j, blk, 0.0), axis=1, keepdims=True) for j in range(HPG)]


def _ssd_spread(cols):
    lane = lax.broadcasted_iota(jnp.int32, (1, GROUP_W), 1)
    out = jnp.broadcast_to(cols[HPG - 1], (CHUNK, GROUP_W))
    for j in range(HPG - 2, -1, -1):
        out = jnp.where(lane < HEAD_P * (j + 1), cols[j], out)
    return out


def _ssd_gather(val, cols, masks, g):
    lane = lax.broadcasted_iota(jnp.int32, (1, LANE), 1)
    out = jnp.zeros((CHUNK, LANE), F32)
    for j in range(HPG):
        tot = jnp.sum(val * masks[j], axis=1, keepdims=True)
        if cols is not None:
            tot = tot + cols[j]
        out = out + tot * (lane == HPG * g + j).astype(F32)
    return out


def _dot(a, b, dims):
    return lax.dot_general(a.astype(BF16), b.astype(BF16), (dims, ((), ())), preferred_element_type=F32)


NN = ((1,), (0,))
NT = ((1,), (1,))
TN = ((0,), (0,))


XBC_W = GROUP_W + 2 * STATE_N


SSD_STEP_GROUPS_FWD = 4
SSD_STEP_GROUPS_BWD = 2


def _ssd_load(xbc_ref, dt_ref, ac_ref, masks, g, gi):
    x = xbc_ref[0, :, gi * XBC_W:gi * XBC_W + GROUP_W]
    bm = xbc_ref[0, :, gi * XBC_W + GROUP_W:gi * XBC_W + GROUP_W + STATE_N]
    cm = xbc_ref[0, :, gi * XBC_W + GROUP_W + STATE_N:(gi + 1) * XBC_W]
    ac_cols = _ssd_cols(ac_ref[0], g)
    dt = _ssd_spread(_ssd_cols(dt_ref[0], g))
    ac = _ssd_spread(ac_cols)
    is_last = (lax.broadcasted_iota(jnp.int32, (CHUNK, GROUP_W), 0) == CHUNK - 1).astype(F32)
    return x, bm, cm, dt, ac, ac_cols, is_last


def _ssd_in_specs(nc, rev, gb):
    cc = (lambda c: nc - 1 - c) if rev else (lambda c: c)
    return [pl.BlockSpec((1, CHUNK, gb * XBC_W), lambda b, g, c: (b, cc(c), g)),
            pl.BlockSpec((1, CHUNK, LANE), lambda b, g, c: (b, cc(c), 0)),
            pl.BlockSpec((1, CHUNK, LANE), lambda b, g, c: (b, cc(c), 0)),
            pl.BlockSpec((1, gb * HPG, 1, CHUNK), lambda b, g, c: (b, g, 0, cc(c))),
            pl.BlockSpec((1, gb * GROUP_W), lambda b, g, c: (0, g))]


def _ssd_fwd_call(xbc, dtp, acp, acr, dsk):
    B, S, _ = xbc.shape
    nc = S // CHUNK
    gb = SSD_STEP_GROUPS_FWD

    def body(xbc_ref, dt_ref, ac_ref, ar_ref, ds_ref, y_ref, hp_ref, h_sc):
        @pl.when(pl.program_id(2) == 0)
        def _():
            h_sc[...] = jnp.zeros(h_sc.shape, F32)

        masks = _ssd_masks()
        ys = []
        for gi in range(gb):
            grp = gb * pl.program_id(1) + gi
            x, bm, cm, dt, ac, ac_cols, is_last = _ssd_load(xbc_ref, dt_ref, ac_ref, masks, grp, gi)
            last = jnp.sum(ac * is_last, axis=0, keepdims=True)
            decays = _ssd_decays(ac_cols, ar_ref, gi)
            xd = x * dt
            cb = _dot(cm, bm, NT)
            hprev = h_sc[gi]
            hp_ref[0, gi, 0] = hprev
            y = _dot(cm, hprev, NN) * jnp.exp(ac) + ds_ref[:, gi * GROUP_W:(gi + 1) * GROUP_W] * x
            for j in range(HPG):
                y = y + _dot(cb * decays[j], xd * masks[j], NN)
            ys.append(y)
            h_sc[gi] = hprev * jnp.exp(last) + _dot(bm, xd * jnp.exp(last - ac), TN)
        y_ref[0] = jnp.concatenate(ys, axis=1)

    ng = SSM_GROUPS // gb
    return pl.pallas_call(
        body, grid=(B, ng, nc), in_specs=_ssd_in_specs(nc, False, gb),
        out_specs=[pl.BlockSpec((1, CHUNK, gb * GROUP_W), lambda b, g, c: (b, c, g)),
                   pl.BlockSpec((1, gb, 1, STATE_N, GROUP_W), lambda b, g, c: (b, g, c, 0, 0))],
        out_shape=[jax.ShapeDtypeStruct((B, S, D_INNER), F32),
                   jax.ShapeDtypeStruct((B, SSM_GROUPS, nc, STATE_N, GROUP_W), F32)],
        scratch_shapes=[pltpu.VMEM((gb, STATE_N, GROUP_W), F32)], name="ssd_fwd",
        compiler_params=_cparams(("parallel", "parallel", "arbitrary")))(xbc, dtp, acp, acr, dsk)


def _ssd_bwd_call(xbc, dtp, acp, acr, dsk, hps, dy):
    B, S, _ = xbc.shape
    nc = S // CHUNK
    gb = SSD_STEP_GROUPS_BWD

    def body(xbc_ref, dt_ref, ac_ref, ar_ref, ds_ref, hp_ref, dy_ref,
             dxbc_ref, ddt_ref, dac_ref, dar_ref, dds_ref, dh_sc):
        first = pl.program_id(2) == 0

        @pl.when(first)
        def _():
            dh_sc[...] = jnp.zeros(dh_sc.shape, F32)

        masks = _ssd_masks()
        dxbc_parts, dds_parts = [], []
        for gi in range(gb):
            grp = gb * pl.program_id(0) + gi
            x, bm, cm, dt, ac, ac_cols, is_last = _ssd_load(xbc_ref, dt_ref, ac_ref, masks, grp, gi)
            last = jnp.sum(ac * is_last, axis=0, keepdims=True)
            g = dy_ref[0, :, gi * GROUP_W:(gi + 1) * GROUP_W]
            hprev = hp_ref[0, gi, 0]
            dh = dh_sc[gi]
            decays = _ssd_decays(ac_cols, ar_ref, gi)
            dcols = []
            xd = x * dt
            cb = _dot(cm, bm, NT)
            e_c = jnp.exp(ac)
            e_end = jnp.exp(last - ac)
            e_last = jnp.exp(last)
            z = _dot(cm, hprev, NN)
            dz = g * e_c
            dac = g * z * e_c
            dc = _dot(dz, hprev, NT)
            dhprev = _dot(cm, dz, TN) + dh * e_last
            dcb = jnp.zeros((CHUNK, CHUNK), F32)
            dxd = jnp.zeros(xd.shape, F32)
            for j in range(HPG):
                gj = cb * decays[j]
                dgj = _dot(g * masks[j], xd, NT)
                dxd = dxd + _dot(gj, g, TN) * masks[j]
                dcb = dcb + dgj * decays[j]
                dseg = dgj * gj
                dcols.append(jnp.sum(dseg, axis=1, keepdims=True))
                dar_ref[0, gi * HPG + j] = -jnp.sum(dseg, axis=0, keepdims=True)
            dc = dc + _dot(dcb, bm, NN)
            db = _dot(dcb, cm, TN)
            sx = xd * e_end
            db = db + _dot(sx, dh, NT)
            dsx = _dot(bm, dh, NN)
            dxd = dxd + dsx * e_end
            de = dsx * sx
            dac = dac - de
            dlast = jnp.sum(de, axis=0, keepdims=True) + jnp.sum(dh * hprev, axis=0, keepdims=True) * e_last
            dsk = ds_ref[:, gi * GROUP_W:(gi + 1) * GROUP_W]
            dxbc_parts += [dxd * dt + dsk * g, db, dc]
            ddt_ref[0, gi] = _ssd_gather(dxd * x, None, masks, grp)
            dac_ref[0, gi] = _ssd_gather(dac + is_last * dlast, dcols, masks, grp)
            dds_parts.append(jnp.sum(g * x, axis=0, keepdims=True))
            dh_sc[gi] = dhprev
        dxbc_ref[0] = jnp.concatenate(dxbc_parts, axis=1)
        dds = jnp.concatenate(dds_parts, axis=1)
        first_all = first & (pl.program_id(1) == 0)

        @pl.when(first_all)
        def _():
            dds_ref[...] = dds

        @pl.when(jnp.logical_not(first_all))
        def _():
            dds_ref[...] += dds

    rc = lambda c: nc - 1 - c
    ng = SSM_GROUPS // gb
    in_specs = [pl.BlockSpec(s.block_shape, (lambda g, b, c, f=s.index_map: f(b, g, c))) for s in _ssd_in_specs(nc, True, gb)]
    in_specs.append(pl.BlockSpec((1, gb, 1, STATE_N, GROUP_W), lambda g, b, c: (b, g, rc(c), 0, 0)))
    in_specs.append(pl.BlockSpec((1, CHUNK, gb * GROUP_W), lambda g, b, c: (b, rc(c), g)))
    per_group = pl.BlockSpec((1, gb, CHUNK, LANE), lambda g, b, c: (b, g, rc(c), 0))
    out_specs = [pl.BlockSpec((1, CHUNK, gb * XBC_W), lambda g, b, c: (b, rc(c), g)), per_group, per_group,
                 pl.BlockSpec((1, gb * HPG, 1, CHUNK), lambda g, b, c: (b, g, 0, rc(c))),
                 pl.BlockSpec((1, gb * GROUP_W), lambda g, b, c: (0, g))]
    out_shape = [jax.ShapeDtypeStruct(xbc.shape, F32),
                 jax.ShapeDtypeStruct((B, SSM_GROUPS, S, LANE), F32), jax.ShapeDtypeStruct((B, SSM_GROUPS, S, LANE), F32),
                 jax.ShapeDtypeStruct(acr.shape, F32), jax.ShapeDtypeStruct(dsk.shape, F32)]
    return pl.pallas_call(
        body, grid=(ng, B, nc), in_specs=in_specs, out_specs=out_specs, out_shape=out_shape,
        scratch_shapes=[pltpu.VMEM((gb, STATE_N, GROUP_W), F32)], name="ssd_bwd",
        compiler_params=_cparams(("arbitrary", "arbitrary", "arbitrary")))(xbc, dtp, acp, acr, dsk, hps, dy)


@jax.custom_vjp
def ssd(xbc, dtp, acp, acr, dsk):
    return _ssd_fwd_call(xbc, dtp, acp, acr, dsk)[0]


def _ssd_fwd(xbc, dtp, acp, acr, dsk):
    y, hps = _ssd_fwd_call(xbc, dtp, acp, acr, dsk)
    return y, (xbc, dtp, acp, acr, dsk, hps)


def _ssd_bwd(res, dy):
    dxbc, ddt, dac, dacr, dds = _ssd_bwd_call(*res, dy)
    return dxbc, jnp.sum(ddt, axis=1), jnp.sum(dac, axis=1), dacr, dds


ssd.defvjp(_ssd_fwd, _ssd_bwd)


def _pack_small(arrs):
    flat = jnp.concatenate([a.reshape(-1) for a in arrs])
    rows = -(-flat.shape[0] // (8 * LANE)) * 8
    return jnp.pad(flat, (0, rows * LANE - flat.shape[0])).reshape(rows, LANE)


def _unpack_small(buf, shapes):
    flat = buf.reshape(-1)
    out, off = [], 0
    for shp in shapes:
        n = int(np.prod(shp))
        out.append(flat[off:off + n].reshape(shp))
        off += n
    return out


def _rows_tile(rows, cap):
    for cand in range(min(rows, cap), 7, -8):
        if rows % cand == 0:
            return cand
    return rows


def _pair_sum(mine, theirs, cidx, name):
    n4, kk, nn = mine.shape
    half = kk // 2
    tr = _rows_tile(half, 256)
    nb = half // tr

    def body(c_ref, a_ref, b_ref, o_ref, ob_ref):
        tot = a_ref[...] + b_ref[...]
        o_ref[...] = tot
        ob_ref[...] = tot.astype(BF16)

    spec = pl.BlockSpec((1, tr, nn), lambda j, i, c: (j, i, 0))
    grid_spec = pltpu.PrefetchScalarGridSpec(
        num_scalar_prefetch=1, grid=(n4, nb),
        in_specs=[pl.BlockSpec((1, tr, nn), lambda j, i, c: (j, c[0] * nb + i, 0)), spec], out_specs=[spec, spec])
    return pl.pallas_call(
        body, grid_spec=grid_spec,
        out_shape=[jax.ShapeDtypeStruct((n4, half, nn), F32), jax.ShapeDtypeStruct((n4, half, nn), BF16)],
        name=name, compiler_params=_cparams(("parallel", "parallel")))(cidx, mine, theirs)


def _chip_sum(quad, pair, chip_idx, name):
    _, rows, nn = quad.shape
    tr = _rows_tile(rows, 256)

    def body(s_ref, q_ref, p_ref, o_ref):
        for mine in range(4):
            @pl.when(s_ref[0] == mine)
            def _(mine=mine):
                acc = None
                for d in range(4):
                    term = p_ref[0] if d == mine else q_ref[d].astype(F32)
                    acc = term if acc is None else acc + term
                o_ref[...] = acc

    grid_spec = pltpu.PrefetchScalarGridSpec(
        num_scalar_prefetch=1, grid=(rows // tr,),
        in_specs=[pl.BlockSpec((4, tr, nn), lambda i, s: (0, i, 0)), pl.BlockSpec((1, tr, nn), lambda i, s: (s[0], i, 0))],
        out_specs=pl.BlockSpec((tr, nn), lambda i, s: (i, 0)))
    return pl.pallas_call(body, grid_spec=grid_spec, out_shape=jax.ShapeDtypeStruct((rows, nn), F32), name=name,
                          compiler_params=_cparams(("parallel",)))(chip_idx, quad, pair)


def _adam_halves_call(w, mine, other, cidx, m, v, name):
    _, rows, nn = w.shape
    half = rows // 2
    tr = _rows_tile(half, 128)
    nb = half // tr

    def body(c_ref, w_ref, a_ref, b_ref, m_ref, v_ref, g_ref, d_ref, nm_ref, nv_ref):
        upper = (pl.program_id(0) >= nb).astype(jnp.int32)
        g = jnp.where(upper == c_ref[0], a_ref[...], b_ref[...])
        g_ref[0] = g
        d_ref[0], nm_ref[0], nv_ref[0] = _adam_fn(w_ref[0], g, m_ref[0], v_ref[0])

    spec = pl.BlockSpec((1, tr, nn), lambda i, c: (0, i, 0))
    hspec = pl.BlockSpec((tr, nn), lambda i, c: (i % nb, 0))
    grid_spec = pltpu.PrefetchScalarGridSpec(num_scalar_prefetch=1, grid=(2 * nb,),
                                             in_specs=[spec, hspec, hspec, spec, spec], out_specs=[spec] * 4)
    return pl.pallas_call(body, grid_spec=grid_spec, out_shape=[jax.ShapeDtypeStruct(w.shape, F32)] * 4, name=name,
                          compiler_params=_cparams(("parallel",)))(cidx, w, mine, other, m, v)


def _stack_sum(stack, name):
    n, rows, nn = stack.shape
    tr = _rows_tile(rows, 256)

    def body(s_ref, o_ref):
        acc = s_ref[0]
        for d in range(1, n):
            acc = acc + s_ref[d]
        o_ref[...] = acc

    return pl.pallas_call(
        body, grid=(rows // tr,), in_specs=[pl.BlockSpec((n, tr, nn), lambda i: (0, i, 0))],
        out_specs=pl.BlockSpec((tr, nn), lambda i: (i, 0)), out_shape=jax.ShapeDtypeStruct((rows, nn), F32),
        name=name, compiler_params=_cparams(("parallel",)))(stack)


def _adam_call(w, g, m, v, name):
    rows, nn = w.shape
    tr = _rows_tile(rows, 128)

    def body(w_ref, g_ref, m_ref, v_ref, d_ref, nm_ref, nv_ref):
        d_ref[...], nm_ref[...], nv_ref[...] = _adam_fn(w_ref[...], g_ref[...], m_ref[...], v_ref[...])

    spec = pl.BlockSpec((tr, nn), lambda i: (i, 0))
    sds = jax.ShapeDtypeStruct((rows, nn), F32)
    return pl.pallas_call(body, grid=(rows // tr,), in_specs=[spec] * 4, out_specs=[spec] * 3,
                          out_shape=[sds] * 3, name=name, compiler_params=_cparams(("parallel",)))(w, g, m, v)


def _adam_fn(w, g, m, v):
    m = ADAM_B1 * m + (1.0 - ADAM_B1) * g
    v = ADAM_B2 * v + (1.0 - ADAM_B2) * (g * g)
    m_hat = m / (1.0 - ADAM_B1 ** ADAM_STEP)
    v_hat = v / (1.0 - ADAM_B2 ** ADAM_STEP)
    delta = -ADAM_LR * (m_hat / (jnp.sqrt(v_hat) + ADAM_EPS) + ADAM_WD * w)
    return delta, m, v


def _mesh_pos():
    return lax.axis_index("x"), lax.axis_index("y"), lax.axis_index("c")


def _other_chips(x, y):
    return [(1 - x, y), (x, 1 - y), (1 - x, 1 - y)]


HBM_SPEC = pl.BlockSpec(memory_space=pl.ANY)


def _remote(src, dst, send_sems, recv_sems, k, to):
    return pltpu.make_async_remote_copy(src_ref=src, dst_ref=dst, send_sem=send_sems.at[k], recv_sem=recv_sems.at[k],
                                        device_id=to, device_id_type=MESH)


def _half_rows(c, rows, align):
    half = rows // 2
    return (pl.ds(pl.multiple_of(c * half, align), half), pl.ds(pl.multiple_of((1 - c) * half, align), half))


def _gather_weights(mats, conv):
    n = len(mats)

    def body(*refs):
        ins, conv_in = refs[:n], refs[n]
        outs, conv_out = refs[n + 1:2 * n + 1], refs[2 * n + 1]
        send_sems, recv_sems, local_sem = refs[2 * n + 2:]
        x, y, c = _mesh_pos()
        me, sibling, s = (x, y, c), (x, y, 1 - c), 2 * x + y
        chips = _other_chips(x, y)
        rows = [_half_rows(c, m.shape[0], 16) for m in mats]
        own = pltpu.make_async_copy(conv_in, conv_out.at[s], local_sem)
        own.start()
        sent = []
        for i in range(n):
            mine = rows[i][0]
            for j, (cx, cy) in enumerate(chips):
                sent.append(_remote(ins[i].at[mine], outs[i].at[s, mine], send_sems, recv_sems, 6 * i + j, (cx, cy, c)))
        for j, (cx, cy) in enumerate(chips):
            sent.append(_remote(conv_in, conv_out.at[s], send_sems, recv_sems, 6 * n + j, (cx, cy, c)))
        for cp in sent:
            cp.start()
        for i in range(n):
            mine = rows[i][0]
            for j, (cx, cy) in enumerate(chips):
                landed = outs[i].at[2 * cx + cy, mine]
                _remote(landed, landed, send_sems, recv_sems, 6 * i + j, me).wait_recv()
                fwd = _remote(landed, landed, send_sems, recv_sems, 6 * i + 3 + j, sibling)
                fwd.start()
                sent.append(fwd)
        for j, (cx, cy) in enumerate(chips):
            slot = conv_out.at[2 * cx + cy]
            _remote(slot, slot, send_sems, recv_sems, 6 * n + j, me).wait_recv()
        for i in range(n):
            theirs_rows = rows[i][1]
            for j, (cx, cy) in enumerate(chips):
                theirs = outs[i].at[2 * cx + cy, theirs_rows]
                _remote(theirs, theirs, send_sems, recv_sems, 6 * i + 3 + j, me).wait_recv()
        for cp in sent:
            cp.wait_send()
        own.wait()

    out_shape = [jax.ShapeDtypeStruct((4,) + m.shape, m.dtype) for m in mats]
    out_shape.append(jax.ShapeDtypeStruct((4,) + conv.shape, conv.dtype))
    res = pl.pallas_call(
        body, in_specs=[HBM_SPEC] * (n + 1), out_specs=[HBM_SPEC] * (n + 1), out_shape=out_shape,
        scratch_shapes=[pltpu.SemaphoreType.DMA((6 * n + 3,)), pltpu.SemaphoreType.DMA((6 * n + 3,)),
                        pltpu.SemaphoreType.DMA],
        name="all_gather_weights")(*mats, conv)
    return res[:n], res[n]


def _sibling_exchange(stacks):
    n = len(stacks)

    def body(*refs):
        ins, outs = refs[:n], refs[n:2 * n]
        send_sems, recv_sems = refs[2 * n:]
        x, y, c = _mesh_pos()
        cps = []
        for i in range(n):
            theirs = _half_rows(c, stacks[i].shape[1], 8)[1]
            cps.append(_remote(ins[i].at[:, theirs, :], outs[i], send_sems, recv_sems, i, (x, y, 1 - c)))
        for cp in cps:
            cp.start()
        for cp in cps:
            cp.wait()

    out_shape = [jax.ShapeDtypeStruct((4, s.shape[1] // 2, s.shape[2]), s.dtype) for s in stacks]
    return pl.pallas_call(
        body, in_specs=[HBM_SPEC] * n, out_specs=[HBM_SPEC] * n, out_shape=out_shape,
        scratch_shapes=[pltpu.SemaphoreType.DMA((n,)), pltpu.SemaphoreType.DMA((n,))],
        name="grad_sibling_exchange")(*stacks)


def _chip_exchange(parts):
    n = len(parts)

    def body(*refs):
        ins, outs = refs[:n], refs[n:2 * n]
        send_sems, recv_sems = refs[2 * n:]
        x, y, c = _mesh_pos()
        me, s = (x, y, c), 2 * x + y
        chips = _other_chips(x, y)
        sent = [_remote(ins[i].at[2 * cx + cy], outs[i].at[s], send_sems, recv_sems, 3 * i + j, (cx, cy, c))
                for i in range(n) for j, (cx, cy) in enumerate(chips)]
        for cp in sent:
            cp.start()
        for i in range(n):
            for j, (cx, cy) in enumerate(chips):
                slot = outs[i].at[2 * cx + cy]
                _remote(slot, slot, send_sems, recv_sems, 3 * i + j, me).wait_recv()
        for cp in sent:
            cp.wait_send()

    return pl.pallas_call(
        body, in_specs=[HBM_SPEC] * n, out_specs=[HBM_SPEC] * n,
        out_shape=[jax.ShapeDtypeStruct(p.shape, p.dtype) for p in parts],
        scratch_shapes=[pltpu.SemaphoreType.DMA((3 * n,)), pltpu.SemaphoreType.DMA((3 * n,))],
        name="grad_chip_exchange")(*parts)


def _sibling_swap(halves):
    n = len(halves)

    def body(*refs):
        ins, outs = refs[:n], refs[n:2 * n]
        send_sems, recv_sems = refs[2 * n:]
        x, y, c = _mesh_pos()
        cps = [_remote(ins[i], outs[i], send_sems, recv_sems, i, (x, y, 1 - c)) for i in range(n)]
        for cp in cps:
            cp.start()
        for cp in cps:
            cp.wait()

    return pl.pallas_call(
        body, in_specs=[HBM_SPEC] * n, out_specs=[HBM_SPEC] * n,
        out_shape=[jax.ShapeDtypeStruct(h.shape, h.dtype) for h in halves],
        scratch_shapes=[pltpu.SemaphoreType.DMA((n,)), pltpu.SemaphoreType.DMA((n,))],
        name="grad_sibling_swap")(*halves)


def _gather_small(vec):
    def body(in_ref, out_ref, send_sems, recv_sems, local_sem):
        x, y, c = _mesh_pos()
        me = (x, y, c)
        own = pltpu.make_async_copy(in_ref, out_ref.at[4 * x + 2 * y + c], local_sem)
        own.start()
        peers = [(1 - x if k & 4 else x, 1 - y if k & 2 else y, 1 - c if k & 1 else c) for k in range(1, 8)]
        sent = [_remote(in_ref, out_ref.at[4 * x + 2 * y + c], send_sems, recv_sems, k, p) for k, p in enumerate(peers)]
        for cp in sent:
            cp.start()
        for k, (px, py, pc) in enumerate(peers):
            slot = out_ref.at[4 * px + 2 * py + pc]
            _remote(slot, slot, send_sems, recv_sems, k, me).wait_recv()
        for cp in sent:
            cp.wait_send()
        own.wait()

    return pl.pallas_call(
        body, in_specs=[HBM_SPEC], out_specs=HBM_SPEC, out_shape=jax.ShapeDtypeStruct((8,) + vec.shape, vec.dtype),
        scratch_shapes=[pltpu.SemaphoreType.DMA((7,)), pltpu.SemaphoreType.DMA((7,)), pltpu.SemaphoreType.DMA],
        name="grad_gather_small")(vec)


def _reduce_matrices(stacks, names):
    cidx = lax.axis_index("c").astype(jnp.int32).reshape(1)
    chip = (2 * lax.axis_index("x") + lax.axis_index("y")).astype(jnp.int32).reshape(1)
    got = _sibling_exchange(stacks)
    pairs = [_pair_sum(a, b, cidx, "grad_pair_sum_" + nm) for a, b, nm in zip(stacks, got, names)]
    quads = _chip_exchange([p[1] for p in pairs])
    mine = [_chip_sum(q, p[0], chip, "grad_chip_sum_" + nm) for q, p, nm in zip(quads, pairs, names)]
    return mine, _sibling_swap(mine)


def _pad_cols(a, n):
    return jnp.concatenate([a, jnp.zeros((a.shape[0], n - a.shape[1]), a.dtype)], axis=1)


def _group_channels(a):
    lead = a.shape[:-1]
    xs = a[..., :D_INNER].reshape(lead + (SSM_GROUPS, GROUP_W))
    bs = a[..., D_INNER:D_INNER + SSM_GROUPS * STATE_N].reshape(lead + (SSM_GROUPS, STATE_N))
    cs = a[..., D_INNER + SSM_GROUPS * STATE_N:].reshape(lead + (SSM_GROUPS, STATE_N))
    return jnp.concatenate([xs, bs, cs], axis=-1).reshape(lead + (CONV_CH,))


PROJ_SEGS = (('gate_a', D_MODEL), ('gate_b', D_MODEL), ('z', D_INNER), ('xbc', CONV_CH), ('q_lat', Q_RANK),
             ('kv_lat', KV_RANK), ('k_rope', LANE), ('dt', LANE))
PROJ_WIDE = sum(w for _, w in PROJ_SEGS[:4])
PROJ_LANE0 = {n: (v if v < PROJ_WIDE else v - PROJ_WIDE) for n, v in
              zip([n for n, _ in PROJ_SEGS], [int(v) for v in np.cumsum([0] + [w for _, w in PROJ_SEGS])[:-1]])}
CONV_LANE0 = PROJ_LANE0['xbc']
KR_LANE0 = PROJ_LANE0['k_rope']


def _lay_w_in(w):
    idx = np.cumsum(IN_SIZES)[:-1]
    q_lat, kv_lat, k_rope, z, xbc, dt, gate_a, gate_b = jnp.split(w, [int(v) for v in idx], axis=1)
    return jnp.concatenate([gate_a, gate_b, z, _group_channels(xbc), q_lat, kv_lat, _pad_cols(k_rope, LANE),
                            _pad_cols(dt, LANE)], axis=1)


@jax.custom_vjp
def project(h, w, tok):
    return _project_impl(h, w)


def _project_impl(h, w):
    return (_mm(h, w[:, :PROJ_WIDE], "w_in_fwd", BF16), _mm(h, w[:, PROJ_WIDE:], "w_in_narrow_fwd")) + tuple(
        jnp.zeros((h.shape[0], wd), BF16) for _, wd in PROJ_SEGS)


def _project_fwd(h, w, tok):
    return _project_impl(h, w), (h, w)


def _project_bwd(res, cots):
    h, w = res
    g = jnp.concatenate(cots[2:], axis=1)
    return _mm(g, w.T, "w_in_dx", h.dtype), jnp.zeros_like(w), _mm_tn(h, g, "w_in_dw")


project.defvjp(_project_fwd, _project_bwd)


def _lay_w_uq(w):
    w3 = w.reshape(Q_RANK, N_HEADS, NOPE + ROPE)
    w3 = jnp.concatenate([w3, jnp.zeros((Q_RANK, N_HEADS, QK_PAD - NOPE - ROPE), w.dtype)], axis=2)
    return w3.reshape(Q_RANK, N_HEADS * QK_PAD)


def _lay_w_ukv(w):
    w3 = w.reshape(KV_RANK, N_HEADS, NOPE + V_DIM)
    return jnp.concatenate([w3[:, :, :NOPE].reshape(KV_RANK, -1), w3[:, :, NOPE:].reshape(KV_RANK, -1)], axis=1)


def _pad_lanes(v, n=LANE):
    return jnp.concatenate([v, jnp.zeros((v.shape[0], n - v.shape[1]), v.dtype)], axis=1)


def _local_loss(toks, small, x, wb, c8, posf, target):
    B, S, D = x.shape
    T = B * S

    def lin(name, a, key, lay=lambda w: w, out_dtype=F32):
        return make_linear(name, out_dtype)(a, lay(wb[key]), lay(toks[key]))

    rows2 = lambda a: a.reshape(T, a.shape[-1])
    rows3 = lambda a: a.reshape(B, S, a.shape[-1])

    sc = make_rowwise("silu_c", _f_silu, 1, 0, 0, ('row',))((c8[None],), (), ())[0][0]
    mod = make_linear("ada", F32, 4)(sc, wb['w_ada'], toks['w_ada'])[:B] + small['b_ada']
    shift1, scale1, gate1, shift2, scale2, gate2 = [m[:, None, :] for m in jnp.split(mod, 6, axis=-1)]

    modulate = make_rowwise("modulate1", _f_modulate, 1, 2, 1, ('row',))
    h = modulate((x,), (scale1, shift1), (small['g_pre_mix'],))[0]
    outs = project(rows2(h), _lay_w_in(wb['w_in']), _lay_w_in(toks['w_in']))
    wide = lax.stop_gradient(rows3(outs[0]))
    proj = lax.stop_gradient(rows3(outs[1]))
    stand = {n: rows3(o) for (n, _), o in zip(PROJ_SEGS, outs[2:])}

    def win(seg, block):
        return (PROJ_LANE0[seg] // block, dict(PROJ_SEGS)[seg])

    inv = ROPE_THETA ** (-jnp.arange(ROPE // 2, dtype=F32) / (ROPE // 2))
    inv_lane = jnp.concatenate([inv, inv, jnp.zeros((LANE - ROPE,), F32)])[None]
    tabs = tuple(_rope_tables(posf, inv_lane))
    qn = make_rowwise("rms_q", _f_rms, 1, 0, 1, ('row',), windows={0: win('q_lat', Q_RANK)})(
        (proj,), (), (small['g_q_lat'],), (stand['q_lat'],))[0]
    kvn = make_rowwise("rms_kv", _f_rms, 1, 0, 1, ('row',), windows={0: win('kv_lat', KV_RANK)})(
        (proj,), (), (small['g_kv_lat'],), (stand['kv_lat'],))[0]
    qp = rows3(lin("w_uq", rows2(qn), 'w_uq', _lay_w_uq))
    kvp = rows3(lin("w_ukv", rows2(kvn), 'w_ukv', _lay_w_ukv, BF16))
    qr = rope_q(qp, tabs)
    kr = build_k(kvp, proj, stand['k_rope'], tabs)
    att = attention(qr, kr, kvp)
    attn = rows3(lin("w_o_attn", rows2(att), 'w_o_attn'))

    xa = conv_silu(wide, stand['xbc'], _group_channels(wb['conv_w_f32']), _group_channels(small['conv_b']))
    dt_pad, a_pad = make_rowwise("dt_softplus", _f_dt, 1, 0, 2, ('row', 'row'), windows={0: win('dt', LANE)})(
        (proj,), (), (_pad_lanes(small['dt_bias']), _pad_lanes(small['a_log'])), (stand['dt'],))
    ac_pad = chunk_cumsum(a_pad)
    acr = jnp.transpose(ac_pad[..., :SSM_HEADS], (0, 2, 1))[:, :, None, :]
    dsk = jnp.repeat(small['d_skip'], HEAD_P, axis=-1)
    y = ssd(xa, dt_pad, ac_pad, acr, dsk)
    yg = make_rowwise("gated_norm", _f_gated_norm, 2, 0, 1, ('row',), ncol=SSM_GROUPS, ts_cap=2048,
                      windows={1: win('z', GROUP_W)})((y, wide), (), (small['g_ssm_out'],), (stand['z'],))[0]
    ssm = rows3(lin("w_o_ssm", rows2(yg), 'w_o_ssm'))

    merged = make_rowwise("merge", _f_merge, 4, 0, 0, ('row',),
                          windows={2: win('gate_a', D_MODEL), 3: win('gate_b', D_MODEL)})(
        (attn, ssm, wide, wide), (), (), (stand['gate_a'], stand['gate_b']))[0]
    mix = rows3(lin("w_out", rows2(merged), 'w_out'))
    x1 = make_rowwise("post_mix", _f_post, 2, 1, 1, ('row',))((x, mix), (gate1,), (small['g_post_mix'],))[0]

    h2 = make_rowwise("modulate2", _f_modulate, 1, 2, 1, ('row',))((x1,), (scale2, shift2), (small['g_pre_mlp'],))[0]
    ff = rows3(ffn(rows2(h2), wb['w_ff1'], toks['w_ff1'], wb['w_ff2'], toks['w_ff2']))
    lvec = make_rowwise("final_loss", _f_final_loss, 3, 1, 1, ('sum',), nodiff=(2,))(
        (x1, ff, target), (gate2,), (small['g_post_mlp'],))[0]
    return jnp.sum(lvec)


MATRICES = COL_SHARDED + ROW_SHARDED
STACKED_DW = ('w_ada', 'w_ff1')


def _local_step(x, c, positions, target, wb, small):
    B = x.shape[0]
    c8 = jnp.concatenate([c, jnp.zeros((16 - B, c.shape[1]), F32)], axis=0)
    posf = positions.astype(F32)[..., None]
    toks = {k: jnp.zeros(wb[k].shape, F32) for k in MATRICES if k != 'conv_w'}
    for k in STACKED_DW:
        rows, cols = wb[k].shape
        toks[k] = jnp.zeros((4, rows, cols // 4), F32)
    conv_w = wb['conv_w_f32']

    def loss_fn(toks, small, conv_w, x):
        wbl = dict(wb)
        wbl['conv_w_f32'] = conv_w
        return _local_loss(toks, small, x, wbl, c8, posf, target)

    loss, (g_tok, g_small, g_conv, g_x) = jax.value_and_grad(loss_fn, argnums=(0, 1, 2, 3))(toks, small, conv_w, x)
    grads = dict(g_tok)
    grads.update(g_small)
    grads['conv_w'] = g_conv
    return loss, g_x, grads


def kernel(x, c, positions, w_ada, b_ada, g_pre_mix, g_post_mix, w_in, g_q_lat, g_kv_lat, w_uq, w_ukv, w_o_attn, conv_w, conv_b, dt_bias, a_log, d_skip, g_ssm_out, w_o_ssm, w_out, g_pre_mlp, g_post_mlp, w_ff1, w_ff2, loss_target, m_w_ada, m_b_ada, m_g_pre_mix, m_g_post_mix, m_w_in, m_g_q_lat, m_g_kv_lat, m_w_uq, m_w_ukv, m_w_o_attn, m_conv_w, m_conv_b, m_dt_bias, m_a_log, m_d_skip, m_g_ssm_out, m_w_o_ssm, m_w_out, m_g_pre_mlp, m_g_post_mlp, m_w_ff1, m_w_ff2, v_w_ada, v_b_ada, v_g_pre_mix, v_g_post_mix, v_w_in, v_g_q_lat, v_g_kv_lat, v_w_uq, v_w_ukv, v_w_o_attn, v_conv_w, v_conv_b, v_dt_bias, v_a_log, v_d_skip, v_g_ssm_out, v_w_o_ssm, v_w_out, v_g_pre_mlp, v_g_post_mlp, v_w_ff1, v_w_ff2):
    given = dict(locals())
    w_loc = {n: given[n] for n in WEIGHTS}
    m_loc = {n: given["m_" + n] for n in WEIGHTS}
    v_loc = {n: given["v_" + n] for n in WEIGHTS}
    mats = [n for n in WEIGHTS if n in MATRICES and n != 'conv_w']
    vecs = [n for n in WEIGHTS if n not in MATRICES]

    own = [w_loc[n][0].astype(BF16) for n in mats]
    g_mats, g_conv = _gather_weights(own, conv_w[0])
    chip = 2 * lax.axis_index("x") + lax.axis_index("y")
    wb = {}
    for n, g, mine in zip(mats, g_mats, own):
        g = lax.dynamic_update_slice_in_dim(g, mine[None], chip, axis=0)
        if n in COL_SHARDED:
            wb[n] = jnp.transpose(g, (1, 0, 2)).reshape(g.shape[1], -1)
        else:
            wb[n] = g.reshape(-1, g.shape[2])
    wb['conv_w_f32'] = jnp.transpose(g_conv, (1, 0, 2)).reshape(CONV_K, -1)
    small = {n: w_loc[n] for n in vecs}

    loss_part, grad_x, grads = _local_step(x, c, positions, loss_target, wb, small)
    loss = lax.psum(loss_part, ("x", "y", "c"))

    stacks = []
    for n in mats:
        kk, nn = w_loc[n].shape[1:]
        if n in STACKED_DW:
            stacks.append(grads[n])
        elif n in COL_SHARDED:
            stacks.append(jnp.transpose(grads[n].reshape(kk, 4, nn), (1, 0, 2)))
        else:
            stacks.append(grads[n].reshape(4, kk, nn))
    g_mine, g_other = _reduce_matrices(stacks, mats)
    g_shard = {}

    vec_shapes = [tuple(grads[n].shape) for n in vecs] + [tuple(grads['conv_w'].shape)]
    total = _stack_sum(_gather_small(_pack_small([grads[n] for n in vecs] + [grads['conv_w']])), "grad_sum_small")
    g_vec = _unpack_small(total, vec_shapes)
    n_conv = conv_w.shape[2]
    chip = 2 * lax.axis_index("x") + lax.axis_index("y")
    g_shard['conv_w'] = lax.dynamic_slice_in_dim(g_vec[-1], chip * n_conv, n_conv, axis=1)
    for n, g in zip(vecs, g_vec):
        g_shard[n] = g

    delta, new_m, new_v = {}, {}, {}
    cidx = lax.axis_index("c").astype(jnp.int32).reshape(1)
    for n, mine, other in zip(mats, g_mine, g_other):
        g_shard[n], delta[n], new_m[n], new_v[n] = _adam_halves_call(
            w_loc[n], mine, other, cidx, m_loc[n], v_loc[n], "adamw_" + n)
    rest = vecs + ['conv_w']
    rest_shapes = [tuple(w_loc[n].shape) for n in rest]
    packed = [_pack_small([src[n] for n in rest]) for src in (w_loc, g_shard, m_loc, v_loc)]
    for dst, buf in zip((delta, new_m, new_v), _adam_call(*packed, "adamw_small")):
        dst.update(zip(rest, _unpack_small(buf, rest_shapes)))

    def out(d):
        return [d[n].reshape(w_loc[n].shape) for n in WEIGHTS]

    return (loss, grad_x, *out(g_shard), *out(delta), *out(new_m), *out(new_v))
```

```python
import functools
import math

import numpy as np
import jax
import jax.numpy as jnp
from jax import lax
from jax.experimental import pallas as pl
from jax.experimental.pallas import tpu as pltpu

F32 = jnp.float32
BF16 = jnp.bfloat16
MESH = pl.DeviceIdType.MESH

D_MODEL = 1024
N_HEADS = 8
NOPE = 128
ROPE = 64
V_DIM = 128
Q_RANK = 256
KV_RANK = 256
ROPE_THETA = 10000.0
D_INNER = 2048
SSM_HEADS = 32
SSM_GROUPS = 8
HEAD_P = 64
STATE_N = 128
CONV_K = 4
CHUNK = 128
CONV_CH = D_INNER + 2 * SSM_GROUPS * STATE_N
D_FF = 4096
EPS = 1e-6
IN_SIZES = (Q_RANK, KV_RANK, ROPE, D_INNER, CONV_CH, SSM_HEADS, D_MODEL, D_MODEL)
ADAM_LR, ADAM_B1, ADAM_B2, ADAM_EPS, ADAM_WD, ADAM_STEP = 0.001, 0.9, 0.999, 1e-08, 0.01, 10

VMEM_LIMIT_BYTES = 52 * 1024 * 1024
LANE = 128
QK_PAD = 256

WEIGHTS = ['w_ada', 'b_ada', 'g_pre_mix', 'g_post_mix', 'w_in', 'g_q_lat', 'g_kv_lat', 'w_uq', 'w_ukv',
           'w_o_attn', 'conv_w', 'conv_b', 'dt_bias', 'a_log', 'd_skip', 'g_ssm_out', 'w_o_ssm', 'w_out',
           'g_pre_mlp', 'g_post_mlp', 'w_ff1', 'w_ff2']
COL_SHARDED = ('w_ada', 'w_in', 'w_uq', 'w_ukv', 'conv_w', 'w_ff1')
ROW_SHARDED = ('w_o_attn', 'w_o_ssm', 'w_out', 'w_ff2')


def _cparams(sem):
    return pltpu.CompilerParams(dimension_semantics=sem, vmem_limit_bytes=VMEM_LIMIT_BYTES)


def _tile(n, cap):
    if n <= cap:
        return n
    k = n // LANE
    best = LANE
    for d in range(1, k + 1):
        if k % d == 0 and d * LANE <= cap:
            best = d * LANE
    return best


def _mm(a, w, name, out_dtype=F32, epilogue=None, extras=(), out_dtypes=None):
    M, K = a.shape
    N = w.shape[1]
    tm = min(M, 1024)
    tn = _tile(N, 1024)
    tk = _tile(K, 2048)
    nk = K // tk
    dts = tuple(out_dtypes) if epilogue is not None else (out_dtype,)
    n_x, n_o = len(extras), len(dts)

    def finish(acc, refs):
        res = epilogue(acc, *[r[...] for r in refs[:n_x]]) if epilogue is not None else (acc,)
        for o_ref, val, dt in zip(refs[n_x:n_x + n_o], res, dts):
            o_ref[...] = val.astype(dt)

    def body(a_ref, w_ref, *refs):
        part = jnp.dot(a_ref[...].astype(BF16), w_ref[...], preferred_element_type=F32)
        if nk == 1:
            finish(part, refs)
        else:
            acc_ref = refs[-1]
            k = pl.program_id(2)

            @pl.when(k == 0)
            def _():
                acc_ref[...] = part

            @pl.when(k > 0)
            def _():
                acc_ref[...] += part

            @pl.when(k == nk - 1)
            def _():
                finish(acc_ref[...], refs)

    ospec = pl.BlockSpec((tm, tn), lambda i, j, k: (i, j))
    res = pl.pallas_call(
        body, grid=(M // tm, N // tn, nk),
        in_specs=[pl.BlockSpec((tm, tk), lambda i, j, k: (i, k)), pl.BlockSpec((tk, tn), lambda i, j, k: (k, j))]
        + [ospec] * n_x,
        out_specs=[ospec] * n_o, out_shape=[jax.ShapeDtypeStruct((M, N), dt) for dt in dts],
        scratch_shapes=[pltpu.VMEM((tm, tn), F32)] if nk > 1 else [], name=name,
        compiler_params=_cparams(("parallel", "parallel", "arbitrary")))(a, w, *extras)
    return res if epilogue is not None else res[0]


def _mm_tn(a, g, name, col_shards=1):
    M, K = a.shape
    N = g.shape[1]
    tm = min(M, 1024)
    tk = _tile(K, 1024)
    tn = _tile(N // col_shards, 1024)
    nm = M // tm
    per = N // col_shards // tn

    def body(a_ref, g_ref, o_ref):
        part = lax.dot_general(a_ref[...].astype(BF16), g_ref[...].astype(BF16), (((0,), (0,)), ((), ())),
                               preferred_element_type=F32)
        m = pl.program_id(2)

        @pl.when(m == 0)
        def _():
            o_ref[...] = part.reshape(o_ref.shape)

        @pl.when(m > 0)
        def _():
            o_ref[...] += part.reshape(o_ref.shape)

    if col_shards == 1:
        out_spec = pl.BlockSpec((tk, tn), lambda i, j, m: (i, j))
        out_shape = jax.ShapeDtypeStruct((K, N), F32)
    else:
        out_spec = pl.BlockSpec((1, tk, tn), lambda i, j, m: (j // per, i, j % per))
        out_shape = jax.ShapeDtypeStruct((col_shards, K, N // col_shards), F32)
    return pl.pallas_call(
        body, grid=(K // tk, N // tn, nm),
        in_specs=[pl.BlockSpec((tm, tk), lambda i, j, m: (m, i)), pl.BlockSpec((tm, tn), lambda i, j, m: (m, j))],
        out_specs=out_spec, out_shape=out_shape, name=name,
        compiler_params=_cparams(("parallel", "parallel", "arbitrary")))(a, g)


def make_linear(name, out_dtype=F32, dw_col_shards=1):
    @jax.custom_vjp
    def linear(a, w, tok):
        return _mm(a, w, name + "_fwd", out_dtype)

    def fwd(a, w, tok):
        return _mm(a, w, name + "_fwd", out_dtype), (a, w)

    def bwd(res, g):
        a, w = res
        da = _mm(g, w.T, name + "_dx", a.dtype)
        dw = _mm_tn(a, g, name + "_dw", dw_col_shards)
        return da, jnp.zeros_like(w), dw

    linear.defvjp(fwd, bwd)
    return linear


def _relu2_epilogue(acc):
    r = jnp.maximum(acc, 0.0)
    return r * r, r


def _relu2_bwd_epilogue(acc, r):
    return (acc * (2.0 * r.astype(F32)),)


@jax.custom_vjp
def ffn(h, w1, tok1, w2, tok2):
    act, _ = _mm(h, w1, "w_ff1_fwd", epilogue=_relu2_epilogue, out_dtypes=(BF16, BF16))
    return _mm(act, w2, "w_ff2_fwd", BF16)


def _ffn_fwd(h, w1, tok1, w2, tok2):
    act, r = _mm(h, w1, "w_ff1_fwd", epilogue=_relu2_epilogue, out_dtypes=(BF16, BF16))
    return _mm(act, w2, "w_ff2_fwd", BF16), (h, w1, w2, act, r)


def _ffn_bwd(res, g):
    h, w1, w2, act, r = res
    du = _mm(g, w2.T, "w_ff2_dx", epilogue=_relu2_bwd_epilogue, extras=(r,), out_dtypes=(BF16,))[0]
    dw2 = _mm_tn(act, g, "w_ff2_dw")
    dw1 = _mm_tn(h, du, "w_ff1_dw", 4)
    dh = _mm(du, w1.T, "w_ff1_dx", h.dtype)
    return dh, jnp.zeros_like(w1), dw1, jnp.zeros_like(w2), dw2


ffn.defvjp(_ffn_fwd, _ffn_bwd)


def make_rowwise(name, f, n_rows, n_seqs, n_pars, out_kinds, ncol=1, nodiff=(), ts_cap=512, windows=None,
                 forward_row=None):
    windows = dict(windows or {})
    n_in = n_rows + n_seqs + n_pars
    diff_idx = [i for i in range(n_in) if i not in nodiff]

    def _dims(rows):
        B, S = rows[0].shape[0], rows[0].shape[1]
        ts = min(S, ts_cap)
        return B, S, ts

    def _width(i, r):
        return windows[i][1] if i in windows else r.shape[2]

    def _in_specs(rows, seqs, pars, ts):
        specs = []
        for i, r in enumerate(rows):
            col0 = windows[i][0] if i in windows else 0
            specs.append(pl.BlockSpec((1, ts, _width(i, r) // ncol), lambda k, b, s, col0=col0: (b, s, k + col0)))
        for q in seqs:
            specs.append(pl.BlockSpec((1, 1, q.shape[2] // ncol), lambda k, b, s: (b, 0, k)))
        for p in pars:
            specs.append(pl.BlockSpec((1, p.shape[1] // ncol), lambda k, b, s: (0, k)))
        return specs

    def _load(refs):
        vals = [r[0] for r in refs[:n_rows + n_seqs]]
        vals += [r[...] for r in refs[n_rows + n_seqs:n_in]]
        return vals

    def _out_struct(rows, seqs, pars, ts):
        blocks = [jax.ShapeDtypeStruct((ts, _width(i, r) // ncol), r.dtype) for i, r in enumerate(rows)]
        blocks += [jax.ShapeDtypeStruct((1, q.shape[2] // ncol), q.dtype) for q in seqs]
        blocks += [jax.ShapeDtypeStruct((1, p.shape[1] // ncol), p.dtype) for p in pars]
        return jax.eval_shape(f, *blocks)

    def _fwd_call(rows, seqs, pars):
        B, S, ts = _dims(rows)
        outs = _out_struct(rows, seqs, pars, ts)
        n_out = len(outs)

        def body(*refs):
            res = f(*_load(refs))
            first = (pl.program_id(1) == 0) & (pl.program_id(2) == 0)
            for o_ref, val, kind in zip(refs[n_in:], res, out_kinds):
                if kind == 'row':
                    o_ref[0] = val
                else:
                    tot = jnp.sum(val, axis=0, keepdims=True)

                    @pl.when(first)
                    def _(o_ref=o_ref, tot=tot):
                        o_ref[...] = tot

                    @pl.when(jnp.logical_not(first))
                    def _(o_ref=o_ref, tot=tot):
                        o_ref[...] += tot

        out_shape, out_specs = [], []
        for o, kind in zip(outs, out_kinds):
            d = o.shape[1]
            if kind == 'row':
                out_shape.append(jax.ShapeDtypeStruct((B, S, ncol * d), o.dtype))
                out_specs.append(pl.BlockSpec((1, ts, d), lambda k, b, s: (b, s, k)))
            else:
                out_shape.append(jax.ShapeDtypeStruct((1, ncol * d), o.dtype))
                out_specs.append(pl.BlockSpec((1, d), lambda k, b, s: (0, k)))
        res = pl.pallas_call(
            body, grid=(ncol, B, S // ts), in_specs=_in_specs(rows, seqs, pars, ts), out_specs=out_specs,
            out_shape=out_shape, name=name + "_fwd",
            compiler_params=_cparams(("arbitrary", "arbitrary", "arbitrary")))(*rows, *seqs, *pars)
        return tuple(res)

    def _bwd_call(rows, seqs, pars, cots, carried=None):
        B, S, ts = _dims(rows)
        outs = _out_struct(rows, seqs, pars, ts)
        n_out = len(outs)
        all_in = list(rows) + list(seqs) + list(pars)
        extra = [] if carried is None else [carried]

        def body(*refs):
            vals = _load(refs)
            if carried is not None:
                carried_ref, refs = refs[n_in + n_out], refs[:n_in + n_out] + refs[n_in + n_out + 1:]
            cts = []
            for c_ref, o, kind in zip(refs[n_in:n_in + n_out], outs, out_kinds):
                if kind == 'row':
                    cts.append(c_ref[0])
                else:
                    cts.append(jnp.broadcast_to(c_ref[...], o.shape))

            def g(*dv):
                full = list(vals)
                for i, v in zip(diff_idx, dv):
                    full[i] = v
                return tuple(f(*full))

            _, vjp = jax.vjp(g, *[vals[i] for i in diff_idx])
            grads = vjp(tuple(cts))
            b, s = pl.program_id(1), pl.program_id(2)
            for o_ref, i, gr in zip(refs[n_in + n_out:], diff_idx, grads):
                if i < n_rows:
                    if carried is not None and i == forward_row:
                        gr = gr + carried_ref[0]
                    o_ref[0] = gr.astype(o_ref.dtype)
                else:
                    first = (s == 0) if i < n_rows + n_seqs else ((b == 0) & (s == 0))
                    target = (lambda r: r.at[0]) if i < n_rows + n_seqs else (lambda r: r)

                    @pl.when(first)
                    def _(o_ref=o_ref, gr=gr, target=target):
                        target(o_ref)[...] = gr

                    @pl.when(jnp.logical_not(first))
                    def _(o_ref=o_ref, gr=gr, target=target):
                        target(o_ref)[...] += gr

        cot_specs = []
        for o, kind in zip(outs, out_kinds):
            d = o.shape[1]
            if kind == 'row':
                cot_specs.append(pl.BlockSpec((1, ts, d), lambda k, b, s: (b, s, k)))
            else:
                cot_specs.append(pl.BlockSpec((1, d), lambda k, b, s: (0, k)))
        out_shape, out_specs = [], []
        for i in diff_idx:
            a = all_in[i]
            if i < n_rows:
                out_shape.append(jax.ShapeDtypeStruct((B, S, _width(i, a)), BF16 if i in windows else a.dtype))
                out_specs.append(pl.BlockSpec((1, ts, _width(i, a) // ncol), lambda k, b, s: (b, s, k)))
                continue
            out_shape.append(jax.ShapeDtypeStruct(a.shape, a.dtype))
            if i < n_rows + n_seqs:
                out_specs.append(pl.BlockSpec((1, 1, a.shape[2] // ncol), lambda k, b, s: (b, 0, k)))
            else:
                out_specs.append(pl.BlockSpec((1, a.shape[1] // ncol), lambda k, b, s: (0, k)))
        if carried is not None:
            cot_specs.append(pl.BlockSpec((1, ts, carried.shape[2] // ncol), lambda k, b, s: (b, s, k)))
        res = pl.pallas_call(
            body, grid=(ncol, B, S // ts), in_specs=_in_specs(rows, seqs, pars, ts) + cot_specs,
            out_specs=out_specs, out_shape=out_shape, name=name + "_bwd",
            compiler_params=_cparams(("arbitrary", "arbitrary", "arbitrary")))(*all_in, *cots, *extra)
        grads = [None] * n_in
        for i, r in zip(diff_idx, res):
            grads[i] = r
        for i in nodiff:
            grads[i] = jnp.zeros_like(all_in[i])
        stand_in_grads = tuple(grads[i] for i in sorted(windows))
        for i in windows:
            grads[i] = jnp.zeros_like(all_in[i])
        return (tuple(grads[:n_rows]), tuple(grads[n_rows:n_rows + n_seqs]), tuple(grads[n_rows + n_seqs:]),
                stand_in_grads)

    def _outputs(rows, seqs, pars):
        res = _fwd_call(rows, seqs, pars)
        return res if forward_row is None else res + (rows[forward_row],)

    @jax.custom_vjp
    def op(rows, seqs, pars, stand_ins):
        return _outputs(rows, seqs, pars)

    def fwd(rows, seqs, pars, stand_ins):
        return _outputs(rows, seqs, pars), (rows, seqs, pars)

    def bwd(res, cots):
        rows, seqs, pars = res
        if forward_row is None:
            return _bwd_call(rows, seqs, pars, cots)
        return _bwd_call(rows, seqs, pars, cots[:-1], cots[-1])

    op.defvjp(fwd, bwd)
    return lambda rows, seqs, pars, stand_ins=(): op(tuple(rows), tuple(seqs), tuple(pars), tuple(stand_ins))


def _rms(x, g):
    x = x.astype(F32)
    return x * lax.rsqrt(jnp.mean(x * x, axis=-1, keepdims=True) + EPS) * g


def _silu(x):
    return x * lax.logistic(x)


def _f_silu(c):
    return (_silu(c),)


def _f_modulate(x, scale, shift, g):
    return ((_rms(x, g) * (1.0 + scale) + shift).astype(BF16),)


def _f_rms(x, g):
    return (_rms(x, g).astype(BF16),)


def _f_dt(dt_raw, dt_bias, a_log):
    z = dt_raw + dt_bias
    dt = jnp.maximum(z, 0.0) + jnp.log1p(jnp.exp(-jnp.abs(z)))
    return dt, dt * (-jnp.exp(a_log))


def _f_gated_norm(y, z, g):
    return (_rms(y * _silu(z.astype(F32)), g).astype(BF16),)


def _f_merge(attn, ssm, ga, gb):
    return ((lax.logistic(ga.astype(F32)) * attn + lax.logistic(gb.astype(F32)) * ssm).astype(BF16),)


def _f_post(x, m, gate, g):
    return (x + gate * _rms(m, g),)


def _f_final_loss(x, ff, target, gate, g):
    e = x + gate * _rms(ff, g) - target
    return (e * e * (0.5 / D_MODEL),)


def _rope_tables(posf, inv_lane):
    B, S, _ = posf.shape
    ts = min(S, 512)

    def body(p_ref, inv_ref, c_ref, a_ref, b_ref):
        ang = p_ref[0] * inv_ref[...]
        cs, sn = jnp.cos(ang), jnp.sin(ang)
        lane = lax.broadcasted_iota(jnp.int32, ang.shape, 1)
        c_ref[0] = jnp.where(lane < ROPE, cs, 0.0)
        a_ref[0] = jnp.where(lane < ROPE // 2, -sn, 0.0)
        b_ref[0] = jnp.where((lane >= ROPE // 2) & (lane < ROPE), sn, 0.0)

    spec = pl.BlockSpec((1, ts, LANE), lambda b, s: (b, s, 0))
    sds = jax.ShapeDtypeStruct((B, S, LANE), F32)
    return pl.pallas_call(
        body, grid=(B, S // ts),
        in_specs=[pl.BlockSpec((1, ts, 1), lambda b, s: (b, s, 0)), pl.BlockSpec((1, LANE), lambda b, s: (0, 0))],
        out_specs=[spec, spec, spec], out_shape=[sds, sds, sds], name="rope_tables",
        compiler_params=_cparams(("parallel", "parallel")))(posf, inv_lane)


def _rot(u, c, a, bm):
    return u * c + pltpu.roll(u, 96, 1) * a + pltpu.roll(u, 32, 1) * bm


def _rot_t(g, c, a, bm):
    return g * c + pltpu.roll(g * a, 32, 1) + pltpu.roll(g * bm, 96, 1)


def _rope_q_call(q, tabs, transpose, name):
    B, S, W = q.shape
    ts = min(S, 512)
    fn = _rot_t if transpose else _rot
    out_dtype = F32 if transpose else BF16

    def body(q_ref, c_ref, a_ref, b_ref, o_ref):
        tc, ta, tb = c_ref[0], a_ref[0], b_ref[0]
        for h in range(W // QK_PAD):
            u = q_ref[0, :, h * QK_PAD:(h + 1) * QK_PAD].astype(F32) * ATT_SCALE
            r = fn(u[:, NOPE:], tc, ta, tb)
            o_ref[0, :, h * QK_PAD:(h + 1) * QK_PAD] = jnp.concatenate([u[:, :NOPE], r], axis=1).astype(out_dtype)

    tspec = pl.BlockSpec((1, ts, LANE), lambda b, s: (b, s, 0))
    qspec = pl.BlockSpec((1, ts, W), lambda b, s: (b, s, 0))
    return pl.pallas_call(
        body, grid=(B, S // ts), in_specs=[qspec, tspec, tspec, tspec], out_specs=qspec,
        out_shape=jax.ShapeDtypeStruct(q.shape, out_dtype), name=name,
        compiler_params=_cparams(("parallel", "parallel")))(q, *tabs)


@jax.custom_vjp
def rope_q(q, tabs):
    return _rope_q_call(q, tabs, False, "rope_q_fwd")


def _rope_q_fwd(q, tabs):
    return _rope_q_call(q, tabs, False, "rope_q_fwd"), tabs


def _rope_q_bwd(tabs, g):
    return _rope_q_call(g, tabs, True, "rope_q_bwd"), tuple(jnp.zeros_like(t) for t in tabs)


rope_q.defvjp(_rope_q_fwd, _rope_q_bwd)


def _build_k_fwd_call(kv, kr, tabs):
    B, S, _ = kv.shape
    ts = min(S, 512)

    def body(kv_ref, kr_ref, c_ref, a_ref, b_ref, o_ref):
        r = _rot(kr_ref[0], c_ref[0], a_ref[0], b_ref[0]).astype(BF16)
        for h in range(N_HEADS):
            o_ref[0, :, h * QK_PAD:(h + 1) * QK_PAD] = jnp.concatenate(
                [kv_ref[0, :, h * NOPE:(h + 1) * NOPE], r], axis=1)

    tspec = pl.BlockSpec((1, ts, LANE), lambda b, s: (b, s, 0))
    kr_spec = pl.BlockSpec((1, ts, LANE), lambda b, s: (b, s, KR_LANE0 // LANE))
    return pl.pallas_call(
        body, grid=(B, S // ts),
        in_specs=[pl.BlockSpec((1, ts, N_HEADS * NOPE), lambda b, s: (b, s, 0)), kr_spec, tspec, tspec, tspec],
        out_specs=pl.BlockSpec((1, ts, N_HEADS * QK_PAD), lambda b, s: (b, s, 0)),
        out_shape=jax.ShapeDtypeStruct((B, S, N_HEADS * QK_PAD), BF16), name="build_k_fwd",
        compiler_params=_cparams(("parallel", "parallel")))(kv, kr, *tabs)


def _build_k_bwd_call(g, tabs):
    B, S, _ = g.shape
    ts = min(S, 512)

    def body(g_ref, c_ref, a_ref, b_ref, dk_ref, dr_ref):
        tot = None
        for h in range(N_HEADS):
            dk_ref[0, :, h * NOPE:(h + 1) * NOPE] = g_ref[0, :, h * QK_PAD:h * QK_PAD + NOPE]
            part = g_ref[0, :, h * QK_PAD + NOPE:(h + 1) * QK_PAD].astype(F32)
            tot = part if tot is None else tot + part
        dr_ref[0] = _rot_t(tot, c_ref[0], a_ref[0], b_ref[0]).astype(BF16)

    tspec = pl.BlockSpec((1, ts, LANE), lambda b, s: (b, s, 0))
    return pl.pallas_call(
        body, grid=(B, S // ts),
        in_specs=[pl.BlockSpec((1, ts, N_HEADS * QK_PAD), lambda b, s: (b, s, 0)), tspec, tspec, tspec],
        out_specs=[pl.BlockSpec((1, ts, N_HEADS * NOPE), lambda b, s: (b, s, 0)), tspec],
        out_shape=[jax.ShapeDtypeStruct((B, S, N_HEADS * NOPE), BF16), jax.ShapeDtypeStruct((B, S, LANE), BF16)],
        name="build_k_bwd", compiler_params=_cparams(("parallel", "parallel")))(g, *tabs)


@jax.custom_vjp
def build_k(kv, src, stand_in, tabs):
    return _build_k_fwd_call(kv, src, tabs)


def _build_k_fwd(kv, src, stand_in, tabs):
    return _build_k_fwd_call(kv, src, tabs), (tabs, kv.shape, src)


def _build_k_bwd(res, g):
    tabs, kv_shape, src = res
    dk, dr = _build_k_bwd_call(g, tabs)
    dkv = jnp.concatenate([dk, jnp.zeros((kv_shape[0], kv_shape[1], kv_shape[2] - dk.shape[2]), BF16)], axis=-1)
    return dkv, jnp.zeros_like(src), dr, tuple(jnp.zeros_like(t) for t in tabs)


build_k.defvjp(_build_k_fwd, _build_k_bwd)


ATT_SCALE = (NOPE + ROPE) ** -0.5
NEG = -1e30


def _att_tiles(S):
    t = min(S, 512)
    return t, S // t


def _scores(q, k, diagonal):
    s = lax.dot_general(q, k, (((1,), (1,)), ((), ())), preferred_element_type=F32)
    if diagonal:
        row = lax.broadcasted_iota(jnp.int32, s.shape, 0)
        col = lax.broadcasted_iota(jnp.int32, s.shape, 1)
        s = jnp.where(col <= row, s, NEG)
    return s


ATT_HB = 4


def _causal_pairs(n):
    pairs = [(i, j) for i in range(n) for j in range(i + 1)]
    return (jnp.asarray([p[0] for p in pairs], jnp.int32), jnp.asarray([p[1] for p in pairs], jnp.int32))


def _head(ref_or_val, h, w):
    return ref_or_val[:, h * w:(h + 1) * w]


def _attn_fwd_call(q, k, vsrc, v_blk0):
    B, S, _ = q.shape
    t, n = _att_tiles(S)
    qi, kj = _causal_pairs(n)

    def body(qi_ref, kj_ref, q_ref, k_ref, v_ref, o_ref, lse_ref, m_sc, l_sc, acc_sc):
        p_id = pl.program_id(2)
        i, j = qi_ref[p_id], kj_ref[p_id]

        @pl.when(j == 0)
        def _():
            m_sc[...] = jnp.full(m_sc.shape, NEG, F32)
            l_sc[...] = jnp.zeros(l_sc.shape, F32)
            acc_sc[...] = jnp.zeros(acc_sc.shape, F32)

        def step(diagonal):
            qa, ka, va = q_ref[0], k_ref[0], v_ref[0]
            for h in range(ATT_HB):
                lanes = slice(h * LANE, (h + 1) * LANE)
                s = _scores(_head(qa, h, QK_PAD), _head(ka, h, QK_PAD), diagonal)
                m_prev = m_sc[:, lanes]
                m_new = jnp.maximum(m_prev, jnp.max(s, axis=1, keepdims=True))
                alpha = jnp.exp(m_prev - m_new)
                p = jnp.exp(s - jnp.tile(m_new, (1, t // LANE)))
                l_sc[:, lanes] = alpha * l_sc[:, lanes] + jnp.sum(p, axis=1, keepdims=True)
                acc_sc[:, lanes] = alpha * acc_sc[:, lanes] + jnp.dot(p.astype(BF16), _head(va, h, V_DIM),
                                                                      preferred_element_type=F32)
                m_sc[:, lanes] = m_new

        @pl.when(j < i)
        def _():
            step(False)

        @pl.when(j == i)
        def _():
            step(True)
            o_ref[0] = acc_sc[...] / l_sc[...]
            lse_ref[0] = m_sc[...] + jnp.log(l_sc[...])

    wq, wv = ATT_HB * QK_PAD, ATT_HB * V_DIM
    grid_spec = pltpu.PrefetchScalarGridSpec(
        num_scalar_prefetch=2, grid=(B, N_HEADS // ATT_HB, qi.shape[0]),
        in_specs=[pl.BlockSpec((1, t, wq), lambda b, h, p, qi, kj: (b, qi[p], h)),
                  pl.BlockSpec((1, t, wq), lambda b, h, p, qi, kj: (b, kj[p], h)),
                  pl.BlockSpec((1, t, wv), lambda b, h, p, qi, kj: (b, kj[p], v_blk0 + h))],
        out_specs=[pl.BlockSpec((1, t, wv), lambda b, h, p, qi, kj: (b, qi[p], h)),
                   pl.BlockSpec((1, t, wv), lambda b, h, p, qi, kj: (b, qi[p], h))],
        scratch_shapes=[pltpu.VMEM((t, wv), F32), pltpu.VMEM((t, wv), F32), pltpu.VMEM((t, wv), F32)])
    return pl.pallas_call(
        body, grid_spec=grid_spec,
        out_shape=[jax.ShapeDtypeStruct((B, S, N_HEADS * V_DIM), F32),
                   jax.ShapeDtypeStruct((B, S, N_HEADS * LANE), F32)],
        name="attn_fwd", compiler_params=_cparams(("parallel", "parallel", "arbitrary")))(qi, kj, q, k, vsrc)


def _attn_p_ds(q, k, v, o, do, lse, diagonal, t):
    s = _scores(q, k, diagonal)
    p = jnp.exp(s - jnp.tile(lse, (1, t // LANE)))
    dp = lax.dot_general(do.astype(BF16), v, (((1,), (1,)), ((), ())), preferred_element_type=F32)
    delta = jnp.sum(do * o, axis=1, keepdims=True)
    ds = p * (dp - delta)
    return p, ds


ATT_HB_BWD = 2


def _attn_bwd_call(q, k, vsrc, o, do, lse):
    B, S, _ = q.shape
    t, n = _att_tiles(S)
    qi, kj = _causal_pairs(n)
    n_pairs = qi.shape[0]
    hb = ATT_HB_BWD
    v_blk0 = N_HEADS // hb

    def body(qi_ref, kj_ref, q_ref, k_ref, v_ref, o_ref, do_ref, lse_ref, dq_ref, dk_ref, dv_ref, dq_sc, dk_sc, dv_sc):
        p_id = pl.program_id(2)
        i, j = qi_ref[p_id], kj_ref[p_id]

        @pl.when(p_id == 0)
        def _():
            dk_sc[...] = jnp.zeros(dk_sc.shape, F32)
            dv_sc[...] = jnp.zeros(dv_sc.shape, F32)

        @pl.when(j == 0)
        def _():
            dq_sc[...] = jnp.zeros(dq_sc.shape, F32)

        rows = pl.ds(pl.multiple_of(j * t, t), t)

        def step(diagonal):
            qa, ka, va, oa, doa, la = q_ref[0], k_ref[0], v_ref[0], o_ref[0], do_ref[0], lse_ref[0]
            for h in range(hb):
                qb, kb, dob = _head(qa, h, QK_PAD), _head(ka, h, QK_PAD), _head(doa, h, V_DIM)
                p, ds = _attn_p_ds(qb, kb, _head(va, h, V_DIM), _head(oa, h, V_DIM), dob, _head(la, h, LANE),
                                   diagonal, t)
                dsb = ds.astype(BF16)
                dq_sc[:, h * QK_PAD:(h + 1) * QK_PAD] += jnp.dot(dsb, kb, preferred_element_type=F32)
                dv_sc[rows, h * V_DIM:(h + 1) * V_DIM] += lax.dot_general(
                    p.astype(BF16), dob.astype(BF16), (((0,), (0,)), ((), ())), preferred_element_type=F32)
                dk_sc[rows, h * QK_PAD:(h + 1) * QK_PAD] += lax.dot_general(
                    dsb, qb, (((0,), (0,)), ((), ())), preferred_element_type=F32)

        @pl.when(j < i)
        def _():
            step(False)

        @pl.when(j == i)
        def _():
            step(True)
            dq_ref[0] = dq_sc[...].astype(BF16)

        @pl.when(p_id == n_pairs - 1)
        def _():
            dk_ref[0] = dk_sc[...].astype(BF16)
            dv_ref[0] = dv_sc[...].astype(BF16)

    wq, wv = hb * QK_PAD, hb * V_DIM
    at_q = lambda b, h, p, qi, kj: (b, qi[p], h)
    at_k = lambda b, h, p, qi, kj: (b, kj[p], h)
    whole = lambda b, h, p, qi, kj: (b, 0, h)
    grid_spec = pltpu.PrefetchScalarGridSpec(
        num_scalar_prefetch=2, grid=(B, N_HEADS // hb, n_pairs),
        in_specs=[pl.BlockSpec((1, t, wq), at_q), pl.BlockSpec((1, t, wq), at_k),
                  pl.BlockSpec((1, t, wv), lambda b, h, p, qi, kj: (b, kj[p], v_blk0 + h)),
                  pl.BlockSpec((1, t, wv), at_q), pl.BlockSpec((1, t, wv), at_q), pl.BlockSpec((1, t, wv), at_q)],
        out_specs=[pl.BlockSpec((1, t, wq), at_q), pl.BlockSpec((1, S, wq), whole), pl.BlockSpec((1, S, wv), whole)],
        scratch_shapes=[pltpu.VMEM((t, wq), F32), pltpu.VMEM((S, wq), F32), pltpu.VMEM((S, wv), F32)])
    return pl.pallas_call(
        body, grid_spec=grid_spec,
        out_shape=[jax.ShapeDtypeStruct((B, S, N_HEADS * QK_PAD), BF16),
                   jax.ShapeDtypeStruct((B, S, N_HEADS * QK_PAD), BF16),
                   jax.ShapeDtypeStruct((B, S, N_HEADS * V_DIM), BF16)],
        name="attn_bwd", compiler_params=_cparams(("parallel", "parallel", "arbitrary")))(
            qi, kj, q, k, vsrc, o, do, lse)


@jax.custom_vjp
def attention(q, k, kv):
    return _attn_fwd_call(q, k, kv, N_HEADS // ATT_HB)[0]


def _attention_fwd(q, k, kv):
    o, lse = _attn_fwd_call(q, k, kv, N_HEADS // ATT_HB)
    return o, (q, k, kv, o, lse)


def _attention_bwd(res, do):
    q, k, kv, o, lse = res
    dq, dk, dv = _attn_bwd_call(q, k, kv, o, do, lse)
    dkv = jnp.concatenate([jnp.zeros_like(dv), dv], axis=-1)
    return dq, dk, dkv


attention.defvjp(_attention_fwd, _attention_bwd)


SUBLANES = 8


def _zero_tail(v):
    return jnp.concatenate([v, jnp.zeros((SUBLANES, v.shape[1]), v.dtype)], axis=0)


def _shift_down(vz, sh):
    return pltpu.roll(vz, sh, 0)[:vz.shape[0] - SUBLANES]


def _shift_up(vz, sh):
    return pltpu.roll(vz, vz.shape[0] - sh, 0)[:vz.shape[0] - SUBLANES]


def _conv_pre(u, uz, w_ref, b_ref):
    acc = b_ref[...] + w_ref[pl.ds(CONV_K - 1, 1), :] * u
    for k in range(CONV_K - 1):
        acc = acc + w_ref[pl.ds(k, 1), :] * _shift_down(uz, CONV_K - 1 - k)
    return acc


def _conv_fwd_call(src, w, b):
    B, S, _ = src.shape
    C = w.shape[1]

    def body(u_ref, w_ref, b_ref, o_ref):
        uu = u_ref[0].astype(F32)
        o_ref[0] = _silu(_conv_pre(uu, _zero_tail(uu), w_ref, b_ref))

    spec = pl.BlockSpec((1, S, LANE), lambda c, bb: (bb, 0, c))
    return pl.pallas_call(
        body, grid=(C // LANE, B),
        in_specs=[pl.BlockSpec((1, S, LANE), lambda c, bb: (bb, 0, c + CONV_LANE0 // LANE)),
                  pl.BlockSpec((CONV_K, LANE), lambda c, bb: (0, c)), pl.BlockSpec((1, LANE), lambda c, bb: (0, c))],
        out_specs=spec, out_shape=jax.ShapeDtypeStruct((B, S, C), F32), name="conv_fwd",
        compiler_params=_cparams(("parallel", "arbitrary")))(src, w, b)


def _conv_bwd_call(src, w, b, g):
    B, S, _ = src.shape
    C = w.shape[1]

    def body(u_ref, w_ref, b_ref, g_ref, du_ref, dw_ref, db_ref):
        uu = u_ref[0].astype(F32)
        uz = _zero_tail(uu)
        pre = _conv_pre(uu, uz, w_ref, b_ref)
        sg = lax.logistic(pre)
        dpre = g_ref[0] * sg * (1.0 + pre * (1.0 - sg))
        dz = _zero_tail(dpre)
        du = w_ref[pl.ds(CONV_K - 1, 1), :] * dpre
        dws = [None] * CONV_K
        dws[CONV_K - 1] = jnp.sum(dpre * uu, axis=0, keepdims=True)
        for k in range(CONV_K - 1):
            sh = CONV_K - 1 - k
            du = du + w_ref[pl.ds(k, 1), :] * _shift_up(dz, sh)
            dws[k] = jnp.sum(dpre * _shift_down(uz, sh), axis=0, keepdims=True)
        du_ref[0] = du.astype(du_ref.dtype)
        dbv = jnp.sum(dpre, axis=0, keepdims=True)
        first = pl.program_id(1) == 0

        @pl.when(first)
        def _():
            for k in range(CONV_K):
                dw_ref[pl.ds(k, 1), :] = dws[k]
            db_ref[...] = dbv

        @pl.when(jnp.logical_not(first))
        def _():
            for k in range(CONV_K):
                dw_ref[pl.ds(k, 1), :] += dws[k]
            db_ref[...] += dbv

    spec = pl.BlockSpec((1, S, LANE), lambda c, bb: (bb, 0, c))
    wspec = pl.BlockSpec((CONV_K, LANE), lambda c, bb: (0, c))
    bspec = pl.BlockSpec((1, LANE), lambda c, bb: (0, c))
    uspec = pl.BlockSpec((1, S, LANE), lambda c, bb: (bb, 0, c + CONV_LANE0 // LANE))
    return pl.pallas_call(
        body, grid=(C // LANE, B), in_specs=[uspec, wspec, bspec, spec], out_specs=[spec, wspec, bspec],
        out_shape=[jax.ShapeDtypeStruct((B, S, C), BF16), jax.ShapeDtypeStruct(w.shape, F32),
                   jax.ShapeDtypeStruct(b.shape, F32)],
        name="conv_bwd", compiler_params=_cparams(("parallel", "arbitrary")))(src, w, b, g)


@jax.custom_vjp
def conv_silu(src, stand_in, w, b):
    return _conv_fwd_call(src, w, b)


def _conv_silu_fwd(src, stand_in, w, b):
    return _conv_fwd_call(src, w, b), (src, w, b)


def _conv_silu_bwd(res, g):
    du, dw, db = _conv_bwd_call(*res, g)
    return jnp.zeros_like(res[0]), du, dw, db


conv_silu.defvjp(_conv_silu_fwd, _conv_silu_bwd)


def _chunk_cumsum_call(a, reverse, name):
    B, S, W = a.shape
    per_step = min(S // CHUNK, 8)

    def body(a_ref, o_ref):
        r = lax.broadcasted_iota(jnp.int32, (CHUNK, CHUNK), 0)
        c = lax.broadcasted_iota(jnp.int32, (CHUNK, CHUNK), 1)
        tri = jnp.where((c >= r) if reverse else (c <= r), 1.0, 0.0).astype(F32)
        for i in range(per_step):
            rows = pl.ds(i * CHUNK, CHUNK)
            o_ref[0, rows, :] = jnp.dot(tri, a_ref[0, rows, :], preferred_element_type=F32,
                                        precision=lax.Precision.HIGHEST)

    spec = pl.BlockSpec((1, per_step * CHUNK, W), lambda b, c: (b, c, 0))
    return pl.pallas_call(body, grid=(B, S // (per_step * CHUNK)), in_specs=[spec], out_specs=spec,
                          out_shape=jax.ShapeDtypeStruct(a.shape, F32), name=name,
                          compiler_params=_cparams(("parallel", "parallel")))(a)


@jax.custom_vjp
def chunk_cumsum(a):
    return _chunk_cumsum_call(a, False, "chunk_cumsum_fwd")


chunk_cumsum.defvjp(lambda a: (_chunk_cumsum_call(a, False, "chunk_cumsum_fwd"), None),
                    lambda _, g: (_chunk_cumsum_call(g, True, "chunk_cumsum_bwd"),))


GROUP_W = 4 * HEAD_P
HPG = SSM_HEADS // SSM_GROUPS


def _ssd_masks():
    lane = lax.broadcasted_iota(jnp.int32, (1, GROUP_W), 1)
    return [((lane >= HEAD_P * j) & (lane < HEAD_P * (j + 1))).astype(F32) for j in range(HPG)]


def _ssd_decays(ac_cols, acr_ref, gi):
    r = lax.broadcasted_iota(jnp.int32, (CHUNK, CHUNK), 0)
    c = lax.broadcasted_iota(jnp.int32, (CHUNK, CHUNK), 1)
    return [jnp.exp(jnp.where(c <= r, ac_cols[j] - acr_ref[0, gi * HPG + j], NEG)) for j in range(HPG)]


def _ssd_cols(blk, g):
    lane = lax.broadcasted_iota(jnp.int32, blk.shape, 1)
    return [jnp.sum(jnp.where(lane == HPG * g + j, blk, 0.0), axis=1, keepdims=True) for j in range(HPG)]


def _ssd_spread(cols):
    lane = lax.broadcasted_iota(jnp.int32, (1, GROUP_W), 1)
    out = jnp.broadcast_to(cols[HPG - 1], (CHUNK, GROUP_W))
    for j in range(HPG - 2, -1, -1):
        out = jnp.where(lane < HEAD_P * (j + 1), cols[j], out)
    return out


def _ssd_gather(val, cols, masks, g):
    lane = lax.broadcasted_iota(jnp.int32, (1, LANE), 1)
    out = jnp.zeros((CHUNK, LANE), F32)
    for j in range(HPG):
        tot = jnp.sum(val * masks[j], axis=1, keepdims=True)
        if cols is not None:
            tot = tot + cols[j]
        out = out + tot * (lane == HPG * g + j).astype(F32)
    return out


def _dot(a, b, dims):
    return lax.dot_general(a.astype(BF16), b.astype(BF16), (dims, ((), ())), preferred_element_type=F32)


NN = ((1,), (0,))
NT = ((1,), (1,))
TN = ((0,), (0,))


XBC_W = GROUP_W + 2 * STATE_N


SSD_STEP_GROUPS_FWD = 4
SSD_STEP_GROUPS_BWD = 2


def _ssd_load(xbc_ref, dt_ref, ac_ref, masks, g, gi):
    x = xbc_ref[0, :, gi * XBC_W:gi * XBC_W + GROUP_W]
    bm = xbc_ref[0, :, gi * XBC_W + GROUP_W:gi * XBC_W + GROUP_W + STATE_N]
    cm = xbc_ref[0, :, gi * XBC_W + GROUP_W + STATE_N:(gi + 1) * XBC_W]
    ac_cols = _ssd_cols(ac_ref[0], g)
    dt = _ssd_spread(_ssd_cols(dt_ref[0], g))
    ac = _ssd_spread(ac_cols)
    is_last = (lax.broadcasted_iota(jnp.int32, (CHUNK, GROUP_W), 0) == CHUNK - 1).astype(F32)
    return x, bm, cm, dt, ac, ac_cols, is_last


def _ssd_in_specs(nc, rev, gb):
    cc = (lambda c: nc - 1 - c) if rev else (lambda c: c)
    return [pl.BlockSpec((1, CHUNK, gb * XBC_W), lambda b, g, c: (b, cc(c), g)),
            pl.BlockSpec((1, CHUNK, LANE), lambda b, g, c: (b, cc(c), 0)),
            pl.BlockSpec((1, CHUNK, LANE), lambda b, g, c: (b, cc(c), 0)),
            pl.BlockSpec((1, gb * HPG, 1, CHUNK), lambda b, g, c: (b, g, 0, cc(c))),
            pl.BlockSpec((1, gb * GROUP_W), lambda b, g, c: (0, g))]


def _ssd_fwd_call(xbc, dtp, acp, acr, dsk):
    B, S, _ = xbc.shape
    nc = S // CHUNK
    gb = SSD_STEP_GROUPS_FWD

    def body(xbc_ref, dt_ref, ac_ref, ar_ref, ds_ref, y_ref, hp_ref, h_sc):
        @pl.when(pl.program_id(2) == 0)
        def _():
            h_sc[...] = jnp.zeros(h_sc.shape, F32)

        masks = _ssd_masks()
        ys = []
        for gi in range(gb):
            grp = gb * pl.program_id(1) + gi
            x, bm, cm, dt, ac, ac_cols, is_last = _ssd_load(xbc_ref, dt_ref, ac_ref, masks, grp, gi)
            last = jnp.sum(ac * is_last, axis=0, keepdims=True)
            decays = _ssd_decays(ac_cols, ar_ref, gi)
            xd = x * dt
            cb = _dot(cm, bm, NT)
            hprev = h_sc[gi]
            hp_ref[0, gi, 0] = hprev
            y = _dot(cm, hprev, NN) * jnp.exp(ac) + ds_ref[:, gi * GROUP_W:(gi + 1) * GROUP_W] * x
            for j in range(HPG):
                y = y + _dot(cb * decays[j], xd * masks[j], NN)
            ys.append(y)
            h_sc[gi] = hprev * jnp.exp(last) + _dot(bm, xd * jnp.exp(last - ac), TN)
        y_ref[0] = jnp.concatenate(ys, axis=1)

    ng = SSM_GROUPS // gb
    return pl.pallas_call(
        body, grid=(B, ng, nc), in_specs=_ssd_in_specs(nc, False, gb),
        out_specs=[pl.BlockSpec((1, CHUNK, gb * GROUP_W), lambda b, g, c: (b, c, g)),
                   pl.BlockSpec((1, gb, 1, STATE_N, GROUP_W), lambda b, g, c: (b, g, c, 0, 0))],
        out_shape=[jax.ShapeDtypeStruct((B, S, D_INNER), F32),
                   jax.ShapeDtypeStruct((B, SSM_GROUPS, nc, STATE_N, GROUP_W), F32)],
        scratch_shapes=[pltpu.VMEM((gb, STATE_N, GROUP_W), F32)], name="ssd_fwd",
        compiler_params=_cparams(("parallel", "parallel", "arbitrary")))(xbc, dtp, acp, acr, dsk)


def _ssd_bwd_call(xbc, dtp, acp, acr, dsk, hps, dy):
    B, S, _ = xbc.shape
    nc = S // CHUNK
    gb = SSD_STEP_GROUPS_BWD

    def body(xbc_ref, dt_ref, ac_ref, ar_ref, ds_ref, hp_ref, dy_ref,
             dxbc_ref, ddt_ref, dac_ref, dar_ref, dds_ref, dh_sc):
        first = pl.program_id(2) == 0

        @pl.when(first)
        def _():
            dh_sc[...] = jnp.zeros(dh_sc.shape, F32)

        masks = _ssd_masks()
        dxbc_parts, dds_parts = [], []
        for gi in range(gb):
            grp = gb * pl.program_id(0) + gi
            x, bm, cm, dt, ac, ac_cols, is_last = _ssd_load(xbc_ref, dt_ref, ac_ref, masks, grp, gi)
            last = jnp.sum(ac * is_last, axis=0, keepdims=True)
            g = dy_ref[0, :, gi * GROUP_W:(gi + 1) * GROUP_W]
            hprev = hp_ref[0, gi, 0]
            dh = dh_sc[gi]
            decays = _ssd_decays(ac_cols, ar_ref, gi)
            dcols = []
            xd = x * dt
            cb = _dot(cm, bm, NT)
            e_c = jnp.exp(ac)
            e_end = jnp.exp(last - ac)
            e_last = jnp.exp(last)
            z = _dot(cm, hprev, NN)
            dz = g * e_c
            dac = g * z * e_c
            dc = _dot(dz, hprev, NT)
            dhprev = _dot(cm, dz, TN) + dh * e_last
            dcb = jnp.zeros((CHUNK, CHUNK), F32)
            dxd = jnp.zeros(xd.shape, F32)
            for j in range(HPG):
                gj = cb * decays[j]
                dgj = _dot(g * masks[j], xd, NT)
                dxd = dxd + _dot(gj, g, TN) * masks[j]
                dcb = dcb + dgj * decays[j]
                dseg = dgj * gj
                dcols.append(jnp.sum(dseg, axis=1, keepdims=True))
                dar_ref[0, gi * HPG + j] = -jnp.sum(dseg, axis=0, keepdims=True)
            dc = dc + _dot(dcb, bm, NN)
            db = _dot(dcb, cm, TN)
            sx = xd * e_end
            db = db + _dot(sx, dh, NT)
            dsx = _dot(bm, dh, NN)
            dxd = dxd + dsx * e_end
            de = dsx * sx
            dac = dac - de
            dlast = jnp.sum(de, axis=0, keepdims=True) + jnp.sum(dh * hprev, axis=0, keepdims=True) * e_last
            dsk = ds_ref[:, gi * GROUP_W:(gi + 1) * GROUP_W]
            dxbc_parts += [dxd * dt + dsk * g, db, dc]
            ddt_ref[0, gi] = _ssd_gather(dxd * x, None, masks, grp)
            dac_ref[0, gi] = _ssd_gather(dac + is_last * dlast, dcols, masks, grp)
            dds_parts.append(jnp.sum(g * x, axis=0, keepdims=True))
            dh_sc[gi] = dhprev
        dxbc_ref[0] = jnp.concatenate(dxbc_parts, axis=1)
        dds = jnp.concatenate(dds_parts, axis=1)
        first_all = first & (pl.program_id(1) == 0)

        @pl.when(first_all)
        def _():
            dds_ref[...] = dds

        @pl.when(jnp.logical_not(first_all))
        def _():
            dds_ref[...] += dds

    rc = lambda c: nc - 1 - c
    ng = SSM_GROUPS // gb
    in_specs = [pl.BlockSpec(s.block_shape, (lambda g, b, c, f=s.index_map: f(b, g, c))) for s in _ssd_in_specs(nc, True, gb)]
    in_specs.append(pl.BlockSpec((1, gb, 1, STATE_N, GROUP_W), lambda g, b, c: (b, g, rc(c), 0, 0)))
    in_specs.append(pl.BlockSpec((1, CHUNK, gb * GROUP_W), lambda g, b, c: (b, rc(c), g)))
    per_group = pl.BlockSpec((1, gb, CHUNK, LANE), lambda g, b, c: (b, g, rc(c), 0))
    out_specs = [pl.BlockSpec((1, CHUNK, gb * XBC_W), lambda g, b, c: (b, rc(c), g)), per_group, per_group,
                 pl.BlockSpec((1, gb * HPG, 1, CHUNK), lambda g, b, c: (b, g, 0, rc(c))),
                 pl.BlockSpec((1, gb * GROUP_W), lambda g, b, c: (0, g))]
    out_shape = [jax.ShapeDtypeStruct(xbc.shape, F32),
                 jax.ShapeDtypeStruct((B, SSM_GROUPS, S, LANE), F32), jax.ShapeDtypeStruct((B, SSM_GROUPS, S, LANE), F32),
                 jax.ShapeDtypeStruct(acr.shape, F32), jax.ShapeDtypeStruct(dsk.shape, F32)]
    return pl.pallas_call(
        body, grid=(ng, B, nc), in_specs=in_specs, out_specs=out_specs, out_shape=out_shape,
        scratch_shapes=[pltpu.VMEM((gb, STATE_N, GROUP_W), F32)], name="ssd_bwd",
        compiler_params=_cparams(("arbitrary", "arbitrary", "arbitrary")))(xbc, dtp, acp, acr, dsk, hps, dy)


@jax.custom_vjp
def ssd(xbc, dtp, acp, acr, dsk):
    return _ssd_fwd_call(xbc, dtp, acp, acr, dsk)[0]


def _ssd_fwd(xbc, dtp, acp, acr, dsk):
    y, hps = _ssd_fwd_call(xbc, dtp, acp, acr, dsk)
    return y, (xbc, dtp, acp, acr, dsk, hps)


def _ssd_bwd(res, dy):
    dxbc, ddt, dac, dacr, dds = _ssd_bwd_call(*res, dy)
    return dxbc, jnp.sum(ddt, axis=1), jnp.sum(dac, axis=1), dacr, dds


ssd.defvjp(_ssd_fwd, _ssd_bwd)


def _pack_small(arrs):
    flat = jnp.concatenate([a.reshape(-1) for a in arrs])
    rows = -(-flat.shape[0] // (8 * LANE)) * 8
    return jnp.pad(flat, (0, rows * LANE - flat.shape[0])).reshape(rows, LANE)


def _unpack_small(buf, shapes):
    flat = buf.reshape(-1)
    out, off = [], 0
    for shp in shapes:
        n = int(np.prod(shp))
        out.append(flat[off:off + n].reshape(shp))
        off += n
    return out


def _rows_tile(rows, cap):
    for cand in range(min(rows, cap), 7, -8):
        if rows % cand == 0:
            return cand
    return rows


def _pair_sum(mine, theirs, cidx, name):
    n4, kk, nn = mine.shape
    half = kk // 2
    tr = _rows_tile(half, 256)
    nb = half // tr

    def body(c_ref, a_ref, b_ref, o_ref, ob_ref):
        tot = a_ref[...] + b_ref[...]
        o_ref[...] = tot
        ob_ref[...] = tot.astype(BF16)

    spec = pl.BlockSpec((1, tr, nn), lambda j, i, c: (j, i, 0))
    grid_spec = pltpu.PrefetchScalarGridSpec(
        num_scalar_prefetch=1, grid=(n4, nb),
        in_specs=[pl.BlockSpec((1, tr, nn), lambda j, i, c: (j, c[0] * nb + i, 0)), spec], out_specs=[spec, spec])
    return pl.pallas_call(
        body, grid_spec=grid_spec,
        out_shape=[jax.ShapeDtypeStruct((n4, half, nn), F32), jax.ShapeDtypeStruct((n4, half, nn), BF16)],
        name=name, compiler_params=_cparams(("parallel", "parallel")))(cidx, mine, theirs)


def _chip_sum(quad, pair, chip_idx, name):
    _, rows, nn = quad.shape
    tr = _rows_tile(rows, 256)

    def body(s_ref, q_ref, p_ref, o_ref):
        for mine in range(4):
            @pl.when(s_ref[0] == mine)
            def _(mine=mine):
                acc = None
                for d in range(4):
                    term = p_ref[0] if d == mine else q_ref[d].astype(F32)
                    acc = term if acc is None else acc + term
                o_ref[...] = acc

    grid_spec = pltpu.PrefetchScalarGridSpec(
        num_scalar_prefetch=1, grid=(rows // tr,),
        in_specs=[pl.BlockSpec((4, tr, nn), lambda i, s: (0, i, 0)), pl.BlockSpec((1, tr, nn), lambda i, s: (s[0], i, 0))],
        out_specs=pl.BlockSpec((tr, nn), lambda i, s: (i, 0)))
    return pl.pallas_call(body, grid_spec=grid_spec, out_shape=jax.ShapeDtypeStruct((rows, nn), F32), name=name,
                          compiler_params=_cparams(("parallel",)))(chip_idx, quad, pair)


def _adam_halves_call(w, mine, other, cidx, m, v, name):
    _, rows, nn = w.shape
    half = rows // 2
    tr = _rows_tile(half, 128)
    nb = half // tr

    def body(c_ref, w_ref, a_ref, b_ref, m_ref, v_ref, g_ref, d_ref, nm_ref, nv_ref):
        upper = (pl.program_id(0) >= nb).astype(jnp.int32)
        g = jnp.where(upper == c_ref[0], a_ref[...], b_ref[...])
        g_ref[0] = g
        d_ref[0], nm_ref[0], nv_ref[0] = _adam_fn(w_ref[0], g, m_ref[0], v_ref[0])

    spec = pl.BlockSpec((1, tr, nn), lambda i, c: (0, i, 0))
    hspec = pl.BlockSpec((tr, nn), lambda i, c: (i % nb, 0))
    grid_spec = pltpu.PrefetchScalarGridSpec(num_scalar_prefetch=1, grid=(2 * nb,),
                                             in_specs=[spec, hspec, hspec, spec, spec], out_specs=[spec] * 4)
    return pl.pallas_call(body, grid_spec=grid_spec, out_shape=[jax.ShapeDtypeStruct(w.shape, F32)] * 4, name=name,
                          compiler_params=_cparams(("parallel",)))(cidx, w, mine, other, m, v)


def _stack_sum(stack, name):
    n, rows, nn = stack.shape
    tr = _rows_tile(rows, 256)

    def body(s_ref, o_ref):
        acc = s_ref[0]
        for d in range(1, n):
            acc = acc + s_ref[d]
        o_ref[...] = acc

    return pl.pallas_call(
        body, grid=(rows // tr,), in_specs=[pl.BlockSpec((n, tr, nn), lambda i: (0, i, 0))],
        out_specs=pl.BlockSpec((tr, nn), lambda i: (i, 0)), out_shape=jax.ShapeDtypeStruct((rows, nn), F32),
        name=name, compiler_params=_cparams(("parallel",)))(stack)


def _adam_call(w, g, m, v, name):
    rows, nn = w.shape
    tr = _rows_tile(rows, 128)

    def body(w_ref, g_ref, m_ref, v_ref, d_ref, nm_ref, nv_ref):
        d_ref[...], nm_ref[...], nv_ref[...] = _adam_fn(w_ref[...], g_ref[...], m_ref[...], v_ref[...])

    spec = pl.BlockSpec((tr, nn), lambda i: (i, 0))
    sds = jax.ShapeDtypeStruct((rows, nn), F32)
    return pl.pallas_call(body, grid=(rows // tr,), in_specs=[spec] * 4, out_specs=[spec] * 3,
                          out_shape=[sds] * 3, name=name, compiler_params=_cparams(("parallel",)))(w, g, m, v)


def _adam_fn(w, g, m, v):
    m = ADAM_B1 * m + (1.0 - ADAM_B1) * g
    v = ADAM_B2 * v + (1.0 - ADAM_B2) * (g * g)
    m_hat = m / (1.0 - ADAM_B1 ** ADAM_STEP)
    v_hat = v / (1.0 - ADAM_B2 ** ADAM_STEP)
    delta = -ADAM_LR * (m_hat / (jnp.sqrt(v_hat) + ADAM_EPS) + ADAM_WD * w)
    return delta, m, v


def _mesh_pos():
    return lax.axis_index("x"), lax.axis_index("y"), lax.axis_index("c")


def _other_chips(x, y):
    return [(1 - x, y), (x, 1 - y), (1 - x, 1 - y)]


HBM_SPEC = pl.BlockSpec(memory_space=pl.ANY)


def _remote(src, dst, send_sems, recv_sems, k, to):
    return pltpu.make_async_remote_copy(src_ref=src, dst_ref=dst, send_sem=send_sems.at[k], recv_sem=recv_sems.at[k],
                                        device_id=to, device_id_type=MESH)


def _half_rows(c, rows, align):
    half = rows // 2
    return (pl.ds(pl.multiple_of(c * half, align), half), pl.ds(pl.multiple_of((1 - c) * half, align), half))


def _gather_weights(mats, conv):
    n = len(mats)

    def body(*refs):
        ins, conv_in = refs[:n], refs[n]
        outs, conv_out = refs[n + 1:2 * n + 1], refs[2 * n + 1]
        send_sems, recv_sems, local_sem = refs[2 * n + 2:]
        x, y, c = _mesh_pos()
        me, sibling, s = (x, y, c), (x, y, 1 - c), 2 * x + y
        chips = _other_chips(x, y)
        rows = [_half_rows(c, m.shape[0], 16) for m in mats]
        own = pltpu.make_async_copy(conv_in, conv_out.at[s], local_sem)
        own.start()
        sent = []
        for i in range(n):
            mine = rows[i][0]
            for j, (cx, cy) in enumerate(chips):
                sent.append(_remote(ins[i].at[mine], outs[i].at[s, mine], send_sems, recv_sems, 6 * i + j, (cx, cy, c)))
        for j, (cx, cy) in enumerate(chips):
            sent.append(_remote(conv_in, conv_out.at[s], send_sems, recv_sems, 6 * n + j, (cx, cy, c)))
        for cp in sent:
            cp.start()
        for i in range(n):
            mine = rows[i][0]
            for j, (cx, cy) in enumerate(chips):
                landed = outs[i].at[2 * cx + cy, mine]
                _remote(landed, landed, send_sems, recv_sems, 6 * i + j, me).wait_recv()
                fwd = _remote(landed, landed, send_sems, recv_sems, 6 * i + 3 + j, sibling)
                fwd.start()
                sent.append(fwd)
        for j, (cx, cy) in enumerate(chips):
            slot = conv_out.at[2 * cx + cy]
            _remote(slot, slot, send_sems, recv_sems, 6 * n + j, me).wait_recv()
        for i in range(n):
            theirs_rows = rows[i][1]
            for j, (cx, cy) in enumerate(chips):
                theirs = outs[i].at[2 * cx + cy, theirs_rows]
                _remote(theirs, theirs, send_sems, recv_sems, 6 * i + 3 + j, me).wait_recv()
        for cp in sent:
            cp.wait_send()
        own.wait()

    out_shape = [jax.ShapeDtypeStruct((4,) + m.shape, m.dtype) for m in mats]
    out_shape.append(jax.ShapeDtypeStruct((4,) + conv.shape, conv.dtype))
    res = pl.pallas_call(
        body, in_specs=[HBM_SPEC] * (n + 1), out_specs=[HBM_SPEC] * (n + 1), out_shape=out_shape,
        scratch_shapes=[pltpu.SemaphoreType.DMA((6 * n + 3,)), pltpu.SemaphoreType.DMA((6 * n + 3,)),
                        pltpu.SemaphoreType.DMA],
        name="all_gather_weights")(*mats, conv)
    return res[:n], res[n]


def _sibling_exchange(stacks):
    n = len(stacks)

    def body(*refs):
        ins, outs = refs[:n], refs[n:2 * n]
        send_sems, recv_sems = refs[2 * n:]
        x, y, c = _mesh_pos()
        cps = []
        for i in range(n):
            theirs = _half_rows(c, stacks[i].shape[1], 8)[1]
            cps.append(_remote(ins[i].at[:, theirs, :], outs[i], send_sems, recv_sems, i, (x, y, 1 - c)))
        for cp in cps:
            cp.start()
        for cp in cps:
            cp.wait()

    out_shape = [jax.ShapeDtypeStruct((4, s.shape[1] // 2, s.shape[2]), s.dtype) for s in stacks]
    return pl.pallas_call(
        body, in_specs=[HBM_SPEC] * n, out_specs=[HBM_SPEC] * n, out_shape=out_shape,
        scratch_shapes=[pltpu.SemaphoreType.DMA((n,)), pltpu.SemaphoreType.DMA((n,))],
        name="grad_sibling_exchange")(*stacks)


def _chip_exchange(parts):
    n = len(parts)

    def body(*refs):
        ins, outs = refs[:n], refs[n:2 * n]
        send_sems, recv_sems = refs[2 * n:]
        x, y, c = _mesh_pos()
        me, s = (x, y, c), 2 * x + y
        chips = _other_chips(x, y)
        sent = [_remote(ins[i].at[2 * cx + cy], outs[i].at[s], send_sems, recv_sems, 3 * i + j, (cx, cy, c))
                for i in range(n) for j, (cx, cy) in enumerate(chips)]
        for cp in sent:
            cp.start()
        for i in range(n):
            for j, (cx, cy) in enumerate(chips):
                slot = outs[i].at[2 * cx + cy]
                _remote(slot, slot, send_sems, recv_sems, 3 * i + j, me).wait_recv()
        for cp in sent:
            cp.wait_send()

    return pl.pallas_call(
        body, in_specs=[HBM_SPEC] * n, out_specs=[HBM_SPEC] * n,
        out_shape=[jax.ShapeDtypeStruct(p.shape, p.dtype) for p in parts],
        scratch_shapes=[pltpu.SemaphoreType.DMA((3 * n,)), pltpu.SemaphoreType.DMA((3 * n,))],
        name="grad_chip_exchange")(*parts)


def _sibling_swap(halves):
    n = len(halves)

    def body(*refs):
        ins, outs = refs[:n], refs[n:2 * n]
        send_sems, recv_sems = refs[2 * n:]
        x, y, c = _mesh_pos()
        cps = [_remote(ins[i], outs[i], send_sems, recv_sems, i, (x, y, 1 - c)) for i in range(n)]
        for cp in cps:
            cp.start()
        for cp in cps:
            cp.wait()

    return pl.pallas_call(
        body, in_specs=[HBM_SPEC] * n, out_specs=[HBM_SPEC] * n,
        out_shape=[jax.ShapeDtypeStruct(h.shape, h.dtype) for h in halves],
        scratch_shapes=[pltpu.SemaphoreType.DMA((n,)), pltpu.SemaphoreType.DMA((n,))],
        name="grad_sibling_swap")(*halves)


def _gather_small(vec):
    def body(in_ref, out_ref, send_sems, recv_sems, local_sem):
        x, y, c = _mesh_pos()
        me = (x, y, c)
        own = pltpu.make_async_copy(in_ref, out_ref.at[4 * x + 2 * y + c], local_sem)
        own.start()
        peers = [(1 - x if k & 4 else x, 1 - y if k & 2 else y, 1 - c if k & 1 else c) for k in range(1, 8)]
        sent = [_remote(in_ref, out_ref.at[4 * x + 2 * y + c], send_sems, recv_sems, k, p) for k, p in enumerate(peers)]
        for cp in sent:
            cp.start()
        for k, (px, py, pc) in enumerate(peers):
            slot = out_ref.at[4 * px + 2 * py + pc]
            _remote(slot, slot, send_sems, recv_sems, k, me).wait_recv()
        for cp in sent:
            cp.wait_send()
        own.wait()

    return pl.pallas_call(
        body, in_specs=[HBM_SPEC], out_specs=HBM_SPEC, out_shape=jax.ShapeDtypeStruct((8,) + vec.shape, vec.dtype),
        scratch_shapes=[pltpu.SemaphoreType.DMA((7,)), pltpu.SemaphoreType.DMA((7,)), pltpu.SemaphoreType.DMA],
        name="grad_gather_small")(vec)


def _reduce_matrices(stacks, names):
    cidx = lax.axis_index("c").astype(jnp.int32).reshape(1)
    chip = (2 * lax.axis_index("x") + lax.axis_index("y")).astype(jnp.int32).reshape(1)
    got = _sibling_exchange(stacks)
    pairs = [_pair_sum(a, b, cidx, "grad_pair_sum_" + nm) for a, b, nm in zip(stacks, got, names)]
    quads = _chip_exchange([p[1] for p in pairs])
    mine = [_chip_sum(q, p[0], chip, "grad_chip_sum_" + nm) for q, p, nm in zip(quads, pairs, names)]
    return mine, _sibling_swap(mine)


def _pad_cols(a, n):
    return jnp.concatenate([a, jnp.zeros((a.shape[0], n - a.shape[1]), a.dtype)], axis=1)


def _group_channels(a):
    lead = a.shape[:-1]
    xs = a[..., :D_INNER].reshape(lead + (SSM_GROUPS, GROUP_W))
    bs = a[..., D_INNER:D_INNER + SSM_GROUPS * STATE_N].reshape(lead + (SSM_GROUPS, STATE_N))
    cs = a[..., D_INNER + SSM_GROUPS * STATE_N:].reshape(lead + (SSM_GROUPS, STATE_N))
    return jnp.concatenate([xs, bs, cs], axis=-1).reshape(lead + (CONV_CH,))


PROJ_SEGS = (('gate_a', D_MODEL), ('gate_b', D_MODEL), ('z', D_INNER), ('xbc', CONV_CH), ('q_lat', Q_RANK),
             ('kv_lat', KV_RANK), ('k_rope', LANE), ('dt', LANE))
PROJ_WIDE = sum(w for _, w in PROJ_SEGS[:4])
PROJ_LANE0 = {n: (v if v < PROJ_WIDE else v - PROJ_WIDE) for n, v in
              zip([n for n, _ in PROJ_SEGS], [int(v) for v in np.cumsum([0] + [w for _, w in PROJ_SEGS])[:-1]])}
CONV_LANE0 = PROJ_LANE0['xbc']
KR_LANE0 = PROJ_LANE0['k_rope']


def _lay_w_in(w):
    idx = np.cumsum(IN_SIZES)[:-1]
    q_lat, kv_lat, k_rope, z, xbc, dt, gate_a, gate_b = jnp.split(w, [int(v) for v in idx], axis=1)
    return jnp.concatenate([gate_a, gate_b, z, _group_channels(xbc), q_lat, kv_lat, _pad_cols(k_rope, LANE),
                            _pad_cols(dt, LANE)], axis=1)


@jax.custom_vjp
def project(h, w, tok):
    return _project_impl(h, w)


def _project_impl(h, w):
    return (_mm(h, w[:, :PROJ_WIDE], "w_in_fwd", BF16), _mm(h, w[:, PROJ_WIDE:], "w_in_narrow_fwd")) + tuple(
        jnp.zeros((h.shape[0], wd), BF16) for _, wd in PROJ_SEGS)


def _project_fwd(h, w, tok):
    return _project_impl(h, w), (h, w)


def _project_bwd(res, cots):
    h, w = res
    g = jnp.concatenate(cots[2:], axis=1)
    return _mm(g, w.T, "w_in_dx", h.dtype), jnp.zeros_like(w), _mm_tn(h, g, "w_in_dw")


project.defvjp(_project_fwd, _project_bwd)


def _lay_w_uq(w):
    w3 = w.reshape(Q_RANK, N_HEADS, NOPE + ROPE)
    w3 = jnp.concatenate([w3, jnp.zeros((Q_RANK, N_HEADS, QK_PAD - NOPE - ROPE), w.dtype)], axis=2)
    return w3.reshape(Q_RANK, N_HEADS * QK_PAD)


def _lay_w_ukv(w):
    w3 = w.reshape(KV_RANK, N_HEADS, NOPE + V_DIM)
    return jnp.concatenate([w3[:, :, :NOPE].reshape(KV_RANK, -1), w3[:, :, NOPE:].reshape(KV_RANK, -1)], axis=1)


def _pad_lanes(v, n=LANE):
    return jnp.concatenate([v, jnp.zeros((v.shape[0], n - v.shape[1]), v.dtype)], axis=1)


def _local_loss(toks, small, x, wb, c8, posf, target):
    B, S, D = x.shape
    T = B * S

    def lin(name, a, key, lay=lambda w: w, out_dtype=F32):
        return make_linear(name, out_dtype)(a, lay(wb[key]), lay(toks[key]))

    rows2 = lambda a: a.reshape(T, a.shape[-1])
    rows3 = lambda a: a.reshape(B, S, a.shape[-1])

    sc = make_rowwise("silu_c", _f_silu, 1, 0, 0, ('row',))((c8[None],), (), ())[0][0]
    mod = make_linear("ada", F32, 4)(sc, wb['w_ada'], toks['w_ada'])[:B] + small['b_ada']
    shift1, scale1, gate1, shift2, scale2, gate2 = [m[:, None, :] for m in jnp.split(mod, 6, axis=-1)]

    h, x_res = make_rowwise("modulate1", _f_modulate, 1, 2, 1, ('row',), forward_row=0)(
        (x,), (scale1, shift1), (small['g_pre_mix'],))
    outs = project(rows2(h), _lay_w_in(wb['w_in']), _lay_w_in(toks['w_in']))
    wide = lax.stop_gradient(rows3(outs[0]))
    proj = lax.stop_gradient(rows3(outs[1]))
    stand = {n: rows3(o) for (n, _), o in zip(PROJ_SEGS, outs[2:])}

    def win(seg, block):
        return (PROJ_LANE0[seg] // block, dict(PROJ_SEGS)[seg])

    inv = ROPE_THETA ** (-jnp.arange(ROPE // 2, dtype=F32) / (ROPE // 2))
    inv_lane = jnp.concatenate([inv, inv, jnp.zeros((LANE - ROPE,), F32)])[None]
    tabs = tuple(_rope_tables(posf, inv_lane))
    qn = make_rowwise("rms_q", _f_rms, 1, 0, 1, ('row',), windows={0: win('q_lat', Q_RANK)})(
        (proj,), (), (small['g_q_lat'],), (stand['q_lat'],))[0]
    kvn = make_rowwise("rms_kv", _f_rms, 1, 0, 1, ('row',), windows={0: win('kv_lat', KV_RANK)})(
        (proj,), (), (small['g_kv_lat'],), (stand['kv_lat'],))[0]
    qp = rows3(lin("w_uq", rows2(qn), 'w_uq', _lay_w_uq))
    kvp = rows3(lin("w_ukv", rows2(kvn), 'w_ukv', _lay_w_ukv, BF16))
    qr = rope_q(qp, tabs)
    kr = build_k(kvp, proj, stand['k_rope'], tabs)
    att = attention(qr, kr, kvp)
    attn = rows3(lin("w_o_attn", rows2(att), 'w_o_attn', out_dtype=BF16))

    xa = conv_silu(wide, stand['xbc'], _group_channels(wb['conv_w_f32']), _group_channels(small['conv_b']))
    dt_pad, a_pad = make_rowwise("dt_softplus", _f_dt, 1, 0, 2, ('row', 'row'), windows={0: win('dt', LANE)})(
        (proj,), (), (_pad_lanes(small['dt_bias']), _pad_lanes(small['a_log'])), (stand['dt'],))
    ac_pad = chunk_cumsum(a_pad)
    acr = jnp.transpose(ac_pad[..., :SSM_HEADS], (0, 2, 1))[:, :, None, :]
    dsk = jnp.repeat(small['d_skip'], HEAD_P, axis=-1)
    y = ssd(xa, dt_pad, ac_pad, acr, dsk)
    yg = make_rowwise("gated_norm", _f_gated_norm, 2, 0, 1, ('row',), ncol=SSM_GROUPS, ts_cap=2048,
                      windows={1: win('z', GROUP_W)})((y, wide), (), (small['g_ssm_out'],), (stand['z'],))[0]
    ssm = rows3(lin("w_o_ssm", rows2(yg), 'w_o_ssm', out_dtype=BF16))

    merged = make_rowwise("merge", _f_merge, 4, 0, 0, ('row',),
                          windows={2: win('gate_a', D_MODEL), 3: win('gate_b', D_MODEL)})(
        (attn, ssm, wide, wide), (), (), (stand['gate_a'], stand['gate_b']))[0]
    mix = rows3(lin("w_out", rows2(merged), 'w_out', out_dtype=BF16))
    x1 = make_rowwise("post_mix", _f_post, 2, 1, 1, ('row',))((x_res, mix), (gate1,), (small['g_post_mix'],))[0]

    h2, x1_res = make_rowwise("modulate2", _f_modulate, 1, 2, 1, ('row',), forward_row=0)(
        (x1,), (scale2, shift2), (small['g_pre_mlp'],))
    ff = rows3(ffn(rows2(h2), wb['w_ff1'], toks['w_ff1'], wb['w_ff2'], toks['w_ff2']))
    lvec = make_rowwise("final_loss", _f_final_loss, 3, 1, 1, ('sum',), nodiff=(2,))(
        (x1_res, ff, target), (gate2,), (small['g_post_mlp'],))[0]
    return jnp.sum(lvec)


MATRICES = COL_SHARDED + ROW_SHARDED
STACKED_DW = ('w_ada', 'w_ff1')


def _local_step(x, c, positions, target, wb, small):
    B = x.shape[0]
    c8 = jnp.concatenate([c, jnp.zeros((16 - B, c.shape[1]), F32)], axis=0)
    posf = positions.astype(F32)[..., None]
    toks = {k: jnp.zeros(wb[k].shape, F32) for k in MATRICES if k != 'conv_w'}
    for k in STACKED_DW:
        rows, cols = wb[k].shape
        toks[k] = jnp.zeros((4, rows, cols // 4), F32)
    conv_w = wb['conv_w_f32']

    def loss_fn(toks, small, conv_w, x):
        wbl = dict(wb)
        wbl['conv_w_f32'] = conv_w
        return _local_loss(toks, small, x, wbl, c8, posf, target)

    loss, (g_tok, g_small, g_conv, g_x) = jax.value_and_grad(loss_fn, argnums=(0, 1, 2, 3))(toks, small, conv_w, x)
    grads = dict(g_tok)
    grads.update(g_small)
    grads['conv_w'] = g_conv
    return loss, g_x, grads


def kernel(x, c, positions, w_ada, b_ada, g_pre_mix, g_post_mix, w_in, g_q_lat, g_kv_lat, w_uq, w_ukv, w_o_attn, conv_w, conv_b, dt_bias, a_log, d_skip, g_ssm_out, w_o_ssm, w_out, g_pre_mlp, g_post_mlp, w_ff1, w_ff2, loss_target, m_w_ada, m_b_ada, m_g_pre_mix, m_g_post_mix, m_w_in, m_g_q_lat, m_g_kv_lat, m_w_uq, m_w_ukv, m_w_o_attn, m_conv_w, m_conv_b, m_dt_bias, m_a_log, m_d_skip, m_g_ssm_out, m_w_o_ssm, m_w_out, m_g_pre_mlp, m_g_post_mlp, m_w_ff1, m_w_ff2, v_w_ada, v_b_ada, v_g_pre_mix, v_g_post_mix, v_w_in, v_g_q_lat, v_g_kv_lat, v_w_uq, v_w_ukv, v_w_o_attn, v_conv_w, v_conv_b, v_dt_bias, v_a_log, v_d_skip, v_g_ssm_out, v_w_o_ssm, v_w_out, v_g_pre_mlp, v_g_post_mlp, v_w_ff1, v_w_ff2):
    given = dict(locals())
    w_loc = {n: given[n] for n in WEIGHTS}
    m_loc = {n: given["m_" + n] for n in WEIGHTS}
    v_loc = {n: given["v_" + n] for n in WEIGHTS}
    mats = [n for n in WEIGHTS if n in MATRICES and n != 'conv_w']
    vecs = [n for n in WEIGHTS if n not in MATRICES]

    own = [w_loc[n][0].astype(BF16) for n in mats]
    g_mats, g_conv = _gather_weights(own, conv_w[0])
    chip = 2 * lax.axis_index("x") + lax.axis_index("y")
    wb = {}
    for n, g, mine in zip(mats, g_mats, own):
        g = lax.dynamic_update_slice_in_dim(g, mine[None], chip, axis=0)
        if n in COL_SHARDED:
            wb[n] = jnp.transpose(g, (1, 0, 2)).reshape(g.shape[1], -1)
        else:
            wb[n] = g.reshape(-1, g.shape[2])
    wb['conv_w_f32'] = jnp.transpose(g_conv, (1, 0, 2)).reshape(CONV_K, -1)
    small = {n: w_loc[n] for n in vecs}

    loss_part, grad_x, grads = _local_step(x, c, positions, loss_target, wb, small)
    loss = lax.psum(loss_part, ("x", "y", "c"))

    stacks = []
    for n in mats:
        kk, nn = w_loc[n].shape[1:]
        if n in STACKED_DW:
            stacks.append(grads[n])
        elif n in COL_SHARDED:
            stacks.append(jnp.transpose(grads[n].reshape(kk, 4, nn), (1, 0, 2)))
        else:
            stacks.append(grads[n].reshape(4, kk, nn))
    g_mine, g_other = _reduce_matrices(stacks, mats)
    g_shard = {}

    vec_shapes = [tuple(grads[n].shape) for n in vecs] + [tuple(grads['conv_w'].shape)]
    total = _stack_sum(_gather_small(_pack_small([grads[n] for n in vecs] + [grads['conv_w']])), "grad_sum_small")
    g_vec = _unpack_small(total, vec_shapes)
    n_conv = conv_w.shape[2]
    chip = 2 * lax.axis_index("x") + lax.axis_index("y")
    g_shard['conv_w'] = lax.dynamic_slice_in_dim(g_vec[-1], chip * n_conv, n_conv, axis=1)
    for n, g in zip(vecs, g_vec):
        g_shard[n] = g

    delta, new_m, new_v = {}, {}, {}
    cidx = lax.axis_index("c").astype(jnp.int32).reshape(1)
    for n, mine, other in zip(mats, g_mine, g_other):
        g_shard[n], delta[n], new_m[n], new_v[n] = _adam_halves_call(
            w_loc[n], mine, other, cidx, m_loc[n], v_loc[n], "adamw_" + n)
    rest = vecs + ['conv_w']
    rest_shapes = [tuple(w_loc[n].shape) for n in rest]
    packed = [_pack_small([src[n] for n in rest]) for src in (w_loc, g_shard, m_loc, v_loc)]
    for dst, buf in zip((delta, new_m, new_v), _adam_call(*packed, "adamw_small")):
        dst.update(zip(rest, _unpack_small(buf, rest_shapes)))

    def out(d):
        return [d[n].reshape(w_loc[n].shape) for n in WEIGHTS]

    return (loss, grad_x, *out(g_shard), *out(delta), *out(new_m), *out(new_v))
```

```python
import functools
import math

import numpy as np
import jax
import jax.numpy as jnp
from jax import lax
from jax.experimental import pallas as pl
from jax.experimental.pallas import tpu as pltpu

F32 = jnp.float32
BF16 = jnp.bfloat16
MESH = pl.DeviceIdType.MESH

D_MODEL = 1024
N_HEADS = 8
NOPE = 128
ROPE = 64
V_DIM = 128
Q_RANK = 256
KV_RANK = 256
ROPE_THETA = 10000.0
D_INNER = 2048
SSM_HEADS = 32
SSM_GROUPS = 8
HEAD_P = 64
STATE_N = 128
CONV_K = 4
CHUNK = 128
CONV_CH = D_INNER + 2 * SSM_GROUPS * STATE_N
D_FF = 4096
EPS = 1e-6
IN_SIZES = (Q_RANK, KV_RANK, ROPE, D_INNER, CONV_CH, SSM_HEADS, D_MODEL, D_MODEL)
ADAM_LR, ADAM_B1, ADAM_B2, ADAM_EPS, ADAM_WD, ADAM_STEP = 0.001, 0.9, 0.999, 1e-08, 0.01, 10

VMEM_LIMIT_BYTES = 52 * 1024 * 1024
LANE = 128
QK_PAD = 256

WEIGHTS = ['w_ada', 'b_ada', 'g_pre_mix', 'g_post_mix', 'w_in', 'g_q_lat', 'g_kv_lat', 'w_uq', 'w_ukv',
           'w_o_attn', 'conv_w', 'conv_b', 'dt_bias', 'a_log', 'd_skip', 'g_ssm_out', 'w_o_ssm', 'w_out',
           'g_pre_mlp', 'g_post_mlp', 'w_ff1', 'w_ff2']
COL_SHARDED = ('w_ada', 'w_in', 'w_uq', 'w_ukv', 'conv_w', 'w_ff1')
ROW_SHARDED = ('w_o_attn', 'w_o_ssm', 'w_out', 'w_ff2')


def _cparams(sem):
    return pltpu.CompilerParams(dimension_semantics=sem, vmem_limit_bytes=VMEM_LIMIT_BYTES)


def _tile(n, cap):
    if n <= cap:
        return n
    k = n // LANE
    best = LANE
    for d in range(1, k + 1):
        if k % d == 0 and d * LANE <= cap:
            best = d * LANE
    return best


def _mm(a, w, name, out_dtype=F32, epilogue=None, extras=(), out_dtypes=None):
    M, K = a.shape
    N = w.shape[1]
    tm = min(M, 1024)
    tn = _tile(N, 1024)
    tk = _tile(K, 2048)
    nk = K // tk
    dts = tuple(out_dtypes) if epilogue is not None else (out_dtype,)
    n_x, n_o = len(extras), len(dts)

    def finish(acc, refs):
        res = epilogue(acc, *[r[...] for r in refs[:n_x]]) if epilogue is not None else (acc,)
        for o_ref, val, dt in zip(refs[n_x:n_x + n_o], res, dts):
            o_ref[...] = val.astype(dt)

    def body(a_ref, w_ref, *refs):
        part = jnp.dot(a_ref[...].astype(BF16), w_ref[...], preferred_element_type=F32)
        if nk == 1:
            finish(part, refs)
        else:
            acc_ref = refs[-1]
            k = pl.program_id(2)

            @pl.when(k == 0)
            def _():
                acc_ref[...] = part

            @pl.when(k > 0)
            def _():
                acc_ref[...] += part

            @pl.when(k == nk - 1)
            def _():
                finish(acc_ref[...], refs)

    ospec = pl.BlockSpec((tm, tn), lambda i, j, k: (i, j))
    res = pl.pallas_call(
        body, grid=(M // tm, N // tn, nk),
        in_specs=[pl.BlockSpec((tm, tk), lambda i, j, k: (i, k)), pl.BlockSpec((tk, tn), lambda i, j, k: (k, j))]
        + [ospec] * n_x,
        out_specs=[ospec] * n_o, out_shape=[jax.ShapeDtypeStruct((M, N), dt) for dt in dts],
        scratch_shapes=[pltpu.VMEM((tm, tn), F32)] if nk > 1 else [], name=name,
        compiler_params=_cparams(("parallel", "parallel", "arbitrary")))(a, w, *extras)
    return res if epilogue is not None else res[0]


def _mm_tn(a, g, name, col_shards=1):
    M, K = a.shape
    N = g.shape[1]
    tm = min(M, 1024)
    tk = _tile(K, 1024)
    tn = _tile(N // col_shards, 1024)
    nm = M // tm
    per = N // col_shards // tn

    def body(a_ref, g_ref, o_ref):
        part = lax.dot_general(a_ref[...].astype(BF16), g_ref[...].astype(BF16), (((0,), (0,)), ((), ())),
                               preferred_element_type=F32)
        m = pl.program_id(2)

        @pl.when(m == 0)
        def _():
            o_ref[...] = part.reshape(o_ref.shape)

        @pl.when(m > 0)
        def _():
            o_ref[...] += part.reshape(o_ref.shape)

    if col_shards == 1:
        out_spec = pl.BlockSpec((tk, tn), lambda i, j, m: (i, j))
        out_shape = jax.ShapeDtypeStruct((K, N), F32)
    else:
        out_spec = pl.BlockSpec((1, tk, tn), lambda i, j, m: (j // per, i, j % per))
        out_shape = jax.ShapeDtypeStruct((col_shards, K, N // col_shards), F32)
    return pl.pallas_call(
        body, grid=(K // tk, N // tn, nm),
        in_specs=[pl.BlockSpec((tm, tk), lambda i, j, m: (m, i)), pl.BlockSpec((tm, tn), lambda i, j, m: (m, j))],
        out_specs=out_spec, out_shape=out_shape, name=name,
        compiler_params=_cparams(("parallel", "parallel", "arbitrary")))(a, g)


def make_linear(name, out_dtype=F32, dw_col_shards=1):
    @jax.custom_vjp
    def linear(a, w, tok):
        return _mm(a, w, name + "_fwd", out_dtype)

    def fwd(a, w, tok):
        return _mm(a, w, name + "_fwd", out_dtype), (a, w)

    def bwd(res, g):
        a, w = res
        da = _mm(g, w.T, name + "_dx", a.dtype)
        dw = _mm_tn(a, g, name + "_dw", dw_col_shards)
        return da, jnp.zeros_like(w), dw

    linear.defvjp(fwd, bwd)
    return linear


def _relu2_epilogue(acc):
    r = jnp.maximum(acc, 0.0)
    return r * r, r


def _relu2_bwd_epilogue(acc, r):
    return (acc * (2.0 * r.astype(F32)),)


@jax.custom_vjp
def ffn(h, w1, tok1, w2, tok2):
    act, _ = _mm(h, w1, "w_ff1_fwd", epilogue=_relu2_epilogue, out_dtypes=(BF16, BF16))
    return _mm(act, w2, "w_ff2_fwd", BF16)


def _ffn_fwd(h, w1, tok1, w2, tok2):
    act, r = _mm(h, w1, "w_ff1_fwd", epilogue=_relu2_epilogue, out_dtypes=(BF16, BF16))
    return _mm(act, w2, "w_ff2_fwd", BF16), (h, w1, w2, act, r)


def _ffn_bwd(res, g):
    h, w1, w2, act, r = res
    du = _mm(g, w2.T, "w_ff2_dx", epilogue=_relu2_bwd_epilogue, extras=(r,), out_dtypes=(BF16,))[0]
    dw2 = _mm_tn(act, g, "w_ff2_dw")
    dw1 = _mm_tn(h, du, "w_ff1_dw", 4)
    dh = _mm(du, w1.T, "w_ff1_dx", h.dtype)
    return dh, jnp.zeros_like(w1), dw1, jnp.zeros_like(w2), dw2


ffn.defvjp(_ffn_fwd, _ffn_bwd)


def make_rowwise(name, f, n_rows, n_seqs, n_pars, out_kinds, ncol=1, nodiff=(), ts_cap=512, windows=None,
                 forward_row=None):
    windows = dict(windows or {})
    n_in = n_rows + n_seqs + n_pars
    diff_idx = [i for i in range(n_in) if i not in nodiff]

    def _dims(rows):
        B, S = rows[0].shape[0], rows[0].shape[1]
        ts = min(S, ts_cap)
        return B, S, ts

    def _width(i, r):
        return windows[i][1] if i in windows else r.shape[2]

    def _in_specs(rows, seqs, pars, ts):
        specs = []
        for i, r in enumerate(rows):
            col0 = windows[i][0] if i in windows else 0
            specs.append(pl.BlockSpec((1, ts, _width(i, r) // ncol), lambda k, b, s, col0=col0: (b, s, k + col0)))
        for q in seqs:
            specs.append(pl.BlockSpec((1, 1, q.shape[2] // ncol), lambda k, b, s: (b, 0, k)))
        for p in pars:
            specs.append(pl.BlockSpec((1, p.shape[1] // ncol), lambda k, b, s: (0, k)))
        return specs

    def _load(refs):
        vals = [r[0] for r in refs[:n_rows + n_seqs]]
        vals += [r[...] for r in refs[n_rows + n_seqs:n_in]]
        return vals

    def _out_struct(rows, seqs, pars, ts):
        blocks = [jax.ShapeDtypeStruct((ts, _width(i, r) // ncol), r.dtype) for i, r in enumerate(rows)]
        blocks += [jax.ShapeDtypeStruct((1, q.shape[2] // ncol), q.dtype) for q in seqs]
        blocks += [jax.ShapeDtypeStruct((1, p.shape[1] // ncol), p.dtype) for p in pars]
        return jax.eval_shape(f, *blocks)

    def _fwd_call(rows, seqs, pars):
        B, S, ts = _dims(rows)
        outs = _out_struct(rows, seqs, pars, ts)
        n_out = len(outs)

        def body(*refs):
            res = f(*_load(refs))
            first = (pl.program_id(1) == 0) & (pl.program_id(2) == 0)
            for o_ref, val, kind in zip(refs[n_in:], res, out_kinds):
                if kind == 'row':
                    o_ref[0] = val
                else:
                    tot = jnp.sum(val, axis=0, keepdims=True)

                    @pl.when(first)
                    def _(o_ref=o_ref, tot=tot):
                        o_ref[...] = tot

                    @pl.when(jnp.logical_not(first))
                    def _(o_ref=o_ref, tot=tot):
                        o_ref[...] += tot

        out_shape, out_specs = [], []
        for o, kind in zip(outs, out_kinds):
            d = o.shape[1]
            if kind == 'row':
                out_shape.append(jax.ShapeDtypeStruct((B, S, ncol * d), o.dtype))
                out_specs.append(pl.BlockSpec((1, ts, d), lambda k, b, s: (b, s, k)))
            else:
                out_shape.append(jax.ShapeDtypeStruct((1, ncol * d), o.dtype))
                out_specs.append(pl.BlockSpec((1, d), lambda k, b, s: (0, k)))
        res = pl.pallas_call(
            body, grid=(ncol, B, S // ts), in_specs=_in_specs(rows, seqs, pars, ts), out_specs=out_specs,
            out_shape=out_shape, name=name + "_fwd",
            compiler_params=_cparams(("arbitrary", "arbitrary", "arbitrary")))(*rows, *seqs, *pars)
        return tuple(res)

    def _bwd_call(rows, seqs, pars, cots, carried=None):
        B, S, ts = _dims(rows)
        outs = _out_struct(rows, seqs, pars, ts)
        n_out = len(outs)
        all_in = list(rows) + list(seqs) + list(pars)
        extra = [] if carried is None else [carried]

        def body(*refs):
            vals = _load(refs)
            if carried is not None:
                carried_ref, refs = refs[n_in + n_out], refs[:n_in + n_out] + refs[n_in + n_out + 1:]
            cts = []
            for c_ref, o, kind in zip(refs[n_in:n_in + n_out], outs, out_kinds):
                if kind == 'row':
                    cts.append(c_ref[0])
                else:
                    cts.append(jnp.broadcast_to(c_ref[...], o.shape))

            def g(*dv):
                full = list(vals)
                for i, v in zip(diff_idx, dv):
                    full[i] = v
                return tuple(f(*full))

            _, vjp = jax.vjp(g, *[vals[i] for i in diff_idx])
            grads = vjp(tuple(cts))
            b, s = pl.program_id(1), pl.program_id(2)
            for o_ref, i, gr in zip(refs[n_in + n_out:], diff_idx, grads):
                if i < n_rows:
                    if carried is not None and i == forward_row:
                        gr = gr + carried_ref[0]
                    o_ref[0] = gr.astype(o_ref.dtype)
                else:
                    first = (s == 0) if i < n_rows + n_seqs else ((b == 0) & (s == 0))
                    target = (lambda r: r.at[0]) if i < n_rows + n_seqs else (lambda r: r)

                    @pl.when(first)
                    def _(o_ref=o_ref, gr=gr, target=target):
                        target(o_ref)[...] = gr

                    @pl.when(jnp.logical_not(first))
                    def _(o_ref=o_ref, gr=gr, target=target):
                        target(o_ref)[...] += gr

        cot_specs = []
        for o, kind in zip(outs, out_kinds):
            d = o.shape[1]
            if kind == 'row':
                cot_specs.append(pl.BlockSpec((1, ts, d), lambda k, b, s: (b, s, k)))
            else:
                cot_specs.append(pl.BlockSpec((1, d), lambda k, b, s: (0, k)))
        out_shape, out_specs = [], []
        for i in diff_idx:
            a = all_in[i]
            if i < n_rows:
                out_shape.append(jax.ShapeDtypeStruct((B, S, _width(i, a)), BF16 if i in windows else a.dtype))
                out_specs.append(pl.BlockSpec((1, ts, _width(i, a) // ncol), lambda k, b, s: (b, s, k)))
                continue
            out_shape.append(jax.ShapeDtypeStruct(a.shape, a.dtype))
            if i < n_rows + n_seqs:
                out_specs.append(pl.BlockSpec((1, 1, a.shape[2] // ncol), lambda k, b, s: (b, 0, k)))
            else:
                out_specs.append(pl.BlockSpec((1, a.shape[1] // ncol), lambda k, b, s: (0, k)))
        if carried is not None:
            cot_specs.append(pl.BlockSpec((1, ts, carried.shape[2] // ncol), lambda k, b, s: (b, s, k)))
        res = pl.pallas_call(
            body, grid=(ncol, B, S // ts), in_specs=_in_specs(rows, seqs, pars, ts) + cot_specs,
            out_specs=out_specs, out_shape=out_shape, name=name + "_bwd",
            compiler_params=_cparams(("arbitrary", "arbitrary", "arbitrary")))(*all_in, *cots, *extra)
        grads = [None] * n_in
        for i, r in zip(diff_idx, res):
            grads[i] = r
        for i in nodiff:
            grads[i] = jnp.zeros_like(all_in[i])
        stand_in_grads = tuple(grads[i] for i in sorted(windows))
        for i in windows:
            grads[i] = jnp.zeros_like(all_in[i])
        return (tuple(grads[:n_rows]), tuple(grads[n_rows:n_rows + n_seqs]), tuple(grads[n_rows + n_seqs:]),
                stand_in_grads)

    def _outputs(rows, seqs, pars):
        res = _fwd_call(rows, seqs, pars)
        return res if forward_row is None else res + (rows[forward_row],)

    @jax.custom_vjp
    def op(rows, seqs, pars, stand_ins):
        return _outputs(rows, seqs, pars)

    def fwd(rows, seqs, pars, stand_ins):
        return _outputs(rows, seqs, pars), (rows, seqs, pars)

    def bwd(res, cots):
        rows, seqs, pars = res
        if forward_row is None:
            return _bwd_call(rows, seqs, pars, cots)
        return _bwd_call(rows, seqs, pars, cots[:-1], cots[-1])

    op.defvjp(fwd, bwd)
    return lambda rows, seqs, pars, stand_ins=(): op(tuple(rows), tuple(seqs), tuple(pars), tuple(stand_ins))


def _rms(x, g):
    x = x.astype(F32)
    return x * lax.rsqrt(jnp.mean(x * x, axis=-1, keepdims=True) + EPS) * g


def _silu(x):
    return x * lax.logistic(x)


def _f_silu(c):
    return (_silu(c),)


def _f_modulate(x, scale, shift, g):
    return ((_rms(x, g) * (1.0 + scale) + shift).astype(BF16),)


def _f_rms(x, g):
    return (_rms(x, g).astype(BF16),)


def _f_dt(dt_raw, dt_bias, a_log):
    z = dt_raw + dt_bias
    dt = jnp.maximum(z, 0.0) + jnp.log1p(jnp.exp(-jnp.abs(z)))
    return dt, dt * (-jnp.exp(a_log))


def _f_gated_norm(y, z, g):
    return (_rms(y * _silu(z.astype(F32)), g).astype(BF16),)


def _f_merge(attn, ssm, ga, gb):
    return ((lax.logistic(ga.astype(F32)) * attn + lax.logistic(gb.astype(F32)) * ssm).astype(BF16),)


def _f_post(x, m, gate, g):
    return (x + gate * _rms(m, g),)


def _f_final_loss(x, ff, target, gate, g):
    e = x + gate * _rms(ff, g) - target
    return (e * e * (0.5 / D_MODEL),)


def _rope_tables(posf, inv_lane):
    B, S, _ = posf.shape
    ts = min(S, 512)

    def body(p_ref, inv_ref, c_ref, a_ref, b_ref):
        ang = p_ref[0] * inv_ref[...]
        cs, sn = jnp.cos(ang), jnp.sin(ang)
        lane = lax.broadcasted_iota(jnp.int32, ang.shape, 1)
        c_ref[0] = jnp.where(lane < ROPE, cs, 0.0)
        a_ref[0] = jnp.where(lane < ROPE // 2, -sn, 0.0)
        b_ref[0] = jnp.where((lane >= ROPE // 2) & (lane < ROPE), sn, 0.0)

    spec = pl.BlockSpec((1, ts, LANE), lambda b, s: (b, s, 0))
    sds = jax.ShapeDtypeStruct((B, S, LANE), F32)
    return pl.pallas_call(
        body, grid=(B, S // ts),
        in_specs=[pl.BlockSpec((1, ts, 1), lambda b, s: (b, s, 0)), pl.BlockSpec((1, LANE), lambda b, s: (0, 0))],
        out_specs=[spec, spec, spec], out_shape=[sds, sds, sds], name="rope_tables",
        compiler_params=_cparams(("parallel", "parallel")))(posf, inv_lane)


def _rot(u, c, a, bm):
    return u * c + pltpu.roll(u, 96, 1) * a + pltpu.roll(u, 32, 1) * bm


def _rot_t(g, c, a, bm):
    return g * c + pltpu.roll(g * a, 32, 1) + pltpu.roll(g * bm, 96, 1)


def _rope_q_call(q, tabs, transpose, name):
    B, S, W = q.shape
    ts = min(S, 512)
    fn = _rot_t if transpose else _rot
    out_dtype = BF16

    def body(q_ref, c_ref, a_ref, b_ref, o_ref):
        tc, ta, tb = c_ref[0], a_ref[0], b_ref[0]
        for h in range(W // QK_PAD):
            u = q_ref[0, :, h * QK_PAD:(h + 1) * QK_PAD].astype(F32) * ATT_SCALE
            r = fn(u[:, NOPE:], tc, ta, tb)
            o_ref[0, :, h * QK_PAD:(h + 1) * QK_PAD] = jnp.concatenate([u[:, :NOPE], r], axis=1).astype(out_dtype)

    tspec = pl.BlockSpec((1, ts, LANE), lambda b, s: (b, s, 0))
    qspec = pl.BlockSpec((1, ts, W), lambda b, s: (b, s, 0))
    return pl.pallas_call(
        body, grid=(B, S // ts), in_specs=[qspec, tspec, tspec, tspec], out_specs=qspec,
        out_shape=jax.ShapeDtypeStruct(q.shape, out_dtype), name=name,
        compiler_params=_cparams(("parallel", "parallel")))(q, *tabs)


@jax.custom_vjp
def rope_q(q, tabs):
    return _rope_q_call(q, tabs, False, "rope_q_fwd")


def _rope_q_fwd(q, tabs):
    return _rope_q_call(q, tabs, False, "rope_q_fwd"), tabs


def _rope_q_bwd(tabs, g):
    return _rope_q_call(g, tabs, True, "rope_q_bwd"), tuple(jnp.zeros_like(t) for t in tabs)


rope_q.defvjp(_rope_q_fwd, _rope_q_bwd)


def _build_k_fwd_call(kv, kr, tabs):
    B, S, _ = kv.shape
    ts = min(S, 512)

    def body(kv_ref, kr_ref, c_ref, a_ref, b_ref, o_ref):
        r = _rot(kr_ref[0], c_ref[0], a_ref[0], b_ref[0]).astype(BF16)
        for h in range(N_HEADS):
            o_ref[0, :, h * QK_PAD:(h + 1) * QK_PAD] = jnp.concatenate(
                [kv_ref[0, :, h * NOPE:(h + 1) * NOPE], r], axis=1)

    tspec = pl.BlockSpec((1, ts, LANE), lambda b, s: (b, s, 0))
    kr_spec = pl.BlockSpec((1, ts, LANE), lambda b, s: (b, s, KR_LANE0 // LANE))
    return pl.pallas_call(
        body, grid=(B, S // ts),
        in_specs=[pl.BlockSpec((1, ts, N_HEADS * NOPE), lambda b, s: (b, s, 0)), kr_spec, tspec, tspec, tspec],
        out_specs=pl.BlockSpec((1, ts, N_HEADS * QK_PAD), lambda b, s: (b, s, 0)),
        out_shape=jax.ShapeDtypeStruct((B, S, N_HEADS * QK_PAD), BF16), name="build_k_fwd",
        compiler_params=_cparams(("parallel", "parallel")))(kv, kr, *tabs)


def _build_k_bwd_call(g, tabs):
    B, S, _ = g.shape
    ts = min(S, 512)

    def body(g_ref, c_ref, a_ref, b_ref, dk_ref, dr_ref):
        tot = None
        for h in range(N_HEADS):
            dk_ref[0, :, h * NOPE:(h + 1) * NOPE] = g_ref[0, :, h * QK_PAD:h * QK_PAD + NOPE]
            part = g_ref[0, :, h * QK_PAD + NOPE:(h + 1) * QK_PAD].astype(F32)
            tot = part if tot is None else tot + part
        dr_ref[0] = _rot_t(tot, c_ref[0], a_ref[0], b_ref[0]).astype(BF16)

    tspec = pl.BlockSpec((1, ts, LANE), lambda b, s: (b, s, 0))
    return pl.pallas_call(
        body, grid=(B, S // ts),
        in_specs=[pl.BlockSpec((1, ts, N_HEADS * QK_PAD), lambda b, s: (b, s, 0)), tspec, tspec, tspec],
        out_specs=[pl.BlockSpec((1, ts, N_HEADS * NOPE), lambda b, s: (b, s, 0)), tspec],
        out_shape=[jax.ShapeDtypeStruct((B, S, N_HEADS * NOPE), BF16), jax.ShapeDtypeStruct((B, S, LANE), BF16)],
        name="build_k_bwd", compiler_params=_cparams(("parallel", "parallel")))(g, *tabs)


@jax.custom_vjp
def build_k(kv, src, stand_in, tabs):
    return _build_k_fwd_call(kv, src, tabs)


def _build_k_fwd(kv, src, stand_in, tabs):
    return _build_k_fwd_call(kv, src, tabs), (tabs, kv.shape, src)


def _build_k_bwd(res, g):
    tabs, kv_shape, src = res
    dk, dr = _build_k_bwd_call(g, tabs)
    dkv = jnp.concatenate([dk, jnp.zeros((kv_shape[0], kv_shape[1], kv_shape[2] - dk.shape[2]), BF16)], axis=-1)
    return dkv, jnp.zeros_like(src), dr, tuple(jnp.zeros_like(t) for t in tabs)


build_k.defvjp(_build_k_fwd, _build_k_bwd)


ATT_SCALE = (NOPE + ROPE) ** -0.5
NEG = -1e30


def _att_tiles(S):
    t = min(S, 512)
    return t, S // t


def _scores(q, k, diagonal):
    s = lax.dot_general(q, k, (((1,), (1,)), ((), ())), preferred_element_type=F32)
    if diagonal:
        row = lax.broadcasted_iota(jnp.int32, s.shape, 0)
        col = lax.broadcasted_iota(jnp.int32, s.shape, 1)
        s = jnp.where(col <= row, s, NEG)
    return s


ATT_HB = 4


def _causal_pairs(n):
    pairs = [(i, j) for i in range(n) for j in range(i + 1)]
    return (jnp.asarray([p[0] for p in pairs], jnp.int32), jnp.asarray([p[1] for p in pairs], jnp.int32))


def _head(ref_or_val, h, w):
    return ref_or_val[:, h * w:(h + 1) * w]


def _attn_fwd_call(q, k, vsrc, v_blk0):
    B, S, _ = q.shape
    t, n = _att_tiles(S)
    qi, kj = _causal_pairs(n)

    def body(qi_ref, kj_ref, q_ref, k_ref, v_ref, o_ref, lse_ref, m_sc, l_sc, acc_sc):
        p_id = pl.program_id(2)
        i, j = qi_ref[p_id], kj_ref[p_id]

        @pl.when(j == 0)
        def _():
            m_sc[...] = jnp.full(m_sc.shape, NEG, F32)
            l_sc[...] = jnp.zeros(l_sc.shape, F32)
            acc_sc[...] = jnp.zeros(acc_sc.shape, F32)

        def step(diagonal):
            qa, ka, va = q_ref[0], k_ref[0], v_ref[0]
            for h in range(ATT_HB):
                lanes = slice(h * LANE, (h + 1) * LANE)
                s = _scores(_head(qa, h, QK_PAD), _head(ka, h, QK_PAD), diagonal)
                m_prev = m_sc[:, lanes]
                m_new = jnp.maximum(m_prev, jnp.max(s, axis=1, keepdims=True))
                alpha = jnp.exp(m_prev - m_new)
                p = jnp.exp(s - jnp.tile(m_new, (1, t // LANE)))
                l_sc[:, lanes] = alpha * l_sc[:, lanes] + jnp.sum(p, axis=1, keepdims=True)
                acc_sc[:, lanes] = alpha * acc_sc[:, lanes] + jnp.dot(p.astype(BF16), _head(va, h, V_DIM),
                                                                      preferred_element_type=F32)
                m_sc[:, lanes] = m_new

        @pl.when(j < i)
        def _():
            step(False)

        @pl.when(j == i)
        def _():
            step(True)
            o_ref[0] = (acc_sc[...] / l_sc[...]).astype(BF16)
            lse_ref[0] = m_sc[...] + jnp.log(l_sc[...])

    wq, wv = ATT_HB * QK_PAD, ATT_HB * V_DIM
    grid_spec = pltpu.PrefetchScalarGridSpec(
        num_scalar_prefetch=2, grid=(B, N_HEADS // ATT_HB, qi.shape[0]),
        in_specs=[pl.BlockSpec((1, t, wq), lambda b, h, p, qi, kj: (b, qi[p], h)),
                  pl.BlockSpec((1, t, wq), lambda b, h, p, qi, kj: (b, kj[p], h)),
                  pl.BlockSpec((1, t, wv), lambda b, h, p, qi, kj: (b, kj[p], v_blk0 + h))],
        out_specs=[pl.BlockSpec((1, t, wv), lambda b, h, p, qi, kj: (b, qi[p], h)),
                   pl.BlockSpec((1, t, wv), lambda b, h, p, qi, kj: (b, qi[p], h))],
        scratch_shapes=[pltpu.VMEM((t, wv), F32), pltpu.VMEM((t, wv), F32), pltpu.VMEM((t, wv), F32)])
    return pl.pallas_call(
        body, grid_spec=grid_spec,
        out_shape=[jax.ShapeDtypeStruct((B, S, N_HEADS * V_DIM), BF16),
                   jax.ShapeDtypeStruct((B, S, N_HEADS * LANE), F32)],
        name="attn_fwd", compiler_params=_cparams(("parallel", "parallel", "arbitrary")))(qi, kj, q, k, vsrc)


def _attn_p_ds(q, k, v, o, do, lse, diagonal, t):
    s = _scores(q, k, diagonal)
    p = jnp.exp(s - jnp.tile(lse, (1, t // LANE)))
    dp = lax.dot_general(do.astype(BF16), v, (((1,), (1,)), ((), ())), preferred_element_type=F32)
    delta = jnp.sum(do.astype(F32) * o.astype(F32), axis=1, keepdims=True)
    ds = p * (dp - delta)
    return p, ds


ATT_HB_BWD = 2


def _attn_bwd_call(q, k, vsrc, o, do, lse):
    B, S, _ = q.shape
    t, n = _att_tiles(S)
    qi, kj = _causal_pairs(n)
    n_pairs = qi.shape[0]
    hb = ATT_HB_BWD
    v_blk0 = N_HEADS // hb

    def body(qi_ref, kj_ref, q_ref, k_ref, v_ref, o_ref, do_ref, lse_ref, dq_ref, dk_ref, dv_ref, dq_sc, dk_sc, dv_sc):
        p_id = pl.program_id(2)
        i, j = qi_ref[p_id], kj_ref[p_id]

        @pl.when(p_id == 0)
        def _():
            dk_sc[...] = jnp.zeros(dk_sc.shape, F32)
            dv_sc[...] = jnp.zeros(dv_sc.shape, F32)

        @pl.when(j == 0)
        def _():
            dq_sc[...] = jnp.zeros(dq_sc.shape, F32)

        rows = pl.ds(pl.multiple_of(j * t, t), t)

        def step(diagonal):
            qa, ka, va, oa, doa, la = q_ref[0], k_ref[0], v_ref[0], o_ref[0], do_ref[0], lse_ref[0]
            for h in range(hb):
                qb, kb, dob = _head(qa, h, QK_PAD), _head(ka, h, QK_PAD), _head(doa, h, V_DIM)
                p, ds = _attn_p_ds(qb, kb, _head(va, h, V_DIM), _head(oa, h, V_DIM), dob, _head(la, h, LANE),
                                   diagonal, t)
                dsb = ds.astype(BF16)
                dq_sc[:, h * QK_PAD:(h + 1) * QK_PAD] += jnp.dot(dsb, kb, preferred_element_type=F32)
                dv_sc[rows, h * V_DIM:(h + 1) * V_DIM] += lax.dot_general(
                    p.astype(BF16), dob.astype(BF16), (((0,), (0,)), ((), ())), preferred_element_type=F32)
                dk_sc[rows, h * QK_PAD:(h + 1) * QK_PAD] += lax.dot_general(
                    dsb, qb, (((0,), (0,)), ((), ())), preferred_element_type=F32)

        @pl.when(j < i)
        def _():
            step(False)

        @pl.when(j == i)
        def _():
            step(True)
            dq_ref[0] = dq_sc[...].astype(BF16)

        @pl.when(p_id == n_pairs - 1)
        def _():
            dk_ref[0] = dk_sc[...].astype(BF16)
            dv_ref[0] = dv_sc[...].astype(BF16)

    wq, wv = hb * QK_PAD, hb * V_DIM
    at_q = lambda b, h, p, qi, kj: (b, qi[p], h)
    at_k = lambda b, h, p, qi, kj: (b, kj[p], h)
    whole = lambda b, h, p, qi, kj: (b, 0, h)
    grid_spec = pltpu.PrefetchScalarGridSpec(
        num_scalar_prefetch=2, grid=(B, N_HEADS // hb, n_pairs),
        in_specs=[pl.BlockSpec((1, t, wq), at_q), pl.BlockSpec((1, t, wq), at_k),
                  pl.BlockSpec((1, t, wv), lambda b, h, p, qi, kj: (b, kj[p], v_blk0 + h)),
                  pl.BlockSpec((1, t, wv), at_q), pl.BlockSpec((1, t, wv), at_q), pl.BlockSpec((1, t, wv), at_q)],
        out_specs=[pl.BlockSpec((1, t, wq), at_q), pl.BlockSpec((1, S, wq), whole), pl.BlockSpec((1, S, wv), whole)],
        scratch_shapes=[pltpu.VMEM((t, wq), F32), pltpu.VMEM((S, wq), F32), pltpu.VMEM((S, wv), F32)])
    return pl.pallas_call(
        body, grid_spec=grid_spec,
        out_shape=[jax.ShapeDtypeStruct((B, S, N_HEADS * QK_PAD), BF16),
                   jax.ShapeDtypeStruct((B, S, N_HEADS * QK_PAD), BF16),
                   jax.ShapeDtypeStruct((B, S, N_HEADS * V_DIM), BF16)],
        name="attn_bwd", compiler_params=_cparams(("parallel", "parallel", "arbitrary")))(
            qi, kj, q, k, vsrc, o, do, lse)


@jax.custom_vjp
def attention(q, k, kv):
    return _attn_fwd_call(q, k, kv, N_HEADS // ATT_HB)[0]


def _attention_fwd(q, k, kv):
    o, lse = _attn_fwd_call(q, k, kv, N_HEADS // ATT_HB)
    return o, (q, k, kv, o, lse)


def _attention_bwd(res, do):
    q, k, kv, o, lse = res
    dq, dk, dv = _attn_bwd_call(q, k, kv, o, do, lse)
    dkv = jnp.concatenate([jnp.zeros_like(dv), dv], axis=-1)
    return dq, dk, dkv


attention.defvjp(_attention_fwd, _attention_bwd)


SUBLANES = 8


def _zero_tail(v):
    return jnp.concatenate([v, jnp.zeros((SUBLANES, v.shape[1]), v.dtype)], axis=0)


def _shift_down(vz, sh):
    return pltpu.roll(vz, sh, 0)[:vz.shape[0] - SUBLANES]


def _shift_up(vz, sh):
    return pltpu.roll(vz, vz.shape[0] - sh, 0)[:vz.shape[0] - SUBLANES]


def _conv_pre(u, uz, w_ref, b_ref):
    acc = b_ref[...] + w_ref[pl.ds(CONV_K - 1, 1), :] * u
    for k in range(CONV_K - 1):
        acc = acc + w_ref[pl.ds(k, 1), :] * _shift_down(uz, CONV_K - 1 - k)
    return acc


def _conv_fwd_call(src, w, b):
    B, S, _ = src.shape
    C = w.shape[1]

    def body(u_ref, w_ref, b_ref, o_ref):
        uu = u_ref[0].astype(F32)
        o_ref[0] = _silu(_conv_pre(uu, _zero_tail(uu), w_ref, b_ref))

    spec = pl.BlockSpec((1, S, LANE), lambda c, bb: (bb, 0, c))
    return pl.pallas_call(
        body, grid=(C // LANE, B),
        in_specs=[pl.BlockSpec((1, S, LANE), lambda c, bb: (bb, 0, c + CONV_LANE0 // LANE)),
                  pl.BlockSpec((CONV_K, LANE), lambda c, bb: (0, c)), pl.BlockSpec((1, LANE), lambda c, bb: (0, c))],
        out_specs=spec, out_shape=jax.ShapeDtypeStruct((B, S, C), F32), name="conv_fwd",
        compiler_params=_cparams(("parallel", "arbitrary")))(src, w, b)


def _conv_bwd_call(src, w, b, g):
    B, S, _ = src.shape
    C = w.shape[1]

    def body(u_ref, w_ref, b_ref, g_ref, du_ref, dw_ref, db_ref):
        uu = u_ref[0].astype(F32)
        uz = _zero_tail(uu)
        pre = _conv_pre(uu, uz, w_ref, b_ref)
        sg = lax.logistic(pre)
        dpre = g_ref[0] * sg * (1.0 + pre * (1.0 - sg))
        dz = _zero_tail(dpre)
        du = w_ref[pl.ds(CONV_K - 1, 1), :] * dpre
        dws = [None] * CONV_K
        dws[CONV_K - 1] = jnp.sum(dpre * uu, axis=0, keepdims=True)
        for k in range(CONV_K - 1):
            sh = CONV_K - 1 - k
            du = du + w_ref[pl.ds(k, 1), :] * _shift_up(dz, sh)
            dws[k] = jnp.sum(dpre * _shift_down(uz, sh), axis=0, keepdims=True)
        du_ref[0] = du.astype(du_ref.dtype)
        dbv = jnp.sum(dpre, axis=0, keepdims=True)
        first = pl.program_id(1) == 0

        @pl.when(first)
        def _():
            for k in range(CONV_K):
                dw_ref[pl.ds(k, 1), :] = dws[k]
            db_ref[...] = dbv

        @pl.when(jnp.logical_not(first))
        def _():
            for k in range(CONV_K):
                dw_ref[pl.ds(k, 1), :] += dws[k]
            db_ref[...] += dbv

    spec = pl.BlockSpec((1, S, LANE), lambda c, bb: (bb, 0, c))
    wspec = pl.BlockSpec((CONV_K, LANE), lambda c, bb: (0, c))
    bspec = pl.BlockSpec((1, LANE), lambda c, bb: (0, c))
    uspec = pl.BlockSpec((1, S, LANE), lambda c, bb: (bb, 0, c + CONV_LANE0 // LANE))
    return pl.pallas_call(
        body, grid=(C // LANE, B), in_specs=[uspec, wspec, bspec, spec], out_specs=[spec, wspec, bspec],
        out_shape=[jax.ShapeDtypeStruct((B, S, C), BF16), jax.ShapeDtypeStruct(w.shape, F32),
                   jax.ShapeDtypeStruct(b.shape, F32)],
        name="conv_bwd", compiler_params=_cparams(("parallel", "arbitrary")))(src, w, b, g)


@jax.custom_vjp
def conv_silu(src, stand_in, w, b):
    return _conv_fwd_call(src, w, b)


def _conv_silu_fwd(src, stand_in, w, b):
    return _conv_fwd_call(src, w, b), (src, w, b)


def _conv_silu_bwd(res, g):
    du, dw, db = _conv_bwd_call(*res, g)
    return jnp.zeros_like(res[0]), du, dw, db


conv_silu.defvjp(_conv_silu_fwd, _conv_silu_bwd)


def _chunk_cumsum_call(a, reverse, name):
    B, S, W = a.shape
    per_step = min(S // CHUNK, 8)

    def body(a_ref, o_ref):
        r = lax.broadcasted_iota(jnp.int32, (CHUNK, CHUNK), 0)
        c = lax.broadcasted_iota(jnp.int32, (CHUNK, CHUNK), 1)
        tri = jnp.where((c >= r) if reverse else (c <= r), 1.0, 0.0).astype(F32)
        for i in range(per_step):
            rows = pl.ds(i * CHUNK, CHUNK)
            o_ref[0, rows, :] = jnp.dot(tri, a_ref[0, rows, :], preferred_element_type=F32,
                                        precision=lax.Precision.HIGHEST)

    spec = pl.BlockSpec((1, per_step * CHUNK, W), lambda b, c: (b, c, 0))
    return pl.pallas_call(body, grid=(B, S // (per_step * CHUNK)), in_specs=[spec], out_specs=spec,
                          out_shape=jax.ShapeDtypeStruct(a.shape, F32), name=name,
                          compiler_params=_cparams(("parallel", "parallel")))(a)


@jax.custom_vjp
def chunk_cumsum(a):
    return _chunk_cumsum_call(a, False, "chunk_cumsum_fwd")


chunk_cumsum.defvjp(lambda a: (_chunk_cumsum_call(a, False, "chunk_cumsum_fwd"), None),
                    lambda _, g: (_chunk_cumsum_call(g, True, "chunk_cumsum_bwd"),))


GROUP_W = 4 * HEAD_P
HPG = SSM_HEADS // SSM_GROUPS


def _ssd_masks():
    lane = lax.broadcasted_iota(jnp.int32, (1, GROUP_W), 1)
    return [((lane >= HEAD_P * j) & (lane < HEAD_P * (j + 1))).astype(F32) for j in range(HPG)]


def _ssd_decays(ac_cols, acr_ref, gi):
    r = lax.broadcasted_iota(jnp.int32, (CHUNK, CHUNK), 0)
    c = lax.broadcasted_iota(jnp.int32, (CHUNK, CHUNK), 1)
    return [jnp.exp(jnp.where(c <= r, ac_cols[j] - acr_ref[0, gi * HPG + j], NEG)) for j in range(HPG)]


def _ssd_cols(blk, g):
    lane = lax.broadcasted_iota(jnp.int32, blk.shape, 1)
    return [jnp.sum(jnp.where(lane == HPG * g + j, blk, 0.0), axis=1, keepdims=True) for j in range(HPG)]


def _ssd_spread(cols):
    lane = lax.broadcasted_iota(jnp.int32, (1, GROUP_W), 1)
    out = jnp.broadcast_to(cols[HPG - 1], (CHUNK, GROUP_W))
    for j in range(HPG - 2, -1, -1):
        out = jnp.where(lane < HEAD_P * (j + 1), cols[j], out)
    return out


def _ssd_gather(val, cols, masks, g):
    lane = lax.broadcasted_iota(jnp.int32, (1, LANE), 1)
    out = jnp.zeros((CHUNK, LANE), F32)
    for j in range(HPG):
        tot = jnp.sum(val * masks[j], axis=1, keepdims=True)
        if cols is not None:
            tot = tot + cols[j]
        out = out + tot * (lane == HPG * g + j).astype(F32)
    return out


def _dot(a, b, dims):
    return lax.dot_general(a.astype(BF16), b.astype(BF16), (dims, ((), ())), preferred_element_type=F32)


NN = ((1,), (0,))
NT = ((1,), (1,))
TN = ((0,), (0,))


XBC_W = GROUP_W + 2 * STATE_N


SSD_STEP_GROUPS_FWD = 4
SSD_STEP_GROUPS_BWD = 2


def _ssd_load(xbc_ref, dt_ref, ac_ref, masks, g, gi):
    x = xbc_ref[0, :, gi * XBC_W:gi * XBC_W + GROUP_W]
    bm = xbc_ref[0, :, gi * XBC_W + GROUP_W:gi * XBC_W + GROUP_W + STATE_N]
    cm = xbc_ref[0, :, gi * XBC_W + GROUP_W + STATE_N:(gi + 1) * XBC_W]
    ac_cols = _ssd_cols(ac_ref[0], g)
    dt = _ssd_spread(_ssd_cols(dt_ref[0], g))
    ac = _ssd_spread(ac_cols)
    is_last = (lax.broadcasted_iota(jnp.int32, (CHUNK, GROUP_W), 0) == CHUNK - 1).astype(F32)
    return x, bm, cm, dt, ac, ac_cols, is_last


def _ssd_in_specs(nc, rev, gb):
    cc = (lambda c: nc - 1 - c) if rev else (lambda c: c)
    return [pl.BlockSpec((1, CHUNK, gb * XBC_W), lambda b, g, c: (b, cc(c), g)),
            pl.BlockSpec((1, CHUNK, LANE), lambda b, g, c: (b, cc(c), 0)),
            pl.BlockSpec((1, CHUNK, LANE), lambda b, g, c: (b, cc(c), 0)),
            pl.BlockSpec((1, gb * HPG, 1, CHUNK), lambda b, g, c: (b, g, 0, cc(c))),
            pl.BlockSpec((1, gb * GROUP_W), lambda b, g, c: (0, g))]


def _ssd_fwd_call(xbc, dtp, acp, acr, dsk):
    B, S, _ = xbc.shape
    nc = S // CHUNK
    gb = SSD_STEP_GROUPS_FWD

    def body(xbc_ref, dt_ref, ac_ref, ar_ref, ds_ref, y_ref, hp_ref, h_sc):
        @pl.when(pl.program_id(2) == 0)
        def _():
            h_sc[...] = jnp.zeros(h_sc.shape, F32)

        masks = _ssd_masks()
        ys = []
        for gi in range(gb):
            grp = gb * pl.program_id(1) + gi
            x, bm, cm, dt, ac, ac_cols, is_last = _ssd_load(xbc_ref, dt_ref, ac_ref, masks, grp, gi)
            last = jnp.sum(ac * is_last, axis=0, keepdims=True)
            decays = _ssd_decays(ac_cols, ar_ref, gi)
            xd = x * dt
            cb = _dot(cm, bm, NT)
            hprev = h_sc[gi]
            hp_ref[0, gi, 0] = hprev
            y = _dot(cm, hprev, NN) * jnp.exp(ac) + ds_ref[:, gi * GROUP_W:(gi + 1) * GROUP_W] * x
            for j in range(HPG):
                y = y + _dot(cb * decays[j], xd * masks[j], NN)
            ys.append(y)
            h_sc[gi] = hprev * jnp.exp(last) + _dot(bm, xd * jnp.exp(last - ac), TN)
        y_ref[0] = jnp.concatenate(ys, axis=1)

    ng = SSM_GROUPS // gb
    return pl.pallas_call(
        body, grid=(B, ng, nc), in_specs=_ssd_in_specs(nc, False, gb),
        out_specs=[pl.BlockSpec((1, CHUNK, gb * GROUP_W), lambda b, g, c: (b, c, g)),
                   pl.BlockSpec((1, gb, 1, STATE_N, GROUP_W), lambda b, g, c: (b, g, c, 0, 0))],
        out_shape=[jax.ShapeDtypeStruct((B, S, D_INNER), F32),
                   jax.ShapeDtypeStruct((B, SSM_GROUPS, nc, STATE_N, GROUP_W), F32)],
        scratch_shapes=[pltpu.VMEM((gb, STATE_N, GROUP_W), F32)], name="ssd_fwd",
        compiler_params=_cparams(("parallel", "parallel", "arbitrary")))(xbc, dtp, acp, acr, dsk)


def _ssd_bwd_call(xbc, dtp, acp, acr, dsk, hps, dy):
    B, S, _ = xbc.shape
    nc = S // CHUNK
    gb = SSD_STEP_GROUPS_BWD

    def body(xbc_ref, dt_ref, ac_ref, ar_ref, ds_ref, hp_ref, dy_ref,
             dxbc_ref, ddt_ref, dac_ref, dar_ref, dds_ref, dh_sc):
        first = pl.program_id(2) == 0

        @pl.when(first)
        def _():
            dh_sc[...] = jnp.zeros(dh_sc.shape, F32)

        masks = _ssd_masks()
        dxbc_parts, dds_parts = [], []
        for gi in range(gb):
            grp = gb * pl.program_id(0) + gi
            x, bm, cm, dt, ac, ac_cols, is_last = _ssd_load(xbc_ref, dt_ref, ac_ref, masks, grp, gi)
            last = jnp.sum(ac * is_last, axis=0, keepdims=True)
            g = dy_ref[0, :, gi * GROUP_W:(gi + 1) * GROUP_W]
            hprev = hp_ref[0, gi, 0]
            dh = dh_sc[gi]
            decays = _ssd_decays(ac_cols, ar_ref, gi)
            dcols = []
            xd = x * dt
            cb = _dot(cm, bm, NT)
            e_c = jnp.exp(ac)
            e_end = jnp.exp(last - ac)
            e_last = jnp.exp(last)
            z = _dot(cm, hprev, NN)
            dz = g * e_c
            dac = g * z * e_c
            dc = _dot(dz, hprev, NT)
            dhprev = _dot(cm, dz, TN) + dh * e_last
            dcb = jnp.zeros((CHUNK, CHUNK), F32)
            dxd = jnp.zeros(xd.shape, F32)
            for j in range(HPG):
                gj = cb * decays[j]
                dgj = _dot(g * masks[j], xd, NT)
                dxd = dxd + _dot(gj, g, TN) * masks[j]
                dcb = dcb + dgj * decays[j]
                dseg = dgj * gj
                dcols.append(jnp.sum(dseg, axis=1, keepdims=True))
                dar_ref[0, gi * HPG + j] = -jnp.sum(dseg, axis=0, keepdims=True)
            dc = dc + _dot(dcb, bm, NN)
            db = _dot(dcb, cm, TN)
            sx = xd * e_end
            db = db + _dot(sx, dh, NT)
            dsx = _dot(bm, dh, NN)
            dxd = dxd + dsx * e_end
            de = dsx * sx
            dac = dac - de
            dlast = jnp.sum(de, axis=0, keepdims=True) + jnp.sum(dh * hprev, axis=0, keepdims=True) * e_last
            dsk = ds_ref[:, gi * GROUP_W:(gi + 1) * GROUP_W]
            dxbc_parts += [dxd * dt + dsk * g, db, dc]
            ddt_ref[0, gi] = _ssd_gather(dxd * x, None, masks, grp)
            dac_ref[0, gi] = _ssd_gather(dac + is_last * dlast, dcols, masks, grp)
            dds_parts.append(jnp.sum(g * x, axis=0, keepdims=True))
            dh_sc[gi] = dhprev
        dxbc_ref[0] = jnp.concatenate(dxbc_parts, axis=1)
        dds = jnp.concatenate(dds_parts, axis=1)
        first_all = first & (pl.program_id(1) == 0)

        @pl.when(first_all)
        def _():
            dds_ref[...] = dds

        @pl.when(jnp.logical_not(first_all))
        def _():
            dds_ref[...] += dds

    rc = lambda c: nc - 1 - c
    ng = SSM_GROUPS // gb
    in_specs = [pl.BlockSpec(s.block_shape, (lambda g, b, c, f=s.index_map: f(b, g, c))) for s in _ssd_in_specs(nc, True, gb)]
    in_specs.append(pl.BlockSpec((1, gb, 1, STATE_N, GROUP_W), lambda g, b, c: (b, g, rc(c), 0, 0)))
    in_specs.append(pl.BlockSpec((1, CHUNK, gb * GROUP_W), lambda g, b, c: (b, rc(c), g)))
    per_group = pl.BlockSpec((1, gb, CHUNK, LANE), lambda g, b, c: (b, g, rc(c), 0))
    out_specs = [pl.BlockSpec((1, CHUNK, gb * XBC_W), lambda g, b, c: (b, rc(c), g)), per_group, per_group,
                 pl.BlockSpec((1, gb * HPG, 1, CHUNK), lambda g, b, c: (b, g, 0, rc(c))),
                 pl.BlockSpec((1, gb * GROUP_W), lambda g, b, c: (0, g))]
    out_shape = [jax.ShapeDtypeStruct(xbc.shape, F32),
                 jax.ShapeDtypeStruct((B, SSM_GROUPS, S, LANE), F32), jax.ShapeDtypeStruct((B, SSM_GROUPS, S, LANE), F32),
                 jax.ShapeDtypeStruct(acr.shape, F32), jax.ShapeDtypeStruct(dsk.shape, F32)]
    return pl.pallas_call(
        body, grid=(ng, B, nc), in_specs=in_specs, out_specs=out_specs, out_shape=out_shape,
        scratch_shapes=[pltpu.VMEM((gb, STATE_N, GROUP_W), F32)], name="ssd_bwd",
        compiler_params=_cparams(("arbitrary", "arbitrary", "arbitrary")))(xbc, dtp, acp, acr, dsk, hps, dy)


@jax.custom_vjp
def ssd(xbc, dtp, acp, acr, dsk):
    return _ssd_fwd_call(xbc, dtp, acp, acr, dsk)[0]


def _ssd_fwd(xbc, dtp, acp, acr, dsk):
    y, hps = _ssd_fwd_call(xbc, dtp, acp, acr, dsk)
    return y, (xbc, dtp, acp, acr, dsk, hps)


def _ssd_bwd(res, dy):
    dxbc, ddt, dac, dacr, dds = _ssd_bwd_call(*res, dy)
    return dxbc, jnp.sum(ddt, axis=1), jnp.sum(dac, axis=1), dacr, dds


ssd.defvjp(_ssd_fwd, _ssd_bwd)


def _pack_small(arrs):
    flat = jnp.concatenate([a.reshape(-1) for a in arrs])
    rows = -(-flat.shape[0] // (8 * LANE)) * 8
    return jnp.pad(flat, (0, rows * LANE - flat.shape[0])).reshape(rows, LANE)


def _unpack_small(buf, shapes):
    flat = buf.reshape(-1)
    out, off = [], 0
    for shp in shapes:
        n = int(np.prod(shp))
        out.append(flat[off:off + n].reshape(shp))
        off += n
    return out


def _rows_tile(rows, cap):
    for cand in range(min(rows, cap), 7, -8):
        if rows % cand == 0:
            return cand
    return rows


def _pair_sum(mine, theirs, cidx, name):
    n4, kk, nn = mine.shape
    half = kk // 2
    tr = _rows_tile(half, 256)
    nb = half // tr

    def body(c_ref, a_ref, b_ref, o_ref, ob_ref):
        tot = a_ref[...] + b_ref[...]
        o_ref[...] = tot
        ob_ref[...] = tot.astype(BF16)

    spec = pl.BlockSpec((1, tr, nn), lambda j, i, c: (j, i, 0))
    grid_spec = pltpu.PrefetchScalarGridSpec(
        num_scalar_prefetch=1, grid=(n4, nb),
        in_specs=[pl.BlockSpec((1, tr, nn), lambda j, i, c: (j, c[0] * nb + i, 0)), spec], out_specs=[spec, spec])
    return pl.pallas_call(
        body, grid_spec=grid_spec,
        out_shape=[jax.ShapeDtypeStruct((n4, half, nn), F32), jax.ShapeDtypeStruct((n4, half, nn), BF16)],
        name=name, compiler_params=_cparams(("parallel", "parallel")))(cidx, mine, theirs)


def _chip_sum(quad, pair, chip_idx, name):
    _, rows, nn = quad.shape
    tr = _rows_tile(rows, 256)

    def body(s_ref, q_ref, p_ref, o_ref):
        for mine in range(4):
            @pl.when(s_ref[0] == mine)
            def _(mine=mine):
                acc = None
                for d in range(4):
                    term = p_ref[0] if d == mine else q_ref[d].astype(F32)
                    acc = term if acc is None else acc + term
                o_ref[...] = acc

    grid_spec = pltpu.PrefetchScalarGridSpec(
        num_scalar_prefetch=1, grid=(rows // tr,),
        in_specs=[pl.BlockSpec((4, tr, nn), lambda i, s: (0, i, 0)), pl.BlockSpec((1, tr, nn), lambda i, s: (s[0], i, 0))],
        out_specs=pl.BlockSpec((tr, nn), lambda i, s: (i, 0)))
    return pl.pallas_call(body, grid_spec=grid_spec, out_shape=jax.ShapeDtypeStruct((rows, nn), F32), name=name,
                          compiler_params=_cparams(("parallel",)))(chip_idx, quad, pair)


def _adam_halves_call(w, mine, other, cidx, m, v, name):
    _, rows, nn = w.shape
    half = rows // 2
    tr = _rows_tile(half, 128)
    nb = half // tr

    def body(c_ref, w_ref, a_ref, b_ref, m_ref, v_ref, g_ref, d_ref, nm_ref, nv_ref):
        upper = (pl.program_id(0) >= nb).astype(jnp.int32)
        g = jnp.where(upper == c_ref[0], a_ref[...], b_ref[...])
        g_ref[0] = g
        d_ref[0], nm_ref[0], nv_ref[0] = _adam_fn(w_ref[0], g, m_ref[0], v_ref[0])

    spec = pl.BlockSpec((1, tr, nn), lambda i, c: (0, i, 0))
    hspec = pl.BlockSpec((tr, nn), lambda i, c: (i % nb, 0))
    grid_spec = pltpu.PrefetchScalarGridSpec(num_scalar_prefetch=1, grid=(2 * nb,),
                                             in_specs=[spec, hspec, hspec, spec, spec], out_specs=[spec] * 4)
    return pl.pallas_call(body, grid_spec=grid_spec, out_shape=[jax.ShapeDtypeStruct(w.shape, F32)] * 4, name=name,
                          compiler_params=_cparams(("parallel",)))(cidx, w, mine, other, m, v)


def _stack_sum(stack, name):
    n, rows, nn = stack.shape
    tr = _rows_tile(rows, 256)

    def body(s_ref, o_ref):
        acc = s_ref[0]
        for d in range(1, n):
            acc = acc + s_ref[d]
        o_ref[...] = acc

    return pl.pallas_call(
        body, grid=(rows // tr,), in_specs=[pl.BlockSpec((n, tr, nn), lambda i: (0, i, 0))],
        out_specs=pl.BlockSpec((tr, nn), lambda i: (i, 0)), out_shape=jax.ShapeDtypeStruct((rows, nn), F32),
        name=name, compiler_params=_cparams(("parallel",)))(stack)


def _adam_call(w, g, m, v, name):
    rows, nn = w.shape
    tr = _rows_tile(rows, 128)

    def body(w_ref, g_ref, m_ref, v_ref, d_ref, nm_ref, nv_ref):
        d_ref[...], nm_ref[...], nv_ref[...] = _adam_fn(w_ref[...], g_ref[...], m_ref[...], v_ref[...])

    spec = pl.BlockSpec((tr, nn), lambda i: (i, 0))
    sds = jax.ShapeDtypeStruct((rows, nn), F32)
    return pl.pallas_call(body, grid=(rows // tr,), in_specs=[spec] * 4, out_specs=[spec] * 3,
                          out_shape=[sds] * 3, name=name, compiler_params=_cparams(("parallel",)))(w, g, m, v)


def _adam_fn(w, g, m, v):
    m = ADAM_B1 * m + (1.0 - ADAM_B1) * g
    v = ADAM_B2 * v + (1.0 - ADAM_B2) * (g * g)
    m_hat = m / (1.0 - ADAM_B1 ** ADAM_STEP)
    v_hat = v / (1.0 - ADAM_B2 ** ADAM_STEP)
    delta = -ADAM_LR * (m_hat / (jnp.sqrt(v_hat) + ADAM_EPS) + ADAM_WD * w)
    return delta, m, v


def _mesh_pos():
    return lax.axis_index("x"), lax.axis_index("y"), lax.axis_index("c")


def _other_chips(x, y):
    return [(1 - x, y), (x, 1 - y), (1 - x, 1 - y)]


HBM_SPEC = pl.BlockSpec(memory_space=pl.ANY)


def _remote(src, dst, send_sems, recv_sems, k, to):
    return pltpu.make_async_remote_copy(src_ref=src, dst_ref=dst, send_sem=send_sems.at[k], recv_sem=recv_sems.at[k],
                                        device_id=to, device_id_type=MESH)


def _half_rows(c, rows, align):
    half = rows // 2
    return (pl.ds(pl.multiple_of(c * half, align), half), pl.ds(pl.multiple_of((1 - c) * half, align), half))


def _gather_weights(mats, conv):
    n = len(mats)

    def body(*refs):
        ins, conv_in = refs[:n], refs[n]
        outs, conv_out = refs[n + 1:2 * n + 1], refs[2 * n + 1]
        send_sems, recv_sems, local_sem = refs[2 * n + 2:]
        x, y, c = _mesh_pos()
        me, sibling, s = (x, y, c), (x, y, 1 - c), 2 * x + y
        chips = _other_chips(x, y)
        rows = [_half_rows(c, m.shape[0], 16) for m in mats]
        own = pltpu.make_async_copy(conv_in, conv_out.at[s], local_sem)
        own.start()
        sent = []
        for i in range(n):
            mine = rows[i][0]
            for j, (cx, cy) in enumerate(chips):
                sent.append(_remote(ins[i].at[mine], outs[i].at[s, mine], send_sems, recv_sems, 6 * i + j, (cx, cy, c)))
        for j, (cx, cy) in enumerate(chips):
            sent.append(_remote(conv_in, conv_out.at[s], send_sems, recv_sems, 6 * n + j, (cx, cy, c)))
        for cp in sent:
            cp.start()
        for i in range(n):
            mine = rows[i][0]
            for j, (cx, cy) in enumerate(chips):
                landed = outs[i].at[2 * cx + cy, mine]
                _remote(landed, landed, send_sems, recv_sems, 6 * i + j, me).wait_recv()
                fwd = _remote(landed, landed, send_sems, recv_sems, 6 * i + 3 + j, sibling)
                fwd.start()
                sent.append(fwd)
        for j, (cx, cy) in enumerate(chips):
            slot = conv_out.at[2 * cx + cy]
            _remote(slot, slot, send_sems, recv_sems, 6 * n + j, me).wait_recv()
        for i in range(n):
            theirs_rows = rows[i][1]
            for j, (cx, cy) in enumerate(chips):
                theirs = outs[i].at[2 * cx + cy, theirs_rows]
                _remote(theirs, theirs, send_sems, recv_sems, 6 * i + 3 + j, me).wait_recv()
        for cp in sent:
            cp.wait_send()
        own.wait()

    out_shape = [jax.ShapeDtypeStruct((4,) + m.shape, m.dtype) for m in mats]
    out_shape.append(jax.ShapeDtypeStruct((4,) + conv.shape, conv.dtype))
    res = pl.pallas_call(
        body, in_specs=[HBM_SPEC] * (n + 1), out_specs=[HBM_SPEC] * (n + 1), out_shape=out_shape,
        scratch_shapes=[pltpu.SemaphoreType.DMA((6 * n + 3,)), pltpu.SemaphoreType.DMA((6 * n + 3,)),
                        pltpu.SemaphoreType.DMA],
        name="all_gather_weights")(*mats, conv)
    return res[:n], res[n]


def _sibling_exchange(stacks):
    n = len(stacks)

    def body(*refs):
        ins, outs = refs[:n], refs[n:2 * n]
        send_sems, recv_sems = refs[2 * n:]
        x, y, c = _mesh_pos()
        cps = []
        for i in range(n):
            theirs = _half_rows(c, stacks[i].shape[1], 8)[1]
            cps.append(_remote(ins[i].at[:, theirs, :], outs[i], send_sems, recv_sems, i, (x, y, 1 - c)))
        for cp in cps:
            cp.start()
        for cp in cps:
            cp.wait()

    out_shape = [jax.ShapeDtypeStruct((4, s.shape[1] // 2, s.shape[2]), s.dtype) for s in stacks]
    return pl.pallas_call(
        body, in_specs=[HBM_SPEC] * n, out_specs=[HBM_SPEC] * n, out_shape=out_shape,
        scratch_shapes=[pltpu.SemaphoreType.DMA((n,)), pltpu.SemaphoreType.DMA((n,))],
        name="grad_sibling_exchange")(*stacks)


def _chip_exchange(parts):
    n = len(parts)

    def body(*refs):
        ins, outs = refs[:n], refs[n:2 * n]
        send_sems, recv_sems = refs[2 * n:]
        x, y, c = _mesh_pos()
        me, s = (x, y, c), 2 * x + y
        chips = _other_chips(x, y)
        sent = [_remote(ins[i].at[2 * cx + cy], outs[i].at[s], send_sems, recv_sems, 3 * i + j, (cx, cy, c))
                for i in range(n) for j, (cx, cy) in enumerate(chips)]
        for cp in sent:
            cp.start()
        for i in range(n):
            for j, (cx, cy) in enumerate(chips):
                slot = outs[i].at[2 * cx + cy]
                _remote(slot, slot, send_sems, recv_sems, 3 * i + j, me).wait_recv()
        for cp in sent:
            cp.wait_send()

    return pl.pallas_call(
        body, in_specs=[HBM_SPEC] * n, out_specs=[HBM_SPEC] * n,
        out_shape=[jax.ShapeDtypeStruct(p.shape, p.dtype) for p in parts],
        scratch_shapes=[pltpu.SemaphoreType.DMA((3 * n,)), pltpu.SemaphoreType.DMA((3 * n,))],
        name="grad_chip_exchange")(*parts)


def _sibling_swap(halves):
    n = len(halves)

    def body(*refs):
        ins, outs = refs[:n], refs[n:2 * n]
        send_sems, recv_sems = refs[2 * n:]
        x, y, c = _mesh_pos()
        cps = [_remote(ins[i], outs[i], send_sems, recv_sems, i, (x, y, 1 - c)) for i in range(n)]
        for cp in cps:
            cp.start()
        for cp in cps:
            cp.wait()

    return pl.pallas_call(
        body, in_specs=[HBM_SPEC] * n, out_specs=[HBM_SPEC] * n,
        out_shape=[jax.ShapeDtypeStruct(h.shape, h.dtype) for h in halves],
        scratch_shapes=[pltpu.SemaphoreType.DMA((n,)), pltpu.SemaphoreType.DMA((n,))],
        name="grad_sibling_swap")(*halves)


def _gather_small(vec):
    def body(in_ref, out_ref, send_sems, recv_sems, local_sem):
        x, y, c = _mesh_pos()
        me = (x, y, c)
        own = pltpu.make_async_copy(in_ref, out_ref.at[4 * x + 2 * y + c], local_sem)
        own.start()
        peers = [(1 - x if k & 4 else x, 1 - y if k & 2 else y, 1 - c if k & 1 else c) for k in range(1, 8)]
        sent = [_remote(in_ref, out_ref.at[4 * x + 2 * y + c], send_sems, recv_sems, k, p) for k, p in enumerate(peers)]
        for cp in sent:
            cp.start()
        for k, (px, py, pc) in enumerate(peers):
            slot = out_ref.at[4 * px + 2 * py + pc]
            _remote(slot, slot, send_sems, recv_sems, k, me).wait_recv()
        for cp in sent:
            cp.wait_send()
        own.wait()

    return pl.pallas_call(
        body, in_specs=[HBM_SPEC], out_specs=HBM_SPEC, out_shape=jax.ShapeDtypeStruct((8,) + vec.shape, vec.dtype),
        scratch_shapes=[pltpu.SemaphoreType.DMA((7,)), pltpu.SemaphoreType.DMA((7,)), pltpu.SemaphoreType.DMA],
        name="grad_gather_small")(vec)


def _reduce_matrices(stacks, names):
    cidx = lax.axis_index("c").astype(jnp.int32).reshape(1)
    chip = (2 * lax.axis_index("x") + lax.axis_index("y")).astype(jnp.int32).reshape(1)
    got = _sibling_exchange(stacks)
    pairs = [_pair_sum(a, b, cidx, "grad_pair_sum_" + nm) for a, b, nm in zip(stacks, got, names)]
    quads = _chip_exchange([p[1] for p in pairs])
    mine = [_chip_sum(q, p[0], chip, "grad_chip_sum_" + nm) for q, p, nm in zip(quads, pairs, names)]
    return mine, _sibling_swap(mine)


def _pad_cols(a, n):
    return jnp.concatenate([a, jnp.zeros((a.shape[0], n - a.shape[1]), a.dtype)], axis=1)


def _group_channels(a):
    lead = a.shape[:-1]
    xs = a[..., :D_INNER].reshape(lead + (SSM_GROUPS, GROUP_W))
    bs = a[..., D_INNER:D_INNER + SSM_GROUPS * STATE_N].reshape(lead + (SSM_GROUPS, STATE_N))
    cs = a[..., D_INNER + SSM_GROUPS * STATE_N:].reshape(lead + (SSM_GROUPS, STATE_N))
    return jnp.concatenate([xs, bs, cs], axis=-1).reshape(lead + (CONV_CH,))


PROJ_SEGS = (('gate_a', D_MODEL), ('gate_b', D_MODEL), ('z', D_INNER), ('xbc', CONV_CH), ('q_lat', Q_RANK),
             ('kv_lat', KV_RANK), ('k_rope', LANE), ('dt', LANE))
PROJ_WIDE = sum(w for _, w in PROJ_SEGS[:4])
PROJ_LANE0 = {n: (v if v < PROJ_WIDE else v - PROJ_WIDE) for n, v in
              zip([n for n, _ in PROJ_SEGS], [int(v) for v in np.cumsum([0] + [w for _, w in PROJ_SEGS])[:-1]])}
CONV_LANE0 = PROJ_LANE0['xbc']
KR_LANE0 = PROJ_LANE0['k_rope']


def _lay_w_in(w):
    idx = np.cumsum(IN_SIZES)[:-1]
    q_lat, kv_lat, k_rope, z, xbc, dt, gate_a, gate_b = jnp.split(w, [int(v) for v in idx], axis=1)
    return jnp.concatenate([gate_a, gate_b, z, _group_channels(xbc), q_lat, kv_lat, _pad_cols(k_rope, LANE),
                            _pad_cols(dt, LANE)], axis=1)


@jax.custom_vjp
def project(h, w, tok):
    return _project_impl(h, w)


def _project_impl(h, w):
    return (_mm(h, w[:, :PROJ_WIDE], "w_in_fwd", BF16), _mm(h, w[:, PROJ_WIDE:], "w_in_narrow_fwd")) + tuple(
        jnp.zeros((h.shape[0], wd), BF16) for _, wd in PROJ_SEGS)


def _project_fwd(h, w, tok):
    return _project_impl(h, w), (h, w)


def _project_bwd(res, cots):
    h, w = res
    g = jnp.concatenate(cots[2:], axis=1)
    return _mm(g, w.T, "w_in_dx", h.dtype), jnp.zeros_like(w), _mm(h.T, g, "w_in_dw")


project.defvjp(_project_fwd, _project_bwd)


def _lay_w_uq(w):
    w3 = w.reshape(Q_RANK, N_HEADS, NOPE + ROPE)
    w3 = jnp.concatenate([w3, jnp.zeros((Q_RANK, N_HEADS, QK_PAD - NOPE - ROPE), w.dtype)], axis=2)
    return w3.reshape(Q_RANK, N_HEADS * QK_PAD)


def _lay_w_ukv(w):
    w3 = w.reshape(KV_RANK, N_HEADS, NOPE + V_DIM)
    return jnp.concatenate([w3[:, :, :NOPE].reshape(KV_RANK, -1), w3[:, :, NOPE:].reshape(KV_RANK, -1)], axis=1)


def _pad_lanes(v, n=LANE):
    return jnp.concatenate([v, jnp.zeros((v.shape[0], n - v.shape[1]), v.dtype)], axis=1)


def _local_loss(toks, small, x, wb, c8, posf, target):
    B, S, D = x.shape
    T = B * S

    def lin(name, a, key, lay=lambda w: w, out_dtype=F32):
        return make_linear(name, out_dtype)(a, lay(wb[key]), lay(toks[key]))

    rows2 = lambda a: a.reshape(T, a.shape[-1])
    rows3 = lambda a: a.reshape(B, S, a.shape[-1])

    sc = make_rowwise("silu_c", _f_silu, 1, 0, 0, ('row',))((c8[None],), (), ())[0][0]
    mod = make_linear("ada", F32, 4)(sc, wb['w_ada'], toks['w_ada'])[:B] + small['b_ada']
    shift1, scale1, gate1, shift2, scale2, gate2 = [m[:, None, :] for m in jnp.split(mod, 6, axis=-1)]

    h, x_res = make_rowwise("modulate1", _f_modulate, 1, 2, 1, ('row',), forward_row=0)(
        (x,), (scale1, shift1), (small['g_pre_mix'],))
    outs = project(rows2(h), _lay_w_in(wb['w_in']), _lay_w_in(toks['w_in']))
    wide = lax.stop_gradient(rows3(outs[0]))
    proj = lax.stop_gradient(rows3(outs[1]))
    stand = {n: rows3(o) for (n, _), o in zip(PROJ_SEGS, outs[2:])}

    def win(seg, block):
        return (PROJ_LANE0[seg] // block, dict(PROJ_SEGS)[seg])

    inv = ROPE_THETA ** (-jnp.arange(ROPE // 2, dtype=F32) / (ROPE // 2))
    inv_lane = jnp.concatenate([inv, inv, jnp.zeros((LANE - ROPE,), F32)])[None]
    tabs = tuple(_rope_tables(posf, inv_lane))
    qn = make_rowwise("rms_q", _f_rms, 1, 0, 1, ('row',), windows={0: win('q_lat', Q_RANK)})(
        (proj,), (), (small['g_q_lat'],), (stand['q_lat'],))[0]
    kvn = make_rowwise("rms_kv", _f_rms, 1, 0, 1, ('row',), windows={0: win('kv_lat', KV_RANK)})(
        (proj,), (), (small['g_kv_lat'],), (stand['kv_lat'],))[0]
    qp = rows3(lin("w_uq", rows2(qn), 'w_uq', _lay_w_uq, BF16))
    kvp = rows3(lin("w_ukv", rows2(kvn), 'w_ukv', _lay_w_ukv, BF16))
    qr = rope_q(qp, tabs)
    kr = build_k(kvp, proj, stand['k_rope'], tabs)
    att = attention(qr, kr, kvp)
    attn = rows3(lin("w_o_attn", rows2(att), 'w_o_attn', out_dtype=BF16))

    xa = conv_silu(wide, stand['xbc'], _group_channels(wb['conv_w_f32']), _group_channels(small['conv_b']))
    dt_pad, a_pad = make_rowwise("dt_softplus", _f_dt, 1, 0, 2, ('row', 'row'), windows={0: win('dt', LANE)})(
        (proj,), (), (_pad_lanes(small['dt_bias']), _pad_lanes(small['a_log'])), (stand['dt'],))
    ac_pad = chunk_cumsum(a_pad)
    acr = jnp.transpose(ac_pad[..., :SSM_HEADS], (0, 2, 1))[:, :, None, :]
    dsk = jnp.repeat(small['d_skip'], HEAD_P, axis=-1)
    y = ssd(xa, dt_pad, ac_pad, acr, dsk)
    yg = make_rowwise("gated_norm", _f_gated_norm, 2, 0, 1, ('row',), ncol=SSM_GROUPS, ts_cap=2048,
                      windows={1: win('z', GROUP_W)})((y, wide), (), (small['g_ssm_out'],), (stand['z'],))[0]
    ssm = rows3(lin("w_o_ssm", rows2(yg), 'w_o_ssm', out_dtype=BF16))

    merged = make_rowwise("merge", _f_merge, 4, 0, 0, ('row',),
                          windows={2: win('gate_a', D_MODEL), 3: win('gate_b', D_MODEL)})(
        (attn, ssm, wide, wide), (), (), (stand['gate_a'], stand['gate_b']))[0]
    mix = rows3(lin("w_out", rows2(merged), 'w_out', out_dtype=BF16))
    x1 = make_rowwise("post_mix", _f_post, 2, 1, 1, ('row',))((x_res, mix), (gate1,), (small['g_post_mix'],))[0]

    h2, x1_res = make_rowwise("modulate2", _f_modulate, 1, 2, 1, ('row',), forward_row=0)(
        (x1,), (scale2, shift2), (small['g_pre_mlp'],))
    ff = rows3(ffn(rows2(h2), wb['w_ff1'], toks['w_ff1'], wb['w_ff2'], toks['w_ff2']))
    lvec = make_rowwise("final_loss", _f_final_loss, 3, 1, 1, ('sum',), nodiff=(2,))(
        (x1_res, ff, target), (gate2,), (small['g_post_mlp'],))[0]
    return jnp.sum(lvec)


MATRICES = COL_SHARDED + ROW_SHARDED
STACKED_DW = ('w_ada', 'w_ff1')


def _local_step(x, c, positions, target, wb, small):
    B = x.shape[0]
    c8 = jnp.concatenate([c, jnp.zeros((16 - B, c.shape[1]), F32)], axis=0)
    posf = positions.astype(F32)[..., None]
    toks = {k: jnp.zeros(wb[k].shape, F32) for k in MATRICES if k != 'conv_w'}
    for k in STACKED_DW:
        rows, cols = wb[k].shape
        toks[k] = jnp.zeros((4, rows, cols // 4), F32)
    conv_w = wb['conv_w_f32']

    def loss_fn(toks, small, conv_w, x):
        wbl = dict(wb)
        wbl['conv_w_f32'] = conv_w
        return _local_loss(toks, small, x, wbl, c8, posf, target)

    loss, (g_tok, g_small, g_conv, g_x) = jax.value_and_grad(loss_fn, argnums=(0, 1, 2, 3))(toks, small, conv_w, x)
    grads = dict(g_tok)
    grads.update(g_small)
    grads['conv_w'] = g_conv
    return loss, g_x, grads


def kernel(x, c, positions, w_ada, b_ada, g_pre_mix, g_post_mix, w_in, g_q_lat, g_kv_lat, w_uq, w_ukv, w_o_attn, conv_w, conv_b, dt_bias, a_log, d_skip, g_ssm_out, w_o_ssm, w_out, g_pre_mlp, g_post_mlp, w_ff1, w_ff2, loss_target, m_w_ada, m_b_ada, m_g_pre_mix, m_g_post_mix, m_w_in, m_g_q_lat, m_g_kv_lat, m_w_uq, m_w_ukv, m_w_o_attn, m_conv_w, m_conv_b, m_dt_bias, m_a_log, m_d_skip, m_g_ssm_out, m_w_o_ssm, m_w_out, m_g_pre_mlp, m_g_post_mlp, m_w_ff1, m_w_ff2, v_w_ada, v_b_ada, v_g_pre_mix, v_g_post_mix, v_w_in, v_g_q_lat, v_g_kv_lat, v_w_uq, v_w_ukv, v_w_o_attn, v_conv_w, v_conv_b, v_dt_bias, v_a_log, v_d_skip, v_g_ssm_out, v_w_o_ssm, v_w_out, v_g_pre_mlp, v_g_post_mlp, v_w_ff1, v_w_ff2):
    given = dict(locals())
    w_loc = {n: given[n] for n in WEIGHTS}
    m_loc = {n: given["m_" + n] for n in WEIGHTS}
    v_loc = {n: given["v_" + n] for n in WEIGHTS}
    mats = [n for n in WEIGHTS if n in MATRICES and n != 'conv_w']
    vecs = [n for n in WEIGHTS if n not in MATRICES]

    own = [w_loc[n][0].astype(BF16) for n in mats]
    g_mats, g_conv = _gather_weights(own, conv_w[0])
    chip = 2 * lax.axis_index("x") + lax.axis_index("y")
    wb = {}
    for n, g, mine in zip(mats, g_mats, own):
        g = lax.dynamic_update_slice_in_dim(g, mine[None], chip, axis=0)
        if n in COL_SHARDED:
            wb[n] = jnp.transpose(g, (1, 0, 2)).reshape(g.shape[1], -1)
        else:
            wb[n] = g.reshape(-1, g.shape[2])
    wb['conv_w_f32'] = jnp.transpose(g_conv, (1, 0, 2)).reshape(CONV_K, -1)
    small = {n: w_loc[n] for n in vecs}

    loss_part, grad_x, grads = _local_step(x, c, positions, loss_target, wb, small)
    loss = lax.psum(loss_part, ("x", "y", "c"))

    stacks = []
    for n in mats:
        kk, nn = w_loc[n].shape[1:]
        if n in STACKED_DW:
            stacks.append(grads[n])
        elif n in COL_SHARDED:
            stacks.append(jnp.transpose(grads[n].reshape(kk, 4, nn), (1, 0, 2)))
        else:
            stacks.append(grads[n].reshape(4, kk, nn))
    g_mine, g_other = _reduce_matrices(stacks, mats)
    g_shard = {}

    vec_shapes = [tuple(grads[n].shape) for n in vecs] + [tuple(grads['conv_w'].shape)]
    total = _stack_sum(_gather_small(_pack_small([grads[n] for n in vecs] + [grads['conv_w']])), "grad_sum_small")
    g_vec = _unpack_small(total, vec_shapes)
    n_conv = conv_w.shape[2]
    chip = 2 * lax.axis_index("x") + lax.axis_index("y")
    g_shard['conv_w'] = lax.dynamic_slice_in_dim(g_vec[-1], chip * n_conv, n_conv, axis=1)
    for n, g in zip(vecs, g_vec):
        g_shard[n] = g

    delta, new_m, new_v = {}, {}, {}
    cidx = lax.axis_index("c").astype(jnp.int32).reshape(1)
    for n, mine, other in zip(mats, g_mine, g_other):
        g_shard[n], delta[n], new_m[n], new_v[n] = _adam_halves_call(
            w_loc[n], mine, other, cidx, m_loc[n], v_loc[n], "adamw_" + n)
    rest = vecs + ['conv_w']
    rest_shapes = [tuple(w_loc[n].shape) for n in rest]
    packed = [_pack_small([src[n] for n in rest]) for src in (w_loc, g_shard, m_loc, v_loc)]
    for dst, buf in zip((delta, new_m, new_v), _adam_call(*packed, "adamw_small")):
        dst.update(zip(rest, _unpack_small(buf, rest_shapes)))

    def out(d):
        return [d[n].reshape(w_loc[n].shape) for n in WEIGHTS]

    return (loss, grad_x, *out(g_shard), *out(delta), *out(new_m), *out(new_v))
```

```python
import functools
import math

import numpy as np
import jax
import jax.numpy as jnp
from jax import lax
from jax.experimental import pallas as pl
from jax.experimental.pallas import tpu as pltpu

F32 = jnp.float32
BF16 = jnp.bfloat16
MESH = pl.DeviceIdType.MESH

D_MODEL = 1024
N_HEADS = 8
NOPE = 128
ROPE = 64
V_DIM = 128
Q_RANK = 256
KV_RANK = 256
ROPE_THETA = 10000.0
D_INNER = 2048
SSM_HEADS = 32
SSM_GROUPS = 8
HEAD_P = 64
STATE_N = 128
CONV_K = 4
CHUNK = 128
CONV_CH = D_INNER + 2 * SSM_GROUPS * STATE_N
D_FF = 4096
EPS = 1e-6
IN_SIZES = (Q_RANK, KV_RANK, ROPE, D_INNER, CONV_CH, SSM_HEADS, D_MODEL, D_MODEL)
ADAM_LR, ADAM_B1, ADAM_B2, ADAM_EPS, ADAM_WD, ADAM_STEP = 0.001, 0.9, 0.999, 1e-08, 0.01, 10

VMEM_LIMIT_BYTES = 52 * 1024 * 1024
LANE = 128
QK_PAD = 256

WEIGHTS = ['w_ada', 'b_ada', 'g_pre_mix', 'g_post_mix', 'w_in', 'g_q_lat', 'g_kv_lat', 'w_uq', 'w_ukv',
           'w_o_attn', 'conv_w', 'conv_b', 'dt_bias', 'a_log', 'd_skip', 'g_ssm_out', 'w_o_ssm', 'w_out',
           'g_pre_mlp', 'g_post_mlp', 'w_ff1', 'w_ff2']
COL_SHARDED = ('w_ada', 'w_in', 'w_uq', 'w_ukv', 'conv_w', 'w_ff1')
ROW_SHARDED = ('w_o_attn', 'w_o_ssm', 'w_out', 'w_ff2')


def _cparams(sem):
    return pltpu.CompilerParams(dimension_semantics=sem, vmem_limit_bytes=VMEM_LIMIT_BYTES)


def _tile(n, cap):
    if n <= cap:
        return n
    k = n // LANE
    best = LANE
    for d in range(1, k + 1):
        if k % d == 0 and d * LANE <= cap:
            best = d * LANE
    return best


def _mm(a, w, name, out_dtype=F32, epilogue=None, extras=(), out_dtypes=None):
    M, K = a.shape
    N = w.shape[1]
    tm = min(M, 1024)
    tn = _tile(N, 1024)
    tk = _tile(K, 2048)
    nk = K // tk
    dts = tuple(out_dtypes) if epilogue is not None else (out_dtype,)
    n_x, n_o = len(extras), len(dts)

    def finish(acc, refs):
        res = epilogue(acc, *[r[...] for r in refs[:n_x]]) if epilogue is not None else (acc,)
        for o_ref, val, dt in zip(refs[n_x:n_x + n_o], res, dts):
            o_ref[...] = val.astype(dt)

    def body(a_ref, w_ref, *refs):
        part = jnp.dot(a_ref[...].astype(BF16), w_ref[...], preferred_element_type=F32)
        if nk == 1:
            finish(part, refs)
        else:
            acc_ref = refs[-1]
            k = pl.program_id(2)

            @pl.when(k == 0)
            def _():
                acc_ref[...] = part

            @pl.when(k > 0)
            def _():
                acc_ref[...] += part

            @pl.when(k == nk - 1)
            def _():
                finish(acc_ref[...], refs)

    ospec = pl.BlockSpec((tm, tn), lambda i, j, k: (i, j))
    res = pl.pallas_call(
        body, grid=(M // tm, N // tn, nk),
        in_specs=[pl.BlockSpec((tm, tk), lambda i, j, k: (i, k)), pl.BlockSpec((tk, tn), lambda i, j, k: (k, j))]
        + [ospec] * n_x,
        out_specs=[ospec] * n_o, out_shape=[jax.ShapeDtypeStruct((M, N), dt) for dt in dts],
        scratch_shapes=[pltpu.VMEM((tm, tn), F32)] if nk > 1 else [], name=name,
        compiler_params=_cparams(("parallel", "parallel", "arbitrary")))(a, w, *extras)
    return res if epilogue is not None else res[0]


def _mm_tn(a, g, name, col_shards=1):
    M, K = a.shape
    N = g.shape[1]
    tm = min(M, 1024)
    tk = _tile(K, 1024)
    tn = _tile(N // col_shards, 1024)
    nm = M // tm
    per = N // col_shards // tn

    def body(a_ref, g_ref, o_ref):
        part = lax.dot_general(a_ref[...].astype(BF16), g_ref[...].astype(BF16), (((0,), (0,)), ((), ())),
                               preferred_element_type=F32)
        m = pl.program_id(2)

        @pl.when(m == 0)
        def _():
            o_ref[...] = part.reshape(o_ref.shape)

        @pl.when(m > 0)
        def _():
            o_ref[...] += part.reshape(o_ref.shape)

    if col_shards == 1:
        out_spec = pl.BlockSpec((tk, tn), lambda i, j, m: (i, j))
        out_shape = jax.ShapeDtypeStruct((K, N), F32)
    else:
        out_spec = pl.BlockSpec((1, tk, tn), lambda i, j, m: (j // per, i, j % per))
        out_shape = jax.ShapeDtypeStruct((col_shards, K, N // col_shards), F32)
    return pl.pallas_call(
        body, grid=(K // tk, N // tn, nm),
        in_specs=[pl.BlockSpec((tm, tk), lambda i, j, m: (m, i)), pl.BlockSpec((tm, tn), lambda i, j, m: (m, j))],
        out_specs=out_spec, out_shape=out_shape, name=name,
        compiler_params=_cparams(("parallel", "parallel", "arbitrary")))(a, g)


def make_linear(name, out_dtype=F32, dw_col_shards=1):
    @jax.custom_vjp
    def linear(a, w, tok):
        return _mm(a, w, name + "_fwd", out_dtype)

    def fwd(a, w, tok):
        return _mm(a, w, name + "_fwd", out_dtype), (a, w)

    def bwd(res, g):
        a, w = res
        da = _mm(g, w.T, name + "_dx", a.dtype)
        dw = _mm_tn(a, g, name + "_dw", dw_col_shards)
        return da, jnp.zeros_like(w), dw

    linear.defvjp(fwd, bwd)
    return linear


def _relu2_epilogue(acc):
    r = jnp.maximum(acc, 0.0)
    return r * r, r


def _relu2_bwd_epilogue(acc, r):
    return (acc * (2.0 * r.astype(F32)),)


@jax.custom_vjp
def ffn(h, w1, tok1, w2, tok2):
    act, _ = _mm(h, w1, "w_ff1_fwd", epilogue=_relu2_epilogue, out_dtypes=(BF16, BF16))
    return _mm(act, w2, "w_ff2_fwd", BF16)


def _ffn_fwd(h, w1, tok1, w2, tok2):
    act, r = _mm(h, w1, "w_ff1_fwd", epilogue=_relu2_epilogue, out_dtypes=(BF16, BF16))
    return _mm(act, w2, "w_ff2_fwd", BF16), (h, w1, w2, act, r)


def _ffn_bwd(res, g):
    h, w1, w2, act, r = res
    du = _mm(g, w2.T, "w_ff2_dx", epilogue=_relu2_bwd_epilogue, extras=(r,), out_dtypes=(BF16,))[0]
    dw2 = _mm_tn(act, g, "w_ff2_dw")
    dw1 = _mm_tn(h, du, "w_ff1_dw", 4)
    dh = _mm(du, w1.T, "w_ff1_dx", h.dtype)
    return dh, jnp.zeros_like(w1), dw1, jnp.zeros_like(w2), dw2


ffn.defvjp(_ffn_fwd, _ffn_bwd)


def make_rowwise(name, f, n_rows, n_seqs, n_pars, out_kinds, ncol=1, nodiff=(), ts_cap=512, windows=None,
                 forward_row=None):
    windows = dict(windows or {})
    n_in = n_rows + n_seqs + n_pars
    diff_idx = [i for i in range(n_in) if i not in nodiff]

    def _dims(rows):
        B, S = rows[0].shape[0], rows[0].shape[1]
        ts = min(S, ts_cap)
        return B, S, ts

    def _width(i, r):
        return windows[i][1] if i in windows else r.shape[2]

    def _in_specs(rows, seqs, pars, ts):
        specs = []
        for i, r in enumerate(rows):
            col0 = windows[i][0] if i in windows else 0
            specs.append(pl.BlockSpec((1, ts, _width(i, r) // ncol), lambda k, b, s, col0=col0: (b, s, k + col0)))
        for q in seqs:
            specs.append(pl.BlockSpec((1, 1, q.shape[2] // ncol), lambda k, b, s: (b, 0, k)))
        for p in pars:
            specs.append(pl.BlockSpec((1, p.shape[1] // ncol), lambda k, b, s: (0, k)))
        return specs

    def _load(refs):
        vals = [r[0] for r in refs[:n_rows + n_seqs]]
        vals += [r[...] for r in refs[n_rows + n_seqs:n_in]]
        return vals

    def _out_struct(rows, seqs, pars, ts):
        blocks = [jax.ShapeDtypeStruct((ts, _width(i, r) // ncol), r.dtype) for i, r in enumerate(rows)]
        blocks += [jax.ShapeDtypeStruct((1, q.shape[2] // ncol), q.dtype) for q in seqs]
        blocks += [jax.ShapeDtypeStruct((1, p.shape[1] // ncol), p.dtype) for p in pars]
        return jax.eval_shape(f, *blocks)

    def _fwd_call(rows, seqs, pars):
        B, S, ts = _dims(rows)
        outs = _out_struct(rows, seqs, pars, ts)
        n_out = len(outs)

        def body(*refs):
            res = f(*_load(refs))
            first = (pl.program_id(1) == 0) & (pl.program_id(2) == 0)
            for o_ref, val, kind in zip(refs[n_in:], res, out_kinds):
                if kind == 'row':
                    o_ref[0] = val
                else:
                    tot = jnp.sum(val, axis=0, keepdims=True)

                    @pl.when(first)
                    def _(o_ref=o_ref, tot=tot):
                        o_ref[...] = tot

                    @pl.when(jnp.logical_not(first))
                    def _(o_ref=o_ref, tot=tot):
                        o_ref[...] += tot

        out_shape, out_specs = [], []
        for o, kind in zip(outs, out_kinds):
            d = o.shape[1]
            if kind == 'row':
                out_shape.append(jax.ShapeDtypeStruct((B, S, ncol * d), o.dtype))
                out_specs.append(pl.BlockSpec((1, ts, d), lambda k, b, s: (b, s, k)))
            else:
                out_shape.append(jax.ShapeDtypeStruct((1, ncol * d), o.dtype))
                out_specs.append(pl.BlockSpec((1, d), lambda k, b, s: (0, k)))
        res = pl.pallas_call(
            body, grid=(ncol, B, S // ts), in_specs=_in_specs(rows, seqs, pars, ts), out_specs=out_specs,
            out_shape=out_shape, name=name + "_fwd",
            compiler_params=_cparams(("arbitrary", "arbitrary", "arbitrary")))(*rows, *seqs, *pars)
        return tuple(res)

    def _bwd_call(rows, seqs, pars, cots, carried=None):
        B, S, ts = _dims(rows)
        outs = _out_struct(rows, seqs, pars, ts)
        n_out = len(outs)
        all_in = list(rows) + list(seqs) + list(pars)
        extra = [] if carried is None else [carried]

        def body(*refs):
            vals = _load(refs)
            if carried is not None:
                carried_ref, refs = refs[n_in + n_out], refs[:n_in + n_out] + refs[n_in + n_out + 1:]
            cts = []
            for c_ref, o, kind in zip(refs[n_in:n_in + n_out], outs, out_kinds):
                if kind == 'row':
                    cts.append(c_ref[0])
                else:
                    cts.append(jnp.broadcast_to(c_ref[...], o.shape))

            def g(*dv):
                full = list(vals)
                for i, v in zip(diff_idx, dv):
                    full[i] = v
                return tuple(f(*full))

            _, vjp = jax.vjp(g, *[vals[i] for i in diff_idx])
            grads = vjp(tuple(cts))
            b, s = pl.program_id(1), pl.program_id(2)
            for o_ref, i, gr in zip(refs[n_in + n_out:], diff_idx, grads):
                if i < n_rows:
                    if carried is not None and i == forward_row:
                        gr = gr + carried_ref[0]
                    o_ref[0] = gr.astype(o_ref.dtype)
                else:
                    first = (s == 0) if i < n_rows + n_seqs else ((b == 0) & (s == 0))
                    target = (lambda r: r.at[0]) if i < n_rows + n_seqs else (lambda r: r)

                    @pl.when(first)
                    def _(o_ref=o_ref, gr=gr, target=target):
                        target(o_ref)[...] = gr

                    @pl.when(jnp.logical_not(first))
                    def _(o_ref=o_ref, gr=gr, target=target):
                        target(o_ref)[...] += gr

        cot_specs = []
        for o, kind in zip(outs, out_kinds):
            d = o.shape[1]
            if kind == 'row':
                cot_specs.append(pl.BlockSpec((1, ts, d), lambda k, b, s: (b, s, k)))
            else:
                cot_specs.append(pl.BlockSpec((1, d), lambda k, b, s: (0, k)))
        out_shape, out_specs = [], []
        for i in diff_idx:
            a = all_in[i]
            if i < n_rows:
                out_shape.append(jax.ShapeDtypeStruct((B, S, _width(i, a)), BF16 if i in windows else a.dtype))
                out_specs.append(pl.BlockSpec((1, ts, _width(i, a) // ncol), lambda k, b, s: (b, s, k)))
                continue
            out_shape.append(jax.ShapeDtypeStruct(a.shape, a.dtype))
            if i < n_rows + n_seqs:
                out_specs.append(pl.BlockSpec((1, 1, a.shape[2] // ncol), lambda k, b, s: (b, 0, k)))
            else:
                out_specs.append(pl.BlockSpec((1, a.shape[1] // ncol), lambda k, b, s: (0, k)))
        if carried is not None:
            cot_specs.append(pl.BlockSpec((1, ts, carried.shape[2] // ncol), lambda k, b, s: (b, s, k)))
        res = pl.pallas_call(
            body, grid=(ncol, B, S // ts), in_specs=_in_specs(rows, seqs, pars, ts) + cot_specs,
            out_specs=out_specs, out_shape=out_shape, name=name + "_bwd",
            compiler_params=_cparams(("arbitrary", "arbitrary", "arbitrary")))(*all_in, *cots, *extra)
        grads = [None] * n_in
        for i, r in zip(diff_idx, res):
            grads[i] = r
        for i in nodiff:
            grads[i] = jnp.zeros_like(all_in[i])
        stand_in_grads = tuple(grads[i] for i in sorted(windows))
        for i in windows:
            grads[i] = jnp.zeros_like(all_in[i])
        return (tuple(grads[:n_rows]), tuple(grads[n_rows:n_rows + n_seqs]), tuple(grads[n_rows + n_seqs:]),
                stand_in_grads)

    def _outputs(rows, seqs, pars):
        res = _fwd_call(rows, seqs, pars)
        return res if forward_row is None else res + (rows[forward_row],)

    @jax.custom_vjp
    def op(rows, seqs, pars, stand_ins):
        return _outputs(rows, seqs, pars)

    def fwd(rows, seqs, pars, stand_ins):
        return _outputs(rows, seqs, pars), (rows, seqs, pars)

    def bwd(res, cots):
        rows, seqs, pars = res
        if forward_row is None:
            return _bwd_call(rows, seqs, pars, cots)
        return _bwd_call(rows, seqs, pars, cots[:-1], cots[-1])

    op.defvjp(fwd, bwd)
    return lambda rows, seqs, pars, stand_ins=(): op(tuple(rows), tuple(seqs), tuple(pars), tuple(stand_ins))


def _rms(x, g):
    x = x.astype(F32)
    return x * lax.rsqrt(jnp.mean(x * x, axis=-1, keepdims=True) + EPS) * g


def _silu(x):
    return x * lax.logistic(x)


def _f_silu(c):
    return (_silu(c),)


def _f_modulate(x, scale, shift, g):
    return ((_rms(x, g) * (1.0 + scale) + shift).astype(BF16),)


def _f_rms(x, g):
    return (_rms(x, g).astype(BF16),)


def _f_dt(dt_raw, dt_bias, a_log):
    z = dt_raw + dt_bias
    dt = jnp.maximum(z, 0.0) + jnp.log1p(jnp.exp(-jnp.abs(z)))
    return dt, dt * (-jnp.exp(a_log))


def _f_gated_norm(y, z, g):
    return (_rms(y * _silu(z.astype(F32)), g).astype(BF16),)


def _f_merge(attn, ssm, ga, gb):
    return ((lax.logistic(ga.astype(F32)) * attn + lax.logistic(gb.astype(F32)) * ssm).astype(BF16),)


def _f_post(x, m, gate, g):
    return (x + gate * _rms(m, g),)


def _f_final_loss(x, ff, target, gate, g):
    e = x + gate * _rms(ff, g) - target
    return (e * e * (0.5 / D_MODEL),)


def _rope_tables(posf, inv_lane):
    B, S, _ = posf.shape
    ts = min(S, 512)

    def body(p_ref, inv_ref, c_ref, a_ref, b_ref):
        ang = p_ref[0] * inv_ref[...]
        cs, sn = jnp.cos(ang), jnp.sin(ang)
        lane = lax.broadcasted_iota(jnp.int32, ang.shape, 1)
        c_ref[0] = jnp.where(lane < ROPE, cs, 0.0)
        a_ref[0] = jnp.where(lane < ROPE // 2, -sn, 0.0)
        b_ref[0] = jnp.where((lane >= ROPE // 2) & (lane < ROPE), sn, 0.0)

    spec = pl.BlockSpec((1, ts, LANE), lambda b, s: (b, s, 0))
    sds = jax.ShapeDtypeStruct((B, S, LANE), F32)
    return pl.pallas_call(
        body, grid=(B, S // ts),
        in_specs=[pl.BlockSpec((1, ts, 1), lambda b, s: (b, s, 0)), pl.BlockSpec((1, LANE), lambda b, s: (0, 0))],
        out_specs=[spec, spec, spec], out_shape=[sds, sds, sds], name="rope_tables",
        compiler_params=_cparams(("parallel", "parallel")))(posf, inv_lane)


def _rot(u, c, a, bm):
    return u * c + pltpu.roll(u, 96, 1) * a + pltpu.roll(u, 32, 1) * bm


def _rot_t(g, c, a, bm):
    return g * c + pltpu.roll(g * a, 32, 1) + pltpu.roll(g * bm, 96, 1)


def _rope_q_call(q, tabs, transpose, name):
    B, S, W = q.shape
    ts = min(S, 512)
    fn = _rot_t if transpose else _rot
    out_dtype = BF16

    def body(q_ref, c_ref, a_ref, b_ref, o_ref):
        tc, ta, tb = c_ref[0], a_ref[0], b_ref[0]
        for h in range(W // QK_PAD):
            u = q_ref[0, :, h * QK_PAD:(h + 1) * QK_PAD].astype(F32) * ATT_SCALE
            r = fn(u[:, NOPE:], tc, ta, tb)
            o_ref[0, :, h * QK_PAD:(h + 1) * QK_PAD] = jnp.concatenate([u[:, :NOPE], r], axis=1).astype(out_dtype)

    tspec = pl.BlockSpec((1, ts, LANE), lambda b, s: (b, s, 0))
    qspec = pl.BlockSpec((1, ts, W), lambda b, s: (b, s, 0))
    return pl.pallas_call(
        body, grid=(B, S // ts), in_specs=[qspec, tspec, tspec, tspec], out_specs=qspec,
        out_shape=jax.ShapeDtypeStruct(q.shape, out_dtype), name=name,
        compiler_params=_cparams(("parallel", "parallel")))(q, *tabs)


@jax.custom_vjp
def rope_q(q, tabs):
    return _rope_q_call(q, tabs, False, "rope_q_fwd")


def _rope_q_fwd(q, tabs):
    return _rope_q_call(q, tabs, False, "rope_q_fwd"), tabs


def _rope_q_bwd(tabs, g):
    return _rope_q_call(g, tabs, True, "rope_q_bwd"), tuple(jnp.zeros_like(t) for t in tabs)


rope_q.defvjp(_rope_q_fwd, _rope_q_bwd)


def _build_k_fwd_call(kv, kr, tabs):
    B, S, _ = kv.shape
    ts = min(S, 512)

    def body(kv_ref, kr_ref, c_ref, a_ref, b_ref, o_ref):
        r = _rot(kr_ref[0], c_ref[0], a_ref[0], b_ref[0]).astype(BF16)
        for h in range(N_HEADS):
            o_ref[0, :, h * QK_PAD:(h + 1) * QK_PAD] = jnp.concatenate(
                [kv_ref[0, :, h * NOPE:(h + 1) * NOPE], r], axis=1)

    tspec = pl.BlockSpec((1, ts, LANE), lambda b, s: (b, s, 0))
    kr_spec = pl.BlockSpec((1, ts, LANE), lambda b, s: (b, s, KR_LANE0 // LANE))
    return pl.pallas_call(
        body, grid=(B, S // ts),
        in_specs=[pl.BlockSpec((1, ts, N_HEADS * NOPE), lambda b, s: (b, s, 0)), kr_spec, tspec, tspec, tspec],
        out_specs=pl.BlockSpec((1, ts, N_HEADS * QK_PAD), lambda b, s: (b, s, 0)),
        out_shape=jax.ShapeDtypeStruct((B, S, N_HEADS * QK_PAD), BF16), name="build_k_fwd",
        compiler_params=_cparams(("parallel", "parallel")))(kv, kr, *tabs)


def _build_k_bwd_call(g, tabs):
    B, S, _ = g.shape
    ts = min(S, 512)

    def body(g_ref, c_ref, a_ref, b_ref, dk_ref, dr_ref):
        tot = None
        for h in range(N_HEADS):
            dk_ref[0, :, h * NOPE:(h + 1) * NOPE] = g_ref[0, :, h * QK_PAD:h * QK_PAD + NOPE]
            part = g_ref[0, :, h * QK_PAD + NOPE:(h + 1) * QK_PAD].astype(F32)
            tot = part if tot is None else tot + part
        dr_ref[0] = _rot_t(tot, c_ref[0], a_ref[0], b_ref[0]).astype(BF16)

    tspec = pl.BlockSpec((1, ts, LANE), lambda b, s: (b, s, 0))
    return pl.pallas_call(
        body, grid=(B, S // ts),
        in_specs=[pl.BlockSpec((1, ts, N_HEADS * QK_PAD), lambda b, s: (b, s, 0)), tspec, tspec, tspec],
        out_specs=[pl.BlockSpec((1, ts, N_HEADS * NOPE), lambda b, s: (b, s, 0)), tspec],
        out_shape=[jax.ShapeDtypeStruct((B, S, N_HEADS * NOPE), BF16), jax.ShapeDtypeStruct((B, S, LANE), BF16)],
        name="build_k_bwd", compiler_params=_cparams(("parallel", "parallel")))(g, *tabs)


@jax.custom_vjp
def build_k(kv, src, stand_in, tabs):
    return _build_k_fwd_call(kv, src, tabs)


def _build_k_fwd(kv, src, stand_in, tabs):
    return _build_k_fwd_call(kv, src, tabs), (tabs, kv.shape, src)


def _build_k_bwd(res, g):
    tabs, kv_shape, src = res
    dk, dr = _build_k_bwd_call(g, tabs)
    dkv = jnp.concatenate([dk, jnp.zeros((kv_shape[0], kv_shape[1], kv_shape[2] - dk.shape[2]), BF16)], axis=-1)
    return dkv, jnp.zeros_like(src), dr, tuple(jnp.zeros_like(t) for t in tabs)


build_k.defvjp(_build_k_fwd, _build_k_bwd)


ATT_SCALE = (NOPE + ROPE) ** -0.5
NEG = -1e30


def _att_tiles(S):
    t = min(S, 512)
    return t, S // t


def _scores(q, k, diagonal):
    s = lax.dot_general(q, k, (((1,), (1,)), ((), ())), preferred_element_type=F32)
    if diagonal:
        row = lax.broadcasted_iota(jnp.int32, s.shape, 0)
        col = lax.broadcasted_iota(jnp.int32, s.shape, 1)
        s = jnp.where(col <= row, s, NEG)
    return s


ATT_HB = 8


def _causal_pairs(n):
    pairs = [(i, j) for i in range(n) for j in range(i + 1)]
    return (jnp.asarray([p[0] for p in pairs], jnp.int32), jnp.asarray([p[1] for p in pairs], jnp.int32))


def _head(ref_or_val, h, w):
    return ref_or_val[:, h * w:(h + 1) * w]


def _attn_fwd_call(q, k, vsrc, v_blk0):
    B, S, _ = q.shape
    t, n = _att_tiles(S)
    qi, kj = _causal_pairs(n)

    def body(qi_ref, kj_ref, q_ref, k_ref, v_ref, o_ref, lse_ref, m_sc, l_sc, acc_sc):
        p_id = pl.program_id(2)
        i, j = qi_ref[p_id], kj_ref[p_id]

        @pl.when(j == 0)
        def _():
            m_sc[...] = jnp.full(m_sc.shape, NEG, F32)
            l_sc[...] = jnp.zeros(l_sc.shape, F32)
            acc_sc[...] = jnp.zeros(acc_sc.shape, F32)

        def step(diagonal):
            qa, ka, va = q_ref[0], k_ref[0], v_ref[0]
            for h in range(ATT_HB):
                lanes = slice(h * LANE, (h + 1) * LANE)
                s = _scores(_head(qa, h, QK_PAD), _head(ka, h, QK_PAD), diagonal)
                m_prev = m_sc[:, lanes]
                m_new = jnp.maximum(m_prev, jnp.max(s, axis=1, keepdims=True))
                alpha = jnp.exp(m_prev - m_new)
                p = jnp.exp(s - jnp.tile(m_new, (1, t // LANE)))
                l_sc[:, lanes] = alpha * l_sc[:, lanes] + jnp.sum(p, axis=1, keepdims=True)
                acc_sc[:, lanes] = alpha * acc_sc[:, lanes] + jnp.dot(p.astype(BF16), _head(va, h, V_DIM),
                                                                      preferred_element_type=F32)
                m_sc[:, lanes] = m_new

        @pl.when(j < i)
        def _():
            step(False)

        @pl.when(j == i)
        def _():
            step(True)
            o_ref[0] = (acc_sc[...] / l_sc[...]).astype(BF16)
            lse_ref[0] = m_sc[...] + jnp.log(l_sc[...])

    wq, wv = ATT_HB * QK_PAD, ATT_HB * V_DIM
    grid_spec = pltpu.PrefetchScalarGridSpec(
        num_scalar_prefetch=2, grid=(B, N_HEADS // ATT_HB, qi.shape[0]),
        in_specs=[pl.BlockSpec((1, t, wq), lambda b, h, p, qi, kj: (b, qi[p], h)),
                  pl.BlockSpec((1, t, wq), lambda b, h, p, qi, kj: (b, kj[p], h)),
                  pl.BlockSpec((1, t, wv), lambda b, h, p, qi, kj: (b, kj[p], v_blk0 + h))],
        out_specs=[pl.BlockSpec((1, t, wv), lambda b, h, p, qi, kj: (b, qi[p], h)),
                   pl.BlockSpec((1, t, wv), lambda b, h, p, qi, kj: (b, qi[p], h))],
        scratch_shapes=[pltpu.VMEM((t, wv), F32), pltpu.VMEM((t, wv), F32), pltpu.VMEM((t, wv), F32)])
    return pl.pallas_call(
        body, grid_spec=grid_spec,
        out_shape=[jax.ShapeDtypeStruct((B, S, N_HEADS * V_DIM), BF16),
                   jax.ShapeDtypeStruct((B, S, N_HEADS * LANE), F32)],
        name="attn_fwd", compiler_params=_cparams(("parallel", "parallel", "arbitrary")))(qi, kj, q, k, vsrc)


def _attn_p_ds(q, k, v, o, do, lse, diagonal, t):
    s = _scores(q, k, diagonal)
    p = jnp.exp(s - jnp.tile(lse, (1, t // LANE)))
    dp = lax.dot_general(do.astype(BF16), v, (((1,), (1,)), ((), ())), preferred_element_type=F32)
    delta = jnp.sum(do.astype(F32) * o.astype(F32), axis=1, keepdims=True)
    ds = p * (dp - delta)
    return p, ds


ATT_HB_BWD = 2


def _attn_bwd_call(q, k, vsrc, o, do, lse):
    B, S, _ = q.shape
    t, n = _att_tiles(S)
    qi, kj = _causal_pairs(n)
    n_pairs = qi.shape[0]
    hb = ATT_HB_BWD
    v_blk0 = N_HEADS // hb

    def body(qi_ref, kj_ref, q_ref, k_ref, v_ref, o_ref, do_ref, lse_ref, dq_ref, dk_ref, dv_ref, dq_sc, dk_sc, dv_sc):
        p_id = pl.program_id(2)
        i, j = qi_ref[p_id], kj_ref[p_id]

        @pl.when(p_id == 0)
        def _():
            dk_sc[...] = jnp.zeros(dk_sc.shape, F32)
            dv_sc[...] = jnp.zeros(dv_sc.shape, F32)

        @pl.when(j == 0)
        def _():
            dq_sc[...] = jnp.zeros(dq_sc.shape, F32)

        rows = pl.ds(pl.multiple_of(j * t, t), t)

        def step(diagonal):
            qa, ka, va, oa, doa, la = q_ref[0], k_ref[0], v_ref[0], o_ref[0], do_ref[0], lse_ref[0]
            for h in range(hb):
                qb, kb, dob = _head(qa, h, QK_PAD), _head(ka, h, QK_PAD), _head(doa, h, V_DIM)
                p, ds = _attn_p_ds(qb, kb, _head(va, h, V_DIM), _head(oa, h, V_DIM), dob, _head(la, h, LANE),
                                   diagonal, t)
                dsb = ds.astype(BF16)
                dq_sc[:, h * QK_PAD:(h + 1) * QK_PAD] += jnp.dot(dsb, kb, preferred_element_type=F32)
                dv_sc[rows, h * V_DIM:(h + 1) * V_DIM] += lax.dot_general(
                    p.astype(BF16), dob.astype(BF16), (((0,), (0,)), ((), ())), preferred_element_type=F32)
                dk_sc[rows, h * QK_PAD:(h + 1) * QK_PAD] += lax.dot_general(
                    dsb, qb, (((0,), (0,)), ((), ())), preferred_element_type=F32)

        @pl.when(j < i)
        def _():
            step(False)

        @pl.when(j == i)
        def _():
            step(True)
            dq_ref[0] = dq_sc[...].astype(BF16)

        @pl.when(p_id == n_pairs - 1)
        def _():
            dk_ref[0] = dk_sc[...].astype(BF16)
            dv_ref[0] = dv_sc[...].astype(BF16)

    wq, wv = hb * QK_PAD, hb * V_DIM
    at_q = lambda b, h, p, qi, kj: (b, qi[p], h)
    at_k = lambda b, h, p, qi, kj: (b, kj[p], h)
    whole = lambda b, h, p, qi, kj: (b, 0, h)
    grid_spec = pltpu.PrefetchScalarGridSpec(
        num_scalar_prefetch=2, grid=(B, N_HEADS // hb, n_pairs),
        in_specs=[pl.BlockSpec((1, t, wq), at_q), pl.BlockSpec((1, t, wq), at_k),
                  pl.BlockSpec((1, t, wv), lambda b, h, p, qi, kj: (b, kj[p], v_blk0 + h)),
                  pl.BlockSpec((1, t, wv), at_q), pl.BlockSpec((1, t, wv), at_q), pl.BlockSpec((1, t, wv), at_q)],
        out_specs=[pl.BlockSpec((1, t, wq), at_q), pl.BlockSpec((1, S, wq), whole), pl.BlockSpec((1, S, wv), whole)],
        scratch_shapes=[pltpu.VMEM((t, wq), F32), pltpu.VMEM((S, wq), F32), pltpu.VMEM((S, wv), F32)])
    return pl.pallas_call(
        body, grid_spec=grid_spec,
        out_shape=[jax.ShapeDtypeStruct((B, S, N_HEADS * QK_PAD), BF16),
                   jax.ShapeDtypeStruct((B, S, N_HEADS * QK_PAD), BF16),
                   jax.ShapeDtypeStruct((B, S, N_HEADS * V_DIM), BF16)],
        name="attn_bwd", compiler_params=_cparams(("parallel", "parallel", "arbitrary")))(
            qi, kj, q, k, vsrc, o, do, lse)


@jax.custom_vjp
def attention(q, k, kv):
    return _attn_fwd_call(q, k, kv, N_HEADS // ATT_HB)[0]


def _attention_fwd(q, k, kv):
    o, lse = _attn_fwd_call(q, k, kv, N_HEADS // ATT_HB)
    return o, (q, k, kv, o, lse)


def _attention_bwd(res, do):
    q, k, kv, o, lse = res
    dq, dk, dv = _attn_bwd_call(q, k, kv, o, do, lse)
    dkv = jnp.concatenate([jnp.zeros_like(dv), dv], axis=-1)
    return dq, dk, dkv


attention.defvjp(_attention_fwd, _attention_bwd)


SUBLANES = 8


def _zero_tail(v):
    return jnp.concatenate([v, jnp.zeros((SUBLANES, v.shape[1]), v.dtype)], axis=0)


def _shift_down(vz, sh):
    return pltpu.roll(vz, sh, 0)[:vz.shape[0] - SUBLANES]


def _shift_up(vz, sh):
    return pltpu.roll(vz, vz.shape[0] - sh, 0)[:vz.shape[0] - SUBLANES]


def _conv_pre(u, uz, w_ref, b_ref):
    acc = b_ref[...] + w_ref[pl.ds(CONV_K - 1, 1), :] * u
    for k in range(CONV_K - 1):
        acc = acc + w_ref[pl.ds(k, 1), :] * _shift_down(uz, CONV_K - 1 - k)
    return acc


def _conv_fwd_call(src, w, b):
    B, S, _ = src.shape
    C = w.shape[1]

    def body(u_ref, w_ref, b_ref, o_ref):
        uu = u_ref[0].astype(F32)
        o_ref[0] = _silu(_conv_pre(uu, _zero_tail(uu), w_ref, b_ref))

    spec = pl.BlockSpec((1, S, LANE), lambda c, bb: (bb, 0, c))
    return pl.pallas_call(
        body, grid=(C // LANE, B),
        in_specs=[pl.BlockSpec((1, S, LANE), lambda c, bb: (bb, 0, c + CONV_LANE0 // LANE)),
                  pl.BlockSpec((CONV_K, LANE), lambda c, bb: (0, c)), pl.BlockSpec((1, LANE), lambda c, bb: (0, c))],
        out_specs=spec, out_shape=jax.ShapeDtypeStruct((B, S, C), F32), name="conv_fwd",
        compiler_params=_cparams(("parallel", "arbitrary")))(src, w, b)


def _conv_bwd_call(src, w, b, g):
    B, S, _ = src.shape
    C = w.shape[1]

    def body(u_ref, w_ref, b_ref, g_ref, du_ref, dw_ref, db_ref):
        uu = u_ref[0].astype(F32)
        uz = _zero_tail(uu)
        pre = _conv_pre(uu, uz, w_ref, b_ref)
        sg = lax.logistic(pre)
        dpre = g_ref[0] * sg * (1.0 + pre * (1.0 - sg))
        dz = _zero_tail(dpre)
        du = w_ref[pl.ds(CONV_K - 1, 1), :] * dpre
        dws = [None] * CONV_K
        dws[CONV_K - 1] = jnp.sum(dpre * uu, axis=0, keepdims=True)
        for k in range(CONV_K - 1):
            sh = CONV_K - 1 - k
            du = du + w_ref[pl.ds(k, 1), :] * _shift_up(dz, sh)
            dws[k] = jnp.sum(dpre * _shift_down(uz, sh), axis=0, keepdims=True)
        du_ref[0] = du.astype(du_ref.dtype)
        dbv = jnp.sum(dpre, axis=0, keepdims=True)
        first = pl.program_id(1) == 0

        @pl.when(first)
        def _():
            for k in range(CONV_K):
                dw_ref[pl.ds(k, 1), :] = dws[k]
            db_ref[...] = dbv

        @pl.when(jnp.logical_not(first))
        def _():
            for k in range(CONV_K):
                dw_ref[pl.ds(k, 1), :] += dws[k]
            db_ref[...] += dbv

    spec = pl.BlockSpec((1, S, LANE), lambda c, bb: (bb, 0, c))
    wspec = pl.BlockSpec((CONV_K, LANE), lambda c, bb: (0, c))
    bspec = pl.BlockSpec((1, LANE), lambda c, bb: (0, c))
    uspec = pl.BlockSpec((1, S, LANE), lambda c, bb: (bb, 0, c + CONV_LANE0 // LANE))
    return pl.pallas_call(
        body, grid=(C // LANE, B), in_specs=[uspec, wspec, bspec, spec], out_specs=[spec, wspec, bspec],
        out_shape=[jax.ShapeDtypeStruct((B, S, C), BF16), jax.ShapeDtypeStruct(w.shape, F32),
                   jax.ShapeDtypeStruct(b.shape, F32)],
        name="conv_bwd", compiler_params=_cparams(("parallel", "arbitrary")))(src, w, b, g)


@jax.custom_vjp
def conv_silu(src, stand_in, w, b):
    return _conv_fwd_call(src, w, b)


def _conv_silu_fwd(src, stand_in, w, b):
    return _conv_fwd_call(src, w, b), (src, w, b)


def _conv_silu_bwd(res, g):
    du, dw, db = _conv_bwd_call(*res, g)
    return jnp.zeros_like(res[0]), du, dw, db


conv_silu.defvjp(_conv_silu_fwd, _conv_silu_bwd)


def _chunk_cumsum_call(a, reverse, name):
    B, S, W = a.shape
    per_step = min(S // CHUNK, 8)

    def body(a_ref, o_ref):
        r = lax.broadcasted_iota(jnp.int32, (CHUNK, CHUNK), 0)
        c = lax.broadcasted_iota(jnp.int32, (CHUNK, CHUNK), 1)
        tri = jnp.where((c >= r) if reverse else (c <= r), 1.0, 0.0).astype(F32)
        for i in range(per_step):
            rows = pl.ds(i * CHUNK, CHUNK)
            o_ref[0, rows, :] = jnp.dot(tri, a_ref[0, rows, :], preferred_element_type=F32,
                                        precision=lax.Precision.HIGHEST)

    spec = pl.BlockSpec((1, per_step * CHUNK, W), lambda b, c: (b, c, 0))
    return pl.pallas_call(body, grid=(B, S // (per_step * CHUNK)), in_specs=[spec], out_specs=spec,
                          out_shape=jax.ShapeDtypeStruct(a.shape, F32), name=name,
                          compiler_params=_cparams(("parallel", "parallel")))(a)


@jax.custom_vjp
def chunk_cumsum(a):
    return _chunk_cumsum_call(a, False, "chunk_cumsum_fwd")


chunk_cumsum.defvjp(lambda a: (_chunk_cumsum_call(a, False, "chunk_cumsum_fwd"), None),
                    lambda _, g: (_chunk_cumsum_call(g, True, "chunk_cumsum_bwd"),))


GROUP_W = 4 * HEAD_P
HPG = SSM_HEADS // SSM_GROUPS


def _ssd_masks():
    lane = lax.broadcasted_iota(jnp.int32, (1, GROUP_W), 1)
    return [((lane >= HEAD_P * j) & (lane < HEAD_P * (j + 1))).astype(F32) for j in range(HPG)]


def _ssd_decays(ac_cols, acr_ref, gi):
    r = lax.broadcasted_iota(jnp.int32, (CHUNK, CHUNK), 0)
    c = lax.broadcasted_iota(jnp.int32, (CHUNK, CHUNK), 1)
    return [jnp.exp(jnp.where(c <= r, ac_cols[j] - acr_ref[0, gi * HPG + j], NEG)) for j in range(HPG)]


def _ssd_cols(blk, g):
    lane = lax.broadcasted_iota(jnp.int32, blk.shape, 1)
    return [jnp.sum(jnp.where(lane == HPG * g + j, blk, 0.0), axis=1, keepdims=True) for j in range(HPG)]


def _ssd_spread(cols):
    lane = lax.broadcasted_iota(jnp.int32, (1, GROUP_W), 1)
    out = jnp.broadcast_to(cols[HPG - 1], (CHUNK, GROUP_W))
    for j in range(HPG - 2, -1, -1):
        out = jnp.where(lane < HEAD_P * (j + 1), cols[j], out)
    return out


def _ssd_gather(val, cols, masks, g):
    lane = lax.broadcasted_iota(jnp.int32, (1, LANE), 1)
    out = jnp.zeros((CHUNK, LANE), F32)
    for j in range(HPG):
        tot = jnp.sum(val * masks[j], axis=1, keepdims=True)
        if cols is not None:
            tot = tot + cols[j]
        out = out + tot * (lane == HPG * g + j).astype(F32)
    return out


def _dot(a, b, dims):
    return lax.dot_general(a.astype(BF16), b.astype(BF16), (dims, ((), ())), preferred_element_type=F32)


NN = ((1,), (0,))
NT = ((1,), (1,))
TN = ((0,), (0,))


XBC_W = GROUP_W + 2 * STATE_N


SSD_STEP_GROUPS_FWD = 8
SSD_STEP_GROUPS_BWD = 2


def _ssd_load(xbc_ref, dt_ref, ac_ref, masks, g, gi):
    x = xbc_ref[0, :, gi * XBC_W:gi * XBC_W + GROUP_W]
    bm = xbc_ref[0, :, gi * XBC_W + GROUP_W:gi * XBC_W + GROUP_W + STATE_N]
    cm = xbc_ref[0, :, gi * XBC_W + GROUP_W + STATE_N:(gi + 1) * XBC_W]
    ac_cols = _ssd_cols(ac_ref[0], g)
    dt = _ssd_spread(_ssd_cols(dt_ref[0], g))
    ac = _ssd_spread(ac_cols)
    is_last = (lax.broadcasted_iota(jnp.int32, (CHUNK, GROUP_W), 0) == CHUNK - 1).astype(F32)
    return x, bm, cm, dt, ac, ac_cols, is_last


def _ssd_in_specs(nc, rev, gb):
    cc = (lambda c: nc - 1 - c) if rev else (lambda c: c)
    return [pl.BlockSpec((1, CHUNK, gb * XBC_W), lambda b, g, c: (b, cc(c), g)),
            pl.BlockSpec((1, CHUNK, LANE), lambda b, g, c: (b, cc(c), 0)),
            pl.BlockSpec((1, CHUNK, LANE), lambda b, g, c: (b, cc(c), 0)),
            pl.BlockSpec((1, gb * HPG, 1, CHUNK), lambda b, g, c: (b, g, 0, cc(c))),
            pl.BlockSpec((1, gb * GROUP_W), lambda b, g, c: (0, g))]


def _ssd_fwd_call(xbc, dtp, acp, acr, dsk):
    B, S, _ = xbc.shape
    nc = S // CHUNK
    gb = SSD_STEP_GROUPS_FWD

    def body(xbc_ref, dt_ref, ac_ref, ar_ref, ds_ref, y_ref, hp_ref, h_sc):
        @pl.when(pl.program_id(2) == 0)
        def _():
            h_sc[...] = jnp.zeros(h_sc.shape, F32)

        masks = _ssd_masks()
        ys = []
        for gi in range(gb):
            grp = gb * pl.program_id(1) + gi
            x, bm, cm, dt, ac, ac_cols, is_last = _ssd_load(xbc_ref, dt_ref, ac_ref, masks, grp, gi)
            last = jnp.sum(ac * is_last, axis=0, keepdims=True)
            decays = _ssd_decays(ac_cols, ar_ref, gi)
            xd = x * dt
            cb = _dot(cm, bm, NT)
            hprev = h_sc[gi]
            hp_ref[0, gi, 0] = hprev
            y = _dot(cm, hprev, NN) * jnp.exp(ac) + ds_ref[:, gi * GROUP_W:(gi + 1) * GROUP_W] * x
            for j in range(HPG):
                y = y + _dot(cb * decays[j], xd * masks[j], NN)
            ys.append(y)
            h_sc[gi] = hprev * jnp.exp(last) + _dot(bm, xd * jnp.exp(last - ac), TN)
        y_ref[0] = jnp.concatenate(ys, axis=1)

    ng = SSM_GROUPS // gb
    return pl.pallas_call(
        body, grid=(B, ng, nc), in_specs=_ssd_in_specs(nc, False, gb),
        out_specs=[pl.BlockSpec((1, CHUNK, gb * GROUP_W), lambda b, g, c: (b, c, g)),
                   pl.BlockSpec((1, gb, 1, STATE_N, GROUP_W), lambda b, g, c: (b, g, c, 0, 0))],
        out_shape=[jax.ShapeDtypeStruct((B, S, D_INNER), F32),
                   jax.ShapeDtypeStruct((B, SSM_GROUPS, nc, STATE_N, GROUP_W), F32)],
        scratch_shapes=[pltpu.VMEM((gb, STATE_N, GROUP_W), F32)], name="ssd_fwd",
        compiler_params=_cparams(("parallel", "parallel", "arbitrary")))(xbc, dtp, acp, acr, dsk)


def _ssd_bwd_call(xbc, dtp, acp, acr, dsk, hps, dy):
    B, S, _ = xbc.shape
    nc = S // CHUNK
    gb = SSD_STEP_GROUPS_BWD

    def body(xbc_ref, dt_ref, ac_ref, ar_ref, ds_ref, hp_ref, dy_ref,
             dxbc_ref, ddt_ref, dac_ref, dar_ref, dds_ref, dh_sc):
        first = pl.program_id(2) == 0

        @pl.when(first)
        def _():
            dh_sc[...] = jnp.zeros(dh_sc.shape, F32)

        masks = _ssd_masks()
        dxbc_parts, dds_parts = [], []
        for gi in range(gb):
            grp = gb * pl.program_id(0) + gi
            x, bm, cm, dt, ac, ac_cols, is_last = _ssd_load(xbc_ref, dt_ref, ac_ref, masks, grp, gi)
            last = jnp.sum(ac * is_last, axis=0, keepdims=True)
            g = dy_ref[0, :, gi * GROUP_W:(gi + 1) * GROUP_W]
            hprev = hp_ref[0, gi, 0]
            dh = dh_sc[gi]
            decays = _ssd_decays(ac_cols, ar_ref, gi)
            dcols = []
            xd = x * dt
            cb = _dot(cm, bm, NT)
            e_c = jnp.exp(ac)
            e_end = jnp.exp(last - ac)
            e_last = jnp.exp(last)
            z = _dot(cm, hprev, NN)
            dz = g * e_c
            dac = g * z * e_c
            dc = _dot(dz, hprev, NT)
            dhprev = _dot(cm, dz, TN) + dh * e_last
            dcb = jnp.zeros((CHUNK, CHUNK), F32)
            dxd = jnp.zeros(xd.shape, F32)
            for j in range(HPG):
                gj = cb * decays[j]
                dgj = _dot(g * masks[j], xd, NT)
                dxd = dxd + _dot(gj, g, TN) * masks[j]
                dcb = dcb + dgj * decays[j]
                dseg = dgj * gj
                dcols.append(jnp.sum(dseg, axis=1, keepdims=True))
                dar_ref[0, gi * HPG + j] = -jnp.sum(dseg, axis=0, keepdims=True)
            dc = dc + _dot(dcb, bm, NN)
            db = _dot(dcb, cm, TN)
            sx = xd * e_end
            db = db + _dot(sx, dh, NT)
            dsx = _dot(bm, dh, NN)
            dxd = dxd + dsx * e_end
            de = dsx * sx
            dac = dac - de
            dlast = jnp.sum(de, axis=0, keepdims=True) + jnp.sum(dh * hprev, axis=0, keepdims=True) * e_last
            dsk = ds_ref[:, gi * GROUP_W:(gi + 1) * GROUP_W]
            dxbc_parts += [dxd * dt + dsk * g, db, dc]
            ddt_ref[0, gi] = _ssd_gather(dxd * x, None, masks, grp)
            dac_ref[0, gi] = _ssd_gather(dac + is_last * dlast, dcols, masks, grp)
            dds_parts.append(jnp.sum(g * x, axis=0, keepdims=True))
            dh_sc[gi] = dhprev
        dxbc_ref[0] = jnp.concatenate(dxbc_parts, axis=1)
        dds = jnp.concatenate(dds_parts, axis=1)
        first_all = first & (pl.program_id(1) == 0)

        @pl.when(first_all)
        def _():
            dds_ref[...] = dds

        @pl.when(jnp.logical_not(first_all))
        def _():
            dds_ref[...] += dds

    rc = lambda c: nc - 1 - c
    ng = SSM_GROUPS // gb
    in_specs = [pl.BlockSpec(s.block_shape, (lambda g, b, c, f=s.index_map: f(b, g, c))) for s in _ssd_in_specs(nc, True, gb)]
    in_specs.append(pl.BlockSpec((1, gb, 1, STATE_N, GROUP_W), lambda g, b, c: (b, g, rc(c), 0, 0)))
    in_specs.append(pl.BlockSpec((1, CHUNK, gb * GROUP_W), lambda g, b, c: (b, rc(c), g)))
    per_group = pl.BlockSpec((1, gb, CHUNK, LANE), lambda g, b, c: (b, g, rc(c), 0))
    out_specs = [pl.BlockSpec((1, CHUNK, gb * XBC_W), lambda g, b, c: (b, rc(c), g)), per_group, per_group,
                 pl.BlockSpec((1, gb * HPG, 1, CHUNK), lambda g, b, c: (b, g, 0, rc(c))),
                 pl.BlockSpec((1, gb * GROUP_W), lambda g, b, c: (0, g))]
    out_shape = [jax.ShapeDtypeStruct(xbc.shape, F32),
                 jax.ShapeDtypeStruct((B, SSM_GROUPS, S, LANE), F32), jax.ShapeDtypeStruct((B, SSM_GROUPS, S, LANE), F32),
                 jax.ShapeDtypeStruct(acr.shape, F32), jax.ShapeDtypeStruct(dsk.shape, F32)]
    return pl.pallas_call(
        body, grid=(ng, B, nc), in_specs=in_specs, out_specs=out_specs, out_shape=out_shape,
        scratch_shapes=[pltpu.VMEM((gb, STATE_N, GROUP_W), F32)], name="ssd_bwd",
        compiler_params=_cparams(("arbitrary", "arbitrary", "arbitrary")))(xbc, dtp, acp, acr, dsk, hps, dy)


@jax.custom_vjp
def ssd(xbc, dtp, acp, acr, dsk):
    return _ssd_fwd_call(xbc, dtp, acp, acr, dsk)[0]


def _ssd_fwd(xbc, dtp, acp, acr, dsk):
    y, hps = _ssd_fwd_call(xbc, dtp, acp, acr, dsk)
    return y, (xbc, dtp, acp, acr, dsk, hps)


def _ssd_bwd(res, dy):
    dxbc, ddt, dac, dacr, dds = _ssd_bwd_call(*res, dy)
    return dxbc, jnp.sum(ddt, axis=1), jnp.sum(dac, axis=1), dacr, dds


ssd.defvjp(_ssd_fwd, _ssd_bwd)


def _pack_small(arrs):
    flat = jnp.concatenate([a.reshape(-1) for a in arrs])
    rows = -(-flat.shape[0] // (8 * LANE)) * 8
    return jnp.pad(flat, (0, rows * LANE - flat.shape[0])).reshape(rows, LANE)


def _unpack_small(buf, shapes):
    flat = buf.reshape(-1)
    out, off = [], 0
    for shp in shapes:
        n = int(np.prod(shp))
        out.append(flat[off:off + n].reshape(shp))
        off += n
    return out


def _rows_tile(rows, cap):
    for cand in range(min(rows, cap), 7, -8):
        if rows % cand == 0:
            return cand
    return rows


def _pair_sum(mine, theirs, cidx, name):
    n4, kk, nn = mine.shape
    half = kk // 2
    tr = _rows_tile(half, 256)
    nb = half // tr

    def body(c_ref, a_ref, b_ref, o_ref, ob_ref):
        tot = a_ref[...] + b_ref[...]
        o_ref[...] = tot
        ob_ref[...] = tot.astype(BF16)

    spec = pl.BlockSpec((1, tr, nn), lambda j, i, c: (j, i, 0))
    grid_spec = pltpu.PrefetchScalarGridSpec(
        num_scalar_prefetch=1, grid=(n4, nb),
        in_specs=[pl.BlockSpec((1, tr, nn), lambda j, i, c: (j, c[0] * nb + i, 0)), spec], out_specs=[spec, spec])
    return pl.pallas_call(
        body, grid_spec=grid_spec,
        out_shape=[jax.ShapeDtypeStruct((n4, half, nn), F32), jax.ShapeDtypeStruct((n4, half, nn), BF16)],
        name=name, compiler_params=_cparams(("parallel", "parallel")))(cidx, mine, theirs)


def _chip_sum(quad, pair, chip_idx, name):
    _, rows, nn = quad.shape
    tr = _rows_tile(rows, 256)

    def body(s_ref, q_ref, p_ref, o_ref):
        for mine in range(4):
            @pl.when(s_ref[0] == mine)
            def _(mine=mine):
                acc = None
                for d in range(4):
                    term = p_ref[0] if d == mine else q_ref[d].astype(F32)
                    acc = term if acc is None else acc + term
                o_ref[...] = acc

    grid_spec = pltpu.PrefetchScalarGridSpec(
        num_scalar_prefetch=1, grid=(rows // tr,),
        in_specs=[pl.BlockSpec((4, tr, nn), lambda i, s: (0, i, 0)), pl.BlockSpec((1, tr, nn), lambda i, s: (s[0], i, 0))],
        out_specs=pl.BlockSpec((tr, nn), lambda i, s: (i, 0)))
    return pl.pallas_call(body, grid_spec=grid_spec, out_shape=jax.ShapeDtypeStruct((rows, nn), F32), name=name,
                          compiler_params=_cparams(("parallel",)))(chip_idx, quad, pair)


def _adam_halves_call(w, mine, other, cidx, m, v, name):
    _, rows, nn = w.shape
    half = rows // 2
    tr = _rows_tile(half, 128)
    nb = half // tr

    def body(c_ref, w_ref, a_ref, b_ref, m_ref, v_ref, g_ref, d_ref, nm_ref, nv_ref):
        upper = (pl.program_id(0) >= nb).astype(jnp.int32)
        g = jnp.where(upper == c_ref[0], a_ref[...], b_ref[...])
        g_ref[0] = g
        d_ref[0], nm_ref[0], nv_ref[0] = _adam_fn(w_ref[0], g, m_ref[0], v_ref[0])

    spec = pl.BlockSpec((1, tr, nn), lambda i, c: (0, i, 0))
    hspec = pl.BlockSpec((tr, nn), lambda i, c: (i % nb, 0))
    grid_spec = pltpu.PrefetchScalarGridSpec(num_scalar_prefetch=1, grid=(2 * nb,),
                                             in_specs=[spec, hspec, hspec, spec, spec], out_specs=[spec] * 4)
    return pl.pallas_call(body, grid_spec=grid_spec, out_shape=[jax.ShapeDtypeStruct(w.shape, F32)] * 4, name=name,
                          compiler_params=_cparams(("parallel",)))(cidx, w, mine, other, m, v)


def _stack_sum(stack, name):
    n, rows, nn = stack.shape
    tr = _rows_tile(rows, 256)

    def body(s_ref, o_ref):
        acc = s_ref[0]
        for d in range(1, n):
            acc = acc + s_ref[d]
        o_ref[...] = acc

    return pl.pallas_call(
        body, grid=(rows // tr,), in_specs=[pl.BlockSpec((n, tr, nn), lambda i: (0, i, 0))],
        out_specs=pl.BlockSpec((tr, nn), lambda i: (i, 0)), out_shape=jax.ShapeDtypeStruct((rows, nn), F32),
        name=name, compiler_params=_cparams(("parallel",)))(stack)


def _adam_call(w, g, m, v, name):
    rows, nn = w.shape
    tr = _rows_tile(rows, 128)

    def body(w_ref, g_ref, m_ref, v_ref, d_ref, nm_ref, nv_ref):
        d_ref[...], nm_ref[...], nv_ref[...] = _adam_fn(w_ref[...], g_ref[...], m_ref[...], v_ref[...])

    spec = pl.BlockSpec((tr, nn), lambda i: (i, 0))
    sds = jax.ShapeDtypeStruct((rows, nn), F32)
    return pl.pallas_call(body, grid=(rows // tr,), in_specs=[spec] * 4, out_specs=[spec] * 3,
                          out_shape=[sds] * 3, name=name, compiler_params=_cparams(("parallel",)))(w, g, m, v)


def _adam_fn(w, g, m, v):
    m = ADAM_B1 * m + (1.0 - ADAM_B1) * g
    v = ADAM_B2 * v + (1.0 - ADAM_B2) * (g * g)
    m_hat = m / (1.0 - ADAM_B1 ** ADAM_STEP)
    v_hat = v / (1.0 - ADAM_B2 ** ADAM_STEP)
    delta = -ADAM_LR * (m_hat / (jnp.sqrt(v_hat) + ADAM_EPS) + ADAM_WD * w)
    return delta, m, v


def _mesh_pos():
    return lax.axis_index("x"), lax.axis_index("y"), lax.axis_index("c")


def _other_chips(x, y):
    return [(1 - x, y), (x, 1 - y), (1 - x, 1 - y)]


HBM_SPEC = pl.BlockSpec(memory_space=pl.ANY)


def _remote(src, dst, send_sems, recv_sems, k, to):
    return pltpu.make_async_remote_copy(src_ref=src, dst_ref=dst, send_sem=send_sems.at[k], recv_sem=recv_sems.at[k],
                                        device_id=to, device_id_type=MESH)


def _half_rows(c, rows, align):
    half = rows // 2
    return (pl.ds(pl.multiple_of(c * half, align), half), pl.ds(pl.multiple_of((1 - c) * half, align), half))


def _gather_weights(mats, conv):
    n = len(mats)

    def body(*refs):
        ins, conv_in = refs[:n], refs[n]
        outs, conv_out = refs[n + 1:2 * n + 1], refs[2 * n + 1]
        send_sems, recv_sems, local_sem = refs[2 * n + 2:]
        x, y, c = _mesh_pos()
        me, sibling, s = (x, y, c), (x, y, 1 - c), 2 * x + y
        chips = _other_chips(x, y)
        rows = [_half_rows(c, m.shape[0], 16) for m in mats]
        own = pltpu.make_async_copy(conv_in, conv_out.at[s], local_sem)
        own.start()
        sent = []
        for i in range(n):
            mine = rows[i][0]
            for j, (cx, cy) in enumerate(chips):
                sent.append(_remote(ins[i].at[mine], outs[i].at[s, mine], send_sems, recv_sems, 6 * i + j, (cx, cy, c)))
        for j, (cx, cy) in enumerate(chips):
            sent.append(_remote(conv_in, conv_out.at[s], send_sems, recv_sems, 6 * n + j, (cx, cy, c)))
        for cp in sent:
            cp.start()
        for i in range(n):
            mine = rows[i][0]
            for j, (cx, cy) in enumerate(chips):
                landed = outs[i].at[2 * cx + cy, mine]
                _remote(landed, landed, send_sems, recv_sems, 6 * i + j, me).wait_recv()
                fwd = _remote(landed, landed, send_sems, recv_sems, 6 * i + 3 + j, sibling)
                fwd.start()
                sent.append(fwd)
        for j, (cx, cy) in enumerate(chips):
            slot = conv_out.at[2 * cx + cy]
            _remote(slot, slot, send_sems, recv_sems, 6 * n + j, me).wait_recv()
        for i in range(n):
            theirs_rows = rows[i][1]
            for j, (cx, cy) in enumerate(chips):
                theirs = outs[i].at[2 * cx + cy, theirs_rows]
                _remote(theirs, theirs, send_sems, recv_sems, 6 * i + 3 + j, me).wait_recv()
        for cp in sent:
            cp.wait_send()
        own.wait()

    out_shape = [jax.ShapeDtypeStruct((4,) + m.shape, m.dtype) for m in mats]
    out_shape.append(jax.ShapeDtypeStruct((4,) + conv.shape, conv.dtype))
    res = pl.pallas_call(
        body, in_specs=[HBM_SPEC] * (n + 1), out_specs=[HBM_SPEC] * (n + 1), out_shape=out_shape,
        scratch_shapes=[pltpu.SemaphoreType.DMA((6 * n + 3,)), pltpu.SemaphoreType.DMA((6 * n + 3,)),
                        pltpu.SemaphoreType.DMA],
        name="all_gather_weights")(*mats, conv)
    return res[:n], res[n]


def _sibling_exchange(stacks):
    n = len(stacks)

    def body(*refs):
        ins, outs = refs[:n], refs[n:2 * n]
        send_sems, recv_sems = refs[2 * n:]
        x, y, c = _mesh_pos()
        cps = []
        for i in range(n):
            theirs = _half_rows(c, stacks[i].shape[1], 8)[1]
            cps.append(_remote(ins[i].at[:, theirs, :], outs[i], send_sems, recv_sems, i, (x, y, 1 - c)))
        for cp in cps:
            cp.start()
        for cp in cps:
            cp.wait()

    out_shape = [jax.ShapeDtypeStruct((4, s.shape[1] // 2, s.shape[2]), s.dtype) for s in stacks]
    return pl.pallas_call(
        body, in_specs=[HBM_SPEC] * n, out_specs=[HBM_SPEC] * n, out_shape=out_shape,
        scratch_shapes=[pltpu.SemaphoreType.DMA((n,)), pltpu.SemaphoreType.DMA((n,))],
        name="grad_sibling_exchange")(*stacks)


def _chip_exchange(parts):
    n = len(parts)

    def body(*refs):
        ins, outs = refs[:n], refs[n:2 * n]
        send_sems, recv_sems = refs[2 * n:]
        x, y, c = _mesh_pos()
        me, s = (x, y, c), 2 * x + y
        chips = _other_chips(x, y)
        sent = [_remote(ins[i].at[2 * cx + cy], outs[i].at[s], send_sems, recv_sems, 3 * i + j, (cx, cy, c))
                for i in range(n) for j, (cx, cy) in enumerate(chips)]
        for cp in sent:
            cp.start()
        for i in range(n):
            for j, (cx, cy) in enumerate(chips):
                slot = outs[i].at[2 * cx + cy]
                _remote(slot, slot, send_sems, recv_sems, 3 * i + j, me).wait_recv()
        for cp in sent:
            cp.wait_send()

    return pl.pallas_call(
        body, in_specs=[HBM_SPEC] * n, out_specs=[HBM_SPEC] * n,
        out_shape=[jax.ShapeDtypeStruct(p.shape, p.dtype) for p in parts],
        scratch_shapes=[pltpu.SemaphoreType.DMA((3 * n,)), pltpu.SemaphoreType.DMA((3 * n,))],
        name="grad_chip_exchange")(*parts)


def _sibling_swap(halves):
    n = len(halves)

    def body(*refs):
        ins, outs = refs[:n], refs[n:2 * n]
        send_sems, recv_sems = refs[2 * n:]
        x, y, c = _mesh_pos()
        cps = [_remote(ins[i], outs[i], send_sems, recv_sems, i, (x, y, 1 - c)) for i in range(n)]
        for cp in cps:
            cp.start()
        for cp in cps:
            cp.wait()

    return pl.pallas_call(
        body, in_specs=[HBM_SPEC] * n, out_specs=[HBM_SPEC] * n,
        out_shape=[jax.ShapeDtypeStruct(h.shape, h.dtype) for h in halves],
        scratch_shapes=[pltpu.SemaphoreType.DMA((n,)), pltpu.SemaphoreType.DMA((n,))],
        name="grad_sibling_swap")(*halves)


def _gather_small(vec):
    def body(in_ref, out_ref, send_sems, recv_sems, local_sem):
        x, y, c = _mesh_pos()
        me = (x, y, c)
        own = pltpu.make_async_copy(in_ref, out_ref.at[4 * x + 2 * y + c], local_sem)
        own.start()
        peers = [(1 - x if k & 4 else x, 1 - y if k & 2 else y, 1 - c if k & 1 else c) for k in range(1, 8)]
        sent = [_remote(in_ref, out_ref.at[4 * x + 2 * y + c], send_sems, recv_sems, k, p) for k, p in enumerate(peers)]
        for cp in sent:
            cp.start()
        for k, (px, py, pc) in enumerate(peers):
            slot = out_ref.at[4 * px + 2 * py + pc]
            _remote(slot, slot, send_sems, recv_sems, k, me).wait_recv()
        for cp in sent:
            cp.wait_send()
        own.wait()

    return pl.pallas_call(
        body, in_specs=[HBM_SPEC], out_specs=HBM_SPEC, out_shape=jax.ShapeDtypeStruct((8,) + vec.shape, vec.dtype),
        scratch_shapes=[pltpu.SemaphoreType.DMA((7,)), pltpu.SemaphoreType.DMA((7,)), pltpu.SemaphoreType.DMA],
        name="grad_gather_small")(vec)


def _reduce_matrices(stacks, names):
    cidx = lax.axis_index("c").astype(jnp.int32).reshape(1)
    chip = (2 * lax.axis_index("x") + lax.axis_index("y")).astype(jnp.int32).reshape(1)
    got = _sibling_exchange(stacks)
    pairs = [_pair_sum(a, b, cidx, "grad_pair_sum_" + nm) for a, b, nm in zip(stacks, got, names)]
    quads = _chip_exchange([p[1] for p in pairs])
    mine = [_chip_sum(q, p[0], chip, "grad_chip_sum_" + nm) for q, p, nm in zip(quads, pairs, names)]
    return mine, _sibling_swap(mine)


def _pad_cols(a, n):
    return jnp.concatenate([a, jnp.zeros((a.shape[0], n - a.shape[1]), a.dtype)], axis=1)


def _group_channels(a):
    lead = a.shape[:-1]
    xs = a[..., :D_INNER].reshape(lead + (SSM_GROUPS, GROUP_W))
    bs = a[..., D_INNER:D_INNER + SSM_GROUPS * STATE_N].reshape(lead + (SSM_GROUPS, STATE_N))
    cs = a[..., D_INNER + SSM_GROUPS * STATE_N:].reshape(lead + (SSM_GROUPS, STATE_N))
    return jnp.concatenate([xs, bs, cs], axis=-1).reshape(lead + (CONV_CH,))


PROJ_SEGS = (('gate_a', D_MODEL), ('gate_b', D_MODEL), ('z', D_INNER), ('xbc', CONV_CH), ('q_lat', Q_RANK),
             ('kv_lat', KV_RANK), ('k_rope', LANE), ('dt', LANE))
PROJ_WIDE = sum(w for _, w in PROJ_SEGS[:4])
PROJ_LANE0 = {n: (v if v < PROJ_WIDE else v - PROJ_WIDE) for n, v in
              zip([n for n, _ in PROJ_SEGS], [int(v) for v in np.cumsum([0] + [w for _, w in PROJ_SEGS])[:-1]])}
CONV_LANE0 = PROJ_LANE0['xbc']
KR_LANE0 = PROJ_LANE0['k_rope']


def _lay_w_in(w):
    idx = np.cumsum(IN_SIZES)[:-1]
    q_lat, kv_lat, k_rope, z, xbc, dt, gate_a, gate_b = jnp.split(w, [int(v) for v in idx], axis=1)
    return jnp.concatenate([gate_a, gate_b, z, _group_channels(xbc), q_lat, kv_lat, _pad_cols(k_rope, LANE),
                            _pad_cols(dt, LANE)], axis=1)


@jax.custom_vjp
def project(h, w, tok):
    return _project_impl(h, w)


def _project_impl(h, w):
    return (_mm(h, w[:, :PROJ_WIDE], "w_in_fwd", BF16), _mm(h, w[:, PROJ_WIDE:], "w_in_narrow_fwd")) + tuple(
        jnp.zeros((h.shape[0], wd), BF16) for _, wd in PROJ_SEGS)


def _project_fwd(h, w, tok):
    return _project_impl(h, w), (h, w)


def _project_bwd(res, cots):
    h, w = res
    g = jnp.concatenate(cots[2:], axis=1)
    return _mm(g, w.T, "w_in_dx", h.dtype), jnp.zeros_like(w), _mm(h.T, g, "w_in_dw")


project.defvjp(_project_fwd, _project_bwd)


def _lay_w_uq(w):
    w3 = w.reshape(Q_RANK, N_HEADS, NOPE + ROPE)
    w3 = jnp.concatenate([w3, jnp.zeros((Q_RANK, N_HEADS, QK_PAD - NOPE - ROPE), w.dtype)], axis=2)
    return w3.reshape(Q_RANK, N_HEADS * QK_PAD)


def _lay_w_ukv(w):
    w3 = w.reshape(KV_RANK, N_HEADS, NOPE + V_DIM)
    return jnp.concatenate([w3[:, :, :NOPE].reshape(KV_RANK, -1), w3[:, :, NOPE:].reshape(KV_RANK, -1)], axis=1)


def _pad_lanes(v, n=LANE):
    return jnp.concatenate([v, jnp.zeros((v.shape[0], n - v.shape[1]), v.dtype)], axis=1)


def _local_loss(toks, small, x, wb, c8, posf, target):
    B, S, D = x.shape
    T = B * S

    def lin(name, a, key, lay=lambda w: w, out_dtype=F32):
        return make_linear(name, out_dtype)(a, lay(wb[key]), lay(toks[key]))

    rows2 = lambda a: a.reshape(T, a.shape[-1])
    rows3 = lambda a: a.reshape(B, S, a.shape[-1])

    sc = make_rowwise("silu_c", _f_silu, 1, 0, 0, ('row',))((c8[None],), (), ())[0][0]
    mod = make_linear("ada", F32, 4)(sc, wb['w_ada'], toks['w_ada'])[:B] + small['b_ada']
    shift1, scale1, gate1, shift2, scale2, gate2 = [m[:, None, :] for m in jnp.split(mod, 6, axis=-1)]

    h, x_res = make_rowwise("modulate1", _f_modulate, 1, 2, 1, ('row',), forward_row=0)(
        (x,), (scale1, shift1), (small['g_pre_mix'],))
    outs = project(rows2(h), _lay_w_in(wb['w_in']), _lay_w_in(toks['w_in']))
    wide = lax.stop_gradient(rows3(outs[0]))
    proj = lax.stop_gradient(rows3(outs[1]))
    stand = {n: rows3(o) for (n, _), o in zip(PROJ_SEGS, outs[2:])}

    def win(seg, block):
        return (PROJ_LANE0[seg] // block, dict(PROJ_SEGS)[seg])

    inv = ROPE_THETA ** (-jnp.arange(ROPE // 2, dtype=F32) / (ROPE // 2))
    inv_lane = jnp.concatenate([inv, inv, jnp.zeros((LANE - ROPE,), F32)])[None]
    tabs = tuple(_rope_tables(posf, inv_lane))
    qn = make_rowwise("rms_q", _f_rms, 1, 0, 1, ('row',), windows={0: win('q_lat', Q_RANK)})(
        (proj,), (), (small['g_q_lat'],), (stand['q_lat'],))[0]
    kvn = make_rowwise("rms_kv", _f_rms, 1, 0, 1, ('row',), windows={0: win('kv_lat', KV_RANK)})(
        (proj,), (), (small['g_kv_lat'],), (stand['kv_lat'],))[0]
    qp = rows3(lin("w_uq", rows2(qn), 'w_uq', _lay_w_uq, BF16))
    kvp = rows3(lin("w_ukv", rows2(kvn), 'w_ukv', _lay_w_ukv, BF16))
    qr = rope_q(qp, tabs)
    kr = build_k(kvp, proj, stand['k_rope'], tabs)
    att = attention(qr, kr, kvp)
    attn = rows3(lin("w_o_attn", rows2(att), 'w_o_attn', out_dtype=BF16))

    xa = conv_silu(wide, stand['xbc'], _group_channels(wb['conv_w_f32']), _group_channels(small['conv_b']))
    dt_pad, a_pad = make_rowwise("dt_softplus", _f_dt, 1, 0, 2, ('row', 'row'), windows={0: win('dt', LANE)})(
        (proj,), (), (_pad_lanes(small['dt_bias']), _pad_lanes(small['a_log'])), (stand['dt'],))
    ac_pad = chunk_cumsum(a_pad)
    acr = jnp.transpose(ac_pad[..., :SSM_HEADS], (0, 2, 1))[:, :, None, :]
    dsk = jnp.repeat(small['d_skip'], HEAD_P, axis=-1)
    y = ssd(xa, dt_pad, ac_pad, acr, dsk)
    yg = make_rowwise("gated_norm", _f_gated_norm, 2, 0, 1, ('row',), ncol=SSM_GROUPS, ts_cap=2048,
                      windows={1: win('z', GROUP_W)})((y, wide), (), (small['g_ssm_out'],), (stand['z'],))[0]
    ssm = rows3(lin("w_o_ssm", rows2(yg), 'w_o_ssm', out_dtype=BF16))

    merged = make_rowwise("merge", _f_merge, 4, 0, 0, ('row',),
                          windows={2: win('gate_a', D_MODEL), 3: win('gate_b', D_MODEL)})(
        (attn, ssm, wide, wide), (), (), (stand['gate_a'], stand['gate_b']))[0]
    mix = rows3(lin("w_out", rows2(merged), 'w_out', out_dtype=BF16))
    x1 = make_rowwise("post_mix", _f_post, 2, 1, 1, ('row',))((x_res, mix), (gate1,), (small['g_post_mix'],))[0]

    h2, x1_res = make_rowwise("modulate2", _f_modulate, 1, 2, 1, ('row',), forward_row=0)(
        (x1,), (scale2, shift2), (small['g_pre_mlp'],))
    ff = rows3(ffn(rows2(h2), wb['w_ff1'], toks['w_ff1'], wb['w_ff2'], toks['w_ff2']))
    lvec = make_rowwise("final_loss", _f_final_loss, 3, 1, 1, ('sum',), nodiff=(2,))(
        (x1_res, ff, target), (gate2,), (small['g_post_mlp'],))[0]
    return jnp.sum(lvec)


MATRICES = COL_SHARDED + ROW_SHARDED
STACKED_DW = ('w_ada', 'w_ff1')


def _local_step(x, c, positions, target, wb, small):
    B = x.shape[0]
    c8 = jnp.concatenate([c, jnp.zeros((16 - B, c.shape[1]), F32)], axis=0)
    posf = positions.astype(F32)[..., None]
    toks = {k: jnp.zeros(wb[k].shape, F32) for k in MATRICES if k != 'conv_w'}
    for k in STACKED_DW:
        rows, cols = wb[k].shape
        toks[k] = jnp.zeros((4, rows, cols // 4), F32)
    conv_w = wb['conv_w_f32']

    def loss_fn(toks, small, conv_w, x):
        wbl = dict(wb)
        wbl['conv_w_f32'] = conv_w
        return _local_loss(toks, small, x, wbl, c8, posf, target)

    loss, (g_tok, g_small, g_conv, g_x) = jax.value_and_grad(loss_fn, argnums=(0, 1, 2, 3))(toks, small, conv_w, x)
    grads = dict(g_tok)
    grads.update(g_small)
    grads['conv_w'] = g_conv
    return loss, g_x, grads


def kernel(x, c, positions, w_ada, b_ada, g_pre_mix, g_post_mix, w_in, g_q_lat, g_kv_lat, w_uq, w_ukv, w_o_attn, conv_w, conv_b, dt_bias, a_log, d_skip, g_ssm_out, w_o_ssm, w_out, g_pre_mlp, g_post_mlp, w_ff1, w_ff2, loss_target, m_w_ada, m_b_ada, m_g_pre_mix, m_g_post_mix, m_w_in, m_g_q_lat, m_g_kv_lat, m_w_uq, m_w_ukv, m_w_o_attn, m_conv_w, m_conv_b, m_dt_bias, m_a_log, m_d_skip, m_g_ssm_out, m_w_o_ssm, m_w_out, m_g_pre_mlp, m_g_post_mlp, m_w_ff1, m_w_ff2, v_w_ada, v_b_ada, v_g_pre_mix, v_g_post_mix, v_w_in, v_g_q_lat, v_g_kv_lat, v_w_uq, v_w_ukv, v_w_o_attn, v_conv_w, v_conv_b, v_dt_bias, v_a_log, v_d_skip, v_g_ssm_out, v_w_o_ssm, v_w_out, v_g_pre_mlp, v_g_post_mlp, v_w_ff1, v_w_ff2):
    given = dict(locals())
    w_loc = {n: given[n] for n in WEIGHTS}
    m_loc = {n: given["m_" + n] for n in WEIGHTS}
    v_loc = {n: given["v_" + n] for n in WEIGHTS}
    mats = [n for n in WEIGHTS if n in MATRICES and n != 'conv_w']
    vecs = [n for n in WEIGHTS if n not in MATRICES]

    own = [w_loc[n][0].astype(BF16) for n in mats]
    g_mats, g_conv = _gather_weights(own, conv_w[0])
    chip = 2 * lax.axis_index("x") + lax.axis_index("y")
    wb = {}
    for n, g, mine in zip(mats, g_mats, own):
        g = lax.dynamic_update_slice_in_dim(g, mine[None], chip, axis=0)
        if n in COL_SHARDED:
            wb[n] = jnp.transpose(g, (1, 0, 2)).reshape(g.shape[1], -1)
        else:
            wb[n] = g.reshape(-1, g.shape[2])
    wb['conv_w_f32'] = jnp.transpose(g_conv, (1, 0, 2)).reshape(CONV_K, -1)
    small = {n: w_loc[n] for n in vecs}

    loss_part, grad_x, grads = _local_step(x, c, positions, loss_target, wb, small)
    loss = lax.psum(loss_part, ("x", "y", "c"))

    stacks = []
    for n in mats:
        kk, nn = w_loc[n].shape[1:]
        if n in STACKED_DW:
            stacks.append(grads[n])
        elif n in COL_SHARDED:
            stacks.append(jnp.transpose(grads[n].reshape(kk, 4, nn), (1, 0, 2)))
        else:
            stacks.append(grads[n].reshape(4, kk, nn))
    g_mine, g_other = _reduce_matrices(stacks, mats)
    g_shard = {}

    vec_shapes = [tuple(grads[n].shape) for n in vecs] + [tuple(grads['conv_w'].shape)]
    total = _stack_sum(_gather_small(_pack_small([grads[n] for n in vecs] + [grads['conv_w']])), "grad_sum_small")
    g_vec = _unpack_small(total, vec_shapes)
    n_conv = conv_w.shape[2]
    chip = 2 * lax.axis_index("x") + lax.axis_index("y")
    g_shard['conv_w'] = lax.dynamic_slice_in_dim(g_vec[-1], chip * n_conv, n_conv, axis=1)
    for n, g in zip(vecs, g_vec):
        g_shard[n] = g

    delta, new_m, new_v = {}, {}, {}
    cidx = lax.axis_index("c").astype(jnp.int32).reshape(1)
    for n, mine, other in zip(mats, g_mine, g_other):
        g_shard[n], delta[n], new_m[n], new_v[n] = _adam_halves_call(
            w_loc[n], mine, other, cidx, m_loc[n], v_loc[n], "adamw_" + n)
    rest = vecs + ['conv_w']
    rest_shapes = [tuple(w_loc[n].shape) for n in rest]
    packed = [_pack_small([src[n] for n in rest]) for src in (w_loc, g_shard, m_loc, v_loc)]
    for dst, buf in zip((delta, new_m, new_v), _adam_call(*packed, "adamw_small")):
        dst.update(zip(rest, _unpack_small(buf, rest_shapes)))

    def out(d):
        return [d[n].reshape(w_loc[n].shape) for n in WEIGHTS]

    return (loss, grad_x, *out(g_shard), *out(delta), *out(new_m), *out(new_v))
```

```python
import functools
import math

import numpy as np
import jax
import jax.numpy as jnp
from jax import lax
from jax.experimental import pallas as pl
from jax.experimental.pallas import tpu as pltpu

F32 = jnp.float32
BF16 = jnp.bfloat16
MESH = pl.DeviceIdType.MESH

D_MODEL = 1024
N_HEADS = 8
NOPE = 128
ROPE = 64
V_DIM = 128
Q_RANK = 256
KV_RANK = 256
ROPE_THETA = 10000.0
D_INNER = 2048
SSM_HEADS = 32
SSM_GROUPS = 8
HEAD_P = 64
STATE_N = 128
CONV_K = 4
CHUNK = 128
CONV_CH = D_INNER + 2 * SSM_GROUPS * STATE_N
D_FF = 4096
EPS = 1e-6
IN_SIZES = (Q_RANK, KV_RANK, ROPE, D_INNER, CONV_CH, SSM_HEADS, D_MODEL, D_MODEL)
ADAM_LR, ADAM_B1, ADAM_B2, ADAM_EPS, ADAM_WD, ADAM_STEP = 0.001, 0.9, 0.999, 1e-08, 0.01, 10

VMEM_LIMIT_BYTES = 52 * 1024 * 1024
LANE = 128
QK_PAD = 256

WEIGHTS = ['w_ada', 'b_ada', 'g_pre_mix', 'g_post_mix', 'w_in', 'g_q_lat', 'g_kv_lat', 'w_uq', 'w_ukv',
           'w_o_attn', 'conv_w', 'conv_b', 'dt_bias', 'a_log', 'd_skip', 'g_ssm_out', 'w_o_ssm', 'w_out',
           'g_pre_mlp', 'g_post_mlp', 'w_ff1', 'w_ff2']
COL_SHARDED = ('w_ada', 'w_in', 'w_uq', 'w_ukv', 'conv_w', 'w_ff1')
ROW_SHARDED = ('w_o_attn', 'w_o_ssm', 'w_out', 'w_ff2')


def _cparams(sem):
    return pltpu.CompilerParams(dimension_semantics=sem, vmem_limit_bytes=VMEM_LIMIT_BYTES)


def _tile(n, cap):
    if n <= cap:
        return n
    k = n // LANE
    best = LANE
    for d in range(1, k + 1):
        if k % d == 0 and d * LANE <= cap:
            best = d * LANE
    return best


def _mm(a, w, name, out_dtype=F32, epilogue=None, extras=(), out_dtypes=None):
    M, K = a.shape
    N = w.shape[1]
    tm = min(M, 1024)
    tn = _tile(N, 1024)
    tk = _tile(K, 2048)
    nk = K // tk
    dts = tuple(out_dtypes) if epilogue is not None else (out_dtype,)
    n_x, n_o = len(extras), len(dts)

    def finish(acc, refs):
        res = epilogue(acc, *[r[...] for r in refs[:n_x]]) if epilogue is not None else (acc,)
        for o_ref, val, dt in zip(refs[n_x:n_x + n_o], res, dts):
            o_ref[...] = val.astype(dt)

    def body(a_ref, w_ref, *refs):
        part = jnp.dot(a_ref[...].astype(BF16), w_ref[...], preferred_element_type=F32)
        if nk == 1:
            finish(part, refs)
        else:
            acc_ref = refs[-1]
            k = pl.program_id(2)

            @pl.when(k == 0)
            def _():
                acc_ref[...] = part

            @pl.when(k > 0)
            def _():
                acc_ref[...] += part

            @pl.when(k == nk - 1)
            def _():
                finish(acc_ref[...], refs)

    ospec = pl.BlockSpec((tm, tn), lambda i, j, k: (i, j))
    res = pl.pallas_call(
        body, grid=(M // tm, N // tn, nk),
        in_specs=[pl.BlockSpec((tm, tk), lambda i, j, k: (i, k)), pl.BlockSpec((tk, tn), lambda i, j, k: (k, j))]
        + [ospec] * n_x,
        out_specs=[ospec] * n_o, out_shape=[jax.ShapeDtypeStruct((M, N), dt) for dt in dts],
        scratch_shapes=[pltpu.VMEM((tm, tn), F32)] if nk > 1 else [], name=name,
        compiler_params=_cparams(("parallel", "parallel", "arbitrary")))(a, w, *extras)
    return res if epilogue is not None else res[0]


def _mm_tn(a, g, name, col_shards=1):
    M, K = a.shape
    N = g.shape[1]
    tm = min(M, 1024)
    tk = _tile(K, 1024)
    tn = _tile(N // col_shards, 1024)
    nm = M // tm
    per = N // col_shards // tn

    def body(a_ref, g_ref, o_ref):
        part = lax.dot_general(a_ref[...].astype(BF16), g_ref[...].astype(BF16), (((0,), (0,)), ((), ())),
                               preferred_element_type=F32)
        m = pl.program_id(2)

        @pl.when(m == 0)
        def _():
            o_ref[...] = part.reshape(o_ref.shape)

        @pl.when(m > 0)
        def _():
            o_ref[...] += part.reshape(o_ref.shape)

    if col_shards == 1:
        out_spec = pl.BlockSpec((tk, tn), lambda i, j, m: (i, j))
        out_shape = jax.ShapeDtypeStruct((K, N), F32)
    else:
        out_spec = pl.BlockSpec((1, tk, tn), lambda i, j, m: (j // per, i, j % per))
        out_shape = jax.ShapeDtypeStruct((col_shards, K, N // col_shards), F32)
    return pl.pallas_call(
        body, grid=(K // tk, N // tn, nm),
        in_specs=[pl.BlockSpec((tm, tk), lambda i, j, m: (m, i)), pl.BlockSpec((tm, tn), lambda i, j, m: (m, j))],
        out_specs=out_spec, out_shape=out_shape, name=name,
        compiler_params=_cparams(("parallel", "parallel", "arbitrary")))(a, g)


def make_linear(name, out_dtype=F32, dw_col_shards=1):
    @jax.custom_vjp
    def linear(a, w, tok):
        return _mm(a, w, name + "_fwd", out_dtype)

    def fwd(a, w, tok):
        return _mm(a, w, name + "_fwd", out_dtype), (a, w)

    def bwd(res, g):
        a, w = res
        da = _mm(g, w.T, name + "_dx", a.dtype)
        dw = _mm_tn(a, g, name + "_dw", dw_col_shards)
        return da, jnp.zeros_like(w), dw

    linear.defvjp(fwd, bwd)
    return linear


def _relu2_epilogue(acc):
    r = jnp.maximum(acc, 0.0)
    return r * r, r


def _relu2_bwd_epilogue(acc, r):
    return (acc * (2.0 * r.astype(F32)),)


@jax.custom_vjp
def ffn(h, w1, tok1, w2, tok2):
    act, _ = _mm(h, w1, "w_ff1_fwd", epilogue=_relu2_epilogue, out_dtypes=(BF16, BF16))
    return _mm(act, w2, "w_ff2_fwd", BF16)


def _ffn_fwd(h, w1, tok1, w2, tok2):
    act, r = _mm(h, w1, "w_ff1_fwd", epilogue=_relu2_epilogue, out_dtypes=(BF16, BF16))
    return _mm(act, w2, "w_ff2_fwd", BF16), (h, w1, w2, act, r)


def _ffn_bwd(res, g):
    h, w1, w2, act, r = res
    du = _mm(g, w2.T, "w_ff2_dx", epilogue=_relu2_bwd_epilogue, extras=(r,), out_dtypes=(BF16,))[0]
    dw2 = _mm_tn(act, g, "w_ff2_dw")
    dw1 = _mm_tn(h, du, "w_ff1_dw", 4)
    dh = _mm(du, w1.T, "w_ff1_dx", h.dtype)
    return dh, jnp.zeros_like(w1), dw1, jnp.zeros_like(w2), dw2


ffn.defvjp(_ffn_fwd, _ffn_bwd)


def make_rowwise(name, f, n_rows, n_seqs, n_pars, out_kinds, ncol=1, nodiff=(), ts_cap=512, windows=None,
                 forward_row=None):
    windows = dict(windows or {})
    n_in = n_rows + n_seqs + n_pars
    diff_idx = [i for i in range(n_in) if i not in nodiff]

    def _dims(rows):
        B, S = rows[0].shape[0], rows[0].shape[1]
        ts = min(S, ts_cap)
        return B, S, ts

    def _width(i, r):
        return windows[i][1] if i in windows else r.shape[2]

    def _in_specs(rows, seqs, pars, ts):
        specs = []
        for i, r in enumerate(rows):
            col0 = windows[i][0] if i in windows else 0
            specs.append(pl.BlockSpec((1, ts, _width(i, r) // ncol), lambda k, b, s, col0=col0: (b, s, k + col0)))
        for q in seqs:
            specs.append(pl.BlockSpec((1, 1, q.shape[2] // ncol), lambda k, b, s: (b, 0, k)))
        for p in pars:
            specs.append(pl.BlockSpec((1, p.shape[1] // ncol), lambda k, b, s: (0, k)))
        return specs

    def _load(refs):
        vals = [r[0] for r in refs[:n_rows + n_seqs]]
        vals += [r[...] for r in refs[n_rows + n_seqs:n_in]]
        return vals

    def _out_struct(rows, seqs, pars, ts):
        blocks = [jax.ShapeDtypeStruct((ts, _width(i, r) // ncol), r.dtype) for i, r in enumerate(rows)]
        blocks += [jax.ShapeDtypeStruct((1, q.shape[2] // ncol), q.dtype) for q in seqs]
        blocks += [jax.ShapeDtypeStruct((1, p.shape[1] // ncol), p.dtype) for p in pars]
        return jax.eval_shape(f, *blocks)

    def _fwd_call(rows, seqs, pars):
        B, S, ts = _dims(rows)
        outs = _out_struct(rows, seqs, pars, ts)
        n_out = len(outs)

        def body(*refs):
            res = f(*_load(refs))
            first = (pl.program_id(1) == 0) & (pl.program_id(2) == 0)
            for o_ref, val, kind in zip(refs[n_in:], res, out_kinds):
                if kind == 'row':
                    o_ref[0] = val
                else:
                    tot = jnp.sum(val, axis=0, keepdims=True)

                    @pl.when(first)
                    def _(o_ref=o_ref, tot=tot):
                        o_ref[...] = tot

                    @pl.when(jnp.logical_not(first))
                    def _(o_ref=o_ref, tot=tot):
                        o_ref[...] += tot

        out_shape, out_specs = [], []
        for o, kind in zip(outs, out_kinds):
            d = o.shape[1]
            if kind == 'row':
                out_shape.append(jax.ShapeDtypeStruct((B, S, ncol * d), o.dtype))
                out_specs.append(pl.BlockSpec((1, ts, d), lambda k, b, s: (b, s, k)))
            else:
                out_shape.append(jax.ShapeDtypeStruct((1, ncol * d), o.dtype))
                out_specs.append(pl.BlockSpec((1, d), lambda k, b, s: (0, k)))
        res = pl.pallas_call(
            body, grid=(ncol, B, S // ts), in_specs=_in_specs(rows, seqs, pars, ts), out_specs=out_specs,
            out_shape=out_shape, name=name + "_fwd",
            compiler_params=_cparams(("arbitrary", "arbitrary", "arbitrary")))(*rows, *seqs, *pars)
        return tuple(res)

    def _bwd_call(rows, seqs, pars, cots, carried=None):
        B, S, ts = _dims(rows)
        outs = _out_struct(rows, seqs, pars, ts)
        n_out = len(outs)
        all_in = list(rows) + list(seqs) + list(pars)
        extra = [] if carried is None else [carried]

        def body(*refs):
            vals = _load(refs)
            if carried is not None:
                carried_ref, refs = refs[n_in + n_out], refs[:n_in + n_out] + refs[n_in + n_out + 1:]
            cts = []
            for c_ref, o, kind in zip(refs[n_in:n_in + n_out], outs, out_kinds):
                if kind == 'row':
                    cts.append(c_ref[0])
                else:
                    cts.append(jnp.broadcast_to(c_ref[...], o.shape))

            def g(*dv):
                full = list(vals)
                for i, v in zip(diff_idx, dv):
                    full[i] = v
                return tuple(f(*full))

            _, vjp = jax.vjp(g, *[vals[i] for i in diff_idx])
            grads = vjp(tuple(cts))
            b, s = pl.program_id(1), pl.program_id(2)
            for o_ref, i, gr in zip(refs[n_in + n_out:], diff_idx, grads):
                if i < n_rows:
                    if carried is not None and i == forward_row:
                        gr = gr + carried_ref[0]
                    o_ref[0] = gr.astype(o_ref.dtype)
                else:
                    first = (s == 0) if i < n_rows + n_seqs else ((b == 0) & (s == 0))
                    target = (lambda r: r.at[0]) if i < n_rows + n_seqs else (lambda r: r)

                    @pl.when(first)
                    def _(o_ref=o_ref, gr=gr, target=target):
                        target(o_ref)[...] = gr

                    @pl.when(jnp.logical_not(first))
                    def _(o_ref=o_ref, gr=gr, target=target):
                        target(o_ref)[...] += gr

        cot_specs = []
        for o, kind in zip(outs, out_kinds):
            d = o.shape[1]
            if kind == 'row':
                cot_specs.append(pl.BlockSpec((1, ts, d), lambda k, b, s: (b, s, k)))
            else:
                cot_specs.append(pl.BlockSpec((1, d), lambda k, b, s: (0, k)))
        out_shape, out_specs = [], []
        for i in diff_idx:
            a = all_in[i]
            if i < n_rows:
                out_shape.append(jax.ShapeDtypeStruct((B, S, _width(i, a)), BF16 if i in windows else a.dtype))
                out_specs.append(pl.BlockSpec((1, ts, _width(i, a) // ncol), lambda k, b, s: (b, s, k)))
                continue
            out_shape.append(jax.ShapeDtypeStruct(a.shape, a.dtype))
            if i < n_rows + n_seqs:
                out_specs.append(pl.BlockSpec((1, 1, a.shape[2] // ncol), lambda k, b, s: (b, 0, k)))
            else:
                out_specs.append(pl.BlockSpec((1, a.shape[1] // ncol), lambda k, b, s: (0, k)))
        if carried is not None:
            cot_specs.append(pl.BlockSpec((1, ts, carried.shape[2] // ncol), lambda k, b, s: (b, s, k)))
        res = pl.pallas_call(
            body, grid=(ncol, B, S // ts), in_specs=_in_specs(rows, seqs, pars, ts) + cot_specs,
            out_specs=out_specs, out_shape=out_shape, name=name + "_bwd",
            compiler_params=_cparams(("arbitrary", "arbitrary", "arbitrary")))(*all_in, *cots, *extra)
        grads = [None] * n_in
        for i, r in zip(diff_idx, res):
            grads[i] = r
        for i in nodiff:
            grads[i] = jnp.zeros_like(all_in[i])
        stand_in_grads = tuple(grads[i] for i in sorted(windows))
        for i in windows:
            grads[i] = jnp.zeros_like(all_in[i])
        return (tuple(grads[:n_rows]), tuple(grads[n_rows:n_rows + n_seqs]), tuple(grads[n_rows + n_seqs:]),
                stand_in_grads)

    def _outputs(rows, seqs, pars):
        res = _fwd_call(rows, seqs, pars)
        return res if forward_row is None else res + (rows[forward_row],)

    @jax.custom_vjp
    def op(rows, seqs, pars, stand_ins):
        return _outputs(rows, seqs, pars)

    def fwd(rows, seqs, pars, stand_ins):
        return _outputs(rows, seqs, pars), (rows, seqs, pars)

    def bwd(res, cots):
        rows, seqs, pars = res
        if forward_row is None:
            return _bwd_call(rows, seqs, pars, cots)
        return _bwd_call(rows, seqs, pars, cots[:-1], cots[-1])

    op.defvjp(fwd, bwd)
    return lambda rows, seqs, pars, stand_ins=(): op(tuple(rows), tuple(seqs), tuple(pars), tuple(stand_ins))


def _rms(x, g):
    x = x.astype(F32)
    return x * lax.rsqrt(jnp.mean(x * x, axis=-1, keepdims=True) + EPS) * g


def _silu(x):
    return x * lax.logistic(x)


def _f_silu(c):
    return (_silu(c),)


def _f_modulate(x, scale, shift, g):
    return ((_rms(x, g) * (1.0 + scale) + shift).astype(BF16),)


def _f_rms(x, g):
    return (_rms(x, g).astype(BF16),)


def _f_dt(dt_raw, dt_bias, a_log):
    z = dt_raw + dt_bias
    dt = jnp.maximum(z, 0.0) + jnp.log1p(jnp.exp(-jnp.abs(z)))
    return dt, dt * (-jnp.exp(a_log))


def _f_gated_norm(y, z, g):
    return (_rms(y * _silu(z.astype(F32)), g).astype(BF16),)


def _f_merge(attn, ssm, ga, gb):
    return ((lax.logistic(ga.astype(F32)) * attn + lax.logistic(gb.astype(F32)) * ssm).astype(BF16),)


def _f_post(x, m, gate, g):
    return (x + gate * _rms(m, g),)


def _f_final_loss(x, ff, target, gate, g):
    e = x + gate * _rms(ff, g) - target
    return (e * e * (0.5 / D_MODEL),)


def _rope_tables(posf, inv_lane):
    B, S, _ = posf.shape
    ts = min(S, 512)

    def body(p_ref, inv_ref, c_ref, a_ref, b_ref):
        ang = p_ref[0] * inv_ref[...]
        cs, sn = jnp.cos(ang), jnp.sin(ang)
        lane = lax.broadcasted_iota(jnp.int32, ang.shape, 1)
        c_ref[0] = jnp.where(lane < ROPE, cs, 0.0)
        a_ref[0] = jnp.where(lane < ROPE // 2, -sn, 0.0)
        b_ref[0] = jnp.where((lane >= ROPE // 2) & (lane < ROPE), sn, 0.0)

    spec = pl.BlockSpec((1, ts, LANE), lambda b, s: (b, s, 0))
    sds = jax.ShapeDtypeStruct((B, S, LANE), F32)
    return pl.pallas_call(
        body, grid=(B, S // ts),
        in_specs=[pl.BlockSpec((1, ts, 1), lambda b, s: (b, s, 0)), pl.BlockSpec((1, LANE), lambda b, s: (0, 0))],
        out_specs=[spec, spec, spec], out_shape=[sds, sds, sds], name="rope_tables",
        compiler_params=_cparams(("parallel", "parallel")))(posf, inv_lane)


def _rot(u, c, a, bm):
    return u * c + pltpu.roll(u, 96, 1) * a + pltpu.roll(u, 32, 1) * bm


def _rot_t(g, c, a, bm):
    return g * c + pltpu.roll(g * a, 32, 1) + pltpu.roll(g * bm, 96, 1)


def _rope_q_call(q, tabs, transpose, name):
    B, S, W = q.shape
    ts = min(S, 512)
    fn = _rot_t if transpose else _rot
    out_dtype = BF16

    def body(q_ref, c_ref, a_ref, b_ref, o_ref):
        tc, ta, tb = c_ref[0], a_ref[0], b_ref[0]
        for h in range(W // QK_PAD):
            u = q_ref[0, :, h * QK_PAD:(h + 1) * QK_PAD].astype(F32) * ATT_SCALE
            r = fn(u[:, NOPE:], tc, ta, tb)
            o_ref[0, :, h * QK_PAD:(h + 1) * QK_PAD] = jnp.concatenate([u[:, :NOPE], r], axis=1).astype(out_dtype)

    tspec = pl.BlockSpec((1, ts, LANE), lambda b, s: (b, s, 0))
    qspec = pl.BlockSpec((1, ts, W), lambda b, s: (b, s, 0))
    return pl.pallas_call(
        body, grid=(B, S // ts), in_specs=[qspec, tspec, tspec, tspec], out_specs=qspec,
        out_shape=jax.ShapeDtypeStruct(q.shape, out_dtype), name=name,
        compiler_params=_cparams(("parallel", "parallel")))(q, *tabs)


@jax.custom_vjp
def rope_q(q, tabs):
    return _rope_q_call(q, tabs, False, "rope_q_fwd")


def _rope_q_fwd(q, tabs):
    return _rope_q_call(q, tabs, False, "rope_q_fwd"), tabs


def _rope_q_bwd(tabs, g):
    return _rope_q_call(g, tabs, True, "rope_q_bwd"), tuple(jnp.zeros_like(t) for t in tabs)


rope_q.defvjp(_rope_q_fwd, _rope_q_bwd)


def _build_k_fwd_call(kv, kr, tabs):
    B, S, _ = kv.shape
    ts = min(S, 512)

    def body(kv_ref, kr_ref, c_ref, a_ref, b_ref, o_ref):
        r = _rot(kr_ref[0], c_ref[0], a_ref[0], b_ref[0]).astype(BF16)
        for h in range(N_HEADS):
            o_ref[0, :, h * QK_PAD:(h + 1) * QK_PAD] = jnp.concatenate(
                [kv_ref[0, :, h * NOPE:(h + 1) * NOPE], r], axis=1)

    tspec = pl.BlockSpec((1, ts, LANE), lambda b, s: (b, s, 0))
    kr_spec = pl.BlockSpec((1, ts, LANE), lambda b, s: (b, s, KR_LANE0 // LANE))
    return pl.pallas_call(
        body, grid=(B, S // ts),
        in_specs=[pl.BlockSpec((1, ts, N_HEADS * NOPE), lambda b, s: (b, s, 0)), kr_spec, tspec, tspec, tspec],
        out_specs=pl.BlockSpec((1, ts, N_HEADS * QK_PAD), lambda b, s: (b, s, 0)),
        out_shape=jax.ShapeDtypeStruct((B, S, N_HEADS * QK_PAD), BF16), name="build_k_fwd",
        compiler_params=_cparams(("parallel", "parallel")))(kv, kr, *tabs)


def _build_k_bwd_call(g, tabs):
    B, S, _ = g.shape
    ts = min(S, 512)

    def body(g_ref, c_ref, a_ref, b_ref, dk_ref, dr_ref):
        tot = None
        for h in range(N_HEADS):
            dk_ref[0, :, h * NOPE:(h + 1) * NOPE] = g_ref[0, :, h * QK_PAD:h * QK_PAD + NOPE]
            part = g_ref[0, :, h * QK_PAD + NOPE:(h + 1) * QK_PAD].astype(F32)
            tot = part if tot is None else tot + part
        dr_ref[0] = _rot_t(tot, c_ref[0], a_ref[0], b_ref[0]).astype(BF16)

    tspec = pl.BlockSpec((1, ts, LANE), lambda b, s: (b, s, 0))
    return pl.pallas_call(
        body, grid=(B, S // ts),
        in_specs=[pl.BlockSpec((1, ts, N_HEADS * QK_PAD), lambda b, s: (b, s, 0)), tspec, tspec, tspec],
        out_specs=[pl.BlockSpec((1, ts, N_HEADS * NOPE), lambda b, s: (b, s, 0)), tspec],
        out_shape=[jax.ShapeDtypeStruct((B, S, N_HEADS * NOPE), BF16), jax.ShapeDtypeStruct((B, S, LANE), BF16)],
        name="build_k_bwd", compiler_params=_cparams(("parallel", "parallel")))(g, *tabs)


@jax.custom_vjp
def build_k(kv, src, stand_in, tabs):
    return _build_k_fwd_call(kv, src, tabs)


def _build_k_fwd(kv, src, stand_in, tabs):
    return _build_k_fwd_call(kv, src, tabs), (tabs, kv.shape, src)


def _build_k_bwd(res, g):
    tabs, kv_shape, src = res
    dk, dr = _build_k_bwd_call(g, tabs)
    dkv = jnp.concatenate([dk, jnp.zeros((kv_shape[0], kv_shape[1], kv_shape[2] - dk.shape[2]), BF16)], axis=-1)
    return dkv, jnp.zeros_like(src), dr, tuple(jnp.zeros_like(t) for t in tabs)


build_k.defvjp(_build_k_fwd, _build_k_bwd)


ATT_SCALE = (NOPE + ROPE) ** -0.5
NEG = -1e30


def _att_tiles(S):
    t = min(S, 512)
    return t, S // t


def _scores(q, k, diagonal):
    s = lax.dot_general(q, k, (((1,), (1,)), ((), ())), preferred_element_type=F32)
    if diagonal:
        row = lax.broadcasted_iota(jnp.int32, s.shape, 0)
        col = lax.broadcasted_iota(jnp.int32, s.shape, 1)
        s = jnp.where(col <= row, s, NEG)
    return s


ATT_HB = 8


def _causal_pairs(n):
    pairs = [(i, j) for i in range(n) for j in range(i + 1)]
    return (jnp.asarray([p[0] for p in pairs], jnp.int32), jnp.asarray([p[1] for p in pairs], jnp.int32))


def _head(ref_or_val, h, w):
    return ref_or_val[:, h * w:(h + 1) * w]


def _attn_fwd_call(q, k, vsrc, v_blk0):
    B, S, _ = q.shape
    t, n = _att_tiles(S)
    qi, kj = _causal_pairs(n)

    def body(qi_ref, kj_ref, q_ref, k_ref, v_ref, o_ref, lse_ref, m_sc, l_sc, acc_sc):
        p_id = pl.program_id(2)
        i, j = qi_ref[p_id], kj_ref[p_id]

        @pl.when(j == 0)
        def _():
            m_sc[...] = jnp.full(m_sc.shape, NEG, F32)
            l_sc[...] = jnp.zeros(l_sc.shape, F32)
            acc_sc[...] = jnp.zeros(acc_sc.shape, F32)

        def step(diagonal):
            qa, ka, va = q_ref[0], k_ref[0], v_ref[0]
            for h in range(ATT_HB):
                lanes = slice(h * LANE, (h + 1) * LANE)
                s = _scores(_head(qa, h, QK_PAD), _head(ka, h, QK_PAD), diagonal)
                m_prev = m_sc[:, lanes]
                m_new = jnp.maximum(m_prev, jnp.max(s, axis=1, keepdims=True))
                alpha = jnp.exp(m_prev - m_new)
                p = jnp.exp(s - jnp.tile(m_new, (1, t // LANE)))
                l_sc[:, lanes] = alpha * l_sc[:, lanes] + jnp.sum(p, axis=1, keepdims=True)
                acc_sc[:, lanes] = alpha * acc_sc[:, lanes] + jnp.dot(p.astype(BF16), _head(va, h, V_DIM),
                                                                      preferred_element_type=F32)
                m_sc[:, lanes] = m_new

        @pl.when(j < i)
        def _():
            step(False)

        @pl.when(j == i)
        def _():
            step(True)
            o_ref[0] = (acc_sc[...] / l_sc[...]).astype(BF16)
            lse_ref[0] = m_sc[...] + jnp.log(l_sc[...])

    wq, wv = ATT_HB * QK_PAD, ATT_HB * V_DIM
    grid_spec = pltpu.PrefetchScalarGridSpec(
        num_scalar_prefetch=2, grid=(B, N_HEADS // ATT_HB, qi.shape[0]),
        in_specs=[pl.BlockSpec((1, t, wq), lambda b, h, p, qi, kj: (b, qi[p], h)),
                  pl.BlockSpec((1, t, wq), lambda b, h, p, qi, kj: (b, kj[p], h)),
                  pl.BlockSpec((1, t, wv), lambda b, h, p, qi, kj: (b, kj[p], v_blk0 + h))],
        out_specs=[pl.BlockSpec((1, t, wv), lambda b, h, p, qi, kj: (b, qi[p], h)),
                   pl.BlockSpec((1, t, wv), lambda b, h, p, qi, kj: (b, qi[p], h))],
        scratch_shapes=[pltpu.VMEM((t, wv), F32), pltpu.VMEM((t, wv), F32), pltpu.VMEM((t, wv), F32)])
    return pl.pallas_call(
        body, grid_spec=grid_spec,
        out_shape=[jax.ShapeDtypeStruct((B, S, N_HEADS * V_DIM), BF16),
                   jax.ShapeDtypeStruct((B, S, N_HEADS * LANE), F32)],
        name="attn_fwd", compiler_params=_cparams(("parallel", "parallel", "arbitrary")))(qi, kj, q, k, vsrc)


def _attn_p_ds(q, k, v, o, do, lse, diagonal, t):
    s = _scores(q, k, diagonal)
    p = jnp.exp(s - jnp.tile(lse, (1, t // LANE)))
    dp = lax.dot_general(do.astype(BF16), v, (((1,), (1,)), ((), ())), preferred_element_type=F32)
    delta = jnp.sum(do.astype(F32) * o.astype(F32), axis=1, keepdims=True)
    ds = p * (dp - delta)
    return p, ds


ATT_HB_BWD = 4


def _attn_bwd_call(q, k, vsrc, o, do, lse):
    B, S, _ = q.shape
    t, n = _att_tiles(S)
    qi, kj = _causal_pairs(n)
    n_pairs = qi.shape[0]
    hb = ATT_HB_BWD
    v_blk0 = N_HEADS // hb

    def body(qi_ref, kj_ref, q_ref, k_ref, v_ref, o_ref, do_ref, lse_ref, dq_ref, dk_ref, dv_ref, dq_sc, dk_sc, dv_sc):
        p_id = pl.program_id(2)
        i, j = qi_ref[p_id], kj_ref[p_id]

        @pl.when(p_id == 0)
        def _():
            dk_sc[...] = jnp.zeros(dk_sc.shape, F32)
            dv_sc[...] = jnp.zeros(dv_sc.shape, F32)

        @pl.when(j == 0)
        def _():
            dq_sc[...] = jnp.zeros(dq_sc.shape, F32)

        rows = pl.ds(pl.multiple_of(j * t, t), t)

        def step(diagonal):
            qa, ka, va, oa, doa, la = q_ref[0], k_ref[0], v_ref[0], o_ref[0], do_ref[0], lse_ref[0]
            for h in range(hb):
                qb, kb, dob = _head(qa, h, QK_PAD), _head(ka, h, QK_PAD), _head(doa, h, V_DIM)
                p, ds = _attn_p_ds(qb, kb, _head(va, h, V_DIM), _head(oa, h, V_DIM), dob, _head(la, h, LANE),
                                   diagonal, t)
                dsb = ds.astype(BF16)
                dq_sc[:, h * QK_PAD:(h + 1) * QK_PAD] += jnp.dot(dsb, kb, preferred_element_type=F32)
                dv_sc[rows, h * V_DIM:(h + 1) * V_DIM] += lax.dot_general(
                    p.astype(BF16), dob.astype(BF16), (((0,), (0,)), ((), ())), preferred_element_type=F32)
                dk_sc[rows, h * QK_PAD:(h + 1) * QK_PAD] += lax.dot_general(
                    dsb, qb, (((0,), (0,)), ((), ())), preferred_element_type=F32)

        @pl.when(j < i)
        def _():
            step(False)

        @pl.when(j == i)
        def _():
            step(True)
            dq_ref[0] = dq_sc[...].astype(BF16)

        @pl.when(i == n - 1)
        def _():
            dk_ref[0] = dk_sc[rows, :].astype(BF16)
            dv_ref[0] = dv_sc[rows, :].astype(BF16)

    wq, wv = hb * QK_PAD, hb * V_DIM
    at_q = lambda b, h, p, qi, kj: (b, qi[p], h)
    at_k = lambda b, h, p, qi, kj: (b, kj[p], h)
    at_done = lambda b, h, p, qi, kj: (b, jnp.where(qi[p] == n - 1, kj[p], 0), h)
    grid_spec = pltpu.PrefetchScalarGridSpec(
        num_scalar_prefetch=2, grid=(B, N_HEADS // hb, n_pairs),
        in_specs=[pl.BlockSpec((1, t, wq), at_q), pl.BlockSpec((1, t, wq), at_k),
                  pl.BlockSpec((1, t, wv), lambda b, h, p, qi, kj: (b, kj[p], v_blk0 + h)),
                  pl.BlockSpec((1, t, wv), at_q), pl.BlockSpec((1, t, wv), at_q), pl.BlockSpec((1, t, wv), at_q)],
        out_specs=[pl.BlockSpec((1, t, wq), at_q), pl.BlockSpec((1, t, wq), at_done), pl.BlockSpec((1, t, wv), at_done)],
        scratch_shapes=[pltpu.VMEM((t, wq), F32), pltpu.VMEM((S, wq), F32), pltpu.VMEM((S, wv), F32)])
    return pl.pallas_call(
        body, grid_spec=grid_spec,
        out_shape=[jax.ShapeDtypeStruct((B, S, N_HEADS * QK_PAD), BF16),
                   jax.ShapeDtypeStruct((B, S, N_HEADS * QK_PAD), BF16),
                   jax.ShapeDtypeStruct((B, S, N_HEADS * V_DIM), BF16)],
        name="attn_bwd", compiler_params=_cparams(("parallel", "parallel", "arbitrary")))(
            qi, kj, q, k, vsrc, o, do, lse)


@jax.custom_vjp
def attention(q, k, kv):
    return _attn_fwd_call(q, k, kv, N_HEADS // ATT_HB)[0]


def _attention_fwd(q, k, kv):
    o, lse = _attn_fwd_call(q, k, kv, N_HEADS // ATT_HB)
    return o, (q, k, kv, o, lse)


def _attention_bwd(res, do):
    q, k, kv, o, lse = res
    dq, dk, dv = _attn_bwd_call(q, k, kv, o, do, lse)
    dkv = jnp.concatenate([jnp.zeros_like(dv), dv], axis=-1)
    return dq, dk, dkv


attention.defvjp(_attention_fwd, _attention_bwd)


SUBLANES = 8


def _zero_tail(v):
    return jnp.concatenate([v, jnp.zeros((SUBLANES, v.shape[1]), v.dtype)], axis=0)


def _shift_down(vz, sh):
    return pltpu.roll(vz, sh, 0)[:vz.shape[0] - SUBLANES]


def _shift_up(vz, sh):
    return pltpu.roll(vz, vz.shape[0] - sh, 0)[:vz.shape[0] - SUBLANES]


def _conv_pre(u, uz, w_ref, b_ref):
    acc = b_ref[...] + w_ref[pl.ds(CONV_K - 1, 1), :] * u
    for k in range(CONV_K - 1):
        acc = acc + w_ref[pl.ds(k, 1), :] * _shift_down(uz, CONV_K - 1 - k)
    return acc


def _conv_fwd_call(src, w, b):
    B, S, _ = src.shape
    C = w.shape[1]

    def body(u_ref, w_ref, b_ref, o_ref):
        uu = u_ref[0].astype(F32)
        o_ref[0] = _silu(_conv_pre(uu, _zero_tail(uu), w_ref, b_ref))

    spec = pl.BlockSpec((1, S, LANE), lambda c, bb: (bb, 0, c))
    return pl.pallas_call(
        body, grid=(C // LANE, B),
        in_specs=[pl.BlockSpec((1, S, LANE), lambda c, bb: (bb, 0, c + CONV_LANE0 // LANE)),
                  pl.BlockSpec((CONV_K, LANE), lambda c, bb: (0, c)), pl.BlockSpec((1, LANE), lambda c, bb: (0, c))],
        out_specs=spec, out_shape=jax.ShapeDtypeStruct((B, S, C), F32), name="conv_fwd",
        compiler_params=_cparams(("parallel", "arbitrary")))(src, w, b)


def _conv_bwd_call(src, w, b, g):
    B, S, _ = src.shape
    C = w.shape[1]

    def body(u_ref, w_ref, b_ref, g_ref, du_ref, dw_ref, db_ref):
        uu = u_ref[0].astype(F32)
        uz = _zero_tail(uu)
        pre = _conv_pre(uu, uz, w_ref, b_ref)
        sg = lax.logistic(pre)
        dpre = g_ref[0] * sg * (1.0 + pre * (1.0 - sg))
        dz = _zero_tail(dpre)
        du = w_ref[pl.ds(CONV_K - 1, 1), :] * dpre
        dws = [None] * CONV_K
        dws[CONV_K - 1] = jnp.sum(dpre * uu, axis=0, keepdims=True)
        for k in range(CONV_K - 1):
            sh = CONV_K - 1 - k
            du = du + w_ref[pl.ds(k, 1), :] * _shift_up(dz, sh)
            dws[k] = jnp.sum(dpre * _shift_down(uz, sh), axis=0, keepdims=True)
        du_ref[0] = du.astype(du_ref.dtype)
        dbv = jnp.sum(dpre, axis=0, keepdims=True)
        first = pl.program_id(1) == 0

        @pl.when(first)
        def _():
            for k in range(CONV_K):
                dw_ref[pl.ds(k, 1), :] = dws[k]
            db_ref[...] = dbv

        @pl.when(jnp.logical_not(first))
        def _():
            for k in range(CONV_K):
                dw_ref[pl.ds(k, 1), :] += dws[k]
            db_ref[...] += dbv

    spec = pl.BlockSpec((1, S, LANE), lambda c, bb: (bb, 0, c))
    wspec = pl.BlockSpec((CONV_K, LANE), lambda c, bb: (0, c))
    bspec = pl.BlockSpec((1, LANE), lambda c, bb: (0, c))
    uspec = pl.BlockSpec((1, S, LANE), lambda c, bb: (bb, 0, c + CONV_LANE0 // LANE))
    return pl.pallas_call(
        body, grid=(C // LANE, B), in_specs=[uspec, wspec, bspec, spec], out_specs=[spec, wspec, bspec],
        out_shape=[jax.ShapeDtypeStruct((B, S, C), BF16), jax.ShapeDtypeStruct(w.shape, F32),
                   jax.ShapeDtypeStruct(b.shape, F32)],
        name="conv_bwd", compiler_params=_cparams(("parallel", "arbitrary")))(src, w, b, g)


@jax.custom_vjp
def conv_silu(src, stand_in, w, b):
    return _conv_fwd_call(src, w, b)


def _conv_silu_fwd(src, stand_in, w, b):
    return _conv_fwd_call(src, w, b), (src, w, b)


def _conv_silu_bwd(res, g):
    du, dw, db = _conv_bwd_call(*res, g)
    return jnp.zeros_like(res[0]), du, dw, db


conv_silu.defvjp(_conv_silu_fwd, _conv_silu_bwd)


def _chunk_cumsum_call(a, reverse, name):
    B, S, W = a.shape
    per_step = min(S // CHUNK, 8)

    def body(a_ref, o_ref):
        r = lax.broadcasted_iota(jnp.int32, (CHUNK, CHUNK), 0)
        c = lax.broadcasted_iota(jnp.int32, (CHUNK, CHUNK), 1)
        tri = jnp.where((c >= r) if reverse else (c <= r), 1.0, 0.0).astype(F32)
        for i in range(per_step):
            rows = pl.ds(i * CHUNK, CHUNK)
            o_ref[0, rows, :] = jnp.dot(tri, a_ref[0, rows, :], preferred_element_type=F32,
                                        precision=lax.Precision.HIGHEST)

    spec = pl.BlockSpec((1, per_step * CHUNK, W), lambda b, c: (b, c, 0))
    return pl.pallas_call(body, grid=(B, S // (per_step * CHUNK)), in_specs=[spec], out_specs=spec,
                          out_shape=jax.ShapeDtypeStruct(a.shape, F32), name=name,
                          compiler_params=_cparams(("parallel", "parallel")))(a)


@jax.custom_vjp
def chunk_cumsum(a):
    return _chunk_cumsum_call(a, False, "chunk_cumsum_fwd")


chunk_cumsum.defvjp(lambda a: (_chunk_cumsum_call(a, False, "chunk_cumsum_fwd"), None),
                    lambda _, g: (_chunk_cumsum_call(g, True, "chunk_cumsum_bwd"),))


GROUP_W = 4 * HEAD_P
HPG = SSM_HEADS // SSM_GROUPS


def _ssd_masks():
    lane = lax.broadcasted_iota(jnp.int32, (1, GROUP_W), 1)
    return [((lane >= HEAD_P * j) & (lane < HEAD_P * (j + 1))).astype(F32) for j in range(HPG)]


def _ssd_decays(ac_cols, acr_ref, gi):
    r = lax.broadcasted_iota(jnp.int32, (CHUNK, CHUNK), 0)
    c = lax.broadcasted_iota(jnp.int32, (CHUNK, CHUNK), 1)
    return [jnp.exp(jnp.where(c <= r, ac_cols[j] - acr_ref[0, gi * HPG + j], NEG)) for j in range(HPG)]


def _ssd_cols(blk, g):
    lane = lax.broadcasted_iota(jnp.int32, blk.shape, 1)
    return [jnp.sum(jnp.where(lane == HPG * g + j, blk, 0.0), axis=1, keepdims=True) for j in range(HPG)]


def _ssd_spread(cols):
    lane = lax.broadcasted_iota(jnp.int32, (1, GROUP_W), 1)
    out = jnp.broadcast_to(cols[HPG - 1], (CHUNK, GROUP_W))
    for j in range(HPG - 2, -1, -1):
        out = jnp.where(lane < HEAD_P * (j + 1), cols[j], out)
    return out


def _ssd_gather(val, cols, masks, g):
    lane = lax.broadcasted_iota(jnp.int32, (1, LANE), 1)
    out = jnp.zeros((CHUNK, LANE), F32)
    for j in range(HPG):
        tot = jnp.sum(val * masks[j], axis=1, keepdims=True)
        if cols is not None:
            tot = tot + cols[j]
        out = out + tot * (lane == HPG * g + j).astype(F32)
    return out


def _dot(a, b, dims):
    return lax.dot_general(a.astype(BF16), b.astype(BF16), (dims, ((), ())), preferred_element_type=F32)


NN = ((1,), (0,))
NT = ((1,), (1,))
TN = ((0,), (0,))


XBC_W = GROUP_W + 2 * STATE_N


SSD_STEP_GROUPS_FWD = 8
SSD_STEP_GROUPS_BWD = 2


def _ssd_load(xbc_ref, dt_ref, ac_ref, masks, g, gi):
    x = xbc_ref[0, :, gi * XBC_W:gi * XBC_W + GROUP_W]
    bm = xbc_ref[0, :, gi * XBC_W + GROUP_W:gi * XBC_W + GROUP_W + STATE_N]
    cm = xbc_ref[0, :, gi * XBC_W + GROUP_W + STATE_N:(gi + 1) * XBC_W]
    ac_cols = _ssd_cols(ac_ref[0], g)
    dt = _ssd_spread(_ssd_cols(dt_ref[0], g))
    ac = _ssd_spread(ac_cols)
    is_last = (lax.broadcasted_iota(jnp.int32, (CHUNK, GROUP_W), 0) == CHUNK - 1).astype(F32)
    return x, bm, cm, dt, ac, ac_cols, is_last


def _ssd_in_specs(nc, rev, gb):
    cc = (lambda c: nc - 1 - c) if rev else (lambda c: c)
    return [pl.BlockSpec((1, CHUNK, gb * XBC_W), lambda b, g, c: (b, cc(c), g)),
            pl.BlockSpec((1, CHUNK, LANE), lambda b, g, c: (b, cc(c), 0)),
            pl.BlockSpec((1, CHUNK, LANE), lambda b, g, c: (b, cc(c), 0)),
            pl.BlockSpec((1, gb * HPG, 1, CHUNK), lambda b, g, c: (b, g, 0, cc(c))),
            pl.BlockSpec((1, gb * GROUP_W), lambda b, g, c: (0, g))]


def _ssd_fwd_call(xbc, dtp, acp, acr, dsk):
    B, S, _ = xbc.shape
    nc = S // CHUNK
    gb = SSD_STEP_GROUPS_FWD

    def body(xbc_ref, dt_ref, ac_ref, ar_ref, ds_ref, y_ref, hp_ref, h_sc):
        @pl.when(pl.program_id(2) == 0)
        def _():
            h_sc[...] = jnp.zeros(h_sc.shape, F32)

        masks = _ssd_masks()
        ys = []
        for gi in range(gb):
            grp = gb * pl.program_id(1) + gi
            x, bm, cm, dt, ac, ac_cols, is_last = _ssd_load(xbc_ref, dt_ref, ac_ref, masks, grp, gi)
            last = jnp.sum(ac * is_last, axis=0, keepdims=True)
            decays = _ssd_decays(ac_cols, ar_ref, gi)
            xd = x * dt
            cb = _dot(cm, bm, NT)
            hprev = h_sc[gi]
            hp_ref[0, gi, 0] = hprev
            y = _dot(cm, hprev, NN) * jnp.exp(ac) + ds_ref[:, gi * GROUP_W:(gi + 1) * GROUP_W] * x
            y = y + _dot(jnp.concatenate([cb * d for d in decays], axis=1),
                         jnp.concatenate([xd * m for m in masks], axis=0), NN)
            ys.append(y)
            h_sc[gi] = hprev * jnp.exp(last) + _dot(bm, xd * jnp.exp(last - ac), TN)
        y_ref[0] = jnp.concatenate(ys, axis=1)

    ng = SSM_GROUPS // gb
    return pl.pallas_call(
        body, grid=(B, ng, nc), in_specs=_ssd_in_specs(nc, False, gb),
        out_specs=[pl.BlockSpec((1, CHUNK, gb * GROUP_W), lambda b, g, c: (b, c, g)),
                   pl.BlockSpec((1, gb, 1, STATE_N, GROUP_W), lambda b, g, c: (b, g, c, 0, 0))],
        out_shape=[jax.ShapeDtypeStruct((B, S, D_INNER), F32),
                   jax.ShapeDtypeStruct((B, SSM_GROUPS, nc, STATE_N, GROUP_W), F32)],
        scratch_shapes=[pltpu.VMEM((gb, STATE_N, GROUP_W), F32)], name="ssd_fwd",
        compiler_params=_cparams(("parallel", "parallel", "arbitrary")))(xbc, dtp, acp, acr, dsk)


def _ssd_bwd_call(xbc, dtp, acp, acr, dsk, hps, dy):
    B, S, _ = xbc.shape
    nc = S // CHUNK
    gb = SSD_STEP_GROUPS_BWD

    def body(xbc_ref, dt_ref, ac_ref, ar_ref, ds_ref, hp_ref, dy_ref,
             dxbc_ref, ddt_ref, dac_ref, dar_ref, dds_ref, dh_sc):
        first = pl.program_id(2) == 0

        @pl.when(first)
        def _():
            dh_sc[...] = jnp.zeros(dh_sc.shape, F32)

        masks = _ssd_masks()
        dxbc_parts, dds_parts = [], []
        for gi in range(gb):
            grp = gb * pl.program_id(0) + gi
            x, bm, cm, dt, ac, ac_cols, is_last = _ssd_load(xbc_ref, dt_ref, ac_ref, masks, grp, gi)
            last = jnp.sum(ac * is_last, axis=0, keepdims=True)
            g = dy_ref[0, :, gi * GROUP_W:(gi + 1) * GROUP_W]
            hprev = hp_ref[0, gi, 0]
            dh = dh_sc[gi]
            decays = _ssd_decays(ac_cols, ar_ref, gi)
            dcols = []
            xd = x * dt
            cb = _dot(cm, bm, NT)
            e_c = jnp.exp(ac)
            e_end = jnp.exp(last - ac)
            e_last = jnp.exp(last)
            z = _dot(cm, hprev, NN)
            dz = g * e_c
            dac = g * z * e_c
            dc = _dot(dz, hprev, NT)
            dhprev = _dot(cm, dz, TN) + dh * e_last
            dcb = jnp.zeros((CHUNK, CHUNK), F32)
            gjs = [cb * d for d in decays]
            g_heads = jnp.concatenate([g * m for m in masks], axis=0)
            dg_heads = _dot(g_heads, xd, NT)
            dxd = _dot(jnp.concatenate(gjs, axis=0), g_heads, TN)
            for j in range(HPG):
                gj = gjs[j]
                dgj = dg_heads[j * CHUNK:(j + 1) * CHUNK]
                dcb = dcb + dgj * decays[j]
                dseg = dgj * gj
                dcols.append(jnp.sum(dseg, axis=1, keepdims=True))
                dar_ref[0, gi * HPG + j] = -jnp.sum(dseg, axis=0, keepdims=True)
            dc = dc + _dot(dcb, bm, NN)
            db = _dot(dcb, cm, TN)
            sx = xd * e_end
            db = db + _dot(sx, dh, NT)
            dsx = _dot(bm, dh, NN)
            dxd = dxd + dsx * e_end
            de = dsx * sx
            dac = dac - de
            dlast = jnp.sum(de, axis=0, keepdims=True) + jnp.sum(dh * hprev, axis=0, keepdims=True) * e_last
            dsk = ds_ref[:, gi * GROUP_W:(gi + 1) * GROUP_W]
            dxbc_parts += [dxd * dt + dsk * g, db, dc]
            ddt_ref[0, gi] = _ssd_gather(dxd * x, None, masks, grp)
            dac_ref[0, gi] = _ssd_gather(dac + is_last * dlast, dcols, masks, grp)
            dds_parts.append(jnp.sum(g * x, axis=0, keepdims=True))
            dh_sc[gi] = dhprev
        dxbc_ref[0] = jnp.concatenate(dxbc_parts, axis=1)
        dds = jnp.concatenate(dds_parts, axis=1)
        first_all = first & (pl.program_id(1) == 0)

        @pl.when(first_all)
        def _():
            dds_ref[...] = dds

        @pl.when(jnp.logical_not(first_all))
        def _():
            dds_ref[...] += dds

    rc = lambda c: nc - 1 - c
    ng = SSM_GROUPS // gb
    in_specs = [pl.BlockSpec(s.block_shape, (lambda g, b, c, f=s.index_map: f(b, g, c))) for s in _ssd_in_specs(nc, True, gb)]
    in_specs.append(pl.BlockSpec((1, gb, 1, STATE_N, GROUP_W), lambda g, b, c: (b, g, rc(c), 0, 0)))
    in_specs.append(pl.BlockSpec((1, CHUNK, gb * GROUP_W), lambda g, b, c: (b, rc(c), g)))
    per_group = pl.BlockSpec((1, gb, CHUNK, LANE), lambda g, b, c: (b, g, rc(c), 0))
    out_specs = [pl.BlockSpec((1, CHUNK, gb * XBC_W), lambda g, b, c: (b, rc(c), g)), per_group, per_group,
                 pl.BlockSpec((1, gb * HPG, 1, CHUNK), lambda g, b, c: (b, g, 0, rc(c))),
                 pl.BlockSpec((1, gb * GROUP_W), lambda g, b, c: (0, g))]
    out_shape = [jax.ShapeDtypeStruct(xbc.shape, F32),
                 jax.ShapeDtypeStruct((B, SSM_GROUPS, S, LANE), F32), jax.ShapeDtypeStruct((B, SSM_GROUPS, S, LANE), F32),
                 jax.ShapeDtypeStruct(acr.shape, F32), jax.ShapeDtypeStruct(dsk.shape, F32)]
    return pl.pallas_call(
        body, grid=(ng, B, nc), in_specs=in_specs, out_specs=out_specs, out_shape=out_shape,
        scratch_shapes=[pltpu.VMEM((gb, STATE_N, GROUP_W), F32)], name="ssd_bwd",
        compiler_params=_cparams(("arbitrary", "arbitrary", "arbitrary")))(xbc, dtp, acp, acr, dsk, hps, dy)


@jax.custom_vjp
def ssd(xbc, dtp, acp, acr, dsk):
    return _ssd_fwd_call(xbc, dtp, acp, acr, dsk)[0]


def _ssd_fwd(xbc, dtp, acp, acr, dsk):
    y, hps = _ssd_fwd_call(xbc, dtp, acp, acr, dsk)
    return y, (xbc, dtp, acp, acr, dsk, hps)


def _ssd_bwd(res, dy):
    dxbc, ddt, dac, dacr, dds = _ssd_bwd_call(*res, dy)
    return dxbc, jnp.sum(ddt, axis=1), jnp.sum(dac, axis=1), dacr, dds


ssd.defvjp(_ssd_fwd, _ssd_bwd)


def _pack_small(arrs):
    flat = jnp.concatenate([a.reshape(-1) for a in arrs])
    rows = -(-flat.shape[0] // (8 * LANE)) * 8
    return jnp.pad(flat, (0, rows * LANE - flat.shape[0])).reshape(rows, LANE)


def _unpack_small(buf, shapes):
    flat = buf.reshape(-1)
    out, off = [], 0
    for shp in shapes:
        n = int(np.prod(shp))
        out.append(flat[off:off + n].reshape(shp))
        off += n
    return out


def _rows_tile(rows, cap):
    for cand in range(min(rows, cap), 7, -8):
        if rows % cand == 0:
            return cand
    return rows


def _pair_sum(mine, theirs, cidx, name):
    n4, kk, nn = mine.shape
    half = kk // 2
    tr = _rows_tile(half, 256)
    nb = half // tr

    def body(c_ref, a_ref, b_ref, o_ref, ob_ref):
        tot = a_ref[...] + b_ref[...]
        o_ref[...] = tot
        ob_ref[...] = tot.astype(BF16)

    spec = pl.BlockSpec((1, tr, nn), lambda j, i, c: (j, i, 0))
    grid_spec = pltpu.PrefetchScalarGridSpec(
        num_scalar_prefetch=1, grid=(n4, nb),
        in_specs=[pl.BlockSpec((1, tr, nn), lambda j, i, c: (j, c[0] * nb + i, 0)), spec], out_specs=[spec, spec])
    return pl.pallas_call(
        body, grid_spec=grid_spec,
        out_shape=[jax.ShapeDtypeStruct((n4, half, nn), F32), jax.ShapeDtypeStruct((n4, half, nn), BF16)],
        name=name, compiler_params=_cparams(("parallel", "parallel")))(cidx, mine, theirs)


def _chip_sum(quad, pair, chip_idx, name):
    _, rows, nn = quad.shape
    tr = _rows_tile(rows, 256)

    def body(s_ref, q_ref, p_ref, o_ref):
        for mine in range(4):
            @pl.when(s_ref[0] == mine)
            def _(mine=mine):
                acc = None
                for d in range(4):
                    term = p_ref[0] if d == mine else q_ref[d].astype(F32)
                    acc = term if acc is None else acc + term
                o_ref[...] = acc

    grid_spec = pltpu.PrefetchScalarGridSpec(
        num_scalar_prefetch=1, grid=(rows // tr,),
        in_specs=[pl.BlockSpec((4, tr, nn), lambda i, s: (0, i, 0)), pl.BlockSpec((1, tr, nn), lambda i, s: (s[0], i, 0))],
        out_specs=pl.BlockSpec((tr, nn), lambda i, s: (i, 0)))
    return pl.pallas_call(body, grid_spec=grid_spec, out_shape=jax.ShapeDtypeStruct((rows, nn), F32), name=name,
                          compiler_params=_cparams(("parallel",)))(chip_idx, quad, pair)


def _adam_halves_call(w, mine, other, cidx, m, v, name):
    _, rows, nn = w.shape
    half = rows // 2
    tr = _rows_tile(half, 128)
    nb = half // tr

    def body(c_ref, w_ref, a_ref, b_ref, m_ref, v_ref, g_ref, d_ref, nm_ref, nv_ref):
        upper = (pl.program_id(0) >= nb).astype(jnp.int32)
        g = jnp.where(upper == c_ref[0], a_ref[...], b_ref[...])
        g_ref[0] = g
        d_ref[0], nm_ref[0], nv_ref[0] = _adam_fn(w_ref[0], g, m_ref[0], v_ref[0])

    spec = pl.BlockSpec((1, tr, nn), lambda i, c: (0, i, 0))
    hspec = pl.BlockSpec((tr, nn), lambda i, c: (i % nb, 0))
    grid_spec = pltpu.PrefetchScalarGridSpec(num_scalar_prefetch=1, grid=(2 * nb,),
                                             in_specs=[spec, hspec, hspec, spec, spec], out_specs=[spec] * 4)
    return pl.pallas_call(body, grid_spec=grid_spec, out_shape=[jax.ShapeDtypeStruct(w.shape, F32)] * 4, name=name,
                          compiler_params=_cparams(("parallel",)))(cidx, w, mine, other, m, v)


def _stack_sum(stack, name):
    n, rows, nn = stack.shape
    tr = _rows_tile(rows, 256)

    def body(s_ref, o_ref):
        acc = s_ref[0]
        for d in range(1, n):
            acc = acc + s_ref[d]
        o_ref[...] = acc

    return pl.pallas_call(
        body, grid=(rows // tr,), in_specs=[pl.BlockSpec((n, tr, nn), lambda i: (0, i, 0))],
        out_specs=pl.BlockSpec((tr, nn), lambda i: (i, 0)), out_shape=jax.ShapeDtypeStruct((rows, nn), F32),
        name=name, compiler_params=_cparams(("parallel",)))(stack)


def _adam_call(w, g, m, v, name):
    rows, nn = w.shape
    tr = _rows_tile(rows, 128)

    def body(w_ref, g_ref, m_ref, v_ref, d_ref, nm_ref, nv_ref):
        d_ref[...], nm_ref[...], nv_ref[...] = _adam_fn(w_ref[...], g_ref[...], m_ref[...], v_ref[...])

    spec = pl.BlockSpec((tr, nn), lambda i: (i, 0))
    sds = jax.ShapeDtypeStruct((rows, nn), F32)
    return pl.pallas_call(body, grid=(rows // tr,), in_specs=[spec] * 4, out_specs=[spec] * 3,
                          out_shape=[sds] * 3, name=name, compiler_params=_cparams(("parallel",)))(w, g, m, v)


def _adam_fn(w, g, m, v):
    m = ADAM_B1 * m + (1.0 - ADAM_B1) * g
    v = ADAM_B2 * v + (1.0 - ADAM_B2) * (g * g)
    m_hat = m / (1.0 - ADAM_B1 ** ADAM_STEP)
    v_hat = v / (1.0 - ADAM_B2 ** ADAM_STEP)
    delta = -ADAM_LR * (m_hat / (jnp.sqrt(v_hat) + ADAM_EPS) + ADAM_WD * w)
    return delta, m, v


def _mesh_pos():
    return lax.axis_index("x"), lax.axis_index("y"), lax.axis_index("c")


def _other_chips(x, y):
    return [(1 - x, y), (x, 1 - y), (1 - x, 1 - y)]


HBM_SPEC = pl.BlockSpec(memory_space=pl.ANY)


def _remote(src, dst, send_sems, recv_sems, k, to):
    return pltpu.make_async_remote_copy(src_ref=src, dst_ref=dst, send_sem=send_sems.at[k], recv_sem=recv_sems.at[k],
                                        device_id=to, device_id_type=MESH)


def _half_rows(c, rows, align):
    half = rows // 2
    return (pl.ds(pl.multiple_of(c * half, align), half), pl.ds(pl.multiple_of((1 - c) * half, align), half))


def _gather_weights(mats, conv):
    n = len(mats)

    def body(*refs):
        ins, conv_in = refs[:n], refs[n]
        outs, conv_out = refs[n + 1:2 * n + 1], refs[2 * n + 1]
        send_sems, recv_sems, local_sem = refs[2 * n + 2:]
        x, y, c = _mesh_pos()
        me, sibling, s = (x, y, c), (x, y, 1 - c), 2 * x + y
        chips = _other_chips(x, y)
        rows = [_half_rows(c, m.shape[0], 16) for m in mats]
        own = pltpu.make_async_copy(conv_in, conv_out.at[s], local_sem)
        own.start()
        sent = []
        for i in range(n):
            mine = rows[i][0]
            for j, (cx, cy) in enumerate(chips):
                sent.append(_remote(ins[i].at[mine], outs[i].at[s, mine], send_sems, recv_sems, 6 * i + j, (cx, cy, c)))
        for j, (cx, cy) in enumerate(chips):
            sent.append(_remote(conv_in, conv_out.at[s], send_sems, recv_sems, 6 * n + j, (cx, cy, c)))
        for cp in sent:
            cp.start()
        for i in range(n):
            mine = rows[i][0]
            for j, (cx, cy) in enumerate(chips):
                landed = outs[i].at[2 * cx + cy, mine]
                _remote(landed, landed, send_sems, recv_sems, 6 * i + j, me).wait_recv()
                fwd = _remote(landed, landed, send_sems, recv_sems, 6 * i + 3 + j, sibling)
                fwd.start()
                sent.append(fwd)
        for j, (cx, cy) in enumerate(chips):
            slot = conv_out.at[2 * cx + cy]
            _remote(slot, slot, send_sems, recv_sems, 6 * n + j, me).wait_recv()
        for i in range(n):
            theirs_rows = rows[i][1]
            for j, (cx, cy) in enumerate(chips):
                theirs = outs[i].at[2 * cx + cy, theirs_rows]
                _remote(theirs, theirs, send_sems, recv_sems, 6 * i + 3 + j, me).wait_recv()
        for cp in sent:
            cp.wait_send()
        own.wait()

    out_shape = [jax.ShapeDtypeStruct((4,) + m.shape, m.dtype) for m in mats]
    out_shape.append(jax.ShapeDtypeStruct((4,) + conv.shape, conv.dtype))
    res = pl.pallas_call(
        body, in_specs=[HBM_SPEC] * (n + 1), out_specs=[HBM_SPEC] * (n + 1), out_shape=out_shape,
        scratch_shapes=[pltpu.SemaphoreType.DMA((6 * n + 3,)), pltpu.SemaphoreType.DMA((6 * n + 3,)),
                        pltpu.SemaphoreType.DMA],
        name="all_gather_weights")(*mats, conv)
    return res[:n], res[n]


def _sibling_exchange(stacks):
    n = len(stacks)

    def body(*refs):
        ins, outs = refs[:n], refs[n:2 * n]
        send_sems, recv_sems = refs[2 * n:]
        x, y, c = _mesh_pos()
        cps = []
        for i in range(n):
            theirs = _half_rows(c, stacks[i].shape[1], 8)[1]
            cps.append(_remote(ins[i].at[:, theirs, :], outs[i], send_sems, recv_sems, i, (x, y, 1 - c)))
        for cp in cps:
            cp.start()
        for cp in cps:
            cp.wait()

    out_shape = [jax.ShapeDtypeStruct((4, s.shape[1] // 2, s.shape[2]), s.dtype) for s in stacks]
    return pl.pallas_call(
        body, in_specs=[HBM_SPEC] * n, out_specs=[HBM_SPEC] * n, out_shape=out_shape,
        scratch_shapes=[pltpu.SemaphoreType.DMA((n,)), pltpu.SemaphoreType.DMA((n,))],
        name="grad_sibling_exchange")(*stacks)


def _chip_exchange(parts):
    n = len(parts)

    def body(*refs):
        ins, outs = refs[:n], refs[n:2 * n]
        send_sems, recv_sems = refs[2 * n:]
        x, y, c = _mesh_pos()
        me, s = (x, y, c), 2 * x + y
        chips = _other_chips(x, y)
        sent = [_remote(ins[i].at[2 * cx + cy], outs[i].at[s], send_sems, recv_sems, 3 * i + j, (cx, cy, c))
                for i in range(n) for j, (cx, cy) in enumerate(chips)]
        for cp in sent:
            cp.start()
        for i in range(n):
            for j, (cx, cy) in enumerate(chips):
                slot = outs[i].at[2 * cx + cy]
                _remote(slot, slot, send_sems, recv_sems, 3 * i + j, me).wait_recv()
        for cp in sent:
            cp.wait_send()

    return pl.pallas_call(
        body, in_specs=[HBM_SPEC] * n, out_specs=[HBM_SPEC] * n,
        out_shape=[jax.ShapeDtypeStruct(p.shape, p.dtype) for p in parts],
        scratch_shapes=[pltpu.SemaphoreType.DMA((3 * n,)), pltpu.SemaphoreType.DMA((3 * n,))],
        name="grad_chip_exchange")(*parts)


def _sibling_swap(halves):
    n = len(halves)

    def body(*refs):
        ins, outs = refs[:n], refs[n:2 * n]
        send_sems, recv_sems = refs[2 * n:]
        x, y, c = _mesh_pos()
        cps = [_remote(ins[i], outs[i], send_sems, recv_sems, i, (x, y, 1 - c)) for i in range(n)]
        for cp in cps:
            cp.start()
        for cp in cps:
            cp.wait()

    return pl.pallas_call(
        body, in_specs=[HBM_SPEC] * n, out_specs=[HBM_SPEC] * n,
        out_shape=[jax.ShapeDtypeStruct(h.shape, h.dtype) for h in halves],
        scratch_shapes=[pltpu.SemaphoreType.DMA((n,)), pltpu.SemaphoreType.DMA((n,))],
        name="grad_sibling_swap")(*halves)


def _gather_small(vec):
    def body(in_ref, out_ref, send_sems, recv_sems, local_sem):
        x, y, c = _mesh_pos()
        me = (x, y, c)
        own = pltpu.make_async_copy(in_ref, out_ref.at[4 * x + 2 * y + c], local_sem)
        own.start()
        peers = [(1 - x if k & 4 else x, 1 - y if k & 2 else y, 1 - c if k & 1 else c) for k in range(1, 8)]
        sent = [_remote(in_ref, out_ref.at[4 * x + 2 * y + c], send_sems, recv_sems, k, p) for k, p in enumerate(peers)]
        for cp in sent:
            cp.start()
        for k, (px, py, pc) in enumerate(peers):
            slot = out_ref.at[4 * px + 2 * py + pc]
            _remote(slot, slot, send_sems, recv_sems, k, me).wait_recv()
        for cp in sent:
            cp.wait_send()
        own.wait()

    return pl.pallas_call(
        body, in_specs=[HBM_SPEC], out_specs=HBM_SPEC, out_shape=jax.ShapeDtypeStruct((8,) + vec.shape, vec.dtype),
        scratch_shapes=[pltpu.SemaphoreType.DMA((7,)), pltpu.SemaphoreType.DMA((7,)), pltpu.SemaphoreType.DMA],
        name="grad_gather_small")(vec)


def _reduce_matrices(stacks, names):
    cidx = lax.axis_index("c").astype(jnp.int32).reshape(1)
    chip = (2 * lax.axis_index("x") + lax.axis_index("y")).astype(jnp.int32).reshape(1)
    got = _sibling_exchange(stacks)
    pairs = [_pair_sum(a, b, cidx, "grad_pair_sum_" + nm) for a, b, nm in zip(stacks, got, names)]
    quads = _chip_exchange([p[1] for p in pairs])
    mine = [_chip_sum(q, p[0], chip, "grad_chip_sum_" + nm) for q, p, nm in zip(quads, pairs, names)]
    return mine, _sibling_swap(mine)


def _pad_cols(a, n):
    return jnp.concatenate([a, jnp.zeros((a.shape[0], n - a.shape[1]), a.dtype)], axis=1)


def _group_channels(a):
    lead = a.shape[:-1]
    xs = a[..., :D_INNER].reshape(lead + (SSM_GROUPS, GROUP_W))
    bs = a[..., D_INNER:D_INNER + SSM_GROUPS * STATE_N].reshape(lead + (SSM_GROUPS, STATE_N))
    cs = a[..., D_INNER + SSM_GROUPS * STATE_N:].reshape(lead + (SSM_GROUPS, STATE_N))
    return jnp.concatenate([xs, bs, cs], axis=-1).reshape(lead + (CONV_CH,))


PROJ_SEGS = (('gate_a', D_MODEL), ('gate_b', D_MODEL), ('z', D_INNER), ('xbc', CONV_CH), ('q_lat', Q_RANK),
             ('kv_lat', KV_RANK), ('k_rope', LANE), ('dt', LANE))
PROJ_WIDE = sum(w for _, w in PROJ_SEGS[:4])
PROJ_LANE0 = {n: (v if v < PROJ_WIDE else v - PROJ_WIDE) for n, v in
              zip([n for n, _ in PROJ_SEGS], [int(v) for v in np.cumsum([0] + [w for _, w in PROJ_SEGS])[:-1]])}
CONV_LANE0 = PROJ_LANE0['xbc']
KR_LANE0 = PROJ_LANE0['k_rope']


def _lay_w_in(w):
    idx = np.cumsum(IN_SIZES)[:-1]
    q_lat, kv_lat, k_rope, z, xbc, dt, gate_a, gate_b = jnp.split(w, [int(v) for v in idx], axis=1)
    return jnp.concatenate([gate_a, gate_b, z, _group_channels(xbc), q_lat, kv_lat, _pad_cols(k_rope, LANE),
                            _pad_cols(dt, LANE)], axis=1)


@jax.custom_vjp
def project(h, w, tok):
    return _project_impl(h, w)


def _project_impl(h, w):
    return (_mm(h, w[:, :PROJ_WIDE], "w_in_fwd", BF16), _mm(h, w[:, PROJ_WIDE:], "w_in_narrow_fwd")) + tuple(
        jnp.zeros((h.shape[0], wd), BF16) for _, wd in PROJ_SEGS)


def _project_fwd(h, w, tok):
    return _project_impl(h, w), (h, w)


def _project_bwd(res, cots):
    h, w = res
    g = jnp.concatenate(cots[2:], axis=1)
    return _mm(g, w.T, "w_in_dx", h.dtype), jnp.zeros_like(w), _mm(h.T, g, "w_in_dw")


project.defvjp(_project_fwd, _project_bwd)


def _lay_w_uq(w):
    w3 = w.reshape(Q_RANK, N_HEADS, NOPE + ROPE)
    w3 = jnp.concatenate([w3, jnp.zeros((Q_RANK, N_HEADS, QK_PAD - NOPE - ROPE), w.dtype)], axis=2)
    return w3.reshape(Q_RANK, N_HEADS * QK_PAD)


def _lay_w_ukv(w):
    w3 = w.reshape(KV_RANK, N_HEADS, NOPE + V_DIM)
    return jnp.concatenate([w3[:, :, :NOPE].reshape(KV_RANK, -1), w3[:, :, NOPE:].reshape(KV_RANK, -1)], axis=1)


def _pad_lanes(v, n=LANE):
    return jnp.concatenate([v, jnp.zeros((v.shape[0], n - v.shape[1]), v.dtype)], axis=1)


def _local_loss(toks, small, x, wb, c8, posf, target):
    B, S, D = x.shape
    T = B * S

    def lin(name, a, key, lay=lambda w: w, out_dtype=F32):
        return make_linear(name, out_dtype)(a, lay(wb[key]), lay(toks[key]))

    rows2 = lambda a: a.reshape(T, a.shape[-1])
    rows3 = lambda a: a.reshape(B, S, a.shape[-1])

    sc = make_rowwise("silu_c", _f_silu, 1, 0, 0, ('row',))((c8[None],), (), ())[0][0]
    mod = make_linear("ada", F32, 4)(sc, wb['w_ada'], toks['w_ada'])[:B] + small['b_ada']
    shift1, scale1, gate1, shift2, scale2, gate2 = [m[:, None, :] for m in jnp.split(mod, 6, axis=-1)]

    h, x_res = make_rowwise("modulate1", _f_modulate, 1, 2, 1, ('row',), forward_row=0)(
        (x,), (scale1, shift1), (small['g_pre_mix'],))
    outs = project(rows2(h), _lay_w_in(wb['w_in']), _lay_w_in(toks['w_in']))
    wide = lax.stop_gradient(rows3(outs[0]))
    proj = lax.stop_gradient(rows3(outs[1]))
    stand = {n: rows3(o) for (n, _), o in zip(PROJ_SEGS, outs[2:])}

    def win(seg, block):
        return (PROJ_LANE0[seg] // block, dict(PROJ_SEGS)[seg])

    inv = ROPE_THETA ** (-jnp.arange(ROPE // 2, dtype=F32) / (ROPE // 2))
    inv_lane = jnp.concatenate([inv, inv, jnp.zeros((LANE - ROPE,), F32)])[None]
    tabs = tuple(_rope_tables(posf, inv_lane))
    qn = make_rowwise("rms_q", _f_rms, 1, 0, 1, ('row',), windows={0: win('q_lat', Q_RANK)})(
        (proj,), (), (small['g_q_lat'],), (stand['q_lat'],))[0]
    kvn = make_rowwise("rms_kv", _f_rms, 1, 0, 1, ('row',), windows={0: win('kv_lat', KV_RANK)})(
        (proj,), (), (small['g_kv_lat'],), (stand['kv_lat'],))[0]
    qp = rows3(lin("w_uq", rows2(qn), 'w_uq', _lay_w_uq, BF16))
    kvp = rows3(lin("w_ukv", rows2(kvn), 'w_ukv', _lay_w_ukv, BF16))
    qr = rope_q(qp, tabs)
    kr = build_k(kvp, proj, stand['k_rope'], tabs)
    att = attention(qr, kr, kvp)
    attn = rows3(lin("w_o_attn", rows2(att), 'w_o_attn', out_dtype=BF16))

    xa = conv_silu(wide, stand['xbc'], _group_channels(wb['conv_w_f32']), _group_channels(small['conv_b']))
    dt_pad, a_pad = make_rowwise("dt_softplus", _f_dt, 1, 0, 2, ('row', 'row'), windows={0: win('dt', LANE)})(
        (proj,), (), (_pad_lanes(small['dt_bias']), _pad_lanes(small['a_log'])), (stand['dt'],))
    ac_pad = chunk_cumsum(a_pad)
    acr = jnp.transpose(ac_pad[..., :SSM_HEADS], (0, 2, 1))[:, :, None, :]
    dsk = jnp.repeat(small['d_skip'], HEAD_P, axis=-1)
    y = ssd(xa, dt_pad, ac_pad, acr, dsk)
    yg = make_rowwise("gated_norm", _f_gated_norm, 2, 0, 1, ('row',), ncol=SSM_GROUPS, ts_cap=2048,
                      windows={1: win('z', GROUP_W)})((y, wide), (), (small['g_ssm_out'],), (stand['z'],))[0]
    ssm = rows3(lin("w_o_ssm", rows2(yg), 'w_o_ssm', out_dtype=BF16))

    merged = make_rowwise("merge", _f_merge, 4, 0, 0, ('row',),
                          windows={2: win('gate_a', D_MODEL), 3: win('gate_b', D_MODEL)})(
        (attn, ssm, wide, wide), (), (), (stand['gate_a'], stand['gate_b']))[0]
    mix = rows3(lin("w_out", rows2(merged), 'w_out', out_dtype=BF16))
    x1 = make_rowwise("post_mix", _f_post, 2, 1, 1, ('row',))((x_res, mix), (gate1,), (small['g_post_mix'],))[0]

    h2, x1_res = make_rowwise("modulate2", _f_modulate, 1, 2, 1, ('row',), forward_row=0)(
        (x1,), (scale2, shift2), (small['g_pre_mlp'],))
    ff = rows3(ffn(rows2(h2), wb['w_ff1'], toks['w_ff1'], wb['w_ff2'], toks['w_ff2']))
    lvec = make_rowwise("final_loss", _f_final_loss, 3, 1, 1, ('sum',), nodiff=(2,))(
        (x1_res, ff, target), (gate2,), (small['g_post_mlp'],))[0]
    return jnp.sum(lvec)


MATRICES = COL_SHARDED + ROW_SHARDED
STACKED_DW = ('w_ada', 'w_ff1')


def _local_step(x, c, positions, target, wb, small):
    B = x.shape[0]
    c8 = jnp.concatenate([c, jnp.zeros((16 - B, c.shape[1]), F32)], axis=0)
    posf = positions.astype(F32)[..., None]
    toks = {k: jnp.zeros(wb[k].shape, F32) for k in MATRICES if k != 'conv_w'}
    for k in STACKED_DW:
        rows, cols = wb[k].shape
        toks[k] = jnp.zeros((4, rows, cols // 4), F32)
    conv_w = wb['conv_w_f32']

    def loss_fn(toks, small, conv_w, x):
        wbl = dict(wb)
        wbl['conv_w_f32'] = conv_w
        return _local_loss(toks, small, x, wbl, c8, posf, target)

    loss, (g_tok, g_small, g_conv, g_x) = jax.value_and_grad(loss_fn, argnums=(0, 1, 2, 3))(toks, small, conv_w, x)
    grads = dict(g_tok)
    grads.update(g_small)
    grads['conv_w'] = g_conv
    return loss, g_x, grads


def kernel(x, c, positions, w_ada, b_ada, g_pre_mix, g_post_mix, w_in, g_q_lat, g_kv_lat, w_uq, w_ukv, w_o_attn, conv_w, conv_b, dt_bias, a_log, d_skip, g_ssm_out, w_o_ssm, w_out, g_pre_mlp, g_post_mlp, w_ff1, w_ff2, loss_target, m_w_ada, m_b_ada, m_g_pre_mix, m_g_post_mix, m_w_in, m_g_q_lat, m_g_kv_lat, m_w_uq, m_w_ukv, m_w_o_attn, m_conv_w, m_conv_b, m_dt_bias, m_a_log, m_d_skip, m_g_ssm_out, m_w_o_ssm, m_w_out, m_g_pre_mlp, m_g_post_mlp, m_w_ff1, m_w_ff2, v_w_ada, v_b_ada, v_g_pre_mix, v_g_post_mix, v_w_in, v_g_q_lat, v_g_kv_lat, v_w_uq, v_w_ukv, v_w_o_attn, v_conv_w, v_conv_b, v_dt_bias, v_a_log, v_d_skip, v_g_ssm_out, v_w_o_ssm, v_w_out, v_g_pre_mlp, v_g_post_mlp, v_w_ff1, v_w_ff2):
    given = dict(locals())
    w_loc = {n: given[n] for n in WEIGHTS}
    m_loc = {n: given["m_" + n] for n in WEIGHTS}
    v_loc = {n: given["v_" + n] for n in WEIGHTS}
    mats = [n for n in WEIGHTS if n in MATRICES and n != 'conv_w']
    vecs = [n for n in WEIGHTS if n not in MATRICES]

    own = [w_loc[n][0].astype(BF16) for n in mats]
    g_mats, g_conv = _gather_weights(own, conv_w[0])
    chip = 2 * lax.axis_index("x") + lax.axis_index("y")
    wb = {}
    for n, g, mine in zip(mats, g_mats, own):
        g = lax.dynamic_update_slice_in_dim(g, mine[None], chip, axis=0)
        if n in COL_SHARDED:
            wb[n] = jnp.transpose(g, (1, 0, 2)).reshape(g.shape[1], -1)
        else:
            wb[n] = g.reshape(-1, g.shape[2])
    wb['conv_w_f32'] = jnp.transpose(g_conv, (1, 0, 2)).reshape(CONV_K, -1)
    small = {n: w_loc[n] for n in vecs}

    loss_part, grad_x, grads = _local_step(x, c, positions, loss_target, wb, small)
    loss = lax.psum(loss_part, ("x", "y", "c"))

    stacks = []
    for n in mats:
        kk, nn = w_loc[n].shape[1:]
        if n in STACKED_DW:
            stacks.append(grads[n])
        elif n in COL_SHARDED:
            stacks.append(jnp.transpose(grads[n].reshape(kk, 4, nn), (1, 0, 2)))
        else:
            stacks.append(grads[n].reshape(4, kk, nn))
    g_mine, g_other = _reduce_matrices(stacks, mats)
    g_shard = {}

    vec_shapes = [tuple(grads[n].shape) for n in vecs] + [tuple(grads['conv_w'].shape)]
    total = _stack_sum(_gather_small(_pack_small([grads[n] for n in vecs] + [grads['conv_w']])), "grad_sum_small")
    g_vec = _unpack_small(total, vec_shapes)
    n_conv = conv_w.shape[2]
    chip = 2 * lax.axis_index("x") + lax.axis_index("y")
    g_shard['conv_w'] = lax.dynamic_slice_in_dim(g_vec[-1], chip * n_conv, n_conv, axis=1)
    for n, g in zip(vecs, g_vec):
        g_shard[n] = g

    delta, new_m, new_v = {}, {}, {}
    cidx = lax.axis_index("c").astype(jnp.int32).reshape(1)
    for n, mine, other in zip(mats, g_mine, g_other):
        g_shard[n], delta[n], new_m[n], new_v[n] = _adam_halves_call(
            w_loc[n], mine, other, cidx, m_loc[n], v_loc[n], "adamw_" + n)
    rest = vecs + ['conv_w']
    rest_shapes = [tuple(w_loc[n].shape) for n in rest]
    packed = [_pack_small([src[n] for n in rest]) for src in (w_loc, g_shard, m_loc, v_loc)]
    for dst, buf in zip((delta, new_m, new_v), _adam_call(*packed, "adamw_small")):
        dst.update(zip(rest, _unpack_small(buf, rest_shapes)))

    def out(d):
        return [d[n].reshape(w_loc[n].shape) for n in WEIGHTS]

    return (loss, grad_x, *out(g_shard), *out(delta), *out(new_m), *out(new_v))
```

```python
import functools
import math

import numpy as np
import jax
import jax.numpy as jnp
from jax import lax
from jax.experimental import pallas as pl
from jax.experimental.pallas import tpu as pltpu

F32 = jnp.float32
BF16 = jnp.bfloat16
MESH = pl.DeviceIdType.MESH

D_MODEL = 1024
N_HEADS = 8
NOPE = 128
ROPE = 64
V_DIM = 128
Q_RANK = 256
KV_RANK = 256
ROPE_THETA = 10000.0
D_INNER = 2048
SSM_HEADS = 32
SSM_GROUPS = 8
HEAD_P = 64
STATE_N = 128
CONV_K = 4
CHUNK = 128
CONV_CH = D_INNER + 2 * SSM_GROUPS * STATE_N
D_FF = 4096
EPS = 1e-6
IN_SIZES = (Q_RANK, KV_RANK, ROPE, D_INNER, CONV_CH, SSM_HEADS, D_MODEL, D_MODEL)
ADAM_LR, ADAM_B1, ADAM_B2, ADAM_EPS, ADAM_WD, ADAM_STEP = 0.001, 0.9, 0.999, 1e-08, 0.01, 10

VMEM_LIMIT_BYTES = 52 * 1024 * 1024
LANE = 128
QK_PAD = 256

WEIGHTS = ['w_ada', 'b_ada', 'g_pre_mix', 'g_post_mix', 'w_in', 'g_q_lat', 'g_kv_lat', 'w_uq', 'w_ukv',
           'w_o_attn', 'conv_w', 'conv_b', 'dt_bias', 'a_log', 'd_skip', 'g_ssm_out', 'w_o_ssm', 'w_out',
           'g_pre_mlp', 'g_post_mlp', 'w_ff1', 'w_ff2']
COL_SHARDED = ('w_ada', 'w_in', 'w_uq', 'w_ukv', 'conv_w', 'w_ff1')
ROW_SHARDED = ('w_o_attn', 'w_o_ssm', 'w_out', 'w_ff2')


def _cparams(sem):
    return pltpu.CompilerParams(dimension_semantics=sem, vmem_limit_bytes=VMEM_LIMIT_BYTES)


def _tile(n, cap):
    if n <= cap:
        return n
    k = n // LANE
    best = LANE
    for d in range(1, k + 1):
        if k % d == 0 and d * LANE <= cap:
            best = d * LANE
    return best


def _mm(a, w, name, out_dtype=F32, epilogue=None, extras=(), out_dtypes=None):
    M, K = a.shape
    N = w.shape[1]
    tm = min(M, 1024)
    tn = _tile(N, 1024)
    tk = _tile(K, 2048)
    nk = K // tk
    dts = tuple(out_dtypes) if epilogue is not None else (out_dtype,)
    n_x, n_o = len(extras), len(dts)

    def finish(acc, refs):
        res = epilogue(acc, *[r[...] for r in refs[:n_x]]) if epilogue is not None else (acc,)
        for o_ref, val, dt in zip(refs[n_x:n_x + n_o], res, dts):
            o_ref[...] = val.astype(dt)

    def body(a_ref, w_ref, *refs):
        part = jnp.dot(a_ref[...].astype(BF16), w_ref[...], preferred_element_type=F32)
        if nk == 1:
            finish(part, refs)
        else:
            acc_ref = refs[-1]
            k = pl.program_id(2)

            @pl.when(k == 0)
            def _():
                acc_ref[...] = part

            @pl.when(k > 0)
            def _():
                acc_ref[...] += part

            @pl.when(k == nk - 1)
            def _():
                finish(acc_ref[...], refs)

    ospec = pl.BlockSpec((tm, tn), lambda i, j, k: (i, j))
    res = pl.pallas_call(
        body, grid=(M // tm, N // tn, nk),
        in_specs=[pl.BlockSpec((tm, tk), lambda i, j, k: (i, k)), pl.BlockSpec((tk, tn), lambda i, j, k: (k, j))]
        + [ospec] * n_x,
        out_specs=[ospec] * n_o, out_shape=[jax.ShapeDtypeStruct((M, N), dt) for dt in dts],
        scratch_shapes=[pltpu.VMEM((tm, tn), F32)] if nk > 1 else [], name=name,
        compiler_params=_cparams(("parallel", "parallel", "arbitrary")))(a, w, *extras)
    return res if epilogue is not None else res[0]


def _mm_tn(a, g, name, col_shards=1):
    M, K = a.shape
    N = g.shape[1]
    tm = min(M, 1024)
    tk = _tile(K, 1024)
    tn = _tile(N // col_shards, 1024)
    nm = M // tm
    per = N // col_shards // tn

    def body(a_ref, g_ref, o_ref):
        part = lax.dot_general(a_ref[...].astype(BF16), g_ref[...].astype(BF16), (((0,), (0,)), ((), ())),
                               preferred_element_type=F32)
        m = pl.program_id(2)

        @pl.when(m == 0)
        def _():
            o_ref[...] = part.reshape(o_ref.shape)

        @pl.when(m > 0)
        def _():
            o_ref[...] += part.reshape(o_ref.shape)

    if col_shards == 1:
        out_spec = pl.BlockSpec((tk, tn), lambda i, j, m: (i, j))
        out_shape = jax.ShapeDtypeStruct((K, N), F32)
    else:
        out_spec = pl.BlockSpec((1, tk, tn), lambda i, j, m: (j // per, i, j % per))
        out_shape = jax.ShapeDtypeStruct((col_shards, K, N // col_shards), F32)
    return pl.pallas_call(
        body, grid=(K // tk, N // tn, nm),
        in_specs=[pl.BlockSpec((tm, tk), lambda i, j, m: (m, i)), pl.BlockSpec((tm, tn), lambda i, j, m: (m, j))],
        out_specs=out_spec, out_shape=out_shape, name=name,
        compiler_params=_cparams(("parallel", "parallel", "arbitrary")))(a, g)


def make_linear(name, out_dtype=F32, dw_col_shards=1):
    @jax.custom_vjp
    def linear(a, w, tok):
        return _mm(a, w, name + "_fwd", out_dtype)

    def fwd(a, w, tok):
        return _mm(a, w, name + "_fwd", out_dtype), (a, w)

    def bwd(res, g):
        a, w = res
        da = _mm(g, w.T, name + "_dx", a.dtype)
        dw = _mm_tn(a, g, name + "_dw", dw_col_shards)
        return da, jnp.zeros_like(w), dw

    linear.defvjp(fwd, bwd)
    return linear


def _relu2_epilogue(acc):
    r = jnp.maximum(acc, 0.0)
    return r * r, r


def _relu2_bwd_epilogue(acc, r):
    return (acc * (2.0 * r.astype(F32)),)


@jax.custom_vjp
def ffn(h, w1, tok1, w2, tok2):
    act, _ = _mm(h, w1, "w_ff1_fwd", epilogue=_relu2_epilogue, out_dtypes=(BF16, BF16))
    return _mm(act, w2, "w_ff2_fwd", BF16)


def _ffn_fwd(h, w1, tok1, w2, tok2):
    act, r = _mm(h, w1, "w_ff1_fwd", epilogue=_relu2_epilogue, out_dtypes=(BF16, BF16))
    return _mm(act, w2, "w_ff2_fwd", BF16), (h, w1, w2, act, r)


def _ffn_bwd(res, g):
    h, w1, w2, act, r = res
    du = _mm(g, w2.T, "w_ff2_dx", epilogue=_relu2_bwd_epilogue, extras=(r,), out_dtypes=(BF16,))[0]
    dw2 = _mm_tn(act, g, "w_ff2_dw")
    dw1 = _mm_tn(h, du, "w_ff1_dw", 4)
    dh = _mm(du, w1.T, "w_ff1_dx", h.dtype)
    return dh, jnp.zeros_like(w1), dw1, jnp.zeros_like(w2), dw2


ffn.defvjp(_ffn_fwd, _ffn_bwd)


def make_rowwise(name, f, n_rows, n_seqs, n_pars, out_kinds, ncol=1, nodiff=(), ts_cap=512, windows=None,
                 forward_row=None):
    windows = dict(windows or {})
    n_in = n_rows + n_seqs + n_pars
    diff_idx = [i for i in range(n_in) if i not in nodiff]

    def _dims(rows):
        B, S = rows[0].shape[0], rows[0].shape[1]
        ts = min(S, ts_cap)
        return B, S, ts

    def _width(i, r):
        return windows[i][1] if i in windows else r.shape[2]

    def _in_specs(rows, seqs, pars, ts):
        specs = []
        for i, r in enumerate(rows):
            col0 = windows[i][0] if i in windows else 0
            specs.append(pl.BlockSpec((1, ts, _width(i, r) // ncol), lambda k, b, s, col0=col0: (b, s, k + col0)))
        for q in seqs:
            specs.append(pl.BlockSpec((1, 1, q.shape[2] // ncol), lambda k, b, s: (b, 0, k)))
        for p in pars:
            specs.append(pl.BlockSpec((1, p.shape[1] // ncol), lambda k, b, s: (0, k)))
        return specs

    def _load(refs):
        vals = [r[0] for r in refs[:n_rows + n_seqs]]
        vals += [r[...] for r in refs[n_rows + n_seqs:n_in]]
        return vals

    def _out_struct(rows, seqs, pars, ts):
        blocks = [jax.ShapeDtypeStruct((ts, _width(i, r) // ncol), r.dtype) for i, r in enumerate(rows)]
        blocks += [jax.ShapeDtypeStruct((1, q.shape[2] // ncol), q.dtype) for q in seqs]
        blocks += [jax.ShapeDtypeStruct((1, p.shape[1] // ncol), p.dtype) for p in pars]
        return jax.eval_shape(f, *blocks)

    def _fwd_call(rows, seqs, pars):
        B, S, ts = _dims(rows)
        outs = _out_struct(rows, seqs, pars, ts)
        n_out = len(outs)

        def body(*refs):
            res = f(*_load(refs))
            first = (pl.program_id(1) == 0) & (pl.program_id(2) == 0)
            for o_ref, val, kind in zip(refs[n_in:], res, out_kinds):
                if kind == 'row':
                    o_ref[0] = val
                else:
                    tot = jnp.sum(val, axis=0, keepdims=True)

                    @pl.when(first)
                    def _(o_ref=o_ref, tot=tot):
                        o_ref[...] = tot

                    @pl.when(jnp.logical_not(first))
                    def _(o_ref=o_ref, tot=tot):
                        o_ref[...] += tot

        out_shape, out_specs = [], []
        for o, kind in zip(outs, out_kinds):
            d = o.shape[1]
            if kind == 'row':
                out_shape.append(jax.ShapeDtypeStruct((B, S, ncol * d), o.dtype))
                out_specs.append(pl.BlockSpec((1, ts, d), lambda k, b, s: (b, s, k)))
            else:
                out_shape.append(jax.ShapeDtypeStruct((1, ncol * d), o.dtype))
                out_specs.append(pl.BlockSpec((1, d), lambda k, b, s: (0, k)))
        res = pl.pallas_call(
            body, grid=(ncol, B, S // ts), in_specs=_in_specs(rows, seqs, pars, ts), out_specs=out_specs,
            out_shape=out_shape, name=name + "_fwd",
            compiler_params=_cparams(("arbitrary", "arbitrary", "arbitrary")))(*rows, *seqs, *pars)
        return tuple(res)

    def _bwd_call(rows, seqs, pars, cots, carried=None):
        B, S, ts = _dims(rows)
        outs = _out_struct(rows, seqs, pars, ts)
        n_out = len(outs)
        all_in = list(rows) + list(seqs) + list(pars)
        extra = [] if carried is None else [carried]

        def body(*refs):
            vals = _load(refs)
            if carried is not None:
                carried_ref, refs = refs[n_in + n_out], refs[:n_in + n_out] + refs[n_in + n_out + 1:]
            cts = []
            for c_ref, o, kind in zip(refs[n_in:n_in + n_out], outs, out_kinds):
                if kind == 'row':
                    cts.append(c_ref[0])
                else:
                    cts.append(jnp.broadcast_to(c_ref[...], o.shape))

            def g(*dv):
                full = list(vals)
                for i, v in zip(diff_idx, dv):
                    full[i] = v
                return tuple(f(*full))

            _, vjp = jax.vjp(g, *[vals[i] for i in diff_idx])
            grads = vjp(tuple(cts))
            b, s = pl.program_id(1), pl.program_id(2)
            for o_ref, i, gr in zip(refs[n_in + n_out:], diff_idx, grads):
                if i < n_rows:
                    if carried is not None and i == forward_row:
                        gr = gr + carried_ref[0]
                    o_ref[0] = gr.astype(o_ref.dtype)
                else:
                    first = (s == 0) if i < n_rows + n_seqs else ((b == 0) & (s == 0))
                    target = (lambda r: r.at[0]) if i < n_rows + n_seqs else (lambda r: r)

                    @pl.when(first)
                    def _(o_ref=o_ref, gr=gr, target=target):
                        target(o_ref)[...] = gr

                    @pl.when(jnp.logical_not(first))
                    def _(o_ref=o_ref, gr=gr, target=target):
                        target(o_ref)[...] += gr

        cot_specs = []
        for o, kind in zip(outs, out_kinds):
            d = o.shape[1]
            if kind == 'row':
                cot_specs.append(pl.BlockSpec((1, ts, d), lambda k, b, s: (b, s, k)))
            else:
                cot_specs.append(pl.BlockSpec((1, d), lambda k, b, s: (0, k)))
        out_shape, out_specs = [], []
        for i in diff_idx:
            a = all_in[i]
            if i < n_rows:
                out_shape.append(jax.ShapeDtypeStruct((B, S, _width(i, a)), BF16 if i in windows else a.dtype))
                out_specs.append(pl.BlockSpec((1, ts, _width(i, a) // ncol), lambda k, b, s: (b, s, k)))
                continue
            out_shape.append(jax.ShapeDtypeStruct(a.shape, a.dtype))
            if i < n_rows + n_seqs:
                out_specs.append(pl.BlockSpec((1, 1, a.shape[2] // ncol), lambda k, b, s: (b, 0, k)))
            else:
                out_specs.append(pl.BlockSpec((1, a.shape[1] // ncol), lambda k, b, s: (0, k)))
        if carried is not None:
            cot_specs.append(pl.BlockSpec((1, ts, carried.shape[2] // ncol), lambda k, b, s: (b, s, k)))
        res = pl.pallas_call(
            body, grid=(ncol, B, S // ts), in_specs=_in_specs(rows, seqs, pars, ts) + cot_specs,
            out_specs=out_specs, out_shape=out_shape, name=name + "_bwd",
            compiler_params=_cparams(("arbitrary", "arbitrary", "arbitrary")))(*all_in, *cots, *extra)
        grads = [None] * n_in
        for i, r in zip(diff_idx, res):
            grads[i] = r
        for i in nodiff:
            grads[i] = jnp.zeros_like(all_in[i])
        stand_in_grads = tuple(grads[i] for i in sorted(windows))
        for i in windows:
            grads[i] = jnp.zeros_like(all_in[i])
        return (tuple(grads[:n_rows]), tuple(grads[n_rows:n_rows + n_seqs]), tuple(grads[n_rows + n_seqs:]),
                stand_in_grads)

    def _outputs(rows, seqs, pars):
        res = _fwd_call(rows, seqs, pars)
        return res if forward_row is None else res + (rows[forward_row],)

    @jax.custom_vjp
    def op(rows, seqs, pars, stand_ins):
        return _outputs(rows, seqs, pars)

    def fwd(rows, seqs, pars, stand_ins):
        return _outputs(rows, seqs, pars), (rows, seqs, pars)

    def bwd(res, cots):
        rows, seqs, pars = res
        if forward_row is None:
            return _bwd_call(rows, seqs, pars, cots)
        return _bwd_call(rows, seqs, pars, cots[:-1], cots[-1])

    op.defvjp(fwd, bwd)
    return lambda rows, seqs, pars, stand_ins=(): op(tuple(rows), tuple(seqs), tuple(pars), tuple(stand_ins))


def _rms(x, g):
    x = x.astype(F32)
    return x * lax.rsqrt(jnp.mean(x * x, axis=-1, keepdims=True) + EPS) * g


def _silu(x):
    return x * lax.logistic(x)


def _f_silu(c):
    return (_silu(c),)


def _f_modulate(x, scale, shift, g):
    return ((_rms(x, g) * (1.0 + scale) + shift).astype(BF16),)


def _f_rms(x, g):
    return (_rms(x, g).astype(BF16),)


def _f_dt(dt_raw, dt_bias, a_log):
    z = dt_raw + dt_bias
    dt = jnp.maximum(z, 0.0) + jnp.log1p(jnp.exp(-jnp.abs(z)))
    return dt, dt * (-jnp.exp(a_log))


def _f_gated_norm(y, z, g):
    return (_rms(y * _silu(z.astype(F32)), g).astype(BF16),)


def _f_merge(attn, ssm, ga, gb):
    return ((lax.logistic(ga.astype(F32)) * attn + lax.logistic(gb.astype(F32)) * ssm).astype(BF16),)


def _f_post(x, m, gate, g):
    return (x + gate * _rms(m, g),)


def _f_final_loss(x, ff, target, gate, g):
    e = x + gate * _rms(ff, g) - target
    return (e * e * (0.5 / D_MODEL),)


def _rope_tables(posf, inv_lane):
    B, S, _ = posf.shape
    ts = min(S, 512)

    def body(p_ref, inv_ref, c_ref, a_ref, b_ref):
        ang = p_ref[0] * inv_ref[...]
        cs, sn = jnp.cos(ang), jnp.sin(ang)
        lane = lax.broadcasted_iota(jnp.int32, ang.shape, 1)
        c_ref[0] = jnp.where(lane < ROPE, cs, 0.0)
        a_ref[0] = jnp.where(lane < ROPE // 2, -sn, 0.0)
        b_ref[0] = jnp.where((lane >= ROPE // 2) & (lane < ROPE), sn, 0.0)

    spec = pl.BlockSpec((1, ts, LANE), lambda b, s: (b, s, 0))
    sds = jax.ShapeDtypeStruct((B, S, LANE), F32)
    return pl.pallas_call(
        body, grid=(B, S // ts),
        in_specs=[pl.BlockSpec((1, ts, 1), lambda b, s: (b, s, 0)), pl.BlockSpec((1, LANE), lambda b, s: (0, 0))],
        out_specs=[spec, spec, spec], out_shape=[sds, sds, sds], name="rope_tables",
        compiler_params=_cparams(("parallel", "parallel")))(posf, inv_lane)


def _rot(u, c, a, bm):
    return u * c + pltpu.roll(u, 96, 1) * a + pltpu.roll(u, 32, 1) * bm


def _rot_t(g, c, a, bm):
    return g * c + pltpu.roll(g * a, 32, 1) + pltpu.roll(g * bm, 96, 1)


def _rope_q_call(q, tabs, transpose, name):
    B, S, W = q.shape
    ts = min(S, 512)
    fn = _rot_t if transpose else _rot
    out_dtype = BF16

    def body(q_ref, c_ref, a_ref, b_ref, o_ref):
        tc, ta, tb = c_ref[0], a_ref[0], b_ref[0]
        for h in range(W // QK_PAD):
            u = q_ref[0, :, h * QK_PAD:(h + 1) * QK_PAD].astype(F32) * ATT_SCALE
            r = fn(u[:, NOPE:], tc, ta, tb)
            o_ref[0, :, h * QK_PAD:(h + 1) * QK_PAD] = jnp.concatenate([u[:, :NOPE], r], axis=1).astype(out_dtype)

    tspec = pl.BlockSpec((1, ts, LANE), lambda b, s: (b, s, 0))
    qspec = pl.BlockSpec((1, ts, W), lambda b, s: (b, s, 0))
    return pl.pallas_call(
        body, grid=(B, S // ts), in_specs=[qspec, tspec, tspec, tspec], out_specs=qspec,
        out_shape=jax.ShapeDtypeStruct(q.shape, out_dtype), name=name,
        compiler_params=_cparams(("parallel", "parallel")))(q, *tabs)


@jax.custom_vjp
def rope_q(q, tabs):
    return _rope_q_call(q, tabs, False, "rope_q_fwd")


def _rope_q_fwd(q, tabs):
    return _rope_q_call(q, tabs, False, "rope_q_fwd"), tabs


def _rope_q_bwd(tabs, g):
    return _rope_q_call(g, tabs, True, "rope_q_bwd"), tuple(jnp.zeros_like(t) for t in tabs)


rope_q.defvjp(_rope_q_fwd, _rope_q_bwd)


def _build_k_fwd_call(kv, kr, tabs):
    B, S, _ = kv.shape
    ts = min(S, 512)

    def body(kv_ref, kr_ref, c_ref, a_ref, b_ref, o_ref):
        r = _rot(kr_ref[0], c_ref[0], a_ref[0], b_ref[0]).astype(BF16)
        for h in range(N_HEADS):
            o_ref[0, :, h * QK_PAD:(h + 1) * QK_PAD] = jnp.concatenate(
                [kv_ref[0, :, h * NOPE:(h + 1) * NOPE], r], axis=1)

    tspec = pl.BlockSpec((1, ts, LANE), lambda b, s: (b, s, 0))
    kr_spec = pl.BlockSpec((1, ts, LANE), lambda b, s: (b, s, KR_LANE0 // LANE))
    return pl.pallas_call(
        body, grid=(B, S // ts),
        in_specs=[pl.BlockSpec((1, ts, N_HEADS * NOPE), lambda b, s: (b, s, 0)), kr_spec, tspec, tspec, tspec],
        out_specs=pl.BlockSpec((1, ts, N_HEADS * QK_PAD), lambda b, s: (b, s, 0)),
        out_shape=jax.ShapeDtypeStruct((B, S, N_HEADS * QK_PAD), BF16), name="build_k_fwd",
        compiler_params=_cparams(("parallel", "parallel")))(kv, kr, *tabs)


def _build_k_bwd_call(g, tabs):
    B, S, _ = g.shape
    ts = min(S, 512)

    def body(g_ref, c_ref, a_ref, b_ref, dk_ref, dr_ref):
        tot = None
        for h in range(N_HEADS):
            dk_ref[0, :, h * NOPE:(h + 1) * NOPE] = g_ref[0, :, h * QK_PAD:h * QK_PAD + NOPE]
            part = g_ref[0, :, h * QK_PAD + NOPE:(h + 1) * QK_PAD].astype(F32)
            tot = part if tot is None else tot + part
        dr_ref[0] = _rot_t(tot, c_ref[0], a_ref[0], b_ref[0]).astype(BF16)

    tspec = pl.BlockSpec((1, ts, LANE), lambda b, s: (b, s, 0))
    return pl.pallas_call(
        body, grid=(B, S // ts),
        in_specs=[pl.BlockSpec((1, ts, N_HEADS * QK_PAD), lambda b, s: (b, s, 0)), tspec, tspec, tspec],
        out_specs=[pl.BlockSpec((1, ts, N_HEADS * NOPE), lambda b, s: (b, s, 0)), tspec],
        out_shape=[jax.ShapeDtypeStruct((B, S, N_HEADS * NOPE), BF16), jax.ShapeDtypeStruct((B, S, LANE), BF16)],
        name="build_k_bwd", compiler_params=_cparams(("parallel", "parallel")))(g, *tabs)


ATT_SCALE = (NOPE + ROPE) ** -0.5
NEG = -1e30


def _att_tiles(S):
    t = min(S, 512)
    return t, S // t


def _scores(q, k, diagonal):
    s = lax.dot_general(q, k, (((1,), (1,)), ((), ())), preferred_element_type=F32)
    if diagonal:
        row = lax.broadcasted_iota(jnp.int32, s.shape, 0)
        col = lax.broadcasted_iota(jnp.int32, s.shape, 1)
        s = jnp.where(col <= row, s, NEG)
    return s


ATT_HB = 8


def _causal_pairs(n):
    pairs = [(i, j) for i in range(n) for j in range(i + 1)]
    return (jnp.asarray([p[0] for p in pairs], jnp.int32), jnp.asarray([p[1] for p in pairs], jnp.int32))


def _head(ref_or_val, h, w):
    return ref_or_val[:, h * w:(h + 1) * w]


def _attn_fwd_call(q, k, vsrc, v_blk0):
    B, S, _ = q.shape
    t, n = _att_tiles(S)
    qi, kj = _causal_pairs(n)

    def body(qi_ref, kj_ref, q_ref, k_ref, v_ref, o_ref, lse_ref, m_sc, l_sc, acc_sc):
        p_id = pl.program_id(2)
        i, j = qi_ref[p_id], kj_ref[p_id]

        @pl.when(j == 0)
        def _():
            m_sc[...] = jnp.full(m_sc.shape, NEG, F32)
            l_sc[...] = jnp.zeros(l_sc.shape, F32)
            acc_sc[...] = jnp.zeros(acc_sc.shape, F32)

        def step(diagonal):
            qa, ka, va = q_ref[0], k_ref[0], v_ref[0]
            for h in range(ATT_HB):
                lanes = slice(h * LANE, (h + 1) * LANE)
                s = _scores(_head(qa, h, QK_PAD), _head(ka, h, QK_PAD), diagonal)
                m_prev = m_sc[:, lanes]
                m_new = jnp.maximum(m_prev, jnp.max(s, axis=1, keepdims=True))
                alpha = jnp.exp(m_prev - m_new)
                p = jnp.exp(s - jnp.tile(m_new, (1, t // LANE)))
                l_sc[:, lanes] = alpha * l_sc[:, lanes] + jnp.sum(p, axis=1, keepdims=True)
                acc_sc[:, lanes] = alpha * acc_sc[:, lanes] + jnp.dot(p.astype(BF16), _head(va, h, V_DIM),
                                                                      preferred_element_type=F32)
                m_sc[:, lanes] = m_new

        @pl.when(j < i)
        def _():
            step(False)

        @pl.when(j == i)
        def _():
            step(True)
            o_ref[0] = (acc_sc[...] / l_sc[...]).astype(BF16)
            lse_ref[0] = m_sc[...] + jnp.log(l_sc[...])

    wq, wv = ATT_HB * QK_PAD, ATT_HB * V_DIM
    grid_spec = pltpu.PrefetchScalarGridSpec(
        num_scalar_prefetch=2, grid=(B, N_HEADS // ATT_HB, qi.shape[0]),
        in_specs=[pl.BlockSpec((1, t, wq), lambda b, h, p, qi, kj: (b, qi[p], h)),
                  pl.BlockSpec((1, t, wq), lambda b, h, p, qi, kj: (b, kj[p], h)),
                  pl.BlockSpec((1, t, wv), lambda b, h, p, qi, kj: (b, kj[p], v_blk0 + h))],
        out_specs=[pl.BlockSpec((1, t, wv), lambda b, h, p, qi, kj: (b, qi[p], h)),
                   pl.BlockSpec((1, t, wv), lambda b, h, p, qi, kj: (b, qi[p], h))],
        scratch_shapes=[pltpu.VMEM((t, wv), F32), pltpu.VMEM((t, wv), F32), pltpu.VMEM((t, wv), F32)])
    return pl.pallas_call(
        body, grid_spec=grid_spec,
        out_shape=[jax.ShapeDtypeStruct((B, S, N_HEADS * V_DIM), BF16),
                   jax.ShapeDtypeStruct((B, S, N_HEADS * LANE), F32)],
        name="attn_fwd", compiler_params=_cparams(("parallel", "parallel", "arbitrary")))(qi, kj, q, k, vsrc)


def _attn_p_ds(q, k, v, o, do, lse, diagonal, t):
    s = _scores(q, k, diagonal)
    p = jnp.exp(s - jnp.tile(lse, (1, t // LANE)))
    dp = lax.dot_general(do.astype(BF16), v, (((1,), (1,)), ((), ())), preferred_element_type=F32)
    delta = jnp.sum(do.astype(F32) * o.astype(F32), axis=1, keepdims=True)
    ds = p * (dp - delta)
    return p, ds


ATT_HB_BWD = 4


def _attn_bwd_call(q, k, vsrc, o, do, lse):
    B, S, _ = q.shape
    t, n = _att_tiles(S)
    qi, kj = _causal_pairs(n)
    n_pairs = qi.shape[0]
    hb = ATT_HB_BWD
    v_blk0 = N_HEADS // hb

    def body(qi_ref, kj_ref, q_ref, k_ref, v_ref, o_ref, do_ref, lse_ref, dq_ref, dk_ref, dv_ref, dq_sc, dk_sc, dv_sc):
        p_id = pl.program_id(2)
        i, j = qi_ref[p_id], kj_ref[p_id]

        @pl.when(p_id == 0)
        def _():
            dk_sc[...] = jnp.zeros(dk_sc.shape, F32)
            dv_sc[...] = jnp.zeros(dv_sc.shape, F32)

        @pl.when(j == 0)
        def _():
            dq_sc[...] = jnp.zeros(dq_sc.shape, F32)

        rows = pl.ds(pl.multiple_of(j * t, t), t)

        def step(diagonal):
            qa, ka, va, oa, doa, la = q_ref[0], k_ref[0], v_ref[0], o_ref[0], do_ref[0], lse_ref[0]
            for h in range(hb):
                qb, kb, dob = _head(qa, h, QK_PAD), _head(ka, h, QK_PAD), _head(doa, h, V_DIM)
                p, ds = _attn_p_ds(qb, kb, _head(va, h, V_DIM), _head(oa, h, V_DIM), dob, _head(la, h, LANE),
                                   diagonal, t)
                dsb = ds.astype(BF16)
                dq_sc[:, h * QK_PAD:(h + 1) * QK_PAD] += jnp.dot(dsb, kb, preferred_element_type=F32)
                dv_sc[rows, h * V_DIM:(h + 1) * V_DIM] += lax.dot_general(
                    p.astype(BF16), dob.astype(BF16), (((0,), (0,)), ((), ())), preferred_element_type=F32)
                dk_sc[rows, h * QK_PAD:(h + 1) * QK_PAD] += lax.dot_general(
                    dsb, qb, (((0,), (0,)), ((), ())), preferred_element_type=F32)

        @pl.when(j < i)
        def _():
            step(False)

        @pl.when(j == i)
        def _():
            step(True)
            dq_ref[0] = dq_sc[...].astype(BF16)

        @pl.when(i == n - 1)
        def _():
            dk_ref[0] = dk_sc[rows, :].astype(BF16)
            dv_ref[0] = dv_sc[rows, :].astype(BF16)

    wq, wv = hb * QK_PAD, hb * V_DIM
    at_q = lambda b, h, p, qi, kj: (b, qi[p], h)
    at_k = lambda b, h, p, qi, kj: (b, kj[p], h)
    at_done = lambda b, h, p, qi, kj: (b, jnp.where(qi[p] == n - 1, kj[p], 0), h)
    grid_spec = pltpu.PrefetchScalarGridSpec(
        num_scalar_prefetch=2, grid=(B, N_HEADS // hb, n_pairs),
        in_specs=[pl.BlockSpec((1, t, wq), at_q), pl.BlockSpec((1, t, wq), at_k),
                  pl.BlockSpec((1, t, wv), lambda b, h, p, qi, kj: (b, kj[p], v_blk0 + h)),
                  pl.BlockSpec((1, t, wv), at_q), pl.BlockSpec((1, t, wv), at_q), pl.BlockSpec((1, t, wv), at_q)],
        out_specs=[pl.BlockSpec((1, t, wq), at_q), pl.BlockSpec((1, t, wq), at_done), pl.BlockSpec((1, t, wv), at_done)],
        scratch_shapes=[pltpu.VMEM((t, wq), F32), pltpu.VMEM((S, wq), F32), pltpu.VMEM((S, wv), F32)])
    return pl.pallas_call(
        body, grid_spec=grid_spec,
        out_shape=[jax.ShapeDtypeStruct((B, S, N_HEADS * QK_PAD), BF16),
                   jax.ShapeDtypeStruct((B, S, N_HEADS * QK_PAD), BF16),
                   jax.ShapeDtypeStruct((B, S, N_HEADS * V_DIM), BF16)],
        name="attn_bwd", compiler_params=_cparams(("parallel", "parallel", "arbitrary")))(
            qi, kj, q, k, vsrc, o, do, lse)


@jax.custom_vjp
def attention(q, kv, src, stand_in, tabs):
    return _attn_fwd_call(q, _build_k_fwd_call(kv, src, tabs), kv, N_HEADS // ATT_HB)[0]


def _attention_fwd(q, kv, src, stand_in, tabs):
    k = _build_k_fwd_call(kv, src, tabs)
    o, lse = _attn_fwd_call(q, k, kv, N_HEADS // ATT_HB)
    return o, (q, k, kv, o, lse, src, tabs)


def _attention_bwd(res, do):
    q, k, kv, o, lse, src, tabs = res
    dq, dk, dv = _attn_bwd_call(q, k, kv, o, do, lse)
    dk_nope, dk_rope = _build_k_bwd_call(dk, tabs)
    return (dq, jnp.concatenate([dk_nope, dv], axis=-1), jnp.zeros_like(src), dk_rope,
            tuple(jnp.zeros_like(t) for t in tabs))


attention.defvjp(_attention_fwd, _attention_bwd)


SUBLANES = 8


def _zero_tail(v):
    return jnp.concatenate([v, jnp.zeros((SUBLANES, v.shape[1]), v.dtype)], axis=0)


def _shift_down(vz, sh):
    return pltpu.roll(vz, sh, 0)[:vz.shape[0] - SUBLANES]


def _shift_up(vz, sh):
    return pltpu.roll(vz, vz.shape[0] - sh, 0)[:vz.shape[0] - SUBLANES]


def _conv_pre(u, uz, w_ref, b_ref):
    acc = b_ref[...] + w_ref[pl.ds(CONV_K - 1, 1), :] * u
    for k in range(CONV_K - 1):
        acc = acc + w_ref[pl.ds(k, 1), :] * _shift_down(uz, CONV_K - 1 - k)
    return acc


def _conv_fwd_call(src, w, b):
    B, S, _ = src.shape
    C = w.shape[1]

    def body(u_ref, w_ref, b_ref, o_ref):
        uu = u_ref[0].astype(F32)
        o_ref[0] = _silu(_conv_pre(uu, _zero_tail(uu), w_ref, b_ref))

    spec = pl.BlockSpec((1, S, LANE), lambda c, bb: (bb, 0, c))
    return pl.pallas_call(
        body, grid=(C // LANE, B),
        in_specs=[pl.BlockSpec((1, S, LANE), lambda c, bb: (bb, 0, c + CONV_LANE0 // LANE)),
                  pl.BlockSpec((CONV_K, LANE), lambda c, bb: (0, c)), pl.BlockSpec((1, LANE), lambda c, bb: (0, c))],
        out_specs=spec, out_shape=jax.ShapeDtypeStruct((B, S, C), F32), name="conv_fwd",
        compiler_params=_cparams(("parallel", "arbitrary")))(src, w, b)


def _conv_bwd_call(src, w, b, g):
    B, S, _ = src.shape
    C = w.shape[1]

    def body(u_ref, w_ref, b_ref, g_ref, du_ref, dw_ref, db_ref):
        uu = u_ref[0].astype(F32)
        uz = _zero_tail(uu)
        pre = _conv_pre(uu, uz, w_ref, b_ref)
        sg = lax.logistic(pre)
        dpre = g_ref[0] * sg * (1.0 + pre * (1.0 - sg))
        dz = _zero_tail(dpre)
        du = w_ref[pl.ds(CONV_K - 1, 1), :] * dpre
        dws = [None] * CONV_K
        dws[CONV_K - 1] = jnp.sum(dpre * uu, axis=0, keepdims=True)
        for k in range(CONV_K - 1):
            sh = CONV_K - 1 - k
            du = du + w_ref[pl.ds(k, 1), :] * _shift_up(dz, sh)
            dws[k] = jnp.sum(dpre * _shift_down(uz, sh), axis=0, keepdims=True)
        du_ref[0] = du.astype(du_ref.dtype)
        dbv = jnp.sum(dpre, axis=0, keepdims=True)
        first = pl.program_id(1) == 0

        @pl.when(first)
        def _():
            for k in range(CONV_K):
                dw_ref[pl.ds(k, 1), :] = dws[k]
            db_ref[...] = dbv

        @pl.when(jnp.logical_not(first))
        def _():
            for k in range(CONV_K):
                dw_ref[pl.ds(k, 1), :] += dws[k]
            db_ref[...] += dbv

    spec = pl.BlockSpec((1, S, LANE), lambda c, bb: (bb, 0, c))
    wspec = pl.BlockSpec((CONV_K, LANE), lambda c, bb: (0, c))
    bspec = pl.BlockSpec((1, LANE), lambda c, bb: (0, c))
    uspec = pl.BlockSpec((1, S, LANE), lambda c, bb: (bb, 0, c + CONV_LANE0 // LANE))
    return pl.pallas_call(
        body, grid=(C // LANE, B), in_specs=[uspec, wspec, bspec, spec], out_specs=[spec, wspec, bspec],
        out_shape=[jax.ShapeDtypeStruct((B, S, C), BF16), jax.ShapeDtypeStruct(w.shape, F32),
                   jax.ShapeDtypeStruct(b.shape, F32)],
        name="conv_bwd", compiler_params=_cparams(("parallel", "arbitrary")))(src, w, b, g)


@jax.custom_vjp
def conv_silu(src, stand_in, w, b):
    return _conv_fwd_call(src, w, b)


def _conv_silu_fwd(src, stand_in, w, b):
    return _conv_fwd_call(src, w, b), (src, w, b)


def _conv_silu_bwd(res, g):
    du, dw, db = _conv_bwd_call(*res, g)
    return jnp.zeros_like(res[0]), du, dw, db


conv_silu.defvjp(_conv_silu_fwd, _conv_silu_bwd)


def _chunk_cumsum_call(a, reverse, name):
    B, S, W = a.shape
    per_step = min(S // CHUNK, 8)

    def body(a_ref, o_ref):
        r = lax.broadcasted_iota(jnp.int32, (CHUNK, CHUNK), 0)
        c = lax.broadcasted_iota(jnp.int32, (CHUNK, CHUNK), 1)
        tri = jnp.where((c >= r) if reverse else (c <= r), 1.0, 0.0).astype(F32)
        for i in range(per_step):
            rows = pl.ds(i * CHUNK, CHUNK)
            o_ref[0, rows, :] = jnp.dot(tri, a_ref[0, rows, :], preferred_element_type=F32,
                                        precision=lax.Precision.HIGHEST)

    spec = pl.BlockSpec((1, per_step * CHUNK, W), lambda b, c: (b, c, 0))
    return pl.pallas_call(body, grid=(B, S // (per_step * CHUNK)), in_specs=[spec], out_specs=spec,
                          out_shape=jax.ShapeDtypeStruct(a.shape, F32), name=name,
                          compiler_params=_cparams(("parallel", "parallel")))(a)


@jax.custom_vjp
def chunk_cumsum(a):
    return _chunk_cumsum_call(a, False, "chunk_cumsum_fwd")


chunk_cumsum.defvjp(lambda a: (_chunk_cumsum_call(a, False, "chunk_cumsum_fwd"), None),
                    lambda _, g: (_chunk_cumsum_call(g, True, "chunk_cumsum_bwd"),))


GROUP_W = 4 * HEAD_P
HPG = SSM_HEADS // SSM_GROUPS


def _ssd_masks():
    lane = lax.broadcasted_iota(jnp.int32, (1, GROUP_W), 1)
    return [((lane >= HEAD_P * j) & (lane < HEAD_P * (j + 1))).astype(F32) for j in range(HPG)]


def _ssd_decays(ac_cols, acr_ref, gi):
    r = lax.broadcasted_iota(jnp.int32, (CHUNK, CHUNK), 0)
    c = lax.broadcasted_iota(jnp.int32, (CHUNK, CHUNK), 1)
    return [jnp.exp(jnp.where(c <= r, ac_cols[j] - acr_ref[0, gi * HPG + j], NEG)) for j in range(HPG)]


def _ssd_cols(blk, g):
    lane = lax.broadcasted_iota(jnp.int32, blk.shape, 1)
    return [jnp.sum(jnp.where(lane == HPG * g + j, blk, 0.0), axis=1, keepdims=True) for j in range(HPG)]


def _ssd_spread(cols):
    lane = lax.broadcasted_iota(jnp.int32, (1, GROUP_W), 1)
    out = jnp.broadcast_to(cols[HPG - 1], (CHUNK, GROUP_W))
    for j in range(HPG - 2, -1, -1):
        out = jnp.where(lane < HEAD_P * (j + 1), cols[j], out)
    return out


def _ssd_gather(val, cols, masks, g):
    lane = lax.broadcasted_iota(jnp.int32, (1, LANE), 1)
    out = jnp.zeros((CHUNK, LANE), F32)
    for j in range(HPG):
        tot = jnp.sum(val * masks[j], axis=1, keepdims=True)
        if cols is not None:
            tot = tot + cols[j]
        out = out + tot * (lane == HPG * g + j).astype(F32)
    return out


def _dot(a, b, dims):
    return lax.dot_general(a.astype(BF16), b.astype(BF16), (dims, ((), ())), preferred_element_type=F32)


NN = ((1,), (0,))
NT = ((1,), (1,))
TN = ((0,), (0,))


XBC_W = GROUP_W + 2 * STATE_N


SSD_STEP_GROUPS_FWD = 8
SSD_STEP_GROUPS_BWD = 2


def _ssd_load(xbc_ref, dt_ref, ac_ref, masks, g, gi):
    x = xbc_ref[0, :, gi * XBC_W:gi * XBC_W + GROUP_W]
    bm = xbc_ref[0, :, gi * XBC_W + GROUP_W:gi * XBC_W + GROUP_W + STATE_N]
    cm = xbc_ref[0, :, gi * XBC_W + GROUP_W + STATE_N:(gi + 1) * XBC_W]
    ac_cols = _ssd_cols(ac_ref[0], g)
    dt = _ssd_spread(_ssd_cols(dt_ref[0], g))
    ac = _ssd_spread(ac_cols)
    is_last = (lax.broadcasted_iota(jnp.int32, (CHUNK, GROUP_W), 0) == CHUNK - 1).astype(F32)
    return x, bm, cm, dt, ac, ac_cols, is_last


def _ssd_in_specs(nc, rev, gb):
    cc = (lambda c: nc - 1 - c) if rev else (lambda c: c)
    return [pl.BlockSpec((1, CHUNK, gb * XBC_W), lambda b, g, c: (b, cc(c), g)),
            pl.BlockSpec((1, CHUNK, LANE), lambda b, g, c: (b, cc(c), 0)),
            pl.BlockSpec((1, CHUNK, LANE), lambda b, g, c: (b, cc(c), 0)),
            pl.BlockSpec((1, gb * HPG, 1, CHUNK), lambda b, g, c: (b, g, 0, cc(c))),
            pl.BlockSpec((1, gb * GROUP_W), lambda b, g, c: (0, g))]


def _ssd_fwd_call(xbc, dtp, acp, acr, dsk):
    B, S, _ = xbc.shape
    nc = S // CHUNK
    gb = SSD_STEP_GROUPS_FWD

    def body(xbc_ref, dt_ref, ac_ref, ar_ref, ds_ref, y_ref, hp_ref, h_sc):
        @pl.when(pl.program_id(2) == 0)
        def _():
            h_sc[...] = jnp.zeros(h_sc.shape, F32)

        masks = _ssd_masks()
        ys = []
        for gi in range(gb):
            grp = gb * pl.program_id(1) + gi
            x, bm, cm, dt, ac, ac_cols, is_last = _ssd_load(xbc_ref, dt_ref, ac_ref, masks, grp, gi)
            last = jnp.sum(ac * is_last, axis=0, keepdims=True)
            decays = _ssd_decays(ac_cols, ar_ref, gi)
            xd = x * dt
            cb = _dot(cm, bm, NT)
            hprev = h_sc[gi]
            hp_ref[0, gi, 0] = hprev
            y = _dot(cm, hprev, NN) * jnp.exp(ac) + ds_ref[:, gi * GROUP_W:(gi + 1) * GROUP_W] * x
            y = y + _dot(jnp.concatenate([cb * d for d in decays], axis=1),
                         jnp.concatenate([xd * m for m in masks], axis=0), NN)
            ys.append(y)
            h_sc[gi] = hprev * jnp.exp(last) + _dot(bm, xd * jnp.exp(last - ac), TN)
        y_ref[0] = jnp.concatenate(ys, axis=1)

    ng = SSM_GROUPS // gb
    return pl.pallas_call(
        body, grid=(B, ng, nc), in_specs=_ssd_in_specs(nc, False, gb),
        out_specs=[pl.BlockSpec((1, CHUNK, gb * GROUP_W), lambda b, g, c: (b, c, g)),
                   pl.BlockSpec((1, gb, 1, STATE_N, GROUP_W), lambda b, g, c: (b, g, c, 0, 0))],
        out_shape=[jax.ShapeDtypeStruct((B, S, D_INNER), F32),
                   jax.ShapeDtypeStruct((B, SSM_GROUPS, nc, STATE_N, GROUP_W), F32)],
        scratch_shapes=[pltpu.VMEM((gb, STATE_N, GROUP_W), F32)], name="ssd_fwd",
        compiler_params=_cparams(("parallel", "parallel", "arbitrary")))(xbc, dtp, acp, acr, dsk)


def _ssd_bwd_call(xbc, dtp, acp, acr, dsk, hps, dy):
    B, S, _ = xbc.shape
    nc = S // CHUNK
    gb = SSD_STEP_GROUPS_BWD

    def body(xbc_ref, dt_ref, ac_ref, ar_ref, ds_ref, hp_ref, dy_ref,
             dxbc_ref, ddt_ref, dac_ref, dar_ref, dds_ref, dh_sc):
        first = pl.program_id(2) == 0

        @pl.when(first)
        def _():
            dh_sc[...] = jnp.zeros(dh_sc.shape, F32)

        masks = _ssd_masks()
        dxbc_parts, dds_parts = [], []
        for gi in range(gb):
            grp = gb * pl.program_id(0) + gi
            x, bm, cm, dt, ac, ac_cols, is_last = _ssd_load(xbc_ref, dt_ref, ac_ref, masks, grp, gi)
            last = jnp.sum(ac * is_last, axis=0, keepdims=True)
            g = dy_ref[0, :, gi * GROUP_W:(gi + 1) * GROUP_W]
            hprev = hp_ref[0, gi, 0]
            dh = dh_sc[gi]
            decays = _ssd_decays(ac_cols, ar_ref, gi)
            dcols = []
            xd = x * dt
            cb = _dot(cm, bm, NT)
            e_c = jnp.exp(ac)
            e_end = jnp.exp(last - ac)
            e_last = jnp.exp(last)
            z = _dot(cm, hprev, NN)
            dz = g * e_c
            dac = g * z * e_c
            dc = _dot(dz, hprev, NT)
            dhprev = _dot(cm, dz, TN) + dh * e_last
            dcb = jnp.zeros((CHUNK, CHUNK), F32)
            gjs = [cb * d for d in decays]
            g_heads = jnp.concatenate([g * m for m in masks], axis=0)
            dg_heads = _dot(g_heads, xd, NT)
            dxd = _dot(jnp.concatenate(gjs, axis=0), g_heads, TN)
            for j in range(HPG):
                gj = gjs[j]
                dgj = dg_heads[j * CHUNK:(j + 1) * CHUNK]
                dcb = dcb + dgj * decays[j]
                dseg = dgj * gj
                dcols.append(jnp.sum(dseg, axis=1, keepdims=True))
                dar_ref[0, gi * HPG + j] = -jnp.sum(dseg, axis=0, keepdims=True)
            dc = dc + _dot(dcb, bm, NN)
            db = _dot(dcb, cm, TN)
            sx = xd * e_end
            db = db + _dot(sx, dh, NT)
            dsx = _dot(bm, dh, NN)
            dxd = dxd + dsx * e_end
            de = dsx * sx
            dac = dac - de
            dlast = jnp.sum(de, axis=0, keepdims=True) + jnp.sum(dh * hprev, axis=0, keepdims=True) * e_last
            dsk = ds_ref[:, gi * GROUP_W:(gi + 1) * GROUP_W]
            dxbc_parts += [dxd * dt + dsk * g, db, dc]
            ddt_ref[0, gi] = _ssd_gather(dxd * x, None, masks, grp)
            dac_ref[0, gi] = _ssd_gather(dac + is_last * dlast, dcols, masks, grp)
            dds_parts.append(jnp.sum(g * x, axis=0, keepdims=True))
            dh_sc[gi] = dhprev
        dxbc_ref[0] = jnp.concatenate(dxbc_parts, axis=1)
        dds = jnp.concatenate(dds_parts, axis=1)
        first_all = first & (pl.program_id(1) == 0)

        @pl.when(first_all)
        def _():
            dds_ref[...] = dds

        @pl.when(jnp.logical_not(first_all))
        def _():
            dds_ref[...] += dds

    rc = lambda c: nc - 1 - c
    ng = SSM_GROUPS // gb
    in_specs = [pl.BlockSpec(s.block_shape, (lambda g, b, c, f=s.index_map: f(b, g, c))) for s in _ssd_in_specs(nc, True, gb)]
    in_specs.append(pl.BlockSpec((1, gb, 1, STATE_N, GROUP_W), lambda g, b, c: (b, g, rc(c), 0, 0)))
    in_specs.append(pl.BlockSpec((1, CHUNK, gb * GROUP_W), lambda g, b, c: (b, rc(c), g)))
    per_group = pl.BlockSpec((1, gb, CHUNK, LANE), lambda g, b, c: (b, g, rc(c), 0))
    out_specs = [pl.BlockSpec((1, CHUNK, gb * XBC_W), lambda g, b, c: (b, rc(c), g)), per_group, per_group,
                 pl.BlockSpec((1, gb * HPG, 1, CHUNK), lambda g, b, c: (b, g, 0, rc(c))),
                 pl.BlockSpec((1, gb * GROUP_W), lambda g, b, c: (0, g))]
    out_shape = [jax.ShapeDtypeStruct(xbc.shape, F32),
                 jax.ShapeDtypeStruct((B, SSM_GROUPS, S, LANE), F32), jax.ShapeDtypeStruct((B, SSM_GROUPS, S, LANE), F32),
                 jax.ShapeDtypeStruct(acr.shape, F32), jax.ShapeDtypeStruct(dsk.shape, F32)]
    return pl.pallas_call(
        body, grid=(ng, B, nc), in_specs=in_specs, out_specs=out_specs, out_shape=out_shape,
        scratch_shapes=[pltpu.VMEM((gb, STATE_N, GROUP_W), F32)], name="ssd_bwd",
        compiler_params=_cparams(("arbitrary", "arbitrary", "arbitrary")))(xbc, dtp, acp, acr, dsk, hps, dy)


@jax.custom_vjp
def ssd(xbc, dtp, acp, acr, dsk):
    return _ssd_fwd_call(xbc, dtp, acp, acr, dsk)[0]


def _ssd_fwd(xbc, dtp, acp, acr, dsk):
    y, hps = _ssd_fwd_call(xbc, dtp, acp, acr, dsk)
    return y, (xbc, dtp, acp, acr, dsk, hps)


def _ssd_bwd(res, dy):
    dxbc, ddt, dac, dacr, dds = _ssd_bwd_call(*res, dy)
    return dxbc, jnp.sum(ddt, axis=1), jnp.sum(dac, axis=1), dacr, dds


ssd.defvjp(_ssd_fwd, _ssd_bwd)


def _pack_small(arrs):
    flat = jnp.concatenate([a.reshape(-1) for a in arrs])
    rows = -(-flat.shape[0] // (8 * LANE)) * 8
    return jnp.pad(flat, (0, rows * LANE - flat.shape[0])).reshape(rows, LANE)


def _unpack_small(buf, shapes):
    flat = buf.reshape(-1)
    out, off = [], 0
    for shp in shapes:
        n = int(np.prod(shp))
        out.append(flat[off:off + n].reshape(shp))
        off += n
    return out


def _rows_tile(rows, cap):
    for cand in range(min(rows, cap), 7, -8):
        if rows % cand == 0:
            return cand
    return rows


def _pair_sum(mine, theirs, cidx, name):
    n4, kk, nn = mine.shape
    half = kk // 2
    tr = _rows_tile(half, 256)
    nb = half // tr

    def body(c_ref, a_ref, b_ref, o_ref, ob_ref):
        tot = a_ref[...] + b_ref[...]
        o_ref[...] = tot
        ob_ref[...] = tot.astype(BF16)

    spec = pl.BlockSpec((1, tr, nn), lambda j, i, c: (j, i, 0))
    grid_spec = pltpu.PrefetchScalarGridSpec(
        num_scalar_prefetch=1, grid=(n4, nb),
        in_specs=[pl.BlockSpec((1, tr, nn), lambda j, i, c: (j, c[0] * nb + i, 0)), spec], out_specs=[spec, spec])
    return pl.pallas_call(
        body, grid_spec=grid_spec,
        out_shape=[jax.ShapeDtypeStruct((n4, half, nn), F32), jax.ShapeDtypeStruct((n4, half, nn), BF16)],
        name=name, compiler_params=_cparams(("parallel", "parallel")))(cidx, mine, theirs)


def _chip_sum(quad, pair, chip_idx, name):
    _, rows, nn = quad.shape
    tr = _rows_tile(rows, 256)

    def body(s_ref, q_ref, p_ref, o_ref):
        for mine in range(4):
            @pl.when(s_ref[0] == mine)
            def _(mine=mine):
                acc = None
                for d in range(4):
                    term = p_ref[0] if d == mine else q_ref[d].astype(F32)
                    acc = term if acc is None else acc + term
                o_ref[...] = acc

    grid_spec = pltpu.PrefetchScalarGridSpec(
        num_scalar_prefetch=1, grid=(rows // tr,),
        in_specs=[pl.BlockSpec((4, tr, nn), lambda i, s: (0, i, 0)), pl.BlockSpec((1, tr, nn), lambda i, s: (s[0], i, 0))],
        out_specs=pl.BlockSpec((tr, nn), lambda i, s: (i, 0)))
    return pl.pallas_call(body, grid_spec=grid_spec, out_shape=jax.ShapeDtypeStruct((rows, nn), F32), name=name,
                          compiler_params=_cparams(("parallel",)))(chip_idx, quad, pair)


def _adam_halves_call(w, mine, other, cidx, m, v, name):
    _, rows, nn = w.shape
    half = rows // 2
    tr = _rows_tile(half, 128)
    nb = half // tr

    def body(c_ref, w_ref, a_ref, b_ref, m_ref, v_ref, g_ref, d_ref, nm_ref, nv_ref):
        upper = (pl.program_id(0) >= nb).astype(jnp.int32)
        g = jnp.where(upper == c_ref[0], a_ref[...], b_ref[...])
        g_ref[0] = g
        d_ref[0], nm_ref[0], nv_ref[0] = _adam_fn(w_ref[0], g, m_ref[0], v_ref[0])

    spec = pl.BlockSpec((1, tr, nn), lambda i, c: (0, i, 0))
    hspec = pl.BlockSpec((tr, nn), lambda i, c: (i % nb, 0))
    grid_spec = pltpu.PrefetchScalarGridSpec(num_scalar_prefetch=1, grid=(2 * nb,),
                                             in_specs=[spec, hspec, hspec, spec, spec], out_specs=[spec] * 4)
    return pl.pallas_call(body, grid_spec=grid_spec, out_shape=[jax.ShapeDtypeStruct(w.shape, F32)] * 4, name=name,
                          compiler_params=_cparams(("parallel",)))(cidx, w, mine, other, m, v)


def _stack_sum(stack, name):
    n, rows, nn = stack.shape
    tr = _rows_tile(rows, 256)

    def body(s_ref, o_ref):
        acc = s_ref[0]
        for d in range(1, n):
            acc = acc + s_ref[d]
        o_ref[...] = acc

    return pl.pallas_call(
        body, grid=(rows // tr,), in_specs=[pl.BlockSpec((n, tr, nn), lambda i: (0, i, 0))],
        out_specs=pl.BlockSpec((tr, nn), lambda i: (i, 0)), out_shape=jax.ShapeDtypeStruct((rows, nn), F32),
        name=name, compiler_params=_cparams(("parallel",)))(stack)


def _adam_call(w, g, m, v, name):
    rows, nn = w.shape
    tr = _rows_tile(rows, 128)

    def body(w_ref, g_ref, m_ref, v_ref, d_ref, nm_ref, nv_ref):
        d_ref[...], nm_ref[...], nv_ref[...] = _adam_fn(w_ref[...], g_ref[...], m_ref[...], v_ref[...])

    spec = pl.BlockSpec((tr, nn), lambda i: (i, 0))
    sds = jax.ShapeDtypeStruct((rows, nn), F32)
    return pl.pallas_call(body, grid=(rows // tr,), in_specs=[spec] * 4, out_specs=[spec] * 3,
                          out_shape=[sds] * 3, name=name, compiler_params=_cparams(("parallel",)))(w, g, m, v)


def _adam_fn(w, g, m, v):
    m = ADAM_B1 * m + (1.0 - ADAM_B1) * g
    v = ADAM_B2 * v + (1.0 - ADAM_B2) * (g * g)
    m_hat = m / (1.0 - ADAM_B1 ** ADAM_STEP)
    v_hat = v / (1.0 - ADAM_B2 ** ADAM_STEP)
    delta = -ADAM_LR * (m_hat / (jnp.sqrt(v_hat) + ADAM_EPS) + ADAM_WD * w)
    return delta, m, v


def _mesh_pos():
    return lax.axis_index("x"), lax.axis_index("y"), lax.axis_index("c")


def _other_chips(x, y):
    return [(1 - x, y), (x, 1 - y), (1 - x, 1 - y)]


HBM_SPEC = pl.BlockSpec(memory_space=pl.ANY)


def _remote(src, dst, send_sems, recv_sems, k, to):
    return pltpu.make_async_remote_copy(src_ref=src, dst_ref=dst, send_sem=send_sems.at[k], recv_sem=recv_sems.at[k],
                                        device_id=to, device_id_type=MESH)


def _half_rows(c, rows, align):
    half = rows // 2
    return (pl.ds(pl.multiple_of(c * half, align), half), pl.ds(pl.multiple_of((1 - c) * half, align), half))


def _gather_weights(mats, conv):
    n = len(mats)

    def body(*refs):
        ins, conv_in = refs[:n], refs[n]
        outs, conv_out = refs[n + 1:2 * n + 1], refs[2 * n + 1]
        send_sems, recv_sems, local_sem = refs[2 * n + 2:]
        x, y, c = _mesh_pos()
        me, sibling, s = (x, y, c), (x, y, 1 - c), 2 * x + y
        chips = _other_chips(x, y)
        rows = [_half_rows(c, m.shape[0], 16) for m in mats]
        own = pltpu.make_async_copy(conv_in, conv_out.at[s], local_sem)
        own.start()
        sent = []
        for i in range(n):
            mine = rows[i][0]
            for j, (cx, cy) in enumerate(chips):
                sent.append(_remote(ins[i].at[mine], outs[i].at[s, mine], send_sems, recv_sems, 6 * i + j, (cx, cy, c)))
        for j, (cx, cy) in enumerate(chips):
            sent.append(_remote(conv_in, conv_out.at[s], send_sems, recv_sems, 6 * n + j, (cx, cy, c)))
        for cp in sent:
            cp.start()
        for i in range(n):
            mine = rows[i][0]
            for j, (cx, cy) in enumerate(chips):
                landed = outs[i].at[2 * cx + cy, mine]
                _remote(landed, landed, send_sems, recv_sems, 6 * i + j, me).wait_recv()
                fwd = _remote(landed, landed, send_sems, recv_sems, 6 * i + 3 + j, sibling)
                fwd.start()
                sent.append(fwd)
        for j, (cx, cy) in enumerate(chips):
            slot = conv_out.at[2 * cx + cy]
            _remote(slot, slot, send_sems, recv_sems, 6 * n + j, me).wait_recv()
        for i in range(n):
            theirs_rows = rows[i][1]
            for j, (cx, cy) in enumerate(chips):
                theirs = outs[i].at[2 * cx + cy, theirs_rows]
                _remote(theirs, theirs, send_sems, recv_sems, 6 * i + 3 + j, me).wait_recv()
        for cp in sent:
            cp.wait_send()
        own.wait()

    out_shape = [jax.ShapeDtypeStruct((4,) + m.shape, m.dtype) for m in mats]
    out_shape.append(jax.ShapeDtypeStruct((4,) + conv.shape, conv.dtype))
    res = pl.pallas_call(
        body, in_specs=[HBM_SPEC] * (n + 1), out_specs=[HBM_SPEC] * (n + 1), out_shape=out_shape,
        scratch_shapes=[pltpu.SemaphoreType.DMA((6 * n + 3,)), pltpu.SemaphoreType.DMA((6 * n + 3,)),
                        pltpu.SemaphoreType.DMA],
        name="all_gather_weights")(*mats, conv)
    return res[:n], res[n]


def _sibling_exchange(stacks):
    n = len(stacks)

    def body(*refs):
        ins, outs = refs[:n], refs[n:2 * n]
        send_sems, recv_sems = refs[2 * n:]
        x, y, c = _mesh_pos()
        cps = []
        for i in range(n):
            theirs = _half_rows(c, stacks[i].shape[1], 8)[1]
            cps.append(_remote(ins[i].at[:, theirs, :], outs[i], send_sems, recv_sems, i, (x, y, 1 - c)))
        for cp in cps:
            cp.start()
        for cp in cps:
            cp.wait()

    out_shape = [jax.ShapeDtypeStruct((4, s.shape[1] // 2, s.shape[2]), s.dtype) for s in stacks]
    return pl.pallas_call(
        body, in_specs=[HBM_SPEC] * n, out_specs=[HBM_SPEC] * n, out_shape=out_shape,
        scratch_shapes=[pltpu.SemaphoreType.DMA((n,)), pltpu.SemaphoreType.DMA((n,))],
        name="grad_sibling_exchange")(*stacks)


def _chip_exchange(parts):
    n = len(parts)

    def body(*refs):
        ins, outs = refs[:n], refs[n:2 * n]
        send_sems, recv_sems = refs[2 * n:]
        x, y, c = _mesh_pos()
        me, s = (x, y, c), 2 * x + y
        chips = _other_chips(x, y)
        sent = [_remote(ins[i].at[2 * cx + cy], outs[i].at[s], send_sems, recv_sems, 3 * i + j, (cx, cy, c))
                for i in range(n) for j, (cx, cy) in enumerate(chips)]
        for cp in sent:
            cp.start()
        for i in range(n):
            for j, (cx, cy) in enumerate(chips):
                slot = outs[i].at[2 * cx + cy]
                _remote(slot, slot, send_sems, recv_sems, 3 * i + j, me).wait_recv()
        for cp in sent:
            cp.wait_send()

    return pl.pallas_call(
        body, in_specs=[HBM_SPEC] * n, out_specs=[HBM_SPEC] * n,
        out_shape=[jax.ShapeDtypeStruct(p.shape, p.dtype) for p in parts],
        scratch_shapes=[pltpu.SemaphoreType.DMA((3 * n,)), pltpu.SemaphoreType.DMA((3 * n,))],
        name="grad_chip_exchange")(*parts)


def _sibling_swap(halves):
    n = len(halves)

    def body(*refs):
        ins, outs = refs[:n], refs[n:2 * n]
        send_sems, recv_sems = refs[2 * n:]
        x, y, c = _mesh_pos()
        cps = [_remote(ins[i], outs[i], send_sems, recv_sems, i, (x, y, 1 - c)) for i in range(n)]
        for cp in cps:
            cp.start()
        for cp in cps:
            cp.wait()

    return pl.pallas_call(
        body, in_specs=[HBM_SPEC] * n, out_specs=[HBM_SPEC] * n,
        out_shape=[jax.ShapeDtypeStruct(h.shape, h.dtype) for h in halves],
        scratch_shapes=[pltpu.SemaphoreType.DMA((n,)), pltpu.SemaphoreType.DMA((n,))],
        name="grad_sibling_swap")(*halves)


def _gather_small(vec):
    def body(in_ref, out_ref, send_sems, recv_sems, local_sem):
        x, y, c = _mesh_pos()
        me = (x, y, c)
        own = pltpu.make_async_copy(in_ref, out_ref.at[4 * x + 2 * y + c], local_sem)
        own.start()
        peers = [(1 - x if k & 4 else x, 1 - y if k & 2 else y, 1 - c if k & 1 else c) for k in range(1, 8)]
        sent = [_remote(in_ref, out_ref.at[4 * x + 2 * y + c], send_sems, recv_sems, k, p) for k, p in enumerate(peers)]
        for cp in sent:
            cp.start()
        for k, (px, py, pc) in enumerate(peers):
            slot = out_ref.at[4 * px + 2 * py + pc]
            _remote(slot, slot, send_sems, recv_sems, k, me).wait_recv()
        for cp in sent:
            cp.wait_send()
        own.wait()

    return pl.pallas_call(
        body, in_specs=[HBM_SPEC], out_specs=HBM_SPEC, out_shape=jax.ShapeDtypeStruct((8,) + vec.shape, vec.dtype),
        scratch_shapes=[pltpu.SemaphoreType.DMA((7,)), pltpu.SemaphoreType.DMA((7,)), pltpu.SemaphoreType.DMA],
        name="grad_gather_small")(vec)


def _reduce_matrices(stacks, names):
    cidx = lax.axis_index("c").astype(jnp.int32).reshape(1)
    chip = (2 * lax.axis_index("x") + lax.axis_index("y")).astype(jnp.int32).reshape(1)
    got = _sibling_exchange(stacks)
    pairs = [_pair_sum(a, b, cidx, "grad_pair_sum_" + nm) for a, b, nm in zip(stacks, got, names)]
    quads = _chip_exchange([p[1] for p in pairs])
    mine = [_chip_sum(q, p[0], chip, "grad_chip_sum_" + nm) for q, p, nm in zip(quads, pairs, names)]
    return mine, _sibling_swap(mine)


def _pad_cols(a, n):
    return jnp.concatenate([a, jnp.zeros((a.shape[0], n - a.shape[1]), a.dtype)], axis=1)


def _group_channels(a):
    lead = a.shape[:-1]
    xs = a[..., :D_INNER].reshape(lead + (SSM_GROUPS, GROUP_W))
    bs = a[..., D_INNER:D_INNER + SSM_GROUPS * STATE_N].reshape(lead + (SSM_GROUPS, STATE_N))
    cs = a[..., D_INNER + SSM_GROUPS * STATE_N:].reshape(lead + (SSM_GROUPS, STATE_N))
    return jnp.concatenate([xs, bs, cs], axis=-1).reshape(lead + (CONV_CH,))


PROJ_SEGS = (('gate_a', D_MODEL), ('gate_b', D_MODEL), ('z', D_INNER), ('xbc', CONV_CH), ('q_lat', Q_RANK),
             ('kv_lat', KV_RANK), ('k_rope', LANE), ('dt', LANE))
PROJ_WIDE = sum(w for _, w in PROJ_SEGS[:4])
PROJ_LANE0 = {n: (v if v < PROJ_WIDE else v - PROJ_WIDE) for n, v in
              zip([n for n, _ in PROJ_SEGS], [int(v) for v in np.cumsum([0] + [w for _, w in PROJ_SEGS])[:-1]])}
CONV_LANE0 = PROJ_LANE0['xbc']
KR_LANE0 = PROJ_LANE0['k_rope']


def _lay_w_in(w):
    idx = np.cumsum(IN_SIZES)[:-1]
    q_lat, kv_lat, k_rope, z, xbc, dt, gate_a, gate_b = jnp.split(w, [int(v) for v in idx], axis=1)
    return jnp.concatenate([gate_a, gate_b, z, _group_channels(xbc), q_lat, kv_lat, _pad_cols(k_rope, LANE),
                            _pad_cols(dt, LANE)], axis=1)


@jax.custom_vjp
def project(h, w, tok):
    return _project_impl(h, w)


def _project_impl(h, w):
    return (_mm(h, w[:, :PROJ_WIDE], "w_in_fwd", BF16), _mm(h, w[:, PROJ_WIDE:], "w_in_narrow_fwd")) + tuple(
        jnp.zeros((h.shape[0], wd), BF16) for _, wd in PROJ_SEGS)


def _project_fwd(h, w, tok):
    return _project_impl(h, w), (h, w)


def _project_bwd(res, cots):
    h, w = res
    g = jnp.concatenate(cots[2:], axis=1)
    return _mm(g, w.T, "w_in_dx", h.dtype), jnp.zeros_like(w), _mm(h.T, g, "w_in_dw")


project.defvjp(_project_fwd, _project_bwd)


def _lay_w_uq(w):
    w3 = w.reshape(Q_RANK, N_HEADS, NOPE + ROPE)
    w3 = jnp.concatenate([w3, jnp.zeros((Q_RANK, N_HEADS, QK_PAD - NOPE - ROPE), w.dtype)], axis=2)
    return w3.reshape(Q_RANK, N_HEADS * QK_PAD)


def _lay_w_ukv(w):
    w3 = w.reshape(KV_RANK, N_HEADS, NOPE + V_DIM)
    return jnp.concatenate([w3[:, :, :NOPE].reshape(KV_RANK, -1), w3[:, :, NOPE:].reshape(KV_RANK, -1)], axis=1)


def _pad_lanes(v, n=LANE):
    return jnp.concatenate([v, jnp.zeros((v.shape[0], n - v.shape[1]), v.dtype)], axis=1)


def _local_loss(toks, small, x, wb, c8, posf, target):
    B, S, D = x.shape
    T = B * S

    def lin(name, a, key, lay=lambda w: w, out_dtype=F32):
        return make_linear(name, out_dtype)(a, lay(wb[key]), lay(toks[key]))

    rows2 = lambda a: a.reshape(T, a.shape[-1])
    rows3 = lambda a: a.reshape(B, S, a.shape[-1])

    sc = make_rowwise("silu_c", _f_silu, 1, 0, 0, ('row',))((c8[None],), (), ())[0][0]
    mod = make_linear("ada", F32, 4)(sc, wb['w_ada'], toks['w_ada'])[:B] + small['b_ada']
    shift1, scale1, gate1, shift2, scale2, gate2 = [m[:, None, :] for m in jnp.split(mod, 6, axis=-1)]

    h, x_res = make_rowwise("modulate1", _f_modulate, 1, 2, 1, ('row',), forward_row=0, ts_cap=1024)(
        (x,), (scale1, shift1), (small['g_pre_mix'],))
    outs = project(rows2(h), _lay_w_in(wb['w_in']), _lay_w_in(toks['w_in']))
    wide = lax.stop_gradient(rows3(outs[0]))
    proj = lax.stop_gradient(rows3(outs[1]))
    stand = {n: rows3(o) for (n, _), o in zip(PROJ_SEGS, outs[2:])}

    def win(seg, block):
        return (PROJ_LANE0[seg] // block, dict(PROJ_SEGS)[seg])

    inv = ROPE_THETA ** (-jnp.arange(ROPE // 2, dtype=F32) / (ROPE // 2))
    inv_lane = jnp.concatenate([inv, inv, jnp.zeros((LANE - ROPE,), F32)])[None]
    tabs = tuple(_rope_tables(posf, inv_lane))
    qn = make_rowwise("rms_q", _f_rms, 1, 0, 1, ('row',), windows={0: win('q_lat', Q_RANK)})(
        (proj,), (), (small['g_q_lat'],), (stand['q_lat'],))[0]
    kvn = make_rowwise("rms_kv", _f_rms, 1, 0, 1, ('row',), windows={0: win('kv_lat', KV_RANK)})(
        (proj,), (), (small['g_kv_lat'],), (stand['kv_lat'],))[0]
    qp = rows3(lin("w_uq", rows2(qn), 'w_uq', _lay_w_uq, BF16))
    kvp = rows3(lin("w_ukv", rows2(kvn), 'w_ukv', _lay_w_ukv, BF16))
    qr = rope_q(qp, tabs)
    att = attention(qr, kvp, proj, stand['k_rope'], tabs)
    attn = rows3(lin("w_o_attn", rows2(att), 'w_o_attn', out_dtype=BF16))

    xa = conv_silu(wide, stand['xbc'], _group_channels(wb['conv_w_f32']), _group_channels(small['conv_b']))
    dt_pad, a_pad = make_rowwise("dt_softplus", _f_dt, 1, 0, 2, ('row', 'row'), windows={0: win('dt', LANE)})(
        (proj,), (), (_pad_lanes(small['dt_bias']), _pad_lanes(small['a_log'])), (stand['dt'],))
    ac_pad = chunk_cumsum(a_pad)
    acr = jnp.transpose(ac_pad[..., :SSM_HEADS], (0, 2, 1))[:, :, None, :]
    dsk = jnp.repeat(small['d_skip'], HEAD_P, axis=-1)
    y = ssd(xa, dt_pad, ac_pad, acr, dsk)
    yg = make_rowwise("gated_norm", _f_gated_norm, 2, 0, 1, ('row',), ncol=SSM_GROUPS, ts_cap=2048,
                      windows={1: win('z', GROUP_W)})((y, wide), (), (small['g_ssm_out'],), (stand['z'],))[0]
    ssm = rows3(lin("w_o_ssm", rows2(yg), 'w_o_ssm', out_dtype=BF16))

    merged = make_rowwise("merge", _f_merge, 4, 0, 0, ('row',), ts_cap=1024,
                          windows={2: win('gate_a', D_MODEL), 3: win('gate_b', D_MODEL)})(
        (attn, ssm, wide, wide), (), (), (stand['gate_a'], stand['gate_b']))[0]
    mix = rows3(lin("w_out", rows2(merged), 'w_out', out_dtype=BF16))
    x1 = make_rowwise("post_mix", _f_post, 2, 1, 1, ('row',), ts_cap=1024)(
        (x_res, mix), (gate1,), (small['g_post_mix'],))[0]

    h2, x1_res = make_rowwise("modulate2", _f_modulate, 1, 2, 1, ('row',), forward_row=0, ts_cap=1024)(
        (x1,), (scale2, shift2), (small['g_pre_mlp'],))
    ff = rows3(ffn(rows2(h2), wb['w_ff1'], toks['w_ff1'], wb['w_ff2'], toks['w_ff2']))
    lvec = make_rowwise("final_loss", _f_final_loss, 3, 1, 1, ('sum',), nodiff=(2,), ts_cap=1024)(
        (x1_res, ff, target), (gate2,), (small['g_post_mlp'],))[0]
    return jnp.sum(lvec)


MATRICES = COL_SHARDED + ROW_SHARDED
STACKED_DW = ('w_ada', 'w_ff1')


def _local_step(x, c, positions, target, wb, small):
    B = x.shape[0]
    c8 = jnp.concatenate([c, jnp.zeros((16 - B, c.shape[1]), F32)], axis=0)
    posf = positions.astype(F32)[..., None]
    toks = {k: jnp.zeros(wb[k].shape, F32) for k in MATRICES if k != 'conv_w'}
    for k in STACKED_DW:
        rows, cols = wb[k].shape
        toks[k] = jnp.zeros((4, rows, cols // 4), F32)
    conv_w = wb['conv_w_f32']

    def loss_fn(toks, small, conv_w, x):
        wbl = dict(wb)
        wbl['conv_w_f32'] = conv_w
        return _local_loss(toks, small, x, wbl, c8, posf, target)

    loss, (g_tok, g_small, g_conv, g_x) = jax.value_and_grad(loss_fn, argnums=(0, 1, 2, 3))(toks, small, conv_w, x)
    grads = dict(g_tok)
    grads.update(g_small)
    grads['conv_w'] = g_conv
    return loss, g_x, grads


def kernel(x, c, positions, w_ada, b_ada, g_pre_mix, g_post_mix, w_in, g_q_lat, g_kv_lat, w_uq, w_ukv, w_o_attn, conv_w, conv_b, dt_bias, a_log, d_skip, g_ssm_out, w_o_ssm, w_out, g_pre_mlp, g_post_mlp, w_ff1, w_ff2, loss_target, m_w_ada, m_b_ada, m_g_pre_mix, m_g_post_mix, m_w_in, m_g_q_lat, m_g_kv_lat, m_w_uq, m_w_ukv, m_w_o_attn, m_conv_w, m_conv_b, m_dt_bias, m_a_log, m_d_skip, m_g_ssm_out, m_w_o_ssm, m_w_out, m_g_pre_mlp, m_g_post_mlp, m_w_ff1, m_w_ff2, v_w_ada, v_b_ada, v_g_pre_mix, v_g_post_mix, v_w_in, v_g_q_lat, v_g_kv_lat, v_w_uq, v_w_ukv, v_w_o_attn, v_conv_w, v_conv_b, v_dt_bias, v_a_log, v_d_skip, v_g_ssm_out, v_w_o_ssm, v_w_out, v_g_pre_mlp, v_g_post_mlp, v_w_ff1, v_w_ff2):
    given = dict(locals())
    w_loc = {n: given[n] for n in WEIGHTS}
    m_loc = {n: given["m_" + n] for n in WEIGHTS}
    v_loc = {n: given["v_" + n] for n in WEIGHTS}
    mats = [n for n in WEIGHTS if n in MATRICES and n != 'conv_w']
    vecs = [n for n in WEIGHTS if n not in MATRICES]

    own = [w_loc[n][0].astype(BF16) for n in mats]
    g_mats, g_conv = _gather_weights(own, conv_w[0])
    chip = 2 * lax.axis_index("x") + lax.axis_index("y")
    wb = {}
    for n, g, mine in zip(mats, g_mats, own):
        g = lax.dynamic_update_slice_in_dim(g, mine[None], chip, axis=0)
        if n in COL_SHARDED:
            wb[n] = jnp.transpose(g, (1, 0, 2)).reshape(g.shape[1], -1)
        else:
            wb[n] = g.reshape(-1, g.shape[2])
    wb['conv_w_f32'] = jnp.transpose(g_conv, (1, 0, 2)).reshape(CONV_K, -1)
    small = {n: w_loc[n] for n in vecs}

    loss_part, grad_x, grads = _local_step(x, c, positions, loss_target, wb, small)
    loss = lax.psum(loss_part, ("x", "y", "c"))

    stacks = []
    for n in mats:
        kk, nn = w_loc[n].shape[1:]
        if n in STACKED_DW:
            stacks.append(grads[n])
        elif n in COL_SHARDED:
            stacks.append(jnp.transpose(grads[n].reshape(kk, 4, nn), (1, 0, 2)))
        else:
            stacks.append(grads[n].reshape(4, kk, nn))
    g_mine, g_other = _reduce_matrices(stacks, mats)
    g_shard = {}

    vec_shapes = [tuple(grads[n].shape) for n in vecs] + [tuple(grads['conv_w'].shape)]
    total = _stack_sum(_gather_small(_pack_small([grads[n] for n in vecs] + [grads['conv_w']])), "grad_sum_small")
    g_vec = _unpack_small(total, vec_shapes)
    n_conv = conv_w.shape[2]
    chip = 2 * lax.axis_index("x") + lax.axis_index("y")
    g_shard['conv_w'] = lax.dynamic_slice_in_dim(g_vec[-1], chip * n_conv, n_conv, axis=1)
    for n, g in zip(vecs, g_vec):
        g_shard[n] = g

    delta, new_m, new_v = {}, {}, {}
    cidx = lax.axis_index("c").astype(jnp.int32).reshape(1)
    for n, mine, other in zip(mats, g_mine, g_other):
        g_shard[n], delta[n], new_m[n], new_v[n] = _adam_halves_call(
            w_loc[n], mine, other, cidx, m_loc[n], v_loc[n], "adamw_" + n)
    rest = vecs + ['conv_w']
    rest_shapes = [tuple(w_loc[n].shape) for n in rest]
    packed = [_pack_small([src[n] for n in rest]) for src in (w_loc, g_shard, m_loc, v_loc)]
    for dst, buf in zip((delta, new_m, new_v), _adam_call(*packed, "adamw_small")):
        dst.update(zip(rest, _unpack_small(buf, rest_shapes)))

    def out(d):
        return [d[n].reshape(w_loc[n].shape) for n in WEIGHTS]

    return (loss, grad_x, *out(g_shard), *out(delta), *out(new_m), *out(new_v))
```

```python
import functools
import math

import numpy as np
import jax
import jax.numpy as jnp
from jax import lax
from jax.experimental import pallas as pl
from jax.experimental.pallas import tpu as pltpu

F32 = jnp.float32
BF16 = jnp.bfloat16
MESH = pl.DeviceIdType.MESH

D_MODEL = 1024
N_HEADS = 8
NOPE = 128
ROPE = 64
V_DIM = 128
Q_RANK = 256
KV_RANK = 256
ROPE_THETA = 10000.0
D_INNER = 2048
SSM_HEADS = 32
SSM_GROUPS = 8
HEAD_P = 64
STATE_N = 128
CONV_K = 4
CHUNK = 128
CONV_CH = D_INNER + 2 * SSM_GROUPS * STATE_N
D_FF = 4096
EPS = 1e-6
IN_SIZES = (Q_RANK, KV_RANK, ROPE, D_INNER, CONV_CH, SSM_HEADS, D_MODEL, D_MODEL)
ADAM_LR, ADAM_B1, ADAM_B2, ADAM_EPS, ADAM_WD, ADAM_STEP = 0.001, 0.9, 0.999, 1e-08, 0.01, 10

VMEM_LIMIT_BYTES = 52 * 1024 * 1024
LANE = 128
QK_PAD = 256

WEIGHTS = ['w_ada', 'b_ada', 'g_pre_mix', 'g_post_mix', 'w_in', 'g_q_lat', 'g_kv_lat', 'w_uq', 'w_ukv',
           'w_o_attn', 'conv_w', 'conv_b', 'dt_bias', 'a_log', 'd_skip', 'g_ssm_out', 'w_o_ssm', 'w_out',
           'g_pre_mlp', 'g_post_mlp', 'w_ff1', 'w_ff2']
COL_SHARDED = ('w_ada', 'w_in', 'w_uq', 'w_ukv', 'conv_w', 'w_ff1')
ROW_SHARDED = ('w_o_attn', 'w_o_ssm', 'w_out', 'w_ff2')


def _cparams(sem):
    return pltpu.CompilerParams(dimension_semantics=sem, vmem_limit_bytes=VMEM_LIMIT_BYTES)


def _tile(n, cap):
    if n <= cap:
        return n
    k = n // LANE
    best = LANE
    for d in range(1, k + 1):
        if k % d == 0 and d * LANE <= cap:
            best = d * LANE
    return best


def _mm(a, w, name, out_dtype=F32, epilogue=None, extras=(), out_dtypes=None):
    M, K = a.shape
    N = w.shape[1]
    tm = min(M, 1024)
    tn = _tile(N, 1024)
    tk = _tile(K, 2048)
    nk = K // tk
    dts = tuple(out_dtypes) if epilogue is not None else (out_dtype,)
    n_x, n_o = len(extras), len(dts)

    def finish(acc, refs):
        res = epilogue(acc, *[r[...] for r in refs[:n_x]]) if epilogue is not None else (acc,)
        for o_ref, val, dt in zip(refs[n_x:n_x + n_o], res, dts):
            o_ref[...] = val.astype(dt)

    def body(a_ref, w_ref, *refs):
        part = jnp.dot(a_ref[...].astype(BF16), w_ref[...], preferred_element_type=F32)
        if nk == 1:
            finish(part, refs)
        else:
            acc_ref = refs[-1]
            k = pl.program_id(2)

            @pl.when(k == 0)
            def _():
                acc_ref[...] = part

            @pl.when(k > 0)
            def _():
                acc_ref[...] += part

            @pl.when(k == nk - 1)
            def _():
                finish(acc_ref[...], refs)

    ospec = pl.BlockSpec((tm, tn), lambda i, j, k: (i, j))
    res = pl.pallas_call(
        body, grid=(M // tm, N // tn, nk),
        in_specs=[pl.BlockSpec((tm, tk), lambda i, j, k: (i, k)), pl.BlockSpec((tk, tn), lambda i, j, k: (k, j))]
        + [ospec] * n_x,
        out_specs=[ospec] * n_o, out_shape=[jax.ShapeDtypeStruct((M, N), dt) for dt in dts],
        scratch_shapes=[pltpu.VMEM((tm, tn), F32)] if nk > 1 else [], name=name,
        compiler_params=_cparams(("parallel", "parallel", "arbitrary")))(a, w, *extras)
    return res if epilogue is not None else res[0]


def _mm_tn(a, g, name, col_shards=1):
    M, K = a.shape
    N = g.shape[1]
    tm = min(M, 1024)
    tk = _tile(K, 1024)
    tn = _tile(N // col_shards, 1024)
    nm = M // tm
    per = N // col_shards // tn

    def body(a_ref, g_ref, o_ref):
        part = lax.dot_general(a_ref[...].astype(BF16), g_ref[...].astype(BF16), (((0,), (0,)), ((), ())),
                               preferred_element_type=F32)
        m = pl.program_id(2)

        @pl.when(m == 0)
        def _():
            o_ref[...] = part.reshape(o_ref.shape)

        @pl.when(m > 0)
        def _():
            o_ref[...] += part.reshape(o_ref.shape)

    if col_shards == 1:
        out_spec = pl.BlockSpec((tk, tn), lambda i, j, m: (i, j))
        out_shape = jax.ShapeDtypeStruct((K, N), F32)
    else:
        out_spec = pl.BlockSpec((1, tk, tn), lambda i, j, m: (j // per, i, j % per))
        out_shape = jax.ShapeDtypeStruct((col_shards, K, N // col_shards), F32)
    return pl.pallas_call(
        body, grid=(K // tk, N // tn, nm),
        in_specs=[pl.BlockSpec((tm, tk), lambda i, j, m: (m, i)), pl.BlockSpec((tm, tn), lambda i, j, m: (m, j))],
        out_specs=out_spec, out_shape=out_shape, name=name,
        compiler_params=_cparams(("parallel", "parallel", "arbitrary")))(a, g)


def make_linear(name, out_dtype=F32, dw_col_shards=1):
    @jax.custom_vjp
    def linear(a, w, tok):
        return _mm(a, w, name + "_fwd", out_dtype)

    def fwd(a, w, tok):
        return _mm(a, w, name + "_fwd", out_dtype), (a, w)

    def bwd(res, g):
        a, w = res
        da = _mm(g, w.T, name + "_dx", a.dtype)
        dw = _mm_tn(a, g, name + "_dw", dw_col_shards)
        return da, jnp.zeros_like(w), dw

    linear.defvjp(fwd, bwd)
    return linear


def _relu2_epilogue(acc):
    r = jnp.maximum(acc, 0.0)
    return r * r, r


def _relu2_bwd_epilogue(acc, r):
    return (acc * (2.0 * r.astype(F32)),)


@jax.custom_vjp
def ffn(h, w1, tok1, w2, tok2):
    act, _ = _mm(h, w1, "w_ff1_fwd", epilogue=_relu2_epilogue, out_dtypes=(BF16, BF16))
    return _mm(act, w2, "w_ff2_fwd", BF16)


def _ffn_fwd(h, w1, tok1, w2, tok2):
    act, r = _mm(h, w1, "w_ff1_fwd", epilogue=_relu2_epilogue, out_dtypes=(BF16, BF16))
    return _mm(act, w2, "w_ff2_fwd", BF16), (h, w1, w2, act, r)


def _ffn_bwd(res, g):
    h, w1, w2, act, r = res
    du = _mm(g, w2.T, "w_ff2_dx", epilogue=_relu2_bwd_epilogue, extras=(r,), out_dtypes=(BF16,))[0]
    dw2 = _mm_tn(act, g, "w_ff2_dw")
    dw1 = _mm_tn(h, du, "w_ff1_dw", 4)
    dh = _mm(du, w1.T, "w_ff1_dx", h.dtype)
    return dh, jnp.zeros_like(w1), dw1, jnp.zeros_like(w2), dw2


ffn.defvjp(_ffn_fwd, _ffn_bwd)


def make_rowwise(name, f, n_rows, n_seqs, n_pars, out_kinds, ncol=1, nodiff=(), ts_cap=512, windows=None,
                 forward_row=None):
    windows = dict(windows or {})
    n_in = n_rows + n_seqs + n_pars
    diff_idx = [i for i in range(n_in) if i not in nodiff]

    def _dims(rows):
        B, S = rows[0].shape[0], rows[0].shape[1]
        ts = min(S, ts_cap)
        return B, S, ts

    def _width(i, r):
        return windows[i][1] if i in windows else r.shape[2]

    def _in_specs(rows, seqs, pars, ts):
        specs = []
        for i, r in enumerate(rows):
            col0 = windows[i][0] if i in windows else 0
            specs.append(pl.BlockSpec((1, ts, _width(i, r) // ncol), lambda k, b, s, col0=col0: (b, s, k + col0)))
        for q in seqs:
            specs.append(pl.BlockSpec((1, 1, q.shape[2] // ncol), lambda k, b, s: (b, 0, k)))
        for p in pars:
            specs.append(pl.BlockSpec((1, p.shape[1] // ncol), lambda k, b, s: (0, k)))
        return specs

    def _load(refs):
        vals = [r[0] for r in refs[:n_rows + n_seqs]]
        vals += [r[...] for r in refs[n_rows + n_seqs:n_in]]
        return vals

    def _out_struct(rows, seqs, pars, ts):
        blocks = [jax.ShapeDtypeStruct((ts, _width(i, r) // ncol), r.dtype) for i, r in enumerate(rows)]
        blocks += [jax.ShapeDtypeStruct((1, q.shape[2] // ncol), q.dtype) for q in seqs]
        blocks += [jax.ShapeDtypeStruct((1, p.shape[1] // ncol), p.dtype) for p in pars]
        return jax.eval_shape(f, *blocks)

    def _fwd_call(rows, seqs, pars):
        B, S, ts = _dims(rows)
        outs = _out_struct(rows, seqs, pars, ts)
        n_out = len(outs)

        def body(*refs):
            res = f(*_load(refs))
            first = (pl.program_id(1) == 0) & (pl.program_id(2) == 0)
            for o_ref, val, kind in zip(refs[n_in:], res, out_kinds):
                if kind == 'row':
                    o_ref[0] = val
                else:
                    tot = jnp.sum(val, axis=0, keepdims=True)

                    @pl.when(first)
                    def _(o_ref=o_ref, tot=tot):
                        o_ref[...] = tot

                    @pl.when(jnp.logical_not(first))
                    def _(o_ref=o_ref, tot=tot):
                        o_ref[...] += tot

        out_shape, out_specs = [], []
        for o, kind in zip(outs, out_kinds):
            d = o.shape[1]
            if kind == 'row':
                out_shape.append(jax.ShapeDtypeStruct((B, S, ncol * d), o.dtype))
                out_specs.append(pl.BlockSpec((1, ts, d), lambda k, b, s: (b, s, k)))
            else:
                out_shape.append(jax.ShapeDtypeStruct((1, ncol * d), o.dtype))
                out_specs.append(pl.BlockSpec((1, d), lambda k, b, s: (0, k)))
        res = pl.pallas_call(
            body, grid=(ncol, B, S // ts), in_specs=_in_specs(rows, seqs, pars, ts), out_specs=out_specs,
            out_shape=out_shape, name=name + "_fwd",
            compiler_params=_cparams(("arbitrary", "arbitrary", "arbitrary")))(*rows, *seqs, *pars)
        return tuple(res)

    def _bwd_call(rows, seqs, pars, cots, carried=None):
        B, S, ts = _dims(rows)
        outs = _out_struct(rows, seqs, pars, ts)
        n_out = len(outs)
        all_in = list(rows) + list(seqs) + list(pars)
        extra = [] if carried is None else [carried]

        def body(*refs):
            vals = _load(refs)
            if carried is not None:
                carried_ref, refs = refs[n_in + n_out], refs[:n_in + n_out] + refs[n_in + n_out + 1:]
            cts = []
            for c_ref, o, kind in zip(refs[n_in:n_in + n_out], outs, out_kinds):
                if kind == 'row':
                    cts.append(c_ref[0])
                else:
                    cts.append(jnp.broadcast_to(c_ref[...], o.shape))

            def g(*dv):
                full = list(vals)
                for i, v in zip(diff_idx, dv):
                    full[i] = v
                return tuple(f(*full))

            _, vjp = jax.vjp(g, *[vals[i] for i in diff_idx])
            grads = vjp(tuple(cts))
            b, s = pl.program_id(1), pl.program_id(2)
            for o_ref, i, gr in zip(refs[n_in + n_out:], diff_idx, grads):
                if i < n_rows:
                    if carried is not None and i == forward_row:
                        gr = gr + carried_ref[0]
                    o_ref[0] = gr.astype(o_ref.dtype)
                else:
                    first = (s == 0) if i < n_rows + n_seqs else ((b == 0) & (s == 0))
                    target = (lambda r: r.at[0]) if i < n_rows + n_seqs else (lambda r: r)

                    @pl.when(first)
                    def _(o_ref=o_ref, gr=gr, target=target):
                        target(o_ref)[...] = gr

                    @pl.when(jnp.logical_not(first))
                    def _(o_ref=o_ref, gr=gr, target=target):
                        target(o_ref)[...] += gr

        cot_specs = []
        for o, kind in zip(outs, out_kinds):
            d = o.shape[1]
            if kind == 'row':
                cot_specs.append(pl.BlockSpec((1, ts, d), lambda k, b, s: (b, s, k)))
            else:
                cot_specs.append(pl.BlockSpec((1, d), lambda k, b, s: (0, k)))
        out_shape, out_specs = [], []
        for i in diff_idx:
            a = all_in[i]
            if i < n_rows:
                out_shape.append(jax.ShapeDtypeStruct((B, S, _width(i, a)), BF16 if i in windows else a.dtype))
                out_specs.append(pl.BlockSpec((1, ts, _width(i, a) // ncol), lambda k, b, s: (b, s, k)))
                continue
            out_shape.append(jax.ShapeDtypeStruct(a.shape, a.dtype))
            if i < n_rows + n_seqs:
                out_specs.append(pl.BlockSpec((1, 1, a.shape[2] // ncol), lambda k, b, s: (b, 0, k)))
            else:
                out_specs.append(pl.BlockSpec((1, a.shape[1] // ncol), lambda k, b, s: (0, k)))
        if carried is not None:
            cot_specs.append(pl.BlockSpec((1, ts, carried.shape[2] // ncol), lambda k, b, s: (b, s, k)))
        res = pl.pallas_call(
            body, grid=(ncol, B, S // ts), in_specs=_in_specs(rows, seqs, pars, ts) + cot_specs,
            out_specs=out_specs, out_shape=out_shape, name=name + "_bwd",
            compiler_params=_cparams(("arbitrary", "arbitrary", "arbitrary")))(*all_in, *cots, *extra)
        grads = [None] * n_in
        for i, r in zip(diff_idx, res):
            grads[i] = r
        for i in nodiff:
            grads[i] = jnp.zeros_like(all_in[i])
        stand_in_grads = tuple(grads[i] for i in sorted(windows))
        for i in windows:
            grads[i] = jnp.zeros_like(all_in[i])
        return (tuple(grads[:n_rows]), tuple(grads[n_rows:n_rows + n_seqs]), tuple(grads[n_rows + n_seqs:]),
                stand_in_grads)

    def _outputs(rows, seqs, pars):
        res = _fwd_call(rows, seqs, pars)
        return res if forward_row is None else res + (rows[forward_row],)

    @jax.custom_vjp
    def op(rows, seqs, pars, stand_ins):
        return _outputs(rows, seqs, pars)

    def fwd(rows, seqs, pars, stand_ins):
        return _outputs(rows, seqs, pars), (rows, seqs, pars)

    def bwd(res, cots):
        rows, seqs, pars = res
        if forward_row is None:
            return _bwd_call(rows, seqs, pars, cots)
        return _bwd_call(rows, seqs, pars, cots[:-1], cots[-1])

    op.defvjp(fwd, bwd)
    return lambda rows, seqs, pars, stand_ins=(): op(tuple(rows), tuple(seqs), tuple(pars), tuple(stand_ins))


def _rms(x, g):
    x = x.astype(F32)
    return x * lax.rsqrt(jnp.mean(x * x, axis=-1, keepdims=True) + EPS) * g


def _silu(x):
    return x * lax.logistic(x)


def _f_silu(c):
    return (_silu(c),)


def _f_modulate(x, scale, shift, g):
    return ((_rms(x, g) * (1.0 + scale) + shift).astype(BF16),)


def _f_rms(x, g):
    return (_rms(x, g).astype(BF16),)


def _f_dt(dt_raw, dt_bias, a_log):
    z = dt_raw + dt_bias
    dt = jnp.maximum(z, 0.0) + jnp.log1p(jnp.exp(-jnp.abs(z)))
    return dt, dt * (-jnp.exp(a_log))


def _f_gated_norm(y, z, g):
    return (_rms(y * _silu(z.astype(F32)), g).astype(BF16),)


def _f_merge(attn, ssm, ga, gb):
    return ((lax.logistic(ga.astype(F32)) * attn + lax.logistic(gb.astype(F32)) * ssm).astype(BF16),)


def _f_post(x, m, gate, g):
    return (x + gate * _rms(m, g),)


def _f_final_loss(x, ff, target, gate, g):
    e = x + gate * _rms(ff, g) - target
    return (e * e * (0.5 / D_MODEL),)


def _rope_tables(posf, inv_lane):
    B, S, _ = posf.shape
    ts = min(S, 512)

    def body(p_ref, inv_ref, c_ref, a_ref, b_ref):
        ang = p_ref[0] * inv_ref[...]
        cs, sn = jnp.cos(ang), jnp.sin(ang)
        lane = lax.broadcasted_iota(jnp.int32, ang.shape, 1)
        c_ref[0] = jnp.where(lane < ROPE, cs, 0.0)
        a_ref[0] = jnp.where(lane < ROPE // 2, -sn, 0.0)
        b_ref[0] = jnp.where((lane >= ROPE // 2) & (lane < ROPE), sn, 0.0)

    spec = pl.BlockSpec((1, ts, LANE), lambda b, s: (b, s, 0))
    sds = jax.ShapeDtypeStruct((B, S, LANE), F32)
    return pl.pallas_call(
        body, grid=(B, S // ts),
        in_specs=[pl.BlockSpec((1, ts, 1), lambda b, s: (b, s, 0)), pl.BlockSpec((1, LANE), lambda b, s: (0, 0))],
        out_specs=[spec, spec, spec], out_shape=[sds, sds, sds], name="rope_tables",
        compiler_params=_cparams(("parallel", "parallel")))(posf, inv_lane)


def _rot(u, c, a, bm):
    return u * c + pltpu.roll(u, 96, 1) * a + pltpu.roll(u, 32, 1) * bm


def _rot_t(g, c, a, bm):
    return g * c + pltpu.roll(g * a, 32, 1) + pltpu.roll(g * bm, 96, 1)


def _rope_q_call(q, tabs, transpose, name):
    B, S, W = q.shape
    ts = min(S, 512)
    fn = _rot_t if transpose else _rot
    out_dtype = BF16

    def body(q_ref, c_ref, a_ref, b_ref, o_ref):
        tc, ta, tb = c_ref[0], a_ref[0], b_ref[0]
        for h in range(W // QK_PAD):
            u = q_ref[0, :, h * QK_PAD:(h + 1) * QK_PAD].astype(F32) * ATT_SCALE
            r = fn(u[:, NOPE:], tc, ta, tb)
            o_ref[0, :, h * QK_PAD:(h + 1) * QK_PAD] = jnp.concatenate([u[:, :NOPE], r], axis=1).astype(out_dtype)

    tspec = pl.BlockSpec((1, ts, LANE), lambda b, s: (b, s, 0))
    qspec = pl.BlockSpec((1, ts, W), lambda b, s: (b, s, 0))
    return pl.pallas_call(
        body, grid=(B, S // ts), in_specs=[qspec, tspec, tspec, tspec], out_specs=qspec,
        out_shape=jax.ShapeDtypeStruct(q.shape, out_dtype), name=name,
        compiler_params=_cparams(("parallel", "parallel")))(q, *tabs)


@jax.custom_vjp
def rope_q(q, tabs):
    return _rope_q_call(q, tabs, False, "rope_q_fwd")


def _rope_q_fwd(q, tabs):
    return _rope_q_call(q, tabs, False, "rope_q_fwd"), tabs


def _rope_q_bwd(tabs, g):
    return _rope_q_call(g, tabs, True, "rope_q_bwd"), tuple(jnp.zeros_like(t) for t in tabs)


rope_q.defvjp(_rope_q_fwd, _rope_q_bwd)


def _build_k_fwd_call(kv, kr, tabs):
    B, S, _ = kv.shape
    ts = min(S, 512)

    def body(kv_ref, kr_ref, c_ref, a_ref, b_ref, o_ref):
        r = _rot(kr_ref[0], c_ref[0], a_ref[0], b_ref[0]).astype(BF16)
        for h in range(N_HEADS):
            o_ref[0, :, h * QK_PAD:(h + 1) * QK_PAD] = jnp.concatenate(
                [kv_ref[0, :, h * NOPE:(h + 1) * NOPE], r], axis=1)

    tspec = pl.BlockSpec((1, ts, LANE), lambda b, s: (b, s, 0))
    kr_spec = pl.BlockSpec((1, ts, LANE), lambda b, s: (b, s, KR_LANE0 // LANE))
    return pl.pallas_call(
        body, grid=(B, S // ts),
        in_specs=[pl.BlockSpec((1, ts, N_HEADS * NOPE), lambda b, s: (b, s, 0)), kr_spec, tspec, tspec, tspec],
        out_specs=pl.BlockSpec((1, ts, N_HEADS * QK_PAD), lambda b, s: (b, s, 0)),
        out_shape=jax.ShapeDtypeStruct((B, S, N_HEADS * QK_PAD), BF16), name="build_k_fwd",
        compiler_params=_cparams(("parallel", "parallel")))(kv, kr, *tabs)


def _build_k_bwd_call(g, tabs):
    B, S, _ = g.shape
    ts = min(S, 512)

    def body(g_ref, c_ref, a_ref, b_ref, dk_ref, dr_ref):
        tot = None
        for h in range(N_HEADS):
            dk_ref[0, :, h * NOPE:(h + 1) * NOPE] = g_ref[0, :, h * QK_PAD:h * QK_PAD + NOPE]
            part = g_ref[0, :, h * QK_PAD + NOPE:(h + 1) * QK_PAD].astype(F32)
            tot = part if tot is None else tot + part
        dr_ref[0] = _rot_t(tot, c_ref[0], a_ref[0], b_ref[0]).astype(BF16)

    tspec = pl.BlockSpec((1, ts, LANE), lambda b, s: (b, s, 0))
    return pl.pallas_call(
        body, grid=(B, S // ts),
        in_specs=[pl.BlockSpec((1, ts, N_HEADS * QK_PAD), lambda b, s: (b, s, 0)), tspec, tspec, tspec],
        out_specs=[pl.BlockSpec((1, ts, N_HEADS * NOPE), lambda b, s: (b, s, 0)), tspec],
        out_shape=[jax.ShapeDtypeStruct((B, S, N_HEADS * NOPE), BF16), jax.ShapeDtypeStruct((B, S, LANE), BF16)],
        name="build_k_bwd", compiler_params=_cparams(("parallel", "parallel")))(g, *tabs)


ATT_SCALE = (NOPE + ROPE) ** -0.5
NEG = -1e30


def _att_tiles(S):
    t = min(S, 512)
    return t, S // t


def _scores(q, k, diagonal):
    s = lax.dot_general(q, k, (((1,), (1,)), ((), ())), preferred_element_type=F32)
    if diagonal:
        row = lax.broadcasted_iota(jnp.int32, s.shape, 0)
        col = lax.broadcasted_iota(jnp.int32, s.shape, 1)
        s = jnp.where(col <= row, s, NEG)
    return s


ATT_HB = 8


def _causal_pairs(n):
    pairs = [(i, j) for i in range(n) for j in range(i + 1)]
    return (jnp.asarray([p[0] for p in pairs], jnp.int32), jnp.asarray([p[1] for p in pairs], jnp.int32))


def _head(ref_or_val, h, w):
    return ref_or_val[:, h * w:(h + 1) * w]


def _attn_fwd_call(q, k, vsrc, v_blk0):
    B, S, _ = q.shape
    t, n = _att_tiles(S)
    qi, kj = _causal_pairs(n)

    def body(qi_ref, kj_ref, q_ref, k_ref, v_ref, o_ref, lse_ref, m_sc, l_sc, acc_sc):
        p_id = pl.program_id(2)
        i, j = qi_ref[p_id], kj_ref[p_id]

        @pl.when(j == 0)
        def _():
            m_sc[...] = jnp.full(m_sc.shape, NEG, F32)
            l_sc[...] = jnp.zeros(l_sc.shape, F32)
            acc_sc[...] = jnp.zeros(acc_sc.shape, F32)

        def step(diagonal):
            qa, ka, va = q_ref[0], k_ref[0], v_ref[0]
            for h in range(ATT_HB):
                lanes = slice(h * LANE, (h + 1) * LANE)
                s = _scores(_head(qa, h, QK_PAD), _head(ka, h, QK_PAD), diagonal)
                m_prev = m_sc[:, lanes]
                m_new = jnp.maximum(m_prev, jnp.max(s, axis=1, keepdims=True))
                alpha = jnp.exp(m_prev - m_new)
                p = jnp.exp(s - jnp.tile(m_new, (1, t // LANE)))
                l_sc[:, lanes] = alpha * l_sc[:, lanes] + jnp.sum(p, axis=1, keepdims=True)
                acc_sc[:, lanes] = alpha * acc_sc[:, lanes] + jnp.dot(p.astype(BF16), _head(va, h, V_DIM),
                                                                      preferred_element_type=F32)
                m_sc[:, lanes] = m_new

        @pl.when(j < i)
        def _():
            step(False)

        @pl.when(j == i)
        def _():
            step(True)
            o_ref[0] = (acc_sc[...] / l_sc[...]).astype(BF16)
            lse_ref[0] = m_sc[...] + jnp.log(l_sc[...])

    wq, wv = ATT_HB * QK_PAD, ATT_HB * V_DIM
    grid_spec = pltpu.PrefetchScalarGridSpec(
        num_scalar_prefetch=2, grid=(B, N_HEADS // ATT_HB, qi.shape[0]),
        in_specs=[pl.BlockSpec((1, t, wq), lambda b, h, p, qi, kj: (b, qi[p], h)),
                  pl.BlockSpec((1, t, wq), lambda b, h, p, qi, kj: (b, kj[p], h)),
                  pl.BlockSpec((1, t, wv), lambda b, h, p, qi, kj: (b, kj[p], v_blk0 + h))],
        out_specs=[pl.BlockSpec((1, t, wv), lambda b, h, p, qi, kj: (b, qi[p], h)),
                   pl.BlockSpec((1, t, wv), lambda b, h, p, qi, kj: (b, qi[p], h))],
        scratch_shapes=[pltpu.VMEM((t, wv), F32), pltpu.VMEM((t, wv), F32), pltpu.VMEM((t, wv), F32)])
    return pl.pallas_call(
        body, grid_spec=grid_spec,
        out_shape=[jax.ShapeDtypeStruct((B, S, N_HEADS * V_DIM), BF16),
                   jax.ShapeDtypeStruct((B, S, N_HEADS * LANE), F32)],
        name="attn_fwd", compiler_params=_cparams(("parallel", "parallel", "arbitrary")))(qi, kj, q, k, vsrc)


def _attn_p_ds(q, k, v, o, do, lse, diagonal, t):
    s = _scores(q, k, diagonal)
    p = jnp.exp(s - jnp.tile(lse, (1, t // LANE)))
    dp = lax.dot_general(do.astype(BF16), v, (((1,), (1,)), ((), ())), preferred_element_type=F32)
    delta = jnp.sum(do.astype(F32) * o.astype(F32), axis=1, keepdims=True)
    ds = p * (dp - delta)
    return p, ds


ATT_HB_BWD = 4


def _attn_bwd_call(q, k, vsrc, o, do, lse):
    B, S, _ = q.shape
    t, n = _att_tiles(S)
    qi, kj = _causal_pairs(n)
    n_pairs = qi.shape[0]
    hb = ATT_HB_BWD
    v_blk0 = N_HEADS // hb

    def body(qi_ref, kj_ref, q_ref, k_ref, v_ref, o_ref, do_ref, lse_ref, dq_ref, dk_ref, dv_ref, dq_sc, dk_sc, dv_sc):
        p_id = pl.program_id(2)
        i, j = qi_ref[p_id], kj_ref[p_id]

        @pl.when(p_id == 0)
        def _():
            dk_sc[...] = jnp.zeros(dk_sc.shape, F32)
            dv_sc[...] = jnp.zeros(dv_sc.shape, F32)

        @pl.when(j == 0)
        def _():
            dq_sc[...] = jnp.zeros(dq_sc.shape, F32)

        rows = pl.ds(pl.multiple_of(j * t, t), t)

        def step(diagonal):
            qa, ka, va, oa, doa, la = q_ref[0], k_ref[0], v_ref[0], o_ref[0], do_ref[0], lse_ref[0]
            for h in range(hb):
                qb, kb, dob = _head(qa, h, QK_PAD), _head(ka, h, QK_PAD), _head(doa, h, V_DIM)
                p, ds = _attn_p_ds(qb, kb, _head(va, h, V_DIM), _head(oa, h, V_DIM), dob, _head(la, h, LANE),
                                   diagonal, t)
                dsb = ds.astype(BF16)
                dq_sc[:, h * QK_PAD:(h + 1) * QK_PAD] += jnp.dot(dsb, kb, preferred_element_type=F32)
                dv_sc[rows, h * V_DIM:(h + 1) * V_DIM] += lax.dot_general(
                    p.astype(BF16), dob.astype(BF16), (((0,), (0,)), ((), ())), preferred_element_type=F32)
                dk_sc[rows, h * QK_PAD:(h + 1) * QK_PAD] += lax.dot_general(
                    dsb, qb, (((0,), (0,)), ((), ())), preferred_element_type=F32)

        @pl.when(j < i)
        def _():
            step(False)

        @pl.when(j == i)
        def _():
            step(True)
            dq_ref[0] = dq_sc[...].astype(BF16)

        @pl.when(i == n - 1)
        def _():
            dk_ref[0] = dk_sc[rows, :].astype(BF16)
            dv_ref[0] = dv_sc[rows, :].astype(BF16)

    wq, wv = hb * QK_PAD, hb * V_DIM
    at_q = lambda b, h, p, qi, kj: (b, qi[p], h)
    at_k = lambda b, h, p, qi, kj: (b, kj[p], h)
    at_done = lambda b, h, p, qi, kj: (b, jnp.where(qi[p] == n - 1, kj[p], 0), h)
    grid_spec = pltpu.PrefetchScalarGridSpec(
        num_scalar_prefetch=2, grid=(B, N_HEADS // hb, n_pairs),
        in_specs=[pl.BlockSpec((1, t, wq), at_q), pl.BlockSpec((1, t, wq), at_k),
                  pl.BlockSpec((1, t, wv), lambda b, h, p, qi, kj: (b, kj[p], v_blk0 + h)),
                  pl.BlockSpec((1, t, wv), at_q), pl.BlockSpec((1, t, wv), at_q), pl.BlockSpec((1, t, wv), at_q)],
        out_specs=[pl.BlockSpec((1, t, wq), at_q), pl.BlockSpec((1, t, wq), at_done), pl.BlockSpec((1, t, wv), at_done)],
        scratch_shapes=[pltpu.VMEM((t, wq), F32), pltpu.VMEM((S, wq), F32), pltpu.VMEM((S, wv), F32)])
    return pl.pallas_call(
        body, grid_spec=grid_spec,
        out_shape=[jax.ShapeDtypeStruct((B, S, N_HEADS * QK_PAD), BF16),
                   jax.ShapeDtypeStruct((B, S, N_HEADS * QK_PAD), BF16),
                   jax.ShapeDtypeStruct((B, S, N_HEADS * V_DIM), BF16)],
        name="attn_bwd", compiler_params=_cparams(("parallel", "parallel", "arbitrary")))(
            qi, kj, q, k, vsrc, o, do, lse)


@jax.custom_vjp
def attention(q, kv, src, stand_in, tabs):
    return _attn_fwd_call(q, _build_k_fwd_call(kv, src, tabs), kv, N_HEADS // ATT_HB)[0]


def _attention_fwd(q, kv, src, stand_in, tabs):
    k = _build_k_fwd_call(kv, src, tabs)
    o, lse = _attn_fwd_call(q, k, kv, N_HEADS // ATT_HB)
    return o, (q, k, kv, o, lse, src, tabs)


def _attention_bwd(res, do):
    q, k, kv, o, lse, src, tabs = res
    dq, dk, dv = _attn_bwd_call(q, k, kv, o, do, lse)
    dk_nope, dk_rope = _build_k_bwd_call(dk, tabs)
    return (dq, jnp.concatenate([dk_nope, dv], axis=-1), jnp.zeros_like(src), dk_rope,
            tuple(jnp.zeros_like(t) for t in tabs))


attention.defvjp(_attention_fwd, _attention_bwd)


SUBLANES = 8


def _zero_tail(v):
    return jnp.concatenate([v, jnp.zeros((SUBLANES, v.shape[1]), v.dtype)], axis=0)


def _shift_down(vz, sh):
    return pltpu.roll(vz, sh, 0)[:vz.shape[0] - SUBLANES]


def _shift_up(vz, sh):
    return pltpu.roll(vz, vz.shape[0] - sh, 0)[:vz.shape[0] - SUBLANES]


def _conv_pre(u, uz, w_ref, b_ref):
    acc = b_ref[...] + w_ref[pl.ds(CONV_K - 1, 1), :] * u
    for k in range(CONV_K - 1):
        acc = acc + w_ref[pl.ds(k, 1), :] * _shift_down(uz, CONV_K - 1 - k)
    return acc


def _conv_fwd_call(src, w, b):
    B, S, _ = src.shape
    C = w.shape[1]

    def body(u_ref, w_ref, b_ref, o_ref):
        uu = u_ref[0].astype(F32)
        o_ref[0] = _silu(_conv_pre(uu, _zero_tail(uu), w_ref, b_ref))

    spec = pl.BlockSpec((1, S, LANE), lambda c, bb: (bb, 0, c))
    return pl.pallas_call(
        body, grid=(C // LANE, B),
        in_specs=[pl.BlockSpec((1, S, LANE), lambda c, bb: (bb, 0, c + CONV_LANE0 // LANE)),
                  pl.BlockSpec((CONV_K, LANE), lambda c, bb: (0, c)), pl.BlockSpec((1, LANE), lambda c, bb: (0, c))],
        out_specs=spec, out_shape=jax.ShapeDtypeStruct((B, S, C), F32), name="conv_fwd",
        compiler_params=_cparams(("parallel", "arbitrary")))(src, w, b)


def _conv_bwd_call(src, w, b, g):
    B, S, _ = src.shape
    C = w.shape[1]

    def body(u_ref, w_ref, b_ref, g_ref, du_ref, dw_ref, db_ref):
        uu = u_ref[0].astype(F32)
        uz = _zero_tail(uu)
        pre = _conv_pre(uu, uz, w_ref, b_ref)
        sg = lax.logistic(pre)
        dpre = g_ref[0] * sg * (1.0 + pre * (1.0 - sg))
        dz = _zero_tail(dpre)
        du = w_ref[pl.ds(CONV_K - 1, 1), :] * dpre
        dws = [None] * CONV_K
        dws[CONV_K - 1] = jnp.sum(dpre * uu, axis=0, keepdims=True)
        for k in range(CONV_K - 1):
            sh = CONV_K - 1 - k
            du = du + w_ref[pl.ds(k, 1), :] * _shift_up(dz, sh)
            dws[k] = jnp.sum(dpre * _shift_down(uz, sh), axis=0, keepdims=True)
        du_ref[0] = du.astype(du_ref.dtype)
        dbv = jnp.sum(dpre, axis=0, keepdims=True)
        first = pl.program_id(1) == 0

        @pl.when(first)
        def _():
            for k in range(CONV_K):
                dw_ref[pl.ds(k, 1), :] = dws[k]
            db_ref[...] = dbv

        @pl.when(jnp.logical_not(first))
        def _():
            for k in range(CONV_K):
                dw_ref[pl.ds(k, 1), :] += dws[k]
            db_ref[...] += dbv

    spec = pl.BlockSpec((1, S, LANE), lambda c, bb: (bb, 0, c))
    wspec = pl.BlockSpec((CONV_K, LANE), lambda c, bb: (0, c))
    bspec = pl.BlockSpec((1, LANE), lambda c, bb: (0, c))
    uspec = pl.BlockSpec((1, S, LANE), lambda c, bb: (bb, 0, c + CONV_LANE0 // LANE))
    return pl.pallas_call(
        body, grid=(C // LANE, B), in_specs=[uspec, wspec, bspec, spec], out_specs=[spec, wspec, bspec],
        out_shape=[jax.ShapeDtypeStruct((B, S, C), BF16), jax.ShapeDtypeStruct(w.shape, F32),
                   jax.ShapeDtypeStruct(b.shape, F32)],
        name="conv_bwd", compiler_params=_cparams(("parallel", "arbitrary")))(src, w, b, g)


@jax.custom_vjp
def conv_silu(src, stand_in, w, b):
    return _conv_fwd_call(src, w, b)


def _conv_silu_fwd(src, stand_in, w, b):
    return _conv_fwd_call(src, w, b), (src, w, b)


def _conv_silu_bwd(res, g):
    du, dw, db = _conv_bwd_call(*res, g)
    return jnp.zeros_like(res[0]), du, dw, db


conv_silu.defvjp(_conv_silu_fwd, _conv_silu_bwd)


def _chunk_cumsum_call(a, reverse, name):
    B, S, W = a.shape
    per_step = min(S // CHUNK, 8)

    def body(a_ref, o_ref):
        r = lax.broadcasted_iota(jnp.int32, (CHUNK, CHUNK), 0)
        c = lax.broadcasted_iota(jnp.int32, (CHUNK, CHUNK), 1)
        tri = jnp.where((c >= r) if reverse else (c <= r), 1.0, 0.0).astype(F32)
        for i in range(per_step):
            rows = pl.ds(i * CHUNK, CHUNK)
            o_ref[0, rows, :] = jnp.dot(tri, a_ref[0, rows, :], preferred_element_type=F32,
                                        precision=lax.Precision.HIGHEST)

    spec = pl.BlockSpec((1, per_step * CHUNK, W), lambda b, c: (b, c, 0))
    return pl.pallas_call(body, grid=(B, S // (per_step * CHUNK)), in_specs=[spec], out_specs=spec,
                          out_shape=jax.ShapeDtypeStruct(a.shape, F32), name=name,
                          compiler_params=_cparams(("parallel", "parallel")))(a)


@jax.custom_vjp
def chunk_cumsum(a):
    return _chunk_cumsum_call(a, False, "chunk_cumsum_fwd")


chunk_cumsum.defvjp(lambda a: (_chunk_cumsum_call(a, False, "chunk_cumsum_fwd"), None),
                    lambda _, g: (_chunk_cumsum_call(g, True, "chunk_cumsum_bwd"),))


GROUP_W = 4 * HEAD_P
HPG = SSM_HEADS // SSM_GROUPS


def _ssd_masks():
    lane = lax.broadcasted_iota(jnp.int32, (1, GROUP_W), 1)
    return [((lane >= HEAD_P * j) & (lane < HEAD_P * (j + 1))).astype(F32) for j in range(HPG)]


def _ssd_decays(ac_cols, acr_ref, gi):
    r = lax.broadcasted_iota(jnp.int32, (CHUNK, CHUNK), 0)
    c = lax.broadcasted_iota(jnp.int32, (CHUNK, CHUNK), 1)
    return [jnp.exp(jnp.where(c <= r, ac_cols[j] - acr_ref[0, gi * HPG + j], NEG)) for j in range(HPG)]


def _ssd_cols(blk, g):
    lane = lax.broadcasted_iota(jnp.int32, blk.shape, 1)
    return [jnp.sum(jnp.where(lane == HPG * g + j, blk, 0.0), axis=1, keepdims=True) for j in range(HPG)]


def _ssd_spread(cols):
    lane = lax.broadcasted_iota(jnp.int32, (1, GROUP_W), 1)
    out = jnp.broadcast_to(cols[HPG - 1], (CHUNK, GROUP_W))
    for j in range(HPG - 2, -1, -1):
        out = jnp.where(lane < HEAD_P * (j + 1), cols[j], out)
    return out


def _ssd_gather(val, cols, masks, g):
    lane = lax.broadcasted_iota(jnp.int32, (1, LANE), 1)
    out = jnp.zeros((CHUNK, LANE), F32)
    for j in range(HPG):
        tot = jnp.sum(val * masks[j], axis=1, keepdims=True)
        if cols is not None:
            tot = tot + cols[j]
        out = out + tot * (lane == HPG * g + j).astype(F32)
    return out


def _dot(a, b, dims):
    return lax.dot_general(a.astype(BF16), b.astype(BF16), (dims, ((), ())), preferred_element_type=F32)


NN = ((1,), (0,))
NT = ((1,), (1,))
TN = ((0,), (0,))


XBC_W = GROUP_W + 2 * STATE_N


SSD_STEP_GROUPS_FWD = 8
SSD_STEP_GROUPS_BWD = 8


def _ssd_load(xbc_ref, dt_ref, ac_ref, masks, g, gi):
    x = xbc_ref[0, :, gi * XBC_W:gi * XBC_W + GROUP_W]
    bm = xbc_ref[0, :, gi * XBC_W + GROUP_W:gi * XBC_W + GROUP_W + STATE_N]
    cm = xbc_ref[0, :, gi * XBC_W + GROUP_W + STATE_N:(gi + 1) * XBC_W]
    ac_cols = _ssd_cols(ac_ref[0], g)
    dt = _ssd_spread(_ssd_cols(dt_ref[0], g))
    ac = _ssd_spread(ac_cols)
    is_last = (lax.broadcasted_iota(jnp.int32, (CHUNK, GROUP_W), 0) == CHUNK - 1).astype(F32)
    return x, bm, cm, dt, ac, ac_cols, is_last


def _ssd_in_specs(nc, rev, gb):
    cc = (lambda c: nc - 1 - c) if rev else (lambda c: c)
    return [pl.BlockSpec((1, CHUNK, gb * XBC_W), lambda b, g, c: (b, cc(c), g)),
            pl.BlockSpec((1, CHUNK, LANE), lambda b, g, c: (b, cc(c), 0)),
            pl.BlockSpec((1, CHUNK, LANE), lambda b, g, c: (b, cc(c), 0)),
            pl.BlockSpec((1, gb * HPG, 1, CHUNK), lambda b, g, c: (b, g, 0, cc(c))),
            pl.BlockSpec((1, gb * GROUP_W), lambda b, g, c: (0, g))]


def _ssd_fwd_call(xbc, dtp, acp, acr, dsk):
    B, S, _ = xbc.shape
    nc = S // CHUNK
    gb = SSD_STEP_GROUPS_FWD

    def body(xbc_ref, dt_ref, ac_ref, ar_ref, ds_ref, y_ref, hp_ref, h_sc):
        @pl.when(pl.program_id(2) == 0)
        def _():
            h_sc[...] = jnp.zeros(h_sc.shape, F32)

        masks = _ssd_masks()
        ys = []
        for gi in range(gb):
            grp = gb * pl.program_id(1) + gi
            x, bm, cm, dt, ac, ac_cols, is_last = _ssd_load(xbc_ref, dt_ref, ac_ref, masks, grp, gi)
            last = jnp.sum(ac * is_last, axis=0, keepdims=True)
            decays = _ssd_decays(ac_cols, ar_ref, gi)
            xd = x * dt
            cb = _dot(cm, bm, NT)
            hprev = h_sc[gi]
            hp_ref[0, gi, 0] = hprev
            y = _dot(cm, hprev, NN) * jnp.exp(ac) + ds_ref[:, gi * GROUP_W:(gi + 1) * GROUP_W] * x
            y = y + _dot(jnp.concatenate([cb * d for d in decays], axis=1),
                         jnp.concatenate([xd * m for m in masks], axis=0), NN)
            ys.append(y)
            h_sc[gi] = hprev * jnp.exp(last) + _dot(bm, xd * jnp.exp(last - ac), TN)
        y_ref[0] = jnp.concatenate(ys, axis=1)

    ng = SSM_GROUPS // gb
    return pl.pallas_call(
        body, grid=(B, ng, nc), in_specs=_ssd_in_specs(nc, False, gb),
        out_specs=[pl.BlockSpec((1, CHUNK, gb * GROUP_W), lambda b, g, c: (b, c, g)),
                   pl.BlockSpec((1, gb, 1, STATE_N, GROUP_W), lambda b, g, c: (b, g, c, 0, 0))],
        out_shape=[jax.ShapeDtypeStruct((B, S, D_INNER), F32),
                   jax.ShapeDtypeStruct((B, SSM_GROUPS, nc, STATE_N, GROUP_W), F32)],
        scratch_shapes=[pltpu.VMEM((gb, STATE_N, GROUP_W), F32)], name="ssd_fwd",
        compiler_params=_cparams(("parallel", "parallel", "arbitrary")))(xbc, dtp, acp, acr, dsk)


def _ssd_bwd_call(xbc, dtp, acp, acr, dsk, hps, dy):
    B, S, _ = xbc.shape
    nc = S // CHUNK
    gb = SSD_STEP_GROUPS_BWD

    def body(xbc_ref, dt_ref, ac_ref, ar_ref, ds_ref, hp_ref, dy_ref,
             dxbc_ref, ddt_ref, dac_ref, dar_ref, dds_ref, dh_sc):
        first = pl.program_id(2) == 0

        @pl.when(first)
        def _():
            dh_sc[...] = jnp.zeros(dh_sc.shape, F32)

        masks = _ssd_masks()
        dxbc_parts, dds_parts = [], []
        for gi in range(gb):
            grp = gb * pl.program_id(0) + gi
            x, bm, cm, dt, ac, ac_cols, is_last = _ssd_load(xbc_ref, dt_ref, ac_ref, masks, grp, gi)
            last = jnp.sum(ac * is_last, axis=0, keepdims=True)
            g = dy_ref[0, :, gi * GROUP_W:(gi + 1) * GROUP_W]
            hprev = hp_ref[0, gi, 0]
            dh = dh_sc[gi]
            decays = _ssd_decays(ac_cols, ar_ref, gi)
            dcols = []
            xd = x * dt
            cb = _dot(cm, bm, NT)
            e_c = jnp.exp(ac)
            e_end = jnp.exp(last - ac)
            e_last = jnp.exp(last)
            z = _dot(cm, hprev, NN)
            dz = g * e_c
            dac = g * z * e_c
            dc = _dot(dz, hprev, NT)
            dhprev = _dot(cm, dz, TN) + dh * e_last
            dcb = jnp.zeros((CHUNK, CHUNK), F32)
            gjs = [cb * d for d in decays]
            g_heads = jnp.concatenate([g * m for m in masks], axis=0)
            dg_heads = _dot(g_heads, xd, NT)
            dxd = _dot(jnp.concatenate(gjs, axis=0), g_heads, TN)
            for j in range(HPG):
                gj = gjs[j]
                dgj = dg_heads[j * CHUNK:(j + 1) * CHUNK]
                dcb = dcb + dgj * decays[j]
                dseg = dgj * gj
                dcols.append(jnp.sum(dseg, axis=1, keepdims=True))
                dar_ref[0, gi * HPG + j] = -jnp.sum(dseg, axis=0, keepdims=True)
            dc = dc + _dot(dcb, bm, NN)
            db = _dot(dcb, cm, TN)
            sx = xd * e_end
            db = db + _dot(sx, dh, NT)
            dsx = _dot(bm, dh, NN)
            dxd = dxd + dsx * e_end
            de = dsx * sx
            dac = dac - de
            dlast = jnp.sum(de, axis=0, keepdims=True) + jnp.sum(dh * hprev, axis=0, keepdims=True) * e_last
            dsk = ds_ref[:, gi * GROUP_W:(gi + 1) * GROUP_W]
            dxbc_parts += [dxd * dt + dsk * g, db, dc]
            ddt_ref[0, gi] = _ssd_gather(dxd * x, None, masks, grp)
            dac_ref[0, gi] = _ssd_gather(dac + is_last * dlast, dcols, masks, grp)
            dds_parts.append(jnp.sum(g * x, axis=0, keepdims=True))
            dh_sc[gi] = dhprev
        dxbc_ref[0] = jnp.concatenate(dxbc_parts, axis=1)
        dds = jnp.concatenate(dds_parts, axis=1)
        first_all = first & (pl.program_id(1) == 0)

        @pl.when(first_all)
        def _():
            dds_ref[...] = dds

        @pl.when(jnp.logical_not(first_all))
        def _():
            dds_ref[...] += dds

    rc = lambda c: nc - 1 - c
    ng = SSM_GROUPS // gb
    in_specs = [pl.BlockSpec(s.block_shape, (lambda g, b, c, f=s.index_map: f(b, g, c))) for s in _ssd_in_specs(nc, True, gb)]
    in_specs.append(pl.BlockSpec((1, gb, 1, STATE_N, GROUP_W), lambda g, b, c: (b, g, rc(c), 0, 0)))
    in_specs.append(pl.BlockSpec((1, CHUNK, gb * GROUP_W), lambda g, b, c: (b, rc(c), g)))
    per_group = pl.BlockSpec((1, gb, CHUNK, LANE), lambda g, b, c: (b, g, rc(c), 0))
    out_specs = [pl.BlockSpec((1, CHUNK, gb * XBC_W), lambda g, b, c: (b, rc(c), g)), per_group, per_group,
                 pl.BlockSpec((1, gb * HPG, 1, CHUNK), lambda g, b, c: (b, g, 0, rc(c))),
                 pl.BlockSpec((1, gb * GROUP_W), lambda g, b, c: (0, g))]
    out_shape = [jax.ShapeDtypeStruct(xbc.shape, F32),
                 jax.ShapeDtypeStruct((B, SSM_GROUPS, S, LANE), F32), jax.ShapeDtypeStruct((B, SSM_GROUPS, S, LANE), F32),
                 jax.ShapeDtypeStruct(acr.shape, F32), jax.ShapeDtypeStruct(dsk.shape, F32)]
    return pl.pallas_call(
        body, grid=(ng, B, nc), in_specs=in_specs, out_specs=out_specs, out_shape=out_shape,
        scratch_shapes=[pltpu.VMEM((gb, STATE_N, GROUP_W), F32)], name="ssd_bwd",
        compiler_params=_cparams(("arbitrary", "arbitrary", "arbitrary")))(xbc, dtp, acp, acr, dsk, hps, dy)


@jax.custom_vjp
def ssd(xbc, dtp, acp, acr, dsk):
    return _ssd_fwd_call(xbc, dtp, acp, acr, dsk)[0]


def _ssd_fwd(xbc, dtp, acp, acr, dsk):
    y, hps = _ssd_fwd_call(xbc, dtp, acp, acr, dsk)
    return y, (xbc, dtp, acp, acr, dsk, hps)


def _ssd_bwd(res, dy):
    dxbc, ddt, dac, dacr, dds = _ssd_bwd_call(*res, dy)
    return dxbc, jnp.sum(ddt, axis=1), jnp.sum(dac, axis=1), dacr, dds


ssd.defvjp(_ssd_fwd, _ssd_bwd)


def _pack_small(arrs):
    flat = jnp.concatenate([a.reshape(-1) for a in arrs])
    rows = -(-flat.shape[0] // (8 * LANE)) * 8
    return jnp.pad(flat, (0, rows * LANE - flat.shape[0])).reshape(rows, LANE)


def _unpack_small(buf, shapes):
    flat = buf.reshape(-1)
    out, off = [], 0
    for shp in shapes:
        n = int(np.prod(shp))
        out.append(flat[off:off + n].reshape(shp))
        off += n
    return out


def _rows_tile(rows, cap):
    for cand in range(min(rows, cap), 7, -8):
        if rows % cand == 0:
            return cand
    return rows


def _pair_sum(mine, theirs, cidx, name):
    n4, kk, nn = mine.shape
    half = kk // 2
    tr = _rows_tile(half, 256)
    nb = half // tr

    def body(c_ref, a_ref, b_ref, o_ref, ob_ref):
        tot = a_ref[...] + b_ref[...]
        o_ref[...] = tot
        ob_ref[...] = tot.astype(BF16)

    spec = pl.BlockSpec((1, tr, nn), lambda j, i, c: (j, i, 0))
    grid_spec = pltpu.PrefetchScalarGridSpec(
        num_scalar_prefetch=1, grid=(n4, nb),
        in_specs=[pl.BlockSpec((1, tr, nn), lambda j, i, c: (j, c[0] * nb + i, 0)), spec], out_specs=[spec, spec])
    return pl.pallas_call(
        body, grid_spec=grid_spec,
        out_shape=[jax.ShapeDtypeStruct((n4, half, nn), F32), jax.ShapeDtypeStruct((n4, half, nn), BF16)],
        name=name, compiler_params=_cparams(("parallel", "parallel")))(cidx, mine, theirs)


def _chip_sum(quad, pair, chip_idx, name):
    _, rows, nn = quad.shape
    tr = _rows_tile(rows, 256)

    def body(s_ref, q_ref, p_ref, o_ref):
        for mine in range(4):
            @pl.when(s_ref[0] == mine)
            def _(mine=mine):
                acc = None
                for d in range(4):
                    term = p_ref[0] if d == mine else q_ref[d].astype(F32)
                    acc = term if acc is None else acc + term
                o_ref[...] = acc

    grid_spec = pltpu.PrefetchScalarGridSpec(
        num_scalar_prefetch=1, grid=(rows // tr,),
        in_specs=[pl.BlockSpec((4, tr, nn), lambda i, s: (0, i, 0)), pl.BlockSpec((1, tr, nn), lambda i, s: (s[0], i, 0))],
        out_specs=pl.BlockSpec((tr, nn), lambda i, s: (i, 0)))
    return pl.pallas_call(body, grid_spec=grid_spec, out_shape=jax.ShapeDtypeStruct((rows, nn), F32), name=name,
                          compiler_params=_cparams(("parallel",)))(chip_idx, quad, pair)


def _adam_halves_call(w, mine, other, cidx, m, v, name):
    _, rows, nn = w.shape
    half = rows // 2
    tr = _rows_tile(half, 128)
    nb = half // tr

    def body(c_ref, w_ref, a_ref, b_ref, m_ref, v_ref, g_ref, d_ref, nm_ref, nv_ref):
        upper = (pl.program_id(0) >= nb).astype(jnp.int32)
        g = jnp.where(upper == c_ref[0], a_ref[...], b_ref[...])
        g_ref[0] = g
        d_ref[0], nm_ref[0], nv_ref[0] = _adam_fn(w_ref[0], g, m_ref[0], v_ref[0])

    spec = pl.BlockSpec((1, tr, nn), lambda i, c: (0, i, 0))
    hspec = pl.BlockSpec((tr, nn), lambda i, c: (i % nb, 0))
    grid_spec = pltpu.PrefetchScalarGridSpec(num_scalar_prefetch=1, grid=(2 * nb,),
                                             in_specs=[spec, hspec, hspec, spec, spec], out_specs=[spec] * 4)
    return pl.pallas_call(body, grid_spec=grid_spec, out_shape=[jax.ShapeDtypeStruct(w.shape, F32)] * 4, name=name,
                          compiler_params=_cparams(("parallel",)))(cidx, w, mine, other, m, v)


def _stack_sum(stack, name):
    n, rows, nn = stack.shape
    tr = _rows_tile(rows, 256)

    def body(s_ref, o_ref):
        acc = s_ref[0]
        for d in range(1, n):
            acc = acc + s_ref[d]
        o_ref[...] = acc

    return pl.pallas_call(
        body, grid=(rows // tr,), in_specs=[pl.BlockSpec((n, tr, nn), lambda i: (0, i, 0))],
        out_specs=pl.BlockSpec((tr, nn), lambda i: (i, 0)), out_shape=jax.ShapeDtypeStruct((rows, nn), F32),
        name=name, compiler_params=_cparams(("parallel",)))(stack)


def _adam_call(w, g, m, v, name):
    rows, nn = w.shape
    tr = _rows_tile(rows, 128)

    def body(w_ref, g_ref, m_ref, v_ref, d_ref, nm_ref, nv_ref):
        d_ref[...], nm_ref[...], nv_ref[...] = _adam_fn(w_ref[...], g_ref[...], m_ref[...], v_ref[...])

    spec = pl.BlockSpec((tr, nn), lambda i: (i, 0))
    sds = jax.ShapeDtypeStruct((rows, nn), F32)
    return pl.pallas_call(body, grid=(rows // tr,), in_specs=[spec] * 4, out_specs=[spec] * 3,
                          out_shape=[sds] * 3, name=name, compiler_params=_cparams(("parallel",)))(w, g, m, v)


def _adam_fn(w, g, m, v):
    m = ADAM_B1 * m + (1.0 - ADAM_B1) * g
    v = ADAM_B2 * v + (1.0 - ADAM_B2) * (g * g)
    m_hat = m / (1.0 - ADAM_B1 ** ADAM_STEP)
    v_hat = v / (1.0 - ADAM_B2 ** ADAM_STEP)
    delta = -ADAM_LR * (m_hat / (jnp.sqrt(v_hat) + ADAM_EPS) + ADAM_WD * w)
    return delta, m, v


def _mesh_pos():
    return lax.axis_index("x"), lax.axis_index("y"), lax.axis_index("c")


def _other_chips(x, y):
    return [(1 - x, y), (x, 1 - y), (1 - x, 1 - y)]


HBM_SPEC = pl.BlockSpec(memory_space=pl.ANY)


def _remote(src, dst, send_sems, recv_sems, k, to):
    return pltpu.make_async_remote_copy(src_ref=src, dst_ref=dst, send_sem=send_sems.at[k], recv_sem=recv_sems.at[k],
                                        device_id=to, device_id_type=MESH)


def _half_rows(c, rows, align):
    half = rows // 2
    return (pl.ds(pl.multiple_of(c * half, align), half), pl.ds(pl.multiple_of((1 - c) * half, align), half))


def _gather_weights(mats, conv):
    n = len(mats)

    def body(*refs):
        ins, conv_in = refs[:n], refs[n]
        outs, conv_out = refs[n + 1:2 * n + 1], refs[2 * n + 1]
        send_sems, recv_sems, local_sem = refs[2 * n + 2:]
        x, y, c = _mesh_pos()
        me, sibling, s = (x, y, c), (x, y, 1 - c), 2 * x + y
        chips = _other_chips(x, y)
        rows = [_half_rows(c, m.shape[0], 16) for m in mats]
        own = pltpu.make_async_copy(conv_in, conv_out.at[s], local_sem)
        own.start()
        sent = []
        for i in range(n):
            mine = rows[i][0]
            for j, (cx, cy) in enumerate(chips):
                sent.append(_remote(ins[i].at[mine], outs[i].at[s, mine], send_sems, recv_sems, 6 * i + j, (cx, cy, c)))
        for j, (cx, cy) in enumerate(chips):
            sent.append(_remote(conv_in, conv_out.at[s], send_sems, recv_sems, 6 * n + j, (cx, cy, c)))
        for cp in sent:
            cp.start()
        for i in range(n):
            mine = rows[i][0]
            for j, (cx, cy) in enumerate(chips):
                landed = outs[i].at[2 * cx + cy, mine]
                _remote(landed, landed, send_sems, recv_sems, 6 * i + j, me).wait_recv()
                fwd = _remote(landed, landed, send_sems, recv_sems, 6 * i + 3 + j, sibling)
                fwd.start()
                sent.append(fwd)
        for j, (cx, cy) in enumerate(chips):
            slot = conv_out.at[2 * cx + cy]
            _remote(slot, slot, send_sems, recv_sems, 6 * n + j, me).wait_recv()
        for i in range(n):
            theirs_rows = rows[i][1]
            for j, (cx, cy) in enumerate(chips):
                theirs = outs[i].at[2 * cx + cy, theirs_rows]
                _remote(theirs, theirs, send_sems, recv_sems, 6 * i + 3 + j, me).wait_recv()
        for cp in sent:
            cp.wait_send()
        own.wait()

    out_shape = [jax.ShapeDtypeStruct((4,) + m.shape, m.dtype) for m in mats]
    out_shape.append(jax.ShapeDtypeStruct((4,) + conv.shape, conv.dtype))
    res = pl.pallas_call(
        body, in_specs=[HBM_SPEC] * (n + 1), out_specs=[HBM_SPEC] * (n + 1), out_shape=out_shape,
        scratch_shapes=[pltpu.SemaphoreType.DMA((6 * n + 3,)), pltpu.SemaphoreType.DMA((6 * n + 3,)),
                        pltpu.SemaphoreType.DMA],
        name="all_gather_weights")(*mats, conv)
    return res[:n], res[n]


def _sibling_exchange(stacks):
    n = len(stacks)

    def body(*refs):
        ins, outs = refs[:n], refs[n:2 * n]
        send_sems, recv_sems = refs[2 * n:]
        x, y, c = _mesh_pos()
        cps = []
        for i in range(n):
            theirs = _half_rows(c, stacks[i].shape[1], 8)[1]
            cps.append(_remote(ins[i].at[:, theirs, :], outs[i], send_sems, recv_sems, i, (x, y, 1 - c)))
        for cp in cps:
            cp.start()
        for cp in cps:
            cp.wait()

    out_shape = [jax.ShapeDtypeStruct((4, s.shape[1] // 2, s.shape[2]), s.dtype) for s in stacks]
    return pl.pallas_call(
        body, in_specs=[HBM_SPEC] * n, out_specs=[HBM_SPEC] * n, out_shape=out_shape,
        scratch_shapes=[pltpu.SemaphoreType.DMA((n,)), pltpu.SemaphoreType.DMA((n,))],
        name="grad_sibling_exchange")(*stacks)


def _chip_exchange(parts):
    n = len(parts)

    def body(*refs):
        ins, outs = refs[:n], refs[n:2 * n]
        send_sems, recv_sems = refs[2 * n:]
        x, y, c = _mesh_pos()
        me, s = (x, y, c), 2 * x + y
        chips = _other_chips(x, y)
        sent = [_remote(ins[i].at[2 * cx + cy], outs[i].at[s], send_sems, recv_sems, 3 * i + j, (cx, cy, c))
                for i in range(n) for j, (cx, cy) in enumerate(chips)]
        for cp in sent:
            cp.start()
        for i in range(n):
            for j, (cx, cy) in enumerate(chips):
                slot = outs[i].at[2 * cx + cy]
                _remote(slot, slot, send_sems, recv_sems, 3 * i + j, me).wait_recv()
        for cp in sent:
            cp.wait_send()

    return pl.pallas_call(
        body, in_specs=[HBM_SPEC] * n, out_specs=[HBM_SPEC] * n,
        out_shape=[jax.ShapeDtypeStruct(p.shape, p.dtype) for p in parts],
        scratch_shapes=[pltpu.SemaphoreType.DMA((3 * n,)), pltpu.SemaphoreType.DMA((3 * n,))],
        name="grad_chip_exchange")(*parts)


def _sibling_swap(halves):
    n = len(halves)

    def body(*refs):
        ins, outs = refs[:n], refs[n:2 * n]
        send_sems, recv_sems = refs[2 * n:]
        x, y, c = _mesh_pos()
        cps = [_remote(ins[i], outs[i], send_sems, recv_sems, i, (x, y, 1 - c)) for i in range(n)]
        for cp in cps:
            cp.start()
        for cp in cps:
            cp.wait()

    return pl.pallas_call(
        body, in_specs=[HBM_SPEC] * n, out_specs=[HBM_SPEC] * n,
        out_shape=[jax.ShapeDtypeStruct(h.shape, h.dtype) for h in halves],
        scratch_shapes=[pltpu.SemaphoreType.DMA((n,)), pltpu.SemaphoreType.DMA((n,))],
        name="grad_sibling_swap")(*halves)


def _gather_small(vec):
    def body(in_ref, out_ref, send_sems, recv_sems, local_sem):
        x, y, c = _mesh_pos()
        me = (x, y, c)
        own = pltpu.make_async_copy(in_ref, out_ref.at[4 * x + 2 * y + c], local_sem)
        own.start()
        peers = [(1 - x if k & 4 else x, 1 - y if k & 2 else y, 1 - c if k & 1 else c) for k in range(1, 8)]
        sent = [_remote(in_ref, out_ref.at[4 * x + 2 * y + c], send_sems, recv_sems, k, p) for k, p in enumerate(peers)]
        for cp in sent:
            cp.start()
        for k, (px, py, pc) in enumerate(peers):
            slot = out_ref.at[4 * px + 2 * py + pc]
            _remote(slot, slot, send_sems, recv_sems, k, me).wait_recv()
        for cp in sent:
            cp.wait_send()
        own.wait()

    return pl.pallas_call(
        body, in_specs=[HBM_SPEC], out_specs=HBM_SPEC, out_shape=jax.ShapeDtypeStruct((8,) + vec.shape, vec.dtype),
        scratch_shapes=[pltpu.SemaphoreType.DMA((7,)), pltpu.SemaphoreType.DMA((7,)), pltpu.SemaphoreType.DMA],
        name="grad_gather_small")(vec)


def _reduce_matrices(stacks, names):
    cidx = lax.axis_index("c").astype(jnp.int32).reshape(1)
    chip = (2 * lax.axis_index("x") + lax.axis_index("y")).astype(jnp.int32).reshape(1)
    got = _sibling_exchange(stacks)
    pairs = [_pair_sum(a, b, cidx, "grad_pair_sum_" + nm) for a, b, nm in zip(stacks, got, names)]
    quads = _chip_exchange([p[1] for p in pairs])
    mine = [_chip_sum(q, p[0], chip, "grad_chip_sum_" + nm) for q, p, nm in zip(quads, pairs, names)]
    return mine, _sibling_swap(mine)


def _pad_cols(a, n):
    return jnp.concatenate([a, jnp.zeros((a.shape[0], n - a.shape[1]), a.dtype)], axis=1)


def _group_channels(a):
    lead = a.shape[:-1]
    xs = a[..., :D_INNER].reshape(lead + (SSM_GROUPS, GROUP_W))
    bs = a[..., D_INNER:D_INNER + SSM_GROUPS * STATE_N].reshape(lead + (SSM_GROUPS, STATE_N))
    cs = a[..., D_INNER + SSM_GROUPS * STATE_N:].reshape(lead + (SSM_GROUPS, STATE_N))
    return jnp.concatenate([xs, bs, cs], axis=-1).reshape(lead + (CONV_CH,))


PROJ_SEGS = (('gate_a', D_MODEL), ('gate_b', D_MODEL), ('z', D_INNER), ('xbc', CONV_CH), ('q_lat', Q_RANK),
             ('kv_lat', KV_RANK), ('k_rope', LANE), ('dt', LANE))
PROJ_WIDE = sum(w for _, w in PROJ_SEGS[:4])
PROJ_LANE0 = {n: (v if v < PROJ_WIDE else v - PROJ_WIDE) for n, v in
              zip([n for n, _ in PROJ_SEGS], [int(v) for v in np.cumsum([0] + [w for _, w in PROJ_SEGS])[:-1]])}
CONV_LANE0 = PROJ_LANE0['xbc']
KR_LANE0 = PROJ_LANE0['k_rope']


def _lay_w_in(w):
    idx = np.cumsum(IN_SIZES)[:-1]
    q_lat, kv_lat, k_rope, z, xbc, dt, gate_a, gate_b = jnp.split(w, [int(v) for v in idx], axis=1)
    return jnp.concatenate([gate_a, gate_b, z, _group_channels(xbc), q_lat, kv_lat, _pad_cols(k_rope, LANE),
                            _pad_cols(dt, LANE)], axis=1)


@jax.custom_vjp
def project(h, w, tok):
    return _project_impl(h, w)


def _project_impl(h, w):
    return (_mm(h, w[:, :PROJ_WIDE], "w_in_fwd", BF16), _mm(h, w[:, PROJ_WIDE:], "w_in_narrow_fwd")) + tuple(
        jnp.zeros((h.shape[0], wd), BF16) for _, wd in PROJ_SEGS)


def _project_fwd(h, w, tok):
    return _project_impl(h, w), (h, w)


def _project_bwd(res, cots):
    h, w = res
    g = jnp.concatenate(cots[2:], axis=1)
    return _mm(g, w.T, "w_in_dx", h.dtype), jnp.zeros_like(w), _mm(h.T, g, "w_in_dw")


project.defvjp(_project_fwd, _project_bwd)


def _lay_w_uq(w):
    w3 = w.reshape(Q_RANK, N_HEADS, NOPE + ROPE)
    w3 = jnp.concatenate([w3, jnp.zeros((Q_RANK, N_HEADS, QK_PAD - NOPE - ROPE), w.dtype)], axis=2)
    return w3.reshape(Q_RANK, N_HEADS * QK_PAD)


def _lay_w_ukv(w):
    w3 = w.reshape(KV_RANK, N_HEADS, NOPE + V_DIM)
    return jnp.concatenate([w3[:, :, :NOPE].reshape(KV_RANK, -1), w3[:, :, NOPE:].reshape(KV_RANK, -1)], axis=1)


def _pad_lanes(v, n=LANE):
    return jnp.concatenate([v, jnp.zeros((v.shape[0], n - v.shape[1]), v.dtype)], axis=1)


def _local_loss(toks, small, x, wb, c8, posf, target):
    B, S, D = x.shape
    T = B * S

    def lin(name, a, key, lay=lambda w: w, out_dtype=F32):
        return make_linear(name, out_dtype)(a, lay(wb[key]), lay(toks[key]))

    rows2 = lambda a: a.reshape(T, a.shape[-1])
    rows3 = lambda a: a.reshape(B, S, a.shape[-1])

    sc = make_rowwise("silu_c", _f_silu, 1, 0, 0, ('row',))((c8[None],), (), ())[0][0]
    mod = make_linear("ada", F32, 4)(sc, wb['w_ada'], toks['w_ada'])[:B] + small['b_ada']
    shift1, scale1, gate1, shift2, scale2, gate2 = [m[:, None, :] for m in jnp.split(mod, 6, axis=-1)]

    h, x_res = make_rowwise("modulate1", _f_modulate, 1, 2, 1, ('row',), forward_row=0, ts_cap=1024)(
        (x,), (scale1, shift1), (small['g_pre_mix'],))
    outs = project(rows2(h), _lay_w_in(wb['w_in']), _lay_w_in(toks['w_in']))
    wide = lax.stop_gradient(rows3(outs[0]))
    proj = lax.stop_gradient(rows3(outs[1]))
    stand = {n: rows3(o) for (n, _), o in zip(PROJ_SEGS, outs[2:])}

    def win(seg, block):
        return (PROJ_LANE0[seg] // block, dict(PROJ_SEGS)[seg])

    inv = ROPE_THETA ** (-jnp.arange(ROPE // 2, dtype=F32) / (ROPE // 2))
    inv_lane = jnp.concatenate([inv, inv, jnp.zeros((LANE - ROPE,), F32)])[None]
    tabs = tuple(_rope_tables(posf, inv_lane))
    qn = make_rowwise("rms_q", _f_rms, 1, 0, 1, ('row',), windows={0: win('q_lat', Q_RANK)})(
        (proj,), (), (small['g_q_lat'],), (stand['q_lat'],))[0]
    kvn = make_rowwise("rms_kv", _f_rms, 1, 0, 1, ('row',), windows={0: win('kv_lat', KV_RANK)})(
        (proj,), (), (small['g_kv_lat'],), (stand['kv_lat'],))[0]
    qp = rows3(lin("w_uq", rows2(qn), 'w_uq', _lay_w_uq, BF16))
    kvp = rows3(lin("w_ukv", rows2(kvn), 'w_ukv', _lay_w_ukv, BF16))
    qr = rope_q(qp, tabs)
    att = attention(qr, kvp, proj, stand['k_rope'], tabs)
    attn = rows3(lin("w_o_attn", rows2(att), 'w_o_attn', out_dtype=BF16))

    xa = conv_silu(wide, stand['xbc'], _group_channels(wb['conv_w_f32']), _group_channels(small['conv_b']))
    dt_pad, a_pad = make_rowwise("dt_softplus", _f_dt, 1, 0, 2, ('row', 'row'), windows={0: win('dt', LANE)})(
        (proj,), (), (_pad_lanes(small['dt_bias']), _pad_lanes(small['a_log'])), (stand['dt'],))
    ac_pad = chunk_cumsum(a_pad)
    acr = jnp.transpose(ac_pad[..., :SSM_HEADS], (0, 2, 1))[:, :, None, :]
    dsk = jnp.repeat(small['d_skip'], HEAD_P, axis=-1)
    y = ssd(xa, dt_pad, ac_pad, acr, dsk)
    yg = make_rowwise("gated_norm", _f_gated_norm, 2, 0, 1, ('row',), ncol=SSM_GROUPS, ts_cap=2048,
                      windows={1: win('z', GROUP_W)})((y, wide), (), (small['g_ssm_out'],), (stand['z'],))[0]
    ssm = rows3(lin("w_o_ssm", rows2(yg), 'w_o_ssm', out_dtype=BF16))

    merged = make_rowwise("merge", _f_merge, 4, 0, 0, ('row',), ts_cap=1024,
                          windows={2: win('gate_a', D_MODEL), 3: win('gate_b', D_MODEL)})(
        (attn, ssm, wide, wide), (), (), (stand['gate_a'], stand['gate_b']))[0]
    mix = rows3(lin("w_out", rows2(merged), 'w_out', out_dtype=BF16))
    x1 = make_rowwise("post_mix", _f_post, 2, 1, 1, ('row',), ts_cap=1024)(
        (x_res, mix), (gate1,), (small['g_post_mix'],))[0]

    h2, x1_res = make_rowwise("modulate2", _f_modulate, 1, 2, 1, ('row',), forward_row=0, ts_cap=1024)(
        (x1,), (scale2, shift2), (small['g_pre_mlp'],))
    ff = rows3(ffn(rows2(h2), wb['w_ff1'], toks['w_ff1'], wb['w_ff2'], toks['w_ff2']))
    lvec = make_rowwise("final_loss", _f_final_loss, 3, 1, 1, ('sum',), nodiff=(2,), ts_cap=1024)(
        (x1_res, ff, target), (gate2,), (small['g_post_mlp'],))[0]
    return jnp.sum(lvec)


MATRICES = COL_SHARDED + ROW_SHARDED
STACKED_DW = ('w_ada', 'w_ff1')


def _local_step(x, c, positions, target, wb, small):
    B = x.shape[0]
    c8 = jnp.concatenate([c, jnp.zeros((16 - B, c.shape[1]), F32)], axis=0)
    posf = positions.astype(F32)[..., None]
    toks = {k: jnp.zeros(wb[k].shape, F32) for k in MATRICES if k != 'conv_w'}
    for k in STACKED_DW:
        rows, cols = wb[k].shape
        toks[k] = jnp.zeros((4, rows, cols // 4), F32)
    conv_w = wb['conv_w_f32']

    def loss_fn(toks, small, conv_w, x):
        wbl = dict(wb)
        wbl['conv_w_f32'] = conv_w
        return _local_loss(toks, small, x, wbl, c8, posf, target)

    loss, (g_tok, g_small, g_conv, g_x) = jax.value_and_grad(loss_fn, argnums=(0, 1, 2, 3))(toks, small, conv_w, x)
    grads = dict(g_tok)
    grads.update(g_small)
    grads['conv_w'] = g_conv
    return loss, g_x, grads


def kernel(x, c, positions, w_ada, b_ada, g_pre_mix, g_post_mix, w_in, g_q_lat, g_kv_lat, w_uq, w_ukv, w_o_attn, conv_w, conv_b, dt_bias, a_log, d_skip, g_ssm_out, w_o_ssm, w_out, g_pre_mlp, g_post_mlp, w_ff1, w_ff2, loss_target, m_w_ada, m_b_ada, m_g_pre_mix, m_g_post_mix, m_w_in, m_g_q_lat, m_g_kv_lat, m_w_uq, m_w_ukv, m_w_o_attn, m_conv_w, m_conv_b, m_dt_bias, m_a_log, m_d_skip, m_g_ssm_out, m_w_o_ssm, m_w_out, m_g_pre_mlp, m_g_post_mlp, m_w_ff1, m_w_ff2, v_w_ada, v_b_ada, v_g_pre_mix, v_g_post_mix, v_w_in, v_g_q_lat, v_g_kv_lat, v_w_uq, v_w_ukv, v_w_o_attn, v_conv_w, v_conv_b, v_dt_bias, v_a_log, v_d_skip, v_g_ssm_out, v_w_o_ssm, v_w_out, v_g_pre_mlp, v_g_post_mlp, v_w_ff1, v_w_ff2):
    given = dict(locals())
    w_loc = {n: given[n] for n in WEIGHTS}
    m_loc = {n: given["m_" + n] for n in WEIGHTS}
    v_loc = {n: given["v_" + n] for n in WEIGHTS}
    mats = [n for n in WEIGHTS if n in MATRICES and n != 'conv_w']
    vecs = [n for n in WEIGHTS if n not in MATRICES]

    own = [w_loc[n][0].astype(BF16) for n in mats]
    g_mats, g_conv = _gather_weights(own, conv_w[0])
    chip = 2 * lax.axis_index("x") + lax.axis_index("y")
    wb = {}
    for n, g, mine in zip(mats, g_mats, own):
        g = lax.dynamic_update_slice_in_dim(g, mine[None], chip, axis=0)
        if n in COL_SHARDED:
            wb[n] = jnp.transpose(g, (1, 0, 2)).reshape(g.shape[1], -1)
        else:
            wb[n] = g.reshape(-1, g.shape[2])
    wb['conv_w_f32'] = jnp.transpose(g_conv, (1, 0, 2)).reshape(CONV_K, -1)
    small = {n: w_loc[n] for n in vecs}

    loss_part, grad_x, grads = _local_step(x, c, positions, loss_target, wb, small)
    loss = lax.psum(loss_part, ("x", "y", "c"))

    stacks = []
    for n in mats:
        kk, nn = w_loc[n].shape[1:]
        if n in STACKED_DW:
            stacks.append(grads[n])
        elif n in COL_SHARDED:
            stacks.append(jnp.transpose(grads[n].reshape(kk, 4, nn), (1, 0, 2)))
        else:
            stacks.append(grads[n].reshape(4, kk, nn))
    g_mine, g_other = _reduce_matrices(stacks, mats)
    g_shard = {}

    vec_shapes = [tuple(grads[n].shape) for n in vecs] + [tuple(grads['conv_w'].shape)]
    total = _stack_sum(_gather_small(_pack_small([grads[n] for n in vecs] + [grads['conv_w']])), "grad_sum_small")
    g_vec = _unpack_small(total, vec_shapes)
    n_conv = conv_w.shape[2]
    chip = 2 * lax.axis_index("x") + lax.axis_index("y")
    g_shard['conv_w'] = lax.dynamic_slice_in_dim(g_vec[-1], chip * n_conv, n_conv, axis=1)
    for n, g in zip(vecs, g_vec):
        g_shard[n] = g

    delta, new_m, new_v = {}, {}, {}
    cidx = lax.axis_index("c").astype(jnp.int32).reshape(1)
    for n, mine, other in zip(mats, g_mine, g_other):
        g_shard[n], delta[n], new_m[n], new_v[n] = _adam_halves_call(
            w_loc[n], mine, other, cidx, m_loc[n], v_loc[n], "adamw_" + n)
    rest = vecs + ['conv_w']
    rest_shapes = [tuple(w_loc[n].shape) for n in rest]
    packed = [_pack_small([src[n] for n in rest]) for src in (w_loc, g_shard, m_loc, v_loc)]
    for dst, buf in zip((delta, new_m, new_v), _adam_call(*packed, "adamw_small")):
        dst.update(zip(rest, _unpack_small(buf, rest_shapes)))

    def out(d):
        return [d[n].reshape(w_loc[n].shape) for n in WEIGHTS]

    return (loss, grad_x, *out(g_shard), *out(delta), *out(new_m), *out(new_v))
```

```python
import functools
import math

import numpy as np
import jax
import jax.numpy as jnp
from jax import lax
from jax.experimental import pallas as pl
from jax.experimental.pallas import tpu as pltpu

F32 = jnp.float32
BF16 = jnp.bfloat16
MESH = pl.DeviceIdType.MESH

D_MODEL = 1024
N_HEADS = 8
NOPE = 128
ROPE = 64
V_DIM = 128
Q_RANK = 256
KV_RANK = 256
ROPE_THETA = 10000.0
D_INNER = 2048
SSM_HEADS = 32
SSM_GROUPS = 8
HEAD_P = 64
STATE_N = 128
CONV_K = 4
CHUNK = 128
CONV_CH = D_INNER + 2 * SSM_GROUPS * STATE_N
D_FF = 4096
EPS = 1e-6
IN_SIZES = (Q_RANK, KV_RANK, ROPE, D_INNER, CONV_CH, SSM_HEADS, D_MODEL, D_MODEL)
ADAM_LR, ADAM_B1, ADAM_B2, ADAM_EPS, ADAM_WD, ADAM_STEP = 0.001, 0.9, 0.999, 1e-08, 0.01, 10

VMEM_LIMIT_BYTES = 52 * 1024 * 1024
LANE = 128
QK_PAD = 256

WEIGHTS = ['w_ada', 'b_ada', 'g_pre_mix', 'g_post_mix', 'w_in', 'g_q_lat', 'g_kv_lat', 'w_uq', 'w_ukv',
           'w_o_attn', 'conv_w', 'conv_b', 'dt_bias', 'a_log', 'd_skip', 'g_ssm_out', 'w_o_ssm', 'w_out',
           'g_pre_mlp', 'g_post_mlp', 'w_ff1', 'w_ff2']
COL_SHARDED = ('w_ada', 'w_in', 'w_uq', 'w_ukv', 'conv_w', 'w_ff1')
ROW_SHARDED = ('w_o_attn', 'w_o_ssm', 'w_out', 'w_ff2')


def _cparams(sem):
    return pltpu.CompilerParams(dimension_semantics=sem, vmem_limit_bytes=VMEM_LIMIT_BYTES)


def _tile(n, cap):
    if n <= cap:
        return n
    k = n // LANE
    best = LANE
    for d in range(1, k + 1):
        if k % d == 0 and d * LANE <= cap:
            best = d * LANE
    return best


def _mm(a, w, name, out_dtype=F32, epilogue=None, extras=(), out_dtypes=None):
    M, K = a.shape
    N = w.shape[1]
    tm = min(M, 1024)
    tn = _tile(N, 1024)
    tk = _tile(K, 2048)
    nk = K // tk
    dts = tuple(out_dtypes) if epilogue is not None else (out_dtype,)
    n_x, n_o = len(extras), len(dts)

    def finish(acc, refs):
        res = epilogue(acc, *[r[...] for r in refs[:n_x]]) if epilogue is not None else (acc,)
        for o_ref, val, dt in zip(refs[n_x:n_x + n_o], res, dts):
            o_ref[...] = val.astype(dt)

    def body(a_ref, w_ref, *refs):
        part = jnp.dot(a_ref[...].astype(BF16), w_ref[...], preferred_element_type=F32)
        if nk == 1:
            finish(part, refs)
        else:
            acc_ref = refs[-1]
            k = pl.program_id(2)

            @pl.when(k == 0)
            def _():
                acc_ref[...] = part

            @pl.when(k > 0)
            def _():
                acc_ref[...] += part

            @pl.when(k == nk - 1)
            def _():
                finish(acc_ref[...], refs)

    ospec = pl.BlockSpec((tm, tn), lambda i, j, k: (i, j))
    res = pl.pallas_call(
        body, grid=(M // tm, N // tn, nk),
        in_specs=[pl.BlockSpec((tm, tk), lambda i, j, k: (i, k)), pl.BlockSpec((tk, tn), lambda i, j, k: (k, j))]
        + [ospec] * n_x,
        out_specs=[ospec] * n_o, out_shape=[jax.ShapeDtypeStruct((M, N), dt) for dt in dts],
        scratch_shapes=[pltpu.VMEM((tm, tn), F32)] if nk > 1 else [], name=name,
        compiler_params=_cparams(("parallel", "parallel", "arbitrary")))(a, w, *extras)
    return res if epilogue is not None else res[0]


def _mm_tn(a, g, name, col_shards=1):
    M, K = a.shape
    N = g.shape[1]
    tm = min(M, 1024)
    tk = _tile(K, 1024)
    tn = _tile(N // col_shards, 1024)
    nm = M // tm
    per = N // col_shards // tn

    def body(a_ref, g_ref, o_ref):
        part = lax.dot_general(a_ref[...].astype(BF16), g_ref[...].astype(BF16), (((0,), (0,)), ((), ())),
                               preferred_element_type=F32)
        m = pl.program_id(2)

        @pl.when(m == 0)
        def _():
            o_ref[...] = part.reshape(o_ref.shape)

        @pl.when(m > 0)
        def _():
            o_ref[...] += part.reshape(o_ref.shape)

    if col_shards == 1:
        out_spec = pl.BlockSpec((tk, tn), lambda i, j, m: (i, j))
        out_shape = jax.ShapeDtypeStruct((K, N), F32)
    else:
        out_spec = pl.BlockSpec((1, tk, tn), lambda i, j, m: (j // per, i, j % per))
        out_shape = jax.ShapeDtypeStruct((col_shards, K, N // col_shards), F32)
    return pl.pallas_call(
        body, grid=(K // tk, N // tn, nm),
        in_specs=[pl.BlockSpec((tm, tk), lambda i, j, m: (m, i)), pl.BlockSpec((tm, tn), lambda i, j, m: (m, j))],
        out_specs=out_spec, out_shape=out_shape, name=name,
        compiler_params=_cparams(("parallel", "parallel", "arbitrary")))(a, g)


def make_linear(name, out_dtype=F32, dw_col_shards=1):
    @jax.custom_vjp
    def linear(a, w, tok):
        return _mm(a, w, name + "_fwd", out_dtype)

    def fwd(a, w, tok):
        return _mm(a, w, name + "_fwd", out_dtype), (a, w)

    def bwd(res, g):
        a, w = res
        da = _mm(g, w.T, name + "_dx", a.dtype)
        dw = _mm_tn(a, g, name + "_dw", dw_col_shards)
        return da, jnp.zeros_like(w), dw

    linear.defvjp(fwd, bwd)
    return linear


def _relu2_epilogue(acc):
    r = jnp.maximum(acc, 0.0)
    return r * r, r


def _relu2_bwd_epilogue(acc, r):
    return (acc * (2.0 * r.astype(F32)),)


@jax.custom_vjp
def ffn(h, w1, tok1, w2, tok2):
    act, _ = _mm(h, w1, "w_ff1_fwd", epilogue=_relu2_epilogue, out_dtypes=(BF16, BF16))
    return _mm(act, w2, "w_ff2_fwd", BF16)


def _ffn_fwd(h, w1, tok1, w2, tok2):
    act, r = _mm(h, w1, "w_ff1_fwd", epilogue=_relu2_epilogue, out_dtypes=(BF16, BF16))
    return _mm(act, w2, "w_ff2_fwd", BF16), (h, w1, w2, act, r)


def _ffn_bwd(res, g):
    h, w1, w2, act, r = res
    du = _mm(g, w2.T, "w_ff2_dx", epilogue=_relu2_bwd_epilogue, extras=(r,), out_dtypes=(BF16,))[0]
    dw2 = _mm_tn(act, g, "w_ff2_dw")
    dw1 = _mm_tn(h, du, "w_ff1_dw", 4)
    dh = _mm(du, w1.T, "w_ff1_dx", h.dtype)
    return dh, jnp.zeros_like(w1), dw1, jnp.zeros_like(w2), dw2


ffn.defvjp(_ffn_fwd, _ffn_bwd)


def make_rowwise(name, f, n_rows, n_seqs, n_pars, out_kinds, ncol=1, nodiff=(), ts_cap=512, windows=None,
                 forward_row=None):
    windows = dict(windows or {})
    n_in = n_rows + n_seqs + n_pars
    diff_idx = [i for i in range(n_in) if i not in nodiff]

    def _dims(rows):
        B, S = rows[0].shape[0], rows[0].shape[1]
        ts = min(S, ts_cap)
        return B, S, ts

    def _width(i, r):
        return windows[i][1] if i in windows else r.shape[2]

    def _in_specs(rows, seqs, pars, ts):
        specs = []
        for i, r in enumerate(rows):
            col0 = windows[i][0] if i in windows else 0
            specs.append(pl.BlockSpec((1, ts, _width(i, r) // ncol), lambda k, b, s, col0=col0: (b, s, k + col0)))
        for q in seqs:
            specs.append(pl.BlockSpec((1, 1, q.shape[2] // ncol), lambda k, b, s: (b, 0, k)))
        for p in pars:
            specs.append(pl.BlockSpec((1, p.shape[1] // ncol), lambda k, b, s: (0, k)))
        return specs

    def _load(refs):
        vals = [r[0] for r in refs[:n_rows + n_seqs]]
        vals += [r[...] for r in refs[n_rows + n_seqs:n_in]]
        return vals

    def _out_struct(rows, seqs, pars, ts):
        blocks = [jax.ShapeDtypeStruct((ts, _width(i, r) // ncol), r.dtype) for i, r in enumerate(rows)]
        blocks += [jax.ShapeDtypeStruct((1, q.shape[2] // ncol), q.dtype) for q in seqs]
        blocks += [jax.ShapeDtypeStruct((1, p.shape[1] // ncol), p.dtype) for p in pars]
        return jax.eval_shape(f, *blocks)

    def _fwd_call(rows, seqs, pars):
        B, S, ts = _dims(rows)
        outs = _out_struct(rows, seqs, pars, ts)
        n_out = len(outs)

        def body(*refs):
            res = f(*_load(refs))
            first = (pl.program_id(1) == 0) & (pl.program_id(2) == 0)
            for o_ref, val, kind in zip(refs[n_in:], res, out_kinds):
                if kind == 'row':
                    o_ref[0] = val
                else:
                    tot = jnp.sum(val, axis=0, keepdims=True)

                    @pl.when(first)
                    def _(o_ref=o_ref, tot=tot):
                        o_ref[...] = tot

                    @pl.when(jnp.logical_not(first))
                    def _(o_ref=o_ref, tot=tot):
                        o_ref[...] += tot

        out_shape, out_specs = [], []
        for o, kind in zip(outs, out_kinds):
            d = o.shape[1]
            if kind == 'row':
                out_shape.append(jax.ShapeDtypeStruct((B, S, ncol * d), o.dtype))
                out_specs.append(pl.BlockSpec((1, ts, d), lambda k, b, s: (b, s, k)))
            else:
                out_shape.append(jax.ShapeDtypeStruct((1, ncol * d), o.dtype))
                out_specs.append(pl.BlockSpec((1, d), lambda k, b, s: (0, k)))
        res = pl.pallas_call(
            body, grid=(ncol, B, S // ts), in_specs=_in_specs(rows, seqs, pars, ts), out_specs=out_specs,
            out_shape=out_shape, name=name + "_fwd",
            compiler_params=_cparams(("arbitrary", "arbitrary", "arbitrary")))(*rows, *seqs, *pars)
        return tuple(res)

    def _bwd_call(rows, seqs, pars, cots, carried=None):
        B, S, ts = _dims(rows)
        outs = _out_struct(rows, seqs, pars, ts)
        n_out = len(outs)
        all_in = list(rows) + list(seqs) + list(pars)
        extra = [] if carried is None else [carried]

        def body(*refs):
            vals = _load(refs)
            if carried is not None:
                carried_ref, refs = refs[n_in + n_out], refs[:n_in + n_out] + refs[n_in + n_out + 1:]
            cts = []
            for c_ref, o, kind in zip(refs[n_in:n_in + n_out], outs, out_kinds):
                if kind == 'row':
                    cts.append(c_ref[0])
                else:
                    cts.append(jnp.broadcast_to(c_ref[...], o.shape))

            def g(*dv):
                full = list(vals)
                for i, v in zip(diff_idx, dv):
                    full[i] = v
                return tuple(f(*full))

            _, vjp = jax.vjp(g, *[vals[i] for i in diff_idx])
            grads = vjp(tuple(cts))
            b, s = pl.program_id(1), pl.program_id(2)
            for o_ref, i, gr in zip(refs[n_in + n_out:], diff_idx, grads):
                if i < n_rows:
                    if carried is not None and i == forward_row:
                        gr = gr + carried_ref[0]
                    o_ref[0] = gr.astype(o_ref.dtype)
                else:
                    first = (s == 0) if i < n_rows + n_seqs else ((b == 0) & (s == 0))
                    target = (lambda r: r.at[0]) if i < n_rows + n_seqs else (lambda r: r)

                    @pl.when(first)
                    def _(o_ref=o_ref, gr=gr, target=target):
                        target(o_ref)[...] = gr

                    @pl.when(jnp.logical_not(first))
                    def _(o_ref=o_ref, gr=gr, target=target):
                        target(o_ref)[...] += gr

        cot_specs = []
        for o, kind in zip(outs, out_kinds):
            d = o.shape[1]
            if kind == 'row':
                cot_specs.append(pl.BlockSpec((1, ts, d), lambda k, b, s: (b, s, k)))
            else:
                cot_specs.append(pl.BlockSpec((1, d), lambda k, b, s: (0, k)))
        out_shape, out_specs = [], []
        for i in diff_idx:
            a = all_in[i]
            if i < n_rows:
                out_shape.append(jax.ShapeDtypeStruct((B, S, _width(i, a)), BF16 if i in windows else a.dtype))
                out_specs.append(pl.BlockSpec((1, ts, _width(i, a) // ncol), lambda k, b, s: (b, s, k)))
                continue
            out_shape.append(jax.ShapeDtypeStruct(a.shape, a.dtype))
            if i < n_rows + n_seqs:
                out_specs.append(pl.BlockSpec((1, 1, a.shape[2] // ncol), lambda k, b, s: (b, 0, k)))
            else:
                out_specs.append(pl.BlockSpec((1, a.shape[1] // ncol), lambda k, b, s: (0, k)))
        if carried is not None:
            cot_specs.append(pl.BlockSpec((1, ts, carried.shape[2] // ncol), lambda k, b, s: (b, s, k)))
        res = pl.pallas_call(
            body, grid=(ncol, B, S // ts), in_specs=_in_specs(rows, seqs, pars, ts) + cot_specs,
            out_specs=out_specs, out_shape=out_shape, name=name + "_bwd",
            compiler_params=_cparams(("arbitrary", "arbitrary", "arbitrary")))(*all_in, *cots, *extra)
        grads = [None] * n_in
        for i, r in zip(diff_idx, res):
            grads[i] = r
        for i in nodiff:
            grads[i] = jnp.zeros_like(all_in[i])
        stand_in_grads = tuple(grads[i] for i in sorted(windows))
        for i in windows:
            grads[i] = jnp.zeros_like(all_in[i])
        return (tuple(grads[:n_rows]), tuple(grads[n_rows:n_rows + n_seqs]), tuple(grads[n_rows + n_seqs:]),
                stand_in_grads)

    def _outputs(rows, seqs, pars):
        res = _fwd_call(rows, seqs, pars)
        return res if forward_row is None else res + (rows[forward_row],)

    @jax.custom_vjp
    def op(rows, seqs, pars, stand_ins):
        return _outputs(rows, seqs, pars)

    def fwd(rows, seqs, pars, stand_ins):
        return _outputs(rows, seqs, pars), (rows, seqs, pars)

    def bwd(res, cots):
        rows, seqs, pars = res
        if forward_row is None:
            return _bwd_call(rows, seqs, pars, cots)
        return _bwd_call(rows, seqs, pars, cots[:-1], cots[-1])

    op.defvjp(fwd, bwd)
    return lambda rows, seqs, pars, stand_ins=(): op(tuple(rows), tuple(seqs), tuple(pars), tuple(stand_ins))


def _rms(x, g):
    x = x.astype(F32)
    return x * lax.rsqrt(jnp.mean(x * x, axis=-1, keepdims=True) + EPS) * g


def _silu(x):
    return x * lax.logistic(x)


def _f_silu(c):
    return (_silu(c),)


def _f_modulate(x, scale, shift, g):
    return ((_rms(x, g) * (1.0 + scale) + shift).astype(BF16),)


def _f_rms(x, g):
    return (_rms(x, g).astype(BF16),)


def _f_dt(dt_raw, dt_bias, a_log):
    z = dt_raw + dt_bias
    dt = jnp.maximum(z, 0.0) + jnp.log1p(jnp.exp(-jnp.abs(z)))
    return dt, dt * (-jnp.exp(a_log))


def _f_gated_norm(y, z, g):
    return (_rms(y * _silu(z.astype(F32)), g).astype(BF16),)


def _f_merge(attn, ssm, ga, gb):
    return ((lax.logistic(ga.astype(F32)) * attn + lax.logistic(gb.astype(F32)) * ssm).astype(BF16),)


def _f_post(x, m, gate, g):
    return (x + gate * _rms(m, g),)


def _f_final_loss(x, ff, target, gate, g):
    e = x + gate * _rms(ff, g) - target
    return (e * e * (0.5 / D_MODEL),)


def _rope_tables(posf, inv_lane):
    B, S, _ = posf.shape
    ts = min(S, 512)

    def body(p_ref, inv_ref, c_ref, a_ref, b_ref):
        ang = p_ref[0] * inv_ref[...]
        cs, sn = jnp.cos(ang), jnp.sin(ang)
        lane = lax.broadcasted_iota(jnp.int32, ang.shape, 1)
        c_ref[0] = jnp.where(lane < ROPE, cs, 0.0)
        a_ref[0] = jnp.where(lane < ROPE // 2, -sn, 0.0)
        b_ref[0] = jnp.where((lane >= ROPE // 2) & (lane < ROPE), sn, 0.0)

    spec = pl.BlockSpec((1, ts, LANE), lambda b, s: (b, s, 0))
    sds = jax.ShapeDtypeStruct((B, S, LANE), F32)
    return pl.pallas_call(
        body, grid=(B, S // ts),
        in_specs=[pl.BlockSpec((1, ts, 1), lambda b, s: (b, s, 0)), pl.BlockSpec((1, LANE), lambda b, s: (0, 0))],
        out_specs=[spec, spec, spec], out_shape=[sds, sds, sds], name="rope_tables",
        compiler_params=_cparams(("parallel", "parallel")))(posf, inv_lane)


def _rot(u, c, a, bm):
    return u * c + pltpu.roll(u, 96, 1) * a + pltpu.roll(u, 32, 1) * bm


def _rot_t(g, c, a, bm):
    return g * c + pltpu.roll(g * a, 32, 1) + pltpu.roll(g * bm, 96, 1)


def _rope_q_call(q, tabs, transpose, name):
    B, S, W = q.shape
    ts = min(S, 1024)
    fn = _rot_t if transpose else _rot
    out_dtype = BF16

    def body(q_ref, c_ref, a_ref, b_ref, o_ref):
        tc, ta, tb = c_ref[0], a_ref[0], b_ref[0]
        for h in range(W // QK_PAD):
            u = q_ref[0, :, h * QK_PAD:(h + 1) * QK_PAD].astype(F32) * ATT_SCALE
            r = fn(u[:, NOPE:], tc, ta, tb)
            o_ref[0, :, h * QK_PAD:(h + 1) * QK_PAD] = jnp.concatenate([u[:, :NOPE], r], axis=1).astype(out_dtype)

    tspec = pl.BlockSpec((1, ts, LANE), lambda b, s: (b, s, 0))
    qspec = pl.BlockSpec((1, ts, W), lambda b, s: (b, s, 0))
    return pl.pallas_call(
        body, grid=(B, S // ts), in_specs=[qspec, tspec, tspec, tspec], out_specs=qspec,
        out_shape=jax.ShapeDtypeStruct(q.shape, out_dtype), name=name,
        compiler_params=_cparams(("parallel", "parallel")))(q, *tabs)


@jax.custom_vjp
def rope_q(q, tabs):
    return _rope_q_call(q, tabs, False, "rope_q_fwd")


def _rope_q_fwd(q, tabs):
    return _rope_q_call(q, tabs, False, "rope_q_fwd"), tabs


def _rope_q_bwd(tabs, g):
    return _rope_q_call(g, tabs, True, "rope_q_bwd"), tuple(jnp.zeros_like(t) for t in tabs)


rope_q.defvjp(_rope_q_fwd, _rope_q_bwd)


def _build_k_fwd_call(kv, kr, tabs):
    B, S, _ = kv.shape
    ts = min(S, 1024)

    def body(kv_ref, kr_ref, c_ref, a_ref, b_ref, o_ref):
        r = _rot(kr_ref[0], c_ref[0], a_ref[0], b_ref[0]).astype(BF16)
        for h in range(N_HEADS):
            o_ref[0, :, h * QK_PAD:(h + 1) * QK_PAD] = jnp.concatenate(
                [kv_ref[0, :, h * NOPE:(h + 1) * NOPE], r], axis=1)

    tspec = pl.BlockSpec((1, ts, LANE), lambda b, s: (b, s, 0))
    kr_spec = pl.BlockSpec((1, ts, LANE), lambda b, s: (b, s, KR_LANE0 // LANE))
    return pl.pallas_call(
        body, grid=(B, S // ts),
        in_specs=[pl.BlockSpec((1, ts, N_HEADS * NOPE), lambda b, s: (b, s, 0)), kr_spec, tspec, tspec, tspec],
        out_specs=pl.BlockSpec((1, ts, N_HEADS * QK_PAD), lambda b, s: (b, s, 0)),
        out_shape=jax.ShapeDtypeStruct((B, S, N_HEADS * QK_PAD), BF16), name="build_k_fwd",
        compiler_params=_cparams(("parallel", "parallel")))(kv, kr, *tabs)


def _build_k_bwd_call(g, tabs):
    B, S, _ = g.shape
    ts = min(S, 1024)

    def body(g_ref, c_ref, a_ref, b_ref, dk_ref, dr_ref):
        tot = None
        for h in range(N_HEADS):
            dk_ref[0, :, h * NOPE:(h + 1) * NOPE] = g_ref[0, :, h * QK_PAD:h * QK_PAD + NOPE]
            part = g_ref[0, :, h * QK_PAD + NOPE:(h + 1) * QK_PAD].astype(F32)
            tot = part if tot is None else tot + part
        dr_ref[0] = _rot_t(tot, c_ref[0], a_ref[0], b_ref[0]).astype(BF16)

    tspec = pl.BlockSpec((1, ts, LANE), lambda b, s: (b, s, 0))
    return pl.pallas_call(
        body, grid=(B, S // ts),
        in_specs=[pl.BlockSpec((1, ts, N_HEADS * QK_PAD), lambda b, s: (b, s, 0)), tspec, tspec, tspec],
        out_specs=[pl.BlockSpec((1, ts, N_HEADS * NOPE), lambda b, s: (b, s, 0)), tspec],
        out_shape=[jax.ShapeDtypeStruct((B, S, N_HEADS * NOPE), BF16), jax.ShapeDtypeStruct((B, S, LANE), BF16)],
        name="build_k_bwd", compiler_params=_cparams(("parallel", "parallel")))(g, *tabs)


ATT_SCALE = (NOPE + ROPE) ** -0.5
NEG = -1e30


def _att_tiles(S):
    t = min(S, 512)
    return t, S // t


def _scores(q, k, diagonal):
    s = lax.dot_general(q, k, (((1,), (1,)), ((), ())), preferred_element_type=F32)
    if diagonal:
        row = lax.broadcasted_iota(jnp.int32, s.shape, 0)
        col = lax.broadcasted_iota(jnp.int32, s.shape, 1)
        s = jnp.where(col <= row, s, NEG)
    return s


ATT_HB = 8


def _causal_pairs(n):
    pairs = [(i, j) for i in range(n) for j in range(i + 1)]
    return (jnp.asarray([p[0] for p in pairs], jnp.int32), jnp.asarray([p[1] for p in pairs], jnp.int32))


def _head(ref_or_val, h, w):
    return ref_or_val[:, h * w:(h + 1) * w]


def _attn_fwd_call(q, k, vsrc, v_blk0):
    B, S, _ = q.shape
    t, n = _att_tiles(S)
    qi, kj = _causal_pairs(n)

    def body(qi_ref, kj_ref, q_ref, k_ref, v_ref, o_ref, lse_ref, m_sc, l_sc, acc_sc):
        p_id = pl.program_id(2)
        i, j = qi_ref[p_id], kj_ref[p_id]

        @pl.when(j == 0)
        def _():
            m_sc[...] = jnp.full(m_sc.shape, NEG, F32)
            l_sc[...] = jnp.zeros(l_sc.shape, F32)
            acc_sc[...] = jnp.zeros(acc_sc.shape, F32)

        def step(diagonal):
            qa, ka, va = q_ref[0], k_ref[0], v_ref[0]
            for h in range(ATT_HB):
                lanes = slice(h * LANE, (h + 1) * LANE)
                s = _scores(_head(qa, h, QK_PAD), _head(ka, h, QK_PAD), diagonal)
                m_prev = m_sc[:, lanes]
                m_new = jnp.maximum(m_prev, jnp.max(s, axis=1, keepdims=True))
                alpha = jnp.exp(m_prev - m_new)
                p = jnp.exp(s - jnp.tile(m_new, (1, t // LANE)))
                l_sc[:, lanes] = alpha * l_sc[:, lanes] + jnp.sum(p, axis=1, keepdims=True)
                acc_sc[:, lanes] = alpha * acc_sc[:, lanes] + jnp.dot(p.astype(BF16), _head(va, h, V_DIM),
                                                                      preferred_element_type=F32)
                m_sc[:, lanes] = m_new

        @pl.when(j < i)
        def _():
            step(False)

        @pl.when(j == i)
        def _():
            step(True)
            o_ref[0] = (acc_sc[...] / l_sc[...]).astype(BF16)
            lse_ref[0] = m_sc[...] + jnp.log(l_sc[...])

    wq, wv = ATT_HB * QK_PAD, ATT_HB * V_DIM
    grid_spec = pltpu.PrefetchScalarGridSpec(
        num_scalar_prefetch=2, grid=(B, N_HEADS // ATT_HB, qi.shape[0]),
        in_specs=[pl.BlockSpec((1, t, wq), lambda b, h, p, qi, kj: (b, qi[p], h)),
                  pl.BlockSpec((1, t, wq), lambda b, h, p, qi, kj: (b, kj[p], h)),
                  pl.BlockSpec((1, t, wv), lambda b, h, p, qi, kj: (b, kj[p], v_blk0 + h))],
        out_specs=[pl.BlockSpec((1, t, wv), lambda b, h, p, qi, kj: (b, qi[p], h)),
                   pl.BlockSpec((1, t, wv), lambda b, h, p, qi, kj: (b, qi[p], h))],
        scratch_shapes=[pltpu.VMEM((t, wv), F32), pltpu.VMEM((t, wv), F32), pltpu.VMEM((t, wv), F32)])
    return pl.pallas_call(
        body, grid_spec=grid_spec,
        out_shape=[jax.ShapeDtypeStruct((B, S, N_HEADS * V_DIM), BF16),
                   jax.ShapeDtypeStruct((B, S, N_HEADS * LANE), F32)],
        name="attn_fwd", compiler_params=_cparams(("parallel", "parallel", "arbitrary")))(qi, kj, q, k, vsrc)


def _attn_p_ds(q, k, v, o, do, lse, diagonal, t):
    s = _scores(q, k, diagonal)
    p = jnp.exp(s - jnp.tile(lse, (1, t // LANE)))
    dp = lax.dot_general(do.astype(BF16), v, (((1,), (1,)), ((), ())), preferred_element_type=F32)
    delta = jnp.sum(do.astype(F32) * o.astype(F32), axis=1, keepdims=True)
    ds = p * (dp - delta)
    return p, ds


ATT_HB_BWD = 4


def _attn_bwd_call(q, k, vsrc, o, do, lse):
    B, S, _ = q.shape
    t, n = _att_tiles(S)
    qi, kj = _causal_pairs(n)
    n_pairs = qi.shape[0]
    hb = ATT_HB_BWD
    v_blk0 = N_HEADS // hb

    def body(qi_ref, kj_ref, q_ref, k_ref, v_ref, o_ref, do_ref, lse_ref, dq_ref, dk_ref, dv_ref, dq_sc, dk_sc, dv_sc):
        p_id = pl.program_id(2)
        i, j = qi_ref[p_id], kj_ref[p_id]

        @pl.when(p_id == 0)
        def _():
            dk_sc[...] = jnp.zeros(dk_sc.shape, F32)
            dv_sc[...] = jnp.zeros(dv_sc.shape, F32)

        @pl.when(j == 0)
        def _():
            dq_sc[...] = jnp.zeros(dq_sc.shape, F32)

        rows = pl.ds(pl.multiple_of(j * t, t), t)

        def step(diagonal):
            qa, ka, va, oa, doa, la = q_ref[0], k_ref[0], v_ref[0], o_ref[0], do_ref[0], lse_ref[0]
            for h in range(hb):
                qb, kb, dob = _head(qa, h, QK_PAD), _head(ka, h, QK_PAD), _head(doa, h, V_DIM)
                p, ds = _attn_p_ds(qb, kb, _head(va, h, V_DIM), _head(oa, h, V_DIM), dob, _head(la, h, LANE),
                                   diagonal, t)
                dsb = ds.astype(BF16)
                dq_sc[:, h * QK_PAD:(h + 1) * QK_PAD] += jnp.dot(dsb, kb, preferred_element_type=F32)
                dv_sc[rows, h * V_DIM:(h + 1) * V_DIM] += lax.dot_general(
                    p.astype(BF16), dob.astype(BF16), (((0,), (0,)), ((), ())), preferred_element_type=F32)
                dk_sc[rows, h * QK_PAD:(h + 1) * QK_PAD] += lax.dot_general(
                    dsb, qb, (((0,), (0,)), ((), ())), preferred_element_type=F32)

        @pl.when(j < i)
        def _():
            step(False)

        @pl.when(j == i)
        def _():
            step(True)
            dq_ref[0] = dq_sc[...].astype(BF16)

        @pl.when(i == n - 1)
        def _():
            dk_ref[0] = dk_sc[rows, :].astype(BF16)
            dv_ref[0] = dv_sc[rows, :].astype(BF16)

    wq, wv = hb * QK_PAD, hb * V_DIM
    at_q = lambda b, h, p, qi, kj: (b, qi[p], h)
    at_k = lambda b, h, p, qi, kj: (b, kj[p], h)
    at_done = lambda b, h, p, qi, kj: (b, jnp.where(qi[p] == n - 1, kj[p], 0), h)
    grid_spec = pltpu.PrefetchScalarGridSpec(
        num_scalar_prefetch=2, grid=(B, N_HEADS // hb, n_pairs),
        in_specs=[pl.BlockSpec((1, t, wq), at_q), pl.BlockSpec((1, t, wq), at_k),
                  pl.BlockSpec((1, t, wv), lambda b, h, p, qi, kj: (b, kj[p], v_blk0 + h)),
                  pl.BlockSpec((1, t, wv), at_q), pl.BlockSpec((1, t, wv), at_q), pl.BlockSpec((1, t, wv), at_q)],
        out_specs=[pl.BlockSpec((1, t, wq), at_q), pl.BlockSpec((1, t, wq), at_done), pl.BlockSpec((1, t, wv), at_done)],
        scratch_shapes=[pltpu.VMEM((t, wq), F32), pltpu.VMEM((S, wq), F32), pltpu.VMEM((S, wv), F32)])
    return pl.pallas_call(
        body, grid_spec=grid_spec,
        out_shape=[jax.ShapeDtypeStruct((B, S, N_HEADS * QK_PAD), BF16),
                   jax.ShapeDtypeStruct((B, S, N_HEADS * QK_PAD), BF16),
                   jax.ShapeDtypeStruct((B, S, N_HEADS * V_DIM), BF16)],
        name="attn_bwd", compiler_params=_cparams(("parallel", "parallel", "arbitrary")))(
            qi, kj, q, k, vsrc, o, do, lse)


@jax.custom_vjp
def attention(q, kv, src, stand_in, tabs):
    return _attn_fwd_call(q, _build_k_fwd_call(kv, src, tabs), kv, N_HEADS // ATT_HB)[0]


def _attention_fwd(q, kv, src, stand_in, tabs):
    k = _build_k_fwd_call(kv, src, tabs)
    o, lse = _attn_fwd_call(q, k, kv, N_HEADS // ATT_HB)
    return o, (q, k, kv, o, lse, src, tabs)


def _attention_bwd(res, do):
    q, k, kv, o, lse, src, tabs = res
    dq, dk, dv = _attn_bwd_call(q, k, kv, o, do, lse)
    dk_nope, dk_rope = _build_k_bwd_call(dk, tabs)
    return (dq, jnp.concatenate([dk_nope, dv], axis=-1), jnp.zeros_like(src), dk_rope,
            tuple(jnp.zeros_like(t) for t in tabs))


attention.defvjp(_attention_fwd, _attention_bwd)


SUBLANES = 8
CONV_BLOCK = 2 * LANE


def _zero_tail(v):
    return jnp.concatenate([v, jnp.zeros((SUBLANES, v.shape[1]), v.dtype)], axis=0)


def _shift_down(vz, sh):
    return pltpu.roll(vz, sh, 0)[:vz.shape[0] - SUBLANES]


def _shift_up(vz, sh):
    return pltpu.roll(vz, vz.shape[0] - sh, 0)[:vz.shape[0] - SUBLANES]


def _conv_pre(u, uz, w_ref, b_ref):
    acc = b_ref[...] + w_ref[pl.ds(CONV_K - 1, 1), :] * u
    for k in range(CONV_K - 1):
        acc = acc + w_ref[pl.ds(k, 1), :] * _shift_down(uz, CONV_K - 1 - k)
    return acc


def _conv_fwd_call(src, w, b):
    B, S, _ = src.shape
    C = w.shape[1]

    def body(u_ref, w_ref, b_ref, o_ref):
        uu = u_ref[0].astype(F32)
        o_ref[0] = _silu(_conv_pre(uu, _zero_tail(uu), w_ref, b_ref))

    spec = pl.BlockSpec((1, S, CONV_BLOCK), lambda c, bb: (bb, 0, c))
    return pl.pallas_call(
        body, grid=(C // CONV_BLOCK, B),
        in_specs=[pl.BlockSpec((1, S, CONV_BLOCK), lambda c, bb: (bb, 0, c + CONV_LANE0 // CONV_BLOCK)),
                  pl.BlockSpec((CONV_K, CONV_BLOCK), lambda c, bb: (0, c)),
                  pl.BlockSpec((1, CONV_BLOCK), lambda c, bb: (0, c))],
        out_specs=spec, out_shape=jax.ShapeDtypeStruct((B, S, C), F32), name="conv_fwd",
        compiler_params=_cparams(("parallel", "arbitrary")))(src, w, b)


def _conv_bwd_call(src, w, b, g):
    B, S, _ = src.shape
    C = w.shape[1]

    def body(u_ref, w_ref, b_ref, g_ref, du_ref, dw_ref, db_ref):
        uu = u_ref[0].astype(F32)
        uz = _zero_tail(uu)
        pre = _conv_pre(uu, uz, w_ref, b_ref)
        sg = lax.logistic(pre)
        dpre = g_ref[0] * sg * (1.0 + pre * (1.0 - sg))
        dz = _zero_tail(dpre)
        du = w_ref[pl.ds(CONV_K - 1, 1), :] * dpre
        dws = [None] * CONV_K
        dws[CONV_K - 1] = jnp.sum(dpre * uu, axis=0, keepdims=True)
        for k in range(CONV_K - 1):
            sh = CONV_K - 1 - k
            du = du + w_ref[pl.ds(k, 1), :] * _shift_up(dz, sh)
            dws[k] = jnp.sum(dpre * _shift_down(uz, sh), axis=0, keepdims=True)
        du_ref[0] = du.astype(du_ref.dtype)
        dbv = jnp.sum(dpre, axis=0, keepdims=True)
        first = pl.program_id(1) == 0

        @pl.when(first)
        def _():
            for k in range(CONV_K):
                dw_ref[pl.ds(k, 1), :] = dws[k]
            db_ref[...] = dbv

        @pl.when(jnp.logical_not(first))
        def _():
            for k in range(CONV_K):
                dw_ref[pl.ds(k, 1), :] += dws[k]
            db_ref[...] += dbv

    spec = pl.BlockSpec((1, S, CONV_BLOCK), lambda c, bb: (bb, 0, c))
    wspec = pl.BlockSpec((CONV_K, CONV_BLOCK), lambda c, bb: (0, c))
    bspec = pl.BlockSpec((1, CONV_BLOCK), lambda c, bb: (0, c))
    uspec = pl.BlockSpec((1, S, CONV_BLOCK), lambda c, bb: (bb, 0, c + CONV_LANE0 // CONV_BLOCK))
    return pl.pallas_call(
        body, grid=(C // CONV_BLOCK, B), in_specs=[uspec, wspec, bspec, spec], out_specs=[spec, wspec, bspec],
        out_shape=[jax.ShapeDtypeStruct((B, S, C), BF16), jax.ShapeDtypeStruct(w.shape, F32),
                   jax.ShapeDtypeStruct(b.shape, F32)],
        name="conv_bwd", compiler_params=_cparams(("parallel", "arbitrary")))(src, w, b, g)


@jax.custom_vjp
def conv_silu(src, stand_in, w, b):
    return _conv_fwd_call(src, w, b)


def _conv_silu_fwd(src, stand_in, w, b):
    return _conv_fwd_call(src, w, b), (src, w, b)


def _conv_silu_bwd(res, g):
    du, dw, db = _conv_bwd_call(*res, g)
    return jnp.zeros_like(res[0]), du, dw, db


conv_silu.defvjp(_conv_silu_fwd, _conv_silu_bwd)


def _chunk_cumsum_call(a, reverse, name):
    B, S, W = a.shape
    per_step = min(S // CHUNK, 8)

    def body(a_ref, o_ref):
        r = lax.broadcasted_iota(jnp.int32, (CHUNK, CHUNK), 0)
        c = lax.broadcasted_iota(jnp.int32, (CHUNK, CHUNK), 1)
        tri = jnp.where((c >= r) if reverse else (c <= r), 1.0, 0.0).astype(F32)
        for i in range(per_step):
            rows = pl.ds(i * CHUNK, CHUNK)
            o_ref[0, rows, :] = jnp.dot(tri, a_ref[0, rows, :], preferred_element_type=F32,
                                        precision=lax.Precision.HIGHEST)

    spec = pl.BlockSpec((1, per_step * CHUNK, W), lambda b, c: (b, c, 0))
    return pl.pallas_call(body, grid=(B, S // (per_step * CHUNK)), in_specs=[spec], out_specs=spec,
                          out_shape=jax.ShapeDtypeStruct(a.shape, F32), name=name,
                          compiler_params=_cparams(("parallel", "parallel")))(a)


@jax.custom_vjp
def chunk_cumsum(a):
    return _chunk_cumsum_call(a, False, "chunk_cumsum_fwd")


chunk_cumsum.defvjp(lambda a: (_chunk_cumsum_call(a, False, "chunk_cumsum_fwd"), None),
                    lambda _, g: (_chunk_cumsum_call(g, True, "chunk_cumsum_bwd"),))


GROUP_W = 4 * HEAD_P
HPG = SSM_HEADS // SSM_GROUPS


def _ssd_masks():
    lane = lax.broadcasted_iota(jnp.int32, (1, GROUP_W), 1)
    return [((lane >= HEAD_P * j) & (lane < HEAD_P * (j + 1))).astype(F32) for j in range(HPG)]


def _ssd_decays(ac_cols, acr_ref, gi):
    r = lax.broadcasted_iota(jnp.int32, (CHUNK, CHUNK), 0)
    c = lax.broadcasted_iota(jnp.int32, (CHUNK, CHUNK), 1)
    return [jnp.exp(jnp.where(c <= r, ac_cols[j] - acr_ref[0, gi * HPG + j], NEG)) for j in range(HPG)]


def _ssd_cols(blk, g):
    lane = lax.broadcasted_iota(jnp.int32, blk.shape, 1)
    return [jnp.sum(jnp.where(lane == HPG * g + j, blk, 0.0), axis=1, keepdims=True) for j in range(HPG)]


def _ssd_spread(cols):
    lane = lax.broadcasted_iota(jnp.int32, (1, GROUP_W), 1)
    out = jnp.broadcast_to(cols[HPG - 1], (CHUNK, GROUP_W))
    for j in range(HPG - 2, -1, -1):
        out = jnp.where(lane < HEAD_P * (j + 1), cols[j], out)
    return out


def _ssd_gather(val, cols, masks, g):
    lane = lax.broadcasted_iota(jnp.int32, (1, LANE), 1)
    out = jnp.zeros((CHUNK, LANE), F32)
    for j in range(HPG):
        tot = jnp.sum(val * masks[j], axis=1, keepdims=True)
        if cols is not None:
            tot = tot + cols[j]
        out = out + tot * (lane == HPG * g + j).astype(F32)
    return out


def _dot(a, b, dims):
    return lax.dot_general(a.astype(BF16), b.astype(BF16), (dims, ((), ())), preferred_element_type=F32)


NN = ((1,), (0,))
NT = ((1,), (1,))
TN = ((0,), (0,))


XBC_W = GROUP_W + 2 * STATE_N


SSD_STEP_GROUPS_FWD = 8
SSD_STEP_GROUPS_BWD = 8


def _ssd_load(xbc_ref, dt_ref, ac_ref, masks, g, gi):
    x = xbc_ref[0, :, gi * XBC_W:gi * XBC_W + GROUP_W]
    bm = xbc_ref[0, :, gi * XBC_W + GROUP_W:gi * XBC_W + GROUP_W + STATE_N]
    cm = xbc_ref[0, :, gi * XBC_W + GROUP_W + STATE_N:(gi + 1) * XBC_W]
    ac_cols = _ssd_cols(ac_ref[0], g)
    dt = _ssd_spread(_ssd_cols(dt_ref[0], g))
    ac = _ssd_spread(ac_cols)
    is_last = (lax.broadcasted_iota(jnp.int32, (CHUNK, GROUP_W), 0) == CHUNK - 1).astype(F32)
    return x, bm, cm, dt, ac, ac_cols, is_last


def _ssd_in_specs(nc, rev, gb):
    cc = (lambda c: nc - 1 - c) if rev else (lambda c: c)
    return [pl.BlockSpec((1, CHUNK, gb * XBC_W), lambda b, g, c: (b, cc(c), g)),
            pl.BlockSpec((1, CHUNK, LANE), lambda b, g, c: (b, cc(c), 0)),
            pl.BlockSpec((1, CHUNK, LANE), lambda b, g, c: (b, cc(c), 0)),
            pl.BlockSpec((1, gb * HPG, 1, CHUNK), lambda b, g, c: (b, g, 0, cc(c))),
            pl.BlockSpec((1, gb * GROUP_W), lambda b, g, c: (0, g))]


def _ssd_fwd_call(xbc, dtp, acp, acr, dsk):
    B, S, _ = xbc.shape
    nc = S // CHUNK
    gb = SSD_STEP_GROUPS_FWD

    def body(xbc_ref, dt_ref, ac_ref, ar_ref, ds_ref, y_ref, hp_ref, h_sc):
        @pl.when(pl.program_id(2) == 0)
        def _():
            h_sc[...] = jnp.zeros(h_sc.shape, F32)

        masks = _ssd_masks()
        ys = []
        for gi in range(gb):
            grp = gb * pl.program_id(1) + gi
            x, bm, cm, dt, ac, ac_cols, is_last = _ssd_load(xbc_ref, dt_ref, ac_ref, masks, grp, gi)
            last = jnp.sum(ac * is_last, axis=0, keepdims=True)
            decays = _ssd_decays(ac_cols, ar_ref, gi)
            xd = x * dt
            cb = _dot(cm, bm, NT)
            hprev = h_sc[gi]
            hp_ref[0, gi, 0] = hprev
            y = _dot(cm, hprev, NN) * jnp.exp(ac) + ds_ref[:, gi * GROUP_W:(gi + 1) * GROUP_W] * x
            y = y + _dot(jnp.concatenate([cb * d for d in decays], axis=1),
                         jnp.concatenate([xd * m for m in masks], axis=0), NN)
            ys.append(y)
            h_sc[gi] = hprev * jnp.exp(last) + _dot(bm, xd * jnp.exp(last - ac), TN)
        y_ref[0] = jnp.concatenate(ys, axis=1)

    ng = SSM_GROUPS // gb
    return pl.pallas_call(
        body, grid=(B, ng, nc), in_specs=_ssd_in_specs(nc, False, gb),
        out_specs=[pl.BlockSpec((1, CHUNK, gb * GROUP_W), lambda b, g, c: (b, c, g)),
                   pl.BlockSpec((1, gb, 1, STATE_N, GROUP_W), lambda b, g, c: (b, g, c, 0, 0))],
        out_shape=[jax.ShapeDtypeStruct((B, S, D_INNER), F32),
                   jax.ShapeDtypeStruct((B, SSM_GROUPS, nc, STATE_N, GROUP_W), F32)],
        scratch_shapes=[pltpu.VMEM((gb, STATE_N, GROUP_W), F32)], name="ssd_fwd",
        compiler_params=_cparams(("parallel", "parallel", "arbitrary")))(xbc, dtp, acp, acr, dsk)


def _ssd_bwd_call(xbc, dtp, acp, acr, dsk, hps, dy):
    B, S, _ = xbc.shape
    nc = S // CHUNK
    gb = SSD_STEP_GROUPS_BWD

    def body(xbc_ref, dt_ref, ac_ref, ar_ref, ds_ref, hp_ref, dy_ref,
             dxbc_ref, ddt_ref, dac_ref, dar_ref, dds_ref, dh_sc):
        first = pl.program_id(2) == 0

        @pl.when(first)
        def _():
            dh_sc[...] = jnp.zeros(dh_sc.shape, F32)

        masks = _ssd_masks()
        dxbc_parts, dds_parts = [], []
        for gi in range(gb):
            grp = gb * pl.program_id(0) + gi
            x, bm, cm, dt, ac, ac_cols, is_last = _ssd_load(xbc_ref, dt_ref, ac_ref, masks, grp, gi)
            last = jnp.sum(ac * is_last, axis=0, keepdims=True)
            g = dy_ref[0, :, gi * GROUP_W:(gi + 1) * GROUP_W]
            hprev = hp_ref[0, gi, 0]
            dh = dh_sc[gi]
            decays = _ssd_decays(ac_cols, ar_ref, gi)
            dcols = []
            xd = x * dt
            cb = _dot(cm, bm, NT)
            e_c = jnp.exp(ac)
            e_end = jnp.exp(last - ac)
            e_last = jnp.exp(last)
            z = _dot(cm, hprev, NN)
            dz = g * e_c
            dac = g * z * e_c
            dc = _dot(dz, hprev, NT)
            dhprev = _dot(cm, dz, TN) + dh * e_last
            dcb = jnp.zeros((CHUNK, CHUNK), F32)
            gjs = [cb * d for d in decays]
            g_heads = jnp.concatenate([g * m for m in masks], axis=0)
            dg_heads = _dot(g_heads, xd, NT)
            dxd = _dot(jnp.concatenate(gjs, axis=0), g_heads, TN)
            for j in range(HPG):
                gj = gjs[j]
                dgj = dg_heads[j * CHUNK:(j + 1) * CHUNK]
                dcb = dcb + dgj * decays[j]
                dseg = dgj * gj
                dcols.append(jnp.sum(dseg, axis=1, keepdims=True))
                dar_ref[0, gi * HPG + j] = -jnp.sum(dseg, axis=0, keepdims=True)
            dc = dc + _dot(dcb, bm, NN)
            db = _dot(dcb, cm, TN)
            sx = xd * e_end
            db = db + _dot(sx, dh, NT)
            dsx = _dot(bm, dh, NN)
            dxd = dxd + dsx * e_end
            de = dsx * sx
            dac = dac - de
            dlast = jnp.sum(de, axis=0, keepdims=True) + jnp.sum(dh * hprev, axis=0, keepdims=True) * e_last
            dsk = ds_ref[:, gi * GROUP_W:(gi + 1) * GROUP_W]
            dxbc_parts += [dxd * dt + dsk * g, db, dc]
            ddt_ref[0, gi] = _ssd_gather(dxd * x, None, masks, grp)
            dac_ref[0, gi] = _ssd_gather(dac + is_last * dlast, dcols, masks, grp)
            dds_parts.append(jnp.sum(g * x, axis=0, keepdims=True))
            dh_sc[gi] = dhprev
        dxbc_ref[0] = jnp.concatenate(dxbc_parts, axis=1)
        dds = jnp.concatenate(dds_parts, axis=1)
        first_all = first & (pl.program_id(1) == 0)

        @pl.when(first_all)
        def _():
            dds_ref[...] = dds

        @pl.when(jnp.logical_not(first_all))
        def _():
            dds_ref[...] += dds

    rc = lambda c: nc - 1 - c
    ng = SSM_GROUPS // gb
    in_specs = [pl.BlockSpec(s.block_shape, (lambda g, b, c, f=s.index_map: f(b, g, c))) for s in _ssd_in_specs(nc, True, gb)]
    in_specs.append(pl.BlockSpec((1, gb, 1, STATE_N, GROUP_W), lambda g, b, c: (b, g, rc(c), 0, 0)))
    in_specs.append(pl.BlockSpec((1, CHUNK, gb * GROUP_W), lambda g, b, c: (b, rc(c), g)))
    per_group = pl.BlockSpec((1, gb, CHUNK, LANE), lambda g, b, c: (b, g, rc(c), 0))
    out_specs = [pl.BlockSpec((1, CHUNK, gb * XBC_W), lambda g, b, c: (b, rc(c), g)), per_group, per_group,
                 pl.BlockSpec((1, gb * HPG, 1, CHUNK), lambda g, b, c: (b, g, 0, rc(c))),
                 pl.BlockSpec((1, gb * GROUP_W), lambda g, b, c: (0, g))]
    out_shape = [jax.ShapeDtypeStruct(xbc.shape, F32),
                 jax.ShapeDtypeStruct((B, SSM_GROUPS, S, LANE), F32), jax.ShapeDtypeStruct((B, SSM_GROUPS, S, LANE), F32),
                 jax.ShapeDtypeStruct(acr.shape, F32), jax.ShapeDtypeStruct(dsk.shape, F32)]
    return pl.pallas_call(
        body, grid=(ng, B, nc), in_specs=in_specs, out_specs=out_specs, out_shape=out_shape,
        scratch_shapes=[pltpu.VMEM((gb, STATE_N, GROUP_W), F32)], name="ssd_bwd",
        compiler_params=_cparams(("arbitrary", "arbitrary", "arbitrary")))(xbc, dtp, acp, acr, dsk, hps, dy)


@jax.custom_vjp
def ssd(xbc, dtp, acp, acr, dsk):
    return _ssd_fwd_call(xbc, dtp, acp, acr, dsk)[0]


def _ssd_fwd(xbc, dtp, acp, acr, dsk):
    y, hps = _ssd_fwd_call(xbc, dtp, acp, acr, dsk)
    return y, (xbc, dtp, acp, acr, dsk, hps)


def _ssd_bwd(res, dy):
    dxbc, ddt, dac, dacr, dds = _ssd_bwd_call(*res, dy)
    return dxbc, jnp.sum(ddt, axis=1), jnp.sum(dac, axis=1), dacr, dds


ssd.defvjp(_ssd_fwd, _ssd_bwd)


def _pack_small(arrs):
    flat = jnp.concatenate([a.reshape(-1) for a in arrs])
    rows = -(-flat.shape[0] // (8 * LANE)) * 8
    return jnp.pad(flat, (0, rows * LANE - flat.shape[0])).reshape(rows, LANE)


def _unpack_small(buf, shapes):
    flat = buf.reshape(-1)
    out, off = [], 0
    for shp in shapes:
        n = int(np.prod(shp))
        out.append(flat[off:off + n].reshape(shp))
        off += n
    return out


def _rows_tile(rows, cap):
    for cand in range(min(rows, cap), 7, -8):
        if rows % cand == 0:
            return cand
    return rows


def _pair_sum(mine, theirs, cidx, name):
    n4, kk, nn = mine.shape
    half = kk // 2
    tr = _rows_tile(half, 256)
    nb = half // tr

    def body(c_ref, a_ref, b_ref, o_ref, ob_ref):
        tot = a_ref[...] + b_ref[...]
        o_ref[...] = tot
        ob_ref[...] = tot.astype(BF16)

    spec = pl.BlockSpec((1, tr, nn), lambda j, i, c: (j, i, 0))
    grid_spec = pltpu.PrefetchScalarGridSpec(
        num_scalar_prefetch=1, grid=(n4, nb),
        in_specs=[pl.BlockSpec((1, tr, nn), lambda j, i, c: (j, c[0] * nb + i, 0)), spec], out_specs=[spec, spec])
    return pl.pallas_call(
        body, grid_spec=grid_spec,
        out_shape=[jax.ShapeDtypeStruct((n4, half, nn), F32), jax.ShapeDtypeStruct((n4, half, nn), BF16)],
        name=name, compiler_params=_cparams(("parallel", "parallel")))(cidx, mine, theirs)


def _chip_sum(quad, pair, chip_idx, name):
    _, rows, nn = quad.shape
    tr = _rows_tile(rows, 256)

    def body(s_ref, q_ref, p_ref, o_ref):
        for mine in range(4):
            @pl.when(s_ref[0] == mine)
            def _(mine=mine):
                acc = None
                for d in range(4):
                    term = p_ref[0] if d == mine else q_ref[d].astype(F32)
                    acc = term if acc is None else acc + term
                o_ref[...] = acc

    grid_spec = pltpu.PrefetchScalarGridSpec(
        num_scalar_prefetch=1, grid=(rows // tr,),
        in_specs=[pl.BlockSpec((4, tr, nn), lambda i, s: (0, i, 0)), pl.BlockSpec((1, tr, nn), lambda i, s: (s[0], i, 0))],
        out_specs=pl.BlockSpec((tr, nn), lambda i, s: (i, 0)))
    return pl.pallas_call(body, grid_spec=grid_spec, out_shape=jax.ShapeDtypeStruct((rows, nn), F32), name=name,
                          compiler_params=_cparams(("parallel",)))(chip_idx, quad, pair)


def _adam_halves_call(w, mine, other, cidx, m, v, name):
    _, rows, nn = w.shape
    half = rows // 2
    tr = _rows_tile(half, 128)
    nb = half // tr

    def body(c_ref, w_ref, a_ref, b_ref, m_ref, v_ref, g_ref, d_ref, nm_ref, nv_ref):
        upper = (pl.program_id(0) >= nb).astype(jnp.int32)
        g = jnp.where(upper == c_ref[0], a_ref[...], b_ref[...])
        g_ref[0] = g
        d_ref[0], nm_ref[0], nv_ref[0] = _adam_fn(w_ref[0], g, m_ref[0], v_ref[0])

    spec = pl.BlockSpec((1, tr, nn), lambda i, c: (0, i, 0))
    hspec = pl.BlockSpec((tr, nn), lambda i, c: (i % nb, 0))
    grid_spec = pltpu.PrefetchScalarGridSpec(num_scalar_prefetch=1, grid=(2 * nb,),
                                             in_specs=[spec, hspec, hspec, spec, spec], out_specs=[spec] * 4)
    return pl.pallas_call(body, grid_spec=grid_spec, out_shape=[jax.ShapeDtypeStruct(w.shape, F32)] * 4, name=name,
                          compiler_params=_cparams(("parallel",)))(cidx, w, mine, other, m, v)


def _stack_sum(stack, name):
    n, rows, nn = stack.shape
    tr = _rows_tile(rows, 256)

    def body(s_ref, o_ref):
        acc = s_ref[0]
        for d in range(1, n):
            acc = acc + s_ref[d]
        o_ref[...] = acc

    return pl.pallas_call(
        body, grid=(rows // tr,), in_specs=[pl.BlockSpec((n, tr, nn), lambda i: (0, i, 0))],
        out_specs=pl.BlockSpec((tr, nn), lambda i: (i, 0)), out_shape=jax.ShapeDtypeStruct((rows, nn), F32),
        name=name, compiler_params=_cparams(("parallel",)))(stack)


def _adam_call(w, g, m, v, name):
    rows, nn = w.shape
    tr = _rows_tile(rows, 128)

    def body(w_ref, g_ref, m_ref, v_ref, d_ref, nm_ref, nv_ref):
        d_ref[...], nm_ref[...], nv_ref[...] = _adam_fn(w_ref[...], g_ref[...], m_ref[...], v_ref[...])

    spec = pl.BlockSpec((tr, nn), lambda i: (i, 0))
    sds = jax.ShapeDtypeStruct((rows, nn), F32)
    return pl.pallas_call(body, grid=(rows // tr,), in_specs=[spec] * 4, out_specs=[spec] * 3,
                          out_shape=[sds] * 3, name=name, compiler_params=_cparams(("parallel",)))(w, g, m, v)


def _adam_fn(w, g, m, v):
    m = ADAM_B1 * m + (1.0 - ADAM_B1) * g
    v = ADAM_B2 * v + (1.0 - ADAM_B2) * (g * g)
    m_hat = m / (1.0 - ADAM_B1 ** ADAM_STEP)
    v_hat = v / (1.0 - ADAM_B2 ** ADAM_STEP)
    delta = -ADAM_LR * (m_hat / (jnp.sqrt(v_hat) + ADAM_EPS) + ADAM_WD * w)
    return delta, m, v


def _mesh_pos():
    return lax.axis_index("x"), lax.axis_index("y"), lax.axis_index("c")


def _other_chips(x, y):
    return [(1 - x, y), (x, 1 - y), (1 - x, 1 - y)]


HBM_SPEC = pl.BlockSpec(memory_space=pl.ANY)


def _remote(src, dst, send_sems, recv_sems, k, to):
    return pltpu.make_async_remote_copy(src_ref=src, dst_ref=dst, send_sem=send_sems.at[k], recv_sem=recv_sems.at[k],
                                        device_id=to, device_id_type=MESH)


def _half_rows(c, rows, align):
    half = rows // 2
    return (pl.ds(pl.multiple_of(c * half, align), half), pl.ds(pl.multiple_of((1 - c) * half, align), half))


def _gather_weights(mats, conv):
    n = len(mats)

    def body(*refs):
        ins, conv_in = refs[:n], refs[n]
        outs, conv_out = refs[n + 1:2 * n + 1], refs[2 * n + 1]
        send_sems, recv_sems, local_sem = refs[2 * n + 2:]
        x, y, c = _mesh_pos()
        me, sibling, s = (x, y, c), (x, y, 1 - c), 2 * x + y
        chips = _other_chips(x, y)
        rows = [_half_rows(c, m.shape[0], 16) for m in mats]
        own = pltpu.make_async_copy(conv_in, conv_out.at[s], local_sem)
        own.start()
        sent = []
        for i in range(n):
            mine = rows[i][0]
            for j, (cx, cy) in enumerate(chips):
                sent.append(_remote(ins[i].at[mine], outs[i].at[s, mine], send_sems, recv_sems, 6 * i + j, (cx, cy, c)))
        for j, (cx, cy) in enumerate(chips):
            sent.append(_remote(conv_in, conv_out.at[s], send_sems, recv_sems, 6 * n + j, (cx, cy, c)))
        for cp in sent:
            cp.start()
        for i in range(n):
            mine = rows[i][0]
            for j, (cx, cy) in enumerate(chips):
                landed = outs[i].at[2 * cx + cy, mine]
                _remote(landed, landed, send_sems, recv_sems, 6 * i + j, me).wait_recv()
                fwd = _remote(landed, landed, send_sems, recv_sems, 6 * i + 3 + j, sibling)
                fwd.start()
                sent.append(fwd)
        for j, (cx, cy) in enumerate(chips):
            slot = conv_out.at[2 * cx + cy]
            _remote(slot, slot, send_sems, recv_sems, 6 * n + j, me).wait_recv()
        for i in range(n):
            theirs_rows = rows[i][1]
            for j, (cx, cy) in enumerate(chips):
                theirs = outs[i].at[2 * cx + cy, theirs_rows]
                _remote(theirs, theirs, send_sems, recv_sems, 6 * i + 3 + j, me).wait_recv()
        for cp in sent:
            cp.wait_send()
        own.wait()

    out_shape = [jax.ShapeDtypeStruct((4,) + m.shape, m.dtype) for m in mats]
    out_shape.append(jax.ShapeDtypeStruct((4,) + conv.shape, conv.dtype))
    res = pl.pallas_call(
        body, in_specs=[HBM_SPEC] * (n + 1), out_specs=[HBM_SPEC] * (n + 1), out_shape=out_shape,
        scratch_shapes=[pltpu.SemaphoreType.DMA((6 * n + 3,)), pltpu.SemaphoreType.DMA((6 * n + 3,)),
                        pltpu.SemaphoreType.DMA],
        name="all_gather_weights")(*mats, conv)
    return res[:n], res[n]


def _sibling_exchange(stacks):
    n = len(stacks)

    def body(*refs):
        ins, outs = refs[:n], refs[n:2 * n]
        send_sems, recv_sems = refs[2 * n:]
        x, y, c = _mesh_pos()
        cps = []
        for i in range(n):
            theirs = _half_rows(c, stacks[i].shape[1], 8)[1]
            cps.append(_remote(ins[i].at[:, theirs, :], outs[i], send_sems, recv_sems, i, (x, y, 1 - c)))
        for cp in cps:
            cp.start()
        for cp in cps:
            cp.wait()

    out_shape = [jax.ShapeDtypeStruct((4, s.shape[1] // 2, s.shape[2]), s.dtype) for s in stacks]
    return pl.pallas_call(
        body, in_specs=[HBM_SPEC] * n, out_specs=[HBM_SPEC] * n, out_shape=out_shape,
        scratch_shapes=[pltpu.SemaphoreType.DMA((n,)), pltpu.SemaphoreType.DMA((n,))],
        name="grad_sibling_exchange")(*stacks)


def _chip_exchange(parts):
    n = len(parts)

    def body(*refs):
        ins, outs = refs[:n], refs[n:2 * n]
        send_sems, recv_sems = refs[2 * n:]
        x, y, c = _mesh_pos()
        me, s = (x, y, c), 2 * x + y
        chips = _other_chips(x, y)
        sent = [_remote(ins[i].at[2 * cx + cy], outs[i].at[s], send_sems, recv_sems, 3 * i + j, (cx, cy, c))
                for i in range(n) for j, (cx, cy) in enumerate(chips)]
        for cp in sent:
            cp.start()
        for i in range(n):
            for j, (cx, cy) in enumerate(chips):
                slot = outs[i].at[2 * cx + cy]
                _remote(slot, slot, send_sems, recv_sems, 3 * i + j, me).wait_recv()
        for cp in sent:
            cp.wait_send()

    return pl.pallas_call(
        body, in_specs=[HBM_SPEC] * n, out_specs=[HBM_SPEC] * n,
        out_shape=[jax.ShapeDtypeStruct(p.shape, p.dtype) for p in parts],
        scratch_shapes=[pltpu.SemaphoreType.DMA((3 * n,)), pltpu.SemaphoreType.DMA((3 * n,))],
        name="grad_chip_exchange")(*parts)


def _sibling_swap(halves):
    n = len(halves)

    def body(*refs):
        ins, outs = refs[:n], refs[n:2 * n]
        send_sems, recv_sems = refs[2 * n:]
        x, y, c = _mesh_pos()
        cps = [_remote(ins[i], outs[i], send_sems, recv_sems, i, (x, y, 1 - c)) for i in range(n)]
        for cp in cps:
            cp.start()
        for cp in cps:
            cp.wait()

    return pl.pallas_call(
        body, in_specs=[HBM_SPEC] * n, out_specs=[HBM_SPEC] * n,
        out_shape=[jax.ShapeDtypeStruct(h.shape, h.dtype) for h in halves],
        scratch_shapes=[pltpu.SemaphoreType.DMA((n,)), pltpu.SemaphoreType.DMA((n,))],
        name="grad_sibling_swap")(*halves)


def _gather_small(vec):
    def body(in_ref, out_ref, send_sems, recv_sems, local_sem):
        x, y, c = _mesh_pos()
        me = (x, y, c)
        own = pltpu.make_async_copy(in_ref, out_ref.at[4 * x + 2 * y + c], local_sem)
        own.start()
        peers = [(1 - x if k & 4 else x, 1 - y if k & 2 else y, 1 - c if k & 1 else c) for k in range(1, 8)]
        sent = [_remote(in_ref, out_ref.at[4 * x + 2 * y + c], send_sems, recv_sems, k, p) for k, p in enumerate(peers)]
        for cp in sent:
            cp.start()
        for k, (px, py, pc) in enumerate(peers):
            slot = out_ref.at[4 * px + 2 * py + pc]
            _remote(slot, slot, send_sems, recv_sems, k, me).wait_recv()
        for cp in sent:
            cp.wait_send()
        own.wait()

    return pl.pallas_call(
        body, in_specs=[HBM_SPEC], out_specs=HBM_SPEC, out_shape=jax.ShapeDtypeStruct((8,) + vec.shape, vec.dtype),
        scratch_shapes=[pltpu.SemaphoreType.DMA((7,)), pltpu.SemaphoreType.DMA((7,)), pltpu.SemaphoreType.DMA],
        name="grad_gather_small")(vec)


def _reduce_matrices(stacks, names):
    cidx = lax.axis_index("c").astype(jnp.int32).reshape(1)
    chip = (2 * lax.axis_index("x") + lax.axis_index("y")).astype(jnp.int32).reshape(1)
    got = _sibling_exchange(stacks)
    pairs = [_pair_sum(a, b, cidx, "grad_pair_sum_" + nm) for a, b, nm in zip(stacks, got, names)]
    quads = _chip_exchange([p[1] for p in pairs])
    mine = [_chip_sum(q, p[0], chip, "grad_chip_sum_" + nm) for q, p, nm in zip(quads, pairs, names)]
    return mine, _sibling_swap(mine)


def _pad_cols(a, n):
    return jnp.concatenate([a, jnp.zeros((a.shape[0], n - a.shape[1]), a.dtype)], axis=1)


def _group_channels(a):
    lead = a.shape[:-1]
    xs = a[..., :D_INNER].reshape(lead + (SSM_GROUPS, GROUP_W))
    bs = a[..., D_INNER:D_INNER + SSM_GROUPS * STATE_N].reshape(lead + (SSM_GROUPS, STATE_N))
    cs = a[..., D_INNER + SSM_GROUPS * STATE_N:].reshape(lead + (SSM_GROUPS, STATE_N))
    return jnp.concatenate([xs, bs, cs], axis=-1).reshape(lead + (CONV_CH,))


PROJ_SEGS = (('gate_a', D_MODEL), ('gate_b', D_MODEL), ('z', D_INNER), ('xbc', CONV_CH), ('q_lat', Q_RANK),
             ('kv_lat', KV_RANK), ('k_rope', LANE), ('dt', LANE))
PROJ_WIDE = sum(w for _, w in PROJ_SEGS[:4])
PROJ_LANE0 = {n: (v if v < PROJ_WIDE else v - PROJ_WIDE) for n, v in
              zip([n for n, _ in PROJ_SEGS], [int(v) for v in np.cumsum([0] + [w for _, w in PROJ_SEGS])[:-1]])}
CONV_LANE0 = PROJ_LANE0['xbc']
KR_LANE0 = PROJ_LANE0['k_rope']


def _lay_w_in(w):
    idx = np.cumsum(IN_SIZES)[:-1]
    q_lat, kv_lat, k_rope, z, xbc, dt, gate_a, gate_b = jnp.split(w, [int(v) for v in idx], axis=1)
    return jnp.concatenate([gate_a, gate_b, z, _group_channels(xbc), q_lat, kv_lat, _pad_cols(k_rope, LANE),
                            _pad_cols(dt, LANE)], axis=1)


@jax.custom_vjp
def project(h, w, tok):
    return _project_impl(h, w)


def _project_impl(h, w):
    return (_mm(h, w[:, :PROJ_WIDE], "w_in_fwd", BF16), _mm(h, w[:, PROJ_WIDE:], "w_in_narrow_fwd")) + tuple(
        jnp.zeros((h.shape[0], wd), BF16) for _, wd in PROJ_SEGS)


def _project_fwd(h, w, tok):
    return _project_impl(h, w), (h, w)


def _project_bwd(res, cots):
    h, w = res
    g = jnp.concatenate(cots[2:], axis=1)
    return _mm(g, w.T, "w_in_dx", h.dtype), jnp.zeros_like(w), _mm(h.T, g, "w_in_dw")


project.defvjp(_project_fwd, _project_bwd)


def _lay_w_uq(w):
    w3 = w.reshape(Q_RANK, N_HEADS, NOPE + ROPE)
    w3 = jnp.concatenate([w3, jnp.zeros((Q_RANK, N_HEADS, QK_PAD - NOPE - ROPE), w.dtype)], axis=2)
    return w3.reshape(Q_RANK, N_HEADS * QK_PAD)


def _lay_w_ukv(w):
    w3 = w.reshape(KV_RANK, N_HEADS, NOPE + V_DIM)
    return jnp.concatenate([w3[:, :, :NOPE].reshape(KV_RANK, -1), w3[:, :, NOPE:].reshape(KV_RANK, -1)], axis=1)


def _pad_lanes(v, n=LANE):
    return jnp.concatenate([v, jnp.zeros((v.shape[0], n - v.shape[1]), v.dtype)], axis=1)


def _local_loss(toks, small, x, wb, c8, posf, target):
    B, S, D = x.shape
    T = B * S

    def lin(name, a, key, lay=lambda w: w, out_dtype=F32):
        return make_linear(name, out_dtype)(a, lay(wb[key]), lay(toks[key]))

    rows2 = lambda a: a.reshape(T, a.shape[-1])
    rows3 = lambda a: a.reshape(B, S, a.shape[-1])

    sc = make_rowwise("silu_c", _f_silu, 1, 0, 0, ('row',))((c8[None],), (), ())[0][0]
    mod = make_linear("ada", F32, 4)(sc, wb['w_ada'], toks['w_ada'])[:B] + small['b_ada']
    shift1, scale1, gate1, shift2, scale2, gate2 = [m[:, None, :] for m in jnp.split(mod, 6, axis=-1)]

    h, x_res = make_rowwise("modulate1", _f_modulate, 1, 2, 1, ('row',), forward_row=0, ts_cap=1024)(
        (x,), (scale1, shift1), (small['g_pre_mix'],))
    outs = project(rows2(h), _lay_w_in(wb['w_in']), _lay_w_in(toks['w_in']))
    wide = lax.stop_gradient(rows3(outs[0]))
    proj = lax.stop_gradient(rows3(outs[1]))
    stand = {n: rows3(o) for (n, _), o in zip(PROJ_SEGS, outs[2:])}

    def win(seg, block):
        return (PROJ_LANE0[seg] // block, dict(PROJ_SEGS)[seg])

    inv = ROPE_THETA ** (-jnp.arange(ROPE // 2, dtype=F32) / (ROPE // 2))
    inv_lane = jnp.concatenate([inv, inv, jnp.zeros((LANE - ROPE,), F32)])[None]
    tabs = tuple(_rope_tables(posf, inv_lane))
    qn = make_rowwise("rms_q", _f_rms, 1, 0, 1, ('row',), ts_cap=4096, windows={0: win('q_lat', Q_RANK)})(
        (proj,), (), (small['g_q_lat'],), (stand['q_lat'],))[0]
    kvn = make_rowwise("rms_kv", _f_rms, 1, 0, 1, ('row',), ts_cap=4096, windows={0: win('kv_lat', KV_RANK)})(
        (proj,), (), (small['g_kv_lat'],), (stand['kv_lat'],))[0]
    qp = rows3(lin("w_uq", rows2(qn), 'w_uq', _lay_w_uq, BF16))
    kvp = rows3(lin("w_ukv", rows2(kvn), 'w_ukv', _lay_w_ukv, BF16))
    qr = rope_q(qp, tabs)
    att = attention(qr, kvp, proj, stand['k_rope'], tabs)
    attn = rows3(lin("w_o_attn", rows2(att), 'w_o_attn', out_dtype=BF16))

    xa = conv_silu(wide, stand['xbc'], _group_channels(wb['conv_w_f32']), _group_channels(small['conv_b']))
    dt_pad, a_pad = make_rowwise("dt_softplus", _f_dt, 1, 0, 2, ('row', 'row'), ts_cap=4096,
                                 windows={0: win('dt', LANE)})(
        (proj,), (), (_pad_lanes(small['dt_bias']), _pad_lanes(small['a_log'])), (stand['dt'],))
    ac_pad = chunk_cumsum(a_pad)
    acr = jnp.transpose(ac_pad[..., :SSM_HEADS], (0, 2, 1))[:, :, None, :]
    dsk = jnp.repeat(small['d_skip'], HEAD_P, axis=-1)
    y = ssd(xa, dt_pad, ac_pad, acr, dsk)
    yg = make_rowwise("gated_norm", _f_gated_norm, 2, 0, 1, ('row',), ncol=SSM_GROUPS, ts_cap=2048,
                      windows={1: win('z', GROUP_W)})((y, wide), (), (small['g_ssm_out'],), (stand['z'],))[0]
    ssm = rows3(lin("w_o_ssm", rows2(yg), 'w_o_ssm', out_dtype=BF16))

    merged = make_rowwise("merge", _f_merge, 4, 0, 0, ('row',), ts_cap=1024,
                          windows={2: win('gate_a', D_MODEL), 3: win('gate_b', D_MODEL)})(
        (attn, ssm, wide, wide), (), (), (stand['gate_a'], stand['gate_b']))[0]
    mix = rows3(lin("w_out", rows2(merged), 'w_out', out_dtype=BF16))
    x1 = make_rowwise("post_mix", _f_post, 2, 1, 1, ('row',), ts_cap=1024)(
        (x_res, mix), (gate1,), (small['g_post_mix'],))[0]

    h2, x1_res = make_rowwise("modulate2", _f_modulate, 1, 2, 1, ('row',), forward_row=0, ts_cap=1024)(
        (x1,), (scale2, shift2), (small['g_pre_mlp'],))
    ff = rows3(ffn(rows2(h2), wb['w_ff1'], toks['w_ff1'], wb['w_ff2'], toks['w_ff2']))
    lvec = make_rowwise("final_loss", _f_final_loss, 3, 1, 1, ('sum',), nodiff=(2,), ts_cap=1024)(
        (x1_res, ff, target), (gate2,), (small['g_post_mlp'],))[0]
    return jnp.sum(lvec)


MATRICES = COL_SHARDED + ROW_SHARDED
STACKED_DW = ('w_ada', 'w_ff1')


def _local_step(x, c, positions, target, wb, small):
    B = x.shape[0]
    c8 = jnp.concatenate([c, jnp.zeros((16 - B, c.shape[1]), F32)], axis=0)
    posf = positions.astype(F32)[..., None]
    toks = {k: jnp.zeros(wb[k].shape, F32) for k in MATRICES if k != 'conv_w'}
    for k in STACKED_DW:
        rows, cols = wb[k].shape
        toks[k] = jnp.zeros((4, rows, cols // 4), F32)
    conv_w = wb['conv_w_f32']

    def loss_fn(toks, small, conv_w, x):
        wbl = dict(wb)
        wbl['conv_w_f32'] = conv_w
        return _local_loss(toks, small, x, wbl, c8, posf, target)

    loss, (g_tok, g_small, g_conv, g_x) = jax.value_and_grad(loss_fn, argnums=(0, 1, 2, 3))(toks, small, conv_w, x)
    grads = dict(g_tok)
    grads.update(g_small)
    grads['conv_w'] = g_conv
    return loss, g_x, grads


def kernel(x, c, positions, w_ada, b_ada, g_pre_mix, g_post_mix, w_in, g_q_lat, g_kv_lat, w_uq, w_ukv, w_o_attn, conv_w, conv_b, dt_bias, a_log, d_skip, g_ssm_out, w_o_ssm, w_out, g_pre_mlp, g_post_mlp, w_ff1, w_ff2, loss_target, m_w_ada, m_b_ada, m_g_pre_mix, m_g_post_mix, m_w_in, m_g_q_lat, m_g_kv_lat, m_w_uq, m_w_ukv, m_w_o_attn, m_conv_w, m_conv_b, m_dt_bias, m_a_log, m_d_skip, m_g_ssm_out, m_w_o_ssm, m_w_out, m_g_pre_mlp, m_g_post_mlp, m_w_ff1, m_w_ff2, v_w_ada, v_b_ada, v_g_pre_mix, v_g_post_mix, v_w_in, v_g_q_lat, v_g_kv_lat, v_w_uq, v_w_ukv, v_w_o_attn, v_conv_w, v_conv_b, v_dt_bias, v_a_log, v_d_skip, v_g_ssm_out, v_w_o_ssm, v_w_out, v_g_pre_mlp, v_g_post_mlp, v_w_ff1, v_w_ff2):
    given = dict(locals())
    w_loc = {n: given[n] for n in WEIGHTS}
    m_loc = {n: given["m_" + n] for n in WEIGHTS}
    v_loc = {n: given["v_" + n] for n in WEIGHTS}
    mats = [n for n in WEIGHTS if n in MATRICES and n != 'conv_w']
    vecs = [n for n in WEIGHTS if n not in MATRICES]

    own = [w_loc[n][0].astype(BF16) for n in mats]
    g_mats, g_conv = _gather_weights(own, conv_w[0])
    chip = 2 * lax.axis_index("x") + lax.axis_index("y")
    wb = {}
    for n, g, mine in zip(mats, g_mats, own):
        g = lax.dynamic_update_slice_in_dim(g, mine[None], chip, axis=0)
        if n in COL_SHARDED:
            wb[n] = jnp.transpose(g, (1, 0, 2)).reshape(g.shape[1], -1)
        else:
            wb[n] = g.reshape(-1, g.shape[2])
    wb['conv_w_f32'] = jnp.transpose(g_conv, (1, 0, 2)).reshape(CONV_K, -1)
    small = {n: w_loc[n] for n in vecs}

    loss_part, grad_x, grads = _local_step(x, c, positions, loss_target, wb, small)
    loss = lax.psum(loss_part, ("x", "y", "c"))

    stacks = []
    for n in mats:
        kk, nn = w_loc[n].shape[1:]
        if n in STACKED_DW:
            stacks.append(grads[n])
        elif n in COL_SHARDED:
            stacks.append(jnp.transpose(grads[n].reshape(kk, 4, nn), (1, 0, 2)))
        else:
            stacks.append(grads[n].reshape(4, kk, nn))
    g_mine, g_other = _reduce_matrices(stacks, mats)
    g_shard = {}

    vec_shapes = [tuple(grads[n].shape) for n in vecs] + [tuple(grads['conv_w'].shape)]
    total = _stack_sum(_gather_small(_pack_small([grads[n] for n in vecs] + [grads['conv_w']])), "grad_sum_small")
    g_vec = _unpack_small(total, vec_shapes)
    n_conv = conv_w.shape[2]
    chip = 2 * lax.axis_index("x") + lax.axis_index("y")
    g_shard['conv_w'] = lax.dynamic_slice_in_dim(g_vec[-1], chip * n_conv, n_conv, axis=1)
    for n, g in zip(vecs, g_vec):
        g_shard[n] = g

    delta, new_m, new_v = {}, {}, {}
    cidx = lax.axis_index("c").astype(jnp.int32).reshape(1)
    for n, mine, other in zip(mats, g_mine, g_other):
        g_shard[n], delta[n], new_m[n], new_v[n] = _adam_halves_call(
            w_loc[n], mine, other, cidx, m_loc[n], v_loc[n], "adamw_" + n)
    rest = vecs + ['conv_w']
    rest_shapes = [tuple(w_loc[n].shape) for n in rest]
    packed = [_pack_small([src[n] for n in rest]) for src in (w_loc, g_shard, m_loc, v_loc)]
    for dst, buf in zip((delta, new_m, new_v), _adam_call(*packed, "adamw_small")):
        dst.update(zip(rest, _unpack_small(buf, rest_shapes)))

    def out(d):
        return [d[n].reshape(w_loc[n].shape) for n in WEIGHTS]

    return (loss, grad_x, *out(g_shard), *out(delta), *out(new_m), *out(new_v))
```

```python
import numpy as np
import jax
import jax.numpy as jnp
from jax import lax
from jax.experimental import pallas as pl
from jax.experimental.pallas import tpu as pltpu

F32 = jnp.float32
BF16 = jnp.bfloat16
MESH = pl.DeviceIdType.MESH

D_MODEL = 1024
N_HEADS = 8
NOPE = 128
ROPE = 64
V_DIM = 128
Q_RANK = 256
KV_RANK = 256
ROPE_THETA = 10000.0
D_INNER = 2048
SSM_HEADS = 32
SSM_GROUPS = 8
HEAD_P = 64
STATE_N = 128
CONV_K = 4
CHUNK = 128
CONV_CH = D_INNER + 2 * SSM_GROUPS * STATE_N
EPS = 1e-6
IN_SIZES = (Q_RANK, KV_RANK, ROPE, D_INNER, CONV_CH, SSM_HEADS, D_MODEL, D_MODEL)
ADAM_LR, ADAM_B1, ADAM_B2, ADAM_EPS, ADAM_WD, ADAM_STEP = 0.001, 0.9, 0.999, 1e-08, 0.01, 10

VMEM_LIMIT_BYTES = 52 * 1024 * 1024
LANE = 128
QK_PAD = 256

WEIGHTS = ['w_ada', 'b_ada', 'g_pre_mix', 'g_post_mix', 'w_in', 'g_q_lat', 'g_kv_lat', 'w_uq', 'w_ukv',
           'w_o_attn', 'conv_w', 'conv_b', 'dt_bias', 'a_log', 'd_skip', 'g_ssm_out', 'w_o_ssm', 'w_out',
           'g_pre_mlp', 'g_post_mlp', 'w_ff1', 'w_ff2']
COL_SHARDED = ('w_ada', 'w_in', 'w_uq', 'w_ukv', 'conv_w', 'w_ff1')
ROW_SHARDED = ('w_o_attn', 'w_o_ssm', 'w_out', 'w_ff2')


def _cparams(sem):
    return pltpu.CompilerParams(dimension_semantics=sem, vmem_limit_bytes=VMEM_LIMIT_BYTES)


def _tile(n, cap):
    if n <= cap:
        return n
    k = n // LANE
    best = LANE
    for d in range(1, k + 1):
        if k % d == 0 and d * LANE <= cap:
            best = d * LANE
    return best


def _mm(a, w, name, out_dtype=F32, epilogue=None, extras=(), out_dtypes=None):
    M, K = a.shape
    N = w.shape[1]
    tm = min(M, 1024)
    tn = _tile(N, 1024)
    tk = _tile(K, 2048)
    nk = K // tk
    dts = tuple(out_dtypes) if epilogue is not None else (out_dtype,)
    n_x, n_o = len(extras), len(dts)

    def finish(acc, refs):
        res = epilogue(acc, *[r[...] for r in refs[:n_x]]) if epilogue is not None else (acc,)
        for o_ref, val, dt in zip(refs[n_x:n_x + n_o], res, dts):
            o_ref[...] = val.astype(dt)

    def body(a_ref, w_ref, *refs):
        part = jnp.dot(a_ref[...].astype(BF16), w_ref[...], preferred_element_type=F32)
        if nk == 1:
            finish(part, refs)
        else:
            acc_ref = refs[-1]
            k = pl.program_id(2)

            @pl.when(k == 0)
            def _():
                acc_ref[...] = part

            @pl.when(k > 0)
            def _():
                acc_ref[...] += part

            @pl.when(k == nk - 1)
            def _():
                finish(acc_ref[...], refs)

    ospec = pl.BlockSpec((tm, tn), lambda i, j, k: (i, j))
    res = pl.pallas_call(
        body, grid=(M // tm, N // tn, nk),
        in_specs=[pl.BlockSpec((tm, tk), lambda i, j, k: (i, k)), pl.BlockSpec((tk, tn), lambda i, j, k: (k, j))]
        + [ospec] * n_x,
        out_specs=[ospec] * n_o, out_shape=[jax.ShapeDtypeStruct((M, N), dt) for dt in dts],
        scratch_shapes=[pltpu.VMEM((tm, tn), F32)] if nk > 1 else [], name=name,
        compiler_params=_cparams(("parallel", "parallel", "arbitrary")))(a, w, *extras)
    return res if epilogue is not None else res[0]


def _mm_tn(a, g, name, col_shards=1):
    M, K = a.shape
    N = g.shape[1]
    tm = min(M, 1024)
    tk = _tile(K, 1024)
    tn = _tile(N // col_shards, 1024)
    nm = M // tm
    per = N // col_shards // tn

    def body(a_ref, g_ref, o_ref):
        part = lax.dot_general(a_ref[...].astype(BF16), g_ref[...].astype(BF16), (((0,), (0,)), ((), ())),
                               preferred_element_type=F32)
        m = pl.program_id(2)

        @pl.when(m == 0)
        def _():
            o_ref[...] = part.reshape(o_ref.shape)

        @pl.when(m > 0)
        def _():
            o_ref[...] += part.reshape(o_ref.shape)

    if col_shards == 1:
        out_spec = pl.BlockSpec((tk, tn), lambda i, j, m: (i, j))
        out_shape = jax.ShapeDtypeStruct((K, N), F32)
    else:
        out_spec = pl.BlockSpec((1, tk, tn), lambda i, j, m: (j // per, i, j % per))
        out_shape = jax.ShapeDtypeStruct((col_shards, K, N // col_shards), F32)
    return pl.pallas_call(
        body, grid=(K // tk, N // tn, nm),
        in_specs=[pl.BlockSpec((tm, tk), lambda i, j, m: (m, i)), pl.BlockSpec((tm, tn), lambda i, j, m: (m, j))],
        out_specs=out_spec, out_shape=out_shape, name=name,
        compiler_params=_cparams(("parallel", "parallel", "arbitrary")))(a, g)


def make_linear(name, out_dtype=F32, dw_col_shards=1):
    @jax.custom_vjp
    def linear(a, w, tok):
        return _mm(a, w, name + "_fwd", out_dtype)

    def fwd(a, w, tok):
        return _mm(a, w, name + "_fwd", out_dtype), (a, w)

    def bwd(res, g):
        a, w = res
        da = _mm(g, w.T, name + "_dx", a.dtype)
        dw = _mm_tn(a, g, name + "_dw", dw_col_shards)
        return da, jnp.zeros_like(w), dw

    linear.defvjp(fwd, bwd)
    return linear


def _relu2_epilogue(acc):
    r = jnp.maximum(acc, 0.0)
    return r * r, r


def _relu2_bwd_epilogue(acc, r):
    return (acc * (2.0 * r.astype(F32)),)


@jax.custom_vjp
def ffn(h, w1, tok1, w2, tok2):
    act, _ = _mm(h, w1, "w_ff1_fwd", epilogue=_relu2_epilogue, out_dtypes=(BF16, BF16))
    return _mm(act, w2, "w_ff2_fwd", BF16)


def _ffn_fwd(h, w1, tok1, w2, tok2):
    act, r = _mm(h, w1, "w_ff1_fwd", epilogue=_relu2_epilogue, out_dtypes=(BF16, BF16))
    return _mm(act, w2, "w_ff2_fwd", BF16), (h, w1, w2, act, r)


def _ffn_bwd(res, g):
    h, w1, w2, act, r = res
    du = _mm(g, w2.T, "w_ff2_dx", epilogue=_relu2_bwd_epilogue, extras=(r,), out_dtypes=(BF16,))[0]
    dw2 = _mm_tn(act, g, "w_ff2_dw")
    dw1 = _mm_tn(h, du, "w_ff1_dw", 4)
    dh = _mm(du, w1.T, "w_ff1_dx", h.dtype)
    return dh, jnp.zeros_like(w1), dw1, jnp.zeros_like(w2), dw2


ffn.defvjp(_ffn_fwd, _ffn_bwd)


def make_rowwise(name, f, n_rows, n_seqs, n_pars, out_kinds, ncol=1, nodiff=(), ts_cap=512, windows=None,
                 forward_row=None):
    windows = dict(windows or {})
    n_in = n_rows + n_seqs + n_pars
    diff_idx = [i for i in range(n_in) if i not in nodiff]

    def _dims(rows):
        B, S = rows[0].shape[0], rows[0].shape[1]
        ts = min(S, ts_cap)
        return B, S, ts

    def _width(i, r):
        return windows[i][1] if i in windows else r.shape[2]

    def _in_specs(rows, seqs, pars, ts):
        specs = []
        for i, r in enumerate(rows):
            col0 = windows[i][0] if i in windows else 0
            specs.append(pl.BlockSpec((1, ts, _width(i, r) // ncol), lambda k, b, s, col0=col0: (b, s, k + col0)))
        for q in seqs:
            specs.append(pl.BlockSpec((1, 1, q.shape[2] // ncol), lambda k, b, s: (b, 0, k)))
        for p in pars:
            specs.append(pl.BlockSpec((1, p.shape[1] // ncol), lambda k, b, s: (0, k)))
        return specs

    def _load(refs):
        vals = [r[0] for r in refs[:n_rows + n_seqs]]
        vals += [r[...] for r in refs[n_rows + n_seqs:n_in]]
        return vals

    def _out_struct(rows, seqs, pars, ts):
        blocks = [jax.ShapeDtypeStruct((ts, _width(i, r) // ncol), r.dtype) for i, r in enumerate(rows)]
        blocks += [jax.ShapeDtypeStruct((1, q.shape[2] // ncol), q.dtype) for q in seqs]
        blocks += [jax.ShapeDtypeStruct((1, p.shape[1] // ncol), p.dtype) for p in pars]
        return jax.eval_shape(f, *blocks)

    def _fwd_call(rows, seqs, pars):
        B, S, ts = _dims(rows)
        outs = _out_struct(rows, seqs, pars, ts)
        n_out = len(outs)

        def body(*refs):
            res = f(*_load(refs))
            first = (pl.program_id(1) == 0) & (pl.program_id(2) == 0)
            for o_ref, val, kind in zip(refs[n_in:], res, out_kinds):
                if kind == 'row':
                    o_ref[0] = val
                else:
                    tot = jnp.sum(val, axis=0, keepdims=True)

                    @pl.when(first)
                    def _(o_ref=o_ref, tot=tot):
                        o_ref[...] = tot

                    @pl.when(jnp.logical_not(first))
                    def _(o_ref=o_ref, tot=tot):
                        o_ref[...] += tot

        out_shape, out_specs = [], []
        for o, kind in zip(outs, out_kinds):
            d = o.shape[1]
            if kind == 'row':
                out_shape.append(jax.ShapeDtypeStruct((B, S, ncol * d), o.dtype))
                out_specs.append(pl.BlockSpec((1, ts, d), lambda k, b, s: (b, s, k)))
            else:
                out_shape.append(jax.ShapeDtypeStruct((1, ncol * d), o.dtype))
                out_specs.append(pl.BlockSpec((1, d), lambda k, b, s: (0, k)))
        res = pl.pallas_call(
            body, grid=(ncol, B, S // ts), in_specs=_in_specs(rows, seqs, pars, ts), out_specs=out_specs,
            out_shape=out_shape, name=name + "_fwd",
            compiler_params=_cparams(("arbitrary", "arbitrary", "arbitrary")))(*rows, *seqs, *pars)
        return tuple(res)

    def _bwd_call(rows, seqs, pars, cots, carried=None):
        B, S, ts = _dims(rows)
        outs = _out_struct(rows, seqs, pars, ts)
        n_out = len(outs)
        all_in = list(rows) + list(seqs) + list(pars)
        extra = [] if carried is None else [carried]

        def body(*refs):
            vals = _load(refs)
            if carried is not None:
                carried_ref, refs = refs[n_in + n_out], refs[:n_in + n_out] + refs[n_in + n_out + 1:]
            cts = []
            for c_ref, o, kind in zip(refs[n_in:n_in + n_out], outs, out_kinds):
                if kind == 'row':
                    cts.append(c_ref[0])
                else:
                    cts.append(jnp.broadcast_to(c_ref[...], o.shape))

            def g(*dv):
                full = list(vals)
                for i, v in zip(diff_idx, dv):
                    full[i] = v
                return tuple(f(*full))

            _, vjp = jax.vjp(g, *[vals[i] for i in diff_idx])
            grads = vjp(tuple(cts))
            b, s = pl.program_id(1), pl.program_id(2)
            for o_ref, i, gr in zip(refs[n_in + n_out:], diff_idx, grads):
                if i < n_rows:
                    if carried is not None and i == forward_row:
                        gr = gr + carried_ref[0]
                    o_ref[0] = gr.astype(o_ref.dtype)
                else:
                    first = (s == 0) if i < n_rows + n_seqs else ((b == 0) & (s == 0))
                    target = (lambda r: r.at[0]) if i < n_rows + n_seqs else (lambda r: r)

                    @pl.when(first)
                    def _(o_ref=o_ref, gr=gr, target=target):
                        target(o_ref)[...] = gr

                    @pl.when(jnp.logical_not(first))
                    def _(o_ref=o_ref, gr=gr, target=target):
                        target(o_ref)[...] += gr

        cot_specs = []
        for o, kind in zip(outs, out_kinds):
            d = o.shape[1]
            if kind == 'row':
                cot_specs.append(pl.BlockSpec((1, ts, d), lambda k, b, s: (b, s, k)))
            else:
                cot_specs.append(pl.BlockSpec((1, d), lambda k, b, s: (0, k)))
        out_shape, out_specs = [], []
        for i in diff_idx:
            a = all_in[i]
            if i < n_rows:
                out_shape.append(jax.ShapeDtypeStruct((B, S, _width(i, a)), BF16 if i in windows else a.dtype))
                out_specs.append(pl.BlockSpec((1, ts, _width(i, a) // ncol), lambda k, b, s: (b, s, k)))
                continue
            out_shape.append(jax.ShapeDtypeStruct(a.shape, a.dtype))
            if i < n_rows + n_seqs:
                out_specs.append(pl.BlockSpec((1, 1, a.shape[2] // ncol), lambda k, b, s: (b, 0, k)))
            else:
                out_specs.append(pl.BlockSpec((1, a.shape[1] // ncol), lambda k, b, s: (0, k)))
        if carried is not None:
            cot_specs.append(pl.BlockSpec((1, ts, carried.shape[2] // ncol), lambda k, b, s: (b, s, k)))
        res = pl.pallas_call(
            body, grid=(ncol, B, S // ts), in_specs=_in_specs(rows, seqs, pars, ts) + cot_specs,
            out_specs=out_specs, out_shape=out_shape, name=name + "_bwd",
            compiler_params=_cparams(("arbitrary", "arbitrary", "arbitrary")))(*all_in, *cots, *extra)
        grads = [None] * n_in
        for i, r in zip(diff_idx, res):
            grads[i] = r
        for i in nodiff:
            grads[i] = jnp.zeros_like(all_in[i])
        stand_in_grads = tuple(grads[i] for i in sorted(windows))
        for i in windows:
            grads[i] = jnp.zeros_like(all_in[i])
        return (tuple(grads[:n_rows]), tuple(grads[n_rows:n_rows + n_seqs]), tuple(grads[n_rows + n_seqs:]),
                stand_in_grads)

    def _outputs(rows, seqs, pars):
        res = _fwd_call(rows, seqs, pars)
        return res if forward_row is None else res + (rows[forward_row],)

    @jax.custom_vjp
    def op(rows, seqs, pars, stand_ins):
        return _outputs(rows, seqs, pars)

    def fwd(rows, seqs, pars, stand_ins):
        return _outputs(rows, seqs, pars), (rows, seqs, pars)

    def bwd(res, cots):
        rows, seqs, pars = res
        if forward_row is None:
            return _bwd_call(rows, seqs, pars, cots)
        return _bwd_call(rows, seqs, pars, cots[:-1], cots[-1])

    op.defvjp(fwd, bwd)
    return lambda rows, seqs, pars, stand_ins=(): op(tuple(rows), tuple(seqs), tuple(pars), tuple(stand_ins))


def _rms(x, g):
    x = x.astype(F32)
    return x * lax.rsqrt(jnp.mean(x * x, axis=-1, keepdims=True) + EPS) * g


def _silu(x):
    return x * lax.logistic(x)


def _f_silu(c):
    return (_silu(c),)


def _f_modulate(x, scale, shift, g):
    return ((_rms(x, g) * (1.0 + scale) + shift).astype(BF16),)


def _f_rms(x, g):
    return (_rms(x, g).astype(BF16),)


def _f_dt(dt_raw, dt_bias, a_log):
    z = dt_raw + dt_bias
    dt = jnp.maximum(z, 0.0) + jnp.log1p(jnp.exp(-jnp.abs(z)))
    return dt, dt * (-jnp.exp(a_log))


def _f_gated_norm(y, z, g):
    return (_rms(y * _silu(z.astype(F32)), g).astype(BF16),)


def _f_merge(attn, ssm, ga, gb):
    return ((lax.logistic(ga.astype(F32)) * attn + lax.logistic(gb.astype(F32)) * ssm).astype(BF16),)


def _f_post(x, m, gate, g):
    return (x + gate * _rms(m, g),)


def _f_final_loss(x, ff, target, gate, g):
    e = x + gate * _rms(ff, g) - target
    return (e * e * (0.5 / D_MODEL),)


def _rope_tables(posf, inv_lane):
    B, S, _ = posf.shape
    ts = min(S, 512)

    def body(p_ref, inv_ref, c_ref, a_ref, b_ref):
        ang = p_ref[0] * inv_ref[...]
        cs, sn = jnp.cos(ang), jnp.sin(ang)
        lane = lax.broadcasted_iota(jnp.int32, ang.shape, 1)
        c_ref[0] = jnp.where(lane < ROPE, cs, 0.0)
        a_ref[0] = jnp.where(lane < ROPE // 2, -sn, 0.0)
        b_ref[0] = jnp.where((lane >= ROPE // 2) & (lane < ROPE), sn, 0.0)

    spec = pl.BlockSpec((1, ts, LANE), lambda b, s: (b, s, 0))
    sds = jax.ShapeDtypeStruct((B, S, LANE), F32)
    return pl.pallas_call(
        body, grid=(B, S // ts),
        in_specs=[pl.BlockSpec((1, ts, 1), lambda b, s: (b, s, 0)), pl.BlockSpec((1, LANE), lambda b, s: (0, 0))],
        out_specs=[spec, spec, spec], out_shape=[sds, sds, sds], name="rope_tables",
        compiler_params=_cparams(("parallel", "parallel")))(posf, inv_lane)


def _rot(u, c, a, bm):
    return u * c + pltpu.roll(u, 96, 1) * a + pltpu.roll(u, 32, 1) * bm


def _rot_t(g, c, a, bm):
    return g * c + pltpu.roll(g * a, 32, 1) + pltpu.roll(g * bm, 96, 1)


def _rope_q_call(q, tabs, transpose, name):
    B, S, W = q.shape
    ts = min(S, 1024)
    fn = _rot_t if transpose else _rot
    out_dtype = BF16

    def body(q_ref, c_ref, a_ref, b_ref, o_ref):
        tc, ta, tb = c_ref[0], a_ref[0], b_ref[0]
        for h in range(W // QK_PAD):
            u = q_ref[0, :, h * QK_PAD:(h + 1) * QK_PAD].astype(F32) * ATT_SCALE
            r = fn(u[:, NOPE:], tc, ta, tb)
            o_ref[0, :, h * QK_PAD:(h + 1) * QK_PAD] = jnp.concatenate([u[:, :NOPE], r], axis=1).astype(out_dtype)

    tspec = pl.BlockSpec((1, ts, LANE), lambda b, s: (b, s, 0))
    qspec = pl.BlockSpec((1, ts, W), lambda b, s: (b, s, 0))
    return pl.pallas_call(
        body, grid=(B, S // ts), in_specs=[qspec, tspec, tspec, tspec], out_specs=qspec,
        out_shape=jax.ShapeDtypeStruct(q.shape, out_dtype), name=name,
        compiler_params=_cparams(("parallel", "parallel")))(q, *tabs)


@jax.custom_vjp
def rope_q(q, tabs):
    return _rope_q_call(q, tabs, False, "rope_q_fwd")


def _rope_q_fwd(q, tabs):
    return _rope_q_call(q, tabs, False, "rope_q_fwd"), tabs


def _rope_q_bwd(tabs, g):
    return _rope_q_call(g, tabs, True, "rope_q_bwd"), tuple(jnp.zeros_like(t) for t in tabs)


rope_q.defvjp(_rope_q_fwd, _rope_q_bwd)


def _build_k_fwd_call(kv, kr, tabs):
    B, S, _ = kv.shape
    ts = min(S, 1024)

    def body(kv_ref, kr_ref, c_ref, a_ref, b_ref, o_ref):
        r = _rot(kr_ref[0], c_ref[0], a_ref[0], b_ref[0]).astype(BF16)
        for h in range(N_HEADS):
            o_ref[0, :, h * QK_PAD:(h + 1) * QK_PAD] = jnp.concatenate(
                [kv_ref[0, :, h * NOPE:(h + 1) * NOPE], r], axis=1)

    tspec = pl.BlockSpec((1, ts, LANE), lambda b, s: (b, s, 0))
    kr_spec = pl.BlockSpec((1, ts, LANE), lambda b, s: (b, s, KR_LANE0 // LANE))
    return pl.pallas_call(
        body, grid=(B, S // ts),
        in_specs=[pl.BlockSpec((1, ts, N_HEADS * NOPE), lambda b, s: (b, s, 0)), kr_spec, tspec, tspec, tspec],
        out_specs=pl.BlockSpec((1, ts, N_HEADS * QK_PAD), lambda b, s: (b, s, 0)),
        out_shape=jax.ShapeDtypeStruct((B, S, N_HEADS * QK_PAD), BF16), name="build_k_fwd",
        compiler_params=_cparams(("parallel", "parallel")))(kv, kr, *tabs)


def _build_k_bwd_call(g, tabs):
    B, S, _ = g.shape
    ts = min(S, 1024)

    def body(g_ref, c_ref, a_ref, b_ref, dk_ref, dr_ref):
        tot = None
        for h in range(N_HEADS):
            dk_ref[0, :, h * NOPE:(h + 1) * NOPE] = g_ref[0, :, h * QK_PAD:h * QK_PAD + NOPE]
            part = g_ref[0, :, h * QK_PAD + NOPE:(h + 1) * QK_PAD].astype(F32)
            tot = part if tot is None else tot + part
        dr_ref[0] = _rot_t(tot, c_ref[0], a_ref[0], b_ref[0]).astype(BF16)

    tspec = pl.BlockSpec((1, ts, LANE), lambda b, s: (b, s, 0))
    return pl.pallas_call(
        body, grid=(B, S // ts),
        in_specs=[pl.BlockSpec((1, ts, N_HEADS * QK_PAD), lambda b, s: (b, s, 0)), tspec, tspec, tspec],
        out_specs=[pl.BlockSpec((1, ts, N_HEADS * NOPE), lambda b, s: (b, s, 0)), tspec],
        out_shape=[jax.ShapeDtypeStruct((B, S, N_HEADS * NOPE), BF16), jax.ShapeDtypeStruct((B, S, LANE), BF16)],
        name="build_k_bwd", compiler_params=_cparams(("parallel", "parallel")))(g, *tabs)


ATT_SCALE = (NOPE + ROPE) ** -0.5
NEG = -1e30


def _att_tiles(S):
    t = min(S, 512)
    return t, S // t


def _scores(q, k, diagonal):
    s = lax.dot_general(q, k, (((1,), (1,)), ((), ())), preferred_element_type=F32)
    if diagonal:
        row = lax.broadcasted_iota(jnp.int32, s.shape, 0)
        col = lax.broadcasted_iota(jnp.int32, s.shape, 1)
        s = jnp.where(col <= row, s, NEG)
    return s


ATT_HB = 8


def _causal_pairs(n):
    pairs = [(i, j) for i in range(n) for j in range(i + 1)]
    return (jnp.asarray([p[0] for p in pairs], jnp.int32), jnp.asarray([p[1] for p in pairs], jnp.int32))


def _head(ref_or_val, h, w):
    return ref_or_val[:, h * w:(h + 1) * w]


def _attn_fwd_call(q, k, vsrc, v_blk0):
    B, S, _ = q.shape
    t, n = _att_tiles(S)
    qi, kj = _causal_pairs(n)

    def body(qi_ref, kj_ref, q_ref, k_ref, v_ref, o_ref, lse_ref, m_sc, l_sc, acc_sc):
        p_id = pl.program_id(2)
        i, j = qi_ref[p_id], kj_ref[p_id]

        @pl.when(j == 0)
        def _():
            m_sc[...] = jnp.full(m_sc.shape, NEG, F32)
            l_sc[...] = jnp.zeros(l_sc.shape, F32)
            acc_sc[...] = jnp.zeros(acc_sc.shape, F32)

        def step(diagonal):
            qa, ka, va = q_ref[0], k_ref[0], v_ref[0]
            for h in range(ATT_HB):
                lanes = slice(h * LANE, (h + 1) * LANE)
                s = _scores(_head(qa, h, QK_PAD), _head(ka, h, QK_PAD), diagonal)
                m_prev = m_sc[:, lanes]
                m_new = jnp.maximum(m_prev, jnp.max(s, axis=1, keepdims=True))
                alpha = jnp.exp(m_prev - m_new)
                p = jnp.exp(s - jnp.tile(m_new, (1, t // LANE)))
                l_sc[:, lanes] = alpha * l_sc[:, lanes] + jnp.sum(p, axis=1, keepdims=True)
                acc_sc[:, lanes] = alpha * acc_sc[:, lanes] + jnp.dot(p.astype(BF16), _head(va, h, V_DIM),
                                                                      preferred_element_type=F32)
                m_sc[:, lanes] = m_new

        @pl.when(j < i)
        def _():
            step(False)

        @pl.when(j == i)
        def _():
            step(True)
            o_ref[0] = (acc_sc[...] / l_sc[...]).astype(BF16)
            lse_ref[0] = m_sc[...] + jnp.log(l_sc[...])

    wq, wv = ATT_HB * QK_PAD, ATT_HB * V_DIM
    grid_spec = pltpu.PrefetchScalarGridSpec(
        num_scalar_prefetch=2, grid=(B, N_HEADS // ATT_HB, qi.shape[0]),
        in_specs=[pl.BlockSpec((1, t, wq), lambda b, h, p, qi, kj: (b, qi[p], h)),
                  pl.BlockSpec((1, t, wq), lambda b, h, p, qi, kj: (b, kj[p], h)),
                  pl.BlockSpec((1, t, wv), lambda b, h, p, qi, kj: (b, kj[p], v_blk0 + h))],
        out_specs=[pl.BlockSpec((1, t, wv), lambda b, h, p, qi, kj: (b, qi[p], h)),
                   pl.BlockSpec((1, t, wv), lambda b, h, p, qi, kj: (b, qi[p], h))],
        scratch_shapes=[pltpu.VMEM((t, wv), F32), pltpu.VMEM((t, wv), F32), pltpu.VMEM((t, wv), F32)])
    return pl.pallas_call(
        body, grid_spec=grid_spec,
        out_shape=[jax.ShapeDtypeStruct((B, S, N_HEADS * V_DIM), BF16),
                   jax.ShapeDtypeStruct((B, S, N_HEADS * LANE), F32)],
        name="attn_fwd", compiler_params=_cparams(("parallel", "parallel", "arbitrary")))(qi, kj, q, k, vsrc)


def _attn_p_ds(q, k, v, o, do, lse, diagonal, t):
    s = _scores(q, k, diagonal)
    p = jnp.exp(s - jnp.tile(lse, (1, t // LANE)))
    dp = lax.dot_general(do.astype(BF16), v, (((1,), (1,)), ((), ())), preferred_element_type=F32)
    delta = jnp.sum(do.astype(F32) * o.astype(F32), axis=1, keepdims=True)
    ds = p * (dp - delta)
    return p, ds


ATT_HB_BWD = 4


def _attn_bwd_call(q, k, vsrc, o, do, lse):
    B, S, _ = q.shape
    t, n = _att_tiles(S)
    qi, kj = _causal_pairs(n)
    n_pairs = qi.shape[0]
    hb = ATT_HB_BWD
    v_blk0 = N_HEADS // hb

    def body(qi_ref, kj_ref, q_ref, k_ref, v_ref, o_ref, do_ref, lse_ref, dq_ref, dk_ref, dv_ref, dq_sc, dk_sc, dv_sc):
        p_id = pl.program_id(2)
        i, j = qi_ref[p_id], kj_ref[p_id]

        @pl.when(p_id == 0)
        def _():
            dk_sc[...] = jnp.zeros(dk_sc.shape, F32)
            dv_sc[...] = jnp.zeros(dv_sc.shape, F32)

        @pl.when(j == 0)
        def _():
            dq_sc[...] = jnp.zeros(dq_sc.shape, F32)

        rows = pl.ds(pl.multiple_of(j * t, t), t)

        def step(diagonal):
            qa, ka, va, oa, doa, la = q_ref[0], k_ref[0], v_ref[0], o_ref[0], do_ref[0], lse_ref[0]
            for h in range(hb):
                qb, kb, dob = _head(qa, h, QK_PAD), _head(ka, h, QK_PAD), _head(doa, h, V_DIM)
                p, ds = _attn_p_ds(qb, kb, _head(va, h, V_DIM), _head(oa, h, V_DIM), dob, _head(la, h, LANE),
                                   diagonal, t)
                dsb = ds.astype(BF16)
                dq_sc[:, h * QK_PAD:(h + 1) * QK_PAD] += jnp.dot(dsb, kb, preferred_element_type=F32)
                dv_sc[rows, h * V_DIM:(h + 1) * V_DIM] += lax.dot_general(
                    p.astype(BF16), dob.astype(BF16), (((0,), (0,)), ((), ())), preferred_element_type=F32)
                dk_sc[rows, h * QK_PAD:(h + 1) * QK_PAD] += lax.dot_general(
                    dsb, qb, (((0,), (0,)), ((), ())), preferred_element_type=F32)

        @pl.when(j < i)
        def _():
            step(False)

        @pl.when(j == i)
        def _():
            step(True)
            dq_ref[0] = dq_sc[...].astype(BF16)

        @pl.when(i == n - 1)
        def _():
            dk_ref[0] = dk_sc[rows, :].astype(BF16)
            dv_ref[0] = dv_sc[rows, :].astype(BF16)

    wq, wv = hb * QK_PAD, hb * V_DIM
    at_q = lambda b, h, p, qi, kj: (b, qi[p], h)
    at_k = lambda b, h, p, qi, kj: (b, kj[p], h)
    at_done = lambda b, h, p, qi, kj: (b, jnp.where(qi[p] == n - 1, kj[p], 0), h)
    grid_spec = pltpu.PrefetchScalarGridSpec(
        num_scalar_prefetch=2, grid=(B, N_HEADS // hb, n_pairs),
        in_specs=[pl.BlockSpec((1, t, wq), at_q), pl.BlockSpec((1, t, wq), at_k),
                  pl.BlockSpec((1, t, wv), lambda b, h, p, qi, kj: (b, kj[p], v_blk0 + h)),
                  pl.BlockSpec((1, t, wv), at_q), pl.BlockSpec((1, t, wv), at_q), pl.BlockSpec((1, t, wv), at_q)],
        out_specs=[pl.BlockSpec((1, t, wq), at_q), pl.BlockSpec((1, t, wq), at_done), pl.BlockSpec((1, t, wv), at_done)],
        scratch_shapes=[pltpu.VMEM((t, wq), F32), pltpu.VMEM((S, wq), F32), pltpu.VMEM((S, wv), F32)])
    return pl.pallas_call(
        body, grid_spec=grid_spec,
        out_shape=[jax.ShapeDtypeStruct((B, S, N_HEADS * QK_PAD), BF16),
                   jax.ShapeDtypeStruct((B, S, N_HEADS * QK_PAD), BF16),
                   jax.ShapeDtypeStruct((B, S, N_HEADS * V_DIM), BF16)],
        name="attn_bwd", compiler_params=_cparams(("parallel", "parallel", "arbitrary")))(
            qi, kj, q, k, vsrc, o, do, lse)


@jax.custom_vjp
def attention(q, kv, src, stand_in, tabs):
    return _attn_fwd_call(q, _build_k_fwd_call(kv, src, tabs), kv, N_HEADS // ATT_HB)[0]


def _attention_fwd(q, kv, src, stand_in, tabs):
    k = _build_k_fwd_call(kv, src, tabs)
    o, lse = _attn_fwd_call(q, k, kv, N_HEADS // ATT_HB)
    return o, (q, k, kv, o, lse, src, tabs)


def _attention_bwd(res, do):
    q, k, kv, o, lse, src, tabs = res
    dq, dk, dv = _attn_bwd_call(q, k, kv, o, do, lse)
    dk_nope, dk_rope = _build_k_bwd_call(dk, tabs)
    return (dq, jnp.concatenate([dk_nope, dv], axis=-1), jnp.zeros_like(src), dk_rope,
            tuple(jnp.zeros_like(t) for t in tabs))


attention.defvjp(_attention_fwd, _attention_bwd)


SUBLANES = 8
CONV_BLOCK = 2 * LANE


def _zero_tail(v):
    return jnp.concatenate([v, jnp.zeros((SUBLANES, v.shape[1]), v.dtype)], axis=0)


def _shift_down(vz, sh):
    return pltpu.roll(vz, sh, 0)[:vz.shape[0] - SUBLANES]


def _shift_up(vz, sh):
    return pltpu.roll(vz, vz.shape[0] - sh, 0)[:vz.shape[0] - SUBLANES]


def _conv_pre(u, uz, w_ref, b_ref):
    acc = b_ref[...] + w_ref[pl.ds(CONV_K - 1, 1), :] * u
    for k in range(CONV_K - 1):
        acc = acc + w_ref[pl.ds(k, 1), :] * _shift_down(uz, CONV_K - 1 - k)
    return acc


def _conv_fwd_call(src, w, b):
    B, S, _ = src.shape
    C = w.shape[1]

    def body(u_ref, w_ref, b_ref, o_ref):
        uu = u_ref[0].astype(F32)
        o_ref[0] = _silu(_conv_pre(uu, _zero_tail(uu), w_ref, b_ref))

    spec = pl.BlockSpec((1, S, CONV_BLOCK), lambda c, bb: (bb, 0, c))
    return pl.pallas_call(
        body, grid=(C // CONV_BLOCK, B),
        in_specs=[pl.BlockSpec((1, S, CONV_BLOCK), lambda c, bb: (bb, 0, c + CONV_LANE0 // CONV_BLOCK)),
                  pl.BlockSpec((CONV_K, CONV_BLOCK), lambda c, bb: (0, c)),
                  pl.BlockSpec((1, CONV_BLOCK), lambda c, bb: (0, c))],
        out_specs=spec, out_shape=jax.ShapeDtypeStruct((B, S, C), F32), name="conv_fwd",
        compiler_params=_cparams(("parallel", "arbitrary")))(src, w, b)


def _conv_bwd_call(src, w, b, g):
    B, S, _ = src.shape
    C = w.shape[1]

    def body(u_ref, w_ref, b_ref, g_ref, du_ref, dw_ref, db_ref):
        uu = u_ref[0].astype(F32)
        uz = _zero_tail(uu)
        pre = _conv_pre(uu, uz, w_ref, b_ref)
        sg = lax.logistic(pre)
        dpre = g_ref[0] * sg * (1.0 + pre * (1.0 - sg))
        dz = _zero_tail(dpre)
        du = w_ref[pl.ds(CONV_K - 1, 1), :] * dpre
        dws = [None] * CONV_K
        dws[CONV_K - 1] = jnp.sum(dpre * uu, axis=0, keepdims=True)
        for k in range(CONV_K - 1):
            sh = CONV_K - 1 - k
            du = du + w_ref[pl.ds(k, 1), :] * _shift_up(dz, sh)
            dws[k] = jnp.sum(dpre * _shift_down(uz, sh), axis=0, keepdims=True)
        du_ref[0] = du.astype(du_ref.dtype)
        dbv = jnp.sum(dpre, axis=0, keepdims=True)
        first = pl.program_id(1) == 0

        @pl.when(first)
        def _():
            for k in range(CONV_K):
                dw_ref[pl.ds(k, 1), :] = dws[k]
            db_ref[...] = dbv

        @pl.when(jnp.logical_not(first))
        def _():
            for k in range(CONV_K):
                dw_ref[pl.ds(k, 1), :] += dws[k]
            db_ref[...] += dbv

    spec = pl.BlockSpec((1, S, CONV_BLOCK), lambda c, bb: (bb, 0, c))
    wspec = pl.BlockSpec((CONV_K, CONV_BLOCK), lambda c, bb: (0, c))
    bspec = pl.BlockSpec((1, CONV_BLOCK), lambda c, bb: (0, c))
    uspec = pl.BlockSpec((1, S, CONV_BLOCK), lambda c, bb: (bb, 0, c + CONV_LANE0 // CONV_BLOCK))
    return pl.pallas_call(
        body, grid=(C // CONV_BLOCK, B), in_specs=[uspec, wspec, bspec, spec], out_specs=[spec, wspec, bspec],
        out_shape=[jax.ShapeDtypeStruct((B, S, C), BF16), jax.ShapeDtypeStruct(w.shape, F32),
                   jax.ShapeDtypeStruct(b.shape, F32)],
        name="conv_bwd", compiler_params=_cparams(("parallel", "arbitrary")))(src, w, b, g)


@jax.custom_vjp
def conv_silu(src, stand_in, w, b):
    return _conv_fwd_call(src, w, b)


def _conv_silu_fwd(src, stand_in, w, b):
    return _conv_fwd_call(src, w, b), (src, w, b)


def _conv_silu_bwd(res, g):
    du, dw, db = _conv_bwd_call(*res, g)
    return jnp.zeros_like(res[0]), du, dw, db


conv_silu.defvjp(_conv_silu_fwd, _conv_silu_bwd)


def _chunk_cumsum_call(a, reverse, name):
    B, S, W = a.shape
    per_step = min(S // CHUNK, 32)

    def body(a_ref, o_ref):
        r = lax.broadcasted_iota(jnp.int32, (CHUNK, CHUNK), 0)
        c = lax.broadcasted_iota(jnp.int32, (CHUNK, CHUNK), 1)
        tri = jnp.where((c >= r) if reverse else (c <= r), 1.0, 0.0).astype(F32)
        for i in range(per_step):
            rows = pl.ds(i * CHUNK, CHUNK)
            o_ref[0, rows, :] = jnp.dot(tri, a_ref[0, rows, :], preferred_element_type=F32,
                                        precision=lax.Precision.HIGHEST)

    spec = pl.BlockSpec((1, per_step * CHUNK, W), lambda b, c: (b, c, 0))
    return pl.pallas_call(body, grid=(B, S // (per_step * CHUNK)), in_specs=[spec], out_specs=spec,
                          out_shape=jax.ShapeDtypeStruct(a.shape, F32), name=name,
                          compiler_params=_cparams(("parallel", "parallel")))(a)


@jax.custom_vjp
def chunk_cumsum(a):
    return _chunk_cumsum_call(a, False, "chunk_cumsum_fwd")


chunk_cumsum.defvjp(lambda a: (_chunk_cumsum_call(a, False, "chunk_cumsum_fwd"), None),
                    lambda _, g: (_chunk_cumsum_call(g, True, "chunk_cumsum_bwd"),))


GROUP_W = 4 * HEAD_P
HPG = SSM_HEADS // SSM_GROUPS


def _ssd_masks():
    lane = lax.broadcasted_iota(jnp.int32, (1, GROUP_W), 1)
    return [((lane >= HEAD_P * j) & (lane < HEAD_P * (j + 1))).astype(F32) for j in range(HPG)]


def _ssd_decays(ac_cols, acr_ref, gi):
    r = lax.broadcasted_iota(jnp.int32, (CHUNK, CHUNK), 0)
    c = lax.broadcasted_iota(jnp.int32, (CHUNK, CHUNK), 1)
    return [jnp.exp(jnp.where(c <= r, ac_cols[j] - acr_ref[0, gi * HPG + j], NEG)) for j in range(HPG)]


def _ssd_cols(blk, g):
    lane = lax.broadcasted_iota(jnp.int32, blk.shape, 1)
    return [jnp.sum(jnp.where(lane == HPG * g + j, blk, 0.0), axis=1, keepdims=True) for j in range(HPG)]


def _ssd_spread(cols):
    lane = lax.broadcasted_iota(jnp.int32, (1, GROUP_W), 1)
    out = jnp.broadcast_to(cols[HPG - 1], (CHUNK, GROUP_W))
    for j in range(HPG - 2, -1, -1):
        out = jnp.where(lane < HEAD_P * (j + 1), cols[j], out)
    return out


def _ssd_gather(val, cols, masks, g):
    lane = lax.broadcasted_iota(jnp.int32, (1, LANE), 1)
    out = jnp.zeros((CHUNK, LANE), F32)
    for j in range(HPG):
        tot = jnp.sum(val * masks[j], axis=1, keepdims=True)
        if cols is not None:
            tot = tot + cols[j]
        out = out + tot * (lane == HPG * g + j).astype(F32)
    return out


def _dot(a, b, dims):
    return lax.dot_general(a.astype(BF16), b.astype(BF16), (dims, ((), ())), preferred_element_type=F32)


NN = ((1,), (0,))
NT = ((1,), (1,))
TN = ((0,), (0,))


XBC_W = GROUP_W + 2 * STATE_N


SSD_STEP_GROUPS_FWD = 8
SSD_STEP_GROUPS_BWD = 8


def _ssd_load(xbc_ref, dt_ref, ac_ref, masks, g, gi):
    x = xbc_ref[0, :, gi * XBC_W:gi * XBC_W + GROUP_W]
    bm = xbc_ref[0, :, gi * XBC_W + GROUP_W:gi * XBC_W + GROUP_W + STATE_N]
    cm = xbc_ref[0, :, gi * XBC_W + GROUP_W + STATE_N:(gi + 1) * XBC_W]
    ac_cols = _ssd_cols(ac_ref[0], g)
    dt = _ssd_spread(_ssd_cols(dt_ref[0], g))
    ac = _ssd_spread(ac_cols)
    is_last = (lax.broadcasted_iota(jnp.int32, (CHUNK, GROUP_W), 0) == CHUNK - 1).astype(F32)
    return x, bm, cm, dt, ac, ac_cols, is_last


def _ssd_in_specs(nc, rev, gb):
    cc = (lambda c: nc - 1 - c) if rev else (lambda c: c)
    return [pl.BlockSpec((1, CHUNK, gb * XBC_W), lambda b, g, c: (b, cc(c), g)),
            pl.BlockSpec((1, CHUNK, LANE), lambda b, g, c: (b, cc(c), 0)),
            pl.BlockSpec((1, CHUNK, LANE), lambda b, g, c: (b, cc(c), 0)),
            pl.BlockSpec((1, gb * HPG, 1, CHUNK), lambda b, g, c: (b, g, 0, cc(c))),
            pl.BlockSpec((1, gb * GROUP_W), lambda b, g, c: (0, g))]


def _ssd_fwd_call(xbc, dtp, acp, acr, dsk):
    B, S, _ = xbc.shape
    nc = S // CHUNK
    gb = SSD_STEP_GROUPS_FWD

    def body(xbc_ref, dt_ref, ac_ref, ar_ref, ds_ref, y_ref, hp_ref, h_sc):
        @pl.when(pl.program_id(2) == 0)
        def _():
            h_sc[...] = jnp.zeros(h_sc.shape, F32)

        masks = _ssd_masks()
        ys = []
        for gi in range(gb):
            grp = gb * pl.program_id(1) + gi
            x, bm, cm, dt, ac, ac_cols, is_last = _ssd_load(xbc_ref, dt_ref, ac_ref, masks, grp, gi)
            last = jnp.sum(ac * is_last, axis=0, keepdims=True)
            decays = _ssd_decays(ac_cols, ar_ref, gi)
            xd = x * dt
            cb = _dot(cm, bm, NT)
            hprev = h_sc[gi]
            hp_ref[0, gi, 0] = hprev
            y = _dot(cm, hprev, NN) * jnp.exp(ac) + ds_ref[:, gi * GROUP_W:(gi + 1) * GROUP_W] * x
            y = y + _dot(jnp.concatenate([cb * d for d in decays], axis=1),
                         jnp.concatenate([xd * m for m in masks], axis=0), NN)
            ys.append(y)
            h_sc[gi] = hprev * jnp.exp(last) + _dot(bm, xd * jnp.exp(last - ac), TN)
        y_ref[0] = jnp.concatenate(ys, axis=1)

    ng = SSM_GROUPS // gb
    return pl.pallas_call(
        body, grid=(B, ng, nc), in_specs=_ssd_in_specs(nc, False, gb),
        out_specs=[pl.BlockSpec((1, CHUNK, gb * GROUP_W), lambda b, g, c: (b, c, g)),
                   pl.BlockSpec((1, gb, 1, STATE_N, GROUP_W), lambda b, g, c: (b, g, c, 0, 0))],
        out_shape=[jax.ShapeDtypeStruct((B, S, D_INNER), F32),
                   jax.ShapeDtypeStruct((B, SSM_GROUPS, nc, STATE_N, GROUP_W), F32)],
        scratch_shapes=[pltpu.VMEM((gb, STATE_N, GROUP_W), F32)], name="ssd_fwd",
        compiler_params=_cparams(("parallel", "parallel", "arbitrary")))(xbc, dtp, acp, acr, dsk)


def _ssd_bwd_call(xbc, dtp, acp, acr, dsk, hps, dy):
    B, S, _ = xbc.shape
    nc = S // CHUNK
    gb = SSD_STEP_GROUPS_BWD

    def body(xbc_ref, dt_ref, ac_ref, ar_ref, ds_ref, hp_ref, dy_ref,
             dxbc_ref, ddt_ref, dac_ref, dar_ref, dds_ref, dh_sc):
        first = pl.program_id(2) == 0

        @pl.when(first)
        def _():
            dh_sc[...] = jnp.zeros(dh_sc.shape, F32)

        masks = _ssd_masks()
        dxbc_parts, dds_parts = [], []
        for gi in range(gb):
            grp = gb * pl.program_id(0) + gi
            x, bm, cm, dt, ac, ac_cols, is_last = _ssd_load(xbc_ref, dt_ref, ac_ref, masks, grp, gi)
            last = jnp.sum(ac * is_last, axis=0, keepdims=True)
            g = dy_ref[0, :, gi * GROUP_W:(gi + 1) * GROUP_W]
            hprev = hp_ref[0, gi, 0]
            dh = dh_sc[gi]
            decays = _ssd_decays(ac_cols, ar_ref, gi)
            dcols = []
            xd = x * dt
            cb = _dot(cm, bm, NT)
            e_c = jnp.exp(ac)
            e_end = jnp.exp(last - ac)
            e_last = jnp.exp(last)
            z = _dot(cm, hprev, NN)
            dz = g * e_c
            dac = g * z * e_c
            dc = _dot(dz, hprev, NT)
            dhprev = _dot(cm, dz, TN) + dh * e_last
            dcb = jnp.zeros((CHUNK, CHUNK), F32)
            gjs = [cb * d for d in decays]
            g_heads = jnp.concatenate([g * m for m in masks], axis=0)
            dg_heads = _dot(g_heads, xd, NT)
            dxd = _dot(jnp.concatenate(gjs, axis=0), g_heads, TN)
            for j in range(HPG):
                gj = gjs[j]
                dgj = dg_heads[j * CHUNK:(j + 1) * CHUNK]
                dcb = dcb + dgj * decays[j]
                dseg = dgj * gj
                dcols.append(jnp.sum(dseg, axis=1, keepdims=True))
                dar_ref[0, gi * HPG + j] = -jnp.sum(dseg, axis=0, keepdims=True)
            dc = dc + _dot(dcb, bm, NN)
            db = _dot(dcb, cm, TN)
            sx = xd * e_end
            db = db + _dot(sx, dh, NT)
            dsx = _dot(bm, dh, NN)
            dxd = dxd + dsx * e_end
            de = dsx * sx
            dac = dac - de
            dlast = jnp.sum(de, axis=0, keepdims=True) + jnp.sum(dh * hprev, axis=0, keepdims=True) * e_last
            dsk = ds_ref[:, gi * GROUP_W:(gi + 1) * GROUP_W]
            dxbc_parts += [dxd * dt + dsk * g, db, dc]
            ddt_ref[0, gi] = _ssd_gather(dxd * x, None, masks, grp)
            dac_ref[0, gi] = _ssd_gather(dac + is_last * dlast, dcols, masks, grp)
            dds_parts.append(jnp.sum(g * x, axis=0, keepdims=True))
            dh_sc[gi] = dhprev
        dxbc_ref[0] = jnp.concatenate(dxbc_parts, axis=1)
        dds = jnp.concatenate(dds_parts, axis=1)
        first_all = first & (pl.program_id(1) == 0)

        @pl.when(first_all)
        def _():
            dds_ref[...] = dds

        @pl.when(jnp.logical_not(first_all))
        def _():
            dds_ref[...] += dds

    rc = lambda c: nc - 1 - c
    ng = SSM_GROUPS // gb
    in_specs = [pl.BlockSpec(s.block_shape, (lambda g, b, c, f=s.index_map: f(b, g, c))) for s in _ssd_in_specs(nc, True, gb)]
    in_specs.append(pl.BlockSpec((1, gb, 1, STATE_N, GROUP_W), lambda g, b, c: (b, g, rc(c), 0, 0)))
    in_specs.append(pl.BlockSpec((1, CHUNK, gb * GROUP_W), lambda g, b, c: (b, rc(c), g)))
    per_group = pl.BlockSpec((1, gb, CHUNK, LANE), lambda g, b, c: (b, g, rc(c), 0))
    out_specs = [pl.BlockSpec((1, CHUNK, gb * XBC_W), lambda g, b, c: (b, rc(c), g)), per_group, per_group,
                 pl.BlockSpec((1, gb * HPG, 1, CHUNK), lambda g, b, c: (b, g, 0, rc(c))),
                 pl.BlockSpec((1, gb * GROUP_W), lambda g, b, c: (0, g))]
    out_shape = [jax.ShapeDtypeStruct(xbc.shape, F32),
                 jax.ShapeDtypeStruct((B, SSM_GROUPS, S, LANE), F32), jax.ShapeDtypeStruct((B, SSM_GROUPS, S, LANE), F32),
                 jax.ShapeDtypeStruct(acr.shape, F32), jax.ShapeDtypeStruct(dsk.shape, F32)]
    return pl.pallas_call(
        body, grid=(ng, B, nc), in_specs=in_specs, out_specs=out_specs, out_shape=out_shape,
        scratch_shapes=[pltpu.VMEM((gb, STATE_N, GROUP_W), F32)], name="ssd_bwd",
        compiler_params=_cparams(("arbitrary", "arbitrary", "arbitrary")))(xbc, dtp, acp, acr, dsk, hps, dy)


@jax.custom_vjp
def ssd(xbc, dtp, acp, acr, dsk):
    return _ssd_fwd_call(xbc, dtp, acp, acr, dsk)[0]


def _ssd_fwd(xbc, dtp, acp, acr, dsk):
    y, hps = _ssd_fwd_call(xbc, dtp, acp, acr, dsk)
    return y, (xbc, dtp, acp, acr, dsk, hps)


def _ssd_bwd(res, dy):
    dxbc, ddt, dac, dacr, dds = _ssd_bwd_call(*res, dy)
    return dxbc, jnp.sum(ddt, axis=1), jnp.sum(dac, axis=1), dacr, dds


ssd.defvjp(_ssd_fwd, _ssd_bwd)


def _pack_small(arrs):
    flat = jnp.concatenate([a.reshape(-1) for a in arrs])
    rows = -(-flat.shape[0] // (8 * LANE)) * 8
    return jnp.pad(flat, (0, rows * LANE - flat.shape[0])).reshape(rows, LANE)


def _unpack_small(buf, shapes):
    flat = buf.reshape(-1)
    out, off = [], 0
    for shp in shapes:
        n = int(np.prod(shp))
        out.append(flat[off:off + n].reshape(shp))
        off += n
    return out


def _rows_tile(rows, cap):
    for cand in range(min(rows, cap), 7, -8):
        if rows % cand == 0:
            return cand
    return rows


def _pair_sum(mine, theirs, cidx, name):
    n4, kk, nn = mine.shape
    half = kk // 2
    tr = _rows_tile(half, 256)
    nb = half // tr

    def body(c_ref, a_ref, b_ref, o_ref, ob_ref):
        tot = a_ref[...] + b_ref[...]
        o_ref[...] = tot
        ob_ref[...] = tot.astype(BF16)

    spec = pl.BlockSpec((1, tr, nn), lambda j, i, c: (j, i, 0))
    grid_spec = pltpu.PrefetchScalarGridSpec(
        num_scalar_prefetch=1, grid=(n4, nb),
        in_specs=[pl.BlockSpec((1, tr, nn), lambda j, i, c: (j, c[0] * nb + i, 0)), spec], out_specs=[spec, spec])
    return pl.pallas_call(
        body, grid_spec=grid_spec,
        out_shape=[jax.ShapeDtypeStruct((n4, half, nn), F32), jax.ShapeDtypeStruct((n4, half, nn), BF16)],
        name=name, compiler_params=_cparams(("parallel", "parallel")))(cidx, mine, theirs)


def _chip_sum(quad, pair, chip_idx, name):
    _, rows, nn = quad.shape
    tr = _rows_tile(rows, 256)

    def body(s_ref, q_ref, p_ref, o_ref):
        for mine in range(4):
            @pl.when(s_ref[0] == mine)
            def _(mine=mine):
                acc = None
                for d in range(4):
                    term = p_ref[0] if d == mine else q_ref[d].astype(F32)
                    acc = term if acc is None else acc + term
                o_ref[...] = acc

    grid_spec = pltpu.PrefetchScalarGridSpec(
        num_scalar_prefetch=1, grid=(rows // tr,),
        in_specs=[pl.BlockSpec((4, tr, nn), lambda i, s: (0, i, 0)), pl.BlockSpec((1, tr, nn), lambda i, s: (s[0], i, 0))],
        out_specs=pl.BlockSpec((tr, nn), lambda i, s: (i, 0)))
    return pl.pallas_call(body, grid_spec=grid_spec, out_shape=jax.ShapeDtypeStruct((rows, nn), F32), name=name,
                          compiler_params=_cparams(("parallel",)))(chip_idx, quad, pair)


def _adam_halves_call(w, mine, other, cidx, m, v, name):
    _, rows, nn = w.shape
    half = rows // 2
    tr = _rows_tile(half, 128)
    nb = half // tr

    def body(c_ref, w_ref, a_ref, b_ref, m_ref, v_ref, g_ref, d_ref, nm_ref, nv_ref):
        upper = (pl.program_id(0) >= nb).astype(jnp.int32)
        g = jnp.where(upper == c_ref[0], a_ref[...], b_ref[...])
        g_ref[0] = g
        d_ref[0], nm_ref[0], nv_ref[0] = _adam_fn(w_ref[0], g, m_ref[0], v_ref[0])

    spec = pl.BlockSpec((1, tr, nn), lambda i, c: (0, i, 0))
    hspec = pl.BlockSpec((tr, nn), lambda i, c: (i % nb, 0))
    grid_spec = pltpu.PrefetchScalarGridSpec(num_scalar_prefetch=1, grid=(2 * nb,),
                                             in_specs=[spec, hspec, hspec, spec, spec], out_specs=[spec] * 4)
    return pl.pallas_call(body, grid_spec=grid_spec, out_shape=[jax.ShapeDtypeStruct(w.shape, F32)] * 4, name=name,
                          compiler_params=_cparams(("parallel",)))(cidx, w, mine, other, m, v)


def _stack_sum(stack, name):
    n, rows, nn = stack.shape
    tr = _rows_tile(rows, 256)

    def body(s_ref, o_ref):
        acc = s_ref[0]
        for d in range(1, n):
            acc = acc + s_ref[d]
        o_ref[...] = acc

    return pl.pallas_call(
        body, grid=(rows // tr,), in_specs=[pl.BlockSpec((n, tr, nn), lambda i: (0, i, 0))],
        out_specs=pl.BlockSpec((tr, nn), lambda i: (i, 0)), out_shape=jax.ShapeDtypeStruct((rows, nn), F32),
        name=name, compiler_params=_cparams(("parallel",)))(stack)


def _adam_call(w, g, m, v, name):
    rows, nn = w.shape
    tr = _rows_tile(rows, 128)

    def body(w_ref, g_ref, m_ref, v_ref, d_ref, nm_ref, nv_ref):
        d_ref[...], nm_ref[...], nv_ref[...] = _adam_fn(w_ref[...], g_ref[...], m_ref[...], v_ref[...])

    spec = pl.BlockSpec((tr, nn), lambda i: (i, 0))
    sds = jax.ShapeDtypeStruct((rows, nn), F32)
    return pl.pallas_call(body, grid=(rows // tr,), in_specs=[spec] * 4, out_specs=[spec] * 3,
                          out_shape=[sds] * 3, name=name, compiler_params=_cparams(("parallel",)))(w, g, m, v)


def _adam_fn(w, g, m, v):
    m = ADAM_B1 * m + (1.0 - ADAM_B1) * g
    v = ADAM_B2 * v + (1.0 - ADAM_B2) * (g * g)
    m_hat = m / (1.0 - ADAM_B1 ** ADAM_STEP)
    v_hat = v / (1.0 - ADAM_B2 ** ADAM_STEP)
    delta = -ADAM_LR * (m_hat / (jnp.sqrt(v_hat) + ADAM_EPS) + ADAM_WD * w)
    return delta, m, v


def _mesh_pos():
    return lax.axis_index("x"), lax.axis_index("y"), lax.axis_index("c")


def _other_chips(x, y):
    return [(1 - x, y), (x, 1 - y), (1 - x, 1 - y)]


HBM_SPEC = pl.BlockSpec(memory_space=pl.ANY)


def _remote(src, dst, send_sems, recv_sems, k, to):
    return pltpu.make_async_remote_copy(src_ref=src, dst_ref=dst, send_sem=send_sems.at[k], recv_sem=recv_sems.at[k],
                                        device_id=to, device_id_type=MESH)


def _half_rows(c, rows, align):
    half = rows // 2
    return (pl.ds(pl.multiple_of(c * half, align), half), pl.ds(pl.multiple_of((1 - c) * half, align), half))


def _gather_weights(mats, conv):
    n = len(mats)

    def body(*refs):
        ins, conv_in = refs[:n], refs[n]
        outs, conv_out = refs[n + 1:2 * n + 1], refs[2 * n + 1]
        send_sems, recv_sems, local_sem = refs[2 * n + 2:]
        x, y, c = _mesh_pos()
        me, sibling, s = (x, y, c), (x, y, 1 - c), 2 * x + y
        chips = _other_chips(x, y)
        rows = [_half_rows(c, m.shape[0], 16) for m in mats]
        own = pltpu.make_async_copy(conv_in, conv_out.at[s], local_sem)
        own.start()
        sent = []
        for i in range(n):
            mine = rows[i][0]
            for j, (cx, cy) in enumerate(chips):
                sent.append(_remote(ins[i].at[mine], outs[i].at[s, mine], send_sems, recv_sems, 6 * i + j, (cx, cy, c)))
        for j, (cx, cy) in enumerate(chips):
            sent.append(_remote(conv_in, conv_out.at[s], send_sems, recv_sems, 6 * n + j, (cx, cy, c)))
        for cp in sent:
            cp.start()
        for i in range(n):
            mine = rows[i][0]
            for j, (cx, cy) in enumerate(chips):
                landed = outs[i].at[2 * cx + cy, mine]
                _remote(landed, landed, send_sems, recv_sems, 6 * i + j, me).wait_recv()
                fwd = _remote(landed, landed, send_sems, recv_sems, 6 * i + 3 + j, sibling)
                fwd.start()
                sent.append(fwd)
        for j, (cx, cy) in enumerate(chips):
            slot = conv_out.at[2 * cx + cy]
            _remote(slot, slot, send_sems, recv_sems, 6 * n + j, me).wait_recv()
        for i in range(n):
            theirs_rows = rows[i][1]
            for j, (cx, cy) in enumerate(chips):
                theirs = outs[i].at[2 * cx + cy, theirs_rows]
                _remote(theirs, theirs, send_sems, recv_sems, 6 * i + 3 + j, me).wait_recv()
        for cp in sent:
            cp.wait_send()
        own.wait()

    out_shape = [jax.ShapeDtypeStruct((4,) + m.shape, m.dtype) for m in mats]
    out_shape.append(jax.ShapeDtypeStruct((4,) + conv.shape, conv.dtype))
    res = pl.pallas_call(
        body, in_specs=[HBM_SPEC] * (n + 1), out_specs=[HBM_SPEC] * (n + 1), out_shape=out_shape,
        scratch_shapes=[pltpu.SemaphoreType.DMA((6 * n + 3,)), pltpu.SemaphoreType.DMA((6 * n + 3,)),
                        pltpu.SemaphoreType.DMA],
        name="all_gather_weights")(*mats, conv)
    return res[:n], res[n]


def _sibling_exchange(stacks):
    n = len(stacks)

    def body(*refs):
        ins, outs = refs[:n], refs[n:2 * n]
        send_sems, recv_sems = refs[2 * n:]
        x, y, c = _mesh_pos()
        cps = []
        for i in range(n):
            theirs = _half_rows(c, stacks[i].shape[1], 8)[1]
            cps.append(_remote(ins[i].at[:, theirs, :], outs[i], send_sems, recv_sems, i, (x, y, 1 - c)))
        for cp in cps:
            cp.start()
        for cp in cps:
            cp.wait()

    out_shape = [jax.ShapeDtypeStruct((4, s.shape[1] // 2, s.shape[2]), s.dtype) for s in stacks]
    return pl.pallas_call(
        body, in_specs=[HBM_SPEC] * n, out_specs=[HBM_SPEC] * n, out_shape=out_shape,
        scratch_shapes=[pltpu.SemaphoreType.DMA((n,)), pltpu.SemaphoreType.DMA((n,))],
        name="grad_sibling_exchange")(*stacks)


def _chip_exchange(parts):
    n = len(parts)

    def body(*refs):
        ins, outs = refs[:n], refs[n:2 * n]
        send_sems, recv_sems = refs[2 * n:]
        x, y, c = _mesh_pos()
        me, s = (x, y, c), 2 * x + y
        chips = _other_chips(x, y)
        sent = [_remote(ins[i].at[2 * cx + cy], outs[i].at[s], send_sems, recv_sems, 3 * i + j, (cx, cy, c))
                for i in range(n) for j, (cx, cy) in enumerate(chips)]
        for cp in sent:
            cp.start()
        for i in range(n):
            for j, (cx, cy) in enumerate(chips):
                slot = outs[i].at[2 * cx + cy]
                _remote(slot, slot, send_sems, recv_sems, 3 * i + j, me).wait_recv()
        for cp in sent:
            cp.wait_send()

    return pl.pallas_call(
        body, in_specs=[HBM_SPEC] * n, out_specs=[HBM_SPEC] * n,
        out_shape=[jax.ShapeDtypeStruct(p.shape, p.dtype) for p in parts],
        scratch_shapes=[pltpu.SemaphoreType.DMA((3 * n,)), pltpu.SemaphoreType.DMA((3 * n,))],
        name="grad_chip_exchange")(*parts)


def _sibling_swap(halves):
    n = len(halves)

    def body(*refs):
        ins, outs = refs[:n], refs[n:2 * n]
        send_sems, recv_sems = refs[2 * n:]
        x, y, c = _mesh_pos()
        cps = [_remote(ins[i], outs[i], send_sems, recv_sems, i, (x, y, 1 - c)) for i in range(n)]
        for cp in cps:
            cp.start()
        for cp in cps:
            cp.wait()

    return pl.pallas_call(
        body, in_specs=[HBM_SPEC] * n, out_specs=[HBM_SPEC] * n,
        out_shape=[jax.ShapeDtypeStruct(h.shape, h.dtype) for h in halves],
        scratch_shapes=[pltpu.SemaphoreType.DMA((n,)), pltpu.SemaphoreType.DMA((n,))],
        name="grad_sibling_swap")(*halves)


def _gather_small(vec):
    def body(in_ref, out_ref, send_sems, recv_sems, local_sem):
        x, y, c = _mesh_pos()
        me = (x, y, c)
        own = pltpu.make_async_copy(in_ref, out_ref.at[4 * x + 2 * y + c], local_sem)
        own.start()
        peers = [(1 - x if k & 4 else x, 1 - y if k & 2 else y, 1 - c if k & 1 else c) for k in range(1, 8)]
        sent = [_remote(in_ref, out_ref.at[4 * x + 2 * y + c], send_sems, recv_sems, k, p) for k, p in enumerate(peers)]
        for cp in sent:
            cp.start()
        for k, (px, py, pc) in enumerate(peers):
            slot = out_ref.at[4 * px + 2 * py + pc]
            _remote(slot, slot, send_sems, recv_sems, k, me).wait_recv()
        for cp in sent:
            cp.wait_send()
        own.wait()

    return pl.pallas_call(
        body, in_specs=[HBM_SPEC], out_specs=HBM_SPEC, out_shape=jax.ShapeDtypeStruct((8,) + vec.shape, vec.dtype),
        scratch_shapes=[pltpu.SemaphoreType.DMA((7,)), pltpu.SemaphoreType.DMA((7,)), pltpu.SemaphoreType.DMA],
        name="grad_gather_small")(vec)


def _reduce_matrices(stacks, names):
    cidx = lax.axis_index("c").astype(jnp.int32).reshape(1)
    chip = (2 * lax.axis_index("x") + lax.axis_index("y")).astype(jnp.int32).reshape(1)
    got = _sibling_exchange(stacks)
    pairs = [_pair_sum(a, b, cidx, "grad_pair_sum_" + nm) for a, b, nm in zip(stacks, got, names)]
    quads = _chip_exchange([p[1] for p in pairs])
    mine = [_chip_sum(q, p[0], chip, "grad_chip_sum_" + nm) for q, p, nm in zip(quads, pairs, names)]
    return mine, _sibling_swap(mine)


def _pad_cols(a, n):
    return jnp.concatenate([a, jnp.zeros((a.shape[0], n - a.shape[1]), a.dtype)], axis=1)


def _group_channels(a):
    lead = a.shape[:-1]
    xs = a[..., :D_INNER].reshape(lead + (SSM_GROUPS, GROUP_W))
    bs = a[..., D_INNER:D_INNER + SSM_GROUPS * STATE_N].reshape(lead + (SSM_GROUPS, STATE_N))
    cs = a[..., D_INNER + SSM_GROUPS * STATE_N:].reshape(lead + (SSM_GROUPS, STATE_N))
    return jnp.concatenate([xs, bs, cs], axis=-1).reshape(lead + (CONV_CH,))


PROJ_SEGS = (('gate_a', D_MODEL), ('gate_b', D_MODEL), ('z', D_INNER), ('xbc', CONV_CH), ('q_lat', Q_RANK),
             ('kv_lat', KV_RANK), ('k_rope', LANE), ('dt', LANE))
PROJ_WIDE = sum(w for _, w in PROJ_SEGS[:4])
PROJ_LANE0 = {n: (v if v < PROJ_WIDE else v - PROJ_WIDE) for n, v in
              zip([n for n, _ in PROJ_SEGS], [int(v) for v in np.cumsum([0] + [w for _, w in PROJ_SEGS])[:-1]])}
CONV_LANE0 = PROJ_LANE0['xbc']
KR_LANE0 = PROJ_LANE0['k_rope']


def _lay_w_in(w):
    idx = np.cumsum(IN_SIZES)[:-1]
    q_lat, kv_lat, k_rope, z, xbc, dt, gate_a, gate_b = jnp.split(w, [int(v) for v in idx], axis=1)
    return jnp.concatenate([gate_a, gate_b, z, _group_channels(xbc), q_lat, kv_lat, _pad_cols(k_rope, LANE),
                            _pad_cols(dt, LANE)], axis=1)


@jax.custom_vjp
def project(h, w, tok):
    return _project_impl(h, w)


def _project_impl(h, w):
    return (_mm(h, w[:, :PROJ_WIDE], "w_in_fwd", BF16), _mm(h, w[:, PROJ_WIDE:], "w_in_narrow_fwd")) + tuple(
        jnp.zeros((h.shape[0], wd), BF16) for _, wd in PROJ_SEGS)


def _project_fwd(h, w, tok):
    return _project_impl(h, w), (h, w)


def _project_bwd(res, cots):
    h, w = res
    g = jnp.concatenate(cots[2:], axis=1)
    return _mm(g, w.T, "w_in_dx", h.dtype), jnp.zeros_like(w), _mm(h.T, g, "w_in_dw")


project.defvjp(_project_fwd, _project_bwd)


def _lay_w_uq(w):
    w3 = w.reshape(Q_RANK, N_HEADS, NOPE + ROPE)
    w3 = jnp.concatenate([w3, jnp.zeros((Q_RANK, N_HEADS, QK_PAD - NOPE - ROPE), w.dtype)], axis=2)
    return w3.reshape(Q_RANK, N_HEADS * QK_PAD)


def _lay_w_ukv(w):
    w3 = w.reshape(KV_RANK, N_HEADS, NOPE + V_DIM)
    return jnp.concatenate([w3[:, :, :NOPE].reshape(KV_RANK, -1), w3[:, :, NOPE:].reshape(KV_RANK, -1)], axis=1)


def _pad_lanes(v, n=LANE):
    return jnp.concatenate([v, jnp.zeros((v.shape[0], n - v.shape[1]), v.dtype)], axis=1)


def _local_loss(toks, small, x, wb, c8, posf, target):
    B, S, D = x.shape
    T = B * S

    def lin(name, a, key, lay=lambda w: w, out_dtype=F32):
        return make_linear(name, out_dtype)(a, lay(wb[key]), lay(toks[key]))

    rows2 = lambda a: a.reshape(T, a.shape[-1])
    rows3 = lambda a: a.reshape(B, S, a.shape[-1])

    sc = make_rowwise("silu_c", _f_silu, 1, 0, 0, ('row',))((c8[None],), (), ())[0][0]
    mod = make_linear("ada", F32, 4)(sc, wb['w_ada'], toks['w_ada'])[:B] + small['b_ada']
    shift1, scale1, gate1, shift2, scale2, gate2 = [m[:, None, :] for m in jnp.split(mod, 6, axis=-1)]

    h, x_res = make_rowwise("modulate1", _f_modulate, 1, 2, 1, ('row',), forward_row=0, ts_cap=1024)(
        (x,), (scale1, shift1), (small['g_pre_mix'],))
    outs = project(rows2(h), _lay_w_in(wb['w_in']), _lay_w_in(toks['w_in']))
    wide = lax.stop_gradient(rows3(outs[0]))
    proj = lax.stop_gradient(rows3(outs[1]))
    stand = {n: rows3(o) for (n, _), o in zip(PROJ_SEGS, outs[2:])}

    def win(seg, block):
        return (PROJ_LANE0[seg] // block, dict(PROJ_SEGS)[seg])

    inv = ROPE_THETA ** (-jnp.arange(ROPE // 2, dtype=F32) / (ROPE // 2))
    inv_lane = jnp.concatenate([inv, inv, jnp.zeros((LANE - ROPE,), F32)])[None]
    tabs = tuple(_rope_tables(posf, inv_lane))
    qn = make_rowwise("rms_q", _f_rms, 1, 0, 1, ('row',), ts_cap=4096, windows={0: win('q_lat', Q_RANK)})(
        (proj,), (), (small['g_q_lat'],), (stand['q_lat'],))[0]
    kvn = make_rowwise("rms_kv", _f_rms, 1, 0, 1, ('row',), ts_cap=4096, windows={0: win('kv_lat', KV_RANK)})(
        (proj,), (), (small['g_kv_lat'],), (stand['kv_lat'],))[0]
    qp = rows3(lin("w_uq", rows2(qn), 'w_uq', _lay_w_uq, BF16))
    kvp = rows3(lin("w_ukv", rows2(kvn), 'w_ukv', _lay_w_ukv, BF16))
    qr = rope_q(qp, tabs)
    att = attention(qr, kvp, proj, stand['k_rope'], tabs)
    attn = rows3(lin("w_o_attn", rows2(att), 'w_o_attn', out_dtype=BF16))

    xa = conv_silu(wide, stand['xbc'], _group_channels(wb['conv_w_f32']), _group_channels(small['conv_b']))
    dt_pad, a_pad = make_rowwise("dt_softplus", _f_dt, 1, 0, 2, ('row', 'row'), ts_cap=4096,
                                 windows={0: win('dt', LANE)})(
        (proj,), (), (_pad_lanes(small['dt_bias']), _pad_lanes(small['a_log'])), (stand['dt'],))
    ac_pad = chunk_cumsum(a_pad)
    acr = jnp.transpose(ac_pad[..., :SSM_HEADS], (0, 2, 1))[:, :, None, :]
    dsk = jnp.repeat(small['d_skip'], HEAD_P, axis=-1)
    y = ssd(xa, dt_pad, ac_pad, acr, dsk)
    yg = make_rowwise("gated_norm", _f_gated_norm, 2, 0, 1, ('row',), ncol=SSM_GROUPS, ts_cap=4096,
                      windows={1: win('z', GROUP_W)})((y, wide), (), (small['g_ssm_out'],), (stand['z'],))[0]
    ssm = rows3(lin("w_o_ssm", rows2(yg), 'w_o_ssm', out_dtype=BF16))

    merged = make_rowwise("merge", _f_merge, 4, 0, 0, ('row',), ts_cap=1024,
                          windows={2: win('gate_a', D_MODEL), 3: win('gate_b', D_MODEL)})(
        (attn, ssm, wide, wide), (), (), (stand['gate_a'], stand['gate_b']))[0]
    mix = rows3(lin("w_out", rows2(merged), 'w_out', out_dtype=BF16))
    x1 = make_rowwise("post_mix", _f_post, 2, 1, 1, ('row',), ts_cap=1024)(
        (x_res, mix), (gate1,), (small['g_post_mix'],))[0]

    h2, x1_res = make_rowwise("modulate2", _f_modulate, 1, 2, 1, ('row',), forward_row=0, ts_cap=1024)(
        (x1,), (scale2, shift2), (small['g_pre_mlp'],))
    ff = rows3(ffn(rows2(h2), wb['w_ff1'], toks['w_ff1'], wb['w_ff2'], toks['w_ff2']))
    lvec = make_rowwise("final_loss", _f_final_loss, 3, 1, 1, ('sum',), nodiff=(2,), ts_cap=1024)(
        (x1_res, ff, target), (gate2,), (small['g_post_mlp'],))[0]
    return jnp.sum(lvec)


MATRICES = COL_SHARDED + ROW_SHARDED
STACKED_DW = ('w_ada', 'w_ff1')


def _local_step(x, c, positions, target, wb, small):
    B = x.shape[0]
    c8 = jnp.concatenate([c, jnp.zeros((16 - B, c.shape[1]), F32)], axis=0)
    posf = positions.astype(F32)[..., None]
    toks = {k: jnp.zeros(wb[k].shape, F32) for k in MATRICES if k != 'conv_w'}
    for k in STACKED_DW:
        rows, cols = wb[k].shape
        toks[k] = jnp.zeros((4, rows, cols // 4), F32)
    conv_w = wb['conv_w_f32']

    def loss_fn(toks, small, conv_w, x):
        wbl = dict(wb)
        wbl['conv_w_f32'] = conv_w
        return _local_loss(toks, small, x, wbl, c8, posf, target)

    loss, (g_tok, g_small, g_conv, g_x) = jax.value_and_grad(loss_fn, argnums=(0, 1, 2, 3))(toks, small, conv_w, x)
    grads = dict(g_tok)
    grads.update(g_small)
    grads['conv_w'] = g_conv
    return loss, g_x, grads


def kernel(x, c, positions, w_ada, b_ada, g_pre_mix, g_post_mix, w_in, g_q_lat, g_kv_lat, w_uq, w_ukv, w_o_attn, conv_w, conv_b, dt_bias, a_log, d_skip, g_ssm_out, w_o_ssm, w_out, g_pre_mlp, g_post_mlp, w_ff1, w_ff2, loss_target, m_w_ada, m_b_ada, m_g_pre_mix, m_g_post_mix, m_w_in, m_g_q_lat, m_g_kv_lat, m_w_uq, m_w_ukv, m_w_o_attn, m_conv_w, m_conv_b, m_dt_bias, m_a_log, m_d_skip, m_g_ssm_out, m_w_o_ssm, m_w_out, m_g_pre_mlp, m_g_post_mlp, m_w_ff1, m_w_ff2, v_w_ada, v_b_ada, v_g_pre_mix, v_g_post_mix, v_w_in, v_g_q_lat, v_g_kv_lat, v_w_uq, v_w_ukv, v_w_o_attn, v_conv_w, v_conv_b, v_dt_bias, v_a_log, v_d_skip, v_g_ssm_out, v_w_o_ssm, v_w_out, v_g_pre_mlp, v_g_post_mlp, v_w_ff1, v_w_ff2):
    given = dict(locals())
    w_loc = {n: given[n] for n in WEIGHTS}
    m_loc = {n: given["m_" + n] for n in WEIGHTS}
    v_loc = {n: given["v_" + n] for n in WEIGHTS}
    mats = [n for n in WEIGHTS if n in MATRICES and n != 'conv_w']
    vecs = [n for n in WEIGHTS if n not in MATRICES]

    own = [w_loc[n][0].astype(BF16) for n in mats]
    g_mats, g_conv = _gather_weights(own, conv_w[0])
    chip = 2 * lax.axis_index("x") + lax.axis_index("y")
    wb = {}
    for n, g, mine in zip(mats, g_mats, own):
        g = lax.dynamic_update_slice_in_dim(g, mine[None], chip, axis=0)
        if n in COL_SHARDED:
            wb[n] = jnp.transpose(g, (1, 0, 2)).reshape(g.shape[1], -1)
        else:
            wb[n] = g.reshape(-1, g.shape[2])
    wb['conv_w_f32'] = jnp.transpose(g_conv, (1, 0, 2)).reshape(CONV_K, -1)
    small = {n: w_loc[n] for n in vecs}

    loss_part, grad_x, grads = _local_step(x, c, positions, loss_target, wb, small)
    loss = lax.psum(loss_part, ("x", "y", "c"))

    stacks = []
    for n in mats:
        kk, nn = w_loc[n].shape[1:]
        if n in STACKED_DW:
            stacks.append(grads[n])
        elif n in COL_SHARDED:
            stacks.append(jnp.transpose(grads[n].reshape(kk, 4, nn), (1, 0, 2)))
        else:
            stacks.append(grads[n].reshape(4, kk, nn))
    g_mine, g_other = _reduce_matrices(stacks, mats)
    g_shard = {}

    vec_shapes = [tuple(grads[n].shape) for n in vecs] + [tuple(grads['conv_w'].shape)]
    total = _stack_sum(_gather_small(_pack_small([grads[n] for n in vecs] + [grads['conv_w']])), "grad_sum_small")
    g_vec = _unpack_small(total, vec_shapes)
    n_conv = conv_w.shape[2]
    chip = 2 * lax.axis_index("x") + lax.axis_index("y")
    g_shard['conv_w'] = lax.dynamic_slice_in_dim(g_vec[-1], chip * n_conv, n_conv, axis=1)
    for n, g in zip(vecs, g_vec):
        g_shard[n] = g

    delta, new_m, new_v = {}, {}, {}
    cidx = lax.axis_index("c").astype(jnp.int32).reshape(1)
    for n, mine, other in zip(mats, g_mine, g_other):
        g_shard[n], delta[n], new_m[n], new_v[n] = _adam_halves_call(
            w_loc[n], mine, other, cidx, m_loc[n], v_loc[n], "adamw_" + n)
    rest = vecs + ['conv_w']
    rest_shapes = [tuple(w_loc[n].shape) for n in rest]
    packed = [_pack_small([src[n] for n in rest]) for src in (w_loc, g_shard, m_loc, v_loc)]
    for dst, buf in zip((delta, new_m, new_v), _adam_call(*packed, "adamw_small")):
        dst.update(zip(rest, _unpack_small(buf, rest_shapes)))

    def out(d):
        return [d[n].reshape(w_loc[n].shape) for n in WEIGHTS]

    return (loss, grad_x, *out(g_shard), *out(delta), *out(new_m), *out(new_v))
```

```python
import numpy as np
import jax
import jax.numpy as jnp
from jax import lax
from jax.experimental import pallas as pl
from jax.experimental.pallas import tpu as pltpu

F32 = jnp.float32
BF16 = jnp.bfloat16
MESH = pl.DeviceIdType.MESH

D_MODEL = 1024
N_HEADS = 8
NOPE = 128
ROPE = 64
V_DIM = 128
Q_RANK = 256
KV_RANK = 256
ROPE_THETA = 10000.0
D_INNER = 2048
SSM_HEADS = 32
SSM_GROUPS = 8
HEAD_P = 64
STATE_N = 128
CONV_K = 4
CHUNK = 128
CONV_CH = D_INNER + 2 * SSM_GROUPS * STATE_N
EPS = 1e-6
IN_SIZES = (Q_RANK, KV_RANK, ROPE, D_INNER, CONV_CH, SSM_HEADS, D_MODEL, D_MODEL)
ADAM_LR, ADAM_B1, ADAM_B2, ADAM_EPS, ADAM_WD, ADAM_STEP = 0.001, 0.9, 0.999, 1e-08, 0.01, 10

VMEM_LIMIT_BYTES = 52 * 1024 * 1024
LANE = 128
QK_PAD = 256

WEIGHTS = ['w_ada', 'b_ada', 'g_pre_mix', 'g_post_mix', 'w_in', 'g_q_lat', 'g_kv_lat', 'w_uq', 'w_ukv',
           'w_o_attn', 'conv_w', 'conv_b', 'dt_bias', 'a_log', 'd_skip', 'g_ssm_out', 'w_o_ssm', 'w_out',
           'g_pre_mlp', 'g_post_mlp', 'w_ff1', 'w_ff2']
COL_SHARDED = ('w_ada', 'w_in', 'w_uq', 'w_ukv', 'conv_w', 'w_ff1')
ROW_SHARDED = ('w_o_attn', 'w_o_ssm', 'w_out', 'w_ff2')


def _cparams(sem):
    return pltpu.CompilerParams(dimension_semantics=sem, vmem_limit_bytes=VMEM_LIMIT_BYTES)


def _tile(n, cap):
    if n <= cap:
        return n
    k = n // LANE
    best = LANE
    for d in range(1, k + 1):
        if k % d == 0 and d * LANE <= cap:
            best = d * LANE
    return best


def _mm(a, w, name, out_dtype=F32, epilogue=None, extras=(), out_dtypes=None):
    M, K = a.shape
    N = w.shape[1]
    tm = min(M, 2048 if K <= 1024 else 1024)
    tn = _tile(N, 1024)
    tk = _tile(K, 2048)
    nk = K // tk
    dts = tuple(out_dtypes) if epilogue is not None else (out_dtype,)
    n_x, n_o = len(extras), len(dts)

    def finish(acc, refs):
        res = epilogue(acc, *[r[...] for r in refs[:n_x]]) if epilogue is not None else (acc,)
        for o_ref, val, dt in zip(refs[n_x:n_x + n_o], res, dts):
            o_ref[...] = val.astype(dt)

    def body(a_ref, w_ref, *refs):
        part = jnp.dot(a_ref[...].astype(BF16), w_ref[...], preferred_element_type=F32)
        if nk == 1:
            finish(part, refs)
        else:
            acc_ref = refs[-1]
            k = pl.program_id(2)

            @pl.when(k == 0)
            def _():
                acc_ref[...] = part

            @pl.when(k > 0)
            def _():
                acc_ref[...] += part

            @pl.when(k == nk - 1)
            def _():
                finish(acc_ref[...], refs)

    ospec = pl.BlockSpec((tm, tn), lambda i, j, k: (i, j))
    res = pl.pallas_call(
        body, grid=(M // tm, N // tn, nk),
        in_specs=[pl.BlockSpec((tm, tk), lambda i, j, k: (i, k)), pl.BlockSpec((tk, tn), lambda i, j, k: (k, j))]
        + [ospec] * n_x,
        out_specs=[ospec] * n_o, out_shape=[jax.ShapeDtypeStruct((M, N), dt) for dt in dts],
        scratch_shapes=[pltpu.VMEM((tm, tn), F32)] if nk > 1 else [], name=name,
        compiler_params=_cparams(("parallel", "parallel", "arbitrary")))(a, w, *extras)
    return res if epilogue is not None else res[0]


def _mm_tn(a, g, name, col_shards=1):
    M, K = a.shape
    N = g.shape[1]
    tm = min(M, 1024)
    tk = _tile(K, 1024)
    tn = _tile(N // col_shards, 1024)
    nm = M // tm
    per = N // col_shards // tn

    def body(a_ref, g_ref, o_ref):
        part = lax.dot_general(a_ref[...].astype(BF16), g_ref[...].astype(BF16), (((0,), (0,)), ((), ())),
                               preferred_element_type=F32)
        m = pl.program_id(2)

        @pl.when(m == 0)
        def _():
            o_ref[...] = part.reshape(o_ref.shape)

        @pl.when(m > 0)
        def _():
            o_ref[...] += part.reshape(o_ref.shape)

    if col_shards == 1:
        out_spec = pl.BlockSpec((tk, tn), lambda i, j, m: (i, j))
        out_shape = jax.ShapeDtypeStruct((K, N), F32)
    else:
        out_spec = pl.BlockSpec((1, tk, tn), lambda i, j, m: (j // per, i, j % per))
        out_shape = jax.ShapeDtypeStruct((col_shards, K, N // col_shards), F32)
    return pl.pallas_call(
        body, grid=(K // tk, N // tn, nm),
        in_specs=[pl.BlockSpec((tm, tk), lambda i, j, m: (m, i)), pl.BlockSpec((tm, tn), lambda i, j, m: (m, j))],
        out_specs=out_spec, out_shape=out_shape, name=name,
        compiler_params=_cparams(("parallel", "parallel", "arbitrary")))(a, g)


def make_linear(name, out_dtype=F32, dw_col_shards=1):
    @jax.custom_vjp
    def linear(a, w, tok):
        return _mm(a, w, name + "_fwd", out_dtype)

    def fwd(a, w, tok):
        return _mm(a, w, name + "_fwd", out_dtype), (a, w)

    def bwd(res, g):
        a, w = res
        da = _mm(g, w.T, name + "_dx", a.dtype)
        dw = _mm_tn(a, g, name + "_dw", dw_col_shards)
        return da, jnp.zeros_like(w), dw

    linear.defvjp(fwd, bwd)
    return linear


def _relu2_epilogue(acc):
    r = jnp.maximum(acc, 0.0)
    return r * r, r


def _relu2_bwd_epilogue(acc, r):
    return (acc * (2.0 * r.astype(F32)),)


@jax.custom_vjp
def ffn(h, w1, tok1, w2, tok2):
    act, _ = _mm(h, w1, "w_ff1_fwd", epilogue=_relu2_epilogue, out_dtypes=(BF16, BF16))
    return _mm(act, w2, "w_ff2_fwd", BF16)


def _ffn_fwd(h, w1, tok1, w2, tok2):
    act, r = _mm(h, w1, "w_ff1_fwd", epilogue=_relu2_epilogue, out_dtypes=(BF16, BF16))
    return _mm(act, w2, "w_ff2_fwd", BF16), (h, w1, w2, act, r)


def _ffn_bwd(res, g):
    h, w1, w2, act, r = res
    du = _mm(g, w2.T, "w_ff2_dx", epilogue=_relu2_bwd_epilogue, extras=(r,), out_dtypes=(BF16,))[0]
    dw2 = _mm_tn(act, g, "w_ff2_dw")
    dw1 = _mm_tn(h, du, "w_ff1_dw", 4)
    dh = _mm(du, w1.T, "w_ff1_dx", h.dtype)
    return dh, jnp.zeros_like(w1), dw1, jnp.zeros_like(w2), dw2


ffn.defvjp(_ffn_fwd, _ffn_bwd)


def make_rowwise(name, f, n_rows, n_seqs, n_pars, out_kinds, ncol=1, nodiff=(), ts_cap=512, windows=None,
                 forward_row=None):
    windows = dict(windows or {})
    n_in = n_rows + n_seqs + n_pars
    diff_idx = [i for i in range(n_in) if i not in nodiff]

    def _dims(rows):
        B, S = rows[0].shape[0], rows[0].shape[1]
        ts = min(S, ts_cap)
        return B, S, ts

    def _width(i, r):
        return windows[i][1] if i in windows else r.shape[2]

    def _in_specs(rows, seqs, pars, ts):
        specs = []
        for i, r in enumerate(rows):
            col0 = windows[i][0] if i in windows else 0
            specs.append(pl.BlockSpec((1, ts, _width(i, r) // ncol), lambda k, b, s, col0=col0: (b, s, k + col0)))
        for q in seqs:
            specs.append(pl.BlockSpec((1, 1, q.shape[2] // ncol), lambda k, b, s: (b, 0, k)))
        for p in pars:
            specs.append(pl.BlockSpec((1, p.shape[1] // ncol), lambda k, b, s: (0, k)))
        return specs

    def _load(refs):
        vals = [r[0] for r in refs[:n_rows + n_seqs]]
        vals += [r[...] for r in refs[n_rows + n_seqs:n_in]]
        return vals

    def _out_struct(rows, seqs, pars, ts):
        blocks = [jax.ShapeDtypeStruct((ts, _width(i, r) // ncol), r.dtype) for i, r in enumerate(rows)]
        blocks += [jax.ShapeDtypeStruct((1, q.shape[2] // ncol), q.dtype) for q in seqs]
        blocks += [jax.ShapeDtypeStruct((1, p.shape[1] // ncol), p.dtype) for p in pars]
        return jax.eval_shape(f, *blocks)

    def _fwd_call(rows, seqs, pars):
        B, S, ts = _dims(rows)
        outs = _out_struct(rows, seqs, pars, ts)
        n_out = len(outs)

        def body(*refs):
            res = f(*_load(refs))
            first = (pl.program_id(1) == 0) & (pl.program_id(2) == 0)
            for o_ref, val, kind in zip(refs[n_in:], res, out_kinds):
                if kind == 'row':
                    o_ref[0] = val
                else:
                    tot = jnp.sum(val, axis=0, keepdims=True)

                    @pl.when(first)
                    def _(o_ref=o_ref, tot=tot):
                        o_ref[...] = tot

                    @pl.when(jnp.logical_not(first))
                    def _(o_ref=o_ref, tot=tot):
                        o_ref[...] += tot

        out_shape, out_specs = [], []
        for o, kind in zip(outs, out_kinds):
            d = o.shape[1]
            if kind == 'row':
                out_shape.append(jax.ShapeDtypeStruct((B, S, ncol * d), o.dtype))
                out_specs.append(pl.BlockSpec((1, ts, d), lambda k, b, s: (b, s, k)))
            else:
                out_shape.append(jax.ShapeDtypeStruct((1, ncol * d), o.dtype))
                out_specs.append(pl.BlockSpec((1, d), lambda k, b, s: (0, k)))
        res = pl.pallas_call(
            body, grid=(ncol, B, S // ts), in_specs=_in_specs(rows, seqs, pars, ts), out_specs=out_specs,
            out_shape=out_shape, name=name + "_fwd",
            compiler_params=_cparams(("arbitrary", "arbitrary", "arbitrary")))(*rows, *seqs, *pars)
        return tuple(res)

    def _bwd_call(rows, seqs, pars, cots, carried=None):
        B, S, ts = _dims(rows)
        outs = _out_struct(rows, seqs, pars, ts)
        n_out = len(outs)
        all_in = list(rows) + list(seqs) + list(pars)
        extra = [] if carried is None else [carried]

        def body(*refs):
            vals = _load(refs)
            if carried is not None:
                carried_ref, refs = refs[n_in + n_out], refs[:n_in + n_out] + refs[n_in + n_out + 1:]
            cts = []
            for c_ref, o, kind in zip(refs[n_in:n_in + n_out], outs, out_kinds):
                if kind == 'row':
                    cts.append(c_ref[0])
                else:
                    cts.append(jnp.broadcast_to(c_ref[...], o.shape))

            def g(*dv):
                full = list(vals)
                for i, v in zip(diff_idx, dv):
                    full[i] = v
                return tuple(f(*full))

            _, vjp = jax.vjp(g, *[vals[i] for i in diff_idx])
            grads = vjp(tuple(cts))
            b, s = pl.program_id(1), pl.program_id(2)
            for o_ref, i, gr in zip(refs[n_in + n_out:], diff_idx, grads):
                if i < n_rows:
                    if carried is not None and i == forward_row:
                        gr = gr + carried_ref[0]
                    o_ref[0] = gr.astype(o_ref.dtype)
                else:
                    first = (s == 0) if i < n_rows + n_seqs else ((b == 0) & (s == 0))
                    target = (lambda r: r.at[0]) if i < n_rows + n_seqs else (lambda r: r)

                    @pl.when(first)
                    def _(o_ref=o_ref, gr=gr, target=target):
                        target(o_ref)[...] = gr

                    @pl.when(jnp.logical_not(first))
                    def _(o_ref=o_ref, gr=gr, target=target):
                        target(o_ref)[...] += gr

        cot_specs = []
        for o, kind in zip(outs, out_kinds):
            d = o.shape[1]
            if kind == 'row':
                cot_specs.append(pl.BlockSpec((1, ts, d), lambda k, b, s: (b, s, k)))
            else:
                cot_specs.append(pl.BlockSpec((1, d), lambda k, b, s: (0, k)))
        out_shape, out_specs = [], []
        for i in diff_idx:
            a = all_in[i]
            if i < n_rows:
                out_shape.append(jax.ShapeDtypeStruct((B, S, _width(i, a)), BF16 if i in windows else a.dtype))
                out_specs.append(pl.BlockSpec((1, ts, _width(i, a) // ncol), lambda k, b, s: (b, s, k)))
                continue
            out_shape.append(jax.ShapeDtypeStruct(a.shape, a.dtype))
            if i < n_rows + n_seqs:
                out_specs.append(pl.BlockSpec((1, 1, a.shape[2] // ncol), lambda k, b, s: (b, 0, k)))
            else:
                out_specs.append(pl.BlockSpec((1, a.shape[1] // ncol), lambda k, b, s: (0, k)))
        if carried is not None:
            cot_specs.append(pl.BlockSpec((1, ts, carried.shape[2] // ncol), lambda k, b, s: (b, s, k)))
        res = pl.pallas_call(
            body, grid=(ncol, B, S // ts), in_specs=_in_specs(rows, seqs, pars, ts) + cot_specs,
            out_specs=out_specs, out_shape=out_shape, name=name + "_bwd",
            compiler_params=_cparams(("arbitrary", "arbitrary", "arbitrary")))(*all_in, *cots, *extra)
        grads = [None] * n_in
        for i, r in zip(diff_idx, res):
            grads[i] = r
        for i in nodiff:
            grads[i] = jnp.zeros_like(all_in[i])
        stand_in_grads = tuple(grads[i] for i in sorted(windows))
        for i in windows:
            grads[i] = jnp.zeros_like(all_in[i])
        return (tuple(grads[:n_rows]), tuple(grads[n_rows:n_rows + n_seqs]), tuple(grads[n_rows + n_seqs:]),
                stand_in_grads)

    def _outputs(rows, seqs, pars):
        res = _fwd_call(rows, seqs, pars)
        return res if forward_row is None else res + (rows[forward_row],)

    @jax.custom_vjp
    def op(rows, seqs, pars, stand_ins):
        return _outputs(rows, seqs, pars)

    def fwd(rows, seqs, pars, stand_ins):
        return _outputs(rows, seqs, pars), (rows, seqs, pars)

    def bwd(res, cots):
        rows, seqs, pars = res
        if forward_row is None:
            return _bwd_call(rows, seqs, pars, cots)
        return _bwd_call(rows, seqs, pars, cots[:-1], cots[-1])

    op.defvjp(fwd, bwd)
    return lambda rows, seqs, pars, stand_ins=(): op(tuple(rows), tuple(seqs), tuple(pars), tuple(stand_ins))


def _rms(x, g):
    x = x.astype(F32)
    return x * lax.rsqrt(jnp.mean(x * x, axis=-1, keepdims=True) + EPS) * g


def _silu(x):
    return x * lax.logistic(x)


def _f_silu(c):
    return (_silu(c),)


def _f_modulate(x, scale, shift, g):
    return ((_rms(x, g) * (1.0 + scale) + shift).astype(BF16),)


def _f_rms(x, g):
    return (_rms(x, g).astype(BF16),)


def _f_dt(dt_raw, dt_bias, a_log):
    z = dt_raw + dt_bias
    dt = jnp.maximum(z, 0.0) + jnp.log1p(jnp.exp(-jnp.abs(z)))
    return dt, dt * (-jnp.exp(a_log))


def _f_gated_norm(y, z, g):
    return (_rms(y * _silu(z.astype(F32)), g).astype(BF16),)


def _f_merge(attn, ssm, ga, gb):
    return ((lax.logistic(ga.astype(F32)) * attn + lax.logistic(gb.astype(F32)) * ssm).astype(BF16),)


def _f_post(x, m, gate, g):
    return (x + gate * _rms(m, g),)


def _f_final_loss(x, ff, target, gate, g):
    e = x + gate * _rms(ff, g) - target
    return (e * e * (0.5 / D_MODEL),)


def _rope_tables(posf, inv_lane):
    B, S, _ = posf.shape
    ts = min(S, 512)

    def body(p_ref, inv_ref, c_ref, a_ref, b_ref):
        ang = p_ref[0] * inv_ref[...]
        cs, sn = jnp.cos(ang), jnp.sin(ang)
        lane = lax.broadcasted_iota(jnp.int32, ang.shape, 1)
        c_ref[0] = jnp.where(lane < ROPE, cs, 0.0)
        a_ref[0] = jnp.where(lane < ROPE // 2, -sn, 0.0)
        b_ref[0] = jnp.where((lane >= ROPE // 2) & (lane < ROPE), sn, 0.0)

    spec = pl.BlockSpec((1, ts, LANE), lambda b, s: (b, s, 0))
    sds = jax.ShapeDtypeStruct((B, S, LANE), F32)
    return pl.pallas_call(
        body, grid=(B, S // ts),
        in_specs=[pl.BlockSpec((1, ts, 1), lambda b, s: (b, s, 0)), pl.BlockSpec((1, LANE), lambda b, s: (0, 0))],
        out_specs=[spec, spec, spec], out_shape=[sds, sds, sds], name="rope_tables",
        compiler_params=_cparams(("parallel", "parallel")))(posf, inv_lane)


def _rot(u, c, a, bm):
    return u * c + pltpu.roll(u, 96, 1) * a + pltpu.roll(u, 32, 1) * bm


def _rot_t(g, c, a, bm):
    return g * c + pltpu.roll(g * a, 32, 1) + pltpu.roll(g * bm, 96, 1)


def _rope_q_call(q, tabs, transpose, name):
    B, S, W = q.shape
    ts = min(S, 1024)
    fn = _rot_t if transpose else _rot
    out_dtype = BF16

    def body(q_ref, c_ref, a_ref, b_ref, o_ref):
        tc, ta, tb = c_ref[0], a_ref[0], b_ref[0]
        for h in range(W // QK_PAD):
            u = q_ref[0, :, h * QK_PAD:(h + 1) * QK_PAD].astype(F32) * ATT_SCALE
            r = fn(u[:, NOPE:], tc, ta, tb)
            o_ref[0, :, h * QK_PAD:(h + 1) * QK_PAD] = jnp.concatenate([u[:, :NOPE], r], axis=1).astype(out_dtype)

    tspec = pl.BlockSpec((1, ts, LANE), lambda b, s: (b, s, 0))
    qspec = pl.BlockSpec((1, ts, W), lambda b, s: (b, s, 0))
    return pl.pallas_call(
        body, grid=(B, S // ts), in_specs=[qspec, tspec, tspec, tspec], out_specs=qspec,
        out_shape=jax.ShapeDtypeStruct(q.shape, out_dtype), name=name,
        compiler_params=_cparams(("parallel", "parallel")))(q, *tabs)


@jax.custom_vjp
def rope_q(q, tabs):
    return _rope_q_call(q, tabs, False, "rope_q_fwd")


def _rope_q_fwd(q, tabs):
    return _rope_q_call(q, tabs, False, "rope_q_fwd"), tabs


def _rope_q_bwd(tabs, g):
    return _rope_q_call(g, tabs, True, "rope_q_bwd"), tuple(jnp.zeros_like(t) for t in tabs)


rope_q.defvjp(_rope_q_fwd, _rope_q_bwd)


def _build_k_fwd_call(kv, kr, tabs):
    B, S, _ = kv.shape
    ts = min(S, 1024)

    def body(kv_ref, kr_ref, c_ref, a_ref, b_ref, o_ref):
        r = _rot(kr_ref[0], c_ref[0], a_ref[0], b_ref[0]).astype(BF16)
        for h in range(N_HEADS):
            o_ref[0, :, h * QK_PAD:(h + 1) * QK_PAD] = jnp.concatenate(
                [kv_ref[0, :, h * NOPE:(h + 1) * NOPE], r], axis=1)

    tspec = pl.BlockSpec((1, ts, LANE), lambda b, s: (b, s, 0))
    kr_spec = pl.BlockSpec((1, ts, LANE), lambda b, s: (b, s, KR_LANE0 // LANE))
    return pl.pallas_call(
        body, grid=(B, S // ts),
        in_specs=[pl.BlockSpec((1, ts, N_HEADS * NOPE), lambda b, s: (b, s, 0)), kr_spec, tspec, tspec, tspec],
        out_specs=pl.BlockSpec((1, ts, N_HEADS * QK_PAD), lambda b, s: (b, s, 0)),
        out_shape=jax.ShapeDtypeStruct((B, S, N_HEADS * QK_PAD), BF16), name="build_k_fwd",
        compiler_params=_cparams(("parallel", "parallel")))(kv, kr, *tabs)


def _build_k_bwd_call(g, tabs):
    B, S, _ = g.shape
    ts = min(S, 1024)

    def body(g_ref, c_ref, a_ref, b_ref, dk_ref, dr_ref):
        tot = None
        for h in range(N_HEADS):
            dk_ref[0, :, h * NOPE:(h + 1) * NOPE] = g_ref[0, :, h * QK_PAD:h * QK_PAD + NOPE]
            part = g_ref[0, :, h * QK_PAD + NOPE:(h + 1) * QK_PAD].astype(F32)
            tot = part if tot is None else tot + part
        dr_ref[0] = _rot_t(tot, c_ref[0], a_ref[0], b_ref[0]).astype(BF16)

    tspec = pl.BlockSpec((1, ts, LANE), lambda b, s: (b, s, 0))
    return pl.pallas_call(
        body, grid=(B, S // ts),
        in_specs=[pl.BlockSpec((1, ts, N_HEADS * QK_PAD), lambda b, s: (b, s, 0)), tspec, tspec, tspec],
        out_specs=[pl.BlockSpec((1, ts, N_HEADS * NOPE), lambda b, s: (b, s, 0)), tspec],
        out_shape=[jax.ShapeDtypeStruct((B, S, N_HEADS * NOPE), BF16), jax.ShapeDtypeStruct((B, S, LANE), BF16)],
        name="build_k_bwd", compiler_params=_cparams(("parallel", "parallel")))(g, *tabs)


ATT_SCALE = (NOPE + ROPE) ** -0.5
NEG = -1e30


def _att_tiles(S):
    t = min(S, 512)
    return t, S // t


def _scores(q, k, diagonal):
    s = lax.dot_general(q, k, (((1,), (1,)), ((), ())), preferred_element_type=F32)
    if diagonal:
        row = lax.broadcasted_iota(jnp.int32, s.shape, 0)
        col = lax.broadcasted_iota(jnp.int32, s.shape, 1)
        s = jnp.where(col <= row, s, NEG)
    return s


ATT_HB = 8


def _causal_pairs(n):
    pairs = [(i, j) for i in range(n) for j in range(i + 1)]
    return (jnp.asarray([p[0] for p in pairs], jnp.int32), jnp.asarray([p[1] for p in pairs], jnp.int32))


def _head(ref_or_val, h, w):
    return ref_or_val[:, h * w:(h + 1) * w]


def _attn_fwd_call(q, k, vsrc, v_blk0):
    B, S, _ = q.shape
    t, n = _att_tiles(S)
    qi, kj = _causal_pairs(n)

    def body(qi_ref, kj_ref, q_ref, k_ref, v_ref, o_ref, lse_ref, m_sc, l_sc, acc_sc):
        p_id = pl.program_id(2)
        i, j = qi_ref[p_id], kj_ref[p_id]

        @pl.when(j == 0)
        def _():
            m_sc[...] = jnp.full(m_sc.shape, NEG, F32)
            l_sc[...] = jnp.zeros(l_sc.shape, F32)
            acc_sc[...] = jnp.zeros(acc_sc.shape, F32)

        def step(diagonal):
            qa, ka, va = q_ref[0], k_ref[0], v_ref[0]
            for h in range(ATT_HB):
                lanes = slice(h * LANE, (h + 1) * LANE)
                s = _scores(_head(qa, h, QK_PAD), _head(ka, h, QK_PAD), diagonal)
                m_prev = m_sc[:, lanes]
                m_new = jnp.maximum(m_prev, jnp.max(s, axis=1, keepdims=True))
                alpha = jnp.exp(m_prev - m_new)
                p = jnp.exp(s - jnp.tile(m_new, (1, t // LANE)))
                l_sc[:, lanes] = alpha * l_sc[:, lanes] + jnp.sum(p, axis=1, keepdims=True)
                acc_sc[:, lanes] = alpha * acc_sc[:, lanes] + jnp.dot(p.astype(BF16), _head(va, h, V_DIM),
                                                                      preferred_element_type=F32)
                m_sc[:, lanes] = m_new

        @pl.when(j < i)
        def _():
            step(False)

        @pl.when(j == i)
        def _():
            step(True)
            o_ref[0] = (acc_sc[...] / l_sc[...]).astype(BF16)
            lse_ref[0] = m_sc[...] + jnp.log(l_sc[...])

    wq, wv = ATT_HB * QK_PAD, ATT_HB * V_DIM
    grid_spec = pltpu.PrefetchScalarGridSpec(
        num_scalar_prefetch=2, grid=(B, N_HEADS // ATT_HB, qi.shape[0]),
        in_specs=[pl.BlockSpec((1, t, wq), lambda b, h, p, qi, kj: (b, qi[p], h)),
                  pl.BlockSpec((1, t, wq), lambda b, h, p, qi, kj: (b, kj[p], h)),
                  pl.BlockSpec((1, t, wv), lambda b, h, p, qi, kj: (b, kj[p], v_blk0 + h))],
        out_specs=[pl.BlockSpec((1, t, wv), lambda b, h, p, qi, kj: (b, qi[p], h)),
                   pl.BlockSpec((1, t, wv), lambda b, h, p, qi, kj: (b, qi[p], h))],
        scratch_shapes=[pltpu.VMEM((t, wv), F32), pltpu.VMEM((t, wv), F32), pltpu.VMEM((t, wv), F32)])
    return pl.pallas_call(
        body, grid_spec=grid_spec,
        out_shape=[jax.ShapeDtypeStruct((B, S, N_HEADS * V_DIM), BF16),
                   jax.ShapeDtypeStruct((B, S, N_HEADS * LANE), F32)],
        name="attn_fwd", compiler_params=_cparams(("parallel", "parallel", "arbitrary")))(qi, kj, q, k, vsrc)


def _attn_p_ds(q, k, v, o, do, lse, diagonal, t):
    s = _scores(q, k, diagonal)
    p = jnp.exp(s - jnp.tile(lse, (1, t // LANE)))
    dp = lax.dot_general(do.astype(BF16), v, (((1,), (1,)), ((), ())), preferred_element_type=F32)
    delta = jnp.sum(do.astype(F32) * o.astype(F32), axis=1, keepdims=True)
    ds = p * (dp - delta)
    return p, ds


ATT_HB_BWD = 4


def _attn_bwd_call(q, k, vsrc, o, do, lse):
    B, S, _ = q.shape
    t, n = _att_tiles(S)
    qi, kj = _causal_pairs(n)
    n_pairs = qi.shape[0]
    hb = ATT_HB_BWD
    v_blk0 = N_HEADS // hb

    def body(qi_ref, kj_ref, q_ref, k_ref, v_ref, o_ref, do_ref, lse_ref, dq_ref, dk_ref, dv_ref, dq_sc, dk_sc, dv_sc):
        p_id = pl.program_id(2)
        i, j = qi_ref[p_id], kj_ref[p_id]

        @pl.when(p_id == 0)
        def _():
            dk_sc[...] = jnp.zeros(dk_sc.shape, F32)
            dv_sc[...] = jnp.zeros(dv_sc.shape, F32)

        @pl.when(j == 0)
        def _():
            dq_sc[...] = jnp.zeros(dq_sc.shape, F32)

        rows = pl.ds(pl.multiple_of(j * t, t), t)

        def step(diagonal):
            qa, ka, va, oa, doa, la = q_ref[0], k_ref[0], v_ref[0], o_ref[0], do_ref[0], lse_ref[0]
            for h in range(hb):
                qb, kb, dob = _head(qa, h, QK_PAD), _head(ka, h, QK_PAD), _head(doa, h, V_DIM)
                p, ds = _attn_p_ds(qb, kb, _head(va, h, V_DIM), _head(oa, h, V_DIM), dob, _head(la, h, LANE),
                                   diagonal, t)
                dsb = ds.astype(BF16)
                dq_sc[:, h * QK_PAD:(h + 1) * QK_PAD] += jnp.dot(dsb, kb, preferred_element_type=F32)
                dv_sc[rows, h * V_DIM:(h + 1) * V_DIM] += lax.dot_general(
                    p.astype(BF16), dob.astype(BF16), (((0,), (0,)), ((), ())), preferred_element_type=F32)
                dk_sc[rows, h * QK_PAD:(h + 1) * QK_PAD] += lax.dot_general(
                    dsb, qb, (((0,), (0,)), ((), ())), preferred_element_type=F32)

        @pl.when(j < i)
        def _():
            step(False)

        @pl.when(j == i)
        def _():
            step(True)
            dq_ref[0] = dq_sc[...].astype(BF16)

        @pl.when(i == n - 1)
        def _():
            dk_ref[0] = dk_sc[rows, :].astype(BF16)
            dv_ref[0] = dv_sc[rows, :].astype(BF16)

    wq, wv = hb * QK_PAD, hb * V_DIM
    at_q = lambda b, h, p, qi, kj: (b, qi[p], h)
    at_k = lambda b, h, p, qi, kj: (b, kj[p], h)
    at_done = lambda b, h, p, qi, kj: (b, jnp.where(qi[p] == n - 1, kj[p], 0), h)
    grid_spec = pltpu.PrefetchScalarGridSpec(
        num_scalar_prefetch=2, grid=(B, N_HEADS // hb, n_pairs),
        in_specs=[pl.BlockSpec((1, t, wq), at_q), pl.BlockSpec((1, t, wq), at_k),
                  pl.BlockSpec((1, t, wv), lambda b, h, p, qi, kj: (b, kj[p], v_blk0 + h)),
                  pl.BlockSpec((1, t, wv), at_q), pl.BlockSpec((1, t, wv), at_q), pl.BlockSpec((1, t, wv), at_q)],
        out_specs=[pl.BlockSpec((1, t, wq), at_q), pl.BlockSpec((1, t, wq), at_done), pl.BlockSpec((1, t, wv), at_done)],
        scratch_shapes=[pltpu.VMEM((t, wq), F32), pltpu.VMEM((S, wq), F32), pltpu.VMEM((S, wv), F32)])
    return pl.pallas_call(
        body, grid_spec=grid_spec,
        out_shape=[jax.ShapeDtypeStruct((B, S, N_HEADS * QK_PAD), BF16),
                   jax.ShapeDtypeStruct((B, S, N_HEADS * QK_PAD), BF16),
                   jax.ShapeDtypeStruct((B, S, N_HEADS * V_DIM), BF16)],
        name="attn_bwd", compiler_params=_cparams(("parallel", "parallel", "arbitrary")))(
            qi, kj, q, k, vsrc, o, do, lse)


@jax.custom_vjp
def attention(q, kv, src, stand_in, tabs):
    return _attn_fwd_call(q, _build_k_fwd_call(kv, src, tabs), kv, N_HEADS // ATT_HB)[0]


def _attention_fwd(q, kv, src, stand_in, tabs):
    k = _build_k_fwd_call(kv, src, tabs)
    o, lse = _attn_fwd_call(q, k, kv, N_HEADS // ATT_HB)
    return o, (q, k, kv, o, lse, src, tabs)


def _attention_bwd(res, do):
    q, k, kv, o, lse, src, tabs = res
    dq, dk, dv = _attn_bwd_call(q, k, kv, o, do, lse)
    dk_nope, dk_rope = _build_k_bwd_call(dk, tabs)
    return (dq, jnp.concatenate([dk_nope, dv], axis=-1), jnp.zeros_like(src), dk_rope,
            tuple(jnp.zeros_like(t) for t in tabs))


attention.defvjp(_attention_fwd, _attention_bwd)


SUBLANES = 8
CONV_BLOCK = 2 * LANE


def _zero_tail(v):
    return jnp.concatenate([v, jnp.zeros((SUBLANES, v.shape[1]), v.dtype)], axis=0)


def _shift_down(vz, sh):
    return pltpu.roll(vz, sh, 0)[:vz.shape[0] - SUBLANES]


def _shift_up(vz, sh):
    return pltpu.roll(vz, vz.shape[0] - sh, 0)[:vz.shape[0] - SUBLANES]


def _conv_pre(u, uz, w_ref, b_ref):
    acc = b_ref[...] + w_ref[pl.ds(CONV_K - 1, 1), :] * u
    for k in range(CONV_K - 1):
        acc = acc + w_ref[pl.ds(k, 1), :] * _shift_down(uz, CONV_K - 1 - k)
    return acc


def _conv_fwd_call(src, w, b):
    B, S, _ = src.shape
    C = w.shape[1]

    def body(u_ref, w_ref, b_ref, o_ref):
        uu = u_ref[0].astype(F32)
        o_ref[0] = _silu(_conv_pre(uu, _zero_tail(uu), w_ref, b_ref))

    spec = pl.BlockSpec((1, S, CONV_BLOCK), lambda c, bb: (bb, 0, c))
    return pl.pallas_call(
        body, grid=(C // CONV_BLOCK, B),
        in_specs=[pl.BlockSpec((1, S, CONV_BLOCK), lambda c, bb: (bb, 0, c + CONV_LANE0 // CONV_BLOCK)),
                  pl.BlockSpec((CONV_K, CONV_BLOCK), lambda c, bb: (0, c)),
                  pl.BlockSpec((1, CONV_BLOCK), lambda c, bb: (0, c))],
        out_specs=spec, out_shape=jax.ShapeDtypeStruct((B, S, C), F32), name="conv_fwd",
        compiler_params=_cparams(("parallel", "arbitrary")))(src, w, b)


def _conv_bwd_call(src, w, b, g):
    B, S, _ = src.shape
    C = w.shape[1]

    def body(u_ref, w_ref, b_ref, g_ref, du_ref, dw_ref, db_ref):
        uu = u_ref[0].astype(F32)
        uz = _zero_tail(uu)
        pre = _conv_pre(uu, uz, w_ref, b_ref)
        sg = lax.logistic(pre)
        dpre = g_ref[0] * sg * (1.0 + pre * (1.0 - sg))
        dz = _zero_tail(dpre)
        du = w_ref[pl.ds(CONV_K - 1, 1), :] * dpre
        dws = [None] * CONV_K
        dws[CONV_K - 1] = jnp.sum(dpre * uu, axis=0, keepdims=True)
        for k in range(CONV_K - 1):
            sh = CONV_K - 1 - k
            du = du + w_ref[pl.ds(k, 1), :] * _shift_up(dz, sh)
            dws[k] = jnp.sum(dpre * _shift_down(uz, sh), axis=0, keepdims=True)
        du_ref[0] = du.astype(du_ref.dtype)
        dbv = jnp.sum(dpre, axis=0, keepdims=True)
        first = pl.program_id(1) == 0

        @pl.when(first)
        def _():
            for k in range(CONV_K):
                dw_ref[pl.ds(k, 1), :] = dws[k]
            db_ref[...] = dbv

        @pl.when(jnp.logical_not(first))
        def _():
            for k in range(CONV_K):
                dw_ref[pl.ds(k, 1), :] += dws[k]
            db_ref[...] += dbv

    spec = pl.BlockSpec((1, S, CONV_BLOCK), lambda c, bb: (bb, 0, c))
    wspec = pl.BlockSpec((CONV_K, CONV_BLOCK), lambda c, bb: (0, c))
    bspec = pl.BlockSpec((1, CONV_BLOCK), lambda c, bb: (0, c))
    uspec = pl.BlockSpec((1, S, CONV_BLOCK), lambda c, bb: (bb, 0, c + CONV_LANE0 // CONV_BLOCK))
    return pl.pallas_call(
        body, grid=(C // CONV_BLOCK, B), in_specs=[uspec, wspec, bspec, spec], out_specs=[spec, wspec, bspec],
        out_shape=[jax.ShapeDtypeStruct((B, S, C), BF16), jax.ShapeDtypeStruct(w.shape, F32),
                   jax.ShapeDtypeStruct(b.shape, F32)],
        name="conv_bwd", compiler_params=_cparams(("parallel", "arbitrary")))(src, w, b, g)


@jax.custom_vjp
def conv_silu(src, stand_in, w, b):
    return _conv_fwd_call(src, w, b)


def _conv_silu_fwd(src, stand_in, w, b):
    return _conv_fwd_call(src, w, b), (src, w, b)


def _conv_silu_bwd(res, g):
    du, dw, db = _conv_bwd_call(*res, g)
    return jnp.zeros_like(res[0]), du, dw, db


conv_silu.defvjp(_conv_silu_fwd, _conv_silu_bwd)


def _chunk_cumsum_call(a, reverse, name):
    B, S, W = a.shape
    per_step = min(S // CHUNK, 32)

    def body(a_ref, o_ref):
        r = lax.broadcasted_iota(jnp.int32, (CHUNK, CHUNK), 0)
        c = lax.broadcasted_iota(jnp.int32, (CHUNK, CHUNK), 1)
        tri = jnp.where((c >= r) if reverse else (c <= r), 1.0, 0.0).astype(F32)
        for i in range(per_step):
            rows = pl.ds(i * CHUNK, CHUNK)
            o_ref[0, rows, :] = jnp.dot(tri, a_ref[0, rows, :], preferred_element_type=F32,
                                        precision=lax.Precision.HIGHEST)

    spec = pl.BlockSpec((1, per_step * CHUNK, W), lambda b, c: (b, c, 0))
    return pl.pallas_call(body, grid=(B, S // (per_step * CHUNK)), in_specs=[spec], out_specs=spec,
                          out_shape=jax.ShapeDtypeStruct(a.shape, F32), name=name,
                          compiler_params=_cparams(("parallel", "parallel")))(a)


@jax.custom_vjp
def chunk_cumsum(a):
    return _chunk_cumsum_call(a, False, "chunk_cumsum_fwd")


chunk_cumsum.defvjp(lambda a: (_chunk_cumsum_call(a, False, "chunk_cumsum_fwd"), None),
                    lambda _, g: (_chunk_cumsum_call(g, True, "chunk_cumsum_bwd"),))


GROUP_W = 4 * HEAD_P
HPG = SSM_HEADS // SSM_GROUPS


def _ssd_masks():
    lane = lax.broadcasted_iota(jnp.int32, (1, GROUP_W), 1)
    return [((lane >= HEAD_P * j) & (lane < HEAD_P * (j + 1))).astype(F32) for j in range(HPG)]


def _ssd_decays(ac_cols, acr_ref, gi):
    r = lax.broadcasted_iota(jnp.int32, (CHUNK, CHUNK), 0)
    c = lax.broadcasted_iota(jnp.int32, (CHUNK, CHUNK), 1)
    return [jnp.exp(jnp.where(c <= r, ac_cols[j] - acr_ref[0, gi * HPG + j], NEG)) for j in range(HPG)]


def _ssd_cols(blk, g):
    lane = lax.broadcasted_iota(jnp.int32, blk.shape, 1)
    return [jnp.sum(jnp.where(lane == HPG * g + j, blk, 0.0), axis=1, keepdims=True) for j in range(HPG)]


def _ssd_spread(cols):
    lane = lax.broadcasted_iota(jnp.int32, (1, GROUP_W), 1)
    out = jnp.broadcast_to(cols[HPG - 1], (CHUNK, GROUP_W))
    for j in range(HPG - 2, -1, -1):
        out = jnp.where(lane < HEAD_P * (j + 1), cols[j], out)
    return out


def _ssd_gather(val, cols, masks, g):
    lane = lax.broadcasted_iota(jnp.int32, (1, LANE), 1)
    out = jnp.zeros((CHUNK, LANE), F32)
    for j in range(HPG):
        tot = jnp.sum(val * masks[j], axis=1, keepdims=True)
        if cols is not None:
            tot = tot + cols[j]
        out = out + tot * (lane == HPG * g + j).astype(F32)
    return out


def _dot(a, b, dims):
    return lax.dot_general(a.astype(BF16), b.astype(BF16), (dims, ((), ())), preferred_element_type=F32)


NN = ((1,), (0,))
NT = ((1,), (1,))
TN = ((0,), (0,))


XBC_W = GROUP_W + 2 * STATE_N


SSD_STEP_GROUPS_FWD = 8
SSD_STEP_GROUPS_BWD = 8


def _ssd_load(xbc_ref, dt_ref, ac_ref, masks, g, gi):
    x = xbc_ref[0, :, gi * XBC_W:gi * XBC_W + GROUP_W]
    bm = xbc_ref[0, :, gi * XBC_W + GROUP_W:gi * XBC_W + GROUP_W + STATE_N]
    cm = xbc_ref[0, :, gi * XBC_W + GROUP_W + STATE_N:(gi + 1) * XBC_W]
    ac_cols = _ssd_cols(ac_ref[0], g)
    dt = _ssd_spread(_ssd_cols(dt_ref[0], g))
    ac = _ssd_spread(ac_cols)
    is_last = (lax.broadcasted_iota(jnp.int32, (CHUNK, GROUP_W), 0) == CHUNK - 1).astype(F32)
    return x, bm, cm, dt, ac, ac_cols, is_last


def _ssd_in_specs(nc, rev, gb):
    cc = (lambda c: nc - 1 - c) if rev else (lambda c: c)
    return [pl.BlockSpec((1, CHUNK, gb * XBC_W), lambda b, g, c: (b, cc(c), g)),
            pl.BlockSpec((1, CHUNK, LANE), lambda b, g, c: (b, cc(c), 0)),
            pl.BlockSpec((1, CHUNK, LANE), lambda b, g, c: (b, cc(c), 0)),
            pl.BlockSpec((1, gb * HPG, 1, CHUNK), lambda b, g, c: (b, g, 0, cc(c))),
            pl.BlockSpec((1, gb * GROUP_W), lambda b, g, c: (0, g))]


def _ssd_fwd_call(xbc, dtp, acp, acr, dsk):
    B, S, _ = xbc.shape
    nc = S // CHUNK
    gb = SSD_STEP_GROUPS_FWD

    def body(xbc_ref, dt_ref, ac_ref, ar_ref, ds_ref, y_ref, hp_ref, h_sc):
        @pl.when(pl.program_id(2) == 0)
        def _():
            h_sc[...] = jnp.zeros(h_sc.shape, F32)

        masks = _ssd_masks()
        ys = []
        for gi in range(gb):
            grp = gb * pl.program_id(1) + gi
            x, bm, cm, dt, ac, ac_cols, is_last = _ssd_load(xbc_ref, dt_ref, ac_ref, masks, grp, gi)
            last = jnp.sum(ac * is_last, axis=0, keepdims=True)
            decays = _ssd_decays(ac_cols, ar_ref, gi)
            xd = x * dt
            cb = _dot(cm, bm, NT)
            hprev = h_sc[gi]
            hp_ref[0, gi, 0] = hprev
            y = _dot(cm, hprev, NN) * jnp.exp(ac) + ds_ref[:, gi * GROUP_W:(gi + 1) * GROUP_W] * x
            y = y + _dot(jnp.concatenate([cb * d for d in decays], axis=1),
                         jnp.concatenate([xd * m for m in masks], axis=0), NN)
            ys.append(y)
            h_sc[gi] = hprev * jnp.exp(last) + _dot(bm, xd * jnp.exp(last - ac), TN)
        y_ref[0] = jnp.concatenate(ys, axis=1)

    ng = SSM_GROUPS // gb
    return pl.pallas_call(
        body, grid=(B, ng, nc), in_specs=_ssd_in_specs(nc, False, gb),
        out_specs=[pl.BlockSpec((1, CHUNK, gb * GROUP_W), lambda b, g, c: (b, c, g)),
                   pl.BlockSpec((1, gb, 1, STATE_N, GROUP_W), lambda b, g, c: (b, g, c, 0, 0))],
        out_shape=[jax.ShapeDtypeStruct((B, S, D_INNER), F32),
                   jax.ShapeDtypeStruct((B, SSM_GROUPS, nc, STATE_N, GROUP_W), F32)],
        scratch_shapes=[pltpu.VMEM((gb, STATE_N, GROUP_W), F32)], name="ssd_fwd",
        compiler_params=_cparams(("parallel", "parallel", "arbitrary")))(xbc, dtp, acp, acr, dsk)


def _ssd_bwd_call(xbc, dtp, acp, acr, dsk, hps, dy):
    B, S, _ = xbc.shape
    nc = S // CHUNK
    gb = SSD_STEP_GROUPS_BWD

    def body(xbc_ref, dt_ref, ac_ref, ar_ref, ds_ref, hp_ref, dy_ref,
             dxbc_ref, ddt_ref, dac_ref, dar_ref, dds_ref, dh_sc):
        first = pl.program_id(2) == 0

        @pl.when(first)
        def _():
            dh_sc[...] = jnp.zeros(dh_sc.shape, F32)

        masks = _ssd_masks()
        dxbc_parts, dds_parts = [], []
        for gi in range(gb):
            grp = gb * pl.program_id(0) + gi
            x, bm, cm, dt, ac, ac_cols, is_last = _ssd_load(xbc_ref, dt_ref, ac_ref, masks, grp, gi)
            last = jnp.sum(ac * is_last, axis=0, keepdims=True)
            g = dy_ref[0, :, gi * GROUP_W:(gi + 1) * GROUP_W]
            hprev = hp_ref[0, gi, 0]
            dh = dh_sc[gi]
            decays = _ssd_decays(ac_cols, ar_ref, gi)
            dcols = []
            xd = x * dt
            cb = _dot(cm, bm, NT)
            e_c = jnp.exp(ac)
            e_end = jnp.exp(last - ac)
            e_last = jnp.exp(last)
            z = _dot(cm, hprev, NN)
            dz = g * e_c
            dac = g * z * e_c
            dc = _dot(dz, hprev, NT)
            dhprev = _dot(cm, dz, TN) + dh * e_last
            dcb = jnp.zeros((CHUNK, CHUNK), F32)
            gjs = [cb * d for d in decays]
            g_heads = jnp.concatenate([g * m for m in masks], axis=0)
            dg_heads = _dot(g_heads, xd, NT)
            dxd = _dot(jnp.concatenate(gjs, axis=0), g_heads, TN)
            for j in range(HPG):
                gj = gjs[j]
                dgj = dg_heads[j * CHUNK:(j + 1) * CHUNK]
                dcb = dcb + dgj * decays[j]
                dseg = dgj * gj
                dcols.append(jnp.sum(dseg, axis=1, keepdims=True))
                dar_ref[0, gi * HPG + j] = -jnp.sum(dseg, axis=0, keepdims=True)
            dc = dc + _dot(dcb, bm, NN)
            db = _dot(dcb, cm, TN)
            sx = xd * e_end
            db = db + _dot(sx, dh, NT)
            dsx = _dot(bm, dh, NN)
            dxd = dxd + dsx * e_end
            de = dsx * sx
            dac = dac - de
            dlast = jnp.sum(de, axis=0, keepdims=True) + jnp.sum(dh * hprev, axis=0, keepdims=True) * e_last
            dsk = ds_ref[:, gi * GROUP_W:(gi + 1) * GROUP_W]
            dxbc_parts += [dxd * dt + dsk * g, db, dc]
            ddt_ref[0, gi] = _ssd_gather(dxd * x, None, masks, grp)
            dac_ref[0, gi] = _ssd_gather(dac + is_last * dlast, dcols, masks, grp)
            dds_parts.append(jnp.sum(g * x, axis=0, keepdims=True))
            dh_sc[gi] = dhprev
        dxbc_ref[0] = jnp.concatenate(dxbc_parts, axis=1)
        dds = jnp.concatenate(dds_parts, axis=1)
        first_all = first & (pl.program_id(1) == 0)

        @pl.when(first_all)
        def _():
            dds_ref[...] = dds

        @pl.when(jnp.logical_not(first_all))
        def _():
            dds_ref[...] += dds

    rc = lambda c: nc - 1 - c
    ng = SSM_GROUPS // gb
    in_specs = [pl.BlockSpec(s.block_shape, (lambda g, b, c, f=s.index_map: f(b, g, c))) for s in _ssd_in_specs(nc, True, gb)]
    in_specs.append(pl.BlockSpec((1, gb, 1, STATE_N, GROUP_W), lambda g, b, c: (b, g, rc(c), 0, 0)))
    in_specs.append(pl.BlockSpec((1, CHUNK, gb * GROUP_W), lambda g, b, c: (b, rc(c), g)))
    per_group = pl.BlockSpec((1, gb, CHUNK, LANE), lambda g, b, c: (b, g, rc(c), 0))
    out_specs = [pl.BlockSpec((1, CHUNK, gb * XBC_W), lambda g, b, c: (b, rc(c), g)), per_group, per_group,
                 pl.BlockSpec((1, gb * HPG, 1, CHUNK), lambda g, b, c: (b, g, 0, rc(c))),
                 pl.BlockSpec((1, gb * GROUP_W), lambda g, b, c: (0, g))]
    out_shape = [jax.ShapeDtypeStruct(xbc.shape, F32),
                 jax.ShapeDtypeStruct((B, SSM_GROUPS, S, LANE), F32), jax.ShapeDtypeStruct((B, SSM_GROUPS, S, LANE), F32),
                 jax.ShapeDtypeStruct(acr.shape, F32), jax.ShapeDtypeStruct(dsk.shape, F32)]
    return pl.pallas_call(
        body, grid=(ng, B, nc), in_specs=in_specs, out_specs=out_specs, out_shape=out_shape,
        scratch_shapes=[pltpu.VMEM((gb, STATE_N, GROUP_W), F32)], name="ssd_bwd",
        compiler_params=_cparams(("arbitrary", "arbitrary", "arbitrary")))(xbc, dtp, acp, acr, dsk, hps, dy)


@jax.custom_vjp
def ssd(xbc, dtp, acp, acr, dsk):
    return _ssd_fwd_call(xbc, dtp, acp, acr, dsk)[0]


def _ssd_fwd(xbc, dtp, acp, acr, dsk):
    y, hps = _ssd_fwd_call(xbc, dtp, acp, acr, dsk)
    return y, (xbc, dtp, acp, acr, dsk, hps)


def _ssd_bwd(res, dy):
    dxbc, ddt, dac, dacr, dds = _ssd_bwd_call(*res, dy)
    return dxbc, jnp.sum(ddt, axis=1), jnp.sum(dac, axis=1), dacr, dds


ssd.defvjp(_ssd_fwd, _ssd_bwd)


def _pack_small(arrs):
    flat = jnp.concatenate([a.reshape(-1) for a in arrs])
    rows = -(-flat.shape[0] // (8 * LANE)) * 8
    return jnp.pad(flat, (0, rows * LANE - flat.shape[0])).reshape(rows, LANE)


def _unpack_small(buf, shapes):
    flat = buf.reshape(-1)
    out, off = [], 0
    for shp in shapes:
        n = int(np.prod(shp))
        out.append(flat[off:off + n].reshape(shp))
        off += n
    return out


def _rows_tile(rows, cap):
    for cand in range(min(rows, cap), 7, -8):
        if rows % cand == 0:
            return cand
    return rows


def _pair_sum(mine, theirs, cidx, name):
    n4, kk, nn = mine.shape
    half = kk // 2
    tr = _rows_tile(half, 256)
    nb = half // tr

    def body(c_ref, a_ref, b_ref, o_ref, ob_ref):
        tot = a_ref[...] + b_ref[...]
        o_ref[...] = tot
        ob_ref[...] = tot.astype(BF16)

    spec = pl.BlockSpec((1, tr, nn), lambda j, i, c: (j, i, 0))
    grid_spec = pltpu.PrefetchScalarGridSpec(
        num_scalar_prefetch=1, grid=(n4, nb),
        in_specs=[pl.BlockSpec((1, tr, nn), lambda j, i, c: (j, c[0] * nb + i, 0)), spec], out_specs=[spec, spec])
    return pl.pallas_call(
        body, grid_spec=grid_spec,
        out_shape=[jax.ShapeDtypeStruct((n4, half, nn), F32), jax.ShapeDtypeStruct((n4, half, nn), BF16)],
        name=name, compiler_params=_cparams(("parallel", "parallel")))(cidx, mine, theirs)


def _chip_sum(quad, pair, chip_idx, name):
    _, rows, nn = quad.shape
    tr = _rows_tile(rows, 256)

    def body(s_ref, q_ref, p_ref, o_ref):
        for mine in range(4):
            @pl.when(s_ref[0] == mine)
            def _(mine=mine):
                acc = None
                for d in range(4):
                    term = p_ref[0] if d == mine else q_ref[d].astype(F32)
                    acc = term if acc is None else acc + term
                o_ref[...] = acc

    grid_spec = pltpu.PrefetchScalarGridSpec(
        num_scalar_prefetch=1, grid=(rows // tr,),
        in_specs=[pl.BlockSpec((4, tr, nn), lambda i, s: (0, i, 0)), pl.BlockSpec((1, tr, nn), lambda i, s: (s[0], i, 0))],
        out_specs=pl.BlockSpec((tr, nn), lambda i, s: (i, 0)))
    return pl.pallas_call(body, grid_spec=grid_spec, out_shape=jax.ShapeDtypeStruct((rows, nn), F32), name=name,
                          compiler_params=_cparams(("parallel",)))(chip_idx, quad, pair)


def _adam_halves_call(w, mine, other, cidx, m, v, name):
    _, rows, nn = w.shape
    half = rows // 2
    tr = _rows_tile(half, 128)
    nb = half // tr

    def body(c_ref, w_ref, a_ref, b_ref, m_ref, v_ref, g_ref, d_ref, nm_ref, nv_ref):
        upper = (pl.program_id(0) >= nb).astype(jnp.int32)
        g = jnp.where(upper == c_ref[0], a_ref[...], b_ref[...])
        g_ref[0] = g
        d_ref[0], nm_ref[0], nv_ref[0] = _adam_fn(w_ref[0], g, m_ref[0], v_ref[0])

    spec = pl.BlockSpec((1, tr, nn), lambda i, c: (0, i, 0))
    hspec = pl.BlockSpec((tr, nn), lambda i, c: (i % nb, 0))
    grid_spec = pltpu.PrefetchScalarGridSpec(num_scalar_prefetch=1, grid=(2 * nb,),
                                             in_specs=[spec, hspec, hspec, spec, spec], out_specs=[spec] * 4)
    return pl.pallas_call(body, grid_spec=grid_spec, out_shape=[jax.ShapeDtypeStruct(w.shape, F32)] * 4, name=name,
                          compiler_params=_cparams(("parallel",)))(cidx, w, mine, other, m, v)


def _stack_sum(stack, name):
    n, rows, nn = stack.shape
    tr = _rows_tile(rows, 256)

    def body(s_ref, o_ref):
        acc = s_ref[0]
        for d in range(1, n):
            acc = acc + s_ref[d]
        o_ref[...] = acc

    return pl.pallas_call(
        body, grid=(rows // tr,), in_specs=[pl.BlockSpec((n, tr, nn), lambda i: (0, i, 0))],
        out_specs=pl.BlockSpec((tr, nn), lambda i: (i, 0)), out_shape=jax.ShapeDtypeStruct((rows, nn), F32),
        name=name, compiler_params=_cparams(("parallel",)))(stack)


def _adam_call(w, g, m, v, name):
    rows, nn = w.shape
    tr = _rows_tile(rows, 128)

    def body(w_ref, g_ref, m_ref, v_ref, d_ref, nm_ref, nv_ref):
        d_ref[...], nm_ref[...], nv_ref[...] = _adam_fn(w_ref[...], g_ref[...], m_ref[...], v_ref[...])

    spec = pl.BlockSpec((tr, nn), lambda i: (i, 0))
    sds = jax.ShapeDtypeStruct((rows, nn), F32)
    return pl.pallas_call(body, grid=(rows // tr,), in_specs=[spec] * 4, out_specs=[spec] * 3,
                          out_shape=[sds] * 3, name=name, compiler_params=_cparams(("parallel",)))(w, g, m, v)


def _adam_fn(w, g, m, v):
    m = ADAM_B1 * m + (1.0 - ADAM_B1) * g
    v = ADAM_B2 * v + (1.0 - ADAM_B2) * (g * g)
    m_hat = m / (1.0 - ADAM_B1 ** ADAM_STEP)
    v_hat = v / (1.0 - ADAM_B2 ** ADAM_STEP)
    delta = -ADAM_LR * (m_hat / (jnp.sqrt(v_hat) + ADAM_EPS) + ADAM_WD * w)
    return delta, m, v


def _mesh_pos():
    return lax.axis_index("x"), lax.axis_index("y"), lax.axis_index("c")


def _other_chips(x, y):
    return [(1 - x, y), (x, 1 - y), (1 - x, 1 - y)]


HBM_SPEC = pl.BlockSpec(memory_space=pl.ANY)


def _remote(src, dst, send_sems, recv_sems, k, to):
    return pltpu.make_async_remote_copy(src_ref=src, dst_ref=dst, send_sem=send_sems.at[k], recv_sem=recv_sems.at[k],
                                        device_id=to, device_id_type=MESH)


def _half_rows(c, rows, align):
    half = rows // 2
    return (pl.ds(pl.multiple_of(c * half, align), half), pl.ds(pl.multiple_of((1 - c) * half, align), half))


def _gather_weights(mats, conv):
    n = len(mats)

    def body(*refs):
        ins, conv_in = refs[:n], refs[n]
        outs, conv_out = refs[n + 1:2 * n + 1], refs[2 * n + 1]
        send_sems, recv_sems, local_sem = refs[2 * n + 2:]
        x, y, c = _mesh_pos()
        me, sibling, s = (x, y, c), (x, y, 1 - c), 2 * x + y
        chips = _other_chips(x, y)
        rows = [_half_rows(c, m.shape[0], 16) for m in mats]
        own = pltpu.make_async_copy(conv_in, conv_out.at[s], local_sem)
        own.start()
        sent = []
        for i in range(n):
            mine = rows[i][0]
            for j, (cx, cy) in enumerate(chips):
                sent.append(_remote(ins[i].at[mine], outs[i].at[s, mine], send_sems, recv_sems, 6 * i + j, (cx, cy, c)))
        for j, (cx, cy) in enumerate(chips):
            sent.append(_remote(conv_in, conv_out.at[s], send_sems, recv_sems, 6 * n + j, (cx, cy, c)))
        for cp in sent:
            cp.start()
        for i in range(n):
            mine = rows[i][0]
            for j, (cx, cy) in enumerate(chips):
                landed = outs[i].at[2 * cx + cy, mine]
                _remote(landed, landed, send_sems, recv_sems, 6 * i + j, me).wait_recv()
                fwd = _remote(landed, landed, send_sems, recv_sems, 6 * i + 3 + j, sibling)
                fwd.start()
                sent.append(fwd)
        for j, (cx, cy) in enumerate(chips):
            slot = conv_out.at[2 * cx + cy]
            _remote(slot, slot, send_sems, recv_sems, 6 * n + j, me).wait_recv()
        for i in range(n):
            theirs_rows = rows[i][1]
            for j, (cx, cy) in enumerate(chips):
                theirs = outs[i].at[2 * cx + cy, theirs_rows]
                _remote(theirs, theirs, send_sems, recv_sems, 6 * i + 3 + j, me).wait_recv()
        for cp in sent:
            cp.wait_send()
        own.wait()

    out_shape = [jax.ShapeDtypeStruct((4,) + m.shape, m.dtype) for m in mats]
    out_shape.append(jax.ShapeDtypeStruct((4,) + conv.shape, conv.dtype))
    res = pl.pallas_call(
        body, in_specs=[HBM_SPEC] * (n + 1), out_specs=[HBM_SPEC] * (n + 1), out_shape=out_shape,
        scratch_shapes=[pltpu.SemaphoreType.DMA((6 * n + 3,)), pltpu.SemaphoreType.DMA((6 * n + 3,)),
                        pltpu.SemaphoreType.DMA],
        name="all_gather_weights")(*mats, conv)
    return res[:n], res[n]


def _sibling_exchange(stacks):
    n = len(stacks)

    def body(*refs):
        ins, outs = refs[:n], refs[n:2 * n]
        send_sems, recv_sems = refs[2 * n:]
        x, y, c = _mesh_pos()
        cps = []
        for i in range(n):
            theirs = _half_rows(c, stacks[i].shape[1], 8)[1]
            cps.append(_remote(ins[i].at[:, theirs, :], outs[i], send_sems, recv_sems, i, (x, y, 1 - c)))
        for cp in cps:
            cp.start()
        for cp in cps:
            cp.wait()

    out_shape = [jax.ShapeDtypeStruct((4, s.shape[1] // 2, s.shape[2]), s.dtype) for s in stacks]
    return pl.pallas_call(
        body, in_specs=[HBM_SPEC] * n, out_specs=[HBM_SPEC] * n, out_shape=out_shape,
        scratch_shapes=[pltpu.SemaphoreType.DMA((n,)), pltpu.SemaphoreType.DMA((n,))],
        name="grad_sibling_exchange")(*stacks)


def _chip_exchange(parts):
    n = len(parts)

    def body(*refs):
        ins, outs = refs[:n], refs[n:2 * n]
        send_sems, recv_sems = refs[2 * n:]
        x, y, c = _mesh_pos()
        me, s = (x, y, c), 2 * x + y
        chips = _other_chips(x, y)
        sent = [_remote(ins[i].at[2 * cx + cy], outs[i].at[s], send_sems, recv_sems, 3 * i + j, (cx, cy, c))
                for i in range(n) for j, (cx, cy) in enumerate(chips)]
        for cp in sent:
            cp.start()
        for i in range(n):
            for j, (cx, cy) in enumerate(chips):
                slot = outs[i].at[2 * cx + cy]
                _remote(slot, slot, send_sems, recv_sems, 3 * i + j, me).wait_recv()
        for cp in sent:
            cp.wait_send()

    return pl.pallas_call(
        body, in_specs=[HBM_SPEC] * n, out_specs=[HBM_SPEC] * n,
        out_shape=[jax.ShapeDtypeStruct(p.shape, p.dtype) for p in parts],
        scratch_shapes=[pltpu.SemaphoreType.DMA((3 * n,)), pltpu.SemaphoreType.DMA((3 * n,))],
        name="grad_chip_exchange")(*parts)


def _sibling_swap(halves):
    n = len(halves)

    def body(*refs):
        ins, outs = refs[:n], refs[n:2 * n]
        send_sems, recv_sems = refs[2 * n:]
        x, y, c = _mesh_pos()
        cps = [_remote(ins[i], outs[i], send_sems, recv_sems, i, (x, y, 1 - c)) for i in range(n)]
        for cp in cps:
            cp.start()
        for cp in cps:
            cp.wait()

    return pl.pallas_call(
        body, in_specs=[HBM_SPEC] * n, out_specs=[HBM_SPEC] * n,
        out_shape=[jax.ShapeDtypeStruct(h.shape, h.dtype) for h in halves],
        scratch_shapes=[pltpu.SemaphoreType.DMA((n,)), pltpu.SemaphoreType.DMA((n,))],
        name="grad_sibling_swap")(*halves)


def _gather_small(vec):
    def body(in_ref, out_ref, send_sems, recv_sems, local_sem):
        x, y, c = _mesh_pos()
        me = (x, y, c)
        own = pltpu.make_async_copy(in_ref, out_ref.at[4 * x + 2 * y + c], local_sem)
        own.start()
        peers = [(1 - x if k & 4 else x, 1 - y if k & 2 else y, 1 - c if k & 1 else c) for k in range(1, 8)]
        sent = [_remote(in_ref, out_ref.at[4 * x + 2 * y + c], send_sems, recv_sems, k, p) for k, p in enumerate(peers)]
        for cp in sent:
            cp.start()
        for k, (px, py, pc) in enumerate(peers):
            slot = out_ref.at[4 * px + 2 * py + pc]
            _remote(slot, slot, send_sems, recv_sems, k, me).wait_recv()
        for cp in sent:
            cp.wait_send()
        own.wait()

    return pl.pallas_call(
        body, in_specs=[HBM_SPEC], out_specs=HBM_SPEC, out_shape=jax.ShapeDtypeStruct((8,) + vec.shape, vec.dtype),
        scratch_shapes=[pltpu.SemaphoreType.DMA((7,)), pltpu.SemaphoreType.DMA((7,)), pltpu.SemaphoreType.DMA],
        name="grad_gather_small")(vec)


def _reduce_matrices(stacks, names):
    cidx = lax.axis_index("c").astype(jnp.int32).reshape(1)
    chip = (2 * lax.axis_index("x") + lax.axis_index("y")).astype(jnp.int32).reshape(1)
    got = _sibling_exchange(stacks)
    pairs = [_pair_sum(a, b, cidx, "grad_pair_sum_" + nm) for a, b, nm in zip(stacks, got, names)]
    quads = _chip_exchange([p[1] for p in pairs])
    mine = [_chip_sum(q, p[0], chip, "grad_chip_sum_" + nm) for q, p, nm in zip(quads, pairs, names)]
    return mine, _sibling_swap(mine)


def _pad_cols(a, n):
    return jnp.concatenate([a, jnp.zeros((a.shape[0], n - a.shape[1]), a.dtype)], axis=1)


def _group_channels(a):
    lead = a.shape[:-1]
    xs = a[..., :D_INNER].reshape(lead + (SSM_GROUPS, GROUP_W))
    bs = a[..., D_INNER:D_INNER + SSM_GROUPS * STATE_N].reshape(lead + (SSM_GROUPS, STATE_N))
    cs = a[..., D_INNER + SSM_GROUPS * STATE_N:].reshape(lead + (SSM_GROUPS, STATE_N))
    return jnp.concatenate([xs, bs, cs], axis=-1).reshape(lead + (CONV_CH,))


PROJ_SEGS = (('gate_a', D_MODEL), ('gate_b', D_MODEL), ('z', D_INNER), ('xbc', CONV_CH), ('q_lat', Q_RANK),
             ('kv_lat', KV_RANK), ('k_rope', LANE), ('dt', LANE))
PROJ_WIDE = sum(w for _, w in PROJ_SEGS[:4])
PROJ_LANE0 = {n: (v if v < PROJ_WIDE else v - PROJ_WIDE) for n, v in
              zip([n for n, _ in PROJ_SEGS], [int(v) for v in np.cumsum([0] + [w for _, w in PROJ_SEGS])[:-1]])}
CONV_LANE0 = PROJ_LANE0['xbc']
KR_LANE0 = PROJ_LANE0['k_rope']


def _lay_w_in(w):
    idx = np.cumsum(IN_SIZES)[:-1]
    q_lat, kv_lat, k_rope, z, xbc, dt, gate_a, gate_b = jnp.split(w, [int(v) for v in idx], axis=1)
    return jnp.concatenate([gate_a, gate_b, z, _group_channels(xbc), q_lat, kv_lat, _pad_cols(k_rope, LANE),
                            _pad_cols(dt, LANE)], axis=1)


@jax.custom_vjp
def project(h, w, tok):
    return _project_impl(h, w)


def _project_impl(h, w):
    return (_mm(h, w[:, :PROJ_WIDE], "w_in_fwd", BF16), _mm(h, w[:, PROJ_WIDE:], "w_in_narrow_fwd")) + tuple(
        jnp.zeros((h.shape[0], wd), BF16) for _, wd in PROJ_SEGS)


def _project_fwd(h, w, tok):
    return _project_impl(h, w), (h, w)


def _project_bwd(res, cots):
    h, w = res
    g = jnp.concatenate(cots[2:], axis=1)
    return _mm(g, w.T, "w_in_dx", h.dtype), jnp.zeros_like(w), _mm(h.T, g, "w_in_dw")


project.defvjp(_project_fwd, _project_bwd)


def _lay_w_uq(w):
    w3 = w.reshape(Q_RANK, N_HEADS, NOPE + ROPE)
    w3 = jnp.concatenate([w3, jnp.zeros((Q_RANK, N_HEADS, QK_PAD - NOPE - ROPE), w.dtype)], axis=2)
    return w3.reshape(Q_RANK, N_HEADS * QK_PAD)


def _lay_w_ukv(w):
    w3 = w.reshape(KV_RANK, N_HEADS, NOPE + V_DIM)
    return jnp.concatenate([w3[:, :, :NOPE].reshape(KV_RANK, -1), w3[:, :, NOPE:].reshape(KV_RANK, -1)], axis=1)


def _pad_lanes(v, n=LANE):
    return jnp.concatenate([v, jnp.zeros((v.shape[0], n - v.shape[1]), v.dtype)], axis=1)


def _local_loss(toks, small, x, wb, c8, posf, target):
    B, S, D = x.shape
    T = B * S

    def lin(name, a, key, lay=lambda w: w, out_dtype=F32):
        return make_linear(name, out_dtype)(a, lay(wb[key]), lay(toks[key]))

    rows2 = lambda a: a.reshape(T, a.shape[-1])
    rows3 = lambda a: a.reshape(B, S, a.shape[-1])

    sc = make_rowwise("silu_c", _f_silu, 1, 0, 0, ('row',))((c8[None],), (), ())[0][0]
    mod = make_linear("ada", F32, 4)(sc, wb['w_ada'], toks['w_ada'])[:B] + small['b_ada']
    shift1, scale1, gate1, shift2, scale2, gate2 = [m[:, None, :] for m in jnp.split(mod, 6, axis=-1)]

    h, x_res = make_rowwise("modulate1", _f_modulate, 1, 2, 1, ('row',), forward_row=0, ts_cap=1024)(
        (x,), (scale1, shift1), (small['g_pre_mix'],))
    outs = project(rows2(h), _lay_w_in(wb['w_in']), _lay_w_in(toks['w_in']))
    wide = lax.stop_gradient(rows3(outs[0]))
    proj = lax.stop_gradient(rows3(outs[1]))
    stand = {n: rows3(o) for (n, _), o in zip(PROJ_SEGS, outs[2:])}

    def win(seg, block):
        return (PROJ_LANE0[seg] // block, dict(PROJ_SEGS)[seg])

    inv = ROPE_THETA ** (-jnp.arange(ROPE // 2, dtype=F32) / (ROPE // 2))
    inv_lane = jnp.concatenate([inv, inv, jnp.zeros((LANE - ROPE,), F32)])[None]
    tabs = tuple(_rope_tables(posf, inv_lane))
    qn = make_rowwise("rms_q", _f_rms, 1, 0, 1, ('row',), ts_cap=4096, windows={0: win('q_lat', Q_RANK)})(
        (proj,), (), (small['g_q_lat'],), (stand['q_lat'],))[0]
    kvn = make_rowwise("rms_kv", _f_rms, 1, 0, 1, ('row',), ts_cap=4096, windows={0: win('kv_lat', KV_RANK)})(
        (proj,), (), (small['g_kv_lat'],), (stand['kv_lat'],))[0]
    qp = rows3(lin("w_uq", rows2(qn), 'w_uq', _lay_w_uq, BF16))
    kvp = rows3(lin("w_ukv", rows2(kvn), 'w_ukv', _lay_w_ukv, BF16))
    qr = rope_q(qp, tabs)
    att = attention(qr, kvp, proj, stand['k_rope'], tabs)
    attn = rows3(lin("w_o_attn", rows2(att), 'w_o_attn', out_dtype=BF16))

    xa = conv_silu(wide, stand['xbc'], _group_channels(wb['conv_w_f32']), _group_channels(small['conv_b']))
    dt_pad, a_pad = make_rowwise("dt_softplus", _f_dt, 1, 0, 2, ('row', 'row'), ts_cap=4096,
                                 windows={0: win('dt', LANE)})(
        (proj,), (), (_pad_lanes(small['dt_bias']), _pad_lanes(small['a_log'])), (stand['dt'],))
    ac_pad = chunk_cumsum(a_pad)
    acr = jnp.transpose(ac_pad[..., :SSM_HEADS], (0, 2, 1))[:, :, None, :]
    dsk = jnp.repeat(small['d_skip'], HEAD_P, axis=-1)
    y = ssd(xa, dt_pad, ac_pad, acr, dsk)
    yg = make_rowwise("gated_norm", _f_gated_norm, 2, 0, 1, ('row',), ncol=SSM_GROUPS, ts_cap=4096,
                      windows={1: win('z', GROUP_W)})((y, wide), (), (small['g_ssm_out'],), (stand['z'],))[0]
    ssm = rows3(lin("w_o_ssm", rows2(yg), 'w_o_ssm', out_dtype=BF16))

    merged = make_rowwise("merge", _f_merge, 4, 0, 0, ('row',), ts_cap=1024,
                          windows={2: win('gate_a', D_MODEL), 3: win('gate_b', D_MODEL)})(
        (attn, ssm, wide, wide), (), (), (stand['gate_a'], stand['gate_b']))[0]
    mix = rows3(lin("w_out", rows2(merged), 'w_out', out_dtype=BF16))
    x1 = make_rowwise("post_mix", _f_post, 2, 1, 1, ('row',), ts_cap=1024)(
        (x_res, mix), (gate1,), (small['g_post_mix'],))[0]

    h2, x1_res = make_rowwise("modulate2", _f_modulate, 1, 2, 1, ('row',), forward_row=0, ts_cap=1024)(
        (x1,), (scale2, shift2), (small['g_pre_mlp'],))
    ff = rows3(ffn(rows2(h2), wb['w_ff1'], toks['w_ff1'], wb['w_ff2'], toks['w_ff2']))
    lvec = make_rowwise("final_loss", _f_final_loss, 3, 1, 1, ('sum',), nodiff=(2,), ts_cap=1024)(
        (x1_res, ff, target), (gate2,), (small['g_post_mlp'],))[0]
    return jnp.sum(lvec)


MATRICES = COL_SHARDED + ROW_SHARDED
STACKED_DW = ('w_ada', 'w_ff1')


def _local_step(x, c, positions, target, wb, small):
    B = x.shape[0]
    c8 = jnp.concatenate([c, jnp.zeros((16 - B, c.shape[1]), F32)], axis=0)
    posf = positions.astype(F32)[..., None]
    toks = {k: jnp.zeros(wb[k].shape, F32) for k in MATRICES if k != 'conv_w'}
    for k in STACKED_DW:
        rows, cols = wb[k].shape
        toks[k] = jnp.zeros((4, rows, cols // 4), F32)
    conv_w = wb['conv_w_f32']

    def loss_fn(toks, small, conv_w, x):
        wbl = dict(wb)
        wbl['conv_w_f32'] = conv_w
        return _local_loss(toks, small, x, wbl, c8, posf, target)

    loss, (g_tok, g_small, g_conv, g_x) = jax.value_and_grad(loss_fn, argnums=(0, 1, 2, 3))(toks, small, conv_w, x)
    grads = dict(g_tok)
    grads.update(g_small)
    grads['conv_w'] = g_conv
    return loss, g_x, grads


def kernel(x, c, positions, w_ada, b_ada, g_pre_mix, g_post_mix, w_in, g_q_lat, g_kv_lat, w_uq, w_ukv, w_o_attn, conv_w, conv_b, dt_bias, a_log, d_skip, g_ssm_out, w_o_ssm, w_out, g_pre_mlp, g_post_mlp, w_ff1, w_ff2, loss_target, m_w_ada, m_b_ada, m_g_pre_mix, m_g_post_mix, m_w_in, m_g_q_lat, m_g_kv_lat, m_w_uq, m_w_ukv, m_w_o_attn, m_conv_w, m_conv_b, m_dt_bias, m_a_log, m_d_skip, m_g_ssm_out, m_w_o_ssm, m_w_out, m_g_pre_mlp, m_g_post_mlp, m_w_ff1, m_w_ff2, v_w_ada, v_b_ada, v_g_pre_mix, v_g_post_mix, v_w_in, v_g_q_lat, v_g_kv_lat, v_w_uq, v_w_ukv, v_w_o_attn, v_conv_w, v_conv_b, v_dt_bias, v_a_log, v_d_skip, v_g_ssm_out, v_w_o_ssm, v_w_out, v_g_pre_mlp, v_g_post_mlp, v_w_ff1, v_w_ff2):
    given = dict(locals())
    w_loc = {n: given[n] for n in WEIGHTS}
    m_loc = {n: given["m_" + n] for n in WEIGHTS}
    v_loc = {n: given["v_" + n] for n in WEIGHTS}
    mats = [n for n in WEIGHTS if n in MATRICES and n != 'conv_w']
    vecs = [n for n in WEIGHTS if n not in MATRICES]

    own = [w_loc[n][0].astype(BF16) for n in mats]
    g_mats, g_conv = _gather_weights(own, conv_w[0])
    chip = 2 * lax.axis_index("x") + lax.axis_index("y")
    wb = {}
    for n, g, mine in zip(mats, g_mats, own):
        g = lax.dynamic_update_slice_in_dim(g, mine[None], chip, axis=0)
        if n in COL_SHARDED:
            wb[n] = jnp.transpose(g, (1, 0, 2)).reshape(g.shape[1], -1)
        else:
            wb[n] = g.reshape(-1, g.shape[2])
    wb['conv_w_f32'] = jnp.transpose(g_conv, (1, 0, 2)).reshape(CONV_K, -1)
    small = {n: w_loc[n] for n in vecs}

    loss_part, grad_x, grads = _local_step(x, c, positions, loss_target, wb, small)
    loss = lax.psum(loss_part, ("x", "y", "c"))

    stacks = []
    for n in mats:
        kk, nn = w_loc[n].shape[1:]
        if n in STACKED_DW:
            stacks.append(grads[n])
        elif n in COL_SHARDED:
            stacks.append(jnp.transpose(grads[n].reshape(kk, 4, nn), (1, 0, 2)))
        else:
            stacks.append(grads[n].reshape(4, kk, nn))
    g_mine, g_other = _reduce_matrices(stacks, mats)
    g_shard = {}

    vec_shapes = [tuple(grads[n].shape) for n in vecs] + [tuple(grads['conv_w'].shape)]
    total = _stack_sum(_gather_small(_pack_small([grads[n] for n in vecs] + [grads['conv_w']])), "grad_sum_small")
    g_vec = _unpack_small(total, vec_shapes)
    n_conv = conv_w.shape[2]
    chip = 2 * lax.axis_index("x") + lax.axis_index("y")
    g_shard['conv_w'] = lax.dynamic_slice_in_dim(g_vec[-1], chip * n_conv, n_conv, axis=1)
    for n, g in zip(vecs, g_vec):
        g_shard[n] = g

    delta, new_m, new_v = {}, {}, {}
    cidx = lax.axis_index("c").astype(jnp.int32).reshape(1)
    for n, mine, other in zip(mats, g_mine, g_other):
        g_shard[n], delta[n], new_m[n], new_v[n] = _adam_halves_call(
            w_loc[n], mine, other, cidx, m_loc[n], v_loc[n], "adamw_" + n)
    rest = vecs + ['conv_w']
    rest_shapes = [tuple(w_loc[n].shape) for n in rest]
    packed = [_pack_small([src[n] for n in rest]) for src in (w_loc, g_shard, m_loc, v_loc)]
    for dst, buf in zip((delta, new_m, new_v), _adam_call(*packed, "adamw_small")):
        dst.update(zip(rest, _unpack_small(buf, rest_shapes)))

    def out(d):
        return [d[n].reshape(w_loc[n].shape) for n in WEIGHTS]

    return (loss, grad_x, *out(g_shard), *out(delta), *out(new_m), *out(new_v))
```

```python
import numpy as np
import jax
import jax.numpy as jnp
from jax import lax
from jax.experimental import pallas as pl
from jax.experimental.pallas import tpu as pltpu

F32 = jnp.float32
BF16 = jnp.bfloat16
MESH = pl.DeviceIdType.MESH

D_MODEL = 1024
N_HEADS = 8
NOPE = 128
ROPE = 64
V_DIM = 128
Q_RANK = 256
KV_RANK = 256
ROPE_THETA = 10000.0
D_INNER = 2048
SSM_HEADS = 32
SSM_GROUPS = 8
HEAD_P = 64
STATE_N = 128
CONV_K = 4
CHUNK = 128
CONV_CH = D_INNER + 2 * SSM_GROUPS * STATE_N
EPS = 1e-6
IN_SIZES = (Q_RANK, KV_RANK, ROPE, D_INNER, CONV_CH, SSM_HEADS, D_MODEL, D_MODEL)
ADAM_LR, ADAM_B1, ADAM_B2, ADAM_EPS, ADAM_WD, ADAM_STEP = 0.001, 0.9, 0.999, 1e-08, 0.01, 10

VMEM_LIMIT_BYTES = 52 * 1024 * 1024
LANE = 128
QK_PAD = 256

WEIGHTS = ['w_ada', 'b_ada', 'g_pre_mix', 'g_post_mix', 'w_in', 'g_q_lat', 'g_kv_lat', 'w_uq', 'w_ukv',
           'w_o_attn', 'conv_w', 'conv_b', 'dt_bias', 'a_log', 'd_skip', 'g_ssm_out', 'w_o_ssm', 'w_out',
           'g_pre_mlp', 'g_post_mlp', 'w_ff1', 'w_ff2']
COL_SHARDED = ('w_ada', 'w_in', 'w_uq', 'w_ukv', 'conv_w', 'w_ff1')
ROW_SHARDED = ('w_o_attn', 'w_o_ssm', 'w_out', 'w_ff2')


def _cparams(sem):
    return pltpu.CompilerParams(dimension_semantics=sem, vmem_limit_bytes=VMEM_LIMIT_BYTES)


def _tile(n, cap):
    if n <= cap:
        return n
    k = n // LANE
    best = LANE
    for d in range(1, k + 1):
        if k % d == 0 and d * LANE <= cap:
            best = d * LANE
    return best


def _mm(a, w, name, out_dtype=F32, epilogue=None, extras=(), out_dtypes=None):
    M, K = a.shape
    N = w.shape[1]
    tm = min(M, 2048 if K <= 1024 else 1024)
    tn = _tile(N, 1024)
    tk = _tile(K, 2048)
    nk = K // tk
    dts = tuple(out_dtypes) if epilogue is not None else (out_dtype,)
    n_x, n_o = len(extras), len(dts)

    def finish(acc, refs):
        res = epilogue(acc, *[r[...] for r in refs[:n_x]]) if epilogue is not None else (acc,)
        for o_ref, val, dt in zip(refs[n_x:n_x + n_o], res, dts):
            o_ref[...] = val.astype(dt)

    def body(a_ref, w_ref, *refs):
        part = jnp.dot(a_ref[...].astype(BF16), w_ref[...], preferred_element_type=F32)
        if nk == 1:
            finish(part, refs)
        else:
            acc_ref = refs[-1]
            k = pl.program_id(2)

            @pl.when(k == 0)
            def _():
                acc_ref[...] = part

            @pl.when(k > 0)
            def _():
                acc_ref[...] += part

            @pl.when(k == nk - 1)
            def _():
                finish(acc_ref[...], refs)

    ospec = pl.BlockSpec((tm, tn), lambda i, j, k: (i, j))
    res = pl.pallas_call(
        body, grid=(M // tm, N // tn, nk),
        in_specs=[pl.BlockSpec((tm, tk), lambda i, j, k: (i, k)), pl.BlockSpec((tk, tn), lambda i, j, k: (k, j))]
        + [ospec] * n_x,
        out_specs=[ospec] * n_o, out_shape=[jax.ShapeDtypeStruct((M, N), dt) for dt in dts],
        scratch_shapes=[pltpu.VMEM((tm, tn), F32)] if nk > 1 else [], name=name,
        compiler_params=_cparams(("parallel", "parallel", "arbitrary")))(a, w, *extras)
    return res if epilogue is not None else res[0]


def _mm_tn(a, g, name, col_shards=1):
    M, K = a.shape
    N = g.shape[1]
    tm = min(M, 2048)
    tk = _tile(K, 1024)
    tn = _tile(N // col_shards, 1024)
    nm = M // tm
    per = N // col_shards // tn

    def body(a_ref, g_ref, o_ref):
        part = lax.dot_general(a_ref[...].astype(BF16), g_ref[...].astype(BF16), (((0,), (0,)), ((), ())),
                               preferred_element_type=F32)
        m = pl.program_id(2)

        @pl.when(m == 0)
        def _():
            o_ref[...] = part.reshape(o_ref.shape)

        @pl.when(m > 0)
        def _():
            o_ref[...] += part.reshape(o_ref.shape)

    if col_shards == 1:
        out_spec = pl.BlockSpec((tk, tn), lambda i, j, m: (i, j))
        out_shape = jax.ShapeDtypeStruct((K, N), F32)
    else:
        out_spec = pl.BlockSpec((1, tk, tn), lambda i, j, m: (j // per, i, j % per))
        out_shape = jax.ShapeDtypeStruct((col_shards, K, N // col_shards), F32)
    return pl.pallas_call(
        body, grid=(K // tk, N // tn, nm),
        in_specs=[pl.BlockSpec((tm, tk), lambda i, j, m: (m, i)), pl.BlockSpec((tm, tn), lambda i, j, m: (m, j))],
        out_specs=out_spec, out_shape=out_shape, name=name,
        compiler_params=_cparams(("parallel", "parallel", "arbitrary")))(a, g)


def make_linear(name, out_dtype=F32, dw_col_shards=1):
    @jax.custom_vjp
    def linear(a, w, tok):
        return _mm(a, w, name + "_fwd", out_dtype)

    def fwd(a, w, tok):
        return _mm(a, w, name + "_fwd", out_dtype), (a, w)

    def bwd(res, g):
        a, w = res
        da = _mm(g, w.T, name + "_dx", a.dtype)
        dw = _mm_tn(a, g, name + "_dw", dw_col_shards)
        return da, jnp.zeros_like(w), dw

    linear.defvjp(fwd, bwd)
    return linear


def _relu2_epilogue(acc):
    r = jnp.maximum(acc, 0.0)
    return r * r, r


def _relu2_bwd_epilogue(acc, r):
    return (acc * (2.0 * r.astype(F32)),)


@jax.custom_vjp
def ffn(h, w1, tok1, w2, tok2):
    act, _ = _mm(h, w1, "w_ff1_fwd", epilogue=_relu2_epilogue, out_dtypes=(BF16, BF16))
    return _mm(act, w2, "w_ff2_fwd", BF16)


def _ffn_fwd(h, w1, tok1, w2, tok2):
    act, r = _mm(h, w1, "w_ff1_fwd", epilogue=_relu2_epilogue, out_dtypes=(BF16, BF16))
    return _mm(act, w2, "w_ff2_fwd", BF16), (h, w1, w2, act, r)


def _ffn_bwd(res, g):
    h, w1, w2, act, r = res
    du = _mm(g, w2.T, "w_ff2_dx", epilogue=_relu2_bwd_epilogue, extras=(r,), out_dtypes=(BF16,))[0]
    dw2 = _mm_tn(act, g, "w_ff2_dw")
    dw1 = _mm_tn(h, du, "w_ff1_dw", 4)
    dh = _mm(du, w1.T, "w_ff1_dx", h.dtype)
    return dh, jnp.zeros_like(w1), dw1, jnp.zeros_like(w2), dw2


ffn.defvjp(_ffn_fwd, _ffn_bwd)


def make_rowwise(name, f, n_rows, n_seqs, n_pars, out_kinds, ncol=1, nodiff=(), ts_cap=512, windows=None,
                 forward_row=None):
    windows = dict(windows or {})
    n_in = n_rows + n_seqs + n_pars
    diff_idx = [i for i in range(n_in) if i not in nodiff]

    def _dims(rows):
        B, S = rows[0].shape[0], rows[0].shape[1]
        ts = min(S, ts_cap)
        return B, S, ts

    def _width(i, r):
        return windows[i][1] if i in windows else r.shape[2]

    def _in_specs(rows, seqs, pars, ts):
        specs = []
        for i, r in enumerate(rows):
            col0 = windows[i][0] if i in windows else 0
            specs.append(pl.BlockSpec((1, ts, _width(i, r) // ncol), lambda k, b, s, col0=col0: (b, s, k + col0)))
        for q in seqs:
            specs.append(pl.BlockSpec((1, 1, q.shape[2] // ncol), lambda k, b, s: (b, 0, k)))
        for p in pars:
            specs.append(pl.BlockSpec((1, p.shape[1] // ncol), lambda k, b, s: (0, k)))
        return specs

    def _load(refs):
        vals = [r[0] for r in refs[:n_rows + n_seqs]]
        vals += [r[...] for r in refs[n_rows + n_seqs:n_in]]
        return vals

    def _out_struct(rows, seqs, pars, ts):
        blocks = [jax.ShapeDtypeStruct((ts, _width(i, r) // ncol), r.dtype) for i, r in enumerate(rows)]
        blocks += [jax.ShapeDtypeStruct((1, q.shape[2] // ncol), q.dtype) for q in seqs]
        blocks += [jax.ShapeDtypeStruct((1, p.shape[1] // ncol), p.dtype) for p in pars]
        return jax.eval_shape(f, *blocks)

    def _fwd_call(rows, seqs, pars):
        B, S, ts = _dims(rows)
        outs = _out_struct(rows, seqs, pars, ts)
        n_out = len(outs)

        def body(*refs):
            res = f(*_load(refs))
            first = (pl.program_id(1) == 0) & (pl.program_id(2) == 0)
            for o_ref, val, kind in zip(refs[n_in:], res, out_kinds):
                if kind == 'row':
                    o_ref[0] = val
                else:
                    tot = jnp.sum(val, axis=0, keepdims=True)

                    @pl.when(first)
                    def _(o_ref=o_ref, tot=tot):
                        o_ref[...] = tot

                    @pl.when(jnp.logical_not(first))
                    def _(o_ref=o_ref, tot=tot):
                        o_ref[...] += tot

        out_shape, out_specs = [], []
        for o, kind in zip(outs, out_kinds):
            d = o.shape[1]
            if kind == 'row':
                out_shape.append(jax.ShapeDtypeStruct((B, S, ncol * d), o.dtype))
                out_specs.append(pl.BlockSpec((1, ts, d), lambda k, b, s: (b, s, k)))
            else:
                out_shape.append(jax.ShapeDtypeStruct((1, ncol * d), o.dtype))
                out_specs.append(pl.BlockSpec((1, d), lambda k, b, s: (0, k)))
        res = pl.pallas_call(
            body, grid=(ncol, B, S // ts), in_specs=_in_specs(rows, seqs, pars, ts), out_specs=out_specs,
            out_shape=out_shape, name=name + "_fwd",
            compiler_params=_cparams(("arbitrary", "arbitrary", "arbitrary")))(*rows, *seqs, *pars)
        return tuple(res)

    def _bwd_call(rows, seqs, pars, cots, carried=None):
        B, S, ts = _dims(rows)
        outs = _out_struct(rows, seqs, pars, ts)
        n_out = len(outs)
        all_in = list(rows) + list(seqs) + list(pars)
        extra = [] if carried is None else [carried]

        def body(*refs):
            vals = _load(refs)
            if carried is not None:
                carried_ref, refs = refs[n_in + n_out], refs[:n_in + n_out] + refs[n_in + n_out + 1:]
            cts = []
            for c_ref, o, kind in zip(refs[n_in:n_in + n_out], outs, out_kinds):
                if kind == 'row':
                    cts.append(c_ref[0])
                else:
                    cts.append(jnp.broadcast_to(c_ref[...], o.shape))

            def g(*dv):
                full = list(vals)
                for i, v in zip(diff_idx, dv):
                    full[i] = v
                return tuple(f(*full))

            _, vjp = jax.vjp(g, *[vals[i] for i in diff_idx])
            grads = vjp(tuple(cts))
            b, s = pl.program_id(1), pl.program_id(2)
            for o_ref, i, gr in zip(refs[n_in + n_out:], diff_idx, grads):
                if i < n_rows:
                    if carried is not None and i == forward_row:
                        gr = gr + carried_ref[0]
                    o_ref[0] = gr.astype(o_ref.dtype)
                else:
                    first = (s == 0) if i < n_rows + n_seqs else ((b == 0) & (s == 0))
                    target = (lambda r: r.at[0]) if i < n_rows + n_seqs else (lambda r: r)

                    @pl.when(first)
                    def _(o_ref=o_ref, gr=gr, target=target):
                        target(o_ref)[...] = gr

                    @pl.when(jnp.logical_not(first))
                    def _(o_ref=o_ref, gr=gr, target=target):
                        target(o_ref)[...] += gr

        cot_specs = []
        for o, kind in zip(outs, out_kinds):
            d = o.shape[1]
            if kind == 'row':
                cot_specs.append(pl.BlockSpec((1, ts, d), lambda k, b, s: (b, s, k)))
            else:
                cot_specs.append(pl.BlockSpec((1, d), lambda k, b, s: (0, k)))
        out_shape, out_specs = [], []
        for i in diff_idx:
            a = all_in[i]
            if i < n_rows:
                out_shape.append(jax.ShapeDtypeStruct((B, S, _width(i, a)), BF16 if i in windows else a.dtype))
                out_specs.append(pl.BlockSpec((1, ts, _width(i, a) // ncol), lambda k, b, s: (b, s, k)))
                continue
            out_shape.append(jax.ShapeDtypeStruct(a.shape, a.dtype))
            if i < n_rows + n_seqs:
                out_specs.append(pl.BlockSpec((1, 1, a.shape[2] // ncol), lambda k, b, s: (b, 0, k)))
            else:
                out_specs.append(pl.BlockSpec((1, a.shape[1] // ncol), lambda k, b, s: (0, k)))
        if carried is not None:
            cot_specs.append(pl.BlockSpec((1, ts, carried.shape[2] // ncol), lambda k, b, s: (b, s, k)))
        res = pl.pallas_call(
            body, grid=(ncol, B, S // ts), in_specs=_in_specs(rows, seqs, pars, ts) + cot_specs,
            out_specs=out_specs, out_shape=out_shape, name=name + "_bwd",
            compiler_params=_cparams(("arbitrary", "arbitrary", "arbitrary")))(*all_in, *cots, *extra)
        grads = [None] * n_in
        for i, r in zip(diff_idx, res):
            grads[i] = r
        for i in nodiff:
            grads[i] = jnp.zeros_like(all_in[i])
        stand_in_grads = tuple(grads[i] for i in sorted(windows))
        for i in windows:
            grads[i] = jnp.zeros_like(all_in[i])
        return (tuple(grads[:n_rows]), tuple(grads[n_rows:n_rows + n_seqs]), tuple(grads[n_rows + n_seqs:]),
                stand_in_grads)

    def _outputs(rows, seqs, pars):
        res = _fwd_call(rows, seqs, pars)
        return res if forward_row is None else res + (rows[forward_row],)

    @jax.custom_vjp
    def op(rows, seqs, pars, stand_ins):
        return _outputs(rows, seqs, pars)

    def fwd(rows, seqs, pars, stand_ins):
        return _outputs(rows, seqs, pars), (rows, seqs, pars)

    def bwd(res, cots):
        rows, seqs, pars = res
        if forward_row is None:
            return _bwd_call(rows, seqs, pars, cots)
        return _bwd_call(rows, seqs, pars, cots[:-1], cots[-1])

    op.defvjp(fwd, bwd)
    return lambda rows, seqs, pars, stand_ins=(): op(tuple(rows), tuple(seqs), tuple(pars), tuple(stand_ins))


def _rms(x, g):
    x = x.astype(F32)
    return x * lax.rsqrt(jnp.mean(x * x, axis=-1, keepdims=True) + EPS) * g


def _silu(x):
    return x * lax.logistic(x)


def _f_silu(c):
    return (_silu(c),)


def _f_modulate(x, scale, shift, g):
    return ((_rms(x, g) * (1.0 + scale) + shift).astype(BF16),)


def _f_rms(x, g):
    return (_rms(x, g).astype(BF16),)


def _f_dt(dt_raw, dt_bias, a_log):
    z = dt_raw + dt_bias
    dt = jnp.maximum(z, 0.0) + jnp.log1p(jnp.exp(-jnp.abs(z)))
    return dt, dt * (-jnp.exp(a_log))


def _f_gated_norm(y, z, g):
    return (_rms(y * _silu(z.astype(F32)), g).astype(BF16),)


def _f_merge(attn, ssm, ga, gb):
    return ((lax.logistic(ga.astype(F32)) * attn + lax.logistic(gb.astype(F32)) * ssm).astype(BF16),)


def _f_post(x, m, gate, g):
    return (x + gate * _rms(m, g),)


def _f_final_loss(x, ff, target, gate, g):
    e = x + gate * _rms(ff, g) - target
    return (e * e * (0.5 / D_MODEL),)


def _rope_tables(posf, inv_lane):
    B, S, _ = posf.shape
    ts = min(S, 512)

    def body(p_ref, inv_ref, c_ref, a_ref, b_ref):
        ang = p_ref[0] * inv_ref[...]
        cs, sn = jnp.cos(ang), jnp.sin(ang)
        lane = lax.broadcasted_iota(jnp.int32, ang.shape, 1)
        c_ref[0] = jnp.where(lane < ROPE, cs, 0.0)
        a_ref[0] = jnp.where(lane < ROPE // 2, -sn, 0.0)
        b_ref[0] = jnp.where((lane >= ROPE // 2) & (lane < ROPE), sn, 0.0)

    spec = pl.BlockSpec((1, ts, LANE), lambda b, s: (b, s, 0))
    sds = jax.ShapeDtypeStruct((B, S, LANE), F32)
    return pl.pallas_call(
        body, grid=(B, S // ts),
        in_specs=[pl.BlockSpec((1, ts, 1), lambda b, s: (b, s, 0)), pl.BlockSpec((1, LANE), lambda b, s: (0, 0))],
        out_specs=[spec, spec, spec], out_shape=[sds, sds, sds], name="rope_tables",
        compiler_params=_cparams(("parallel", "parallel")))(posf, inv_lane)


def _rot(u, c, a, bm):
    return u * c + pltpu.roll(u, 96, 1) * a + pltpu.roll(u, 32, 1) * bm


def _rot_t(g, c, a, bm):
    return g * c + pltpu.roll(g * a, 32, 1) + pltpu.roll(g * bm, 96, 1)


def _rope_q_call(q, tabs, transpose, name):
    B, S, W = q.shape
    ts = min(S, 1024)
    fn = _rot_t if transpose else _rot
    out_dtype = BF16

    def body(q_ref, c_ref, a_ref, b_ref, o_ref):
        tc, ta, tb = c_ref[0], a_ref[0], b_ref[0]
        for h in range(W // QK_PAD):
            u = q_ref[0, :, h * QK_PAD:(h + 1) * QK_PAD].astype(F32) * ATT_SCALE
            r = fn(u[:, NOPE:], tc, ta, tb)
            o_ref[0, :, h * QK_PAD:(h + 1) * QK_PAD] = jnp.concatenate([u[:, :NOPE], r], axis=1).astype(out_dtype)

    tspec = pl.BlockSpec((1, ts, LANE), lambda b, s: (b, s, 0))
    qspec = pl.BlockSpec((1, ts, W), lambda b, s: (b, s, 0))
    return pl.pallas_call(
        body, grid=(B, S // ts), in_specs=[qspec, tspec, tspec, tspec], out_specs=qspec,
        out_shape=jax.ShapeDtypeStruct(q.shape, out_dtype), name=name,
        compiler_params=_cparams(("parallel", "parallel")))(q, *tabs)


@jax.custom_vjp
def rope_q(q, tabs):
    return _rope_q_call(q, tabs, False, "rope_q_fwd")


def _rope_q_fwd(q, tabs):
    return _rope_q_call(q, tabs, False, "rope_q_fwd"), tabs


def _rope_q_bwd(tabs, g):
    return _rope_q_call(g, tabs, True, "rope_q_bwd"), tuple(jnp.zeros_like(t) for t in tabs)


rope_q.defvjp(_rope_q_fwd, _rope_q_bwd)


def _build_k_fwd_call(kv, kr, tabs):
    B, S, _ = kv.shape
    ts = min(S, 1024)

    def body(kv_ref, kr_ref, c_ref, a_ref, b_ref, o_ref):
        r = _rot(kr_ref[0], c_ref[0], a_ref[0], b_ref[0]).astype(BF16)
        for h in range(N_HEADS):
            o_ref[0, :, h * QK_PAD:(h + 1) * QK_PAD] = jnp.concatenate(
                [kv_ref[0, :, h * NOPE:(h + 1) * NOPE], r], axis=1)

    tspec = pl.BlockSpec((1, ts, LANE), lambda b, s: (b, s, 0))
    kr_spec = pl.BlockSpec((1, ts, LANE), lambda b, s: (b, s, KR_LANE0 // LANE))
    return pl.pallas_call(
        body, grid=(B, S // ts),
        in_specs=[pl.BlockSpec((1, ts, N_HEADS * NOPE), lambda b, s: (b, s, 0)), kr_spec, tspec, tspec, tspec],
        out_specs=pl.BlockSpec((1, ts, N_HEADS * QK_PAD), lambda b, s: (b, s, 0)),
        out_shape=jax.ShapeDtypeStruct((B, S, N_HEADS * QK_PAD), BF16), name="build_k_fwd",
        compiler_params=_cparams(("parallel", "parallel")))(kv, kr, *tabs)


def _build_k_bwd_call(g, tabs):
    B, S, _ = g.shape
    ts = min(S, 1024)

    def body(g_ref, c_ref, a_ref, b_ref, dk_ref, dr_ref):
        tot = None
        for h in range(N_HEADS):
            dk_ref[0, :, h * NOPE:(h + 1) * NOPE] = g_ref[0, :, h * QK_PAD:h * QK_PAD + NOPE]
            part = g_ref[0, :, h * QK_PAD + NOPE:(h + 1) * QK_PAD].astype(F32)
            tot = part if tot is None else tot + part
        dr_ref[0] = _rot_t(tot, c_ref[0], a_ref[0], b_ref[0]).astype(BF16)

    tspec = pl.BlockSpec((1, ts, LANE), lambda b, s: (b, s, 0))
    return pl.pallas_call(
        body, grid=(B, S // ts),
        in_specs=[pl.BlockSpec((1, ts, N_HEADS * QK_PAD), lambda b, s: (b, s, 0)), tspec, tspec, tspec],
        out_specs=[pl.BlockSpec((1, ts, N_HEADS * NOPE), lambda b, s: (b, s, 0)), tspec],
        out_shape=[jax.ShapeDtypeStruct((B, S, N_HEADS * NOPE), BF16), jax.ShapeDtypeStruct((B, S, LANE), BF16)],
        name="build_k_bwd", compiler_params=_cparams(("parallel", "parallel")))(g, *tabs)


ATT_SCALE = (NOPE + ROPE) ** -0.5
NEG = -1e30


def _att_tiles(S):
    t = min(S, 512)
    return t, S // t


def _scores(q, k, diagonal):
    s = lax.dot_general(q, k, (((1,), (1,)), ((), ())), preferred_element_type=F32)
    if diagonal:
        row = lax.broadcasted_iota(jnp.int32, s.shape, 0)
        col = lax.broadcasted_iota(jnp.int32, s.shape, 1)
        s = jnp.where(col <= row, s, NEG)
    return s


ATT_HB = 8


def _causal_pairs(n):
    pairs = [(i, j) for i in range(n) for j in range(i + 1)]
    return (jnp.asarray([p[0] for p in pairs], jnp.int32), jnp.asarray([p[1] for p in pairs], jnp.int32))


def _head(ref_or_val, h, w):
    return ref_or_val[:, h * w:(h + 1) * w]


def _attn_fwd_call(q, k, vsrc, v_blk0):
    B, S, _ = q.shape
    t, n = _att_tiles(S)
    qi, kj = _causal_pairs(n)

    def body(qi_ref, kj_ref, q_ref, k_ref, v_ref, o_ref, lse_ref, m_sc, l_sc, acc_sc):
        p_id = pl.program_id(2)
        i, j = qi_ref[p_id], kj_ref[p_id]

        @pl.when(j == 0)
        def _():
            m_sc[...] = jnp.full(m_sc.shape, NEG, F32)
            l_sc[...] = jnp.zeros(l_sc.shape, F32)
            acc_sc[...] = jnp.zeros(acc_sc.shape, F32)

        def step(diagonal):
            qa, ka, va = q_ref[0], k_ref[0], v_ref[0]
            for h in range(ATT_HB):
                lanes = slice(h * LANE, (h + 1) * LANE)
                s = _scores(_head(qa, h, QK_PAD), _head(ka, h, QK_PAD), diagonal)
                m_prev = m_sc[:, lanes]
                m_new = jnp.maximum(m_prev, jnp.max(s, axis=1, keepdims=True))
                alpha = jnp.exp(m_prev - m_new)
                p = jnp.exp(s - jnp.tile(m_new, (1, t // LANE)))
                l_sc[:, lanes] = alpha * l_sc[:, lanes] + jnp.sum(p, axis=1, keepdims=True)
                acc_sc[:, lanes] = alpha * acc_sc[:, lanes] + jnp.dot(p.astype(BF16), _head(va, h, V_DIM),
                                                                      preferred_element_type=F32)
                m_sc[:, lanes] = m_new

        @pl.when(j < i)
        def _():
            step(False)

        @pl.when(j == i)
        def _():
            step(True)
            o_ref[0] = (acc_sc[...] / l_sc[...]).astype(BF16)
            lse_ref[0] = m_sc[...] + jnp.log(l_sc[...])

    wq, wv = ATT_HB * QK_PAD, ATT_HB * V_DIM
    grid_spec = pltpu.PrefetchScalarGridSpec(
        num_scalar_prefetch=2, grid=(B, N_HEADS // ATT_HB, qi.shape[0]),
        in_specs=[pl.BlockSpec((1, t, wq), lambda b, h, p, qi, kj: (b, qi[p], h)),
                  pl.BlockSpec((1, t, wq), lambda b, h, p, qi, kj: (b, kj[p], h)),
                  pl.BlockSpec((1, t, wv), lambda b, h, p, qi, kj: (b, kj[p], v_blk0 + h))],
        out_specs=[pl.BlockSpec((1, t, wv), lambda b, h, p, qi, kj: (b, qi[p], h)),
                   pl.BlockSpec((1, t, wv), lambda b, h, p, qi, kj: (b, qi[p], h))],
        scratch_shapes=[pltpu.VMEM((t, wv), F32), pltpu.VMEM((t, wv), F32), pltpu.VMEM((t, wv), F32)])
    return pl.pallas_call(
        body, grid_spec=grid_spec,
        out_shape=[jax.ShapeDtypeStruct((B, S, N_HEADS * V_DIM), BF16),
                   jax.ShapeDtypeStruct((B, S, N_HEADS * LANE), F32)],
        name="attn_fwd", compiler_params=_cparams(("parallel", "parallel", "arbitrary")))(qi, kj, q, k, vsrc)


def _attn_p_ds(q, k, v, o, do, lse, diagonal, t):
    s = _scores(q, k, diagonal)
    p = jnp.exp(s - jnp.tile(lse, (1, t // LANE)))
    dp = lax.dot_general(do.astype(BF16), v, (((1,), (1,)), ((), ())), preferred_element_type=F32)
    delta = jnp.sum(do.astype(F32) * o.astype(F32), axis=1, keepdims=True)
    ds = p * (dp - delta)
    return p, ds


ATT_HB_BWD = 4


def _attn_bwd_call(q, k, vsrc, o, do, lse):
    B, S, _ = q.shape
    t, n = _att_tiles(S)
    qi, kj = _causal_pairs(n)
    n_pairs = qi.shape[0]
    hb = ATT_HB_BWD
    v_blk0 = N_HEADS // hb

    def body(qi_ref, kj_ref, q_ref, k_ref, v_ref, o_ref, do_ref, lse_ref, dq_ref, dk_ref, dv_ref, dq_sc, dk_sc, dv_sc):
        p_id = pl.program_id(2)
        i, j = qi_ref[p_id], kj_ref[p_id]

        @pl.when(p_id == 0)
        def _():
            dk_sc[...] = jnp.zeros(dk_sc.shape, F32)
            dv_sc[...] = jnp.zeros(dv_sc.shape, F32)

        @pl.when(j == 0)
        def _():
            dq_sc[...] = jnp.zeros(dq_sc.shape, F32)

        rows = pl.ds(pl.multiple_of(j * t, t), t)

        def step(diagonal):
            qa, ka, va, oa, doa, la = q_ref[0], k_ref[0], v_ref[0], o_ref[0], do_ref[0], lse_ref[0]
            for h in range(hb):
                qb, kb, dob = _head(qa, h, QK_PAD), _head(ka, h, QK_PAD), _head(doa, h, V_DIM)
                p, ds = _attn_p_ds(qb, kb, _head(va, h, V_DIM), _head(oa, h, V_DIM), dob, _head(la, h, LANE),
                                   diagonal, t)
                dsb = ds.astype(BF16)
                dq_sc[:, h * QK_PAD:(h + 1) * QK_PAD] += jnp.dot(dsb, kb, preferred_element_type=F32)
                dv_sc[rows, h * V_DIM:(h + 1) * V_DIM] += lax.dot_general(
                    p.astype(BF16), dob.astype(BF16), (((0,), (0,)), ((), ())), preferred_element_type=F32)
                dk_sc[rows, h * QK_PAD:(h + 1) * QK_PAD] += lax.dot_general(
                    dsb, qb, (((0,), (0,)), ((), ())), preferred_element_type=F32)

        @pl.when(j < i)
        def _():
            step(False)

        @pl.when(j == i)
        def _():
            step(True)
            dq_ref[0] = dq_sc[...].astype(BF16)

        @pl.when(i == n - 1)
        def _():
            dk_ref[0] = dk_sc[rows, :].astype(BF16)
            dv_ref[0] = dv_sc[rows, :].astype(BF16)

    wq, wv = hb * QK_PAD, hb * V_DIM
    at_q = lambda b, h, p, qi, kj: (b, qi[p], h)
    at_k = lambda b, h, p, qi, kj: (b, kj[p], h)
    at_done = lambda b, h, p, qi, kj: (b, jnp.where(qi[p] == n - 1, kj[p], 0), h)
    grid_spec = pltpu.PrefetchScalarGridSpec(
        num_scalar_prefetch=2, grid=(B, N_HEADS // hb, n_pairs),
        in_specs=[pl.BlockSpec((1, t, wq), at_q), pl.BlockSpec((1, t, wq), at_k),
                  pl.BlockSpec((1, t, wv), lambda b, h, p, qi, kj: (b, kj[p], v_blk0 + h)),
                  pl.BlockSpec((1, t, wv), at_q), pl.BlockSpec((1, t, wv), at_q), pl.BlockSpec((1, t, wv), at_q)],
        out_specs=[pl.BlockSpec((1, t, wq), at_q), pl.BlockSpec((1, t, wq), at_done), pl.BlockSpec((1, t, wv), at_done)],
        scratch_shapes=[pltpu.VMEM((t, wq), F32), pltpu.VMEM((S, wq), F32), pltpu.VMEM((S, wv), F32)])
    return pl.pallas_call(
        body, grid_spec=grid_spec,
        out_shape=[jax.ShapeDtypeStruct((B, S, N_HEADS * QK_PAD), BF16),
                   jax.ShapeDtypeStruct((B, S, N_HEADS * QK_PAD), BF16),
                   jax.ShapeDtypeStruct((B, S, N_HEADS * V_DIM), BF16)],
        name="attn_bwd", compiler_params=_cparams(("parallel", "parallel", "arbitrary")))(
            qi, kj, q, k, vsrc, o, do, lse)


@jax.custom_vjp
def attention(q, kv, src, stand_in, tabs):
    return _attn_fwd_call(q, _build_k_fwd_call(kv, src, tabs), kv, N_HEADS // ATT_HB)[0]


def _attention_fwd(q, kv, src, stand_in, tabs):
    k = _build_k_fwd_call(kv, src, tabs)
    o, lse = _attn_fwd_call(q, k, kv, N_HEADS // ATT_HB)
    return o, (q, k, kv, o, lse, src, tabs)


def _attention_bwd(res, do):
    q, k, kv, o, lse, src, tabs = res
    dq, dk, dv = _attn_bwd_call(q, k, kv, o, do, lse)
    dk_nope, dk_rope = _build_k_bwd_call(dk, tabs)
    return (dq, jnp.concatenate([dk_nope, dv], axis=-1), jnp.zeros_like(src), dk_rope,
            tuple(jnp.zeros_like(t) for t in tabs))


attention.defvjp(_attention_fwd, _attention_bwd)


SUBLANES = 8
CONV_BLOCK = 2 * LANE


def _zero_tail(v):
    return jnp.concatenate([v, jnp.zeros((SUBLANES, v.shape[1]), v.dtype)], axis=0)


def _shift_down(vz, sh):
    return pltpu.roll(vz, sh, 0)[:vz.shape[0] - SUBLANES]


def _shift_up(vz, sh):
    return pltpu.roll(vz, vz.shape[0] - sh, 0)[:vz.shape[0] - SUBLANES]


def _conv_pre(u, uz, w_ref, b_ref):
    acc = b_ref[...] + w_ref[pl.ds(CONV_K - 1, 1), :] * u
    for k in range(CONV_K - 1):
        acc = acc + w_ref[pl.ds(k, 1), :] * _shift_down(uz, CONV_K - 1 - k)
    return acc


def _conv_fwd_call(src, w, b):
    B, S, _ = src.shape
    C = w.shape[1]

    def body(u_ref, w_ref, b_ref, o_ref):
        uu = u_ref[0].astype(F32)
        o_ref[0] = _silu(_conv_pre(uu, _zero_tail(uu), w_ref, b_ref))

    spec = pl.BlockSpec((1, S, CONV_BLOCK), lambda c, bb: (bb, 0, c))
    return pl.pallas_call(
        body, grid=(C // CONV_BLOCK, B),
        in_specs=[pl.BlockSpec((1, S, CONV_BLOCK), lambda c, bb: (bb, 0, c + CONV_LANE0 // CONV_BLOCK)),
                  pl.BlockSpec((CONV_K, CONV_BLOCK), lambda c, bb: (0, c)),
                  pl.BlockSpec((1, CONV_BLOCK), lambda c, bb: (0, c))],
        out_specs=spec, out_shape=jax.ShapeDtypeStruct((B, S, C), F32), name="conv_fwd",
        compiler_params=_cparams(("parallel", "arbitrary")))(src, w, b)


def _conv_bwd_call(src, w, b, g):
    B, S, _ = src.shape
    C = w.shape[1]

    def body(u_ref, w_ref, b_ref, g_ref, du_ref, dw_ref, db_ref):
        uu = u_ref[0].astype(F32)
        uz = _zero_tail(uu)
        pre = _conv_pre(uu, uz, w_ref, b_ref)
        sg = lax.logistic(pre)
        dpre = g_ref[0] * sg * (1.0 + pre * (1.0 - sg))
        dz = _zero_tail(dpre)
        du = w_ref[pl.ds(CONV_K - 1, 1), :] * dpre
        dws = [None] * CONV_K
        dws[CONV_K - 1] = jnp.sum(dpre * uu, axis=0, keepdims=True)
        for k in range(CONV_K - 1):
            sh = CONV_K - 1 - k
            du = du + w_ref[pl.ds(k, 1), :] * _shift_up(dz, sh)
            dws[k] = jnp.sum(dpre * _shift_down(uz, sh), axis=0, keepdims=True)
        du_ref[0] = du.astype(du_ref.dtype)
        dbv = jnp.sum(dpre, axis=0, keepdims=True)
        first = pl.program_id(1) == 0

        @pl.when(first)
        def _():
            for k in range(CONV_K):
                dw_ref[pl.ds(k, 1), :] = dws[k]
            db_ref[...] = dbv

        @pl.when(jnp.logical_not(first))
        def _():
            for k in range(CONV_K):
                dw_ref[pl.ds(k, 1), :] += dws[k]
            db_ref[...] += dbv

    spec = pl.BlockSpec((1, S, CONV_BLOCK), lambda c, bb: (bb, 0, c))
    wspec = pl.BlockSpec((CONV_K, CONV_BLOCK), lambda c, bb: (0, c))
    bspec = pl.BlockSpec((1, CONV_BLOCK), lambda c, bb: (0, c))
    uspec = pl.BlockSpec((1, S, CONV_BLOCK), lambda c, bb: (bb, 0, c + CONV_LANE0 // CONV_BLOCK))
    return pl.pallas_call(
        body, grid=(C // CONV_BLOCK, B), in_specs=[uspec, wspec, bspec, spec], out_specs=[spec, wspec, bspec],
        out_shape=[jax.ShapeDtypeStruct((B, S, C), BF16), jax.ShapeDtypeStruct(w.shape, F32),
                   jax.ShapeDtypeStruct(b.shape, F32)],
        name="conv_bwd", compiler_params=_cparams(("parallel", "arbitrary")))(src, w, b, g)


@jax.custom_vjp
def conv_silu(src, stand_in, w, b):
    return _conv_fwd_call(src, w, b)


def _conv_silu_fwd(src, stand_in, w, b):
    return _conv_fwd_call(src, w, b), (src, w, b)


def _conv_silu_bwd(res, g):
    du, dw, db = _conv_bwd_call(*res, g)
    return jnp.zeros_like(res[0]), du, dw, db


conv_silu.defvjp(_conv_silu_fwd, _conv_silu_bwd)


def _chunk_cumsum_call(a, reverse, name):
    B, S, W = a.shape
    per_step = min(S // CHUNK, 32)

    def body(a_ref, o_ref):
        r = lax.broadcasted_iota(jnp.int32, (CHUNK, CHUNK), 0)
        c = lax.broadcasted_iota(jnp.int32, (CHUNK, CHUNK), 1)
        tri = jnp.where((c >= r) if reverse else (c <= r), 1.0, 0.0).astype(F32)
        for i in range(per_step):
            rows = pl.ds(i * CHUNK, CHUNK)
            o_ref[0, rows, :] = jnp.dot(tri, a_ref[0, rows, :], preferred_element_type=F32,
                                        precision=lax.Precision.HIGHEST)

    spec = pl.BlockSpec((1, per_step * CHUNK, W), lambda b, c: (b, c, 0))
    return pl.pallas_call(body, grid=(B, S // (per_step * CHUNK)), in_specs=[spec], out_specs=spec,
                          out_shape=jax.ShapeDtypeStruct(a.shape, F32), name=name,
                          compiler_params=_cparams(("parallel", "parallel")))(a)


@jax.custom_vjp
def chunk_cumsum(a):
    return _chunk_cumsum_call(a, False, "chunk_cumsum_fwd")


chunk_cumsum.defvjp(lambda a: (_chunk_cumsum_call(a, False, "chunk_cumsum_fwd"), None),
                    lambda _, g: (_chunk_cumsum_call(g, True, "chunk_cumsum_bwd"),))


GROUP_W = 4 * HEAD_P
HPG = SSM_HEADS // SSM_GROUPS


def _ssd_masks():
    lane = lax.broadcasted_iota(jnp.int32, (1, GROUP_W), 1)
    return [((lane >= HEAD_P * j) & (lane < HEAD_P * (j + 1))).astype(F32) for j in range(HPG)]


def _ssd_decays(ac_cols, acr_ref, gi):
    r = lax.broadcasted_iota(jnp.int32, (CHUNK, CHUNK), 0)
    c = lax.broadcasted_iota(jnp.int32, (CHUNK, CHUNK), 1)
    return [jnp.exp(jnp.where(c <= r, ac_cols[j] - acr_ref[0, gi * HPG + j], NEG)) for j in range(HPG)]


def _ssd_cols(blk, g):
    lane = lax.broadcasted_iota(jnp.int32, blk.shape, 1)
    return [jnp.sum(jnp.where(lane == HPG * g + j, blk, 0.0), axis=1, keepdims=True) for j in range(HPG)]


def _ssd_spread(cols):
    lane = lax.broadcasted_iota(jnp.int32, (1, GROUP_W), 1)
    out = jnp.broadcast_to(cols[HPG - 1], (CHUNK, GROUP_W))
    for j in range(HPG - 2, -1, -1):
        out = jnp.where(lane < HEAD_P * (j + 1), cols[j], out)
    return out


def _ssd_gather(val, cols, masks, g):
    lane = lax.broadcasted_iota(jnp.int32, (1, LANE), 1)
    out = jnp.zeros((CHUNK, LANE), F32)
    for j in range(HPG):
        tot = jnp.sum(val * masks[j], axis=1, keepdims=True)
        if cols is not None:
            tot = tot + cols[j]
        out = out + tot * (lane == HPG * g + j).astype(F32)
    return out


def _dot(a, b, dims):
    return lax.dot_general(a.astype(BF16), b.astype(BF16), (dims, ((), ())), preferred_element_type=F32)


NN = ((1,), (0,))
NT = ((1,), (1,))
TN = ((0,), (0,))


XBC_W = GROUP_W + 2 * STATE_N


SSD_STEP_GROUPS_FWD = 8
SSD_STEP_GROUPS_BWD = 8


def _ssd_load(xbc_ref, dt_ref, ac_ref, masks, g, gi):
    x = xbc_ref[0, :, gi * XBC_W:gi * XBC_W + GROUP_W]
    bm = xbc_ref[0, :, gi * XBC_W + GROUP_W:gi * XBC_W + GROUP_W + STATE_N]
    cm = xbc_ref[0, :, gi * XBC_W + GROUP_W + STATE_N:(gi + 1) * XBC_W]
    ac_cols = _ssd_cols(ac_ref[0], g)
    dt = _ssd_spread(_ssd_cols(dt_ref[0], g))
    ac = _ssd_spread(ac_cols)
    is_last = (lax.broadcasted_iota(jnp.int32, (CHUNK, GROUP_W), 0) == CHUNK - 1).astype(F32)
    return x, bm, cm, dt, ac, ac_cols, is_last


def _ssd_in_specs(nc, rev, gb):
    cc = (lambda c: nc - 1 - c) if rev else (lambda c: c)
    return [pl.BlockSpec((1, CHUNK, gb * XBC_W), lambda b, g, c: (b, cc(c), g)),
            pl.BlockSpec((1, CHUNK, LANE), lambda b, g, c: (b, cc(c), 0)),
            pl.BlockSpec((1, CHUNK, LANE), lambda b, g, c: (b, cc(c), 0)),
            pl.BlockSpec((1, gb * HPG, 1, CHUNK), lambda b, g, c: (b, g, 0, cc(c))),
            pl.BlockSpec((1, gb * GROUP_W), lambda b, g, c: (0, g))]


def _ssd_fwd_call(xbc, dtp, acp, acr, dsk):
    B, S, _ = xbc.shape
    nc = S // CHUNK
    gb = SSD_STEP_GROUPS_FWD

    def body(xbc_ref, dt_ref, ac_ref, ar_ref, ds_ref, y_ref, hp_ref, h_sc):
        @pl.when(pl.program_id(2) == 0)
        def _():
            h_sc[...] = jnp.zeros(h_sc.shape, F32)

        masks = _ssd_masks()
        ys = []
        for gi in range(gb):
            grp = gb * pl.program_id(1) + gi
            x, bm, cm, dt, ac, ac_cols, is_last = _ssd_load(xbc_ref, dt_ref, ac_ref, masks, grp, gi)
            last = jnp.sum(ac * is_last, axis=0, keepdims=True)
            decays = _ssd_decays(ac_cols, ar_ref, gi)
            xd = x * dt
            cb = _dot(cm, bm, NT)
            hprev = h_sc[gi]
            hp_ref[0, gi, 0] = hprev
            y = _dot(cm, hprev, NN) * jnp.exp(ac) + ds_ref[:, gi * GROUP_W:(gi + 1) * GROUP_W] * x
            y = y + _dot(jnp.concatenate([cb * d for d in decays], axis=1),
                         jnp.concatenate([xd * m for m in masks], axis=0), NN)
            ys.append(y)
            h_sc[gi] = hprev * jnp.exp(last) + _dot(bm, xd * jnp.exp(last - ac), TN)
        y_ref[0] = jnp.concatenate(ys, axis=1)

    ng = SSM_GROUPS // gb
    return pl.pallas_call(
        body, grid=(B, ng, nc), in_specs=_ssd_in_specs(nc, False, gb),
        out_specs=[pl.BlockSpec((1, CHUNK, gb * GROUP_W), lambda b, g, c: (b, c, g)),
                   pl.BlockSpec((1, gb, 1, STATE_N, GROUP_W), lambda b, g, c: (b, g, c, 0, 0))],
        out_shape=[jax.ShapeDtypeStruct((B, S, D_INNER), F32),
                   jax.ShapeDtypeStruct((B, SSM_GROUPS, nc, STATE_N, GROUP_W), F32)],
        scratch_shapes=[pltpu.VMEM((gb, STATE_N, GROUP_W), F32)], name="ssd_fwd",
        compiler_params=_cparams(("parallel", "parallel", "arbitrary")))(xbc, dtp, acp, acr, dsk)


def _ssd_bwd_call(xbc, dtp, acp, acr, dsk, hps, dy):
    B, S, _ = xbc.shape
    nc = S // CHUNK
    gb = SSD_STEP_GROUPS_BWD

    def body(xbc_ref, dt_ref, ac_ref, ar_ref, ds_ref, hp_ref, dy_ref,
             dxbc_ref, ddt_ref, dac_ref, dar_ref, dds_ref, dh_sc):
        first = pl.program_id(2) == 0

        @pl.when(first)
        def _():
            dh_sc[...] = jnp.zeros(dh_sc.shape, F32)

        masks = _ssd_masks()
        dxbc_parts, dds_parts = [], []
        for gi in range(gb):
            grp = gb * pl.program_id(0) + gi
            x, bm, cm, dt, ac, ac_cols, is_last = _ssd_load(xbc_ref, dt_ref, ac_ref, masks, grp, gi)
            last = jnp.sum(ac * is_last, axis=0, keepdims=True)
            g = dy_ref[0, :, gi * GROUP_W:(gi + 1) * GROUP_W]
            hprev = hp_ref[0, gi, 0]
            dh = dh_sc[gi]
            decays = _ssd_decays(ac_cols, ar_ref, gi)
            dcols = []
            xd = x * dt
            cb = _dot(cm, bm, NT)
            e_c = jnp.exp(ac)
            e_end = jnp.exp(last - ac)
            e_last = jnp.exp(last)
            z = _dot(cm, hprev, NN)
            dz = g * e_c
            dac = g * z * e_c
            dc = _dot(dz, hprev, NT)
            dhprev = _dot(cm, dz, TN) + dh * e_last
            dcb = jnp.zeros((CHUNK, CHUNK), F32)
            gjs = [cb * d for d in decays]
            g_heads = jnp.concatenate([g * m for m in masks], axis=0)
            dg_heads = _dot(g_heads, xd, NT)
            dxd = _dot(jnp.concatenate(gjs, axis=0), g_heads, TN)
            for j in range(HPG):
                gj = gjs[j]
                dgj = dg_heads[j * CHUNK:(j + 1) * CHUNK]
                dcb = dcb + dgj * decays[j]
                dseg = dgj * gj
                dcols.append(jnp.sum(dseg, axis=1, keepdims=True))
                dar_ref[0, gi * HPG + j] = -jnp.sum(dseg, axis=0, keepdims=True)
            dc = dc + _dot(dcb, bm, NN)
            db = _dot(dcb, cm, TN)
            sx = xd * e_end
            db = db + _dot(sx, dh, NT)
            dsx = _dot(bm, dh, NN)
            dxd = dxd + dsx * e_end
            de = dsx * sx
            dac = dac - de
            dlast = jnp.sum(de, axis=0, keepdims=True) + jnp.sum(dh * hprev, axis=0, keepdims=True) * e_last
            dsk = ds_ref[:, gi * GROUP_W:(gi + 1) * GROUP_W]
            dxbc_parts += [dxd * dt + dsk * g, db, dc]
            ddt_ref[0, gi] = _ssd_gather(dxd * x, None, masks, grp)
            dac_ref[0, gi] = _ssd_gather(dac + is_last * dlast, dcols, masks, grp)
            dds_parts.append(jnp.sum(g * x, axis=0, keepdims=True))
            dh_sc[gi] = dhprev
        dxbc_ref[0] = jnp.concatenate(dxbc_parts, axis=1)
        dds = jnp.concatenate(dds_parts, axis=1)
        first_all = first & (pl.program_id(1) == 0)

        @pl.when(first_all)
        def _():
            dds_ref[...] = dds

        @pl.when(jnp.logical_not(first_all))
        def _():
            dds_ref[...] += dds

    rc = lambda c: nc - 1 - c
    ng = SSM_GROUPS // gb
    in_specs = [pl.BlockSpec(s.block_shape, (lambda g, b, c, f=s.index_map: f(b, g, c))) for s in _ssd_in_specs(nc, True, gb)]
    in_specs.append(pl.BlockSpec((1, gb, 1, STATE_N, GROUP_W), lambda g, b, c: (b, g, rc(c), 0, 0)))
    in_specs.append(pl.BlockSpec((1, CHUNK, gb * GROUP_W), lambda g, b, c: (b, rc(c), g)))
    per_group = pl.BlockSpec((1, gb, CHUNK, LANE), lambda g, b, c: (b, g, rc(c), 0))
    out_specs = [pl.BlockSpec((1, CHUNK, gb * XBC_W), lambda g, b, c: (b, rc(c), g)), per_group, per_group,
                 pl.BlockSpec((1, gb * HPG, 1, CHUNK), lambda g, b, c: (b, g, 0, rc(c))),
                 pl.BlockSpec((1, gb * GROUP_W), lambda g, b, c: (0, g))]
    out_shape = [jax.ShapeDtypeStruct(xbc.shape, F32),
                 jax.ShapeDtypeStruct((B, SSM_GROUPS, S, LANE), F32), jax.ShapeDtypeStruct((B, SSM_GROUPS, S, LANE), F32),
                 jax.ShapeDtypeStruct(acr.shape, F32), jax.ShapeDtypeStruct(dsk.shape, F32)]
    return pl.pallas_call(
        body, grid=(ng, B, nc), in_specs=in_specs, out_specs=out_specs, out_shape=out_shape,
        scratch_shapes=[pltpu.VMEM((gb, STATE_N, GROUP_W), F32)], name="ssd_bwd",
        compiler_params=_cparams(("arbitrary", "arbitrary", "arbitrary")))(xbc, dtp, acp, acr, dsk, hps, dy)


@jax.custom_vjp
def ssd(xbc, dtp, acp, acr, dsk):
    return _ssd_fwd_call(xbc, dtp, acp, acr, dsk)[0]


def _ssd_fwd(xbc, dtp, acp, acr, dsk):
    y, hps = _ssd_fwd_call(xbc, dtp, acp, acr, dsk)
    return y, (xbc, dtp, acp, acr, dsk, hps)


def _ssd_bwd(res, dy):
    dxbc, ddt, dac, dacr, dds = _ssd_bwd_call(*res, dy)
    return dxbc, jnp.sum(ddt, axis=1), jnp.sum(dac, axis=1), dacr, dds


ssd.defvjp(_ssd_fwd, _ssd_bwd)


def _pack_small(arrs):
    flat = jnp.concatenate([a.reshape(-1) for a in arrs])
    rows = -(-flat.shape[0] // (8 * LANE)) * 8
    return jnp.pad(flat, (0, rows * LANE - flat.shape[0])).reshape(rows, LANE)


def _unpack_small(buf, shapes):
    flat = buf.reshape(-1)
    out, off = [], 0
    for shp in shapes:
        n = int(np.prod(shp))
        out.append(flat[off:off + n].reshape(shp))
        off += n
    return out


def _rows_tile(rows, cap):
    for cand in range(min(rows, cap), 7, -8):
        if rows % cand == 0:
            return cand
    return rows


def _pair_sum(mine, theirs, cidx, name):
    n4, kk, nn = mine.shape
    half = kk // 2
    tr = _rows_tile(half, 256)
    nb = half // tr

    def body(c_ref, a_ref, b_ref, o_ref, ob_ref):
        tot = a_ref[...] + b_ref[...]
        o_ref[...] = tot
        ob_ref[...] = tot.astype(BF16)

    spec = pl.BlockSpec((1, tr, nn), lambda j, i, c: (j, i, 0))
    grid_spec = pltpu.PrefetchScalarGridSpec(
        num_scalar_prefetch=1, grid=(n4, nb),
        in_specs=[pl.BlockSpec((1, tr, nn), lambda j, i, c: (j, c[0] * nb + i, 0)), spec], out_specs=[spec, spec])
    return pl.pallas_call(
        body, grid_spec=grid_spec,
        out_shape=[jax.ShapeDtypeStruct((n4, half, nn), F32), jax.ShapeDtypeStruct((n4, half, nn), BF16)],
        name=name, compiler_params=_cparams(("parallel", "parallel")))(cidx, mine, theirs)


def _chip_sum(quad, pair, chip_idx, name):
    _, rows, nn = quad.shape
    tr = _rows_tile(rows, 256)

    def body(s_ref, q_ref, p_ref, o_ref):
        for mine in range(4):
            @pl.when(s_ref[0] == mine)
            def _(mine=mine):
                acc = None
                for d in range(4):
                    term = p_ref[0] if d == mine else q_ref[d].astype(F32)
                    acc = term if acc is None else acc + term
                o_ref[...] = acc

    grid_spec = pltpu.PrefetchScalarGridSpec(
        num_scalar_prefetch=1, grid=(rows // tr,),
        in_specs=[pl.BlockSpec((4, tr, nn), lambda i, s: (0, i, 0)), pl.BlockSpec((1, tr, nn), lambda i, s: (s[0], i, 0))],
        out_specs=pl.BlockSpec((tr, nn), lambda i, s: (i, 0)))
    return pl.pallas_call(body, grid_spec=grid_spec, out_shape=jax.ShapeDtypeStruct((rows, nn), F32), name=name,
                          compiler_params=_cparams(("parallel",)))(chip_idx, quad, pair)


def _adam_halves_call(w, mine, other, cidx, m, v, name):
    _, rows, nn = w.shape
    half = rows // 2
    tr = _rows_tile(half, 128)
    nb = half // tr

    def body(c_ref, w_ref, a_ref, b_ref, m_ref, v_ref, g_ref, d_ref, nm_ref, nv_ref):
        upper = (pl.program_id(0) >= nb).astype(jnp.int32)
        g = jnp.where(upper == c_ref[0], a_ref[...], b_ref[...])
        g_ref[0] = g
        d_ref[0], nm_ref[0], nv_ref[0] = _adam_fn(w_ref[0], g, m_ref[0], v_ref[0])

    spec = pl.BlockSpec((1, tr, nn), lambda i, c: (0, i, 0))
    hspec = pl.BlockSpec((tr, nn), lambda i, c: (i % nb, 0))
    grid_spec = pltpu.PrefetchScalarGridSpec(num_scalar_prefetch=1, grid=(2 * nb,),
                                             in_specs=[spec, hspec, hspec, spec, spec], out_specs=[spec] * 4)
    return pl.pallas_call(body, grid_spec=grid_spec, out_shape=[jax.ShapeDtypeStruct(w.shape, F32)] * 4, name=name,
                          compiler_params=_cparams(("parallel",)))(cidx, w, mine, other, m, v)


def _stack_sum(stack, name):
    n, rows, nn = stack.shape
    tr = _rows_tile(rows, 256)

    def body(s_ref, o_ref):
        acc = s_ref[0]
        for d in range(1, n):
            acc = acc + s_ref[d]
        o_ref[...] = acc

    return pl.pallas_call(
        body, grid=(rows // tr,), in_specs=[pl.BlockSpec((n, tr, nn), lambda i: (0, i, 0))],
        out_specs=pl.BlockSpec((tr, nn), lambda i: (i, 0)), out_shape=jax.ShapeDtypeStruct((rows, nn), F32),
        name=name, compiler_params=_cparams(("parallel",)))(stack)


def _adam_call(w, g, m, v, name):
    rows, nn = w.shape
    tr = _rows_tile(rows, 128)

    def body(w_ref, g_ref, m_ref, v_ref, d_ref, nm_ref, nv_ref):
        d_ref[...], nm_ref[...], nv_ref[...] = _adam_fn(w_ref[...], g_ref[...], m_ref[...], v_ref[...])

    spec = pl.BlockSpec((tr, nn), lambda i: (i, 0))
    sds = jax.ShapeDtypeStruct((rows, nn), F32)
    return pl.pallas_call(body, grid=(rows // tr,), in_specs=[spec] * 4, out_specs=[spec] * 3,
                          out_shape=[sds] * 3, name=name, compiler_params=_cparams(("parallel",)))(w, g, m, v)


def _adam_fn(w, g, m, v):
    m = ADAM_B1 * m + (1.0 - ADAM_B1) * g
    v = ADAM_B2 * v + (1.0 - ADAM_B2) * (g * g)
    m_hat = m / (1.0 - ADAM_B1 ** ADAM_STEP)
    v_hat = v / (1.0 - ADAM_B2 ** ADAM_STEP)
    delta = -ADAM_LR * (m_hat / (jnp.sqrt(v_hat) + ADAM_EPS) + ADAM_WD * w)
    return delta, m, v


def _mesh_pos():
    return lax.axis_index("x"), lax.axis_index("y"), lax.axis_index("c")


def _other_chips(x, y):
    return [(1 - x, y), (x, 1 - y), (1 - x, 1 - y)]


HBM_SPEC = pl.BlockSpec(memory_space=pl.ANY)


def _remote(src, dst, send_sems, recv_sems, k, to):
    return pltpu.make_async_remote_copy(src_ref=src, dst_ref=dst, send_sem=send_sems.at[k], recv_sem=recv_sems.at[k],
                                        device_id=to, device_id_type=MESH)


def _half_rows(c, rows, align):
    half = rows // 2
    return (pl.ds(pl.multiple_of(c * half, align), half), pl.ds(pl.multiple_of((1 - c) * half, align), half))


def _gather_weights(mats, conv):
    n = len(mats)

    def body(*refs):
        ins, conv_in = refs[:n], refs[n]
        outs, conv_out = refs[n + 1:2 * n + 1], refs[2 * n + 1]
        send_sems, recv_sems, local_sem = refs[2 * n + 2:]
        x, y, c = _mesh_pos()
        me, sibling, s = (x, y, c), (x, y, 1 - c), 2 * x + y
        chips = _other_chips(x, y)
        rows = [_half_rows(c, m.shape[0], 16) for m in mats]
        own = pltpu.make_async_copy(conv_in, conv_out.at[s], local_sem)
        own.start()
        sent = []
        for i in range(n):
            mine = rows[i][0]
            for j, (cx, cy) in enumerate(chips):
                sent.append(_remote(ins[i].at[mine], outs[i].at[s, mine], send_sems, recv_sems, 6 * i + j, (cx, cy, c)))
        for j, (cx, cy) in enumerate(chips):
            sent.append(_remote(conv_in, conv_out.at[s], send_sems, recv_sems, 6 * n + j, (cx, cy, c)))
        for cp in sent:
            cp.start()
        for i in range(n):
            mine = rows[i][0]
            for j, (cx, cy) in enumerate(chips):
                landed = outs[i].at[2 * cx + cy, mine]
                _remote(landed, landed, send_sems, recv_sems, 6 * i + j, me).wait_recv()
                fwd = _remote(landed, landed, send_sems, recv_sems, 6 * i + 3 + j, sibling)
                fwd.start()
                sent.append(fwd)
        for j, (cx, cy) in enumerate(chips):
            slot = conv_out.at[2 * cx + cy]
            _remote(slot, slot, send_sems, recv_sems, 6 * n + j, me).wait_recv()
        for i in range(n):
            theirs_rows = rows[i][1]
            for j, (cx, cy) in enumerate(chips):
                theirs = outs[i].at[2 * cx + cy, theirs_rows]
                _remote(theirs, theirs, send_sems, recv_sems, 6 * i + 3 + j, me).wait_recv()
        for cp in sent:
            cp.wait_send()
        own.wait()

    out_shape = [jax.ShapeDtypeStruct((4,) + m.shape, m.dtype) for m in mats]
    out_shape.append(jax.ShapeDtypeStruct((4,) + conv.shape, conv.dtype))
    res = pl.pallas_call(
        body, in_specs=[HBM_SPEC] * (n + 1), out_specs=[HBM_SPEC] * (n + 1), out_shape=out_shape,
        scratch_shapes=[pltpu.SemaphoreType.DMA((6 * n + 3,)), pltpu.SemaphoreType.DMA((6 * n + 3,)),
                        pltpu.SemaphoreType.DMA],
        name="all_gather_weights")(*mats, conv)
    return res[:n], res[n]


def _sibling_exchange(stacks):
    n = len(stacks)

    def body(*refs):
        ins, outs = refs[:n], refs[n:2 * n]
        send_sems, recv_sems = refs[2 * n:]
        x, y, c = _mesh_pos()
        cps = []
        for i in range(n):
            theirs = _half_rows(c, stacks[i].shape[1], 8)[1]
            cps.append(_remote(ins[i].at[:, theirs, :], outs[i], send_sems, recv_sems, i, (x, y, 1 - c)))
        for cp in cps:
            cp.start()
        for cp in cps:
            cp.wait()

    out_shape = [jax.ShapeDtypeStruct((4, s.shape[1] // 2, s.shape[2]), s.dtype) for s in stacks]
    return pl.pallas_call(
        body, in_specs=[HBM_SPEC] * n, out_specs=[HBM_SPEC] * n, out_shape=out_shape,
        scratch_shapes=[pltpu.SemaphoreType.DMA((n,)), pltpu.SemaphoreType.DMA((n,))],
        name="grad_sibling_exchange")(*stacks)


def _chip_exchange(parts):
    n = len(parts)

    def body(*refs):
        ins, outs = refs[:n], refs[n:2 * n]
        send_sems, recv_sems = refs[2 * n:]
        x, y, c = _mesh_pos()
        me, s = (x, y, c), 2 * x + y
        chips = _other_chips(x, y)
        sent = [_remote(ins[i].at[2 * cx + cy], outs[i].at[s], send_sems, recv_sems, 3 * i + j, (cx, cy, c))
                for i in range(n) for j, (cx, cy) in enumerate(chips)]
        for cp in sent:
            cp.start()
        for i in range(n):
            for j, (cx, cy) in enumerate(chips):
                slot = outs[i].at[2 * cx + cy]
                _remote(slot, slot, send_sems, recv_sems, 3 * i + j, me).wait_recv()
        for cp in sent:
            cp.wait_send()

    return pl.pallas_call(
        body, in_specs=[HBM_SPEC] * n, out_specs=[HBM_SPEC] * n,
        out_shape=[jax.ShapeDtypeStruct(p.shape, p.dtype) for p in parts],
        scratch_shapes=[pltpu.SemaphoreType.DMA((3 * n,)), pltpu.SemaphoreType.DMA((3 * n,))],
        name="grad_chip_exchange")(*parts)


def _sibling_swap(halves):
    n = len(halves)

    def body(*refs):
        ins, outs = refs[:n], refs[n:2 * n]
        send_sems, recv_sems = refs[2 * n:]
        x, y, c = _mesh_pos()
        cps = [_remote(ins[i], outs[i], send_sems, recv_sems, i, (x, y, 1 - c)) for i in range(n)]
        for cp in cps:
            cp.start()
        for cp in cps:
            cp.wait()

    return pl.pallas_call(
        body, in_specs=[HBM_SPEC] * n, out_specs=[HBM_SPEC] * n,
        out_shape=[jax.ShapeDtypeStruct(h.shape, h.dtype) for h in halves],
        scratch_shapes=[pltpu.SemaphoreType.DMA((n,)), pltpu.SemaphoreType.DMA((n,))],
        name="grad_sibling_swap")(*halves)


def _gather_small(vec):
    def body(in_ref, out_ref, send_sems, recv_sems, local_sem):
        x, y, c = _mesh_pos()
        me = (x, y, c)
        own = pltpu.make_async_copy(in_ref, out_ref.at[4 * x + 2 * y + c], local_sem)
        own.start()
        peers = [(1 - x if k & 4 else x, 1 - y if k & 2 else y, 1 - c if k & 1 else c) for k in range(1, 8)]
        sent = [_remote(in_ref, out_ref.at[4 * x + 2 * y + c], send_sems, recv_sems, k, p) for k, p in enumerate(peers)]
        for cp in sent:
            cp.start()
        for k, (px, py, pc) in enumerate(peers):
            slot = out_ref.at[4 * px + 2 * py + pc]
            _remote(slot, slot, send_sems, recv_sems, k, me).wait_recv()
        for cp in sent:
            cp.wait_send()
        own.wait()

    return pl.pallas_call(
        body, in_specs=[HBM_SPEC], out_specs=HBM_SPEC, out_shape=jax.ShapeDtypeStruct((8,) + vec.shape, vec.dtype),
        scratch_shapes=[pltpu.SemaphoreType.DMA((7,)), pltpu.SemaphoreType.DMA((7,)), pltpu.SemaphoreType.DMA],
        name="grad_gather_small")(vec)


def _reduce_matrices(stacks, names):
    cidx = lax.axis_index("c").astype(jnp.int32).reshape(1)
    chip = (2 * lax.axis_index("x") + lax.axis_index("y")).astype(jnp.int32).reshape(1)
    got = _sibling_exchange(stacks)
    pairs = [_pair_sum(a, b, cidx, "grad_pair_sum_" + nm) for a, b, nm in zip(stacks, got, names)]
    quads = _chip_exchange([p[1] for p in pairs])
    mine = [_chip_sum(q, p[0], chip, "grad_chip_sum_" + nm) for q, p, nm in zip(quads, pairs, names)]
    return mine, _sibling_swap(mine)


def _pad_cols(a, n):
    return jnp.concatenate([a, jnp.zeros((a.shape[0], n - a.shape[1]), a.dtype)], axis=1)


def _group_channels(a):
    lead = a.shape[:-1]
    xs = a[..., :D_INNER].reshape(lead + (SSM_GROUPS, GROUP_W))
    bs = a[..., D_INNER:D_INNER + SSM_GROUPS * STATE_N].reshape(lead + (SSM_GROUPS, STATE_N))
    cs = a[..., D_INNER + SSM_GROUPS * STATE_N:].reshape(lead + (SSM_GROUPS, STATE_N))
    return jnp.concatenate([xs, bs, cs], axis=-1).reshape(lead + (CONV_CH,))


PROJ_SEGS = (('gate_a', D_MODEL), ('gate_b', D_MODEL), ('z', D_INNER), ('xbc', CONV_CH), ('q_lat', Q_RANK),
             ('kv_lat', KV_RANK), ('k_rope', LANE), ('dt', LANE))
PROJ_WIDE = sum(w for _, w in PROJ_SEGS[:4])
PROJ_LANE0 = {n: (v if v < PROJ_WIDE else v - PROJ_WIDE) for n, v in
              zip([n for n, _ in PROJ_SEGS], [int(v) for v in np.cumsum([0] + [w for _, w in PROJ_SEGS])[:-1]])}
CONV_LANE0 = PROJ_LANE0['xbc']
KR_LANE0 = PROJ_LANE0['k_rope']


def _lay_w_in(w):
    idx = np.cumsum(IN_SIZES)[:-1]
    q_lat, kv_lat, k_rope, z, xbc, dt, gate_a, gate_b = jnp.split(w, [int(v) for v in idx], axis=1)
    return jnp.concatenate([gate_a, gate_b, z, _group_channels(xbc), q_lat, kv_lat, _pad_cols(k_rope, LANE),
                            _pad_cols(dt, LANE)], axis=1)


@jax.custom_vjp
def project(h, w, tok):
    return _project_impl(h, w)


def _project_impl(h, w):
    return (_mm(h, w[:, :PROJ_WIDE], "w_in_fwd", BF16), _mm(h, w[:, PROJ_WIDE:], "w_in_narrow_fwd")) + tuple(
        jnp.zeros((h.shape[0], wd), BF16) for _, wd in PROJ_SEGS)


def _project_fwd(h, w, tok):
    return _project_impl(h, w), (h, w)


def _project_bwd(res, cots):
    h, w = res
    g = jnp.concatenate(cots[2:], axis=1)
    return _mm(g, w.T, "w_in_dx", h.dtype), jnp.zeros_like(w), _mm(h.T, g, "w_in_dw")


project.defvjp(_project_fwd, _project_bwd)


def _lay_w_uq(w):
    w3 = w.reshape(Q_RANK, N_HEADS, NOPE + ROPE)
    w3 = jnp.concatenate([w3, jnp.zeros((Q_RANK, N_HEADS, QK_PAD - NOPE - ROPE), w.dtype)], axis=2)
    return w3.reshape(Q_RANK, N_HEADS * QK_PAD)


def _lay_w_ukv(w):
    w3 = w.reshape(KV_RANK, N_HEADS, NOPE + V_DIM)
    return jnp.concatenate([w3[:, :, :NOPE].reshape(KV_RANK, -1), w3[:, :, NOPE:].reshape(KV_RANK, -1)], axis=1)


def _pad_lanes(v, n=LANE):
    return jnp.concatenate([v, jnp.zeros((v.shape[0], n - v.shape[1]), v.dtype)], axis=1)


def _local_loss(toks, small, x, wb, c8, posf, target):
    B, S, D = x.shape
    T = B * S

    def lin(name, a, key, lay=lambda w: w, out_dtype=F32):
        return make_linear(name, out_dtype)(a, lay(wb[key]), lay(toks[key]))

    rows2 = lambda a: a.reshape(T, a.shape[-1])
    rows3 = lambda a: a.reshape(B, S, a.shape[-1])

    sc = make_rowwise("silu_c", _f_silu, 1, 0, 0, ('row',))((c8[None],), (), ())[0][0]
    mod = make_linear("ada", F32, 4)(sc, wb['w_ada'], toks['w_ada'])[:B] + small['b_ada']
    shift1, scale1, gate1, shift2, scale2, gate2 = [m[:, None, :] for m in jnp.split(mod, 6, axis=-1)]

    h, x_res = make_rowwise("modulate1", _f_modulate, 1, 2, 1, ('row',), forward_row=0, ts_cap=1024)(
        (x,), (scale1, shift1), (small['g_pre_mix'],))
    outs = project(rows2(h), _lay_w_in(wb['w_in']), _lay_w_in(toks['w_in']))
    wide = lax.stop_gradient(rows3(outs[0]))
    proj = lax.stop_gradient(rows3(outs[1]))
    stand = {n: rows3(o) for (n, _), o in zip(PROJ_SEGS, outs[2:])}

    def win(seg, block):
        return (PROJ_LANE0[seg] // block, dict(PROJ_SEGS)[seg])

    inv = ROPE_THETA ** (-jnp.arange(ROPE // 2, dtype=F32) / (ROPE // 2))
    inv_lane = jnp.concatenate([inv, inv, jnp.zeros((LANE - ROPE,), F32)])[None]
    tabs = tuple(_rope_tables(posf, inv_lane))
    qn = make_rowwise("rms_q", _f_rms, 1, 0, 1, ('row',), ts_cap=4096, windows={0: win('q_lat', Q_RANK)})(
        (proj,), (), (small['g_q_lat'],), (stand['q_lat'],))[0]
    kvn = make_rowwise("rms_kv", _f_rms, 1, 0, 1, ('row',), ts_cap=4096, windows={0: win('kv_lat', KV_RANK)})(
        (proj,), (), (small['g_kv_lat'],), (stand['kv_lat'],))[0]
    qp = rows3(lin("w_uq", rows2(qn), 'w_uq', _lay_w_uq, BF16))
    kvp = rows3(lin("w_ukv", rows2(kvn), 'w_ukv', _lay_w_ukv, BF16))
    qr = rope_q(qp, tabs)
    att = attention(qr, kvp, proj, stand['k_rope'], tabs)
    attn = rows3(lin("w_o_attn", rows2(att), 'w_o_attn', out_dtype=BF16))

    xa = conv_silu(wide, stand['xbc'], _group_channels(wb['conv_w_f32']), _group_channels(small['conv_b']))
    dt_pad, a_pad = make_rowwise("dt_softplus", _f_dt, 1, 0, 2, ('row', 'row'), ts_cap=4096,
                                 windows={0: win('dt', LANE)})(
        (proj,), (), (_pad_lanes(small['dt_bias']), _pad_lanes(small['a_log'])), (stand['dt'],))
    ac_pad = chunk_cumsum(a_pad)
    acr = jnp.transpose(ac_pad[..., :SSM_HEADS], (0, 2, 1))[:, :, None, :]
    dsk = jnp.repeat(small['d_skip'], HEAD_P, axis=-1)
    y = ssd(xa, dt_pad, ac_pad, acr, dsk)
    yg = make_rowwise("gated_norm", _f_gated_norm, 2, 0, 1, ('row',), ncol=SSM_GROUPS, ts_cap=4096,
                      windows={1: win('z', GROUP_W)})((y, wide), (), (small['g_ssm_out'],), (stand['z'],))[0]
    ssm = rows3(lin("w_o_ssm", rows2(yg), 'w_o_ssm', out_dtype=BF16))

    merged = make_rowwise("merge", _f_merge, 4, 0, 0, ('row',), ts_cap=1024,
                          windows={2: win('gate_a', D_MODEL), 3: win('gate_b', D_MODEL)})(
        (attn, ssm, wide, wide), (), (), (stand['gate_a'], stand['gate_b']))[0]
    mix = rows3(lin("w_out", rows2(merged), 'w_out', out_dtype=BF16))
    x1 = make_rowwise("post_mix", _f_post, 2, 1, 1, ('row',), ts_cap=1024)(
        (x_res, mix), (gate1,), (small['g_post_mix'],))[0]

    h2, x1_res = make_rowwise("modulate2", _f_modulate, 1, 2, 1, ('row',), forward_row=0, ts_cap=1024)(
        (x1,), (scale2, shift2), (small['g_pre_mlp'],))
    ff = rows3(ffn(rows2(h2), wb['w_ff1'], toks['w_ff1'], wb['w_ff2'], toks['w_ff2']))
    lvec = make_rowwise("final_loss", _f_final_loss, 3, 1, 1, ('sum',), nodiff=(2,), ts_cap=1024)(
        (x1_res, ff, target), (gate2,), (small['g_post_mlp'],))[0]
    return jnp.sum(lvec)


MATRICES = COL_SHARDED + ROW_SHARDED
STACKED_DW = ('w_ada', 'w_ff1')


def _local_step(x, c, positions, target, wb, small):
    B = x.shape[0]
    c8 = jnp.concatenate([c, jnp.zeros((16 - B, c.shape[1]), F32)], axis=0)
    posf = positions.astype(F32)[..., None]
    toks = {k: jnp.zeros(wb[k].shape, F32) for k in MATRICES if k != 'conv_w'}
    for k in STACKED_DW:
        rows, cols = wb[k].shape
        toks[k] = jnp.zeros((4, rows, cols // 4), F32)
    conv_w = wb['conv_w_f32']

    def loss_fn(toks, small, conv_w, x):
        wbl = dict(wb)
        wbl['conv_w_f32'] = conv_w
        return _local_loss(toks, small, x, wbl, c8, posf, target)

    loss, (g_tok, g_small, g_conv, g_x) = jax.value_and_grad(loss_fn, argnums=(0, 1, 2, 3))(toks, small, conv_w, x)
    grads = dict(g_tok)
    grads.update(g_small)
    grads['conv_w'] = g_conv
    return loss, g_x, grads


def kernel(x, c, positions, w_ada, b_ada, g_pre_mix, g_post_mix, w_in, g_q_lat, g_kv_lat, w_uq, w_ukv, w_o_attn, conv_w, conv_b, dt_bias, a_log, d_skip, g_ssm_out, w_o_ssm, w_out, g_pre_mlp, g_post_mlp, w_ff1, w_ff2, loss_target, m_w_ada, m_b_ada, m_g_pre_mix, m_g_post_mix, m_w_in, m_g_q_lat, m_g_kv_lat, m_w_uq, m_w_ukv, m_w_o_attn, m_conv_w, m_conv_b, m_dt_bias, m_a_log, m_d_skip, m_g_ssm_out, m_w_o_ssm, m_w_out, m_g_pre_mlp, m_g_post_mlp, m_w_ff1, m_w_ff2, v_w_ada, v_b_ada, v_g_pre_mix, v_g_post_mix, v_w_in, v_g_q_lat, v_g_kv_lat, v_w_uq, v_w_ukv, v_w_o_attn, v_conv_w, v_conv_b, v_dt_bias, v_a_log, v_d_skip, v_g_ssm_out, v_w_o_ssm, v_w_out, v_g_pre_mlp, v_g_post_mlp, v_w_ff1, v_w_ff2):
    given = dict(locals())
    w_loc = {n: given[n] for n in WEIGHTS}
    m_loc = {n: given["m_" + n] for n in WEIGHTS}
    v_loc = {n: given["v_" + n] for n in WEIGHTS}
    mats = [n for n in WEIGHTS if n in MATRICES and n != 'conv_w']
    vecs = [n for n in WEIGHTS if n not in MATRICES]

    own = [w_loc[n][0].astype(BF16) for n in mats]
    g_mats, g_conv = _gather_weights(own, conv_w[0])
    chip = 2 * lax.axis_index("x") + lax.axis_index("y")
    wb = {}
    for n, g, mine in zip(mats, g_mats, own):
        g = lax.dynamic_update_slice_in_dim(g, mine[None], chip, axis=0)
        if n in COL_SHARDED:
            wb[n] = jnp.transpose(g, (1, 0, 2)).reshape(g.shape[1], -1)
        else:
            wb[n] = g.reshape(-1, g.shape[2])
    wb['conv_w_f32'] = jnp.transpose(g_conv, (1, 0, 2)).reshape(CONV_K, -1)
    small = {n: w_loc[n] for n in vecs}

    loss_part, grad_x, grads = _local_step(x, c, positions, loss_target, wb, small)
    loss = lax.psum(loss_part, ("x", "y", "c"))

    stacks = []
    for n in mats:
        kk, nn = w_loc[n].shape[1:]
        if n in STACKED_DW:
            stacks.append(grads[n])
        elif n in COL_SHARDED:
            stacks.append(jnp.transpose(grads[n].reshape(kk, 4, nn), (1, 0, 2)))
        else:
            stacks.append(grads[n].reshape(4, kk, nn))
    g_mine, g_other = _reduce_matrices(stacks, mats)
    g_shard = {}

    vec_shapes = [tuple(grads[n].shape) for n in vecs] + [tuple(grads['conv_w'].shape)]
    total = _stack_sum(_gather_small(_pack_small([grads[n] for n in vecs] + [grads['conv_w']])), "grad_sum_small")
    g_vec = _unpack_small(total, vec_shapes)
    n_conv = conv_w.shape[2]
    chip = 2 * lax.axis_index("x") + lax.axis_index("y")
    g_shard['conv_w'] = lax.dynamic_slice_in_dim(g_vec[-1], chip * n_conv, n_conv, axis=1)
    for n, g in zip(vecs, g_vec):
        g_shard[n] = g

    delta, new_m, new_v = {}, {}, {}
    cidx = lax.axis_index("c").astype(jnp.int32).reshape(1)
    for n, mine, other in zip(mats, g_mine, g_other):
        g_shard[n], delta[n], new_m[n], new_v[n] = _adam_halves_call(
            w_loc[n], mine, other, cidx, m_loc[n], v_loc[n], "adamw_" + n)
    rest = vecs + ['conv_w']
    rest_shapes = [tuple(w_loc[n].shape) for n in rest]
    packed = [_pack_small([src[n] for n in rest]) for src in (w_loc, g_shard, m_loc, v_loc)]
    for dst, buf in zip((delta, new_m, new_v), _adam_call(*packed, "adamw_small")):
        dst.update(zip(rest, _unpack_small(buf, rest_shapes)))

    def out(d):
        return [d[n].reshape(w_loc[n].shape) for n in WEIGHTS]

    return (loss, grad_x, *out(g_shard), *out(delta), *out(new_m), *out(new_v))
```
